```python
import math
import jax, jax.numpy as jnp
from jax import lax
import numpy as np

D_MODEL = 1024
BATCH = 8
SEQ = 4096
DEPTH = 1

CHUNK = 64
RET_HEADS = 4
RET_HEAD_DIM = D_MODEL // 8
RET_WIDTH = RET_HEADS * RET_HEAD_DIM
POOL_WINDOWS = (2, 4, 8, 16)
POOL_GROUPS = len(POOL_WINDOWS)
POOL_GROUP_DIM = D_MODEL // 8
POOL_WIDTH = POOL_GROUPS * POOL_GROUP_DIM
MIX_WIDTH = RET_WIDTH + POOL_WIDTH
IN_WIDTH = 4 * RET_WIDTH + POOL_WIDTH
D_FF = ((8 * D_MODEL // 3 + 127) // 128) * 128
CONV_WIDTH = 3
ROPE_BASE = 10000.0
LN_EPS = 1e-5
RMS_EPS = 1e-6
DEEPNORM_ALPHA = (2.0 * DEPTH) ** 0.25
DEEPNORM_BETA = (8.0 * DEPTH) ** -0.25

kernel_name = "hybrid_retention_pool_convffn_deepnorm"


def _layernorm(x, g, b):
    xf = x.astype(jnp.float32)
    mu = jnp.mean(xf, axis=-1, keepdims=True)
    var = jnp.mean(jnp.square(xf - mu), axis=-1, keepdims=True)
    y = (xf - mu) * lax.rsqrt(var + LN_EPS) * g.astype(jnp.float32) + b.astype(jnp.float32)
    return y.astype(x.dtype)


def _rope(t):
    s, dh = t.shape[1], t.shape[-1]
    inv_freq = ROPE_BASE ** (-jnp.arange(0, dh, 2, dtype=jnp.float32) / dh)
    ang = jnp.arange(s, dtype=jnp.float32)[:, None] * inv_freq[None, :]
    cos = jnp.cos(ang)[None, :, None, :]
    sin = jnp.sin(ang)[None, :, None, :]
    tf = t.astype(jnp.float32)
    t1, t2 = tf[..., : dh // 2], tf[..., dh // 2:]
    return jnp.concatenate([t1 * cos - t2 * sin, t1 * sin + t2 * cos], axis=-1)


def _retention(q, k, v):
    b, s, h, dh = q.shape
    nc = s // CHUNK
    log_gamma = jnp.log(1.0 - 2.0 ** (-5.0 - jnp.arange(h, dtype=jnp.float32)))
    idx = jnp.arange(CHUNK, dtype=jnp.float32)
    inner_decay = jnp.exp(log_gamma[:, None, None] * jnp.abs(idx[:, None] - idx[None, :]))
    q_decay = jnp.exp(log_gamma[None, :] * (idx[:, None] + 1.0))
    k_decay = jnp.exp(log_gamma[None, :] * (CHUNK - 1.0 - idx[:, None]))
    chunk_decay = jnp.exp(log_gamma * CHUNK)

    qc = q.reshape(b, nc, CHUNK, h, dh)
    kc = k.reshape(b, nc, CHUNK, h, dh)
    vc = v.reshape(b, nc, CHUNK, h, dh)

    scores = jnp.einsum('bnihd,bnjhd->bnhij', qc, kc) * inner_decay[None, None]
    inner = jnp.einsum('bnhij,bnjhe->bnihe', scores, vc)

    def step(state, inp):
        q_n, k_n, v_n = inp
        cross = jnp.einsum('bihd,bhde->bihe', q_n * q_decay[None, :, :, None], state)
        new_state = state * chunk_decay[None, :, None, None] + jnp.einsum(
            'bjhd,bjhe->bhde', k_n * k_decay[None, :, :, None], v_n)
        return new_state, cross

    state0 = jnp.zeros((b, h, dh, dh), jnp.float32)
    xs = (jnp.moveaxis(qc, 1, 0), jnp.moveaxis(kc, 1, 0), jnp.moveaxis(vc, 1, 0))
    _, cross = lax.scan(step, state0, xs)
    out = inner + jnp.moveaxis(cross, 0, 1)
    return out.reshape(b, s, h, dh)


def _pool_mixer(p, w_pool, pool_scale):
    b, s, _ = p.shape
    pf = p.astype(jnp.float32).reshape(b, s, POOL_GROUPS, POOL_GROUP_DIM)
    cs = jnp.cumsum(pf, axis=1)
    pos = jnp.arange(1, s + 1, dtype=jnp.float32)
    outs = []
    for gi, w in enumerate(POOL_WINDOWS):
        c = cs[:, :, gi]
        prev = jnp.pad(c, ((0, 0), (w, 0), (0, 0)))[:, :s]
        mean = (c - prev) / jnp.minimum(pos, float(w))[None, :, None]
        outs.append(mean - pf[:, :, gi])
    pooled = jnp.stack(outs, axis=2).astype(p.dtype)
    y = jnp.einsum('bsgc,gcd->bsgd', pooled, w_pool).reshape(b, s, POOL_WIDTH)
    return y * pool_scale


def _conv_ffn(x, w_up, conv_w, conv_b, w_down):
    s = x.shape[1]
    u = x @ w_up
    val, gate = u[..., :D_FF], u[..., D_FF:]
    gp = jnp.pad(gate, ((0, 0), (CONV_WIDTH - 1, 0), (0, 0)))
    h = conv_b + sum(gp[:, j:j + s] * conv_w[j] for j in range(CONV_WIDTH))
    return (jax.nn.silu(h) * val) @ w_down


def _fwd_setup_inputs(seed: int = 0) -> dict:
    key = jax.random.key(seed)
    ks = jax.random.split(key, 20)
    f32 = jnp.float32
    L = DEPTH
    x = jax.random.normal(ks[0], (BATCH, SEQ, D_MODEL), f32)
    sd = D_MODEL ** -0.5
    w_qk = jax.random.normal(ks[1], (L, D_MODEL, 2 * RET_WIDTH), f32) * sd
    w_v = jax.random.normal(ks[2], (L, D_MODEL, RET_WIDTH), f32) * sd * DEEPNORM_BETA
    w_g = jax.random.normal(ks[3], (L, D_MODEL, RET_WIDTH), f32) * sd
    w_p = jax.random.normal(ks[4], (L, D_MODEL, POOL_WIDTH), f32) * sd * DEEPNORM_BETA
    w_in = jnp.concatenate([w_qk, w_v, w_g, w_p], axis=-1)
    w_pool = jax.random.normal(ks[5], (L, POOL_GROUPS, POOL_GROUP_DIM, POOL_GROUP_DIM), f32) * POOL_GROUP_DIM ** -0.5
    pool_scale = 1.0 + 0.1 * jax.random.normal(ks[6], (L, POOL_WIDTH), f32)
    w_out = jax.random.normal(ks[7], (L, MIX_WIDTH, D_MODEL), f32) * MIX_WIDTH ** -0.5 * DEEPNORM_BETA
    ln1_g = 1.0 + 0.05 * jax.random.normal(ks[8], (L, D_MODEL), f32)
    ln1_b = 0.02 * jax.random.normal(ks[9], (L, D_MODEL), f32)
    w_up = jax.random.normal(ks[10], (L, D_MODEL, 2 * D_FF), f32) * sd * DEEPNORM_BETA
    conv_w = jax.random.normal(ks[11], (L, CONV_WIDTH, D_FF), f32) * CONV_WIDTH ** -0.5
    conv_b = 0.02 * jax.random.normal(ks[12], (L, D_FF), f32)
    w_down = jax.random.normal(ks[13], (L, D_FF, D_MODEL), f32) * D_FF ** -0.5 * DEEPNORM_BETA
    ln2_g = 1.0 + 0.05 * jax.random.normal(ks[14], (L, D_MODEL), f32)
    ln2_b = 0.02 * jax.random.normal(ks[15], (L, D_MODEL), f32)
    return {"x": x, "w_in": w_in, "w_pool": w_pool, "pool_scale": pool_scale, "w_out": w_out,
            "ln1_g": ln1_g, "ln1_b": ln1_b, "w_up": w_up, "conv_w": conv_w, "conv_b": conv_b,
            "w_down": w_down, "ln2_g": ln2_g, "ln2_b": ln2_b}


def _fwd_reference(x, w_in, w_pool, pool_scale, w_out, ln1_g, ln1_b, w_up, conv_w, conv_b,
              w_down, ln2_g, ln2_b):
    b, s, _ = x.shape
    for l in range(DEPTH):
        proj = x @ w_in[l]
        q, k, v, g, p = jnp.split(proj, [RET_WIDTH, 2 * RET_WIDTH, 3 * RET_WIDTH, 4 * RET_WIDTH], axis=-1)
        q = _rope(q.reshape(b, s, RET_HEADS, RET_HEAD_DIM))
        k = _rope(k.reshape(b, s, RET_HEADS, RET_HEAD_DIM)) * (RET_HEAD_DIM ** -0.5)
        v = v.reshape(b, s, RET_HEADS, RET_HEAD_DIM).astype(jnp.float32)
        ret = _retention(q, k, v)
        ret = ret * lax.rsqrt(jnp.mean(jnp.square(ret), axis=-1, keepdims=True) + RMS_EPS)
        ret = ret.reshape(b, s, RET_WIDTH).astype(x.dtype) * jax.nn.silu(g)
        pool = _pool_mixer(p, w_pool[l], pool_scale[l])
        mix = jnp.concatenate([ret, pool], axis=-1) @ w_out[l]
        x = _layernorm(DEEPNORM_ALPHA * x + mix, ln1_g[l], ln1_b[l])
        ffn = _conv_ffn(x, w_up[l], conv_w[l], conv_b[l], w_down[l])
        x = _layernorm(DEEPNORM_ALPHA * x + ffn, ln2_g[l], ln2_b[l])
    return x


import jax as _jax
import jax.numpy as _jnp

TWIN_FORMAT = 'train_step'
FWD_PARAMS = ['x', 'w_in', 'w_pool', 'pool_scale', 'w_out', 'ln1_g', 'ln1_b', 'w_up', 'conv_w', 'conv_b', 'w_down', 'ln2_g', 'ln2_b']
TWIN_WEIGHTS = ['w_in', 'w_pool', 'pool_scale', 'w_out', 'ln1_g', 'ln1_b', 'w_up', 'conv_w', 'conv_b', 'w_down', 'ln2_g', 'ln2_b']
TWIN_DIFF_INPUT = 'x'
TWIN_INPUTS = ['x', 'w_in', 'w_pool', 'pool_scale', 'w_out', 'ln1_g', 'ln1_b', 'w_up', 'conv_w', 'conv_b', 'w_down', 'ln2_g', 'ln2_b', 'loss_target', 'm_w_in', 'm_w_pool', 'm_pool_scale', 'm_w_out', 'm_ln1_g', 'm_ln1_b', 'm_w_up', 'm_conv_w', 'm_conv_b', 'm_w_down', 'm_ln2_g', 'm_ln2_b', 'v_w_in', 'v_w_pool', 'v_pool_scale', 'v_w_out', 'v_ln1_g', 'v_ln1_b', 'v_w_up', 'v_conv_w', 'v_conv_b', 'v_w_down', 'v_ln2_g', 'v_ln2_b']
TWIN_OUTPUTS = ['loss', 'grad_x', 'grad_w_in', 'grad_w_pool', 'grad_pool_scale', 'grad_w_out', 'grad_ln1_g', 'grad_ln1_b', 'grad_w_up', 'grad_conv_w', 'grad_conv_b', 'grad_w_down', 'grad_ln2_g', 'grad_ln2_b', 'delta_w_in', 'delta_w_pool', 'delta_pool_scale', 'delta_w_out', 'delta_ln1_g', 'delta_ln1_b', 'delta_w_up', 'delta_conv_w', 'delta_conv_b', 'delta_w_down', 'delta_ln2_g', 'delta_ln2_b', 'new_m_w_in', 'new_m_w_pool', 'new_m_pool_scale', 'new_m_w_out', 'new_m_ln1_g', 'new_m_ln1_b', 'new_m_w_up', 'new_m_conv_w', 'new_m_conv_b', 'new_m_w_down', 'new_m_ln2_g', 'new_m_ln2_b', 'new_v_w_in', 'new_v_w_pool', 'new_v_pool_scale', 'new_v_w_out', 'new_v_ln1_g', 'new_v_ln1_b', 'new_v_w_up', 'new_v_conv_w', 'new_v_conv_b', 'new_v_w_down', 'new_v_ln2_g', 'new_v_ln2_b']
TWIN_LEAF_KINDS = {'loss': 'loss', 'grad_x': 'grad_x', 'grad_w_in': 'grad_w', 'grad_w_pool': 'grad_w', 'grad_pool_scale': 'grad_w', 'grad_w_out': 'grad_w', 'grad_ln1_g': 'grad_w', 'grad_ln1_b': 'grad_w', 'grad_w_up': 'grad_w', 'grad_conv_w': 'grad_w', 'grad_conv_b': 'grad_w', 'grad_w_down': 'grad_w', 'grad_ln2_g': 'grad_w', 'grad_ln2_b': 'grad_w', 'delta_w_in': 'delta_w', 'delta_w_pool': 'delta_w', 'delta_pool_scale': 'delta_w', 'delta_w_out': 'delta_w', 'delta_ln1_g': 'delta_w', 'delta_ln1_b': 'delta_w', 'delta_w_up': 'delta_w', 'delta_conv_w': 'delta_w', 'delta_conv_b': 'delta_w', 'delta_w_down': 'delta_w', 'delta_ln2_g': 'delta_w', 'delta_ln2_b': 'delta_w', 'new_m_w_in': 'new_m', 'new_m_w_pool': 'new_m', 'new_m_pool_scale': 'new_m', 'new_m_w_out': 'new_m', 'new_m_ln1_g': 'new_m', 'new_m_ln1_b': 'new_m', 'new_m_w_up': 'new_m', 'new_m_conv_w': 'new_m', 'new_m_conv_b': 'new_m', 'new_m_w_down': 'new_m', 'new_m_ln2_g': 'new_m', 'new_m_ln2_b': 'new_m', 'new_v_w_in': 'new_v', 'new_v_w_pool': 'new_v', 'new_v_pool_scale': 'new_v', 'new_v_w_out': 'new_v', 'new_v_ln1_g': 'new_v', 'new_v_ln1_b': 'new_v', 'new_v_w_up': 'new_v', 'new_v_conv_w': 'new_v', 'new_v_conv_b': 'new_v', 'new_v_w_down': 'new_v', 'new_v_ln2_g': 'new_v', 'new_v_ln2_b': 'new_v'}


def _forward(args):
    return _fwd_reference(*[args[k] for k in FWD_PARAMS])


def _output_shape():
    out = _jax.eval_shape(lambda: _forward(_fwd_setup_inputs(0)))
    return out.shape, out.dtype

N_MICROBATCH = 1
ADAM_LR = 0.001
ADAM_B1 = 0.9
ADAM_B2 = 0.999
ADAM_EPS = 1e-08
ADAM_WD = 0.01
ADAM_STEP = 10
PER_EXAMPLE_BATCH_AXIS = {'x': 0, 'loss_target': 0}
SHARED_INPUTS = []
_WEIGHT_DTYPES = {'w_in': _jnp.float32, 'w_pool': _jnp.float32, 'pool_scale': _jnp.float32, 'w_out': _jnp.float32, 'ln1_g': _jnp.float32, 'ln1_b': _jnp.float32, 'w_up': _jnp.float32, 'conv_w': _jnp.float32, 'conv_b': _jnp.float32, 'w_down': _jnp.float32, 'ln2_g': _jnp.float32, 'ln2_b': _jnp.float32}
MOMENT_SCALE = {'w_in': 6.591695e-02, 'w_pool': 4.657884e-02, 'pool_scale': 4.676556e-02, 'w_out': 8.259978e-02, 'ln1_g': 2.970623e+00, 'ln1_b': 4.588398e-01, 'w_up': 1.831266e-02, 'conv_w': 1.093309e-02, 'conv_b': 1.849506e-02, 'w_down': 2.998194e-02, 'ln2_g': 3.217049e+01, 'ln2_b': 6.301350e-01}


def _to_microbatches(a, axis):
    t = _jnp.moveaxis(a, axis, 0)
    t = t.reshape((N_MICROBATCH, t.shape[0] // N_MICROBATCH) + t.shape[1:])
    return _jnp.moveaxis(t, 1, axis + 1)


def setup_inputs(seed: int = 0) -> dict:
    inp = _fwd_setup_inputs(seed)
    key = _jax.random.fold_in(_jax.random.key(seed), 7919)
    shape, _ = _output_shape()
    out = dict(inp)
    out["loss_target"] = _jax.random.normal(_jax.random.fold_in(key, 0), shape, _jnp.float32)
    for i, name in enumerate(TWIN_WEIGHTS):
        w = inp[name].astype(_jnp.float32)
        if MOMENT_SCALE is None:
            s = _jnp.sqrt(_jnp.mean(_jnp.square(w)) + 1e-30)
        else:
            s = MOMENT_SCALE[name]
        km, kv = _jax.random.split(_jax.random.fold_in(key, i + 1))
        out[name] = w
        out["m_" + name] = s * _jax.random.normal(km, w.shape, _jnp.float32)
        out["v_" + name] = (s * s) * _jax.random.uniform(kv, w.shape, _jnp.float32, 0.5, 1.5)
    if N_MICROBATCH > 1:
        for name, axis in PER_EXAMPLE_BATCH_AXIS.items():
            out[name] = _to_microbatches(out[name], axis)
    return {'x': out['x'], 'w_in': out['w_in'], 'w_pool': out['w_pool'], 'pool_scale': out['pool_scale'], 'w_out': out['w_out'], 'ln1_g': out['ln1_g'], 'ln1_b': out['ln1_b'], 'w_up': out['w_up'], 'conv_w': out['conv_w'], 'conv_b': out['conv_b'], 'w_down': out['w_down'], 'ln2_g': out['ln2_g'], 'ln2_b': out['ln2_b'], 'loss_target': out['loss_target'], 'm_w_in': out['m_w_in'], 'm_w_pool': out['m_w_pool'], 'm_pool_scale': out['m_pool_scale'], 'm_w_out': out['m_w_out'], 'm_ln1_g': out['m_ln1_g'], 'm_ln1_b': out['m_ln1_b'], 'm_w_up': out['m_w_up'], 'm_conv_w': out['m_conv_w'], 'm_conv_b': out['m_conv_b'], 'm_w_down': out['m_w_down'], 'm_ln2_g': out['m_ln2_g'], 'm_ln2_b': out['m_ln2_b'], 'v_w_in': out['v_w_in'], 'v_w_pool': out['v_w_pool'], 'v_pool_scale': out['v_pool_scale'], 'v_w_out': out['v_w_out'], 'v_ln1_g': out['v_ln1_g'], 'v_ln1_b': out['v_ln1_b'], 'v_w_up': out['v_w_up'], 'v_conv_w': out['v_conv_w'], 'v_conv_b': out['v_conv_b'], 'v_w_down': out['v_w_down'], 'v_ln2_g': out['v_ln2_g'], 'v_ln2_b': out['v_ln2_b']}


def _loss(weights, diff, rest, loss_target):
    with _jax.named_scope("forward"):
        args = {**rest, TWIN_DIFF_INPUT: diff, **{k: w.astype(_WEIGHT_DTYPES[k]) for k, w in weights.items()}}
        y = _forward(args)
    with _jax.named_scope("loss_head"):
        err = _jnp.square(y.astype(_jnp.float32) - loss_target)
        return 0.5 * _jnp.sum(_jnp.mean(err, axis=-1)) if err.ndim else 0.5 * err


def _adamw(w, g, m, v):
    m = ADAM_B1 * m + (1.0 - ADAM_B1) * g
    v = ADAM_B2 * v + (1.0 - ADAM_B2) * _jnp.square(g)
    m_hat = m / (1.0 - ADAM_B1 ** ADAM_STEP)
    v_hat = v / (1.0 - ADAM_B2 ** ADAM_STEP)
    delta = -ADAM_LR * (m_hat / (_jnp.sqrt(v_hat) + ADAM_EPS) + ADAM_WD * w)
    return delta, m, v


def reference(x, w_in, w_pool, pool_scale, w_out, ln1_g, ln1_b, w_up, conv_w, conv_b, w_down, ln2_g, ln2_b, loss_target, m_w_in, m_w_pool, m_pool_scale, m_w_out, m_ln1_g, m_ln1_b, m_w_up, m_conv_w, m_conv_b, m_w_down, m_ln2_g, m_ln2_b, v_w_in, v_w_pool, v_pool_scale, v_w_out, v_ln1_g, v_ln1_b, v_w_up, v_conv_w, v_conv_b, v_w_down, v_ln2_g, v_ln2_b):
    given = dict(x=x, w_in=w_in, w_pool=w_pool, pool_scale=pool_scale, w_out=w_out, ln1_g=ln1_g, ln1_b=ln1_b, w_up=w_up, conv_w=conv_w, conv_b=conv_b, w_down=w_down, ln2_g=ln2_g, ln2_b=ln2_b, loss_target=loss_target, m_w_in=m_w_in, m_w_pool=m_w_pool, m_pool_scale=m_pool_scale, m_w_out=m_w_out, m_ln1_g=m_ln1_g, m_ln1_b=m_ln1_b, m_w_up=m_w_up, m_conv_w=m_conv_w, m_conv_b=m_conv_b, m_w_down=m_w_down, m_ln2_g=m_ln2_g, m_ln2_b=m_ln2_b, v_w_in=v_w_in, v_w_pool=v_w_pool, v_pool_scale=v_pool_scale, v_w_out=v_w_out, v_ln1_g=v_ln1_g, v_ln1_b=v_ln1_b, v_w_up=v_w_up, v_conv_w=v_conv_w, v_conv_b=v_conv_b, v_w_down=v_w_down, v_ln2_g=v_ln2_g, v_ln2_b=v_ln2_b)
    weights = {n: given[n] for n in TWIN_WEIGHTS}
    shared = {n: given[n] for n in SHARED_INPUTS}
    per_example = {n: given[n] for n in ['x']}
    grad_fn = _jax.value_and_grad(_loss, argnums=(0, 1))

    def one_microbatch(ex, loss_target):
        ex = dict(ex)
        diff = ex.pop(TWIN_DIFF_INPUT)
        return grad_fn(weights, diff, {**shared, **ex}, loss_target)

    if N_MICROBATCH == 1:
        loss, (grad_w, grad_x) = one_microbatch(per_example, given["loss_target"])
    else:
        def body(carry, xs):
            loss_sum, grad_sum = carry
            l_k, (gw_k, gx_k) = one_microbatch(xs[0], xs[1])
            with _jax.named_scope("update"):
                return (loss_sum + l_k, _jax.tree.map(_jnp.add, grad_sum, gw_k)), gx_k

        init = (_jnp.zeros((), _jnp.float32), _jax.tree.map(_jnp.zeros_like, weights))
        (loss, grad_w), grad_x = _jax.lax.scan(body, init, (per_example, given["loss_target"]))
    with _jax.named_scope("update"):
        delta_w, new_m, new_v = {}, {}, {}
        for n in TWIN_WEIGHTS:
            delta_w[n], new_m[n], new_v[n] = _adamw(weights[n], grad_w[n], given["m_" + n], given["v_" + n])
    return (loss, grad_x, *[grad_w[n] for n in TWIN_WEIGHTS], *[delta_w[n] for n in TWIN_WEIGHTS],
            *[new_m[n] for n in TWIN_WEIGHTS], *[new_v[n] for n in TWIN_WEIGHTS])
```

```python
import functools
import math

import numpy as np
import jax
import jax.numpy as jnp
from jax import lax
from jax.experimental import pallas as pl
from jax.experimental.pallas import tpu as pltpu

f32 = jnp.float32
bf16 = jnp.bfloat16

N_DEV = 8
T = 4096
D = 1024
CHUNK = 64
N_CHUNK = T // CHUNK
HEADS = 4
DH = 128
RW = HEADS * DH
PW = 512
GROUPS = 4
WINDOWS = (2, 4, 8, 16)
IN_W = 4 * RW + PW
D_FF = 2816
LN_EPS = 1e-5
RMS_EPS = 1e-6
ALPHA = 2.0 ** 0.25
K_SCALE = DH ** -0.5

ADAM_LR = 0.001
ADAM_B1 = 0.9
ADAM_B2 = 0.999
ADAM_EPS = 1e-08
ADAM_WD = 0.01
ADAM_STEP = 10

ROWS_IN, ROWS_OUT, ROWS_UP, ROWS_DOWN = IN_W // N_DEV, D // N_DEV, 2 * D_FF // N_DEV, D_FF // N_DEV

V7X_VMEM_LIMIT = 56 * 2 ** 20
HALO = 32

NT = (((1,), (1,)), ((), ()))
TN = (((0,), (0,)), ((), ()))
NN = (((1,), (0,)), ((), ()))


def _dot(a, b, dims=NN):
    return lax.dot_general(a, b, dims, preferred_element_type=f32)


def _const_spec(shape):
    zeros = (0,) * len(shape)
    return pl.BlockSpec(shape, lambda i: zeros, pipeline_mode=pl.Buffered(1))


def _sigmoid(x):
    return 1.0 / (1.0 + jnp.exp(-x))


def _decay_tables():
    h = np.arange(HEADS, dtype=np.float64)
    log_gamma = np.log(1.0 - 2.0 ** (-5.0 - h)).astype(np.float32).astype(np.float64)
    idx = np.arange(CHUNK, dtype=np.float64)
    inner = np.exp(log_gamma[:, None, None] * np.abs(idx[:, None] - idx[None, :]))
    qd = np.exp(log_gamma[:, None] * (idx[None, :] + 1.0))
    kd = np.exp(log_gamma[:, None] * (CHUNK - 1.0 - idx[None, :]))
    cd = np.exp(log_gamma * CHUNK)
    qd = np.broadcast_to(qd[:, :, None], (HEADS, CHUNK, DH))
    kd = np.broadcast_to(kd[:, :, None], (HEADS, CHUNK, DH))
    return (jnp.asarray(inner, f32), jnp.asarray(qd, f32), jnp.asarray(kd, f32), [float(c) for c in cd])


def _rope_tables():
    inv_freq = 10000.0 ** (-jnp.arange(0, DH, 2, dtype=f32) / DH)
    ang = jnp.arange(T, dtype=f32)[:, None] * inv_freq[None, :]
    cos, sin = jnp.cos(ang), jnp.sin(ang)
    return jnp.concatenate([cos, cos], axis=1), jnp.concatenate([-sin, sin], axis=1)


def _swap_halves(t):
    return pltpu.roll(t, DH // 2, axis=1)


def _mix_forward(x, w_in_t, cos, sin, dmat, qd, kd, cdec, w_pool, pool_scale, w_out, ln1_g, ln1_b, tt=512):
    n_tiles = T // tt
    cpt = tt // CHUNK

    def body(x_ref, wint_ref, cos_ref, sin_ref, dmat_ref, qd_ref, kd_ref, wpool_ref, pscale_ref, wout_ref,
             g1_ref, b1_ref,
             qkv_ref, g_ref, oret_ref, states_ref, cat_ref, pooled_ref, xhat_ref, rstd_ref, x1b_ref,
             state_s, pext_s, tmp_s):
        i = pl.program_id(0)

        @pl.when(i == 0)
        def _():
            state_s[...] = jnp.zeros_like(state_s)
            pext_s[pl.ds(0, HALO), :] = jnp.zeros((HALO, PW), f32)

        xb = x_ref[...].astype(bf16)
        cos_t, sin_t = cos_ref[...], sin_ref[...]
        for part in range(2):
            pr = _dot(xb, wint_ref[pl.ds(part * RW, RW), :], NT)
            for h in range(HEADS):
                t = pr[:, h * DH:(h + 1) * DH]
                r = t * cos_t + _swap_halves(t) * sin_t
                if part == 1:
                    r = r * K_SCALE
                qkv_ref[:, part * RW + h * DH: part * RW + (h + 1) * DH] = r.astype(bf16)
        qkv_ref[:, 2 * RW:3 * RW] = _dot(xb, wint_ref[pl.ds(2 * RW, RW), :], NT).astype(bf16)
        g_ref[...] = _dot(xb, wint_ref[pl.ds(3 * RW, RW), :], NT)
        pext_s[pl.ds(HALO, tt), :] = _dot(xb, wint_ref[pl.ds(4 * RW, PW), :], NT)

        def chunk_step(c, carry):
            rows = pl.ds(pl.multiple_of(c * CHUNK, CHUNK), CHUNK)
            for h in range(HEADS):
                q = qkv_ref[rows, h * DH:(h + 1) * DH]
                k = qkv_ref[rows, RW + h * DH: RW + (h + 1) * DH]
                v = qkv_ref[rows, 2 * RW + h * DH: 2 * RW + (h + 1) * DH]
                s = _dot(q, k, NT) * dmat_ref[h]
                inner = _dot(s.astype(bf16), v)
                st = state_s[h]
                stb = st.astype(bf16)
                states_ref[c, h] = stb
                cross = _dot((q.astype(f32) * qd_ref[h]).astype(bf16), stb)
                kdk = (k.astype(f32) * kd_ref[h]).astype(bf16)
                state_s[h] = st * cdec[h] + _dot(kdk, v, TN)
                oret_ref[rows, h * DH:(h + 1) * DH] = inner + cross
            return carry

        lax.fori_loop(0, cpt, chunk_step, 0)

        for h in range(HEADS):
            sl = slice(h * DH, (h + 1) * DH)
            o = oret_ref[:, sl]
            r = lax.rsqrt(jnp.mean(o * o, axis=-1, keepdims=True) + RMS_EPS)
            gg = g_ref[:, sl]
            cat_ref[:, sl] = (o * r * (gg * _sigmoid(gg))).astype(bf16)

        pos1 = (i * tt + lax.broadcasted_iota(jnp.int32, (tt, 1), 0) + 1).astype(f32)
        for gi, w in enumerate(WINDOWS):
            sl = slice(gi * DH, (gi + 1) * DH)
            stages = int(math.log2(w))
            src = pext_s
            for s in range(stages):
                lo = HALO - 8 * (stages - 1 - s)
                n = tt + HALO - lo
                shift = 2 ** s
                val = src[pl.ds(lo, n), sl] + src[pl.ds(lo - shift, n), sl]
                if s == stages - 1:
                    wsum = val
                else:
                    tmp_s[pl.ds(lo, n), sl] = val
                    src = tmp_s
            p_g = pext_s[pl.ds(HALO, tt), sl]
            pooled = (wsum / jnp.minimum(pos1, float(w)) - p_g).astype(bf16)
            pooled_ref[:, sl] = pooled
            y = _dot(pooled, wpool_ref[gi]) * pscale_ref[:, sl]
            cat_ref[:, RW + gi * DH: RW + (gi + 1) * DH] = y.astype(bf16)
        pext_s[pl.ds(0, HALO), :] = pext_s[pl.ds(tt, HALO), :]

        z = ALPHA * x_ref[...] + _dot(cat_ref[...], wout_ref[...])
        mu = jnp.mean(z, axis=-1, keepdims=True)
        zc = z - mu
        rstd = lax.rsqrt(jnp.mean(zc * zc, axis=-1, keepdims=True) + LN_EPS)
        xhat = zc * rstd
        xhat_ref[...] = xhat
        rstd_ref[...] = rstd
        x1b_ref[...] = (xhat * g1_ref[...] + b1_ref[...]).astype(bf16)

    tile = lambda w: pl.BlockSpec((tt, w), lambda i: (i, 0))
    out_shape = (
        jax.ShapeDtypeStruct((T, 3 * RW), bf16),
        jax.ShapeDtypeStruct((T, RW), f32),
        jax.ShapeDtypeStruct((T, RW), f32),
        jax.ShapeDtypeStruct((N_CHUNK, HEADS, DH, DH), bf16),
        jax.ShapeDtypeStruct((T, D), bf16),
        jax.ShapeDtypeStruct((T, PW), bf16),
        jax.ShapeDtypeStruct((T, D), f32),
        jax.ShapeDtypeStruct((T, 1), f32),
        jax.ShapeDtypeStruct((T, D), bf16),
    )
    return pl.pallas_call(
        body, name="mix_forward", grid=(n_tiles,), out_shape=out_shape,
        in_specs=[tile(D), _const_spec((IN_W, D)), tile(DH), tile(DH),
                  _const_spec((HEADS, CHUNK, CHUNK)), _const_spec((HEADS, CHUNK, DH)), _const_spec((HEADS, CHUNK, DH)),
                  _const_spec((GROUPS, DH, DH)), _const_spec((1, PW)), _const_spec((D, D)),
                  _const_spec((1, D)), _const_spec((1, D))],
        out_specs=(tile(3 * RW), tile(RW), tile(RW),
                   pl.BlockSpec((cpt, HEADS, DH, DH), lambda i: (i, 0, 0, 0)),
                   tile(D), tile(PW), tile(D), tile(1), tile(D)),
        scratch_shapes=[pltpu.VMEM((HEADS, DH, DH), f32), pltpu.VMEM((tt + HALO, PW), f32),
                        pltpu.VMEM((tt + HALO, PW), f32)],
        compiler_params=pltpu.CompilerParams(dimension_semantics=("arbitrary",), vmem_limit_bytes=V7X_VMEM_LIMIT),
    )(x, w_in_t, cos, sin, dmat, qd, kd, w_pool, pool_scale, w_out, ln1_g, ln1_b)


def _ffn_forward_backward(xhat1, rstd1, ln1_g, ln1_b, w_up_t, conv_w, conv_b, w_down, ln2_g, ln2_b, target,
                          tt=256, ch=256):
    n_tiles = T // tt
    n_ch = D_FF // ch
    hb = tt // 8

    def body(xhat_ref, halo_ref, rstd_ref, g1_ref, b1_ref, wupt_ref, cw_ref, cb_ref, wdown_ref, g2_ref, b2_ref, tgt_ref,
             dz1_ref, dz2b_ref, du_ref, f_ref, loss_ref, dg2_ref, db2_ref, dg1_ref, db1_ref, dcb_ref, dcw_ref,
             gext_s, val_s, dhext_s, carry_s):
        i = pl.program_id(0)
        tile_idx = n_tiles - 1 - i

        @pl.when(i == 0)
        def _():
            for r in (loss_ref, dg2_ref, db2_ref, dg1_ref, db1_ref, dcb_ref, dcw_ref, carry_s):
                r[...] = jnp.zeros_like(r)

        g1, b1 = g1_ref[...], b1_ref[...]
        xhat = xhat_ref[...]
        x1 = xhat * g1 + b1
        x1b = x1.astype(bf16)
        x1h = ((halo_ref[...] * g1 + b1) * jnp.where(tile_idx == 0, 0.0, 1.0)).astype(bf16)

        ffn = jnp.zeros((tt, D), f32)
        for c in range(n_ch):
            cs = slice(c * ch, (c + 1) * ch)
            wg = wupt_ref[pl.ds(D_FF + c * ch, ch), :]
            val = _dot(x1b, wupt_ref[pl.ds(c * ch, ch), :], NT)
            gext_s[pl.ds(8, tt), cs] = _dot(x1b, wg, NT)
            gext_s[pl.ds(0, 8), cs] = _dot(x1h, wg, NT)
            val_s[:, cs] = val
            hh = (cb_ref[:, cs] + cw_ref[0:1, cs] * gext_s[pl.ds(6, tt), cs] + cw_ref[1:2, cs] * gext_s[pl.ds(7, tt), cs]
                  + cw_ref[2:3, cs] * gext_s[pl.ds(8, tt), cs])
            fc = (hh * _sigmoid(hh) * val).astype(bf16)
            f_ref[:, cs] = fc
            ffn = ffn + _dot(fc, wdown_ref[pl.ds(c * ch, ch), :])

        z = ALPHA * x1 + ffn
        mu = jnp.mean(z, axis=-1, keepdims=True)
        zc = z - mu
        rstd2 = lax.rsqrt(jnp.mean(zc * zc, axis=-1, keepdims=True) + LN_EPS)
        xh2 = zc * rstd2
        diff = xh2 * g2_ref[...] + b2_ref[...] - tgt_ref[...]
        loss_ref[...] += 0.5 * jnp.sum(diff * diff) / D
        dy = diff * (1.0 / D)
        dg2_ref[...] += jnp.sum(dy * xh2, axis=0, keepdims=True)
        db2_ref[...] += jnp.sum(dy, axis=0, keepdims=True)
        dyg = dy * g2_ref[...]
        dz2 = rstd2 * (dyg - jnp.mean(dyg, axis=-1, keepdims=True) - xh2 * jnp.mean(dyg * xh2, axis=-1, keepdims=True))
        dz2b = dz2.astype(bf16)
        dz2b_ref[...] = dz2b

        dx1 = ALPHA * dz2
        dhext_s[pl.ds(tt, 8), :] = carry_s[...]
        for c in range(n_ch):
            cs = slice(c * ch, (c + 1) * ch)
            df = _dot(dz2b, wdown_ref[pl.ds(c * ch, ch), :], NT)
            gm2, gm1, g0 = gext_s[pl.ds(6, tt), cs], gext_s[pl.ds(7, tt), cs], gext_s[pl.ds(8, tt), cs]
            hh = cb_ref[:, cs] + cw_ref[0:1, cs] * gm2 + cw_ref[1:2, cs] * gm1 + cw_ref[2:3, cs] * g0
            sg = _sigmoid(hh)
            dval = df * (hh * sg)
            dh = df * val_s[:, cs] * (sg * (1.0 + hh * (1.0 - sg)))
            dcb_ref[:, cs] += jnp.sum(dh, axis=0, keepdims=True)
            dcw_ref[0:1, cs] += jnp.sum(dh * gm2, axis=0, keepdims=True)
            dcw_ref[1:2, cs] += jnp.sum(dh * gm1, axis=0, keepdims=True)
            dcw_ref[2:3, cs] += jnp.sum(dh * g0, axis=0, keepdims=True)
            dhext_s[pl.ds(0, tt), cs] = dh
            dgate = (cw_ref[2:3, cs] * dh + cw_ref[1:2, cs] * dhext_s[pl.ds(1, tt), cs]
                     + cw_ref[0:1, cs] * dhext_s[pl.ds(2, tt), cs])
            dvalb, dgateb = dval.astype(bf16), dgate.astype(bf16)
            du_ref[:, cs] = dvalb
            du_ref[:, D_FF + c * ch: D_FF + (c + 1) * ch] = dgateb
            dx1 = dx1 + _dot(dvalb, wupt_ref[pl.ds(c * ch, ch), :]) + _dot(dgateb, wupt_ref[pl.ds(D_FF + c * ch, ch), :])
        carry_s[...] = dhext_s[pl.ds(0, 8), :]

        dg1_ref[...] += jnp.sum(dx1 * xhat, axis=0, keepdims=True)
        db1_ref[...] += jnp.sum(dx1, axis=0, keepdims=True)
        dxg = dx1 * g1
        dz1_ref[...] = rstd_ref[...] * (dxg - jnp.mean(dxg, axis=-1, keepdims=True)
                                        - xhat * jnp.mean(dxg * xhat, axis=-1, keepdims=True))

    rtile = lambda w: pl.BlockSpec((tt, w), lambda i: (n_tiles - 1 - i, 0))
    acc = lambda shape: pl.BlockSpec(shape, lambda i: (0, 0))
    out_shape = (
        jax.ShapeDtypeStruct((T, D), f32),
        jax.ShapeDtypeStruct((T, D), bf16),
        jax.ShapeDtypeStruct((T, 2 * D_FF), bf16),
        jax.ShapeDtypeStruct((T, D_FF), bf16),
        jax.ShapeDtypeStruct((8, 128), f32),
        jax.ShapeDtypeStruct((1, D), f32), jax.ShapeDtypeStruct((1, D), f32),
        jax.ShapeDtypeStruct((1, D), f32), jax.ShapeDtypeStruct((1, D), f32),
        jax.ShapeDtypeStruct((1, D_FF), f32), jax.ShapeDtypeStruct((3, D_FF), f32),
    )
    return pl.pallas_call(
        body, name="ffn_forward_backward", grid=(n_tiles,), out_shape=out_shape,
        in_specs=[rtile(D),
                  pl.BlockSpec((8, D), lambda i: (jnp.maximum((n_tiles - 1 - i) * hb - 1, 0), 0)),
                  rtile(1), _const_spec((1, D)), _const_spec((1, D)), _const_spec((2 * D_FF, D)),
                  _const_spec((3, D_FF)), _const_spec((1, D_FF)), _const_spec((D_FF, D)),
                  _const_spec((1, D)), _const_spec((1, D)), rtile(D)],
        out_specs=(rtile(D), rtile(D), rtile(2 * D_FF), rtile(D_FF), acc((8, 128)),
                   acc((1, D)), acc((1, D)), acc((1, D)), acc((1, D)), acc((1, D_FF)), acc((3, D_FF))),
        scratch_shapes=[pltpu.VMEM((tt + 8, D_FF), f32), pltpu.VMEM((tt, D_FF), f32),
                        pltpu.VMEM((tt + 8, D_FF), f32), pltpu.VMEM((8, D_FF), f32)],
        compiler_params=pltpu.CompilerParams(dimension_semantics=("arbitrary",), vmem_limit_bytes=V7X_VMEM_LIMIT),
    )(xhat1, xhat1, rstd1, ln1_g, ln1_b, w_up_t, conv_w, conv_b, w_down, ln2_g, ln2_b, target)


def _mix_backward(dz1, w_out, qkv, g, oret, states, pooled, cos, sin, dmat, qd, kd, cdec, w_pool, pool_scale, w_in_t,
                  tt=512):
    n_tiles = T // tt
    cpt = tt // CHUNK

    def body(dz1_ref, wout_ref, qkv_ref, g_ref, oret_ref, states_ref, pooled_ref, cos_ref, sin_ref, dmat_ref, qd_ref,
             kd_ref, wpool_ref, pscale_ref, wint_ref,
             dproj_ref, gx_ref, dwpool_ref, dpscale_ref,
             dstate_s, dout_s, dqkv_s, eext_s, tmp_s):
        i = pl.program_id(0)
        tile_idx = n_tiles - 1 - i

        @pl.when(i == 0)
        def _():
            dstate_s[...] = jnp.zeros_like(dstate_s)
            dwpool_ref[...] = jnp.zeros_like(dwpool_ref)
            dpscale_ref[...] = jnp.zeros_like(dpscale_ref)
            eext_s[pl.ds(tt, HALO), :] = jnp.zeros((HALO, PW), f32)

        dz1 = dz1_ref[...]
        dcat = _dot(dz1.astype(bf16), wout_ref[...], NT)

        pos1 = (tile_idx * tt + lax.broadcasted_iota(jnp.int32, (tt, 1), 0) + 1).astype(f32)
        for gi, w in enumerate(WINDOWS):
            sl = slice(gi * DH, (gi + 1) * DH)
            dpo = dcat[:, RW + gi * DH: RW + (gi + 1) * DH]
            pooled_g = pooled_ref[:, sl]
            ylin = _dot(pooled_g, wpool_ref[gi])
            dpscale_ref[:, sl] += jnp.sum(dpo * ylin, axis=0, keepdims=True)
            dpw = (dpo * pscale_ref[:, sl]).astype(bf16)
            dwpool_ref[gi] += _dot(pooled_g, dpw, TN)
            dpooled = _dot(dpw, wpool_ref[gi], NT)
            eext_s[pl.ds(0, tt), sl] = dpooled / jnp.minimum(pos1, float(w))
            stages = int(math.log2(w))
            src = eext_s
            for s in range(stages):
                n = tt + 8 * (stages - 1 - s)
                shift = 2 ** s
                val = src[pl.ds(0, n), sl] + src[pl.ds(shift, n), sl]
                if s == stages - 1:
                    wsum = val
                else:
                    tmp_s[pl.ds(0, n), sl] = val
                    src = tmp_s
            dproj_ref[:, 4 * RW + gi * DH: 4 * RW + (gi + 1) * DH] = (wsum - dpooled).astype(bf16)
        eext_s[pl.ds(tt, HALO), :] = eext_s[pl.ds(0, HALO), :]

        for h in range(HEADS):
            sl = slice(h * DH, (h + 1) * DH)
            dr = dcat[:, sl]
            o = oret_ref[:, sl]
            r = lax.rsqrt(jnp.mean(o * o, axis=-1, keepdims=True) + RMS_EPS)
            rn = o * r
            gg = g_ref[:, sl]
            sg = _sigmoid(gg)
            dproj_ref[:, 3 * RW + h * DH: 3 * RW + (h + 1) * DH] = (dr * rn * (sg * (1.0 + gg * (1.0 - sg)))).astype(bf16)
            drn = dr * (gg * sg)
            dout_s[:, sl] = (r * (drn - rn * jnp.mean(drn * rn, axis=-1, keepdims=True))).astype(bf16)

        def chunk_step(j, carry):
            c = cpt - 1 - j
            rows = pl.ds(pl.multiple_of(c * CHUNK, CHUNK), CHUNK)
            for h in range(HEADS):
                q = qkv_ref[rows, h * DH:(h + 1) * DH]
                k = qkv_ref[rows, RW + h * DH: RW + (h + 1) * DH]
                v = qkv_ref[rows, 2 * RW + h * DH: 2 * RW + (h + 1) * DH]
                do = dout_s[rows, h * DH:(h + 1) * DH]
                stb = states_ref[c, h]
                dst = dstate_s[h]
                dstb = dst.astype(bf16)
                dm = dmat_ref[h]
                sb = (_dot(q, k, NT) * dm).astype(bf16)
                dsb = (_dot(do, v, NT) * dm).astype(bf16)
                qdq = (q.astype(f32) * qd_ref[h]).astype(bf16)
                kdk = (k.astype(f32) * kd_ref[h]).astype(bf16)
                dq = _dot(dsb, k) + _dot(do, stb, NT) * qd_ref[h]
                dk = _dot(dsb, q, TN) + _dot(v, dstb, NT) * kd_ref[h]
                dv = _dot(sb, do, TN) + _dot(kdk, dstb)
                dstate_s[h] = dst * cdec[h] + _dot(qdq, do, TN)
                dqkv_s[rows, h * DH:(h + 1) * DH] = dq
                dqkv_s[rows, RW + h * DH: RW + (h + 1) * DH] = dk
                dproj_ref[rows, 2 * RW + h * DH: 2 * RW + (h + 1) * DH] = dv.astype(bf16)
            return carry

        lax.fori_loop(0, cpt, chunk_step, 0)

        cos_t, sin_t = cos_ref[...], sin_ref[...]
        for part in range(2):
            for h in range(HEADS):
                sl = slice(part * RW + h * DH, part * RW + (h + 1) * DH)
                dr = dqkv_s[:, sl]
                dt = dr * cos_t - _swap_halves(dr) * sin_t
                if part == 1:
                    dt = dt * K_SCALE
                dproj_ref[:, sl] = dt.astype(bf16)

        gx_ref[...] = ALPHA * dz1 + _dot(dproj_ref[...], wint_ref[...])

    rtile = lambda w: pl.BlockSpec((tt, w), lambda i: (n_tiles - 1 - i, 0))
    out_shape = (
        jax.ShapeDtypeStruct((T, IN_W), bf16),
        jax.ShapeDtypeStruct((T, D), f32),
        jax.ShapeDtypeStruct((GROUPS, DH, DH), f32),
        jax.ShapeDtypeStruct((1, PW), f32),
    )
    return pl.pallas_call(
        body, name="mix_backward", grid=(n_tiles,), out_shape=out_shape,
        in_specs=[rtile(D), _const_spec((D, D)), rtile(3 * RW), rtile(RW), rtile(RW),
                  pl.BlockSpec((cpt, HEADS, DH, DH), lambda i: (n_tiles - 1 - i, 0, 0, 0)),
                  rtile(PW), rtile(DH), rtile(DH),
                  _const_spec((HEADS, CHUNK, CHUNK)), _const_spec((HEADS, CHUNK, DH)), _const_spec((HEADS, CHUNK, DH)),
                  _const_spec((GROUPS, DH, DH)), _const_spec((1, PW)), _const_spec((IN_W, D))],
        out_specs=(rtile(IN_W), rtile(D), pl.BlockSpec((GROUPS, DH, DH), lambda i: (0, 0, 0)),
                   pl.BlockSpec((1, PW), lambda i: (0, 0))),
        scratch_shapes=[pltpu.VMEM((HEADS, DH, DH), f32), pltpu.VMEM((tt, RW), bf16), pltpu.VMEM((tt, 2 * RW), f32),
                        pltpu.VMEM((tt + HALO, PW), f32), pltpu.VMEM((tt + HALO, PW), f32)],
        compiler_params=pltpu.CompilerParams(dimension_semantics=("arbitrary",), vmem_limit_bytes=V7X_VMEM_LIMIT),
    )(dz1, w_out, qkv, g, oret, states, pooled, cos, sin, dmat, qd, kd, w_pool, pool_scale, w_in_t)


def _weight_grad(a, b, name, tm, tk=1024):
    m = a.shape[1]
    n_k = T // tk

    def body(a_ref, b_ref, o_ref, acc_s):
        k = pl.program_id(1)

        @pl.when(k == 0)
        def _():
            acc_s[...] = jnp.zeros_like(acc_s)

        acc_s[...] += _dot(a_ref[...], b_ref[...].astype(bf16), TN)

        @pl.when(k == n_k - 1)
        def _():
            o_ref[...] = acc_s[...].astype(bf16)

    return pl.pallas_call(
        body, name=name, grid=(m // tm, n_k), out_shape=jax.ShapeDtypeStruct((m, D), bf16),
        in_specs=[pl.BlockSpec((tk, tm), lambda i, k: (k, i)), pl.BlockSpec((tk, D), lambda i, k: (k, 0))],
        out_specs=pl.BlockSpec((tm, D), lambda i, k: (i, 0)),
        scratch_shapes=[pltpu.VMEM((tm, D), f32)],
        compiler_params=pltpu.CompilerParams(dimension_semantics=("parallel", "arbitrary"),
                                             vmem_limit_bytes=V7X_VMEM_LIMIT),
    )(a, b)


def _peers():
    return [(dx, dy, dc) for dx in (0, 1) for dy in (0, 1) for dc in (0, 1) if (dx, dy, dc) != (0, 0, 0)]


def _flip(me, d):
    return tuple(1 - m if f else m for m, f in zip(me, d))


def _slot(dev):
    return 4 * dev[0] + 2 * dev[1] + dev[2]


def _all_gather(blocks, name):
    n = len(blocks)
    peers = _peers()

    def body(*refs):
        ins, outs = refs[:n], refs[n:2 * n]
        send_sems, recv_sems, local_sems = refs[2 * n:]
        me = (lax.axis_index("x"), lax.axis_index("y"), lax.axis_index("c"))
        my_slot = _slot(me)
        copies = []
        for j in range(n):
            own = pltpu.make_async_copy(ins[j], outs[j].at[my_slot], local_sems.at[j])
            own.start()
            copies.append(own)
        remote = []
        for k, d in enumerate(peers):
            to = _flip(me, d)
            for j in range(n):
                cp = pltpu.make_async_remote_copy(
                    src_ref=ins[j], dst_ref=outs[j].at[my_slot], send_sem=send_sems.at[k, j],
                    recv_sem=recv_sems.at[k, j], device_id=to, device_id_type=pl.DeviceIdType.MESH)
                cp.start()
                remote.append(cp)
        for k, d in enumerate(peers):
            frm = _flip(me, d)
            for j in range(n):
                pltpu.make_async_remote_copy(
                    src_ref=ins[j], dst_ref=outs[j].at[_slot(frm)], send_sem=send_sems.at[k, j],
                    recv_sem=recv_sems.at[k, j], device_id=frm, device_id_type=pl.DeviceIdType.MESH).wait_recv()
        for cp in remote:
            cp.wait_send()
        for cp in copies:
            cp.wait()

    hbm = pl.BlockSpec(memory_space=pltpu.HBM)
    return pl.pallas_call(
        body, name=name,
        out_shape=tuple(jax.ShapeDtypeStruct((N_DEV,) + b.shape, b.dtype) for b in blocks),
        in_specs=[hbm] * n, out_specs=tuple([hbm] * n),
        scratch_shapes=[pltpu.SemaphoreType.DMA((7, n)), pltpu.SemaphoreType.DMA((7, n)), pltpu.SemaphoreType.DMA((n,))],
    )(*blocks)


def _all_to_all(parts, name):
    n = len(parts)
    peers = _peers()

    def body(*refs):
        ins, outs = refs[:n], refs[n:2 * n]
        send_sems, recv_sems, local_sems = refs[2 * n:]
        me = (lax.axis_index("x"), lax.axis_index("y"), lax.axis_index("c"))
        my_slot = _slot(me)
        copies = []
        for j in range(n):
            own = pltpu.make_async_copy(ins[j].at[my_slot], outs[j].at[my_slot], local_sems.at[j])
            own.start()
            copies.append(own)
        remote = []
        for k, d in enumerate(peers):
            to = _flip(me, d)
            for j in range(n):
                cp = pltpu.make_async_remote_copy(
                    src_ref=ins[j].at[_slot(to)], dst_ref=outs[j].at[my_slot], send_sem=send_sems.at[k, j],
                    recv_sem=recv_sems.at[k, j], device_id=to, device_id_type=pl.DeviceIdType.MESH)
                cp.start()
                remote.append(cp)
        for k, d in enumerate(peers):
            frm = _flip(me, d)
            for j in range(n):
                pltpu.make_async_remote_copy(
                    src_ref=ins[j].at[my_slot], dst_ref=outs[j].at[_slot(frm)], send_sem=send_sems.at[k, j],
                    recv_sem=recv_sems.at[k, j], device_id=frm, device_id_type=pl.DeviceIdType.MESH).wait_recv()
        for cp in remote:
            cp.wait_send()
        for cp in copies:
            cp.wait()

    hbm = pl.BlockSpec(memory_space=pltpu.HBM)
    return pl.pallas_call(
        body, name=name,
        out_shape=tuple(jax.ShapeDtypeStruct(p.shape, p.dtype) for p in parts),
        in_specs=[hbm] * n, out_specs=tuple([hbm] * n),
        scratch_shapes=[pltpu.SemaphoreType.DMA((7, n)), pltpu.SemaphoreType.DMA((7, n)), pltpu.SemaphoreType.DMA((n,))],
    )(*parts)


def _sum_parts(parts, name):
    def body(*refs):
        n = len(refs) // 2
        for a, o in zip(refs[:n], refs[n:]):
            acc = a[0].astype(f32)
            for s in range(1, N_DEV):
                acc = acc + a[s].astype(f32)
            o[...] = acc

    return pl.pallas_call(
        body, name=name,
        out_shape=tuple(jax.ShapeDtypeStruct(p.shape[1:], f32) for p in parts),
        in_specs=[pl.BlockSpec(memory_space=pltpu.VMEM)] * len(parts),
        out_specs=tuple([pl.BlockSpec(memory_space=pltpu.VMEM)] * len(parts)),
        compiler_params=pltpu.CompilerParams(vmem_limit_bytes=V7X_VMEM_LIMIT),
    )(*parts)


def _adamw(ws, gs, ms, vs, name):
    n = len(ws)
    c1 = 1.0 / (1.0 - ADAM_B1 ** ADAM_STEP)
    c2 = 1.0 / (1.0 - ADAM_B2 ** ADAM_STEP)

    def body(*refs):
        w_r, g_r, m_r, v_r = (refs[k * n:(k + 1) * n] for k in range(4))
        d_o, m_o, v_o = (refs[(4 + k) * n:(5 + k) * n] for k in range(3))
        for j in range(n):
            g = g_r[j][...]
            m = ADAM_B1 * m_r[j][...] + (1.0 - ADAM_B1) * g
            v = ADAM_B2 * v_r[j][...] + (1.0 - ADAM_B2) * (g * g)
            m_o[j][...] = m
            v_o[j][...] = v
            d_o[j][...] = -ADAM_LR * ((m * c1) / (jnp.sqrt(v * c2) + ADAM_EPS) + ADAM_WD * w_r[j][...])

    vm = pl.BlockSpec(memory_space=pltpu.VMEM)
    shapes = tuple(jax.ShapeDtypeStruct(w.shape, f32) for w in ws)
    return pl.pallas_call(
        body, name=name, out_shape=shapes * 3, in_specs=[vm] * (4 * n), out_specs=tuple([vm] * (3 * n)),
        compiler_params=pltpu.CompilerParams(vmem_limit_bytes=V7X_VMEM_LIMIT),
    )(*ws, *gs, *ms, *vs)


SMALL = (("w_pool", GROUPS * DH * DH), ("pool_scale", PW), ("ln1_g", D), ("ln1_b", D), ("conv_b", D_FF),
         ("ln2_g", D), ("ln2_b", D), ("conv_w", 3 * D_FF))
SMALL_ROWS = 640


def _pack(named):
    flat = jnp.concatenate([named[k].reshape(-1) for k, _ in SMALL])
    return jnp.pad(flat, (0, SMALL_ROWS * 128 - flat.shape[0])).reshape(SMALL_ROWS, 128)


def _unpack(packed):
    flat, out, at = packed.reshape(-1), {}, 0
    for k, size in SMALL:
        out[k] = flat[at:at + size]
        at += size
    return out


def kernel(x, w_in, w_pool, pool_scale, w_out, ln1_g, ln1_b, w_up, conv_w, conv_b, w_down, ln2_g, ln2_b, loss_target, m_w_in, m_w_pool, m_pool_scale, m_w_out, m_ln1_g, m_ln1_b, m_w_up, m_conv_w, m_conv_b, m_w_down, m_ln2_g, m_ln2_b, v_w_in, v_w_pool, v_pool_scale, v_w_out, v_ln1_g, v_ln1_b, v_w_up, v_conv_w, v_conv_b, v_w_down, v_ln2_g, v_ln2_b):
    me = 4 * lax.axis_index("x") + 2 * lax.axis_index("y") + lax.axis_index("c")
    x2, tgt = x[0], loss_target[0]

    g_in, g_out, g_up, g_down, g_cw = _all_gather(
        [w_in[0].T.astype(bf16), w_out[0].astype(bf16), w_up[0].T.astype(bf16), w_down[0].astype(bf16), conv_w[0]],
        "gather_weights")
    w_in_t = g_in.reshape(IN_W, D)
    w_out_f = g_out.reshape(D, D)
    w_up_t = g_up.reshape(2 * D_FF, D)
    w_down_f = g_down.reshape(D_FF, D)
    conv_w_f = jnp.transpose(g_cw, (1, 0, 2)).reshape(3, D_FF)
    w_pool_b = w_pool[0].astype(bf16)

    cos, sin = _rope_tables()
    dmat, qd, kd, cdec = _decay_tables()

    qkv, g, oret, states, cat, pooled, xhat1, rstd1, x1b = _mix_forward(
        x2, w_in_t, cos, sin, dmat, qd, kd, cdec, w_pool_b, pool_scale, w_out_f, ln1_g, ln1_b)
    dz1, dz2b, du, f, loss8, d_ln2_g, d_ln2_b, d_ln1_g, d_ln1_b, d_conv_b, d_conv_w = _ffn_forward_backward(
        xhat1, rstd1, ln1_g, ln1_b, w_up_t, conv_w_f, conv_b, w_down_f, ln2_g, ln2_b, tgt)
    dproj, grad_x, d_w_pool, d_pool_scale = _mix_backward(
        dz1, w_out_f, qkv, g, oret, states, pooled, cos, sin, dmat, qd, kd, cdec, w_pool_b, pool_scale, w_in_t)

    dw_in_t = _weight_grad(dproj, x2, "grad_w_in", tm=512).reshape(N_DEV, ROWS_IN, D)
    dw_out = _weight_grad(cat, dz1, "grad_w_out", tm=512).reshape(N_DEV, ROWS_OUT, D)
    dw_up_t = _weight_grad(du, x1b, "grad_w_up", tm=512).reshape(N_DEV, ROWS_UP, D)
    dw_down = _weight_grad(f, dz2b, "grad_w_down", tm=256).reshape(N_DEV, ROWS_DOWN, D)
    small = _pack({"w_pool": d_w_pool, "pool_scale": d_pool_scale, "ln1_g": d_ln1_g, "ln1_b": d_ln1_b,
                   "conv_b": d_conv_b, "ln2_g": d_ln2_g, "ln2_b": d_ln2_b, "conv_w": d_conv_w})

    r_in, r_out, r_up, r_down = _all_to_all([dw_in_t, dw_out, dw_up_t, dw_down], "scatter_grads")
    (r_small,) = _all_gather([small], "gather_small_grads")
    gs_in_t, gs_out, gs_up_t, gs_down, gs_small = _sum_parts([r_in, r_out, r_up, r_down, r_small], "sum_grads")

    gsm = _unpack(gs_small)
    g_w_in, g_w_up = gs_in_t.T, gs_up_t.T
    g_conv_w = lax.dynamic_slice(gsm["conv_w"].reshape(3, D_FF), (0, me * (D_FF // N_DEV)), (3, D_FF // N_DEV))
    two_d = lambda a: a.reshape(-1, a.shape[-1])
    names = ["w_in", "w_pool", "pool_scale", "w_out", "ln1_g", "ln1_b", "w_up", "conv_w", "conv_b", "w_down",
             "ln2_g", "ln2_b"]
    w_d = dict(w_in=w_in, w_pool=w_pool, pool_scale=pool_scale, w_out=w_out, ln1_g=ln1_g, ln1_b=ln1_b, w_up=w_up,
               conv_w=conv_w, conv_b=conv_b, w_down=w_down, ln2_g=ln2_g, ln2_b=ln2_b)
    m_d = dict(w_in=m_w_in, w_pool=m_w_pool, pool_scale=m_pool_scale, w_out=m_w_out, ln1_g=m_ln1_g, ln1_b=m_ln1_b,
               w_up=m_w_up, conv_w=m_conv_w, conv_b=m_conv_b, w_down=m_w_down, ln2_g=m_ln2_g, ln2_b=m_ln2_b)
    v_d = dict(w_in=v_w_in, w_pool=v_w_pool, pool_scale=v_pool_scale, w_out=v_w_out, ln1_g=v_ln1_g, ln1_b=v_ln1_b,
               w_up=v_w_up, conv_w=v_conv_w, conv_b=v_conv_b, w_down=v_w_down, ln2_g=v_ln2_g, ln2_b=v_ln2_b)
    g_d = dict(w_in=g_w_in, w_pool=gsm["w_pool"], pool_scale=gsm["pool_scale"], w_out=gs_out, ln1_g=gsm["ln1_g"],
               ln1_b=gsm["ln1_b"], w_up=g_w_up, conv_w=g_conv_w, conv_b=gsm["conv_b"], w_down=gs_down,
               ln2_g=gsm["ln2_g"], ln2_b=gsm["ln2_b"])
    g_d = {k: g_d[k].reshape(w_d[k].shape) for k in names}
    delta, new_m, new_v = {}, {}, {}
    for part, group in enumerate((["w_up", "w_down"], [k for k in names if k not in ("w_up", "w_down")])):
        res = _adamw([two_d(w_d[k]) for k in group], [two_d(g_d[k]) for k in group], [two_d(m_d[k]) for k in group],
                     [two_d(v_d[k]) for k in group], f"adamw_{part}")
        for j, k in enumerate(group):
            delta[k] = res[j].reshape(w_d[k].shape)
            new_m[k] = res[len(group) + j].reshape(w_d[k].shape)
            new_v[k] = res[2 * len(group) + j].reshape(w_d[k].shape)

    loss = lax.psum(loss8[0, 0], ("x", "y", "c"))
    return (loss, grad_x[None], *[g_d[k] for k in names], *[delta[k] for k in names], *[new_m[k] for k in names],
            *[new_v[k] for k in names])
```

```python
import functools
import math

import numpy as np
import jax
import jax.numpy as jnp
from jax import lax
from jax.experimental import pallas as pl
from jax.experimental.pallas import tpu as pltpu

f32 = jnp.float32
bf16 = jnp.bfloat16

N_DEV = 8
T = 4096
D = 1024
CHUNK = 64
N_CHUNK = T // CHUNK
HEADS = 4
DH = 128
RW = HEADS * DH
PW = 512
GROUPS = 4
WINDOWS = (2, 4, 8, 16)
IN_W = 4 * RW + PW
D_FF = 2816
LN_EPS = 1e-5
RMS_EPS = 1e-6
ALPHA = 2.0 ** 0.25
K_SCALE = DH ** -0.5

ADAM_LR = 0.001
ADAM_B1 = 0.9
ADAM_B2 = 0.999
ADAM_EPS = 1e-08
ADAM_WD = 0.01
ADAM_STEP = 10

ROWS_IN, ROWS_OUT, ROWS_UP, ROWS_DOWN = IN_W // N_DEV, D // N_DEV, 2 * D_FF // N_DEV, D_FF // N_DEV

V7X_VMEM_LIMIT = 56 * 2 ** 20
HALO = 32

NT = (((1,), (1,)), ((), ()))
TN = (((0,), (0,)), ((), ()))
NN = (((1,), (0,)), ((), ()))


def _dot(a, b, dims=NN):
    return lax.dot_general(a, b, dims, preferred_element_type=f32)


def _const_spec(shape):
    zeros = (0,) * len(shape)
    return pl.BlockSpec(shape, lambda i: zeros, pipeline_mode=pl.Buffered(1))


def _sigmoid(x):
    return 1.0 / (1.0 + jnp.exp(-x))


def _decay_tables():
    h = np.arange(HEADS, dtype=np.float64)
    log_gamma = np.log(1.0 - 2.0 ** (-5.0 - h)).astype(np.float32).astype(np.float64)
    idx = np.arange(CHUNK, dtype=np.float64)
    inner = np.exp(log_gamma[:, None, None] * np.abs(idx[:, None] - idx[None, :]))
    qd = np.exp(log_gamma[:, None] * (idx[None, :] + 1.0))
    kd = np.exp(log_gamma[:, None] * (CHUNK - 1.0 - idx[None, :]))
    cd = np.exp(log_gamma * CHUNK)
    qd = np.broadcast_to(qd[:, :, None], (HEADS, CHUNK, DH))
    kd = np.broadcast_to(kd[:, :, None], (HEADS, CHUNK, DH))
    return (jnp.asarray(inner, f32), jnp.asarray(qd, f32), jnp.asarray(kd, f32), [float(c) for c in cd])


def _rope_tables():
    inv_freq = 10000.0 ** (-jnp.arange(0, DH, 2, dtype=f32) / DH)
    ang = jnp.arange(T, dtype=f32)[:, None] * inv_freq[None, :]
    cos, sin = jnp.cos(ang), jnp.sin(ang)
    return jnp.concatenate([cos, cos], axis=1), jnp.concatenate([-sin, sin], axis=1)


def _swap_halves(t):
    return pltpu.roll(t, DH // 2, axis=1)


def _mix_forward(x, w_in_t, cos, sin, dmat, qd, kd, cdec, w_pool, pool_scale, w_out, ln1_g, ln1_b, gather, tt=512):
    n_tiles = T // tt
    cpt = tt // CHUNK
    n_g = len(gather)

    def body(x_ref, wint_ref, cos_ref, sin_ref, dmat_ref, qd_ref, kd_ref, wpool_ref, pscale_ref, wout_ref,
             g1_ref, b1_ref, *rest):
        gin, rest = rest[:n_g], rest[n_g:]
        qkv_ref, g_ref, oret_ref, states_ref, cat_ref, pooled_ref, xhat_ref, rstd_ref, x1b_ref = rest[:9]
        gout, (state_s, pext_s, tmp_s, *sems) = rest[9:9 + n_g], rest[9 + n_g:]
        i = pl.program_id(0)

        @pl.when(i == 0)
        def _():
            state_s[...] = jnp.zeros_like(state_s)
            pext_s[pl.ds(0, HALO), :] = jnp.zeros((HALO, PW), f32)
            _gather_start(gin, gout, *sems)

        @pl.when(i == n_tiles // 2)
        def _():
            _gather_forward(gin, gout, *sems)

        xb = x_ref[...].astype(bf16)
        cos_t, sin_t = cos_ref[...], sin_ref[...]
        for part in range(2):
            pr = _dot(xb, wint_ref[pl.ds(part * RW, RW), :], NT)
            for h in range(HEADS):
                t = pr[:, h * DH:(h + 1) * DH]
                r = t * cos_t + _swap_halves(t) * sin_t
                if part == 1:
                    r = r * K_SCALE
                qkv_ref[:, part * RW + h * DH: part * RW + (h + 1) * DH] = r.astype(bf16)
        qkv_ref[:, 2 * RW:3 * RW] = _dot(xb, wint_ref[pl.ds(2 * RW, RW), :], NT).astype(bf16)
        g_ref[...] = _dot(xb, wint_ref[pl.ds(3 * RW, RW), :], NT)
        pext_s[pl.ds(HALO, tt), :] = _dot(xb, wint_ref[pl.ds(4 * RW, PW), :], NT)

        def chunk_step(c, carry):
            rows = pl.ds(pl.multiple_of(c * CHUNK, CHUNK), CHUNK)
            for h in range(HEADS):
                q = qkv_ref[rows, h * DH:(h + 1) * DH]
                k = qkv_ref[rows, RW + h * DH: RW + (h + 1) * DH]
                v = qkv_ref[rows, 2 * RW + h * DH: 2 * RW + (h + 1) * DH]
                s = _dot(q, k, NT) * dmat_ref[h]
                inner = _dot(s.astype(bf16), v)
                st = state_s[h]
                stb = st.astype(bf16)
                states_ref[c, h] = stb
                cross = _dot((q.astype(f32) * qd_ref[h]).astype(bf16), stb)
                kdk = (k.astype(f32) * kd_ref[h]).astype(bf16)
                state_s[h] = st * cdec[h] + _dot(kdk, v, TN)
                oret_ref[rows, h * DH:(h + 1) * DH] = inner + cross
            return carry

        lax.fori_loop(0, cpt, chunk_step, 0)

        for h in range(HEADS):
            sl = slice(h * DH, (h + 1) * DH)
            o = oret_ref[:, sl]
            r = lax.rsqrt(jnp.mean(o * o, axis=-1, keepdims=True) + RMS_EPS)
            gg = g_ref[:, sl]
            cat_ref[:, sl] = (o * r * (gg * _sigmoid(gg))).astype(bf16)

        pos1 = (i * tt + lax.broadcasted_iota(jnp.int32, (tt, 1), 0) + 1).astype(f32)
        for gi, w in enumerate(WINDOWS):
            sl = slice(gi * DH, (gi + 1) * DH)
            stages = int(math.log2(w))
            src = pext_s
            for s in range(stages):
                lo = HALO - 8 * (stages - 1 - s)
                n = tt + HALO - lo
                shift = 2 ** s
                val = src[pl.ds(lo, n), sl] + src[pl.ds(lo - shift, n), sl]
                if s == stages - 1:
                    wsum = val
                else:
                    tmp_s[pl.ds(lo, n), sl] = val
                    src = tmp_s
            p_g = pext_s[pl.ds(HALO, tt), sl]
            pooled = (wsum / jnp.minimum(pos1, float(w)) - p_g).astype(bf16)
            pooled_ref[:, sl] = pooled
            y = _dot(pooled, wpool_ref[gi]) * pscale_ref[:, sl]
            cat_ref[:, RW + gi * DH: RW + (gi + 1) * DH] = y.astype(bf16)
        pext_s[pl.ds(0, HALO), :] = pext_s[pl.ds(tt, HALO), :]

        z = ALPHA * x_ref[...] + _dot(cat_ref[...], wout_ref[...])
        mu = jnp.mean(z, axis=-1, keepdims=True)
        zc = z - mu
        rstd = lax.rsqrt(jnp.mean(zc * zc, axis=-1, keepdims=True) + LN_EPS)
        xhat = zc * rstd
        xhat_ref[...] = xhat
        rstd_ref[...] = rstd
        x1b_ref[...] = (xhat * g1_ref[...] + b1_ref[...]).astype(bf16)

        @pl.when(i == n_tiles - 1)
        def _():
            _gather_finish(gin, gout, *sems)

    tile = lambda w: pl.BlockSpec((tt, w), lambda i: (i, 0))
    hbm = pl.BlockSpec(memory_space=pltpu.HBM)
    out_shape = (
        jax.ShapeDtypeStruct((T, 3 * RW), bf16),
        jax.ShapeDtypeStruct((T, RW), f32),
        jax.ShapeDtypeStruct((T, RW), f32),
        jax.ShapeDtypeStruct((N_CHUNK, HEADS, DH, DH), bf16),
        jax.ShapeDtypeStruct((T, D), bf16),
        jax.ShapeDtypeStruct((T, PW), bf16),
        jax.ShapeDtypeStruct((T, D), f32),
        jax.ShapeDtypeStruct((T, 1), f32),
        jax.ShapeDtypeStruct((T, D), bf16),
    ) + tuple(jax.ShapeDtypeStruct((N_DEV,) + b.shape, b.dtype) for b in gather)
    return pl.pallas_call(
        body, name="mix_forward", grid=(n_tiles,), out_shape=out_shape,
        in_specs=[tile(D), _const_spec((IN_W, D)), tile(DH), tile(DH),
                  _const_spec((HEADS, CHUNK, CHUNK)), _const_spec((HEADS, CHUNK, DH)), _const_spec((HEADS, CHUNK, DH)),
                  _const_spec((GROUPS, DH, DH)), _const_spec((1, PW)), _const_spec((D, D)),
                  _const_spec((1, D)), _const_spec((1, D))] + [hbm] * n_g,
        out_specs=(tile(3 * RW), tile(RW), tile(RW),
                   pl.BlockSpec((cpt, HEADS, DH, DH), lambda i: (i, 0, 0, 0)),
                   tile(D), tile(PW), tile(D), tile(1), tile(D)) + (hbm,) * n_g,
        scratch_shapes=[pltpu.VMEM((HEADS, DH, DH), f32), pltpu.VMEM((tt + HALO, PW), f32),
                        pltpu.VMEM((tt + HALO, PW), f32)] + _gather_sems(n_g),
        compiler_params=pltpu.CompilerParams(dimension_semantics=("arbitrary",), vmem_limit_bytes=V7X_VMEM_LIMIT),
    )(x, w_in_t, cos, sin, dmat, qd, kd, w_pool, pool_scale, w_out, ln1_g, ln1_b, *gather)


def _ffn_forward_backward(xhat1, rstd1, ln1_g, ln1_b, w_up_t, conv_w, conv_b, w_down, ln2_g, ln2_b, target,
                          tt=256, ch=256):
    n_tiles = T // tt
    n_ch = D_FF // ch
    hb = tt // 8

    def body(xhat_ref, halo_ref, rstd_ref, g1_ref, b1_ref, wupt_ref, cw_ref, cb_ref, wdown_ref, g2_ref, b2_ref, tgt_ref,
             dz1_ref, dz2b_ref, du_ref, f_ref, loss_ref, dg2_ref, db2_ref, dg1_ref, db1_ref, dcb_ref, dcw_ref,
             gext_s, val_s, dhext_s, carry_s):
        i = pl.program_id(0)
        tile_idx = n_tiles - 1 - i

        @pl.when(i == 0)
        def _():
            for r in (loss_ref, dg2_ref, db2_ref, dg1_ref, db1_ref, dcb_ref, dcw_ref, carry_s):
                r[...] = jnp.zeros_like(r)

        g1, b1 = g1_ref[...], b1_ref[...]
        xhat = xhat_ref[...]
        x1 = xhat * g1 + b1
        x1b = x1.astype(bf16)
        x1h = ((halo_ref[...] * g1 + b1) * jnp.where(tile_idx == 0, 0.0, 1.0)).astype(bf16)

        ffn = jnp.zeros((tt, D), f32)
        for c in range(n_ch):
            cs = slice(c * ch, (c + 1) * ch)
            wg = wupt_ref[pl.ds(D_FF + c * ch, ch), :]
            val = _dot(x1b, wupt_ref[pl.ds(c * ch, ch), :], NT)
            gext_s[pl.ds(8, tt), cs] = _dot(x1b, wg, NT)
            gext_s[pl.ds(0, 8), cs] = _dot(x1h, wg, NT)
            val_s[:, cs] = val
            hh = (cb_ref[:, cs] + cw_ref[0:1, cs] * gext_s[pl.ds(6, tt), cs] + cw_ref[1:2, cs] * gext_s[pl.ds(7, tt), cs]
                  + cw_ref[2:3, cs] * gext_s[pl.ds(8, tt), cs])
            fc = (hh * _sigmoid(hh) * val).astype(bf16)
            f_ref[:, cs] = fc
            ffn = ffn + _dot(fc, wdown_ref[pl.ds(c * ch, ch), :])

        z = ALPHA * x1 + ffn
        mu = jnp.mean(z, axis=-1, keepdims=True)
        zc = z - mu
        rstd2 = lax.rsqrt(jnp.mean(zc * zc, axis=-1, keepdims=True) + LN_EPS)
        xh2 = zc * rstd2
        diff = xh2 * g2_ref[...] + b2_ref[...] - tgt_ref[...]
        loss_ref[...] += 0.5 * jnp.sum(diff * diff) / D
        dy = diff * (1.0 / D)
        dg2_ref[...] += jnp.sum(dy * xh2, axis=0, keepdims=True)
        db2_ref[...] += jnp.sum(dy, axis=0, keepdims=True)
        dyg = dy * g2_ref[...]
        dz2 = rstd2 * (dyg - jnp.mean(dyg, axis=-1, keepdims=True) - xh2 * jnp.mean(dyg * xh2, axis=-1, keepdims=True))
        dz2b = dz2.astype(bf16)
        dz2b_ref[...] = dz2b

        dx1 = ALPHA * dz2
        dhext_s[pl.ds(tt, 8), :] = carry_s[...]
        for c in range(n_ch):
            cs = slice(c * ch, (c + 1) * ch)
            df = _dot(dz2b, wdown_ref[pl.ds(c * ch, ch), :], NT)
            gm2, gm1, g0 = gext_s[pl.ds(6, tt), cs], gext_s[pl.ds(7, tt), cs], gext_s[pl.ds(8, tt), cs]
            hh = cb_ref[:, cs] + cw_ref[0:1, cs] * gm2 + cw_ref[1:2, cs] * gm1 + cw_ref[2:3, cs] * g0
            sg = _sigmoid(hh)
            dval = df * (hh * sg)
            dh = df * val_s[:, cs] * (sg * (1.0 + hh * (1.0 - sg)))
            dcb_ref[:, cs] += jnp.sum(dh, axis=0, keepdims=True)
            dcw_ref[0:1, cs] += jnp.sum(dh * gm2, axis=0, keepdims=True)
            dcw_ref[1:2, cs] += jnp.sum(dh * gm1, axis=0, keepdims=True)
            dcw_ref[2:3, cs] += jnp.sum(dh * g0, axis=0, keepdims=True)
            dhext_s[pl.ds(0, tt), cs] = dh
            dgate = (cw_ref[2:3, cs] * dh + cw_ref[1:2, cs] * dhext_s[pl.ds(1, tt), cs]
                     + cw_ref[0:1, cs] * dhext_s[pl.ds(2, tt), cs])
            dvalb, dgateb = dval.astype(bf16), dgate.astype(bf16)
            du_ref[:, cs] = dvalb
            du_ref[:, D_FF + c * ch: D_FF + (c + 1) * ch] = dgateb
            dx1 = dx1 + _dot(dvalb, wupt_ref[pl.ds(c * ch, ch), :]) + _dot(dgateb, wupt_ref[pl.ds(D_FF + c * ch, ch), :])
        carry_s[...] = dhext_s[pl.ds(0, 8), :]

        dg1_ref[...] += jnp.sum(dx1 * xhat, axis=0, keepdims=True)
        db1_ref[...] += jnp.sum(dx1, axis=0, keepdims=True)
        dxg = dx1 * g1
        dz1_ref[...] = rstd_ref[...] * (dxg - jnp.mean(dxg, axis=-1, keepdims=True)
                                        - xhat * jnp.mean(dxg * xhat, axis=-1, keepdims=True))

    rtile = lambda w: pl.BlockSpec((tt, w), lambda i: (n_tiles - 1 - i, 0))
    acc = lambda shape: pl.BlockSpec(shape, lambda i: (0, 0))
    out_shape = (
        jax.ShapeDtypeStruct((T, D), f32),
        jax.ShapeDtypeStruct((T, D), bf16),
        jax.ShapeDtypeStruct((T, 2 * D_FF), bf16),
        jax.ShapeDtypeStruct((T, D_FF), bf16),
        jax.ShapeDtypeStruct((8, 128), f32),
        jax.ShapeDtypeStruct((1, D), f32), jax.ShapeDtypeStruct((1, D), f32),
        jax.ShapeDtypeStruct((1, D), f32), jax.ShapeDtypeStruct((1, D), f32),
        jax.ShapeDtypeStruct((1, D_FF), f32), jax.ShapeDtypeStruct((3, D_FF), f32),
    )
    return pl.pallas_call(
        body, name="ffn_forward_backward", grid=(n_tiles,), out_shape=out_shape,
        in_specs=[rtile(D),
                  pl.BlockSpec((8, D), lambda i: (jnp.maximum((n_tiles - 1 - i) * hb - 1, 0), 0)),
                  rtile(1), _const_spec((1, D)), _const_spec((1, D)), _const_spec((2 * D_FF, D)),
                  _const_spec((3, D_FF)), _const_spec((1, D_FF)), _const_spec((D_FF, D)),
                  _const_spec((1, D)), _const_spec((1, D)), rtile(D)],
        out_specs=(rtile(D), rtile(D), rtile(2 * D_FF), rtile(D_FF), acc((8, 128)),
                   acc((1, D)), acc((1, D)), acc((1, D)), acc((1, D)), acc((1, D_FF)), acc((3, D_FF))),
        scratch_shapes=[pltpu.VMEM((tt + 8, D_FF), f32), pltpu.VMEM((tt, D_FF), f32),
                        pltpu.VMEM((tt + 8, D_FF), f32), pltpu.VMEM((8, D_FF), f32)],
        compiler_params=pltpu.CompilerParams(dimension_semantics=("arbitrary",), vmem_limit_bytes=V7X_VMEM_LIMIT),
    )(xhat1, xhat1, rstd1, ln1_g, ln1_b, w_up_t, conv_w, conv_b, w_down, ln2_g, ln2_b, target)


def _mix_backward(dz1, w_out, qkv, g, oret, states, pooled, cos, sin, dmat, qd, kd, cdec, w_pool, pool_scale, w_in_t,
                  exchange, tt=512):
    n_tiles = T // tt
    cpt = tt // CHUNK
    n_e = len(exchange)

    def body(dz1_ref, wout_ref, qkv_ref, g_ref, oret_ref, states_ref, pooled_ref, cos_ref, sin_ref, dmat_ref, qd_ref,
             kd_ref, wpool_ref, pscale_ref, wint_ref, *rest):
        ein, rest = rest[:n_e], rest[n_e:]
        dproj_ref, gx_ref, dwpool_ref, dpscale_ref = rest[:4]
        eout, (dstate_s, dout_s, dqkv_s, eext_s, tmp_s, *sems) = rest[4:4 + n_e], rest[4 + n_e:]
        i = pl.program_id(0)
        tile_idx = n_tiles - 1 - i

        @pl.when(i == 0)
        def _():
            dstate_s[...] = jnp.zeros_like(dstate_s)
            dwpool_ref[...] = jnp.zeros_like(dwpool_ref)
            dpscale_ref[...] = jnp.zeros_like(dpscale_ref)
            eext_s[pl.ds(tt, HALO), :] = jnp.zeros((HALO, PW), f32)
            _chip_exchange_start(ein, eout, *sems)

        dz1 = dz1_ref[...]
        dcat = _dot(dz1.astype(bf16), wout_ref[...], NT)

        pos1 = (tile_idx * tt + lax.broadcasted_iota(jnp.int32, (tt, 1), 0) + 1).astype(f32)
        for gi, w in enumerate(WINDOWS):
            sl = slice(gi * DH, (gi + 1) * DH)
            dpo = dcat[:, RW + gi * DH: RW + (gi + 1) * DH]
            pooled_g = pooled_ref[:, sl]
            ylin = _dot(pooled_g, wpool_ref[gi])
            dpscale_ref[:, sl] += jnp.sum(dpo * ylin, axis=0, keepdims=True)
            dpw = (dpo * pscale_ref[:, sl]).astype(bf16)
            dwpool_ref[gi] += _dot(pooled_g, dpw, TN)
            dpooled = _dot(dpw, wpool_ref[gi], NT)
            eext_s[pl.ds(0, tt), sl] = dpooled / jnp.minimum(pos1, float(w))
            stages = int(math.log2(w))
            src = eext_s
            for s in range(stages):
                n = tt + 8 * (stages - 1 - s)
                shift = 2 ** s
                val = src[pl.ds(0, n), sl] + src[pl.ds(shift, n), sl]
                if s == stages - 1:
                    wsum = val
                else:
                    tmp_s[pl.ds(0, n), sl] = val
                    src = tmp_s
            dproj_ref[:, 4 * RW + gi * DH: 4 * RW + (gi + 1) * DH] = (wsum - dpooled).astype(bf16)
        eext_s[pl.ds(tt, HALO), :] = eext_s[pl.ds(0, HALO), :]

        for h in range(HEADS):
            sl = slice(h * DH, (h + 1) * DH)
            dr = dcat[:, sl]
            o = oret_ref[:, sl]
            r = lax.rsqrt(jnp.mean(o * o, axis=-1, keepdims=True) + RMS_EPS)
            rn = o * r
            gg = g_ref[:, sl]
            sg = _sigmoid(gg)
            dproj_ref[:, 3 * RW + h * DH: 3 * RW + (h + 1) * DH] = (dr * rn * (sg * (1.0 + gg * (1.0 - sg)))).astype(bf16)
            drn = dr * (gg * sg)
            dout_s[:, sl] = (r * (drn - rn * jnp.mean(drn * rn, axis=-1, keepdims=True))).astype(bf16)

        def chunk_step(j, carry):
            c = cpt - 1 - j
            rows = pl.ds(pl.multiple_of(c * CHUNK, CHUNK), CHUNK)
            for h in range(HEADS):
                q = qkv_ref[rows, h * DH:(h + 1) * DH]
                k = qkv_ref[rows, RW + h * DH: RW + (h + 1) * DH]
                v = qkv_ref[rows, 2 * RW + h * DH: 2 * RW + (h + 1) * DH]
                do = dout_s[rows, h * DH:(h + 1) * DH]
                stb = states_ref[c, h]
                dst = dstate_s[h]
                dstb = dst.astype(bf16)
                dm = dmat_ref[h]
                sb = (_dot(q, k, NT) * dm).astype(bf16)
                dsb = (_dot(do, v, NT) * dm).astype(bf16)
                qdq = (q.astype(f32) * qd_ref[h]).astype(bf16)
                kdk = (k.astype(f32) * kd_ref[h]).astype(bf16)
                dq = _dot(dsb, k) + _dot(do, stb, NT) * qd_ref[h]
                dk = _dot(dsb, q, TN) + _dot(v, dstb, NT) * kd_ref[h]
                dv = _dot(sb, do, TN) + _dot(kdk, dstb)
                dstate_s[h] = dst * cdec[h] + _dot(qdq, do, TN)
                dqkv_s[rows, h * DH:(h + 1) * DH] = dq
                dqkv_s[rows, RW + h * DH: RW + (h + 1) * DH] = dk
                dproj_ref[rows, 2 * RW + h * DH: 2 * RW + (h + 1) * DH] = dv.astype(bf16)
            return carry

        lax.fori_loop(0, cpt, chunk_step, 0)

        cos_t, sin_t = cos_ref[...], sin_ref[...]
        for part in range(2):
            for h in range(HEADS):
                sl = slice(part * RW + h * DH, part * RW + (h + 1) * DH)
                dr = dqkv_s[:, sl]
                dt = dr * cos_t - _swap_halves(dr) * sin_t
                if part == 1:
                    dt = dt * K_SCALE
                dproj_ref[:, sl] = dt.astype(bf16)

        gx_ref[...] = ALPHA * dz1 + _dot(dproj_ref[...], wint_ref[...])

        @pl.when(i == n_tiles - 1)
        def _():
            _chip_exchange_finish(ein, eout, *sems)

    rtile = lambda w: pl.BlockSpec((tt, w), lambda i: (n_tiles - 1 - i, 0))
    hbm = pl.BlockSpec(memory_space=pltpu.HBM)
    out_shape = (
        jax.ShapeDtypeStruct((T, IN_W), bf16),
        jax.ShapeDtypeStruct((T, D), f32),
        jax.ShapeDtypeStruct((GROUPS, DH, DH), f32),
        jax.ShapeDtypeStruct((1, PW), f32),
    ) + tuple(jax.ShapeDtypeStruct(e.shape, e.dtype) for e in exchange)
    return pl.pallas_call(
        body, name="mix_backward", grid=(n_tiles,), out_shape=out_shape,
        in_specs=[rtile(D), _const_spec((D, D)), rtile(3 * RW), rtile(RW), rtile(RW),
                  pl.BlockSpec((cpt, HEADS, DH, DH), lambda i: (n_tiles - 1 - i, 0, 0, 0)),
                  rtile(PW), rtile(DH), rtile(DH),
                  _const_spec((HEADS, CHUNK, CHUNK)), _const_spec((HEADS, CHUNK, DH)), _const_spec((HEADS, CHUNK, DH)),
                  _const_spec((GROUPS, DH, DH)), _const_spec((1, PW)), _const_spec((IN_W, D))] + [hbm] * n_e,
        out_specs=(rtile(IN_W), rtile(D), pl.BlockSpec((GROUPS, DH, DH), lambda i: (0, 0, 0)),
                   pl.BlockSpec((1, PW), lambda i: (0, 0))) + (hbm,) * n_e,
        scratch_shapes=[pltpu.VMEM((HEADS, DH, DH), f32), pltpu.VMEM((tt, RW), bf16), pltpu.VMEM((tt, 2 * RW), f32),
                        pltpu.VMEM((tt + HALO, PW), f32), pltpu.VMEM((tt + HALO, PW), f32)] + _chip_exchange_sems(n_e),
        compiler_params=pltpu.CompilerParams(dimension_semantics=("arbitrary",), vmem_limit_bytes=V7X_VMEM_LIMIT),
    )(dz1, w_out, qkv, g, oret, states, pooled, cos, sin, dmat, qd, kd, w_pool, pool_scale, w_in_t, *exchange)


def _weight_grad(a, b, name, tm, tk=1024):
    m = a.shape[1]
    n_k = T // tk

    def body(a_ref, b_ref, o_ref, acc_s):
        k = pl.program_id(1)

        @pl.when(k == 0)
        def _():
            acc_s[...] = jnp.zeros_like(acc_s)

        acc_s[...] += _dot(a_ref[...], b_ref[...].astype(bf16), TN)

        @pl.when(k == n_k - 1)
        def _():
            o_ref[...] = acc_s[...].astype(bf16)

    return pl.pallas_call(
        body, name=name, grid=(m // tm, n_k), out_shape=jax.ShapeDtypeStruct((m, D), bf16),
        in_specs=[pl.BlockSpec((tk, tm), lambda i, k: (k, i)), pl.BlockSpec((tk, D), lambda i, k: (k, 0))],
        out_specs=pl.BlockSpec((tm, D), lambda i, k: (i, 0)),
        scratch_shapes=[pltpu.VMEM((tm, D), f32)],
        compiler_params=pltpu.CompilerParams(dimension_semantics=("parallel", "arbitrary"),
                                             vmem_limit_bytes=V7X_VMEM_LIMIT),
    )(a, b)


CHIP_FLIPS = ((1, 0), (0, 1), (1, 1))


def _me():
    return lax.axis_index("x"), lax.axis_index("y"), lax.axis_index("c")


def _chip(me, k):
    x, y, _ = me
    if k == 0:
        return x, y
    fx, fy = CHIP_FLIPS[k - 1]
    return (1 - x if fx else x), (1 - y if fy else y)


def _slot(x, y, c):
    return 4 * x + 2 * y + c


def _remote(src, dst, send_sem, recv_sem, to):
    return pltpu.make_async_remote_copy(src_ref=src, dst_ref=dst, send_sem=send_sem, recv_sem=recv_sem,
                                        device_id=to, device_id_type=pl.DeviceIdType.MESH)


def _gather_sems(n):
    return [pltpu.SemaphoreType.DMA((7, n)), pltpu.SemaphoreType.DMA((7, n)), pltpu.SemaphoreType.DMA((n,))] if n else []


def _gather_copy(k, j, gin, gout, send_sems, recv_sems, sending):
    me = _me()
    x, y, c = me
    sibling = (x, y, 1 - c)
    src, to = gin[j], sibling
    if sending:
        block = me if k <= 3 else (*_chip(me, k - 3), c)
        if 1 <= k <= 3:
            to = (*_chip(me, k), c)
        if k >= 4:
            src = gout[j].at[_slot(*block)]
    else:
        block = sibling if k == 0 else (*_chip(me, k), c) if k <= 3 else (*_chip(me, k - 3), 1 - c)
    return _remote(src, gout[j].at[_slot(*block)], send_sems.at[k, j], recv_sems.at[k, j], to)


def _gather_start(gin, gout, send_sems, recv_sems, local_sems):
    for j in range(len(gin)):
        pltpu.make_async_copy(gin[j], gout[j].at[_slot(*_me())], local_sems.at[j]).start()
    for k in range(4):
        for j in range(len(gin)):
            _gather_copy(k, j, gin, gout, send_sems, recv_sems, True).start()


def _gather_forward(gin, gout, send_sems, recv_sems, local_sems):
    for k in range(1, 4):
        for j in range(len(gin)):
            _gather_copy(k, j, gin, gout, send_sems, recv_sems, False).wait_recv()
            _gather_copy(k + 3, j, gin, gout, send_sems, recv_sems, True).start()


def _gather_finish(gin, gout, send_sems, recv_sems, local_sems):
    for k in (0, 4, 5, 6):
        for j in range(len(gin)):
            _gather_copy(k, j, gin, gout, send_sems, recv_sems, False).wait_recv()
    for k in range(7):
        for j in range(len(gin)):
            _gather_copy(k, j, gin, gout, send_sems, recv_sems, True).wait_send()
    for j in range(len(gin)):
        pltpu.make_async_copy(gin[j], gout[j].at[_slot(*_me())], local_sems.at[j]).wait()


def _all_gather(blocks, name):
    n = len(blocks)

    def body(*refs):
        gin, gout, sems = refs[:n], refs[n:2 * n], refs[2 * n:]
        _gather_start(gin, gout, *sems)
        _gather_forward(gin, gout, *sems)
        _gather_finish(gin, gout, *sems)

    hbm = pl.BlockSpec(memory_space=pltpu.HBM)
    return pl.pallas_call(
        body, name=name,
        out_shape=tuple(jax.ShapeDtypeStruct((N_DEV,) + b.shape, b.dtype) for b in blocks),
        in_specs=[hbm] * n, out_specs=(hbm,) * n, scratch_shapes=_gather_sems(n),
    )(*blocks)


def _pair_reduce(parts, name):
    n = len(parts)

    def body(*refs):
        ins, own, others, landing = (refs[k * n:(k + 1) * n] for k in range(4))
        send_sems, recv_sems = refs[4 * n:]
        me = _me()
        x, y, c = me
        sibling = (x, y, 1 - c)
        sends = []
        for k in range(4):
            for j in range(n):
                cp = _remote(ins[j].at[_slot(*_chip(me, k), 1 - c)], landing[j].at[k], send_sems.at[k, j],
                             recv_sems.at[k, j], sibling)
                cp.start()
                sends.append(cp)
        for k in range(4):
            for j in range(n):
                _remote(ins[j].at[0], landing[j].at[k], send_sems.at[k, j], recv_sems.at[k, j], sibling).wait_recv()
                total = ins[j][_slot(*_chip(me, k), c)].astype(f32) + landing[j][k].astype(f32)
                if k == 0:
                    own[j][...] = total.astype(own[j].dtype)
                else:
                    others[j][k - 1] = total.astype(others[j].dtype)
        for cp in sends:
            cp.wait_send()

    vm = pl.BlockSpec(memory_space=pltpu.VMEM)
    return pl.pallas_call(
        body, name=name,
        out_shape=tuple(jax.ShapeDtypeStruct(p.shape[1:], p.dtype) for p in parts)
        + tuple(jax.ShapeDtypeStruct((3,) + p.shape[1:], p.dtype) for p in parts),
        in_specs=[vm] * n, out_specs=(vm,) * (2 * n),
        scratch_shapes=[pltpu.VMEM((4,) + p.shape[1:], p.dtype) for p in parts]
        + [pltpu.SemaphoreType.DMA((4, n)), pltpu.SemaphoreType.DMA((4, n))],
        compiler_params=pltpu.CompilerParams(vmem_limit_bytes=V7X_VMEM_LIMIT),
    )(*parts)


def _chip_exchange_sems(n):
    return [pltpu.SemaphoreType.DMA((3, n)), pltpu.SemaphoreType.DMA((3, n))] if n else []


def _chip_exchange_copy(k, j, ein, eout, send_sems, recv_sems):
    me = _me()
    return _remote(ein[j].at[k - 1], eout[j].at[k - 1], send_sems.at[k - 1, j], recv_sems.at[k - 1, j],
                   (*_chip(me, k), me[2]))


def _chip_exchange_start(ein, eout, send_sems, recv_sems):
    for k in range(1, 4):
        for j in range(len(ein)):
            _chip_exchange_copy(k, j, ein, eout, send_sems, recv_sems).start()


def _chip_exchange_finish(ein, eout, send_sems, recv_sems):
    for k in range(1, 4):
        for j in range(len(ein)):
            _chip_exchange_copy(k, j, ein, eout, send_sems, recv_sems).wait_recv()
    for k in range(1, 4):
        for j in range(len(ein)):
            _chip_exchange_copy(k, j, ein, eout, send_sems, recv_sems).wait_send()


def _chip_exchange(others, name):
    n = len(others)

    def body(*refs):
        ein, eout, sems = refs[:n], refs[n:2 * n], refs[2 * n:]
        _chip_exchange_start(ein, eout, *sems)
        _chip_exchange_finish(ein, eout, *sems)

    hbm = pl.BlockSpec(memory_space=pltpu.HBM)
    return pl.pallas_call(
        body, name=name, out_shape=tuple(jax.ShapeDtypeStruct(e.shape, e.dtype) for e in others),
        in_specs=[hbm] * n, out_specs=(hbm,) * n, scratch_shapes=_chip_exchange_sems(n),
    )(*others)


def _sum_parts(owns, arrived, name):
    n = len(owns)

    def body(*refs):
        for own, arr, out in zip(refs[:n], refs[n:2 * n], refs[2 * n:]):
            acc = own[...].astype(f32)
            for k in range(3):
                acc = acc + arr[k].astype(f32)
            out[...] = acc

    vm = pl.BlockSpec(memory_space=pltpu.VMEM)
    return pl.pallas_call(
        body, name=name, out_shape=tuple(jax.ShapeDtypeStruct(o.shape, f32) for o in owns),
        in_specs=[vm] * (2 * n), out_specs=(vm,) * n,
        compiler_params=pltpu.CompilerParams(vmem_limit_bytes=V7X_VMEM_LIMIT),
    )(*owns, *arrived)


def _adamw(ws, gs, ms, vs, name):
    n = len(ws)
    c1 = 1.0 / (1.0 - ADAM_B1 ** ADAM_STEP)
    c2 = 1.0 / (1.0 - ADAM_B2 ** ADAM_STEP)

    def body(*refs):
        w_r, g_r, m_r, v_r = (refs[k * n:(k + 1) * n] for k in range(4))
        d_o, m_o, v_o = (refs[(4 + k) * n:(5 + k) * n] for k in range(3))
        for j in range(n):
            g = g_r[j][...]
            m = ADAM_B1 * m_r[j][...] + (1.0 - ADAM_B1) * g
            v = ADAM_B2 * v_r[j][...] + (1.0 - ADAM_B2) * (g * g)
            m_o[j][...] = m
            v_o[j][...] = v
            d_o[j][...] = -ADAM_LR * ((m * c1) / (jnp.sqrt(v * c2) + ADAM_EPS) + ADAM_WD * w_r[j][...])

    vm = pl.BlockSpec(memory_space=pltpu.VMEM)
    shapes = tuple(jax.ShapeDtypeStruct(w.shape, f32) for w in ws)
    return pl.pallas_call(
        body, name=name, out_shape=shapes * 3, in_specs=[vm] * (4 * n), out_specs=tuple([vm] * (3 * n)),
        compiler_params=pltpu.CompilerParams(vmem_limit_bytes=V7X_VMEM_LIMIT),
    )(*ws, *gs, *ms, *vs)


SMALL = (("w_pool", GROUPS * DH * DH), ("pool_scale", PW), ("ln1_g", D), ("ln1_b", D), ("conv_b", D_FF),
         ("ln2_g", D), ("ln2_b", D), ("conv_w", 3 * D_FF))
SMALL_ROWS = 640


def _pack(named):
    flat = jnp.concatenate([named[k].reshape(-1) for k, _ in SMALL])
    return jnp.pad(flat, (0, SMALL_ROWS * 128 - flat.shape[0])).reshape(SMALL_ROWS, 128)


def _unpack(packed):
    flat, out, at = packed.reshape(-1), {}, 0
    for k, size in SMALL:
        out[k] = flat[at:at + size]
        at += size
    return out


def kernel(x, w_in, w_pool, pool_scale, w_out, ln1_g, ln1_b, w_up, conv_w, conv_b, w_down, ln2_g, ln2_b, loss_target, m_w_in, m_w_pool, m_pool_scale, m_w_out, m_ln1_g, m_ln1_b, m_w_up, m_conv_w, m_conv_b, m_w_down, m_ln2_g, m_ln2_b, v_w_in, v_w_pool, v_pool_scale, v_w_out, v_ln1_g, v_ln1_b, v_w_up, v_conv_w, v_conv_b, v_w_down, v_ln2_g, v_ln2_b):
    me = 4 * lax.axis_index("x") + 2 * lax.axis_index("y") + lax.axis_index("c")
    x2, tgt = x[0], loss_target[0]

    g_in, g_out, g_cw = _all_gather([w_in[0].T.astype(bf16), w_out[0].astype(bf16), conv_w[0]], "gather_weights")
    w_in_t = g_in.reshape(IN_W, D)
    w_out_f = g_out.reshape(D, D)
    conv_w_f = jnp.transpose(g_cw, (1, 0, 2)).reshape(3, D_FF)
    w_pool_b = w_pool[0].astype(bf16)

    cos, sin = _rope_tables()
    dmat, qd, kd, cdec = _decay_tables()

    qkv, g, oret, states, cat, pooled, xhat1, rstd1, x1b, g_up, g_down = _mix_forward(
        x2, w_in_t, cos, sin, dmat, qd, kd, cdec, w_pool_b, pool_scale, w_out_f, ln1_g, ln1_b,
        gather=[w_up[0].T.astype(bf16), w_down[0].astype(bf16)])
    w_up_t = g_up.reshape(2 * D_FF, D)
    w_down_f = g_down.reshape(D_FF, D)
    dz1, dz2b, du, f, loss8, d_ln2_g, d_ln2_b, d_ln1_g, d_ln1_b, d_conv_b, d_conv_w = _ffn_forward_backward(
        xhat1, rstd1, ln1_g, ln1_b, w_up_t, conv_w_f, conv_b, w_down_f, ln2_g, ln2_b, tgt)

    dw_up_t = _weight_grad(du, x1b, "grad_w_up", tm=512).reshape(N_DEV, ROWS_UP, D)
    dw_down = _weight_grad(f, dz2b, "grad_w_down", tm=256).reshape(N_DEV, ROWS_DOWN, D)
    own_up, own_down, oth_up, oth_down = _pair_reduce([dw_up_t, dw_down], "pair_reduce_ffn")
    dproj, grad_x, d_w_pool, d_pool_scale, arr_up, arr_down = _mix_backward(
        dz1, w_out_f, qkv, g, oret, states, pooled, cos, sin, dmat, qd, kd, cdec, w_pool_b, pool_scale, w_in_t,
        exchange=[oth_up, oth_down])
    dw_in_t = _weight_grad(dproj, x2, "grad_w_in", tm=512).reshape(N_DEV, ROWS_IN, D)
    dw_out = _weight_grad(cat, dz1, "grad_w_out", tm=512).reshape(N_DEV, ROWS_OUT, D)
    small = _pack({"w_pool": d_w_pool, "pool_scale": d_pool_scale, "ln1_g": d_ln1_g, "ln1_b": d_ln1_b,
                   "conv_b": d_conv_b, "ln2_g": d_ln2_g, "ln2_b": d_ln2_b, "conv_w": d_conv_w})
    own_in, own_out, own_small, oth_in, oth_out, oth_small = _pair_reduce(
        [dw_in_t, dw_out, small.reshape(N_DEV, SMALL_ROWS // N_DEV, 128)], "pair_reduce_mix")
    arr_in, arr_out, arr_small = _chip_exchange([oth_in, oth_out, oth_small], "exchange_mix")
    gs_in_t, gs_out, gs_up_t, gs_down, small_piece = _sum_parts(
        [own_in, own_out, own_up, own_down, own_small], [arr_in, arr_out, arr_up, arr_down, arr_small], "sum_grads")
    (gs_small,) = _all_gather([small_piece], "gather_small_grads")

    gsm = _unpack(gs_small)
    g_w_in, g_w_up = gs_in_t.T, gs_up_t.T
    g_conv_w = lax.dynamic_slice(gsm["conv_w"].reshape(3, D_FF), (0, me * (D_FF // N_DEV)), (3, D_FF // N_DEV))
    two_d = lambda a: a.reshape(-1, a.shape[-1])
    names = ["w_in", "w_pool", "pool_scale", "w_out", "ln1_g", "ln1_b", "w_up", "conv_w", "conv_b", "w_down",
             "ln2_g", "ln2_b"]
    w_d = dict(w_in=w_in, w_pool=w_pool, pool_scale=pool_scale, w_out=w_out, ln1_g=ln1_g, ln1_b=ln1_b, w_up=w_up,
               conv_w=conv_w, conv_b=conv_b, w_down=w_down, ln2_g=ln2_g, ln2_b=ln2_b)
    m_d = dict(w_in=m_w_in, w_pool=m_w_pool, pool_scale=m_pool_scale, w_out=m_w_out, ln1_g=m_ln1_g, ln1_b=m_ln1_b,
               w_up=m_w_up, conv_w=m_conv_w, conv_b=m_conv_b, w_down=m_w_down, ln2_g=m_ln2_g, ln2_b=m_ln2_b)
    v_d = dict(w_in=v_w_in, w_pool=v_w_pool, pool_scale=v_pool_scale, w_out=v_w_out, ln1_g=v_ln1_g, ln1_b=v_ln1_b,
               w_up=v_w_up, conv_w=v_conv_w, conv_b=v_conv_b, w_down=v_w_down, ln2_g=v_ln2_g, ln2_b=v_ln2_b)
    g_d = dict(w_in=g_w_in, w_pool=gsm["w_pool"], pool_scale=gsm["pool_scale"], w_out=gs_out, ln1_g=gsm["ln1_g"],
               ln1_b=gsm["ln1_b"], w_up=g_w_up, conv_w=g_conv_w, conv_b=gsm["conv_b"], w_down=gs_down,
               ln2_g=gsm["ln2_g"], ln2_b=gsm["ln2_b"])
    g_d = {k: g_d[k].reshape(w_d[k].shape) for k in names}
    delta, new_m, new_v = {}, {}, {}
    for part, group in enumerate((["w_up", "w_down"], [k for k in names if k not in ("w_up", "w_down")])):
        res = _adamw([two_d(w_d[k]) for k in group], [two_d(g_d[k]) for k in group], [two_d(m_d[k]) for k in group],
                     [two_d(v_d[k]) for k in group], f"adamw_{part}")
        for j, k in enumerate(group):
            delta[k] = res[j].reshape(w_d[k].shape)
            new_m[k] = res[len(group) + j].reshape(w_d[k].shape)
            new_v[k] = res[2 * len(group) + j].reshape(w_d[k].shape)

    loss = lax.psum(loss8[0, 0], ("x", "y", "c"))
    return (loss, grad_x[None], *[g_d[k] for k in names], *[delta[k] for k in names], *[new_m[k] for k in names],
            *[new_v[k] for k in names])
```

```python
import functools
import math

import numpy as np
import jax
import jax.numpy as jnp
from jax import lax
from jax.experimental import pallas as pl
from jax.experimental.pallas import tpu as pltpu

f32 = jnp.float32
bf16 = jnp.bfloat16

N_DEV = 8
T = 4096
D = 1024
CHUNK = 64
N_CHUNK = T // CHUNK
HEADS = 4
DH = 128
RW = HEADS * DH
PW = 512
GROUPS = 4
WINDOWS = (2, 4, 8, 16)
IN_W = 4 * RW + PW
D_FF = 2816
LN_EPS = 1e-5
RMS_EPS = 1e-6
ALPHA = 2.0 ** 0.25
K_SCALE = DH ** -0.5

ADAM_LR = 0.001
ADAM_B1 = 0.9
ADAM_B2 = 0.999
ADAM_EPS = 1e-08
ADAM_WD = 0.01
ADAM_STEP = 10

ROWS_IN, ROWS_OUT, ROWS_UP, ROWS_DOWN = IN_W // N_DEV, D // N_DEV, 2 * D_FF // N_DEV, D_FF // N_DEV

V7X_VMEM_LIMIT = 56 * 2 ** 20
HALO = 32

NT = (((1,), (1,)), ((), ()))
TN = (((0,), (0,)), ((), ()))
NN = (((1,), (0,)), ((), ()))


def _dot(a, b, dims=NN):
    return lax.dot_general(a, b, dims, preferred_element_type=f32)


def _const_spec(shape):
    zeros = (0,) * len(shape)
    return pl.BlockSpec(shape, lambda i: zeros, pipeline_mode=pl.Buffered(1))


def _sigmoid(x):
    return 0.5 * jnp.tanh(0.5 * x) + 0.5


def _decay_tables():
    h = np.arange(HEADS, dtype=np.float64)
    log_gamma = np.log(1.0 - 2.0 ** (-5.0 - h)).astype(np.float32).astype(np.float64)
    idx = np.arange(CHUNK, dtype=np.float64)
    inner = np.exp(log_gamma[:, None, None] * np.abs(idx[:, None] - idx[None, :]))
    qd = np.exp(log_gamma[:, None] * (idx[None, :] + 1.0))
    kd = np.exp(log_gamma[:, None] * (CHUNK - 1.0 - idx[None, :]))
    cd = np.exp(log_gamma * CHUNK)
    qd = np.broadcast_to(qd[:, :, None], (HEADS, CHUNK, DH))
    kd = np.broadcast_to(kd[:, :, None], (HEADS, CHUNK, DH))
    return (jnp.asarray(inner, f32), jnp.asarray(qd, f32), jnp.asarray(kd, f32), [float(c) for c in cd])


def _rope_tables():
    inv_freq = (10000.0 ** (-np.arange(0, DH, 2, dtype=np.float64) / DH)).astype(np.float32)
    ang = (np.arange(T, dtype=np.float32)[:, None] * inv_freq[None, :]).astype(np.float64)
    cos, sin = np.cos(ang), np.sin(ang)
    return (jnp.asarray(np.concatenate([cos, cos], axis=1), f32), jnp.asarray(np.concatenate([-sin, sin], axis=1), f32))


def _swap_halves(t):
    return pltpu.roll(t, DH // 2, axis=1)


def _mix_forward(x, w_in_t, cos, sin, dmat, qd, kd, cdec, w_pool, pool_scale, w_out, ln1_g, ln1_b, gather, tt=512):
    n_tiles = T // tt
    cpt = tt // CHUNK
    n_g = len(gather)

    def body(x_ref, wint_ref, cos_ref, sin_ref, dmat_ref, qd_ref, kd_ref, wpool_ref, pscale_ref, wout_ref,
             g1_ref, b1_ref, *rest):
        gin, rest = rest[:n_g], rest[n_g:]
        qkv_ref, g_ref, oret_ref, states_ref, cat_ref, pooled_ref, xhat_ref, rstd_ref, x1b_ref = rest[:9]
        gout, (state_s, pext_s, tmp_s, *sems) = rest[9:9 + n_g], rest[9 + n_g:]
        i = pl.program_id(0)

        @pl.when(i == 0)
        def _():
            state_s[...] = jnp.zeros_like(state_s)
            pext_s[pl.ds(0, HALO), :] = jnp.zeros((HALO, PW), f32)
            _gather_start(gin, gout, *sems)

        @pl.when(i == n_tiles - 1)
        def _():
            _gather_forward(gin, gout, *sems)

        xb = x_ref[...].astype(bf16)
        cos_t, sin_t = cos_ref[...], sin_ref[...]
        for part in range(2):
            pr = _dot(xb, wint_ref[pl.ds(part * RW, RW), :], NT)
            for h in range(HEADS):
                t = pr[:, h * DH:(h + 1) * DH]
                r = t * cos_t + _swap_halves(t) * sin_t
                if part == 1:
                    r = r * K_SCALE
                qkv_ref[:, part * RW + h * DH: part * RW + (h + 1) * DH] = r.astype(bf16)
        qkv_ref[:, 2 * RW:3 * RW] = _dot(xb, wint_ref[pl.ds(2 * RW, RW), :], NT).astype(bf16)
        g_ref[...] = _dot(xb, wint_ref[pl.ds(3 * RW, RW), :], NT)
        pext_s[pl.ds(HALO, tt), :] = _dot(xb, wint_ref[pl.ds(4 * RW, PW), :], NT)

        def chunk_step(c, carry):
            rows = pl.ds(pl.multiple_of(c * CHUNK, CHUNK), CHUNK)
            for h in range(HEADS):
                q = qkv_ref[rows, h * DH:(h + 1) * DH]
                k = qkv_ref[rows, RW + h * DH: RW + (h + 1) * DH]
                v = qkv_ref[rows, 2 * RW + h * DH: 2 * RW + (h + 1) * DH]
                s = _dot(q, k, NT) * dmat_ref[h]
                inner = _dot(s.astype(bf16), v)
                st = state_s[h]
                stb = st.astype(bf16)
                states_ref[c, h] = stb
                cross = _dot((q.astype(f32) * qd_ref[h]).astype(bf16), stb)
                kdk = (k.astype(f32) * kd_ref[h]).astype(bf16)
                state_s[h] = st * cdec[h] + _dot(kdk, v, TN)
                oret_ref[rows, h * DH:(h + 1) * DH] = inner + cross
            return carry

        lax.fori_loop(0, cpt, chunk_step, 0)

        for h in range(HEADS):
            sl = slice(h * DH, (h + 1) * DH)
            o = oret_ref[:, sl]
            r = lax.rsqrt(jnp.mean(o * o, axis=-1, keepdims=True) + RMS_EPS)
            gg = g_ref[:, sl]
            cat_ref[:, sl] = (o * r * (gg * _sigmoid(gg))).astype(bf16)

        pos1 = (i * tt + lax.broadcasted_iota(jnp.int32, (tt, 1), 0) + 1).astype(f32)
        for gi, w in enumerate(WINDOWS):
            sl = slice(gi * DH, (gi + 1) * DH)
            stages = int(math.log2(w))
            src = pext_s
            for s in range(stages):
                lo = HALO - 8 * (stages - 1 - s)
                n = tt + HALO - lo
                shift = 2 ** s
                val = src[pl.ds(lo, n), sl] + src[pl.ds(lo - shift, n), sl]
                if s == stages - 1:
                    wsum = val
                else:
                    tmp_s[pl.ds(lo, n), sl] = val
                    src = tmp_s
            p_g = pext_s[pl.ds(HALO, tt), sl]
            pooled = (wsum / jnp.minimum(pos1, float(w)) - p_g).astype(bf16)
            pooled_ref[:, sl] = pooled
            y = _dot(pooled, wpool_ref[gi]) * pscale_ref[:, sl]
            cat_ref[:, RW + gi * DH: RW + (gi + 1) * DH] = y.astype(bf16)
        pext_s[pl.ds(0, HALO), :] = pext_s[pl.ds(tt, HALO), :]

        z = ALPHA * x_ref[...] + _dot(cat_ref[...], wout_ref[...])
        mu = jnp.mean(z, axis=-1, keepdims=True)
        zc = z - mu
        rstd = lax.rsqrt(jnp.mean(zc * zc, axis=-1, keepdims=True) + LN_EPS)
        xhat = zc * rstd
        xhat_ref[...] = xhat
        rstd_ref[...] = rstd
        x1b_ref[...] = (xhat * g1_ref[...] + b1_ref[...]).astype(bf16)

        @pl.when(i == n_tiles - 1)
        def _():
            _gather_finish(gin, gout, *sems)

    tile = lambda w: pl.BlockSpec((tt, w), lambda i: (i, 0))
    hbm = pl.BlockSpec(memory_space=pltpu.HBM)
    out_shape = (
        jax.ShapeDtypeStruct((T, 3 * RW), bf16),
        jax.ShapeDtypeStruct((T, RW), f32),
        jax.ShapeDtypeStruct((T, RW), f32),
        jax.ShapeDtypeStruct((N_CHUNK, HEADS, DH, DH), bf16),
        jax.ShapeDtypeStruct((T, D), bf16),
        jax.ShapeDtypeStruct((T, PW), bf16),
        jax.ShapeDtypeStruct((T, D), f32),
        jax.ShapeDtypeStruct((T, 1), f32),
        jax.ShapeDtypeStruct((T, D), bf16),
    ) + tuple(jax.ShapeDtypeStruct((N_DEV,) + b.shape, b.dtype) for b in gather)
    return pl.pallas_call(
        body, name="mix_forward", grid=(n_tiles,), out_shape=out_shape,
        in_specs=[tile(D), _const_spec((IN_W, D)), tile(DH), tile(DH),
                  _const_spec((HEADS, CHUNK, CHUNK)), _const_spec((HEADS, CHUNK, DH)), _const_spec((HEADS, CHUNK, DH)),
                  _const_spec((GROUPS, DH, DH)), _const_spec((1, PW)), _const_spec((D, D)),
                  _const_spec((1, D)), _const_spec((1, D))] + [hbm] * n_g,
        out_specs=(tile(3 * RW), tile(RW), tile(RW),
                   pl.BlockSpec((cpt, HEADS, DH, DH), lambda i: (i, 0, 0, 0)),
                   tile(D), tile(PW), tile(D), tile(1), tile(D)) + (hbm,) * n_g,
        scratch_shapes=[pltpu.VMEM((HEADS, DH, DH), f32), pltpu.VMEM((tt + HALO, PW), f32),
                        pltpu.VMEM((tt + HALO, PW), f32)] + _gather_sems(n_g),
        compiler_params=pltpu.CompilerParams(dimension_semantics=("arbitrary",), vmem_limit_bytes=V7X_VMEM_LIMIT),
    )(x, w_in_t, cos, sin, dmat, qd, kd, w_pool, pool_scale, w_out, ln1_g, ln1_b, *gather)


def _ffn_forward_backward(xhat1, rstd1, ln1_g, ln1_b, w_up_t, conv_w, conv_b, w_down, ln2_g, ln2_b, target,
                          tt=256, ch=256):
    n_tiles = T // tt
    n_ch = D_FF // ch
    hb = tt // 8

    def body(xhat_ref, halo_ref, rstd_ref, g1_ref, b1_ref, wupt_ref, cw_ref, cb_ref, wdown_ref, g2_ref, b2_ref, tgt_ref,
             dz1_ref, dz2b_ref, du_ref, f_ref, loss_ref, dg2_ref, db2_ref, dg1_ref, db1_ref, dcb_ref, dcw_ref,
             gext_s, val_s, dhext_s, carry_s):
        i = pl.program_id(0)
        tile_idx = n_tiles - 1 - i

        @pl.when(i == 0)
        def _():
            for r in (loss_ref, dg2_ref, db2_ref, dg1_ref, db1_ref, dcb_ref, dcw_ref, carry_s):
                r[...] = jnp.zeros_like(r)

        g1, b1 = g1_ref[...], b1_ref[...]
        xhat = xhat_ref[...]
        x1 = xhat * g1 + b1
        x1b = x1.astype(bf16)
        x1h = ((halo_ref[...] * g1 + b1) * jnp.where(tile_idx == 0, 0.0, 1.0)).astype(bf16)

        ffn = jnp.zeros((tt, D), f32)
        for c in range(n_ch):
            cs = slice(c * ch, (c + 1) * ch)
            wg = wupt_ref[pl.ds(D_FF + c * ch, ch), :]
            val = _dot(x1b, wupt_ref[pl.ds(c * ch, ch), :], NT)
            gext_s[pl.ds(8, tt), cs] = _dot(x1b, wg, NT)
            gext_s[pl.ds(0, 8), cs] = _dot(x1h, wg, NT)
            hh = (cb_ref[:, cs] + cw_ref[0:1, cs] * gext_s[pl.ds(6, tt), cs] + cw_ref[1:2, cs] * gext_s[pl.ds(7, tt), cs]
                  + cw_ref[2:3, cs] * gext_s[pl.ds(8, tt), cs])
            sg = _sigmoid(hh)
            act = hh * sg
            dhext_s[pl.ds(0, tt), cs] = act
            val_s[:, cs] = val * (sg + act * (1.0 - sg))
            fc = (act * val).astype(bf16)
            f_ref[:, cs] = fc
            ffn = ffn + _dot(fc, wdown_ref[pl.ds(c * ch, ch), :])

        z = ALPHA * x1 + ffn
        mu = jnp.mean(z, axis=-1, keepdims=True)
        zc = z - mu
        rstd2 = lax.rsqrt(jnp.mean(zc * zc, axis=-1, keepdims=True) + LN_EPS)
        xh2 = zc * rstd2
        diff = xh2 * g2_ref[...] + b2_ref[...] - tgt_ref[...]
        loss_ref[...] += 0.5 * jnp.sum(diff * diff) / D
        dy = diff * (1.0 / D)
        dg2_ref[...] += jnp.sum(dy * xh2, axis=0, keepdims=True)
        db2_ref[...] += jnp.sum(dy, axis=0, keepdims=True)
        dyg = dy * g2_ref[...]
        dz2 = rstd2 * (dyg - jnp.mean(dyg, axis=-1, keepdims=True) - xh2 * jnp.mean(dyg * xh2, axis=-1, keepdims=True))
        dz2b = dz2.astype(bf16)
        dz2b_ref[...] = dz2b

        dx1 = ALPHA * dz2
        dhext_s[pl.ds(tt, 8), :] = carry_s[...]
        for c in range(n_ch):
            cs = slice(c * ch, (c + 1) * ch)
            df = _dot(dz2b, wdown_ref[pl.ds(c * ch, ch), :], NT)
            dval = df * dhext_s[pl.ds(0, tt), cs]
            dh = df * val_s[:, cs]
            dhext_s[pl.ds(0, tt), cs] = dh
            dh1, dh2, g0 = dhext_s[pl.ds(1, tt), cs], dhext_s[pl.ds(2, tt), cs], gext_s[pl.ds(8, tt), cs]
            dcb_ref[:, cs] += jnp.sum(dh, axis=0, keepdims=True)
            dcw_ref[0:1, cs] += jnp.sum(dh2 * g0, axis=0, keepdims=True)
            dcw_ref[1:2, cs] += jnp.sum(dh1 * g0, axis=0, keepdims=True)
            dcw_ref[2:3, cs] += jnp.sum(dh * g0, axis=0, keepdims=True)
            dgate = cw_ref[2:3, cs] * dh + cw_ref[1:2, cs] * dh1 + cw_ref[0:1, cs] * dh2
            dvalb, dgateb = dval.astype(bf16), dgate.astype(bf16)
            du_ref[:, cs] = dvalb
            du_ref[:, D_FF + c * ch: D_FF + (c + 1) * ch] = dgateb
            dx1 = dx1 + _dot(dvalb, wupt_ref[pl.ds(c * ch, ch), :]) + _dot(dgateb, wupt_ref[pl.ds(D_FF + c * ch, ch), :])
        carry_s[...] = dhext_s[pl.ds(0, 8), :]

        dg1_ref[...] += jnp.sum(dx1 * xhat, axis=0, keepdims=True)
        db1_ref[...] += jnp.sum(dx1, axis=0, keepdims=True)
        dxg = dx1 * g1
        dz1_ref[...] = rstd_ref[...] * (dxg - jnp.mean(dxg, axis=-1, keepdims=True)
                                        - xhat * jnp.mean(dxg * xhat, axis=-1, keepdims=True))

    rtile = lambda w: pl.BlockSpec((tt, w), lambda i: (n_tiles - 1 - i, 0))
    acc = lambda shape: pl.BlockSpec(shape, lambda i: (0, 0))
    out_shape = (
        jax.ShapeDtypeStruct((T, D), f32),
        jax.ShapeDtypeStruct((T, D), bf16),
        jax.ShapeDtypeStruct((T, 2 * D_FF), bf16),
        jax.ShapeDtypeStruct((T, D_FF), bf16),
        jax.ShapeDtypeStruct((8, 128), f32),
        jax.ShapeDtypeStruct((1, D), f32), jax.ShapeDtypeStruct((1, D), f32),
        jax.ShapeDtypeStruct((1, D), f32), jax.ShapeDtypeStruct((1, D), f32),
        jax.ShapeDtypeStruct((1, D_FF), f32), jax.ShapeDtypeStruct((3, D_FF), f32),
    )
    return pl.pallas_call(
        body, name="ffn_forward_backward", grid=(n_tiles,), out_shape=out_shape,
        in_specs=[rtile(D),
                  pl.BlockSpec((8, D), lambda i: (jnp.maximum((n_tiles - 1 - i) * hb - 1, 0), 0)),
                  rtile(1), _const_spec((1, D)), _const_spec((1, D)), _const_spec((2 * D_FF, D)),
                  _const_spec((3, D_FF)), _const_spec((1, D_FF)), _const_spec((D_FF, D)),
                  _const_spec((1, D)), _const_spec((1, D)), rtile(D)],
        out_specs=(rtile(D), rtile(D), rtile(2 * D_FF), rtile(D_FF), acc((8, 128)),
                   acc((1, D)), acc((1, D)), acc((1, D)), acc((1, D)), acc((1, D_FF)), acc((3, D_FF))),
        scratch_shapes=[pltpu.VMEM((tt + 8, D_FF), f32), pltpu.VMEM((tt, D_FF), f32),
                        pltpu.VMEM((tt + 8, D_FF), f32), pltpu.VMEM((8, D_FF), f32)],
        compiler_params=pltpu.CompilerParams(dimension_semantics=("arbitrary",), vmem_limit_bytes=V7X_VMEM_LIMIT),
    )(xhat1, xhat1, rstd1, ln1_g, ln1_b, w_up_t, conv_w, conv_b, w_down, ln2_g, ln2_b, target)


def _mix_backward(dz1, w_out, qkv, g, oret, states, pooled, cos, sin, dmat, qd, kd, cdec, w_pool, pool_scale, w_in_t,
                  exchange, tt=512):
    n_tiles = T // tt
    cpt = tt // CHUNK
    n_e = len(exchange)

    def body(dz1_ref, wout_ref, qkv_ref, g_ref, oret_ref, states_ref, pooled_ref, cos_ref, sin_ref, dmat_ref, qd_ref,
             kd_ref, wpool_ref, pscale_ref, wint_ref, *rest):
        ein, rest = rest[:n_e], rest[n_e:]
        dproj_ref, gx_ref, dwpool_ref, dpscale_ref = rest[:4]
        eout, (dstate_s, dout_s, dqkv_s, eext_s, tmp_s, *sems) = rest[4:4 + n_e], rest[4 + n_e:]
        i = pl.program_id(0)
        tile_idx = n_tiles - 1 - i

        @pl.when(i == 0)
        def _():
            dstate_s[...] = jnp.zeros_like(dstate_s)
            dwpool_ref[...] = jnp.zeros_like(dwpool_ref)
            dpscale_ref[...] = jnp.zeros_like(dpscale_ref)
            eext_s[pl.ds(tt, HALO), :] = jnp.zeros((HALO, PW), f32)
            _chip_exchange_start(ein, eout, *sems)

        dz1 = dz1_ref[...]
        dcat = _dot(dz1.astype(bf16), wout_ref[...], NT)

        pos1 = (tile_idx * tt + lax.broadcasted_iota(jnp.int32, (tt, 1), 0) + 1).astype(f32)
        for gi, w in enumerate(WINDOWS):
            sl = slice(gi * DH, (gi + 1) * DH)
            dpo = dcat[:, RW + gi * DH: RW + (gi + 1) * DH]
            pooled_g = pooled_ref[:, sl]
            ylin = _dot(pooled_g, wpool_ref[gi])
            dpscale_ref[:, sl] += jnp.sum(dpo * ylin, axis=0, keepdims=True)
            dpw = (dpo * pscale_ref[:, sl]).astype(bf16)
            dwpool_ref[gi] += _dot(pooled_g, dpw, TN)
            dpooled = _dot(dpw, wpool_ref[gi], NT)
            eext_s[pl.ds(0, tt), sl] = dpooled / jnp.minimum(pos1, float(w))
            stages = int(math.log2(w))
            src = eext_s
            for s in range(stages):
                n = tt + 8 * (stages - 1 - s)
                shift = 2 ** s
                val = src[pl.ds(0, n), sl] + src[pl.ds(shift, n), sl]
                if s == stages - 1:
                    wsum = val
                else:
                    tmp_s[pl.ds(0, n), sl] = val
                    src = tmp_s
            dproj_ref[:, 4 * RW + gi * DH: 4 * RW + (gi + 1) * DH] = (wsum - dpooled).astype(bf16)
        eext_s[pl.ds(tt, HALO), :] = eext_s[pl.ds(0, HALO), :]

        for h in range(HEADS):
            sl = slice(h * DH, (h + 1) * DH)
            dr = dcat[:, sl]
            o = oret_ref[:, sl]
            r = lax.rsqrt(jnp.mean(o * o, axis=-1, keepdims=True) + RMS_EPS)
            rn = o * r
            gg = g_ref[:, sl]
            sg = _sigmoid(gg)
            dproj_ref[:, 3 * RW + h * DH: 3 * RW + (h + 1) * DH] = (dr * rn * (sg * (1.0 + gg * (1.0 - sg)))).astype(bf16)
            drn = dr * (gg * sg)
            dout_s[:, sl] = (r * (drn - rn * jnp.mean(drn * rn, axis=-1, keepdims=True))).astype(bf16)

        def chunk_step(j, carry):
            c = cpt - 1 - j
            rows = pl.ds(pl.multiple_of(c * CHUNK, CHUNK), CHUNK)
            for h in range(HEADS):
                q = qkv_ref[rows, h * DH:(h + 1) * DH]
                k = qkv_ref[rows, RW + h * DH: RW + (h + 1) * DH]
                v = qkv_ref[rows, 2 * RW + h * DH: 2 * RW + (h + 1) * DH]
                do = dout_s[rows, h * DH:(h + 1) * DH]
                stb = states_ref[c, h]
                dst = dstate_s[h]
                dstb = dst.astype(bf16)
                dm = dmat_ref[h]
                sb = (_dot(q, k, NT) * dm).astype(bf16)
                dsb = (_dot(do, v, NT) * dm).astype(bf16)
                qdq = (q.astype(f32) * qd_ref[h]).astype(bf16)
                kdk = (k.astype(f32) * kd_ref[h]).astype(bf16)
                dq = _dot(dsb, k) + _dot(do, stb, NT) * qd_ref[h]
                dk = _dot(dsb, q, TN) + _dot(v, dstb, NT) * kd_ref[h]
                dv = _dot(sb, do, TN) + _dot(kdk, dstb)
                dstate_s[h] = dst * cdec[h] + _dot(qdq, do, TN)
                dqkv_s[rows, h * DH:(h + 1) * DH] = dq
                dqkv_s[rows, RW + h * DH: RW + (h + 1) * DH] = dk
                dproj_ref[rows, 2 * RW + h * DH: 2 * RW + (h + 1) * DH] = dv.astype(bf16)
            return carry

        lax.fori_loop(0, cpt, chunk_step, 0)

        cos_t, sin_t = cos_ref[...], sin_ref[...]
        for part in range(2):
            for h in range(HEADS):
                sl = slice(part * RW + h * DH, part * RW + (h + 1) * DH)
                dr = dqkv_s[:, sl]
                dt = dr * cos_t - _swap_halves(dr) * sin_t
                if part == 1:
                    dt = dt * K_SCALE
                dproj_ref[:, sl] = dt.astype(bf16)

        gx_ref[...] = ALPHA * dz1 + _dot(dproj_ref[...], wint_ref[...])

        @pl.when(i == n_tiles - 1)
        def _():
            _chip_exchange_finish(ein, eout, *sems)

    rtile = lambda w: pl.BlockSpec((tt, w), lambda i: (n_tiles - 1 - i, 0))
    hbm = pl.BlockSpec(memory_space=pltpu.HBM)
    out_shape = (
        jax.ShapeDtypeStruct((T, IN_W), bf16),
        jax.ShapeDtypeStruct((T, D), f32),
        jax.ShapeDtypeStruct((GROUPS, DH, DH), f32),
        jax.ShapeDtypeStruct((1, PW), f32),
    ) + tuple(jax.ShapeDtypeStruct(e.shape, e.dtype) for e in exchange)
    return pl.pallas_call(
        body, name="mix_backward", grid=(n_tiles,), out_shape=out_shape,
        in_specs=[rtile(D), _const_spec((D, D)), rtile(3 * RW), rtile(RW), rtile(RW),
                  pl.BlockSpec((cpt, HEADS, DH, DH), lambda i: (n_tiles - 1 - i, 0, 0, 0)),
                  rtile(PW), rtile(DH), rtile(DH),
                  _const_spec((HEADS, CHUNK, CHUNK)), _const_spec((HEADS, CHUNK, DH)), _const_spec((HEADS, CHUNK, DH)),
                  _const_spec((GROUPS, DH, DH)), _const_spec((1, PW)), _const_spec((IN_W, D))] + [hbm] * n_e,
        out_specs=(rtile(IN_W), rtile(D), pl.BlockSpec((GROUPS, DH, DH), lambda i: (0, 0, 0)),
                   pl.BlockSpec((1, PW), lambda i: (0, 0))) + (hbm,) * n_e,
        scratch_shapes=[pltpu.VMEM((HEADS, DH, DH), f32), pltpu.VMEM((tt, RW), bf16), pltpu.VMEM((tt, 2 * RW), f32),
                        pltpu.VMEM((tt + HALO, PW), f32), pltpu.VMEM((tt + HALO, PW), f32)] + _chip_exchange_sems(n_e),
        compiler_params=pltpu.CompilerParams(dimension_semantics=("arbitrary",), vmem_limit_bytes=V7X_VMEM_LIMIT),
    )(dz1, w_out, qkv, g, oret, states, pooled, cos, sin, dmat, qd, kd, w_pool, pool_scale, w_in_t, *exchange)


def _weight_grad(a, b, name, tm, tk=1024):
    m = a.shape[1]
    n_k = T // tk

    def body(a_ref, b_ref, o_ref, acc_s):
        k = pl.program_id(1)

        @pl.when(k == 0)
        def _():
            acc_s[...] = jnp.zeros_like(acc_s)

        acc_s[...] += _dot(a_ref[...], b_ref[...].astype(bf16), TN)

        @pl.when(k == n_k - 1)
        def _():
            o_ref[...] = acc_s[...].astype(bf16)

    return pl.pallas_call(
        body, name=name, grid=(m // tm, n_k), out_shape=jax.ShapeDtypeStruct((m, D), bf16),
        in_specs=[pl.BlockSpec((tk, tm), lambda i, k: (k, i)), pl.BlockSpec((tk, D), lambda i, k: (k, 0))],
        out_specs=pl.BlockSpec((tm, D), lambda i, k: (i, 0)),
        scratch_shapes=[pltpu.VMEM((tm, D), f32)],
        compiler_params=pltpu.CompilerParams(dimension_semantics=("parallel", "arbitrary"),
                                             vmem_limit_bytes=V7X_VMEM_LIMIT),
    )(a, b)


CHIP_FLIPS = ((1, 0), (0, 1), (1, 1))


def _me():
    return lax.axis_index("x"), lax.axis_index("y"), lax.axis_index("c")


def _chip(me, k):
    x, y, _ = me
    if k == 0:
        return x, y
    fx, fy = CHIP_FLIPS[k - 1]
    return (1 - x if fx else x), (1 - y if fy else y)


def _slot(x, y, c):
    return 4 * x + 2 * y + c


def _remote(src, dst, send_sem, recv_sem, to):
    return pltpu.make_async_remote_copy(src_ref=src, dst_ref=dst, send_sem=send_sem, recv_sem=recv_sem,
                                        device_id=to, device_id_type=pl.DeviceIdType.MESH)


def _gather_sems(n):
    return [pltpu.SemaphoreType.DMA((7, n)), pltpu.SemaphoreType.DMA((7, n)), pltpu.SemaphoreType.DMA((n,))] if n else []


def _gather_copy(k, j, gin, gout, send_sems, recv_sems, sending):
    me = _me()
    x, y, c = me
    sibling = (x, y, 1 - c)
    src, to = gin[j], sibling
    if sending:
        block = me if k <= 3 else (*_chip(me, k - 3), c)
        if 1 <= k <= 3:
            to = (*_chip(me, k), c)
        if k >= 4:
            src = gout[j].at[_slot(*block)]
    else:
        block = sibling if k == 0 else (*_chip(me, k), c) if k <= 3 else (*_chip(me, k - 3), 1 - c)
    return _remote(src, gout[j].at[_slot(*block)], send_sems.at[k, j], recv_sems.at[k, j], to)


def _gather_start(gin, gout, send_sems, recv_sems, local_sems):
    for j in range(len(gin)):
        pltpu.make_async_copy(gin[j], gout[j].at[_slot(*_me())], local_sems.at[j]).start()
    for k in range(4):
        for j in range(len(gin)):
            _gather_copy(k, j, gin, gout, send_sems, recv_sems, True).start()


def _gather_forward(gin, gout, send_sems, recv_sems, local_sems):
    for k in range(1, 4):
        for j in range(len(gin)):
            _gather_copy(k, j, gin, gout, send_sems, recv_sems, False).wait_recv()
            _gather_copy(k + 3, j, gin, gout, send_sems, recv_sems, True).start()


def _gather_finish(gin, gout, send_sems, recv_sems, local_sems):
    for k in (0, 4, 5, 6):
        for j in range(len(gin)):
            _gather_copy(k, j, gin, gout, send_sems, recv_sems, False).wait_recv()
    for k in range(7):
        for j in range(len(gin)):
            _gather_copy(k, j, gin, gout, send_sems, recv_sems, True).wait_send()
    for j in range(len(gin)):
        pltpu.make_async_copy(gin[j], gout[j].at[_slot(*_me())], local_sems.at[j]).wait()


def _all_gather(blocks, name):
    n = len(blocks)

    def body(*refs):
        gin, gout, sems = refs[:n], refs[n:2 * n], refs[2 * n:]
        _gather_start(gin, gout, *sems)
        _gather_forward(gin, gout, *sems)
        _gather_finish(gin, gout, *sems)

    hbm = pl.BlockSpec(memory_space=pltpu.HBM)
    return pl.pallas_call(
        body, name=name,
        out_shape=tuple(jax.ShapeDtypeStruct((N_DEV,) + b.shape, b.dtype) for b in blocks),
        in_specs=[hbm] * n, out_specs=(hbm,) * n, scratch_shapes=_gather_sems(n),
    )(*blocks)


def _pair_reduce(parts, name):
    n = len(parts)

    def body(*refs):
        ins, own, others, landing = (refs[k * n:(k + 1) * n] for k in range(4))
        send_sems, recv_sems = refs[4 * n:]
        me = _me()
        x, y, c = me
        sibling = (x, y, 1 - c)
        sends = []
        for k in range(4):
            for j in range(n):
                cp = _remote(ins[j].at[_slot(*_chip(me, k), 1 - c)], landing[j].at[k], send_sems.at[k, j],
                             recv_sems.at[k, j], sibling)
                cp.start()
                sends.append(cp)
        for k in range(4):
            for j in range(n):
                _remote(ins[j].at[0], landing[j].at[k], send_sems.at[k, j], recv_sems.at[k, j], sibling).wait_recv()
                total = ins[j][_slot(*_chip(me, k), c)].astype(f32) + landing[j][k].astype(f32)
                if k == 0:
                    own[j][...] = total.astype(own[j].dtype)
                else:
                    others[j][k - 1] = total.astype(others[j].dtype)
        for cp in sends:
            cp.wait_send()

    vm = pl.BlockSpec(memory_space=pltpu.VMEM)
    return pl.pallas_call(
        body, name=name,
        out_shape=tuple(jax.ShapeDtypeStruct(p.shape[1:], p.dtype) for p in parts)
        + tuple(jax.ShapeDtypeStruct((3,) + p.shape[1:], p.dtype) for p in parts),
        in_specs=[vm] * n, out_specs=(vm,) * (2 * n),
        scratch_shapes=[pltpu.VMEM((4,) + p.shape[1:], p.dtype) for p in parts]
        + [pltpu.SemaphoreType.DMA((4, n)), pltpu.SemaphoreType.DMA((4, n))],
        compiler_params=pltpu.CompilerParams(vmem_limit_bytes=V7X_VMEM_LIMIT),
    )(*parts)


def _chip_exchange_sems(n):
    return [pltpu.SemaphoreType.DMA((3, n)), pltpu.SemaphoreType.DMA((3, n))] if n else []


def _chip_exchange_copy(k, j, ein, eout, send_sems, recv_sems):
    me = _me()
    return _remote(ein[j].at[k - 1], eout[j].at[k - 1], send_sems.at[k - 1, j], recv_sems.at[k - 1, j],
                   (*_chip(me, k), me[2]))


def _chip_exchange_start(ein, eout, send_sems, recv_sems):
    for k in range(1, 4):
        for j in range(len(ein)):
            _chip_exchange_copy(k, j, ein, eout, send_sems, recv_sems).start()


def _chip_exchange_finish(ein, eout, send_sems, recv_sems):
    for k in range(1, 4):
        for j in range(len(ein)):
            _chip_exchange_copy(k, j, ein, eout, send_sems, recv_sems).wait_recv()
    for k in range(1, 4):
        for j in range(len(ein)):
            _chip_exchange_copy(k, j, ein, eout, send_sems, recv_sems).wait_send()


def _chip_exchange(others, name):
    n = len(others)

    def body(*refs):
        ein, eout, sems = refs[:n], refs[n:2 * n], refs[2 * n:]
        _chip_exchange_start(ein, eout, *sems)
        _chip_exchange_finish(ein, eout, *sems)

    hbm = pl.BlockSpec(memory_space=pltpu.HBM)
    return pl.pallas_call(
        body, name=name, out_shape=tuple(jax.ShapeDtypeStruct(e.shape, e.dtype) for e in others),
        in_specs=[hbm] * n, out_specs=(hbm,) * n, scratch_shapes=_chip_exchange_sems(n),
    )(*others)


def _sum_parts(owns, arrived, name):
    n = len(owns)

    def body(*refs):
        for own, arr, out in zip(refs[:n], refs[n:2 * n], refs[2 * n:]):
            acc = own[...].astype(f32)
            for k in range(3):
                acc = acc + arr[k].astype(f32)
            out[...] = acc

    vm = pl.BlockSpec(memory_space=pltpu.VMEM)
    return pl.pallas_call(
        body, name=name, out_shape=tuple(jax.ShapeDtypeStruct(o.shape, f32) for o in owns),
        in_specs=[vm] * (2 * n), out_specs=(vm,) * n,
        compiler_params=pltpu.CompilerParams(vmem_limit_bytes=V7X_VMEM_LIMIT),
    )(*owns, *arrived)


ADAM_C1 = 1.0 / (1.0 - ADAM_B1 ** ADAM_STEP)
ADAM_C2 = 1.0 / (1.0 - ADAM_B2 ** ADAM_STEP)


def _adam_update(w, g, m, v):
    m = ADAM_B1 * m + (1.0 - ADAM_B1) * g
    v = ADAM_B2 * v + (1.0 - ADAM_B2) * (g * g)
    return -ADAM_LR * ((m * ADAM_C1) / (jnp.sqrt(v * ADAM_C2) + ADAM_EPS) + ADAM_WD * w), m, v


def _sum_adamw(own, arrived, w, m, v, name, steps):
    rows = own.shape[0]
    br = rows // steps

    def body(own_ref, arr_ref, w_ref, m_ref, v_ref, g_out, d_out, m_out, v_out):
        g = own_ref[...].astype(f32)
        for k in range(3):
            g = g + arr_ref[k].astype(f32)
        g_out[...] = g
        d_out[...], m_out[...], v_out[...] = _adam_update(w_ref[...], g, m_ref[...], v_ref[...])

    blk = pl.BlockSpec((br, D), lambda i: (i, 0))
    return pl.pallas_call(
        body, name=name, grid=(steps,), out_shape=(jax.ShapeDtypeStruct((rows, D), f32),) * 4,
        in_specs=[blk, pl.BlockSpec((3, br, D), lambda i: (0, i, 0)), blk, blk, blk], out_specs=(blk,) * 4,
        compiler_params=pltpu.CompilerParams(dimension_semantics=("parallel",), vmem_limit_bytes=V7X_VMEM_LIMIT),
    )(own, arrived, w, m, v)


def _adamw(ws, gs, ms, vs, name):
    n = len(ws)

    def body(*refs):
        w_r, g_r, m_r, v_r = (refs[k * n:(k + 1) * n] for k in range(4))
        d_o, m_o, v_o = (refs[(4 + k) * n:(5 + k) * n] for k in range(3))
        for j in range(n):
            d_o[j][...], m_o[j][...], v_o[j][...] = _adam_update(w_r[j][...], g_r[j][...], m_r[j][...], v_r[j][...])

    vm = pl.BlockSpec(memory_space=pltpu.VMEM)
    shapes = tuple(jax.ShapeDtypeStruct(w.shape, f32) for w in ws)
    return pl.pallas_call(
        body, name=name, out_shape=shapes * 3, in_specs=[vm] * (4 * n), out_specs=tuple([vm] * (3 * n)),
        compiler_params=pltpu.CompilerParams(vmem_limit_bytes=V7X_VMEM_LIMIT),
    )(*ws, *gs, *ms, *vs)


SMALL = (("w_pool", GROUPS * DH * DH), ("pool_scale", PW), ("ln1_g", D), ("ln1_b", D), ("conv_b", D_FF),
         ("ln2_g", D), ("ln2_b", D), ("conv_w", 3 * D_FF), ("loss", 1))
SMALL_ROWS = 640


def _pack(named):
    flat = jnp.concatenate([named[k].reshape(-1) for k, _ in SMALL])
    return jnp.pad(flat, (0, SMALL_ROWS * 128 - flat.shape[0])).reshape(SMALL_ROWS, 128)


def _unpack(packed):
    flat, out, at = packed.reshape(-1), {}, 0
    for k, size in SMALL:
        out[k] = flat[at:at + size]
        at += size
    return out


def kernel(x, w_in, w_pool, pool_scale, w_out, ln1_g, ln1_b, w_up, conv_w, conv_b, w_down, ln2_g, ln2_b, loss_target, m_w_in, m_w_pool, m_pool_scale, m_w_out, m_ln1_g, m_ln1_b, m_w_up, m_conv_w, m_conv_b, m_w_down, m_ln2_g, m_ln2_b, v_w_in, v_w_pool, v_pool_scale, v_w_out, v_ln1_g, v_ln1_b, v_w_up, v_conv_w, v_conv_b, v_w_down, v_ln2_g, v_ln2_b):
    me = 4 * lax.axis_index("x") + 2 * lax.axis_index("y") + lax.axis_index("c")
    x2, tgt = x[0], loss_target[0]

    g_in, g_out, g_cw = _all_gather([w_in[0].T.astype(bf16), w_out[0].astype(bf16), conv_w[0]], "gather_weights")
    w_in_t = g_in.reshape(IN_W, D)
    w_out_f = g_out.reshape(D, D)
    conv_w_f = jnp.transpose(g_cw, (1, 0, 2)).reshape(3, D_FF)
    w_pool_b = w_pool[0].astype(bf16)

    cos, sin = _rope_tables()
    dmat, qd, kd, cdec = _decay_tables()

    qkv, g, oret, states, cat, pooled, xhat1, rstd1, x1b, g_up, g_down = _mix_forward(
        x2, w_in_t, cos, sin, dmat, qd, kd, cdec, w_pool_b, pool_scale, w_out_f, ln1_g, ln1_b,
        gather=[w_up[0].T.astype(bf16), w_down[0].astype(bf16)])
    w_up_t = g_up.reshape(2 * D_FF, D)
    w_down_f = g_down.reshape(D_FF, D)
    dz1, dz2b, du, f, loss8, d_ln2_g, d_ln2_b, d_ln1_g, d_ln1_b, d_conv_b, d_conv_w = _ffn_forward_backward(
        xhat1, rstd1, ln1_g, ln1_b, w_up_t, conv_w_f, conv_b, w_down_f, ln2_g, ln2_b, tgt)

    dw_up_t = _weight_grad(du, x1b, "grad_w_up", tm=512).reshape(N_DEV, ROWS_UP, D)
    dw_down = _weight_grad(f, dz2b, "grad_w_down", tm=256).reshape(N_DEV, ROWS_DOWN, D)
    own_up, own_down, oth_up, oth_down = _pair_reduce([dw_up_t, dw_down], "pair_reduce_ffn")
    dproj, grad_x, d_w_pool, d_pool_scale, arr_up, arr_down = _mix_backward(
        dz1, w_out_f, qkv, g, oret, states, pooled, cos, sin, dmat, qd, kd, cdec, w_pool_b, pool_scale, w_in_t,
        exchange=[oth_up, oth_down])
    dw_in_t = _weight_grad(dproj, x2, "grad_w_in", tm=512).reshape(N_DEV, ROWS_IN, D)
    dw_out = _weight_grad(cat, dz1, "grad_w_out", tm=512).reshape(N_DEV, ROWS_OUT, D)
    small = _pack({"w_pool": d_w_pool, "pool_scale": d_pool_scale, "ln1_g": d_ln1_g, "ln1_b": d_ln1_b,
                   "conv_b": d_conv_b, "ln2_g": d_ln2_g, "ln2_b": d_ln2_b, "conv_w": d_conv_w, "loss": loss8[0, :1]})
    own_in, own_out, own_small, oth_in, oth_out, oth_small = _pair_reduce(
        [dw_in_t, dw_out, small.reshape(N_DEV, SMALL_ROWS // N_DEV, 128)], "pair_reduce_mix")
    arr_in, arr_out, arr_small = _chip_exchange([oth_in, oth_out, oth_small], "exchange_mix")

    names = ["w_in", "w_pool", "pool_scale", "w_out", "ln1_g", "ln1_b", "w_up", "conv_w", "conv_b", "w_down",
             "ln2_g", "ln2_b"]
    w_d = dict(w_in=w_in, w_pool=w_pool, pool_scale=pool_scale, w_out=w_out, ln1_g=ln1_g, ln1_b=ln1_b, w_up=w_up,
               conv_w=conv_w, conv_b=conv_b, w_down=w_down, ln2_g=ln2_g, ln2_b=ln2_b)
    m_d = dict(w_in=m_w_in, w_pool=m_w_pool, pool_scale=m_pool_scale, w_out=m_w_out, ln1_g=m_ln1_g, ln1_b=m_ln1_b,
               w_up=m_w_up, conv_w=m_conv_w, conv_b=m_conv_b, w_down=m_w_down, ln2_g=m_ln2_g, ln2_b=m_ln2_b)
    v_d = dict(w_in=v_w_in, w_pool=v_w_pool, pool_scale=v_pool_scale, w_out=v_w_out, ln1_g=v_ln1_g, ln1_b=v_ln1_b,
               w_up=v_w_up, conv_w=v_conv_w, conv_b=v_conv_b, w_down=v_w_down, ln2_g=v_ln2_g, ln2_b=v_ln2_b)
    g_d, delta, new_m, new_v = {}, {}, {}, {}

    big = (("w_in", own_in, arr_in, True, 4), ("w_out", own_out, arr_out, False, 2),
           ("w_up", own_up, arr_up, True, 4), ("w_down", own_down, arr_down, False, 2))
    for k, own, arr, transposed, steps in big:
        lay = (lambda a: a[0].T) if transposed else (lambda a: a[0])
        back = (lambda a: a.T[None]) if transposed else (lambda a: a[None])
        res = _sum_adamw(own, arr, lay(w_d[k]), lay(m_d[k]), lay(v_d[k]), "adamw_" + k, steps)
        g_d[k], delta[k], new_m[k], new_v[k] = (back(r) for r in res)

    (small_piece,) = _sum_parts([own_small], [arr_small], "sum_small_grads")
    (gs_small,) = _all_gather([small_piece], "gather_small_grads")
    gsm = _unpack(gs_small)
    gsm["conv_w"] = lax.dynamic_slice(gsm["conv_w"].reshape(3, D_FF), (0, me * (D_FF // N_DEV)), (3, D_FF // N_DEV))
    two_d = lambda a: a.reshape(-1, a.shape[-1])
    group = [k for k in names if k not in g_d]
    for k in group:
        g_d[k] = gsm[k].reshape(w_d[k].shape)
    res = _adamw([two_d(w_d[k]) for k in group], [two_d(g_d[k]) for k in group], [two_d(m_d[k]) for k in group],
                 [two_d(v_d[k]) for k in group], "adamw_small")
    for j, k in enumerate(group):
        delta[k] = res[j].reshape(w_d[k].shape)
        new_m[k] = res[len(group) + j].reshape(w_d[k].shape)
        new_v[k] = res[2 * len(group) + j].reshape(w_d[k].shape)

    loss = gsm["loss"].reshape(())
    return (loss, grad_x[None], *[g_d[k] for k in names], *[delta[k] for k in names], *[new_m[k] for k in names],
            *[new_v[k] for k in names])
```

```python
import functools
import math

import numpy as np
import jax
import jax.numpy as jnp
from jax import lax
from jax.experimental import pallas as pl
from jax.experimental.pallas import tpu as pltpu

f32 = jnp.float32
bf16 = jnp.bfloat16

N_DEV = 8
T = 4096
D = 1024
CHUNK = 64
N_CHUNK = T // CHUNK
HEADS = 4
DH = 128
RW = HEADS * DH
PW = 512
GROUPS = 4
WINDOWS = (2, 4, 8, 16)
IN_W = 4 * RW + PW
D_FF = 2816
LN_EPS = 1e-5
RMS_EPS = 1e-6
ALPHA = 2.0 ** 0.25
K_SCALE = DH ** -0.5

ADAM_LR = 0.001
ADAM_B1 = 0.9
ADAM_B2 = 0.999
ADAM_EPS = 1e-08
ADAM_WD = 0.01
ADAM_STEP = 10

ROWS_IN, ROWS_OUT, ROWS_UP, ROWS_DOWN = IN_W // N_DEV, D // N_DEV, 2 * D_FF // N_DEV, D_FF // N_DEV

V7X_VMEM_LIMIT = 56 * 2 ** 20
HALO = 32

NT = (((1,), (1,)), ((), ()))
TN = (((0,), (0,)), ((), ()))
NN = (((1,), (0,)), ((), ()))


def _dot(a, b, dims=NN):
    return lax.dot_general(a, b, dims, preferred_element_type=f32)


def _const_spec(shape):
    zeros = (0,) * len(shape)
    return pl.BlockSpec(shape, lambda i: zeros, pipeline_mode=pl.Buffered(1))


def _sigmoid(x):
    return 0.5 * jnp.tanh(0.5 * x) + 0.5


def _decay_tables():
    h = np.arange(HEADS, dtype=np.float64)
    log_gamma = np.log(1.0 - 2.0 ** (-5.0 - h)).astype(np.float32).astype(np.float64)
    idx = np.arange(CHUNK, dtype=np.float64)
    inner = np.exp(log_gamma[:, None, None] * np.abs(idx[:, None] - idx[None, :]))
    qd = np.exp(log_gamma[:, None] * (idx[None, :] + 1.0))
    kd = np.exp(log_gamma[:, None] * (CHUNK - 1.0 - idx[None, :]))
    cd = np.exp(log_gamma * CHUNK)
    qd = np.broadcast_to(qd[:, :, None], (HEADS, CHUNK, DH))
    kd = np.broadcast_to(kd[:, :, None], (HEADS, CHUNK, DH))
    return (jnp.asarray(inner, f32), jnp.asarray(qd, f32), jnp.asarray(kd, f32), [float(c) for c in cd])


def _rope_tables():
    inv_freq = (10000.0 ** (-np.arange(0, DH, 2, dtype=np.float64) / DH)).astype(np.float32)
    ang = (np.arange(T, dtype=np.float32)[:, None] * inv_freq[None, :]).astype(np.float64)
    cos, sin = np.cos(ang), np.sin(ang)
    return (jnp.asarray(np.concatenate([cos, cos], axis=1), f32), jnp.asarray(np.concatenate([-sin, sin], axis=1), f32))


def _swap_halves(t):
    return pltpu.roll(t, DH // 2, axis=1)


def _mix_forward(x, w_in_t, cos, sin, dmat, qd, kd, cdec, w_pool, pool_scale, w_out, ln1_g, ln1_b, gather, tt=512):
    n_tiles = T // tt
    cpt = tt // CHUNK
    n_g = len(gather)

    def body(x_ref, wint_ref, cos_ref, sin_ref, dmat_ref, qd_ref, kd_ref, wpool_ref, pscale_ref, wout_ref,
             g1_ref, b1_ref, *rest):
        gin, rest = rest[:n_g], rest[n_g:]
        qkv_ref, g_ref, oret_ref, states_ref, cat_ref, pooled_ref, xhat_ref, rstd_ref, x1b_ref = rest[:9]
        gout, (state_s, pext_s, tmp_s, *sems) = rest[9:9 + n_g], rest[9 + n_g:]
        i = pl.program_id(0)

        @pl.when(i == 0)
        def _():
            state_s[...] = jnp.zeros_like(state_s)
            pext_s[pl.ds(0, HALO), :] = jnp.zeros((HALO, PW), f32)
            _gather_start(gin, gout, *sems)

        @pl.when(i == n_tiles - 1)
        def _():
            _gather_forward(gin, gout, *sems)

        xb = x_ref[...].astype(bf16)
        cos_t, sin_t = cos_ref[...], sin_ref[...]
        for part in range(2):
            pr = _dot(xb, wint_ref[pl.ds(part * RW, RW), :], NT)
            for h in range(HEADS):
                t = pr[:, h * DH:(h + 1) * DH]
                r = t * cos_t + _swap_halves(t) * sin_t
                if part == 1:
                    r = r * K_SCALE
                qkv_ref[:, part * RW + h * DH: part * RW + (h + 1) * DH] = r.astype(bf16)
        qkv_ref[:, 2 * RW:3 * RW] = _dot(xb, wint_ref[pl.ds(2 * RW, RW), :], NT).astype(bf16)
        g_ref[...] = _dot(xb, wint_ref[pl.ds(3 * RW, RW), :], NT)
        pext_s[pl.ds(HALO, tt), :] = _dot(xb, wint_ref[pl.ds(4 * RW, PW), :], NT)

        def chunk_step(c, carry):
            rows = pl.ds(pl.multiple_of(c * CHUNK, CHUNK), CHUNK)
            for h in range(HEADS):
                q = qkv_ref[rows, h * DH:(h + 1) * DH]
                k = qkv_ref[rows, RW + h * DH: RW + (h + 1) * DH]
                v = qkv_ref[rows, 2 * RW + h * DH: 2 * RW + (h + 1) * DH]
                s = _dot(q, k, NT) * dmat_ref[h]
                inner = _dot(s.astype(bf16), v)
                st = state_s[h]
                stb = st.astype(bf16)
                states_ref[c, h] = stb
                cross = _dot((q.astype(f32) * qd_ref[h]).astype(bf16), stb)
                kdk = (k.astype(f32) * kd_ref[h]).astype(bf16)
                state_s[h] = st * cdec[h] + _dot(kdk, v, TN)
                oret_ref[rows, h * DH:(h + 1) * DH] = inner + cross
            return carry

        lax.fori_loop(0, cpt, chunk_step, 0)

        for h in range(HEADS):
            sl = slice(h * DH, (h + 1) * DH)
            o = oret_ref[:, sl]
            r = lax.rsqrt(jnp.mean(o * o, axis=-1, keepdims=True) + RMS_EPS)
            gg = g_ref[:, sl]
            cat_ref[:, sl] = (o * r * (gg * _sigmoid(gg))).astype(bf16)

        pos1 = (i * tt + lax.broadcasted_iota(jnp.int32, (tt, 1), 0) + 1).astype(f32)
        for gi, w in enumerate(WINDOWS):
            sl = slice(gi * DH, (gi + 1) * DH)
            stages = int(math.log2(w))
            src = pext_s
            for s in range(stages):
                lo = HALO - 8 * (stages - 1 - s)
                n = tt + HALO - lo
                shift = 2 ** s
                val = src[pl.ds(lo, n), sl] + src[pl.ds(lo - shift, n), sl]
                if s == stages - 1:
                    wsum = val
                else:
                    tmp_s[pl.ds(lo, n), sl] = val
                    src = tmp_s
            p_g = pext_s[pl.ds(HALO, tt), sl]
            pooled = (wsum / jnp.minimum(pos1, float(w)) - p_g).astype(bf16)
            pooled_ref[:, sl] = pooled
            y = _dot(pooled, wpool_ref[gi]) * pscale_ref[:, sl]
            cat_ref[:, RW + gi * DH: RW + (gi + 1) * DH] = y.astype(bf16)
        pext_s[pl.ds(0, HALO), :] = pext_s[pl.ds(tt, HALO), :]

        z = ALPHA * x_ref[...] + _dot(cat_ref[...], wout_ref[...])
        mu = jnp.mean(z, axis=-1, keepdims=True)
        zc = z - mu
        rstd = lax.rsqrt(jnp.mean(zc * zc, axis=-1, keepdims=True) + LN_EPS)
        xhat = zc * rstd
        xhat_ref[...] = xhat
        rstd_ref[...] = rstd
        x1b_ref[...] = (xhat * g1_ref[...] + b1_ref[...]).astype(bf16)

        @pl.when(i == n_tiles - 1)
        def _():
            _gather_finish(gin, gout, *sems)

    tile = lambda w: pl.BlockSpec((tt, w), lambda i: (i, 0))
    hbm = pl.BlockSpec(memory_space=pltpu.HBM)
    out_shape = (
        jax.ShapeDtypeStruct((T, 3 * RW), bf16),
        jax.ShapeDtypeStruct((T, RW), f32),
        jax.ShapeDtypeStruct((T, RW), f32),
        jax.ShapeDtypeStruct((N_CHUNK, HEADS, DH, DH), bf16),
        jax.ShapeDtypeStruct((T, D), bf16),
        jax.ShapeDtypeStruct((T, PW), bf16),
        jax.ShapeDtypeStruct((T, D), f32),
        jax.ShapeDtypeStruct((T, 1), f32),
        jax.ShapeDtypeStruct((T, D), bf16),
    ) + tuple(jax.ShapeDtypeStruct((N_DEV,) + b.shape, b.dtype) for b in gather)
    return pl.pallas_call(
        body, name="mix_forward", grid=(n_tiles,), out_shape=out_shape,
        in_specs=[tile(D), _const_spec((IN_W, D)), tile(DH), tile(DH),
                  _const_spec((HEADS, CHUNK, CHUNK)), _const_spec((HEADS, CHUNK, DH)), _const_spec((HEADS, CHUNK, DH)),
                  _const_spec((GROUPS, DH, DH)), _const_spec((1, PW)), _const_spec((D, D)),
                  _const_spec((1, D)), _const_spec((1, D))] + [hbm] * n_g,
        out_specs=(tile(3 * RW), tile(RW), tile(RW),
                   pl.BlockSpec((cpt, HEADS, DH, DH), lambda i: (i, 0, 0, 0)),
                   tile(D), tile(PW), tile(D), tile(1), tile(D)) + (hbm,) * n_g,
        scratch_shapes=[pltpu.VMEM((HEADS, DH, DH), f32), pltpu.VMEM((tt + HALO, PW), f32),
                        pltpu.VMEM((tt + HALO, PW), f32)] + _gather_sems(n_g),
        compiler_params=pltpu.CompilerParams(dimension_semantics=("arbitrary",), vmem_limit_bytes=V7X_VMEM_LIMIT),
    )(x, w_in_t, cos, sin, dmat, qd, kd, w_pool, pool_scale, w_out, ln1_g, ln1_b, *gather)


def _ffn_forward_backward(xhat1, rstd1, ln1_g, ln1_b, w_up_t, conv_w, conv_b, w_down, ln2_g, ln2_b, target,
                          tt=256, ch=256, kg=4):
    n_tiles = T // tt
    n_ch = D_FF // ch
    per = ch // 128
    FH = 16
    hb = tt // FH

    def body(xhat_ref, halo_ref, rstd_ref, g1_ref, b1_ref, wupt_ref, cw_ref, cb_ref, wdown_ref, g2_ref, b2_ref, tgt_ref,
             dz1_ref, dz2b_ref, du_ref, f_ref, loss_ref, dg2_ref, db2_ref, dg1_ref, db1_ref, dcb_ref, dcw_ref,
             gext_s, val_s, dhext_s):
        i = pl.program_id(0)
        tile_idx = n_tiles - 1 - i

        def rd(ref, off, c):
            return jnp.concatenate([ref[c * per + k, pl.ds(off, tt), :] for k in range(per)], axis=1)

        def wr(ref, off, c, val):
            for k in range(per):
                ref[c * per + k, pl.ds(off, val.shape[0]), :] = val[:, k * 128:(k + 1) * 128]

        @pl.when(i == 0)
        def _():
            for r in (loss_ref, dg2_ref, db2_ref, dg1_ref, db1_ref, dcb_ref, dcw_ref):
                r[...] = jnp.zeros_like(r)
            dhext_s[:, pl.ds(tt, 8), :] = jnp.zeros((D_FF // 128, 8, 128), f32)

        g1, b1 = g1_ref[...], b1_ref[...]
        xhat = xhat_ref[...]
        x1 = xhat * g1 + b1
        x1b = x1.astype(bf16)
        x1h = ((halo_ref[...] * g1 + b1) * jnp.where(tile_idx == 0, 0.0, 1.0)).astype(bf16)
        x1ext = jnp.concatenate([x1h, x1b], axis=0)

        for c in range(n_ch):
            cs = slice(c * ch, (c + 1) * ch)
            val = _dot(x1b, wupt_ref[pl.ds(c * ch, ch), :], NT)
            gate_ext = _dot(x1ext, wupt_ref[pl.ds(D_FF + c * ch, ch), :], NT)
            wr(gext_s, 0, c, gate_ext)
            hh = (cb_ref[:, cs] + cw_ref[0:1, cs] * rd(gext_s, FH - 2, c) + cw_ref[1:2, cs] * rd(gext_s, FH - 1, c)
                  + cw_ref[2:3, cs] * gate_ext[FH:])
            sg = _sigmoid(hh)
            act = hh * sg
            wr(dhext_s, 0, c, act)
            val_s[:, cs] = val * (sg + act * (1.0 - sg))
            f_ref[:, cs] = (act * val).astype(bf16)
            if (c + 1) % kg == 0 or c == n_ch - 1:
                lo, n = (c // kg) * kg * ch, (c % kg + 1) * ch
                part = _dot(f_ref[:, lo:lo + n], wdown_ref[pl.ds(lo, n), :])
                ffn = part if lo == 0 else ffn + part

        z = ALPHA * x1 + ffn
        mu = jnp.mean(z, axis=-1, keepdims=True)
        zc = z - mu
        rstd2 = lax.rsqrt(jnp.mean(zc * zc, axis=-1, keepdims=True) + LN_EPS)
        xh2 = zc * rstd2
        diff = xh2 * g2_ref[...] + b2_ref[...] - tgt_ref[...]
        loss_ref[...] += 0.5 * jnp.sum(diff * diff) / D
        dy = diff * (1.0 / D)
        dg2_ref[...] += jnp.sum(dy * xh2, axis=0, keepdims=True)
        db2_ref[...] += jnp.sum(dy, axis=0, keepdims=True)
        dyg = dy * g2_ref[...]
        dz2 = rstd2 * (dyg - jnp.mean(dyg, axis=-1, keepdims=True) - xh2 * jnp.mean(dyg * xh2, axis=-1, keepdims=True))
        dz2b = dz2.astype(bf16)
        dz2b_ref[...] = dz2b

        ahead = _dot(dz2b, wdown_ref[pl.ds(0, ch), :], NT)
        for c in range(n_ch):
            cs = slice(c * ch, (c + 1) * ch)
            df = ahead
            if c + 1 < n_ch:
                ahead = _dot(dz2b, wdown_ref[pl.ds((c + 1) * ch, ch), :], NT)
            dval = df * rd(dhext_s, 0, c)
            dh = df * val_s[:, cs]
            wr(dhext_s, 0, c, dh)
            dh1, dh2, g0 = rd(dhext_s, 1, c), rd(dhext_s, 2, c), rd(gext_s, FH, c)
            dcb_ref[:, cs] += jnp.sum(dh, axis=0, keepdims=True)
            dcw_ref[0:1, cs] += jnp.sum(dh2 * g0, axis=0, keepdims=True)
            dcw_ref[1:2, cs] += jnp.sum(dh1 * g0, axis=0, keepdims=True)
            dcw_ref[2:3, cs] += jnp.sum(dh * g0, axis=0, keepdims=True)
            dgate = cw_ref[2:3, cs] * dh + cw_ref[1:2, cs] * dh1 + cw_ref[0:1, cs] * dh2
            du_ref[:, cs] = dval.astype(bf16)
            du_ref[:, D_FF + c * ch: D_FF + (c + 1) * ch] = dgate.astype(bf16)
            if (c + 1) % kg == 0 or c == n_ch - 1:
                lo, n = (c // kg) * kg * ch, (c % kg + 1) * ch
                part = (_dot(du_ref[:, lo:lo + n], wupt_ref[pl.ds(lo, n), :])
                        + _dot(du_ref[:, D_FF + lo:D_FF + lo + n], wupt_ref[pl.ds(D_FF + lo, n), :]))
                dx1 = part if lo == 0 else dx1 + part
        dhext_s[:, pl.ds(tt, 8), :] = dhext_s[:, pl.ds(0, 8), :]
        dx1 = dx1 + ALPHA * dz2

        dg1_ref[...] += jnp.sum(dx1 * xhat, axis=0, keepdims=True)
        db1_ref[...] += jnp.sum(dx1, axis=0, keepdims=True)
        dxg = dx1 * g1
        dz1_ref[...] = rstd_ref[...] * (dxg - jnp.mean(dxg, axis=-1, keepdims=True)
                                        - xhat * jnp.mean(dxg * xhat, axis=-1, keepdims=True))

    rtile = lambda w: pl.BlockSpec((tt, w), lambda i: (n_tiles - 1 - i, 0))
    acc = lambda shape: pl.BlockSpec(shape, lambda i: (0, 0))
    out_shape = (
        jax.ShapeDtypeStruct((T, D), f32),
        jax.ShapeDtypeStruct((T, D), bf16),
        jax.ShapeDtypeStruct((T, 2 * D_FF), bf16),
        jax.ShapeDtypeStruct((T, D_FF), bf16),
        jax.ShapeDtypeStruct((8, 128), f32),
        jax.ShapeDtypeStruct((1, D), f32), jax.ShapeDtypeStruct((1, D), f32),
        jax.ShapeDtypeStruct((1, D), f32), jax.ShapeDtypeStruct((1, D), f32),
        jax.ShapeDtypeStruct((1, D_FF), f32), jax.ShapeDtypeStruct((3, D_FF), f32),
    )
    return pl.pallas_call(
        body, name="ffn_forward_backward", grid=(n_tiles,), out_shape=out_shape,
        in_specs=[rtile(D),
                  pl.BlockSpec((FH, D), lambda i: (jnp.maximum((n_tiles - 1 - i) * hb - 1, 0), 0)),
                  rtile(1), _const_spec((1, D)), _const_spec((1, D)), _const_spec((2 * D_FF, D)),
                  _const_spec((3, D_FF)), _const_spec((1, D_FF)), _const_spec((D_FF, D)),
                  _const_spec((1, D)), _const_spec((1, D)), rtile(D)],
        out_specs=(rtile(D), rtile(D), rtile(2 * D_FF), rtile(D_FF), acc((8, 128)),
                   acc((1, D)), acc((1, D)), acc((1, D)), acc((1, D)), acc((1, D_FF)), acc((3, D_FF))),
        scratch_shapes=[pltpu.VMEM((D_FF // 128, tt + FH, 128), f32), pltpu.VMEM((tt, D_FF), f32),
                        pltpu.VMEM((D_FF // 128, tt + 8, 128), f32)],
        compiler_params=pltpu.CompilerParams(dimension_semantics=("arbitrary",), vmem_limit_bytes=V7X_VMEM_LIMIT),
    )(xhat1, xhat1, rstd1, ln1_g, ln1_b, w_up_t, conv_w, conv_b, w_down, ln2_g, ln2_b, target)


def _mix_backward(dz1, w_out, qkv, g, oret, states, pooled, cos, sin, dmat, qd, kd, cdec, w_pool, pool_scale, w_in_t,
                  exchange, tt=512):
    n_tiles = T // tt
    cpt = tt // CHUNK
    n_e = len(exchange)

    def body(dz1_ref, wout_ref, qkv_ref, g_ref, oret_ref, states_ref, pooled_ref, cos_ref, sin_ref, dmat_ref, qd_ref,
             kd_ref, wpool_ref, pscale_ref, wint_ref, *rest):
        ein, rest = rest[:n_e], rest[n_e:]
        dproj_ref, gx_ref, dwpool_ref, dpscale_ref = rest[:4]
        eout, (dstate_s, dout_s, dqkv_s, eext_s, tmp_s, *sems) = rest[4:4 + n_e], rest[4 + n_e:]
        i = pl.program_id(0)
        tile_idx = n_tiles - 1 - i

        @pl.when(i == 0)
        def _():
            dstate_s[...] = jnp.zeros_like(dstate_s)
            dwpool_ref[...] = jnp.zeros_like(dwpool_ref)
            dpscale_ref[...] = jnp.zeros_like(dpscale_ref)
            eext_s[pl.ds(tt, HALO), :] = jnp.zeros((HALO, PW), f32)
            _chip_exchange_start(ein, eout, *sems)

        dz1 = dz1_ref[...]
        dcat = _dot(dz1.astype(bf16), wout_ref[...], NT)

        pos1 = (tile_idx * tt + lax.broadcasted_iota(jnp.int32, (tt, 1), 0) + 1).astype(f32)
        for gi, w in enumerate(WINDOWS):
            sl = slice(gi * DH, (gi + 1) * DH)
            dpo = dcat[:, RW + gi * DH: RW + (gi + 1) * DH]
            pooled_g = pooled_ref[:, sl]
            ylin = _dot(pooled_g, wpool_ref[gi])
            dpscale_ref[:, sl] += jnp.sum(dpo * ylin, axis=0, keepdims=True)
            dpw = (dpo * pscale_ref[:, sl]).astype(bf16)
            dwpool_ref[gi] += _dot(pooled_g, dpw, TN)
            dpooled = _dot(dpw, wpool_ref[gi], NT)
            eext_s[pl.ds(0, tt), sl] = dpooled / jnp.minimum(pos1, float(w))
            stages = int(math.log2(w))
            src = eext_s
            for s in range(stages):
                n = tt + 8 * (stages - 1 - s)
                shift = 2 ** s
                val = src[pl.ds(0, n), sl] + src[pl.ds(shift, n), sl]
                if s == stages - 1:
                    wsum = val
                else:
                    tmp_s[pl.ds(0, n), sl] = val
                    src = tmp_s
            dproj_ref[:, 4 * RW + gi * DH: 4 * RW + (gi + 1) * DH] = (wsum - dpooled).astype(bf16)
        eext_s[pl.ds(tt, HALO), :] = eext_s[pl.ds(0, HALO), :]

        for h in range(HEADS):
            sl = slice(h * DH, (h + 1) * DH)
            dr = dcat[:, sl]
            o = oret_ref[:, sl]
            r = lax.rsqrt(jnp.mean(o * o, axis=-1, keepdims=True) + RMS_EPS)
            rn = o * r
            gg = g_ref[:, sl]
            sg = _sigmoid(gg)
            dproj_ref[:, 3 * RW + h * DH: 3 * RW + (h + 1) * DH] = (dr * rn * (sg * (1.0 + gg * (1.0 - sg)))).astype(bf16)
            drn = dr * (gg * sg)
            dout_s[:, sl] = (r * (drn - rn * jnp.mean(drn * rn, axis=-1, keepdims=True))).astype(bf16)

        def chunk_step(j, carry):
            c = cpt - 1 - j
            rows = pl.ds(pl.multiple_of(c * CHUNK, CHUNK), CHUNK)
            for h in range(HEADS):
                q = qkv_ref[rows, h * DH:(h + 1) * DH]
                k = qkv_ref[rows, RW + h * DH: RW + (h + 1) * DH]
                v = qkv_ref[rows, 2 * RW + h * DH: 2 * RW + (h + 1) * DH]
                do = dout_s[rows, h * DH:(h + 1) * DH]
                stb = states_ref[c, h]
                dst = dstate_s[h]
                dstb = dst.astype(bf16)
                dm = dmat_ref[h]
                sb = (_dot(q, k, NT) * dm).astype(bf16)
                dsb = (_dot(do, v, NT) * dm).astype(bf16)
                qdq = (q.astype(f32) * qd_ref[h]).astype(bf16)
                kdk = (k.astype(f32) * kd_ref[h]).astype(bf16)
                dq = _dot(dsb, k) + _dot(do, stb, NT) * qd_ref[h]
                dk = _dot(dsb, q, TN) + _dot(v, dstb, NT) * kd_ref[h]
                dv = _dot(sb, do, TN) + _dot(kdk, dstb)
                dstate_s[h] = dst * cdec[h] + _dot(qdq, do, TN)
                dqkv_s[rows, h * DH:(h + 1) * DH] = dq
                dqkv_s[rows, RW + h * DH: RW + (h + 1) * DH] = dk
                dproj_ref[rows, 2 * RW + h * DH: 2 * RW + (h + 1) * DH] = dv.astype(bf16)
            return carry

        lax.fori_loop(0, cpt, chunk_step, 0)

        cos_t, sin_t = cos_ref[...], sin_ref[...]
        for part in range(2):
            for h in range(HEADS):
                sl = slice(part * RW + h * DH, part * RW + (h + 1) * DH)
                dr = dqkv_s[:, sl]
                dt = dr * cos_t - _swap_halves(dr) * sin_t
                if part == 1:
                    dt = dt * K_SCALE
                dproj_ref[:, sl] = dt.astype(bf16)

        gx_ref[...] = ALPHA * dz1 + _dot(dproj_ref[...], wint_ref[...])

        @pl.when(i == n_tiles - 1)
        def _():
            _chip_exchange_finish(ein, eout, *sems)

    rtile = lambda w: pl.BlockSpec((tt, w), lambda i: (n_tiles - 1 - i, 0))
    hbm = pl.BlockSpec(memory_space=pltpu.HBM)
    out_shape = (
        jax.ShapeDtypeStruct((T, IN_W), bf16),
        jax.ShapeDtypeStruct((T, D), f32),
        jax.ShapeDtypeStruct((GROUPS, DH, DH), f32),
        jax.ShapeDtypeStruct((1, PW), f32),
    ) + tuple(jax.ShapeDtypeStruct(e.shape, e.dtype) for e in exchange)
    return pl.pallas_call(
        body, name="mix_backward", grid=(n_tiles,), out_shape=out_shape,
        in_specs=[rtile(D), _const_spec((D, D)), rtile(3 * RW), rtile(RW), rtile(RW),
                  pl.BlockSpec((cpt, HEADS, DH, DH), lambda i: (n_tiles - 1 - i, 0, 0, 0)),
                  rtile(PW), rtile(DH), rtile(DH),
                  _const_spec((HEADS, CHUNK, CHUNK)), _const_spec((HEADS, CHUNK, DH)), _const_spec((HEADS, CHUNK, DH)),
                  _const_spec((GROUPS, DH, DH)), _const_spec((1, PW)), _const_spec((IN_W, D))] + [hbm] * n_e,
        out_specs=(rtile(IN_W), rtile(D), pl.BlockSpec((GROUPS, DH, DH), lambda i: (0, 0, 0)),
                   pl.BlockSpec((1, PW), lambda i: (0, 0))) + (hbm,) * n_e,
        scratch_shapes=[pltpu.VMEM((HEADS, DH, DH), f32), pltpu.VMEM((tt, RW), bf16), pltpu.VMEM((tt, 2 * RW), f32),
                        pltpu.VMEM((tt + HALO, PW), f32), pltpu.VMEM((tt + HALO, PW), f32)] + _chip_exchange_sems(n_e),
        compiler_params=pltpu.CompilerParams(dimension_semantics=("arbitrary",), vmem_limit_bytes=V7X_VMEM_LIMIT),
    )(dz1, w_out, qkv, g, oret, states, pooled, cos, sin, dmat, qd, kd, w_pool, pool_scale, w_in_t, *exchange)


def _weight_grad(a, b, name, tm, tk=1024):
    m = a.shape[1]
    n_k = T // tk

    def body(a_ref, b_ref, o_ref, acc_s):
        k = pl.program_id(1)

        @pl.when(k == 0)
        def _():
            acc_s[...] = jnp.zeros_like(acc_s)

        acc_s[...] += _dot(a_ref[...], b_ref[...].astype(bf16), TN)

        @pl.when(k == n_k - 1)
        def _():
            o_ref[...] = acc_s[...].astype(bf16)

    return pl.pallas_call(
        body, name=name, grid=(m // tm, n_k), out_shape=jax.ShapeDtypeStruct((m, D), bf16),
        in_specs=[pl.BlockSpec((tk, tm), lambda i, k: (k, i)), pl.BlockSpec((tk, D), lambda i, k: (k, 0))],
        out_specs=pl.BlockSpec((tm, D), lambda i, k: (i, 0)),
        scratch_shapes=[pltpu.VMEM((tm, D), f32)],
        compiler_params=pltpu.CompilerParams(dimension_semantics=("parallel", "arbitrary"),
                                             vmem_limit_bytes=V7X_VMEM_LIMIT),
    )(a, b)


CHIP_FLIPS = ((1, 0), (0, 1), (1, 1))


def _me():
    return lax.axis_index("x"), lax.axis_index("y"), lax.axis_index("c")


def _chip(me, k):
    x, y, _ = me
    if k == 0:
        return x, y
    fx, fy = CHIP_FLIPS[k - 1]
    return (1 - x if fx else x), (1 - y if fy else y)


def _slot(x, y, c):
    return 4 * x + 2 * y + c


def _remote(src, dst, send_sem, recv_sem, to):
    return pltpu.make_async_remote_copy(src_ref=src, dst_ref=dst, send_sem=send_sem, recv_sem=recv_sem,
                                        device_id=to, device_id_type=pl.DeviceIdType.MESH)


def _gather_sems(n):
    return [pltpu.SemaphoreType.DMA((7, n)), pltpu.SemaphoreType.DMA((7, n)), pltpu.SemaphoreType.DMA((n,))] if n else []


def _gather_copy(k, j, gin, gout, send_sems, recv_sems, sending):
    me = _me()
    x, y, c = me
    sibling = (x, y, 1 - c)
    src, to = gin[j], sibling
    if sending:
        block = me if k <= 3 else (*_chip(me, k - 3), c)
        if 1 <= k <= 3:
            to = (*_chip(me, k), c)
        if k >= 4:
            src = gout[j].at[_slot(*block)]
    else:
        block = sibling if k == 0 else (*_chip(me, k), c) if k <= 3 else (*_chip(me, k - 3), 1 - c)
    return _remote(src, gout[j].at[_slot(*block)], send_sems.at[k, j], recv_sems.at[k, j], to)


def _gather_start(gin, gout, send_sems, recv_sems, local_sems):
    for j in range(len(gin)):
        pltpu.make_async_copy(gin[j], gout[j].at[_slot(*_me())], local_sems.at[j]).start()
    for k in range(4):
        for j in range(len(gin)):
            _gather_copy(k, j, gin, gout, send_sems, recv_sems, True).start()


def _gather_forward(gin, gout, send_sems, recv_sems, local_sems):
    for k in range(1, 4):
        for j in range(len(gin)):
            _gather_copy(k, j, gin, gout, send_sems, recv_sems, False).wait_recv()
            _gather_copy(k + 3, j, gin, gout, send_sems, recv_sems, True).start()


def _gather_finish(gin, gout, send_sems, recv_sems, local_sems):
    for k in (0, 4, 5, 6):
        for j in range(len(gin)):
            _gather_copy(k, j, gin, gout, send_sems, recv_sems, False).wait_recv()
    for k in range(7):
        for j in range(len(gin)):
            _gather_copy(k, j, gin, gout, send_sems, recv_sems, True).wait_send()
    for j in range(len(gin)):
        pltpu.make_async_copy(gin[j], gout[j].at[_slot(*_me())], local_sems.at[j]).wait()


def _all_gather(blocks, name):
    n = len(blocks)

    def body(*refs):
        gin, gout, sems = refs[:n], refs[n:2 * n], refs[2 * n:]
        _gather_start(gin, gout, *sems)
        _gather_forward(gin, gout, *sems)
        _gather_finish(gin, gout, *sems)

    hbm = pl.BlockSpec(memory_space=pltpu.HBM)
    return pl.pallas_call(
        body, name=name,
        out_shape=tuple(jax.ShapeDtypeStruct((N_DEV,) + b.shape, b.dtype) for b in blocks),
        in_specs=[hbm] * n, out_specs=(hbm,) * n, scratch_shapes=_gather_sems(n),
    )(*blocks)


def _pair_reduce(parts, name):
    n = len(parts)

    def body(*refs):
        ins, own, others, landing = (refs[k * n:(k + 1) * n] for k in range(4))
        send_sems, recv_sems = refs[4 * n:]
        me = _me()
        x, y, c = me
        sibling = (x, y, 1 - c)
        sends = []
        for k in range(4):
            for j in range(n):
                cp = _remote(ins[j].at[_slot(*_chip(me, k), 1 - c)], landing[j].at[k], send_sems.at[k, j],
                             recv_sems.at[k, j], sibling)
                cp.start()
                sends.append(cp)
        for k in range(4):
            for j in range(n):
                _remote(ins[j].at[0], landing[j].at[k], send_sems.at[k, j], recv_sems.at[k, j], sibling).wait_recv()
                total = ins[j][_slot(*_chip(me, k), c)].astype(f32) + landing[j][k].astype(f32)
                if k == 0:
                    own[j][...] = total.astype(own[j].dtype)
                else:
                    others[j][k - 1] = total.astype(others[j].dtype)
        for cp in sends:
            cp.wait_send()

    vm = pl.BlockSpec(memory_space=pltpu.VMEM)
    return pl.pallas_call(
        body, name=name,
        out_shape=tuple(jax.ShapeDtypeStruct(p.shape[1:], p.dtype) for p in parts)
        + tuple(jax.ShapeDtypeStruct((3,) + p.shape[1:], p.dtype) for p in parts),
        in_specs=[vm] * n, out_specs=(vm,) * (2 * n),
        scratch_shapes=[pltpu.VMEM((4,) + p.shape[1:], p.dtype) for p in parts]
        + [pltpu.SemaphoreType.DMA((4, n)), pltpu.SemaphoreType.DMA((4, n))],
        compiler_params=pltpu.CompilerParams(vmem_limit_bytes=V7X_VMEM_LIMIT),
    )(*parts)


def _chip_exchange_sems(n):
    return [pltpu.SemaphoreType.DMA((3, n)), pltpu.SemaphoreType.DMA((3, n))] if n else []


def _chip_exchange_copy(k, j, ein, eout, send_sems, recv_sems):
    me = _me()
    return _remote(ein[j].at[k - 1], eout[j].at[k - 1], send_sems.at[k - 1, j], recv_sems.at[k - 1, j],
                   (*_chip(me, k), me[2]))


def _chip_exchange_start(ein, eout, send_sems, recv_sems):
    for k in range(1, 4):
        for j in range(len(ein)):
            _chip_exchange_copy(k, j, ein, eout, send_sems, recv_sems).start()


def _chip_exchange_finish(ein, eout, send_sems, recv_sems):
    for k in range(1, 4):
        for j in range(len(ein)):
            _chip_exchange_copy(k, j, ein, eout, send_sems, recv_sems).wait_recv()
    for k in range(1, 4):
        for j in range(len(ein)):
            _chip_exchange_copy(k, j, ein, eout, send_sems, recv_sems).wait_send()


def _chip_exchange(others, name):
    n = len(others)

    def body(*refs):
        ein, eout, sems = refs[:n], refs[n:2 * n], refs[2 * n:]
        _chip_exchange_start(ein, eout, *sems)
        _chip_exchange_finish(ein, eout, *sems)

    hbm = pl.BlockSpec(memory_space=pltpu.HBM)
    return pl.pallas_call(
        body, name=name, out_shape=tuple(jax.ShapeDtypeStruct(e.shape, e.dtype) for e in others),
        in_specs=[hbm] * n, out_specs=(hbm,) * n, scratch_shapes=_chip_exchange_sems(n),
    )(*others)


def _sum_parts(owns, arrived, name):
    n = len(owns)

    def body(*refs):
        for own, arr, out in zip(refs[:n], refs[n:2 * n], refs[2 * n:]):
            acc = own[...].astype(f32)
            for k in range(3):
                acc = acc + arr[k].astype(f32)
            out[...] = acc

    vm = pl.BlockSpec(memory_space=pltpu.VMEM)
    return pl.pallas_call(
        body, name=name, out_shape=tuple(jax.ShapeDtypeStruct(o.shape, f32) for o in owns),
        in_specs=[vm] * (2 * n), out_specs=(vm,) * n,
        compiler_params=pltpu.CompilerParams(vmem_limit_bytes=V7X_VMEM_LIMIT),
    )(*owns, *arrived)


ADAM_C1 = 1.0 / (1.0 - ADAM_B1 ** ADAM_STEP)
ADAM_C2 = 1.0 / (1.0 - ADAM_B2 ** ADAM_STEP)


def _adam_update(w, g, m, v):
    m = ADAM_B1 * m + (1.0 - ADAM_B1) * g
    v = ADAM_B2 * v + (1.0 - ADAM_B2) * (g * g)
    return -ADAM_LR * ((m * ADAM_C1) / (jnp.sqrt(v * ADAM_C2) + ADAM_EPS) + ADAM_WD * w), m, v


def _sum_adamw(own, arrived, w, m, v, name, steps):
    rows = own.shape[0]
    br = rows // steps

    def body(own_ref, arr_ref, w_ref, m_ref, v_ref, g_out, d_out, m_out, v_out):
        g = own_ref[...].astype(f32)
        for k in range(3):
            g = g + arr_ref[k].astype(f32)
        g_out[...] = g
        d_out[...], m_out[...], v_out[...] = _adam_update(w_ref[...], g, m_ref[...], v_ref[...])

    blk = pl.BlockSpec((br, D), lambda i: (i, 0))
    return pl.pallas_call(
        body, name=name, grid=(steps,), out_shape=(jax.ShapeDtypeStruct((rows, D), f32),) * 4,
        in_specs=[blk, pl.BlockSpec((3, br, D), lambda i: (0, i, 0)), blk, blk, blk], out_specs=(blk,) * 4,
        compiler_params=pltpu.CompilerParams(dimension_semantics=("parallel",), vmem_limit_bytes=V7X_VMEM_LIMIT),
    )(own, arrived, w, m, v)


def _adamw(ws, gs, ms, vs, name):
    n = len(ws)

    def body(*refs):
        w_r, g_r, m_r, v_r = (refs[k * n:(k + 1) * n] for k in range(4))
        d_o, m_o, v_o = (refs[(4 + k) * n:(5 + k) * n] for k in range(3))
        for j in range(n):
            d_o[j][...], m_o[j][...], v_o[j][...] = _adam_update(w_r[j][...], g_r[j][...], m_r[j][...], v_r[j][...])

    vm = pl.BlockSpec(memory_space=pltpu.VMEM)
    shapes = tuple(jax.ShapeDtypeStruct(w.shape, f32) for w in ws)
    return pl.pallas_call(
        body, name=name, out_shape=shapes * 3, in_specs=[vm] * (4 * n), out_specs=tuple([vm] * (3 * n)),
        compiler_params=pltpu.CompilerParams(vmem_limit_bytes=V7X_VMEM_LIMIT),
    )(*ws, *gs, *ms, *vs)


SMALL = (("w_pool", GROUPS * DH * DH), ("pool_scale", PW), ("ln1_g", D), ("ln1_b", D), ("conv_b", D_FF),
         ("ln2_g", D), ("ln2_b", D), ("conv_w", 3 * D_FF), ("loss", 1))
SMALL_ROWS = 640


def _pack(named):
    flat = jnp.concatenate([named[k].reshape(-1) for k, _ in SMALL])
    return jnp.pad(flat, (0, SMALL_ROWS * 128 - flat.shape[0])).reshape(SMALL_ROWS, 128)


def _unpack(packed):
    flat, out, at = packed.reshape(-1), {}, 0
    for k, size in SMALL:
        out[k] = flat[at:at + size]
        at += size
    return out


def kernel(x, w_in, w_pool, pool_scale, w_out, ln1_g, ln1_b, w_up, conv_w, conv_b, w_down, ln2_g, ln2_b, loss_target, m_w_in, m_w_pool, m_pool_scale, m_w_out, m_ln1_g, m_ln1_b, m_w_up, m_conv_w, m_conv_b, m_w_down, m_ln2_g, m_ln2_b, v_w_in, v_w_pool, v_pool_scale, v_w_out, v_ln1_g, v_ln1_b, v_w_up, v_conv_w, v_conv_b, v_w_down, v_ln2_g, v_ln2_b):
    me = 4 * lax.axis_index("x") + 2 * lax.axis_index("y") + lax.axis_index("c")
    x2, tgt = x[0], loss_target[0]

    g_in, g_out, g_cw = _all_gather([w_in[0].T.astype(bf16), w_out[0].astype(bf16), conv_w[0]], "gather_weights")
    w_in_t = g_in.reshape(IN_W, D)
    w_out_f = g_out.reshape(D, D)
    conv_w_f = jnp.transpose(g_cw, (1, 0, 2)).reshape(3, D_FF)
    w_pool_b = w_pool[0].astype(bf16)

    cos, sin = _rope_tables()
    dmat, qd, kd, cdec = _decay_tables()

    qkv, g, oret, states, cat, pooled, xhat1, rstd1, x1b, g_up, g_down = _mix_forward(
        x2, w_in_t, cos, sin, dmat, qd, kd, cdec, w_pool_b, pool_scale, w_out_f, ln1_g, ln1_b,
        gather=[w_up[0].T.astype(bf16), w_down[0].astype(bf16)])
    w_up_t = g_up.reshape(2 * D_FF, D)
    w_down_f = g_down.reshape(D_FF, D)
    dz1, dz2b, du, f, loss8, d_ln2_g, d_ln2_b, d_ln1_g, d_ln1_b, d_conv_b, d_conv_w = _ffn_forward_backward(
        xhat1, rstd1, ln1_g, ln1_b, w_up_t, conv_w_f, conv_b, w_down_f, ln2_g, ln2_b, tgt)

    dw_up_t = _weight_grad(du, x1b, "grad_w_up", tm=512).reshape(N_DEV, ROWS_UP, D)
    dw_down = _weight_grad(f, dz2b, "grad_w_down", tm=256).reshape(N_DEV, ROWS_DOWN, D)
    own_up, own_down, oth_up, oth_down = _pair_reduce([dw_up_t, dw_down], "pair_reduce_ffn")
    dproj, grad_x, d_w_pool, d_pool_scale, arr_up, arr_down = _mix_backward(
        dz1, w_out_f, qkv, g, oret, states, pooled, cos, sin, dmat, qd, kd, cdec, w_pool_b, pool_scale, w_in_t,
        exchange=[oth_up, oth_down])
    dw_in_t = _weight_grad(dproj, x2, "grad_w_in", tm=512).reshape(N_DEV, ROWS_IN, D)
    dw_out = _weight_grad(cat, dz1, "grad_w_out", tm=512).reshape(N_DEV, ROWS_OUT, D)
    small = _pack({"w_pool": d_w_pool, "pool_scale": d_pool_scale, "ln1_g": d_ln1_g, "ln1_b": d_ln1_b,
                   "conv_b": d_conv_b, "ln2_g": d_ln2_g, "ln2_b": d_ln2_b, "conv_w": d_conv_w, "loss": loss8[0, :1]})
    own_in, own_out, own_small, oth_in, oth_out, oth_small = _pair_reduce(
        [dw_in_t, dw_out, small.reshape(N_DEV, SMALL_ROWS // N_DEV, 128)], "pair_reduce_mix")
    arr_in, arr_out, arr_small = _chip_exchange([oth_in, oth_out, oth_small], "exchange_mix")

    names = ["w_in", "w_pool", "pool_scale", "w_out", "ln1_g", "ln1_b", "w_up", "conv_w", "conv_b", "w_down",
             "ln2_g", "ln2_b"]
    w_d = dict(w_in=w_in, w_pool=w_pool, pool_scale=pool_scale, w_out=w_out, ln1_g=ln1_g, ln1_b=ln1_b, w_up=w_up,
               conv_w=conv_w, conv_b=conv_b, w_down=w_down, ln2_g=ln2_g, ln2_b=ln2_b)
    m_d = dict(w_in=m_w_in, w_pool=m_w_pool, pool_scale=m_pool_scale, w_out=m_w_out, ln1_g=m_ln1_g, ln1_b=m_ln1_b,
               w_up=m_w_up, conv_w=m_conv_w, conv_b=m_conv_b, w_down=m_w_down, ln2_g=m_ln2_g, ln2_b=m_ln2_b)
    v_d = dict(w_in=v_w_in, w_pool=v_w_pool, pool_scale=v_pool_scale, w_out=v_w_out, ln1_g=v_ln1_g, ln1_b=v_ln1_b,
               w_up=v_w_up, conv_w=v_conv_w, conv_b=v_conv_b, w_down=v_w_down, ln2_g=v_ln2_g, ln2_b=v_ln2_b)
    g_d, delta, new_m, new_v = {}, {}, {}, {}

    big = (("w_in", own_in, arr_in, True, 4), ("w_out", own_out, arr_out, False, 2),
           ("w_up", own_up, arr_up, True, 4), ("w_down", own_down, arr_down, False, 2))
    for k, own, arr, transposed, steps in big:
        lay = (lambda a: a[0].T) if transposed else (lambda a: a[0])
        back = (lambda a: a.T[None]) if transposed else (lambda a: a[None])
        res = _sum_adamw(own, arr, lay(w_d[k]), lay(m_d[k]), lay(v_d[k]), "adamw_" + k, steps)
        g_d[k], delta[k], new_m[k], new_v[k] = (back(r) for r in res)

    (small_piece,) = _sum_parts([own_small], [arr_small], "sum_small_grads")
    (gs_small,) = _all_gather([small_piece], "gather_small_grads")
    gsm = _unpack(gs_small)
    gsm["conv_w"] = lax.dynamic_slice(gsm["conv_w"].reshape(3, D_FF), (0, me * (D_FF // N_DEV)), (3, D_FF // N_DEV))
    two_d = lambda a: a.reshape(-1, a.shape[-1])
    group = [k for k in names if k not in g_d]
    for k in group:
        g_d[k] = gsm[k].reshape(w_d[k].shape)
    res = _adamw([two_d(w_d[k]) for k in group], [two_d(g_d[k]) for k in group], [two_d(m_d[k]) for k in group],
                 [two_d(v_d[k]) for k in group], "adamw_small")
    for j, k in enumerate(group):
        delta[k] = res[j].reshape(w_d[k].shape)
        new_m[k] = res[len(group) + j].reshape(w_d[k].shape)
        new_v[k] = res[2 * len(group) + j].reshape(w_d[k].shape)

    loss = gsm["loss"].reshape(())
    return (loss, grad_x[None], *[g_d[k] for k in names], *[delta[k] for k in names], *[new_m[k] for k in names],
            *[new_v[k] for k in names])
```

```python
import functools
import math

import numpy as np
import jax
import jax.numpy as jnp
from jax import lax
from jax.experimental import pallas as pl
from jax.experimental.pallas import tpu as pltpu

f32 = jnp.float32
bf16 = jnp.bfloat16

N_DEV = 8
T = 4096
D = 1024
CHUNK = 64
MIX_TILE = 512
HEADS = 4
DH = 128
RW = HEADS * DH
PW = 512
GROUPS = 4
WINDOWS = (2, 4, 8, 16)
IN_W = 4 * RW + PW
D_FF = 2816
LN_EPS = 1e-5
RMS_EPS = 1e-6
ALPHA = 2.0 ** 0.25
K_SCALE = DH ** -0.5

ADAM_LR = 0.001
ADAM_B1 = 0.9
ADAM_B2 = 0.999
ADAM_EPS = 1e-08
ADAM_WD = 0.01
ADAM_STEP = 10

ROWS_IN, ROWS_OUT, ROWS_UP, ROWS_DOWN = IN_W // N_DEV, D // N_DEV, 2 * D_FF // N_DEV, D_FF // N_DEV

V7X_VMEM_LIMIT = 56 * 2 ** 20
HALO = 32

NT = (((1,), (1,)), ((), ()))
TN = (((0,), (0,)), ((), ()))
NN = (((1,), (0,)), ((), ()))


def _dot(a, b, dims=NN):
    return lax.dot_general(a, b, dims, preferred_element_type=f32)


def _const_spec(shape):
    zeros = (0,) * len(shape)
    return pl.BlockSpec(shape, lambda i: zeros, pipeline_mode=pl.Buffered(1))


def _sigmoid(x):
    return 0.5 * jnp.tanh(0.5 * x) + 0.5


def _decay_tables(tt):
    h = np.arange(HEADS, dtype=np.float64)
    log_gamma = np.log(1.0 - 2.0 ** (-5.0 - h)).astype(np.float32).astype(np.float64)[:, None, None]
    idx = np.arange(tt, dtype=np.float64)
    visible = (idx[None, :] // CHUNK) <= (idx[:, None] // CHUNK)
    mask = np.where(visible[None], np.exp(log_gamma * np.abs(idx[:, None] - idx[None, :])[None]), 0.0)
    qd = np.broadcast_to(np.exp(log_gamma * (idx[None, :, None] + 1.0)), (HEADS, tt, DH))
    kd = np.broadcast_to(np.exp(log_gamma * (tt - 1.0 - idx[None, :, None])), (HEADS, tt, DH))
    cd = np.exp(log_gamma[:, 0, 0] * tt)
    return (jnp.asarray(mask, f32), jnp.asarray(qd, f32), jnp.asarray(kd, f32), [float(c) for c in cd])


def _rope_tables():
    inv_freq = (10000.0 ** (-np.arange(0, DH, 2, dtype=np.float64) / DH)).astype(np.float32)
    ang = (np.arange(T, dtype=np.float32)[:, None] * inv_freq[None, :]).astype(np.float64)
    cos, sin = np.cos(ang), np.sin(ang)
    return (jnp.asarray(np.concatenate([cos, cos], axis=1), f32), jnp.asarray(np.concatenate([-sin, sin], axis=1), f32))


def _swap_halves(t):
    return pltpu.roll(t, DH // 2, axis=1)


def _mix_forward(x, w_in_t, cos, sin, dmat, qd, kd, cdec, w_pool, pool_scale, w_out, ln1_g, ln1_b, gather,
                 tt=MIX_TILE):
    n_tiles = T // tt
    n_g = len(gather)

    def body(x_ref, wint_ref, cos_ref, sin_ref, dmat_ref, qd_ref, kd_ref, wpool_ref, pscale_ref, wout_ref,
             g1_ref, b1_ref, *rest):
        gin, rest = rest[:n_g], rest[n_g:]
        qkv_ref, g_ref, oret_ref, states_ref, cat_ref, pooled_ref, xhat_ref, rstd_ref, x1b_ref = rest[:9]
        gout, (state_s, pext_s, tmp_s, *sems) = rest[9:9 + n_g], rest[9 + n_g:]
        i = pl.program_id(0)

        @pl.when(i == 0)
        def _():
            state_s[...] = jnp.zeros_like(state_s)
            pext_s[pl.ds(0, HALO), :] = jnp.zeros((HALO, PW), f32)
            _gather_start(gin, gout, *sems)

        @pl.when(i == n_tiles - 1)
        def _():
            _gather_forward(gin, gout, *sems)

        xb = x_ref[...].astype(bf16)
        cos_t, sin_t = cos_ref[...], sin_ref[...]
        for part in range(2):
            pr = _dot(xb, wint_ref[pl.ds(part * RW, RW), :], NT)
            for h in range(HEADS):
                t = pr[:, h * DH:(h + 1) * DH]
                r = t * cos_t + _swap_halves(t) * sin_t
                if part == 1:
                    r = r * K_SCALE
                qkv_ref[:, part * RW + h * DH: part * RW + (h + 1) * DH] = r.astype(bf16)
        qkv_ref[:, 2 * RW:3 * RW] = _dot(xb, wint_ref[pl.ds(2 * RW, RW), :], NT).astype(bf16)
        g_ref[...] = _dot(xb, wint_ref[pl.ds(3 * RW, RW), :], NT)
        pext_s[pl.ds(HALO, tt), :] = _dot(xb, wint_ref[pl.ds(4 * RW, PW), :], NT)

        for h in range(HEADS):
            q = qkv_ref[:, h * DH:(h + 1) * DH]
            k = qkv_ref[:, RW + h * DH: RW + (h + 1) * DH]
            v = qkv_ref[:, 2 * RW + h * DH: 2 * RW + (h + 1) * DH]
            s = _dot(q, k, NT) * dmat_ref[h]
            st = state_s[h]
            stb = st.astype(bf16)
            states_ref[0, h] = stb
            oret_ref[:, h * DH:(h + 1) * DH] = (_dot(s.astype(bf16), v)
                                               + _dot((q.astype(f32) * qd_ref[h]).astype(bf16), stb))
            state_s[h] = st * cdec[h] + _dot((k.astype(f32) * kd_ref[h]).astype(bf16), v, TN)

        for h in range(HEADS):
            sl = slice(h * DH, (h + 1) * DH)
            o = oret_ref[:, sl]
            r = lax.rsqrt(jnp.mean(o * o, axis=-1, keepdims=True) + RMS_EPS)
            gg = g_ref[:, sl]
            cat_ref[:, sl] = (o * r * (gg * _sigmoid(gg))).astype(bf16)

        pos1 = (i * tt + lax.broadcasted_iota(jnp.int32, (tt, 1), 0) + 1).astype(f32)
        for gi, w in enumerate(WINDOWS):
            sl = slice(gi * DH, (gi + 1) * DH)
            stages = int(math.log2(w))
            src = pext_s
            for s in range(stages):
                lo = HALO - 8 * (stages - 1 - s)
                n = tt + HALO - lo
                shift = 2 ** s
                val = src[pl.ds(lo, n), sl] + src[pl.ds(lo - shift, n), sl]
                if s == stages - 1:
                    wsum = val
                else:
                    tmp_s[pl.ds(lo, n), sl] = val
                    src = tmp_s
            p_g = pext_s[pl.ds(HALO, tt), sl]
            pooled = (wsum / jnp.minimum(pos1, float(w)) - p_g).astype(bf16)
            pooled_ref[:, sl] = pooled
            y = _dot(pooled, wpool_ref[gi]) * pscale_ref[:, sl]
            cat_ref[:, RW + gi * DH: RW + (gi + 1) * DH] = y.astype(bf16)
        pext_s[pl.ds(0, HALO), :] = pext_s[pl.ds(tt, HALO), :]

        z = ALPHA * x_ref[...] + _dot(cat_ref[...], wout_ref[...])
        mu = jnp.mean(z, axis=-1, keepdims=True)
        zc = z - mu
        rstd = lax.rsqrt(jnp.mean(zc * zc, axis=-1, keepdims=True) + LN_EPS)
        xhat = zc * rstd
        xhat_ref[...] = xhat
        rstd_ref[...] = rstd
        x1b_ref[...] = (xhat * g1_ref[...] + b1_ref[...]).astype(bf16)

        @pl.when(i == n_tiles - 1)
        def _():
            _gather_finish(gin, gout, *sems)

    tile = lambda w: pl.BlockSpec((tt, w), lambda i: (i, 0))
    hbm = pl.BlockSpec(memory_space=pltpu.HBM)
    out_shape = (
        jax.ShapeDtypeStruct((T, 3 * RW), bf16),
        jax.ShapeDtypeStruct((T, RW), f32),
        jax.ShapeDtypeStruct((T, RW), f32),
        jax.ShapeDtypeStruct((n_tiles, HEADS, DH, DH), bf16),
        jax.ShapeDtypeStruct((T, D), bf16),
        jax.ShapeDtypeStruct((T, PW), bf16),
        jax.ShapeDtypeStruct((T, D), f32),
        jax.ShapeDtypeStruct((T, 1), f32),
        jax.ShapeDtypeStruct((T, D), bf16),
    ) + tuple(jax.ShapeDtypeStruct((N_DEV,) + b.shape, b.dtype) for b in gather)
    return pl.pallas_call(
        body, name="mix_forward", grid=(n_tiles,), out_shape=out_shape,
        in_specs=[tile(D), _const_spec((IN_W, D)), tile(DH), tile(DH),
                  _const_spec((HEADS, tt, tt)), _const_spec((HEADS, tt, DH)), _const_spec((HEADS, tt, DH)),
                  _const_spec((GROUPS, DH, DH)), _const_spec((1, PW)), _const_spec((D, D)),
                  _const_spec((1, D)), _const_spec((1, D))] + [hbm] * n_g,
        out_specs=(tile(3 * RW), tile(RW), tile(RW),
                   pl.BlockSpec((1, HEADS, DH, DH), lambda i: (i, 0, 0, 0)),
                   tile(D), tile(PW), tile(D), tile(1), tile(D)) + (hbm,) * n_g,
        scratch_shapes=[pltpu.VMEM((HEADS, DH, DH), f32), pltpu.VMEM((tt + HALO, PW), f32),
                        pltpu.VMEM((tt + HALO, PW), f32)] + _gather_sems(n_g),
        compiler_params=pltpu.CompilerParams(dimension_semantics=("arbitrary",), vmem_limit_bytes=V7X_VMEM_LIMIT),
    )(x, w_in_t, cos, sin, dmat, qd, kd, w_pool, pool_scale, w_out, ln1_g, ln1_b, *gather)


def _ffn_forward_backward(xhat1, rstd1, ln1_g, ln1_b, w_up_t, conv_w, conv_b, w_down, ln2_g, ln2_b, target,
                          tt=256, ch=256, kg=4):
    n_tiles = T // tt
    n_ch = D_FF // ch
    per = ch // 128
    FH = 16
    hb = tt // FH

    def body(xhat_ref, halo_ref, rstd_ref, g1_ref, b1_ref, wupt_ref, cw_ref, cb_ref, wdown_ref, g2_ref, b2_ref, tgt_ref,
             dz1_ref, dz2b_ref, du_ref, f_ref, loss_ref, dg2_ref, db2_ref, dg1_ref, db1_ref, dcb_ref, dcw_ref,
             gext_s, val_s, dhext_s):
        i = pl.program_id(0)
        tile_idx = n_tiles - 1 - i

        def rd(ref, off, c):
            return jnp.concatenate([ref[c * per + k, pl.ds(off, tt), :] for k in range(per)], axis=1)

        def wr(ref, off, c, val):
            for k in range(per):
                ref[c * per + k, pl.ds(off, val.shape[0]), :] = val[:, k * 128:(k + 1) * 128]

        @pl.when(i == 0)
        def _():
            for r in (loss_ref, dg2_ref, db2_ref, dg1_ref, db1_ref, dcb_ref, dcw_ref):
                r[...] = jnp.zeros_like(r)
            dhext_s[:, pl.ds(tt, 8), :] = jnp.zeros((D_FF // 128, 8, 128), f32)

        g1, b1 = g1_ref[...], b1_ref[...]
        xhat = xhat_ref[...]
        x1 = xhat * g1 + b1
        x1b = x1.astype(bf16)
        x1h = ((halo_ref[...] * g1 + b1) * jnp.where(tile_idx == 0, 0.0, 1.0)).astype(bf16)
        x1ext = jnp.concatenate([x1h, x1b], axis=0)

        for c in range(n_ch):
            cs = slice(c * ch, (c + 1) * ch)
            val = _dot(x1b, wupt_ref[pl.ds(c * ch, ch), :], NT)
            gate_ext = _dot(x1ext, wupt_ref[pl.ds(D_FF + c * ch, ch), :], NT)
            wr(gext_s, 0, c, gate_ext)
            hh = (cb_ref[:, cs] + cw_ref[0:1, cs] * rd(gext_s, FH - 2, c) + cw_ref[1:2, cs] * rd(gext_s, FH - 1, c)
                  + cw_ref[2:3, cs] * gate_ext[FH:])
            sg = _sigmoid(hh)
            act = hh * sg
            wr(dhext_s, 0, c, act)
            val_s[:, cs] = val * (sg + act * (1.0 - sg))
            f_ref[:, cs] = (act * val).astype(bf16)
            if (c + 1) % kg == 0 or c == n_ch - 1:
                lo, n = (c // kg) * kg * ch, (c % kg + 1) * ch
                part = _dot(f_ref[:, lo:lo + n], wdown_ref[pl.ds(lo, n), :])
                ffn = part if lo == 0 else ffn + part

        z = ALPHA * x1 + ffn
        mu = jnp.mean(z, axis=-1, keepdims=True)
        zc = z - mu
        rstd2 = lax.rsqrt(jnp.mean(zc * zc, axis=-1, keepdims=True) + LN_EPS)
        xh2 = zc * rstd2
        diff = xh2 * g2_ref[...] + b2_ref[...] - tgt_ref[...]
        loss_ref[...] += 0.5 * jnp.sum(diff * diff) / D
        dy = diff * (1.0 / D)
        dg2_ref[...] += jnp.sum(dy * xh2, axis=0, keepdims=True)
        db2_ref[...] += jnp.sum(dy, axis=0, keepdims=True)
        dyg = dy * g2_ref[...]
        dz2 = rstd2 * (dyg - jnp.mean(dyg, axis=-1, keepdims=True) - xh2 * jnp.mean(dyg * xh2, axis=-1, keepdims=True))
        dz2b = dz2.astype(bf16)
        dz2b_ref[...] = dz2b

        ahead = _dot(dz2b, wdown_ref[pl.ds(0, ch), :], NT)
        for c in range(n_ch):
            cs = slice(c * ch, (c + 1) * ch)
            df = ahead
            if c + 1 < n_ch:
                ahead = _dot(dz2b, wdown_ref[pl.ds((c + 1) * ch, ch), :], NT)
            dval = df * rd(dhext_s, 0, c)
            dh = df * val_s[:, cs]
            wr(dhext_s, 0, c, dh)
            dh1, dh2, g0 = rd(dhext_s, 1, c), rd(dhext_s, 2, c), rd(gext_s, FH, c)
            dcb_ref[:, cs] += jnp.sum(dh, axis=0, keepdims=True)
            dcw_ref[0:1, cs] += jnp.sum(dh2 * g0, axis=0, keepdims=True)
            dcw_ref[1:2, cs] += jnp.sum(dh1 * g0, axis=0, keepdims=True)
            dcw_ref[2:3, cs] += jnp.sum(dh * g0, axis=0, keepdims=True)
            dgate = cw_ref[2:3, cs] * dh + cw_ref[1:2, cs] * dh1 + cw_ref[0:1, cs] * dh2
            du_ref[:, cs] = dval.astype(bf16)
            du_ref[:, D_FF + c * ch: D_FF + (c + 1) * ch] = dgate.astype(bf16)
            if (c + 1) % kg == 0 or c == n_ch - 1:
                lo, n = (c // kg) * kg * ch, (c % kg + 1) * ch
                part = (_dot(du_ref[:, lo:lo + n], wupt_ref[pl.ds(lo, n), :])
                        + _dot(du_ref[:, D_FF + lo:D_FF + lo + n], wupt_ref[pl.ds(D_FF + lo, n), :]))
                dx1 = part if lo == 0 else dx1 + part
        dhext_s[:, pl.ds(tt, 8), :] = dhext_s[:, pl.ds(0, 8), :]
        dx1 = dx1 + ALPHA * dz2

        dg1_ref[...] += jnp.sum(dx1 * xhat, axis=0, keepdims=True)
        db1_ref[...] += jnp.sum(dx1, axis=0, keepdims=True)
        dxg = dx1 * g1
        dz1_ref[...] = rstd_ref[...] * (dxg - jnp.mean(dxg, axis=-1, keepdims=True)
                                        - xhat * jnp.mean(dxg * xhat, axis=-1, keepdims=True))

    rtile = lambda w: pl.BlockSpec((tt, w), lambda i: (n_tiles - 1 - i, 0))
    acc = lambda shape: pl.BlockSpec(shape, lambda i: (0, 0))
    out_shape = (
        jax.ShapeDtypeStruct((T, D), f32),
        jax.ShapeDtypeStruct((T, D), bf16),
        jax.ShapeDtypeStruct((T, 2 * D_FF), bf16),
        jax.ShapeDtypeStruct((T, D_FF), bf16),
        jax.ShapeDtypeStruct((8, 128), f32),
        jax.ShapeDtypeStruct((1, D), f32), jax.ShapeDtypeStruct((1, D), f32),
        jax.ShapeDtypeStruct((1, D), f32), jax.ShapeDtypeStruct((1, D), f32),
        jax.ShapeDtypeStruct((1, D_FF), f32), jax.ShapeDtypeStruct((3, D_FF), f32),
    )
    return pl.pallas_call(
        body, name="ffn_forward_backward", grid=(n_tiles,), out_shape=out_shape,
        in_specs=[rtile(D),
                  pl.BlockSpec((FH, D), lambda i: (jnp.maximum((n_tiles - 1 - i) * hb - 1, 0), 0)),
                  rtile(1), _const_spec((1, D)), _const_spec((1, D)), _const_spec((2 * D_FF, D)),
                  _const_spec((3, D_FF)), _const_spec((1, D_FF)), _const_spec((D_FF, D)),
                  _const_spec((1, D)), _const_spec((1, D)), rtile(D)],
        out_specs=(rtile(D), rtile(D), rtile(2 * D_FF), rtile(D_FF), acc((8, 128)),
                   acc((1, D)), acc((1, D)), acc((1, D)), acc((1, D)), acc((1, D_FF)), acc((3, D_FF))),
        scratch_shapes=[pltpu.VMEM((D_FF // 128, tt + FH, 128), f32), pltpu.VMEM((tt, D_FF), f32),
                        pltpu.VMEM((D_FF // 128, tt + 8, 128), f32)],
        compiler_params=pltpu.CompilerParams(dimension_semantics=("arbitrary",), vmem_limit_bytes=V7X_VMEM_LIMIT),
    )(xhat1, xhat1, rstd1, ln1_g, ln1_b, w_up_t, conv_w, conv_b, w_down, ln2_g, ln2_b, target)


def _mix_backward(dz1, w_out, qkv, g, oret, states, pooled, cos, sin, dmat, qd, kd, cdec, w_pool, pool_scale, w_in_t,
                  exchange, tt=MIX_TILE):
    n_tiles = T // tt
    n_e = len(exchange)

    def body(dz1_ref, wout_ref, qkv_ref, g_ref, oret_ref, states_ref, pooled_ref, cos_ref, sin_ref, dmat_ref, qd_ref,
             kd_ref, wpool_ref, pscale_ref, wint_ref, *rest):
        ein, rest = rest[:n_e], rest[n_e:]
        dproj_ref, gx_ref, dwpool_ref, dpscale_ref = rest[:4]
        eout, (dstate_s, dout_s, eext_s, tmp_s, *sems) = rest[4:4 + n_e], rest[4 + n_e:]
        i = pl.program_id(0)
        tile_idx = n_tiles - 1 - i

        @pl.when(i == 0)
        def _():
            dstate_s[...] = jnp.zeros_like(dstate_s)
            dwpool_ref[...] = jnp.zeros_like(dwpool_ref)
            dpscale_ref[...] = jnp.zeros_like(dpscale_ref)
            eext_s[pl.ds(tt, HALO), :] = jnp.zeros((HALO, PW), f32)
            _chip_exchange_start(ein, eout, *sems)

        dz1 = dz1_ref[...]
        dcat = _dot(dz1.astype(bf16), wout_ref[...], NT)

        pos1 = (tile_idx * tt + lax.broadcasted_iota(jnp.int32, (tt, 1), 0) + 1).astype(f32)
        for gi, w in enumerate(WINDOWS):
            sl = slice(gi * DH, (gi + 1) * DH)
            dpo = dcat[:, RW + gi * DH: RW + (gi + 1) * DH]
            pooled_g = pooled_ref[:, sl]
            ylin = _dot(pooled_g, wpool_ref[gi])
            dpscale_ref[:, sl] += jnp.sum(dpo * ylin, axis=0, keepdims=True)
            dpw = (dpo * pscale_ref[:, sl]).astype(bf16)
            dwpool_ref[gi] += _dot(pooled_g, dpw, TN)
            dpooled = _dot(dpw, wpool_ref[gi], NT)
            eext_s[pl.ds(0, tt), sl] = dpooled / jnp.minimum(pos1, float(w))
            stages = int(math.log2(w))
            src = eext_s
            for s in range(stages):
                n = tt + 8 * (stages - 1 - s)
                shift = 2 ** s
                val = src[pl.ds(0, n), sl] + src[pl.ds(shift, n), sl]
                if s == stages - 1:
                    wsum = val
                else:
                    tmp_s[pl.ds(0, n), sl] = val
                    src = tmp_s
            dproj_ref[:, 4 * RW + gi * DH: 4 * RW + (gi + 1) * DH] = (wsum - dpooled).astype(bf16)
        eext_s[pl.ds(tt, HALO), :] = eext_s[pl.ds(0, HALO), :]

        for h in range(HEADS):
            sl = slice(h * DH, (h + 1) * DH)
            dr = dcat[:, sl]
            o = oret_ref[:, sl]
            r = lax.rsqrt(jnp.mean(o * o, axis=-1, keepdims=True) + RMS_EPS)
            rn = o * r
            gg = g_ref[:, sl]
            sg = _sigmoid(gg)
            dproj_ref[:, 3 * RW + h * DH: 3 * RW + (h + 1) * DH] = (dr * rn * (sg * (1.0 + gg * (1.0 - sg)))).astype(bf16)
            drn = dr * (gg * sg)
            dout_s[:, sl] = (r * (drn - rn * jnp.mean(drn * rn, axis=-1, keepdims=True))).astype(bf16)

        cos_t, sin_t = cos_ref[...], sin_ref[...]
        for h in range(HEADS):
            q = qkv_ref[:, h * DH:(h + 1) * DH]
            k = qkv_ref[:, RW + h * DH: RW + (h + 1) * DH]
            v = qkv_ref[:, 2 * RW + h * DH: 2 * RW + (h + 1) * DH]
            do = dout_s[:, h * DH:(h + 1) * DH]
            stb = states_ref[0, h]
            dst = dstate_s[h]
            dstb = dst.astype(bf16)
            sb = (_dot(q, k, NT) * dmat_ref[h]).astype(bf16)
            dsb = (_dot(do, v, NT) * dmat_ref[h]).astype(bf16)
            dq = _dot(dsb, k) + _dot(do, stb, NT) * qd_ref[h]
            dk = _dot(dsb, q, TN) + _dot(v, dstb, NT) * kd_ref[h]
            dv = _dot(sb, do, TN) + _dot((k.astype(f32) * kd_ref[h]).astype(bf16), dstb)
            dstate_s[h] = dst * cdec[h] + _dot((q.astype(f32) * qd_ref[h]).astype(bf16), do, TN)
            dproj_ref[:, h * DH:(h + 1) * DH] = (dq * cos_t - _swap_halves(dq) * sin_t).astype(bf16)
            dproj_ref[:, RW + h * DH: RW + (h + 1) * DH] = ((dk * cos_t - _swap_halves(dk) * sin_t) * K_SCALE).astype(bf16)
            dproj_ref[:, 2 * RW + h * DH: 2 * RW + (h + 1) * DH] = dv.astype(bf16)

        gx_ref[...] = ALPHA * dz1 + _dot(dproj_ref[...], wint_ref[...])

        @pl.when(i == n_tiles - 1)
        def _():
            _chip_exchange_finish(ein, eout, *sems)

    rtile = lambda w: pl.BlockSpec((tt, w), lambda i: (n_tiles - 1 - i, 0))
    hbm = pl.BlockSpec(memory_space=pltpu.HBM)
    out_shape = (
        jax.ShapeDtypeStruct((T, IN_W), bf16),
        jax.ShapeDtypeStruct((T, D), f32),
        jax.ShapeDtypeStruct((GROUPS, DH, DH), f32),
        jax.ShapeDtypeStruct((1, PW), f32),
    ) + tuple(jax.ShapeDtypeStruct(e.shape, e.dtype) for e in exchange)
    return pl.pallas_call(
        body, name="mix_backward", grid=(n_tiles,), out_shape=out_shape,
        in_specs=[rtile(D), _const_spec((D, D)), rtile(3 * RW), rtile(RW), rtile(RW),
                  pl.BlockSpec((1, HEADS, DH, DH), lambda i: (n_tiles - 1 - i, 0, 0, 0)),
                  rtile(PW), rtile(DH), rtile(DH),
                  _const_spec((HEADS, tt, tt)), _const_spec((HEADS, tt, DH)), _const_spec((HEADS, tt, DH)),
                  _const_spec((GROUPS, DH, DH)), _const_spec((1, PW)), _const_spec((IN_W, D))] + [hbm] * n_e,
        out_specs=(rtile(IN_W), rtile(D), pl.BlockSpec((GROUPS, DH, DH), lambda i: (0, 0, 0)),
                   pl.BlockSpec((1, PW), lambda i: (0, 0))) + (hbm,) * n_e,
        scratch_shapes=[pltpu.VMEM((HEADS, DH, DH), f32), pltpu.VMEM((tt, RW), bf16),
                        pltpu.VMEM((tt + HALO, PW), f32), pltpu.VMEM((tt + HALO, PW), f32)] + _chip_exchange_sems(n_e),
        compiler_params=pltpu.CompilerParams(dimension_semantics=("arbitrary",), vmem_limit_bytes=V7X_VMEM_LIMIT),
    )(dz1, w_out, qkv, g, oret, states, pooled, cos, sin, dmat, qd, kd, w_pool, pool_scale, w_in_t, *exchange)


def _weight_grad(a, b, name, tm, tk=1024):
    m = a.shape[1]
    n_k = T // tk

    def body(a_ref, b_ref, o_ref, acc_s):
        k = pl.program_id(1)

        @pl.when(k == 0)
        def _():
            acc_s[...] = jnp.zeros_like(acc_s)

        acc_s[...] += _dot(a_ref[...], b_ref[...].astype(bf16), TN)

        @pl.when(k == n_k - 1)
        def _():
            o_ref[...] = acc_s[...].astype(bf16)

    return pl.pallas_call(
        body, name=name, grid=(m // tm, n_k), out_shape=jax.ShapeDtypeStruct((m, D), bf16),
        in_specs=[pl.BlockSpec((tk, tm), lambda i, k: (k, i)), pl.BlockSpec((tk, D), lambda i, k: (k, 0))],
        out_specs=pl.BlockSpec((tm, D), lambda i, k: (i, 0)),
        scratch_shapes=[pltpu.VMEM((tm, D), f32)],
        compiler_params=pltpu.CompilerParams(dimension_semantics=("parallel", "arbitrary"),
                                             vmem_limit_bytes=V7X_VMEM_LIMIT),
    )(a, b)


CHIP_FLIPS = ((1, 0), (0, 1), (1, 1))


def _me():
    return lax.axis_index("x"), lax.axis_index("y"), lax.axis_index("c")


def _chip(me, k):
    x, y, _ = me
    if k == 0:
        return x, y
    fx, fy = CHIP_FLIPS[k - 1]
    return (1 - x if fx else x), (1 - y if fy else y)


def _slot(x, y, c):
    return 4 * x + 2 * y + c


def _remote(src, dst, send_sem, recv_sem, to):
    return pltpu.make_async_remote_copy(src_ref=src, dst_ref=dst, send_sem=send_sem, recv_sem=recv_sem,
                                        device_id=to, device_id_type=pl.DeviceIdType.MESH)


def _gather_sems(n):
    return [pltpu.SemaphoreType.DMA((7, n)), pltpu.SemaphoreType.DMA((7, n)), pltpu.SemaphoreType.DMA((n,))] if n else []


def _gather_copy(k, j, gin, gout, send_sems, recv_sems, sending):
    me = _me()
    x, y, c = me
    sibling = (x, y, 1 - c)
    src, to = gin[j], sibling
    if sending:
        block = me if k <= 3 else (*_chip(me, k - 3), c)
        if 1 <= k <= 3:
            to = (*_chip(me, k), c)
        if k >= 4:
            src = gout[j].at[_slot(*block)]
    else:
        block = sibling if k == 0 else (*_chip(me, k), c) if k <= 3 else (*_chip(me, k - 3), 1 - c)
    return _remote(src, gout[j].at[_slot(*block)], send_sems.at[k, j], recv_sems.at[k, j], to)


def _gather_start(gin, gout, send_sems, recv_sems, local_sems):
    for j in range(len(gin)):
        pltpu.make_async_copy(gin[j], gout[j].at[_slot(*_me())], local_sems.at[j]).start()
    for k in range(4):
        for j in range(len(gin)):
            _gather_copy(k, j, gin, gout, send_sems, recv_sems, True).start()


def _gather_forward(gin, gout, send_sems, recv_sems, local_sems):
    for k in range(1, 4):
        for j in range(len(gin)):
            _gather_copy(k, j, gin, gout, send_sems, recv_sems, False).wait_recv()
            _gather_copy(k + 3, j, gin, gout, send_sems, recv_sems, True).start()


def _gather_finish(gin, gout, send_sems, recv_sems, local_sems):
    for k in (0, 4, 5, 6):
        for j in range(len(gin)):
            _gather_copy(k, j, gin, gout, send_sems, recv_sems, False).wait_recv()
    for k in range(7):
        for j in range(len(gin)):
            _gather_copy(k, j, gin, gout, send_sems, recv_sems, True).wait_send()
    for j in range(len(gin)):
        pltpu.make_async_copy(gin[j], gout[j].at[_slot(*_me())], local_sems.at[j]).wait()


def _all_gather(blocks, name):
    n = len(blocks)

    def body(*refs):
        gin, gout, sems = refs[:n], refs[n:2 * n], refs[2 * n:]
        _gather_start(gin, gout, *sems)
        _gather_forward(gin, gout, *sems)
        _gather_finish(gin, gout, *sems)

    hbm = pl.BlockSpec(memory_space=pltpu.HBM)
    return pl.pallas_call(
        body, name=name,
        out_shape=tuple(jax.ShapeDtypeStruct((N_DEV,) + b.shape, b.dtype) for b in blocks),
        in_specs=[hbm] * n, out_specs=(hbm,) * n, scratch_shapes=_gather_sems(n),
    )(*blocks)


def _pair_reduce(parts, name):
    n = len(parts)

    def body(*refs):
        ins, own, others, landing = (refs[k * n:(k + 1) * n] for k in range(4))
        send_sems, recv_sems = refs[4 * n:]
        me = _me()
        x, y, c = me
        sibling = (x, y, 1 - c)
        sends = []
        for k in range(4):
            for j in range(n):
                cp = _remote(ins[j].at[_slot(*_chip(me, k), 1 - c)], landing[j].at[k], send_sems.at[k, j],
                             recv_sems.at[k, j], sibling)
                cp.start()
                sends.append(cp)
        for k in range(4):
            for j in range(n):
                _remote(ins[j].at[0], landing[j].at[k], send_sems.at[k, j], recv_sems.at[k, j], sibling).wait_recv()
                total = ins[j][_slot(*_chip(me, k), c)].astype(f32) + landing[j][k].astype(f32)
                if k == 0:
                    own[j][...] = total.astype(own[j].dtype)
                else:
                    others[j][k - 1] = total.astype(others[j].dtype)
        for cp in sends:
            cp.wait_send()

    vm = pl.BlockSpec(memory_space=pltpu.VMEM)
    return pl.pallas_call(
        body, name=name,
        out_shape=tuple(jax.ShapeDtypeStruct(p.shape[1:], p.dtype) for p in parts)
        + tuple(jax.ShapeDtypeStruct((3,) + p.shape[1:], p.dtype) for p in parts),
        in_specs=[vm] * n, out_specs=(vm,) * (2 * n),
        scratch_shapes=[pltpu.VMEM((4,) + p.shape[1:], p.dtype) for p in parts]
        + [pltpu.SemaphoreType.DMA((4, n)), pltpu.SemaphoreType.DMA((4, n))],
        compiler_params=pltpu.CompilerParams(vmem_limit_bytes=V7X_VMEM_LIMIT),
    )(*parts)


def _chip_exchange_sems(n):
    return [pltpu.SemaphoreType.DMA((3, n)), pltpu.SemaphoreType.DMA((3, n))] if n else []


def _chip_exchange_copy(k, j, ein, eout, send_sems, recv_sems):
    me = _me()
    return _remote(ein[j].at[k - 1], eout[j].at[k - 1], send_sems.at[k - 1, j], recv_sems.at[k - 1, j],
                   (*_chip(me, k), me[2]))


def _chip_exchange_start(ein, eout, send_sems, recv_sems):
    for k in range(1, 4):
        for j in range(len(ein)):
            _chip_exchange_copy(k, j, ein, eout, send_sems, recv_sems).start()


def _chip_exchange_finish(ein, eout, send_sems, recv_sems):
    for k in range(1, 4):
        for j in range(len(ein)):
            _chip_exchange_copy(k, j, ein, eout, send_sems, recv_sems).wait_recv()
    for k in range(1, 4):
        for j in range(len(ein)):
            _chip_exchange_copy(k, j, ein, eout, send_sems, recv_sems).wait_send()


def _chip_exchange(others, name):
    n = len(others)

    def body(*refs):
        ein, eout, sems = refs[:n], refs[n:2 * n], refs[2 * n:]
        _chip_exchange_start(ein, eout, *sems)
        _chip_exchange_finish(ein, eout, *sems)

    hbm = pl.BlockSpec(memory_space=pltpu.HBM)
    return pl.pallas_call(
        body, name=name, out_shape=tuple(jax.ShapeDtypeStruct(e.shape, e.dtype) for e in others),
        in_specs=[hbm] * n, out_specs=(hbm,) * n, scratch_shapes=_chip_exchange_sems(n),
    )(*others)


def _sum_parts(owns, arrived, name):
    n = len(owns)

    def body(*refs):
        for own, arr, out in zip(refs[:n], refs[n:2 * n], refs[2 * n:]):
            acc = own[...].astype(f32)
            for k in range(3):
                acc = acc + arr[k].astype(f32)
            out[...] = acc

    vm = pl.BlockSpec(memory_space=pltpu.VMEM)
    return pl.pallas_call(
        body, name=name, out_shape=tuple(jax.ShapeDtypeStruct(o.shape, f32) for o in owns),
        in_specs=[vm] * (2 * n), out_specs=(vm,) * n,
        compiler_params=pltpu.CompilerParams(vmem_limit_bytes=V7X_VMEM_LIMIT),
    )(*owns, *arrived)


ADAM_C1 = 1.0 / (1.0 - ADAM_B1 ** ADAM_STEP)
ADAM_C2 = 1.0 / (1.0 - ADAM_B2 ** ADAM_STEP)


def _adam_update(w, g, m, v):
    m = ADAM_B1 * m + (1.0 - ADAM_B1) * g
    v = ADAM_B2 * v + (1.0 - ADAM_B2) * (g * g)
    return -ADAM_LR * ((m * ADAM_C1) / (jnp.sqrt(v * ADAM_C2) + ADAM_EPS) + ADAM_WD * w), m, v


def _sum_adamw(own, arrived, w, m, v, name, steps):
    rows = own.shape[0]
    br = rows // steps

    def body(own_ref, arr_ref, w_ref, m_ref, v_ref, g_out, d_out, m_out, v_out):
        g = own_ref[...].astype(f32)
        for k in range(3):
            g = g + arr_ref[k].astype(f32)
        g_out[...] = g
        d_out[...], m_out[...], v_out[...] = _adam_update(w_ref[...], g, m_ref[...], v_ref[...])

    blk = pl.BlockSpec((br, D), lambda i: (i, 0))
    return pl.pallas_call(
        body, name=name, grid=(steps,), out_shape=(jax.ShapeDtypeStruct((rows, D), f32),) * 4,
        in_specs=[blk, pl.BlockSpec((3, br, D), lambda i: (0, i, 0)), blk, blk, blk], out_specs=(blk,) * 4,
        compiler_params=pltpu.CompilerParams(dimension_semantics=("parallel",), vmem_limit_bytes=V7X_VMEM_LIMIT),
    )(own, arrived, w, m, v)


def _adamw(ws, gs, ms, vs, name):
    n = len(ws)

    def body(*refs):
        w_r, g_r, m_r, v_r = (refs[k * n:(k + 1) * n] for k in range(4))
        d_o, m_o, v_o = (refs[(4 + k) * n:(5 + k) * n] for k in range(3))
        for j in range(n):
            d_o[j][...], m_o[j][...], v_o[j][...] = _adam_update(w_r[j][...], g_r[j][...], m_r[j][...], v_r[j][...])

    vm = pl.BlockSpec(memory_space=pltpu.VMEM)
    shapes = tuple(jax.ShapeDtypeStruct(w.shape, f32) for w in ws)
    return pl.pallas_call(
        body, name=name, out_shape=shapes * 3, in_specs=[vm] * (4 * n), out_specs=tuple([vm] * (3 * n)),
        compiler_params=pltpu.CompilerParams(vmem_limit_bytes=V7X_VMEM_LIMIT),
    )(*ws, *gs, *ms, *vs)


SMALL = (("w_pool", GROUPS * DH * DH), ("pool_scale", PW), ("ln1_g", D), ("ln1_b", D), ("conv_b", D_FF),
         ("ln2_g", D), ("ln2_b", D), ("conv_w", 3 * D_FF), ("loss", 1))
SMALL_ROWS = 640


def _pack(named):
    flat = jnp.concatenate([named[k].reshape(-1) for k, _ in SMALL])
    return jnp.pad(flat, (0, SMALL_ROWS * 128 - flat.shape[0])).reshape(SMALL_ROWS, 128)


def _unpack(packed):
    flat, out, at = packed.reshape(-1), {}, 0
    for k, size in SMALL:
        out[k] = flat[at:at + size]
        at += size
    return out


def kernel(x, w_in, w_pool, pool_scale, w_out, ln1_g, ln1_b, w_up, conv_w, conv_b, w_down, ln2_g, ln2_b, loss_target, m_w_in, m_w_pool, m_pool_scale, m_w_out, m_ln1_g, m_ln1_b, m_w_up, m_conv_w, m_conv_b, m_w_down, m_ln2_g, m_ln2_b, v_w_in, v_w_pool, v_pool_scale, v_w_out, v_ln1_g, v_ln1_b, v_w_up, v_conv_w, v_conv_b, v_w_down, v_ln2_g, v_ln2_b):
    me = 4 * lax.axis_index("x") + 2 * lax.axis_index("y") + lax.axis_index("c")
    x2, tgt = x[0], loss_target[0]

    g_in, g_out, g_cw = _all_gather([w_in[0].T.astype(bf16), w_out[0].astype(bf16), conv_w[0]], "gather_weights")
    w_in_t = g_in.reshape(IN_W, D)
    w_out_f = g_out.reshape(D, D)
    conv_w_f = jnp.transpose(g_cw, (1, 0, 2)).reshape(3, D_FF)
    w_pool_b = w_pool[0].astype(bf16)

    cos, sin = _rope_tables()
    dmat, qd, kd, cdec = _decay_tables(MIX_TILE)

    qkv, g, oret, states, cat, pooled, xhat1, rstd1, x1b, g_up, g_down = _mix_forward(
        x2, w_in_t, cos, sin, dmat, qd, kd, cdec, w_pool_b, pool_scale, w_out_f, ln1_g, ln1_b,
        gather=[w_up[0].T.astype(bf16), w_down[0].astype(bf16)])
    w_up_t = g_up.reshape(2 * D_FF, D)
    w_down_f = g_down.reshape(D_FF, D)
    dz1, dz2b, du, f, loss8, d_ln2_g, d_ln2_b, d_ln1_g, d_ln1_b, d_conv_b, d_conv_w = _ffn_forward_backward(
        xhat1, rstd1, ln1_g, ln1_b, w_up_t, conv_w_f, conv_b, w_down_f, ln2_g, ln2_b, tgt)

    dw_up_t = _weight_grad(du, x1b, "grad_w_up", tm=512).reshape(N_DEV, ROWS_UP, D)
    dw_down = _weight_grad(f, dz2b, "grad_w_down", tm=256).reshape(N_DEV, ROWS_DOWN, D)
    own_up, own_down, oth_up, oth_down = _pair_reduce([dw_up_t, dw_down], "pair_reduce_ffn")
    dproj, grad_x, d_w_pool, d_pool_scale, arr_up, arr_down = _mix_backward(
        dz1, w_out_f, qkv, g, oret, states, pooled, cos, sin, dmat, qd, kd, cdec, w_pool_b, pool_scale, w_in_t,
        exchange=[oth_up, oth_down])
    dw_in_t = _weight_grad(dproj, x2, "grad_w_in", tm=512).reshape(N_DEV, ROWS_IN, D)
    dw_out = _weight_grad(cat, dz1, "grad_w_out", tm=512).reshape(N_DEV, ROWS_OUT, D)
    small = _pack({"w_pool": d_w_pool, "pool_scale": d_pool_scale, "ln1_g": d_ln1_g, "ln1_b": d_ln1_b,
                   "conv_b": d_conv_b, "ln2_g": d_ln2_g, "ln2_b": d_ln2_b, "conv_w": d_conv_w, "loss": loss8[0, :1]})
    own_in, own_out, own_small, oth_in, oth_out, oth_small = _pair_reduce(
        [dw_in_t, dw_out, small.reshape(N_DEV, SMALL_ROWS // N_DEV, 128)], "pair_reduce_mix")
    arr_in, arr_out, arr_small = _chip_exchange([oth_in, oth_out, oth_small], "exchange_mix")

    names = ["w_in", "w_pool", "pool_scale", "w_out", "ln1_g", "ln1_b", "w_up", "conv_w", "conv_b", "w_down",
             "ln2_g", "ln2_b"]
    w_d = dict(w_in=w_in, w_pool=w_pool, pool_scale=pool_scale, w_out=w_out, ln1_g=ln1_g, ln1_b=ln1_b, w_up=w_up,
               conv_w=conv_w, conv_b=conv_b, w_down=w_down, ln2_g=ln2_g, ln2_b=ln2_b)
    m_d = dict(w_in=m_w_in, w_pool=m_w_pool, pool_scale=m_pool_scale, w_out=m_w_out, ln1_g=m_ln1_g, ln1_b=m_ln1_b,
               w_up=m_w_up, conv_w=m_conv_w, conv_b=m_conv_b, w_down=m_w_down, ln2_g=m_ln2_g, ln2_b=m_ln2_b)
    v_d = dict(w_in=v_w_in, w_pool=v_w_pool, pool_scale=v_pool_scale, w_out=v_w_out, ln1_g=v_ln1_g, ln1_b=v_ln1_b,
               w_up=v_w_up, conv_w=v_conv_w, conv_b=v_conv_b, w_down=v_w_down, ln2_g=v_ln2_g, ln2_b=v_ln2_b)
    g_d, delta, new_m, new_v = {}, {}, {}, {}

    big = (("w_in", own_in, arr_in, True, 4), ("w_out", own_out, arr_out, False, 2),
           ("w_up", own_up, arr_up, True, 4), ("w_down", own_down, arr_down, False, 2))
    for k, own, arr, transposed, steps in big:
        lay = (lambda a: a[0].T) if transposed else (lambda a: a[0])
        back = (lambda a: a.T[None]) if transposed else (lambda a: a[None])
        res = _sum_adamw(own, arr, lay(w_d[k]), lay(m_d[k]), lay(v_d[k]), "adamw_" + k, steps)
        g_d[k], delta[k], new_m[k], new_v[k] = (back(r) for r in res)

    (small_piece,) = _sum_parts([own_small], [arr_small], "sum_small_grads")
    (gs_small,) = _all_gather([small_piece], "gather_small_grads")
    gsm = _unpack(gs_small)
    gsm["conv_w"] = lax.dynamic_slice(gsm["conv_w"].reshape(3, D_FF), (0, me * (D_FF // N_DEV)), (3, D_FF // N_DEV))
    two_d = lambda a: a.reshape(-1, a.shape[-1])
    group = [k for k in names if k not in g_d]
    for k in group:
        g_d[k] = gsm[k].reshape(w_d[k].shape)
    res = _adamw([two_d(w_d[k]) for k in group], [two_d(g_d[k]) for k in group], [two_d(m_d[k]) for k in group],
                 [two_d(v_d[k]) for k in group], "adamw_small")
    for j, k in enumerate(group):
        delta[k] = res[j].reshape(w_d[k].shape)
        new_m[k] = res[len(group) + j].reshape(w_d[k].shape)
        new_v[k] = res[2 * len(group) + j].reshape(w_d[k].shape)

    loss = gsm["loss"].reshape(())
    return (loss, grad_x[None], *[g_d[k] for k in names], *[delta[k] for k in names], *[new_m[k] for k in names],
            *[new_v[k] for k in names])
```

```python
import functools
import math

import numpy as np
import jax
import jax.numpy as jnp
from jax import lax
from jax.experimental import pallas as pl
from jax.experimental.pallas import tpu as pltpu

f32 = jnp.float32
bf16 = jnp.bfloat16

N_DEV = 8
T = 4096
D = 1024
CHUNK = 64
MIX_TILE = 512
HEADS = 4
DH = 128
RW = HEADS * DH
PW = 512
GROUPS = 4
WINDOWS = (2, 4, 8, 16)
IN_W = 4 * RW + PW
D_FF = 2816
LN_EPS = 1e-5
RMS_EPS = 1e-6
ALPHA = 2.0 ** 0.25
K_SCALE = DH ** -0.5

ADAM_LR = 0.001
ADAM_B1 = 0.9
ADAM_B2 = 0.999
ADAM_EPS = 1e-08
ADAM_WD = 0.01
ADAM_STEP = 10

ROWS_IN, ROWS_OUT, ROWS_UP, ROWS_DOWN = IN_W // N_DEV, D // N_DEV, 2 * D_FF // N_DEV, D_FF // N_DEV

V7X_VMEM_LIMIT = 56 * 2 ** 20
HALO = 32

NT = (((1,), (1,)), ((), ()))
TN = (((0,), (0,)), ((), ()))
NN = (((1,), (0,)), ((), ()))


def _dot(a, b, dims=NN):
    return lax.dot_general(a, b, dims, preferred_element_type=f32)


def _const_spec(shape):
    zeros = (0,) * len(shape)
    return pl.BlockSpec(shape, lambda i: zeros, pipeline_mode=pl.Buffered(1))


def _sigmoid(x):
    return 0.5 * jnp.tanh(0.5 * x) + 0.5


def _decay_tables(tt):
    h = np.arange(HEADS, dtype=np.float64)
    log_gamma = np.log(1.0 - 2.0 ** (-5.0 - h)).astype(np.float32).astype(np.float64)[:, None, None]
    idx = np.arange(tt, dtype=np.float64)
    visible = (idx[None, :] // CHUNK) <= (idx[:, None] // CHUNK)
    mask = np.where(visible[None], np.exp(log_gamma * np.abs(idx[:, None] - idx[None, :])[None]), 0.0)
    qd = np.broadcast_to(np.exp(log_gamma * (idx[None, :, None] + 1.0)), (HEADS, tt, DH))
    kd = np.broadcast_to(np.exp(log_gamma * (tt - 1.0 - idx[None, :, None])), (HEADS, tt, DH))
    cd = np.exp(log_gamma[:, 0, 0] * tt)
    return (jnp.asarray(mask, f32), jnp.asarray(qd, f32), jnp.asarray(kd, f32), [float(c) for c in cd])


def _rope_tables():
    inv_freq = (10000.0 ** (-np.arange(0, DH, 2, dtype=np.float64) / DH)).astype(np.float32)
    ang = (np.arange(T, dtype=np.float32)[:, None] * inv_freq[None, :]).astype(np.float64)
    cos, sin = np.cos(ang), np.sin(ang)
    return (jnp.asarray(np.concatenate([cos, cos], axis=1), f32), jnp.asarray(np.concatenate([-sin, sin], axis=1), f32))


def _swap_halves(t):
    return pltpu.roll(t, DH // 2, axis=1)


def _mix_forward(x, w_in_t, cos, sin, dmat, qd, kd, cdec, w_pool, pool_scale, w_out, ln1_g, ln1_b, gather,
                 tt=MIX_TILE):
    n_tiles = T // tt
    n_g = len(gather)

    def body(x_ref, wint_ref, cos_ref, sin_ref, dmat_ref, qd_ref, kd_ref, wpool_ref, pscale_ref, wout_ref,
             g1_ref, b1_ref, *rest):
        gin, rest = rest[:n_g], rest[n_g:]
        qkv_ref, g_ref, oret_ref, states_ref, cat_ref, pooled_ref, xhat_ref, rstd_ref, x1b_ref = rest[:9]
        gout, (state_s, pext_s, tmp_s, *sems) = rest[9:9 + n_g], rest[9 + n_g:]
        i = pl.program_id(0)

        @pl.when(i == 0)
        def _():
            state_s[...] = jnp.zeros_like(state_s)
            pext_s[pl.ds(0, HALO), :] = jnp.zeros((HALO, PW), f32)
            _gather_start(gin, gout, *sems)

        @pl.when(i == n_tiles - 1)
        def _():
            _gather_forward(gin, gout, *sems)

        xb = x_ref[...].astype(bf16)
        cos_t, sin_t = cos_ref[...], sin_ref[...]
        for part in range(2):
            pr = _dot(xb, wint_ref[pl.ds(part * RW, RW), :], NT)
            for h in range(HEADS):
                t = pr[:, h * DH:(h + 1) * DH]
                r = t * cos_t + _swap_halves(t) * sin_t
                if part == 1:
                    r = r * K_SCALE
                qkv_ref[:, part * RW + h * DH: part * RW + (h + 1) * DH] = r.astype(bf16)
        qkv_ref[:, 2 * RW:3 * RW] = _dot(xb, wint_ref[pl.ds(2 * RW, RW), :], NT).astype(bf16)
        g_ref[...] = _dot(xb, wint_ref[pl.ds(3 * RW, RW), :], NT)
        pext_s[pl.ds(HALO, tt), :] = _dot(xb, wint_ref[pl.ds(4 * RW, PW), :], NT)

        for h in range(HEADS):
            q = qkv_ref[:, h * DH:(h + 1) * DH]
            k = qkv_ref[:, RW + h * DH: RW + (h + 1) * DH]
            v = qkv_ref[:, 2 * RW + h * DH: 2 * RW + (h + 1) * DH]
            s = _dot(q, k, NT) * dmat_ref[h]
            st = state_s[h]
            stb = st.astype(bf16)
            states_ref[0, h] = stb
            oret_ref[:, h * DH:(h + 1) * DH] = (_dot(s.astype(bf16), v)
                                               + _dot((q.astype(f32) * qd_ref[h]).astype(bf16), stb))
            state_s[h] = st * cdec[h] + _dot((k.astype(f32) * kd_ref[h]).astype(bf16), v, TN)

        for h in range(HEADS):
            sl = slice(h * DH, (h + 1) * DH)
            o = oret_ref[:, sl]
            r = lax.rsqrt(jnp.mean(o * o, axis=-1, keepdims=True) + RMS_EPS)
            gg = g_ref[:, sl]
            cat_ref[:, sl] = (o * r * (gg * _sigmoid(gg))).astype(bf16)

        pos1 = (i * tt + lax.broadcasted_iota(jnp.int32, (tt, 1), 0) + 1).astype(f32)
        for gi, w in enumerate(WINDOWS):
            sl = slice(gi * DH, (gi + 1) * DH)
            stages = int(math.log2(w))
            src = pext_s
            for s in range(stages):
                lo = HALO - 8 * (stages - 1 - s)
                n = tt + HALO - lo
                shift = 2 ** s
                val = src[pl.ds(lo, n), sl] + src[pl.ds(lo - shift, n), sl]
                if s == stages - 1:
                    wsum = val
                else:
                    tmp_s[pl.ds(lo, n), sl] = val
                    src = tmp_s
            p_g = pext_s[pl.ds(HALO, tt), sl]
            pooled = (wsum / jnp.minimum(pos1, float(w)) - p_g).astype(bf16)
            pooled_ref[:, sl] = pooled
            y = _dot(pooled, wpool_ref[gi]) * pscale_ref[:, sl]
            cat_ref[:, RW + gi * DH: RW + (gi + 1) * DH] = y.astype(bf16)
        pext_s[pl.ds(0, HALO), :] = pext_s[pl.ds(tt, HALO), :]

        z = ALPHA * x_ref[...] + _dot(cat_ref[...], wout_ref[...])
        mu = jnp.mean(z, axis=-1, keepdims=True)
        zc = z - mu
        rstd = lax.rsqrt(jnp.mean(zc * zc, axis=-1, keepdims=True) + LN_EPS)
        xhat = zc * rstd
        xhat_ref[...] = xhat
        rstd_ref[...] = rstd
        x1b_ref[...] = (xhat * g1_ref[...] + b1_ref[...]).astype(bf16)

        @pl.when(i == n_tiles - 1)
        def _():
            _gather_finish(gin, gout, *sems)

    tile = lambda w: pl.BlockSpec((tt, w), lambda i: (i, 0))
    hbm = pl.BlockSpec(memory_space=pltpu.HBM)
    out_shape = (
        jax.ShapeDtypeStruct((T, 3 * RW), bf16),
        jax.ShapeDtypeStruct((T, RW), f32),
        jax.ShapeDtypeStruct((T, RW), f32),
        jax.ShapeDtypeStruct((n_tiles, HEADS, DH, DH), bf16),
        jax.ShapeDtypeStruct((T, D), bf16),
        jax.ShapeDtypeStruct((T, PW), bf16),
        jax.ShapeDtypeStruct((T, D), f32),
        jax.ShapeDtypeStruct((T, 1), f32),
        jax.ShapeDtypeStruct((T, D), bf16),
    ) + tuple(jax.ShapeDtypeStruct((N_DEV,) + b.shape, b.dtype) for b in gather)
    return pl.pallas_call(
        body, name="mix_forward", grid=(n_tiles,), out_shape=out_shape,
        in_specs=[tile(D), _const_spec((IN_W, D)), tile(DH), tile(DH),
                  _const_spec((HEADS, tt, tt)), _const_spec((HEADS, tt, DH)), _const_spec((HEADS, tt, DH)),
                  _const_spec((GROUPS, DH, DH)), _const_spec((1, PW)), _const_spec((D, D)),
                  _const_spec((1, D)), _const_spec((1, D))] + [hbm] * n_g,
        out_specs=(tile(3 * RW), tile(RW), tile(RW),
                   pl.BlockSpec((1, HEADS, DH, DH), lambda i: (i, 0, 0, 0)),
                   tile(D), tile(PW), tile(D), tile(1), tile(D)) + (hbm,) * n_g,
        scratch_shapes=[pltpu.VMEM((HEADS, DH, DH), f32), pltpu.VMEM((tt + HALO, PW), f32),
                        pltpu.VMEM((tt + HALO, PW), f32)] + _gather_sems(n_g),
        compiler_params=pltpu.CompilerParams(dimension_semantics=("arbitrary",), vmem_limit_bytes=V7X_VMEM_LIMIT),
    )(x, w_in_t, cos, sin, dmat, qd, kd, w_pool, pool_scale, w_out, ln1_g, ln1_b, *gather)


def _ffn_forward_backward(xhat1, rstd1, ln1_g, ln1_b, w_up_t, conv_w, conv_b, w_down, ln2_g, ln2_b, target,
                          tt=256, ch=256, kg=4):
    n_tiles = T // tt
    n_ch = D_FF // ch
    per = ch // 128
    FH = 16
    hb = tt // FH

    def body(xhat_ref, halo_ref, rstd_ref, g1_ref, b1_ref, wupt_ref, cw_ref, cb_ref, wdown_ref, g2_ref, b2_ref, tgt_ref,
             dz1_ref, dz2b_ref, du_ref, f_ref, loss_ref, dg2_ref, db2_ref, dg1_ref, db1_ref, dcb_ref, dcw_ref,
             gext_s, val_s, dhext_s):
        i = pl.program_id(0)
        tile_idx = n_tiles - 1 - i

        def rd(ref, off, c):
            return jnp.concatenate([ref[c * per + k, pl.ds(off, tt), :] for k in range(per)], axis=1)

        def wr(ref, off, c, val):
            for k in range(per):
                ref[c * per + k, pl.ds(off, val.shape[0]), :] = val[:, k * 128:(k + 1) * 128]

        @pl.when(i == 0)
        def _():
            for r in (loss_ref, dg2_ref, db2_ref, dg1_ref, db1_ref, dcb_ref, dcw_ref):
                r[...] = jnp.zeros_like(r)
            dhext_s[:, pl.ds(tt, 8), :] = jnp.zeros((D_FF // 128, 8, 128), f32)

        g1, b1 = g1_ref[...], b1_ref[...]
        xhat = xhat_ref[...]
        x1 = xhat * g1 + b1
        x1b = x1.astype(bf16)
        x1h = ((halo_ref[...] * g1 + b1) * jnp.where(tile_idx == 0, 0.0, 1.0)).astype(bf16)
        x1ext = jnp.concatenate([x1h, x1b], axis=0)

        for c in range(n_ch):
            cs = slice(c * ch, (c + 1) * ch)
            val = _dot(x1b, wupt_ref[pl.ds(c * ch, ch), :], NT)
            gate_ext = _dot(x1ext, wupt_ref[pl.ds(D_FF + c * ch, ch), :], NT)
            wr(gext_s, 0, c, gate_ext)
            hh = (cb_ref[:, cs] + cw_ref[0:1, cs] * rd(gext_s, FH - 2, c) + cw_ref[1:2, cs] * rd(gext_s, FH - 1, c)
                  + cw_ref[2:3, cs] * gate_ext[FH:])
            sg = _sigmoid(hh)
            act = hh * sg
            wr(dhext_s, 0, c, act)
            val_s[:, cs] = val * (sg + act * (1.0 - sg))
            f_ref[:, cs] = (act * val).astype(bf16)
            if (c + 1) % kg == 0 or c == n_ch - 1:
                lo, n = (c // kg) * kg * ch, (c % kg + 1) * ch
                part = _dot(f_ref[:, lo:lo + n], wdown_ref[pl.ds(lo, n), :])
                ffn = part if lo == 0 else ffn + part

        z = ALPHA * x1 + ffn
        mu = jnp.mean(z, axis=-1, keepdims=True)
        zc = z - mu
        rstd2 = lax.rsqrt(jnp.mean(zc * zc, axis=-1, keepdims=True) + LN_EPS)
        xh2 = zc * rstd2
        diff = xh2 * g2_ref[...] + b2_ref[...] - tgt_ref[...]
        loss_ref[...] += 0.5 * jnp.sum(diff * diff) / D
        dy = diff * (1.0 / D)
        dg2_ref[...] += jnp.sum(dy * xh2, axis=0, keepdims=True)
        db2_ref[...] += jnp.sum(dy, axis=0, keepdims=True)
        dyg = dy * g2_ref[...]
        dz2 = rstd2 * (dyg - jnp.mean(dyg, axis=-1, keepdims=True) - xh2 * jnp.mean(dyg * xh2, axis=-1, keepdims=True))
        dz2b = dz2.astype(bf16)
        dz2b_ref[...] = dz2b

        ahead = _dot(dz2b, wdown_ref[pl.ds(0, ch), :], NT)
        for c in range(n_ch):
            cs = slice(c * ch, (c + 1) * ch)
            df = ahead
            if c + 1 < n_ch:
                ahead = _dot(dz2b, wdown_ref[pl.ds((c + 1) * ch, ch), :], NT)
            dval = df * rd(dhext_s, 0, c)
            dh = df * val_s[:, cs]
            wr(dhext_s, 0, c, dh)
            dh1, dh2, g0 = rd(dhext_s, 1, c), rd(dhext_s, 2, c), rd(gext_s, FH, c)
            dcb_ref[:, cs] += jnp.sum(dh, axis=0, keepdims=True)
            dcw_ref[0:1, cs] += jnp.sum(dh2 * g0, axis=0, keepdims=True)
            dcw_ref[1:2, cs] += jnp.sum(dh1 * g0, axis=0, keepdims=True)
            dcw_ref[2:3, cs] += jnp.sum(dh * g0, axis=0, keepdims=True)
            dgate = cw_ref[2:3, cs] * dh + cw_ref[1:2, cs] * dh1 + cw_ref[0:1, cs] * dh2
            du_ref[:, cs] = dval.astype(bf16)
            du_ref[:, D_FF + c * ch: D_FF + (c + 1) * ch] = dgate.astype(bf16)
            if (c + 1) % kg == 0 or c == n_ch - 1:
                lo, n = (c // kg) * kg * ch, (c % kg + 1) * ch
                part = (_dot(du_ref[:, lo:lo + n], wupt_ref[pl.ds(lo, n), :])
                        + _dot(du_ref[:, D_FF + lo:D_FF + lo + n], wupt_ref[pl.ds(D_FF + lo, n), :]))
                dx1 = part if lo == 0 else dx1 + part
        dhext_s[:, pl.ds(tt, 8), :] = dhext_s[:, pl.ds(0, 8), :]
        dx1 = dx1 + ALPHA * dz2

        dg1_ref[...] += jnp.sum(dx1 * xhat, axis=0, keepdims=True)
        db1_ref[...] += jnp.sum(dx1, axis=0, keepdims=True)
        dxg = dx1 * g1
        dz1_ref[...] = rstd_ref[...] * (dxg - jnp.mean(dxg, axis=-1, keepdims=True)
                                        - xhat * jnp.mean(dxg * xhat, axis=-1, keepdims=True))

    rtile = lambda w: pl.BlockSpec((tt, w), lambda i: (n_tiles - 1 - i, 0))
    acc = lambda shape: pl.BlockSpec(shape, lambda i: (0, 0))
    out_shape = (
        jax.ShapeDtypeStruct((T, D), f32),
        jax.ShapeDtypeStruct((T, D), bf16),
        jax.ShapeDtypeStruct((T, 2 * D_FF), bf16),
        jax.ShapeDtypeStruct((T, D_FF), bf16),
        jax.ShapeDtypeStruct((8, 128), f32),
        jax.ShapeDtypeStruct((1, D), f32), jax.ShapeDtypeStruct((1, D), f32),
        jax.ShapeDtypeStruct((1, D), f32), jax.ShapeDtypeStruct((1, D), f32),
        jax.ShapeDtypeStruct((1, D_FF), f32), jax.ShapeDtypeStruct((3, D_FF), f32),
    )
    return pl.pallas_call(
        body, name="ffn_forward_backward", grid=(n_tiles,), out_shape=out_shape,
        in_specs=[rtile(D),
                  pl.BlockSpec((FH, D), lambda i: (jnp.maximum((n_tiles - 1 - i) * hb - 1, 0), 0)),
                  rtile(1), _const_spec((1, D)), _const_spec((1, D)), _const_spec((2 * D_FF, D)),
                  _const_spec((3, D_FF)), _const_spec((1, D_FF)), _const_spec((D_FF, D)),
                  _const_spec((1, D)), _const_spec((1, D)), rtile(D)],
        out_specs=(rtile(D), rtile(D), rtile(2 * D_FF), rtile(D_FF), acc((8, 128)),
                   acc((1, D)), acc((1, D)), acc((1, D)), acc((1, D)), acc((1, D_FF)), acc((3, D_FF))),
        scratch_shapes=[pltpu.VMEM((D_FF // 128, tt + FH, 128), f32), pltpu.VMEM((tt, D_FF), f32),
                        pltpu.VMEM((D_FF // 128, tt + 8, 128), f32)],
        compiler_params=pltpu.CompilerParams(dimension_semantics=("arbitrary",), vmem_limit_bytes=V7X_VMEM_LIMIT),
    )(xhat1, xhat1, rstd1, ln1_g, ln1_b, w_up_t, conv_w, conv_b, w_down, ln2_g, ln2_b, target)


def _mix_backward(dz1, w_out, qkv, g, oret, states, pooled, cos, sin, dmat, qd, kd, cdec, w_pool, pool_scale, w_in_t,
                  exchange, tt=MIX_TILE):
    n_tiles = T // tt
    n_e = len(exchange)

    def body(dz1_ref, wout_ref, qkv_ref, g_ref, oret_ref, states_ref, pooled_ref, cos_ref, sin_ref, dmat_ref, qd_ref,
             kd_ref, wpool_ref, pscale_ref, wint_ref, *rest):
        ein, rest = rest[:n_e], rest[n_e:]
        dproj_ref, gx_ref, dwpool_ref, dpscale_ref = rest[:4]
        eout, (dstate_s, dout_s, eext_s, tmp_s, *sems) = rest[4:4 + n_e], rest[4 + n_e:]
        i = pl.program_id(0)
        tile_idx = n_tiles - 1 - i

        @pl.when(i == 0)
        def _():
            dstate_s[...] = jnp.zeros_like(dstate_s)
            dwpool_ref[...] = jnp.zeros_like(dwpool_ref)
            dpscale_ref[...] = jnp.zeros_like(dpscale_ref)
            eext_s[pl.ds(tt, HALO), :] = jnp.zeros((HALO, PW), f32)
            _chip_exchange_start(ein, eout, *sems)

        dz1 = dz1_ref[...]
        dcat = _dot(dz1.astype(bf16), wout_ref[...], NT)

        pos1 = (tile_idx * tt + lax.broadcasted_iota(jnp.int32, (tt, 1), 0) + 1).astype(f32)
        for gi, w in enumerate(WINDOWS):
            sl = slice(gi * DH, (gi + 1) * DH)
            dpo = dcat[:, RW + gi * DH: RW + (gi + 1) * DH]
            pooled_g = pooled_ref[:, sl]
            ylin = _dot(pooled_g, wpool_ref[gi])
            dpscale_ref[:, sl] += jnp.sum(dpo * ylin, axis=0, keepdims=True)
            dpw = (dpo * pscale_ref[:, sl]).astype(bf16)
            dwpool_ref[gi] += _dot(pooled_g, dpw, TN)
            dpooled = _dot(dpw, wpool_ref[gi], NT)
            eext_s[pl.ds(0, tt), sl] = dpooled / jnp.minimum(pos1, float(w))
            stages = int(math.log2(w))
            src = eext_s
            for s in range(stages):
                n = tt + 8 * (stages - 1 - s)
                shift = 2 ** s
                val = src[pl.ds(0, n), sl] + src[pl.ds(shift, n), sl]
                if s == stages - 1:
                    wsum = val
                else:
                    tmp_s[pl.ds(0, n), sl] = val
                    src = tmp_s
            dproj_ref[:, 4 * RW + gi * DH: 4 * RW + (gi + 1) * DH] = (wsum - dpooled).astype(bf16)
        eext_s[pl.ds(tt, HALO), :] = eext_s[pl.ds(0, HALO), :]

        for h in range(HEADS):
            sl = slice(h * DH, (h + 1) * DH)
            dr = dcat[:, sl]
            o = oret_ref[:, sl]
            r = lax.rsqrt(jnp.mean(o * o, axis=-1, keepdims=True) + RMS_EPS)
            rn = o * r
            gg = g_ref[:, sl]
            sg = _sigmoid(gg)
            dproj_ref[:, 3 * RW + h * DH: 3 * RW + (h + 1) * DH] = (dr * rn * (sg * (1.0 + gg * (1.0 - sg)))).astype(bf16)
            drn = dr * (gg * sg)
            dout_s[:, sl] = (r * (drn - rn * jnp.mean(drn * rn, axis=-1, keepdims=True))).astype(bf16)

        cos_t, sin_t = cos_ref[...], sin_ref[...]
        for h in range(HEADS):
            q = qkv_ref[:, h * DH:(h + 1) * DH]
            k = qkv_ref[:, RW + h * DH: RW + (h + 1) * DH]
            v = qkv_ref[:, 2 * RW + h * DH: 2 * RW + (h + 1) * DH]
            do = dout_s[:, h * DH:(h + 1) * DH]
            stb = states_ref[0, h]
            dst = dstate_s[h]
            dstb = dst.astype(bf16)
            sb = (_dot(q, k, NT) * dmat_ref[h]).astype(bf16)
            dsb = (_dot(do, v, NT) * dmat_ref[h]).astype(bf16)
            dq = _dot(dsb, k) + _dot(do, stb, NT) * qd_ref[h]
            dk = _dot(dsb, q, TN) + _dot(v, dstb, NT) * kd_ref[h]
            dv = _dot(sb, do, TN) + _dot((k.astype(f32) * kd_ref[h]).astype(bf16), dstb)
            dstate_s[h] = dst * cdec[h] + _dot((q.astype(f32) * qd_ref[h]).astype(bf16), do, TN)
            dproj_ref[:, h * DH:(h + 1) * DH] = (dq * cos_t - _swap_halves(dq) * sin_t).astype(bf16)
            dproj_ref[:, RW + h * DH: RW + (h + 1) * DH] = ((dk * cos_t - _swap_halves(dk) * sin_t) * K_SCALE).astype(bf16)
            dproj_ref[:, 2 * RW + h * DH: 2 * RW + (h + 1) * DH] = dv.astype(bf16)

        gx_ref[...] = ALPHA * dz1 + _dot(dproj_ref[...], wint_ref[...])

        @pl.when(i == n_tiles - 1)
        def _():
            _chip_exchange_finish(ein, eout, *sems)

    rtile = lambda w: pl.BlockSpec((tt, w), lambda i: (n_tiles - 1 - i, 0))
    hbm = pl.BlockSpec(memory_space=pltpu.HBM)
    out_shape = (
        jax.ShapeDtypeStruct((T, IN_W), bf16),
        jax.ShapeDtypeStruct((T, D), f32),
        jax.ShapeDtypeStruct((GROUPS, DH, DH), f32),
        jax.ShapeDtypeStruct((1, PW), f32),
    ) + tuple(jax.ShapeDtypeStruct(e.shape, e.dtype) for e in exchange)
    return pl.pallas_call(
        body, name="mix_backward", grid=(n_tiles,), out_shape=out_shape,
        in_specs=[rtile(D), _const_spec((D, D)), rtile(3 * RW), rtile(RW), rtile(RW),
                  pl.BlockSpec((1, HEADS, DH, DH), lambda i: (n_tiles - 1 - i, 0, 0, 0)),
                  rtile(PW), rtile(DH), rtile(DH),
                  _const_spec((HEADS, tt, tt)), _const_spec((HEADS, tt, DH)), _const_spec((HEADS, tt, DH)),
                  _const_spec((GROUPS, DH, DH)), _const_spec((1, PW)), _const_spec((IN_W, D))] + [hbm] * n_e,
        out_specs=(rtile(IN_W), rtile(D), pl.BlockSpec((GROUPS, DH, DH), lambda i: (0, 0, 0)),
                   pl.BlockSpec((1, PW), lambda i: (0, 0))) + (hbm,) * n_e,
        scratch_shapes=[pltpu.VMEM((HEADS, DH, DH), f32), pltpu.VMEM((tt, RW), bf16),
                        pltpu.VMEM((tt + HALO, PW), f32), pltpu.VMEM((tt + HALO, PW), f32)] + _chip_exchange_sems(n_e),
        compiler_params=pltpu.CompilerParams(dimension_semantics=("arbitrary",), vmem_limit_bytes=V7X_VMEM_LIMIT),
    )(dz1, w_out, qkv, g, oret, states, pooled, cos, sin, dmat, qd, kd, w_pool, pool_scale, w_in_t, *exchange)


def _weight_grad(a, b, name, tm, exchange=(), tk=2048):
    m = a.shape[1]
    n_m, n_k, n_e = m // tm, T // tk, len(exchange)

    def body(a_ref, b_ref, *rest):
        ein, o_ref, eout, (acc_s, *sems) = rest[:n_e], rest[n_e], rest[n_e + 1:2 * n_e + 1], rest[2 * n_e + 1:]
        i, k = pl.program_id(0), pl.program_id(1)

        if n_e:
            @pl.when((i == 0) & (k == 0))
            def _():
                _chip_exchange_start(ein, eout, *sems)

        @pl.when(k == 0)
        def _():
            acc_s[...] = jnp.zeros_like(acc_s)

        acc_s[...] += _dot(a_ref[...], b_ref[pl.ds(pl.multiple_of(k * tk, tk), tk), :].astype(bf16), TN)

        @pl.when(k == n_k - 1)
        def _():
            o_ref[...] = acc_s[...].astype(bf16)

        if n_e:
            @pl.when((i == n_m - 1) & (k == n_k - 1))
            def _():
                _chip_exchange_finish(ein, eout, *sems)

    hbm = pl.BlockSpec(memory_space=pltpu.HBM)
    return pl.pallas_call(
        body, name=name, grid=(n_m, n_k),
        out_shape=(jax.ShapeDtypeStruct((m, D), bf16),) + tuple(jax.ShapeDtypeStruct(e.shape, e.dtype) for e in exchange),
        in_specs=[pl.BlockSpec((tk, tm), lambda i, k: (k, i)),
                  pl.BlockSpec((T, D), lambda i, k: (0, 0), pipeline_mode=pl.Buffered(1))] + [hbm] * n_e,
        out_specs=(pl.BlockSpec((tm, D), lambda i, k: (i, 0)),) + (hbm,) * n_e,
        scratch_shapes=[pltpu.VMEM((tm, D), f32)] + _chip_exchange_sems(n_e),
        compiler_params=pltpu.CompilerParams(dimension_semantics=("arbitrary", "arbitrary"),
                                             vmem_limit_bytes=V7X_VMEM_LIMIT),
    )(a, b, *exchange)


CHIP_FLIPS = ((1, 0), (0, 1), (1, 1))


def _me():
    return lax.axis_index("x"), lax.axis_index("y"), lax.axis_index("c")


def _chip(me, k):
    x, y, _ = me
    if k == 0:
        return x, y
    fx, fy = CHIP_FLIPS[k - 1]
    return (1 - x if fx else x), (1 - y if fy else y)


def _slot(x, y, c):
    return 4 * x + 2 * y + c


def _remote(src, dst, send_sem, recv_sem, to):
    return pltpu.make_async_remote_copy(src_ref=src, dst_ref=dst, send_sem=send_sem, recv_sem=recv_sem,
                                        device_id=to, device_id_type=pl.DeviceIdType.MESH)


def _gather_sems(n):
    return [pltpu.SemaphoreType.DMA((7, n)), pltpu.SemaphoreType.DMA((7, n)), pltpu.SemaphoreType.DMA((n,))] if n else []


def _gather_copy(k, j, gin, gout, send_sems, recv_sems, sending):
    me = _me()
    x, y, c = me
    sibling = (x, y, 1 - c)
    src, to = gin[j], sibling
    if sending:
        block = me if k <= 3 else (*_chip(me, k - 3), c)
        if 1 <= k <= 3:
            to = (*_chip(me, k), c)
        if k >= 4:
            src = gout[j].at[_slot(*block)]
    else:
        block = sibling if k == 0 else (*_chip(me, k), c) if k <= 3 else (*_chip(me, k - 3), 1 - c)
    return _remote(src, gout[j].at[_slot(*block)], send_sems.at[k, j], recv_sems.at[k, j], to)


def _gather_start(gin, gout, send_sems, recv_sems, local_sems):
    for j in range(len(gin)):
        pltpu.make_async_copy(gin[j], gout[j].at[_slot(*_me())], local_sems.at[j]).start()
    for k in range(4):
        for j in range(len(gin)):
            _gather_copy(k, j, gin, gout, send_sems, recv_sems, True).start()


def _gather_forward(gin, gout, send_sems, recv_sems, local_sems):
    for k in range(1, 4):
        for j in range(len(gin)):
            _gather_copy(k, j, gin, gout, send_sems, recv_sems, False).wait_recv()
            _gather_copy(k + 3, j, gin, gout, send_sems, recv_sems, True).start()


def _gather_finish(gin, gout, send_sems, recv_sems, local_sems):
    for k in (0, 4, 5, 6):
        for j in range(len(gin)):
            _gather_copy(k, j, gin, gout, send_sems, recv_sems, False).wait_recv()
    for k in range(7):
        for j in range(len(gin)):
            _gather_copy(k, j, gin, gout, send_sems, recv_sems, True).wait_send()
    for j in range(len(gin)):
        pltpu.make_async_copy(gin[j], gout[j].at[_slot(*_me())], local_sems.at[j]).wait()


def _all_gather(blocks, name):
    n = len(blocks)

    def body(*refs):
        gin, gout, sems = refs[:n], refs[n:2 * n], refs[2 * n:]
        _gather_start(gin, gout, *sems)
        _gather_forward(gin, gout, *sems)
        _gather_finish(gin, gout, *sems)

    hbm = pl.BlockSpec(memory_space=pltpu.HBM)
    return pl.pallas_call(
        body, name=name,
        out_shape=tuple(jax.ShapeDtypeStruct((N_DEV,) + b.shape, b.dtype) for b in blocks),
        in_specs=[hbm] * n, out_specs=(hbm,) * n, scratch_shapes=_gather_sems(n),
    )(*blocks)


def _pair_reduce(parts, name):
    n = len(parts)

    def body(*refs):
        ins, own, others, landing = (refs[k * n:(k + 1) * n] for k in range(4))
        send_sems, recv_sems = refs[4 * n:]
        me = _me()
        x, y, c = me
        sibling = (x, y, 1 - c)
        sends = []
        for k in range(4):
            for j in range(n):
                cp = _remote(ins[j].at[_slot(*_chip(me, k), 1 - c)], landing[j].at[k], send_sems.at[k, j],
                             recv_sems.at[k, j], sibling)
                cp.start()
                sends.append(cp)
        for k in range(4):
            for j in range(n):
                _remote(ins[j].at[0], landing[j].at[k], send_sems.at[k, j], recv_sems.at[k, j], sibling).wait_recv()
                total = ins[j][_slot(*_chip(me, k), c)].astype(f32) + landing[j][k].astype(f32)
                if k == 0:
                    own[j][...] = total.astype(own[j].dtype)
                else:
                    others[j][k - 1] = total.astype(others[j].dtype)
        for cp in sends:
            cp.wait_send()

    vm = pl.BlockSpec(memory_space=pltpu.VMEM)
    return pl.pallas_call(
        body, name=name,
        out_shape=tuple(jax.ShapeDtypeStruct(p.shape[1:], p.dtype) for p in parts)
        + tuple(jax.ShapeDtypeStruct((3,) + p.shape[1:], p.dtype) for p in parts),
        in_specs=[vm] * n, out_specs=(vm,) * (2 * n),
        scratch_shapes=[pltpu.VMEM((4,) + p.shape[1:], p.dtype) for p in parts]
        + [pltpu.SemaphoreType.DMA((4, n)), pltpu.SemaphoreType.DMA((4, n))],
        compiler_params=pltpu.CompilerParams(vmem_limit_bytes=V7X_VMEM_LIMIT),
    )(*parts)


def _chip_exchange_sems(n):
    return [pltpu.SemaphoreType.DMA((3, n)), pltpu.SemaphoreType.DMA((3, n))] if n else []


def _chip_exchange_copy(k, j, ein, eout, send_sems, recv_sems):
    me = _me()
    return _remote(ein[j].at[k - 1], eout[j].at[k - 1], send_sems.at[k - 1, j], recv_sems.at[k - 1, j],
                   (*_chip(me, k), me[2]))


def _chip_exchange_start(ein, eout, send_sems, recv_sems):
    for k in range(1, 4):
        for j in range(len(ein)):
            _chip_exchange_copy(k, j, ein, eout, send_sems, recv_sems).start()


def _chip_exchange_finish(ein, eout, send_sems, recv_sems):
    for k in range(1, 4):
        for j in range(len(ein)):
            _chip_exchange_copy(k, j, ein, eout, send_sems, recv_sems).wait_recv()
    for k in range(1, 4):
        for j in range(len(ein)):
            _chip_exchange_copy(k, j, ein, eout, send_sems, recv_sems).wait_send()


def _chip_exchange(others, name):
    n = len(others)

    def body(*refs):
        ein, eout, sems = refs[:n], refs[n:2 * n], refs[2 * n:]
        _chip_exchange_start(ein, eout, *sems)
        _chip_exchange_finish(ein, eout, *sems)

    hbm = pl.BlockSpec(memory_space=pltpu.HBM)
    return pl.pallas_call(
        body, name=name, out_shape=tuple(jax.ShapeDtypeStruct(e.shape, e.dtype) for e in others),
        in_specs=[hbm] * n, out_specs=(hbm,) * n, scratch_shapes=_chip_exchange_sems(n),
    )(*others)


def _sum_parts(owns, arrived, name):
    n = len(owns)

    def body(*refs):
        for own, arr, out in zip(refs[:n], refs[n:2 * n], refs[2 * n:]):
            acc = own[...].astype(f32)
            for k in range(3):
                acc = acc + arr[k].astype(f32)
            out[...] = acc

    vm = pl.BlockSpec(memory_space=pltpu.VMEM)
    return pl.pallas_call(
        body, name=name, out_shape=tuple(jax.ShapeDtypeStruct(o.shape, f32) for o in owns),
        in_specs=[vm] * (2 * n), out_specs=(vm,) * n,
        compiler_params=pltpu.CompilerParams(vmem_limit_bytes=V7X_VMEM_LIMIT),
    )(*owns, *arrived)


ADAM_C1 = 1.0 / (1.0 - ADAM_B1 ** ADAM_STEP)
ADAM_C2 = 1.0 / (1.0 - ADAM_B2 ** ADAM_STEP)


def _adam_update(w, g, m, v):
    m = ADAM_B1 * m + (1.0 - ADAM_B1) * g
    v = ADAM_B2 * v + (1.0 - ADAM_B2) * (g * g)
    return -ADAM_LR * ((m * ADAM_C1) / (jnp.sqrt(v * ADAM_C2) + ADAM_EPS) + ADAM_WD * w), m, v


def _sum_adamw(own, arrived, w, m, v, name, steps):
    rows = own.shape[0]
    br = rows // steps

    def body(own_ref, arr_ref, w_ref, m_ref, v_ref, g_out, d_out, m_out, v_out):
        g = own_ref[...].astype(f32)
        for k in range(3):
            g = g + arr_ref[k].astype(f32)
        g_out[...] = g
        d_out[...], m_out[...], v_out[...] = _adam_update(w_ref[...], g, m_ref[...], v_ref[...])

    blk = pl.BlockSpec((br, D), lambda i: (i, 0))
    return pl.pallas_call(
        body, name=name, grid=(steps,), out_shape=(jax.ShapeDtypeStruct((rows, D), f32),) * 4,
        in_specs=[blk, pl.BlockSpec((3, br, D), lambda i: (0, i, 0)), blk, blk, blk], out_specs=(blk,) * 4,
        compiler_params=pltpu.CompilerParams(dimension_semantics=("parallel",), vmem_limit_bytes=V7X_VMEM_LIMIT),
    )(own, arrived, w, m, v)


def _adamw(ws, gs, ms, vs, name):
    n = len(ws)

    def body(*refs):
        w_r, g_r, m_r, v_r = (refs[k * n:(k + 1) * n] for k in range(4))
        d_o, m_o, v_o = (refs[(4 + k) * n:(5 + k) * n] for k in range(3))
        for j in range(n):
            d_o[j][...], m_o[j][...], v_o[j][...] = _adam_update(w_r[j][...], g_r[j][...], m_r[j][...], v_r[j][...])

    vm = pl.BlockSpec(memory_space=pltpu.VMEM)
    shapes = tuple(jax.ShapeDtypeStruct(w.shape, f32) for w in ws)
    return pl.pallas_call(
        body, name=name, out_shape=shapes * 3, in_specs=[vm] * (4 * n), out_specs=tuple([vm] * (3 * n)),
        compiler_params=pltpu.CompilerParams(vmem_limit_bytes=V7X_VMEM_LIMIT),
    )(*ws, *gs, *ms, *vs)


SMALL = (("w_pool", GROUPS * DH * DH), ("pool_scale", PW), ("ln1_g", D), ("ln1_b", D), ("conv_b", D_FF),
         ("ln2_g", D), ("ln2_b", D), ("conv_w", 3 * D_FF), ("loss", 1))
SMALL_ROWS = 640


def _pack(named):
    flat = jnp.concatenate([named[k].reshape(-1) for k, _ in SMALL])
    return jnp.pad(flat, (0, SMALL_ROWS * 128 - flat.shape[0])).reshape(SMALL_ROWS, 128)


def _unpack(packed):
    flat, out, at = packed.reshape(-1), {}, 0
    for k, size in SMALL:
        out[k] = flat[at:at + size]
        at += size
    return out


def kernel(x, w_in, w_pool, pool_scale, w_out, ln1_g, ln1_b, w_up, conv_w, conv_b, w_down, ln2_g, ln2_b, loss_target, m_w_in, m_w_pool, m_pool_scale, m_w_out, m_ln1_g, m_ln1_b, m_w_up, m_conv_w, m_conv_b, m_w_down, m_ln2_g, m_ln2_b, v_w_in, v_w_pool, v_pool_scale, v_w_out, v_ln1_g, v_ln1_b, v_w_up, v_conv_w, v_conv_b, v_w_down, v_ln2_g, v_ln2_b):
    me = 4 * lax.axis_index("x") + 2 * lax.axis_index("y") + lax.axis_index("c")
    x2, tgt = x[0], loss_target[0]

    g_in, g_out, g_cw = _all_gather([w_in[0].T.astype(bf16), w_out[0].astype(bf16), conv_w[0]], "gather_weights")
    w_in_t = g_in.reshape(IN_W, D)
    w_out_f = g_out.reshape(D, D)
    conv_w_f = jnp.transpose(g_cw, (1, 0, 2)).reshape(3, D_FF)
    w_pool_b = w_pool[0].astype(bf16)

    cos, sin = _rope_tables()
    dmat, qd, kd, cdec = _decay_tables(MIX_TILE)

    qkv, g, oret, states, cat, pooled, xhat1, rstd1, x1b, g_up, g_down = _mix_forward(
        x2, w_in_t, cos, sin, dmat, qd, kd, cdec, w_pool_b, pool_scale, w_out_f, ln1_g, ln1_b,
        gather=[w_up[0].T.astype(bf16), w_down[0].astype(bf16)])
    w_up_t = g_up.reshape(2 * D_FF, D)
    w_down_f = g_down.reshape(D_FF, D)
    dz1, dz2b, du, f, loss8, d_ln2_g, d_ln2_b, d_ln1_g, d_ln1_b, d_conv_b, d_conv_w = _ffn_forward_backward(
        xhat1, rstd1, ln1_g, ln1_b, w_up_t, conv_w_f, conv_b, w_down_f, ln2_g, ln2_b, tgt)

    (dw_down,) = _weight_grad(f, dz2b, "grad_w_down", tm=256)
    own_down, oth_down = _pair_reduce([dw_down.reshape(N_DEV, ROWS_DOWN, D)], "pair_reduce_down")
    dw_up_t, arr_down = _weight_grad(du, x1b, "grad_w_up", tm=512, exchange=[oth_down])
    own_up, oth_up = _pair_reduce([dw_up_t.reshape(N_DEV, ROWS_UP, D)], "pair_reduce_up")
    dproj, grad_x, d_w_pool, d_pool_scale, arr_up = _mix_backward(
        dz1, w_out_f, qkv, g, oret, states, pooled, cos, sin, dmat, qd, kd, cdec, w_pool_b, pool_scale, w_in_t,
        exchange=[oth_up])
    (dw_out,) = _weight_grad(cat, dz1, "grad_w_out", tm=512)
    small = _pack({"w_pool": d_w_pool, "pool_scale": d_pool_scale, "ln1_g": d_ln1_g, "ln1_b": d_ln1_b,
                   "conv_b": d_conv_b, "ln2_g": d_ln2_g, "ln2_b": d_ln2_b, "conv_w": d_conv_w, "loss": loss8[0, :1]})
    own_out, own_small, oth_out, oth_small = _pair_reduce(
        [dw_out.reshape(N_DEV, ROWS_OUT, D), small.reshape(N_DEV, SMALL_ROWS // N_DEV, 128)], "pair_reduce_out")
    dw_in_t, arr_out, arr_small = _weight_grad(dproj, x2, "grad_w_in", tm=512, exchange=[oth_out, oth_small])
    own_in, oth_in = _pair_reduce([dw_in_t.reshape(N_DEV, ROWS_IN, D)], "pair_reduce_in")
    (arr_in,) = _chip_exchange([oth_in], "exchange_in")

    names = ["w_in", "w_pool", "pool_scale", "w_out", "ln1_g", "ln1_b", "w_up", "conv_w", "conv_b", "w_down",
             "ln2_g", "ln2_b"]
    w_d = dict(w_in=w_in, w_pool=w_pool, pool_scale=pool_scale, w_out=w_out, ln1_g=ln1_g, ln1_b=ln1_b, w_up=w_up,
               conv_w=conv_w, conv_b=conv_b, w_down=w_down, ln2_g=ln2_g, ln2_b=ln2_b)
    m_d = dict(w_in=m_w_in, w_pool=m_w_pool, pool_scale=m_pool_scale, w_out=m_w_out, ln1_g=m_ln1_g, ln1_b=m_ln1_b,
               w_up=m_w_up, conv_w=m_conv_w, conv_b=m_conv_b, w_down=m_w_down, ln2_g=m_ln2_g, ln2_b=m_ln2_b)
    v_d = dict(w_in=v_w_in, w_pool=v_w_pool, pool_scale=v_pool_scale, w_out=v_w_out, ln1_g=v_ln1_g, ln1_b=v_ln1_b,
               w_up=v_w_up, conv_w=v_conv_w, conv_b=v_conv_b, w_down=v_w_down, ln2_g=v_ln2_g, ln2_b=v_ln2_b)
    g_d, delta, new_m, new_v = {}, {}, {}, {}

    big = (("w_in", own_in, arr_in, True, 4), ("w_out", own_out, arr_out, False, 2),
           ("w_up", own_up, arr_up, True, 4), ("w_down", own_down, arr_down, False, 2))
    for k, own, arr, transposed, steps in big:
        lay = (lambda a: a[0].T) if transposed else (lambda a: a[0])
        back = (lambda a: a.T[None]) if transposed else (lambda a: a[None])
        res = _sum_adamw(own, arr, lay(w_d[k]), lay(m_d[k]), lay(v_d[k]), "adamw_" + k, steps)
        g_d[k], delta[k], new_m[k], new_v[k] = (back(r) for r in res)

    (small_piece,) = _sum_parts([own_small], [arr_small], "sum_small_grads")
    (gs_small,) = _all_gather([small_piece], "gather_small_grads")
    gsm = _unpack(gs_small)
    gsm["conv_w"] = lax.dynamic_slice(gsm["conv_w"].reshape(3, D_FF), (0, me * (D_FF // N_DEV)), (3, D_FF // N_DEV))
    two_d = lambda a: a.reshape(-1, a.shape[-1])
    group = [k for k in names if k not in g_d]
    for k in group:
        g_d[k] = gsm[k].reshape(w_d[k].shape)
    res = _adamw([two_d(w_d[k]) for k in group], [two_d(g_d[k]) for k in group], [two_d(m_d[k]) for k in group],
                 [two_d(v_d[k]) for k in group], "adamw_small")
    for j, k in enumerate(group):
        delta[k] = res[j].reshape(w_d[k].shape)
        new_m[k] = res[len(group) + j].reshape(w_d[k].shape)
        new_v[k] = res[2 * len(group) + j].reshape(w_d[k].shape)

    loss = gsm["loss"].reshape(())
    return (loss, grad_x[None], *[g_d[k] for k in names], *[delta[k] for k in names], *[new_m[k] for k in names],
            *[new_v[k] for k in names])
```

```python
import functools
import math

import numpy as np
import jax
import jax.numpy as jnp
from jax import lax
from jax.experimental import pallas as pl
from jax.experimental.pallas import tpu as pltpu

f32 = jnp.float32
bf16 = jnp.bfloat16

N_DEV = 8
T = 4096
D = 1024
CHUNK = 64
MIX_TILE = 512
HEADS = 4
DH = 128
RW = HEADS * DH
PW = 512
GROUPS = 4
WINDOWS = (2, 4, 8, 16)
IN_W = 4 * RW + PW
D_FF = 2816
LN_EPS = 1e-5
RMS_EPS = 1e-6
ALPHA = 2.0 ** 0.25
K_SCALE = DH ** -0.5

ADAM_LR = 0.001
ADAM_B1 = 0.9
ADAM_B2 = 0.999
ADAM_EPS = 1e-08
ADAM_WD = 0.01
ADAM_STEP = 10

ROWS_IN, ROWS_OUT, ROWS_UP, ROWS_DOWN = IN_W // N_DEV, D // N_DEV, 2 * D_FF // N_DEV, D_FF // N_DEV

V7X_VMEM_LIMIT = 56 * 2 ** 20
HALO = 32

NT = (((1,), (1,)), ((), ()))
TN = (((0,), (0,)), ((), ()))
NN = (((1,), (0,)), ((), ()))


def _dot(a, b, dims=NN):
    return lax.dot_general(a, b, dims, preferred_element_type=f32)


def _const_spec(shape):
    zeros = (0,) * len(shape)
    return pl.BlockSpec(shape, lambda i: zeros, pipeline_mode=pl.Buffered(1))


def _sigmoid(x):
    return 0.5 * jnp.tanh(0.5 * x) + 0.5


def _decay_tables(tt):
    h = np.arange(HEADS, dtype=np.float64)
    log_gamma = np.log(1.0 - 2.0 ** (-5.0 - h)).astype(np.float32).astype(np.float64)[:, None, None]
    idx = np.arange(tt, dtype=np.float64)
    visible = (idx[None, :] // CHUNK) <= (idx[:, None] // CHUNK)
    mask = np.where(visible[None], np.exp(log_gamma * np.abs(idx[:, None] - idx[None, :])[None]), 0.0)
    qd = np.broadcast_to(np.exp(log_gamma * (idx[None, :, None] + 1.0)), (HEADS, tt, DH))
    kd = np.broadcast_to(np.exp(log_gamma * (tt - 1.0 - idx[None, :, None])), (HEADS, tt, DH))
    cd = np.exp(log_gamma[:, 0, 0] * tt)
    return (jnp.asarray(mask, f32), jnp.asarray(qd, f32), jnp.asarray(kd, f32), [float(c) for c in cd])


def _rope_tables():
    inv_freq = (10000.0 ** (-np.arange(0, DH, 2, dtype=np.float64) / DH)).astype(np.float32)
    ang = (np.arange(T, dtype=np.float32)[:, None] * inv_freq[None, :]).astype(np.float64)
    cos, sin = np.cos(ang), np.sin(ang)
    return (jnp.asarray(np.concatenate([cos, cos], axis=1), f32), jnp.asarray(np.concatenate([-sin, sin], axis=1), f32))


def _swap_halves(t):
    return pltpu.roll(t, DH // 2, axis=1)


def _mix_forward(x, w_in_t, cos, sin, dmat, qd, kd, cdec, w_pool, pool_scale, w_out, ln1_g, ln1_b, gather,
                 tt=MIX_TILE):
    n_tiles = T // tt
    n_g = len(gather)

    def body(x_ref, wint_ref, cos_ref, sin_ref, dmat_ref, qd_ref, kd_ref, wpool_ref, pscale_ref, wout_ref,
             g1_ref, b1_ref, *rest):
        gin, rest = rest[:n_g], rest[n_g:]
        qkv_ref, g_ref, oret_ref, states_ref, cat_ref, pooled_ref, xhat_ref, rstd_ref, x1b_ref = rest[:9]
        gout, (state_s, pext_s, tmp_s, *sems) = rest[9:9 + n_g], rest[9 + n_g:]
        i = pl.program_id(0)

        @pl.when(i == 0)
        def _():
            state_s[...] = jnp.zeros_like(state_s)
            pext_s[pl.ds(0, HALO), :] = jnp.zeros((HALO, PW), f32)
            _gather_start(gin, gout, *sems)

        @pl.when(i == n_tiles - 2)
        def _():
            _gather_forward(gin, gout, *sems)

        xb = x_ref[...].astype(bf16)
        cos_t, sin_t = cos_ref[...], sin_ref[...]
        for part in range(2):
            pr = _dot(xb, wint_ref[pl.ds(part * RW, RW), :], NT)
            for h in range(HEADS):
                t = pr[:, h * DH:(h + 1) * DH]
                r = t * cos_t + _swap_halves(t) * sin_t
                if part == 1:
                    r = r * K_SCALE
                qkv_ref[:, part * RW + h * DH: part * RW + (h + 1) * DH] = r.astype(bf16)
        qkv_ref[:, 2 * RW:3 * RW] = _dot(xb, wint_ref[pl.ds(2 * RW, RW), :], NT).astype(bf16)
        g_ref[...] = _dot(xb, wint_ref[pl.ds(3 * RW, RW), :], NT)
        pext_s[pl.ds(HALO, tt), :] = _dot(xb, wint_ref[pl.ds(4 * RW, PW), :], NT)

        for h in range(HEADS):
            q = qkv_ref[:, h * DH:(h + 1) * DH]
            k = qkv_ref[:, RW + h * DH: RW + (h + 1) * DH]
            v = qkv_ref[:, 2 * RW + h * DH: 2 * RW + (h + 1) * DH]
            s = _dot(q, k, NT) * dmat_ref[h]
            st = state_s[h]
            stb = st.astype(bf16)
            states_ref[0, h] = stb
            oret_ref[:, h * DH:(h + 1) * DH] = (_dot(s.astype(bf16), v)
                                               + _dot((q.astype(f32) * qd_ref[h]).astype(bf16), stb))
            state_s[h] = st * cdec[h] + _dot((k.astype(f32) * kd_ref[h]).astype(bf16), v, TN)

        for h in range(HEADS):
            sl = slice(h * DH, (h + 1) * DH)
            o = oret_ref[:, sl]
            r = lax.rsqrt(jnp.mean(o * o, axis=-1, keepdims=True) + RMS_EPS)
            gg = g_ref[:, sl]
            cat_ref[:, sl] = (o * r * (gg * _sigmoid(gg))).astype(bf16)

        pos1 = (i * tt + lax.broadcasted_iota(jnp.int32, (tt, 1), 0) + 1).astype(f32)
        for gi, w in enumerate(WINDOWS):
            sl = slice(gi * DH, (gi + 1) * DH)
            stages = int(math.log2(w))
            src = pext_s
            for s in range(stages):
                lo = HALO - 8 * (stages - 1 - s)
                n = tt + HALO - lo
                shift = 2 ** s
                val = src[pl.ds(lo, n), sl] + src[pl.ds(lo - shift, n), sl]
                if s == stages - 1:
                    wsum = val
                else:
                    tmp_s[pl.ds(lo, n), sl] = val
                    src = tmp_s
            p_g = pext_s[pl.ds(HALO, tt), sl]
            pooled = (wsum / jnp.minimum(pos1, float(w)) - p_g).astype(bf16)
            pooled_ref[:, sl] = pooled
            y = _dot(pooled, wpool_ref[gi]) * pscale_ref[:, sl]
            cat_ref[:, RW + gi * DH: RW + (gi + 1) * DH] = y.astype(bf16)
        pext_s[pl.ds(0, HALO), :] = pext_s[pl.ds(tt, HALO), :]

        z = ALPHA * x_ref[...] + _dot(cat_ref[...], wout_ref[...])
        mu = jnp.mean(z, axis=-1, keepdims=True)
        zc = z - mu
        rstd = lax.rsqrt(jnp.mean(zc * zc, axis=-1, keepdims=True) + LN_EPS)
        xhat = zc * rstd
        xhat_ref[...] = xhat
        rstd_ref[...] = rstd
        x1b_ref[...] = (xhat * g1_ref[...] + b1_ref[...]).astype(bf16)

        @pl.when(i == n_tiles - 1)
        def _():
            _gather_finish(gin, gout, *sems)

    tile = lambda w: pl.BlockSpec((tt, w), lambda i: (i, 0))
    hbm = pl.BlockSpec(memory_space=pltpu.HBM)
    out_shape = (
        jax.ShapeDtypeStruct((T, 3 * RW), bf16),
        jax.ShapeDtypeStruct((T, RW), f32),
        jax.ShapeDtypeStruct((T, RW), f32),
        jax.ShapeDtypeStruct((n_tiles, HEADS, DH, DH), bf16),
        jax.ShapeDtypeStruct((T, D), bf16),
        jax.ShapeDtypeStruct((T, PW), bf16),
        jax.ShapeDtypeStruct((T, D), f32),
        jax.ShapeDtypeStruct((T, 1), f32),
        jax.ShapeDtypeStruct((T, D), bf16),
    ) + tuple(jax.ShapeDtypeStruct((N_DEV,) + b.shape, b.dtype) for b in gather)
    return pl.pallas_call(
        body, name="mix_forward", grid=(n_tiles,), out_shape=out_shape,
        in_specs=[tile(D), _const_spec((IN_W, D)), tile(DH), tile(DH),
                  _const_spec((HEADS, tt, tt)), _const_spec((HEADS, tt, DH)), _const_spec((HEADS, tt, DH)),
                  _const_spec((GROUPS, DH, DH)), _const_spec((1, PW)), _const_spec((D, D)),
                  _const_spec((1, D)), _const_spec((1, D))] + [hbm] * n_g,
        out_specs=(tile(3 * RW), tile(RW), tile(RW),
                   pl.BlockSpec((1, HEADS, DH, DH), lambda i: (i, 0, 0, 0)),
                   tile(D), tile(PW), tile(D), tile(1), tile(D)) + (hbm,) * n_g,
        scratch_shapes=[pltpu.VMEM((HEADS, DH, DH), f32), pltpu.VMEM((tt + HALO, PW), f32),
                        pltpu.VMEM((tt + HALO, PW), f32)] + _gather_sems(n_g),
        compiler_params=pltpu.CompilerParams(dimension_semantics=("arbitrary",), vmem_limit_bytes=V7X_VMEM_LIMIT),
    )(x, w_in_t, cos, sin, dmat, qd, kd, w_pool, pool_scale, w_out, ln1_g, ln1_b, *gather)


def _ffn_forward_backward(xhat1, rstd1, ln1_g, ln1_b, w_up_t, conv_w, conv_b, w_down, ln2_g, ln2_b, target,
                          tt=256, ch=256, kg=4):
    n_tiles = T // tt
    n_ch = D_FF // ch
    per = ch // 128
    FH = 16
    hb = tt // FH

    def body(xhat_ref, halo_ref, rstd_ref, g1_ref, b1_ref, wupt_ref, cw_ref, cb_ref, wdown_ref, g2_ref, b2_ref, tgt_ref,
             dz1_ref, dz2b_ref, du_ref, f_ref, loss_ref, dg2_ref, db2_ref, dg1_ref, db1_ref, dcb_ref, dcw_ref,
             gext_s, val_s, dhext_s):
        i = pl.program_id(0)
        tile_idx = n_tiles - 1 - i

        def rd(ref, off, c):
            return jnp.concatenate([ref[c * per + k, pl.ds(off, tt), :] for k in range(per)], axis=1)

        def wr(ref, off, c, val):
            for k in range(per):
                ref[c * per + k, pl.ds(off, val.shape[0]), :] = val[:, k * 128:(k + 1) * 128]

        @pl.when(i == 0)
        def _():
            for r in (loss_ref, dg2_ref, db2_ref, dg1_ref, db1_ref, dcb_ref, dcw_ref):
                r[...] = jnp.zeros_like(r)
            dhext_s[:, pl.ds(tt, 8), :] = jnp.zeros((D_FF // 128, 8, 128), f32)

        g1, b1 = g1_ref[...], b1_ref[...]
        xhat = xhat_ref[...]
        x1 = xhat * g1 + b1
        x1b = x1.astype(bf16)
        x1h = ((halo_ref[...] * g1 + b1) * jnp.where(tile_idx == 0, 0.0, 1.0)).astype(bf16)
        x1ext = jnp.concatenate([x1h, x1b], axis=0)

        for c in range(n_ch):
            cs = slice(c * ch, (c + 1) * ch)
            val = _dot(x1b, wupt_ref[pl.ds(c * ch, ch), :], NT)
            gate_ext = _dot(x1ext, wupt_ref[pl.ds(D_FF + c * ch, ch), :], NT)
            wr(gext_s, 0, c, gate_ext)
            hh = (cb_ref[:, cs] + cw_ref[0:1, cs] * rd(gext_s, FH - 2, c) + cw_ref[1:2, cs] * rd(gext_s, FH - 1, c)
                  + cw_ref[2:3, cs] * gate_ext[FH:])
            sg = _sigmoid(hh)
            act = hh * sg
            wr(dhext_s, 0, c, act)
            val_s[:, cs] = val * (sg + act * (1.0 - sg))
            f_ref[:, cs] = (act * val).astype(bf16)
            if (c + 1) % kg == 0 or c == n_ch - 1:
                lo, n = (c // kg) * kg * ch, (c % kg + 1) * ch
                part = _dot(f_ref[:, lo:lo + n], wdown_ref[pl.ds(lo, n), :])
                ffn = part if lo == 0 else ffn + part

        z = ALPHA * x1 + ffn
        mu = jnp.mean(z, axis=-1, keepdims=True)
        zc = z - mu
        rstd2 = lax.rsqrt(jnp.mean(zc * zc, axis=-1, keepdims=True) + LN_EPS)
        xh2 = zc * rstd2
        diff = xh2 * g2_ref[...] + b2_ref[...] - tgt_ref[...]
        loss_ref[...] += 0.5 * jnp.sum(diff * diff) / D
        dy = diff * (1.0 / D)
        dg2_ref[...] += jnp.sum(dy * xh2, axis=0, keepdims=True)
        db2_ref[...] += jnp.sum(dy, axis=0, keepdims=True)
        dyg = dy * g2_ref[...]
        dz2 = rstd2 * (dyg - jnp.mean(dyg, axis=-1, keepdims=True) - xh2 * jnp.mean(dyg * xh2, axis=-1, keepdims=True))
        dz2b = dz2.astype(bf16)
        dz2b_ref[...] = dz2b

        ahead = _dot(dz2b, wdown_ref[pl.ds(0, ch), :], NT)
        for c in range(n_ch):
            cs = slice(c * ch, (c + 1) * ch)
            df = ahead
            if c + 1 < n_ch:
                ahead = _dot(dz2b, wdown_ref[pl.ds((c + 1) * ch, ch), :], NT)
            dval = df * rd(dhext_s, 0, c)
            dh = df * val_s[:, cs]
            wr(dhext_s, 0, c, dh)
            dh1, dh2, g0 = rd(dhext_s, 1, c), rd(dhext_s, 2, c), rd(gext_s, FH, c)
            dcb_ref[:, cs] += jnp.sum(dh, axis=0, keepdims=True)
            dcw_ref[0:1, cs] += jnp.sum(dh2 * g0, axis=0, keepdims=True)
            dcw_ref[1:2, cs] += jnp.sum(dh1 * g0, axis=0, keepdims=True)
            dcw_ref[2:3, cs] += jnp.sum(dh * g0, axis=0, keepdims=True)
            dgate = cw_ref[2:3, cs] * dh + cw_ref[1:2, cs] * dh1 + cw_ref[0:1, cs] * dh2
            du_ref[:, cs] = dval.astype(bf16)
            du_ref[:, D_FF + c * ch: D_FF + (c + 1) * ch] = dgate.astype(bf16)
            if (c + 1) % kg == 0 or c == n_ch - 1:
                lo, n = (c // kg) * kg * ch, (c % kg + 1) * ch
                part = (_dot(du_ref[:, lo:lo + n], wupt_ref[pl.ds(lo, n), :])
                        + _dot(du_ref[:, D_FF + lo:D_FF + lo + n], wupt_ref[pl.ds(D_FF + lo, n), :]))
                dx1 = part if lo == 0 else dx1 + part
        dhext_s[:, pl.ds(tt, 8), :] = dhext_s[:, pl.ds(0, 8), :]
        dx1 = dx1 + ALPHA * dz2

        dg1_ref[...] += jnp.sum(dx1 * xhat, axis=0, keepdims=True)
        db1_ref[...] += jnp.sum(dx1, axis=0, keepdims=True)
        dxg = dx1 * g1
        dz1_ref[...] = rstd_ref[...] * (dxg - jnp.mean(dxg, axis=-1, keepdims=True)
                                        - xhat * jnp.mean(dxg * xhat, axis=-1, keepdims=True))

    rtile = lambda w: pl.BlockSpec((tt, w), lambda i: (n_tiles - 1 - i, 0))
    acc = lambda shape: pl.BlockSpec(shape, lambda i: (0, 0))
    out_shape = (
        jax.ShapeDtypeStruct((T, D), f32),
        jax.ShapeDtypeStruct((T, D), bf16),
        jax.ShapeDtypeStruct((T, 2 * D_FF), bf16),
        jax.ShapeDtypeStruct((T, D_FF), bf16),
        jax.ShapeDtypeStruct((8, 128), f32),
        jax.ShapeDtypeStruct((1, D), f32), jax.ShapeDtypeStruct((1, D), f32),
        jax.ShapeDtypeStruct((1, D), f32), jax.ShapeDtypeStruct((1, D), f32),
        jax.ShapeDtypeStruct((1, D_FF), f32), jax.ShapeDtypeStruct((3, D_FF), f32),
    )
    return pl.pallas_call(
        body, name="ffn_forward_backward", grid=(n_tiles,), out_shape=out_shape,
        in_specs=[rtile(D),
                  pl.BlockSpec((FH, D), lambda i: (jnp.maximum((n_tiles - 1 - i) * hb - 1, 0), 0)),
                  rtile(1), _const_spec((1, D)), _const_spec((1, D)), _const_spec((2 * D_FF, D)),
                  _const_spec((3, D_FF)), _const_spec((1, D_FF)), _const_spec((D_FF, D)),
                  _const_spec((1, D)), _const_spec((1, D)), rtile(D)],
        out_specs=(rtile(D), rtile(D), rtile(2 * D_FF), rtile(D_FF), acc((8, 128)),
                   acc((1, D)), acc((1, D)), acc((1, D)), acc((1, D)), acc((1, D_FF)), acc((3, D_FF))),
        scratch_shapes=[pltpu.VMEM((D_FF // 128, tt + FH, 128), f32), pltpu.VMEM((tt, D_FF), f32),
                        pltpu.VMEM((D_FF // 128, tt + 8, 128), f32)],
        compiler_params=pltpu.CompilerParams(dimension_semantics=("arbitrary",), vmem_limit_bytes=V7X_VMEM_LIMIT),
    )(xhat1, xhat1, rstd1, ln1_g, ln1_b, w_up_t, conv_w, conv_b, w_down, ln2_g, ln2_b, target)


def _mix_backward(dz1, w_out, qkv, g, oret, states, pooled, cos, sin, dmat, qd, kd, cdec, w_pool, pool_scale, w_in_t,
                  exchange, tt=MIX_TILE):
    n_tiles = T // tt
    n_e = len(exchange)

    def body(dz1_ref, wout_ref, qkv_ref, g_ref, oret_ref, states_ref, pooled_ref, cos_ref, sin_ref, dmat_ref, qd_ref,
             kd_ref, wpool_ref, pscale_ref, wint_ref, *rest):
        ein, rest = rest[:n_e], rest[n_e:]
        dproj_ref, gx_ref, dwpool_ref, dpscale_ref = rest[:4]
        eout, (dstate_s, dout_s, eext_s, tmp_s, *sems) = rest[4:4 + n_e], rest[4 + n_e:]
        i = pl.program_id(0)
        tile_idx = n_tiles - 1 - i

        @pl.when(i == 0)
        def _():
            dstate_s[...] = jnp.zeros_like(dstate_s)
            dwpool_ref[...] = jnp.zeros_like(dwpool_ref)
            dpscale_ref[...] = jnp.zeros_like(dpscale_ref)
            eext_s[pl.ds(tt, HALO), :] = jnp.zeros((HALO, PW), f32)
            _chip_exchange_start(ein, eout, *sems)

        dz1 = dz1_ref[...]
        dcat = _dot(dz1.astype(bf16), wout_ref[...], NT)

        pos1 = (tile_idx * tt + lax.broadcasted_iota(jnp.int32, (tt, 1), 0) + 1).astype(f32)
        for gi, w in enumerate(WINDOWS):
            sl = slice(gi * DH, (gi + 1) * DH)
            dpo = dcat[:, RW + gi * DH: RW + (gi + 1) * DH]
            pooled_g = pooled_ref[:, sl]
            ylin = _dot(pooled_g, wpool_ref[gi])
            dpscale_ref[:, sl] += jnp.sum(dpo * ylin, axis=0, keepdims=True)
            dpw = (dpo * pscale_ref[:, sl]).astype(bf16)
            dwpool_ref[gi] += _dot(pooled_g, dpw, TN)
            dpooled = _dot(dpw, wpool_ref[gi], NT)
            eext_s[pl.ds(0, tt), sl] = dpooled / jnp.minimum(pos1, float(w))
            stages = int(math.log2(w))
            src = eext_s
            for s in range(stages):
                n = tt + 8 * (stages - 1 - s)
                shift = 2 ** s
                val = src[pl.ds(0, n), sl] + src[pl.ds(shift, n), sl]
                if s == stages - 1:
                    wsum = val
                else:
                    tmp_s[pl.ds(0, n), sl] = val
                    src = tmp_s
            dproj_ref[:, 4 * RW + gi * DH: 4 * RW + (gi + 1) * DH] = (wsum - dpooled).astype(bf16)
        eext_s[pl.ds(tt, HALO), :] = eext_s[pl.ds(0, HALO), :]

        for h in range(HEADS):
            sl = slice(h * DH, (h + 1) * DH)
            dr = dcat[:, sl]
            o = oret_ref[:, sl]
            r = lax.rsqrt(jnp.mean(o * o, axis=-1, keepdims=True) + RMS_EPS)
            rn = o * r
            gg = g_ref[:, sl]
            sg = _sigmoid(gg)
            dproj_ref[:, 3 * RW + h * DH: 3 * RW + (h + 1) * DH] = (dr * rn * (sg * (1.0 + gg * (1.0 - sg)))).astype(bf16)
            drn = dr * (gg * sg)
            dout_s[:, sl] = (r * (drn - rn * jnp.mean(drn * rn, axis=-1, keepdims=True))).astype(bf16)

        cos_t, sin_t = cos_ref[...], sin_ref[...]
        for h in range(HEADS):
            q = qkv_ref[:, h * DH:(h + 1) * DH]
            k = qkv_ref[:, RW + h * DH: RW + (h + 1) * DH]
            v = qkv_ref[:, 2 * RW + h * DH: 2 * RW + (h + 1) * DH]
            do = dout_s[:, h * DH:(h + 1) * DH]
            stb = states_ref[0, h]
            dst = dstate_s[h]
            dstb = dst.astype(bf16)
            sb = (_dot(q, k, NT) * dmat_ref[h]).astype(bf16)
            dsb = (_dot(do, v, NT) * dmat_ref[h]).astype(bf16)
            dq = _dot(dsb, k) + _dot(do, stb, NT) * qd_ref[h]
            dk = _dot(dsb, q, TN) + _dot(v, dstb, NT) * kd_ref[h]
            dv = _dot(sb, do, TN) + _dot((k.astype(f32) * kd_ref[h]).astype(bf16), dstb)
            dstate_s[h] = dst * cdec[h] + _dot((q.astype(f32) * qd_ref[h]).astype(bf16), do, TN)
            dproj_ref[:, h * DH:(h + 1) * DH] = (dq * cos_t - _swap_halves(dq) * sin_t).astype(bf16)
            dproj_ref[:, RW + h * DH: RW + (h + 1) * DH] = ((dk * cos_t - _swap_halves(dk) * sin_t) * K_SCALE).astype(bf16)
            dproj_ref[:, 2 * RW + h * DH: 2 * RW + (h + 1) * DH] = dv.astype(bf16)

        gx_ref[...] = ALPHA * dz1 + _dot(dproj_ref[...], wint_ref[...])

        @pl.when(i == n_tiles - 1)
        def _():
            _chip_exchange_finish(ein, eout, *sems)

    rtile = lambda w: pl.BlockSpec((tt, w), lambda i: (n_tiles - 1 - i, 0))
    hbm = pl.BlockSpec(memory_space=pltpu.HBM)
    out_shape = (
        jax.ShapeDtypeStruct((T, IN_W), bf16),
        jax.ShapeDtypeStruct((T, D), f32),
        jax.ShapeDtypeStruct((GROUPS, DH, DH), f32),
        jax.ShapeDtypeStruct((1, PW), f32),
    ) + tuple(jax.ShapeDtypeStruct(e.shape, e.dtype) for e in exchange)
    return pl.pallas_call(
        body, name="mix_backward", grid=(n_tiles,), out_shape=out_shape,
        in_specs=[rtile(D), _const_spec((D, D)), rtile(3 * RW), rtile(RW), rtile(RW),
                  pl.BlockSpec((1, HEADS, DH, DH), lambda i: (n_tiles - 1 - i, 0, 0, 0)),
                  rtile(PW), rtile(DH), rtile(DH),
                  _const_spec((HEADS, tt, tt)), _const_spec((HEADS, tt, DH)), _const_spec((HEADS, tt, DH)),
                  _const_spec((GROUPS, DH, DH)), _const_spec((1, PW)), _const_spec((IN_W, D))] + [hbm] * n_e,
        out_specs=(rtile(IN_W), rtile(D), pl.BlockSpec((GROUPS, DH, DH), lambda i: (0, 0, 0)),
                   pl.BlockSpec((1, PW), lambda i: (0, 0))) + (hbm,) * n_e,
        scratch_shapes=[pltpu.VMEM((HEADS, DH, DH), f32), pltpu.VMEM((tt, RW), bf16),
                        pltpu.VMEM((tt + HALO, PW), f32), pltpu.VMEM((tt + HALO, PW), f32)] + _chip_exchange_sems(n_e),
        compiler_params=pltpu.CompilerParams(dimension_semantics=("arbitrary",), vmem_limit_bytes=V7X_VMEM_LIMIT),
    )(dz1, w_out, qkv, g, oret, states, pooled, cos, sin, dmat, qd, kd, w_pool, pool_scale, w_in_t, *exchange)


def _weight_grad(a, b, name, tm, exchange=(), tk=2048):
    m = a.shape[1]
    n_m, n_k, n_e = m // tm, T // tk, len(exchange)

    def body(a_ref, b_ref, *rest):
        ein, o_ref, eout, (acc_s, *sems) = rest[:n_e], rest[n_e], rest[n_e + 1:2 * n_e + 1], rest[2 * n_e + 1:]
        i, k = pl.program_id(0), pl.program_id(1)

        if n_e:
            @pl.when((i == 0) & (k == 0))
            def _():
                _chip_exchange_start(ein, eout, *sems)

        @pl.when(k == 0)
        def _():
            acc_s[...] = jnp.zeros_like(acc_s)

        acc_s[...] += _dot(a_ref[...], b_ref[pl.ds(pl.multiple_of(k * tk, tk), tk), :].astype(bf16), TN)

        @pl.when(k == n_k - 1)
        def _():
            o_ref[...] = acc_s[...].astype(bf16)

        if n_e:
            @pl.when((i == n_m - 1) & (k == n_k - 1))
            def _():
                _chip_exchange_finish(ein, eout, *sems)

    hbm = pl.BlockSpec(memory_space=pltpu.HBM)
    return pl.pallas_call(
        body, name=name, grid=(n_m, n_k),
        out_shape=(jax.ShapeDtypeStruct((m, D), bf16),) + tuple(jax.ShapeDtypeStruct(e.shape, e.dtype) for e in exchange),
        in_specs=[pl.BlockSpec((tk, tm), lambda i, k: (k, i)),
                  pl.BlockSpec((T, D), lambda i, k: (0, 0), pipeline_mode=pl.Buffered(1))] + [hbm] * n_e,
        out_specs=(pl.BlockSpec((tm, D), lambda i, k: (i, 0)),) + (hbm,) * n_e,
        scratch_shapes=[pltpu.VMEM((tm, D), f32)] + _chip_exchange_sems(n_e),
        compiler_params=pltpu.CompilerParams(dimension_semantics=("arbitrary", "arbitrary"),
                                             vmem_limit_bytes=V7X_VMEM_LIMIT),
    )(a, b, *exchange)


CHIP_FLIPS = ((1, 0), (0, 1), (1, 1))


def _me():
    return lax.axis_index("x"), lax.axis_index("y"), lax.axis_index("c")


def _chip(me, k):
    x, y, _ = me
    if k == 0:
        return x, y
    fx, fy = CHIP_FLIPS[k - 1]
    return (1 - x if fx else x), (1 - y if fy else y)


def _slot(x, y, c):
    return 4 * x + 2 * y + c


def _remote(src, dst, send_sem, recv_sem, to):
    return pltpu.make_async_remote_copy(src_ref=src, dst_ref=dst, send_sem=send_sem, recv_sem=recv_sem,
                                        device_id=to, device_id_type=pl.DeviceIdType.MESH)


def _gather_sems(n):
    return [pltpu.SemaphoreType.DMA((7, n)), pltpu.SemaphoreType.DMA((7, n)), pltpu.SemaphoreType.DMA((n,))] if n else []


def _gather_copy(k, j, gin, gout, send_sems, recv_sems, sending):
    x, y, c = _me()
    sibling, x_chip, y_chip, d_chip = (x, y, 1 - c), (1 - x, y), (x, 1 - y), (1 - x, 1 - y)
    south = c == 0
    passed_on = (jnp.where(south, 1 - x, x), jnp.where(south, y, 1 - y), c)
    src, to = gin[j], sibling
    if sending:
        block = {0: (x, y, c), 1: (x, y, c), 2: (x, y, c), 3: passed_on, 4: (*x_chip, c), 5: (*y_chip, c), 6: (*d_chip, c)}[k]
        to = {1: (*x_chip, c), 2: (*y_chip, c), 3: (jnp.where(south, x, 1 - x), jnp.where(south, 1 - y, y), c)}.get(k, sibling)
        if k >= 3:
            src = gout[j].at[_slot(*block)]
    else:
        block = {0: sibling, 1: (*x_chip, c), 2: (*y_chip, c), 3: (*d_chip, c), 4: (*x_chip, 1 - c), 5: (*y_chip, 1 - c),
                 6: (*d_chip, 1 - c)}[k]
    return _remote(src, gout[j].at[_slot(*block)], send_sems.at[k, j], recv_sems.at[k, j], to)


def _gather_do(ks, action, gin, gout, send_sems, recv_sems):
    for k in ks:
        for j in range(len(gin)):
            cp = _gather_copy(k, j, gin, gout, send_sems, recv_sems, action != "wait_recv")
            getattr(cp, action)()


def _gather_start(gin, gout, send_sems, recv_sems, local_sems):
    for j in range(len(gin)):
        pltpu.make_async_copy(gin[j], gout[j].at[_slot(*_me())], local_sems.at[j]).start()
    _gather_do((0, 1, 2), "start", gin, gout, send_sems, recv_sems)


def _gather_forward(gin, gout, send_sems, recv_sems, local_sems):
    _gather_do((1, 2), "wait_recv", gin, gout, send_sems, recv_sems)
    _gather_do((3, 4, 5), "start", gin, gout, send_sems, recv_sems)


def _gather_finish(gin, gout, send_sems, recv_sems, local_sems):
    _gather_do((3,), "wait_recv", gin, gout, send_sems, recv_sems)
    _gather_do((6,), "start", gin, gout, send_sems, recv_sems)
    _gather_do((0, 4, 5, 6), "wait_recv", gin, gout, send_sems, recv_sems)
    _gather_do(range(7), "wait_send", gin, gout, send_sems, recv_sems)
    for j in range(len(gin)):
        pltpu.make_async_copy(gin[j], gout[j].at[_slot(*_me())], local_sems.at[j]).wait()


def _all_gather(blocks, name):
    n = len(blocks)

    def body(*refs):
        gin, gout, sems = refs[:n], refs[n:2 * n], refs[2 * n:]
        _gather_start(gin, gout, *sems)
        _gather_forward(gin, gout, *sems)
        _gather_finish(gin, gout, *sems)

    hbm = pl.BlockSpec(memory_space=pltpu.HBM)
    return pl.pallas_call(
        body, name=name,
        out_shape=tuple(jax.ShapeDtypeStruct((N_DEV,) + b.shape, b.dtype) for b in blocks),
        in_specs=[hbm] * n, out_specs=(hbm,) * n, scratch_shapes=_gather_sems(n),
    )(*blocks)


def _pair_reduce(parts, name):
    n = len(parts)

    def body(*refs):
        ins, own, others, landing = (refs[k * n:(k + 1) * n] for k in range(4))
        send_sems, recv_sems = refs[4 * n:]
        me = _me()
        x, y, c = me
        sibling = (x, y, 1 - c)
        sends = []
        for k in range(4):
            for j in range(n):
                cp = _remote(ins[j].at[_slot(*_chip(me, k), 1 - c)], landing[j].at[k], send_sems.at[k, j],
                             recv_sems.at[k, j], sibling)
                cp.start()
                sends.append(cp)
        for k in range(4):
            for j in range(n):
                _remote(ins[j].at[0], landing[j].at[k], send_sems.at[k, j], recv_sems.at[k, j], sibling).wait_recv()
                total = ins[j][_slot(*_chip(me, k), c)].astype(f32) + landing[j][k].astype(f32)
                if k == 0:
                    own[j][...] = total.astype(own[j].dtype)
                else:
                    others[j][k - 1] = total.astype(others[j].dtype)
        for cp in sends:
            cp.wait_send()

    vm = pl.BlockSpec(memory_space=pltpu.VMEM)
    return pl.pallas_call(
        body, name=name,
        out_shape=tuple(jax.ShapeDtypeStruct(p.shape[1:], p.dtype) for p in parts)
        + tuple(jax.ShapeDtypeStruct((3,) + p.shape[1:], p.dtype) for p in parts),
        in_specs=[vm] * n, out_specs=(vm,) * (2 * n),
        scratch_shapes=[pltpu.VMEM((4,) + p.shape[1:], p.dtype) for p in parts]
        + [pltpu.SemaphoreType.DMA((4, n)), pltpu.SemaphoreType.DMA((4, n))],
        compiler_params=pltpu.CompilerParams(vmem_limit_bytes=V7X_VMEM_LIMIT),
    )(*parts)


def _chip_exchange_sems(n):
    return [pltpu.SemaphoreType.DMA((3, n)), pltpu.SemaphoreType.DMA((3, n))] if n else []


def _chip_exchange_copy(k, j, ein, eout, send_sems, recv_sems):
    me = _me()
    return _remote(ein[j].at[k - 1], eout[j].at[k - 1], send_sems.at[k - 1, j], recv_sems.at[k - 1, j],
                   (*_chip(me, k), me[2]))


def _chip_exchange_start(ein, eout, send_sems, recv_sems):
    for k in range(1, 4):
        for j in range(len(ein)):
            _chip_exchange_copy(k, j, ein, eout, send_sems, recv_sems).start()


def _chip_exchange_finish(ein, eout, send_sems, recv_sems):
    for k in range(1, 4):
        for j in range(len(ein)):
            _chip_exchange_copy(k, j, ein, eout, send_sems, recv_sems).wait_recv()
    for k in range(1, 4):
        for j in range(len(ein)):
            _chip_exchange_copy(k, j, ein, eout, send_sems, recv_sems).wait_send()


def _chip_exchange(others, name):
    n = len(others)

    def body(*refs):
        ein, eout, sems = refs[:n], refs[n:2 * n], refs[2 * n:]
        _chip_exchange_start(ein, eout, *sems)
        _chip_exchange_finish(ein, eout, *sems)

    hbm = pl.BlockSpec(memory_space=pltpu.HBM)
    return pl.pallas_call(
        body, name=name, out_shape=tuple(jax.ShapeDtypeStruct(e.shape, e.dtype) for e in others),
        in_specs=[hbm] * n, out_specs=(hbm,) * n, scratch_shapes=_chip_exchange_sems(n),
    )(*others)


def _sum_parts(owns, arrived, name):
    n = len(owns)

    def body(*refs):
        for own, arr, out in zip(refs[:n], refs[n:2 * n], refs[2 * n:]):
            acc = own[...].astype(f32)
            for k in range(3):
                acc = acc + arr[k].astype(f32)
            out[...] = acc

    vm = pl.BlockSpec(memory_space=pltpu.VMEM)
    return pl.pallas_call(
        body, name=name, out_shape=tuple(jax.ShapeDtypeStruct(o.shape, f32) for o in owns),
        in_specs=[vm] * (2 * n), out_specs=(vm,) * n,
        compiler_params=pltpu.CompilerParams(vmem_limit_bytes=V7X_VMEM_LIMIT),
    )(*owns, *arrived)


ADAM_C1 = 1.0 / (1.0 - ADAM_B1 ** ADAM_STEP)
ADAM_C2 = 1.0 / (1.0 - ADAM_B2 ** ADAM_STEP)


def _adam_update(w, g, m, v):
    m = ADAM_B1 * m + (1.0 - ADAM_B1) * g
    v = ADAM_B2 * v + (1.0 - ADAM_B2) * (g * g)
    return -ADAM_LR * ((m * ADAM_C1) / (jnp.sqrt(v * ADAM_C2) + ADAM_EPS) + ADAM_WD * w), m, v


def _sum_adamw(own, arrived, w, m, v, name, steps):
    rows = own.shape[0]
    br = rows // steps

    def body(own_ref, arr_ref, w_ref, m_ref, v_ref, g_out, d_out, m_out, v_out):
        g = own_ref[...].astype(f32)
        for k in range(3):
            g = g + arr_ref[k].astype(f32)
        g_out[...] = g
        d_out[...], m_out[...], v_out[...] = _adam_update(w_ref[...], g, m_ref[...], v_ref[...])

    blk = pl.BlockSpec((br, D), lambda i: (i, 0))
    return pl.pallas_call(
        body, name=name, grid=(steps,), out_shape=(jax.ShapeDtypeStruct((rows, D), f32),) * 4,
        in_specs=[blk, pl.BlockSpec((3, br, D), lambda i: (0, i, 0)), blk, blk, blk], out_specs=(blk,) * 4,
        compiler_params=pltpu.CompilerParams(dimension_semantics=("parallel",), vmem_limit_bytes=V7X_VMEM_LIMIT),
    )(own, arrived, w, m, v)


def _adamw(ws, gs, ms, vs, name):
    n = len(ws)

    def body(*refs):
        w_r, g_r, m_r, v_r = (refs[k * n:(k + 1) * n] for k in range(4))
        d_o, m_o, v_o = (refs[(4 + k) * n:(5 + k) * n] for k in range(3))
        for j in range(n):
            d_o[j][...], m_o[j][...], v_o[j][...] = _adam_update(w_r[j][...], g_r[j][...], m_r[j][...], v_r[j][...])

    vm = pl.BlockSpec(memory_space=pltpu.VMEM)
    shapes = tuple(jax.ShapeDtypeStruct(w.shape, f32) for w in ws)
    return pl.pallas_call(
        body, name=name, out_shape=shapes * 3, in_specs=[vm] * (4 * n), out_specs=tuple([vm] * (3 * n)),
        compiler_params=pltpu.CompilerParams(vmem_limit_bytes=V7X_VMEM_LIMIT),
    )(*ws, *gs, *ms, *vs)


SMALL = (("w_pool", GROUPS * DH * DH), ("pool_scale", PW), ("ln1_g", D), ("ln1_b", D), ("conv_b", D_FF),
         ("ln2_g", D), ("ln2_b", D), ("conv_w", 3 * D_FF), ("loss", 1))
SMALL_ROWS = 640


def _pack(named):
    flat = jnp.concatenate([named[k].reshape(-1) for k, _ in SMALL])
    return jnp.pad(flat, (0, SMALL_ROWS * 128 - flat.shape[0])).reshape(SMALL_ROWS, 128)


def _unpack(packed):
    flat, out, at = packed.reshape(-1), {}, 0
    for k, size in SMALL:
        out[k] = flat[at:at + size]
        at += size
    return out


def kernel(x, w_in, w_pool, pool_scale, w_out, ln1_g, ln1_b, w_up, conv_w, conv_b, w_down, ln2_g, ln2_b, loss_target, m_w_in, m_w_pool, m_pool_scale, m_w_out, m_ln1_g, m_ln1_b, m_w_up, m_conv_w, m_conv_b, m_w_down, m_ln2_g, m_ln2_b, v_w_in, v_w_pool, v_pool_scale, v_w_out, v_ln1_g, v_ln1_b, v_w_up, v_conv_w, v_conv_b, v_w_down, v_ln2_g, v_ln2_b):
    me = 4 * lax.axis_index("x") + 2 * lax.axis_index("y") + lax.axis_index("c")
    x2, tgt = x[0], loss_target[0]

    g_in, g_out, g_cw = _all_gather([w_in[0].T.astype(bf16), w_out[0].astype(bf16), conv_w[0]], "gather_weights")
    w_in_t = g_in.reshape(IN_W, D)
    w_out_f = g_out.reshape(D, D)
    conv_w_f = jnp.transpose(g_cw, (1, 0, 2)).reshape(3, D_FF)
    w_pool_b = w_pool[0].astype(bf16)

    cos, sin = _rope_tables()
    dmat, qd, kd, cdec = _decay_tables(MIX_TILE)

    qkv, g, oret, states, cat, pooled, xhat1, rstd1, x1b, g_up, g_down = _mix_forward(
        x2, w_in_t, cos, sin, dmat, qd, kd, cdec, w_pool_b, pool_scale, w_out_f, ln1_g, ln1_b,
        gather=[w_up[0].T.astype(bf16), w_down[0].astype(bf16)])
    w_up_t = g_up.reshape(2 * D_FF, D)
    w_down_f = g_down.reshape(D_FF, D)
    dz1, dz2b, du, f, loss8, d_ln2_g, d_ln2_b, d_ln1_g, d_ln1_b, d_conv_b, d_conv_w = _ffn_forward_backward(
        xhat1, rstd1, ln1_g, ln1_b, w_up_t, conv_w_f, conv_b, w_down_f, ln2_g, ln2_b, tgt)

    (dw_down,) = _weight_grad(f, dz2b, "grad_w_down", tm=256)
    own_down, oth_down = _pair_reduce([dw_down.reshape(N_DEV, ROWS_DOWN, D)], "pair_reduce_down")
    dw_up_t, arr_down = _weight_grad(du, x1b, "grad_w_up", tm=512, exchange=[oth_down])
    own_up, oth_up = _pair_reduce([dw_up_t.reshape(N_DEV, ROWS_UP, D)], "pair_reduce_up")
    dproj, grad_x, d_w_pool, d_pool_scale, arr_up = _mix_backward(
        dz1, w_out_f, qkv, g, oret, states, pooled, cos, sin, dmat, qd, kd, cdec, w_pool_b, pool_scale, w_in_t,
        exchange=[oth_up])
    (dw_out,) = _weight_grad(cat, dz1, "grad_w_out", tm=512)
    small = _pack({"w_pool": d_w_pool, "pool_scale": d_pool_scale, "ln1_g": d_ln1_g, "ln1_b": d_ln1_b,
                   "conv_b": d_conv_b, "ln2_g": d_ln2_g, "ln2_b": d_ln2_b, "conv_w": d_conv_w, "loss": loss8[0, :1]})
    own_out, own_small, oth_out, oth_small = _pair_reduce(
        [dw_out.reshape(N_DEV, ROWS_OUT, D), small.reshape(N_DEV, SMALL_ROWS // N_DEV, 128)], "pair_reduce_out")
    dw_in_t, arr_out, arr_small = _weight_grad(dproj, x2, "grad_w_in", tm=512, exchange=[oth_out, oth_small])
    own_in, oth_in = _pair_reduce([dw_in_t.reshape(N_DEV, ROWS_IN, D)], "pair_reduce_in")
    (arr_in,) = _chip_exchange([oth_in], "exchange_in")

    names = ["w_in", "w_pool", "pool_scale", "w_out", "ln1_g", "ln1_b", "w_up", "conv_w", "conv_b", "w_down",
             "ln2_g", "ln2_b"]
    w_d = dict(w_in=w_in, w_pool=w_pool, pool_scale=pool_scale, w_out=w_out, ln1_g=ln1_g, ln1_b=ln1_b, w_up=w_up,
               conv_w=conv_w, conv_b=conv_b, w_down=w_down, ln2_g=ln2_g, ln2_b=ln2_b)
    m_d = dict(w_in=m_w_in, w_pool=m_w_pool, pool_scale=m_pool_scale, w_out=m_w_out, ln1_g=m_ln1_g, ln1_b=m_ln1_b,
               w_up=m_w_up, conv_w=m_conv_w, conv_b=m_conv_b, w_down=m_w_down, ln2_g=m_ln2_g, ln2_b=m_ln2_b)
    v_d = dict(w_in=v_w_in, w_pool=v_w_pool, pool_scale=v_pool_scale, w_out=v_w_out, ln1_g=v_ln1_g, ln1_b=v_ln1_b,
               w_up=v_w_up, conv_w=v_conv_w, conv_b=v_conv_b, w_down=v_w_down, ln2_g=v_ln2_g, ln2_b=v_ln2_b)
    g_d, delta, new_m, new_v = {}, {}, {}, {}

    big = (("w_in", own_in, arr_in, True, 4), ("w_out", own_out, arr_out, False, 2),
           ("w_up", own_up, arr_up, True, 4), ("w_down", own_down, arr_down, False, 2))
    for k, own, arr, transposed, steps in big:
        lay = (lambda a: a[0].T) if transposed else (lambda a: a[0])
        back = (lambda a: a.T[None]) if transposed else (lambda a: a[None])
        res = _sum_adamw(own, arr, lay(w_d[k]), lay(m_d[k]), lay(v_d[k]), "adamw_" + k, steps)
        g_d[k], delta[k], new_m[k], new_v[k] = (back(r) for r in res)

    (small_piece,) = _sum_parts([own_small], [arr_small], "sum_small_grads")
    (gs_small,) = _all_gather([small_piece], "gather_small_grads")
    gsm = _unpack(gs_small)
    gsm["conv_w"] = lax.dynamic_slice(gsm["conv_w"].reshape(3, D_FF), (0, me * (D_FF // N_DEV)), (3, D_FF // N_DEV))
    two_d = lambda a: a.reshape(-1, a.shape[-1])
    group = [k for k in names if k not in g_d]
    for k in group:
        g_d[k] = gsm[k].reshape(w_d[k].shape)
    res = _adamw([two_d(w_d[k]) for k in group], [two_d(g_d[k]) for k in group], [two_d(m_d[k]) for k in group],
                 [two_d(v_d[k]) for k in group], "adamw_small")
    for j, k in enumerate(group):
        delta[k] = res[j].reshape(w_d[k].shape)
        new_m[k] = res[len(group) + j].reshape(w_d[k].shape)
        new_v[k] = res[2 * len(group) + j].reshape(w_d[k].shape)

    loss = gsm["loss"].reshape(())
    return (loss, grad_x[None], *[g_d[k] for k in names], *[delta[k] for k in names], *[new_m[k] for k in names],
            *[new_v[k] for k in names])
```

```python
import functools
import math

import numpy as np
import jax
import jax.numpy as jnp
from jax import lax
from jax.experimental import pallas as pl
from jax.experimental.pallas import tpu as pltpu

f32 = jnp.float32
bf16 = jnp.bfloat16

N_DEV = 8
T = 4096
D = 1024
CHUNK = 64
MIX_TILE = 512
HEADS = 4
DH = 128
RW = HEADS * DH
PW = 512
GROUPS = 4
WINDOWS = (2, 4, 8, 16)
IN_W = 4 * RW + PW
D_FF = 2816
LN_EPS = 1e-5
RMS_EPS = 1e-6
ALPHA = 2.0 ** 0.25
K_SCALE = DH ** -0.5

ADAM_LR = 0.001
ADAM_B1 = 0.9
ADAM_B2 = 0.999
ADAM_EPS = 1e-08
ADAM_WD = 0.01
ADAM_STEP = 10

ROWS_IN, ROWS_OUT, ROWS_UP, ROWS_DOWN = IN_W // N_DEV, D // N_DEV, 2 * D_FF // N_DEV, D_FF // N_DEV

V7X_VMEM_LIMIT = 56 * 2 ** 20
HALO = 32

NT = (((1,), (1,)), ((), ()))
TN = (((0,), (0,)), ((), ()))
NN = (((1,), (0,)), ((), ()))


def _dot(a, b, dims=NN):
    return lax.dot_general(a, b, dims, preferred_element_type=f32)


def _const_spec(shape):
    zeros = (0,) * len(shape)
    return pl.BlockSpec(shape, lambda i: zeros, pipeline_mode=pl.Buffered(1))


def _sigmoid(x):
    return 0.5 * jnp.tanh(0.5 * x) + 0.5


def _decay_tables(tt):
    h = np.arange(HEADS, dtype=np.float64)
    log_gamma = np.log(1.0 - 2.0 ** (-5.0 - h)).astype(np.float32).astype(np.float64)[:, None, None]
    idx = np.arange(tt, dtype=np.float64)
    visible = (idx[None, :] // CHUNK) <= (idx[:, None] // CHUNK)
    mask = np.where(visible[None], np.exp(log_gamma * np.abs(idx[:, None] - idx[None, :])[None]), 0.0)
    qd = np.broadcast_to(np.exp(log_gamma * (idx[None, :, None] + 1.0)), (HEADS, tt, DH))
    kd = np.broadcast_to(np.exp(log_gamma * (tt - 1.0 - idx[None, :, None])), (HEADS, tt, DH))
    cd = np.exp(log_gamma[:, 0, 0] * tt)
    return (jnp.asarray(mask, f32), jnp.asarray(qd, f32), jnp.asarray(kd, f32), [float(c) for c in cd])


def _rope_tables():
    inv_freq = (10000.0 ** (-np.arange(0, DH, 2, dtype=np.float64) / DH)).astype(np.float32)
    ang = (np.arange(T, dtype=np.float32)[:, None] * inv_freq[None, :]).astype(np.float64)
    cos, sin = np.cos(ang), np.sin(ang)
    return (jnp.asarray(np.concatenate([cos, cos], axis=1), f32), jnp.asarray(np.concatenate([-sin, sin], axis=1), f32))


def _swap_halves(t):
    return pltpu.roll(t, DH // 2, axis=1)


def _mix_forward(x, w_in_t, cos, sin, dmat, qd, kd, cdec, w_pool, pool_scale, w_out, ln1_g, ln1_b, gather,
                 tt=MIX_TILE):
    n_tiles = T // tt
    n_g = len(gather)

    def body(x_ref, wint_ref, cos_ref, sin_ref, dmat_ref, qd_ref, kd_ref, wpool_ref, pscale_ref, wout_ref,
             g1_ref, b1_ref, *rest):
        gin, rest = rest[:n_g], rest[n_g:]
        qkv_ref, g_ref, oret_ref, states_ref, cat_ref, pooled_ref, xhat_ref, rstd_ref, x1b_ref = rest[:9]
        gout, (state_s, pext_s, tmp_s, *sems) = rest[9:9 + n_g], rest[9 + n_g:]
        i = pl.program_id(0)

        @pl.when(i == 0)
        def _():
            state_s[...] = jnp.zeros_like(state_s)
            pext_s[pl.ds(0, HALO), :] = jnp.zeros((HALO, PW), f32)
            _gather_start(gin, gout, *sems)

        @pl.when(i == n_tiles - 2)
        def _():
            _gather_forward(gin, gout, *sems)

        xb = x_ref[...].astype(bf16)
        cos_t, sin_t = cos_ref[...], sin_ref[...]
        for part in range(2):
            pr = _dot(xb, wint_ref[pl.ds(part * RW, RW), :], NT)
            for h in range(HEADS):
                t = pr[:, h * DH:(h + 1) * DH]
                r = t * cos_t + _swap_halves(t) * sin_t
                if part == 1:
                    r = r * K_SCALE
                qkv_ref[:, part * RW + h * DH: part * RW + (h + 1) * DH] = r.astype(bf16)
        qkv_ref[:, 2 * RW:3 * RW] = _dot(xb, wint_ref[pl.ds(2 * RW, RW), :], NT).astype(bf16)
        g_ref[...] = _dot(xb, wint_ref[pl.ds(3 * RW, RW), :], NT)
        pext_s[pl.ds(HALO, tt), :] = _dot(xb, wint_ref[pl.ds(4 * RW, PW), :], NT)

        for h in range(HEADS):
            q = qkv_ref[:, h * DH:(h + 1) * DH]
            k = qkv_ref[:, RW + h * DH: RW + (h + 1) * DH]
            v = qkv_ref[:, 2 * RW + h * DH: 2 * RW + (h + 1) * DH]
            s = _dot(q, k, NT) * dmat_ref[h]
            st = state_s[h]
            stb = st.astype(bf16)
            states_ref[0, h] = stb
            oret_ref[:, h * DH:(h + 1) * DH] = (_dot(s.astype(bf16), v)
                                               + _dot((q.astype(f32) * qd_ref[h]).astype(bf16), stb))
            state_s[h] = st * cdec[h] + _dot((k.astype(f32) * kd_ref[h]).astype(bf16), v, TN)

        for h in range(HEADS):
            sl = slice(h * DH, (h + 1) * DH)
            o = oret_ref[:, sl]
            r = lax.rsqrt(jnp.mean(o * o, axis=-1, keepdims=True) + RMS_EPS)
            gg = g_ref[:, sl]
            cat_ref[:, sl] = (o * r * (gg * _sigmoid(gg))).astype(bf16)

        pos1 = (i * tt + lax.broadcasted_iota(jnp.int32, (tt, 1), 0) + 1).astype(f32)
        for gi, w in enumerate(WINDOWS):
            sl = slice(gi * DH, (gi + 1) * DH)
            stages = int(math.log2(w))
            src = pext_s
            for s in range(stages):
                lo = HALO - 8 * (stages - 1 - s)
                n = tt + HALO - lo
                shift = 2 ** s
                val = src[pl.ds(lo, n), sl] + src[pl.ds(lo - shift, n), sl]
                if s == stages - 1:
                    wsum = val
                else:
                    tmp_s[pl.ds(lo, n), sl] = val
                    src = tmp_s
            p_g = pext_s[pl.ds(HALO, tt), sl]
            pooled = (wsum / jnp.minimum(pos1, float(w)) - p_g).astype(bf16)
            pooled_ref[:, sl] = pooled
            y = _dot(pooled, wpool_ref[gi]) * pscale_ref[:, sl]
            cat_ref[:, RW + gi * DH: RW + (gi + 1) * DH] = y.astype(bf16)
        pext_s[pl.ds(0, HALO), :] = pext_s[pl.ds(tt, HALO), :]

        z = ALPHA * x_ref[...] + _dot(cat_ref[...], wout_ref[...])
        mu = jnp.mean(z, axis=-1, keepdims=True)
        zc = z - mu
        rstd = lax.rsqrt(jnp.mean(zc * zc, axis=-1, keepdims=True) + LN_EPS)
        xhat = zc * rstd
        xhat_ref[...] = xhat
        rstd_ref[...] = rstd
        x1b_ref[...] = (xhat * g1_ref[...] + b1_ref[...]).astype(bf16)

        @pl.when(i == n_tiles - 1)
        def _():
            _gather_finish(gin, gout, *sems)

    tile = lambda w: pl.BlockSpec((tt, w), lambda i: (i, 0))
    hbm = pl.BlockSpec(memory_space=pltpu.HBM)
    out_shape = (
        jax.ShapeDtypeStruct((T, 3 * RW), bf16),
        jax.ShapeDtypeStruct((T, RW), f32),
        jax.ShapeDtypeStruct((T, RW), f32),
        jax.ShapeDtypeStruct((n_tiles, HEADS, DH, DH), bf16),
        jax.ShapeDtypeStruct((T, D), bf16),
        jax.ShapeDtypeStruct((T, PW), bf16),
        jax.ShapeDtypeStruct((T, D), f32),
        jax.ShapeDtypeStruct((T, 1), f32),
        jax.ShapeDtypeStruct((T, D), bf16),
    ) + tuple(jax.ShapeDtypeStruct((N_DEV,) + b.shape, b.dtype) for b in gather)
    return pl.pallas_call(
        body, name="mix_forward", grid=(n_tiles,), out_shape=out_shape,
        in_specs=[tile(D), _const_spec((IN_W, D)), tile(DH), tile(DH),
                  _const_spec((HEADS, tt, tt)), _const_spec((HEADS, tt, DH)), _const_spec((HEADS, tt, DH)),
                  _const_spec((GROUPS, DH, DH)), _const_spec((1, PW)), _const_spec((D, D)),
                  _const_spec((1, D)), _const_spec((1, D))] + [hbm] * n_g,
        out_specs=(tile(3 * RW), tile(RW), tile(RW),
                   pl.BlockSpec((1, HEADS, DH, DH), lambda i: (i, 0, 0, 0)),
                   tile(D), tile(PW), tile(D), tile(1), tile(D)) + (hbm,) * n_g,
        scratch_shapes=[pltpu.VMEM((HEADS, DH, DH), f32), pltpu.VMEM((tt + HALO, PW), f32),
                        pltpu.VMEM((tt + HALO, PW), f32)] + _gather_sems(n_g),
        compiler_params=pltpu.CompilerParams(dimension_semantics=("arbitrary",), vmem_limit_bytes=V7X_VMEM_LIMIT),
    )(x, w_in_t, cos, sin, dmat, qd, kd, w_pool, pool_scale, w_out, ln1_g, ln1_b, *gather)


def _ffn_forward_backward(xhat1, rstd1, ln1_g, ln1_b, w_up_t, conv_w, conv_b, w_down, ln2_g, ln2_b, target,
                          tt=256, widths=(512, 512, 512, 512, 512, 256)):
    n_tiles = T // tt
    assert sum(widths) == D_FF and all(w % 128 == 0 for w in widths)
    chunks = [(sum(widths[:c]), w) for c, w in enumerate(widths)]
    FH = 16
    hb = tt // FH

    def body(xhat_ref, halo_ref, rstd_ref, g1_ref, b1_ref, wupt_ref, cw_ref, cb_ref, wdown_ref, g2_ref, b2_ref, tgt_ref,
             dz1_ref, dz2b_ref, du_ref, f_ref, loss_ref, dg2_ref, db2_ref, dg1_ref, db1_ref, dcb_ref, dcw_ref,
             gext_s, val_s, dhext_s):
        i = pl.program_id(0)
        tile_idx = n_tiles - 1 - i

        def rd(ref, off, lo, w):
            return jnp.concatenate([ref[lo // 128 + k, pl.ds(off, tt), :] for k in range(w // 128)], axis=1)

        def wr(ref, lo, val):
            for k in range(val.shape[1] // 128):
                ref[lo // 128 + k, pl.ds(0, val.shape[0]), :] = val[:, k * 128:(k + 1) * 128]

        @pl.when(i == 0)
        def _():
            for r in (loss_ref, dg2_ref, db2_ref, dg1_ref, db1_ref, dcb_ref, dcw_ref):
                r[...] = jnp.zeros_like(r)
            dhext_s[:, pl.ds(tt, 8), :] = jnp.zeros((D_FF // 128, 8, 128), f32)

        g1, b1 = g1_ref[...], b1_ref[...]
        xhat = xhat_ref[...]
        x1 = xhat * g1 + b1
        x1b = x1.astype(bf16)
        x1h = ((halo_ref[...] * g1 + b1) * jnp.where(tile_idx == 0, 0.0, 1.0)).astype(bf16)
        x1ext = jnp.concatenate([x1h, x1b], axis=0)

        for lo, w in chunks:
            cs = slice(lo, lo + w)
            val = _dot(x1b, wupt_ref[pl.ds(lo, w), :], NT)
            gate_ext = _dot(x1ext, wupt_ref[pl.ds(D_FF + lo, w), :], NT)
            wr(gext_s, lo, gate_ext)
            hh = (cb_ref[:, cs] + cw_ref[0:1, cs] * rd(gext_s, FH - 2, lo, w) + cw_ref[1:2, cs] * rd(gext_s, FH - 1, lo, w)
                  + cw_ref[2:3, cs] * gate_ext[FH:])
            sg = _sigmoid(hh)
            act = hh * sg
            wr(dhext_s, lo, act)
            val_s[:, cs] = val * (sg + act * (1.0 - sg))
            f_ref[:, cs] = (act * val).astype(bf16)

        z = ALPHA * x1 + _dot(f_ref[...], wdown_ref[...])
        mu = jnp.mean(z, axis=-1, keepdims=True)
        zc = z - mu
        rstd2 = lax.rsqrt(jnp.mean(zc * zc, axis=-1, keepdims=True) + LN_EPS)
        xh2 = zc * rstd2
        diff = xh2 * g2_ref[...] + b2_ref[...] - tgt_ref[...]
        loss_ref[...] += 0.5 * jnp.sum(diff * diff) / D
        dy = diff * (1.0 / D)
        dg2_ref[...] += jnp.sum(dy * xh2, axis=0, keepdims=True)
        db2_ref[...] += jnp.sum(dy, axis=0, keepdims=True)
        dyg = dy * g2_ref[...]
        dz2 = rstd2 * (dyg - jnp.mean(dyg, axis=-1, keepdims=True) - xh2 * jnp.mean(dyg * xh2, axis=-1, keepdims=True))
        dz2b = dz2.astype(bf16)
        dz2b_ref[...] = dz2b

        for lo, w in chunks:
            cs = slice(lo, lo + w)
            df = _dot(dz2b, wdown_ref[pl.ds(lo, w), :], NT)
            dval = df * rd(dhext_s, 0, lo, w)
            dh = df * val_s[:, cs]
            wr(dhext_s, lo, dh)
            dh1, dh2, g0 = rd(dhext_s, 1, lo, w), rd(dhext_s, 2, lo, w), rd(gext_s, FH, lo, w)
            dcb_ref[:, cs] += jnp.sum(dh, axis=0, keepdims=True)
            dcw_ref[0:1, cs] += jnp.sum(dh2 * g0, axis=0, keepdims=True)
            dcw_ref[1:2, cs] += jnp.sum(dh1 * g0, axis=0, keepdims=True)
            dcw_ref[2:3, cs] += jnp.sum(dh * g0, axis=0, keepdims=True)
            dgate = cw_ref[2:3, cs] * dh + cw_ref[1:2, cs] * dh1 + cw_ref[0:1, cs] * dh2
            du_ref[:, cs] = dval.astype(bf16)
            du_ref[:, D_FF + lo: D_FF + lo + w] = dgate.astype(bf16)
        dhext_s[:, pl.ds(tt, 8), :] = dhext_s[:, pl.ds(0, 8), :]
        dx1 = ALPHA * dz2 + _dot(du_ref[...], wupt_ref[...])

        dg1_ref[...] += jnp.sum(dx1 * xhat, axis=0, keepdims=True)
        db1_ref[...] += jnp.sum(dx1, axis=0, keepdims=True)
        dxg = dx1 * g1
        dz1_ref[...] = rstd_ref[...] * (dxg - jnp.mean(dxg, axis=-1, keepdims=True)
                                        - xhat * jnp.mean(dxg * xhat, axis=-1, keepdims=True))

    rtile = lambda w: pl.BlockSpec((tt, w), lambda i: (n_tiles - 1 - i, 0))
    acc = lambda shape: pl.BlockSpec(shape, lambda i: (0, 0))
    out_shape = (
        jax.ShapeDtypeStruct((T, D), f32),
        jax.ShapeDtypeStruct((T, D), bf16),
        jax.ShapeDtypeStruct((T, 2 * D_FF), bf16),
        jax.ShapeDtypeStruct((T, D_FF), bf16),
        jax.ShapeDtypeStruct((8, 128), f32),
        jax.ShapeDtypeStruct((1, D), f32), jax.ShapeDtypeStruct((1, D), f32),
        jax.ShapeDtypeStruct((1, D), f32), jax.ShapeDtypeStruct((1, D), f32),
        jax.ShapeDtypeStruct((1, D_FF), f32), jax.ShapeDtypeStruct((3, D_FF), f32),
    )
    return pl.pallas_call(
        body, name="ffn_forward_backward", grid=(n_tiles,), out_shape=out_shape,
        in_specs=[rtile(D),
                  pl.BlockSpec((FH, D), lambda i: (jnp.maximum((n_tiles - 1 - i) * hb - 1, 0), 0)),
                  rtile(1), _const_spec((1, D)), _const_spec((1, D)), _const_spec((2 * D_FF, D)),
                  _const_spec((3, D_FF)), _const_spec((1, D_FF)), _const_spec((D_FF, D)),
                  _const_spec((1, D)), _const_spec((1, D)), rtile(D)],
        out_specs=(rtile(D), rtile(D), rtile(2 * D_FF), rtile(D_FF), acc((8, 128)),
                   acc((1, D)), acc((1, D)), acc((1, D)), acc((1, D)), acc((1, D_FF)), acc((3, D_FF))),
        scratch_shapes=[pltpu.VMEM((D_FF // 128, tt + FH, 128), f32), pltpu.VMEM((tt, D_FF), f32),
                        pltpu.VMEM((D_FF // 128, tt + 8, 128), f32)],
        compiler_params=pltpu.CompilerParams(dimension_semantics=("arbitrary",), vmem_limit_bytes=V7X_VMEM_LIMIT),
    )(xhat1, xhat1, rstd1, ln1_g, ln1_b, w_up_t, conv_w, conv_b, w_down, ln2_g, ln2_b, target)


def _mix_backward(dz1, w_out, qkv, g, oret, states, pooled, cos, sin, dmat, qd, kd, cdec, w_pool, pool_scale, w_in_t,
                  exchange, tt=MIX_TILE):
    n_tiles = T // tt
    n_e = len(exchange)

    def body(dz1_ref, wout_ref, qkv_ref, g_ref, oret_ref, states_ref, pooled_ref, cos_ref, sin_ref, dmat_ref, qd_ref,
             kd_ref, wpool_ref, pscale_ref, wint_ref, *rest):
        ein, rest = rest[:n_e], rest[n_e:]
        dproj_ref, gx_ref, dwpool_ref, dpscale_ref = rest[:4]
        eout, (dstate_s, dout_s, eext_s, tmp_s, *sems) = rest[4:4 + n_e], rest[4 + n_e:]
        i = pl.program_id(0)
        tile_idx = n_tiles - 1 - i

        @pl.when(i == 0)
        def _():
            dstate_s[...] = jnp.zeros_like(dstate_s)
            dwpool_ref[...] = jnp.zeros_like(dwpool_ref)
            dpscale_ref[...] = jnp.zeros_like(dpscale_ref)
            eext_s[pl.ds(tt, HALO), :] = jnp.zeros((HALO, PW), f32)
            _chip_exchange_start(ein, eout, *sems)

        dz1 = dz1_ref[...]
        dcat = _dot(dz1.astype(bf16), wout_ref[...], NT)

        pos1 = (tile_idx * tt + lax.broadcasted_iota(jnp.int32, (tt, 1), 0) + 1).astype(f32)
        for gi, w in enumerate(WINDOWS):
            sl = slice(gi * DH, (gi + 1) * DH)
            dpo = dcat[:, RW + gi * DH: RW + (gi + 1) * DH]
            pooled_g = pooled_ref[:, sl]
            ylin = _dot(pooled_g, wpool_ref[gi])
            dpscale_ref[:, sl] += jnp.sum(dpo * ylin, axis=0, keepdims=True)
            dpw = (dpo * pscale_ref[:, sl]).astype(bf16)
            dwpool_ref[gi] += _dot(pooled_g, dpw, TN)
            dpooled = _dot(dpw, wpool_ref[gi], NT)
            eext_s[pl.ds(0, tt), sl] = dpooled / jnp.minimum(pos1, float(w))
            stages = int(math.log2(w))
            src = eext_s
            for s in range(stages):
                n = tt + 8 * (stages - 1 - s)
                shift = 2 ** s
                val = src[pl.ds(0, n), sl] + src[pl.ds(shift, n), sl]
                if s == stages - 1:
                    wsum = val
                else:
                    tmp_s[pl.ds(0, n), sl] = val
                    src = tmp_s
            dproj_ref[:, 4 * RW + gi * DH: 4 * RW + (gi + 1) * DH] = (wsum - dpooled).astype(bf16)
        eext_s[pl.ds(tt, HALO), :] = eext_s[pl.ds(0, HALO), :]

        for h in range(HEADS):
            sl = slice(h * DH, (h + 1) * DH)
            dr = dcat[:, sl]
            o = oret_ref[:, sl]
            r = lax.rsqrt(jnp.mean(o * o, axis=-1, keepdims=True) + RMS_EPS)
            rn = o * r
            gg = g_ref[:, sl]
            sg = _sigmoid(gg)
            dproj_ref[:, 3 * RW + h * DH: 3 * RW + (h + 1) * DH] = (dr * rn * (sg * (1.0 + gg * (1.0 - sg)))).astype(bf16)
            drn = dr * (gg * sg)
            dout_s[:, sl] = (r * (drn - rn * jnp.mean(drn * rn, axis=-1, keepdims=True))).astype(bf16)

        cos_t, sin_t = cos_ref[...], sin_ref[...]
        for h in range(HEADS):
            q = qkv_ref[:, h * DH:(h + 1) * DH]
            k = qkv_ref[:, RW + h * DH: RW + (h + 1) * DH]
            v = qkv_ref[:, 2 * RW + h * DH: 2 * RW + (h + 1) * DH]
            do = dout_s[:, h * DH:(h + 1) * DH]
            stb = states_ref[0, h]
            dst = dstate_s[h]
            dstb = dst.astype(bf16)
            sb = (_dot(q, k, NT) * dmat_ref[h]).astype(bf16)
            dsb = (_dot(do, v, NT) * dmat_ref[h]).astype(bf16)
            dq = _dot(dsb, k) + _dot(do, stb, NT) * qd_ref[h]
            dk = _dot(dsb, q, TN) + _dot(v, dstb, NT) * kd_ref[h]
            dv = _dot(sb, do, TN) + _dot((k.astype(f32) * kd_ref[h]).astype(bf16), dstb)
            dstate_s[h] = dst * cdec[h] + _dot((q.astype(f32) * qd_ref[h]).astype(bf16), do, TN)
            dproj_ref[:, h * DH:(h + 1) * DH] = (dq * cos_t - _swap_halves(dq) * sin_t).astype(bf16)
            dproj_ref[:, RW + h * DH: RW + (h + 1) * DH] = ((dk * cos_t - _swap_halves(dk) * sin_t) * K_SCALE).astype(bf16)
            dproj_ref[:, 2 * RW + h * DH: 2 * RW + (h + 1) * DH] = dv.astype(bf16)

        gx_ref[...] = ALPHA * dz1 + _dot(dproj_ref[...], wint_ref[...])

        @pl.when(i == n_tiles - 1)
        def _():
            _chip_exchange_finish(ein, eout, *sems)

    rtile = lambda w: pl.BlockSpec((tt, w), lambda i: (n_tiles - 1 - i, 0))
    hbm = pl.BlockSpec(memory_space=pltpu.HBM)
    out_shape = (
        jax.ShapeDtypeStruct((T, IN_W), bf16),
        jax.ShapeDtypeStruct((T, D), f32),
        jax.ShapeDtypeStruct((GROUPS, DH, DH), f32),
        jax.ShapeDtypeStruct((1, PW), f32),
    ) + tuple(jax.ShapeDtypeStruct(e.shape, e.dtype) for e in exchange)
    return pl.pallas_call(
        body, name="mix_backward", grid=(n_tiles,), out_shape=out_shape,
        in_specs=[rtile(D), _const_spec((D, D)), rtile(3 * RW), rtile(RW), rtile(RW),
                  pl.BlockSpec((1, HEADS, DH, DH), lambda i: (n_tiles - 1 - i, 0, 0, 0)),
                  rtile(PW), rtile(DH), rtile(DH),
                  _const_spec((HEADS, tt, tt)), _const_spec((HEADS, tt, DH)), _const_spec((HEADS, tt, DH)),
                  _const_spec((GROUPS, DH, DH)), _const_spec((1, PW)), _const_spec((IN_W, D))] + [hbm] * n_e,
        out_specs=(rtile(IN_W), rtile(D), pl.BlockSpec((GROUPS, DH, DH), lambda i: (0, 0, 0)),
                   pl.BlockSpec((1, PW), lambda i: (0, 0))) + (hbm,) * n_e,
        scratch_shapes=[pltpu.VMEM((HEADS, DH, DH), f32), pltpu.VMEM((tt, RW), bf16),
                        pltpu.VMEM((tt + HALO, PW), f32), pltpu.VMEM((tt + HALO, PW), f32)] + _chip_exchange_sems(n_e),
        compiler_params=pltpu.CompilerParams(dimension_semantics=("arbitrary",), vmem_limit_bytes=V7X_VMEM_LIMIT),
    )(dz1, w_out, qkv, g, oret, states, pooled, cos, sin, dmat, qd, kd, w_pool, pool_scale, w_in_t, *exchange)


def _weight_grad(a, b, name, tm, exchange=(), tk=2048):
    m = a.shape[1]
    n_m, n_k, n_e = m // tm, T // tk, len(exchange)

    def body(a_ref, b_ref, *rest):
        ein, o_ref, eout, (acc_s, *sems) = rest[:n_e], rest[n_e], rest[n_e + 1:2 * n_e + 1], rest[2 * n_e + 1:]
        i, k = pl.program_id(0), pl.program_id(1)

        if n_e:
            @pl.when((i == 0) & (k == 0))
            def _():
                _chip_exchange_start(ein, eout, *sems)

        @pl.when(k == 0)
        def _():
            acc_s[...] = jnp.zeros_like(acc_s)

        acc_s[...] += _dot(a_ref[...], b_ref[pl.ds(pl.multiple_of(k * tk, tk), tk), :].astype(bf16), TN)

        @pl.when(k == n_k - 1)
        def _():
            o_ref[...] = acc_s[...].astype(bf16)

        if n_e:
            @pl.when((i == n_m - 1) & (k == n_k - 1))
            def _():
                _chip_exchange_finish(ein, eout, *sems)

    hbm = pl.BlockSpec(memory_space=pltpu.HBM)
    return pl.pallas_call(
        body, name=name, grid=(n_m, n_k),
        out_shape=(jax.ShapeDtypeStruct((m, D), bf16),) + tuple(jax.ShapeDtypeStruct(e.shape, e.dtype) for e in exchange),
        in_specs=[pl.BlockSpec((tk, tm), lambda i, k: (k, i)),
                  pl.BlockSpec((T, D), lambda i, k: (0, 0), pipeline_mode=pl.Buffered(1))] + [hbm] * n_e,
        out_specs=(pl.BlockSpec((tm, D), lambda i, k: (i, 0)),) + (hbm,) * n_e,
        scratch_shapes=[pltpu.VMEM((tm, D), f32)] + _chip_exchange_sems(n_e),
        compiler_params=pltpu.CompilerParams(dimension_semantics=("arbitrary", "arbitrary"),
                                             vmem_limit_bytes=V7X_VMEM_LIMIT),
    )(a, b, *exchange)


CHIP_FLIPS = ((1, 0), (0, 1), (1, 1))
PAIR_BARRIER = 0


def _me():
    return lax.axis_index("x"), lax.axis_index("y"), lax.axis_index("c")


def _chip(me, k):
    x, y, _ = me
    if k == 0:
        return x, y
    fx, fy = CHIP_FLIPS[k - 1]
    return (1 - x if fx else x), (1 - y if fy else y)


def _slot(x, y, c):
    return 4 * x + 2 * y + c


def _remote(src, dst, send_sem, recv_sem, to):
    return pltpu.make_async_remote_copy(src_ref=src, dst_ref=dst, send_sem=send_sem, recv_sem=recv_sem,
                                        device_id=to, device_id_type=pl.DeviceIdType.MESH)


def _gather_sems(n):
    return [pltpu.SemaphoreType.DMA((7, n)), pltpu.SemaphoreType.DMA((7, n)), pltpu.SemaphoreType.DMA((n,))] if n else []


def _gather_copy(k, j, gin, gout, send_sems, recv_sems, sending):
    x, y, c = _me()
    sibling, x_chip, y_chip, d_chip = (x, y, 1 - c), (1 - x, y), (x, 1 - y), (1 - x, 1 - y)
    south = c == 0
    passed_on = (jnp.where(south, 1 - x, x), jnp.where(south, y, 1 - y), c)
    src, to = gin[j], sibling
    if sending:
        block = {0: (x, y, c), 1: (x, y, c), 2: (x, y, c), 3: passed_on, 4: (*x_chip, c), 5: (*y_chip, c), 6: (*d_chip, c)}[k]
        to = {1: (*x_chip, c), 2: (*y_chip, c), 3: (jnp.where(south, x, 1 - x), jnp.where(south, 1 - y, y), c)}.get(k, sibling)
        if k >= 3:
            src = gout[j].at[_slot(*block)]
    else:
        block = {0: sibling, 1: (*x_chip, c), 2: (*y_chip, c), 3: (*d_chip, c), 4: (*x_chip, 1 - c), 5: (*y_chip, 1 - c),
                 6: (*d_chip, 1 - c)}[k]
    return _remote(src, gout[j].at[_slot(*block)], send_sems.at[k, j], recv_sems.at[k, j], to)


def _gather_do(ks, action, gin, gout, send_sems, recv_sems):
    for k in ks:
        for j in range(len(gin)):
            cp = _gather_copy(k, j, gin, gout, send_sems, recv_sems, action != "wait_recv")
            getattr(cp, action)()


def _gather_start(gin, gout, send_sems, recv_sems, local_sems):
    for j in range(len(gin)):
        pltpu.make_async_copy(gin[j], gout[j].at[_slot(*_me())], local_sems.at[j]).start()
    _gather_do((0, 1, 2), "start", gin, gout, send_sems, recv_sems)


def _gather_forward(gin, gout, send_sems, recv_sems, local_sems):
    _gather_do((1, 2), "wait_recv", gin, gout, send_sems, recv_sems)
    _gather_do((3, 4, 5), "start", gin, gout, send_sems, recv_sems)


def _gather_finish(gin, gout, send_sems, recv_sems, local_sems):
    _gather_do((3,), "wait_recv", gin, gout, send_sems, recv_sems)
    _gather_do((6,), "start", gin, gout, send_sems, recv_sems)
    _gather_do((0, 4, 5, 6), "wait_recv", gin, gout, send_sems, recv_sems)
    _gather_do(range(7), "wait_send", gin, gout, send_sems, recv_sems)
    for j in range(len(gin)):
        pltpu.make_async_copy(gin[j], gout[j].at[_slot(*_me())], local_sems.at[j]).wait()


def _all_gather(blocks, name):
    n = len(blocks)

    def body(*refs):
        gin, gout, sems = refs[:n], refs[n:2 * n], refs[2 * n:]
        _gather_start(gin, gout, *sems)
        _gather_forward(gin, gout, *sems)
        _gather_finish(gin, gout, *sems)

    hbm = pl.BlockSpec(memory_space=pltpu.HBM)
    return pl.pallas_call(
        body, name=name,
        out_shape=tuple(jax.ShapeDtypeStruct((N_DEV,) + b.shape, b.dtype) for b in blocks),
        in_specs=[hbm] * n, out_specs=(hbm,) * n, scratch_shapes=_gather_sems(n),
    )(*blocks)


def _pair_reduce(parts, name):
    n = len(parts)

    def body(*refs):
        ins, own, others, landing, mine = (refs[k * n:(k + 1) * n] for k in range(5))
        send_sems, recv_sems, local_sems = refs[5 * n:]
        me = _me()
        x, y, c = me
        sibling = (x, y, 1 - c)
        barrier = pltpu.get_barrier_semaphore()
        pl.semaphore_signal(barrier, inc=1, device_id=sibling, device_id_type=pl.DeviceIdType.MESH)
        pl.semaphore_wait(barrier, 1)
        sends, loads = [], []
        for k in range(4):
            for j in range(n):
                cp = _remote(ins[j].at[_slot(*_chip(me, k), 1 - c)], landing[j].at[k], send_sems.at[k, j],
                             recv_sems.at[k, j], sibling)
                cp.start()
                sends.append(cp)
                ld = pltpu.make_async_copy(ins[j].at[_slot(*_chip(me, k), c)], mine[j].at[k], local_sems.at[k, j])
                ld.start()
                loads.append(ld)
        for k in range(4):
            for j in range(n):
                loads[k * n + j].wait()
                _remote(ins[j].at[0], landing[j].at[k], send_sems.at[k, j], recv_sems.at[k, j], sibling).wait_recv()
                total = mine[j][k].astype(f32) + landing[j][k].astype(f32)
                if k == 0:
                    own[j][...] = total.astype(own[j].dtype)
                else:
                    others[j][k - 1] = total.astype(others[j].dtype)
        for cp in sends:
            cp.wait_send()

    vm = pl.BlockSpec(memory_space=pltpu.VMEM)
    return pl.pallas_call(
        body, name=name,
        out_shape=tuple(jax.ShapeDtypeStruct(p.shape[1:], p.dtype) for p in parts)
        + tuple(jax.ShapeDtypeStruct((3,) + p.shape[1:], p.dtype) for p in parts),
        in_specs=[pl.BlockSpec(memory_space=pltpu.HBM)] * n, out_specs=(vm,) * (2 * n),
        scratch_shapes=[pltpu.VMEM((4,) + p.shape[1:], p.dtype) for p in parts] * 2
        + [pltpu.SemaphoreType.DMA((4, n)), pltpu.SemaphoreType.DMA((4, n)), pltpu.SemaphoreType.DMA((4, n))],
        compiler_params=pltpu.CompilerParams(vmem_limit_bytes=V7X_VMEM_LIMIT, collective_id=PAIR_BARRIER),
    )(*parts)


def _chip_exchange_sems(n):
    return [pltpu.SemaphoreType.DMA((3, n)), pltpu.SemaphoreType.DMA((3, n))] if n else []


def _chip_exchange_copy(k, j, ein, eout, send_sems, recv_sems):
    me = _me()
    return _remote(ein[j].at[k - 1], eout[j].at[k - 1], send_sems.at[k - 1, j], recv_sems.at[k - 1, j],
                   (*_chip(me, k), me[2]))


def _chip_exchange_start(ein, eout, send_sems, recv_sems):
    for k in range(1, 4):
        for j in range(len(ein)):
            _chip_exchange_copy(k, j, ein, eout, send_sems, recv_sems).start()


def _chip_exchange_finish(ein, eout, send_sems, recv_sems):
    for k in range(1, 4):
        for j in range(len(ein)):
            _chip_exchange_copy(k, j, ein, eout, send_sems, recv_sems).wait_recv()
    for k in range(1, 4):
        for j in range(len(ein)):
            _chip_exchange_copy(k, j, ein, eout, send_sems, recv_sems).wait_send()


def _chip_exchange(others, name):
    n = len(others)

    def body(*refs):
        ein, eout, sems = refs[:n], refs[n:2 * n], refs[2 * n:]
        _chip_exchange_start(ein, eout, *sems)
        _chip_exchange_finish(ein, eout, *sems)

    hbm = pl.BlockSpec(memory_space=pltpu.HBM)
    return pl.pallas_call(
        body, name=name, out_shape=tuple(jax.ShapeDtypeStruct(e.shape, e.dtype) for e in others),
        in_specs=[hbm] * n, out_specs=(hbm,) * n, scratch_shapes=_chip_exchange_sems(n),
    )(*others)


def _sum_parts(owns, arrived, name):
    n = len(owns)

    def body(*refs):
        for own, arr, out in zip(refs[:n], refs[n:2 * n], refs[2 * n:]):
            acc = own[...].astype(f32)
            for k in range(3):
                acc = acc + arr[k].astype(f32)
            out[...] = acc

    vm = pl.BlockSpec(memory_space=pltpu.VMEM)
    return pl.pallas_call(
        body, name=name, out_shape=tuple(jax.ShapeDtypeStruct(o.shape, f32) for o in owns),
        in_specs=[vm] * (2 * n), out_specs=(vm,) * n,
        compiler_params=pltpu.CompilerParams(vmem_limit_bytes=V7X_VMEM_LIMIT),
    )(*owns, *arrived)


ADAM_C1 = 1.0 / (1.0 - ADAM_B1 ** ADAM_STEP)
ADAM_C2 = 1.0 / (1.0 - ADAM_B2 ** ADAM_STEP)


def _adam_update(w, g, m, v):
    m = ADAM_B1 * m + (1.0 - ADAM_B1) * g
    v = ADAM_B2 * v + (1.0 - ADAM_B2) * (g * g)
    return -ADAM_LR * ((m * ADAM_C1) / (jnp.sqrt(v * ADAM_C2) + ADAM_EPS) + ADAM_WD * w), m, v


def _sum_adamw(own, arrived, w, m, v, name, steps):
    rows = own.shape[0]
    br = rows // steps

    def body(own_ref, arr_ref, w_ref, m_ref, v_ref, g_out, d_out, m_out, v_out):
        g = own_ref[...].astype(f32)
        for k in range(3):
            g = g + arr_ref[k].astype(f32)
        g_out[...] = g
        d_out[...], m_out[...], v_out[...] = _adam_update(w_ref[...], g, m_ref[...], v_ref[...])

    blk = pl.BlockSpec((br, D), lambda i: (i, 0))
    return pl.pallas_call(
        body, name=name, grid=(steps,), out_shape=(jax.ShapeDtypeStruct((rows, D), f32),) * 4,
        in_specs=[blk, pl.BlockSpec((3, br, D), lambda i: (0, i, 0)), blk, blk, blk], out_specs=(blk,) * 4,
        compiler_params=pltpu.CompilerParams(dimension_semantics=("parallel",), vmem_limit_bytes=V7X_VMEM_LIMIT),
    )(own, arrived, w, m, v)


def _adamw(ws, gs, ms, vs, name):
    n = len(ws)

    def body(*refs):
        w_r, g_r, m_r, v_r = (refs[k * n:(k + 1) * n] for k in range(4))
        d_o, m_o, v_o = (refs[(4 + k) * n:(5 + k) * n] for k in range(3))
        for j in range(n):
            d_o[j][...], m_o[j][...], v_o[j][...] = _adam_update(w_r[j][...], g_r[j][...], m_r[j][...], v_r[j][...])

    vm = pl.BlockSpec(memory_space=pltpu.VMEM)
    shapes = tuple(jax.ShapeDtypeStruct(w.shape, f32) for w in ws)
    return pl.pallas_call(
        body, name=name, out_shape=shapes * 3, in_specs=[vm] * (4 * n), out_specs=tuple([vm] * (3 * n)),
        compiler_params=pltpu.CompilerParams(vmem_limit_bytes=V7X_VMEM_LIMIT),
    )(*ws, *gs, *ms, *vs)


SMALL = (("w_pool", GROUPS * DH * DH), ("pool_scale", PW), ("ln1_g", D), ("ln1_b", D), ("conv_b", D_FF),
         ("ln2_g", D), ("ln2_b", D), ("conv_w", 3 * D_FF), ("loss", 1))
SMALL_ROWS = 640


def _pack(named):
    flat = jnp.concatenate([named[k].reshape(-1) for k, _ in SMALL])
    return jnp.pad(flat, (0, SMALL_ROWS * 128 - flat.shape[0])).reshape(SMALL_ROWS, 128)


def _unpack(packed):
    flat, out, at = packed.reshape(-1), {}, 0
    for k, size in SMALL:
        out[k] = flat[at:at + size]
        at += size
    return out


def kernel(x, w_in, w_pool, pool_scale, w_out, ln1_g, ln1_b, w_up, conv_w, conv_b, w_down, ln2_g, ln2_b, loss_target, m_w_in, m_w_pool, m_pool_scale, m_w_out, m_ln1_g, m_ln1_b, m_w_up, m_conv_w, m_conv_b, m_w_down, m_ln2_g, m_ln2_b, v_w_in, v_w_pool, v_pool_scale, v_w_out, v_ln1_g, v_ln1_b, v_w_up, v_conv_w, v_conv_b, v_w_down, v_ln2_g, v_ln2_b):
    me = 4 * lax.axis_index("x") + 2 * lax.axis_index("y") + lax.axis_index("c")
    x2, tgt = x[0], loss_target[0]

    g_in, g_out, g_cw = _all_gather([w_in[0].T.astype(bf16), w_out[0].astype(bf16), conv_w[0]], "gather_weights")
    w_in_t = g_in.reshape(IN_W, D)
    w_out_f = g_out.reshape(D, D)
    conv_w_f = jnp.transpose(g_cw, (1, 0, 2)).reshape(3, D_FF)
    w_pool_b = w_pool[0].astype(bf16)

    cos, sin = _rope_tables()
    dmat, qd, kd, cdec = _decay_tables(MIX_TILE)

    qkv, g, oret, states, cat, pooled, xhat1, rstd1, x1b, g_up, g_down = _mix_forward(
        x2, w_in_t, cos, sin, dmat, qd, kd, cdec, w_pool_b, pool_scale, w_out_f, ln1_g, ln1_b,
        gather=[w_up[0].T.astype(bf16), w_down[0].astype(bf16)])
    w_up_t = g_up.reshape(2 * D_FF, D)
    w_down_f = g_down.reshape(D_FF, D)
    dz1, dz2b, du, f, loss8, d_ln2_g, d_ln2_b, d_ln1_g, d_ln1_b, d_conv_b, d_conv_w = _ffn_forward_backward(
        xhat1, rstd1, ln1_g, ln1_b, w_up_t, conv_w_f, conv_b, w_down_f, ln2_g, ln2_b, tgt)

    (dw_down,) = _weight_grad(f, dz2b, "grad_w_down", tm=D_FF // 2)
    own_down, oth_down = _pair_reduce([dw_down.reshape(N_DEV, ROWS_DOWN, D)], "pair_reduce_down")
    dw_up_t, arr_down = _weight_grad(du, x1b, "grad_w_up", tm=D_FF // 2, exchange=[oth_down])
    own_up, oth_up = _pair_reduce([dw_up_t.reshape(N_DEV, ROWS_UP, D)], "pair_reduce_up")
    dproj, grad_x, d_w_pool, d_pool_scale, arr_up = _mix_backward(
        dz1, w_out_f, qkv, g, oret, states, pooled, cos, sin, dmat, qd, kd, cdec, w_pool_b, pool_scale, w_in_t,
        exchange=[oth_up])
    (dw_out,) = _weight_grad(cat, dz1, "grad_w_out", tm=D)
    small = _pack({"w_pool": d_w_pool, "pool_scale": d_pool_scale, "ln1_g": d_ln1_g, "ln1_b": d_ln1_b,
                   "conv_b": d_conv_b, "ln2_g": d_ln2_g, "ln2_b": d_ln2_b, "conv_w": d_conv_w, "loss": loss8[0, :1]})
    own_out, own_small, oth_out, oth_small = _pair_reduce(
        [dw_out.reshape(N_DEV, ROWS_OUT, D), small.reshape(N_DEV, SMALL_ROWS // N_DEV, 128)], "pair_reduce_out")
    dw_in_t, arr_out, arr_small = _weight_grad(dproj, x2, "grad_w_in", tm=IN_W // 2, exchange=[oth_out, oth_small])
    own_in, oth_in = _pair_reduce([dw_in_t.reshape(N_DEV, ROWS_IN, D)], "pair_reduce_in")
    (arr_in,) = _chip_exchange([oth_in], "exchange_in")

    names = ["w_in", "w_pool", "pool_scale", "w_out", "ln1_g", "ln1_b", "w_up", "conv_w", "conv_b", "w_down",
             "ln2_g", "ln2_b"]
    w_d = dict(w_in=w_in, w_pool=w_pool, pool_scale=pool_scale, w_out=w_out, ln1_g=ln1_g, ln1_b=ln1_b, w_up=w_up,
               conv_w=conv_w, conv_b=conv_b, w_down=w_down, ln2_g=ln2_g, ln2_b=ln2_b)
    m_d = dict(w_in=m_w_in, w_pool=m_w_pool, pool_scale=m_pool_scale, w_out=m_w_out, ln1_g=m_ln1_g, ln1_b=m_ln1_b,
               w_up=m_w_up, conv_w=m_conv_w, conv_b=m_conv_b, w_down=m_w_down, ln2_g=m_ln2_g, ln2_b=m_ln2_b)
    v_d = dict(w_in=v_w_in, w_pool=v_w_pool, pool_scale=v_pool_scale, w_out=v_w_out, ln1_g=v_ln1_g, ln1_b=v_ln1_b,
               w_up=v_w_up, conv_w=v_conv_w, conv_b=v_conv_b, w_down=v_w_down, ln2_g=v_ln2_g, ln2_b=v_ln2_b)
    g_d, delta, new_m, new_v = {}, {}, {}, {}

    big = (("w_in", own_in, arr_in, True, 4), ("w_out", own_out, arr_out, False, 2),
           ("w_up", own_up, arr_up, True, 4), ("w_down", own_down, arr_down, False, 2))
    for k, own, arr, transposed, steps in big:
        lay = (lambda a: a[0].T) if transposed else (lambda a: a[0])
        back = (lambda a: a.T[None]) if transposed else (lambda a: a[None])
        res = _sum_adamw(own, arr, lay(w_d[k]), lay(m_d[k]), lay(v_d[k]), "adamw_" + k, steps)
        g_d[k], delta[k], new_m[k], new_v[k] = (back(r) for r in res)

    (small_piece,) = _sum_parts([own_small], [arr_small], "sum_small_grads")
    (gs_small,) = _all_gather([small_piece], "gather_small_grads")
    gsm = _unpack(gs_small)
    gsm["conv_w"] = lax.dynamic_slice(gsm["conv_w"].reshape(3, D_FF), (0, me * (D_FF // N_DEV)), (3, D_FF // N_DEV))
    two_d = lambda a: a.reshape(-1, a.shape[-1])
    group = [k for k in names if k not in g_d]
    for k in group:
        g_d[k] = gsm[k].reshape(w_d[k].shape)
    res = _adamw([two_d(w_d[k]) for k in group], [two_d(g_d[k]) for k in group], [two_d(m_d[k]) for k in group],
                 [two_d(v_d[k]) for k in group], "adamw_small")
    for j, k in enumerate(group):
        delta[k] = res[j].reshape(w_d[k].shape)
        new_m[k] = res[len(group) + j].reshape(w_d[k].shape)
        new_v[k] = res[2 * len(group) + j].reshape(w_d[k].shape)

    loss = gsm["loss"].reshape(())
    return (loss, grad_x[None], *[g_d[k] for k in names], *[delta[k] for k in names], *[new_m[k] for k in names],
            *[new_v[k] for k in names])
```

```python
import functools
import math

import numpy as np
import jax
import jax.numpy as jnp
from jax import lax
from jax.experimental import pallas as pl
from jax.experimental.pallas import tpu as pltpu

f32 = jnp.float32
bf16 = jnp.bfloat16

N_DEV = 8
T = 4096
D = 1024
CHUNK = 64
MIX_TILE = 512
HEADS = 4
DH = 128
RW = HEADS * DH
PW = 512
GROUPS = 4
WINDOWS = (2, 4, 8, 16)
IN_W = 4 * RW + PW
D_FF = 2816
LN_EPS = 1e-5
RMS_EPS = 1e-6
ALPHA = 2.0 ** 0.25
K_SCALE = DH ** -0.5

ADAM_LR = 0.001
ADAM_B1 = 0.9
ADAM_B2 = 0.999
ADAM_EPS = 1e-08
ADAM_WD = 0.01
ADAM_STEP = 10

ROWS_IN, ROWS_OUT, ROWS_UP, ROWS_DOWN = IN_W // N_DEV, D // N_DEV, 2 * D_FF // N_DEV, D_FF // N_DEV

V7X_VMEM_LIMIT = 56 * 2 ** 20
HALO = 32

NT = (((1,), (1,)), ((), ()))
TN = (((0,), (0,)), ((), ()))
NN = (((1,), (0,)), ((), ()))


def _dot(a, b, dims=NN):
    return lax.dot_general(a, b, dims, preferred_element_type=f32)


def _const_spec(shape):
    zeros = (0,) * len(shape)
    return pl.BlockSpec(shape, lambda i: zeros, pipeline_mode=pl.Buffered(1))


def _sigmoid(x):
    return 0.5 * jnp.tanh(0.5 * x) + 0.5


def _decay_tables(tt):
    h = np.arange(HEADS, dtype=np.float64)
    log_gamma = np.log(1.0 - 2.0 ** (-5.0 - h)).astype(np.float32).astype(np.float64)[:, None, None]
    idx = np.arange(tt, dtype=np.float64)
    visible = (idx[None, :] // CHUNK) <= (idx[:, None] // CHUNK)
    mask = np.where(visible[None], np.exp(log_gamma * np.abs(idx[:, None] - idx[None, :])[None]), 0.0)
    qd = np.broadcast_to(np.exp(log_gamma * (idx[None, :, None] + 1.0)), (HEADS, tt, DH))
    kd = np.broadcast_to(np.exp(log_gamma * (tt - 1.0 - idx[None, :, None])), (HEADS, tt, DH))
    cd = np.exp(log_gamma[:, 0, 0] * tt)
    return (jnp.asarray(mask, f32), jnp.asarray(qd, f32), jnp.asarray(kd, f32), [float(c) for c in cd])


def _rope_tables():
    inv_freq = (10000.0 ** (-np.arange(0, DH, 2, dtype=np.float64) / DH)).astype(np.float32)
    ang = (np.arange(T, dtype=np.float32)[:, None] * inv_freq[None, :]).astype(np.float64)
    cos, sin = np.cos(ang), np.sin(ang)
    return (jnp.asarray(np.concatenate([cos, cos], axis=1), f32), jnp.asarray(np.concatenate([-sin, sin], axis=1), f32))


def _swap_halves(t):
    return pltpu.roll(t, DH // 2, axis=1)


def _mix_forward(x, w_in_t, cos, sin, dmat, qd, kd, cdec, w_pool, pool_scale, w_out, ln1_g, ln1_b, gather,
                 tt=MIX_TILE):
    n_tiles = T // tt
    n_g = len(gather)

    def body(x_ref, wint_ref, cos_ref, sin_ref, dmat_ref, qd_ref, kd_ref, wpool_ref, pscale_ref, wout_ref,
             g1_ref, b1_ref, *rest):
        gin, rest = rest[:n_g], rest[n_g:]
        qkv_ref, g_ref, oret_ref, states_ref, cat_ref, pooled_ref, xhat_ref, rstd_ref, x1b_ref = rest[:9]
        gout, (state_s, pext_s, tmp_s, *sems) = rest[9:9 + n_g], rest[9 + n_g:]
        i = pl.program_id(0)

        @pl.when(i == 0)
        def _():
            state_s[...] = jnp.zeros_like(state_s)
            pext_s[pl.ds(0, HALO), :] = jnp.zeros((HALO, PW), f32)
            _gather_start(gin, gout, *sems)

        @pl.when(i == n_tiles - 2)
        def _():
            _gather_forward(gin, gout, *sems)

        xb = x_ref[...].astype(bf16)
        cos_t, sin_t = cos_ref[...], sin_ref[...]
        for part in range(2):
            pr = _dot(xb, wint_ref[pl.ds(part * RW, RW), :], NT)
            for h in range(HEADS):
                t = pr[:, h * DH:(h + 1) * DH]
                r = t * cos_t + _swap_halves(t) * sin_t
                if part == 1:
                    r = r * K_SCALE
                qkv_ref[:, part * RW + h * DH: part * RW + (h + 1) * DH] = r.astype(bf16)
        qkv_ref[:, 2 * RW:3 * RW] = _dot(xb, wint_ref[pl.ds(2 * RW, RW), :], NT).astype(bf16)
        g_ref[...] = _dot(xb, wint_ref[pl.ds(3 * RW, RW), :], NT)
        pext_s[pl.ds(HALO, tt), :] = _dot(xb, wint_ref[pl.ds(4 * RW, PW), :], NT)

        for h in range(HEADS):
            q = qkv_ref[:, h * DH:(h + 1) * DH]
            k = qkv_ref[:, RW + h * DH: RW + (h + 1) * DH]
            v = qkv_ref[:, 2 * RW + h * DH: 2 * RW + (h + 1) * DH]
            s = _dot(q, k, NT) * dmat_ref[h]
            st = state_s[h]
            stb = st.astype(bf16)
            states_ref[0, h] = stb
            oret_ref[:, h * DH:(h + 1) * DH] = (_dot(s.astype(bf16), v)
                                               + _dot((q.astype(f32) * qd_ref[h]).astype(bf16), stb))
            state_s[h] = st * cdec[h] + _dot((k.astype(f32) * kd_ref[h]).astype(bf16), v, TN)

        for h in range(HEADS):
            sl = slice(h * DH, (h + 1) * DH)
            o = oret_ref[:, sl]
            r = lax.rsqrt(jnp.mean(o * o, axis=-1, keepdims=True) + RMS_EPS)
            gg = g_ref[:, sl]
            cat_ref[:, sl] = (o * r * (gg * _sigmoid(gg))).astype(bf16)

        pos1 = (i * tt + lax.broadcasted_iota(jnp.int32, (tt, 1), 0) + 1).astype(f32)
        for gi, w in enumerate(WINDOWS):
            sl = slice(gi * DH, (gi + 1) * DH)
            stages = int(math.log2(w))
            src = pext_s
            for s in range(stages):
                lo = HALO - 8 * (stages - 1 - s)
                n = tt + HALO - lo
                shift = 2 ** s
                val = src[pl.ds(lo, n), sl] + src[pl.ds(lo - shift, n), sl]
                if s == stages - 1:
                    wsum = val
                else:
                    tmp_s[pl.ds(lo, n), sl] = val
                    src = tmp_s
            p_g = pext_s[pl.ds(HALO, tt), sl]
            pooled = (wsum / jnp.minimum(pos1, float(w)) - p_g).astype(bf16)
            pooled_ref[:, sl] = pooled
            y = _dot(pooled, wpool_ref[gi]) * pscale_ref[:, sl]
            cat_ref[:, RW + gi * DH: RW + (gi + 1) * DH] = y.astype(bf16)
        pext_s[pl.ds(0, HALO), :] = pext_s[pl.ds(tt, HALO), :]

        z = ALPHA * x_ref[...] + _dot(cat_ref[...], wout_ref[...])
        mu = jnp.mean(z, axis=-1, keepdims=True)
        zc = z - mu
        rstd = lax.rsqrt(jnp.mean(zc * zc, axis=-1, keepdims=True) + LN_EPS)
        xhat = zc * rstd
        xhat_ref[...] = xhat
        rstd_ref[...] = rstd
        x1b_ref[...] = (xhat * g1_ref[...] + b1_ref[...]).astype(bf16)

        @pl.when(i == n_tiles - 1)
        def _():
            _gather_finish(gin, gout, *sems)

    tile = lambda w: pl.BlockSpec((tt, w), lambda i: (i, 0))
    hbm = pl.BlockSpec(memory_space=pltpu.HBM)
    out_shape = (
        jax.ShapeDtypeStruct((T, 3 * RW), bf16),
        jax.ShapeDtypeStruct((T, RW), f32),
        jax.ShapeDtypeStruct((T, RW), f32),
        jax.ShapeDtypeStruct((n_tiles, HEADS, DH, DH), bf16),
        jax.ShapeDtypeStruct((T, D), bf16),
        jax.ShapeDtypeStruct((T, PW), bf16),
        jax.ShapeDtypeStruct((T, D), f32),
        jax.ShapeDtypeStruct((T, 1), f32),
        jax.ShapeDtypeStruct((T, D), bf16),
    ) + tuple(jax.ShapeDtypeStruct((N_DEV,) + b.shape, b.dtype) for b in gather)
    return pl.pallas_call(
        body, name="mix_forward", grid=(n_tiles,), out_shape=out_shape,
        in_specs=[tile(D), _const_spec((IN_W, D)), tile(DH), tile(DH),
                  _const_spec((HEADS, tt, tt)), _const_spec((HEADS, tt, DH)), _const_spec((HEADS, tt, DH)),
                  _const_spec((GROUPS, DH, DH)), _const_spec((1, PW)), _const_spec((D, D)),
                  _const_spec((1, D)), _const_spec((1, D))] + [hbm] * n_g,
        out_specs=(tile(3 * RW), tile(RW), tile(RW),
                   pl.BlockSpec((1, HEADS, DH, DH), lambda i: (i, 0, 0, 0)),
                   tile(D), tile(PW), tile(D), tile(1), tile(D)) + (hbm,) * n_g,
        scratch_shapes=[pltpu.VMEM((HEADS, DH, DH), f32), pltpu.VMEM((tt + HALO, PW), f32),
                        pltpu.VMEM((tt + HALO, PW), f32)] + _gather_sems(n_g),
        compiler_params=pltpu.CompilerParams(dimension_semantics=("arbitrary",), vmem_limit_bytes=V7X_VMEM_LIMIT,
                                             collective_id=GATHER_BARRIER),
    )(x, w_in_t, cos, sin, dmat, qd, kd, w_pool, pool_scale, w_out, ln1_g, ln1_b, *gather)


def _ffn_forward_backward(xhat1, rstd1, ln1_g, ln1_b, w_up_t, conv_w, conv_b, w_down, ln2_g, ln2_b, target,
                          tt=256, widths=(512, 512, 512, 512, 512, 256)):
    n_tiles = T // tt
    assert sum(widths) == D_FF and all(w % 128 == 0 for w in widths)
    chunks = [(sum(widths[:c]), w) for c, w in enumerate(widths)]
    FH = 16
    hb = tt // FH

    def body(xhat_ref, halo_ref, rstd_ref, g1_ref, b1_ref, wupt_ref, cw_ref, cb_ref, wdown_ref, g2_ref, b2_ref, tgt_ref,
             dz1_ref, dz2b_ref, du_ref, f_ref, loss_ref, dg2_ref, db2_ref, dg1_ref, db1_ref, dcb_ref, dcw_ref,
             gext_s, val_s, dhext_s):
        i = pl.program_id(0)
        tile_idx = n_tiles - 1 - i

        def rd(ref, off, lo, w):
            return jnp.concatenate([ref[lo // 128 + k, pl.ds(off, tt), :] for k in range(w // 128)], axis=1)

        def wr(ref, lo, val):
            for k in range(val.shape[1] // 128):
                ref[lo // 128 + k, pl.ds(0, val.shape[0]), :] = val[:, k * 128:(k + 1) * 128]

        @pl.when(i == 0)
        def _():
            for r in (loss_ref, dg2_ref, db2_ref, dg1_ref, db1_ref, dcb_ref, dcw_ref):
                r[...] = jnp.zeros_like(r)
            dhext_s[:, pl.ds(tt, 8), :] = jnp.zeros((D_FF // 128, 8, 128), f32)

        g1, b1 = g1_ref[...], b1_ref[...]
        xhat = xhat_ref[...]
        x1 = xhat * g1 + b1
        x1b = x1.astype(bf16)
        x1h = ((halo_ref[...] * g1 + b1) * jnp.where(tile_idx == 0, 0.0, 1.0)).astype(bf16)
        x1ext = jnp.concatenate([x1h, x1b], axis=0)

        for lo, w in chunks:
            cs = slice(lo, lo + w)
            val = _dot(x1b, wupt_ref[pl.ds(lo, w), :], NT)
            gate_ext = _dot(x1ext, wupt_ref[pl.ds(D_FF + lo, w), :], NT)
            wr(gext_s, lo, gate_ext)
            hh = (cb_ref[:, cs] + cw_ref[0:1, cs] * rd(gext_s, FH - 2, lo, w) + cw_ref[1:2, cs] * rd(gext_s, FH - 1, lo, w)
                  + cw_ref[2:3, cs] * gate_ext[FH:])
            sg = _sigmoid(hh)
            act = hh * sg
            wr(dhext_s, lo, act)
            val_s[:, cs] = val * (sg + act * (1.0 - sg))
            f_ref[:, cs] = (act * val).astype(bf16)

        z = ALPHA * x1 + _dot(f_ref[...], wdown_ref[...])
        mu = jnp.mean(z, axis=-1, keepdims=True)
        zc = z - mu
        rstd2 = lax.rsqrt(jnp.mean(zc * zc, axis=-1, keepdims=True) + LN_EPS)
        xh2 = zc * rstd2
        diff = xh2 * g2_ref[...] + b2_ref[...] - tgt_ref[...]
        loss_ref[...] += 0.5 * jnp.sum(diff * diff) / D
        dy = diff * (1.0 / D)
        dg2_ref[...] += jnp.sum(dy * xh2, axis=0, keepdims=True)
        db2_ref[...] += jnp.sum(dy, axis=0, keepdims=True)
        dyg = dy * g2_ref[...]
        dz2 = rstd2 * (dyg - jnp.mean(dyg, axis=-1, keepdims=True) - xh2 * jnp.mean(dyg * xh2, axis=-1, keepdims=True))
        dz2b = dz2.astype(bf16)
        dz2b_ref[...] = dz2b

        for lo, w in chunks:
            cs = slice(lo, lo + w)
            df = _dot(dz2b, wdown_ref[pl.ds(lo, w), :], NT)
            dval = df * rd(dhext_s, 0, lo, w)
            dh = df * val_s[:, cs]
            wr(dhext_s, lo, dh)
            dh1, dh2, g0 = rd(dhext_s, 1, lo, w), rd(dhext_s, 2, lo, w), rd(gext_s, FH, lo, w)
            dcb_ref[:, cs] += jnp.sum(dh, axis=0, keepdims=True)
            dcw_ref[0:1, cs] += jnp.sum(dh2 * g0, axis=0, keepdims=True)
            dcw_ref[1:2, cs] += jnp.sum(dh1 * g0, axis=0, keepdims=True)
            dcw_ref[2:3, cs] += jnp.sum(dh * g0, axis=0, keepdims=True)
            dgate = cw_ref[2:3, cs] * dh + cw_ref[1:2, cs] * dh1 + cw_ref[0:1, cs] * dh2
            du_ref[:, cs] = dval.astype(bf16)
            du_ref[:, D_FF + lo: D_FF + lo + w] = dgate.astype(bf16)
        dhext_s[:, pl.ds(tt, 8), :] = dhext_s[:, pl.ds(0, 8), :]
        dx1 = ALPHA * dz2 + _dot(du_ref[...], wupt_ref[...])

        dg1_ref[...] += jnp.sum(dx1 * xhat, axis=0, keepdims=True)
        db1_ref[...] += jnp.sum(dx1, axis=0, keepdims=True)
        dxg = dx1 * g1
        dz1_ref[...] = rstd_ref[...] * (dxg - jnp.mean(dxg, axis=-1, keepdims=True)
                                        - xhat * jnp.mean(dxg * xhat, axis=-1, keepdims=True))

    rtile = lambda w: pl.BlockSpec((tt, w), lambda i: (n_tiles - 1 - i, 0))
    acc = lambda shape: pl.BlockSpec(shape, lambda i: (0, 0))
    out_shape = (
        jax.ShapeDtypeStruct((T, D), f32),
        jax.ShapeDtypeStruct((T, D), bf16),
        jax.ShapeDtypeStruct((T, 2 * D_FF), bf16),
        jax.ShapeDtypeStruct((T, D_FF), bf16),
        jax.ShapeDtypeStruct((8, 128), f32),
        jax.ShapeDtypeStruct((1, D), f32), jax.ShapeDtypeStruct((1, D), f32),
        jax.ShapeDtypeStruct((1, D), f32), jax.ShapeDtypeStruct((1, D), f32),
        jax.ShapeDtypeStruct((1, D_FF), f32), jax.ShapeDtypeStruct((3, D_FF), f32),
    )
    return pl.pallas_call(
        body, name="ffn_forward_backward", grid=(n_tiles,), out_shape=out_shape,
        in_specs=[rtile(D),
                  pl.BlockSpec((FH, D), lambda i: (jnp.maximum((n_tiles - 1 - i) * hb - 1, 0), 0)),
                  rtile(1), _const_spec((1, D)), _const_spec((1, D)), _const_spec((2 * D_FF, D)),
                  _const_spec((3, D_FF)), _const_spec((1, D_FF)), _const_spec((D_FF, D)),
                  _const_spec((1, D)), _const_spec((1, D)), rtile(D)],
        out_specs=(rtile(D), rtile(D), rtile(2 * D_FF), rtile(D_FF), acc((8, 128)),
                   acc((1, D)), acc((1, D)), acc((1, D)), acc((1, D)), acc((1, D_FF)), acc((3, D_FF))),
        scratch_shapes=[pltpu.VMEM((D_FF // 128, tt + FH, 128), f32), pltpu.VMEM((tt, D_FF), f32),
                        pltpu.VMEM((D_FF // 128, tt + 8, 128), f32)],
        compiler_params=pltpu.CompilerParams(dimension_semantics=("arbitrary",), vmem_limit_bytes=V7X_VMEM_LIMIT),
    )(xhat1, xhat1, rstd1, ln1_g, ln1_b, w_up_t, conv_w, conv_b, w_down, ln2_g, ln2_b, target)


def _mix_backward(dz1, w_out, qkv, g, oret, states, pooled, cos, sin, dmat, qd, kd, cdec, w_pool, pool_scale, w_in_t,
                  exchange, tt=MIX_TILE):
    n_tiles = T // tt
    n_e = len(exchange)

    def body(dz1_ref, wout_ref, qkv_ref, g_ref, oret_ref, states_ref, pooled_ref, cos_ref, sin_ref, dmat_ref, qd_ref,
             kd_ref, wpool_ref, pscale_ref, wint_ref, *rest):
        ein, rest = rest[:n_e], rest[n_e:]
        dproj_ref, gx_ref, dwpool_ref, dpscale_ref = rest[:4]
        eout, (dstate_s, dout_s, eext_s, tmp_s, *sems) = rest[4:4 + n_e], rest[4 + n_e:]
        i = pl.program_id(0)
        tile_idx = n_tiles - 1 - i

        @pl.when(i == 0)
        def _():
            dstate_s[...] = jnp.zeros_like(dstate_s)
            dwpool_ref[...] = jnp.zeros_like(dwpool_ref)
            dpscale_ref[...] = jnp.zeros_like(dpscale_ref)
            eext_s[pl.ds(tt, HALO), :] = jnp.zeros((HALO, PW), f32)
            _chip_exchange_start(ein, eout, *sems)

        dz1 = dz1_ref[...]
        dcat = _dot(dz1.astype(bf16), wout_ref[...], NT)

        pos1 = (tile_idx * tt + lax.broadcasted_iota(jnp.int32, (tt, 1), 0) + 1).astype(f32)
        for gi, w in enumerate(WINDOWS):
            sl = slice(gi * DH, (gi + 1) * DH)
            dpo = dcat[:, RW + gi * DH: RW + (gi + 1) * DH]
            pooled_g = pooled_ref[:, sl]
            ylin = _dot(pooled_g, wpool_ref[gi])
            dpscale_ref[:, sl] += jnp.sum(dpo * ylin, axis=0, keepdims=True)
            dpw = (dpo * pscale_ref[:, sl]).astype(bf16)
            dwpool_ref[gi] += _dot(pooled_g, dpw, TN)
            dpooled = _dot(dpw, wpool_ref[gi], NT)
            eext_s[pl.ds(0, tt), sl] = dpooled / jnp.minimum(pos1, float(w))
            stages = int(math.log2(w))
            src = eext_s
            for s in range(stages):
                n = tt + 8 * (stages - 1 - s)
                shift = 2 ** s
                val = src[pl.ds(0, n), sl] + src[pl.ds(shift, n), sl]
                if s == stages - 1:
                    wsum = val
                else:
                    tmp_s[pl.ds(0, n), sl] = val
                    src = tmp_s
            dproj_ref[:, 4 * RW + gi * DH: 4 * RW + (gi + 1) * DH] = (wsum - dpooled).astype(bf16)
        eext_s[pl.ds(tt, HALO), :] = eext_s[pl.ds(0, HALO), :]

        for h in range(HEADS):
            sl = slice(h * DH, (h + 1) * DH)
            dr = dcat[:, sl]
            o = oret_ref[:, sl]
            r = lax.rsqrt(jnp.mean(o * o, axis=-1, keepdims=True) + RMS_EPS)
            rn = o * r
            gg = g_ref[:, sl]
            sg = _sigmoid(gg)
            dproj_ref[:, 3 * RW + h * DH: 3 * RW + (h + 1) * DH] = (dr * rn * (sg * (1.0 + gg * (1.0 - sg)))).astype(bf16)
            drn = dr * (gg * sg)
            dout_s[:, sl] = (r * (drn - rn * jnp.mean(drn * rn, axis=-1, keepdims=True))).astype(bf16)

        cos_t, sin_t = cos_ref[...], sin_ref[...]
        for h in range(HEADS):
            q = qkv_ref[:, h * DH:(h + 1) * DH]
            k = qkv_ref[:, RW + h * DH: RW + (h + 1) * DH]
            v = qkv_ref[:, 2 * RW + h * DH: 2 * RW + (h + 1) * DH]
            do = dout_s[:, h * DH:(h + 1) * DH]
            stb = states_ref[0, h]
            dst = dstate_s[h]
            dstb = dst.astype(bf16)
            sb = (_dot(q, k, NT) * dmat_ref[h]).astype(bf16)
            dsb = (_dot(do, v, NT) * dmat_ref[h]).astype(bf16)
            dq = _dot(dsb, k) + _dot(do, stb, NT) * qd_ref[h]
            dk = _dot(dsb, q, TN) + _dot(v, dstb, NT) * kd_ref[h]
            dv = _dot(sb, do, TN) + _dot((k.astype(f32) * kd_ref[h]).astype(bf16), dstb)
            dstate_s[h] = dst * cdec[h] + _dot((q.astype(f32) * qd_ref[h]).astype(bf16), do, TN)
            dproj_ref[:, h * DH:(h + 1) * DH] = (dq * cos_t - _swap_halves(dq) * sin_t).astype(bf16)
            dproj_ref[:, RW + h * DH: RW + (h + 1) * DH] = ((dk * cos_t - _swap_halves(dk) * sin_t) * K_SCALE).astype(bf16)
            dproj_ref[:, 2 * RW + h * DH: 2 * RW + (h + 1) * DH] = dv.astype(bf16)

        gx_ref[...] = ALPHA * dz1 + _dot(dproj_ref[...], wint_ref[...])

        @pl.when(i == n_tiles - 1)
        def _():
            _chip_exchange_finish(ein, eout, *sems)

    rtile = lambda w: pl.BlockSpec((tt, w), lambda i: (n_tiles - 1 - i, 0))
    hbm = pl.BlockSpec(memory_space=pltpu.HBM)
    out_shape = (
        jax.ShapeDtypeStruct((T, IN_W), bf16),
        jax.ShapeDtypeStruct((T, D), f32),
        jax.ShapeDtypeStruct((GROUPS, DH, DH), f32),
        jax.ShapeDtypeStruct((1, PW), f32),
    ) + tuple(jax.ShapeDtypeStruct(e.shape, e.dtype) for e in exchange)
    return pl.pallas_call(
        body, name="mix_backward", grid=(n_tiles,), out_shape=out_shape,
        in_specs=[rtile(D), _const_spec((D, D)), rtile(3 * RW), rtile(RW), rtile(RW),
                  pl.BlockSpec((1, HEADS, DH, DH), lambda i: (n_tiles - 1 - i, 0, 0, 0)),
                  rtile(PW), rtile(DH), rtile(DH),
                  _const_spec((HEADS, tt, tt)), _const_spec((HEADS, tt, DH)), _const_spec((HEADS, tt, DH)),
                  _const_spec((GROUPS, DH, DH)), _const_spec((1, PW)), _const_spec((IN_W, D))] + [hbm] * n_e,
        out_specs=(rtile(IN_W), rtile(D), pl.BlockSpec((GROUPS, DH, DH), lambda i: (0, 0, 0)),
                   pl.BlockSpec((1, PW), lambda i: (0, 0))) + (hbm,) * n_e,
        scratch_shapes=[pltpu.VMEM((HEADS, DH, DH), f32), pltpu.VMEM((tt, RW), bf16),
                        pltpu.VMEM((tt + HALO, PW), f32), pltpu.VMEM((tt + HALO, PW), f32)] + _chip_exchange_sems(n_e),
        compiler_params=pltpu.CompilerParams(dimension_semantics=("arbitrary",), vmem_limit_bytes=V7X_VMEM_LIMIT,
                                             collective_id=CHIP_BARRIER),
    )(dz1, w_out, qkv, g, oret, states, pooled, cos, sin, dmat, qd, kd, w_pool, pool_scale, w_in_t, *exchange)


def _weight_grad(a, b, name, tm, exchange=(), tk=2048):
    m = a.shape[1]
    n_m, n_k, n_e = m // tm, T // tk, len(exchange)

    def body(a_ref, b_ref, *rest):
        ein, o_ref, eout, (acc_s, *sems) = rest[:n_e], rest[n_e], rest[n_e + 1:2 * n_e + 1], rest[2 * n_e + 1:]
        i, k = pl.program_id(0), pl.program_id(1)

        if n_e:
            @pl.when((i == 0) & (k == 0))
            def _():
                _chip_exchange_start(ein, eout, *sems)

        @pl.when(k == 0)
        def _():
            acc_s[...] = jnp.zeros_like(acc_s)

        acc_s[...] += _dot(a_ref[...], b_ref[pl.ds(pl.multiple_of(k * tk, tk), tk), :].astype(bf16), TN)

        @pl.when(k == n_k - 1)
        def _():
            o_ref[...] = acc_s[...].astype(bf16)

        if n_e:
            @pl.when((i == n_m - 1) & (k == n_k - 1))
            def _():
                _chip_exchange_finish(ein, eout, *sems)

    hbm = pl.BlockSpec(memory_space=pltpu.HBM)
    return pl.pallas_call(
        body, name=name, grid=(n_m, n_k),
        out_shape=(jax.ShapeDtypeStruct((m, D), bf16),) + tuple(jax.ShapeDtypeStruct(e.shape, e.dtype) for e in exchange),
        in_specs=[pl.BlockSpec((tk, tm), lambda i, k: (k, i)),
                  pl.BlockSpec((T, D), lambda i, k: (0, 0), pipeline_mode=pl.Buffered(1))] + [hbm] * n_e,
        out_specs=(pl.BlockSpec((tm, D), lambda i, k: (i, 0)),) + (hbm,) * n_e,
        scratch_shapes=[pltpu.VMEM((tm, D), f32)] + _chip_exchange_sems(n_e),
        compiler_params=pltpu.CompilerParams(dimension_semantics=("arbitrary", "arbitrary"),
                                             vmem_limit_bytes=V7X_VMEM_LIMIT,
                                             collective_id=CHIP_BARRIER if n_e else None),
    )(a, b, *exchange)


CHIP_FLIPS = ((1, 0), (0, 1), (1, 1))
PAIR_BARRIER, CHIP_BARRIER, GATHER_BARRIER = 0, 1, 2


def _barrier(peers):
    sem = pltpu.get_barrier_semaphore()
    for peer in peers:
        pl.semaphore_signal(sem, inc=1, device_id=peer, device_id_type=pl.DeviceIdType.MESH)
    pl.semaphore_wait(sem, len(peers))


def _me():
    return lax.axis_index("x"), lax.axis_index("y"), lax.axis_index("c")


def _chip(me, k):
    x, y, _ = me
    if k == 0:
        return x, y
    fx, fy = CHIP_FLIPS[k - 1]
    return (1 - x if fx else x), (1 - y if fy else y)


def _slot(x, y, c):
    return 4 * x + 2 * y + c


def _remote(src, dst, send_sem, recv_sem, to):
    return pltpu.make_async_remote_copy(src_ref=src, dst_ref=dst, send_sem=send_sem, recv_sem=recv_sem,
                                        device_id=to, device_id_type=pl.DeviceIdType.MESH)


def _gather_sems(n):
    return [pltpu.SemaphoreType.DMA((7, n)), pltpu.SemaphoreType.DMA((7, n)), pltpu.SemaphoreType.DMA((n,))] if n else []


def _gather_copy(k, j, gin, gout, send_sems, recv_sems, sending):
    x, y, c = _me()
    sibling, x_chip, y_chip, d_chip = (x, y, 1 - c), (1 - x, y), (x, 1 - y), (1 - x, 1 - y)
    south = c == 0
    passed_on = (jnp.where(south, 1 - x, x), jnp.where(south, y, 1 - y), c)
    src, to = gin[j], sibling
    if sending:
        block = {0: (x, y, c), 1: (x, y, c), 2: (x, y, c), 3: passed_on, 4: (*x_chip, c), 5: (*y_chip, c), 6: (*d_chip, c)}[k]
        to = {1: (*x_chip, c), 2: (*y_chip, c), 3: (jnp.where(south, x, 1 - x), jnp.where(south, 1 - y, y), c)}.get(k, sibling)
        if k >= 3:
            src = gout[j].at[_slot(*block)]
    else:
        block = {0: sibling, 1: (*x_chip, c), 2: (*y_chip, c), 3: (*d_chip, c), 4: (*x_chip, 1 - c), 5: (*y_chip, 1 - c),
                 6: (*d_chip, 1 - c)}[k]
    return _remote(src, gout[j].at[_slot(*block)], send_sems.at[k, j], recv_sems.at[k, j], to)


def _gather_do(ks, action, gin, gout, send_sems, recv_sems):
    for k in ks:
        for j in range(len(gin)):
            cp = _gather_copy(k, j, gin, gout, send_sems, recv_sems, action != "wait_recv")
            getattr(cp, action)()


def _gather_start(gin, gout, send_sems, recv_sems, local_sems):
    x, y, c = _me()
    _barrier([(x, y, 1 - c), (1 - x, y, c), (x, 1 - y, c)])
    for j in range(len(gin)):
        pltpu.make_async_copy(gin[j], gout[j].at[_slot(*_me())], local_sems.at[j]).start()
    _gather_do((0, 1, 2), "start", gin, gout, send_sems, recv_sems)


def _gather_forward(gin, gout, send_sems, recv_sems, local_sems):
    _gather_do((1, 2), "wait_recv", gin, gout, send_sems, recv_sems)
    _gather_do((3, 4, 5), "start", gin, gout, send_sems, recv_sems)


def _gather_finish(gin, gout, send_sems, recv_sems, local_sems):
    _gather_do((3,), "wait_recv", gin, gout, send_sems, recv_sems)
    _gather_do((6,), "start", gin, gout, send_sems, recv_sems)
    _gather_do((0, 4, 5, 6), "wait_recv", gin, gout, send_sems, recv_sems)
    _gather_do(range(7), "wait_send", gin, gout, send_sems, recv_sems)
    for j in range(len(gin)):
        pltpu.make_async_copy(gin[j], gout[j].at[_slot(*_me())], local_sems.at[j]).wait()


def _all_gather(blocks, name):
    n = len(blocks)

    def body(*refs):
        gin, gout, sems = refs[:n], refs[n:2 * n], refs[2 * n:]
        _gather_start(gin, gout, *sems)
        _gather_forward(gin, gout, *sems)
        _gather_finish(gin, gout, *sems)

    hbm = pl.BlockSpec(memory_space=pltpu.HBM)
    return pl.pallas_call(
        body, name=name,
        out_shape=tuple(jax.ShapeDtypeStruct((N_DEV,) + b.shape, b.dtype) for b in blocks),
        in_specs=[hbm] * n, out_specs=(hbm,) * n, scratch_shapes=_gather_sems(n),
        compiler_params=pltpu.CompilerParams(collective_id=GATHER_BARRIER),
    )(*blocks)


def _pair_reduce(parts, name):
    n = len(parts)

    def body(*refs):
        ins, own, others, landing, mine = (refs[k * n:(k + 1) * n] for k in range(5))
        send_sems, recv_sems, local_sems = refs[5 * n:]
        me = _me()
        x, y, c = me
        sibling = (x, y, 1 - c)
        _barrier([sibling])
        sends, loads = [], []
        for k in range(4):
            for j in range(n):
                cp = _remote(ins[j].at[_slot(*_chip(me, k), 1 - c)], landing[j].at[k], send_sems.at[k, j],
                             recv_sems.at[k, j], sibling)
                cp.start()
                sends.append(cp)
                ld = pltpu.make_async_copy(ins[j].at[_slot(*_chip(me, k), c)], mine[j].at[k], local_sems.at[k, j])
                ld.start()
                loads.append(ld)
        for k in range(4):
            for j in range(n):
                loads[k * n + j].wait()
                _remote(ins[j].at[0], landing[j].at[k], send_sems.at[k, j], recv_sems.at[k, j], sibling).wait_recv()
                total = mine[j][k].astype(f32) + landing[j][k].astype(f32)
                if k == 0:
                    own[j][...] = total.astype(own[j].dtype)
                else:
                    others[j][k - 1] = total.astype(others[j].dtype)
        for cp in sends:
            cp.wait_send()

    vm = pl.BlockSpec(memory_space=pltpu.VMEM)
    return pl.pallas_call(
        body, name=name,
        out_shape=tuple(jax.ShapeDtypeStruct(p.shape[1:], p.dtype) for p in parts)
        + tuple(jax.ShapeDtypeStruct((3,) + p.shape[1:], p.dtype) for p in parts),
        in_specs=[pl.BlockSpec(memory_space=pltpu.HBM)] * n, out_specs=(vm,) * (2 * n),
        scratch_shapes=[pltpu.VMEM((4,) + p.shape[1:], p.dtype) for p in parts] * 2
        + [pltpu.SemaphoreType.DMA((4, n)), pltpu.SemaphoreType.DMA((4, n)), pltpu.SemaphoreType.DMA((4, n))],
        compiler_params=pltpu.CompilerParams(vmem_limit_bytes=V7X_VMEM_LIMIT, collective_id=PAIR_BARRIER),
    )(*parts)


def _chip_exchange_sems(n):
    return [pltpu.SemaphoreType.DMA((3, n)), pltpu.SemaphoreType.DMA((3, n))] if n else []


def _chip_exchange_copy(k, j, ein, eout, send_sems, recv_sems):
    me = _me()
    return _remote(ein[j].at[k - 1], eout[j].at[k - 1], send_sems.at[k - 1, j], recv_sems.at[k - 1, j],
                   (*_chip(me, k), me[2]))


def _chip_exchange_start(ein, eout, send_sems, recv_sems):
    me = _me()
    _barrier([(*_chip(me, k), me[2]) for k in range(1, 4)])
    for k in range(1, 4):
        for j in range(len(ein)):
            _chip_exchange_copy(k, j, ein, eout, send_sems, recv_sems).start()


def _chip_exchange_finish(ein, eout, send_sems, recv_sems):
    for k in range(1, 4):
        for j in range(len(ein)):
            _chip_exchange_copy(k, j, ein, eout, send_sems, recv_sems).wait_recv()
    for k in range(1, 4):
        for j in range(len(ein)):
            _chip_exchange_copy(k, j, ein, eout, send_sems, recv_sems).wait_send()


def _chip_exchange(others, name):
    n = len(others)

    def body(*refs):
        ein, eout, sems = refs[:n], refs[n:2 * n], refs[2 * n:]
        _chip_exchange_start(ein, eout, *sems)
        _chip_exchange_finish(ein, eout, *sems)

    hbm = pl.BlockSpec(memory_space=pltpu.HBM)
    return pl.pallas_call(
        body, name=name, out_shape=tuple(jax.ShapeDtypeStruct(e.shape, e.dtype) for e in others),
        in_specs=[hbm] * n, out_specs=(hbm,) * n, scratch_shapes=_chip_exchange_sems(n),
        compiler_params=pltpu.CompilerParams(collective_id=CHIP_BARRIER),
    )(*others)


def _sum_parts(owns, arrived, name):
    n = len(owns)

    def body(*refs):
        for own, arr, out in zip(refs[:n], refs[n:2 * n], refs[2 * n:]):
            acc = own[...].astype(f32)
            for k in range(3):
                acc = acc + arr[k].astype(f32)
            out[...] = acc

    vm = pl.BlockSpec(memory_space=pltpu.VMEM)
    return pl.pallas_call(
        body, name=name, out_shape=tuple(jax.ShapeDtypeStruct(o.shape, f32) for o in owns),
        in_specs=[vm] * (2 * n), out_specs=(vm,) * n,
        compiler_params=pltpu.CompilerParams(vmem_limit_bytes=V7X_VMEM_LIMIT),
    )(*owns, *arrived)


ADAM_C1 = 1.0 / (1.0 - ADAM_B1 ** ADAM_STEP)
ADAM_C2 = 1.0 / (1.0 - ADAM_B2 ** ADAM_STEP)


def _adam_update(w, g, m, v):
    m = ADAM_B1 * m + (1.0 - ADAM_B1) * g
    v = ADAM_B2 * v + (1.0 - ADAM_B2) * (g * g)
    return -ADAM_LR * ((m * ADAM_C1) / (jnp.sqrt(v * ADAM_C2) + ADAM_EPS) + ADAM_WD * w), m, v


def _sum_adamw(own, arrived, w, m, v, name, steps):
    rows = own.shape[0]
    br = rows // steps

    def body(own_ref, arr_ref, w_ref, m_ref, v_ref, g_out, d_out, m_out, v_out):
        g = own_ref[...].astype(f32)
        for k in range(3):
            g = g + arr_ref[k].astype(f32)
        g_out[...] = g
        d_out[...], m_out[...], v_out[...] = _adam_update(w_ref[...], g, m_ref[...], v_ref[...])

    blk = pl.BlockSpec((br, D), lambda i: (i, 0))
    return pl.pallas_call(
        body, name=name, grid=(steps,), out_shape=(jax.ShapeDtypeStruct((rows, D), f32),) * 4,
        in_specs=[blk, pl.BlockSpec((3, br, D), lambda i: (0, i, 0)), blk, blk, blk], out_specs=(blk,) * 4,
        compiler_params=pltpu.CompilerParams(dimension_semantics=("parallel",), vmem_limit_bytes=V7X_VMEM_LIMIT),
    )(own, arrived, w, m, v)


def _adamw(ws, gs, ms, vs, name):
    n = len(ws)

    def body(*refs):
        w_r, g_r, m_r, v_r = (refs[k * n:(k + 1) * n] for k in range(4))
        d_o, m_o, v_o = (refs[(4 + k) * n:(5 + k) * n] for k in range(3))
        for j in range(n):
            d_o[j][...], m_o[j][...], v_o[j][...] = _adam_update(w_r[j][...], g_r[j][...], m_r[j][...], v_r[j][...])

    vm = pl.BlockSpec(memory_space=pltpu.VMEM)
    shapes = tuple(jax.ShapeDtypeStruct(w.shape, f32) for w in ws)
    return pl.pallas_call(
        body, name=name, out_shape=shapes * 3, in_specs=[vm] * (4 * n), out_specs=tuple([vm] * (3 * n)),
        compiler_params=pltpu.CompilerParams(vmem_limit_bytes=V7X_VMEM_LIMIT),
    )(*ws, *gs, *ms, *vs)


SMALL = (("w_pool", GROUPS * DH * DH), ("pool_scale", PW), ("ln1_g", D), ("ln1_b", D), ("conv_b", D_FF),
         ("ln2_g", D), ("ln2_b", D), ("conv_w", 3 * D_FF), ("loss", 1))
SMALL_ROWS = 640


def _pack(named):
    flat = jnp.concatenate([named[k].reshape(-1) for k, _ in SMALL])
    return jnp.pad(flat, (0, SMALL_ROWS * 128 - flat.shape[0])).reshape(SMALL_ROWS, 128)


def _unpack(packed):
    flat, out, at = packed.reshape(-1), {}, 0
    for k, size in SMALL:
        out[k] = flat[at:at + size]
        at += size
    return out


def kernel(x, w_in, w_pool, pool_scale, w_out, ln1_g, ln1_b, w_up, conv_w, conv_b, w_down, ln2_g, ln2_b, loss_target, m_w_in, m_w_pool, m_pool_scale, m_w_out, m_ln1_g, m_ln1_b, m_w_up, m_conv_w, m_conv_b, m_w_down, m_ln2_g, m_ln2_b, v_w_in, v_w_pool, v_pool_scale, v_w_out, v_ln1_g, v_ln1_b, v_w_up, v_conv_w, v_conv_b, v_w_down, v_ln2_g, v_ln2_b):
    me = 4 * lax.axis_index("x") + 2 * lax.axis_index("y") + lax.axis_index("c")
    x2, tgt = x[0], loss_target[0]

    g_in, g_out, g_cw = _all_gather([w_in[0].T.astype(bf16), w_out[0].astype(bf16), conv_w[0]], "gather_weights")
    w_in_t = g_in.reshape(IN_W, D)
    w_out_f = g_out.reshape(D, D)
    conv_w_f = jnp.transpose(g_cw, (1, 0, 2)).reshape(3, D_FF)
    w_pool_b = w_pool[0].astype(bf16)

    cos, sin = _rope_tables()
    dmat, qd, kd, cdec = _decay_tables(MIX_TILE)

    qkv, g, oret, states, cat, pooled, xhat1, rstd1, x1b, g_up, g_down = _mix_forward(
        x2, w_in_t, cos, sin, dmat, qd, kd, cdec, w_pool_b, pool_scale, w_out_f, ln1_g, ln1_b,
        gather=[w_up[0].T.astype(bf16), w_down[0].astype(bf16)])
    w_up_t = g_up.reshape(2 * D_FF, D)
    w_down_f = g_down.reshape(D_FF, D)
    dz1, dz2b, du, f, loss8, d_ln2_g, d_ln2_b, d_ln1_g, d_ln1_b, d_conv_b, d_conv_w = _ffn_forward_backward(
        xhat1, rstd1, ln1_g, ln1_b, w_up_t, conv_w_f, conv_b, w_down_f, ln2_g, ln2_b, tgt)

    (dw_down,) = _weight_grad(f, dz2b, "grad_w_down", tm=D_FF // 2)
    own_down, oth_down = _pair_reduce([dw_down.reshape(N_DEV, ROWS_DOWN, D)], "pair_reduce_down")
    dw_up_t, arr_down = _weight_grad(du, x1b, "grad_w_up", tm=D_FF // 2, exchange=[oth_down])
    own_up, oth_up = _pair_reduce([dw_up_t.reshape(N_DEV, ROWS_UP, D)], "pair_reduce_up")
    dproj, grad_x, d_w_pool, d_pool_scale, arr_up = _mix_backward(
        dz1, w_out_f, qkv, g, oret, states, pooled, cos, sin, dmat, qd, kd, cdec, w_pool_b, pool_scale, w_in_t,
        exchange=[oth_up])
    (dw_out,) = _weight_grad(cat, dz1, "grad_w_out", tm=D)
    small = _pack({"w_pool": d_w_pool, "pool_scale": d_pool_scale, "ln1_g": d_ln1_g, "ln1_b": d_ln1_b,
                   "conv_b": d_conv_b, "ln2_g": d_ln2_g, "ln2_b": d_ln2_b, "conv_w": d_conv_w, "loss": loss8[0, :1]})
    own_out, own_small, oth_out, oth_small = _pair_reduce(
        [dw_out.reshape(N_DEV, ROWS_OUT, D), small.reshape(N_DEV, SMALL_ROWS // N_DEV, 128)], "pair_reduce_out")
    dw_in_t, arr_out, arr_small = _weight_grad(dproj, x2, "grad_w_in", tm=IN_W // 2, exchange=[oth_out, oth_small])
    own_in, oth_in = _pair_reduce([dw_in_t.reshape(N_DEV, ROWS_IN, D)], "pair_reduce_in")
    (arr_in,) = _chip_exchange([oth_in], "exchange_in")

    names = ["w_in", "w_pool", "pool_scale", "w_out", "ln1_g", "ln1_b", "w_up", "conv_w", "conv_b", "w_down",
             "ln2_g", "ln2_b"]
    w_d = dict(w_in=w_in, w_pool=w_pool, pool_scale=pool_scale, w_out=w_out, ln1_g=ln1_g, ln1_b=ln1_b, w_up=w_up,
               conv_w=conv_w, conv_b=conv_b, w_down=w_down, ln2_g=ln2_g, ln2_b=ln2_b)
    m_d = dict(w_in=m_w_in, w_pool=m_w_pool, pool_scale=m_pool_scale, w_out=m_w_out, ln1_g=m_ln1_g, ln1_b=m_ln1_b,
               w_up=m_w_up, conv_w=m_conv_w, conv_b=m_conv_b, w_down=m_w_down, ln2_g=m_ln2_g, ln2_b=m_ln2_b)
    v_d = dict(w_in=v_w_in, w_pool=v_w_pool, pool_scale=v_pool_scale, w_out=v_w_out, ln1_g=v_ln1_g, ln1_b=v_ln1_b,
               w_up=v_w_up, conv_w=v_conv_w, conv_b=v_conv_b, w_down=v_w_down, ln2_g=v_ln2_g, ln2_b=v_ln2_b)
    g_d, delta, new_m, new_v = {}, {}, {}, {}

    big = (("w_in", own_in, arr_in, True, 4), ("w_out", own_out, arr_out, False, 2),
           ("w_up", own_up, arr_up, True, 4), ("w_down", own_down, arr_down, False, 2))
    for k, own, arr, transposed, steps in big:
        lay = (lambda a: a[0].T) if transposed else (lambda a: a[0])
        back = (lambda a: a.T[None]) if transposed else (lambda a: a[None])
        res = _sum_adamw(own, arr, lay(w_d[k]), lay(m_d[k]), lay(v_d[k]), "adamw_" + k, steps)
        g_d[k], delta[k], new_m[k], new_v[k] = (back(r) for r in res)

    (small_piece,) = _sum_parts([own_small], [arr_small], "sum_small_grads")
    (gs_small,) = _all_gather([small_piece], "gather_small_grads")
    gsm = _unpack(gs_small)
    gsm["conv_w"] = lax.dynamic_slice(gsm["conv_w"].reshape(3, D_FF), (0, me * (D_FF // N_DEV)), (3, D_FF // N_DEV))
    two_d = lambda a: a.reshape(-1, a.shape[-1])
    group = [k for k in names if k not in g_d]
    for k in group:
        g_d[k] = gsm[k].reshape(w_d[k].shape)
    res = _adamw([two_d(w_d[k]) for k in group], [two_d(g_d[k]) for k in group], [two_d(m_d[k]) for k in group],
                 [two_d(v_d[k]) for k in group], "adamw_small")
    for j, k in enumerate(group):
        delta[k] = res[j].reshape(w_d[k].shape)
        new_m[k] = res[len(group) + j].reshape(w_d[k].shape)
        new_v[k] = res[2 * len(group) + j].reshape(w_d[k].shape)

    loss = gsm["loss"].reshape(())
    return (loss, grad_x[None], *[g_d[k] for k in names], *[delta[k] for k in names], *[new_m[k] for k in names],
            *[new_v[k] for k in names])
```

```python
import functools
import math

import numpy as np
import jax
import jax.numpy as jnp
from jax import lax
from jax.experimental import pallas as pl
from jax.experimental.pallas import tpu as pltpu

f32 = jnp.float32
bf16 = jnp.bfloat16

N_DEV = 8
T = 4096
D = 1024
CHUNK = 64
MIX_TILE = 512
HEADS = 4
DH = 128
RW = HEADS * DH
PW = 512
GROUPS = 4
WINDOWS = (2, 4, 8, 16)
IN_W = 4 * RW + PW
D_FF = 2816
LN_EPS = 1e-5
RMS_EPS = 1e-6
ALPHA = 2.0 ** 0.25
K_SCALE = DH ** -0.5

ADAM_LR = 0.001
ADAM_B1 = 0.9
ADAM_B2 = 0.999
ADAM_EPS = 1e-08
ADAM_WD = 0.01
ADAM_STEP = 10

ROWS_IN, ROWS_OUT, ROWS_UP, ROWS_DOWN = IN_W // N_DEV, D // N_DEV, 2 * D_FF // N_DEV, D_FF // N_DEV

V7X_VMEM_LIMIT = 56 * 2 ** 20
HALO = 32

NT = (((1,), (1,)), ((), ()))
TN = (((0,), (0,)), ((), ()))
NN = (((1,), (0,)), ((), ()))


def _dot(a, b, dims=NN):
    return lax.dot_general(a, b, dims, preferred_element_type=f32)


def _const_spec(shape):
    zeros = (0,) * len(shape)
    return pl.BlockSpec(shape, lambda i: zeros, pipeline_mode=pl.Buffered(1))


def _sigmoid(x):
    return 0.5 * jnp.tanh(0.5 * x) + 0.5


def _decay_tables(tt):
    h = np.arange(HEADS, dtype=np.float64)
    log_gamma = np.log(1.0 - 2.0 ** (-5.0 - h)).astype(np.float32).astype(np.float64)[:, None, None]
    idx = np.arange(tt, dtype=np.float64)
    visible = (idx[None, :] // CHUNK) <= (idx[:, None] // CHUNK)
    mask = np.where(visible[None], np.exp(log_gamma * np.abs(idx[:, None] - idx[None, :])[None]), 0.0)
    qd = np.broadcast_to(np.exp(log_gamma * (idx[None, :, None] + 1.0)), (HEADS, tt, DH))
    kd = np.broadcast_to(np.exp(log_gamma * (tt - 1.0 - idx[None, :, None])), (HEADS, tt, DH))
    cd = np.exp(log_gamma[:, 0, 0] * tt)
    return (jnp.asarray(mask, f32), jnp.asarray(qd, f32), jnp.asarray(kd, f32), [float(c) for c in cd])


def _rope_tables():
    inv_freq = (10000.0 ** (-np.arange(0, DH, 2, dtype=np.float64) / DH)).astype(np.float32)
    ang = (np.arange(T, dtype=np.float32)[:, None] * inv_freq[None, :]).astype(np.float64)
    cos, sin = np.cos(ang), np.sin(ang)
    return (jnp.asarray(np.concatenate([cos, cos], axis=1), f32), jnp.asarray(np.concatenate([-sin, sin], axis=1), f32))


def _swap_halves(t):
    return pltpu.roll(t, DH // 2, axis=1)


def _mix_forward(x, w_in_t, cos, sin, dmat, qd, kd, cdec, w_pool, pool_scale, w_out, ln1_g, ln1_b, gather,
                 tt=MIX_TILE):
    n_tiles = T // tt
    n_g = len(gather)

    def body(x_ref, wint_ref, cos_ref, sin_ref, dmat_ref, qd_ref, kd_ref, wpool_ref, pscale_ref, wout_ref,
             g1_ref, b1_ref, *rest):
        gin, rest = rest[:n_g], rest[n_g:]
        qkv_ref, g_ref, oret_ref, states_ref, cat_ref, pooled_ref, xhat_ref, rstd_ref, x1b_ref = rest[:9]
        gout, (state_s, pext_s, tmp_s, *sems) = rest[9:9 + n_g], rest[9 + n_g:]
        i = pl.program_id(0)

        @pl.when(i == 0)
        def _():
            state_s[...] = jnp.zeros_like(state_s)
            pext_s[pl.ds(0, HALO), :] = jnp.zeros((HALO, PW), f32)
            _gather_start(gin, gout, *sems)

        @pl.when(i == n_tiles - 2)
        def _():
            _gather_forward(gin, gout, *sems)

        xb = x_ref[...].astype(bf16)
        cos_t, sin_t = cos_ref[...], sin_ref[...]
        for part in range(2):
            pr = _dot(xb, wint_ref[pl.ds(part * RW, RW), :], NT)
            for h in range(HEADS):
                t = pr[:, h * DH:(h + 1) * DH]
                r = t * cos_t + _swap_halves(t) * sin_t
                if part == 1:
                    r = r * K_SCALE
                qkv_ref[:, part * RW + h * DH: part * RW + (h + 1) * DH] = r.astype(bf16)
        qkv_ref[:, 2 * RW:3 * RW] = _dot(xb, wint_ref[pl.ds(2 * RW, RW), :], NT).astype(bf16)
        g_ref[...] = _dot(xb, wint_ref[pl.ds(3 * RW, RW), :], NT)
        pext_s[pl.ds(HALO, tt), :] = _dot(xb, wint_ref[pl.ds(4 * RW, PW), :], NT)

        for h in range(HEADS):
            q = qkv_ref[:, h * DH:(h + 1) * DH]
            k = qkv_ref[:, RW + h * DH: RW + (h + 1) * DH]
            v = qkv_ref[:, 2 * RW + h * DH: 2 * RW + (h + 1) * DH]
            s = _dot(q, k, NT) * dmat_ref[h]
            st = state_s[h]
            stb = st.astype(bf16)
            states_ref[0, h] = stb
            oret_ref[:, h * DH:(h + 1) * DH] = (_dot(s.astype(bf16), v)
                                               + _dot((q.astype(f32) * qd_ref[h]).astype(bf16), stb))
            state_s[h] = st * cdec[h] + _dot((k.astype(f32) * kd_ref[h]).astype(bf16), v, TN)

        for h in range(HEADS):
            sl = slice(h * DH, (h + 1) * DH)
            o = oret_ref[:, sl]
            r = lax.rsqrt(jnp.mean(o * o, axis=-1, keepdims=True) + RMS_EPS)
            gg = g_ref[:, sl]
            cat_ref[:, sl] = (o * r * (gg * _sigmoid(gg))).astype(bf16)

        pos1 = (i * tt + lax.broadcasted_iota(jnp.int32, (tt, 1), 0) + 1).astype(f32)
        for gi, w in enumerate(WINDOWS):
            sl = slice(gi * DH, (gi + 1) * DH)
            stages = int(math.log2(w))
            src = pext_s
            for s in range(stages):
                lo = HALO - 8 * (stages - 1 - s)
                n = tt + HALO - lo
                shift = 2 ** s
                val = src[pl.ds(lo, n), sl] + src[pl.ds(lo - shift, n), sl]
                if s == stages - 1:
                    wsum = val
                else:
                    tmp_s[pl.ds(lo, n), sl] = val
                    src = tmp_s
            p_g = pext_s[pl.ds(HALO, tt), sl]
            pooled = (wsum / jnp.minimum(pos1, float(w)) - p_g).astype(bf16)
            pooled_ref[:, sl] = pooled
            y = _dot(pooled, wpool_ref[gi]) * pscale_ref[:, sl]
            cat_ref[:, RW + gi * DH: RW + (gi + 1) * DH] = y.astype(bf16)
        pext_s[pl.ds(0, HALO), :] = pext_s[pl.ds(tt, HALO), :]

        z = ALPHA * x_ref[...] + _dot(cat_ref[...], wout_ref[...])
        mu = jnp.mean(z, axis=-1, keepdims=True)
        zc = z - mu
        rstd = lax.rsqrt(jnp.mean(zc * zc, axis=-1, keepdims=True) + LN_EPS)
        xhat = zc * rstd
        xhat_ref[...] = xhat
        rstd_ref[...] = rstd
        x1b_ref[...] = (xhat * g1_ref[...] + b1_ref[...]).astype(bf16)

        @pl.when(i == n_tiles - 1)
        def _():
            _gather_finish(gin, gout, *sems)

    tile = lambda w: pl.BlockSpec((tt, w), lambda i: (i, 0))
    hbm = pl.BlockSpec(memory_space=pltpu.HBM)
    out_shape = (
        jax.ShapeDtypeStruct((T, 3 * RW), bf16),
        jax.ShapeDtypeStruct((T, RW), f32),
        jax.ShapeDtypeStruct((T, RW), f32),
        jax.ShapeDtypeStruct((n_tiles, HEADS, DH, DH), bf16),
        jax.ShapeDtypeStruct((T, D), bf16),
        jax.ShapeDtypeStruct((T, PW), bf16),
        jax.ShapeDtypeStruct((T, D), f32),
        jax.ShapeDtypeStruct((T, 1), f32),
        jax.ShapeDtypeStruct((T, D), bf16),
    ) + tuple(jax.ShapeDtypeStruct((N_DEV,) + b.shape, b.dtype) for b in gather)
    return pl.pallas_call(
        body, name="mix_forward", grid=(n_tiles,), out_shape=out_shape,
        in_specs=[tile(D), _const_spec((IN_W, D)), tile(DH), tile(DH),
                  _const_spec((HEADS, tt, tt)), _const_spec((HEADS, tt, DH)), _const_spec((HEADS, tt, DH)),
                  _const_spec((GROUPS, DH, DH)), _const_spec((1, PW)), _const_spec((D, D)),
                  _const_spec((1, D)), _const_spec((1, D))] + [hbm] * n_g,
        out_specs=(tile(3 * RW), tile(RW), tile(RW),
                   pl.BlockSpec((1, HEADS, DH, DH), lambda i: (i, 0, 0, 0)),
                   tile(D), tile(PW), tile(D), tile(1), tile(D)) + (hbm,) * n_g,
        scratch_shapes=[pltpu.VMEM((HEADS, DH, DH), f32), pltpu.VMEM((tt + HALO, PW), f32),
                        pltpu.VMEM((tt + HALO, PW), f32)] + _gather_sems(n_g),
        compiler_params=pltpu.CompilerParams(dimension_semantics=("arbitrary",), vmem_limit_bytes=V7X_VMEM_LIMIT,
                                             collective_id=GATHER_BARRIER),
    )(x, w_in_t, cos, sin, dmat, qd, kd, w_pool, pool_scale, w_out, ln1_g, ln1_b, *gather)


def _ffn_forward_backward(xhat1, rstd1, ln1_g, ln1_b, w_up_t, conv_w, conv_b, w_down, ln2_g, ln2_b, target,
                          tt=256, widths=(512, 512, 512, 512, 512, 256)):
    n_tiles = T // tt
    assert sum(widths) == D_FF and all(w % 128 == 0 for w in widths)
    chunks = [(sum(widths[:c]), w) for c, w in enumerate(widths)]
    FH = 16
    hb = tt // FH

    def body(xhat_ref, halo_ref, rstd_ref, g1_ref, b1_ref, wupt_ref, cw_ref, cb_ref, wdown_ref, g2_ref, b2_ref, tgt_ref,
             dz1_ref, dz2b_ref, du_ref, f_ref, loss_ref, dg2_ref, db2_ref, dg1_ref, db1_ref, dcb_ref, dcw_ref,
             gext_s, val_s, dhext_s):
        i = pl.program_id(0)
        tile_idx = n_tiles - 1 - i

        def rd(ref, off, lo, w):
            return jnp.concatenate([ref[lo // 128 + k, pl.ds(off, tt), :] for k in range(w // 128)], axis=1)

        def wr(ref, lo, val):
            for k in range(val.shape[1] // 128):
                ref[lo // 128 + k, pl.ds(0, val.shape[0]), :] = val[:, k * 128:(k + 1) * 128]

        @pl.when(i == 0)
        def _():
            for r in (loss_ref, dg2_ref, db2_ref, dg1_ref, db1_ref, dcb_ref, dcw_ref):
                r[...] = jnp.zeros_like(r)
            dhext_s[:, pl.ds(tt, 8), :] = jnp.zeros((D_FF // 128, 8, 128), f32)

        g1, b1 = g1_ref[...], b1_ref[...]
        xhat = xhat_ref[...]
        x1 = xhat * g1 + b1
        x1b = x1.astype(bf16)
        x1h = ((halo_ref[...] * g1 + b1) * jnp.where(tile_idx == 0, 0.0, 1.0)).astype(bf16)
        x1ext = jnp.concatenate([x1h, x1b], axis=0)

        for lo, w in chunks:
            cs = slice(lo, lo + w)
            val = _dot(x1b, wupt_ref[pl.ds(lo, w), :], NT)
            gate_ext = _dot(x1ext, wupt_ref[pl.ds(D_FF + lo, w), :], NT)
            wr(gext_s, lo, gate_ext)
            hh = (cb_ref[:, cs] + cw_ref[0:1, cs] * rd(gext_s, FH - 2, lo, w) + cw_ref[1:2, cs] * rd(gext_s, FH - 1, lo, w)
                  + cw_ref[2:3, cs] * gate_ext[FH:])
            sg = _sigmoid(hh)
            act = hh * sg
            wr(dhext_s, lo, act)
            val_s[:, cs] = val * (sg + act * (1.0 - sg))
            f_ref[:, cs] = (act * val).astype(bf16)

        z = ALPHA * x1 + _dot(f_ref[...], wdown_ref[...])
        mu = jnp.mean(z, axis=-1, keepdims=True)
        zc = z - mu
        rstd2 = lax.rsqrt(jnp.mean(zc * zc, axis=-1, keepdims=True) + LN_EPS)
        xh2 = zc * rstd2
        diff = xh2 * g2_ref[...] + b2_ref[...] - tgt_ref[...]
        loss_ref[...] += 0.5 * jnp.sum(diff * diff) / D
        dy = diff * (1.0 / D)
        dg2_ref[...] += jnp.sum(dy * xh2, axis=0, keepdims=True)
        db2_ref[...] += jnp.sum(dy, axis=0, keepdims=True)
        dyg = dy * g2_ref[...]
        dz2 = rstd2 * (dyg - jnp.mean(dyg, axis=-1, keepdims=True) - xh2 * jnp.mean(dyg * xh2, axis=-1, keepdims=True))
        dz2b = dz2.astype(bf16)
        dz2b_ref[...] = dz2b

        for lo, w in chunks:
            cs = slice(lo, lo + w)
            df = _dot(dz2b, wdown_ref[pl.ds(lo, w), :], NT)
            dval = df * rd(dhext_s, 0, lo, w)
            dh = df * val_s[:, cs]
            wr(dhext_s, lo, dh)
            dh1, dh2, g0 = rd(dhext_s, 1, lo, w), rd(dhext_s, 2, lo, w), rd(gext_s, FH, lo, w)
            dcb_ref[:, cs] += jnp.sum(dh, axis=0, keepdims=True)
            dcw_ref[0:1, cs] += jnp.sum(dh2 * g0, axis=0, keepdims=True)
            dcw_ref[1:2, cs] += jnp.sum(dh1 * g0, axis=0, keepdims=True)
            dcw_ref[2:3, cs] += jnp.sum(dh * g0, axis=0, keepdims=True)
            dgate = cw_ref[2:3, cs] * dh + cw_ref[1:2, cs] * dh1 + cw_ref[0:1, cs] * dh2
            du_ref[:, cs] = dval.astype(bf16)
            du_ref[:, D_FF + lo: D_FF + lo + w] = dgate.astype(bf16)
        dhext_s[:, pl.ds(tt, 8), :] = dhext_s[:, pl.ds(0, 8), :]
        dx1 = ALPHA * dz2 + _dot(du_ref[...], wupt_ref[...])

        dg1_ref[...] += jnp.sum(dx1 * xhat, axis=0, keepdims=True)
        db1_ref[...] += jnp.sum(dx1, axis=0, keepdims=True)
        dxg = dx1 * g1
        dz1_ref[...] = rstd_ref[...] * (dxg - jnp.mean(dxg, axis=-1, keepdims=True)
                                        - xhat * jnp.mean(dxg * xhat, axis=-1, keepdims=True))

    rtile = lambda w: pl.BlockSpec((tt, w), lambda i: (n_tiles - 1 - i, 0))
    acc = lambda shape: pl.BlockSpec(shape, lambda i: (0, 0))
    out_shape = (
        jax.ShapeDtypeStruct((T, D), f32),
        jax.ShapeDtypeStruct((T, D), bf16),
        jax.ShapeDtypeStruct((T, 2 * D_FF), bf16),
        jax.ShapeDtypeStruct((T, D_FF), bf16),
        jax.ShapeDtypeStruct((8, 128), f32),
        jax.ShapeDtypeStruct((1, D), f32), jax.ShapeDtypeStruct((1, D), f32),
        jax.ShapeDtypeStruct((1, D), f32), jax.ShapeDtypeStruct((1, D), f32),
        jax.ShapeDtypeStruct((1, D_FF), f32), jax.ShapeDtypeStruct((3, D_FF), f32),
    )
    return pl.pallas_call(
        body, name="ffn_forward_backward", grid=(n_tiles,), out_shape=out_shape,
        in_specs=[rtile(D),
                  pl.BlockSpec((FH, D), lambda i: (jnp.maximum((n_tiles - 1 - i) * hb - 1, 0), 0)),
                  rtile(1), _const_spec((1, D)), _const_spec((1, D)), _const_spec((2 * D_FF, D)),
                  _const_spec((3, D_FF)), _const_spec((1, D_FF)), _const_spec((D_FF, D)),
                  _const_spec((1, D)), _const_spec((1, D)), rtile(D)],
        out_specs=(rtile(D), rtile(D), rtile(2 * D_FF), rtile(D_FF), acc((8, 128)),
                   acc((1, D)), acc((1, D)), acc((1, D)), acc((1, D)), acc((1, D_FF)), acc((3, D_FF))),
        scratch_shapes=[pltpu.VMEM((D_FF // 128, tt + FH, 128), f32), pltpu.VMEM((tt, D_FF), f32),
                        pltpu.VMEM((D_FF // 128, tt + 8, 128), f32)],
        compiler_params=pltpu.CompilerParams(dimension_semantics=("arbitrary",), vmem_limit_bytes=V7X_VMEM_LIMIT),
    )(xhat1, xhat1, rstd1, ln1_g, ln1_b, w_up_t, conv_w, conv_b, w_down, ln2_g, ln2_b, target)


def _mix_backward(dz1, w_out, qkv, g, oret, states, pooled, cat, cos, sin, dmat, qd, kd, cdec, w_pool, pool_scale, w_in_t,
                  exchange, tt=MIX_TILE):
    n_tiles = T // tt
    n_e = len(exchange)

    def body(dz1_ref, wout_ref, qkv_ref, g_ref, oret_ref, states_ref, pooled_ref, cat_ref, cos_ref, sin_ref, dmat_ref,
             qd_ref, kd_ref, wpool_ref, pscale_ref, wint_ref, *rest):
        ein, rest = rest[:n_e], rest[n_e:]
        dproj_ref, gx_ref, dwpool_ref, dpscale_ref, dwout_ref = rest[:5]
        eout, (dstate_s, dout_s, eext_s, tmp_s, dwout_s, *sems) = rest[5:5 + n_e], rest[5 + n_e:]
        i = pl.program_id(0)
        tile_idx = n_tiles - 1 - i

        @pl.when(i == 0)
        def _():
            dstate_s[...] = jnp.zeros_like(dstate_s)
            dwpool_ref[...] = jnp.zeros_like(dwpool_ref)
            dpscale_ref[...] = jnp.zeros_like(dpscale_ref)
            dwout_s[...] = jnp.zeros_like(dwout_s)
            eext_s[pl.ds(tt, HALO), :] = jnp.zeros((HALO, PW), f32)
            _chip_exchange_start(ein, eout, *sems)

        dz1 = dz1_ref[...]
        dz1b = dz1.astype(bf16)
        dcat = _dot(dz1b, wout_ref[...], NT)
        dwout_s[...] += _dot(cat_ref[...], dz1b, TN)

        pos1 = (tile_idx * tt + lax.broadcasted_iota(jnp.int32, (tt, 1), 0) + 1).astype(f32)
        for gi, w in enumerate(WINDOWS):
            sl = slice(gi * DH, (gi + 1) * DH)
            dpo = dcat[:, RW + gi * DH: RW + (gi + 1) * DH]
            pooled_g = pooled_ref[:, sl]
            ylin = _dot(pooled_g, wpool_ref[gi])
            dpscale_ref[:, sl] += jnp.sum(dpo * ylin, axis=0, keepdims=True)
            dpw = (dpo * pscale_ref[:, sl]).astype(bf16)
            dwpool_ref[gi] += _dot(pooled_g, dpw, TN)
            dpooled = _dot(dpw, wpool_ref[gi], NT)
            eext_s[pl.ds(0, tt), sl] = dpooled / jnp.minimum(pos1, float(w))
            stages = int(math.log2(w))
            src = eext_s
            for s in range(stages):
                n = tt + 8 * (stages - 1 - s)
                shift = 2 ** s
                val = src[pl.ds(0, n), sl] + src[pl.ds(shift, n), sl]
                if s == stages - 1:
                    wsum = val
                else:
                    tmp_s[pl.ds(0, n), sl] = val
                    src = tmp_s
            dproj_ref[:, 4 * RW + gi * DH: 4 * RW + (gi + 1) * DH] = (wsum - dpooled).astype(bf16)
        eext_s[pl.ds(tt, HALO), :] = eext_s[pl.ds(0, HALO), :]

        for h in range(HEADS):
            sl = slice(h * DH, (h + 1) * DH)
            dr = dcat[:, sl]
            o = oret_ref[:, sl]
            r = lax.rsqrt(jnp.mean(o * o, axis=-1, keepdims=True) + RMS_EPS)
            rn = o * r
            gg = g_ref[:, sl]
            sg = _sigmoid(gg)
            dproj_ref[:, 3 * RW + h * DH: 3 * RW + (h + 1) * DH] = (dr * rn * (sg * (1.0 + gg * (1.0 - sg)))).astype(bf16)
            drn = dr * (gg * sg)
            dout_s[:, sl] = (r * (drn - rn * jnp.mean(drn * rn, axis=-1, keepdims=True))).astype(bf16)

        cos_t, sin_t = cos_ref[...], sin_ref[...]
        for h in range(HEADS):
            q = qkv_ref[:, h * DH:(h + 1) * DH]
            k = qkv_ref[:, RW + h * DH: RW + (h + 1) * DH]
            v = qkv_ref[:, 2 * RW + h * DH: 2 * RW + (h + 1) * DH]
            do = dout_s[:, h * DH:(h + 1) * DH]
            stb = states_ref[0, h]
            dst = dstate_s[h]
            dstb = dst.astype(bf16)
            sb = (_dot(q, k, NT) * dmat_ref[h]).astype(bf16)
            dsb = (_dot(do, v, NT) * dmat_ref[h]).astype(bf16)
            dq = _dot(dsb, k) + _dot(do, stb, NT) * qd_ref[h]
            dk = _dot(dsb, q, TN) + _dot(v, dstb, NT) * kd_ref[h]
            dv = _dot(sb, do, TN) + _dot((k.astype(f32) * kd_ref[h]).astype(bf16), dstb)
            dstate_s[h] = dst * cdec[h] + _dot((q.astype(f32) * qd_ref[h]).astype(bf16), do, TN)
            dproj_ref[:, h * DH:(h + 1) * DH] = (dq * cos_t - _swap_halves(dq) * sin_t).astype(bf16)
            dproj_ref[:, RW + h * DH: RW + (h + 1) * DH] = ((dk * cos_t - _swap_halves(dk) * sin_t) * K_SCALE).astype(bf16)
            dproj_ref[:, 2 * RW + h * DH: 2 * RW + (h + 1) * DH] = dv.astype(bf16)

        gx_ref[...] = ALPHA * dz1 + _dot(dproj_ref[...], wint_ref[...])

        @pl.when(i == n_tiles - 1)
        def _():
            dwout_ref[...] = dwout_s[...].astype(bf16)
            _chip_exchange_finish(ein, eout, *sems)

    rtile = lambda w: pl.BlockSpec((tt, w), lambda i: (n_tiles - 1 - i, 0))
    hbm = pl.BlockSpec(memory_space=pltpu.HBM)
    out_shape = (
        jax.ShapeDtypeStruct((T, IN_W), bf16),
        jax.ShapeDtypeStruct((T, D), f32),
        jax.ShapeDtypeStruct((GROUPS, DH, DH), f32),
        jax.ShapeDtypeStruct((1, PW), f32),
        jax.ShapeDtypeStruct((D, D), bf16),
    ) + tuple(jax.ShapeDtypeStruct(e.shape, e.dtype) for e in exchange)
    return pl.pallas_call(
        body, name="mix_backward", grid=(n_tiles,), out_shape=out_shape,
        in_specs=[rtile(D), _const_spec((D, D)), rtile(3 * RW), rtile(RW), rtile(RW),
                  pl.BlockSpec((1, HEADS, DH, DH), lambda i: (n_tiles - 1 - i, 0, 0, 0)),
                  rtile(PW), rtile(D), rtile(DH), rtile(DH),
                  _const_spec((HEADS, tt, tt)), _const_spec((HEADS, tt, DH)), _const_spec((HEADS, tt, DH)),
                  _const_spec((GROUPS, DH, DH)), _const_spec((1, PW)), _const_spec((IN_W, D))] + [hbm] * n_e,
        out_specs=(rtile(IN_W), rtile(D), pl.BlockSpec((GROUPS, DH, DH), lambda i: (0, 0, 0)),
                   pl.BlockSpec((1, PW), lambda i: (0, 0)),
                   pl.BlockSpec((D, D), lambda i: (0, 0), pipeline_mode=pl.Buffered(1))) + (hbm,) * n_e,
        scratch_shapes=[pltpu.VMEM((HEADS, DH, DH), f32), pltpu.VMEM((tt, RW), bf16),
                        pltpu.VMEM((tt + HALO, PW), f32), pltpu.VMEM((tt + HALO, PW), f32),
                        pltpu.VMEM((D, D), f32)] + _chip_exchange_sems(n_e),
        compiler_params=pltpu.CompilerParams(dimension_semantics=("arbitrary",), vmem_limit_bytes=V7X_VMEM_LIMIT,
                                             collective_id=CHIP_BARRIER),
    )(dz1, w_out, qkv, g, oret, states, pooled, cat, cos, sin, dmat, qd, kd, w_pool, pool_scale, w_in_t, *exchange)


def _weight_grad(a, b, name, tm, exchange=(), tk=2048):
    m = a.shape[1]
    n_m, n_k, n_e = m // tm, T // tk, len(exchange)

    def body(a_ref, b_ref, *rest):
        ein, o_ref, eout, (acc_s, *sems) = rest[:n_e], rest[n_e], rest[n_e + 1:2 * n_e + 1], rest[2 * n_e + 1:]
        i, k = pl.program_id(0), pl.program_id(1)

        if n_e:
            @pl.when((i == 0) & (k == 0))
            def _():
                _chip_exchange_start(ein, eout, *sems)

        @pl.when(k == 0)
        def _():
            acc_s[...] = jnp.zeros_like(acc_s)

        acc_s[...] += _dot(a_ref[...], b_ref[pl.ds(pl.multiple_of(k * tk, tk), tk), :].astype(bf16), TN)

        @pl.when(k == n_k - 1)
        def _():
            o_ref[...] = acc_s[...].astype(bf16)

        if n_e:
            @pl.when((i == n_m - 1) & (k == n_k - 1))
            def _():
                _chip_exchange_finish(ein, eout, *sems)

    hbm = pl.BlockSpec(memory_space=pltpu.HBM)
    return pl.pallas_call(
        body, name=name, grid=(n_m, n_k),
        out_shape=(jax.ShapeDtypeStruct((m, D), bf16),) + tuple(jax.ShapeDtypeStruct(e.shape, e.dtype) for e in exchange),
        in_specs=[pl.BlockSpec((tk, tm), lambda i, k: (k, i)),
                  pl.BlockSpec((T, D), lambda i, k: (0, 0), pipeline_mode=pl.Buffered(1))] + [hbm] * n_e,
        out_specs=(pl.BlockSpec((tm, D), lambda i, k: (i, 0)),) + (hbm,) * n_e,
        scratch_shapes=[pltpu.VMEM((tm, D), f32)] + _chip_exchange_sems(n_e),
        compiler_params=pltpu.CompilerParams(dimension_semantics=("arbitrary", "arbitrary"),
                                             vmem_limit_bytes=V7X_VMEM_LIMIT,
                                             collective_id=CHIP_BARRIER if n_e else None),
    )(a, b, *exchange)


CHIP_FLIPS = ((1, 0), (0, 1), (1, 1))
PAIR_BARRIER, CHIP_BARRIER, GATHER_BARRIER = 0, 1, 2


def _barrier(peers):
    sem = pltpu.get_barrier_semaphore()
    for peer in peers:
        pl.semaphore_signal(sem, inc=1, device_id=peer, device_id_type=pl.DeviceIdType.MESH)
    pl.semaphore_wait(sem, len(peers))


def _me():
    return lax.axis_index("x"), lax.axis_index("y"), lax.axis_index("c")


def _chip(me, k):
    x, y, _ = me
    if k == 0:
        return x, y
    fx, fy = CHIP_FLIPS[k - 1]
    return (1 - x if fx else x), (1 - y if fy else y)


def _slot(x, y, c):
    return 4 * x + 2 * y + c


def _remote(src, dst, send_sem, recv_sem, to):
    return pltpu.make_async_remote_copy(src_ref=src, dst_ref=dst, send_sem=send_sem, recv_sem=recv_sem,
                                        device_id=to, device_id_type=pl.DeviceIdType.MESH)


def _gather_sems(n):
    return [pltpu.SemaphoreType.DMA((7, n)), pltpu.SemaphoreType.DMA((7, n)), pltpu.SemaphoreType.DMA((n,))] if n else []


def _gather_copy(k, j, gin, gout, send_sems, recv_sems, sending):
    x, y, c = _me()
    sibling, x_chip, y_chip, d_chip = (x, y, 1 - c), (1 - x, y), (x, 1 - y), (1 - x, 1 - y)
    south = c == 0
    passed_on = (jnp.where(south, 1 - x, x), jnp.where(south, y, 1 - y), c)
    src, to = gin[j], sibling
    if sending:
        block = {0: (x, y, c), 1: (x, y, c), 2: (x, y, c), 3: passed_on, 4: (*x_chip, c), 5: (*y_chip, c), 6: (*d_chip, c)}[k]
        to = {1: (*x_chip, c), 2: (*y_chip, c), 3: (jnp.where(south, x, 1 - x), jnp.where(south, 1 - y, y), c)}.get(k, sibling)
        if k >= 3:
            src = gout[j].at[_slot(*block)]
    else:
        block = {0: sibling, 1: (*x_chip, c), 2: (*y_chip, c), 3: (*d_chip, c), 4: (*x_chip, 1 - c), 5: (*y_chip, 1 - c),
                 6: (*d_chip, 1 - c)}[k]
    return _remote(src, gout[j].at[_slot(*block)], send_sems.at[k, j], recv_sems.at[k, j], to)


def _gather_do(ks, action, gin, gout, send_sems, recv_sems):
    for k in ks:
        for j in range(len(gin)):
            cp = _gather_copy(k, j, gin, gout, send_sems, recv_sems, action != "wait_recv")
            getattr(cp, action)()


def _gather_start(gin, gout, send_sems, recv_sems, local_sems):
    x, y, c = _me()
    _barrier([(x, y, 1 - c), (1 - x, y, c), (x, 1 - y, c)])
    for j in range(len(gin)):
        pltpu.make_async_copy(gin[j], gout[j].at[_slot(*_me())], local_sems.at[j]).start()
    _gather_do((0, 1, 2), "start", gin, gout, send_sems, recv_sems)


def _gather_forward(gin, gout, send_sems, recv_sems, local_sems):
    _gather_do((1, 2), "wait_recv", gin, gout, send_sems, recv_sems)
    _gather_do((3, 4, 5), "start", gin, gout, send_sems, recv_sems)


def _gather_finish(gin, gout, send_sems, recv_sems, local_sems):
    _gather_do((3,), "wait_recv", gin, gout, send_sems, recv_sems)
    _gather_do((6,), "start", gin, gout, send_sems, recv_sems)
    _gather_do((0, 4, 5, 6), "wait_recv", gin, gout, send_sems, recv_sems)
    _gather_do(range(7), "wait_send", gin, gout, send_sems, recv_sems)
    for j in range(len(gin)):
        pltpu.make_async_copy(gin[j], gout[j].at[_slot(*_me())], local_sems.at[j]).wait()


def _all_gather(blocks, name):
    n = len(blocks)

    def body(*refs):
        gin, gout, sems = refs[:n], refs[n:2 * n], refs[2 * n:]
        _gather_start(gin, gout, *sems)
        _gather_forward(gin, gout, *sems)
        _gather_finish(gin, gout, *sems)

    hbm = pl.BlockSpec(memory_space=pltpu.HBM)
    return pl.pallas_call(
        body, name=name,
        out_shape=tuple(jax.ShapeDtypeStruct((N_DEV,) + b.shape, b.dtype) for b in blocks),
        in_specs=[hbm] * n, out_specs=(hbm,) * n, scratch_shapes=_gather_sems(n),
        compiler_params=pltpu.CompilerParams(collective_id=GATHER_BARRIER),
    )(*blocks)


def _pair_reduce(parts, name):
    n = len(parts)

    def body(*refs):
        ins, own, others, landing, mine = (refs[k * n:(k + 1) * n] for k in range(5))
        send_sems, recv_sems, local_sems = refs[5 * n:]
        me = _me()
        x, y, c = me
        sibling = (x, y, 1 - c)
        _barrier([sibling])
        sends, loads = [], []
        for k in range(4):
            for j in range(n):
                cp = _remote(ins[j].at[_slot(*_chip(me, k), 1 - c)], landing[j].at[k], send_sems.at[k, j],
                             recv_sems.at[k, j], sibling)
                cp.start()
                sends.append(cp)
                ld = pltpu.make_async_copy(ins[j].at[_slot(*_chip(me, k), c)], mine[j].at[k], local_sems.at[k, j])
                ld.start()
                loads.append(ld)
        for k in range(4):
            for j in range(n):
                loads[k * n + j].wait()
                _remote(ins[j].at[0], landing[j].at[k], send_sems.at[k, j], recv_sems.at[k, j], sibling).wait_recv()
                total = mine[j][k].astype(f32) + landing[j][k].astype(f32)
                if k == 0:
                    own[j][...] = total.astype(own[j].dtype)
                else:
                    others[j][k - 1] = total.astype(others[j].dtype)
        for cp in sends:
            cp.wait_send()

    vm = pl.BlockSpec(memory_space=pltpu.VMEM)
    return pl.pallas_call(
        body, name=name,
        out_shape=tuple(jax.ShapeDtypeStruct(p.shape[1:], p.dtype) for p in parts)
        + tuple(jax.ShapeDtypeStruct((3,) + p.shape[1:], p.dtype) for p in parts),
        in_specs=[pl.BlockSpec(memory_space=pltpu.HBM)] * n, out_specs=(vm,) * (2 * n),
        scratch_shapes=[pltpu.VMEM((4,) + p.shape[1:], p.dtype) for p in parts] * 2
        + [pltpu.SemaphoreType.DMA((4, n)), pltpu.SemaphoreType.DMA((4, n)), pltpu.SemaphoreType.DMA((4, n))],
        compiler_params=pltpu.CompilerParams(vmem_limit_bytes=V7X_VMEM_LIMIT, collective_id=PAIR_BARRIER),
    )(*parts)


def _chip_exchange_sems(n):
    return [pltpu.SemaphoreType.DMA((3, n)), pltpu.SemaphoreType.DMA((3, n))] if n else []


def _chip_exchange_copy(k, j, ein, eout, send_sems, recv_sems):
    me = _me()
    return _remote(ein[j].at[k - 1], eout[j].at[k - 1], send_sems.at[k - 1, j], recv_sems.at[k - 1, j],
                   (*_chip(me, k), me[2]))


def _chip_exchange_start(ein, eout, send_sems, recv_sems):
    me = _me()
    _barrier([(*_chip(me, k), me[2]) for k in range(1, 4)])
    for k in range(1, 4):
        for j in range(len(ein)):
            _chip_exchange_copy(k, j, ein, eout, send_sems, recv_sems).start()


def _chip_exchange_finish(ein, eout, send_sems, recv_sems):
    for k in range(1, 4):
        for j in range(len(ein)):
            _chip_exchange_copy(k, j, ein, eout, send_sems, recv_sems).wait_recv()
    for k in range(1, 4):
        for j in range(len(ein)):
            _chip_exchange_copy(k, j, ein, eout, send_sems, recv_sems).wait_send()


def _chip_exchange(others, name):
    n = len(others)

    def body(*refs):
        ein, eout, sems = refs[:n], refs[n:2 * n], refs[2 * n:]
        _chip_exchange_start(ein, eout, *sems)
        _chip_exchange_finish(ein, eout, *sems)

    hbm = pl.BlockSpec(memory_space=pltpu.HBM)
    return pl.pallas_call(
        body, name=name, out_shape=tuple(jax.ShapeDtypeStruct(e.shape, e.dtype) for e in others),
        in_specs=[hbm] * n, out_specs=(hbm,) * n, scratch_shapes=_chip_exchange_sems(n),
        compiler_params=pltpu.CompilerParams(collective_id=CHIP_BARRIER),
    )(*others)


def _sum_parts(owns, arrived, name):
    n = len(owns)

    def body(*refs):
        for own, arr, out in zip(refs[:n], refs[n:2 * n], refs[2 * n:]):
            acc = own[...].astype(f32)
            for k in range(3):
                acc = acc + arr[k].astype(f32)
            out[...] = acc

    vm = pl.BlockSpec(memory_space=pltpu.VMEM)
    return pl.pallas_call(
        body, name=name, out_shape=tuple(jax.ShapeDtypeStruct(o.shape, f32) for o in owns),
        in_specs=[vm] * (2 * n), out_specs=(vm,) * n,
        compiler_params=pltpu.CompilerParams(vmem_limit_bytes=V7X_VMEM_LIMIT),
    )(*owns, *arrived)


ADAM_C1 = 1.0 / (1.0 - ADAM_B1 ** ADAM_STEP)
ADAM_C2 = 1.0 / (1.0 - ADAM_B2 ** ADAM_STEP)


def _adam_update(w, g, m, v):
    m = ADAM_B1 * m + (1.0 - ADAM_B1) * g
    v = ADAM_B2 * v + (1.0 - ADAM_B2) * (g * g)
    return -ADAM_LR * ((m * ADAM_C1) / (jnp.sqrt(v * ADAM_C2) + ADAM_EPS) + ADAM_WD * w), m, v


def _sum_adamw(own, arrived, w, m, v, name, steps):
    rows = own.shape[0]
    br = rows // steps

    def body(own_ref, arr_ref, w_ref, m_ref, v_ref, g_out, d_out, m_out, v_out):
        g = own_ref[...].astype(f32)
        for k in range(3):
            g = g + arr_ref[k].astype(f32)
        g_out[...] = g
        d_out[...], m_out[...], v_out[...] = _adam_update(w_ref[...], g, m_ref[...], v_ref[...])

    blk = pl.BlockSpec((br, D), lambda i: (i, 0))
    return pl.pallas_call(
        body, name=name, grid=(steps,), out_shape=(jax.ShapeDtypeStruct((rows, D), f32),) * 4,
        in_specs=[blk, pl.BlockSpec((3, br, D), lambda i: (0, i, 0)), blk, blk, blk], out_specs=(blk,) * 4,
        compiler_params=pltpu.CompilerParams(dimension_semantics=("parallel",), vmem_limit_bytes=V7X_VMEM_LIMIT),
    )(own, arrived, w, m, v)


def _adamw(ws, gs, ms, vs, name):
    n = len(ws)

    def body(*refs):
        w_r, g_r, m_r, v_r = (refs[k * n:(k + 1) * n] for k in range(4))
        d_o, m_o, v_o = (refs[(4 + k) * n:(5 + k) * n] for k in range(3))
        for j in range(n):
            d_o[j][...], m_o[j][...], v_o[j][...] = _adam_update(w_r[j][...], g_r[j][...], m_r[j][...], v_r[j][...])

    vm = pl.BlockSpec(memory_space=pltpu.VMEM)
    shapes = tuple(jax.ShapeDtypeStruct(w.shape, f32) for w in ws)
    return pl.pallas_call(
        body, name=name, out_shape=shapes * 3, in_specs=[vm] * (4 * n), out_specs=tuple([vm] * (3 * n)),
        compiler_params=pltpu.CompilerParams(vmem_limit_bytes=V7X_VMEM_LIMIT),
    )(*ws, *gs, *ms, *vs)


SMALL = (("w_pool", GROUPS * DH * DH), ("pool_scale", PW), ("ln1_g", D), ("ln1_b", D), ("conv_b", D_FF),
         ("ln2_g", D), ("ln2_b", D), ("conv_w", 3 * D_FF), ("loss", 1))
SMALL_ROWS = 640


def _pack(named):
    flat = jnp.concatenate([named[k].reshape(-1) for k, _ in SMALL])
    return jnp.pad(flat, (0, SMALL_ROWS * 128 - flat.shape[0])).reshape(SMALL_ROWS, 128)


def _unpack(packed):
    flat, out, at = packed.reshape(-1), {}, 0
    for k, size in SMALL:
        out[k] = flat[at:at + size]
        at += size
    return out


def kernel(x, w_in, w_pool, pool_scale, w_out, ln1_g, ln1_b, w_up, conv_w, conv_b, w_down, ln2_g, ln2_b, loss_target, m_w_in, m_w_pool, m_pool_scale, m_w_out, m_ln1_g, m_ln1_b, m_w_up, m_conv_w, m_conv_b, m_w_down, m_ln2_g, m_ln2_b, v_w_in, v_w_pool, v_pool_scale, v_w_out, v_ln1_g, v_ln1_b, v_w_up, v_conv_w, v_conv_b, v_w_down, v_ln2_g, v_ln2_b):
    me = 4 * lax.axis_index("x") + 2 * lax.axis_index("y") + lax.axis_index("c")
    x2, tgt = x[0], loss_target[0]

    g_in, g_out, g_cw = _all_gather([w_in[0].T.astype(bf16), w_out[0].astype(bf16), conv_w[0]], "gather_weights")
    w_in_t = g_in.reshape(IN_W, D)
    w_out_f = g_out.reshape(D, D)
    conv_w_f = jnp.transpose(g_cw, (1, 0, 2)).reshape(3, D_FF)
    w_pool_b = w_pool[0].astype(bf16)

    cos, sin = _rope_tables()
    dmat, qd, kd, cdec = _decay_tables(MIX_TILE)

    qkv, g, oret, states, cat, pooled, xhat1, rstd1, x1b, g_up, g_down = _mix_forward(
        x2, w_in_t, cos, sin, dmat, qd, kd, cdec, w_pool_b, pool_scale, w_out_f, ln1_g, ln1_b,
        gather=[w_up[0].T.astype(bf16), w_down[0].astype(bf16)])
    w_up_t = g_up.reshape(2 * D_FF, D)
    w_down_f = g_down.reshape(D_FF, D)
    dz1, dz2b, du, f, loss8, d_ln2_g, d_ln2_b, d_ln1_g, d_ln1_b, d_conv_b, d_conv_w = _ffn_forward_backward(
        xhat1, rstd1, ln1_g, ln1_b, w_up_t, conv_w_f, conv_b, w_down_f, ln2_g, ln2_b, tgt)

    (dw_down,) = _weight_grad(f, dz2b, "grad_w_down", tm=D_FF // 2)
    own_down, oth_down = _pair_reduce([dw_down.reshape(N_DEV, ROWS_DOWN, D)], "pair_reduce_down")
    dw_up_t, arr_down = _weight_grad(du, x1b, "grad_w_up", tm=D_FF // 2, exchange=[oth_down])
    own_up, oth_up = _pair_reduce([dw_up_t.reshape(N_DEV, ROWS_UP, D)], "pair_reduce_up")
    dproj, grad_x, d_w_pool, d_pool_scale, dw_out, arr_up = _mix_backward(
        dz1, w_out_f, qkv, g, oret, states, pooled, cat, cos, sin, dmat, qd, kd, cdec, w_pool_b, pool_scale, w_in_t,
        exchange=[oth_up])
    small = _pack({"w_pool": d_w_pool, "pool_scale": d_pool_scale, "ln1_g": d_ln1_g, "ln1_b": d_ln1_b,
                   "conv_b": d_conv_b, "ln2_g": d_ln2_g, "ln2_b": d_ln2_b, "conv_w": d_conv_w, "loss": loss8[0, :1]})
    own_out, own_small, oth_out, oth_small = _pair_reduce(
        [dw_out.reshape(N_DEV, ROWS_OUT, D), small.reshape(N_DEV, SMALL_ROWS // N_DEV, 128)], "pair_reduce_out")
    dw_in_t, arr_out, arr_small = _weight_grad(dproj, x2, "grad_w_in", tm=IN_W // 2, exchange=[oth_out, oth_small])
    own_in, oth_in = _pair_reduce([dw_in_t.reshape(N_DEV, ROWS_IN, D)], "pair_reduce_in")
    (arr_in,) = _chip_exchange([oth_in], "exchange_in")

    names = ["w_in", "w_pool", "pool_scale", "w_out", "ln1_g", "ln1_b", "w_up", "conv_w", "conv_b", "w_down",
             "ln2_g", "ln2_b"]
    w_d = dict(w_in=w_in, w_pool=w_pool, pool_scale=pool_scale, w_out=w_out, ln1_g=ln1_g, ln1_b=ln1_b, w_up=w_up,
               conv_w=conv_w, conv_b=conv_b, w_down=w_down, ln2_g=ln2_g, ln2_b=ln2_b)
    m_d = dict(w_in=m_w_in, w_pool=m_w_pool, pool_scale=m_pool_scale, w_out=m_w_out, ln1_g=m_ln1_g, ln1_b=m_ln1_b,
               w_up=m_w_up, conv_w=m_conv_w, conv_b=m_conv_b, w_down=m_w_down, ln2_g=m_ln2_g, ln2_b=m_ln2_b)
    v_d = dict(w_in=v_w_in, w_pool=v_w_pool, pool_scale=v_pool_scale, w_out=v_w_out, ln1_g=v_ln1_g, ln1_b=v_ln1_b,
               w_up=v_w_up, conv_w=v_conv_w, conv_b=v_conv_b, w_down=v_w_down, ln2_g=v_ln2_g, ln2_b=v_ln2_b)
    g_d, delta, new_m, new_v = {}, {}, {}, {}

    big = (("w_in", own_in, arr_in, True, 4), ("w_out", own_out, arr_out, False, 2),
           ("w_up", own_up, arr_up, True, 4), ("w_down", own_down, arr_down, False, 2))
    for k, own, arr, transposed, steps in big:
        lay = (lambda a: a[0].T) if transposed else (lambda a: a[0])
        back = (lambda a: a.T[None]) if transposed else (lambda a: a[None])
        res = _sum_adamw(own, arr, lay(w_d[k]), lay(m_d[k]), lay(v_d[k]), "adamw_" + k, steps)
        g_d[k], delta[k], new_m[k], new_v[k] = (back(r) for r in res)

    (small_piece,) = _sum_parts([own_small], [arr_small], "sum_small_grads")
    (gs_small,) = _all_gather([small_piece], "gather_small_grads")
    gsm = _unpack(gs_small)
    gsm["conv_w"] = lax.dynamic_slice(gsm["conv_w"].reshape(3, D_FF), (0, me * (D_FF // N_DEV)), (3, D_FF // N_DEV))
    two_d = lambda a: a.reshape(-1, a.shape[-1])
    group = [k for k in names if k not in g_d]
    for k in group:
        g_d[k] = gsm[k].reshape(w_d[k].shape)
    res = _adamw([two_d(w_d[k]) for k in group], [two_d(g_d[k]) for k in group], [two_d(m_d[k]) for k in group],
                 [two_d(v_d[k]) for k in group], "adamw_small")
    for j, k in enumerate(group):
        delta[k] = res[j].reshape(w_d[k].shape)
        new_m[k] = res[len(group) + j].reshape(w_d[k].shape)
        new_v[k] = res[2 * len(group) + j].reshape(w_d[k].shape)

    loss = gsm["loss"].reshape(())
    return (loss, grad_x[None], *[g_d[k] for k in names], *[delta[k] for k in names], *[new_m[k] for k in names],
            *[new_v[k] for k in names])
```

```python
import functools
import math

import numpy as np
import jax
import jax.numpy as jnp
from jax import lax
from jax.experimental import pallas as pl
from jax.experimental.pallas import tpu as pltpu

f32 = jnp.float32
bf16 = jnp.bfloat16

N_DEV = 8
T = 4096
D = 1024
CHUNK = 64
MIX_TILE = 512
RET_TILE = 256
HEADS = 4
DH = 128
RW = HEADS * DH
PW = 512
GROUPS = 4
WINDOWS = (2, 4, 8, 16)
IN_W = 4 * RW + PW
D_FF = 2816
LN_EPS = 1e-5
RMS_EPS = 1e-6
ALPHA = 2.0 ** 0.25
K_SCALE = DH ** -0.5

ADAM_LR = 0.001
ADAM_B1 = 0.9
ADAM_B2 = 0.999
ADAM_EPS = 1e-08
ADAM_WD = 0.01
ADAM_STEP = 10

ROWS_IN, ROWS_OUT, ROWS_UP, ROWS_DOWN = IN_W // N_DEV, D // N_DEV, 2 * D_FF // N_DEV, D_FF // N_DEV

V7X_VMEM_LIMIT = 56 * 2 ** 20
HALO = 32

NT = (((1,), (1,)), ((), ()))
TN = (((0,), (0,)), ((), ()))
NN = (((1,), (0,)), ((), ()))


def _dot(a, b, dims=NN):
    return lax.dot_general(a, b, dims, preferred_element_type=f32)


def _const_spec(shape):
    zeros = (0,) * len(shape)
    return pl.BlockSpec(shape, lambda i: zeros, pipeline_mode=pl.Buffered(1))


def _sigmoid(x):
    return 0.5 * jnp.tanh(0.5 * x) + 0.5


def _decay_tables(tt):
    h = np.arange(HEADS, dtype=np.float64)
    log_gamma = np.log(1.0 - 2.0 ** (-5.0 - h)).astype(np.float32).astype(np.float64)[:, None, None]
    idx = np.arange(tt, dtype=np.float64)
    visible = (idx[None, :] // CHUNK) <= (idx[:, None] // CHUNK)
    mask = np.where(visible[None], np.exp(log_gamma * np.abs(idx[:, None] - idx[None, :])[None]), 0.0)
    qd = np.broadcast_to(np.exp(log_gamma * (idx[None, :, None] + 1.0)), (HEADS, tt, DH))
    kd = np.broadcast_to(np.exp(log_gamma * (tt - 1.0 - idx[None, :, None])), (HEADS, tt, DH))
    cd = np.exp(log_gamma[:, 0, 0] * tt)
    return (jnp.asarray(mask, f32), jnp.asarray(qd, f32), jnp.asarray(kd, f32), [float(c) for c in cd])


def _rope_tables():
    inv_freq = (10000.0 ** (-np.arange(0, DH, 2, dtype=np.float64) / DH)).astype(np.float32)
    ang = (np.arange(T, dtype=np.float32)[:, None] * inv_freq[None, :]).astype(np.float64)
    cos, sin = np.cos(ang), np.sin(ang)
    return (jnp.asarray(np.concatenate([cos, cos], axis=1), f32), jnp.asarray(np.concatenate([-sin, sin], axis=1), f32))


def _swap_halves(t):
    return pltpu.roll(t, DH // 2, axis=1)


def _mix_forward(x, w_in_t, cos, sin, dmat, qd, kd, cdec, w_pool, pool_scale, w_out, ln1_g, ln1_b, gather,
                 tt=MIX_TILE):
    n_tiles = T // tt
    n_g = len(gather)

    def body(x_ref, wint_ref, cos_ref, sin_ref, dmat_ref, qd_ref, kd_ref, wpool_ref, pscale_ref, wout_ref,
             g1_ref, b1_ref, *rest):
        gin, rest = rest[:n_g], rest[n_g:]
        qkv_ref, g_ref, oret_ref, states_ref, cat_ref, pooled_ref, xhat_ref, rstd_ref, x1b_ref = rest[:9]
        gout, (state_s, pext_s, tmp_s, *sems) = rest[9:9 + n_g], rest[9 + n_g:]
        i = pl.program_id(0)

        @pl.when(i == 0)
        def _():
            state_s[...] = jnp.zeros_like(state_s)
            pext_s[pl.ds(0, HALO), :] = jnp.zeros((HALO, PW), f32)
            _gather_start(gin, gout, *sems)

        @pl.when(i == n_tiles - 2)
        def _():
            _gather_forward(gin, gout, *sems)

        xb = x_ref[...].astype(bf16)
        cos_t, sin_t = cos_ref[...], sin_ref[...]
        for part in range(2):
            pr = _dot(xb, wint_ref[pl.ds(part * RW, RW), :], NT)
            for h in range(HEADS):
                t = pr[:, h * DH:(h + 1) * DH]
                r = t * cos_t + _swap_halves(t) * sin_t
                if part == 1:
                    r = r * K_SCALE
                qkv_ref[:, part * RW + h * DH: part * RW + (h + 1) * DH] = r.astype(bf16)
        qkv_ref[:, 2 * RW:3 * RW] = _dot(xb, wint_ref[pl.ds(2 * RW, RW), :], NT).astype(bf16)
        g_ref[...] = _dot(xb, wint_ref[pl.ds(3 * RW, RW), :], NT)
        pext_s[pl.ds(HALO, tt), :] = _dot(xb, wint_ref[pl.ds(4 * RW, PW), :], NT)

        for sub in range(tt // RET_TILE):
            rows = pl.ds(sub * RET_TILE, RET_TILE)
            for h in range(HEADS):
                q = qkv_ref[rows, h * DH:(h + 1) * DH]
                k = qkv_ref[rows, RW + h * DH: RW + (h + 1) * DH]
                v = qkv_ref[rows, 2 * RW + h * DH: 2 * RW + (h + 1) * DH]
                s = _dot(q, k, NT) * dmat_ref[h]
                st = state_s[h]
                stb = st.astype(bf16)
                states_ref[sub, h] = stb
                oret_ref[rows, h * DH:(h + 1) * DH] = (_dot(s.astype(bf16), v)
                                                      + _dot((q.astype(f32) * qd_ref[h]).astype(bf16), stb))
                state_s[h] = st * cdec[h] + _dot((k.astype(f32) * kd_ref[h]).astype(bf16), v, TN)

        for h in range(HEADS):
            sl = slice(h * DH, (h + 1) * DH)
            o = oret_ref[:, sl]
            r = lax.rsqrt(jnp.mean(o * o, axis=-1, keepdims=True) + RMS_EPS)
            gg = g_ref[:, sl]
            cat_ref[:, sl] = (o * r * (gg * _sigmoid(gg))).astype(bf16)

        pos1 = (i * tt + lax.broadcasted_iota(jnp.int32, (tt, 1), 0) + 1).astype(f32)
        for gi, w in enumerate(WINDOWS):
            sl = slice(gi * DH, (gi + 1) * DH)
            stages = int(math.log2(w))
            src = pext_s
            for s in range(stages):
                lo = HALO - 8 * (stages - 1 - s)
                n = tt + HALO - lo
                shift = 2 ** s
                val = src[pl.ds(lo, n), sl] + src[pl.ds(lo - shift, n), sl]
                if s == stages - 1:
                    wsum = val
                else:
                    tmp_s[pl.ds(lo, n), sl] = val
                    src = tmp_s
            p_g = pext_s[pl.ds(HALO, tt), sl]
            pooled = (wsum / jnp.minimum(pos1, float(w)) - p_g).astype(bf16)
            pooled_ref[:, sl] = pooled
            y = _dot(pooled, wpool_ref[gi]) * pscale_ref[:, sl]
            cat_ref[:, RW + gi * DH: RW + (gi + 1) * DH] = y.astype(bf16)
        pext_s[pl.ds(0, HALO), :] = pext_s[pl.ds(tt, HALO), :]

        z = ALPHA * x_ref[...] + _dot(cat_ref[...], wout_ref[...])
        mu = jnp.mean(z, axis=-1, keepdims=True)
        zc = z - mu
        rstd = lax.rsqrt(jnp.mean(zc * zc, axis=-1, keepdims=True) + LN_EPS)
        xhat = zc * rstd
        xhat_ref[...] = xhat
        rstd_ref[...] = rstd
        x1b_ref[...] = (xhat * g1_ref[...] + b1_ref[...]).astype(bf16)

        @pl.when(i == n_tiles - 1)
        def _():
            _gather_finish(gin, gout, *sems)

    tile = lambda w: pl.BlockSpec((tt, w), lambda i: (i, 0))
    hbm = pl.BlockSpec(memory_space=pltpu.HBM)
    out_shape = (
        jax.ShapeDtypeStruct((T, 3 * RW), bf16),
        jax.ShapeDtypeStruct((T, RW), f32),
        jax.ShapeDtypeStruct((T, RW), f32),
        jax.ShapeDtypeStruct((T // RET_TILE, HEADS, DH, DH), bf16),
        jax.ShapeDtypeStruct((T, D), bf16),
        jax.ShapeDtypeStruct((T, PW), bf16),
        jax.ShapeDtypeStruct((T, D), f32),
        jax.ShapeDtypeStruct((T, 1), f32),
        jax.ShapeDtypeStruct((T, D), bf16),
    ) + tuple(jax.ShapeDtypeStruct((N_DEV,) + b.shape, b.dtype) for b in gather)
    return pl.pallas_call(
        body, name="mix_forward", grid=(n_tiles,), out_shape=out_shape,
        in_specs=[tile(D), _const_spec((IN_W, D)), tile(DH), tile(DH),
                  _const_spec((HEADS, RET_TILE, RET_TILE)), _const_spec((HEADS, RET_TILE, DH)),
                  _const_spec((HEADS, RET_TILE, DH)),
                  _const_spec((GROUPS, DH, DH)), _const_spec((1, PW)), _const_spec((D, D)),
                  _const_spec((1, D)), _const_spec((1, D))] + [hbm] * n_g,
        out_specs=(tile(3 * RW), tile(RW), tile(RW),
                   pl.BlockSpec((tt // RET_TILE, HEADS, DH, DH), lambda i: (i, 0, 0, 0)),
                   tile(D), tile(PW), tile(D), tile(1), tile(D)) + (hbm,) * n_g,
        scratch_shapes=[pltpu.VMEM((HEADS, DH, DH), f32), pltpu.VMEM((tt + HALO, PW), f32),
                        pltpu.VMEM((tt + HALO, PW), f32)] + _gather_sems(n_g),
        compiler_params=pltpu.CompilerParams(dimension_semantics=("arbitrary",), vmem_limit_bytes=V7X_VMEM_LIMIT,
                                             collective_id=GATHER_BARRIER),
    )(x, w_in_t, cos, sin, dmat, qd, kd, w_pool, pool_scale, w_out, ln1_g, ln1_b, *gather)


def _ffn_forward_backward(xhat1, rstd1, ln1_g, ln1_b, w_up_t, conv_w, conv_b, w_down, ln2_g, ln2_b, target,
                          tt=256, widths=(512, 512, 512, 512, 512, 256)):
    n_tiles = T // tt
    assert sum(widths) == D_FF and all(w % 128 == 0 for w in widths)
    chunks = [(sum(widths[:c]), w) for c, w in enumerate(widths)]
    FH = 16
    hb = tt // FH

    def body(xhat_ref, halo_ref, rstd_ref, g1_ref, b1_ref, wupt_ref, cw_ref, cb_ref, wdown_ref, g2_ref, b2_ref, tgt_ref,
             dz1_ref, dz2b_ref, du_ref, f_ref, loss_ref, dg2_ref, db2_ref, dg1_ref, db1_ref, dcb_ref, dcw_ref,
             gext_s, val_s, dhext_s):
        i = pl.program_id(0)
        tile_idx = n_tiles - 1 - i

        def rd(ref, off, lo, w):
            return jnp.concatenate([ref[lo // 128 + k, pl.ds(off, tt), :] for k in range(w // 128)], axis=1)

        def wr(ref, lo, val):
            for k in range(val.shape[1] // 128):
                ref[lo // 128 + k, pl.ds(0, val.shape[0]), :] = val[:, k * 128:(k + 1) * 128]

        @pl.when(i == 0)
        def _():
            for r in (loss_ref, dg2_ref, db2_ref, dg1_ref, db1_ref, dcb_ref, dcw_ref):
                r[...] = jnp.zeros_like(r)
            dhext_s[:, pl.ds(tt, 8), :] = jnp.zeros((D_FF // 128, 8, 128), f32)

        g1, b1 = g1_ref[...], b1_ref[...]
        xhat = xhat_ref[...]
        x1 = xhat * g1 + b1
        x1b = x1.astype(bf16)
        x1h = ((halo_ref[...] * g1 + b1) * jnp.where(tile_idx == 0, 0.0, 1.0)).astype(bf16)
        x1ext = jnp.concatenate([x1h, x1b], axis=0)

        for lo, w in chunks:
            cs = slice(lo, lo + w)
            val = _dot(x1b, wupt_ref[pl.ds(lo, w), :], NT)
            gate_ext = _dot(x1ext, wupt_ref[pl.ds(D_FF + lo, w), :], NT)
            wr(gext_s, lo, gate_ext)
            hh = (cb_ref[:, cs] + cw_ref[0:1, cs] * rd(gext_s, FH - 2, lo, w) + cw_ref[1:2, cs] * rd(gext_s, FH - 1, lo, w)
                  + cw_ref[2:3, cs] * gate_ext[FH:])
            sg = _sigmoid(hh)
            act = hh * sg
            wr(dhext_s, lo, act)
            val_s[:, cs] = val * (sg + act * (1.0 - sg))
            f_ref[:, cs] = (act * val).astype(bf16)

        z = ALPHA * x1 + _dot(f_ref[...], wdown_ref[...])
        mu = jnp.mean(z, axis=-1, keepdims=True)
        zc = z - mu
        rstd2 = lax.rsqrt(jnp.mean(zc * zc, axis=-1, keepdims=True) + LN_EPS)
        xh2 = zc * rstd2
        diff = xh2 * g2_ref[...] + b2_ref[...] - tgt_ref[...]
        loss_ref[...] += 0.5 * jnp.sum(diff * diff) / D
        dy = diff * (1.0 / D)
        dg2_ref[...] += jnp.sum(dy * xh2, axis=0, keepdims=True)
        db2_ref[...] += jnp.sum(dy, axis=0, keepdims=True)
        dyg = dy * g2_ref[...]
        dz2 = rstd2 * (dyg - jnp.mean(dyg, axis=-1, keepdims=True) - xh2 * jnp.mean(dyg * xh2, axis=-1, keepdims=True))
        dz2b = dz2.astype(bf16)
        dz2b_ref[...] = dz2b

        for lo, w in chunks:
            cs = slice(lo, lo + w)
            df = _dot(dz2b, wdown_ref[pl.ds(lo, w), :], NT)
            dval = df * rd(dhext_s, 0, lo, w)
            dh = df * val_s[:, cs]
            wr(dhext_s, lo, dh)
            dh1, dh2, g0 = rd(dhext_s, 1, lo, w), rd(dhext_s, 2, lo, w), rd(gext_s, FH, lo, w)
            dcb_ref[:, cs] += jnp.sum(dh, axis=0, keepdims=True)
            dcw_ref[0:1, cs] += jnp.sum(dh2 * g0, axis=0, keepdims=True)
            dcw_ref[1:2, cs] += jnp.sum(dh1 * g0, axis=0, keepdims=True)
            dcw_ref[2:3, cs] += jnp.sum(dh * g0, axis=0, keepdims=True)
            dgate = cw_ref[2:3, cs] * dh + cw_ref[1:2, cs] * dh1 + cw_ref[0:1, cs] * dh2
            du_ref[:, cs] = dval.astype(bf16)
            du_ref[:, D_FF + lo: D_FF + lo + w] = dgate.astype(bf16)
        dhext_s[:, pl.ds(tt, 8), :] = dhext_s[:, pl.ds(0, 8), :]
        dx1 = ALPHA * dz2 + _dot(du_ref[...], wupt_ref[...])

        dg1_ref[...] += jnp.sum(dx1 * xhat, axis=0, keepdims=True)
        db1_ref[...] += jnp.sum(dx1, axis=0, keepdims=True)
        dxg = dx1 * g1
        dz1_ref[...] = rstd_ref[...] * (dxg - jnp.mean(dxg, axis=-1, keepdims=True)
                                        - xhat * jnp.mean(dxg * xhat, axis=-1, keepdims=True))

    rtile = lambda w: pl.BlockSpec((tt, w), lambda i: (n_tiles - 1 - i, 0))
    acc = lambda shape: pl.BlockSpec(shape, lambda i: (0, 0))
    out_shape = (
        jax.ShapeDtypeStruct((T, D), f32),
        jax.ShapeDtypeStruct((T, D), bf16),
        jax.ShapeDtypeStruct((T, 2 * D_FF), bf16),
        jax.ShapeDtypeStruct((T, D_FF), bf16),
        jax.ShapeDtypeStruct((8, 128), f32),
        jax.ShapeDtypeStruct((1, D), f32), jax.ShapeDtypeStruct((1, D), f32),
        jax.ShapeDtypeStruct((1, D), f32), jax.ShapeDtypeStruct((1, D), f32),
        jax.ShapeDtypeStruct((1, D_FF), f32), jax.ShapeDtypeStruct((3, D_FF), f32),
    )
    return pl.pallas_call(
        body, name="ffn_forward_backward", grid=(n_tiles,), out_shape=out_shape,
        in_specs=[rtile(D),
                  pl.BlockSpec((FH, D), lambda i: (jnp.maximum((n_tiles - 1 - i) * hb - 1, 0), 0)),
                  rtile(1), _const_spec((1, D)), _const_spec((1, D)), _const_spec((2 * D_FF, D)),
                  _const_spec((3, D_FF)), _const_spec((1, D_FF)), _const_spec((D_FF, D)),
                  _const_spec((1, D)), _const_spec((1, D)), rtile(D)],
        out_specs=(rtile(D), rtile(D), rtile(2 * D_FF), rtile(D_FF), acc((8, 128)),
                   acc((1, D)), acc((1, D)), acc((1, D)), acc((1, D)), acc((1, D_FF)), acc((3, D_FF))),
        scratch_shapes=[pltpu.VMEM((D_FF // 128, tt + FH, 128), f32), pltpu.VMEM((tt, D_FF), f32),
                        pltpu.VMEM((D_FF // 128, tt + 8, 128), f32)],
        compiler_params=pltpu.CompilerParams(dimension_semantics=("arbitrary",), vmem_limit_bytes=V7X_VMEM_LIMIT),
    )(xhat1, xhat1, rstd1, ln1_g, ln1_b, w_up_t, conv_w, conv_b, w_down, ln2_g, ln2_b, target)


def _mix_backward(dz1, w_out, qkv, g, oret, states, pooled, cat, cos, sin, dmat, qd, kd, cdec, w_pool, pool_scale, w_in_t,
                  exchange, tt=MIX_TILE):
    n_tiles = T // tt
    n_e = len(exchange)

    def body(dz1_ref, wout_ref, qkv_ref, g_ref, oret_ref, states_ref, pooled_ref, cat_ref, cos_ref, sin_ref, dmat_ref,
             qd_ref, kd_ref, wpool_ref, pscale_ref, wint_ref, *rest):
        ein, rest = rest[:n_e], rest[n_e:]
        dproj_ref, gx_ref, dwpool_ref, dpscale_ref, dwout_ref = rest[:5]
        eout, (dstate_s, dout_s, eext_s, tmp_s, dwout_s, *sems) = rest[5:5 + n_e], rest[5 + n_e:]
        i = pl.program_id(0)
        tile_idx = n_tiles - 1 - i

        @pl.when(i == 0)
        def _():
            dstate_s[...] = jnp.zeros_like(dstate_s)
            dwpool_ref[...] = jnp.zeros_like(dwpool_ref)
            dpscale_ref[...] = jnp.zeros_like(dpscale_ref)
            dwout_s[...] = jnp.zeros_like(dwout_s)
            eext_s[pl.ds(tt, HALO), :] = jnp.zeros((HALO, PW), f32)
            _chip_exchange_start(ein, eout, *sems)

        dz1 = dz1_ref[...]
        dz1b = dz1.astype(bf16)
        dcat = _dot(dz1b, wout_ref[...], NT)
        dwout_s[...] += _dot(cat_ref[...], dz1b, TN)

        pos1 = (tile_idx * tt + lax.broadcasted_iota(jnp.int32, (tt, 1), 0) + 1).astype(f32)
        for gi, w in enumerate(WINDOWS):
            sl = slice(gi * DH, (gi + 1) * DH)
            dpo = dcat[:, RW + gi * DH: RW + (gi + 1) * DH]
            pooled_g = pooled_ref[:, sl]
            ylin = _dot(pooled_g, wpool_ref[gi])
            dpscale_ref[:, sl] += jnp.sum(dpo * ylin, axis=0, keepdims=True)
            dpw = (dpo * pscale_ref[:, sl]).astype(bf16)
            dwpool_ref[gi] += _dot(pooled_g, dpw, TN)
            dpooled = _dot(dpw, wpool_ref[gi], NT)
            eext_s[pl.ds(0, tt), sl] = dpooled / jnp.minimum(pos1, float(w))
            stages = int(math.log2(w))
            src = eext_s
            for s in range(stages):
                n = tt + 8 * (stages - 1 - s)
                shift = 2 ** s
                val = src[pl.ds(0, n), sl] + src[pl.ds(shift, n), sl]
                if s == stages - 1:
                    wsum = val
                else:
                    tmp_s[pl.ds(0, n), sl] = val
                    src = tmp_s
            dproj_ref[:, 4 * RW + gi * DH: 4 * RW + (gi + 1) * DH] = (wsum - dpooled).astype(bf16)
        eext_s[pl.ds(tt, HALO), :] = eext_s[pl.ds(0, HALO), :]

        for h in range(HEADS):
            sl = slice(h * DH, (h + 1) * DH)
            dr = dcat[:, sl]
            o = oret_ref[:, sl]
            r = lax.rsqrt(jnp.mean(o * o, axis=-1, keepdims=True) + RMS_EPS)
            rn = o * r
            gg = g_ref[:, sl]
            sg = _sigmoid(gg)
            dproj_ref[:, 3 * RW + h * DH: 3 * RW + (h + 1) * DH] = (dr * rn * (sg * (1.0 + gg * (1.0 - sg)))).astype(bf16)
            drn = dr * (gg * sg)
            dout_s[:, sl] = (r * (drn - rn * jnp.mean(drn * rn, axis=-1, keepdims=True))).astype(bf16)

        for sub in reversed(range(tt // RET_TILE)):
            rows = pl.ds(sub * RET_TILE, RET_TILE)
            cos_t, sin_t = cos_ref[rows, :], sin_ref[rows, :]
            for h in range(HEADS):
                q = qkv_ref[rows, h * DH:(h + 1) * DH]
                k = qkv_ref[rows, RW + h * DH: RW + (h + 1) * DH]
                v = qkv_ref[rows, 2 * RW + h * DH: 2 * RW + (h + 1) * DH]
                do = dout_s[rows, h * DH:(h + 1) * DH]
                stb = states_ref[sub, h]
                dst = dstate_s[h]
                dstb = dst.astype(bf16)
                sb = (_dot(q, k, NT) * dmat_ref[h]).astype(bf16)
                dsb = (_dot(do, v, NT) * dmat_ref[h]).astype(bf16)
                dq = _dot(dsb, k) + _dot(do, stb, NT) * qd_ref[h]
                dk = _dot(dsb, q, TN) + _dot(v, dstb, NT) * kd_ref[h]
                dv = _dot(sb, do, TN) + _dot((k.astype(f32) * kd_ref[h]).astype(bf16), dstb)
                dstate_s[h] = dst * cdec[h] + _dot((q.astype(f32) * qd_ref[h]).astype(bf16), do, TN)
                dproj_ref[rows, h * DH:(h + 1) * DH] = (dq * cos_t - _swap_halves(dq) * sin_t).astype(bf16)
                dproj_ref[rows, RW + h * DH: RW + (h + 1) * DH] = (
                    (dk * cos_t - _swap_halves(dk) * sin_t) * K_SCALE).astype(bf16)
                dproj_ref[rows, 2 * RW + h * DH: 2 * RW + (h + 1) * DH] = dv.astype(bf16)

        gx_ref[...] = ALPHA * dz1 + _dot(dproj_ref[...], wint_ref[...])

        @pl.when(i == n_tiles - 1)
        def _():
            dwout_ref[...] = dwout_s[...].astype(bf16)
            _chip_exchange_finish(ein, eout, *sems)

    rtile = lambda w: pl.BlockSpec((tt, w), lambda i: (n_tiles - 1 - i, 0))
    hbm = pl.BlockSpec(memory_space=pltpu.HBM)
    out_shape = (
        jax.ShapeDtypeStruct((T, IN_W), bf16),
        jax.ShapeDtypeStruct((T, D), f32),
        jax.ShapeDtypeStruct((GROUPS, DH, DH), f32),
        jax.ShapeDtypeStruct((1, PW), f32),
        jax.ShapeDtypeStruct((D, D), bf16),
    ) + tuple(jax.ShapeDtypeStruct(e.shape, e.dtype) for e in exchange)
    return pl.pallas_call(
        body, name="mix_backward", grid=(n_tiles,), out_shape=out_shape,
        in_specs=[rtile(D), _const_spec((D, D)), rtile(3 * RW), rtile(RW), rtile(RW),
                  pl.BlockSpec((tt // RET_TILE, HEADS, DH, DH), lambda i: (n_tiles - 1 - i, 0, 0, 0)),
                  rtile(PW), rtile(D), rtile(DH), rtile(DH),
                  _const_spec((HEADS, RET_TILE, RET_TILE)), _const_spec((HEADS, RET_TILE, DH)),
                  _const_spec((HEADS, RET_TILE, DH)),
                  _const_spec((GROUPS, DH, DH)), _const_spec((1, PW)), _const_spec((IN_W, D))] + [hbm] * n_e,
        out_specs=(rtile(IN_W), rtile(D), pl.BlockSpec((GROUPS, DH, DH), lambda i: (0, 0, 0)),
                   pl.BlockSpec((1, PW), lambda i: (0, 0)),
                   pl.BlockSpec((D, D), lambda i: (0, 0), pipeline_mode=pl.Buffered(1))) + (hbm,) * n_e,
        scratch_shapes=[pltpu.VMEM((HEADS, DH, DH), f32), pltpu.VMEM((tt, RW), bf16),
                        pltpu.VMEM((tt + HALO, PW), f32), pltpu.VMEM((tt + HALO, PW), f32),
                        pltpu.VMEM((D, D), f32)] + _chip_exchange_sems(n_e),
        compiler_params=pltpu.CompilerParams(dimension_semantics=("arbitrary",), vmem_limit_bytes=V7X_VMEM_LIMIT,
                                             collective_id=CHIP_BARRIER),
    )(dz1, w_out, qkv, g, oret, states, pooled, cat, cos, sin, dmat, qd, kd, w_pool, pool_scale, w_in_t, *exchange)


def _weight_grad(a, b, name, tm, exchange=(), tk=2048):
    m = a.shape[1]
    n_m, n_k, n_e = m // tm, T // tk, len(exchange)

    def body(a_ref, b_ref, *rest):
        ein, o_ref, eout, (acc_s, *sems) = rest[:n_e], rest[n_e], rest[n_e + 1:2 * n_e + 1], rest[2 * n_e + 1:]
        i, k = pl.program_id(0), pl.program_id(1)

        if n_e:
            @pl.when((i == 0) & (k == 0))
            def _():
                _chip_exchange_start(ein, eout, *sems)

        @pl.when(k == 0)
        def _():
            acc_s[...] = jnp.zeros_like(acc_s)

        acc_s[...] += _dot(a_ref[...], b_ref[pl.ds(pl.multiple_of(k * tk, tk), tk), :].astype(bf16), TN)

        @pl.when(k == n_k - 1)
        def _():
            o_ref[...] = acc_s[...].astype(bf16)

        if n_e:
            @pl.when((i == n_m - 1) & (k == n_k - 1))
            def _():
                _chip_exchange_finish(ein, eout, *sems)

    hbm = pl.BlockSpec(memory_space=pltpu.HBM)
    return pl.pallas_call(
        body, name=name, grid=(n_m, n_k),
        out_shape=(jax.ShapeDtypeStruct((m, D), bf16),) + tuple(jax.ShapeDtypeStruct(e.shape, e.dtype) for e in exchange),
        in_specs=[pl.BlockSpec((tk, tm), lambda i, k: (k, i)),
                  pl.BlockSpec((T, D), lambda i, k: (0, 0), pipeline_mode=pl.Buffered(1))] + [hbm] * n_e,
        out_specs=(pl.BlockSpec((tm, D), lambda i, k: (i, 0)),) + (hbm,) * n_e,
        scratch_shapes=[pltpu.VMEM((tm, D), f32)] + _chip_exchange_sems(n_e),
        compiler_params=pltpu.CompilerParams(dimension_semantics=("arbitrary", "arbitrary"),
                                             vmem_limit_bytes=V7X_VMEM_LIMIT,
                                             collective_id=CHIP_BARRIER if n_e else None),
    )(a, b, *exchange)


CHIP_FLIPS = ((1, 0), (0, 1), (1, 1))
PAIR_BARRIER, CHIP_BARRIER, GATHER_BARRIER = 0, 1, 2


def _barrier(peers):
    sem = pltpu.get_barrier_semaphore()
    for peer in peers:
        pl.semaphore_signal(sem, inc=1, device_id=peer, device_id_type=pl.DeviceIdType.MESH)
    pl.semaphore_wait(sem, len(peers))


def _me():
    return lax.axis_index("x"), lax.axis_index("y"), lax.axis_index("c")


def _chip(me, k):
    x, y, _ = me
    if k == 0:
        return x, y
    fx, fy = CHIP_FLIPS[k - 1]
    return (1 - x if fx else x), (1 - y if fy else y)


def _slot(x, y, c):
    return 4 * x + 2 * y + c


def _remote(src, dst, send_sem, recv_sem, to):
    return pltpu.make_async_remote_copy(src_ref=src, dst_ref=dst, send_sem=send_sem, recv_sem=recv_sem,
                                        device_id=to, device_id_type=pl.DeviceIdType.MESH)


def _gather_sems(n):
    return [pltpu.SemaphoreType.DMA((7, n)), pltpu.SemaphoreType.DMA((7, n)), pltpu.SemaphoreType.DMA((n,))] if n else []


def _gather_copy(k, j, gin, gout, send_sems, recv_sems, sending):
    x, y, c = _me()
    sibling, x_chip, y_chip, d_chip = (x, y, 1 - c), (1 - x, y), (x, 1 - y), (1 - x, 1 - y)
    south = c == 0
    passed_on = (jnp.where(south, 1 - x, x), jnp.where(south, y, 1 - y), c)
    src, to = gin[j], sibling
    if sending:
        block = {0: (x, y, c), 1: (x, y, c), 2: (x, y, c), 3: passed_on, 4: (*x_chip, c), 5: (*y_chip, c), 6: (*d_chip, c)}[k]
        to = {1: (*x_chip, c), 2: (*y_chip, c), 3: (jnp.where(south, x, 1 - x), jnp.where(south, 1 - y, y), c)}.get(k, sibling)
        if k >= 3:
            src = gout[j].at[_slot(*block)]
    else:
        block = {0: sibling, 1: (*x_chip, c), 2: (*y_chip, c), 3: (*d_chip, c), 4: (*x_chip, 1 - c), 5: (*y_chip, 1 - c),
                 6: (*d_chip, 1 - c)}[k]
    return _remote(src, gout[j].at[_slot(*block)], send_sems.at[k, j], recv_sems.at[k, j], to)


def _gather_do(ks, action, gin, gout, send_sems, recv_sems):
    for k in ks:
        for j in range(len(gin)):
            cp = _gather_copy(k, j, gin, gout, send_sems, recv_sems, action != "wait_recv")
            getattr(cp, action)()


def _gather_start(gin, gout, send_sems, recv_sems, local_sems):
    x, y, c = _me()
    _barrier([(x, y, 1 - c), (1 - x, y, c), (x, 1 - y, c)])
    for j in range(len(gin)):
        pltpu.make_async_copy(gin[j], gout[j].at[_slot(*_me())], local_sems.at[j]).start()
    _gather_do((0, 1, 2), "start", gin, gout, send_sems, recv_sems)


def _gather_forward(gin, gout, send_sems, recv_sems, local_sems):
    _gather_do((1, 2), "wait_recv", gin, gout, send_sems, recv_sems)
    _gather_do((3, 4, 5), "start", gin, gout, send_sems, recv_sems)


def _gather_finish(gin, gout, send_sems, recv_sems, local_sems):
    _gather_do((3,), "wait_recv", gin, gout, send_sems, recv_sems)
    _gather_do((6,), "start", gin, gout, send_sems, recv_sems)
    _gather_do((0, 4, 5, 6), "wait_recv", gin, gout, send_sems, recv_sems)
    _gather_do(range(7), "wait_send", gin, gout, send_sems, recv_sems)
    for j in range(len(gin)):
        pltpu.make_async_copy(gin[j], gout[j].at[_slot(*_me())], local_sems.at[j]).wait()


def _all_gather(blocks, name):
    n = len(blocks)

    def body(*refs):
        gin, gout, sems = refs[:n], refs[n:2 * n], refs[2 * n:]
        _gather_start(gin, gout, *sems)
        _gather_forward(gin, gout, *sems)
        _gather_finish(gin, gout, *sems)

    hbm = pl.BlockSpec(memory_space=pltpu.HBM)
    return pl.pallas_call(
        body, name=name,
        out_shape=tuple(jax.ShapeDtypeStruct((N_DEV,) + b.shape, b.dtype) for b in blocks),
        in_specs=[hbm] * n, out_specs=(hbm,) * n, scratch_shapes=_gather_sems(n),
        compiler_params=pltpu.CompilerParams(collective_id=GATHER_BARRIER),
    )(*blocks)


def _pair_reduce(parts, name):
    n = len(parts)

    def body(*refs):
        ins, own, others, landing, mine = (refs[k * n:(k + 1) * n] for k in range(5))
        send_sems, recv_sems, local_sems = refs[5 * n:]
        me = _me()
        x, y, c = me
        sibling = (x, y, 1 - c)
        _barrier([sibling])
        sends, loads = [], []
        for k in range(4):
            for j in range(n):
                cp = _remote(ins[j].at[_slot(*_chip(me, k), 1 - c)], landing[j].at[k], send_sems.at[k, j],
                             recv_sems.at[k, j], sibling)
                cp.start()
                sends.append(cp)
                ld = pltpu.make_async_copy(ins[j].at[_slot(*_chip(me, k), c)], mine[j].at[k], local_sems.at[k, j])
                ld.start()
                loads.append(ld)
        for k in range(4):
            for j in range(n):
                loads[k * n + j].wait()
                _remote(ins[j].at[0], landing[j].at[k], send_sems.at[k, j], recv_sems.at[k, j], sibling).wait_recv()
                total = mine[j][k].astype(f32) + landing[j][k].astype(f32)
                if k == 0:
                    own[j][...] = total.astype(own[j].dtype)
                else:
                    others[j][k - 1] = total.astype(others[j].dtype)
        for cp in sends:
            cp.wait_send()

    vm = pl.BlockSpec(memory_space=pltpu.VMEM)
    return pl.pallas_call(
        body, name=name,
        out_shape=tuple(jax.ShapeDtypeStruct(p.shape[1:], p.dtype) for p in parts)
        + tuple(jax.ShapeDtypeStruct((3,) + p.shape[1:], p.dtype) for p in parts),
        in_specs=[pl.BlockSpec(memory_space=pltpu.HBM)] * n, out_specs=(vm,) * (2 * n),
        scratch_shapes=[pltpu.VMEM((4,) + p.shape[1:], p.dtype) for p in parts] * 2
        + [pltpu.SemaphoreType.DMA((4, n)), pltpu.SemaphoreType.DMA((4, n)), pltpu.SemaphoreType.DMA((4, n))],
        compiler_params=pltpu.CompilerParams(vmem_limit_bytes=V7X_VMEM_LIMIT, collective_id=PAIR_BARRIER),
    )(*parts)


def _chip_exchange_sems(n):
    return [pltpu.SemaphoreType.DMA((3, n)), pltpu.SemaphoreType.DMA((3, n))] if n else []


def _chip_exchange_copy(k, j, ein, eout, send_sems, recv_sems):
    me = _me()
    return _remote(ein[j].at[k - 1], eout[j].at[k - 1], send_sems.at[k - 1, j], recv_sems.at[k - 1, j],
                   (*_chip(me, k), me[2]))


def _chip_exchange_start(ein, eout, send_sems, recv_sems):
    me = _me()
    _barrier([(*_chip(me, k), me[2]) for k in range(1, 4)])
    for k in range(1, 4):
        for j in range(len(ein)):
            _chip_exchange_copy(k, j, ein, eout, send_sems, recv_sems).start()


def _chip_exchange_finish(ein, eout, send_sems, recv_sems):
    for k in range(1, 4):
        for j in range(len(ein)):
            _chip_exchange_copy(k, j, ein, eout, send_sems, recv_sems).wait_recv()
    for k in range(1, 4):
        for j in range(len(ein)):
            _chip_exchange_copy(k, j, ein, eout, send_sems, recv_sems).wait_send()


def _chip_exchange(others, name):
    n = len(others)

    def body(*refs):
        ein, eout, sems = refs[:n], refs[n:2 * n], refs[2 * n:]
        _chip_exchange_start(ein, eout, *sems)
        _chip_exchange_finish(ein, eout, *sems)

    hbm = pl.BlockSpec(memory_space=pltpu.HBM)
    return pl.pallas_call(
        body, name=name, out_shape=tuple(jax.ShapeDtypeStruct(e.shape, e.dtype) for e in others),
        in_specs=[hbm] * n, out_specs=(hbm,) * n, scratch_shapes=_chip_exchange_sems(n),
        compiler_params=pltpu.CompilerParams(collective_id=CHIP_BARRIER),
    )(*others)


def _sum_parts(owns, arrived, name):
    n = len(owns)

    def body(*refs):
        for own, arr, out in zip(refs[:n], refs[n:2 * n], refs[2 * n:]):
            acc = own[...].astype(f32)
            for k in range(3):
                acc = acc + arr[k].astype(f32)
            out[...] = acc

    vm = pl.BlockSpec(memory_space=pltpu.VMEM)
    return pl.pallas_call(
        body, name=name, out_shape=tuple(jax.ShapeDtypeStruct(o.shape, f32) for o in owns),
        in_specs=[vm] * (2 * n), out_specs=(vm,) * n,
        compiler_params=pltpu.CompilerParams(vmem_limit_bytes=V7X_VMEM_LIMIT),
    )(*owns, *arrived)


ADAM_C1 = 1.0 / (1.0 - ADAM_B1 ** ADAM_STEP)
ADAM_C2 = 1.0 / (1.0 - ADAM_B2 ** ADAM_STEP)


def _adam_update(w, g, m, v):
    m = ADAM_B1 * m + (1.0 - ADAM_B1) * g
    v = ADAM_B2 * v + (1.0 - ADAM_B2) * (g * g)
    return -ADAM_LR * ((m * ADAM_C1) / (jnp.sqrt(v * ADAM_C2) + ADAM_EPS) + ADAM_WD * w), m, v


def _sum_adamw(own, arrived, w, m, v, name, steps):
    rows = own.shape[0]
    br = rows // steps

    def body(own_ref, arr_ref, w_ref, m_ref, v_ref, g_out, d_out, m_out, v_out):
        g = own_ref[...].astype(f32)
        for k in range(3):
            g = g + arr_ref[k].astype(f32)
        g_out[...] = g
        d_out[...], m_out[...], v_out[...] = _adam_update(w_ref[...], g, m_ref[...], v_ref[...])

    blk = pl.BlockSpec((br, D), lambda i: (i, 0))
    return pl.pallas_call(
        body, name=name, grid=(steps,), out_shape=(jax.ShapeDtypeStruct((rows, D), f32),) * 4,
        in_specs=[blk, pl.BlockSpec((3, br, D), lambda i: (0, i, 0)), blk, blk, blk], out_specs=(blk,) * 4,
        compiler_params=pltpu.CompilerParams(dimension_semantics=("parallel",), vmem_limit_bytes=V7X_VMEM_LIMIT),
    )(own, arrived, w, m, v)


def _adamw(ws, gs, ms, vs, name):
    n = len(ws)

    def body(*refs):
        w_r, g_r, m_r, v_r = (refs[k * n:(k + 1) * n] for k in range(4))
        d_o, m_o, v_o = (refs[(4 + k) * n:(5 + k) * n] for k in range(3))
        for j in range(n):
            d_o[j][...], m_o[j][...], v_o[j][...] = _adam_update(w_r[j][...], g_r[j][...], m_r[j][...], v_r[j][...])

    vm = pl.BlockSpec(memory_space=pltpu.VMEM)
    shapes = tuple(jax.ShapeDtypeStruct(w.shape, f32) for w in ws)
    return pl.pallas_call(
        body, name=name, out_shape=shapes * 3, in_specs=[vm] * (4 * n), out_specs=tuple([vm] * (3 * n)),
        compiler_params=pltpu.CompilerParams(vmem_limit_bytes=V7X_VMEM_LIMIT),
    )(*ws, *gs, *ms, *vs)


SMALL = (("w_pool", GROUPS * DH * DH), ("pool_scale", PW), ("ln1_g", D), ("ln1_b", D), ("conv_b", D_FF),
         ("ln2_g", D), ("ln2_b", D), ("conv_w", 3 * D_FF), ("loss", 1))
SMALL_ROWS = 640


def _pack(named):
    flat = jnp.concatenate([named[k].reshape(-1) for k, _ in SMALL])
    return jnp.pad(flat, (0, SMALL_ROWS * 128 - flat.shape[0])).reshape(SMALL_ROWS, 128)


def _unpack(packed):
    flat, out, at = packed.reshape(-1), {}, 0
    for k, size in SMALL:
        out[k] = flat[at:at + size]
        at += size
    return out


def kernel(x, w_in, w_pool, pool_scale, w_out, ln1_g, ln1_b, w_up, conv_w, conv_b, w_down, ln2_g, ln2_b, loss_target, m_w_in, m_w_pool, m_pool_scale, m_w_out, m_ln1_g, m_ln1_b, m_w_up, m_conv_w, m_conv_b, m_w_down, m_ln2_g, m_ln2_b, v_w_in, v_w_pool, v_pool_scale, v_w_out, v_ln1_g, v_ln1_b, v_w_up, v_conv_w, v_conv_b, v_w_down, v_ln2_g, v_ln2_b):
    me = 4 * lax.axis_index("x") + 2 * lax.axis_index("y") + lax.axis_index("c")
    x2, tgt = x[0], loss_target[0]

    g_in, g_out, g_cw = _all_gather([w_in[0].T.astype(bf16), w_out[0].astype(bf16), conv_w[0]], "gather_weights")
    w_in_t = g_in.reshape(IN_W, D)
    w_out_f = g_out.reshape(D, D)
    conv_w_f = jnp.transpose(g_cw, (1, 0, 2)).reshape(3, D_FF)
    w_pool_b = w_pool[0].astype(bf16)

    cos, sin = _rope_tables()
    dmat, qd, kd, cdec = _decay_tables(RET_TILE)

    qkv, g, oret, states, cat, pooled, xhat1, rstd1, x1b, g_up, g_down = _mix_forward(
        x2, w_in_t, cos, sin, dmat, qd, kd, cdec, w_pool_b, pool_scale, w_out_f, ln1_g, ln1_b,
        gather=[w_up[0].T.astype(bf16), w_down[0].astype(bf16)])
    w_up_t = g_up.reshape(2 * D_FF, D)
    w_down_f = g_down.reshape(D_FF, D)
    dz1, dz2b, du, f, loss8, d_ln2_g, d_ln2_b, d_ln1_g, d_ln1_b, d_conv_b, d_conv_w = _ffn_forward_backward(
        xhat1, rstd1, ln1_g, ln1_b, w_up_t, conv_w_f, conv_b, w_down_f, ln2_g, ln2_b, tgt)

    (dw_down,) = _weight_grad(f, dz2b, "grad_w_down", tm=D_FF // 2)
    own_down, oth_down = _pair_reduce([dw_down.reshape(N_DEV, ROWS_DOWN, D)], "pair_reduce_down")
    dw_up_t, arr_down = _weight_grad(du, x1b, "grad_w_up", tm=D_FF // 2, exchange=[oth_down])
    own_up, oth_up = _pair_reduce([dw_up_t.reshape(N_DEV, ROWS_UP, D)], "pair_reduce_up")
    dproj, grad_x, d_w_pool, d_pool_scale, dw_out, arr_up = _mix_backward(
        dz1, w_out_f, qkv, g, oret, states, pooled, cat, cos, sin, dmat, qd, kd, cdec, w_pool_b, pool_scale, w_in_t,
        exchange=[oth_up])
    small = _pack({"w_pool": d_w_pool, "pool_scale": d_pool_scale, "ln1_g": d_ln1_g, "ln1_b": d_ln1_b,
                   "conv_b": d_conv_b, "ln2_g": d_ln2_g, "ln2_b": d_ln2_b, "conv_w": d_conv_w, "loss": loss8[0, :1]})
    own_out, own_small, oth_out, oth_small = _pair_reduce(
        [dw_out.reshape(N_DEV, ROWS_OUT, D), small.reshape(N_DEV, SMALL_ROWS // N_DEV, 128)], "pair_reduce_out")
    dw_in_t, arr_out, arr_small = _weight_grad(dproj, x2, "grad_w_in", tm=IN_W // 2, exchange=[oth_out, oth_small])
    own_in, oth_in = _pair_reduce([dw_in_t.reshape(N_DEV, ROWS_IN, D)], "pair_reduce_in")
    (arr_in,) = _chip_exchange([oth_in], "exchange_in")

    names = ["w_in", "w_pool", "pool_scale", "w_out", "ln1_g", "ln1_b", "w_up", "conv_w", "conv_b", "w_down",
             "ln2_g", "ln2_b"]
    w_d = dict(w_in=w_in, w_pool=w_pool, pool_scale=pool_scale, w_out=w_out, ln1_g=ln1_g, ln1_b=ln1_b, w_up=w_up,
               conv_w=conv_w, conv_b=conv_b, w_down=w_down, ln2_g=ln2_g, ln2_b=ln2_b)
    m_d = dict(w_in=m_w_in, w_pool=m_w_pool, pool_scale=m_pool_scale, w_out=m_w_out, ln1_g=m_ln1_g, ln1_b=m_ln1_b,
               w_up=m_w_up, conv_w=m_conv_w, conv_b=m_conv_b, w_down=m_w_down, ln2_g=m_ln2_g, ln2_b=m_ln2_b)
    v_d = dict(w_in=v_w_in, w_pool=v_w_pool, pool_scale=v_pool_scale, w_out=v_w_out, ln1_g=v_ln1_g, ln1_b=v_ln1_b,
               w_up=v_w_up, conv_w=v_conv_w, conv_b=v_conv_b, w_down=v_w_down, ln2_g=v_ln2_g, ln2_b=v_ln2_b)
    g_d, delta, new_m, new_v = {}, {}, {}, {}

    big = (("w_in", own_in, arr_in, True, 4), ("w_out", own_out, arr_out, False, 2),
           ("w_up", own_up, arr_up, True, 4), ("w_down", own_down, arr_down, False, 2))
    for k, own, arr, transposed, steps in big:
        lay = (lambda a: a[0].T) if transposed else (lambda a: a[0])
        back = (lambda a: a.T[None]) if transposed else (lambda a: a[None])
        res = _sum_adamw(own, arr, lay(w_d[k]), lay(m_d[k]), lay(v_d[k]), "adamw_" + k, steps)
        g_d[k], delta[k], new_m[k], new_v[k] = (back(r) for r in res)

    (small_piece,) = _sum_parts([own_small], [arr_small], "sum_small_grads")
    (gs_small,) = _all_gather([small_piece], "gather_small_grads")
    gsm = _unpack(gs_small)
    gsm["conv_w"] = lax.dynamic_slice(gsm["conv_w"].reshape(3, D_FF), (0, me * (D_FF // N_DEV)), (3, D_FF // N_DEV))
    two_d = lambda a: a.reshape(-1, a.shape[-1])
    group = [k for k in names if k not in g_d]
    for k in group:
        g_d[k] = gsm[k].reshape(w_d[k].shape)
    res = _adamw([two_d(w_d[k]) for k in group], [two_d(g_d[k]) for k in group], [two_d(m_d[k]) for k in group],
                 [two_d(v_d[k]) for k in group], "adamw_small")
    for j, k in enumerate(group):
        delta[k] = res[j].reshape(w_d[k].shape)
        new_m[k] = res[len(group) + j].reshape(w_d[k].shape)
        new_v[k] = res[2 * len(group) + j].reshape(w_d[k].shape)

    loss = gsm["loss"].reshape(())
    return (loss, grad_x[None], *[g_d[k] for k in names], *[delta[k] for k in names], *[new_m[k] for k in names],
            *[new_v[k] for k in names])
```

```python
import functools
import math

import numpy as np
import jax
import jax.numpy as jnp
from jax import lax
from jax.experimental import pallas as pl
from jax.experimental.pallas import tpu as pltpu

f32 = jnp.float32
bf16 = jnp.bfloat16

N_DEV = 8
T = 4096
D = 1024
CHUNK = 64
MIX_TILE = 512
RET_TILE = 256
HEADS = 4
DH = 128
RW = HEADS * DH
PW = 512
GROUPS = 4
WINDOWS = (2, 4, 8, 16)
IN_W = 4 * RW + PW
D_FF = 2816
LN_EPS = 1e-5
RMS_EPS = 1e-6
ALPHA = 2.0 ** 0.25
K_SCALE = DH ** -0.5

ADAM_LR = 0.001
ADAM_B1 = 0.9
ADAM_B2 = 0.999
ADAM_EPS = 1e-08
ADAM_WD = 0.01
ADAM_STEP = 10

ROWS_IN, ROWS_OUT, ROWS_UP, ROWS_DOWN = IN_W // N_DEV, D // N_DEV, 2 * D_FF // N_DEV, D_FF // N_DEV

V7X_VMEM_LIMIT = 56 * 2 ** 20
HALO = 32

NT = (((1,), (1,)), ((), ()))
TN = (((0,), (0,)), ((), ()))
NN = (((1,), (0,)), ((), ()))


def _dot(a, b, dims=NN):
    return lax.dot_general(a, b, dims, preferred_element_type=f32)


def _const_spec(shape):
    zeros = (0,) * len(shape)
    return pl.BlockSpec(shape, lambda i: zeros, pipeline_mode=pl.Buffered(1))


def _sigmoid(x):
    return 0.5 * jnp.tanh(0.5 * x) + 0.5


def _decay_tables(tt):
    h = np.arange(HEADS, dtype=np.float64)
    log_gamma = np.log(1.0 - 2.0 ** (-5.0 - h)).astype(np.float32).astype(np.float64)[:, None, None]
    idx = np.arange(tt, dtype=np.float64)
    visible = (idx[None, :] // CHUNK) <= (idx[:, None] // CHUNK)
    mask = np.where(visible[None], np.exp(log_gamma * np.abs(idx[:, None] - idx[None, :])[None]), 0.0)
    qd = np.broadcast_to(np.exp(log_gamma * (idx[None, :, None] + 1.0)), (HEADS, tt, DH))
    kd = np.broadcast_to(np.exp(log_gamma * (tt - 1.0 - idx[None, :, None])), (HEADS, tt, DH))
    cd = np.exp(log_gamma[:, 0, 0] * tt)
    return (jnp.asarray(mask, f32), jnp.asarray(qd, f32), jnp.asarray(kd, f32), [float(c) for c in cd])


def _rope_tables():
    inv_freq = (10000.0 ** (-np.arange(0, DH, 2, dtype=np.float64) / DH)).astype(np.float32)
    ang = (np.arange(T, dtype=np.float32)[:, None] * inv_freq[None, :]).astype(np.float64)
    cos, sin = np.cos(ang), np.sin(ang)
    return (jnp.asarray(np.concatenate([cos, cos], axis=1), f32), jnp.asarray(np.concatenate([-sin, sin], axis=1), f32))


def _swap_halves(t):
    return pltpu.roll(t, DH // 2, axis=1)


def _mix_forward(x, w_in_t, cos, sin, dmat, qd, kd, cdec, w_pool, pool_scale, w_out, ln1_g, ln1_b, gather,
                 tt=MIX_TILE):
    n_tiles = T // tt
    n_g = len(gather)

    def body(x_ref, wint_ref, cos_ref, sin_ref, dmat_ref, qd_ref, kd_ref, wpool_ref, pscale_ref, wout_ref,
             g1_ref, b1_ref, *rest):
        gin, rest = rest[:n_g], rest[n_g:]
        qkv_ref, g_ref, oret_ref, states_ref, cat_ref, pooled_ref, xhat_ref, rstd_ref, x1b_ref = rest[:9]
        gout, (state_s, pext_s, tmp_s, *sems) = rest[9:9 + n_g], rest[9 + n_g:]
        i = pl.program_id(0)

        @pl.when(i == 0)
        def _():
            state_s[...] = jnp.zeros_like(state_s)
            pext_s[pl.ds(0, HALO), :] = jnp.zeros((HALO, PW), f32)
            _gather_start(gin, gout, *sems)

        @pl.when(i == n_tiles - 2)
        def _():
            _gather_forward(gin, gout, *sems)

        xb = x_ref[...].astype(bf16)
        cos_t, sin_t = cos_ref[...], sin_ref[...]
        for part in range(2):
            pr = _dot(xb, wint_ref[pl.ds(part * RW, RW), :], NT)
            for h in range(HEADS):
                t = pr[:, h * DH:(h + 1) * DH]
                r = t * cos_t + _swap_halves(t) * sin_t
                if part == 1:
                    r = r * K_SCALE
                qkv_ref[:, part * RW + h * DH: part * RW + (h + 1) * DH] = r.astype(bf16)
        qkv_ref[:, 2 * RW:3 * RW] = _dot(xb, wint_ref[pl.ds(2 * RW, RW), :], NT).astype(bf16)
        g_ref[...] = _dot(xb, wint_ref[pl.ds(3 * RW, RW), :], NT)
        pext_s[pl.ds(HALO, tt), :] = _dot(xb, wint_ref[pl.ds(4 * RW, PW), :], NT)

        for sub in range(tt // RET_TILE):
            rows = pl.ds(sub * RET_TILE, RET_TILE)
            for h in range(HEADS):
                q = qkv_ref[rows, h * DH:(h + 1) * DH]
                k = qkv_ref[rows, RW + h * DH: RW + (h + 1) * DH]
                v = qkv_ref[rows, 2 * RW + h * DH: 2 * RW + (h + 1) * DH]
                s = _dot(q, k, NT) * dmat_ref[h]
                st = state_s[h]
                stb = st.astype(bf16)
                states_ref[sub, h] = stb
                oret_ref[rows, h * DH:(h + 1) * DH] = (_dot(s.astype(bf16), v)
                                                      + _dot((q.astype(f32) * qd_ref[h]).astype(bf16), stb))
                state_s[h] = st * cdec[h] + _dot((k.astype(f32) * kd_ref[h]).astype(bf16), v, TN)

        for h in range(HEADS):
            sl = slice(h * DH, (h + 1) * DH)
            o = oret_ref[:, sl]
            r = lax.rsqrt(jnp.mean(o * o, axis=-1, keepdims=True) + RMS_EPS)
            gg = g_ref[:, sl]
            cat_ref[:, sl] = (o * r * (gg * _sigmoid(gg))).astype(bf16)

        pos1 = (i * tt + lax.broadcasted_iota(jnp.int32, (tt, 1), 0) + 1).astype(f32)
        for gi, w in enumerate(WINDOWS):
            sl = slice(gi * DH, (gi + 1) * DH)
            stages = int(math.log2(w))
            src = pext_s
            for s in range(stages):
                lo = HALO - 8 * (stages - 1 - s)
                n = tt + HALO - lo
                shift = 2 ** s
                val = src[pl.ds(lo, n), sl] + src[pl.ds(lo - shift, n), sl]
                if s == stages - 1:
                    wsum = val
                else:
                    tmp_s[pl.ds(lo, n), sl] = val
                    src = tmp_s
            p_g = pext_s[pl.ds(HALO, tt), sl]
            pooled = (wsum / jnp.minimum(pos1, float(w)) - p_g).astype(bf16)
            pooled_ref[:, sl] = pooled
            y = _dot(pooled, wpool_ref[gi]) * pscale_ref[:, sl]
            cat_ref[:, RW + gi * DH: RW + (gi + 1) * DH] = y.astype(bf16)
        pext_s[pl.ds(0, HALO), :] = pext_s[pl.ds(tt, HALO), :]

        z = ALPHA * x_ref[...] + _dot(cat_ref[...], wout_ref[...])
        mu = jnp.mean(z, axis=-1, keepdims=True)
        zc = z - mu
        rstd = lax.rsqrt(jnp.mean(zc * zc, axis=-1, keepdims=True) + LN_EPS)
        xhat = zc * rstd
        xhat_ref[...] = xhat
        rstd_ref[...] = rstd
        x1b_ref[...] = (xhat * g1_ref[...] + b1_ref[...]).astype(bf16)

        @pl.when(i == n_tiles - 1)
        def _():
            _gather_finish(gin, gout, *sems)

    tile = lambda w: pl.BlockSpec((tt, w), lambda i: (i, 0))
    hbm = pl.BlockSpec(memory_space=pltpu.HBM)
    out_shape = (
        jax.ShapeDtypeStruct((T, 3 * RW), bf16),
        jax.ShapeDtypeStruct((T, RW), f32),
        jax.ShapeDtypeStruct((T, RW), f32),
        jax.ShapeDtypeStruct((T // RET_TILE, HEADS, DH, DH), bf16),
        jax.ShapeDtypeStruct((T, D), bf16),
        jax.ShapeDtypeStruct((T, PW), bf16),
        jax.ShapeDtypeStruct((T, D), f32),
        jax.ShapeDtypeStruct((T, 1), f32),
        jax.ShapeDtypeStruct((T, D), bf16),
    ) + tuple(jax.ShapeDtypeStruct((N_DEV,) + b.shape, b.dtype) for b in gather)
    return pl.pallas_call(
        body, name="mix_forward", grid=(n_tiles,), out_shape=out_shape,
        in_specs=[tile(D), _const_spec((IN_W, D)), tile(DH), tile(DH),
                  _const_spec((HEADS, RET_TILE, RET_TILE)), _const_spec((HEADS, RET_TILE, DH)),
                  _const_spec((HEADS, RET_TILE, DH)),
                  _const_spec((GROUPS, DH, DH)), _const_spec((1, PW)), _const_spec((D, D)),
                  _const_spec((1, D)), _const_spec((1, D))] + [hbm] * n_g,
        out_specs=(tile(3 * RW), tile(RW), tile(RW),
                   pl.BlockSpec((tt // RET_TILE, HEADS, DH, DH), lambda i: (i, 0, 0, 0)),
                   tile(D), tile(PW), tile(D), tile(1), tile(D)) + (hbm,) * n_g,
        scratch_shapes=[pltpu.VMEM((HEADS, DH, DH), f32), pltpu.VMEM((tt + HALO, PW), f32),
                        pltpu.VMEM((tt + HALO, PW), f32)] + _gather_sems(n_g),
        compiler_params=pltpu.CompilerParams(dimension_semantics=("arbitrary",), vmem_limit_bytes=V7X_VMEM_LIMIT,
                                             collective_id=GATHER_BARRIER),
    )(x, w_in_t, cos, sin, dmat, qd, kd, w_pool, pool_scale, w_out, ln1_g, ln1_b, *gather)


def _ffn_forward_backward(xhat1, rstd1, ln1_g, ln1_b, w_up_t, conv_w, conv_b, w_down, ln2_g, ln2_b, target,
                          tt=256, widths=(512, 512, 512, 512, 512, 256)):
    n_tiles = T // tt
    assert sum(widths) == D_FF and all(w % 128 == 0 for w in widths)
    chunks = [(sum(widths[:c]), w) for c, w in enumerate(widths)]
    FH = 16
    hb = tt // FH

    def body(xhat_ref, halo_ref, rstd_ref, g1_ref, b1_ref, wupt_ref, cw_ref, cb_ref, wdown_ref, g2_ref, b2_ref, tgt_ref,
             dz1_ref, dz2b_ref, du_ref, f_ref, loss_ref, dg2_ref, db2_ref, dg1_ref, db1_ref, dcb_ref, dcw_ref,
             gext_s, val_s, dhext_s):
        i = pl.program_id(0)
        tile_idx = n_tiles - 1 - i

        def rd(ref, off, lo, w):
            return jnp.concatenate([ref[lo // 128 + k, pl.ds(off, tt), :] for k in range(w // 128)], axis=1)

        def wr(ref, lo, val):
            for k in range(val.shape[1] // 128):
                ref[lo // 128 + k, pl.ds(0, val.shape[0]), :] = val[:, k * 128:(k + 1) * 128]

        @pl.when(i == 0)
        def _():
            for r in (loss_ref, dg2_ref, db2_ref, dg1_ref, db1_ref, dcb_ref, dcw_ref):
                r[...] = jnp.zeros_like(r)
            dhext_s[:, pl.ds(tt, 8), :] = jnp.zeros((D_FF // 128, 8, 128), f32)

        g1, b1 = g1_ref[...], b1_ref[...]
        xhat = xhat_ref[...]
        x1 = xhat * g1 + b1
        x1b = x1.astype(bf16)
        x1h = ((halo_ref[...] * g1 + b1) * jnp.where(tile_idx == 0, 0.0, 1.0)).astype(bf16)
        x1ext = jnp.concatenate([x1h, x1b], axis=0)

        for lo, w in chunks:
            cs = slice(lo, lo + w)
            val = _dot(x1b, wupt_ref[pl.ds(lo, w), :], NT)
            gate_ext = _dot(x1ext, wupt_ref[pl.ds(D_FF + lo, w), :], NT)
            wr(gext_s, lo, gate_ext)
            hh = (cb_ref[:, cs] + cw_ref[0:1, cs] * rd(gext_s, FH - 2, lo, w) + cw_ref[1:2, cs] * rd(gext_s, FH - 1, lo, w)
                  + cw_ref[2:3, cs] * gate_ext[FH:])
            sg = _sigmoid(hh)
            act = hh * sg
            wr(dhext_s, lo, act)
            val_s[:, cs] = val * (sg + act * (1.0 - sg))
            f_ref[:, cs] = (act * val).astype(bf16)

        z = ALPHA * x1 + _dot(f_ref[...], wdown_ref[...])
        mu = jnp.mean(z, axis=-1, keepdims=True)
        zc = z - mu
        rstd2 = lax.rsqrt(jnp.mean(zc * zc, axis=-1, keepdims=True) + LN_EPS)
        xh2 = zc * rstd2
        diff = xh2 * g2_ref[...] + b2_ref[...] - tgt_ref[...]
        loss_ref[...] += 0.5 * jnp.sum(diff * diff) / D
        dy = diff * (1.0 / D)
        dg2_ref[...] += jnp.sum(dy * xh2, axis=0, keepdims=True)
        db2_ref[...] += jnp.sum(dy, axis=0, keepdims=True)
        dyg = dy * g2_ref[...]
        dz2 = rstd2 * (dyg - jnp.mean(dyg, axis=-1, keepdims=True) - xh2 * jnp.mean(dyg * xh2, axis=-1, keepdims=True))
        dz2b = dz2.astype(bf16)
        dz2b_ref[...] = dz2b

        for lo, w in chunks:
            cs = slice(lo, lo + w)
            df = _dot(dz2b, wdown_ref[pl.ds(lo, w), :], NT)
            dval = df * rd(dhext_s, 0, lo, w)
            dh = df * val_s[:, cs]
            wr(dhext_s, lo, dh)
            dh1, dh2, g0 = rd(dhext_s, 1, lo, w), rd(dhext_s, 2, lo, w), rd(gext_s, FH, lo, w)
            dcb_ref[:, cs] += jnp.sum(dh, axis=0, keepdims=True)
            dcw_ref[0:1, cs] += jnp.sum(dh2 * g0, axis=0, keepdims=True)
            dcw_ref[1:2, cs] += jnp.sum(dh1 * g0, axis=0, keepdims=True)
            dcw_ref[2:3, cs] += jnp.sum(dh * g0, axis=0, keepdims=True)
            dgate = cw_ref[2:3, cs] * dh + cw_ref[1:2, cs] * dh1 + cw_ref[0:1, cs] * dh2
            du_ref[:, cs] = dval.astype(bf16)
            du_ref[:, D_FF + lo: D_FF + lo + w] = dgate.astype(bf16)
        dhext_s[:, pl.ds(tt, 8), :] = dhext_s[:, pl.ds(0, 8), :]
        dx1 = ALPHA * dz2 + _dot(du_ref[...], wupt_ref[...])

        dg1_ref[...] += jnp.sum(dx1 * xhat, axis=0, keepdims=True)
        db1_ref[...] += jnp.sum(dx1, axis=0, keepdims=True)
        dxg = dx1 * g1
        dz1_ref[...] = rstd_ref[...] * (dxg - jnp.mean(dxg, axis=-1, keepdims=True)
                                        - xhat * jnp.mean(dxg * xhat, axis=-1, keepdims=True))

    rtile = lambda w: pl.BlockSpec((tt, w), lambda i: (n_tiles - 1 - i, 0))
    acc = lambda shape: pl.BlockSpec(shape, lambda i: (0, 0))
    out_shape = (
        jax.ShapeDtypeStruct((T, D), f32),
        jax.ShapeDtypeStruct((T, D), bf16),
        jax.ShapeDtypeStruct((T, 2 * D_FF), bf16),
        jax.ShapeDtypeStruct((T, D_FF), bf16),
        jax.ShapeDtypeStruct((8, 128), f32),
        jax.ShapeDtypeStruct((1, D), f32), jax.ShapeDtypeStruct((1, D), f32),
        jax.ShapeDtypeStruct((1, D), f32), jax.ShapeDtypeStruct((1, D), f32),
        jax.ShapeDtypeStruct((1, D_FF), f32), jax.ShapeDtypeStruct((3, D_FF), f32),
    )
    return pl.pallas_call(
        body, name="ffn_forward_backward", grid=(n_tiles,), out_shape=out_shape,
        in_specs=[rtile(D),
                  pl.BlockSpec((FH, D), lambda i: (jnp.maximum((n_tiles - 1 - i) * hb - 1, 0), 0)),
                  rtile(1), _const_spec((1, D)), _const_spec((1, D)), _const_spec((2 * D_FF, D)),
                  _const_spec((3, D_FF)), _const_spec((1, D_FF)), _const_spec((D_FF, D)),
                  _const_spec((1, D)), _const_spec((1, D)), rtile(D)],
        out_specs=(rtile(D), rtile(D), rtile(2 * D_FF), rtile(D_FF), acc((8, 128)),
                   acc((1, D)), acc((1, D)), acc((1, D)), acc((1, D)), acc((1, D_FF)), acc((3, D_FF))),
        scratch_shapes=[pltpu.VMEM((D_FF // 128, tt + FH, 128), f32), pltpu.VMEM((tt, D_FF), f32),
                        pltpu.VMEM((D_FF // 128, tt + 8, 128), f32)],
        compiler_params=pltpu.CompilerParams(dimension_semantics=("arbitrary",), vmem_limit_bytes=V7X_VMEM_LIMIT),
    )(xhat1, xhat1, rstd1, ln1_g, ln1_b, w_up_t, conv_w, conv_b, w_down, ln2_g, ln2_b, target)


def _mix_backward(dz1, w_out, qkv, g, oret, states, pooled, cat, cos, sin, dmat, qd, kd, cdec, w_pool, pool_scale, w_in_t,
                  exchange, tt=MIX_TILE):
    n_tiles = T // tt
    n_e = len(exchange)

    def body(dz1_ref, wout_ref, qkv_ref, g_ref, oret_ref, states_ref, pooled_ref, cat_ref, cos_ref, sin_ref, dmat_ref,
             qd_ref, kd_ref, wpool_ref, pscale_ref, wint_ref, *rest):
        ein, rest = rest[:n_e], rest[n_e:]
        dproj_ref, gx_ref, dwpool_ref, dpscale_ref, dwout_ref = rest[:5]
        eout, (dstate_s, dout_s, eext_s, tmp_s, dwout_s, *sems) = rest[5:5 + n_e], rest[5 + n_e:]
        i = pl.program_id(0)
        tile_idx = n_tiles - 1 - i

        @pl.when(i == 0)
        def _():
            dstate_s[...] = jnp.zeros_like(dstate_s)
            dwpool_ref[...] = jnp.zeros_like(dwpool_ref)
            dpscale_ref[...] = jnp.zeros_like(dpscale_ref)
            dwout_s[...] = jnp.zeros_like(dwout_s)
            eext_s[pl.ds(tt, HALO), :] = jnp.zeros((HALO, PW), f32)
            _chip_exchange_start(ein, eout, *sems)

        dz1 = dz1_ref[...]
        dz1b = dz1.astype(bf16)
        dcat = _dot(dz1b, wout_ref[...], NT)
        dwout_s[...] += _dot(cat_ref[...], dz1b, TN)

        pos1 = (tile_idx * tt + lax.broadcasted_iota(jnp.int32, (tt, 1), 0) + 1).astype(f32)
        for gi, w in enumerate(WINDOWS):
            sl = slice(gi * DH, (gi + 1) * DH)
            dpo = dcat[:, RW + gi * DH: RW + (gi + 1) * DH]
            pooled_g = pooled_ref[:, sl]
            ylin = _dot(pooled_g, wpool_ref[gi])
            dpscale_ref[:, sl] += jnp.sum(dpo * ylin, axis=0, keepdims=True)
            dpw = (dpo * pscale_ref[:, sl]).astype(bf16)
            dwpool_ref[gi] += _dot(pooled_g, dpw, TN)
            dpooled = _dot(dpw, wpool_ref[gi], NT)
            eext_s[pl.ds(0, tt), sl] = dpooled / jnp.minimum(pos1, float(w))
            stages = int(math.log2(w))
            src = eext_s
            for s in range(stages):
                n = tt + 8 * (stages - 1 - s)
                shift = 2 ** s
                val = src[pl.ds(0, n), sl] + src[pl.ds(shift, n), sl]
                if s == stages - 1:
                    wsum = val
                else:
                    tmp_s[pl.ds(0, n), sl] = val
                    src = tmp_s
            dproj_ref[:, 4 * RW + gi * DH: 4 * RW + (gi + 1) * DH] = (wsum - dpooled).astype(bf16)
        eext_s[pl.ds(tt, HALO), :] = eext_s[pl.ds(0, HALO), :]

        for h in range(HEADS):
            sl = slice(h * DH, (h + 1) * DH)
            dr = dcat[:, sl]
            o = oret_ref[:, sl]
            r = lax.rsqrt(jnp.mean(o * o, axis=-1, keepdims=True) + RMS_EPS)
            rn = o * r
            gg = g_ref[:, sl]
            sg = _sigmoid(gg)
            dproj_ref[:, 3 * RW + h * DH: 3 * RW + (h + 1) * DH] = (dr * rn * (sg * (1.0 + gg * (1.0 - sg)))).astype(bf16)
            drn = dr * (gg * sg)
            dout_s[:, sl] = (r * (drn - rn * jnp.mean(drn * rn, axis=-1, keepdims=True))).astype(bf16)

        for sub in reversed(range(tt // RET_TILE)):
            rows = pl.ds(sub * RET_TILE, RET_TILE)
            cos_t, sin_t = cos_ref[rows, :], sin_ref[rows, :]
            for h in range(HEADS):
                q = qkv_ref[rows, h * DH:(h + 1) * DH]
                k = qkv_ref[rows, RW + h * DH: RW + (h + 1) * DH]
                v = qkv_ref[rows, 2 * RW + h * DH: 2 * RW + (h + 1) * DH]
                do = dout_s[rows, h * DH:(h + 1) * DH]
                stb = states_ref[sub, h]
                dst = dstate_s[h]
                dstb = dst.astype(bf16)
                sb = (_dot(q, k, NT) * dmat_ref[h]).astype(bf16)
                dsb = (_dot(do, v, NT) * dmat_ref[h]).astype(bf16)
                dq = _dot(dsb, k) + _dot(do, stb, NT) * qd_ref[h]
                dk = _dot(dsb, q, TN) + _dot(v, dstb, NT) * kd_ref[h]
                dv = _dot(sb, do, TN) + _dot((k.astype(f32) * kd_ref[h]).astype(bf16), dstb)
                dstate_s[h] = dst * cdec[h] + _dot((q.astype(f32) * qd_ref[h]).astype(bf16), do, TN)
                dproj_ref[rows, h * DH:(h + 1) * DH] = (dq * cos_t - _swap_halves(dq) * sin_t).astype(bf16)
                dproj_ref[rows, RW + h * DH: RW + (h + 1) * DH] = (
                    (dk * cos_t - _swap_halves(dk) * sin_t) * K_SCALE).astype(bf16)
                dproj_ref[rows, 2 * RW + h * DH: 2 * RW + (h + 1) * DH] = dv.astype(bf16)

        gx_ref[...] = ALPHA * dz1 + _dot(dproj_ref[...], wint_ref[...])

        @pl.when(i == n_tiles - 1)
        def _():
            dwout_ref[...] = dwout_s[...].astype(bf16)
            _chip_exchange_finish(ein, eout, *sems)

    rtile = lambda w: pl.BlockSpec((tt, w), lambda i: (n_tiles - 1 - i, 0))
    hbm = pl.BlockSpec(memory_space=pltpu.HBM)
    out_shape = (
        jax.ShapeDtypeStruct((T, IN_W), bf16),
        jax.ShapeDtypeStruct((T, D), f32),
        jax.ShapeDtypeStruct((GROUPS, DH, DH), f32),
        jax.ShapeDtypeStruct((1, PW), f32),
        jax.ShapeDtypeStruct((D, D), bf16),
    ) + tuple(jax.ShapeDtypeStruct(e.shape, e.dtype) for e in exchange)
    return pl.pallas_call(
        body, name="mix_backward", grid=(n_tiles,), out_shape=out_shape,
        in_specs=[rtile(D), _const_spec((D, D)), rtile(3 * RW), rtile(RW), rtile(RW),
                  pl.BlockSpec((tt // RET_TILE, HEADS, DH, DH), lambda i: (n_tiles - 1 - i, 0, 0, 0)),
                  rtile(PW), rtile(D), rtile(DH), rtile(DH),
                  _const_spec((HEADS, RET_TILE, RET_TILE)), _const_spec((HEADS, RET_TILE, DH)),
                  _const_spec((HEADS, RET_TILE, DH)),
                  _const_spec((GROUPS, DH, DH)), _const_spec((1, PW)), _const_spec((IN_W, D))] + [hbm] * n_e,
        out_specs=(rtile(IN_W), rtile(D), pl.BlockSpec((GROUPS, DH, DH), lambda i: (0, 0, 0)),
                   pl.BlockSpec((1, PW), lambda i: (0, 0)),
                   pl.BlockSpec((D, D), lambda i: (0, 0), pipeline_mode=pl.Buffered(1))) + (hbm,) * n_e,
        scratch_shapes=[pltpu.VMEM((HEADS, DH, DH), f32), pltpu.VMEM((tt, RW), bf16),
                        pltpu.VMEM((tt + HALO, PW), f32), pltpu.VMEM((tt + HALO, PW), f32),
                        pltpu.VMEM((D, D), f32)] + _chip_exchange_sems(n_e),
        compiler_params=pltpu.CompilerParams(dimension_semantics=("arbitrary",), vmem_limit_bytes=V7X_VMEM_LIMIT,
                                             collective_id=CHIP_BARRIER),
    )(dz1, w_out, qkv, g, oret, states, pooled, cat, cos, sin, dmat, qd, kd, w_pool, pool_scale, w_in_t, *exchange)


def _weight_grad(a, b, name, tm, exchange=(), tk=2048):
    m = a.shape[1]
    n_m, n_k, n_e = m // tm, T // tk, len(exchange)

    def body(a_ref, b_ref, *rest):
        ein, o_ref, eout, (acc_s, *sems) = rest[:n_e], rest[n_e], rest[n_e + 1:2 * n_e + 1], rest[2 * n_e + 1:]
        i, k = pl.program_id(0), pl.program_id(1)

        if n_e:
            @pl.when((i == 0) & (k == 0))
            def _():
                _chip_exchange_start(ein, eout, *sems)

        @pl.when(k == 0)
        def _():
            acc_s[...] = jnp.zeros_like(acc_s)

        acc_s[...] += _dot(a_ref[...], b_ref[pl.ds(pl.multiple_of(k * tk, tk), tk), :].astype(bf16), TN)

        @pl.when(k == n_k - 1)
        def _():
            o_ref[...] = acc_s[...].astype(bf16)

        if n_e:
            @pl.when((i == n_m - 1) & (k == n_k - 1))
            def _():
                _chip_exchange_finish(ein, eout, *sems)

    hbm = pl.BlockSpec(memory_space=pltpu.HBM)
    return pl.pallas_call(
        body, name=name, grid=(n_m, n_k),
        out_shape=(jax.ShapeDtypeStruct((m, D), bf16),) + tuple(jax.ShapeDtypeStruct(e.shape, e.dtype) for e in exchange),
        in_specs=[pl.BlockSpec((tk, tm), lambda i, k: (k, i)),
                  pl.BlockSpec((T, D), lambda i, k: (0, 0), pipeline_mode=pl.Buffered(1))] + [hbm] * n_e,
        out_specs=(pl.BlockSpec((tm, D), lambda i, k: (i, 0)),) + (hbm,) * n_e,
        scratch_shapes=[pltpu.VMEM((tm, D), f32)] + _chip_exchange_sems(n_e),
        compiler_params=pltpu.CompilerParams(dimension_semantics=("arbitrary", "arbitrary"),
                                             vmem_limit_bytes=V7X_VMEM_LIMIT,
                                             collective_id=CHIP_BARRIER if n_e else None),
    )(a, b, *exchange)


CHIP_FLIPS = ((1, 0), (0, 1), (1, 1))
PAIR_BARRIER, CHIP_BARRIER, GATHER_BARRIER, CHIP_AND_GATHER_BARRIER = 0, 1, 2, 3


def _barrier(peers):
    sem = pltpu.get_barrier_semaphore()
    for peer in peers:
        pl.semaphore_signal(sem, inc=1, device_id=peer, device_id_type=pl.DeviceIdType.MESH)
    pl.semaphore_wait(sem, len(peers))


def _me():
    return lax.axis_index("x"), lax.axis_index("y"), lax.axis_index("c")


def _chip(me, k):
    x, y, _ = me
    if k == 0:
        return x, y
    fx, fy = CHIP_FLIPS[k - 1]
    return (1 - x if fx else x), (1 - y if fy else y)


def _slot(x, y, c):
    return 4 * x + 2 * y + c


def _remote(src, dst, send_sem, recv_sem, to):
    return pltpu.make_async_remote_copy(src_ref=src, dst_ref=dst, send_sem=send_sem, recv_sem=recv_sem,
                                        device_id=to, device_id_type=pl.DeviceIdType.MESH)


def _gather_sems(n):
    return [pltpu.SemaphoreType.DMA((7, n)), pltpu.SemaphoreType.DMA((7, n)), pltpu.SemaphoreType.DMA((n,))] if n else []


def _gather_copy(k, j, gin, gout, send_sems, recv_sems, sending):
    x, y, c = _me()
    sibling, x_chip, y_chip, d_chip = (x, y, 1 - c), (1 - x, y), (x, 1 - y), (1 - x, 1 - y)
    south = c == 0
    passed_on = (jnp.where(south, 1 - x, x), jnp.where(south, y, 1 - y), c)
    src, to = gin[j], sibling
    if sending:
        block = {0: (x, y, c), 1: (x, y, c), 2: (x, y, c), 3: passed_on, 4: (*x_chip, c), 5: (*y_chip, c), 6: (*d_chip, c)}[k]
        to = {1: (*x_chip, c), 2: (*y_chip, c), 3: (jnp.where(south, x, 1 - x), jnp.where(south, 1 - y, y), c)}.get(k, sibling)
        if k >= 3:
            src = gout[j].at[_slot(*block)]
    else:
        block = {0: sibling, 1: (*x_chip, c), 2: (*y_chip, c), 3: (*d_chip, c), 4: (*x_chip, 1 - c), 5: (*y_chip, 1 - c),
                 6: (*d_chip, 1 - c)}[k]
    return _remote(src, gout[j].at[_slot(*block)], send_sems.at[k, j], recv_sems.at[k, j], to)


def _gather_do(ks, action, gin, gout, send_sems, recv_sems):
    for k in ks:
        for j in range(len(gin)):
            cp = _gather_copy(k, j, gin, gout, send_sems, recv_sems, action != "wait_recv")
            getattr(cp, action)()


def _gather_peers():
    x, y, c = _me()
    return [(x, y, 1 - c), (1 - x, y, c), (x, 1 - y, c)]


def _gather_start(gin, gout, send_sems, recv_sems, local_sems, barrier=True):
    if barrier:
        _barrier(_gather_peers())
    for j in range(len(gin)):
        pltpu.make_async_copy(gin[j], gout[j].at[_slot(*_me())], local_sems.at[j]).start()
    _gather_do((0, 1, 2), "start", gin, gout, send_sems, recv_sems)


def _gather_forward(gin, gout, send_sems, recv_sems, local_sems):
    _gather_do((1, 2), "wait_recv", gin, gout, send_sems, recv_sems)
    _gather_do((3, 4, 5), "start", gin, gout, send_sems, recv_sems)


def _gather_finish(gin, gout, send_sems, recv_sems, local_sems):
    _gather_do((3,), "wait_recv", gin, gout, send_sems, recv_sems)
    _gather_do((6,), "start", gin, gout, send_sems, recv_sems)
    _gather_do((0, 4, 5, 6), "wait_recv", gin, gout, send_sems, recv_sems)
    _gather_do(range(7), "wait_send", gin, gout, send_sems, recv_sems)
    for j in range(len(gin)):
        pltpu.make_async_copy(gin[j], gout[j].at[_slot(*_me())], local_sems.at[j]).wait()


def _all_gather(blocks, name):
    n = len(blocks)

    def body(*refs):
        gin, gout, sems = refs[:n], refs[n:2 * n], refs[2 * n:]
        _gather_start(gin, gout, *sems)
        _gather_forward(gin, gout, *sems)
        _gather_finish(gin, gout, *sems)

    hbm = pl.BlockSpec(memory_space=pltpu.HBM)
    return pl.pallas_call(
        body, name=name,
        out_shape=tuple(jax.ShapeDtypeStruct((N_DEV,) + b.shape, b.dtype) for b in blocks),
        in_specs=[hbm] * n, out_specs=(hbm,) * n, scratch_shapes=_gather_sems(n),
        compiler_params=pltpu.CompilerParams(collective_id=GATHER_BARRIER),
    )(*blocks)


def _pair_reduce(parts, name):
    n = len(parts)

    def body(*refs):
        ins, own, others, landing, mine = (refs[k * n:(k + 1) * n] for k in range(5))
        send_sems, recv_sems, local_sems = refs[5 * n:]
        me = _me()
        x, y, c = me
        sibling = (x, y, 1 - c)
        _barrier([sibling])
        sends, loads = [], []
        for k in range(4):
            for j in range(n):
                cp = _remote(ins[j].at[_slot(*_chip(me, k), 1 - c)], landing[j].at[k], send_sems.at[k, j],
                             recv_sems.at[k, j], sibling)
                cp.start()
                sends.append(cp)
                ld = pltpu.make_async_copy(ins[j].at[_slot(*_chip(me, k), c)], mine[j].at[k], local_sems.at[k, j])
                ld.start()
                loads.append(ld)
        for k in range(4):
            for j in range(n):
                loads[k * n + j].wait()
                _remote(ins[j].at[0], landing[j].at[k], send_sems.at[k, j], recv_sems.at[k, j], sibling).wait_recv()
                total = mine[j][k].astype(f32) + landing[j][k].astype(f32)
                if k == 0:
                    own[j][...] = total.astype(own[j].dtype)
                else:
                    others[j][k - 1] = total.astype(others[j].dtype)
        for cp in sends:
            cp.wait_send()

    vm = pl.BlockSpec(memory_space=pltpu.VMEM)
    return pl.pallas_call(
        body, name=name,
        out_shape=tuple(jax.ShapeDtypeStruct(p.shape[1:], p.dtype) for p in parts)
        + tuple(jax.ShapeDtypeStruct((3,) + p.shape[1:], p.dtype) for p in parts),
        in_specs=[pl.BlockSpec(memory_space=pltpu.HBM)] * n, out_specs=(vm,) * (2 * n),
        scratch_shapes=[pltpu.VMEM((4,) + p.shape[1:], p.dtype) for p in parts] * 2
        + [pltpu.SemaphoreType.DMA((4, n)), pltpu.SemaphoreType.DMA((4, n)), pltpu.SemaphoreType.DMA((4, n))],
        compiler_params=pltpu.CompilerParams(vmem_limit_bytes=V7X_VMEM_LIMIT, collective_id=PAIR_BARRIER),
    )(*parts)


def _chip_exchange_sems(n):
    return [pltpu.SemaphoreType.DMA((3, n)), pltpu.SemaphoreType.DMA((3, n))] if n else []


def _chip_exchange_copy(k, j, ein, eout, send_sems, recv_sems):
    me = _me()
    return _remote(ein[j].at[k - 1], eout[j].at[k - 1], send_sems.at[k - 1, j], recv_sems.at[k - 1, j],
                   (*_chip(me, k), me[2]))


def _chip_peers():
    me = _me()
    return [(*_chip(me, k), me[2]) for k in range(1, 4)]


def _chip_exchange_start(ein, eout, send_sems, recv_sems, barrier=True):
    if barrier:
        _barrier(_chip_peers())
    for k in range(1, 4):
        for j in range(len(ein)):
            _chip_exchange_copy(k, j, ein, eout, send_sems, recv_sems).start()


def _chip_exchange_finish(ein, eout, send_sems, recv_sems):
    for k in range(1, 4):
        for j in range(len(ein)):
            _chip_exchange_copy(k, j, ein, eout, send_sems, recv_sems).wait_recv()
    for k in range(1, 4):
        for j in range(len(ein)):
            _chip_exchange_copy(k, j, ein, eout, send_sems, recv_sems).wait_send()


def _chip_exchange_and_gather(others, blocks, name):
    n_e, n_g = len(others), len(blocks)

    def body(*refs):
        ein, gin, eout, gout = refs[:n_e], refs[n_e:n_e + n_g], refs[n_e + n_g:2 * n_e + n_g], refs[2 * n_e + n_g:2 * (n_e + n_g)]
        esems, gsems = refs[2 * (n_e + n_g):2 * (n_e + n_g) + 2], refs[2 * (n_e + n_g) + 2:]
        _barrier(_chip_peers() + _gather_peers()[:1])
        _chip_exchange_start(ein, eout, *esems, barrier=False)
        _gather_start(gin, gout, *gsems, barrier=False)
        _gather_forward(gin, gout, *gsems)
        _gather_finish(gin, gout, *gsems)
        _chip_exchange_finish(ein, eout, *esems)

    hbm = pl.BlockSpec(memory_space=pltpu.HBM)
    return pl.pallas_call(
        body, name=name,
        out_shape=tuple(jax.ShapeDtypeStruct(e.shape, e.dtype) for e in others)
        + tuple(jax.ShapeDtypeStruct((N_DEV,) + b.shape, b.dtype) for b in blocks),
        in_specs=[hbm] * (n_e + n_g), out_specs=(hbm,) * (n_e + n_g),
        scratch_shapes=_chip_exchange_sems(n_e) + _gather_sems(n_g),
        compiler_params=pltpu.CompilerParams(collective_id=CHIP_AND_GATHER_BARRIER),
    )(*others, *blocks)


def _sum_parts(owns, arrived, name):
    n = len(owns)

    def body(*refs):
        for own, arr, out in zip(refs[:n], refs[n:2 * n], refs[2 * n:]):
            acc = own[...].astype(f32)
            for k in range(3):
                acc = acc + arr[k].astype(f32)
            out[...] = acc

    vm = pl.BlockSpec(memory_space=pltpu.VMEM)
    return pl.pallas_call(
        body, name=name, out_shape=tuple(jax.ShapeDtypeStruct(o.shape, f32) for o in owns),
        in_specs=[vm] * (2 * n), out_specs=(vm,) * n,
        compiler_params=pltpu.CompilerParams(vmem_limit_bytes=V7X_VMEM_LIMIT),
    )(*owns, *arrived)


ADAM_C1 = 1.0 / (1.0 - ADAM_B1 ** ADAM_STEP)
ADAM_C2 = 1.0 / (1.0 - ADAM_B2 ** ADAM_STEP)


def _adam_update(w, g, m, v):
    m = ADAM_B1 * m + (1.0 - ADAM_B1) * g
    v = ADAM_B2 * v + (1.0 - ADAM_B2) * (g * g)
    return -ADAM_LR * ((m * ADAM_C1) / (jnp.sqrt(v * ADAM_C2) + ADAM_EPS) + ADAM_WD * w), m, v


def _sum_adamw(own, arrived, w, m, v, name, steps):
    rows = own.shape[0]
    br = rows // steps

    def body(own_ref, arr_ref, w_ref, m_ref, v_ref, g_out, d_out, m_out, v_out):
        g = own_ref[...].astype(f32)
        for k in range(3):
            g = g + arr_ref[k].astype(f32)
        g_out[...] = g
        d_out[...], m_out[...], v_out[...] = _adam_update(w_ref[...], g, m_ref[...], v_ref[...])

    blk = pl.BlockSpec((br, D), lambda i: (i, 0))
    return pl.pallas_call(
        body, name=name, grid=(steps,), out_shape=(jax.ShapeDtypeStruct((rows, D), f32),) * 4,
        in_specs=[blk, pl.BlockSpec((3, br, D), lambda i: (0, i, 0)), blk, blk, blk], out_specs=(blk,) * 4,
        compiler_params=pltpu.CompilerParams(dimension_semantics=("parallel",), vmem_limit_bytes=V7X_VMEM_LIMIT),
    )(own, arrived, w, m, v)


def _adamw(ws, gs, ms, vs, name):
    n = len(ws)

    def body(*refs):
        w_r, g_r, m_r, v_r = (refs[k * n:(k + 1) * n] for k in range(4))
        d_o, m_o, v_o = (refs[(4 + k) * n:(5 + k) * n] for k in range(3))
        for j in range(n):
            d_o[j][...], m_o[j][...], v_o[j][...] = _adam_update(w_r[j][...], g_r[j][...], m_r[j][...], v_r[j][...])

    vm = pl.BlockSpec(memory_space=pltpu.VMEM)
    shapes = tuple(jax.ShapeDtypeStruct(w.shape, f32) for w in ws)
    return pl.pallas_call(
        body, name=name, out_shape=shapes * 3, in_specs=[vm] * (4 * n), out_specs=tuple([vm] * (3 * n)),
        compiler_params=pltpu.CompilerParams(vmem_limit_bytes=V7X_VMEM_LIMIT),
    )(*ws, *gs, *ms, *vs)


SMALL = (("w_pool", GROUPS * DH * DH), ("pool_scale", PW), ("ln1_g", D), ("ln1_b", D), ("conv_b", D_FF),
         ("ln2_g", D), ("ln2_b", D), ("conv_w", 3 * D_FF), ("loss", 1))
SMALL_ROWS = 640


def _pack(named):
    flat = jnp.concatenate([named[k].reshape(-1) for k, _ in SMALL])
    return jnp.pad(flat, (0, SMALL_ROWS * 128 - flat.shape[0])).reshape(SMALL_ROWS, 128)


def _unpack(packed):
    flat, out, at = packed.reshape(-1), {}, 0
    for k, size in SMALL:
        out[k] = flat[at:at + size]
        at += size
    return out


def kernel(x, w_in, w_pool, pool_scale, w_out, ln1_g, ln1_b, w_up, conv_w, conv_b, w_down, ln2_g, ln2_b, loss_target, m_w_in, m_w_pool, m_pool_scale, m_w_out, m_ln1_g, m_ln1_b, m_w_up, m_conv_w, m_conv_b, m_w_down, m_ln2_g, m_ln2_b, v_w_in, v_w_pool, v_pool_scale, v_w_out, v_ln1_g, v_ln1_b, v_w_up, v_conv_w, v_conv_b, v_w_down, v_ln2_g, v_ln2_b):
    me = 4 * lax.axis_index("x") + 2 * lax.axis_index("y") + lax.axis_index("c")
    x2, tgt = x[0], loss_target[0]

    g_in, g_out, g_cw = _all_gather([w_in[0].T.astype(bf16), w_out[0].astype(bf16), conv_w[0]], "gather_weights")
    w_in_t = g_in.reshape(IN_W, D)
    w_out_f = g_out.reshape(D, D)
    conv_w_f = jnp.transpose(g_cw, (1, 0, 2)).reshape(3, D_FF)
    w_pool_b = w_pool[0].astype(bf16)

    cos, sin = _rope_tables()
    dmat, qd, kd, cdec = _decay_tables(RET_TILE)

    qkv, g, oret, states, cat, pooled, xhat1, rstd1, x1b, g_up, g_down = _mix_forward(
        x2, w_in_t, cos, sin, dmat, qd, kd, cdec, w_pool_b, pool_scale, w_out_f, ln1_g, ln1_b,
        gather=[w_up[0].T.astype(bf16), w_down[0].astype(bf16)])
    w_up_t = g_up.reshape(2 * D_FF, D)
    w_down_f = g_down.reshape(D_FF, D)
    dz1, dz2b, du, f, loss8, d_ln2_g, d_ln2_b, d_ln1_g, d_ln1_b, d_conv_b, d_conv_w = _ffn_forward_backward(
        xhat1, rstd1, ln1_g, ln1_b, w_up_t, conv_w_f, conv_b, w_down_f, ln2_g, ln2_b, tgt)

    (dw_down,) = _weight_grad(f, dz2b, "grad_w_down", tm=D_FF // 2)
    own_down, oth_down = _pair_reduce([dw_down.reshape(N_DEV, ROWS_DOWN, D)], "pair_reduce_down")
    dw_up_t, arr_down = _weight_grad(du, x1b, "grad_w_up", tm=D_FF // 2, exchange=[oth_down])
    own_up, oth_up = _pair_reduce([dw_up_t.reshape(N_DEV, ROWS_UP, D)], "pair_reduce_up")
    dproj, grad_x, d_w_pool, d_pool_scale, dw_out, arr_up = _mix_backward(
        dz1, w_out_f, qkv, g, oret, states, pooled, cat, cos, sin, dmat, qd, kd, cdec, w_pool_b, pool_scale, w_in_t,
        exchange=[oth_up])
    small = _pack({"w_pool": d_w_pool, "pool_scale": d_pool_scale, "ln1_g": d_ln1_g, "ln1_b": d_ln1_b,
                   "conv_b": d_conv_b, "ln2_g": d_ln2_g, "ln2_b": d_ln2_b, "conv_w": d_conv_w, "loss": loss8[0, :1]})
    own_out, own_small, oth_out, oth_small = _pair_reduce(
        [dw_out.reshape(N_DEV, ROWS_OUT, D), small.reshape(N_DEV, SMALL_ROWS // N_DEV, 128)], "pair_reduce_out")
    dw_in_t, arr_out, arr_small = _weight_grad(dproj, x2, "grad_w_in", tm=IN_W // 2, exchange=[oth_out, oth_small])
    own_in, oth_in = _pair_reduce([dw_in_t.reshape(N_DEV, ROWS_IN, D)], "pair_reduce_in")
    (small_piece,) = _sum_parts([own_small], [arr_small], "sum_small_grads")
    arr_in, gs_small = _chip_exchange_and_gather([oth_in], [small_piece], "exchange_in_gather_small")

    names = ["w_in", "w_pool", "pool_scale", "w_out", "ln1_g", "ln1_b", "w_up", "conv_w", "conv_b", "w_down",
             "ln2_g", "ln2_b"]
    w_d = dict(w_in=w_in, w_pool=w_pool, pool_scale=pool_scale, w_out=w_out, ln1_g=ln1_g, ln1_b=ln1_b, w_up=w_up,
               conv_w=conv_w, conv_b=conv_b, w_down=w_down, ln2_g=ln2_g, ln2_b=ln2_b)
    m_d = dict(w_in=m_w_in, w_pool=m_w_pool, pool_scale=m_pool_scale, w_out=m_w_out, ln1_g=m_ln1_g, ln1_b=m_ln1_b,
               w_up=m_w_up, conv_w=m_conv_w, conv_b=m_conv_b, w_down=m_w_down, ln2_g=m_ln2_g, ln2_b=m_ln2_b)
    v_d = dict(w_in=v_w_in, w_pool=v_w_pool, pool_scale=v_pool_scale, w_out=v_w_out, ln1_g=v_ln1_g, ln1_b=v_ln1_b,
               w_up=v_w_up, conv_w=v_conv_w, conv_b=v_conv_b, w_down=v_w_down, ln2_g=v_ln2_g, ln2_b=v_ln2_b)
    g_d, delta, new_m, new_v = {}, {}, {}, {}

    big = (("w_in", own_in, arr_in, True, 4), ("w_out", own_out, arr_out, False, 2),
           ("w_up", own_up, arr_up, True, 4), ("w_down", own_down, arr_down, False, 2))
    for k, own, arr, transposed, steps in big:
        lay = (lambda a: a[0].T) if transposed else (lambda a: a[0])
        back = (lambda a: a.T[None]) if transposed else (lambda a: a[None])
        res = _sum_adamw(own, arr, lay(w_d[k]), lay(m_d[k]), lay(v_d[k]), "adamw_" + k, steps)
        g_d[k], delta[k], new_m[k], new_v[k] = (back(r) for r in res)

    gsm = _unpack(gs_small)
    gsm["conv_w"] = lax.dynamic_slice(gsm["conv_w"].reshape(3, D_FF), (0, me * (D_FF // N_DEV)), (3, D_FF // N_DEV))
    two_d = lambda a: a.reshape(-1, a.shape[-1])
    group = [k for k in names if k not in g_d]
    for k in group:
        g_d[k] = gsm[k].reshape(w_d[k].shape)
    res = _adamw([two_d(w_d[k]) for k in group], [two_d(g_d[k]) for k in group], [two_d(m_d[k]) for k in group],
                 [two_d(v_d[k]) for k in group], "adamw_small")
    for j, k in enumerate(group):
        delta[k] = res[j].reshape(w_d[k].shape)
        new_m[k] = res[len(group) + j].reshape(w_d[k].shape)
        new_v[k] = res[2 * len(group) + j].reshape(w_d[k].shape)

    loss = gsm["loss"].reshape(())
    return (loss, grad_x[None], *[g_d[k] for k in names], *[delta[k] for k in names], *[new_m[k] for k in names],
            *[new_v[k] for k in names])
```

```python
import functools
import math

import numpy as np
import jax
import jax.numpy as jnp
from jax import lax
from jax.experimental import pallas as pl
from jax.experimental.pallas import tpu as pltpu

f32 = jnp.float32
bf16 = jnp.bfloat16

N_DEV = 8
T = 4096
D = 1024
CHUNK = 64
MIX_TILE = 512
RET_TILE = 256
HEADS = 4
DH = 128
RW = HEADS * DH
PW = 512
GROUPS = 4
WINDOWS = (2, 4, 8, 16)
IN_W = 4 * RW + PW
D_FF = 2816
LN_EPS = 1e-5
RMS_EPS = 1e-6
ALPHA = 2.0 ** 0.25
K_SCALE = DH ** -0.5

ADAM_LR = 0.001
ADAM_B1 = 0.9
ADAM_B2 = 0.999
ADAM_EPS = 1e-08
ADAM_WD = 0.01
ADAM_STEP = 10

ROWS_IN, ROWS_OUT, ROWS_UP, ROWS_DOWN = IN_W // N_DEV, D // N_DEV, 2 * D_FF // N_DEV, D_FF // N_DEV

V7X_VMEM_LIMIT = 56 * 2 ** 20
HALO = 32

NT = (((1,), (1,)), ((), ()))
TN = (((0,), (0,)), ((), ()))
NN = (((1,), (0,)), ((), ()))


def _dot(a, b, dims=NN):
    return lax.dot_general(a, b, dims, preferred_element_type=f32)


def _const_spec(shape):
    zeros = (0,) * len(shape)
    return pl.BlockSpec(shape, lambda i: zeros, pipeline_mode=pl.Buffered(1))


def _sigmoid(x):
    return 0.5 * jnp.tanh(0.5 * x) + 0.5


def _decay_tables(tt):
    h = np.arange(HEADS, dtype=np.float64)
    log_gamma = np.log(1.0 - 2.0 ** (-5.0 - h)).astype(np.float32).astype(np.float64)[:, None, None]
    idx = np.arange(tt, dtype=np.float64)
    visible = (idx[None, :] // CHUNK) <= (idx[:, None] // CHUNK)
    mask = np.where(visible[None], np.exp(log_gamma * np.abs(idx[:, None] - idx[None, :])[None]), 0.0)
    qd = np.broadcast_to(np.exp(log_gamma * (idx[None, :, None] + 1.0)), (HEADS, tt, DH))
    kd = np.broadcast_to(np.exp(log_gamma * (tt - 1.0 - idx[None, :, None])), (HEADS, tt, DH))
    cd = np.exp(log_gamma[:, 0, 0] * tt)
    return (jnp.asarray(mask, f32), jnp.asarray(qd, f32), jnp.asarray(kd, f32), [float(c) for c in cd])


def _rope_tables():
    inv_freq = (10000.0 ** (-np.arange(0, DH, 2, dtype=np.float64) / DH)).astype(np.float32)
    ang = (np.arange(T, dtype=np.float32)[:, None] * inv_freq[None, :]).astype(np.float64)
    cos, sin = np.cos(ang), np.sin(ang)
    return (jnp.asarray(np.concatenate([cos, cos], axis=1), f32), jnp.asarray(np.concatenate([-sin, sin], axis=1), f32))


def _swap_halves(t):
    return pltpu.roll(t, DH // 2, axis=1)


def _mix_forward(x, w_in_t, cos, sin, dmat, qd, kd, cdec, w_pool, pool_scale, w_out, ln1_g, ln1_b, gather,
                 tt=MIX_TILE):
    n_tiles = T // tt
    n_g = len(gather)

    def body(x_ref, wint_ref, cos_ref, sin_ref, dmat_ref, qd_ref, kd_ref, wpool_ref, pscale_ref, wout_ref,
             g1_ref, b1_ref, *rest):
        gin, rest = rest[:n_g], rest[n_g:]
        qkv_ref, g_ref, oret_ref, states_ref, cat_ref, pooled_ref, xhat_ref, rstd_ref, x1b_ref = rest[:9]
        gout, (state_s, pext_s, tmp_s, *sems) = rest[9:9 + n_g], rest[9 + n_g:]
        i = pl.program_id(0)

        @pl.when(i == 0)
        def _():
            state_s[...] = jnp.zeros_like(state_s)
            pext_s[pl.ds(0, HALO), :] = jnp.zeros((HALO, PW), f32)
            _gather_start(gin, gout, *sems)

        @pl.when(i == n_tiles - 2)
        def _():
            _gather_forward(gin, gout, *sems)

        xb = x_ref[...].astype(bf16)
        cos_t, sin_t = cos_ref[...], sin_ref[...]
        for part in range(2):
            pr = _dot(xb, wint_ref[pl.ds(part * RW, RW), :], NT)
            for h in range(HEADS):
                t = pr[:, h * DH:(h + 1) * DH]
                r = t * cos_t + _swap_halves(t) * sin_t
                if part == 1:
                    r = r * K_SCALE
                qkv_ref[:, part * RW + h * DH: part * RW + (h + 1) * DH] = r.astype(bf16)
        qkv_ref[:, 2 * RW:3 * RW] = _dot(xb, wint_ref[pl.ds(2 * RW, RW), :], NT).astype(bf16)
        g_ref[...] = _dot(xb, wint_ref[pl.ds(3 * RW, RW), :], NT)
        pext_s[pl.ds(HALO, tt), :] = _dot(xb, wint_ref[pl.ds(4 * RW, PW), :], NT)

        for sub in range(tt // RET_TILE):
            rows = pl.ds(sub * RET_TILE, RET_TILE)
            for h in range(HEADS):
                q = qkv_ref[rows, h * DH:(h + 1) * DH]
                k = qkv_ref[rows, RW + h * DH: RW + (h + 1) * DH]
                v = qkv_ref[rows, 2 * RW + h * DH: 2 * RW + (h + 1) * DH]
                s = _dot(q, k, NT) * dmat_ref[h]
                st = state_s[h]
                stb = st.astype(bf16)
                states_ref[sub, h] = stb
                oret_ref[rows, h * DH:(h + 1) * DH] = (_dot(s.astype(bf16), v)
                                                      + _dot((q.astype(f32) * qd_ref[h]).astype(bf16), stb))
                state_s[h] = st * cdec[h] + _dot((k.astype(f32) * kd_ref[h]).astype(bf16), v, TN)

        for h in range(HEADS):
            sl = slice(h * DH, (h + 1) * DH)
            o = oret_ref[:, sl]
            r = lax.rsqrt(jnp.mean(o * o, axis=-1, keepdims=True) + RMS_EPS)
            gg = g_ref[:, sl]
            cat_ref[:, sl] = (o * r * (gg * _sigmoid(gg))).astype(bf16)

        pos1 = (i * tt + lax.broadcasted_iota(jnp.int32, (tt, 1), 0) + 1).astype(f32)
        for gi, w in enumerate(WINDOWS):
            sl = slice(gi * DH, (gi + 1) * DH)
            stages = int(math.log2(w))
            src = pext_s
            for s in range(stages):
                lo = HALO - 8 * (stages - 1 - s)
                n = tt + HALO - lo
                shift = 2 ** s
                val = src[pl.ds(lo, n), sl] + src[pl.ds(lo - shift, n), sl]
                if s == stages - 1:
                    wsum = val
                else:
                    tmp_s[pl.ds(lo, n), sl] = val
                    src = tmp_s
            p_g = pext_s[pl.ds(HALO, tt), sl]
            pooled = (wsum / jnp.minimum(pos1, float(w)) - p_g).astype(bf16)
            pooled_ref[:, sl] = pooled
            y = _dot(pooled, wpool_ref[gi]) * pscale_ref[:, sl]
            cat_ref[:, RW + gi * DH: RW + (gi + 1) * DH] = y.astype(bf16)
        pext_s[pl.ds(0, HALO), :] = pext_s[pl.ds(tt, HALO), :]

        z = ALPHA * x_ref[...] + _dot(cat_ref[...], wout_ref[...])
        mu = jnp.mean(z, axis=-1, keepdims=True)
        zc = z - mu
        rstd = lax.rsqrt(jnp.mean(zc * zc, axis=-1, keepdims=True) + LN_EPS)
        xhat = zc * rstd
        xhat_ref[...] = xhat
        rstd_ref[...] = rstd
        x1b_ref[...] = (xhat * g1_ref[...] + b1_ref[...]).astype(bf16)

        @pl.when(i == n_tiles - 1)
        def _():
            _gather_finish(gin, gout, *sems)

    tile = lambda w: pl.BlockSpec((tt, w), lambda i: (i, 0))
    hbm = pl.BlockSpec(memory_space=pltpu.HBM)
    out_shape = (
        jax.ShapeDtypeStruct((T, 3 * RW), bf16),
        jax.ShapeDtypeStruct((T, RW), f32),
        jax.ShapeDtypeStruct((T, RW), f32),
        jax.ShapeDtypeStruct((T // RET_TILE, HEADS, DH, DH), bf16),
        jax.ShapeDtypeStruct((T, D), bf16),
        jax.ShapeDtypeStruct((T, PW), bf16),
        jax.ShapeDtypeStruct((T, D), f32),
        jax.ShapeDtypeStruct((T, 1), f32),
        jax.ShapeDtypeStruct((T, D), bf16),
    ) + tuple(jax.ShapeDtypeStruct((N_DEV,) + b.shape, b.dtype) for b in gather)
    return pl.pallas_call(
        body, name="mix_forward", grid=(n_tiles,), out_shape=out_shape,
        in_specs=[tile(D), _const_spec((IN_W, D)), tile(DH), tile(DH),
                  _const_spec((HEADS, RET_TILE, RET_TILE)), _const_spec((HEADS, RET_TILE, DH)),
                  _const_spec((HEADS, RET_TILE, DH)),
                  _const_spec((GROUPS, DH, DH)), _const_spec((1, PW)), _const_spec((D, D)),
                  _const_spec((1, D)), _const_spec((1, D))] + [hbm] * n_g,
        out_specs=(tile(3 * RW), tile(RW), tile(RW),
                   pl.BlockSpec((tt // RET_TILE, HEADS, DH, DH), lambda i: (i, 0, 0, 0)),
                   tile(D), tile(PW), tile(D), tile(1), tile(D)) + (hbm,) * n_g,
        scratch_shapes=[pltpu.VMEM((HEADS, DH, DH), f32), pltpu.VMEM((tt + HALO, PW), f32),
                        pltpu.VMEM((tt + HALO, PW), f32)] + _gather_sems(n_g),
        compiler_params=pltpu.CompilerParams(dimension_semantics=("arbitrary",), vmem_limit_bytes=V7X_VMEM_LIMIT,
                                             collective_id=GATHER_BARRIER),
    )(x, w_in_t, cos, sin, dmat, qd, kd, w_pool, pool_scale, w_out, ln1_g, ln1_b, *gather)


def _ffn_forward_backward(xhat1, rstd1, ln1_g, ln1_b, w_up_t, conv_w, conv_b, w_down, ln2_g, ln2_b, target,
                          tt=256, widths=(512, 512, 512, 512, 512, 256)):
    n_tiles = T // tt
    assert sum(widths) == D_FF and all(w % 128 == 0 for w in widths)
    chunks = [(sum(widths[:c]), w) for c, w in enumerate(widths)]
    FH = 16
    hb = tt // FH

    def body(xhat_ref, halo_ref, rstd_ref, g1_ref, b1_ref, wupt_ref, cw_ref, cb_ref, wdown_ref, g2_ref, b2_ref, tgt_ref,
             dz1_ref, dz2b_ref, du_ref, f_ref, loss_ref, dg2_ref, db2_ref, dg1_ref, db1_ref, dcb_ref, dcw_ref,
             gext_s, val_s, dhext_s):
        i = pl.program_id(0)
        tile_idx = n_tiles - 1 - i

        def rd(ref, off, lo, w):
            return jnp.concatenate([ref[lo // 128 + k, pl.ds(off, tt), :] for k in range(w // 128)], axis=1)

        def wr(ref, lo, val):
            for k in range(val.shape[1] // 128):
                ref[lo // 128 + k, pl.ds(0, val.shape[0]), :] = val[:, k * 128:(k + 1) * 128]

        @pl.when(i == 0)
        def _():
            for r in (loss_ref, dg2_ref, db2_ref, dg1_ref, db1_ref, dcb_ref, dcw_ref):
                r[...] = jnp.zeros_like(r)
            dhext_s[:, pl.ds(tt, 8), :] = jnp.zeros((D_FF // 128, 8, 128), f32)

        g1, b1 = g1_ref[...], b1_ref[...]
        xhat = xhat_ref[...]
        x1 = xhat * g1 + b1
        x1b = x1.astype(bf16)
        x1h = ((halo_ref[...] * g1 + b1) * jnp.where(tile_idx == 0, 0.0, 1.0)).astype(bf16)
        x1ext = jnp.concatenate([x1h, x1b], axis=0)

        for lo, w in chunks:
            cs = slice(lo, lo + w)
            val = _dot(x1b, wupt_ref[pl.ds(lo, w), :], NT)
            gate_ext = _dot(x1ext, wupt_ref[pl.ds(D_FF + lo, w), :], NT)
            wr(gext_s, lo, gate_ext)
            hh = (cb_ref[:, cs] + cw_ref[0:1, cs] * rd(gext_s, FH - 2, lo, w) + cw_ref[1:2, cs] * rd(gext_s, FH - 1, lo, w)
                  + cw_ref[2:3, cs] * gate_ext[FH:])
            sg = _sigmoid(hh)
            act = hh * sg
            wr(dhext_s, lo, act)
            val_s[:, cs] = val * (sg + act * (1.0 - sg))
            f_ref[:, cs] = (act * val).astype(bf16)

        z = ALPHA * x1 + _dot(f_ref[...], wdown_ref[...])
        mu = jnp.mean(z, axis=-1, keepdims=True)
        zc = z - mu
        rstd2 = lax.rsqrt(jnp.mean(zc * zc, axis=-1, keepdims=True) + LN_EPS)
        xh2 = zc * rstd2
        diff = xh2 * g2_ref[...] + b2_ref[...] - tgt_ref[...]
        loss_ref[...] += 0.5 * jnp.sum(diff * diff) / D
        dy = diff * (1.0 / D)
        dg2_ref[...] += jnp.sum(dy * xh2, axis=0, keepdims=True)
        db2_ref[...] += jnp.sum(dy, axis=0, keepdims=True)
        dyg = dy * g2_ref[...]
        dz2 = rstd2 * (dyg - jnp.mean(dyg, axis=-1, keepdims=True) - xh2 * jnp.mean(dyg * xh2, axis=-1, keepdims=True))
        dz2b = dz2.astype(bf16)
        dz2b_ref[...] = dz2b

        for lo, w in chunks:
            cs = slice(lo, lo + w)
            df = _dot(dz2b, wdown_ref[pl.ds(lo, w), :], NT)
            dval = df * rd(dhext_s, 0, lo, w)
            dh = df * val_s[:, cs]
            wr(dhext_s, lo, dh)
            dh1, dh2, g0 = rd(dhext_s, 1, lo, w), rd(dhext_s, 2, lo, w), rd(gext_s, FH, lo, w)
            dcb_ref[:, cs] += jnp.sum(dh, axis=0, keepdims=True)
            dcw_ref[0:1, cs] += jnp.sum(dh2 * g0, axis=0, keepdims=True)
            dcw_ref[1:2, cs] += jnp.sum(dh1 * g0, axis=0, keepdims=True)
            dcw_ref[2:3, cs] += jnp.sum(dh * g0, axis=0, keepdims=True)
            dgate = cw_ref[2:3, cs] * dh + cw_ref[1:2, cs] * dh1 + cw_ref[0:1, cs] * dh2
            du_ref[:, cs] = dval.astype(bf16)
            du_ref[:, D_FF + lo: D_FF + lo + w] = dgate.astype(bf16)
        dhext_s[:, pl.ds(tt, 8), :] = dhext_s[:, pl.ds(0, 8), :]
        dx1 = ALPHA * dz2 + _dot(du_ref[...], wupt_ref[...])

        dg1_ref[...] += jnp.sum(dx1 * xhat, axis=0, keepdims=True)
        db1_ref[...] += jnp.sum(dx1, axis=0, keepdims=True)
        dxg = dx1 * g1
        dz1_ref[...] = rstd_ref[...] * (dxg - jnp.mean(dxg, axis=-1, keepdims=True)
                                        - xhat * jnp.mean(dxg * xhat, axis=-1, keepdims=True))

    rtile = lambda w: pl.BlockSpec((tt, w), lambda i: (n_tiles - 1 - i, 0))
    acc = lambda shape: pl.BlockSpec(shape, lambda i: (0, 0))
    out_shape = (
        jax.ShapeDtypeStruct((T, D), f32),
        jax.ShapeDtypeStruct((T, D), bf16),
        jax.ShapeDtypeStruct((T, 2 * D_FF), bf16),
        jax.ShapeDtypeStruct((T, D_FF), bf16),
        jax.ShapeDtypeStruct((8, 128), f32),
        jax.ShapeDtypeStruct((1, D), f32), jax.ShapeDtypeStruct((1, D), f32),
        jax.ShapeDtypeStruct((1, D), f32), jax.ShapeDtypeStruct((1, D), f32),
        jax.ShapeDtypeStruct((1, D_FF), f32), jax.ShapeDtypeStruct((3, D_FF), f32),
    )
    return pl.pallas_call(
        body, name="ffn_forward_backward", grid=(n_tiles,), out_shape=out_shape,
        in_specs=[rtile(D),
                  pl.BlockSpec((FH, D), lambda i: (jnp.maximum((n_tiles - 1 - i) * hb - 1, 0), 0)),
                  rtile(1), _const_spec((1, D)), _const_spec((1, D)), _const_spec((2 * D_FF, D)),
                  _const_spec((3, D_FF)), _const_spec((1, D_FF)), _const_spec((D_FF, D)),
                  _const_spec((1, D)), _const_spec((1, D)), rtile(D)],
        out_specs=(rtile(D), rtile(D), rtile(2 * D_FF), rtile(D_FF), acc((8, 128)),
                   acc((1, D)), acc((1, D)), acc((1, D)), acc((1, D)), acc((1, D_FF)), acc((3, D_FF))),
        scratch_shapes=[pltpu.VMEM((D_FF // 128, tt + FH, 128), f32), pltpu.VMEM((tt, D_FF), f32),
                        pltpu.VMEM((D_FF // 128, tt + 8, 128), f32)],
        compiler_params=pltpu.CompilerParams(dimension_semantics=("arbitrary",), vmem_limit_bytes=V7X_VMEM_LIMIT),
    )(xhat1, xhat1, rstd1, ln1_g, ln1_b, w_up_t, conv_w, conv_b, w_down, ln2_g, ln2_b, target)


def _mix_backward(dz1, w_out, qkv, g, oret, states, pooled, cat, cos, sin, dmat, qd, kd, cdec, w_pool, pool_scale, w_in_t,
                  exchange, tt=MIX_TILE):
    n_tiles = T // tt
    n_e = len(exchange)

    def body(dz1_ref, wout_ref, qkv_ref, g_ref, oret_ref, states_ref, pooled_ref, cat_ref, cos_ref, sin_ref, dmat_ref,
             qd_ref, kd_ref, wpool_ref, pscale_ref, wint_ref, *rest):
        ein, rest = rest[:n_e], rest[n_e:]
        dproj_ref, gx_ref, dwpool_ref, dpscale_ref, dwout_ref = rest[:5]
        eout, (dstate_s, dout_s, eext_s, tmp_s, dwout_s, *sems) = rest[5:5 + n_e], rest[5 + n_e:]
        i = pl.program_id(0)
        tile_idx = n_tiles - 1 - i

        @pl.when(i == 0)
        def _():
            dstate_s[...] = jnp.zeros_like(dstate_s)
            dwpool_ref[...] = jnp.zeros_like(dwpool_ref)
            dpscale_ref[...] = jnp.zeros_like(dpscale_ref)
            dwout_s[...] = jnp.zeros_like(dwout_s)
            eext_s[pl.ds(tt, HALO), :] = jnp.zeros((HALO, PW), f32)
            _chip_exchange_start(ein, eout, *sems)

        dz1 = dz1_ref[...]
        dz1b = dz1.astype(bf16)
        dcat = _dot(dz1b, wout_ref[...], NT)
        dwout_s[...] += _dot(cat_ref[...], dz1b, TN)

        pos1 = (tile_idx * tt + lax.broadcasted_iota(jnp.int32, (tt, 1), 0) + 1).astype(f32)
        for gi, w in enumerate(WINDOWS):
            sl = slice(gi * DH, (gi + 1) * DH)
            dpo = dcat[:, RW + gi * DH: RW + (gi + 1) * DH]
            pooled_g = pooled_ref[:, sl]
            ylin = _dot(pooled_g, wpool_ref[gi])
            dpscale_ref[:, sl] += jnp.sum(dpo * ylin, axis=0, keepdims=True)
            dpw = (dpo * pscale_ref[:, sl]).astype(bf16)
            dwpool_ref[gi] += _dot(pooled_g, dpw, TN)
            dpooled = _dot(dpw, wpool_ref[gi], NT)
            eext_s[pl.ds(0, tt), sl] = dpooled / jnp.minimum(pos1, float(w))
            stages = int(math.log2(w))
            src = eext_s
            for s in range(stages):
                n = tt + 8 * (stages - 1 - s)
                shift = 2 ** s
                val = src[pl.ds(0, n), sl] + src[pl.ds(shift, n), sl]
                if s == stages - 1:
                    wsum = val
                else:
                    tmp_s[pl.ds(0, n), sl] = val
                    src = tmp_s
            dproj_ref[:, 4 * RW + gi * DH: 4 * RW + (gi + 1) * DH] = (wsum - dpooled).astype(bf16)
        eext_s[pl.ds(tt, HALO), :] = eext_s[pl.ds(0, HALO), :]

        for h in range(HEADS):
            sl = slice(h * DH, (h + 1) * DH)
            dr = dcat[:, sl]
            o = oret_ref[:, sl]
            r = lax.rsqrt(jnp.mean(o * o, axis=-1, keepdims=True) + RMS_EPS)
            rn = o * r
            gg = g_ref[:, sl]
            sg = _sigmoid(gg)
            dproj_ref[:, 3 * RW + h * DH: 3 * RW + (h + 1) * DH] = (dr * rn * (sg * (1.0 + gg * (1.0 - sg)))).astype(bf16)
            drn = dr * (gg * sg)
            dout_s[:, sl] = (r * (drn - rn * jnp.mean(drn * rn, axis=-1, keepdims=True))).astype(bf16)

        for sub in reversed(range(tt // RET_TILE)):
            rows = pl.ds(sub * RET_TILE, RET_TILE)
            cos_t, sin_t = cos_ref[rows, :], sin_ref[rows, :]
            for h in range(HEADS):
                q = qkv_ref[rows, h * DH:(h + 1) * DH]
                k = qkv_ref[rows, RW + h * DH: RW + (h + 1) * DH]
                v = qkv_ref[rows, 2 * RW + h * DH: 2 * RW + (h + 1) * DH]
                do = dout_s[rows, h * DH:(h + 1) * DH]
                stb = states_ref[sub, h]
                dst = dstate_s[h]
                dstb = dst.astype(bf16)
                sb = (_dot(q, k, NT) * dmat_ref[h]).astype(bf16)
                dsb = (_dot(do, v, NT) * dmat_ref[h]).astype(bf16)
                dq = _dot(dsb, k) + _dot(do, stb, NT) * qd_ref[h]
                dk = _dot(dsb, q, TN) + _dot(v, dstb, NT) * kd_ref[h]
                dv = _dot(sb, do, TN) + _dot((k.astype(f32) * kd_ref[h]).astype(bf16), dstb)
                dstate_s[h] = dst * cdec[h] + _dot((q.astype(f32) * qd_ref[h]).astype(bf16), do, TN)
                dproj_ref[rows, h * DH:(h + 1) * DH] = (dq * cos_t - _swap_halves(dq) * sin_t).astype(bf16)
                dproj_ref[rows, RW + h * DH: RW + (h + 1) * DH] = (
                    (dk * cos_t - _swap_halves(dk) * sin_t) * K_SCALE).astype(bf16)
                dproj_ref[rows, 2 * RW + h * DH: 2 * RW + (h + 1) * DH] = dv.astype(bf16)

        gx_ref[...] = ALPHA * dz1 + _dot(dproj_ref[...], wint_ref[...])

        @pl.when(i == n_tiles - 1)
        def _():
            dwout_ref[...] = dwout_s[...].astype(bf16)
            _chip_exchange_finish(ein, eout, *sems)

    rtile = lambda w: pl.BlockSpec((tt, w), lambda i: (n_tiles - 1 - i, 0))
    hbm = pl.BlockSpec(memory_space=pltpu.HBM)
    out_shape = (
        jax.ShapeDtypeStruct((T, IN_W), bf16),
        jax.ShapeDtypeStruct((T, D), f32),
        jax.ShapeDtypeStruct((GROUPS, DH, DH), f32),
        jax.ShapeDtypeStruct((1, PW), f32),
        jax.ShapeDtypeStruct((D, D), bf16),
    ) + tuple(jax.ShapeDtypeStruct(e.shape, e.dtype) for e in exchange)
    return pl.pallas_call(
        body, name="mix_backward", grid=(n_tiles,), out_shape=out_shape,
        in_specs=[rtile(D), _const_spec((D, D)), rtile(3 * RW), rtile(RW), rtile(RW),
                  pl.BlockSpec((tt // RET_TILE, HEADS, DH, DH), lambda i: (n_tiles - 1 - i, 0, 0, 0)),
                  rtile(PW), rtile(D), rtile(DH), rtile(DH),
                  _const_spec((HEADS, RET_TILE, RET_TILE)), _const_spec((HEADS, RET_TILE, DH)),
                  _const_spec((HEADS, RET_TILE, DH)),
                  _const_spec((GROUPS, DH, DH)), _const_spec((1, PW)), _const_spec((IN_W, D))] + [hbm] * n_e,
        out_specs=(rtile(IN_W), rtile(D), pl.BlockSpec((GROUPS, DH, DH), lambda i: (0, 0, 0)),
                   pl.BlockSpec((1, PW), lambda i: (0, 0)),
                   pl.BlockSpec((D, D), lambda i: (0, 0), pipeline_mode=pl.Buffered(1))) + (hbm,) * n_e,
        scratch_shapes=[pltpu.VMEM((HEADS, DH, DH), f32), pltpu.VMEM((tt, RW), bf16),
                        pltpu.VMEM((tt + HALO, PW), f32), pltpu.VMEM((tt + HALO, PW), f32),
                        pltpu.VMEM((D, D), f32)] + _chip_exchange_sems(n_e),
        compiler_params=pltpu.CompilerParams(dimension_semantics=("arbitrary",), vmem_limit_bytes=V7X_VMEM_LIMIT,
                                             collective_id=CHIP_BARRIER),
    )(dz1, w_out, qkv, g, oret, states, pooled, cat, cos, sin, dmat, qd, kd, w_pool, pool_scale, w_in_t, *exchange)


def _weight_grad(a, b, name, tm, exchange=(), tk=2048):
    m = a.shape[1]
    n_m, n_k, n_e = m // tm, T // tk, len(exchange)

    def body(a_ref, b_ref, *rest):
        ein, o_ref, eout, (acc_s, *sems) = rest[:n_e], rest[n_e], rest[n_e + 1:2 * n_e + 1], rest[2 * n_e + 1:]
        i, k = pl.program_id(0), pl.program_id(1)

        if n_e:
            @pl.when((i == 0) & (k == 0))
            def _():
                _chip_exchange_start(ein, eout, *sems)

        @pl.when(k == 0)
        def _():
            acc_s[...] = jnp.zeros_like(acc_s)

        acc_s[...] += _dot(a_ref[...], b_ref[pl.ds(pl.multiple_of(k * tk, tk), tk), :].astype(bf16), TN)

        @pl.when(k == n_k - 1)
        def _():
            o_ref[...] = acc_s[...].astype(bf16)

        if n_e:
            @pl.when((i == n_m - 1) & (k == n_k - 1))
            def _():
                _chip_exchange_finish(ein, eout, *sems)

    hbm = pl.BlockSpec(memory_space=pltpu.HBM)
    return pl.pallas_call(
        body, name=name, grid=(n_m, n_k),
        out_shape=(jax.ShapeDtypeStruct((m, D), bf16),) + tuple(jax.ShapeDtypeStruct(e.shape, e.dtype) for e in exchange),
        in_specs=[pl.BlockSpec((tk, tm), lambda i, k: (k, i)),
                  pl.BlockSpec((T, D), lambda i, k: (0, 0), pipeline_mode=pl.Buffered(1))] + [hbm] * n_e,
        out_specs=(pl.BlockSpec((tm, D), lambda i, k: (i, 0)),) + (hbm,) * n_e,
        scratch_shapes=[pltpu.VMEM((tm, D), f32)] + _chip_exchange_sems(n_e),
        compiler_params=pltpu.CompilerParams(dimension_semantics=("arbitrary", "arbitrary"),
                                             vmem_limit_bytes=V7X_VMEM_LIMIT,
                                             collective_id=CHIP_BARRIER if n_e else None),
    )(a, b, *exchange)


CHIP_FLIPS = ((1, 0), (0, 1), (1, 1))
PAIR_BARRIER, CHIP_BARRIER, GATHER_BARRIER, CHIP_AND_GATHER_BARRIER = 0, 1, 2, 3


def _barrier(peers):
    sem = pltpu.get_barrier_semaphore()
    for peer in peers:
        pl.semaphore_signal(sem, inc=1, device_id=peer, device_id_type=pl.DeviceIdType.MESH)
    pl.semaphore_wait(sem, len(peers))


def _me():
    return lax.axis_index("x"), lax.axis_index("y"), lax.axis_index("c")


def _chip(me, k):
    x, y, _ = me
    if k == 0:
        return x, y
    fx, fy = CHIP_FLIPS[k - 1]
    return (1 - x if fx else x), (1 - y if fy else y)


def _slot(x, y, c):
    return 4 * x + 2 * y + c


def _remote(src, dst, send_sem, recv_sem, to):
    return pltpu.make_async_remote_copy(src_ref=src, dst_ref=dst, send_sem=send_sem, recv_sem=recv_sem,
                                        device_id=to, device_id_type=pl.DeviceIdType.MESH)


def _gather_sems(n):
    return [pltpu.SemaphoreType.DMA((7, n)), pltpu.SemaphoreType.DMA((7, n)), pltpu.SemaphoreType.DMA((n,))] if n else []


def _gather_copy(k, j, gin, gout, send_sems, recv_sems, sending):
    x, y, c = _me()
    sibling, x_chip, y_chip, d_chip = (x, y, 1 - c), (1 - x, y), (x, 1 - y), (1 - x, 1 - y)
    south = c == 0
    passed_on = (jnp.where(south, 1 - x, x), jnp.where(south, y, 1 - y), c)
    src, to = gin[j], sibling
    if sending:
        block = {0: (x, y, c), 1: (x, y, c), 2: (x, y, c), 3: passed_on, 4: (*x_chip, c), 5: (*y_chip, c), 6: (*d_chip, c)}[k]
        to = {1: (*x_chip, c), 2: (*y_chip, c), 3: (jnp.where(south, x, 1 - x), jnp.where(south, 1 - y, y), c)}.get(k, sibling)
        if k >= 3:
            src = gout[j].at[_slot(*block)]
    else:
        block = {0: sibling, 1: (*x_chip, c), 2: (*y_chip, c), 3: (*d_chip, c), 4: (*x_chip, 1 - c), 5: (*y_chip, 1 - c),
                 6: (*d_chip, 1 - c)}[k]
    return _remote(src, gout[j].at[_slot(*block)], send_sems.at[k, j], recv_sems.at[k, j], to)


def _gather_do(ks, action, gin, gout, send_sems, recv_sems):
    for k in ks:
        for j in range(len(gin)):
            cp = _gather_copy(k, j, gin, gout, send_sems, recv_sems, action != "wait_recv")
            getattr(cp, action)()


def _gather_peers():
    x, y, c = _me()
    return [(x, y, 1 - c), (1 - x, y, c), (x, 1 - y, c)]


def _gather_start(gin, gout, send_sems, recv_sems, local_sems, barrier=True):
    if barrier:
        _barrier(_gather_peers())
    for j in range(len(gin)):
        pltpu.make_async_copy(gin[j], gout[j].at[_slot(*_me())], local_sems.at[j]).start()
    _gather_do((0, 1, 2), "start", gin, gout, send_sems, recv_sems)


def _gather_forward(gin, gout, send_sems, recv_sems, local_sems):
    _gather_do((1, 2), "wait_recv", gin, gout, send_sems, recv_sems)
    _gather_do((3, 4, 5), "start", gin, gout, send_sems, recv_sems)


def _gather_finish(gin, gout, send_sems, recv_sems, local_sems):
    _gather_do((3,), "wait_recv", gin, gout, send_sems, recv_sems)
    _gather_do((6,), "start", gin, gout, send_sems, recv_sems)
    _gather_do((0, 4, 5, 6), "wait_recv", gin, gout, send_sems, recv_sems)
    _gather_do(range(7), "wait_send", gin, gout, send_sems, recv_sems)
    for j in range(len(gin)):
        pltpu.make_async_copy(gin[j], gout[j].at[_slot(*_me())], local_sems.at[j]).wait()


def _all_gather(blocks, name):
    n = len(blocks)

    def body(*refs):
        gin, gout, sems = refs[:n], refs[n:2 * n], refs[2 * n:]
        _gather_start(gin, gout, *sems)
        _gather_forward(gin, gout, *sems)
        _gather_finish(gin, gout, *sems)

    hbm = pl.BlockSpec(memory_space=pltpu.HBM)
    return pl.pallas_call(
        body, name=name,
        out_shape=tuple(jax.ShapeDtypeStruct((N_DEV,) + b.shape, b.dtype) for b in blocks),
        in_specs=[hbm] * n, out_specs=(hbm,) * n, scratch_shapes=_gather_sems(n),
        compiler_params=pltpu.CompilerParams(collective_id=GATHER_BARRIER),
    )(*blocks)


def _pair_reduce(parts, name):
    n = len(parts)

    def body(*refs):
        ins, own, others, landing, mine = (refs[k * n:(k + 1) * n] for k in range(5))
        send_sems, recv_sems, local_sems = refs[5 * n:]
        me = _me()
        x, y, c = me
        sibling = (x, y, 1 - c)
        _barrier([sibling])
        sends, loads = [], []
        for k in range(4):
            for j in range(n):
                cp = _remote(ins[j].at[_slot(*_chip(me, k), 1 - c)], landing[j].at[k], send_sems.at[k, j],
                             recv_sems.at[k, j], sibling)
                cp.start()
                sends.append(cp)
                ld = pltpu.make_async_copy(ins[j].at[_slot(*_chip(me, k), c)], mine[j].at[k], local_sems.at[k, j])
                ld.start()
                loads.append(ld)
        for k in range(4):
            for j in range(n):
                loads[k * n + j].wait()
                _remote(ins[j].at[0], landing[j].at[k], send_sems.at[k, j], recv_sems.at[k, j], sibling).wait_recv()
                total = mine[j][k].astype(f32) + landing[j][k].astype(f32)
                if k == 0:
                    own[j][...] = total.astype(own[j].dtype)
                else:
                    others[j][k - 1] = total.astype(others[j].dtype)
        for cp in sends:
            cp.wait_send()

    vm = pl.BlockSpec(memory_space=pltpu.VMEM)
    return pl.pallas_call(
        body, name=name,
        out_shape=tuple(jax.ShapeDtypeStruct(p.shape[1:], p.dtype) for p in parts)
        + tuple(jax.ShapeDtypeStruct((3,) + p.shape[1:], p.dtype) for p in parts),
        in_specs=[pl.BlockSpec(memory_space=pltpu.HBM)] * n, out_specs=(vm,) * (2 * n),
        scratch_shapes=[pltpu.VMEM((4,) + p.shape[1:], p.dtype) for p in parts] * 2
        + [pltpu.SemaphoreType.DMA((4, n)), pltpu.SemaphoreType.DMA((4, n)), pltpu.SemaphoreType.DMA((4, n))],
        compiler_params=pltpu.CompilerParams(vmem_limit_bytes=V7X_VMEM_LIMIT, collective_id=PAIR_BARRIER),
    )(*parts)


def _chip_exchange_sems(n):
    return [pltpu.SemaphoreType.DMA((3, n)), pltpu.SemaphoreType.DMA((3, n))] if n else []


def _chip_exchange_copy(k, j, ein, eout, send_sems, recv_sems):
    me = _me()
    return _remote(ein[j].at[k - 1], eout[j].at[k - 1], send_sems.at[k - 1, j], recv_sems.at[k - 1, j],
                   (*_chip(me, k), me[2]))


def _chip_peers():
    me = _me()
    return [(*_chip(me, k), me[2]) for k in range(1, 4)]


def _chip_exchange_start(ein, eout, send_sems, recv_sems, barrier=True):
    if barrier:
        _barrier(_chip_peers())
    for k in range(1, 4):
        for j in range(len(ein)):
            _chip_exchange_copy(k, j, ein, eout, send_sems, recv_sems).start()


def _chip_exchange_finish(ein, eout, send_sems, recv_sems):
    for k in range(1, 4):
        for j in range(len(ein)):
            _chip_exchange_copy(k, j, ein, eout, send_sems, recv_sems).wait_recv()
    for k in range(1, 4):
        for j in range(len(ein)):
            _chip_exchange_copy(k, j, ein, eout, send_sems, recv_sems).wait_send()


def _chip_exchange_and_gather(others, blocks, name):
    n_e, n_g = len(others), len(blocks)

    def body(*refs):
        ein, gin, eout, gout = refs[:n_e], refs[n_e:n_e + n_g], refs[n_e + n_g:2 * n_e + n_g], refs[2 * n_e + n_g:2 * (n_e + n_g)]
        esems, gsems = refs[2 * (n_e + n_g):2 * (n_e + n_g) + 2], refs[2 * (n_e + n_g) + 2:]
        _barrier(_chip_peers() + _gather_peers()[:1])
        _chip_exchange_start(ein, eout, *esems, barrier=False)
        _gather_start(gin, gout, *gsems, barrier=False)
        _gather_forward(gin, gout, *gsems)
        _gather_finish(gin, gout, *gsems)
        _chip_exchange_finish(ein, eout, *esems)

    hbm = pl.BlockSpec(memory_space=pltpu.HBM)
    return pl.pallas_call(
        body, name=name,
        out_shape=tuple(jax.ShapeDtypeStruct(e.shape, e.dtype) for e in others)
        + tuple(jax.ShapeDtypeStruct((N_DEV,) + b.shape, b.dtype) for b in blocks),
        in_specs=[hbm] * (n_e + n_g), out_specs=(hbm,) * (n_e + n_g),
        scratch_shapes=_chip_exchange_sems(n_e) + _gather_sems(n_g),
        compiler_params=pltpu.CompilerParams(collective_id=CHIP_AND_GATHER_BARRIER),
    )(*others, *blocks)


def _sum_parts(owns, arrived, name):
    n = len(owns)

    def body(*refs):
        for own, arr, out in zip(refs[:n], refs[n:2 * n], refs[2 * n:]):
            acc = own[...].astype(f32)
            for k in range(3):
                acc = acc + arr[k].astype(f32)
            out[...] = acc

    vm = pl.BlockSpec(memory_space=pltpu.VMEM)
    return pl.pallas_call(
        body, name=name, out_shape=tuple(jax.ShapeDtypeStruct(o.shape, f32) for o in owns),
        in_specs=[vm] * (2 * n), out_specs=(vm,) * n,
        compiler_params=pltpu.CompilerParams(vmem_limit_bytes=V7X_VMEM_LIMIT),
    )(*owns, *arrived)


ADAM_C1 = 1.0 / (1.0 - ADAM_B1 ** ADAM_STEP)
ADAM_C2 = 1.0 / (1.0 - ADAM_B2 ** ADAM_STEP)


def _adam_update(w, g, m, v):
    m = ADAM_B1 * m + (1.0 - ADAM_B1) * g
    v = ADAM_B2 * v + (1.0 - ADAM_B2) * (g * g)
    return -ADAM_LR * ((m * ADAM_C1) / (jnp.sqrt(v * ADAM_C2) + ADAM_EPS) + ADAM_WD * w), m, v


def _sum_adamw(own, arrived, w, m, v, name, steps):
    rows = own.shape[0]
    br = rows // steps

    def body(own_ref, arr_ref, w_ref, m_ref, v_ref, g_out, d_out, m_out, v_out):
        g = own_ref[...].astype(f32)
        for k in range(3):
            g = g + arr_ref[k].astype(f32)
        g_out[...] = g
        d_out[...], m_out[...], v_out[...] = _adam_update(w_ref[...], g, m_ref[...], v_ref[...])

    blk = pl.BlockSpec((br, D), lambda i: (i, 0))
    return pl.pallas_call(
        body, name=name, grid=(steps,), out_shape=(jax.ShapeDtypeStruct((rows, D), f32),) * 4,
        in_specs=[blk, pl.BlockSpec((3, br, D), lambda i: (0, i, 0)), blk, blk, blk], out_specs=(blk,) * 4,
        compiler_params=pltpu.CompilerParams(dimension_semantics=("parallel",), vmem_limit_bytes=V7X_VMEM_LIMIT),
    )(own, arrived, w, m, v)


def _adamw(ws, gs, ms, vs, name):
    n = len(ws)

    def body(*refs):
        w_r, g_r, m_r, v_r = (refs[k * n:(k + 1) * n] for k in range(4))
        d_o, m_o, v_o = (refs[(4 + k) * n:(5 + k) * n] for k in range(3))
        for j in range(n):
            d_o[j][...], m_o[j][...], v_o[j][...] = _adam_update(w_r[j][...], g_r[j][...], m_r[j][...], v_r[j][...])

    vm = pl.BlockSpec(memory_space=pltpu.VMEM)
    shapes = tuple(jax.ShapeDtypeStruct(w.shape, f32) for w in ws)
    return pl.pallas_call(
        body, name=name, out_shape=shapes * 3, in_specs=[vm] * (4 * n), out_specs=tuple([vm] * (3 * n)),
        compiler_params=pltpu.CompilerParams(vmem_limit_bytes=V7X_VMEM_LIMIT),
    )(*ws, *gs, *ms, *vs)


SMALL = (("w_pool", GROUPS * DH * DH), ("pool_scale", PW), ("ln1_g", D), ("ln1_b", D), ("conv_b", D_FF),
         ("ln2_g", D), ("ln2_b", D), ("conv_w", 3 * D_FF), ("loss", 1))
SMALL_ROWS = 640


def _pack(named):
    flat = jnp.concatenate([named[k].reshape(-1) for k, _ in SMALL])
    return jnp.pad(flat, (0, SMALL_ROWS * 128 - flat.shape[0])).reshape(SMALL_ROWS, 128)


def _unpack(packed):
    flat, out, at = packed.reshape(-1), {}, 0
    for k, size in SMALL:
        out[k] = flat[at:at + size]
        at += size
    return out


def kernel(x, w_in, w_pool, pool_scale, w_out, ln1_g, ln1_b, w_up, conv_w, conv_b, w_down, ln2_g, ln2_b, loss_target, m_w_in, m_w_pool, m_pool_scale, m_w_out, m_ln1_g, m_ln1_b, m_w_up, m_conv_w, m_conv_b, m_w_down, m_ln2_g, m_ln2_b, v_w_in, v_w_pool, v_pool_scale, v_w_out, v_ln1_g, v_ln1_b, v_w_up, v_conv_w, v_conv_b, v_w_down, v_ln2_g, v_ln2_b):
    me = 4 * lax.axis_index("x") + 2 * lax.axis_index("y") + lax.axis_index("c")
    x2, tgt = x[0], loss_target[0]

    g_in, g_out, g_cw = _all_gather([w_in[0].T.astype(bf16), w_out[0].astype(bf16), jnp.transpose(conv_w, (1, 0, 2))],
                                    "gather_weights")
    w_in_t = g_in.reshape(IN_W, D)
    w_out_f = g_out.reshape(D, D)
    conv_w_f = jnp.transpose(g_cw[:, :, 0, :], (1, 0, 2)).reshape(3, D_FF)
    w_pool_b = w_pool[0].astype(bf16)

    cos, sin = _rope_tables()
    dmat, qd, kd, cdec = _decay_tables(RET_TILE)

    qkv, g, oret, states, cat, pooled, xhat1, rstd1, x1b, g_up, g_down = _mix_forward(
        x2, w_in_t, cos, sin, dmat, qd, kd, cdec, w_pool_b, pool_scale, w_out_f, ln1_g, ln1_b,
        gather=[w_up[0].T.astype(bf16), w_down[0].astype(bf16)])
    w_up_t = g_up.reshape(2 * D_FF, D)
    w_down_f = g_down.reshape(D_FF, D)
    dz1, dz2b, du, f, loss8, d_ln2_g, d_ln2_b, d_ln1_g, d_ln1_b, d_conv_b, d_conv_w = _ffn_forward_backward(
        xhat1, rstd1, ln1_g, ln1_b, w_up_t, conv_w_f, conv_b, w_down_f, ln2_g, ln2_b, tgt)

    (dw_down,) = _weight_grad(f, dz2b, "grad_w_down", tm=D_FF // 2)
    own_down, oth_down = _pair_reduce([dw_down.reshape(N_DEV, ROWS_DOWN, D)], "pair_reduce_down")
    dw_up_t, arr_down = _weight_grad(du, x1b, "grad_w_up", tm=D_FF // 2, exchange=[oth_down])
    own_up, oth_up = _pair_reduce([dw_up_t.reshape(N_DEV, ROWS_UP, D)], "pair_reduce_up")
    dproj, grad_x, d_w_pool, d_pool_scale, dw_out, arr_up = _mix_backward(
        dz1, w_out_f, qkv, g, oret, states, pooled, cat, cos, sin, dmat, qd, kd, cdec, w_pool_b, pool_scale, w_in_t,
        exchange=[oth_up])
    small = _pack({"w_pool": d_w_pool, "pool_scale": d_pool_scale, "ln1_g": d_ln1_g, "ln1_b": d_ln1_b,
                   "conv_b": d_conv_b, "ln2_g": d_ln2_g, "ln2_b": d_ln2_b, "conv_w": d_conv_w, "loss": loss8[0, :1]})
    own_out, own_small, oth_out, oth_small = _pair_reduce(
        [dw_out.reshape(N_DEV, ROWS_OUT, D), small.reshape(N_DEV, SMALL_ROWS // N_DEV, 128)], "pair_reduce_out")
    dw_in_t, arr_out, arr_small = _weight_grad(dproj, x2, "grad_w_in", tm=IN_W // 2, exchange=[oth_out, oth_small])
    own_in, oth_in = _pair_reduce([dw_in_t.reshape(N_DEV, ROWS_IN, D)], "pair_reduce_in")
    (small_piece,) = _sum_parts([own_small], [arr_small], "sum_small_grads")
    arr_in, gs_small = _chip_exchange_and_gather([oth_in], [small_piece], "exchange_in_gather_small")

    names = ["w_in", "w_pool", "pool_scale", "w_out", "ln1_g", "ln1_b", "w_up", "conv_w", "conv_b", "w_down",
             "ln2_g", "ln2_b"]
    w_d = dict(w_in=w_in, w_pool=w_pool, pool_scale=pool_scale, w_out=w_out, ln1_g=ln1_g, ln1_b=ln1_b, w_up=w_up,
               conv_w=conv_w, conv_b=conv_b, w_down=w_down, ln2_g=ln2_g, ln2_b=ln2_b)
    m_d = dict(w_in=m_w_in, w_pool=m_w_pool, pool_scale=m_pool_scale, w_out=m_w_out, ln1_g=m_ln1_g, ln1_b=m_ln1_b,
               w_up=m_w_up, conv_w=m_conv_w, conv_b=m_conv_b, w_down=m_w_down, ln2_g=m_ln2_g, ln2_b=m_ln2_b)
    v_d = dict(w_in=v_w_in, w_pool=v_w_pool, pool_scale=v_pool_scale, w_out=v_w_out, ln1_g=v_ln1_g, ln1_b=v_ln1_b,
               w_up=v_w_up, conv_w=v_conv_w, conv_b=v_conv_b, w_down=v_w_down, ln2_g=v_ln2_g, ln2_b=v_ln2_b)
    g_d, delta, new_m, new_v = {}, {}, {}, {}

    big = (("w_in", own_in, arr_in, True, 4), ("w_out", own_out, arr_out, False, 2),
           ("w_up", own_up, arr_up, True, 4), ("w_down", own_down, arr_down, False, 2))
    for k, own, arr, transposed, steps in big:
        lay = (lambda a: a[0].T) if transposed else (lambda a: a[0])
        back = (lambda a: a.T[None]) if transposed else (lambda a: a[None])
        res = _sum_adamw(own, arr, lay(w_d[k]), lay(m_d[k]), lay(v_d[k]), "adamw_" + k, steps)
        g_d[k], delta[k], new_m[k], new_v[k] = (back(r) for r in res)

    gsm = _unpack(gs_small)
    gsm["conv_w"] = lax.dynamic_slice(gsm["conv_w"].reshape(3, D_FF), (0, me * (D_FF // N_DEV)), (3, D_FF // N_DEV))
    lay = lambda k, a: jnp.transpose(a, (1, 0, 2)) if k == "conv_w" else a.reshape(-1, a.shape[-1])
    back = lambda k, a: jnp.transpose(a, (1, 0, 2)) if k == "conv_w" else a.reshape(w_d[k].shape)
    group = [k for k in names if k not in g_d]
    for k in group:
        g_d[k] = gsm[k].reshape(w_d[k].shape)
    res = _adamw([lay(k, w_d[k]) for k in group], [lay(k, g_d[k]) for k in group], [lay(k, m_d[k]) for k in group],
                 [lay(k, v_d[k]) for k in group], "adamw_small")
    for j, k in enumerate(group):
        delta[k], new_m[k], new_v[k] = (back(k, res[part * len(group) + j]) for part in range(3))

    loss = gsm["loss"].reshape(())
    return (loss, grad_x[None], *[g_d[k] for k in names], *[delta[k] for k in names], *[new_m[k] for k in names],
            *[new_v[k] for k in names])
```

```python
import functools
import math

import numpy as np
import jax
import jax.numpy as jnp
from jax import lax
from jax.experimental import pallas as pl
from jax.experimental.pallas import tpu as pltpu

f32 = jnp.float32
bf16 = jnp.bfloat16

N_DEV = 8
T = 4096
D = 1024
CHUNK = 64
MIX_TILE = 512
RET_TILE = 256
HEADS = 4
DH = 128
RW = HEADS * DH
PW = 512
GROUPS = 4
WINDOWS = (2, 4, 8, 16)
IN_W = 4 * RW + PW
D_FF = 2816
LN_EPS = 1e-5
RMS_EPS = 1e-6
ALPHA = 2.0 ** 0.25
K_SCALE = DH ** -0.5

ADAM_LR = 0.001
ADAM_B1 = 0.9
ADAM_B2 = 0.999
ADAM_EPS = 1e-08
ADAM_WD = 0.01
ADAM_STEP = 10

ROWS_IN, ROWS_OUT, ROWS_UP, ROWS_DOWN = IN_W // N_DEV, D // N_DEV, 2 * D_FF // N_DEV, D_FF // N_DEV

V7X_VMEM_LIMIT = 56 * 2 ** 20
HALO = 32

NT = (((1,), (1,)), ((), ()))
TN = (((0,), (0,)), ((), ()))
NN = (((1,), (0,)), ((), ()))


def _dot(a, b, dims=NN):
    return lax.dot_general(a, b, dims, preferred_element_type=f32)


def _const_spec(shape):
    zeros = (0,) * len(shape)
    return pl.BlockSpec(shape, lambda i: zeros, pipeline_mode=pl.Buffered(1))


def _sigmoid(x):
    return 0.5 * jnp.tanh(0.5 * x) + 0.5


def _decay_tables(tt):
    h = np.arange(HEADS, dtype=np.float64)
    log_gamma = np.log(1.0 - 2.0 ** (-5.0 - h)).astype(np.float32).astype(np.float64)[:, None, None]
    idx = np.arange(tt, dtype=np.float64)
    visible = (idx[None, :] // CHUNK) <= (idx[:, None] // CHUNK)
    mask = np.where(visible[None], np.exp(log_gamma * np.abs(idx[:, None] - idx[None, :])[None]), 0.0)
    qd = np.broadcast_to(np.exp(log_gamma * (idx[None, :, None] + 1.0)), (HEADS, tt, DH))
    kd = np.broadcast_to(np.exp(log_gamma * (tt - 1.0 - idx[None, :, None])), (HEADS, tt, DH))
    cd = np.exp(log_gamma[:, 0, 0] * tt)
    return (jnp.asarray(mask, f32), jnp.asarray(qd, f32), jnp.asarray(kd, f32), [float(c) for c in cd])


def _rope_tables():
    inv_freq = (10000.0 ** (-np.arange(0, DH, 2, dtype=np.float64) / DH)).astype(np.float32)
    ang = (np.arange(T, dtype=np.float32)[:, None] * inv_freq[None, :]).astype(np.float64)
    cos, sin = np.cos(ang), np.sin(ang)
    return (jnp.asarray(np.concatenate([cos, cos], axis=1), f32), jnp.asarray(np.concatenate([-sin, sin], axis=1), f32))


def _swap_halves(t):
    return pltpu.roll(t, DH // 2, axis=1)


def _mix_forward(x, w_in_t, cos, sin, dmat, qd, kd, cdec, w_pool, pool_scale, w_out, ln1_g, ln1_b, gather,
                 tt=MIX_TILE):
    n_tiles = T // tt
    n_g = len(gather)

    def body(x_ref, wint_ref, cos_ref, sin_ref, dmat_ref, qd_ref, kd_ref, wpool_ref, pscale_ref, wout_ref,
             g1_ref, b1_ref, *rest):
        gin, rest = rest[:n_g], rest[n_g:]
        qkv_ref, g_ref, oret_ref, states_ref, cat_ref, pooled_ref, xhat_ref, rstd_ref, x1b_ref = rest[:9]
        gout, (state_s, pext_s, tmp_s, *sems) = rest[9:9 + n_g], rest[9 + n_g:]
        i = pl.program_id(0)

        @pl.when(i == 0)
        def _():
            state_s[...] = jnp.zeros_like(state_s)
            pext_s[pl.ds(0, HALO), :] = jnp.zeros((HALO, PW), f32)
            _gather_start(gin, gout, *sems)

        @pl.when(i == n_tiles - 2)
        def _():
            _gather_forward(gin, gout, *sems)

        xb = x_ref[...].astype(bf16)
        cos_t, sin_t = cos_ref[...], sin_ref[...]
        for part in range(2):
            pr = _dot(xb, wint_ref[pl.ds(part * RW, RW), :], NT)
            for h in range(HEADS):
                t = pr[:, h * DH:(h + 1) * DH]
                r = t * cos_t + _swap_halves(t) * sin_t
                if part == 1:
                    r = r * K_SCALE
                qkv_ref[:, part * RW + h * DH: part * RW + (h + 1) * DH] = r.astype(bf16)
        qkv_ref[:, 2 * RW:3 * RW] = _dot(xb, wint_ref[pl.ds(2 * RW, RW), :], NT).astype(bf16)
        g_ref[...] = _dot(xb, wint_ref[pl.ds(3 * RW, RW), :], NT)
        pext_s[pl.ds(HALO, tt), :] = _dot(xb, wint_ref[pl.ds(4 * RW, PW), :], NT)

        for sub in range(tt // RET_TILE):
            rows = pl.ds(sub * RET_TILE, RET_TILE)
            for h in range(HEADS):
                q = qkv_ref[rows, h * DH:(h + 1) * DH]
                k = qkv_ref[rows, RW + h * DH: RW + (h + 1) * DH]
                v = qkv_ref[rows, 2 * RW + h * DH: 2 * RW + (h + 1) * DH]
                s = _dot(q, k, NT) * dmat_ref[h]
                st = state_s[h]
                stb = st.astype(bf16)
                states_ref[sub, h] = stb
                oret_ref[rows, h * DH:(h + 1) * DH] = (_dot(s.astype(bf16), v)
                                                      + _dot((q.astype(f32) * qd_ref[h]).astype(bf16), stb))
                state_s[h] = st * cdec[h] + _dot((k.astype(f32) * kd_ref[h]).astype(bf16), v, TN)

        for h in range(HEADS):
            sl = slice(h * DH, (h + 1) * DH)
            o = oret_ref[:, sl]
            r = lax.rsqrt(jnp.mean(o * o, axis=-1, keepdims=True) + RMS_EPS)
            gg = g_ref[:, sl]
            cat_ref[:, sl] = (o * r * (gg * _sigmoid(gg))).astype(bf16)

        pos1 = (i * tt + lax.broadcasted_iota(jnp.int32, (tt, 1), 0) + 1).astype(f32)
        for gi, w in enumerate(WINDOWS):
            sl = slice(gi * DH, (gi + 1) * DH)
            stages = int(math.log2(w))
            src = pext_s
            for s in range(stages):
                lo = HALO - 8 * (stages - 1 - s)
                n = tt + HALO - lo
                shift = 2 ** s
                val = src[pl.ds(lo, n), sl] + src[pl.ds(lo - shift, n), sl]
                if s == stages - 1:
                    wsum = val
                else:
                    tmp_s[pl.ds(lo, n), sl] = val
                    src = tmp_s
            p_g = pext_s[pl.ds(HALO, tt), sl]
            pooled = (wsum / jnp.minimum(pos1, float(w)) - p_g).astype(bf16)
            pooled_ref[:, sl] = pooled
            y = _dot(pooled, wpool_ref[gi]) * pscale_ref[:, sl]
            cat_ref[:, RW + gi * DH: RW + (gi + 1) * DH] = y.astype(bf16)
        pext_s[pl.ds(0, HALO), :] = pext_s[pl.ds(tt, HALO), :]

        z = ALPHA * x_ref[...] + _dot(cat_ref[...], wout_ref[...])
        mu = jnp.mean(z, axis=-1, keepdims=True)
        zc = z - mu
        rstd = lax.rsqrt(jnp.mean(zc * zc, axis=-1, keepdims=True) + LN_EPS)
        xhat = zc * rstd
        xhat_ref[...] = xhat
        rstd_ref[...] = rstd
        x1b_ref[...] = (xhat * g1_ref[...] + b1_ref[...]).astype(bf16)

        @pl.when(i == n_tiles - 1)
        def _():
            _gather_finish(gin, gout, *sems)

    tile = lambda w: pl.BlockSpec((tt, w), lambda i: (i, 0))
    hbm = pl.BlockSpec(memory_space=pltpu.HBM)
    out_shape = (
        jax.ShapeDtypeStruct((T, 3 * RW), bf16),
        jax.ShapeDtypeStruct((T, RW), f32),
        jax.ShapeDtypeStruct((T, RW), f32),
        jax.ShapeDtypeStruct((T // RET_TILE, HEADS, DH, DH), bf16),
        jax.ShapeDtypeStruct((T, D), bf16),
        jax.ShapeDtypeStruct((T, PW), bf16),
        jax.ShapeDtypeStruct((T, D), f32),
        jax.ShapeDtypeStruct((T, 1), f32),
        jax.ShapeDtypeStruct((T, D), bf16),
    ) + tuple(jax.ShapeDtypeStruct((N_DEV,) + b.shape, b.dtype) for b in gather)
    return pl.pallas_call(
        body, name="mix_forward", grid=(n_tiles,), out_shape=out_shape,
        in_specs=[tile(D), _const_spec((IN_W, D)), tile(DH), tile(DH),
                  _const_spec((HEADS, RET_TILE, RET_TILE)), _const_spec((HEADS, RET_TILE, DH)),
                  _const_spec((HEADS, RET_TILE, DH)),
                  _const_spec((GROUPS, DH, DH)), _const_spec((1, PW)), _const_spec((D, D)),
                  _const_spec((1, D)), _const_spec((1, D))] + [hbm] * n_g,
        out_specs=(tile(3 * RW), tile(RW), tile(RW),
                   pl.BlockSpec((tt // RET_TILE, HEADS, DH, DH), lambda i: (i, 0, 0, 0)),
                   tile(D), tile(PW), tile(D), tile(1), tile(D)) + (hbm,) * n_g,
        scratch_shapes=[pltpu.VMEM((HEADS, DH, DH), f32), pltpu.VMEM((tt + HALO, PW), f32),
                        pltpu.VMEM((tt + HALO, PW), f32)] + _gather_sems(n_g),
        compiler_params=pltpu.CompilerParams(dimension_semantics=("arbitrary",), vmem_limit_bytes=V7X_VMEM_LIMIT,
                                             collective_id=GATHER_BARRIER),
    )(x, w_in_t, cos, sin, dmat, qd, kd, w_pool, pool_scale, w_out, ln1_g, ln1_b, *gather)


def _ffn_forward_backward(xhat1, rstd1, ln1_g, ln1_b, w_up_t, conv_w, conv_b, w_down, ln2_g, ln2_b, target,
                          tt=256, widths=(512, 512, 512, 512, 512, 256)):
    n_tiles = T // tt
    assert sum(widths) == D_FF and all(w % 128 == 0 for w in widths)
    chunks = [(sum(widths[:c]), w) for c, w in enumerate(widths)]
    FH = 16
    hb = tt // FH

    def body(xhat_ref, halo_ref, rstd_ref, g1_ref, b1_ref, wupt_ref, cw_ref, cb_ref, wdown_ref, g2_ref, b2_ref, tgt_ref,
             dz1_ref, dz2b_ref, du_ref, f_ref, loss_ref, dg2_ref, db2_ref, dg1_ref, db1_ref, dcb_ref, dcw_ref,
             gext_s, val_s, dhext_s):
        i = pl.program_id(0)
        tile_idx = n_tiles - 1 - i

        def rd(ref, off, lo, w):
            return jnp.concatenate([ref[lo // 128 + k, pl.ds(off, tt), :] for k in range(w // 128)], axis=1)

        def wr(ref, lo, val):
            for k in range(val.shape[1] // 128):
                ref[lo // 128 + k, pl.ds(0, val.shape[0]), :] = val[:, k * 128:(k + 1) * 128]

        @pl.when(i == 0)
        def _():
            for r in (loss_ref, dg2_ref, db2_ref, dg1_ref, db1_ref, dcb_ref, dcw_ref):
                r[...] = jnp.zeros_like(r)
            dhext_s[:, pl.ds(tt, 8), :] = jnp.zeros((D_FF // 128, 8, 128), f32)

        g1, b1 = g1_ref[...], b1_ref[...]
        xhat = xhat_ref[...]
        x1 = xhat * g1 + b1
        x1b = x1.astype(bf16)
        x1h = ((halo_ref[...] * g1 + b1) * jnp.where(tile_idx == 0, 0.0, 1.0)).astype(bf16)
        x1ext = jnp.concatenate([x1h, x1b], axis=0)

        for lo, w in chunks:
            cs = slice(lo, lo + w)
            val = _dot(x1b, wupt_ref[pl.ds(lo, w), :], NT)
            gate_ext = _dot(x1ext, wupt_ref[pl.ds(D_FF + lo, w), :], NT)
            wr(gext_s, lo, gate_ext)
            hh = (cb_ref[:, cs] + cw_ref[0:1, cs] * rd(gext_s, FH - 2, lo, w) + cw_ref[1:2, cs] * rd(gext_s, FH - 1, lo, w)
                  + cw_ref[2:3, cs] * gate_ext[FH:])
            sg = _sigmoid(hh)
            act = hh * sg
            wr(dhext_s, lo, act)
            val_s[:, cs] = val * (sg + act * (1.0 - sg))
            f_ref[:, cs] = (act * val).astype(bf16)

        z = ALPHA * x1 + _dot(f_ref[...], wdown_ref[...])
        mu = jnp.mean(z, axis=-1, keepdims=True)
        zc = z - mu
        rstd2 = lax.rsqrt(jnp.mean(zc * zc, axis=-1, keepdims=True) + LN_EPS)
        xh2 = zc * rstd2
        diff = xh2 * g2_ref[...] + b2_ref[...] - tgt_ref[...]
        loss_ref[...] += 0.5 * jnp.sum(diff * diff) / D
        dy = diff * (1.0 / D)
        dg2_ref[...] += jnp.sum(dy * xh2, axis=0, keepdims=True)
        db2_ref[...] += jnp.sum(dy, axis=0, keepdims=True)
        dyg = dy * g2_ref[...]
        dz2 = rstd2 * (dyg - jnp.mean(dyg, axis=-1, keepdims=True) - xh2 * jnp.mean(dyg * xh2, axis=-1, keepdims=True))
        dz2b = dz2.astype(bf16)
        dz2b_ref[...] = dz2b

        for lo, w in chunks:
            cs = slice(lo, lo + w)
            df = _dot(dz2b, wdown_ref[pl.ds(lo, w), :], NT)
            dval = df * rd(dhext_s, 0, lo, w)
            dh = df * val_s[:, cs]
            wr(dhext_s, lo, dh)
            dh1, dh2, g0 = rd(dhext_s, 1, lo, w), rd(dhext_s, 2, lo, w), rd(gext_s, FH, lo, w)
            dcb_ref[:, cs] += jnp.sum(dh, axis=0, keepdims=True)
            dcw_ref[0:1, cs] += jnp.sum(dh2 * g0, axis=0, keepdims=True)
            dcw_ref[1:2, cs] += jnp.sum(dh1 * g0, axis=0, keepdims=True)
            dcw_ref[2:3, cs] += jnp.sum(dh * g0, axis=0, keepdims=True)
            dgate = cw_ref[2:3, cs] * dh + cw_ref[1:2, cs] * dh1 + cw_ref[0:1, cs] * dh2
            du_ref[:, cs] = dval.astype(bf16)
            du_ref[:, D_FF + lo: D_FF + lo + w] = dgate.astype(bf16)
        dhext_s[:, pl.ds(tt, 8), :] = dhext_s[:, pl.ds(0, 8), :]
        dx1 = ALPHA * dz2 + _dot(du_ref[...], wupt_ref[...])

        dg1_ref[...] += jnp.sum(dx1 * xhat, axis=0, keepdims=True)
        db1_ref[...] += jnp.sum(dx1, axis=0, keepdims=True)
        dxg = dx1 * g1
        dz1_ref[...] = rstd_ref[...] * (dxg - jnp.mean(dxg, axis=-1, keepdims=True)
                                        - xhat * jnp.mean(dxg * xhat, axis=-1, keepdims=True))

    rtile = lambda w: pl.BlockSpec((tt, w), lambda i: (n_tiles - 1 - i, 0))
    acc = lambda shape: pl.BlockSpec(shape, lambda i: (0, 0))
    out_shape = (
        jax.ShapeDtypeStruct((T, D), f32),
        jax.ShapeDtypeStruct((T, D), bf16),
        jax.ShapeDtypeStruct((T, 2 * D_FF), bf16),
        jax.ShapeDtypeStruct((T, D_FF), bf16),
        jax.ShapeDtypeStruct((8, 128), f32),
        jax.ShapeDtypeStruct((1, D), f32), jax.ShapeDtypeStruct((1, D), f32),
        jax.ShapeDtypeStruct((1, D), f32), jax.ShapeDtypeStruct((1, D), f32),
        jax.ShapeDtypeStruct((1, D_FF), f32), jax.ShapeDtypeStruct((3, D_FF), f32),
    )
    return pl.pallas_call(
        body, name="ffn_forward_backward", grid=(n_tiles,), out_shape=out_shape,
        in_specs=[rtile(D),
                  pl.BlockSpec((FH, D), lambda i: (jnp.maximum((n_tiles - 1 - i) * hb - 1, 0), 0)),
                  rtile(1), _const_spec((1, D)), _const_spec((1, D)), _const_spec((2 * D_FF, D)),
                  _const_spec((3, D_FF)), _const_spec((1, D_FF)), _const_spec((D_FF, D)),
                  _const_spec((1, D)), _const_spec((1, D)), rtile(D)],
        out_specs=(rtile(D), rtile(D), rtile(2 * D_FF), rtile(D_FF), acc((8, 128)),
                   acc((1, D)), acc((1, D)), acc((1, D)), acc((1, D)), acc((1, D_FF)), acc((3, D_FF))),
        scratch_shapes=[pltpu.VMEM((D_FF // 128, tt + FH, 128), f32), pltpu.VMEM((tt, D_FF), f32),
                        pltpu.VMEM((D_FF // 128, tt + 8, 128), f32)],
        compiler_params=pltpu.CompilerParams(dimension_semantics=("arbitrary",), vmem_limit_bytes=V7X_VMEM_LIMIT),
    )(xhat1, xhat1, rstd1, ln1_g, ln1_b, w_up_t, conv_w, conv_b, w_down, ln2_g, ln2_b, target)


def _mix_backward(dz1, w_out, qkv, g, oret, states, pooled, cat, cos, sin, dmat, qd, kd, cdec, w_pool, pool_scale, w_in_t,
                  exchange, tt=MIX_TILE):
    n_tiles = T // tt
    n_e = len(exchange)

    def body(dz1_ref, wout_ref, qkv_ref, g_ref, oret_ref, states_ref, pooled_ref, cat_ref, cos_ref, sin_ref, dmat_ref,
             qd_ref, kd_ref, wpool_ref, pscale_ref, wint_ref, *rest):
        ein, rest = rest[:n_e], rest[n_e:]
        dproj_ref, gx_ref, dwpool_ref, dpscale_ref, dwout_ref = rest[:5]
        eout, (dstate_s, dout_s, eext_s, tmp_s, dwout_s, *sems) = rest[5:5 + n_e], rest[5 + n_e:]
        i = pl.program_id(0)
        tile_idx = n_tiles - 1 - i

        @pl.when(i == 0)
        def _():
            dstate_s[...] = jnp.zeros_like(dstate_s)
            dwpool_ref[...] = jnp.zeros_like(dwpool_ref)
            dpscale_ref[...] = jnp.zeros_like(dpscale_ref)
            dwout_s[...] = jnp.zeros_like(dwout_s)
            eext_s[pl.ds(tt, HALO), :] = jnp.zeros((HALO, PW), f32)
            _chip_exchange_start(ein, eout, *sems)

        dz1 = dz1_ref[...]
        dz1b = dz1.astype(bf16)
        dcat = _dot(dz1b, wout_ref[...], NT)
        dwout_s[...] += _dot(cat_ref[...], dz1b, TN)

        pos1 = (tile_idx * tt + lax.broadcasted_iota(jnp.int32, (tt, 1), 0) + 1).astype(f32)
        for gi, w in enumerate(WINDOWS):
            sl = slice(gi * DH, (gi + 1) * DH)
            dpo = dcat[:, RW + gi * DH: RW + (gi + 1) * DH]
            pooled_g = pooled_ref[:, sl]
            ylin = _dot(pooled_g, wpool_ref[gi])
            dpscale_ref[:, sl] += jnp.sum(dpo * ylin, axis=0, keepdims=True)
            dpw = (dpo * pscale_ref[:, sl]).astype(bf16)
            dwpool_ref[gi] += _dot(pooled_g, dpw, TN)
            dpooled = _dot(dpw, wpool_ref[gi], NT)
            eext_s[pl.ds(0, tt), sl] = dpooled / jnp.minimum(pos1, float(w))
            stages = int(math.log2(w))
            src = eext_s
            for s in range(stages):
                n = tt + 8 * (stages - 1 - s)
                shift = 2 ** s
                val = src[pl.ds(0, n), sl] + src[pl.ds(shift, n), sl]
                if s == stages - 1:
                    wsum = val
                else:
                    tmp_s[pl.ds(0, n), sl] = val
                    src = tmp_s
            dproj_ref[:, 4 * RW + gi * DH: 4 * RW + (gi + 1) * DH] = (wsum - dpooled).astype(bf16)
        eext_s[pl.ds(tt, HALO), :] = eext_s[pl.ds(0, HALO), :]

        for h in range(HEADS):
            sl = slice(h * DH, (h + 1) * DH)
            dr = dcat[:, sl]
            o = oret_ref[:, sl]
            r = lax.rsqrt(jnp.mean(o * o, axis=-1, keepdims=True) + RMS_EPS)
            rn = o * r
            gg = g_ref[:, sl]
            sg = _sigmoid(gg)
            dproj_ref[:, 3 * RW + h * DH: 3 * RW + (h + 1) * DH] = (dr * rn * (sg * (1.0 + gg * (1.0 - sg)))).astype(bf16)
            drn = dr * (gg * sg)
            dout_s[:, sl] = (r * (drn - rn * jnp.mean(drn * rn, axis=-1, keepdims=True))).astype(bf16)

        for sub in reversed(range(tt // RET_TILE)):
            rows = pl.ds(sub * RET_TILE, RET_TILE)
            cos_t, sin_t = cos_ref[rows, :], sin_ref[rows, :]
            for h in range(HEADS):
                q = qkv_ref[rows, h * DH:(h + 1) * DH]
                k = qkv_ref[rows, RW + h * DH: RW + (h + 1) * DH]
                v = qkv_ref[rows, 2 * RW + h * DH: 2 * RW + (h + 1) * DH]
                do = dout_s[rows, h * DH:(h + 1) * DH]
                stb = states_ref[sub, h]
                dst = dstate_s[h]
                dstb = dst.astype(bf16)
                sb = (_dot(q, k, NT) * dmat_ref[h]).astype(bf16)
                dsb = (_dot(do, v, NT) * dmat_ref[h]).astype(bf16)
                dq = _dot(dsb, k) + _dot(do, stb, NT) * qd_ref[h]
                dk = _dot(dsb, q, TN) + _dot(v, dstb, NT) * kd_ref[h]
                dv = _dot(sb, do, TN) + _dot((k.astype(f32) * kd_ref[h]).astype(bf16), dstb)
                dstate_s[h] = dst * cdec[h] + _dot((q.astype(f32) * qd_ref[h]).astype(bf16), do, TN)
                dproj_ref[rows, h * DH:(h + 1) * DH] = (dq * cos_t - _swap_halves(dq) * sin_t).astype(bf16)
                dproj_ref[rows, RW + h * DH: RW + (h + 1) * DH] = (
                    (dk * cos_t - _swap_halves(dk) * sin_t) * K_SCALE).astype(bf16)
                dproj_ref[rows, 2 * RW + h * DH: 2 * RW + (h + 1) * DH] = dv.astype(bf16)

        gx_ref[...] = ALPHA * dz1 + _dot(dproj_ref[...], wint_ref[...])

        @pl.when(i == n_tiles - 1)
        def _():
            dwout_ref[...] = dwout_s[...].astype(bf16)
            _chip_exchange_finish(ein, eout, *sems)

    rtile = lambda w: pl.BlockSpec((tt, w), lambda i: (n_tiles - 1 - i, 0))
    hbm = pl.BlockSpec(memory_space=pltpu.HBM)
    out_shape = (
        jax.ShapeDtypeStruct((T, IN_W), bf16),
        jax.ShapeDtypeStruct((T, D), f32),
        jax.ShapeDtypeStruct((GROUPS, DH, DH), f32),
        jax.ShapeDtypeStruct((1, PW), f32),
        jax.ShapeDtypeStruct((D, D), bf16),
    ) + tuple(jax.ShapeDtypeStruct(e.shape, e.dtype) for e in exchange)
    return pl.pallas_call(
        body, name="mix_backward", grid=(n_tiles,), out_shape=out_shape,
        in_specs=[rtile(D), _const_spec((D, D)), rtile(3 * RW), rtile(RW), rtile(RW),
                  pl.BlockSpec((tt // RET_TILE, HEADS, DH, DH), lambda i: (n_tiles - 1 - i, 0, 0, 0)),
                  rtile(PW), rtile(D), rtile(DH), rtile(DH),
                  _const_spec((HEADS, RET_TILE, RET_TILE)), _const_spec((HEADS, RET_TILE, DH)),
                  _const_spec((HEADS, RET_TILE, DH)),
                  _const_spec((GROUPS, DH, DH)), _const_spec((1, PW)), _const_spec((IN_W, D))] + [hbm] * n_e,
        out_specs=(rtile(IN_W), rtile(D), pl.BlockSpec((GROUPS, DH, DH), lambda i: (0, 0, 0)),
                   pl.BlockSpec((1, PW), lambda i: (0, 0)),
                   pl.BlockSpec((D, D), lambda i: (0, 0), pipeline_mode=pl.Buffered(1))) + (hbm,) * n_e,
        scratch_shapes=[pltpu.VMEM((HEADS, DH, DH), f32), pltpu.VMEM((tt, RW), bf16),
                        pltpu.VMEM((tt + HALO, PW), f32), pltpu.VMEM((tt + HALO, PW), f32),
                        pltpu.VMEM((D, D), f32)] + _chip_exchange_sems(n_e),
        compiler_params=pltpu.CompilerParams(dimension_semantics=("arbitrary",), vmem_limit_bytes=V7X_VMEM_LIMIT,
                                             collective_id=CHIP_BARRIER),
    )(dz1, w_out, qkv, g, oret, states, pooled, cat, cos, sin, dmat, qd, kd, w_pool, pool_scale, w_in_t, *exchange)


def _weight_grad(a, b, name, tm, exchange=(), tk=2048):
    m = a.shape[1]
    n_m, n_k, n_e = m // tm, T // tk, len(exchange)
    rows = m // N_DEV
    ppt = tm // (2 * rows)
    assert tm % (2 * rows) == 0 and n_m * ppt == N_DEV // 2

    def body(a_ref, b_ref, *rest):
        ein, o_ref, land_ref, eout = rest[:n_e], rest[n_e], rest[n_e + 1], rest[n_e + 2:2 * n_e + 2]
        acc_s, send_s, send_sems, recv_sems, *esems = rest[2 * n_e + 2:]
        i, k = pl.program_id(0), pl.program_id(1)
        x, y, c = _me()
        sibling = (x, y, 1 - c)

        def pair_copy(p, v):
            src = send_s.at[pl.ds(pl.multiple_of((2 * v + 1 - c) * rows, 16), rows), :]
            return _remote(src, land_ref.at[p], send_sems.at[p], recv_sems.at[p], sibling)

        @pl.when((i == 0) & (k == 0))
        def _():
            _barrier((_chip_peers() if n_e else []) + [sibling])
            if n_e:
                _chip_exchange_start(ein, eout, *esems, barrier=False)

        @pl.when(k == 0)
        def _():
            acc_s[...] = jnp.zeros_like(acc_s)

        acc_s[...] += _dot(a_ref[...], b_ref[pl.ds(pl.multiple_of(k * tk, tk), tk), :].astype(bf16), TN)

        for ti in range(n_m):
            @pl.when((i == ti) & (k == n_k - 1))
            def _(ti=ti):
                for v in range(ppt if ti else 0):
                    pair_copy((ti - 1) * ppt + v, v).wait_send()
                out = acc_s[...].astype(bf16)
                o_ref[...] = out
                send_s[...] = out
                for v in range(ppt):
                    pair_copy(ti * ppt + v, v).start()

        @pl.when((i == n_m - 1) & (k == n_k - 1))
        def _():
            for v in range(ppt):
                pair_copy((n_m - 1) * ppt + v, v).wait_send()
            for p in range(N_DEV // 2):
                pair_copy(p, 0).wait_recv()
            if n_e:
                _chip_exchange_finish(ein, eout, *esems)

    hbm = pl.BlockSpec(memory_space=pltpu.HBM)
    return pl.pallas_call(
        body, name=name, grid=(n_m, n_k),
        out_shape=(jax.ShapeDtypeStruct((m, D), bf16), jax.ShapeDtypeStruct((N_DEV // 2, rows, D), bf16))
        + tuple(jax.ShapeDtypeStruct(e.shape, e.dtype) for e in exchange),
        in_specs=[pl.BlockSpec((tk, tm), lambda i, k: (k, i)),
                  pl.BlockSpec((T, D), lambda i, k: (0, 0), pipeline_mode=pl.Buffered(1))] + [hbm] * n_e,
        out_specs=(pl.BlockSpec((tm, D), lambda i, k: (i, 0)), hbm) + (hbm,) * n_e,
        scratch_shapes=[pltpu.VMEM((tm, D), f32), pltpu.VMEM((tm, D), bf16),
                        pltpu.SemaphoreType.DMA((N_DEV // 2,)), pltpu.SemaphoreType.DMA((N_DEV // 2,))]
        + _chip_exchange_sems(n_e),
        compiler_params=pltpu.CompilerParams(dimension_semantics=("arbitrary", "arbitrary"),
                                             vmem_limit_bytes=V7X_VMEM_LIMIT,
                                             collective_id=CHIP_AND_GATHER_BARRIER if n_e else PAIR_BARRIER),
    )(a, b, *exchange)


def _pair_sum(dw, landing, name):
    rows = landing.shape[1]

    def body(dw_ref, land_ref, own_ref, others_ref, mine_s, theirs_s, sems):
        me = _me()
        loads = []
        for k in range(4):
            cx, cy = _chip(me, k)
            loads.append((pltpu.make_async_copy(dw_ref.at[_slot(cx, cy, me[2])], mine_s.at[k], sems.at[0, k]),
                          pltpu.make_async_copy(land_ref.at[2 * cx + cy], theirs_s.at[k], sems.at[1, k])))
            for ld in loads[-1]:
                ld.start()
        for k in range(4):
            for ld in loads[k]:
                ld.wait()
            total = (mine_s[k].astype(f32) + theirs_s[k].astype(f32)).astype(bf16)
            if k == 0:
                own_ref[...] = total
            else:
                others_ref[k - 1] = total

    hbm, vm = pl.BlockSpec(memory_space=pltpu.HBM), pl.BlockSpec(memory_space=pltpu.VMEM)
    return pl.pallas_call(
        body, name=name,
        out_shape=(jax.ShapeDtypeStruct((rows, D), bf16), jax.ShapeDtypeStruct((3, rows, D), bf16)),
        in_specs=[hbm, hbm], out_specs=(vm, vm),
        scratch_shapes=[pltpu.VMEM((4, rows, D), bf16), pltpu.VMEM((4, rows, D), bf16), pltpu.SemaphoreType.DMA((2, 4))],
        compiler_params=pltpu.CompilerParams(vmem_limit_bytes=V7X_VMEM_LIMIT),
    )(dw.reshape(N_DEV, rows, D), landing)


CHIP_FLIPS = ((1, 0), (0, 1), (1, 1))
PAIR_BARRIER, CHIP_BARRIER, GATHER_BARRIER, CHIP_AND_GATHER_BARRIER = 0, 1, 2, 3


def _barrier(peers):
    sem = pltpu.get_barrier_semaphore()
    for peer in peers:
        pl.semaphore_signal(sem, inc=1, device_id=peer, device_id_type=pl.DeviceIdType.MESH)
    pl.semaphore_wait(sem, len(peers))


def _me():
    return lax.axis_index("x"), lax.axis_index("y"), lax.axis_index("c")


def _chip(me, k):
    x, y, _ = me
    if k == 0:
        return x, y
    fx, fy = CHIP_FLIPS[k - 1]
    return (1 - x if fx else x), (1 - y if fy else y)


def _slot(x, y, c):
    return 4 * x + 2 * y + c


def _remote(src, dst, send_sem, recv_sem, to):
    return pltpu.make_async_remote_copy(src_ref=src, dst_ref=dst, send_sem=send_sem, recv_sem=recv_sem,
                                        device_id=to, device_id_type=pl.DeviceIdType.MESH)


def _gather_sems(n):
    return [pltpu.SemaphoreType.DMA((7, n)), pltpu.SemaphoreType.DMA((7, n)), pltpu.SemaphoreType.DMA((n,))] if n else []


def _gather_copy(k, j, gin, gout, send_sems, recv_sems, sending):
    x, y, c = _me()
    sibling, x_chip, y_chip, d_chip = (x, y, 1 - c), (1 - x, y), (x, 1 - y), (1 - x, 1 - y)
    south = c == 0
    passed_on = (jnp.where(south, 1 - x, x), jnp.where(south, y, 1 - y), c)
    src, to = gin[j], sibling
    if sending:
        block = {0: (x, y, c), 1: (x, y, c), 2: (x, y, c), 3: passed_on, 4: (*x_chip, c), 5: (*y_chip, c), 6: (*d_chip, c)}[k]
        to = {1: (*x_chip, c), 2: (*y_chip, c), 3: (jnp.where(south, x, 1 - x), jnp.where(south, 1 - y, y), c)}.get(k, sibling)
        if k >= 3:
            src = gout[j].at[_slot(*block)]
    else:
        block = {0: sibling, 1: (*x_chip, c), 2: (*y_chip, c), 3: (*d_chip, c), 4: (*x_chip, 1 - c), 5: (*y_chip, 1 - c),
                 6: (*d_chip, 1 - c)}[k]
    return _remote(src, gout[j].at[_slot(*block)], send_sems.at[k, j], recv_sems.at[k, j], to)


def _gather_do(ks, action, gin, gout, send_sems, recv_sems):
    for k in ks:
        for j in range(len(gin)):
            cp = _gather_copy(k, j, gin, gout, send_sems, recv_sems, action != "wait_recv")
            getattr(cp, action)()


def _gather_peers():
    x, y, c = _me()
    return [(x, y, 1 - c), (1 - x, y, c), (x, 1 - y, c)]


def _gather_start(gin, gout, send_sems, recv_sems, local_sems, barrier=True):
    if barrier:
        _barrier(_gather_peers())
    for j in range(len(gin)):
        pltpu.make_async_copy(gin[j], gout[j].at[_slot(*_me())], local_sems.at[j]).start()
    _gather_do((0, 1, 2), "start", gin, gout, send_sems, recv_sems)


def _gather_forward(gin, gout, send_sems, recv_sems, local_sems):
    _gather_do((1, 2), "wait_recv", gin, gout, send_sems, recv_sems)
    _gather_do((3, 4, 5), "start", gin, gout, send_sems, recv_sems)


def _gather_finish(gin, gout, send_sems, recv_sems, local_sems):
    _gather_do((3,), "wait_recv", gin, gout, send_sems, recv_sems)
    _gather_do((6,), "start", gin, gout, send_sems, recv_sems)
    _gather_do((0, 4, 5, 6), "wait_recv", gin, gout, send_sems, recv_sems)
    _gather_do(range(7), "wait_send", gin, gout, send_sems, recv_sems)
    for j in range(len(gin)):
        pltpu.make_async_copy(gin[j], gout[j].at[_slot(*_me())], local_sems.at[j]).wait()


def _all_gather(blocks, name):
    n = len(blocks)

    def body(*refs):
        gin, gout, sems = refs[:n], refs[n:2 * n], refs[2 * n:]
        _gather_start(gin, gout, *sems)
        _gather_forward(gin, gout, *sems)
        _gather_finish(gin, gout, *sems)

    hbm = pl.BlockSpec(memory_space=pltpu.HBM)
    return pl.pallas_call(
        body, name=name,
        out_shape=tuple(jax.ShapeDtypeStruct((N_DEV,) + b.shape, b.dtype) for b in blocks),
        in_specs=[hbm] * n, out_specs=(hbm,) * n, scratch_shapes=_gather_sems(n),
        compiler_params=pltpu.CompilerParams(collective_id=GATHER_BARRIER),
    )(*blocks)


def _pair_reduce(parts, name):
    n = len(parts)

    def body(*refs):
        ins, own, others, landing, mine = (refs[k * n:(k + 1) * n] for k in range(5))
        send_sems, recv_sems, local_sems = refs[5 * n:]
        me = _me()
        x, y, c = me
        sibling = (x, y, 1 - c)
        _barrier([sibling])
        sends, loads = [], []
        for k in range(4):
            for j in range(n):
                cp = _remote(ins[j].at[_slot(*_chip(me, k), 1 - c)], landing[j].at[k], send_sems.at[k, j],
                             recv_sems.at[k, j], sibling)
                cp.start()
                sends.append(cp)
                ld = pltpu.make_async_copy(ins[j].at[_slot(*_chip(me, k), c)], mine[j].at[k], local_sems.at[k, j])
                ld.start()
                loads.append(ld)
        for k in range(4):
            for j in range(n):
                loads[k * n + j].wait()
                _remote(ins[j].at[0], landing[j].at[k], send_sems.at[k, j], recv_sems.at[k, j], sibling).wait_recv()
                total = mine[j][k].astype(f32) + landing[j][k].astype(f32)
                if k == 0:
                    own[j][...] = total.astype(own[j].dtype)
                else:
                    others[j][k - 1] = total.astype(others[j].dtype)
        for cp in sends:
            cp.wait_send()

    vm = pl.BlockSpec(memory_space=pltpu.VMEM)
    return pl.pallas_call(
        body, name=name,
        out_shape=tuple(jax.ShapeDtypeStruct(p.shape[1:], p.dtype) for p in parts)
        + tuple(jax.ShapeDtypeStruct((3,) + p.shape[1:], p.dtype) for p in parts),
        in_specs=[pl.BlockSpec(memory_space=pltpu.HBM)] * n, out_specs=(vm,) * (2 * n),
        scratch_shapes=[pltpu.VMEM((4,) + p.shape[1:], p.dtype) for p in parts] * 2
        + [pltpu.SemaphoreType.DMA((4, n)), pltpu.SemaphoreType.DMA((4, n)), pltpu.SemaphoreType.DMA((4, n))],
        compiler_params=pltpu.CompilerParams(vmem_limit_bytes=V7X_VMEM_LIMIT, collective_id=PAIR_BARRIER),
    )(*parts)


def _chip_exchange_sems(n):
    return [pltpu.SemaphoreType.DMA((3, n)), pltpu.SemaphoreType.DMA((3, n))] if n else []


def _chip_exchange_copy(k, j, ein, eout, send_sems, recv_sems):
    me = _me()
    return _remote(ein[j].at[k - 1], eout[j].at[k - 1], send_sems.at[k - 1, j], recv_sems.at[k - 1, j],
                   (*_chip(me, k), me[2]))


def _chip_peers():
    me = _me()
    return [(*_chip(me, k), me[2]) for k in range(1, 4)]


def _chip_exchange_start(ein, eout, send_sems, recv_sems, barrier=True):
    if barrier:
        _barrier(_chip_peers())
    for k in range(1, 4):
        for j in range(len(ein)):
            _chip_exchange_copy(k, j, ein, eout, send_sems, recv_sems).start()


def _chip_exchange_finish(ein, eout, send_sems, recv_sems):
    for k in range(1, 4):
        for j in range(len(ein)):
            _chip_exchange_copy(k, j, ein, eout, send_sems, recv_sems).wait_recv()
    for k in range(1, 4):
        for j in range(len(ein)):
            _chip_exchange_copy(k, j, ein, eout, send_sems, recv_sems).wait_send()


def _chip_exchange_and_gather(others, blocks, name):
    n_e, n_g = len(others), len(blocks)

    def body(*refs):
        ein, gin, eout, gout = refs[:n_e], refs[n_e:n_e + n_g], refs[n_e + n_g:2 * n_e + n_g], refs[2 * n_e + n_g:2 * (n_e + n_g)]
        esems, gsems = refs[2 * (n_e + n_g):2 * (n_e + n_g) + 2], refs[2 * (n_e + n_g) + 2:]
        _barrier(_chip_peers() + _gather_peers()[:1])
        _chip_exchange_start(ein, eout, *esems, barrier=False)
        _gather_start(gin, gout, *gsems, barrier=False)
        _gather_forward(gin, gout, *gsems)
        _gather_finish(gin, gout, *gsems)
        _chip_exchange_finish(ein, eout, *esems)

    hbm = pl.BlockSpec(memory_space=pltpu.HBM)
    return pl.pallas_call(
        body, name=name,
        out_shape=tuple(jax.ShapeDtypeStruct(e.shape, e.dtype) for e in others)
        + tuple(jax.ShapeDtypeStruct((N_DEV,) + b.shape, b.dtype) for b in blocks),
        in_specs=[hbm] * (n_e + n_g), out_specs=(hbm,) * (n_e + n_g),
        scratch_shapes=_chip_exchange_sems(n_e) + _gather_sems(n_g),
        compiler_params=pltpu.CompilerParams(collective_id=CHIP_AND_GATHER_BARRIER),
    )(*others, *blocks)


def _sum_parts(owns, arrived, name):
    n = len(owns)

    def body(*refs):
        for own, arr, out in zip(refs[:n], refs[n:2 * n], refs[2 * n:]):
            acc = own[...].astype(f32)
            for k in range(3):
                acc = acc + arr[k].astype(f32)
            out[...] = acc

    vm = pl.BlockSpec(memory_space=pltpu.VMEM)
    return pl.pallas_call(
        body, name=name, out_shape=tuple(jax.ShapeDtypeStruct(o.shape, f32) for o in owns),
        in_specs=[vm] * (2 * n), out_specs=(vm,) * n,
        compiler_params=pltpu.CompilerParams(vmem_limit_bytes=V7X_VMEM_LIMIT),
    )(*owns, *arrived)


ADAM_C1 = 1.0 / (1.0 - ADAM_B1 ** ADAM_STEP)
ADAM_C2 = 1.0 / (1.0 - ADAM_B2 ** ADAM_STEP)


def _adam_update(w, g, m, v):
    m = ADAM_B1 * m + (1.0 - ADAM_B1) * g
    v = ADAM_B2 * v + (1.0 - ADAM_B2) * (g * g)
    return -ADAM_LR * ((m * ADAM_C1) / (jnp.sqrt(v * ADAM_C2) + ADAM_EPS) + ADAM_WD * w), m, v


def _sum_adamw(own, arrived, w, m, v, name, steps):
    rows = own.shape[0]
    br = rows // steps

    def body(own_ref, arr_ref, w_ref, m_ref, v_ref, g_out, d_out, m_out, v_out):
        g = own_ref[...].astype(f32)
        for k in range(3):
            g = g + arr_ref[k].astype(f32)
        g_out[...] = g
        d_out[...], m_out[...], v_out[...] = _adam_update(w_ref[...], g, m_ref[...], v_ref[...])

    blk = pl.BlockSpec((br, D), lambda i: (i, 0))
    return pl.pallas_call(
        body, name=name, grid=(steps,), out_shape=(jax.ShapeDtypeStruct((rows, D), f32),) * 4,
        in_specs=[blk, pl.BlockSpec((3, br, D), lambda i: (0, i, 0)), blk, blk, blk], out_specs=(blk,) * 4,
        compiler_params=pltpu.CompilerParams(dimension_semantics=("parallel",), vmem_limit_bytes=V7X_VMEM_LIMIT),
    )(own, arrived, w, m, v)


def _adamw(ws, gs, ms, vs, name):
    n = len(ws)

    def body(*refs):
        w_r, g_r, m_r, v_r = (refs[k * n:(k + 1) * n] for k in range(4))
        d_o, m_o, v_o = (refs[(4 + k) * n:(5 + k) * n] for k in range(3))
        for j in range(n):
            d_o[j][...], m_o[j][...], v_o[j][...] = _adam_update(w_r[j][...], g_r[j][...], m_r[j][...], v_r[j][...])

    vm = pl.BlockSpec(memory_space=pltpu.VMEM)
    shapes = tuple(jax.ShapeDtypeStruct(w.shape, f32) for w in ws)
    return pl.pallas_call(
        body, name=name, out_shape=shapes * 3, in_specs=[vm] * (4 * n), out_specs=tuple([vm] * (3 * n)),
        compiler_params=pltpu.CompilerParams(vmem_limit_bytes=V7X_VMEM_LIMIT),
    )(*ws, *gs, *ms, *vs)


SMALL = (("w_pool", GROUPS * DH * DH), ("pool_scale", PW), ("ln1_g", D), ("ln1_b", D), ("conv_b", D_FF),
         ("ln2_g", D), ("ln2_b", D), ("conv_w", 3 * D_FF), ("loss", 1))
SMALL_ROWS = 640


def _pack(named):
    flat = jnp.concatenate([named[k].reshape(-1) for k, _ in SMALL])
    return jnp.pad(flat, (0, SMALL_ROWS * 128 - flat.shape[0])).reshape(SMALL_ROWS, 128)


def _unpack(packed):
    flat, out, at = packed.reshape(-1), {}, 0
    for k, size in SMALL:
        out[k] = flat[at:at + size]
        at += size
    return out


def kernel(x, w_in, w_pool, pool_scale, w_out, ln1_g, ln1_b, w_up, conv_w, conv_b, w_down, ln2_g, ln2_b, loss_target, m_w_in, m_w_pool, m_pool_scale, m_w_out, m_ln1_g, m_ln1_b, m_w_up, m_conv_w, m_conv_b, m_w_down, m_ln2_g, m_ln2_b, v_w_in, v_w_pool, v_pool_scale, v_w_out, v_ln1_g, v_ln1_b, v_w_up, v_conv_w, v_conv_b, v_w_down, v_ln2_g, v_ln2_b):
    me = 4 * lax.axis_index("x") + 2 * lax.axis_index("y") + lax.axis_index("c")
    x2, tgt = x[0], loss_target[0]

    g_in, g_out, g_cw = _all_gather([w_in[0].T.astype(bf16), w_out[0].astype(bf16), jnp.transpose(conv_w, (1, 0, 2))],
                                    "gather_weights")
    w_in_t = g_in.reshape(IN_W, D)
    w_out_f = g_out.reshape(D, D)
    conv_w_f = jnp.transpose(g_cw[:, :, 0, :], (1, 0, 2)).reshape(3, D_FF)
    w_pool_b = w_pool[0].astype(bf16)

    cos, sin = _rope_tables()
    dmat, qd, kd, cdec = _decay_tables(RET_TILE)

    qkv, g, oret, states, cat, pooled, xhat1, rstd1, x1b, g_up, g_down = _mix_forward(
        x2, w_in_t, cos, sin, dmat, qd, kd, cdec, w_pool_b, pool_scale, w_out_f, ln1_g, ln1_b,
        gather=[w_up[0].T.astype(bf16), w_down[0].astype(bf16)])
    w_up_t = g_up.reshape(2 * D_FF, D)
    w_down_f = g_down.reshape(D_FF, D)
    dz1, dz2b, du, f, loss8, d_ln2_g, d_ln2_b, d_ln1_g, d_ln1_b, d_conv_b, d_conv_w = _ffn_forward_backward(
        xhat1, rstd1, ln1_g, ln1_b, w_up_t, conv_w_f, conv_b, w_down_f, ln2_g, ln2_b, tgt)

    own_down, oth_down = _pair_sum(*_weight_grad(f, dz2b, "grad_w_down", tm=D_FF // 2), "pair_sum_down")
    dw_up_t, land_up, arr_down = _weight_grad(du, x1b, "grad_w_up", tm=D_FF // 2, exchange=[oth_down])
    own_up, oth_up = _pair_sum(dw_up_t, land_up, "pair_sum_up")
    dproj, grad_x, d_w_pool, d_pool_scale, dw_out, arr_up = _mix_backward(
        dz1, w_out_f, qkv, g, oret, states, pooled, cat, cos, sin, dmat, qd, kd, cdec, w_pool_b, pool_scale, w_in_t,
        exchange=[oth_up])
    small = _pack({"w_pool": d_w_pool, "pool_scale": d_pool_scale, "ln1_g": d_ln1_g, "ln1_b": d_ln1_b,
                   "conv_b": d_conv_b, "ln2_g": d_ln2_g, "ln2_b": d_ln2_b, "conv_w": d_conv_w, "loss": loss8[0, :1]})
    own_out, own_small, oth_out, oth_small = _pair_reduce(
        [dw_out.reshape(N_DEV, ROWS_OUT, D), small.reshape(N_DEV, SMALL_ROWS // N_DEV, 128)], "pair_reduce_out")
    dw_in_t, land_in, arr_out, arr_small = _weight_grad(dproj, x2, "grad_w_in", tm=IN_W // 2,
                                                        exchange=[oth_out, oth_small])
    own_in, oth_in = _pair_sum(dw_in_t, land_in, "pair_sum_in")
    (small_piece,) = _sum_parts([own_small], [arr_small], "sum_small_grads")
    arr_in, gs_small = _chip_exchange_and_gather([oth_in], [small_piece], "exchange_in_gather_small")

    names = ["w_in", "w_pool", "pool_scale", "w_out", "ln1_g", "ln1_b", "w_up", "conv_w", "conv_b", "w_down",
             "ln2_g", "ln2_b"]
    w_d = dict(w_in=w_in, w_pool=w_pool, pool_scale=pool_scale, w_out=w_out, ln1_g=ln1_g, ln1_b=ln1_b, w_up=w_up,
               conv_w=conv_w, conv_b=conv_b, w_down=w_down, ln2_g=ln2_g, ln2_b=ln2_b)
    m_d = dict(w_in=m_w_in, w_pool=m_w_pool, pool_scale=m_pool_scale, w_out=m_w_out, ln1_g=m_ln1_g, ln1_b=m_ln1_b,
               w_up=m_w_up, conv_w=m_conv_w, conv_b=m_conv_b, w_down=m_w_down, ln2_g=m_ln2_g, ln2_b=m_ln2_b)
    v_d = dict(w_in=v_w_in, w_pool=v_w_pool, pool_scale=v_pool_scale, w_out=v_w_out, ln1_g=v_ln1_g, ln1_b=v_ln1_b,
               w_up=v_w_up, conv_w=v_conv_w, conv_b=v_conv_b, w_down=v_w_down, ln2_g=v_ln2_g, ln2_b=v_ln2_b)
    g_d, delta, new_m, new_v = {}, {}, {}, {}

    big = (("w_in", own_in, arr_in, True, 4), ("w_out", own_out, arr_out, False, 2),
           ("w_up", own_up, arr_up, True, 4), ("w_down", own_down, arr_down, False, 2))
    for k, own, arr, transposed, steps in big:
        lay = (lambda a: a[0].T) if transposed else (lambda a: a[0])
        back = (lambda a: a.T[None]) if transposed else (lambda a: a[None])
        res = _sum_adamw(own, arr, lay(w_d[k]), lay(m_d[k]), lay(v_d[k]), "adamw_" + k, steps)
        g_d[k], delta[k], new_m[k], new_v[k] = (back(r) for r in res)

    gsm = _unpack(gs_small)
    gsm["conv_w"] = lax.dynamic_slice(gsm["conv_w"].reshape(3, D_FF), (0, me * (D_FF // N_DEV)), (3, D_FF // N_DEV))
    lay = lambda k, a: jnp.transpose(a, (1, 0, 2)) if k == "conv_w" else a.reshape(-1, a.shape[-1])
    back = lambda k, a: jnp.transpose(a, (1, 0, 2)) if k == "conv_w" else a.reshape(w_d[k].shape)
    group = [k for k in names if k not in g_d]
    for k in group:
        g_d[k] = gsm[k].reshape(w_d[k].shape)
    res = _adamw([lay(k, w_d[k]) for k in group], [lay(k, g_d[k]) for k in group], [lay(k, m_d[k]) for k in group],
                 [lay(k, v_d[k]) for k in group], "adamw_small")
    for j, k in enumerate(group):
        delta[k], new_m[k], new_v[k] = (back(k, res[part * len(group) + j]) for part in range(3))

    loss = gsm["loss"].reshape(())
    return (loss, grad_x[None], *[g_d[k] for k in names], *[delta[k] for k in names], *[new_m[k] for k in names],
            *[new_v[k] for k in names])
```

```python
import functools
import math

import numpy as np
import jax
import jax.numpy as jnp
from jax import lax
from jax.experimental import pallas as pl
from jax.experimental.pallas import tpu as pltpu

f32 = jnp.float32
bf16 = jnp.bfloat16

N_DEV = 8
T = 4096
D = 1024
CHUNK = 64
MIX_TILE = 512
RET_TILE = 256
HEADS = 4
DH = 128
RW = HEADS * DH
PW = 512
GROUPS = 4
WINDOWS = (2, 4, 8, 16)
IN_W = 4 * RW + PW
D_FF = 2816
LN_EPS = 1e-5
RMS_EPS = 1e-6
ALPHA = 2.0 ** 0.25
K_SCALE = DH ** -0.5

ADAM_LR = 0.001
ADAM_B1 = 0.9
ADAM_B2 = 0.999
ADAM_EPS = 1e-08
ADAM_WD = 0.01
ADAM_STEP = 10

ROWS_IN, ROWS_OUT, ROWS_UP, ROWS_DOWN = IN_W // N_DEV, D // N_DEV, 2 * D_FF // N_DEV, D_FF // N_DEV

V7X_VMEM_LIMIT = 56 * 2 ** 20
HALO = 32

NT = (((1,), (1,)), ((), ()))
TN = (((0,), (0,)), ((), ()))
NN = (((1,), (0,)), ((), ()))


def _dot(a, b, dims=NN):
    return lax.dot_general(a, b, dims, preferred_element_type=f32)


def _const_spec(shape):
    zeros = (0,) * len(shape)
    return pl.BlockSpec(shape, lambda i: zeros, pipeline_mode=pl.Buffered(1))


def _sigmoid(x):
    return 0.5 * jnp.tanh(0.5 * x) + 0.5


def _decay_tables(tt):
    h = np.arange(HEADS, dtype=np.float64)
    log_gamma = np.log(1.0 - 2.0 ** (-5.0 - h)).astype(np.float32).astype(np.float64)[:, None, None]
    idx = np.arange(tt, dtype=np.float64)
    visible = (idx[None, :] // CHUNK) <= (idx[:, None] // CHUNK)
    mask = np.where(visible[None], np.exp(log_gamma * np.abs(idx[:, None] - idx[None, :])[None]), 0.0)
    qd = np.broadcast_to(np.exp(log_gamma * (idx[None, :, None] + 1.0)), (HEADS, tt, DH))
    kd = np.broadcast_to(np.exp(log_gamma * (tt - 1.0 - idx[None, :, None])), (HEADS, tt, DH))
    cd = np.exp(log_gamma[:, 0, 0] * tt)
    return (jnp.asarray(mask, f32), jnp.asarray(qd, f32), jnp.asarray(kd, f32), [float(c) for c in cd])


def _rope_tables():
    inv_freq = (10000.0 ** (-np.arange(0, DH, 2, dtype=np.float64) / DH)).astype(np.float32)
    ang = (np.arange(T, dtype=np.float32)[:, None] * inv_freq[None, :]).astype(np.float64)
    cos, sin = np.cos(ang), np.sin(ang)
    return (jnp.asarray(np.concatenate([cos, cos], axis=1), f32), jnp.asarray(np.concatenate([-sin, sin], axis=1), f32))


def _swap_halves(t):
    return pltpu.roll(t, DH // 2, axis=1)


def _mix_forward(x, w_in_t, cos, sin, dmat, qd, kd, cdec, w_pool, pool_scale, w_out, ln1_g, ln1_b, gather,
                 tt=MIX_TILE):
    n_tiles = T // tt
    n_g = len(gather)

    def body(x_ref, wint_ref, cos_ref, sin_ref, dmat_ref, qd_ref, kd_ref, wpool_ref, pscale_ref, wout_ref,
             g1_ref, b1_ref, *rest):
        gin, rest = rest[:n_g], rest[n_g:]
        qkv_ref, g_ref, oret_ref, states_ref, cat_ref, pooled_ref, xhat_ref, rstd_ref, x1b_ref = rest[:9]
        gout, (state_s, pext_s, tmp_s, *sems) = rest[9:9 + n_g], rest[9 + n_g:]
        i = pl.program_id(0)

        @pl.when(i == 0)
        def _():
            state_s[...] = jnp.zeros_like(state_s)
            pext_s[pl.ds(0, HALO), :] = jnp.zeros((HALO, PW), f32)
            _gather_start(gin, gout, *sems)

        @pl.when(i == n_tiles - 2)
        def _():
            _gather_forward(gin, gout, *sems)

        xb = x_ref[...].astype(bf16)
        cos_t, sin_t = cos_ref[...], sin_ref[...]
        for part in range(2):
            pr = _dot(xb, wint_ref[pl.ds(part * RW, RW), :], NT)
            for h in range(HEADS):
                t = pr[:, h * DH:(h + 1) * DH]
                r = t * cos_t + _swap_halves(t) * sin_t
                if part == 1:
                    r = r * K_SCALE
                qkv_ref[:, part * RW + h * DH: part * RW + (h + 1) * DH] = r.astype(bf16)
        qkv_ref[:, 2 * RW:3 * RW] = _dot(xb, wint_ref[pl.ds(2 * RW, RW), :], NT).astype(bf16)
        g_ref[...] = _dot(xb, wint_ref[pl.ds(3 * RW, RW), :], NT)
        pext_s[pl.ds(HALO, tt), :] = _dot(xb, wint_ref[pl.ds(4 * RW, PW), :], NT)

        for sub in range(tt // RET_TILE):
            rows = pl.ds(sub * RET_TILE, RET_TILE)
            for h in range(HEADS):
                q = qkv_ref[rows, h * DH:(h + 1) * DH]
                k = qkv_ref[rows, RW + h * DH: RW + (h + 1) * DH]
                v = qkv_ref[rows, 2 * RW + h * DH: 2 * RW + (h + 1) * DH]
                s = _dot(q, k, NT) * dmat_ref[h]
                st = state_s[h]
                stb = st.astype(bf16)
                states_ref[sub, h] = stb
                oret_ref[rows, h * DH:(h + 1) * DH] = (_dot(s.astype(bf16), v)
                                                      + _dot((q.astype(f32) * qd_ref[h]).astype(bf16), stb))
                state_s[h] = st * cdec[h] + _dot((k.astype(f32) * kd_ref[h]).astype(bf16), v, TN)

        for h in range(HEADS):
            sl = slice(h * DH, (h + 1) * DH)
            o = oret_ref[:, sl]
            r = lax.rsqrt(jnp.mean(o * o, axis=-1, keepdims=True) + RMS_EPS)
            gg = g_ref[:, sl]
            cat_ref[:, sl] = (o * r * (gg * _sigmoid(gg))).astype(bf16)

        pos1 = (i * tt + lax.broadcasted_iota(jnp.int32, (tt, 1), 0) + 1).astype(f32)
        for gi, w in enumerate(WINDOWS):
            sl = slice(gi * DH, (gi + 1) * DH)
            stages = int(math.log2(w))
            src = pext_s
            for s in range(stages):
                lo = HALO - 8 * (stages - 1 - s)
                n = tt + HALO - lo
                shift = 2 ** s
                val = src[pl.ds(lo, n), sl] + src[pl.ds(lo - shift, n), sl]
                if s == stages - 1:
                    wsum = val
                else:
                    tmp_s[pl.ds(lo, n), sl] = val
                    src = tmp_s
            p_g = pext_s[pl.ds(HALO, tt), sl]
            pooled = (wsum / jnp.minimum(pos1, float(w)) - p_g).astype(bf16)
            pooled_ref[:, sl] = pooled
            y = _dot(pooled, wpool_ref[gi]) * pscale_ref[:, sl]
            cat_ref[:, RW + gi * DH: RW + (gi + 1) * DH] = y.astype(bf16)
        pext_s[pl.ds(0, HALO), :] = pext_s[pl.ds(tt, HALO), :]

        z = ALPHA * x_ref[...] + _dot(cat_ref[...], wout_ref[...])
        mu = jnp.mean(z, axis=-1, keepdims=True)
        zc = z - mu
        rstd = lax.rsqrt(jnp.mean(zc * zc, axis=-1, keepdims=True) + LN_EPS)
        xhat = zc * rstd
        xhat_ref[...] = xhat
        rstd_ref[...] = rstd
        x1b_ref[...] = (xhat * g1_ref[...] + b1_ref[...]).astype(bf16)

        @pl.when(i == n_tiles - 1)
        def _():
            _gather_finish(gin, gout, *sems)

    tile = lambda w: pl.BlockSpec((tt, w), lambda i: (i, 0))
    hbm = pl.BlockSpec(memory_space=pltpu.HBM)
    out_shape = (
        jax.ShapeDtypeStruct((T, 3 * RW), bf16),
        jax.ShapeDtypeStruct((T, RW), f32),
        jax.ShapeDtypeStruct((T, RW), f32),
        jax.ShapeDtypeStruct((T // RET_TILE, HEADS, DH, DH), bf16),
        jax.ShapeDtypeStruct((T, D), bf16),
        jax.ShapeDtypeStruct((T, PW), bf16),
        jax.ShapeDtypeStruct((T, D), f32),
        jax.ShapeDtypeStruct((T, 1), f32),
        jax.ShapeDtypeStruct((T, D), bf16),
    ) + tuple(jax.ShapeDtypeStruct((N_DEV,) + b.shape, b.dtype) for b in gather)
    return pl.pallas_call(
        body, name="mix_forward", grid=(n_tiles,), out_shape=out_shape,
        in_specs=[tile(D), _const_spec((IN_W, D)), tile(DH), tile(DH),
                  _const_spec((HEADS, RET_TILE, RET_TILE)), _const_spec((HEADS, RET_TILE, DH)),
                  _const_spec((HEADS, RET_TILE, DH)),
                  _const_spec((GROUPS, DH, DH)), _const_spec((1, PW)), _const_spec((D, D)),
                  _const_spec((1, D)), _const_spec((1, D))] + [hbm] * n_g,
        out_specs=(tile(3 * RW), tile(RW), tile(RW),
                   pl.BlockSpec((tt // RET_TILE, HEADS, DH, DH), lambda i: (i, 0, 0, 0)),
                   tile(D), tile(PW), tile(D), tile(1), tile(D)) + (hbm,) * n_g,
        scratch_shapes=[pltpu.VMEM((HEADS, DH, DH), f32), pltpu.VMEM((tt + HALO, PW), f32),
                        pltpu.VMEM((tt + HALO, PW), f32)] + _gather_sems(n_g),
        compiler_params=pltpu.CompilerParams(dimension_semantics=("arbitrary",), vmem_limit_bytes=V7X_VMEM_LIMIT,
                                             collective_id=GATHER_BARRIER),
    )(x, w_in_t, cos, sin, dmat, qd, kd, w_pool, pool_scale, w_out, ln1_g, ln1_b, *gather)


def _ffn_forward_backward(xhat1, rstd1, ln1_g, ln1_b, w_up_t, conv_w, conv_b, w_down, ln2_g, ln2_b, target,
                          tt=256, widths=(512, 512, 512, 512, 512, 256)):
    n_tiles = T // tt
    assert sum(widths) == D_FF and all(w % 128 == 0 for w in widths)
    chunks = [(sum(widths[:c]), w) for c, w in enumerate(widths)]
    FH = 16
    hb = tt // FH

    def body(xhat_ref, halo_ref, rstd_ref, g1_ref, b1_ref, wupt_ref, cw_ref, cb_ref, wdown_ref, g2_ref, b2_ref, tgt_ref,
             dz1_ref, dz2b_ref, du_ref, f_ref, loss_ref, dg2_ref, db2_ref, dg1_ref, db1_ref, dcb_ref, dcw_ref,
             gext_s, val_s, dhext_s):
        i = pl.program_id(0)
        tile_idx = n_tiles - 1 - i

        def rd(ref, off, lo, w):
            return jnp.concatenate([ref[lo // 128 + k, pl.ds(off, tt), :] for k in range(w // 128)], axis=1)

        def wr(ref, lo, val):
            for k in range(val.shape[1] // 128):
                ref[lo // 128 + k, pl.ds(0, val.shape[0]), :] = val[:, k * 128:(k + 1) * 128]

        @pl.when(i == 0)
        def _():
            for r in (loss_ref, dg2_ref, db2_ref, dg1_ref, db1_ref, dcb_ref, dcw_ref):
                r[...] = jnp.zeros_like(r)
            dhext_s[:, pl.ds(tt, 8), :] = jnp.zeros((D_FF // 128, 8, 128), f32)

        g1, b1 = g1_ref[...], b1_ref[...]
        xhat = xhat_ref[...]
        x1 = xhat * g1 + b1
        x1b = x1.astype(bf16)
        x1h = ((halo_ref[...] * g1 + b1) * jnp.where(tile_idx == 0, 0.0, 1.0)).astype(bf16)
        x1ext = jnp.concatenate([x1h, x1b], axis=0)

        for lo, w in chunks:
            cs = slice(lo, lo + w)
            val = _dot(x1b, wupt_ref[pl.ds(lo, w), :], NT)
            gate_ext = _dot(x1ext, wupt_ref[pl.ds(D_FF + lo, w), :], NT)
            wr(gext_s, lo, gate_ext)
            hh = (cb_ref[:, cs] + cw_ref[0:1, cs] * rd(gext_s, FH - 2, lo, w) + cw_ref[1:2, cs] * rd(gext_s, FH - 1, lo, w)
                  + cw_ref[2:3, cs] * gate_ext[FH:])
            sg = _sigmoid(hh)
            act = hh * sg
            wr(dhext_s, lo, act)
            val_s[:, cs] = val * (sg + act * (1.0 - sg))
            f_ref[:, cs] = (act * val).astype(bf16)

        z = ALPHA * x1 + _dot(f_ref[...], wdown_ref[...])
        mu = jnp.mean(z, axis=-1, keepdims=True)
        zc = z - mu
        rstd2 = lax.rsqrt(jnp.mean(zc * zc, axis=-1, keepdims=True) + LN_EPS)
        xh2 = zc * rstd2
        diff = xh2 * g2_ref[...] + b2_ref[...] - tgt_ref[...]
        loss_ref[...] += 0.5 * jnp.sum(diff * diff) / D
        dy = diff * (1.0 / D)
        dg2_ref[...] += jnp.sum(dy * xh2, axis=0, keepdims=True)
        db2_ref[...] += jnp.sum(dy, axis=0, keepdims=True)
        dyg = dy * g2_ref[...]
        dz2 = rstd2 * (dyg - jnp.mean(dyg, axis=-1, keepdims=True) - xh2 * jnp.mean(dyg * xh2, axis=-1, keepdims=True))
        dz2b = dz2.astype(bf16)
        dz2b_ref[...] = dz2b

        for lo, w in chunks:
            cs = slice(lo, lo + w)
            df = _dot(dz2b, wdown_ref[pl.ds(lo, w), :], NT)
            dval = df * rd(dhext_s, 0, lo, w)
            dh = df * val_s[:, cs]
            wr(dhext_s, lo, dh)
            dh1, dh2, g0 = rd(dhext_s, 1, lo, w), rd(dhext_s, 2, lo, w), rd(gext_s, FH, lo, w)
            dcb_ref[:, cs] += jnp.sum(dh, axis=0, keepdims=True)
            dcw_ref[0:1, cs] += jnp.sum(dh2 * g0, axis=0, keepdims=True)
            dcw_ref[1:2, cs] += jnp.sum(dh1 * g0, axis=0, keepdims=True)
            dcw_ref[2:3, cs] += jnp.sum(dh * g0, axis=0, keepdims=True)
            dgate = cw_ref[2:3, cs] * dh + cw_ref[1:2, cs] * dh1 + cw_ref[0:1, cs] * dh2
            du_ref[:, cs] = dval.astype(bf16)
            du_ref[:, D_FF + lo: D_FF + lo + w] = dgate.astype(bf16)
        dhext_s[:, pl.ds(tt, 8), :] = dhext_s[:, pl.ds(0, 8), :]
        dx1 = ALPHA * dz2 + _dot(du_ref[...], wupt_ref[...])

        dg1_ref[...] += jnp.sum(dx1 * xhat, axis=0, keepdims=True)
        db1_ref[...] += jnp.sum(dx1, axis=0, keepdims=True)
        dxg = dx1 * g1
        dz1_ref[...] = rstd_ref[...] * (dxg - jnp.mean(dxg, axis=-1, keepdims=True)
                                        - xhat * jnp.mean(dxg * xhat, axis=-1, keepdims=True))

    rtile = lambda w: pl.BlockSpec((tt, w), lambda i: (n_tiles - 1 - i, 0))
    acc = lambda shape: pl.BlockSpec(shape, lambda i: (0, 0))
    out_shape = (
        jax.ShapeDtypeStruct((T, D), f32),
        jax.ShapeDtypeStruct((T, D), bf16),
        jax.ShapeDtypeStruct((T, 2 * D_FF), bf16),
        jax.ShapeDtypeStruct((T, D_FF), bf16),
        jax.ShapeDtypeStruct((8, 128), f32),
        jax.ShapeDtypeStruct((1, D), f32), jax.ShapeDtypeStruct((1, D), f32),
        jax.ShapeDtypeStruct((1, D), f32), jax.ShapeDtypeStruct((1, D), f32),
        jax.ShapeDtypeStruct((1, D_FF), f32), jax.ShapeDtypeStruct((3, D_FF), f32),
    )
    return pl.pallas_call(
        body, name="ffn_forward_backward", grid=(n_tiles,), out_shape=out_shape,
        in_specs=[rtile(D),
                  pl.BlockSpec((FH, D), lambda i: (jnp.maximum((n_tiles - 1 - i) * hb - 1, 0), 0)),
                  rtile(1), _const_spec((1, D)), _const_spec((1, D)), _const_spec((2 * D_FF, D)),
                  _const_spec((3, D_FF)), _const_spec((1, D_FF)), _const_spec((D_FF, D)),
                  _const_spec((1, D)), _const_spec((1, D)), rtile(D)],
        out_specs=(rtile(D), rtile(D), rtile(2 * D_FF), rtile(D_FF), acc((8, 128)),
                   acc((1, D)), acc((1, D)), acc((1, D)), acc((1, D)), acc((1, D_FF)), acc((3, D_FF))),
        scratch_shapes=[pltpu.VMEM((D_FF // 128, tt + FH, 128), f32), pltpu.VMEM((tt, D_FF), f32),
                        pltpu.VMEM((D_FF // 128, tt + 8, 128), f32)],
        compiler_params=pltpu.CompilerParams(dimension_semantics=("arbitrary",), vmem_limit_bytes=V7X_VMEM_LIMIT),
    )(xhat1, xhat1, rstd1, ln1_g, ln1_b, w_up_t, conv_w, conv_b, w_down, ln2_g, ln2_b, target)


def _mix_backward(dz1, w_out, qkv, g, oret, states, pooled, cat, cos, sin, dmat, qd, kd, cdec, w_pool, pool_scale, w_in_t,
                  exchange, tt=MIX_TILE):
    n_tiles = T // tt
    n_e = len(exchange)

    def body(dz1_ref, wout_ref, qkv_ref, g_ref, oret_ref, states_ref, pooled_ref, cat_ref, cos_ref, sin_ref, dmat_ref,
             qd_ref, kd_ref, wpool_ref, pscale_ref, wint_ref, *rest):
        ein, rest = rest[:n_e], rest[n_e:]
        dproj_ref, gx_ref, dwpool_ref, dpscale_ref, dwout_ref = rest[:5]
        eout, (dstate_s, dout_s, eext_s, tmp_s, dwout_s, *sems) = rest[5:5 + n_e], rest[5 + n_e:]
        i = pl.program_id(0)
        tile_idx = n_tiles - 1 - i

        @pl.when(i == 0)
        def _():
            dstate_s[...] = jnp.zeros_like(dstate_s)
            dwpool_ref[...] = jnp.zeros_like(dwpool_ref)
            dpscale_ref[...] = jnp.zeros_like(dpscale_ref)
            dwout_s[...] = jnp.zeros_like(dwout_s)
            eext_s[pl.ds(tt, HALO), :] = jnp.zeros((HALO, PW), f32)
            _chip_exchange_start(ein, eout, *sems)

        dz1 = dz1_ref[...]
        dz1b = dz1.astype(bf16)
        dcat = _dot(dz1b, wout_ref[...], NT)
        dwout_s[...] += _dot(cat_ref[...], dz1b, TN)

        pos1 = (tile_idx * tt + lax.broadcasted_iota(jnp.int32, (tt, 1), 0) + 1).astype(f32)
        for gi, w in enumerate(WINDOWS):
            sl = slice(gi * DH, (gi + 1) * DH)
            dpo = dcat[:, RW + gi * DH: RW + (gi + 1) * DH]
            pooled_g = pooled_ref[:, sl]
            ylin = _dot(pooled_g, wpool_ref[gi])
            dpscale_ref[:, sl] += jnp.sum(dpo * ylin, axis=0, keepdims=True)
            dpw = (dpo * pscale_ref[:, sl]).astype(bf16)
            dwpool_ref[gi] += _dot(pooled_g, dpw, TN)
            dpooled = _dot(dpw, wpool_ref[gi], NT)
            eext_s[pl.ds(0, tt), sl] = dpooled / jnp.minimum(pos1, float(w))
            stages = int(math.log2(w))
            src = eext_s
            for s in range(stages):
                n = tt + 8 * (stages - 1 - s)
                shift = 2 ** s
                val = src[pl.ds(0, n), sl] + src[pl.ds(shift, n), sl]
                if s == stages - 1:
                    wsum = val
                else:
                    tmp_s[pl.ds(0, n), sl] = val
                    src = tmp_s
            dproj_ref[:, 4 * RW + gi * DH: 4 * RW + (gi + 1) * DH] = (wsum - dpooled).astype(bf16)
        eext_s[pl.ds(tt, HALO), :] = eext_s[pl.ds(0, HALO), :]

        for h in range(HEADS):
            sl = slice(h * DH, (h + 1) * DH)
            dr = dcat[:, sl]
            o = oret_ref[:, sl]
            r = lax.rsqrt(jnp.mean(o * o, axis=-1, keepdims=True) + RMS_EPS)
            rn = o * r
            gg = g_ref[:, sl]
            sg = _sigmoid(gg)
            dproj_ref[:, 3 * RW + h * DH: 3 * RW + (h + 1) * DH] = (dr * rn * (sg * (1.0 + gg * (1.0 - sg)))).astype(bf16)
            drn = dr * (gg * sg)
            dout_s[:, sl] = (r * (drn - rn * jnp.mean(drn * rn, axis=-1, keepdims=True))).astype(bf16)

        for sub in reversed(range(tt // RET_TILE)):
            rows = pl.ds(sub * RET_TILE, RET_TILE)
            cos_t, sin_t = cos_ref[rows, :], sin_ref[rows, :]
            for h in range(HEADS):
                q = qkv_ref[rows, h * DH:(h + 1) * DH]
                k = qkv_ref[rows, RW + h * DH: RW + (h + 1) * DH]
                v = qkv_ref[rows, 2 * RW + h * DH: 2 * RW + (h + 1) * DH]
                do = dout_s[rows, h * DH:(h + 1) * DH]
                stb = states_ref[sub, h]
                dst = dstate_s[h]
                dstb = dst.astype(bf16)
                sb = (_dot(q, k, NT) * dmat_ref[h]).astype(bf16)
                dsb = (_dot(do, v, NT) * dmat_ref[h]).astype(bf16)
                dq = _dot(dsb, k) + _dot(do, stb, NT) * qd_ref[h]
                dk = _dot(dsb, q, TN) + _dot(v, dstb, NT) * kd_ref[h]
                dv = _dot(sb, do, TN) + _dot((k.astype(f32) * kd_ref[h]).astype(bf16), dstb)
                dstate_s[h] = dst * cdec[h] + _dot((q.astype(f32) * qd_ref[h]).astype(bf16), do, TN)
                dproj_ref[rows, h * DH:(h + 1) * DH] = (dq * cos_t - _swap_halves(dq) * sin_t).astype(bf16)
                dproj_ref[rows, RW + h * DH: RW + (h + 1) * DH] = (
                    (dk * cos_t - _swap_halves(dk) * sin_t) * K_SCALE).astype(bf16)
                dproj_ref[rows, 2 * RW + h * DH: 2 * RW + (h + 1) * DH] = dv.astype(bf16)

        gx_ref[...] = ALPHA * dz1 + _dot(dproj_ref[...], wint_ref[...])

        @pl.when(i == n_tiles - 1)
        def _():
            dwout_ref[...] = dwout_s[...].astype(bf16)
            _chip_exchange_finish(ein, eout, *sems)

    rtile = lambda w: pl.BlockSpec((tt, w), lambda i: (n_tiles - 1 - i, 0))
    hbm = pl.BlockSpec(memory_space=pltpu.HBM)
    out_shape = (
        jax.ShapeDtypeStruct((T, IN_W), bf16),
        jax.ShapeDtypeStruct((T, D), f32),
        jax.ShapeDtypeStruct((GROUPS, DH, DH), f32),
        jax.ShapeDtypeStruct((1, PW), f32),
        jax.ShapeDtypeStruct((D, D), bf16),
    ) + tuple(jax.ShapeDtypeStruct(e.shape, e.dtype) for e in exchange)
    return pl.pallas_call(
        body, name="mix_backward", grid=(n_tiles,), out_shape=out_shape,
        in_specs=[rtile(D), _const_spec((D, D)), rtile(3 * RW), rtile(RW), rtile(RW),
                  pl.BlockSpec((tt // RET_TILE, HEADS, DH, DH), lambda i: (n_tiles - 1 - i, 0, 0, 0)),
                  rtile(PW), rtile(D), rtile(DH), rtile(DH),
                  _const_spec((HEADS, RET_TILE, RET_TILE)), _const_spec((HEADS, RET_TILE, DH)),
                  _const_spec((HEADS, RET_TILE, DH)),
                  _const_spec((GROUPS, DH, DH)), _const_spec((1, PW)), _const_spec((IN_W, D))] + [hbm] * n_e,
        out_specs=(rtile(IN_W), rtile(D), pl.BlockSpec((GROUPS, DH, DH), lambda i: (0, 0, 0)),
                   pl.BlockSpec((1, PW), lambda i: (0, 0)),
                   pl.BlockSpec((D, D), lambda i: (0, 0), pipeline_mode=pl.Buffered(1))) + (hbm,) * n_e,
        scratch_shapes=[pltpu.VMEM((HEADS, DH, DH), f32), pltpu.VMEM((tt, RW), bf16),
                        pltpu.VMEM((tt + HALO, PW), f32), pltpu.VMEM((tt + HALO, PW), f32),
                        pltpu.VMEM((D, D), f32)] + _chip_exchange_sems(n_e),
        compiler_params=pltpu.CompilerParams(dimension_semantics=("arbitrary",), vmem_limit_bytes=V7X_VMEM_LIMIT,
                                             collective_id=CHIP_BARRIER),
    )(dz1, w_out, qkv, g, oret, states, pooled, cat, cos, sin, dmat, qd, kd, w_pool, pool_scale, w_in_t, *exchange)


def _weight_grad(a, b, name, tm, exchange=(), tk=2048):
    m = a.shape[1]
    n_m, n_k, n_e = m // tm, T // tk, len(exchange)

    def body(a_ref, b_ref, *rest):
        ein, o_ref, eout, (acc_s, *sems) = rest[:n_e], rest[n_e], rest[n_e + 1:2 * n_e + 1], rest[2 * n_e + 1:]
        i, k = pl.program_id(0), pl.program_id(1)

        if n_e:
            @pl.when((i == 0) & (k == 0))
            def _():
                _chip_exchange_start(ein, eout, *sems)

        @pl.when(k == 0)
        def _():
            acc_s[...] = jnp.zeros_like(acc_s)

        acc_s[...] += _dot(a_ref[...], b_ref[pl.ds(pl.multiple_of(k * tk, tk), tk), :].astype(bf16), TN)

        @pl.when(k == n_k - 1)
        def _():
            o_ref[...] = acc_s[...].astype(bf16)

        if n_e:
            @pl.when((i == n_m - 1) & (k == n_k - 1))
            def _():
                _chip_exchange_finish(ein, eout, *sems)

    hbm = pl.BlockSpec(memory_space=pltpu.HBM)
    return pl.pallas_call(
        body, name=name, grid=(n_m, n_k),
        out_shape=(jax.ShapeDtypeStruct((m, D), bf16),) + tuple(jax.ShapeDtypeStruct(e.shape, e.dtype) for e in exchange),
        in_specs=[pl.BlockSpec((tk, tm), lambda i, k: (k, i)),
                  pl.BlockSpec((T, D), lambda i, k: (0, 0), pipeline_mode=pl.Buffered(1))] + [hbm] * n_e,
        out_specs=(pl.BlockSpec((tm, D), lambda i, k: (i, 0)),) + (hbm,) * n_e,
        scratch_shapes=[pltpu.VMEM((tm, D), f32)] + _chip_exchange_sems(n_e),
        compiler_params=pltpu.CompilerParams(dimension_semantics=("arbitrary", "arbitrary"),
                                             vmem_limit_bytes=V7X_VMEM_LIMIT,
                                             collective_id=CHIP_BARRIER if n_e else None),
    )(a, b, *exchange)


CHIP_FLIPS = ((1, 0), (0, 1), (1, 1))
PAIR_BARRIER, CHIP_BARRIER, GATHER_BARRIER = 0, 1, 2


def _barrier(peers):
    sem = pltpu.get_barrier_semaphore()
    for peer in peers:
        pl.semaphore_signal(sem, inc=1, device_id=peer, device_id_type=pl.DeviceIdType.MESH)
    pl.semaphore_wait(sem, len(peers))


def _me():
    return lax.axis_index("x"), lax.axis_index("y"), lax.axis_index("c")


def _chip(me, k):
    x, y, _ = me
    if k == 0:
        return x, y
    fx, fy = CHIP_FLIPS[k - 1]
    return (1 - x if fx else x), (1 - y if fy else y)


def _slot(x, y, c):
    return 4 * x + 2 * y + c


def _remote(src, dst, send_sem, recv_sem, to):
    return pltpu.make_async_remote_copy(src_ref=src, dst_ref=dst, send_sem=send_sem, recv_sem=recv_sem,
                                        device_id=to, device_id_type=pl.DeviceIdType.MESH)


def _gather_sems(n):
    return [pltpu.SemaphoreType.DMA((7, n)), pltpu.SemaphoreType.DMA((7, n)), pltpu.SemaphoreType.DMA((n,))] if n else []


def _gather_copy(k, j, gin, gout, send_sems, recv_sems, sending):
    x, y, c = _me()
    sibling, x_chip, y_chip, d_chip = (x, y, 1 - c), (1 - x, y), (x, 1 - y), (1 - x, 1 - y)
    south = c == 0
    passed_on = (jnp.where(south, 1 - x, x), jnp.where(south, y, 1 - y), c)
    src, to = gin[j], sibling
    if sending:
        block = {0: (x, y, c), 1: (x, y, c), 2: (x, y, c), 3: passed_on, 4: (*x_chip, c), 5: (*y_chip, c), 6: (*d_chip, c)}[k]
        to = {1: (*x_chip, c), 2: (*y_chip, c), 3: (jnp.where(south, x, 1 - x), jnp.where(south, 1 - y, y), c)}.get(k, sibling)
        if k >= 3:
            src = gout[j].at[_slot(*block)]
    else:
        block = {0: sibling, 1: (*x_chip, c), 2: (*y_chip, c), 3: (*d_chip, c), 4: (*x_chip, 1 - c), 5: (*y_chip, 1 - c),
                 6: (*d_chip, 1 - c)}[k]
    return _remote(src, gout[j].at[_slot(*block)], send_sems.at[k, j], recv_sems.at[k, j], to)


def _gather_do(ks, action, gin, gout, send_sems, recv_sems):
    for k in ks:
        for j in range(len(gin)):
            cp = _gather_copy(k, j, gin, gout, send_sems, recv_sems, action != "wait_recv")
            getattr(cp, action)()


def _gather_peers():
    x, y, c = _me()
    return [(x, y, 1 - c), (1 - x, y, c), (x, 1 - y, c)]


def _gather_start(gin, gout, send_sems, recv_sems, local_sems, barrier=True):
    if barrier:
        _barrier(_gather_peers())
    for j in range(len(gin)):
        pltpu.make_async_copy(gin[j], gout[j].at[_slot(*_me())], local_sems.at[j]).start()
    _gather_do((0, 1, 2), "start", gin, gout, send_sems, recv_sems)


def _gather_forward(gin, gout, send_sems, recv_sems, local_sems):
    _gather_do((1, 2), "wait_recv", gin, gout, send_sems, recv_sems)
    _gather_do((3, 4, 5), "start", gin, gout, send_sems, recv_sems)


def _gather_finish(gin, gout, send_sems, recv_sems, local_sems):
    _gather_do((3,), "wait_recv", gin, gout, send_sems, recv_sems)
    _gather_do((6,), "start", gin, gout, send_sems, recv_sems)
    _gather_do((0, 4, 5, 6), "wait_recv", gin, gout, send_sems, recv_sems)
    _gather_do(range(7), "wait_send", gin, gout, send_sems, recv_sems)
    for j in range(len(gin)):
        pltpu.make_async_copy(gin[j], gout[j].at[_slot(*_me())], local_sems.at[j]).wait()


def _all_gather(blocks, name):
    n = len(blocks)

    def body(*refs):
        gin, gout, sems = refs[:n], refs[n:2 * n], refs[2 * n:]
        _gather_start(gin, gout, *sems)
        _gather_forward(gin, gout, *sems)
        _gather_finish(gin, gout, *sems)

    hbm = pl.BlockSpec(memory_space=pltpu.HBM)
    return pl.pallas_call(
        body, name=name,
        out_shape=tuple(jax.ShapeDtypeStruct((N_DEV,) + b.shape, b.dtype) for b in blocks),
        in_specs=[hbm] * n, out_specs=(hbm,) * n, scratch_shapes=_gather_sems(n),
        compiler_params=pltpu.CompilerParams(collective_id=GATHER_BARRIER),
    )(*blocks)


def _pair_reduce(parts, name):
    n = len(parts)

    def body(*refs):
        ins, own, others, landing, mine = (refs[k * n:(k + 1) * n] for k in range(5))
        send_sems, recv_sems, local_sems = refs[5 * n:]
        me = _me()
        x, y, c = me
        sibling = (x, y, 1 - c)
        _barrier([sibling])
        sends, loads = [], []
        for k in range(4):
            for j in range(n):
                cp = _remote(ins[j].at[_slot(*_chip(me, k), 1 - c)], landing[j].at[k], send_sems.at[k, j],
                             recv_sems.at[k, j], sibling)
                cp.start()
                sends.append(cp)
                ld = pltpu.make_async_copy(ins[j].at[_slot(*_chip(me, k), c)], mine[j].at[k], local_sems.at[k, j])
                ld.start()
                loads.append(ld)
        for k in range(4):
            for j in range(n):
                loads[k * n + j].wait()
                _remote(ins[j].at[0], landing[j].at[k], send_sems.at[k, j], recv_sems.at[k, j], sibling).wait_recv()
                total = mine[j][k].astype(f32) + landing[j][k].astype(f32)
                if k == 0:
                    own[j][...] = total.astype(own[j].dtype)
                else:
                    others[j][k - 1] = total.astype(others[j].dtype)
        for cp in sends:
            cp.wait_send()

    vm = pl.BlockSpec(memory_space=pltpu.VMEM)
    return pl.pallas_call(
        body, name=name,
        out_shape=tuple(jax.ShapeDtypeStruct(p.shape[1:], p.dtype) for p in parts)
        + tuple(jax.ShapeDtypeStruct((3,) + p.shape[1:], p.dtype) for p in parts),
        in_specs=[pl.BlockSpec(memory_space=pltpu.HBM)] * n, out_specs=(vm,) * (2 * n),
        scratch_shapes=[pltpu.VMEM((4,) + p.shape[1:], p.dtype) for p in parts] * 2
        + [pltpu.SemaphoreType.DMA((4, n)), pltpu.SemaphoreType.DMA((4, n)), pltpu.SemaphoreType.DMA((4, n))],
        compiler_params=pltpu.CompilerParams(vmem_limit_bytes=V7X_VMEM_LIMIT, collective_id=PAIR_BARRIER),
    )(*parts)


def _chip_exchange_sems(n):
    return [pltpu.SemaphoreType.DMA((3, n)), pltpu.SemaphoreType.DMA((3, n))] if n else []


def _chip_exchange_copy(k, j, ein, eout, send_sems, recv_sems):
    me = _me()
    return _remote(ein[j].at[k - 1], eout[j].at[k - 1], send_sems.at[k - 1, j], recv_sems.at[k - 1, j],
                   (*_chip(me, k), me[2]))


def _chip_peers():
    me = _me()
    return [(*_chip(me, k), me[2]) for k in range(1, 4)]


def _chip_exchange_start(ein, eout, send_sems, recv_sems, barrier=True):
    if barrier:
        _barrier(_chip_peers())
    for k in range(1, 4):
        for j in range(len(ein)):
            _chip_exchange_copy(k, j, ein, eout, send_sems, recv_sems).start()


def _chip_exchange_finish(ein, eout, send_sems, recv_sems):
    for k in range(1, 4):
        for j in range(len(ein)):
            _chip_exchange_copy(k, j, ein, eout, send_sems, recv_sems).wait_recv()
    for k in range(1, 4):
        for j in range(len(ein)):
            _chip_exchange_copy(k, j, ein, eout, send_sems, recv_sems).wait_send()


def _split_copies(src_ref, dst_ref, sems):
    me = _me()
    return [_remote(src_ref.at[k - 1], dst_ref.at[k - 1], sems[k - 1], sems[2 + k], (*_chip(me, k), me[2]))
            for k in range(1, 4)]


def _exchange_start(others, name):
    def body(src_ref, land_ref, *rest):
        sems, token_ref = rest[:6], rest[8]
        _barrier(_chip_peers())
        for copy in _split_copies(src_ref, land_ref, sems):
            copy.start()
        token_ref[...] = jnp.zeros_like(token_ref)

    hbm, sem = pl.BlockSpec(memory_space=pltpu.HBM), pl.BlockSpec(memory_space=pltpu.SEMAPHORE)
    thru = pltpu.HBM(others.shape, others.dtype)
    res = pl.pallas_call(
        body, name=name,
        out_shape=(pltpu.SemaphoreType.DMA(()),) * 6 + (thru, thru, jax.ShapeDtypeStruct((8, 128), f32)),
        in_specs=(hbm, hbm), out_specs=(sem,) * 6 + (hbm, hbm, pl.BlockSpec(memory_space=pltpu.VMEM)),
        input_output_aliases={0: 6, 1: 7},
        compiler_params=pltpu.CompilerParams(has_side_effects=pltpu.SideEffectType.DATAFLOW_SIDE_EFFECTING,
                                             collective_id=CHIP_BARRIER),
    )(pltpu.with_memory_space_constraint(others, pltpu.HBM),
      pltpu.with_memory_space_constraint(lax.empty(others.shape, others.dtype), pltpu.HBM))
    return res[:6], res[6], res[7], res[8]


def _exchange_wait(sems, src_thru, land_thru, after, name):
    n_after = len(after)

    def body(src_ref, land_ref, *rest):
        for copy in _split_copies(src_ref, land_ref, rest[:6]):
            copy.wait_send()
            copy.wait_recv()

    hbm, sem = pl.BlockSpec(memory_space=pltpu.HBM), pl.BlockSpec(memory_space=pltpu.SEMAPHORE)
    thru = pltpu.HBM(src_thru.shape, src_thru.dtype)
    return pl.pallas_call(
        body, name=name, out_shape=(thru, thru),
        in_specs=(hbm, hbm) + (sem,) * 6 + (pl.BlockSpec(memory_space=pl.ANY),) * n_after, out_specs=(hbm, hbm),
        input_output_aliases={0: 0, 1: 1},
        compiler_params=pltpu.CompilerParams(has_side_effects=pltpu.SideEffectType.DATAFLOW_SIDE_EFFECTING),
    )(src_thru, land_thru, *sems, *after)[1]


def _sum_parts(owns, arrived, name):
    n = len(owns)

    def body(*refs):
        for own, arr, out in zip(refs[:n], refs[n:2 * n], refs[2 * n:]):
            acc = own[...].astype(f32)
            for k in range(3):
                acc = acc + arr[k].astype(f32)
            out[...] = acc

    vm = pl.BlockSpec(memory_space=pltpu.VMEM)
    return pl.pallas_call(
        body, name=name, out_shape=tuple(jax.ShapeDtypeStruct(o.shape, f32) for o in owns),
        in_specs=[vm] * (2 * n), out_specs=(vm,) * n,
        compiler_params=pltpu.CompilerParams(vmem_limit_bytes=V7X_VMEM_LIMIT),
    )(*owns, *arrived)


ADAM_C1 = 1.0 / (1.0 - ADAM_B1 ** ADAM_STEP)
ADAM_C2 = 1.0 / (1.0 - ADAM_B2 ** ADAM_STEP)


def _adam_update(w, g, m, v):
    m = ADAM_B1 * m + (1.0 - ADAM_B1) * g
    v = ADAM_B2 * v + (1.0 - ADAM_B2) * (g * g)
    return -ADAM_LR * ((m * ADAM_C1) / (jnp.sqrt(v * ADAM_C2) + ADAM_EPS) + ADAM_WD * w), m, v


def _sum_adamw(own, arrived, w, m, v, name, steps, after=()):
    rows = own.shape[0]
    br = rows // steps

    def body(own_ref, arr_ref, w_ref, m_ref, v_ref, *rest):
        g_out, d_out, m_out, v_out = rest[len(after):]
        g = own_ref[...].astype(f32)
        for k in range(3):
            g = g + arr_ref[k].astype(f32)
        g_out[...] = g
        d_out[...], m_out[...], v_out[...] = _adam_update(w_ref[...], g, m_ref[...], v_ref[...])

    blk = pl.BlockSpec((br, D), lambda i: (i, 0))
    return pl.pallas_call(
        body, name=name, grid=(steps,), out_shape=(jax.ShapeDtypeStruct((rows, D), f32),) * 4,
        in_specs=[blk, pl.BlockSpec((3, br, D), lambda i: (0, i, 0)), blk, blk, blk]
        + [pl.BlockSpec(memory_space=pl.ANY)] * len(after), out_specs=(blk,) * 4,
        compiler_params=pltpu.CompilerParams(dimension_semantics=("parallel",), vmem_limit_bytes=V7X_VMEM_LIMIT),
    )(own, arrived, w, m, v, *after)


def _adamw(ws, gs, ms, vs, name):
    n = len(ws)

    def body(*refs):
        w_r, g_r, m_r, v_r = (refs[k * n:(k + 1) * n] for k in range(4))
        d_o, m_o, v_o = (refs[(4 + k) * n:(5 + k) * n] for k in range(3))
        for j in range(n):
            d_o[j][...], m_o[j][...], v_o[j][...] = _adam_update(w_r[j][...], g_r[j][...], m_r[j][...], v_r[j][...])

    vm = pl.BlockSpec(memory_space=pltpu.VMEM)
    shapes = tuple(jax.ShapeDtypeStruct(w.shape, f32) for w in ws)
    return pl.pallas_call(
        body, name=name, out_shape=shapes * 3, in_specs=[vm] * (4 * n), out_specs=tuple([vm] * (3 * n)),
        compiler_params=pltpu.CompilerParams(vmem_limit_bytes=V7X_VMEM_LIMIT),
    )(*ws, *gs, *ms, *vs)


SMALL = (("w_pool", GROUPS * DH * DH), ("pool_scale", PW), ("ln1_g", D), ("ln1_b", D), ("conv_b", D_FF),
         ("ln2_g", D), ("ln2_b", D), ("conv_w", 3 * D_FF), ("loss", 1))
SMALL_ROWS = 640


def _pack(named):
    flat = jnp.concatenate([named[k].reshape(-1) for k, _ in SMALL])
    return jnp.pad(flat, (0, SMALL_ROWS * 128 - flat.shape[0])).reshape(SMALL_ROWS, 128)


def _unpack(packed):
    flat, out, at = packed.reshape(-1), {}, 0
    for k, size in SMALL:
        out[k] = flat[at:at + size]
        at += size
    return out


def kernel(x, w_in, w_pool, pool_scale, w_out, ln1_g, ln1_b, w_up, conv_w, conv_b, w_down, ln2_g, ln2_b, loss_target, m_w_in, m_w_pool, m_pool_scale, m_w_out, m_ln1_g, m_ln1_b, m_w_up, m_conv_w, m_conv_b, m_w_down, m_ln2_g, m_ln2_b, v_w_in, v_w_pool, v_pool_scale, v_w_out, v_ln1_g, v_ln1_b, v_w_up, v_conv_w, v_conv_b, v_w_down, v_ln2_g, v_ln2_b):
    me = 4 * lax.axis_index("x") + 2 * lax.axis_index("y") + lax.axis_index("c")
    x2, tgt = x[0], loss_target[0]

    g_in, g_out, g_cw = _all_gather([w_in[0].T.astype(bf16), w_out[0].astype(bf16), jnp.transpose(conv_w, (1, 0, 2))],
                                    "gather_weights")
    w_in_t = g_in.reshape(IN_W, D)
    w_out_f = g_out.reshape(D, D)
    conv_w_f = jnp.transpose(g_cw[:, :, 0, :], (1, 0, 2)).reshape(3, D_FF)
    w_pool_b = w_pool[0].astype(bf16)

    cos, sin = _rope_tables()
    dmat, qd, kd, cdec = _decay_tables(RET_TILE)

    qkv, g, oret, states, cat, pooled, xhat1, rstd1, x1b, g_up, g_down = _mix_forward(
        x2, w_in_t, cos, sin, dmat, qd, kd, cdec, w_pool_b, pool_scale, w_out_f, ln1_g, ln1_b,
        gather=[w_up[0].T.astype(bf16), w_down[0].astype(bf16)])
    w_up_t = g_up.reshape(2 * D_FF, D)
    w_down_f = g_down.reshape(D_FF, D)
    dz1, dz2b, du, f, loss8, d_ln2_g, d_ln2_b, d_ln1_g, d_ln1_b, d_conv_b, d_conv_w = _ffn_forward_backward(
        xhat1, rstd1, ln1_g, ln1_b, w_up_t, conv_w_f, conv_b, w_down_f, ln2_g, ln2_b, tgt)

    (dw_down,) = _weight_grad(f, dz2b, "grad_w_down", tm=D_FF // 2)
    own_down, oth_down = _pair_reduce([dw_down.reshape(N_DEV, ROWS_DOWN, D)], "pair_reduce_down")
    dw_up_t, arr_down = _weight_grad(du, x1b, "grad_w_up", tm=D_FF // 2, exchange=[oth_down])
    own_up, oth_up = _pair_reduce([dw_up_t.reshape(N_DEV, ROWS_UP, D)], "pair_reduce_up")
    dproj, grad_x, d_w_pool, d_pool_scale, dw_out, arr_up = _mix_backward(
        dz1, w_out_f, qkv, g, oret, states, pooled, cat, cos, sin, dmat, qd, kd, cdec, w_pool_b, pool_scale, w_in_t,
        exchange=[oth_up])
    small = _pack({"w_pool": d_w_pool, "pool_scale": d_pool_scale, "ln1_g": d_ln1_g, "ln1_b": d_ln1_b,
                   "conv_b": d_conv_b, "ln2_g": d_ln2_g, "ln2_b": d_ln2_b, "conv_w": d_conv_w, "loss": loss8[0, :1]})
    own_out, own_small, oth_out, oth_small = _pair_reduce(
        [dw_out.reshape(N_DEV, ROWS_OUT, D), small.reshape(N_DEV, SMALL_ROWS // N_DEV, 128)], "pair_reduce_out")
    dw_in_t, arr_out, arr_small = _weight_grad(dproj, x2, "grad_w_in", tm=IN_W // 2, exchange=[oth_out, oth_small])
    own_in, oth_in = _pair_reduce([dw_in_t.reshape(N_DEV, ROWS_IN, D)], "pair_reduce_in")
    in_sems, in_src, in_land, started = _exchange_start(oth_in, "exchange_in_start")
    (small_piece,) = _sum_parts([own_small], [arr_small], "sum_small_grads")
    (gs_small,) = _all_gather([small_piece], "gather_small_grads")

    names = ["w_in", "w_pool", "pool_scale", "w_out", "ln1_g", "ln1_b", "w_up", "conv_w", "conv_b", "w_down",
             "ln2_g", "ln2_b"]
    w_d = dict(w_in=w_in, w_pool=w_pool, pool_scale=pool_scale, w_out=w_out, ln1_g=ln1_g, ln1_b=ln1_b, w_up=w_up,
               conv_w=conv_w, conv_b=conv_b, w_down=w_down, ln2_g=ln2_g, ln2_b=ln2_b)
    m_d = dict(w_in=m_w_in, w_pool=m_w_pool, pool_scale=m_pool_scale, w_out=m_w_out, ln1_g=m_ln1_g, ln1_b=m_ln1_b,
               w_up=m_w_up, conv_w=m_conv_w, conv_b=m_conv_b, w_down=m_w_down, ln2_g=m_ln2_g, ln2_b=m_ln2_b)
    v_d = dict(w_in=v_w_in, w_pool=v_w_pool, pool_scale=v_pool_scale, w_out=v_w_out, ln1_g=v_ln1_g, ln1_b=v_ln1_b,
               w_up=v_w_up, conv_w=v_conv_w, conv_b=v_conv_b, w_down=v_w_down, ln2_g=v_ln2_g, ln2_b=v_ln2_b)
    g_d, delta, new_m, new_v = {}, {}, {}, {}

    def big_adamw(k, own, arr, transposed, steps, after=()):
        lay = (lambda a: a[0].T) if transposed else (lambda a: a[0])
        back = (lambda a: a.T[None]) if transposed else (lambda a: a[None])
        res = _sum_adamw(own, arr, lay(w_d[k]), lay(m_d[k]), lay(v_d[k]), "adamw_" + k, steps, after)
        g_d[k], delta[k], new_m[k], new_v[k] = (back(r) for r in res)
        return res[3]

    done = [big_adamw("w_up", own_up, arr_up, True, 4, after=(started,)),
            big_adamw("w_down", own_down, arr_down, False, 2, after=(started,)),
            big_adamw("w_out", own_out, arr_out, False, 2, after=(started,))]

    gsm = _unpack(gs_small)
    gsm["conv_w"] = lax.dynamic_slice(gsm["conv_w"].reshape(3, D_FF), (0, me * (D_FF // N_DEV)), (3, D_FF // N_DEV))
    lay = lambda k, a: jnp.transpose(a, (1, 0, 2)) if k == "conv_w" else a.reshape(-1, a.shape[-1])
    back = lambda k, a: jnp.transpose(a, (1, 0, 2)) if k == "conv_w" else a.reshape(w_d[k].shape)
    group = [k for k in names if k not in ("w_in", "w_out", "w_up", "w_down")]
    for k in group:
        g_d[k] = gsm[k].reshape(w_d[k].shape)
    res = _adamw([lay(k, w_d[k]) for k in group], [lay(k, g_d[k]) for k in group], [lay(k, m_d[k]) for k in group],
                 [lay(k, v_d[k]) for k in group], "adamw_small")
    for j, k in enumerate(group):
        delta[k], new_m[k], new_v[k] = (back(k, res[part * len(group) + j]) for part in range(3))

    arr_in = _exchange_wait(in_sems, in_src, in_land, done + [res[0]], "exchange_in_wait")
    big_adamw("w_in", own_in, arr_in, True, 4)

    loss = gsm["loss"].reshape(())
    return (loss, grad_x[None], *[g_d[k] for k in names], *[delta[k] for k in names], *[new_m[k] for k in names],
            *[new_v[k] for k in names])
```

```python
import functools
import math

import numpy as np
import jax
import jax.numpy as jnp
from jax import lax
from jax.experimental import pallas as pl
from jax.experimental.pallas import tpu as pltpu

f32 = jnp.float32
bf16 = jnp.bfloat16

N_DEV = 8
T = 4096
D = 1024
CHUNK = 64
MIX_TILE = 512
RET_TILE = 256
HEADS = 4
DH = 128
RW = HEADS * DH
PW = 512
GROUPS = 4
WINDOWS = (2, 4, 8, 16)
IN_W = 4 * RW + PW
D_FF = 2816
LN_EPS = 1e-5
RMS_EPS = 1e-6
ALPHA = 2.0 ** 0.25
K_SCALE = DH ** -0.5

ADAM_LR = 0.001
ADAM_B1 = 0.9
ADAM_B2 = 0.999
ADAM_EPS = 1e-08
ADAM_WD = 0.01
ADAM_STEP = 10

ROWS_IN, ROWS_OUT, ROWS_UP, ROWS_DOWN = IN_W // N_DEV, D // N_DEV, 2 * D_FF // N_DEV, D_FF // N_DEV

V7X_VMEM_LIMIT = 56 * 2 ** 20
HALO = 32

NT = (((1,), (1,)), ((), ()))
TN = (((0,), (0,)), ((), ()))
NN = (((1,), (0,)), ((), ()))


def _dot(a, b, dims=NN):
    return lax.dot_general(a, b, dims, preferred_element_type=f32)


def _const_spec(shape):
    zeros = (0,) * len(shape)
    return pl.BlockSpec(shape, lambda i: zeros, pipeline_mode=pl.Buffered(1))


def _sigmoid(x):
    return 0.5 * jnp.tanh(0.5 * x) + 0.5


def _decay_tables(tt):
    h = np.arange(HEADS, dtype=np.float64)
    log_gamma = np.log(1.0 - 2.0 ** (-5.0 - h)).astype(np.float32).astype(np.float64)[:, None, None]
    idx = np.arange(tt, dtype=np.float64)
    visible = (idx[None, :] // CHUNK) <= (idx[:, None] // CHUNK)
    mask = np.where(visible[None], np.exp(log_gamma * np.abs(idx[:, None] - idx[None, :])[None]), 0.0)
    qd = np.broadcast_to(np.exp(log_gamma * (idx[None, :, None] + 1.0)), (HEADS, tt, DH))
    kd = np.broadcast_to(np.exp(log_gamma * (tt - 1.0 - idx[None, :, None])), (HEADS, tt, DH))
    cd = np.exp(log_gamma[:, 0, 0] * tt)
    return (jnp.asarray(mask, f32), jnp.asarray(qd, f32), jnp.asarray(kd, f32), [float(c) for c in cd])


def _rope_tables():
    inv_freq = (10000.0 ** (-np.arange(0, DH, 2, dtype=np.float64) / DH)).astype(np.float32)
    ang = (np.arange(T, dtype=np.float32)[:, None] * inv_freq[None, :]).astype(np.float64)
    cos, sin = np.cos(ang), np.sin(ang)
    return (jnp.asarray(np.concatenate([cos, cos], axis=1), f32), jnp.asarray(np.concatenate([-sin, sin], axis=1), f32))


def _swap_halves(t):
    return pltpu.roll(t, DH // 2, axis=1)


def _mix_forward(x, w_in_t, cos, sin, dmat, qd, kd, cdec, w_pool, pool_scale, w_out, ln1_g, ln1_b, gather,
                 tt=MIX_TILE):
    n_tiles = T // tt
    n_g = len(gather)

    def body(x_ref, wint_ref, cos_ref, sin_ref, dmat_ref, qd_ref, kd_ref, wpool_ref, pscale_ref, wout_ref,
             g1_ref, b1_ref, *rest):
        gin, rest = rest[:n_g], rest[n_g:]
        qkv_ref, g_ref, oret_ref, states_ref, cat_ref, pooled_ref, xhat_ref, rstd_ref, x1b_ref = rest[:9]
        gout, (state_s, pext_s, tmp_s, *sems) = rest[9:9 + n_g], rest[9 + n_g:]
        i = pl.program_id(0)

        @pl.when(i == 0)
        def _():
            state_s[...] = jnp.zeros_like(state_s)
            pext_s[pl.ds(0, HALO), :] = jnp.zeros((HALO, PW), f32)
            _gather_start(gin, gout, *sems)

        @pl.when(i == n_tiles - 2)
        def _():
            _gather_forward(gin, gout, *sems)

        xb = x_ref[...].astype(bf16)
        cos_t, sin_t = cos_ref[...], sin_ref[...]
        for part in range(2):
            pr = _dot(xb, wint_ref[pl.ds(part * RW, RW), :], NT)
            for h in range(HEADS):
                t = pr[:, h * DH:(h + 1) * DH]
                r = t * cos_t + _swap_halves(t) * sin_t
                if part == 1:
                    r = r * K_SCALE
                qkv_ref[:, part * RW + h * DH: part * RW + (h + 1) * DH] = r.astype(bf16)
        qkv_ref[:, 2 * RW:3 * RW] = _dot(xb, wint_ref[pl.ds(2 * RW, RW), :], NT).astype(bf16)
        g_ref[...] = _dot(xb, wint_ref[pl.ds(3 * RW, RW), :], NT)
        pext_s[pl.ds(HALO, tt), :] = _dot(xb, wint_ref[pl.ds(4 * RW, PW), :], NT)

        for sub in range(tt // RET_TILE):
            rows = pl.ds(sub * RET_TILE, RET_TILE)
            for h in range(HEADS):
                q = qkv_ref[rows, h * DH:(h + 1) * DH]
                k = qkv_ref[rows, RW + h * DH: RW + (h + 1) * DH]
                v = qkv_ref[rows, 2 * RW + h * DH: 2 * RW + (h + 1) * DH]
                s = _dot(q, k, NT) * dmat_ref[h]
                st = state_s[h]
                stb = st.astype(bf16)
                states_ref[sub, h] = stb
                oret_ref[rows, h * DH:(h + 1) * DH] = (_dot(s.astype(bf16), v)
                                                      + _dot((q.astype(f32) * qd_ref[h]).astype(bf16), stb))
                state_s[h] = st * cdec[h] + _dot((k.astype(f32) * kd_ref[h]).astype(bf16), v, TN)

        for h in range(HEADS):
            sl = slice(h * DH, (h + 1) * DH)
            o = oret_ref[:, sl]
            r = lax.rsqrt(jnp.mean(o * o, axis=-1, keepdims=True) + RMS_EPS)
            gg = g_ref[:, sl]
            cat_ref[:, sl] = (o * r * (gg * _sigmoid(gg))).astype(bf16)

        pos1 = (i * tt + lax.broadcasted_iota(jnp.int32, (tt, 1), 0) + 1).astype(f32)
        for gi, w in enumerate(WINDOWS):
            sl = slice(gi * DH, (gi + 1) * DH)
            stages = int(math.log2(w))
            src = pext_s
            for s in range(stages):
                lo = HALO - 8 * (stages - 1 - s)
                n = tt + HALO - lo
                shift = 2 ** s
                val = src[pl.ds(lo, n), sl] + src[pl.ds(lo - shift, n), sl]
                if s == stages - 1:
                    wsum = val
                else:
                    tmp_s[pl.ds(lo, n), sl] = val
                    src = tmp_s
            p_g = pext_s[pl.ds(HALO, tt), sl]
            pooled = (wsum / jnp.minimum(pos1, float(w)) - p_g).astype(bf16)
            pooled_ref[:, sl] = pooled
            y = _dot(pooled, wpool_ref[gi]) * pscale_ref[:, sl]
            cat_ref[:, RW + gi * DH: RW + (gi + 1) * DH] = y.astype(bf16)
        pext_s[pl.ds(0, HALO), :] = pext_s[pl.ds(tt, HALO), :]

        z = ALPHA * x_ref[...] + _dot(cat_ref[...], wout_ref[...])
        mu = jnp.mean(z, axis=-1, keepdims=True)
        zc = z - mu
        rstd = lax.rsqrt(jnp.mean(zc * zc, axis=-1, keepdims=True) + LN_EPS)
        xhat = zc * rstd
        xhat_ref[...] = xhat
        rstd_ref[...] = rstd
        x1b_ref[...] = (xhat * g1_ref[...] + b1_ref[...]).astype(bf16)

        @pl.when(i == n_tiles - 1)
        def _():
            _gather_finish(gin, gout, *sems)

    tile = lambda w: pl.BlockSpec((tt, w), lambda i: (i, 0))
    hbm = pl.BlockSpec(memory_space=pltpu.HBM)
    out_shape = (
        jax.ShapeDtypeStruct((T, 3 * RW), bf16),
        jax.ShapeDtypeStruct((T, RW), f32),
        jax.ShapeDtypeStruct((T, RW), f32),
        jax.ShapeDtypeStruct((T // RET_TILE, HEADS, DH, DH), bf16),
        jax.ShapeDtypeStruct((T, D), bf16),
        jax.ShapeDtypeStruct((T, PW), bf16),
        jax.ShapeDtypeStruct((T, D), f32),
        jax.ShapeDtypeStruct((T, 1), f32),
        jax.ShapeDtypeStruct((T, D), bf16),
    ) + tuple(jax.ShapeDtypeStruct((N_DEV,) + b.shape, b.dtype) for b in gather)
    return pl.pallas_call(
        body, name="mix_forward", grid=(n_tiles,), out_shape=out_shape,
        in_specs=[tile(D), _const_spec((IN_W, D)), tile(DH), tile(DH),
                  _const_spec((HEADS, RET_TILE, RET_TILE)), _const_spec((HEADS, RET_TILE, DH)),
                  _const_spec((HEADS, RET_TILE, DH)),
                  _const_spec((GROUPS, DH, DH)), _const_spec((1, PW)), _const_spec((D, D)),
                  _const_spec((1, D)), _const_spec((1, D))] + [hbm] * n_g,
        out_specs=(tile(3 * RW), tile(RW), tile(RW),
                   pl.BlockSpec((tt // RET_TILE, HEADS, DH, DH), lambda i: (i, 0, 0, 0)),
                   tile(D), tile(PW), tile(D), tile(1), tile(D)) + (hbm,) * n_g,
        scratch_shapes=[pltpu.VMEM((HEADS, DH, DH), f32), pltpu.VMEM((tt + HALO, PW), f32),
                        pltpu.VMEM((tt + HALO, PW), f32)] + _gather_sems(n_g),
        compiler_params=pltpu.CompilerParams(dimension_semantics=("arbitrary",), vmem_limit_bytes=V7X_VMEM_LIMIT,
                                             collective_id=GATHER_BARRIER),
    )(x, w_in_t, cos, sin, dmat, qd, kd, w_pool, pool_scale, w_out, ln1_g, ln1_b, *gather)


def _ffn_forward_backward(xhat1, rstd1, ln1_g, ln1_b, w_up_t, conv_w, conv_b, w_down, ln2_g, ln2_b, target,
                          tt=256, widths=(512, 512, 512, 512, 512, 256)):
    n_tiles = T // tt
    assert sum(widths) == D_FF and all(w % 128 == 0 for w in widths)
    chunks = [(sum(widths[:c]), w) for c, w in enumerate(widths)]
    FH = 16
    hb = tt // FH

    def body(xhat_ref, halo_ref, rstd_ref, g1_ref, b1_ref, wupt_ref, cw_ref, cb_ref, wdown_ref, g2_ref, b2_ref, tgt_ref,
             dz1_ref, dz2b_ref, du_ref, f_ref, loss_ref, dg2_ref, db2_ref, dg1_ref, db1_ref, dcb_ref, dcw_ref,
             gext_s, val_s, dhext_s):
        i = pl.program_id(0)
        tile_idx = n_tiles - 1 - i

        def rd(ref, off, lo, w):
            return jnp.concatenate([ref[lo // 128 + k, pl.ds(off, tt), :] for k in range(w // 128)], axis=1)

        def wr(ref, lo, val):
            for k in range(val.shape[1] // 128):
                ref[lo // 128 + k, pl.ds(0, val.shape[0]), :] = val[:, k * 128:(k + 1) * 128]

        @pl.when(i == 0)
        def _():
            for r in (loss_ref, dg2_ref, db2_ref, dg1_ref, db1_ref, dcb_ref, dcw_ref):
                r[...] = jnp.zeros_like(r)
            dhext_s[:, pl.ds(tt, 8), :] = jnp.zeros((D_FF // 128, 8, 128), f32)

        g1, b1 = g1_ref[...], b1_ref[...]
        xhat = xhat_ref[...]
        x1 = xhat * g1 + b1
        x1b = x1.astype(bf16)
        x1h = ((halo_ref[...] * g1 + b1) * jnp.where(tile_idx == 0, 0.0, 1.0)).astype(bf16)
        x1ext = jnp.concatenate([x1h, x1b], axis=0)

        for lo, w in chunks:
            cs = slice(lo, lo + w)
            val = _dot(x1b, wupt_ref[pl.ds(lo, w), :], NT)
            gate_ext = _dot(x1ext, wupt_ref[pl.ds(D_FF + lo, w), :], NT)
            wr(gext_s, lo, gate_ext)
            hh = (cb_ref[:, cs] + cw_ref[0:1, cs] * rd(gext_s, FH - 2, lo, w) + cw_ref[1:2, cs] * rd(gext_s, FH - 1, lo, w)
                  + cw_ref[2:3, cs] * gate_ext[FH:])
            sg = _sigmoid(hh)
            act = hh * sg
            wr(dhext_s, lo, act)
            val_s[:, cs] = val * (sg + act * (1.0 - sg))
            f_ref[:, cs] = (act * val).astype(bf16)

        z = ALPHA * x1 + _dot(f_ref[...], wdown_ref[...])
        mu = jnp.mean(z, axis=-1, keepdims=True)
        zc = z - mu
        rstd2 = lax.rsqrt(jnp.mean(zc * zc, axis=-1, keepdims=True) + LN_EPS)
        xh2 = zc * rstd2
        diff = xh2 * g2_ref[...] + b2_ref[...] - tgt_ref[...]
        loss_ref[...] += 0.5 * jnp.sum(diff * diff) / D
        dy = diff * (1.0 / D)
        dg2_ref[...] += jnp.sum(dy * xh2, axis=0, keepdims=True)
        db2_ref[...] += jnp.sum(dy, axis=0, keepdims=True)
        dyg = dy * g2_ref[...]
        dz2 = rstd2 * (dyg - jnp.mean(dyg, axis=-1, keepdims=True) - xh2 * jnp.mean(dyg * xh2, axis=-1, keepdims=True))
        dz2b = dz2.astype(bf16)
        dz2b_ref[...] = dz2b

        for lo, w in chunks:
            cs = slice(lo, lo + w)
            df = _dot(dz2b, wdown_ref[pl.ds(lo, w), :], NT)
            dval = df * rd(dhext_s, 0, lo, w)
            dh = df * val_s[:, cs]
            wr(dhext_s, lo, dh)
            dh1, dh2, g0 = rd(dhext_s, 1, lo, w), rd(dhext_s, 2, lo, w), rd(gext_s, FH, lo, w)
            dcb_ref[:, cs] += jnp.sum(dh, axis=0, keepdims=True)
            dcw_ref[0:1, cs] += jnp.sum(dh2 * g0, axis=0, keepdims=True)
            dcw_ref[1:2, cs] += jnp.sum(dh1 * g0, axis=0, keepdims=True)
            dcw_ref[2:3, cs] += jnp.sum(dh * g0, axis=0, keepdims=True)
            dgate = cw_ref[2:3, cs] * dh + cw_ref[1:2, cs] * dh1 + cw_ref[0:1, cs] * dh2
            du_ref[:, cs] = dval.astype(bf16)
            du_ref[:, D_FF + lo: D_FF + lo + w] = dgate.astype(bf16)
        dhext_s[:, pl.ds(tt, 8), :] = dhext_s[:, pl.ds(0, 8), :]
        dx1 = ALPHA * dz2 + _dot(du_ref[...], wupt_ref[...])

        dg1_ref[...] += jnp.sum(dx1 * xhat, axis=0, keepdims=True)
        db1_ref[...] += jnp.sum(dx1, axis=0, keepdims=True)
        dxg = dx1 * g1
        dz1_ref[...] = rstd_ref[...] * (dxg - jnp.mean(dxg, axis=-1, keepdims=True)
                                        - xhat * jnp.mean(dxg * xhat, axis=-1, keepdims=True))

    rtile = lambda w: pl.BlockSpec((tt, w), lambda i: (n_tiles - 1 - i, 0))
    acc = lambda shape: pl.BlockSpec(shape, lambda i: (0, 0))
    out_shape = (
        jax.ShapeDtypeStruct((T, D), f32),
        jax.ShapeDtypeStruct((T, D), bf16),
        jax.ShapeDtypeStruct((T, 2 * D_FF), bf16),
        jax.ShapeDtypeStruct((T, D_FF), bf16),
        jax.ShapeDtypeStruct((8, 128), f32),
        jax.ShapeDtypeStruct((1, D), f32), jax.ShapeDtypeStruct((1, D), f32),
        jax.ShapeDtypeStruct((1, D), f32), jax.ShapeDtypeStruct((1, D), f32),
        jax.ShapeDtypeStruct((1, D_FF), f32), jax.ShapeDtypeStruct((3, D_FF), f32),
    )
    return pl.pallas_call(
        body, name="ffn_forward_backward", grid=(n_tiles,), out_shape=out_shape,
        in_specs=[rtile(D),
                  pl.BlockSpec((FH, D), lambda i: (jnp.maximum((n_tiles - 1 - i) * hb - 1, 0), 0)),
                  rtile(1), _const_spec((1, D)), _const_spec((1, D)), _const_spec((2 * D_FF, D)),
                  _const_spec((3, D_FF)), _const_spec((1, D_FF)), _const_spec((D_FF, D)),
                  _const_spec((1, D)), _const_spec((1, D)), rtile(D)],
        out_specs=(rtile(D), rtile(D), rtile(2 * D_FF), rtile(D_FF), acc((8, 128)),
                   acc((1, D)), acc((1, D)), acc((1, D)), acc((1, D)), acc((1, D_FF)), acc((3, D_FF))),
        scratch_shapes=[pltpu.VMEM((D_FF // 128, tt + FH, 128), f32), pltpu.VMEM((tt, D_FF), f32),
                        pltpu.VMEM((D_FF // 128, tt + 8, 128), f32)],
        compiler_params=pltpu.CompilerParams(dimension_semantics=("arbitrary",), vmem_limit_bytes=V7X_VMEM_LIMIT),
    )(xhat1, xhat1, rstd1, ln1_g, ln1_b, w_up_t, conv_w, conv_b, w_down, ln2_g, ln2_b, target)


def _mix_backward(dz1, w_out, qkv, g, oret, states, pooled, cat, cos, sin, dmat, qd, kd, cdec, w_pool, pool_scale, w_in_t,
                  after, tt=MIX_TILE):
    n_tiles = T // tt

    def body(dz1_ref, wout_ref, qkv_ref, g_ref, oret_ref, states_ref, pooled_ref, cat_ref, cos_ref, sin_ref, dmat_ref,
             qd_ref, kd_ref, wpool_ref, pscale_ref, wint_ref, after_ref,
             dproj_ref, gx_ref, dwpool_ref, dpscale_ref, dwout_ref, dstate_s, dout_s, eext_s, tmp_s, dwout_s):
        i = pl.program_id(0)
        tile_idx = n_tiles - 1 - i

        @pl.when(i == 0)
        def _():
            dstate_s[...] = jnp.zeros_like(dstate_s)
            dwpool_ref[...] = jnp.zeros_like(dwpool_ref)
            dpscale_ref[...] = jnp.zeros_like(dpscale_ref)
            dwout_s[...] = jnp.zeros_like(dwout_s)
            eext_s[pl.ds(tt, HALO), :] = jnp.zeros((HALO, PW), f32)

        dz1 = dz1_ref[...]
        dz1b = dz1.astype(bf16)
        dcat = _dot(dz1b, wout_ref[...], NT)
        dwout_s[...] += _dot(cat_ref[...], dz1b, TN)

        pos1 = (tile_idx * tt + lax.broadcasted_iota(jnp.int32, (tt, 1), 0) + 1).astype(f32)
        for gi, w in enumerate(WINDOWS):
            sl = slice(gi * DH, (gi + 1) * DH)
            dpo = dcat[:, RW + gi * DH: RW + (gi + 1) * DH]
            pooled_g = pooled_ref[:, sl]
            ylin = _dot(pooled_g, wpool_ref[gi])
            dpscale_ref[:, sl] += jnp.sum(dpo * ylin, axis=0, keepdims=True)
            dpw = (dpo * pscale_ref[:, sl]).astype(bf16)
            dwpool_ref[gi] += _dot(pooled_g, dpw, TN)
            dpooled = _dot(dpw, wpool_ref[gi], NT)
            eext_s[pl.ds(0, tt), sl] = dpooled / jnp.minimum(pos1, float(w))
            stages = int(math.log2(w))
            src = eext_s
            for s in range(stages):
                n = tt + 8 * (stages - 1 - s)
                shift = 2 ** s
                val = src[pl.ds(0, n), sl] + src[pl.ds(shift, n), sl]
                if s == stages - 1:
                    wsum = val
                else:
                    tmp_s[pl.ds(0, n), sl] = val
                    src = tmp_s
            dproj_ref[:, 4 * RW + gi * DH: 4 * RW + (gi + 1) * DH] = (wsum - dpooled).astype(bf16)
        eext_s[pl.ds(tt, HALO), :] = eext_s[pl.ds(0, HALO), :]

        for h in range(HEADS):
            sl = slice(h * DH, (h + 1) * DH)
            dr = dcat[:, sl]
            o = oret_ref[:, sl]
            r = lax.rsqrt(jnp.mean(o * o, axis=-1, keepdims=True) + RMS_EPS)
            rn = o * r
            gg = g_ref[:, sl]
            sg = _sigmoid(gg)
            dproj_ref[:, 3 * RW + h * DH: 3 * RW + (h + 1) * DH] = (dr * rn * (sg * (1.0 + gg * (1.0 - sg)))).astype(bf16)
            drn = dr * (gg * sg)
            dout_s[:, sl] = (r * (drn - rn * jnp.mean(drn * rn, axis=-1, keepdims=True))).astype(bf16)

        for sub in reversed(range(tt // RET_TILE)):
            rows = pl.ds(sub * RET_TILE, RET_TILE)
            cos_t, sin_t = cos_ref[rows, :], sin_ref[rows, :]
            for h in range(HEADS):
                q = qkv_ref[rows, h * DH:(h + 1) * DH]
                k = qkv_ref[rows, RW + h * DH: RW + (h + 1) * DH]
                v = qkv_ref[rows, 2 * RW + h * DH: 2 * RW + (h + 1) * DH]
                do = dout_s[rows, h * DH:(h + 1) * DH]
                stb = states_ref[sub, h]
                dst = dstate_s[h]
                dstb = dst.astype(bf16)
                sb = (_dot(q, k, NT) * dmat_ref[h]).astype(bf16)
                dsb = (_dot(do, v, NT) * dmat_ref[h]).astype(bf16)
                dq = _dot(dsb, k) + _dot(do, stb, NT) * qd_ref[h]
                dk = _dot(dsb, q, TN) + _dot(v, dstb, NT) * kd_ref[h]
                dv = _dot(sb, do, TN) + _dot((k.astype(f32) * kd_ref[h]).astype(bf16), dstb)
                dstate_s[h] = dst * cdec[h] + _dot((q.astype(f32) * qd_ref[h]).astype(bf16), do, TN)
                dproj_ref[rows, h * DH:(h + 1) * DH] = (dq * cos_t - _swap_halves(dq) * sin_t).astype(bf16)
                dproj_ref[rows, RW + h * DH: RW + (h + 1) * DH] = (
                    (dk * cos_t - _swap_halves(dk) * sin_t) * K_SCALE).astype(bf16)
                dproj_ref[rows, 2 * RW + h * DH: 2 * RW + (h + 1) * DH] = dv.astype(bf16)

        gx_ref[...] = ALPHA * dz1 + _dot(dproj_ref[...], wint_ref[...])

        @pl.when(i == n_tiles - 1)
        def _():
            dwout_ref[...] = dwout_s[...].astype(bf16)

    rtile = lambda w: pl.BlockSpec((tt, w), lambda i: (n_tiles - 1 - i, 0))
    out_shape = (
        jax.ShapeDtypeStruct((T, IN_W), bf16),
        jax.ShapeDtypeStruct((T, D), f32),
        jax.ShapeDtypeStruct((GROUPS, DH, DH), f32),
        jax.ShapeDtypeStruct((1, PW), f32),
        jax.ShapeDtypeStruct((D, D), bf16),
    )
    return pl.pallas_call(
        body, name="mix_backward", grid=(n_tiles,), out_shape=out_shape,
        in_specs=[rtile(D), _const_spec((D, D)), rtile(3 * RW), rtile(RW), rtile(RW),
                  pl.BlockSpec((tt // RET_TILE, HEADS, DH, DH), lambda i: (n_tiles - 1 - i, 0, 0, 0)),
                  rtile(PW), rtile(D), rtile(DH), rtile(DH),
                  _const_spec((HEADS, RET_TILE, RET_TILE)), _const_spec((HEADS, RET_TILE, DH)),
                  _const_spec((HEADS, RET_TILE, DH)),
                  _const_spec((GROUPS, DH, DH)), _const_spec((1, PW)), _const_spec((IN_W, D)),
                  pl.BlockSpec(memory_space=pl.ANY)],
        out_specs=(rtile(IN_W), rtile(D), pl.BlockSpec((GROUPS, DH, DH), lambda i: (0, 0, 0)),
                   pl.BlockSpec((1, PW), lambda i: (0, 0)),
                   pl.BlockSpec((D, D), lambda i: (0, 0), pipeline_mode=pl.Buffered(1))),
        scratch_shapes=[pltpu.VMEM((HEADS, DH, DH), f32), pltpu.VMEM((tt, RW), bf16),
                        pltpu.VMEM((tt + HALO, PW), f32), pltpu.VMEM((tt + HALO, PW), f32),
                        pltpu.VMEM((D, D), f32)],
        compiler_params=pltpu.CompilerParams(dimension_semantics=("arbitrary",), vmem_limit_bytes=V7X_VMEM_LIMIT),
    )(dz1, w_out, qkv, g, oret, states, pooled, cat, cos, sin, dmat, qd, kd, w_pool, pool_scale, w_in_t, after)


def _weight_grad(a, b, name, tm, exchange=(), tk=2048):
    m = a.shape[1]
    n_m, n_k, n_e = m // tm, T // tk, len(exchange)

    def body(a_ref, b_ref, *rest):
        ein, o_ref, eout, (acc_s, *sems) = rest[:n_e], rest[n_e], rest[n_e + 1:2 * n_e + 1], rest[2 * n_e + 1:]
        i, k = pl.program_id(0), pl.program_id(1)

        if n_e:
            @pl.when((i == 0) & (k == 0))
            def _():
                _chip_exchange_start(ein, eout, *sems)

        @pl.when(k == 0)
        def _():
            acc_s[...] = jnp.zeros_like(acc_s)

        acc_s[...] += _dot(a_ref[...], b_ref[pl.ds(pl.multiple_of(k * tk, tk), tk), :].astype(bf16), TN)

        @pl.when(k == n_k - 1)
        def _():
            o_ref[...] = acc_s[...].astype(bf16)

        if n_e:
            @pl.when((i == n_m - 1) & (k == n_k - 1))
            def _():
                _chip_exchange_finish(ein, eout, *sems)

    hbm = pl.BlockSpec(memory_space=pltpu.HBM)
    return pl.pallas_call(
        body, name=name, grid=(n_m, n_k),
        out_shape=(jax.ShapeDtypeStruct((m, D), bf16),) + tuple(jax.ShapeDtypeStruct(e.shape, e.dtype) for e in exchange),
        in_specs=[pl.BlockSpec((tk, tm), lambda i, k: (k, i)),
                  pl.BlockSpec((T, D), lambda i, k: (0, 0), pipeline_mode=pl.Buffered(1))] + [hbm] * n_e,
        out_specs=(pl.BlockSpec((tm, D), lambda i, k: (i, 0)),) + (hbm,) * n_e,
        scratch_shapes=[pltpu.VMEM((tm, D), f32)] + _chip_exchange_sems(n_e),
        compiler_params=pltpu.CompilerParams(dimension_semantics=("arbitrary", "arbitrary"),
                                             vmem_limit_bytes=V7X_VMEM_LIMIT,
                                             collective_id=CHIP_BARRIER if n_e else None),
    )(a, b, *exchange)


CHIP_FLIPS = ((1, 0), (0, 1), (1, 1))
PAIR_BARRIER, CHIP_BARRIER, GATHER_BARRIER, CHIP_BARRIER_SPLIT = 0, 1, 2, 3


def _barrier(peers):
    sem = pltpu.get_barrier_semaphore()
    for peer in peers:
        pl.semaphore_signal(sem, inc=1, device_id=peer, device_id_type=pl.DeviceIdType.MESH)
    pl.semaphore_wait(sem, len(peers))


def _me():
    return lax.axis_index("x"), lax.axis_index("y"), lax.axis_index("c")


def _chip(me, k):
    x, y, _ = me
    if k == 0:
        return x, y
    fx, fy = CHIP_FLIPS[k - 1]
    return (1 - x if fx else x), (1 - y if fy else y)


def _slot(x, y, c):
    return 4 * x + 2 * y + c


def _remote(src, dst, send_sem, recv_sem, to):
    return pltpu.make_async_remote_copy(src_ref=src, dst_ref=dst, send_sem=send_sem, recv_sem=recv_sem,
                                        device_id=to, device_id_type=pl.DeviceIdType.MESH)


def _gather_sems(n):
    return [pltpu.SemaphoreType.DMA((7, n)), pltpu.SemaphoreType.DMA((7, n)), pltpu.SemaphoreType.DMA((n,))] if n else []


def _gather_copy(k, j, gin, gout, send_sems, recv_sems, sending):
    x, y, c = _me()
    sibling, x_chip, y_chip, d_chip = (x, y, 1 - c), (1 - x, y), (x, 1 - y), (1 - x, 1 - y)
    south = c == 0
    passed_on = (jnp.where(south, 1 - x, x), jnp.where(south, y, 1 - y), c)
    src, to = gin[j], sibling
    if sending:
        block = {0: (x, y, c), 1: (x, y, c), 2: (x, y, c), 3: passed_on, 4: (*x_chip, c), 5: (*y_chip, c), 6: (*d_chip, c)}[k]
        to = {1: (*x_chip, c), 2: (*y_chip, c), 3: (jnp.where(south, x, 1 - x), jnp.where(south, 1 - y, y), c)}.get(k, sibling)
        if k >= 3:
            src = gout[j].at[_slot(*block)]
    else:
        block = {0: sibling, 1: (*x_chip, c), 2: (*y_chip, c), 3: (*d_chip, c), 4: (*x_chip, 1 - c), 5: (*y_chip, 1 - c),
                 6: (*d_chip, 1 - c)}[k]
    return _remote(src, gout[j].at[_slot(*block)], send_sems.at[k, j], recv_sems.at[k, j], to)


def _gather_do(ks, action, gin, gout, send_sems, recv_sems):
    for k in ks:
        for j in range(len(gin)):
            cp = _gather_copy(k, j, gin, gout, send_sems, recv_sems, action != "wait_recv")
            getattr(cp, action)()


def _gather_peers():
    x, y, c = _me()
    return [(x, y, 1 - c), (1 - x, y, c), (x, 1 - y, c)]


def _gather_start(gin, gout, send_sems, recv_sems, local_sems, barrier=True):
    if barrier:
        _barrier(_gather_peers())
    for j in range(len(gin)):
        pltpu.make_async_copy(gin[j], gout[j].at[_slot(*_me())], local_sems.at[j]).start()
    _gather_do((0, 1, 2), "start", gin, gout, send_sems, recv_sems)


def _gather_forward(gin, gout, send_sems, recv_sems, local_sems):
    _gather_do((1, 2), "wait_recv", gin, gout, send_sems, recv_sems)
    _gather_do((3, 4, 5), "start", gin, gout, send_sems, recv_sems)


def _gather_finish(gin, gout, send_sems, recv_sems, local_sems):
    _gather_do((3,), "wait_recv", gin, gout, send_sems, recv_sems)
    _gather_do((6,), "start", gin, gout, send_sems, recv_sems)
    _gather_do((0, 4, 5, 6), "wait_recv", gin, gout, send_sems, recv_sems)
    _gather_do(range(7), "wait_send", gin, gout, send_sems, recv_sems)
    for j in range(len(gin)):
        pltpu.make_async_copy(gin[j], gout[j].at[_slot(*_me())], local_sems.at[j]).wait()


def _all_gather(blocks, name):
    n = len(blocks)

    def body(*refs):
        gin, gout, sems = refs[:n], refs[n:2 * n], refs[2 * n:]
        _gather_start(gin, gout, *sems)
        _gather_forward(gin, gout, *sems)
        _gather_finish(gin, gout, *sems)

    hbm = pl.BlockSpec(memory_space=pltpu.HBM)
    return pl.pallas_call(
        body, name=name,
        out_shape=tuple(jax.ShapeDtypeStruct((N_DEV,) + b.shape, b.dtype) for b in blocks),
        in_specs=[hbm] * n, out_specs=(hbm,) * n, scratch_shapes=_gather_sems(n),
        compiler_params=pltpu.CompilerParams(collective_id=GATHER_BARRIER),
    )(*blocks)


def _pair_reduce(parts, name):
    n = len(parts)

    def body(*refs):
        ins, own, others, landing, mine = (refs[k * n:(k + 1) * n] for k in range(5))
        send_sems, recv_sems, local_sems = refs[5 * n:]
        me = _me()
        x, y, c = me
        sibling = (x, y, 1 - c)
        _barrier([sibling])
        sends, loads = [], []
        for k in range(4):
            for j in range(n):
                cp = _remote(ins[j].at[_slot(*_chip(me, k), 1 - c)], landing[j].at[k], send_sems.at[k, j],
                             recv_sems.at[k, j], sibling)
                cp.start()
                sends.append(cp)
                ld = pltpu.make_async_copy(ins[j].at[_slot(*_chip(me, k), c)], mine[j].at[k], local_sems.at[k, j])
                ld.start()
                loads.append(ld)
        for k in range(4):
            for j in range(n):
                loads[k * n + j].wait()
                _remote(ins[j].at[0], landing[j].at[k], send_sems.at[k, j], recv_sems.at[k, j], sibling).wait_recv()
                total = mine[j][k].astype(f32) + landing[j][k].astype(f32)
                if k == 0:
                    own[j][...] = total.astype(own[j].dtype)
                else:
                    others[j][k - 1] = total.astype(others[j].dtype)
        for cp in sends:
            cp.wait_send()

    vm = pl.BlockSpec(memory_space=pltpu.VMEM)
    return pl.pallas_call(
        body, name=name,
        out_shape=tuple(jax.ShapeDtypeStruct(p.shape[1:], p.dtype) for p in parts)
        + tuple(jax.ShapeDtypeStruct((3,) + p.shape[1:], p.dtype) for p in parts),
        in_specs=[pl.BlockSpec(memory_space=pltpu.HBM)] * n, out_specs=(vm,) * (2 * n),
        scratch_shapes=[pltpu.VMEM((4,) + p.shape[1:], p.dtype) for p in parts] * 2
        + [pltpu.SemaphoreType.DMA((4, n)), pltpu.SemaphoreType.DMA((4, n)), pltpu.SemaphoreType.DMA((4, n))],
        compiler_params=pltpu.CompilerParams(vmem_limit_bytes=V7X_VMEM_LIMIT, collective_id=PAIR_BARRIER),
    )(*parts)


def _chip_exchange_sems(n):
    return [pltpu.SemaphoreType.DMA((3, n)), pltpu.SemaphoreType.DMA((3, n))] if n else []


def _chip_exchange_copy(k, j, ein, eout, send_sems, recv_sems):
    me = _me()
    return _remote(ein[j].at[k - 1], eout[j].at[k - 1], send_sems.at[k - 1, j], recv_sems.at[k - 1, j],
                   (*_chip(me, k), me[2]))


def _chip_peers():
    me = _me()
    return [(*_chip(me, k), me[2]) for k in range(1, 4)]


def _chip_exchange_start(ein, eout, send_sems, recv_sems, barrier=True):
    if barrier:
        _barrier(_chip_peers())
    for k in range(1, 4):
        for j in range(len(ein)):
            _chip_exchange_copy(k, j, ein, eout, send_sems, recv_sems).start()


def _chip_exchange_finish(ein, eout, send_sems, recv_sems):
    for k in range(1, 4):
        for j in range(len(ein)):
            _chip_exchange_copy(k, j, ein, eout, send_sems, recv_sems).wait_recv()
    for k in range(1, 4):
        for j in range(len(ein)):
            _chip_exchange_copy(k, j, ein, eout, send_sems, recv_sems).wait_send()


def _split_copies(src_ref, dst_ref, sems):
    me = _me()
    return [_remote(src_ref.at[k - 1], dst_ref.at[k - 1], sems[k - 1], sems[2 + k], (*_chip(me, k), me[2]))
            for k in range(1, 4)]


def _exchange_start(others, name, barrier_id):
    def body(src_ref, land_ref, *rest):
        sems, token_ref = rest[:6], rest[8]
        _barrier(_chip_peers())
        for copy in _split_copies(src_ref, land_ref, sems):
            copy.start()
        token_ref[...] = jnp.zeros_like(token_ref)

    hbm, sem = pl.BlockSpec(memory_space=pltpu.HBM), pl.BlockSpec(memory_space=pltpu.SEMAPHORE)
    thru = pltpu.HBM(others.shape, others.dtype)
    res = pl.pallas_call(
        body, name=name,
        out_shape=(pltpu.SemaphoreType.DMA(()),) * 6 + (thru, thru, jax.ShapeDtypeStruct((8, 128), f32)),
        in_specs=(hbm, hbm), out_specs=(sem,) * 6 + (hbm, hbm, pl.BlockSpec(memory_space=pltpu.VMEM)),
        input_output_aliases={0: 6, 1: 7},
        compiler_params=pltpu.CompilerParams(has_side_effects=pltpu.SideEffectType.DATAFLOW_SIDE_EFFECTING,
                                             collective_id=barrier_id),
    )(pltpu.with_memory_space_constraint(others, pltpu.HBM),
      pltpu.with_memory_space_constraint(lax.empty(others.shape, others.dtype), pltpu.HBM))
    return res[:6], res[6], res[7], res[8]


def _exchange_wait(sems, src_thru, land_thru, after, name):
    n_after = len(after)

    def body(src_ref, land_ref, *rest):
        for copy in _split_copies(src_ref, land_ref, rest[:6]):
            copy.wait_send()
            copy.wait_recv()

    hbm, sem = pl.BlockSpec(memory_space=pltpu.HBM), pl.BlockSpec(memory_space=pltpu.SEMAPHORE)
    thru = pltpu.HBM(src_thru.shape, src_thru.dtype)
    return pl.pallas_call(
        body, name=name, out_shape=(thru, thru),
        in_specs=(hbm, hbm) + (sem,) * 6 + (pl.BlockSpec(memory_space=pl.ANY),) * n_after, out_specs=(hbm, hbm),
        input_output_aliases={0: 0, 1: 1},
        compiler_params=pltpu.CompilerParams(has_side_effects=pltpu.SideEffectType.DATAFLOW_SIDE_EFFECTING),
    )(src_thru, land_thru, *sems, *after)[1]


def _sum_parts(owns, arrived, name):
    n = len(owns)

    def body(*refs):
        for own, arr, out in zip(refs[:n], refs[n:2 * n], refs[2 * n:]):
            acc = own[...].astype(f32)
            for k in range(3):
                acc = acc + arr[k].astype(f32)
            out[...] = acc

    vm = pl.BlockSpec(memory_space=pltpu.VMEM)
    return pl.pallas_call(
        body, name=name, out_shape=tuple(jax.ShapeDtypeStruct(o.shape, f32) for o in owns),
        in_specs=[vm] * (2 * n), out_specs=(vm,) * n,
        compiler_params=pltpu.CompilerParams(vmem_limit_bytes=V7X_VMEM_LIMIT),
    )(*owns, *arrived)


ADAM_C1 = 1.0 / (1.0 - ADAM_B1 ** ADAM_STEP)
ADAM_C2 = 1.0 / (1.0 - ADAM_B2 ** ADAM_STEP)


def _adam_update(w, g, m, v):
    m = ADAM_B1 * m + (1.0 - ADAM_B1) * g
    v = ADAM_B2 * v + (1.0 - ADAM_B2) * (g * g)
    return -ADAM_LR * ((m * ADAM_C1) / (jnp.sqrt(v * ADAM_C2) + ADAM_EPS) + ADAM_WD * w), m, v


def _sum_adamw(own, arrived, w, m, v, name, steps, after=()):
    rows = own.shape[0]
    br = rows // steps

    def body(own_ref, arr_ref, w_ref, m_ref, v_ref, *rest):
        g_out, d_out, m_out, v_out = rest[len(after):]
        g = own_ref[...].astype(f32)
        for k in range(3):
            g = g + arr_ref[k].astype(f32)
        g_out[...] = g
        d_out[...], m_out[...], v_out[...] = _adam_update(w_ref[...], g, m_ref[...], v_ref[...])

    blk = pl.BlockSpec((br, D), lambda i: (i, 0))
    return pl.pallas_call(
        body, name=name, grid=(steps,), out_shape=(jax.ShapeDtypeStruct((rows, D), f32),) * 4,
        in_specs=[blk, pl.BlockSpec((3, br, D), lambda i: (0, i, 0)), blk, blk, blk]
        + [pl.BlockSpec(memory_space=pl.ANY)] * len(after), out_specs=(blk,) * 4,
        compiler_params=pltpu.CompilerParams(dimension_semantics=("parallel",), vmem_limit_bytes=V7X_VMEM_LIMIT),
    )(own, arrived, w, m, v, *after)


def _adamw(ws, gs, ms, vs, name):
    n = len(ws)

    def body(*refs):
        w_r, g_r, m_r, v_r = (refs[k * n:(k + 1) * n] for k in range(4))
        d_o, m_o, v_o = (refs[(4 + k) * n:(5 + k) * n] for k in range(3))
        for j in range(n):
            d_o[j][...], m_o[j][...], v_o[j][...] = _adam_update(w_r[j][...], g_r[j][...], m_r[j][...], v_r[j][...])

    vm = pl.BlockSpec(memory_space=pltpu.VMEM)
    shapes = tuple(jax.ShapeDtypeStruct(w.shape, f32) for w in ws)
    return pl.pallas_call(
        body, name=name, out_shape=shapes * 3, in_specs=[vm] * (4 * n), out_specs=tuple([vm] * (3 * n)),
        compiler_params=pltpu.CompilerParams(vmem_limit_bytes=V7X_VMEM_LIMIT),
    )(*ws, *gs, *ms, *vs)


SMALL = (("w_pool", GROUPS * DH * DH), ("pool_scale", PW), ("ln1_g", D), ("ln1_b", D), ("conv_b", D_FF),
         ("ln2_g", D), ("ln2_b", D), ("conv_w", 3 * D_FF), ("loss", 1))
SMALL_ROWS = 640


def _pack(named):
    flat = jnp.concatenate([named[k].reshape(-1) for k, _ in SMALL])
    return jnp.pad(flat, (0, SMALL_ROWS * 128 - flat.shape[0])).reshape(SMALL_ROWS, 128)


def _unpack(packed):
    flat, out, at = packed.reshape(-1), {}, 0
    for k, size in SMALL:
        out[k] = flat[at:at + size]
        at += size
    return out


def kernel(x, w_in, w_pool, pool_scale, w_out, ln1_g, ln1_b, w_up, conv_w, conv_b, w_down, ln2_g, ln2_b, loss_target, m_w_in, m_w_pool, m_pool_scale, m_w_out, m_ln1_g, m_ln1_b, m_w_up, m_conv_w, m_conv_b, m_w_down, m_ln2_g, m_ln2_b, v_w_in, v_w_pool, v_pool_scale, v_w_out, v_ln1_g, v_ln1_b, v_w_up, v_conv_w, v_conv_b, v_w_down, v_ln2_g, v_ln2_b):
    me = 4 * lax.axis_index("x") + 2 * lax.axis_index("y") + lax.axis_index("c")
    x2, tgt = x[0], loss_target[0]

    g_in, g_out, g_cw = _all_gather([w_in[0].T.astype(bf16), w_out[0].astype(bf16), jnp.transpose(conv_w, (1, 0, 2))],
                                    "gather_weights")
    w_in_t = g_in.reshape(IN_W, D)
    w_out_f = g_out.reshape(D, D)
    conv_w_f = jnp.transpose(g_cw[:, :, 0, :], (1, 0, 2)).reshape(3, D_FF)
    w_pool_b = w_pool[0].astype(bf16)

    cos, sin = _rope_tables()
    dmat, qd, kd, cdec = _decay_tables(RET_TILE)

    qkv, g, oret, states, cat, pooled, xhat1, rstd1, x1b, g_up, g_down = _mix_forward(
        x2, w_in_t, cos, sin, dmat, qd, kd, cdec, w_pool_b, pool_scale, w_out_f, ln1_g, ln1_b,
        gather=[w_up[0].T.astype(bf16), w_down[0].astype(bf16)])
    w_up_t = g_up.reshape(2 * D_FF, D)
    w_down_f = g_down.reshape(D_FF, D)
    dz1, dz2b, du, f, loss8, d_ln2_g, d_ln2_b, d_ln1_g, d_ln1_b, d_conv_b, d_conv_w = _ffn_forward_backward(
        xhat1, rstd1, ln1_g, ln1_b, w_up_t, conv_w_f, conv_b, w_down_f, ln2_g, ln2_b, tgt)

    (dw_down,) = _weight_grad(f, dz2b, "grad_w_down", tm=D_FF // 2)
    own_down, oth_down = _pair_reduce([dw_down.reshape(N_DEV, ROWS_DOWN, D)], "pair_reduce_down")
    dw_up_t, arr_down = _weight_grad(du, x1b, "grad_w_up", tm=D_FF // 2, exchange=[oth_down])
    own_up, oth_up = _pair_reduce([dw_up_t.reshape(N_DEV, ROWS_UP, D)], "pair_reduce_up")
    up_sems, up_src, up_land, up_started = _exchange_start(oth_up, "exchange_up_start", CHIP_BARRIER_SPLIT)
    dproj, grad_x, d_w_pool, d_pool_scale, dw_out = _mix_backward(
        dz1, w_out_f, qkv, g, oret, states, pooled, cat, cos, sin, dmat, qd, kd, cdec, w_pool_b, pool_scale, w_in_t,
        after=up_started)
    small = _pack({"w_pool": d_w_pool, "pool_scale": d_pool_scale, "ln1_g": d_ln1_g, "ln1_b": d_ln1_b,
                   "conv_b": d_conv_b, "ln2_g": d_ln2_g, "ln2_b": d_ln2_b, "conv_w": d_conv_w, "loss": loss8[0, :1]})
    own_out, own_small, oth_out, oth_small = _pair_reduce(
        [dw_out.reshape(N_DEV, ROWS_OUT, D), small.reshape(N_DEV, SMALL_ROWS // N_DEV, 128)], "pair_reduce_out")
    dw_in_t, arr_out, arr_small = _weight_grad(dproj, x2, "grad_w_in", tm=IN_W // 2, exchange=[oth_out, oth_small])
    arr_up = _exchange_wait(up_sems, up_src, up_land, [dw_in_t], "exchange_up_wait")
    own_in, oth_in = _pair_reduce([dw_in_t.reshape(N_DEV, ROWS_IN, D)], "pair_reduce_in")
    in_sems, in_src, in_land, started = _exchange_start(oth_in, "exchange_in_start", CHIP_BARRIER)
    (small_piece,) = _sum_parts([own_small], [arr_small], "sum_small_grads")
    (gs_small,) = _all_gather([small_piece], "gather_small_grads")

    names = ["w_in", "w_pool", "pool_scale", "w_out", "ln1_g", "ln1_b", "w_up", "conv_w", "conv_b", "w_down",
             "ln2_g", "ln2_b"]
    w_d = dict(w_in=w_in, w_pool=w_pool, pool_scale=pool_scale, w_out=w_out, ln1_g=ln1_g, ln1_b=ln1_b, w_up=w_up,
               conv_w=conv_w, conv_b=conv_b, w_down=w_down, ln2_g=ln2_g, ln2_b=ln2_b)
    m_d = dict(w_in=m_w_in, w_pool=m_w_pool, pool_scale=m_pool_scale, w_out=m_w_out, ln1_g=m_ln1_g, ln1_b=m_ln1_b,
               w_up=m_w_up, conv_w=m_conv_w, conv_b=m_conv_b, w_down=m_w_down, ln2_g=m_ln2_g, ln2_b=m_ln2_b)
    v_d = dict(w_in=v_w_in, w_pool=v_w_pool, pool_scale=v_pool_scale, w_out=v_w_out, ln1_g=v_ln1_g, ln1_b=v_ln1_b,
               w_up=v_w_up, conv_w=v_conv_w, conv_b=v_conv_b, w_down=v_w_down, ln2_g=v_ln2_g, ln2_b=v_ln2_b)
    g_d, delta, new_m, new_v = {}, {}, {}, {}

    def big_adamw(k, own, arr, transposed, steps, after=()):
        lay = (lambda a: a[0].T) if transposed else (lambda a: a[0])
        back = (lambda a: a.T[None]) if transposed else (lambda a: a[None])
        res = _sum_adamw(own, arr, lay(w_d[k]), lay(m_d[k]), lay(v_d[k]), "adamw_" + k, steps, after)
        g_d[k], delta[k], new_m[k], new_v[k] = (back(r) for r in res)
        return res[3]

    done = [big_adamw("w_up", own_up, arr_up, True, 4, after=(started,)),
            big_adamw("w_down", own_down, arr_down, False, 2, after=(started,)),
            big_adamw("w_out", own_out, arr_out, False, 2, after=(started,))]

    gsm = _unpack(gs_small)
    gsm["conv_w"] = lax.dynamic_slice(gsm["conv_w"].reshape(3, D_FF), (0, me * (D_FF // N_DEV)), (3, D_FF // N_DEV))
    lay = lambda k, a: jnp.transpose(a, (1, 0, 2)) if k == "conv_w" else a.reshape(-1, a.shape[-1])
    back = lambda k, a: jnp.transpose(a, (1, 0, 2)) if k == "conv_w" else a.reshape(w_d[k].shape)
    group = [k for k in names if k not in ("w_in", "w_out", "w_up", "w_down")]
    for k in group:
        g_d[k] = gsm[k].reshape(w_d[k].shape)
    res = _adamw([lay(k, w_d[k]) for k in group], [lay(k, g_d[k]) for k in group], [lay(k, m_d[k]) for k in group],
                 [lay(k, v_d[k]) for k in group], "adamw_small")
    for j, k in enumerate(group):
        delta[k], new_m[k], new_v[k] = (back(k, res[part * len(group) + j]) for part in range(3))

    arr_in = _exchange_wait(in_sems, in_src, in_land, done + [res[0]], "exchange_in_wait")
    big_adamw("w_in", own_in, arr_in, True, 4)

    loss = gsm["loss"].reshape(())
    return (loss, grad_x[None], *[g_d[k] for k in names], *[delta[k] for k in names], *[new_m[k] for k in names],
            *[new_v[k] for k in names])
```

```python
import math

import numpy as np
import jax
import jax.numpy as jnp
from jax import lax
from jax.experimental import pallas as pl
from jax.experimental.pallas import tpu as pltpu

f32 = jnp.float32
bf16 = jnp.bfloat16

N_DEV = 8
T = 4096
D = 1024
CHUNK = 64
MIX_TILE = 512
RET_TILE = 256
HEADS = 4
DH = 128
RW = HEADS * DH
PW = 512
GROUPS = 4
WINDOWS = (2, 4, 8, 16)
IN_W = 4 * RW + PW
D_FF = 2816
LN_EPS = 1e-5
RMS_EPS = 1e-6
ALPHA = 2.0 ** 0.25
K_SCALE = DH ** -0.5

ADAM_LR = 0.001
ADAM_B1 = 0.9
ADAM_B2 = 0.999
ADAM_EPS = 1e-08
ADAM_WD = 0.01
ADAM_STEP = 10

ROWS_IN, ROWS_OUT, ROWS_UP, ROWS_DOWN = IN_W // N_DEV, D // N_DEV, 2 * D_FF // N_DEV, D_FF // N_DEV

V7X_VMEM_LIMIT = 56 * 2 ** 20
HALO = 32

NT = (((1,), (1,)), ((), ()))
TN = (((0,), (0,)), ((), ()))
NN = (((1,), (0,)), ((), ()))


def _dot(a, b, dims=NN):
    return lax.dot_general(a, b, dims, preferred_element_type=f32)


def _const_spec(shape):
    zeros = (0,) * len(shape)
    return pl.BlockSpec(shape, lambda i: zeros, pipeline_mode=pl.Buffered(1))


def _sigmoid(x):
    return 0.5 * jnp.tanh(0.5 * x) + 0.5


def _decay_tables(tt):
    h = np.arange(HEADS, dtype=np.float64)
    log_gamma = np.log(1.0 - 2.0 ** (-5.0 - h)).astype(np.float32).astype(np.float64)[:, None, None]
    idx = np.arange(tt, dtype=np.float64)
    visible = (idx[None, :] // CHUNK) <= (idx[:, None] // CHUNK)
    mask = np.where(visible[None], np.exp(log_gamma * np.abs(idx[:, None] - idx[None, :])[None]), 0.0)
    qd = np.broadcast_to(np.exp(log_gamma * (idx[None, :, None] + 1.0)), (HEADS, tt, DH))
    kd = np.broadcast_to(np.exp(log_gamma * (tt - 1.0 - idx[None, :, None])), (HEADS, tt, DH))
    cd = np.exp(log_gamma[:, 0, 0] * tt)
    return (jnp.asarray(mask, f32), jnp.asarray(qd, f32), jnp.asarray(kd, f32), [float(c) for c in cd])


def _rope_tables():
    inv_freq = (10000.0 ** (-np.arange(0, DH, 2, dtype=np.float64) / DH)).astype(np.float32)
    ang = (np.arange(T, dtype=np.float32)[:, None] * inv_freq[None, :]).astype(np.float64)
    cos, sin = np.cos(ang), np.sin(ang)
    return (jnp.asarray(np.concatenate([cos, cos], axis=1), f32), jnp.asarray(np.concatenate([-sin, sin], axis=1), f32))


def _swap_halves(t):
    return pltpu.roll(t, DH // 2, axis=1)


def _mix_forward(x, w_in_t, cos, sin, dmat, qd, kd, cdec, w_pool, pool_scale, w_out, ln1_g, ln1_b, gather,
                 tt=MIX_TILE):
    n_tiles = T // tt
    n_g = len(gather)

    def body(x_ref, wint_ref, cos_ref, sin_ref, dmat_ref, qd_ref, kd_ref, wpool_ref, pscale_ref, wout_ref,
             g1_ref, b1_ref, *rest):
        gin, rest = rest[:n_g], rest[n_g:]
        qkv_ref, g_ref, oret_ref, states_ref, cat_ref, pooled_ref, xhat_ref, rstd_ref, x1b_ref = rest[:9]
        gout, (state_s, pext_s, tmp_s, *sems) = rest[9:9 + n_g], rest[9 + n_g:]
        i = pl.program_id(0)

        @pl.when(i == 0)
        def _():
            state_s[...] = jnp.zeros_like(state_s)
            pext_s[:, pl.ds(0, HALO), :] = jnp.zeros((GROUPS, HALO, DH), f32)
            _gather_start(gin, gout, *sems)

        @pl.when(i == n_tiles - 2)
        def _():
            _gather_forward(gin, gout, *sems)

        xb = x_ref[...].astype(bf16)
        cos_t, sin_t = cos_ref[...], sin_ref[...]
        for part in range(2):
            pr = _dot(xb, wint_ref[pl.ds(part * RW, RW), :], NT)
            for h in range(HEADS):
                t = pr[:, h * DH:(h + 1) * DH]
                r = t * cos_t + _swap_halves(t) * sin_t
                if part == 1:
                    r = r * K_SCALE
                qkv_ref[:, part * RW + h * DH: part * RW + (h + 1) * DH] = r.astype(bf16)
        qkv_ref[:, 2 * RW:3 * RW] = _dot(xb, wint_ref[pl.ds(2 * RW, RW), :], NT).astype(bf16)
        g_ref[...] = _dot(xb, wint_ref[pl.ds(3 * RW, RW), :], NT)
        p = _dot(xb, wint_ref[pl.ds(4 * RW, PW), :], NT)
        for gi in range(GROUPS):
            pext_s[gi, pl.ds(HALO, tt), :] = p[:, gi * DH:(gi + 1) * DH]

        for sub in range(tt // RET_TILE):
            rows = pl.ds(sub * RET_TILE, RET_TILE)
            for h in range(HEADS):
                q = qkv_ref[rows, h * DH:(h + 1) * DH]
                k = qkv_ref[rows, RW + h * DH: RW + (h + 1) * DH]
                v = qkv_ref[rows, 2 * RW + h * DH: 2 * RW + (h + 1) * DH]
                s = _dot(q, k, NT) * dmat_ref[h]
                st = state_s[h]
                stb = st.astype(bf16)
                states_ref[sub, h] = stb
                oret_ref[rows, h * DH:(h + 1) * DH] = (_dot(s.astype(bf16), v)
                                                      + _dot((q.astype(f32) * qd_ref[h]).astype(bf16), stb))
                state_s[h] = st * cdec[h] + _dot((k.astype(f32) * kd_ref[h]).astype(bf16), v, TN)

        for h in range(HEADS):
            sl = slice(h * DH, (h + 1) * DH)
            o = oret_ref[:, sl]
            r = lax.rsqrt(jnp.mean(o * o, axis=-1, keepdims=True) + RMS_EPS)
            gg = g_ref[:, sl]
            cat_ref[:, sl] = (o * r * (gg * _sigmoid(gg))).astype(bf16)

        pos1 = (i * tt + lax.broadcasted_iota(jnp.int32, (tt, 1), 0) + 1).astype(f32)
        for gi, w in enumerate(WINDOWS):
            sl = slice(gi * DH, (gi + 1) * DH)
            stages = int(math.log2(w))
            src = pext_s
            for s in range(stages):
                lo = HALO - 8 * (stages - 1 - s)
                n = tt + HALO - lo
                shift = 2 ** s
                val = src[gi, pl.ds(lo, n), :] + src[gi, pl.ds(lo - shift, n), :]
                if s == stages - 1:
                    wsum = val
                else:
                    tmp_s[gi, pl.ds(lo, n), :] = val
                    src = tmp_s
            p_g = pext_s[gi, pl.ds(HALO, tt), :]
            pooled = (wsum / jnp.minimum(pos1, float(w)) - p_g).astype(bf16)
            pooled_ref[:, sl] = pooled
            y = _dot(pooled, wpool_ref[gi]) * pscale_ref[:, sl]
            cat_ref[:, RW + gi * DH: RW + (gi + 1) * DH] = y.astype(bf16)
        pext_s[:, pl.ds(0, HALO), :] = pext_s[:, pl.ds(tt, HALO), :]

        z = ALPHA * x_ref[...] + _dot(cat_ref[...], wout_ref[...])
        mu = jnp.mean(z, axis=-1, keepdims=True)
        zc = z - mu
        rstd = lax.rsqrt(jnp.mean(zc * zc, axis=-1, keepdims=True) + LN_EPS)
        xhat = zc * rstd
        xhat_ref[...] = xhat
        rstd_ref[...] = rstd
        x1b_ref[...] = (xhat * g1_ref[...] + b1_ref[...]).astype(bf16)

        @pl.when(i == n_tiles - 1)
        def _():
            _gather_finish(gin, gout, *sems)

    tile = lambda w: pl.BlockSpec((tt, w), lambda i: (i, 0))
    hbm = pl.BlockSpec(memory_space=pltpu.HBM)
    out_shape = (
        jax.ShapeDtypeStruct((T, 3 * RW), bf16),
        jax.ShapeDtypeStruct((T, RW), f32),
        jax.ShapeDtypeStruct((T, RW), f32),
        jax.ShapeDtypeStruct((T // RET_TILE, HEADS, DH, DH), bf16),
        jax.ShapeDtypeStruct((T, D), bf16),
        jax.ShapeDtypeStruct((T, PW), bf16),
        jax.ShapeDtypeStruct((T, D), f32),
        jax.ShapeDtypeStruct((T, 1), f32),
        jax.ShapeDtypeStruct((T, D), bf16),
    ) + tuple(jax.ShapeDtypeStruct((N_DEV,) + b.shape, b.dtype) for b in gather)
    return pl.pallas_call(
        body, name="mix_forward", grid=(n_tiles,), out_shape=out_shape,
        in_specs=[tile(D), _const_spec((IN_W, D)), tile(DH), tile(DH),
                  _const_spec((HEADS, RET_TILE, RET_TILE)), _const_spec((HEADS, RET_TILE, DH)),
                  _const_spec((HEADS, RET_TILE, DH)),
                  _const_spec((GROUPS, DH, DH)), _const_spec((1, PW)), _const_spec((D, D)),
                  _const_spec((1, D)), _const_spec((1, D))] + [hbm] * n_g,
        out_specs=(tile(3 * RW), tile(RW), tile(RW),
                   pl.BlockSpec((tt // RET_TILE, HEADS, DH, DH), lambda i: (i, 0, 0, 0)),
                   tile(D), tile(PW), tile(D), tile(1), tile(D)) + (hbm,) * n_g,
        scratch_shapes=[pltpu.VMEM((HEADS, DH, DH), f32), pltpu.VMEM((GROUPS, tt + HALO, DH), f32),
                        pltpu.VMEM((GROUPS, tt + HALO, DH), f32)] + _gather_sems(n_g),
        compiler_params=pltpu.CompilerParams(dimension_semantics=("arbitrary",), vmem_limit_bytes=V7X_VMEM_LIMIT,
                                             collective_id=GATHER_BARRIER),
    )(x, w_in_t, cos, sin, dmat, qd, kd, w_pool, pool_scale, w_out, ln1_g, ln1_b, *gather)


def _ffn_forward_backward(xhat1, rstd1, ln1_g, ln1_b, w_up_t, conv_w, conv_b, w_down, ln2_g, ln2_b, target,
                          tt=256, widths=(512, 512, 512, 512, 512, 256)):
    n_tiles = T // tt
    assert sum(widths) == D_FF and all(w % 128 == 0 for w in widths)
    chunks = [(sum(widths[:c]), w) for c, w in enumerate(widths)]
    FH = 16
    hb = tt // FH

    def body(xhat_ref, halo_ref, rstd_ref, g1_ref, b1_ref, wupt_ref, cw_ref, cb_ref, wdown_ref, g2_ref, b2_ref, tgt_ref,
             dz1_ref, dz2b_ref, du_ref, f_ref, loss_ref, dg2_ref, db2_ref, dg1_ref, db1_ref, dcb_ref, dcw_ref,
             gext_s, val_s, dhext_s):
        i = pl.program_id(0)
        tile_idx = n_tiles - 1 - i

        def rd(ref, off, lo, w):
            return jnp.concatenate([ref[lo // 128 + k, pl.ds(off, tt), :] for k in range(w // 128)], axis=1)

        def wr(ref, lo, val):
            for k in range(val.shape[1] // 128):
                ref[lo // 128 + k, pl.ds(0, val.shape[0]), :] = val[:, k * 128:(k + 1) * 128]

        @pl.when(i == 0)
        def _():
            for r in (loss_ref, dg2_ref, db2_ref, dg1_ref, db1_ref, dcb_ref, dcw_ref):
                r[...] = jnp.zeros_like(r)
            dhext_s[:, pl.ds(tt, 8), :] = jnp.zeros((D_FF // 128, 8, 128), f32)

        g1, b1 = g1_ref[...], b1_ref[...]
        xhat = xhat_ref[...]
        x1 = xhat * g1 + b1
        x1b = x1.astype(bf16)
        x1h = ((halo_ref[...] * g1 + b1) * jnp.where(tile_idx == 0, 0.0, 1.0)).astype(bf16)
        x1ext = jnp.concatenate([x1h, x1b], axis=0)

        for lo, w in chunks:
            cs = slice(lo, lo + w)
            val = _dot(x1b, wupt_ref[pl.ds(lo, w), :], NT)
            gate_ext = _dot(x1ext, wupt_ref[pl.ds(D_FF + lo, w), :], NT)
            wr(gext_s, lo, gate_ext)
            hh = (cb_ref[:, cs] + cw_ref[0:1, cs] * rd(gext_s, FH - 2, lo, w) + cw_ref[1:2, cs] * rd(gext_s, FH - 1, lo, w)
                  + cw_ref[2:3, cs] * gate_ext[FH:])
            sg = _sigmoid(hh)
            act = hh * sg
            wr(dhext_s, lo, act)
            val_s[:, cs] = val * (sg + act * (1.0 - sg))
            f_ref[:, cs] = (act * val).astype(bf16)

        z = ALPHA * x1 + _dot(f_ref[...], wdown_ref[...])
        mu = jnp.mean(z, axis=-1, keepdims=True)
        zc = z - mu
        rstd2 = lax.rsqrt(jnp.mean(zc * zc, axis=-1, keepdims=True) + LN_EPS)
        xh2 = zc * rstd2
        diff = xh2 * g2_ref[...] + b2_ref[...] - tgt_ref[...]
        loss_ref[...] += 0.5 * jnp.sum(diff * diff) / D
        dy = diff * (1.0 / D)
        dg2_ref[...] += jnp.sum(dy * xh2, axis=0, keepdims=True)
        db2_ref[...] += jnp.sum(dy, axis=0, keepdims=True)
        dyg = dy * g2_ref[...]
        dz2 = rstd2 * (dyg - jnp.mean(dyg, axis=-1, keepdims=True) - xh2 * jnp.mean(dyg * xh2, axis=-1, keepdims=True))
        dz2b = dz2.astype(bf16)
        dz2b_ref[...] = dz2b

        for lo, w in chunks:
            cs = slice(lo, lo + w)
            df = _dot(dz2b, wdown_ref[pl.ds(lo, w), :], NT)
            dval = df * rd(dhext_s, 0, lo, w)
            dh = df * val_s[:, cs]
            wr(dhext_s, lo, dh)
            dh1, dh2, g0 = rd(dhext_s, 1, lo, w), rd(dhext_s, 2, lo, w), rd(gext_s, FH, lo, w)
            dcb_ref[:, cs] += jnp.sum(dh, axis=0, keepdims=True)
            dcw_ref[0:1, cs] += jnp.sum(dh2 * g0, axis=0, keepdims=True)
            dcw_ref[1:2, cs] += jnp.sum(dh1 * g0, axis=0, keepdims=True)
            dcw_ref[2:3, cs] += jnp.sum(dh * g0, axis=0, keepdims=True)
            dgate = cw_ref[2:3, cs] * dh + cw_ref[1:2, cs] * dh1 + cw_ref[0:1, cs] * dh2
            du_ref[:, cs] = dval.astype(bf16)
            du_ref[:, D_FF + lo: D_FF + lo + w] = dgate.astype(bf16)
        dhext_s[:, pl.ds(tt, 8), :] = dhext_s[:, pl.ds(0, 8), :]
        dx1 = ALPHA * dz2 + _dot(du_ref[...], wupt_ref[...])

        dg1_ref[...] += jnp.sum(dx1 * xhat, axis=0, keepdims=True)
        db1_ref[...] += jnp.sum(dx1, axis=0, keepdims=True)
        dxg = dx1 * g1
        dz1_ref[...] = rstd_ref[...] * (dxg - jnp.mean(dxg, axis=-1, keepdims=True)
                                        - xhat * jnp.mean(dxg * xhat, axis=-1, keepdims=True))

    rtile = lambda w: pl.BlockSpec((tt, w), lambda i: (n_tiles - 1 - i, 0))
    acc = lambda shape: pl.BlockSpec(shape, lambda i: (0, 0))
    out_shape = (
        jax.ShapeDtypeStruct((T, D), f32),
        jax.ShapeDtypeStruct((T, D), bf16),
        jax.ShapeDtypeStruct((T, 2 * D_FF), bf16),
        jax.ShapeDtypeStruct((T, D_FF), bf16),
        jax.ShapeDtypeStruct((8, 128), f32),
        jax.ShapeDtypeStruct((1, D), f32), jax.ShapeDtypeStruct((1, D), f32),
        jax.ShapeDtypeStruct((1, D), f32), jax.ShapeDtypeStruct((1, D), f32),
        jax.ShapeDtypeStruct((1, D_FF), f32), jax.ShapeDtypeStruct((3, D_FF), f32),
    )
    return pl.pallas_call(
        body, name="ffn_forward_backward", grid=(n_tiles,), out_shape=out_shape,
        in_specs=[rtile(D),
                  pl.BlockSpec((FH, D), lambda i: (jnp.maximum((n_tiles - 1 - i) * hb - 1, 0), 0)),
                  rtile(1), _const_spec((1, D)), _const_spec((1, D)), _const_spec((2 * D_FF, D)),
                  _const_spec((3, D_FF)), _const_spec((1, D_FF)), _const_spec((D_FF, D)),
                  _const_spec((1, D)), _const_spec((1, D)), rtile(D)],
        out_specs=(rtile(D), rtile(D), rtile(2 * D_FF), rtile(D_FF), acc((8, 128)),
                   acc((1, D)), acc((1, D)), acc((1, D)), acc((1, D)), acc((1, D_FF)), acc((3, D_FF))),
        scratch_shapes=[pltpu.VMEM((D_FF // 128, tt + FH, 128), f32), pltpu.VMEM((tt, D_FF), f32),
                        pltpu.VMEM((D_FF // 128, tt + 8, 128), f32)],
        compiler_params=pltpu.CompilerParams(dimension_semantics=("arbitrary",), vmem_limit_bytes=V7X_VMEM_LIMIT),
    )(xhat1, xhat1, rstd1, ln1_g, ln1_b, w_up_t, conv_w, conv_b, w_down, ln2_g, ln2_b, target)


def _mix_backward(dz1, w_out, qkv, g, oret, states, pooled, cat, cos, sin, dmat, qd, kd, cdec, w_pool, pool_scale, w_in_t,
                  after, tt=MIX_TILE):
    n_tiles = T // tt

    def body(dz1_ref, wout_ref, qkv_ref, g_ref, oret_ref, states_ref, pooled_ref, cat_ref, cos_ref, sin_ref, dmat_ref,
             qd_ref, kd_ref, wpool_ref, pscale_ref, wint_ref, after_ref,
             dproj_ref, gx_ref, dwpool_ref, dpscale_ref, dwout_ref, dstate_s, dout_s, eext_s, tmp_s, dwout_s):
        i = pl.program_id(0)
        tile_idx = n_tiles - 1 - i

        @pl.when(i == 0)
        def _():
            dstate_s[...] = jnp.zeros_like(dstate_s)
            dwpool_ref[...] = jnp.zeros_like(dwpool_ref)
            dpscale_ref[...] = jnp.zeros_like(dpscale_ref)
            dwout_s[...] = jnp.zeros_like(dwout_s)
            eext_s[:, pl.ds(tt, HALO), :] = jnp.zeros((GROUPS, HALO, DH), f32)

        dz1 = dz1_ref[...]
        dz1b = dz1.astype(bf16)
        dcat = _dot(dz1b, wout_ref[...], NT)
        dwout_s[...] += _dot(cat_ref[...], dz1b, TN)

        pos1 = (tile_idx * tt + lax.broadcasted_iota(jnp.int32, (tt, 1), 0) + 1).astype(f32)
        for gi, w in enumerate(WINDOWS):
            sl = slice(gi * DH, (gi + 1) * DH)
            dpo = dcat[:, RW + gi * DH: RW + (gi + 1) * DH]
            pooled_g = pooled_ref[:, sl]
            ylin = _dot(pooled_g, wpool_ref[gi])
            dpscale_ref[:, sl] += jnp.sum(dpo * ylin, axis=0, keepdims=True)
            dpw = (dpo * pscale_ref[:, sl]).astype(bf16)
            dwpool_ref[gi] += _dot(pooled_g, dpw, TN)
            dpooled = _dot(dpw, wpool_ref[gi], NT)
            eext_s[gi, pl.ds(0, tt), :] = dpooled / jnp.minimum(pos1, float(w))
            stages = int(math.log2(w))
            src = eext_s
            for s in range(stages):
                n = tt + 8 * (stages - 1 - s)
                shift = 2 ** s
                val = src[gi, pl.ds(0, n), :] + src[gi, pl.ds(shift, n), :]
                if s == stages - 1:
                    wsum = val
                else:
                    tmp_s[gi, pl.ds(0, n), :] = val
                    src = tmp_s
            dproj_ref[:, 4 * RW + gi * DH: 4 * RW + (gi + 1) * DH] = (wsum - dpooled).astype(bf16)
        eext_s[:, pl.ds(tt, HALO), :] = eext_s[:, pl.ds(0, HALO), :]

        for h in range(HEADS):
            sl = slice(h * DH, (h + 1) * DH)
            dr = dcat[:, sl]
            o = oret_ref[:, sl]
            r = lax.rsqrt(jnp.mean(o * o, axis=-1, keepdims=True) + RMS_EPS)
            rn = o * r
            gg = g_ref[:, sl]
            sg = _sigmoid(gg)
            dproj_ref[:, 3 * RW + h * DH: 3 * RW + (h + 1) * DH] = (dr * rn * (sg * (1.0 + gg * (1.0 - sg)))).astype(bf16)
            drn = dr * (gg * sg)
            dout_s[:, sl] = (r * (drn - rn * jnp.mean(drn * rn, axis=-1, keepdims=True))).astype(bf16)

        for sub in reversed(range(tt // RET_TILE)):
            rows = pl.ds(sub * RET_TILE, RET_TILE)
            cos_t, sin_t = cos_ref[rows, :], sin_ref[rows, :]
            for h in range(HEADS):
                q = qkv_ref[rows, h * DH:(h + 1) * DH]
                k = qkv_ref[rows, RW + h * DH: RW + (h + 1) * DH]
                v = qkv_ref[rows, 2 * RW + h * DH: 2 * RW + (h + 1) * DH]
                do = dout_s[rows, h * DH:(h + 1) * DH]
                stb = states_ref[sub, h]
                dst = dstate_s[h]
                dstb = dst.astype(bf16)
                sb = (_dot(q, k, NT) * dmat_ref[h]).astype(bf16)
                dsb = (_dot(do, v, NT) * dmat_ref[h]).astype(bf16)
                dq = _dot(dsb, k) + _dot(do, stb, NT) * qd_ref[h]
                dk = _dot(dsb, q, TN) + _dot(v, dstb, NT) * kd_ref[h]
                dv = _dot(sb, do, TN) + _dot((k.astype(f32) * kd_ref[h]).astype(bf16), dstb)
                dstate_s[h] = dst * cdec[h] + _dot((q.astype(f32) * qd_ref[h]).astype(bf16), do, TN)
                dproj_ref[rows, h * DH:(h + 1) * DH] = (dq * cos_t - _swap_halves(dq) * sin_t).astype(bf16)
                dproj_ref[rows, RW + h * DH: RW + (h + 1) * DH] = (
                    (dk * cos_t - _swap_halves(dk) * sin_t) * K_SCALE).astype(bf16)
                dproj_ref[rows, 2 * RW + h * DH: 2 * RW + (h + 1) * DH] = dv.astype(bf16)

        gx_ref[...] = ALPHA * dz1 + _dot(dproj_ref[...], wint_ref[...])

        @pl.when(i == n_tiles - 1)
        def _():
            dwout_ref[...] = dwout_s[...].astype(bf16)

    rtile = lambda w: pl.BlockSpec((tt, w), lambda i: (n_tiles - 1 - i, 0))
    out_shape = (
        jax.ShapeDtypeStruct((T, IN_W), bf16),
        jax.ShapeDtypeStruct((T, D), f32),
        jax.ShapeDtypeStruct((GROUPS, DH, DH), f32),
        jax.ShapeDtypeStruct((1, PW), f32),
        jax.ShapeDtypeStruct((D, D), bf16),
    )
    return pl.pallas_call(
        body, name="mix_backward", grid=(n_tiles,), out_shape=out_shape,
        in_specs=[rtile(D), _const_spec((D, D)), rtile(3 * RW), rtile(RW), rtile(RW),
                  pl.BlockSpec((tt // RET_TILE, HEADS, DH, DH), lambda i: (n_tiles - 1 - i, 0, 0, 0)),
                  rtile(PW), rtile(D), rtile(DH), rtile(DH),
                  _const_spec((HEADS, RET_TILE, RET_TILE)), _const_spec((HEADS, RET_TILE, DH)),
                  _const_spec((HEADS, RET_TILE, DH)),
                  _const_spec((GROUPS, DH, DH)), _const_spec((1, PW)), _const_spec((IN_W, D)),
                  pl.BlockSpec(memory_space=pl.ANY)],
        out_specs=(rtile(IN_W), rtile(D), pl.BlockSpec((GROUPS, DH, DH), lambda i: (0, 0, 0)),
                   pl.BlockSpec((1, PW), lambda i: (0, 0)),
                   pl.BlockSpec((D, D), lambda i: (0, 0), pipeline_mode=pl.Buffered(1))),
        scratch_shapes=[pltpu.VMEM((HEADS, DH, DH), f32), pltpu.VMEM((tt, RW), bf16),
                        pltpu.VMEM((GROUPS, tt + HALO, DH), f32), pltpu.VMEM((GROUPS, tt + HALO, DH), f32),
                        pltpu.VMEM((D, D), f32)],
        compiler_params=pltpu.CompilerParams(dimension_semantics=("arbitrary",), vmem_limit_bytes=V7X_VMEM_LIMIT),
    )(dz1, w_out, qkv, g, oret, states, pooled, cat, cos, sin, dmat, qd, kd, w_pool, pool_scale, w_in_t, after)


def _weight_grad(a, b, name, tm, exchange=(), tk=2048):
    m = a.shape[1]
    n_m, n_k, n_e = m // tm, T // tk, len(exchange)

    def body(a_ref, b_ref, *rest):
        ein, o_ref, eout, (acc_s, *sems) = rest[:n_e], rest[n_e], rest[n_e + 1:2 * n_e + 1], rest[2 * n_e + 1:]
        i, k = pl.program_id(0), pl.program_id(1)

        if n_e:
            @pl.when((i == 0) & (k == 0))
            def _():
                _chip_exchange_start(ein, eout, *sems)

        @pl.when(k == 0)
        def _():
            acc_s[...] = jnp.zeros_like(acc_s)

        acc_s[...] += _dot(a_ref[...], b_ref[pl.ds(pl.multiple_of(k * tk, tk), tk), :].astype(bf16), TN)

        @pl.when(k == n_k - 1)
        def _():
            o_ref[...] = acc_s[...].astype(bf16)

        if n_e:
            @pl.when((i == n_m - 1) & (k == n_k - 1))
            def _():
                _chip_exchange_finish(ein, eout, *sems)

    hbm = pl.BlockSpec(memory_space=pltpu.HBM)
    return pl.pallas_call(
        body, name=name, grid=(n_m, n_k),
        out_shape=(jax.ShapeDtypeStruct((m, D), bf16),) + tuple(jax.ShapeDtypeStruct(e.shape, e.dtype) for e in exchange),
        in_specs=[pl.BlockSpec((tk, tm), lambda i, k: (k, i)),
                  pl.BlockSpec((T, D), lambda i, k: (0, 0), pipeline_mode=pl.Buffered(1))] + [hbm] * n_e,
        out_specs=(pl.BlockSpec((tm, D), lambda i, k: (i, 0)),) + (hbm,) * n_e,
        scratch_shapes=[pltpu.VMEM((tm, D), f32)] + _chip_exchange_sems(n_e),
        compiler_params=pltpu.CompilerParams(dimension_semantics=("arbitrary", "arbitrary"),
                                             vmem_limit_bytes=V7X_VMEM_LIMIT,
                                             collective_id=CHIP_BARRIER if n_e else None),
    )(a, b, *exchange)


CHIP_FLIPS = ((1, 0), (0, 1), (1, 1))
PAIR_BARRIER, CHIP_BARRIER, GATHER_BARRIER, CHIP_BARRIER_SPLIT = 0, 1, 2, 3


def _barrier(peers):
    sem = pltpu.get_barrier_semaphore()
    for peer in peers:
        pl.semaphore_signal(sem, inc=1, device_id=peer, device_id_type=pl.DeviceIdType.MESH)
    pl.semaphore_wait(sem, len(peers))


def _me():
    return lax.axis_index("x"), lax.axis_index("y"), lax.axis_index("c")


def _chip(me, k):
    x, y, _ = me
    if k == 0:
        return x, y
    fx, fy = CHIP_FLIPS[k - 1]
    return (1 - x if fx else x), (1 - y if fy else y)


def _slot(x, y, c):
    return 4 * x + 2 * y + c


def _remote(src, dst, send_sem, recv_sem, to):
    return pltpu.make_async_remote_copy(src_ref=src, dst_ref=dst, send_sem=send_sem, recv_sem=recv_sem,
                                        device_id=to, device_id_type=pl.DeviceIdType.MESH)


def _gather_sems(n):
    return [pltpu.SemaphoreType.DMA((7, n)), pltpu.SemaphoreType.DMA((7, n)), pltpu.SemaphoreType.DMA((n,))] if n else []


def _gather_copy(k, j, gin, gout, send_sems, recv_sems, sending):
    x, y, c = _me()
    sibling, x_chip, y_chip, d_chip = (x, y, 1 - c), (1 - x, y), (x, 1 - y), (1 - x, 1 - y)
    south = c == 0
    passed_on = (jnp.where(south, 1 - x, x), jnp.where(south, y, 1 - y), c)
    src, to = gin[j], sibling
    if sending:
        block = {0: (x, y, c), 1: (x, y, c), 2: (x, y, c), 3: passed_on, 4: (*x_chip, c), 5: (*y_chip, c), 6: (*d_chip, c)}[k]
        to = {1: (*x_chip, c), 2: (*y_chip, c), 3: (jnp.where(south, x, 1 - x), jnp.where(south, 1 - y, y), c)}.get(k, sibling)
        if k >= 3:
            src = gout[j].at[_slot(*block)]
    else:
        block = {0: sibling, 1: (*x_chip, c), 2: (*y_chip, c), 3: (*d_chip, c), 4: (*x_chip, 1 - c), 5: (*y_chip, 1 - c),
                 6: (*d_chip, 1 - c)}[k]
    return _remote(src, gout[j].at[_slot(*block)], send_sems.at[k, j], recv_sems.at[k, j], to)


def _gather_do(ks, action, gin, gout, send_sems, recv_sems):
    for k in ks:
        for j in range(len(gin)):
            cp = _gather_copy(k, j, gin, gout, send_sems, recv_sems, action != "wait_recv")
            getattr(cp, action)()


def _gather_peers():
    x, y, c = _me()
    return [(x, y, 1 - c), (1 - x, y, c), (x, 1 - y, c)]


def _gather_start(gin, gout, send_sems, recv_sems, local_sems, barrier=True):
    if barrier:
        _barrier(_gather_peers())
    for j in range(len(gin)):
        pltpu.make_async_copy(gin[j], gout[j].at[_slot(*_me())], local_sems.at[j]).start()
    _gather_do((0, 1, 2), "start", gin, gout, send_sems, recv_sems)


def _gather_forward(gin, gout, send_sems, recv_sems, local_sems):
    _gather_do((1, 2), "wait_recv", gin, gout, send_sems, recv_sems)
    _gather_do((3, 4, 5), "start", gin, gout, send_sems, recv_sems)


def _gather_finish(gin, gout, send_sems, recv_sems, local_sems):
    _gather_do((3,), "wait_recv", gin, gout, send_sems, recv_sems)
    _gather_do((6,), "start", gin, gout, send_sems, recv_sems)
    _gather_do((0, 4, 5, 6), "wait_recv", gin, gout, send_sems, recv_sems)
    _gather_do(range(7), "wait_send", gin, gout, send_sems, recv_sems)
    for j in range(len(gin)):
        pltpu.make_async_copy(gin[j], gout[j].at[_slot(*_me())], local_sems.at[j]).wait()


def _all_gather(blocks, name):
    n = len(blocks)

    def body(*refs):
        gin, gout, sems = refs[:n], refs[n:2 * n], refs[2 * n:]
        _gather_start(gin, gout, *sems)
        _gather_forward(gin, gout, *sems)
        _gather_finish(gin, gout, *sems)

    hbm = pl.BlockSpec(memory_space=pltpu.HBM)
    return pl.pallas_call(
        body, name=name,
        out_shape=tuple(jax.ShapeDtypeStruct((N_DEV,) + b.shape, b.dtype) for b in blocks),
        in_specs=[hbm] * n, out_specs=(hbm,) * n, scratch_shapes=_gather_sems(n),
        compiler_params=pltpu.CompilerParams(collective_id=GATHER_BARRIER),
    )(*blocks)


def _pair_reduce(parts, name):
    n = len(parts)

    def body(*refs):
        ins, own, others, landing, mine = (refs[k * n:(k + 1) * n] for k in range(5))
        send_sems, recv_sems, local_sems = refs[5 * n:]
        me = _me()
        x, y, c = me
        sibling = (x, y, 1 - c)
        _barrier([sibling])
        sends, loads = [], []
        for k in range(4):
            for j in range(n):
                cp = _remote(ins[j].at[_slot(*_chip(me, k), 1 - c)], landing[j].at[k], send_sems.at[k, j],
                             recv_sems.at[k, j], sibling)
                cp.start()
                sends.append(cp)
                ld = pltpu.make_async_copy(ins[j].at[_slot(*_chip(me, k), c)], mine[j].at[k], local_sems.at[k, j])
                ld.start()
                loads.append(ld)
        for k in range(4):
            for j in range(n):
                loads[k * n + j].wait()
                _remote(ins[j].at[0], landing[j].at[k], send_sems.at[k, j], recv_sems.at[k, j], sibling).wait_recv()
                total = mine[j][k].astype(f32) + landing[j][k].astype(f32)
                if k == 0:
                    own[j][...] = total.astype(own[j].dtype)
                else:
                    others[j][k - 1] = total.astype(others[j].dtype)
        for cp in sends:
            cp.wait_send()

    vm = pl.BlockSpec(memory_space=pltpu.VMEM)
    return pl.pallas_call(
        body, name=name,
        out_shape=tuple(jax.ShapeDtypeStruct(p.shape[1:], p.dtype) for p in parts)
        + tuple(jax.ShapeDtypeStruct((3,) + p.shape[1:], p.dtype) for p in parts),
        in_specs=[pl.BlockSpec(memory_space=pltpu.HBM)] * n, out_specs=(vm,) * (2 * n),
        scratch_shapes=[pltpu.VMEM((4,) + p.shape[1:], p.dtype) for p in parts] * 2
        + [pltpu.SemaphoreType.DMA((4, n)), pltpu.SemaphoreType.DMA((4, n)), pltpu.SemaphoreType.DMA((4, n))],
        compiler_params=pltpu.CompilerParams(vmem_limit_bytes=V7X_VMEM_LIMIT, collective_id=PAIR_BARRIER),
    )(*parts)


def _chip_exchange_sems(n):
    return [pltpu.SemaphoreType.DMA((3, n)), pltpu.SemaphoreType.DMA((3, n))] if n else []


def _chip_exchange_copy(k, j, ein, eout, send_sems, recv_sems):
    me = _me()
    return _remote(ein[j].at[k - 1], eout[j].at[k - 1], send_sems.at[k - 1, j], recv_sems.at[k - 1, j],
                   (*_chip(me, k), me[2]))


def _chip_peers():
    me = _me()
    return [(*_chip(me, k), me[2]) for k in range(1, 4)]


def _chip_exchange_start(ein, eout, send_sems, recv_sems, barrier=True):
    if barrier:
        _barrier(_chip_peers())
    for k in range(1, 4):
        for j in range(len(ein)):
            _chip_exchange_copy(k, j, ein, eout, send_sems, recv_sems).start()


def _chip_exchange_finish(ein, eout, send_sems, recv_sems):
    for k in range(1, 4):
        for j in range(len(ein)):
            _chip_exchange_copy(k, j, ein, eout, send_sems, recv_sems).wait_recv()
    for k in range(1, 4):
        for j in range(len(ein)):
            _chip_exchange_copy(k, j, ein, eout, send_sems, recv_sems).wait_send()


def _split_copies(src_ref, dst_ref, sems):
    me = _me()
    return [_remote(src_ref.at[k - 1], dst_ref.at[k - 1], sems[k - 1], sems[2 + k], (*_chip(me, k), me[2]))
            for k in range(1, 4)]


def _exchange_start(others, name, barrier_id):
    def body(src_ref, land_ref, *rest):
        sems, token_ref = rest[:6], rest[8]
        _barrier(_chip_peers())
        for copy in _split_copies(src_ref, land_ref, sems):
            copy.start()
        token_ref[...] = jnp.zeros_like(token_ref)

    hbm, sem = pl.BlockSpec(memory_space=pltpu.HBM), pl.BlockSpec(memory_space=pltpu.SEMAPHORE)
    thru = pltpu.HBM(others.shape, others.dtype)
    res = pl.pallas_call(
        body, name=name,
        out_shape=(pltpu.SemaphoreType.DMA(()),) * 6 + (thru, thru, jax.ShapeDtypeStruct((8, 128), f32)),
        in_specs=(hbm, hbm), out_specs=(sem,) * 6 + (hbm, hbm, pl.BlockSpec(memory_space=pltpu.VMEM)),
        input_output_aliases={0: 6, 1: 7},
        compiler_params=pltpu.CompilerParams(has_side_effects=pltpu.SideEffectType.DATAFLOW_SIDE_EFFECTING,
                                             collective_id=barrier_id),
    )(pltpu.with_memory_space_constraint(others, pltpu.HBM),
      pltpu.with_memory_space_constraint(lax.empty(others.shape, others.dtype), pltpu.HBM))
    return res[:6], res[6], res[7], res[8]


def _exchange_wait(sems, src_thru, land_thru, after, name):
    n_after = len(after)

    def body(src_ref, land_ref, *rest):
        for copy in _split_copies(src_ref, land_ref, rest[:6]):
            copy.wait_send()
            copy.wait_recv()

    hbm, sem = pl.BlockSpec(memory_space=pltpu.HBM), pl.BlockSpec(memory_space=pltpu.SEMAPHORE)
    thru = pltpu.HBM(src_thru.shape, src_thru.dtype)
    return pl.pallas_call(
        body, name=name, out_shape=(thru, thru),
        in_specs=(hbm, hbm) + (sem,) * 6 + (pl.BlockSpec(memory_space=pl.ANY),) * n_after, out_specs=(hbm, hbm),
        input_output_aliases={0: 0, 1: 1},
        compiler_params=pltpu.CompilerParams(has_side_effects=pltpu.SideEffectType.DATAFLOW_SIDE_EFFECTING),
    )(src_thru, land_thru, *sems, *after)[1]


def _sum_parts(owns, arrived, name):
    n = len(owns)

    def body(*refs):
        for own, arr, out in zip(refs[:n], refs[n:2 * n], refs[2 * n:]):
            acc = own[...].astype(f32)
            for k in range(3):
                acc = acc + arr[k].astype(f32)
            out[...] = acc

    vm = pl.BlockSpec(memory_space=pltpu.VMEM)
    return pl.pallas_call(
        body, name=name, out_shape=tuple(jax.ShapeDtypeStruct(o.shape, f32) for o in owns),
        in_specs=[vm] * (2 * n), out_specs=(vm,) * n,
        compiler_params=pltpu.CompilerParams(vmem_limit_bytes=V7X_VMEM_LIMIT),
    )(*owns, *arrived)


def _adam_update(w, g, m, v):
    m = ADAM_B1 * m + (1.0 - ADAM_B1) * g
    v = ADAM_B2 * v + (1.0 - ADAM_B2) * (g * g)
    m_hat = m / (1.0 - ADAM_B1 ** ADAM_STEP)
    v_hat = v / (1.0 - ADAM_B2 ** ADAM_STEP)
    return -ADAM_LR * (m_hat / (jnp.sqrt(v_hat) + ADAM_EPS) + ADAM_WD * w), m, v


def _sum_adamw(own, arrived, w, m, v, name, steps, after=()):
    rows = own.shape[0]
    br = rows // steps

    def body(own_ref, arr_ref, w_ref, m_ref, v_ref, *rest):
        g_out, d_out, m_out, v_out = rest[len(after):]
        g = own_ref[...].astype(f32)
        for k in range(3):
            g = g + arr_ref[k].astype(f32)
        g_out[...] = g
        d_out[...], m_out[...], v_out[...] = _adam_update(w_ref[...], g, m_ref[...], v_ref[...])

    blk = pl.BlockSpec((br, D), lambda i: (i, 0))
    return pl.pallas_call(
        body, name=name, grid=(steps,), out_shape=(jax.ShapeDtypeStruct((rows, D), f32),) * 4,
        in_specs=[blk, pl.BlockSpec((3, br, D), lambda i: (0, i, 0)), blk, blk, blk]
        + [pl.BlockSpec(memory_space=pl.ANY)] * len(after), out_specs=(blk,) * 4,
        compiler_params=pltpu.CompilerParams(dimension_semantics=("parallel",), vmem_limit_bytes=V7X_VMEM_LIMIT),
    )(own, arrived, w, m, v, *after)


def _adamw(ws, gs, ms, vs, name):
    n = len(ws)

    def body(*refs):
        w_r, g_r, m_r, v_r = (refs[k * n:(k + 1) * n] for k in range(4))
        d_o, m_o, v_o = (refs[(4 + k) * n:(5 + k) * n] for k in range(3))
        for j in range(n):
            d_o[j][...], m_o[j][...], v_o[j][...] = _adam_update(w_r[j][...], g_r[j][...], m_r[j][...], v_r[j][...])

    vm = pl.BlockSpec(memory_space=pltpu.VMEM)
    shapes = tuple(jax.ShapeDtypeStruct(w.shape, f32) for w in ws)
    return pl.pallas_call(
        body, name=name, out_shape=shapes * 3, in_specs=[vm] * (4 * n), out_specs=tuple([vm] * (3 * n)),
        compiler_params=pltpu.CompilerParams(vmem_limit_bytes=V7X_VMEM_LIMIT),
    )(*ws, *gs, *ms, *vs)


SMALL = (("w_pool", GROUPS * DH * DH), ("pool_scale", PW), ("ln1_g", D), ("ln1_b", D), ("conv_b", D_FF),
         ("ln2_g", D), ("ln2_b", D), ("conv_w", 3 * D_FF), ("loss", 1))
SMALL_ROWS = 640


def _pack(named):
    flat = jnp.concatenate([named[k].reshape(-1) for k, _ in SMALL])
    return jnp.pad(flat, (0, SMALL_ROWS * 128 - flat.shape[0])).reshape(SMALL_ROWS, 128)


def _unpack(packed):
    flat, out, at = packed.reshape(-1), {}, 0
    for k, size in SMALL:
        out[k] = flat[at:at + size]
        at += size
    return out


def kernel(x, w_in, w_pool, pool_scale, w_out, ln1_g, ln1_b, w_up, conv_w, conv_b, w_down, ln2_g, ln2_b, loss_target, m_w_in, m_w_pool, m_pool_scale, m_w_out, m_ln1_g, m_ln1_b, m_w_up, m_conv_w, m_conv_b, m_w_down, m_ln2_g, m_ln2_b, v_w_in, v_w_pool, v_pool_scale, v_w_out, v_ln1_g, v_ln1_b, v_w_up, v_conv_w, v_conv_b, v_w_down, v_ln2_g, v_ln2_b):
    me = 4 * lax.axis_index("x") + 2 * lax.axis_index("y") + lax.axis_index("c")
    x2, tgt = x[0], loss_target[0]

    g_in, g_out, g_cw = _all_gather([w_in[0].T.astype(bf16), w_out[0].astype(bf16), jnp.transpose(conv_w, (1, 0, 2))],
                                    "gather_weights")
    w_in_t = g_in.reshape(IN_W, D)
    w_out_f = g_out.reshape(D, D)
    conv_w_f = jnp.transpose(g_cw[:, :, 0, :], (1, 0, 2)).reshape(3, D_FF)
    w_pool_b = w_pool[0].astype(bf16)

    cos, sin = _rope_tables()
    dmat, qd, kd, cdec = _decay_tables(RET_TILE)

    qkv, g, oret, states, cat, pooled, xhat1, rstd1, x1b, g_up, g_down = _mix_forward(
        x2, w_in_t, cos, sin, dmat, qd, kd, cdec, w_pool_b, pool_scale, w_out_f, ln1_g, ln1_b,
        gather=[w_up[0].T.astype(bf16), w_down[0].astype(bf16)])
    w_up_t = g_up.reshape(2 * D_FF, D)
    w_down_f = g_down.reshape(D_FF, D)
    dz1, dz2b, du, f, loss8, d_ln2_g, d_ln2_b, d_ln1_g, d_ln1_b, d_conv_b, d_conv_w = _ffn_forward_backward(
        xhat1, rstd1, ln1_g, ln1_b, w_up_t, conv_w_f, conv_b, w_down_f, ln2_g, ln2_b, tgt)

    (dw_down,) = _weight_grad(f, dz2b, "grad_w_down", tm=D_FF // 2)
    own_down, oth_down = _pair_reduce([dw_down.reshape(N_DEV, ROWS_DOWN, D)], "pair_reduce_down")
    dw_up_t, arr_down = _weight_grad(du, x1b, "grad_w_up", tm=D_FF // 2, exchange=[oth_down])
    own_up, oth_up = _pair_reduce([dw_up_t.reshape(N_DEV, ROWS_UP, D)], "pair_reduce_up")
    up_sems, up_src, up_land, up_started = _exchange_start(oth_up, "exchange_up_start", CHIP_BARRIER_SPLIT)
    dproj, grad_x, d_w_pool, d_pool_scale, dw_out = _mix_backward(
        dz1, w_out_f, qkv, g, oret, states, pooled, cat, cos, sin, dmat, qd, kd, cdec, w_pool_b, pool_scale, w_in_t,
        after=up_started)
    small = _pack({"w_pool": d_w_pool, "pool_scale": d_pool_scale, "ln1_g": d_ln1_g, "ln1_b": d_ln1_b,
                   "conv_b": d_conv_b, "ln2_g": d_ln2_g, "ln2_b": d_ln2_b, "conv_w": d_conv_w, "loss": loss8[0, :1]})
    own_out, own_small, oth_out, oth_small = _pair_reduce(
        [dw_out.reshape(N_DEV, ROWS_OUT, D), small.reshape(N_DEV, SMALL_ROWS // N_DEV, 128)], "pair_reduce_out")
    dw_in_t, arr_out, arr_small = _weight_grad(dproj, x2, "grad_w_in", tm=IN_W // 2, exchange=[oth_out, oth_small])
    arr_up = _exchange_wait(up_sems, up_src, up_land, [dw_in_t], "exchange_up_wait")
    own_in, oth_in = _pair_reduce([dw_in_t.reshape(N_DEV, ROWS_IN, D)], "pair_reduce_in")
    in_sems, in_src, in_land, started = _exchange_start(oth_in, "exchange_in_start", CHIP_BARRIER)
    (small_piece,) = _sum_parts([own_small], [arr_small], "sum_small_grads")
    (gs_small,) = _all_gather([small_piece], "gather_small_grads")

    names = ["w_in", "w_pool", "pool_scale", "w_out", "ln1_g", "ln1_b", "w_up", "conv_w", "conv_b", "w_down",
             "ln2_g", "ln2_b"]
    w_d = dict(w_in=w_in, w_pool=w_pool, pool_scale=pool_scale, w_out=w_out, ln1_g=ln1_g, ln1_b=ln1_b, w_up=w_up,
               conv_w=conv_w, conv_b=conv_b, w_down=w_down, ln2_g=ln2_g, ln2_b=ln2_b)
    m_d = dict(w_in=m_w_in, w_pool=m_w_pool, pool_scale=m_pool_scale, w_out=m_w_out, ln1_g=m_ln1_g, ln1_b=m_ln1_b,
               w_up=m_w_up, conv_w=m_conv_w, conv_b=m_conv_b, w_down=m_w_down, ln2_g=m_ln2_g, ln2_b=m_ln2_b)
    v_d = dict(w_in=v_w_in, w_pool=v_w_pool, pool_scale=v_pool_scale, w_out=v_w_out, ln1_g=v_ln1_g, ln1_b=v_ln1_b,
               w_up=v_w_up, conv_w=v_conv_w, conv_b=v_conv_b, w_down=v_w_down, ln2_g=v_ln2_g, ln2_b=v_ln2_b)
    g_d, delta, new_m, new_v = {}, {}, {}, {}

    def big_adamw(k, own, arr, transposed, steps, after=()):
        lay = (lambda a: a[0].T) if transposed else (lambda a: a[0])
        back = (lambda a: a.T[None]) if transposed else (lambda a: a[None])
        res = _sum_adamw(own, arr, lay(w_d[k]), lay(m_d[k]), lay(v_d[k]), "adamw_" + k, steps, after)
        g_d[k], delta[k], new_m[k], new_v[k] = (back(r) for r in res)
        return res[3]

    done = [big_adamw("w_up", own_up, arr_up, True, 4, after=(started,)),
            big_adamw("w_down", own_down, arr_down, False, 2, after=(started,)),
            big_adamw("w_out", own_out, arr_out, False, 2, after=(started,))]

    gsm = _unpack(gs_small)
    gsm["conv_w"] = lax.dynamic_slice(gsm["conv_w"].reshape(3, D_FF), (0, me * (D_FF // N_DEV)), (3, D_FF // N_DEV))
    lay = lambda k, a: jnp.transpose(a, (1, 0, 2)) if k == "conv_w" else a.reshape(-1, a.shape[-1])
    back = lambda k, a: jnp.transpose(a, (1, 0, 2)) if k == "conv_w" else a.reshape(w_d[k].shape)
    group = [k for k in names if k not in ("w_in", "w_out", "w_up", "w_down")]
    for k in group:
        g_d[k] = gsm[k].reshape(w_d[k].shape)
    res = _adamw([lay(k, w_d[k]) for k in group], [lay(k, g_d[k]) for k in group], [lay(k, m_d[k]) for k in group],
                 [lay(k, v_d[k]) for k in group], "adamw_small")
    for j, k in enumerate(group):
        delta[k], new_m[k], new_v[k] = (back(k, res[part * len(group) + j]) for part in range(3))

    arr_in = _exchange_wait(in_sems, in_src, in_land, done + [res[0]], "exchange_in_wait")
    big_adamw("w_in", own_in, arr_in, True, 4)

    loss = gsm["loss"].reshape(())
    return (loss, grad_x[None], *[g_d[k] for k in names], *[delta[k] for k in names], *[new_m[k] for k in names],
            *[new_v[k] for k in names])
```

```python
import math

import numpy as np
import jax
import jax.numpy as jnp
from jax import lax
from jax.experimental import pallas as pl
from jax.experimental.pallas import tpu as pltpu

f32 = jnp.float32
bf16 = jnp.bfloat16

N_DEV = 8
T = 4096
D = 1024
CHUNK = 64
MIX_TILE = 512
RET_TILE = 256
HEADS = 4
DH = 128
RW = HEADS * DH
PW = 512
GROUPS = 4
WINDOWS = (2, 4, 8, 16)
IN_W = 4 * RW + PW
D_FF = 2816
LN_EPS = 1e-5
RMS_EPS = 1e-6
ALPHA = 2.0 ** 0.25
K_SCALE = DH ** -0.5

ADAM_LR = 0.001
ADAM_B1 = 0.9
ADAM_B2 = 0.999
ADAM_EPS = 1e-08
ADAM_WD = 0.01
ADAM_STEP = 10

ROWS_IN, ROWS_OUT, ROWS_UP, ROWS_DOWN = IN_W // N_DEV, D // N_DEV, 2 * D_FF // N_DEV, D_FF // N_DEV

V7X_VMEM_LIMIT = 56 * 2 ** 20
HALO = 32

NT = (((1,), (1,)), ((), ()))
TN = (((0,), (0,)), ((), ()))
NN = (((1,), (0,)), ((), ()))


def _dot(a, b, dims=NN):
    return lax.dot_general(a, b, dims, preferred_element_type=f32)


def _const_spec(shape):
    zeros = (0,) * len(shape)
    return pl.BlockSpec(shape, lambda i: zeros, pipeline_mode=pl.Buffered(1))


def _sigmoid(x):
    return 0.5 * jnp.tanh(0.5 * x) + 0.5


def _decay_tables(tt):
    h = np.arange(HEADS, dtype=np.float64)
    log_gamma = np.log(1.0 - 2.0 ** (-5.0 - h)).astype(np.float32).astype(np.float64)[:, None, None]
    idx = np.arange(tt, dtype=np.float64)
    visible = (idx[None, :] // CHUNK) <= (idx[:, None] // CHUNK)
    mask = np.where(visible[None], np.exp(log_gamma * np.abs(idx[:, None] - idx[None, :])[None]), 0.0)
    qd = np.broadcast_to(np.exp(log_gamma * (idx[None, :, None] + 1.0)), (HEADS, tt, DH))
    kd = np.broadcast_to(np.exp(log_gamma * (tt - 1.0 - idx[None, :, None])), (HEADS, tt, DH))
    cd = np.exp(log_gamma[:, 0, 0] * tt)
    return (jnp.asarray(mask, f32), jnp.asarray(qd, f32), jnp.asarray(kd, f32), [float(c) for c in cd])


def _rope_tables():
    inv_freq = (10000.0 ** (-np.arange(0, DH, 2, dtype=np.float64) / DH)).astype(np.float32)
    ang = (np.arange(T, dtype=np.float32)[:, None] * inv_freq[None, :]).astype(np.float64)
    cos, sin = np.cos(ang), np.sin(ang)
    return (jnp.asarray(np.concatenate([cos, cos], axis=1), f32), jnp.asarray(np.concatenate([-sin, sin], axis=1), f32))


def _swap_halves(t):
    return pltpu.roll(t, DH // 2, axis=1)


def _mix_forward(x, w_in_t, cos, sin, dmat, qd, kd, cdec, w_pool, pool_scale, w_out, ln1_g, ln1_b, gather,
                 tt=MIX_TILE):
    n_tiles = T // tt
    n_g = len(gather)

    def body(x_ref, wint_ref, cos_ref, sin_ref, dmat_ref, qd_ref, kd_ref, wpool_ref, pscale_ref, wout_ref,
             g1_ref, b1_ref, *rest):
        gin, rest = rest[:n_g], rest[n_g:]
        qkv_ref, g_ref, oret_ref, states_ref, cat_ref, pooled_ref, xhat_ref, rstd_ref, x1b_ref = rest[:9]
        gout, (state_s, pext_s, tmp_s, *sems) = rest[9:9 + n_g], rest[9 + n_g:]
        i = pl.program_id(0)

        @pl.when(i == 0)
        def _():
            state_s[...] = jnp.zeros_like(state_s)
            pext_s[:, pl.ds(0, HALO), :] = jnp.zeros((GROUPS, HALO, DH), f32)
            _gather_start(gin, gout, *sems)

        @pl.when(i == n_tiles - 2)
        def _():
            _gather_forward(gin, gout, *sems)

        xb = x_ref[...].astype(bf16)
        cos_t, sin_t = cos_ref[...], sin_ref[...]
        for part in range(2):
            pr = _dot(xb, wint_ref[pl.ds(part * RW, RW), :], NT)
            for h in range(HEADS):
                t = pr[:, h * DH:(h + 1) * DH]
                r = t * cos_t + _swap_halves(t) * sin_t
                if part == 1:
                    r = r * K_SCALE
                qkv_ref[:, part * RW + h * DH: part * RW + (h + 1) * DH] = r.astype(bf16)
        qkv_ref[:, 2 * RW:3 * RW] = _dot(xb, wint_ref[pl.ds(2 * RW, RW), :], NT).astype(bf16)
        g_ref[...] = _dot(xb, wint_ref[pl.ds(3 * RW, RW), :], NT)
        p = _dot(xb, wint_ref[pl.ds(4 * RW, PW), :], NT)
        for gi in range(GROUPS):
            pext_s[gi, pl.ds(HALO, tt), :] = p[:, gi * DH:(gi + 1) * DH]

        for sub in range(tt // RET_TILE):
            rows = pl.ds(sub * RET_TILE, RET_TILE)
            for h in range(HEADS):
                q = qkv_ref[rows, h * DH:(h + 1) * DH]
                k = qkv_ref[rows, RW + h * DH: RW + (h + 1) * DH]
                v = qkv_ref[rows, 2 * RW + h * DH: 2 * RW + (h + 1) * DH]
                s = _dot(q, k, NT) * dmat_ref[h]
                st = state_s[h]
                stb = st.astype(bf16)
                states_ref[sub, h] = stb
                oret_ref[rows, h * DH:(h + 1) * DH] = (_dot(s.astype(bf16), v)
                                                      + _dot((q.astype(f32) * qd_ref[h]).astype(bf16), stb))
                state_s[h] = st * cdec[h] + _dot((k.astype(f32) * kd_ref[h]).astype(bf16), v, TN)

        for h in range(HEADS):
            sl = slice(h * DH, (h + 1) * DH)
            o = oret_ref[:, sl]
            r = lax.rsqrt(jnp.mean(o * o, axis=-1, keepdims=True) + RMS_EPS)
            gg = g_ref[:, sl]
            cat_ref[:, sl] = (o * r * (gg * _sigmoid(gg))).astype(bf16)

        pos1 = (i * tt + lax.broadcasted_iota(jnp.int32, (tt, 1), 0) + 1).astype(f32)
        for gi, w in enumerate(WINDOWS):
            sl = slice(gi * DH, (gi + 1) * DH)
            stages = int(math.log2(w))
            src = pext_s
            for s in range(stages):
                lo = HALO - 8 * (stages - 1 - s)
                n = tt + HALO - lo
                shift = 2 ** s
                val = src[gi, pl.ds(lo, n), :] + src[gi, pl.ds(lo - shift, n), :]
                if s == stages - 1:
                    wsum = val
                else:
                    tmp_s[gi, pl.ds(lo, n), :] = val
                    src = tmp_s
            p_g = pext_s[gi, pl.ds(HALO, tt), :]
            pooled = (wsum / jnp.minimum(pos1, float(w)) - p_g).astype(bf16)
            pooled_ref[:, sl] = pooled
            y = _dot(pooled, wpool_ref[gi]) * pscale_ref[:, sl]
            cat_ref[:, RW + gi * DH: RW + (gi + 1) * DH] = y.astype(bf16)
        pext_s[:, pl.ds(0, HALO), :] = pext_s[:, pl.ds(tt, HALO), :]

        z = ALPHA * x_ref[...] + _dot(cat_ref[...], wout_ref[...])
        mu = jnp.mean(z, axis=-1, keepdims=True)
        zc = z - mu
        rstd = lax.rsqrt(jnp.mean(zc * zc, axis=-1, keepdims=True) + LN_EPS)
        xhat = zc * rstd
        xhat_ref[...] = xhat
        rstd_ref[...] = rstd
        x1b_ref[...] = (xhat * g1_ref[...] + b1_ref[...]).astype(bf16)

        @pl.when(i == n_tiles - 1)
        def _():
            _gather_finish(gin, gout, *sems)

    tile = lambda w: pl.BlockSpec((tt, w), lambda i: (i, 0))
    hbm = pl.BlockSpec(memory_space=pltpu.HBM)
    out_shape = (
        jax.ShapeDtypeStruct((T, 3 * RW), bf16),
        jax.ShapeDtypeStruct((T, RW), f32),
        jax.ShapeDtypeStruct((T, RW), f32),
        jax.ShapeDtypeStruct((T // RET_TILE, HEADS, DH, DH), bf16),
        jax.ShapeDtypeStruct((T, D), bf16),
        jax.ShapeDtypeStruct((T, PW), bf16),
        jax.ShapeDtypeStruct((T, D), f32),
        jax.ShapeDtypeStruct((T, 1), f32),
        jax.ShapeDtypeStruct((T, D), bf16),
    ) + tuple(jax.ShapeDtypeStruct((N_DEV,) + b.shape, b.dtype) for b in gather)
    return pl.pallas_call(
        body, name="mix_forward", grid=(n_tiles,), out_shape=out_shape,
        in_specs=[tile(D), _const_spec((IN_W, D)), tile(DH), tile(DH),
                  _const_spec((HEADS, RET_TILE, RET_TILE)), _const_spec((HEADS, RET_TILE, DH)),
                  _const_spec((HEADS, RET_TILE, DH)),
                  _const_spec((GROUPS, DH, DH)), _const_spec((1, PW)), _const_spec((D, D)),
                  _const_spec((1, D)), _const_spec((1, D))] + [hbm] * n_g,
        out_specs=(tile(3 * RW), tile(RW), tile(RW),
                   pl.BlockSpec((tt // RET_TILE, HEADS, DH, DH), lambda i: (i, 0, 0, 0)),
                   tile(D), tile(PW), tile(D), tile(1), tile(D)) + (hbm,) * n_g,
        scratch_shapes=[pltpu.VMEM((HEADS, DH, DH), f32), pltpu.VMEM((GROUPS, tt + HALO, DH), f32),
                        pltpu.VMEM((GROUPS, tt + HALO, DH), f32)] + _gather_sems(n_g),
        compiler_params=pltpu.CompilerParams(dimension_semantics=("arbitrary",), vmem_limit_bytes=V7X_VMEM_LIMIT,
                                             collective_id=GATHER_BARRIER),
    )(x, w_in_t, cos, sin, dmat, qd, kd, w_pool, pool_scale, w_out, ln1_g, ln1_b, *gather)


def _ffn_forward_backward(xhat1, rstd1, ln1_g, ln1_b, w_up_t, conv_w, conv_b, w_down, ln2_g, ln2_b, target,
                          tt=256, widths=(1024, 1024, 768)):
    n_tiles = T // tt
    assert sum(widths) == D_FF and all(w % 128 == 0 for w in widths)
    chunks = [(sum(widths[:c]), w) for c, w in enumerate(widths)]
    FH = 16
    hb = tt // FH

    def body(xhat_ref, halo_ref, rstd_ref, g1_ref, b1_ref, wupt_ref, cw_ref, cb_ref, wdown_ref, g2_ref, b2_ref, tgt_ref,
             dz1_ref, dz2b_ref, du_ref, f_ref, loss_ref, dg2_ref, db2_ref, dg1_ref, db1_ref, dcb_ref, dcw_ref,
             gext_s, val_s, dhext_s):
        i = pl.program_id(0)
        tile_idx = n_tiles - 1 - i

        def rd(ref, off, lo, w):
            return jnp.concatenate([ref[lo // 128 + k, pl.ds(off, tt), :] for k in range(w // 128)], axis=1)

        def wr(ref, lo, val):
            for k in range(val.shape[1] // 128):
                ref[lo // 128 + k, pl.ds(0, val.shape[0]), :] = val[:, k * 128:(k + 1) * 128]

        @pl.when(i == 0)
        def _():
            for r in (loss_ref, dg2_ref, db2_ref, dg1_ref, db1_ref, dcb_ref, dcw_ref):
                r[...] = jnp.zeros_like(r)
            dhext_s[:, pl.ds(tt, 8), :] = jnp.zeros((D_FF // 128, 8, 128), f32)

        g1, b1 = g1_ref[...], b1_ref[...]
        xhat = xhat_ref[...]
        x1 = xhat * g1 + b1
        x1b = x1.astype(bf16)
        x1h = ((halo_ref[...] * g1 + b1) * jnp.where(tile_idx == 0, 0.0, 1.0)).astype(bf16)
        x1ext = jnp.concatenate([x1h, x1b], axis=0)

        for lo, w in chunks:
            cs = slice(lo, lo + w)
            val = _dot(x1b, wupt_ref[pl.ds(lo, w), :], NT)
            gate_ext = _dot(x1ext, wupt_ref[pl.ds(D_FF + lo, w), :], NT)
            wr(gext_s, lo, gate_ext)
            hh = (cb_ref[:, cs] + cw_ref[0:1, cs] * rd(gext_s, FH - 2, lo, w) + cw_ref[1:2, cs] * rd(gext_s, FH - 1, lo, w)
                  + cw_ref[2:3, cs] * gate_ext[FH:])
            sg = _sigmoid(hh)
            act = hh * sg
            wr(dhext_s, lo, act)
            val_s[:, cs] = val * (sg + act * (1.0 - sg))
            fb = (act * val).astype(bf16)
            f_ref[:, cs] = fb
            part = _dot(fb, wdown_ref[pl.ds(lo, w), :])
            ffn = part if lo == 0 else ffn + part

        z = ALPHA * x1 + ffn
        mu = jnp.mean(z, axis=-1, keepdims=True)
        zc = z - mu
        rstd2 = lax.rsqrt(jnp.mean(zc * zc, axis=-1, keepdims=True) + LN_EPS)
        xh2 = zc * rstd2
        diff = xh2 * g2_ref[...] + b2_ref[...] - tgt_ref[...]
        loss_ref[...] += 0.5 * jnp.sum(diff * diff) / D
        dy = diff * (1.0 / D)
        dg2_ref[...] += jnp.sum(dy * xh2, axis=0, keepdims=True)
        db2_ref[...] += jnp.sum(dy, axis=0, keepdims=True)
        dyg = dy * g2_ref[...]
        dz2 = rstd2 * (dyg - jnp.mean(dyg, axis=-1, keepdims=True) - xh2 * jnp.mean(dyg * xh2, axis=-1, keepdims=True))
        dz2b = dz2.astype(bf16)
        dz2b_ref[...] = dz2b

        dx1 = ALPHA * dz2
        for lo, w in chunks:
            cs = slice(lo, lo + w)
            df = _dot(dz2b, wdown_ref[pl.ds(lo, w), :], NT)
            dval = df * rd(dhext_s, 0, lo, w)
            dh = df * val_s[:, cs]
            wr(dhext_s, lo, dh)
            dh1, dh2, g0 = rd(dhext_s, 1, lo, w), rd(dhext_s, 2, lo, w), rd(gext_s, FH, lo, w)
            dcb_ref[:, cs] += jnp.sum(dh, axis=0, keepdims=True)
            dcw_ref[0:1, cs] += jnp.sum(dh2 * g0, axis=0, keepdims=True)
            dcw_ref[1:2, cs] += jnp.sum(dh1 * g0, axis=0, keepdims=True)
            dcw_ref[2:3, cs] += jnp.sum(dh * g0, axis=0, keepdims=True)
            dgate = cw_ref[2:3, cs] * dh + cw_ref[1:2, cs] * dh1 + cw_ref[0:1, cs] * dh2
            dvalb, dgateb = dval.astype(bf16), dgate.astype(bf16)
            du_ref[:, cs] = dvalb
            du_ref[:, D_FF + lo: D_FF + lo + w] = dgateb
            dx1 = dx1 + _dot(dvalb, wupt_ref[pl.ds(lo, w), :]) + _dot(dgateb, wupt_ref[pl.ds(D_FF + lo, w), :])
        dhext_s[:, pl.ds(tt, 8), :] = dhext_s[:, pl.ds(0, 8), :]

        dg1_ref[...] += jnp.sum(dx1 * xhat, axis=0, keepdims=True)
        db1_ref[...] += jnp.sum(dx1, axis=0, keepdims=True)
        dxg = dx1 * g1
        dz1_ref[...] = rstd_ref[...] * (dxg - jnp.mean(dxg, axis=-1, keepdims=True)
                                        - xhat * jnp.mean(dxg * xhat, axis=-1, keepdims=True))

    rtile = lambda w: pl.BlockSpec((tt, w), lambda i: (n_tiles - 1 - i, 0))
    acc = lambda shape: pl.BlockSpec(shape, lambda i: (0, 0))
    out_shape = (
        jax.ShapeDtypeStruct((T, D), f32),
        jax.ShapeDtypeStruct((T, D), bf16),
        jax.ShapeDtypeStruct((T, 2 * D_FF), bf16),
        jax.ShapeDtypeStruct((T, D_FF), bf16),
        jax.ShapeDtypeStruct((8, 128), f32),
        jax.ShapeDtypeStruct((1, D), f32), jax.ShapeDtypeStruct((1, D), f32),
        jax.ShapeDtypeStruct((1, D), f32), jax.ShapeDtypeStruct((1, D), f32),
        jax.ShapeDtypeStruct((1, D_FF), f32), jax.ShapeDtypeStruct((3, D_FF), f32),
    )
    return pl.pallas_call(
        body, name="ffn_forward_backward", grid=(n_tiles,), out_shape=out_shape,
        in_specs=[rtile(D),
                  pl.BlockSpec((FH, D), lambda i: (jnp.maximum((n_tiles - 1 - i) * hb - 1, 0), 0)),
                  rtile(1), _const_spec((1, D)), _const_spec((1, D)), _const_spec((2 * D_FF, D)),
                  _const_spec((3, D_FF)), _const_spec((1, D_FF)), _const_spec((D_FF, D)),
                  _const_spec((1, D)), _const_spec((1, D)), rtile(D)],
        out_specs=(rtile(D), rtile(D), rtile(2 * D_FF), rtile(D_FF), acc((8, 128)),
                   acc((1, D)), acc((1, D)), acc((1, D)), acc((1, D)), acc((1, D_FF)), acc((3, D_FF))),
        scratch_shapes=[pltpu.VMEM((D_FF // 128, tt + FH, 128), f32), pltpu.VMEM((tt, D_FF), f32),
                        pltpu.VMEM((D_FF // 128, tt + 8, 128), f32)],
        compiler_params=pltpu.CompilerParams(dimension_semantics=("arbitrary",), vmem_limit_bytes=V7X_VMEM_LIMIT),
    )(xhat1, xhat1, rstd1, ln1_g, ln1_b, w_up_t, conv_w, conv_b, w_down, ln2_g, ln2_b, target)


def _mix_backward(dz1, w_out, qkv, g, oret, states, pooled, cat, cos, sin, dmat, qd, kd, cdec, w_pool, pool_scale, w_in_t,
                  after, tt=MIX_TILE):
    n_tiles = T // tt

    def body(dz1_ref, wout_ref, qkv_ref, g_ref, oret_ref, states_ref, pooled_ref, cat_ref, cos_ref, sin_ref, dmat_ref,
             qd_ref, kd_ref, wpool_ref, pscale_ref, wint_ref, after_ref,
             dproj_ref, gx_ref, dwpool_ref, dpscale_ref, dwout_ref, dstate_s, dout_s, eext_s, tmp_s, dwout_s):
        i = pl.program_id(0)
        tile_idx = n_tiles - 1 - i

        @pl.when(i == 0)
        def _():
            dstate_s[...] = jnp.zeros_like(dstate_s)
            dwpool_ref[...] = jnp.zeros_like(dwpool_ref)
            dpscale_ref[...] = jnp.zeros_like(dpscale_ref)
            dwout_s[...] = jnp.zeros_like(dwout_s)
            eext_s[:, pl.ds(tt, HALO), :] = jnp.zeros((GROUPS, HALO, DH), f32)

        dz1 = dz1_ref[...]
        dz1b = dz1.astype(bf16)
        dcat = _dot(dz1b, wout_ref[...], NT)
        dwout_s[...] += _dot(cat_ref[...], dz1b, TN)

        pos1 = (tile_idx * tt + lax.broadcasted_iota(jnp.int32, (tt, 1), 0) + 1).astype(f32)
        for gi, w in enumerate(WINDOWS):
            sl = slice(gi * DH, (gi + 1) * DH)
            dpo = dcat[:, RW + gi * DH: RW + (gi + 1) * DH]
            pooled_g = pooled_ref[:, sl]
            ylin = _dot(pooled_g, wpool_ref[gi])
            dpscale_ref[:, sl] += jnp.sum(dpo * ylin, axis=0, keepdims=True)
            dpw = (dpo * pscale_ref[:, sl]).astype(bf16)
            dwpool_ref[gi] += _dot(pooled_g, dpw, TN)
            dpooled = _dot(dpw, wpool_ref[gi], NT)
            eext_s[gi, pl.ds(0, tt), :] = dpooled / jnp.minimum(pos1, float(w))
            stages = int(math.log2(w))
            src = eext_s
            for s in range(stages):
                n = tt + 8 * (stages - 1 - s)
                shift = 2 ** s
                val = src[gi, pl.ds(0, n), :] + src[gi, pl.ds(shift, n), :]
                if s == stages - 1:
                    wsum = val
                else:
                    tmp_s[gi, pl.ds(0, n), :] = val
                    src = tmp_s
            dproj_ref[:, 4 * RW + gi * DH: 4 * RW + (gi + 1) * DH] = (wsum - dpooled).astype(bf16)
        eext_s[:, pl.ds(tt, HALO), :] = eext_s[:, pl.ds(0, HALO), :]

        for h in range(HEADS):
            sl = slice(h * DH, (h + 1) * DH)
            dr = dcat[:, sl]
            o = oret_ref[:, sl]
            r = lax.rsqrt(jnp.mean(o * o, axis=-1, keepdims=True) + RMS_EPS)
            rn = o * r
            gg = g_ref[:, sl]
            sg = _sigmoid(gg)
            dproj_ref[:, 3 * RW + h * DH: 3 * RW + (h + 1) * DH] = (dr * rn * (sg * (1.0 + gg * (1.0 - sg)))).astype(bf16)
            drn = dr * (gg * sg)
            dout_s[:, sl] = (r * (drn - rn * jnp.mean(drn * rn, axis=-1, keepdims=True))).astype(bf16)

        for sub in reversed(range(tt // RET_TILE)):
            rows = pl.ds(sub * RET_TILE, RET_TILE)
            cos_t, sin_t = cos_ref[rows, :], sin_ref[rows, :]
            for h in range(HEADS):
                q = qkv_ref[rows, h * DH:(h + 1) * DH]
                k = qkv_ref[rows, RW + h * DH: RW + (h + 1) * DH]
                v = qkv_ref[rows, 2 * RW + h * DH: 2 * RW + (h + 1) * DH]
                do = dout_s[rows, h * DH:(h + 1) * DH]
                stb = states_ref[sub, h]
                dst = dstate_s[h]
                dstb = dst.astype(bf16)
                sb = (_dot(q, k, NT) * dmat_ref[h]).astype(bf16)
                dsb = (_dot(do, v, NT) * dmat_ref[h]).astype(bf16)
                dq = _dot(dsb, k) + _dot(do, stb, NT) * qd_ref[h]
                dk = _dot(dsb, q, TN) + _dot(v, dstb, NT) * kd_ref[h]
                dv = _dot(sb, do, TN) + _dot((k.astype(f32) * kd_ref[h]).astype(bf16), dstb)
                dstate_s[h] = dst * cdec[h] + _dot((q.astype(f32) * qd_ref[h]).astype(bf16), do, TN)
                dproj_ref[rows, h * DH:(h + 1) * DH] = (dq * cos_t - _swap_halves(dq) * sin_t).astype(bf16)
                dproj_ref[rows, RW + h * DH: RW + (h + 1) * DH] = (
                    (dk * cos_t - _swap_halves(dk) * sin_t) * K_SCALE).astype(bf16)
                dproj_ref[rows, 2 * RW + h * DH: 2 * RW + (h + 1) * DH] = dv.astype(bf16)

        gx_ref[...] = ALPHA * dz1 + _dot(dproj_ref[...], wint_ref[...])

        @pl.when(i == n_tiles - 1)
        def _():
            dwout_ref[...] = dwout_s[...].astype(bf16)

    rtile = lambda w: pl.BlockSpec((tt, w), lambda i: (n_tiles - 1 - i, 0))
    out_shape = (
        jax.ShapeDtypeStruct((T, IN_W), bf16),
        jax.ShapeDtypeStruct((T, D), f32),
        jax.ShapeDtypeStruct((GROUPS, DH, DH), f32),
        jax.ShapeDtypeStruct((1, PW), f32),
        jax.ShapeDtypeStruct((D, D), bf16),
    )
    return pl.pallas_call(
        body, name="mix_backward", grid=(n_tiles,), out_shape=out_shape,
        in_specs=[rtile(D), _const_spec((D, D)), rtile(3 * RW), rtile(RW), rtile(RW),
                  pl.BlockSpec((tt // RET_TILE, HEADS, DH, DH), lambda i: (n_tiles - 1 - i, 0, 0, 0)),
                  rtile(PW), rtile(D), rtile(DH), rtile(DH),
                  _const_spec((HEADS, RET_TILE, RET_TILE)), _const_spec((HEADS, RET_TILE, DH)),
                  _const_spec((HEADS, RET_TILE, DH)),
                  _const_spec((GROUPS, DH, DH)), _const_spec((1, PW)), _const_spec((IN_W, D)),
                  pl.BlockSpec(memory_space=pl.ANY)],
        out_specs=(rtile(IN_W), rtile(D), pl.BlockSpec((GROUPS, DH, DH), lambda i: (0, 0, 0)),
                   pl.BlockSpec((1, PW), lambda i: (0, 0)),
                   pl.BlockSpec((D, D), lambda i: (0, 0), pipeline_mode=pl.Buffered(1))),
        scratch_shapes=[pltpu.VMEM((HEADS, DH, DH), f32), pltpu.VMEM((tt, RW), bf16),
                        pltpu.VMEM((GROUPS, tt + HALO, DH), f32), pltpu.VMEM((GROUPS, tt + HALO, DH), f32),
                        pltpu.VMEM((D, D), f32)],
        compiler_params=pltpu.CompilerParams(dimension_semantics=("arbitrary",), vmem_limit_bytes=V7X_VMEM_LIMIT),
    )(dz1, w_out, qkv, g, oret, states, pooled, cat, cos, sin, dmat, qd, kd, w_pool, pool_scale, w_in_t, after)


def _weight_grad(a, b, name, tm, exchange=(), tk=2048):
    m = a.shape[1]
    n_m, n_k, n_e = m // tm, T // tk, len(exchange)

    def body(a_ref, b_ref, *rest):
        ein, o_ref, eout, (acc_s, *sems) = rest[:n_e], rest[n_e], rest[n_e + 1:2 * n_e + 1], rest[2 * n_e + 1:]
        i, k = pl.program_id(0), pl.program_id(1)

        if n_e:
            @pl.when((i == 0) & (k == 0))
            def _():
                _chip_exchange_start(ein, eout, *sems)

        @pl.when(k == 0)
        def _():
            acc_s[...] = jnp.zeros_like(acc_s)

        acc_s[...] += _dot(a_ref[...], b_ref[pl.ds(pl.multiple_of(k * tk, tk), tk), :].astype(bf16), TN)

        @pl.when(k == n_k - 1)
        def _():
            o_ref[...] = acc_s[...].astype(bf16)

        if n_e:
            @pl.when((i == n_m - 1) & (k == n_k - 1))
            def _():
                _chip_exchange_finish(ein, eout, *sems)

    hbm = pl.BlockSpec(memory_space=pltpu.HBM)
    return pl.pallas_call(
        body, name=name, grid=(n_m, n_k),
        out_shape=(jax.ShapeDtypeStruct((m, D), bf16),) + tuple(jax.ShapeDtypeStruct(e.shape, e.dtype) for e in exchange),
        in_specs=[pl.BlockSpec((tk, tm), lambda i, k: (k, i)),
                  pl.BlockSpec((T, D), lambda i, k: (0, 0), pipeline_mode=pl.Buffered(1))] + [hbm] * n_e,
        out_specs=(pl.BlockSpec((tm, D), lambda i, k: (i, 0)),) + (hbm,) * n_e,
        scratch_shapes=[pltpu.VMEM((tm, D), f32)] + _chip_exchange_sems(n_e),
        compiler_params=pltpu.CompilerParams(dimension_semantics=("arbitrary", "arbitrary"),
                                             vmem_limit_bytes=V7X_VMEM_LIMIT,
                                             collective_id=CHIP_BARRIER if n_e else None),
    )(a, b, *exchange)


CHIP_FLIPS = ((1, 0), (0, 1), (1, 1))
PAIR_BARRIER, CHIP_BARRIER, GATHER_BARRIER, CHIP_BARRIER_SPLIT = 0, 1, 2, 3


def _barrier(peers):
    sem = pltpu.get_barrier_semaphore()
    for peer in peers:
        pl.semaphore_signal(sem, inc=1, device_id=peer, device_id_type=pl.DeviceIdType.MESH)
    pl.semaphore_wait(sem, len(peers))


def _me():
    return lax.axis_index("x"), lax.axis_index("y"), lax.axis_index("c")


def _chip(me, k):
    x, y, _ = me
    if k == 0:
        return x, y
    fx, fy = CHIP_FLIPS[k - 1]
    return (1 - x if fx else x), (1 - y if fy else y)


def _slot(x, y, c):
    return 4 * x + 2 * y + c


def _remote(src, dst, send_sem, recv_sem, to):
    return pltpu.make_async_remote_copy(src_ref=src, dst_ref=dst, send_sem=send_sem, recv_sem=recv_sem,
                                        device_id=to, device_id_type=pl.DeviceIdType.MESH)


def _gather_sems(n):
    return [pltpu.SemaphoreType.DMA((7, n)), pltpu.SemaphoreType.DMA((7, n)), pltpu.SemaphoreType.DMA((n,))] if n else []


def _gather_copy(k, j, gin, gout, send_sems, recv_sems, sending):
    x, y, c = _me()
    sibling, x_chip, y_chip, d_chip = (x, y, 1 - c), (1 - x, y), (x, 1 - y), (1 - x, 1 - y)
    south = c == 0
    passed_on = (jnp.where(south, 1 - x, x), jnp.where(south, y, 1 - y), c)
    src, to = gin[j], sibling
    if sending:
        block = {0: (x, y, c), 1: (x, y, c), 2: (x, y, c), 3: passed_on, 4: (*x_chip, c), 5: (*y_chip, c), 6: (*d_chip, c)}[k]
        to = {1: (*x_chip, c), 2: (*y_chip, c), 3: (jnp.where(south, x, 1 - x), jnp.where(south, 1 - y, y), c)}.get(k, sibling)
        if k >= 3:
            src = gout[j].at[_slot(*block)]
    else:
        block = {0: sibling, 1: (*x_chip, c), 2: (*y_chip, c), 3: (*d_chip, c), 4: (*x_chip, 1 - c), 5: (*y_chip, 1 - c),
                 6: (*d_chip, 1 - c)}[k]
    return _remote(src, gout[j].at[_slot(*block)], send_sems.at[k, j], recv_sems.at[k, j], to)


def _gather_do(ks, action, gin, gout, send_sems, recv_sems):
    for k in ks:
        for j in range(len(gin)):
            cp = _gather_copy(k, j, gin, gout, send_sems, recv_sems, action != "wait_recv")
            getattr(cp, action)()


def _gather_peers():
    x, y, c = _me()
    return [(x, y, 1 - c), (1 - x, y, c), (x, 1 - y, c)]


def _gather_start(gin, gout, send_sems, recv_sems, local_sems, barrier=True):
    if barrier:
        _barrier(_gather_peers())
    for j in range(len(gin)):
        pltpu.make_async_copy(gin[j], gout[j].at[_slot(*_me())], local_sems.at[j]).start()
    _gather_do((0, 1, 2), "start", gin, gout, send_sems, recv_sems)


def _gather_forward(gin, gout, send_sems, recv_sems, local_sems):
    _gather_do((1, 2), "wait_recv", gin, gout, send_sems, recv_sems)
    _gather_do((3, 4, 5), "start", gin, gout, send_sems, recv_sems)


def _gather_finish(gin, gout, send_sems, recv_sems, local_sems):
    _gather_do((3,), "wait_recv", gin, gout, send_sems, recv_sems)
    _gather_do((6,), "start", gin, gout, send_sems, recv_sems)
    _gather_do((0, 4, 5, 6), "wait_recv", gin, gout, send_sems, recv_sems)
    _gather_do(range(7), "wait_send", gin, gout, send_sems, recv_sems)
    for j in range(len(gin)):
        pltpu.make_async_copy(gin[j], gout[j].at[_slot(*_me())], local_sems.at[j]).wait()


def _all_gather(blocks, name):
    n = len(blocks)

    def body(*refs):
        gin, gout, sems = refs[:n], refs[n:2 * n], refs[2 * n:]
        _gather_start(gin, gout, *sems)
        _gather_forward(gin, gout, *sems)
        _gather_finish(gin, gout, *sems)

    hbm = pl.BlockSpec(memory_space=pltpu.HBM)
    return pl.pallas_call(
        body, name=name,
        out_shape=tuple(jax.ShapeDtypeStruct((N_DEV,) + b.shape, b.dtype) for b in blocks),
        in_specs=[hbm] * n, out_specs=(hbm,) * n, scratch_shapes=_gather_sems(n),
        compiler_params=pltpu.CompilerParams(collective_id=GATHER_BARRIER),
    )(*blocks)


def _pair_reduce(parts, name):
    n = len(parts)

    def body(*refs):
        ins, own, others, landing, mine = (refs[k * n:(k + 1) * n] for k in range(5))
        send_sems, recv_sems, local_sems = refs[5 * n:]
        me = _me()
        x, y, c = me
        sibling = (x, y, 1 - c)
        _barrier([sibling])
        sends, loads = [], []
        for k in range(4):
            for j in range(n):
                cp = _remote(ins[j].at[_slot(*_chip(me, k), 1 - c)], landing[j].at[k], send_sems.at[k, j],
                             recv_sems.at[k, j], sibling)
                cp.start()
                sends.append(cp)
                ld = pltpu.make_async_copy(ins[j].at[_slot(*_chip(me, k), c)], mine[j].at[k], local_sems.at[k, j])
                ld.start()
                loads.append(ld)
        for k in range(4):
            for j in range(n):
                loads[k * n + j].wait()
                _remote(ins[j].at[0], landing[j].at[k], send_sems.at[k, j], recv_sems.at[k, j], sibling).wait_recv()
                total = mine[j][k].astype(f32) + landing[j][k].astype(f32)
                if k == 0:
                    own[j][...] = total.astype(own[j].dtype)
                else:
                    others[j][k - 1] = total.astype(others[j].dtype)
        for cp in sends:
            cp.wait_send()

    vm = pl.BlockSpec(memory_space=pltpu.VMEM)
    return pl.pallas_call(
        body, name=name,
        out_shape=tuple(jax.ShapeDtypeStruct(p.shape[1:], p.dtype) for p in parts)
        + tuple(jax.ShapeDtypeStruct((3,) + p.shape[1:], p.dtype) for p in parts),
        in_specs=[pl.BlockSpec(memory_space=pltpu.HBM)] * n, out_specs=(vm,) * (2 * n),
        scratch_shapes=[pltpu.VMEM((4,) + p.shape[1:], p.dtype) for p in parts] * 2
        + [pltpu.SemaphoreType.DMA((4, n)), pltpu.SemaphoreType.DMA((4, n)), pltpu.SemaphoreType.DMA((4, n))],
        compiler_params=pltpu.CompilerParams(vmem_limit_bytes=V7X_VMEM_LIMIT, collective_id=PAIR_BARRIER),
    )(*parts)


def _chip_exchange_sems(n):
    return [pltpu.SemaphoreType.DMA((3, n)), pltpu.SemaphoreType.DMA((3, n))] if n else []


def _chip_exchange_copy(k, j, ein, eout, send_sems, recv_sems):
    me = _me()
    return _remote(ein[j].at[k - 1], eout[j].at[k - 1], send_sems.at[k - 1, j], recv_sems.at[k - 1, j],
                   (*_chip(me, k), me[2]))


def _chip_peers():
    me = _me()
    return [(*_chip(me, k), me[2]) for k in range(1, 4)]


def _chip_exchange_start(ein, eout, send_sems, recv_sems, barrier=True):
    if barrier:
        _barrier(_chip_peers())
    for k in range(1, 4):
        for j in range(len(ein)):
            _chip_exchange_copy(k, j, ein, eout, send_sems, recv_sems).start()


def _chip_exchange_finish(ein, eout, send_sems, recv_sems):
    for k in range(1, 4):
        for j in range(len(ein)):
            _chip_exchange_copy(k, j, ein, eout, send_sems, recv_sems).wait_recv()
    for k in range(1, 4):
        for j in range(len(ein)):
            _chip_exchange_copy(k, j, ein, eout, send_sems, recv_sems).wait_send()


def _split_copies(src_ref, dst_ref, sems):
    me = _me()
    return [_remote(src_ref.at[k - 1], dst_ref.at[k - 1], sems[k - 1], sems[2 + k], (*_chip(me, k), me[2]))
            for k in range(1, 4)]


def _exchange_start(others, name, barrier_id):
    def body(src_ref, land_ref, *rest):
        sems, token_ref = rest[:6], rest[8]
        _barrier(_chip_peers())
        for copy in _split_copies(src_ref, land_ref, sems):
            copy.start()
        token_ref[...] = jnp.zeros_like(token_ref)

    hbm, sem = pl.BlockSpec(memory_space=pltpu.HBM), pl.BlockSpec(memory_space=pltpu.SEMAPHORE)
    thru = pltpu.HBM(others.shape, others.dtype)
    res = pl.pallas_call(
        body, name=name,
        out_shape=(pltpu.SemaphoreType.DMA(()),) * 6 + (thru, thru, jax.ShapeDtypeStruct((8, 128), f32)),
        in_specs=(hbm, hbm), out_specs=(sem,) * 6 + (hbm, hbm, pl.BlockSpec(memory_space=pltpu.VMEM)),
        input_output_aliases={0: 6, 1: 7},
        compiler_params=pltpu.CompilerParams(has_side_effects=pltpu.SideEffectType.DATAFLOW_SIDE_EFFECTING,
                                             collective_id=barrier_id),
    )(pltpu.with_memory_space_constraint(others, pltpu.HBM),
      pltpu.with_memory_space_constraint(lax.empty(others.shape, others.dtype), pltpu.HBM))
    return res[:6], res[6], res[7], res[8]


def _exchange_wait(sems, src_thru, land_thru, after, name):
    n_after = len(after)

    def body(src_ref, land_ref, *rest):
        for copy in _split_copies(src_ref, land_ref, rest[:6]):
            copy.wait_send()
            copy.wait_recv()

    hbm, sem = pl.BlockSpec(memory_space=pltpu.HBM), pl.BlockSpec(memory_space=pltpu.SEMAPHORE)
    thru = pltpu.HBM(src_thru.shape, src_thru.dtype)
    return pl.pallas_call(
        body, name=name, out_shape=(thru, thru),
        in_specs=(hbm, hbm) + (sem,) * 6 + (pl.BlockSpec(memory_space=pl.ANY),) * n_after, out_specs=(hbm, hbm),
        input_output_aliases={0: 0, 1: 1},
        compiler_params=pltpu.CompilerParams(has_side_effects=pltpu.SideEffectType.DATAFLOW_SIDE_EFFECTING),
    )(src_thru, land_thru, *sems, *after)[1]


def _sum_parts(owns, arrived, name):
    n = len(owns)

    def body(*refs):
        for own, arr, out in zip(refs[:n], refs[n:2 * n], refs[2 * n:]):
            acc = own[...].astype(f32)
            for k in range(3):
                acc = acc + arr[k].astype(f32)
            out[...] = acc

    vm = pl.BlockSpec(memory_space=pltpu.VMEM)
    return pl.pallas_call(
        body, name=name, out_shape=tuple(jax.ShapeDtypeStruct(o.shape, f32) for o in owns),
        in_specs=[vm] * (2 * n), out_specs=(vm,) * n,
        compiler_params=pltpu.CompilerParams(vmem_limit_bytes=V7X_VMEM_LIMIT),
    )(*owns, *arrived)


def _adam_update(w, g, m, v):
    m = ADAM_B1 * m + (1.0 - ADAM_B1) * g
    v = ADAM_B2 * v + (1.0 - ADAM_B2) * (g * g)
    m_hat = m / (1.0 - ADAM_B1 ** ADAM_STEP)
    v_hat = v / (1.0 - ADAM_B2 ** ADAM_STEP)
    return -ADAM_LR * (m_hat / (jnp.sqrt(v_hat) + ADAM_EPS) + ADAM_WD * w), m, v


def _sum_adamw(own, arrived, w, m, v, name, steps, after=()):
    rows = own.shape[0]
    br = rows // steps

    def body(own_ref, arr_ref, w_ref, m_ref, v_ref, *rest):
        g_out, d_out, m_out, v_out = rest[len(after):]
        g = own_ref[...].astype(f32)
        for k in range(3):
            g = g + arr_ref[k].astype(f32)
        g_out[...] = g
        d_out[...], m_out[...], v_out[...] = _adam_update(w_ref[...], g, m_ref[...], v_ref[...])

    blk = pl.BlockSpec((br, D), lambda i: (i, 0))
    return pl.pallas_call(
        body, name=name, grid=(steps,), out_shape=(jax.ShapeDtypeStruct((rows, D), f32),) * 4,
        in_specs=[blk, pl.BlockSpec((3, br, D), lambda i: (0, i, 0)), blk, blk, blk]
        + [pl.BlockSpec(memory_space=pl.ANY)] * len(after), out_specs=(blk,) * 4,
        compiler_params=pltpu.CompilerParams(dimension_semantics=("parallel",), vmem_limit_bytes=V7X_VMEM_LIMIT),
    )(own, arrived, w, m, v, *after)


def _adamw(ws, gs, ms, vs, name):
    n = len(ws)

    def body(*refs):
        w_r, g_r, m_r, v_r = (refs[k * n:(k + 1) * n] for k in range(4))
        d_o, m_o, v_o = (refs[(4 + k) * n:(5 + k) * n] for k in range(3))
        for j in range(n):
            d_o[j][...], m_o[j][...], v_o[j][...] = _adam_update(w_r[j][...], g_r[j][...], m_r[j][...], v_r[j][...])

    vm = pl.BlockSpec(memory_space=pltpu.VMEM)
    shapes = tuple(jax.ShapeDtypeStruct(w.shape, f32) for w in ws)
    return pl.pallas_call(
        body, name=name, out_shape=shapes * 3, in_specs=[vm] * (4 * n), out_specs=tuple([vm] * (3 * n)),
        compiler_params=pltpu.CompilerParams(vmem_limit_bytes=V7X_VMEM_LIMIT),
    )(*ws, *gs, *ms, *vs)


SMALL = (("w_pool", GROUPS * DH * DH), ("pool_scale", PW), ("ln1_g", D), ("ln1_b", D), ("conv_b", D_FF),
         ("ln2_g", D), ("ln2_b", D), ("conv_w", 3 * D_FF), ("loss", 1))
SMALL_ROWS = 640


def _pack(named):
    flat = jnp.concatenate([named[k].reshape(-1) for k, _ in SMALL])
    return jnp.pad(flat, (0, SMALL_ROWS * 128 - flat.shape[0])).reshape(SMALL_ROWS, 128)


def _unpack(packed):
    flat, out, at = packed.reshape(-1), {}, 0
    for k, size in SMALL:
        out[k] = flat[at:at + size]
        at += size
    return out


def kernel(x, w_in, w_pool, pool_scale, w_out, ln1_g, ln1_b, w_up, conv_w, conv_b, w_down, ln2_g, ln2_b, loss_target, m_w_in, m_w_pool, m_pool_scale, m_w_out, m_ln1_g, m_ln1_b, m_w_up, m_conv_w, m_conv_b, m_w_down, m_ln2_g, m_ln2_b, v_w_in, v_w_pool, v_pool_scale, v_w_out, v_ln1_g, v_ln1_b, v_w_up, v_conv_w, v_conv_b, v_w_down, v_ln2_g, v_ln2_b):
    me = 4 * lax.axis_index("x") + 2 * lax.axis_index("y") + lax.axis_index("c")
    x2, tgt = x[0], loss_target[0]

    g_in, g_out, g_cw = _all_gather([w_in[0].T.astype(bf16), w_out[0].astype(bf16), jnp.transpose(conv_w, (1, 0, 2))],
                                    "gather_weights")
    w_in_t = g_in.reshape(IN_W, D)
    w_out_f = g_out.reshape(D, D)
    conv_w_f = jnp.transpose(g_cw[:, :, 0, :], (1, 0, 2)).reshape(3, D_FF)
    w_pool_b = w_pool[0].astype(bf16)

    cos, sin = _rope_tables()
    dmat, qd, kd, cdec = _decay_tables(RET_TILE)

    qkv, g, oret, states, cat, pooled, xhat1, rstd1, x1b, g_up, g_down = _mix_forward(
        x2, w_in_t, cos, sin, dmat, qd, kd, cdec, w_pool_b, pool_scale, w_out_f, ln1_g, ln1_b,
        gather=[w_up[0].T.astype(bf16), w_down[0].astype(bf16)])
    w_up_t = g_up.reshape(2 * D_FF, D)
    w_down_f = g_down.reshape(D_FF, D)
    dz1, dz2b, du, f, loss8, d_ln2_g, d_ln2_b, d_ln1_g, d_ln1_b, d_conv_b, d_conv_w = _ffn_forward_backward(
        xhat1, rstd1, ln1_g, ln1_b, w_up_t, conv_w_f, conv_b, w_down_f, ln2_g, ln2_b, tgt)

    (dw_down,) = _weight_grad(f, dz2b, "grad_w_down", tm=D_FF // 2)
    own_down, oth_down = _pair_reduce([dw_down.reshape(N_DEV, ROWS_DOWN, D)], "pair_reduce_down")
    dw_up_t, arr_down = _weight_grad(du, x1b, "grad_w_up", tm=D_FF // 2, exchange=[oth_down])
    own_up, oth_up = _pair_reduce([dw_up_t.reshape(N_DEV, ROWS_UP, D)], "pair_reduce_up")
    up_sems, up_src, up_land, up_started = _exchange_start(oth_up, "exchange_up_start", CHIP_BARRIER_SPLIT)
    dproj, grad_x, d_w_pool, d_pool_scale, dw_out = _mix_backward(
        dz1, w_out_f, qkv, g, oret, states, pooled, cat, cos, sin, dmat, qd, kd, cdec, w_pool_b, pool_scale, w_in_t,
        after=up_started)
    small = _pack({"w_pool": d_w_pool, "pool_scale": d_pool_scale, "ln1_g": d_ln1_g, "ln1_b": d_ln1_b,
                   "conv_b": d_conv_b, "ln2_g": d_ln2_g, "ln2_b": d_ln2_b, "conv_w": d_conv_w, "loss": loss8[0, :1]})
    own_out, own_small, oth_out, oth_small = _pair_reduce(
        [dw_out.reshape(N_DEV, ROWS_OUT, D), small.reshape(N_DEV, SMALL_ROWS // N_DEV, 128)], "pair_reduce_out")
    dw_in_t, arr_out, arr_small = _weight_grad(dproj, x2, "grad_w_in", tm=IN_W // 2, exchange=[oth_out, oth_small])
    arr_up = _exchange_wait(up_sems, up_src, up_land, [dw_in_t], "exchange_up_wait")
    own_in, oth_in = _pair_reduce([dw_in_t.reshape(N_DEV, ROWS_IN, D)], "pair_reduce_in")
    in_sems, in_src, in_land, started = _exchange_start(oth_in, "exchange_in_start", CHIP_BARRIER)
    (small_piece,) = _sum_parts([own_small], [arr_small], "sum_small_grads")
    (gs_small,) = _all_gather([small_piece], "gather_small_grads")

    names = ["w_in", "w_pool", "pool_scale", "w_out", "ln1_g", "ln1_b", "w_up", "conv_w", "conv_b", "w_down",
             "ln2_g", "ln2_b"]
    w_d = dict(w_in=w_in, w_pool=w_pool, pool_scale=pool_scale, w_out=w_out, ln1_g=ln1_g, ln1_b=ln1_b, w_up=w_up,
               conv_w=conv_w, conv_b=conv_b, w_down=w_down, ln2_g=ln2_g, ln2_b=ln2_b)
    m_d = dict(w_in=m_w_in, w_pool=m_w_pool, pool_scale=m_pool_scale, w_out=m_w_out, ln1_g=m_ln1_g, ln1_b=m_ln1_b,
               w_up=m_w_up, conv_w=m_conv_w, conv_b=m_conv_b, w_down=m_w_down, ln2_g=m_ln2_g, ln2_b=m_ln2_b)
    v_d = dict(w_in=v_w_in, w_pool=v_w_pool, pool_scale=v_pool_scale, w_out=v_w_out, ln1_g=v_ln1_g, ln1_b=v_ln1_b,
               w_up=v_w_up, conv_w=v_conv_w, conv_b=v_conv_b, w_down=v_w_down, ln2_g=v_ln2_g, ln2_b=v_ln2_b)
    g_d, delta, new_m, new_v = {}, {}, {}, {}

    def big_adamw(k, own, arr, transposed, steps, after=()):
        lay = (lambda a: a[0].T) if transposed else (lambda a: a[0])
        back = (lambda a: a.T[None]) if transposed else (lambda a: a[None])
        res = _sum_adamw(own, arr, lay(w_d[k]), lay(m_d[k]), lay(v_d[k]), "adamw_" + k, steps, after)
        g_d[k], delta[k], new_m[k], new_v[k] = (back(r) for r in res)
        return res[3]

    done = [big_adamw("w_up", own_up, arr_up, True, 4, after=(started,)),
            big_adamw("w_down", own_down, arr_down, False, 2, after=(started,)),
            big_adamw("w_out", own_out, arr_out, False, 2, after=(started,))]

    gsm = _unpack(gs_small)
    gsm["conv_w"] = lax.dynamic_slice(gsm["conv_w"].reshape(3, D_FF), (0, me * (D_FF // N_DEV)), (3, D_FF // N_DEV))
    lay = lambda k, a: jnp.transpose(a, (1, 0, 2)) if k == "conv_w" else a.reshape(-1, a.shape[-1])
    back = lambda k, a: jnp.transpose(a, (1, 0, 2)) if k == "conv_w" else a.reshape(w_d[k].shape)
    group = [k for k in names if k not in ("w_in", "w_out", "w_up", "w_down")]
    for k in group:
        g_d[k] = gsm[k].reshape(w_d[k].shape)
    res = _adamw([lay(k, w_d[k]) for k in group], [lay(k, g_d[k]) for k in group], [lay(k, m_d[k]) for k in group],
                 [lay(k, v_d[k]) for k in group], "adamw_small")
    for j, k in enumerate(group):
        delta[k], new_m[k], new_v[k] = (back(k, res[part * len(group) + j]) for part in range(3))

    arr_in = _exchange_wait(in_sems, in_src, in_land, done + [res[0]], "exchange_in_wait")
    big_adamw("w_in", own_in, arr_in, True, 4)

    loss = gsm["loss"].reshape(())
    return (loss, grad_x[None], *[g_d[k] for k in names], *[delta[k] for k in names], *[new_m[k] for k in names],
            *[new_v[k] for k in names])
```

```python
import math

import numpy as np
import jax
import jax.numpy as jnp
from jax import lax
from jax.experimental import pallas as pl
from jax.experimental.pallas import tpu as pltpu

f32 = jnp.float32
bf16 = jnp.bfloat16

N_DEV = 8
T = 4096
D = 1024
CHUNK = 64
MIX_TILE = 512
RET_TILE = 256
HEADS = 4
DH = 128
RW = HEADS * DH
PW = 512
GROUPS = 4
WINDOWS = (2, 4, 8, 16)
IN_W = 4 * RW + PW
D_FF = 2816
LN_EPS = 1e-5
RMS_EPS = 1e-6
ALPHA = 2.0 ** 0.25
K_SCALE = DH ** -0.5

ADAM_LR = 0.001
ADAM_B1 = 0.9
ADAM_B2 = 0.999
ADAM_EPS = 1e-08
ADAM_WD = 0.01
ADAM_STEP = 10

ROWS_IN, ROWS_OUT, ROWS_UP, ROWS_DOWN = IN_W // N_DEV, D // N_DEV, 2 * D_FF // N_DEV, D_FF // N_DEV

V7X_VMEM_LIMIT = 56 * 2 ** 20
HALO = 32

NT = (((1,), (1,)), ((), ()))
TN = (((0,), (0,)), ((), ()))
NN = (((1,), (0,)), ((), ()))


def _dot(a, b, dims=NN):
    return lax.dot_general(a, b, dims, preferred_element_type=f32)


def _const_spec(shape):
    zeros = (0,) * len(shape)
    return pl.BlockSpec(shape, lambda i: zeros, pipeline_mode=pl.Buffered(1))


def _sigmoid(x):
    return 0.5 * jnp.tanh(0.5 * x) + 0.5


def _decay_tables(tt):
    h = np.arange(HEADS, dtype=np.float64)
    log_gamma = np.log(1.0 - 2.0 ** (-5.0 - h)).astype(np.float32).astype(np.float64)[:, None, None]
    idx = np.arange(tt, dtype=np.float64)
    visible = (idx[None, :] // CHUNK) <= (idx[:, None] // CHUNK)
    mask = np.where(visible[None], np.exp(log_gamma * np.abs(idx[:, None] - idx[None, :])[None]), 0.0)
    qd = np.broadcast_to(np.exp(log_gamma * (idx[None, :, None] + 1.0)), (HEADS, tt, DH))
    kd = np.broadcast_to(np.exp(log_gamma * (tt - 1.0 - idx[None, :, None])), (HEADS, tt, DH))
    cd = np.exp(log_gamma[:, 0, 0] * tt)
    return (jnp.asarray(mask, f32), jnp.asarray(qd, f32), jnp.asarray(kd, f32), [float(c) for c in cd])


def _rope_tables():
    inv_freq = (10000.0 ** (-np.arange(0, DH, 2, dtype=np.float64) / DH)).astype(np.float32)
    ang = (np.arange(T, dtype=np.float32)[:, None] * inv_freq[None, :]).astype(np.float64)
    cos, sin = np.cos(ang), np.sin(ang)
    return (jnp.asarray(np.concatenate([cos, cos], axis=1), f32), jnp.asarray(np.concatenate([-sin, sin], axis=1), f32))


def _swap_halves(t):
    return pltpu.roll(t, DH // 2, axis=1)


def _mix_forward(x, w_in_t, cos, sin, dmat, qd, kd, cdec, w_pool, pool_scale, w_out, ln1_g, ln1_b, gather,
                 tt=MIX_TILE):
    n_tiles = T // tt
    n_g = len(gather)

    def body(x_ref, wint_ref, cos_ref, sin_ref, dmat_ref, qd_ref, kd_ref, wpool_ref, pscale_ref, wout_ref,
             g1_ref, b1_ref, *rest):
        gin, rest = rest[:n_g], rest[n_g:]
        qkv_ref, g_ref, oret_ref, states_ref, cat_ref, pooled_ref, xhat_ref, rstd_ref, x1b_ref = rest[:9]
        gout, (state_s, pext_s, tmp_s, *sems) = rest[9:9 + n_g], rest[9 + n_g:]
        i = pl.program_id(0)

        @pl.when(i == 0)
        def _():
            state_s[...] = jnp.zeros_like(state_s)
            pext_s[:, pl.ds(0, HALO), :] = jnp.zeros((GROUPS, HALO, DH), f32)
            _gather_start(gin, gout, *sems)

        @pl.when(i == n_tiles - 2)
        def _():
            _gather_forward(gin, gout, *sems)

        xb = x_ref[...].astype(bf16)
        cos_t, sin_t = cos_ref[...], sin_ref[...]
        for part in range(2):
            pr = _dot(xb, wint_ref[pl.ds(part * RW, RW), :], NT)
            for h in range(HEADS):
                t = pr[:, h * DH:(h + 1) * DH]
                r = t * cos_t + _swap_halves(t) * sin_t
                if part == 1:
                    r = r * K_SCALE
                qkv_ref[:, part * RW + h * DH: part * RW + (h + 1) * DH] = r.astype(bf16)
        qkv_ref[:, 2 * RW:3 * RW] = _dot(xb, wint_ref[pl.ds(2 * RW, RW), :], NT).astype(bf16)
        g_ref[...] = _dot(xb, wint_ref[pl.ds(3 * RW, RW), :], NT)
        p = _dot(xb, wint_ref[pl.ds(4 * RW, PW), :], NT)
        for gi in range(GROUPS):
            pext_s[gi, pl.ds(HALO, tt), :] = p[:, gi * DH:(gi + 1) * DH]

        for sub in range(tt // RET_TILE):
            rows = pl.ds(sub * RET_TILE, RET_TILE)
            for h in range(HEADS):
                q = qkv_ref[rows, h * DH:(h + 1) * DH]
                k = qkv_ref[rows, RW + h * DH: RW + (h + 1) * DH]
                v = qkv_ref[rows, 2 * RW + h * DH: 2 * RW + (h + 1) * DH]
                s = _dot(q, k, NT) * dmat_ref[h]
                st = state_s[h]
                stb = st.astype(bf16)
                states_ref[sub, h] = stb
                oret_ref[rows, h * DH:(h + 1) * DH] = (_dot(s.astype(bf16), v)
                                                      + _dot((q.astype(f32) * qd_ref[h]).astype(bf16), stb))
                state_s[h] = st * cdec[h] + _dot((k.astype(f32) * kd_ref[h]).astype(bf16), v, TN)

        for h in range(HEADS):
            sl = slice(h * DH, (h + 1) * DH)
            o = oret_ref[:, sl]
            r = lax.rsqrt(jnp.mean(o * o, axis=-1, keepdims=True) + RMS_EPS)
            gg = g_ref[:, sl]
            cat_ref[:, sl] = (o * r * (gg * _sigmoid(gg))).astype(bf16)

        pos1 = (i * tt + lax.broadcasted_iota(jnp.int32, (tt, 1), 0) + 1).astype(f32)
        for gi, w in enumerate(WINDOWS):
            sl = slice(gi * DH, (gi + 1) * DH)
            stages = int(math.log2(w))
            src = pext_s
            for s in range(stages):
                lo = HALO - 8 * (stages - 1 - s)
                n = tt + HALO - lo
                shift = 2 ** s
                val = src[gi, pl.ds(lo, n), :] + src[gi, pl.ds(lo - shift, n), :]
                if s == stages - 1:
                    wsum = val
                else:
                    tmp_s[gi, pl.ds(lo, n), :] = val
                    src = tmp_s
            p_g = pext_s[gi, pl.ds(HALO, tt), :]
            pooled = (wsum / jnp.minimum(pos1, float(w)) - p_g).astype(bf16)
            pooled_ref[:, sl] = pooled
            y = _dot(pooled, wpool_ref[gi]) * pscale_ref[:, sl]
            cat_ref[:, RW + gi * DH: RW + (gi + 1) * DH] = y.astype(bf16)
        pext_s[:, pl.ds(0, HALO), :] = pext_s[:, pl.ds(tt, HALO), :]

        z = ALPHA * x_ref[...] + _dot(cat_ref[...], wout_ref[...])
        mu = jnp.mean(z, axis=-1, keepdims=True)
        zc = z - mu
        rstd = lax.rsqrt(jnp.mean(zc * zc, axis=-1, keepdims=True) + LN_EPS)
        xhat = zc * rstd
        xhat_ref[...] = xhat
        rstd_ref[...] = rstd
        x1b_ref[...] = (xhat * g1_ref[...] + b1_ref[...]).astype(bf16)

        @pl.when(i == n_tiles - 1)
        def _():
            _gather_finish(gin, gout, *sems)

    tile = lambda w: pl.BlockSpec((tt, w), lambda i: (i, 0))
    hbm = pl.BlockSpec(memory_space=pltpu.HBM)
    out_shape = (
        jax.ShapeDtypeStruct((T, 3 * RW), bf16),
        jax.ShapeDtypeStruct((T, RW), f32),
        jax.ShapeDtypeStruct((T, RW), f32),
        jax.ShapeDtypeStruct((T // RET_TILE, HEADS, DH, DH), bf16),
        jax.ShapeDtypeStruct((T, D), bf16),
        jax.ShapeDtypeStruct((T, PW), bf16),
        jax.ShapeDtypeStruct((T, D), f32),
        jax.ShapeDtypeStruct((T, 1), f32),
        jax.ShapeDtypeStruct((T, D), bf16),
    ) + tuple(jax.ShapeDtypeStruct((N_DEV,) + b.shape, b.dtype) for b in gather)
    return pl.pallas_call(
        body, name="mix_forward", grid=(n_tiles,), out_shape=out_shape,
        in_specs=[tile(D), _const_spec((IN_W, D)), tile(DH), tile(DH),
                  _const_spec((HEADS, RET_TILE, RET_TILE)), _const_spec((HEADS, RET_TILE, DH)),
                  _const_spec((HEADS, RET_TILE, DH)),
                  _const_spec((GROUPS, DH, DH)), _const_spec((1, PW)), _const_spec((D, D)),
                  _const_spec((1, D)), _const_spec((1, D))] + [hbm] * n_g,
        out_specs=(tile(3 * RW), tile(RW), tile(RW),
                   pl.BlockSpec((tt // RET_TILE, HEADS, DH, DH), lambda i: (i, 0, 0, 0)),
                   tile(D), tile(PW), tile(D), tile(1), tile(D)) + (hbm,) * n_g,
        scratch_shapes=[pltpu.VMEM((HEADS, DH, DH), f32), pltpu.VMEM((GROUPS, tt + HALO, DH), f32),
                        pltpu.VMEM((GROUPS, tt + HALO, DH), f32)] + _gather_sems(n_g),
        compiler_params=pltpu.CompilerParams(dimension_semantics=("arbitrary",), vmem_limit_bytes=V7X_VMEM_LIMIT,
                                             collective_id=GATHER_BARRIER),
    )(x, w_in_t, cos, sin, dmat, qd, kd, w_pool, pool_scale, w_out, ln1_g, ln1_b, *gather)


def _ffn_forward_backward(xhat1, rstd1, ln1_g, ln1_b, w_up_t, conv_w, conv_b, w_down, ln2_g, ln2_b, target,
                          tt=256):
    n_tiles = T // tt
    FH = 16
    hb = tt // FH

    def body(xhat_ref, halo_ref, rstd_ref, g1_ref, b1_ref, wupt_ref, cw_ref, cb_ref, wdown_ref, g2_ref, b2_ref, tgt_ref,
             dz1_ref, dz2b_ref, du_ref, f_ref, loss_ref, dg2_ref, db2_ref, dg1_ref, db1_ref, dcb_ref, dcw_ref,
             gext_s, val_s, dhext_s):
        i = pl.program_id(0)
        tile_idx = n_tiles - 1 - i

        def rd(ref, off):
            return jnp.concatenate([ref[k, pl.ds(off, tt), :] for k in range(D_FF // 128)], axis=1)

        def wr(ref, val):
            for k in range(D_FF // 128):
                ref[k, pl.ds(0, val.shape[0]), :] = val[:, k * 128:(k + 1) * 128]

        @pl.when(i == 0)
        def _():
            for r in (loss_ref, dg2_ref, db2_ref, dg1_ref, db1_ref, dcb_ref, dcw_ref):
                r[...] = jnp.zeros_like(r)
            dhext_s[:, pl.ds(tt, 8), :] = jnp.zeros((D_FF // 128, 8, 128), f32)

        g1, b1 = g1_ref[...], b1_ref[...]
        xhat = xhat_ref[...]
        x1 = xhat * g1 + b1
        x1b = x1.astype(bf16)
        x1h = ((halo_ref[...] * g1 + b1) * jnp.where(tile_idx == 0, 0.0, 1.0)).astype(bf16)
        x1ext = jnp.concatenate([x1h, x1b], axis=0)

        val = _dot(x1b, wupt_ref[pl.ds(0, D_FF), :], NT)
        gate_ext = _dot(x1ext, wupt_ref[pl.ds(D_FF, D_FF), :], NT)
        wr(gext_s, gate_ext)
        hh = (cb_ref[...] + cw_ref[0:1, :] * rd(gext_s, FH - 2) + cw_ref[1:2, :] * rd(gext_s, FH - 1)
              + cw_ref[2:3, :] * gate_ext[FH:])
        sg = _sigmoid(hh)
        act = hh * sg
        wr(dhext_s, act)
        val_s[...] = val * (sg + act * (1.0 - sg))
        fb = (act * val).astype(bf16)
        f_ref[...] = fb

        z = ALPHA * x1 + _dot(fb, wdown_ref[...])
        mu = jnp.mean(z, axis=-1, keepdims=True)
        zc = z - mu
        rstd2 = lax.rsqrt(jnp.mean(zc * zc, axis=-1, keepdims=True) + LN_EPS)
        xh2 = zc * rstd2
        diff = xh2 * g2_ref[...] + b2_ref[...] - tgt_ref[...]
        loss_ref[...] += 0.5 * jnp.sum(diff * diff) / D
        dy = diff * (1.0 / D)
        dg2_ref[...] += jnp.sum(dy * xh2, axis=0, keepdims=True)
        db2_ref[...] += jnp.sum(dy, axis=0, keepdims=True)
        dyg = dy * g2_ref[...]
        dz2 = rstd2 * (dyg - jnp.mean(dyg, axis=-1, keepdims=True) - xh2 * jnp.mean(dyg * xh2, axis=-1, keepdims=True))
        dz2b = dz2.astype(bf16)
        dz2b_ref[...] = dz2b

        df = _dot(dz2b, wdown_ref[...], NT)
        dval = df * rd(dhext_s, 0)
        dh = df * val_s[...]
        wr(dhext_s, dh)
        dh1, dh2, g0 = rd(dhext_s, 1), rd(dhext_s, 2), rd(gext_s, FH)
        dcb_ref[...] += jnp.sum(dh, axis=0, keepdims=True)
        dcw_ref[0:1, :] += jnp.sum(dh2 * g0, axis=0, keepdims=True)
        dcw_ref[1:2, :] += jnp.sum(dh1 * g0, axis=0, keepdims=True)
        dcw_ref[2:3, :] += jnp.sum(dh * g0, axis=0, keepdims=True)
        dgate = cw_ref[2:3, :] * dh + cw_ref[1:2, :] * dh1 + cw_ref[0:1, :] * dh2
        dvalb, dgateb = dval.astype(bf16), dgate.astype(bf16)
        du_ref[:, :D_FF] = dvalb
        du_ref[:, D_FF:] = dgateb
        dx1 = ALPHA * dz2 + _dot(dvalb, wupt_ref[pl.ds(0, D_FF), :]) + _dot(dgateb, wupt_ref[pl.ds(D_FF, D_FF), :])
        dhext_s[:, pl.ds(tt, 8), :] = dhext_s[:, pl.ds(0, 8), :]

        dg1_ref[...] += jnp.sum(dx1 * xhat, axis=0, keepdims=True)
        db1_ref[...] += jnp.sum(dx1, axis=0, keepdims=True)
        dxg = dx1 * g1
        dz1_ref[...] = rstd_ref[...] * (dxg - jnp.mean(dxg, axis=-1, keepdims=True)
                                        - xhat * jnp.mean(dxg * xhat, axis=-1, keepdims=True))

    rtile = lambda w: pl.BlockSpec((tt, w), lambda i: (n_tiles - 1 - i, 0))
    acc = lambda shape: pl.BlockSpec(shape, lambda i: (0, 0))
    out_shape = (
        jax.ShapeDtypeStruct((T, D), f32),
        jax.ShapeDtypeStruct((T, D), bf16),
        jax.ShapeDtypeStruct((T, 2 * D_FF), bf16),
        jax.ShapeDtypeStruct((T, D_FF), bf16),
        jax.ShapeDtypeStruct((8, 128), f32),
        jax.ShapeDtypeStruct((1, D), f32), jax.ShapeDtypeStruct((1, D), f32),
        jax.ShapeDtypeStruct((1, D), f32), jax.ShapeDtypeStruct((1, D), f32),
        jax.ShapeDtypeStruct((1, D_FF), f32), jax.ShapeDtypeStruct((3, D_FF), f32),
    )
    return pl.pallas_call(
        body, name="ffn_forward_backward", grid=(n_tiles,), out_shape=out_shape,
        in_specs=[rtile(D),
                  pl.BlockSpec((FH, D), lambda i: (jnp.maximum((n_tiles - 1 - i) * hb - 1, 0), 0)),
                  rtile(1), _const_spec((1, D)), _const_spec((1, D)), _const_spec((2 * D_FF, D)),
                  _const_spec((3, D_FF)), _const_spec((1, D_FF)), _const_spec((D_FF, D)),
                  _const_spec((1, D)), _const_spec((1, D)), rtile(D)],
        out_specs=(rtile(D), rtile(D), rtile(2 * D_FF), rtile(D_FF), acc((8, 128)),
                   acc((1, D)), acc((1, D)), acc((1, D)), acc((1, D)), acc((1, D_FF)), acc((3, D_FF))),
        scratch_shapes=[pltpu.VMEM((D_FF // 128, tt + FH, 128), f32), pltpu.VMEM((tt, D_FF), f32),
                        pltpu.VMEM((D_FF // 128, tt + 8, 128), f32)],
        compiler_params=pltpu.CompilerParams(dimension_semantics=("arbitrary",), vmem_limit_bytes=V7X_VMEM_LIMIT),
    )(xhat1, xhat1, rstd1, ln1_g, ln1_b, w_up_t, conv_w, conv_b, w_down, ln2_g, ln2_b, target)


def _mix_backward(dz1, w_out, qkv, g, oret, states, pooled, cat, cos, sin, dmat, qd, kd, cdec, w_pool, pool_scale, w_in_t,
                  after, tt=MIX_TILE):
    n_tiles = T // tt

    def body(dz1_ref, wout_ref, qkv_ref, g_ref, oret_ref, states_ref, pooled_ref, cat_ref, cos_ref, sin_ref, dmat_ref,
             qd_ref, kd_ref, wpool_ref, pscale_ref, wint_ref, after_ref,
             dproj_ref, gx_ref, dwpool_ref, dpscale_ref, dwout_ref, dstate_s, dout_s, eext_s, tmp_s, dwout_s):
        i = pl.program_id(0)
        tile_idx = n_tiles - 1 - i

        @pl.when(i == 0)
        def _():
            dstate_s[...] = jnp.zeros_like(dstate_s)
            dwpool_ref[...] = jnp.zeros_like(dwpool_ref)
            dpscale_ref[...] = jnp.zeros_like(dpscale_ref)
            dwout_s[...] = jnp.zeros_like(dwout_s)
            eext_s[:, pl.ds(tt, HALO), :] = jnp.zeros((GROUPS, HALO, DH), f32)

        dz1 = dz1_ref[...]
        dz1b = dz1.astype(bf16)
        dcat = _dot(dz1b, wout_ref[...], NT)
        dwout_s[...] += _dot(cat_ref[...], dz1b, TN)

        pos1 = (tile_idx * tt + lax.broadcasted_iota(jnp.int32, (tt, 1), 0) + 1).astype(f32)
        for gi, w in enumerate(WINDOWS):
            sl = slice(gi * DH, (gi + 1) * DH)
            dpo = dcat[:, RW + gi * DH: RW + (gi + 1) * DH]
            pooled_g = pooled_ref[:, sl]
            ylin = _dot(pooled_g, wpool_ref[gi])
            dpscale_ref[:, sl] += jnp.sum(dpo * ylin, axis=0, keepdims=True)
            dpw = (dpo * pscale_ref[:, sl]).astype(bf16)
            dwpool_ref[gi] += _dot(pooled_g, dpw, TN)
            dpooled = _dot(dpw, wpool_ref[gi], NT)
            eext_s[gi, pl.ds(0, tt), :] = dpooled / jnp.minimum(pos1, float(w))
            stages = int(math.log2(w))
            src = eext_s
            for s in range(stages):
                n = tt + 8 * (stages - 1 - s)
                shift = 2 ** s
                val = src[gi, pl.ds(0, n), :] + src[gi, pl.ds(shift, n), :]
                if s == stages - 1:
                    wsum = val
                else:
                    tmp_s[gi, pl.ds(0, n), :] = val
                    src = tmp_s
            dproj_ref[:, 4 * RW + gi * DH: 4 * RW + (gi + 1) * DH] = (wsum - dpooled).astype(bf16)
        eext_s[:, pl.ds(tt, HALO), :] = eext_s[:, pl.ds(0, HALO), :]

        for h in range(HEADS):
            sl = slice(h * DH, (h + 1) * DH)
            dr = dcat[:, sl]
            o = oret_ref[:, sl]
            r = lax.rsqrt(jnp.mean(o * o, axis=-1, keepdims=True) + RMS_EPS)
            rn = o * r
            gg = g_ref[:, sl]
            sg = _sigmoid(gg)
            dproj_ref[:, 3 * RW + h * DH: 3 * RW + (h + 1) * DH] = (dr * rn * (sg * (1.0 + gg * (1.0 - sg)))).astype(bf16)
            drn = dr * (gg * sg)
            dout_s[:, sl] = (r * (drn - rn * jnp.mean(drn * rn, axis=-1, keepdims=True))).astype(bf16)

        for sub in reversed(range(tt // RET_TILE)):
            rows = pl.ds(sub * RET_TILE, RET_TILE)
            cos_t, sin_t = cos_ref[rows, :], sin_ref[rows, :]
            for h in range(HEADS):
                q = qkv_ref[rows, h * DH:(h + 1) * DH]
                k = qkv_ref[rows, RW + h * DH: RW + (h + 1) * DH]
                v = qkv_ref[rows, 2 * RW + h * DH: 2 * RW + (h + 1) * DH]
                do = dout_s[rows, h * DH:(h + 1) * DH]
                stb = states_ref[sub, h]
                dst = dstate_s[h]
                dstb = dst.astype(bf16)
                sb = (_dot(q, k, NT) * dmat_ref[h]).astype(bf16)
                dsb = (_dot(do, v, NT) * dmat_ref[h]).astype(bf16)
                dq = _dot(dsb, k) + _dot(do, stb, NT) * qd_ref[h]
                dk = _dot(dsb, q, TN) + _dot(v, dstb, NT) * kd_ref[h]
                dv = _dot(sb, do, TN) + _dot((k.astype(f32) * kd_ref[h]).astype(bf16), dstb)
                dstate_s[h] = dst * cdec[h] + _dot((q.astype(f32) * qd_ref[h]).astype(bf16), do, TN)
                dproj_ref[rows, h * DH:(h + 1) * DH] = (dq * cos_t - _swap_halves(dq) * sin_t).astype(bf16)
                dproj_ref[rows, RW + h * DH: RW + (h + 1) * DH] = (
                    (dk * cos_t - _swap_halves(dk) * sin_t) * K_SCALE).astype(bf16)
                dproj_ref[rows, 2 * RW + h * DH: 2 * RW + (h + 1) * DH] = dv.astype(bf16)

        gx_ref[...] = ALPHA * dz1 + _dot(dproj_ref[...], wint_ref[...])

        @pl.when(i == n_tiles - 1)
        def _():
            dwout_ref[...] = dwout_s[...].astype(bf16)

    rtile = lambda w: pl.BlockSpec((tt, w), lambda i: (n_tiles - 1 - i, 0))
    out_shape = (
        jax.ShapeDtypeStruct((T, IN_W), bf16),
        jax.ShapeDtypeStruct((T, D), f32),
        jax.ShapeDtypeStruct((GROUPS, DH, DH), f32),
        jax.ShapeDtypeStruct((1, PW), f32),
        jax.ShapeDtypeStruct((D, D), bf16),
    )
    return pl.pallas_call(
        body, name="mix_backward", grid=(n_tiles,), out_shape=out_shape,
        in_specs=[rtile(D), _const_spec((D, D)), rtile(3 * RW), rtile(RW), rtile(RW),
                  pl.BlockSpec((tt // RET_TILE, HEADS, DH, DH), lambda i: (n_tiles - 1 - i, 0, 0, 0)),
                  rtile(PW), rtile(D), rtile(DH), rtile(DH),
                  _const_spec((HEADS, RET_TILE, RET_TILE)), _const_spec((HEADS, RET_TILE, DH)),
                  _const_spec((HEADS, RET_TILE, DH)),
                  _const_spec((GROUPS, DH, DH)), _const_spec((1, PW)), _const_spec((IN_W, D)),
                  pl.BlockSpec(memory_space=pl.ANY)],
        out_specs=(rtile(IN_W), rtile(D), pl.BlockSpec((GROUPS, DH, DH), lambda i: (0, 0, 0)),
                   pl.BlockSpec((1, PW), lambda i: (0, 0)),
                   pl.BlockSpec((D, D), lambda i: (0, 0), pipeline_mode=pl.Buffered(1))),
        scratch_shapes=[pltpu.VMEM((HEADS, DH, DH), f32), pltpu.VMEM((tt, RW), bf16),
                        pltpu.VMEM((GROUPS, tt + HALO, DH), f32), pltpu.VMEM((GROUPS, tt + HALO, DH), f32),
                        pltpu.VMEM((D, D), f32)],
        compiler_params=pltpu.CompilerParams(dimension_semantics=("arbitrary",), vmem_limit_bytes=V7X_VMEM_LIMIT),
    )(dz1, w_out, qkv, g, oret, states, pooled, cat, cos, sin, dmat, qd, kd, w_pool, pool_scale, w_in_t, after)


def _weight_grad(a, b, name, tm, exchange=(), tk=2048):
    m = a.shape[1]
    n_m, n_k, n_e = m // tm, T // tk, len(exchange)

    def body(a_ref, b_ref, *rest):
        ein, o_ref, eout, (acc_s, *sems) = rest[:n_e], rest[n_e], rest[n_e + 1:2 * n_e + 1], rest[2 * n_e + 1:]
        i, k = pl.program_id(0), pl.program_id(1)

        if n_e:
            @pl.when((i == 0) & (k == 0))
            def _():
                _chip_exchange_start(ein, eout, *sems)

        @pl.when(k == 0)
        def _():
            acc_s[...] = jnp.zeros_like(acc_s)

        acc_s[...] += _dot(a_ref[...], b_ref[pl.ds(pl.multiple_of(k * tk, tk), tk), :].astype(bf16), TN)

        @pl.when(k == n_k - 1)
        def _():
            o_ref[...] = acc_s[...].astype(bf16)

        if n_e:
            @pl.when((i == n_m - 1) & (k == n_k - 1))
            def _():
                _chip_exchange_finish(ein, eout, *sems)

    hbm = pl.BlockSpec(memory_space=pltpu.HBM)
    return pl.pallas_call(
        body, name=name, grid=(n_m, n_k),
        out_shape=(jax.ShapeDtypeStruct((m, D), bf16),) + tuple(jax.ShapeDtypeStruct(e.shape, e.dtype) for e in exchange),
        in_specs=[pl.BlockSpec((tk, tm), lambda i, k: (k, i)),
                  pl.BlockSpec((T, D), lambda i, k: (0, 0), pipeline_mode=pl.Buffered(1))] + [hbm] * n_e,
        out_specs=(pl.BlockSpec((tm, D), lambda i, k: (i, 0)),) + (hbm,) * n_e,
        scratch_shapes=[pltpu.VMEM((tm, D), f32)] + _chip_exchange_sems(n_e),
        compiler_params=pltpu.CompilerParams(dimension_semantics=("arbitrary", "arbitrary"),
                                             vmem_limit_bytes=V7X_VMEM_LIMIT,
                                             collective_id=CHIP_BARRIER if n_e else None),
    )(a, b, *exchange)


CHIP_FLIPS = ((1, 0), (0, 1), (1, 1))
PAIR_BARRIER, CHIP_BARRIER, GATHER_BARRIER, CHIP_BARRIER_SPLIT = 0, 1, 2, 3


def _barrier(peers):
    sem = pltpu.get_barrier_semaphore()
    for peer in peers:
        pl.semaphore_signal(sem, inc=1, device_id=peer, device_id_type=pl.DeviceIdType.MESH)
    pl.semaphore_wait(sem, len(peers))


def _me():
    return lax.axis_index("x"), lax.axis_index("y"), lax.axis_index("c")


def _chip(me, k):
    x, y, _ = me
    if k == 0:
        return x, y
    fx, fy = CHIP_FLIPS[k - 1]
    return (1 - x if fx else x), (1 - y if fy else y)


def _slot(x, y, c):
    return 4 * x + 2 * y + c


def _remote(src, dst, send_sem, recv_sem, to):
    return pltpu.make_async_remote_copy(src_ref=src, dst_ref=dst, send_sem=send_sem, recv_sem=recv_sem,
                                        device_id=to, device_id_type=pl.DeviceIdType.MESH)


def _gather_sems(n):
    return [pltpu.SemaphoreType.DMA((7, n)), pltpu.SemaphoreType.DMA((7, n)), pltpu.SemaphoreType.DMA((n,))] if n else []


def _gather_copy(k, j, gin, gout, send_sems, recv_sems, sending):
    x, y, c = _me()
    sibling, x_chip, y_chip, d_chip = (x, y, 1 - c), (1 - x, y), (x, 1 - y), (1 - x, 1 - y)
    south = c == 0
    passed_on = (jnp.where(south, 1 - x, x), jnp.where(south, y, 1 - y), c)
    src, to = gin[j], sibling
    if sending:
        block = {0: (x, y, c), 1: (x, y, c), 2: (x, y, c), 3: passed_on, 4: (*x_chip, c), 5: (*y_chip, c), 6: (*d_chip, c)}[k]
        to = {1: (*x_chip, c), 2: (*y_chip, c), 3: (jnp.where(south, x, 1 - x), jnp.where(south, 1 - y, y), c)}.get(k, sibling)
        if k >= 3:
            src = gout[j].at[_slot(*block)]
    else:
        block = {0: sibling, 1: (*x_chip, c), 2: (*y_chip, c), 3: (*d_chip, c), 4: (*x_chip, 1 - c), 5: (*y_chip, 1 - c),
                 6: (*d_chip, 1 - c)}[k]
    return _remote(src, gout[j].at[_slot(*block)], send_sems.at[k, j], recv_sems.at[k, j], to)


def _gather_do(ks, action, gin, gout, send_sems, recv_sems):
    for k in ks:
        for j in range(len(gin)):
            cp = _gather_copy(k, j, gin, gout, send_sems, recv_sems, action != "wait_recv")
            getattr(cp, action)()


def _gather_peers():
    x, y, c = _me()
    return [(x, y, 1 - c), (1 - x, y, c), (x, 1 - y, c)]


def _gather_start(gin, gout, send_sems, recv_sems, local_sems, barrier=True):
    if barrier:
        _barrier(_gather_peers())
    for j in range(len(gin)):
        pltpu.make_async_copy(gin[j], gout[j].at[_slot(*_me())], local_sems.at[j]).start()
    _gather_do((0, 1, 2), "start", gin, gout, send_sems, recv_sems)


def _gather_forward(gin, gout, send_sems, recv_sems, local_sems):
    _gather_do((1, 2), "wait_recv", gin, gout, send_sems, recv_sems)
    _gather_do((3, 4, 5), "start", gin, gout, send_sems, recv_sems)


def _gather_finish(gin, gout, send_sems, recv_sems, local_sems):
    _gather_do((3,), "wait_recv", gin, gout, send_sems, recv_sems)
    _gather_do((6,), "start", gin, gout, send_sems, recv_sems)
    _gather_do((0, 4, 5, 6), "wait_recv", gin, gout, send_sems, recv_sems)
    _gather_do(range(7), "wait_send", gin, gout, send_sems, recv_sems)
    for j in range(len(gin)):
        pltpu.make_async_copy(gin[j], gout[j].at[_slot(*_me())], local_sems.at[j]).wait()


def _all_gather(blocks, name):
    n = len(blocks)

    def body(*refs):
        gin, gout, sems = refs[:n], refs[n:2 * n], refs[2 * n:]
        _gather_start(gin, gout, *sems)
        _gather_forward(gin, gout, *sems)
        _gather_finish(gin, gout, *sems)

    hbm = pl.BlockSpec(memory_space=pltpu.HBM)
    return pl.pallas_call(
        body, name=name,
        out_shape=tuple(jax.ShapeDtypeStruct((N_DEV,) + b.shape, b.dtype) for b in blocks),
        in_specs=[hbm] * n, out_specs=(hbm,) * n, scratch_shapes=_gather_sems(n),
        compiler_params=pltpu.CompilerParams(collective_id=GATHER_BARRIER),
    )(*blocks)


def _pair_reduce(parts, name):
    n = len(parts)

    def body(*refs):
        ins, own, others, landing, mine = (refs[k * n:(k + 1) * n] for k in range(5))
        send_sems, recv_sems, local_sems = refs[5 * n:]
        me = _me()
        x, y, c = me
        sibling = (x, y, 1 - c)
        _barrier([sibling])
        sends, loads = [], []
        for k in range(4):
            for j in range(n):
                cp = _remote(ins[j].at[_slot(*_chip(me, k), 1 - c)], landing[j].at[k], send_sems.at[k, j],
                             recv_sems.at[k, j], sibling)
                cp.start()
                sends.append(cp)
                ld = pltpu.make_async_copy(ins[j].at[_slot(*_chip(me, k), c)], mine[j].at[k], local_sems.at[k, j])
                ld.start()
                loads.append(ld)
        for k in range(4):
            for j in range(n):
                loads[k * n + j].wait()
                _remote(ins[j].at[0], landing[j].at[k], send_sems.at[k, j], recv_sems.at[k, j], sibling).wait_recv()
                total = mine[j][k].astype(f32) + landing[j][k].astype(f32)
                if k == 0:
                    own[j][...] = total.astype(own[j].dtype)
                else:
                    others[j][k - 1] = total.astype(others[j].dtype)
        for cp in sends:
            cp.wait_send()

    vm = pl.BlockSpec(memory_space=pltpu.VMEM)
    return pl.pallas_call(
        body, name=name,
        out_shape=tuple(jax.ShapeDtypeStruct(p.shape[1:], p.dtype) for p in parts)
        + tuple(jax.ShapeDtypeStruct((3,) + p.shape[1:], p.dtype) for p in parts),
        in_specs=[pl.BlockSpec(memory_space=pltpu.HBM)] * n, out_specs=(vm,) * (2 * n),
        scratch_shapes=[pltpu.VMEM((4,) + p.shape[1:], p.dtype) for p in parts] * 2
        + [pltpu.SemaphoreType.DMA((4, n)), pltpu.SemaphoreType.DMA((4, n)), pltpu.SemaphoreType.DMA((4, n))],
        compiler_params=pltpu.CompilerParams(vmem_limit_bytes=V7X_VMEM_LIMIT, collective_id=PAIR_BARRIER),
    )(*parts)


def _chip_exchange_sems(n):
    return [pltpu.SemaphoreType.DMA((3, n)), pltpu.SemaphoreType.DMA((3, n))] if n else []


def _chip_exchange_copy(k, j, ein, eout, send_sems, recv_sems):
    me = _me()
    return _remote(ein[j].at[k - 1], eout[j].at[k - 1], send_sems.at[k - 1, j], recv_sems.at[k - 1, j],
                   (*_chip(me, k), me[2]))


def _chip_peers():
    me = _me()
    return [(*_chip(me, k), me[2]) for k in range(1, 4)]


def _chip_exchange_start(ein, eout, send_sems, recv_sems, barrier=True):
    if barrier:
        _barrier(_chip_peers())
    for k in range(1, 4):
        for j in range(len(ein)):
            _chip_exchange_copy(k, j, ein, eout, send_sems, recv_sems).start()


def _chip_exchange_finish(ein, eout, send_sems, recv_sems):
    for k in range(1, 4):
        for j in range(len(ein)):
            _chip_exchange_copy(k, j, ein, eout, send_sems, recv_sems).wait_recv()
    for k in range(1, 4):
        for j in range(len(ein)):
            _chip_exchange_copy(k, j, ein, eout, send_sems, recv_sems).wait_send()


def _split_copies(src_ref, dst_ref, sems):
    me = _me()
    return [_remote(src_ref.at[k - 1], dst_ref.at[k - 1], sems[k - 1], sems[2 + k], (*_chip(me, k), me[2]))
            for k in range(1, 4)]


def _exchange_start(others, name, barrier_id):
    def body(src_ref, land_ref, *rest):
        sems, token_ref = rest[:6], rest[8]
        _barrier(_chip_peers())
        for copy in _split_copies(src_ref, land_ref, sems):
            copy.start()
        token_ref[...] = jnp.zeros_like(token_ref)

    hbm, sem = pl.BlockSpec(memory_space=pltpu.HBM), pl.BlockSpec(memory_space=pltpu.SEMAPHORE)
    thru = pltpu.HBM(others.shape, others.dtype)
    res = pl.pallas_call(
        body, name=name,
        out_shape=(pltpu.SemaphoreType.DMA(()),) * 6 + (thru, thru, jax.ShapeDtypeStruct((8, 128), f32)),
        in_specs=(hbm, hbm), out_specs=(sem,) * 6 + (hbm, hbm, pl.BlockSpec(memory_space=pltpu.VMEM)),
        input_output_aliases={0: 6, 1: 7},
        compiler_params=pltpu.CompilerParams(has_side_effects=pltpu.SideEffectType.DATAFLOW_SIDE_EFFECTING,
                                             collective_id=barrier_id),
    )(pltpu.with_memory_space_constraint(others, pltpu.HBM),
      pltpu.with_memory_space_constraint(lax.empty(others.shape, others.dtype), pltpu.HBM))
    return res[:6], res[6], res[7], res[8]


def _exchange_wait(sems, src_thru, land_thru, after, name):
    n_after = len(after)

    def body(src_ref, land_ref, *rest):
        for copy in _split_copies(src_ref, land_ref, rest[:6]):
            copy.wait_send()
            copy.wait_recv()

    hbm, sem = pl.BlockSpec(memory_space=pltpu.HBM), pl.BlockSpec(memory_space=pltpu.SEMAPHORE)
    thru = pltpu.HBM(src_thru.shape, src_thru.dtype)
    return pl.pallas_call(
        body, name=name, out_shape=(thru, thru),
        in_specs=(hbm, hbm) + (sem,) * 6 + (pl.BlockSpec(memory_space=pl.ANY),) * n_after, out_specs=(hbm, hbm),
        input_output_aliases={0: 0, 1: 1},
        compiler_params=pltpu.CompilerParams(has_side_effects=pltpu.SideEffectType.DATAFLOW_SIDE_EFFECTING),
    )(src_thru, land_thru, *sems, *after)[1]


def _sum_parts(owns, arrived, name):
    n = len(owns)

    def body(*refs):
        for own, arr, out in zip(refs[:n], refs[n:2 * n], refs[2 * n:]):
            acc = own[...].astype(f32)
            for k in range(3):
                acc = acc + arr[k].astype(f32)
            out[...] = acc

    vm = pl.BlockSpec(memory_space=pltpu.VMEM)
    return pl.pallas_call(
        body, name=name, out_shape=tuple(jax.ShapeDtypeStruct(o.shape, f32) for o in owns),
        in_specs=[vm] * (2 * n), out_specs=(vm,) * n,
        compiler_params=pltpu.CompilerParams(vmem_limit_bytes=V7X_VMEM_LIMIT),
    )(*owns, *arrived)


def _adam_update(w, g, m, v):
    m = ADAM_B1 * m + (1.0 - ADAM_B1) * g
    v = ADAM_B2 * v + (1.0 - ADAM_B2) * (g * g)
    m_hat = m / (1.0 - ADAM_B1 ** ADAM_STEP)
    v_hat = v / (1.0 - ADAM_B2 ** ADAM_STEP)
    return -ADAM_LR * (m_hat / (jnp.sqrt(v_hat) + ADAM_EPS) + ADAM_WD * w), m, v


def _sum_adamw(own, arrived, w, m, v, name, steps, after=()):
    rows = own.shape[0]
    br = rows // steps

    def body(own_ref, arr_ref, w_ref, m_ref, v_ref, *rest):
        g_out, d_out, m_out, v_out = rest[len(after):]
        g = own_ref[...].astype(f32)
        for k in range(3):
            g = g + arr_ref[k].astype(f32)
        g_out[...] = g
        d_out[...], m_out[...], v_out[...] = _adam_update(w_ref[...], g, m_ref[...], v_ref[...])

    blk = pl.BlockSpec((br, D), lambda i: (i, 0))
    return pl.pallas_call(
        body, name=name, grid=(steps,), out_shape=(jax.ShapeDtypeStruct((rows, D), f32),) * 4,
        in_specs=[blk, pl.BlockSpec((3, br, D), lambda i: (0, i, 0)), blk, blk, blk]
        + [pl.BlockSpec(memory_space=pl.ANY)] * len(after), out_specs=(blk,) * 4,
        compiler_params=pltpu.CompilerParams(dimension_semantics=("parallel",), vmem_limit_bytes=V7X_VMEM_LIMIT),
    )(own, arrived, w, m, v, *after)


def _adamw(ws, gs, ms, vs, name):
    n = len(ws)

    def body(*refs):
        w_r, g_r, m_r, v_r = (refs[k * n:(k + 1) * n] for k in range(4))
        d_o, m_o, v_o = (refs[(4 + k) * n:(5 + k) * n] for k in range(3))
        for j in range(n):
            d_o[j][...], m_o[j][...], v_o[j][...] = _adam_update(w_r[j][...], g_r[j][...], m_r[j][...], v_r[j][...])

    vm = pl.BlockSpec(memory_space=pltpu.VMEM)
    shapes = tuple(jax.ShapeDtypeStruct(w.shape, f32) for w in ws)
    return pl.pallas_call(
        body, name=name, out_shape=shapes * 3, in_specs=[vm] * (4 * n), out_specs=tuple([vm] * (3 * n)),
        compiler_params=pltpu.CompilerParams(vmem_limit_bytes=V7X_VMEM_LIMIT),
    )(*ws, *gs, *ms, *vs)


SMALL = (("w_pool", GROUPS * DH * DH), ("pool_scale", PW), ("ln1_g", D), ("ln1_b", D), ("conv_b", D_FF),
         ("ln2_g", D), ("ln2_b", D), ("conv_w", 3 * D_FF), ("loss", 1))
SMALL_ROWS = 640


def _pack(named):
    flat = jnp.concatenate([named[k].reshape(-1) for k, _ in SMALL])
    return jnp.pad(flat, (0, SMALL_ROWS * 128 - flat.shape[0])).reshape(SMALL_ROWS, 128)


def _unpack(packed):
    flat, out, at = packed.reshape(-1), {}, 0
    for k, size in SMALL:
        out[k] = flat[at:at + size]
        at += size
    return out


def kernel(x, w_in, w_pool, pool_scale, w_out, ln1_g, ln1_b, w_up, conv_w, conv_b, w_down, ln2_g, ln2_b, loss_target, m_w_in, m_w_pool, m_pool_scale, m_w_out, m_ln1_g, m_ln1_b, m_w_up, m_conv_w, m_conv_b, m_w_down, m_ln2_g, m_ln2_b, v_w_in, v_w_pool, v_pool_scale, v_w_out, v_ln1_g, v_ln1_b, v_w_up, v_conv_w, v_conv_b, v_w_down, v_ln2_g, v_ln2_b):
    me = 4 * lax.axis_index("x") + 2 * lax.axis_index("y") + lax.axis_index("c")
    x2, tgt = x[0], loss_target[0]

    g_in, g_out, g_cw = _all_gather([w_in[0].T.astype(bf16), w_out[0].astype(bf16), jnp.transpose(conv_w, (1, 0, 2))],
                                    "gather_weights")
    w_in_t = g_in.reshape(IN_W, D)
    w_out_f = g_out.reshape(D, D)
    conv_w_f = jnp.transpose(g_cw[:, :, 0, :], (1, 0, 2)).reshape(3, D_FF)
    w_pool_b = w_pool[0].astype(bf16)

    cos, sin = _rope_tables()
    dmat, qd, kd, cdec = _decay_tables(RET_TILE)

    qkv, g, oret, states, cat, pooled, xhat1, rstd1, x1b, g_up, g_down = _mix_forward(
        x2, w_in_t, cos, sin, dmat, qd, kd, cdec, w_pool_b, pool_scale, w_out_f, ln1_g, ln1_b,
        gather=[w_up[0].T.astype(bf16), w_down[0].astype(bf16)])
    w_up_t = g_up.reshape(2 * D_FF, D)
    w_down_f = g_down.reshape(D_FF, D)
    dz1, dz2b, du, f, loss8, d_ln2_g, d_ln2_b, d_ln1_g, d_ln1_b, d_conv_b, d_conv_w = _ffn_forward_backward(
        xhat1, rstd1, ln1_g, ln1_b, w_up_t, conv_w_f, conv_b, w_down_f, ln2_g, ln2_b, tgt)

    (dw_down,) = _weight_grad(f, dz2b, "grad_w_down", tm=D_FF // 2)
    own_down, oth_down = _pair_reduce([dw_down.reshape(N_DEV, ROWS_DOWN, D)], "pair_reduce_down")
    dw_up_t, arr_down = _weight_grad(du, x1b, "grad_w_up", tm=D_FF // 2, exchange=[oth_down])
    own_up, oth_up = _pair_reduce([dw_up_t.reshape(N_DEV, ROWS_UP, D)], "pair_reduce_up")
    up_sems, up_src, up_land, up_started = _exchange_start(oth_up, "exchange_up_start", CHIP_BARRIER_SPLIT)
    dproj, grad_x, d_w_pool, d_pool_scale, dw_out = _mix_backward(
        dz1, w_out_f, qkv, g, oret, states, pooled, cat, cos, sin, dmat, qd, kd, cdec, w_pool_b, pool_scale, w_in_t,
        after=up_started)
    small = _pack({"w_pool": d_w_pool, "pool_scale": d_pool_scale, "ln1_g": d_ln1_g, "ln1_b": d_ln1_b,
                   "conv_b": d_conv_b, "ln2_g": d_ln2_g, "ln2_b": d_ln2_b, "conv_w": d_conv_w, "loss": loss8[0, :1]})
    own_out, own_small, oth_out, oth_small = _pair_reduce(
        [dw_out.reshape(N_DEV, ROWS_OUT, D), small.reshape(N_DEV, SMALL_ROWS // N_DEV, 128)], "pair_reduce_out")
    dw_in_t, arr_out, arr_small = _weight_grad(dproj, x2, "grad_w_in", tm=IN_W // 2, exchange=[oth_out, oth_small])
    arr_up = _exchange_wait(up_sems, up_src, up_land, [dw_in_t], "exchange_up_wait")
    own_in, oth_in = _pair_reduce([dw_in_t.reshape(N_DEV, ROWS_IN, D)], "pair_reduce_in")
    in_sems, in_src, in_land, started = _exchange_start(oth_in, "exchange_in_start", CHIP_BARRIER)
    (small_piece,) = _sum_parts([own_small], [arr_small], "sum_small_grads")
    (gs_small,) = _all_gather([small_piece], "gather_small_grads")

    names = ["w_in", "w_pool", "pool_scale", "w_out", "ln1_g", "ln1_b", "w_up", "conv_w", "conv_b", "w_down",
             "ln2_g", "ln2_b"]
    w_d = dict(w_in=w_in, w_pool=w_pool, pool_scale=pool_scale, w_out=w_out, ln1_g=ln1_g, ln1_b=ln1_b, w_up=w_up,
               conv_w=conv_w, conv_b=conv_b, w_down=w_down, ln2_g=ln2_g, ln2_b=ln2_b)
    m_d = dict(w_in=m_w_in, w_pool=m_w_pool, pool_scale=m_pool_scale, w_out=m_w_out, ln1_g=m_ln1_g, ln1_b=m_ln1_b,
               w_up=m_w_up, conv_w=m_conv_w, conv_b=m_conv_b, w_down=m_w_down, ln2_g=m_ln2_g, ln2_b=m_ln2_b)
    v_d = dict(w_in=v_w_in, w_pool=v_w_pool, pool_scale=v_pool_scale, w_out=v_w_out, ln1_g=v_ln1_g, ln1_b=v_ln1_b,
               w_up=v_w_up, conv_w=v_conv_w, conv_b=v_conv_b, w_down=v_w_down, ln2_g=v_ln2_g, ln2_b=v_ln2_b)
    g_d, delta, new_m, new_v = {}, {}, {}, {}

    def big_adamw(k, own, arr, transposed, steps, after=()):
        lay = (lambda a: a[0].T) if transposed else (lambda a: a[0])
        back = (lambda a: a.T[None]) if transposed else (lambda a: a[None])
        res = _sum_adamw(own, arr, lay(w_d[k]), lay(m_d[k]), lay(v_d[k]), "adamw_" + k, steps, after)
        g_d[k], delta[k], new_m[k], new_v[k] = (back(r) for r in res)
        return res[3]

    done = [big_adamw("w_up", own_up, arr_up, True, 4, after=(started,)),
            big_adamw("w_down", own_down, arr_down, False, 2, after=(started,)),
            big_adamw("w_out", own_out, arr_out, False, 2, after=(started,))]

    gsm = _unpack(gs_small)
    gsm["conv_w"] = lax.dynamic_slice(gsm["conv_w"].reshape(3, D_FF), (0, me * (D_FF // N_DEV)), (3, D_FF // N_DEV))
    lay = lambda k, a: jnp.transpose(a, (1, 0, 2)) if k == "conv_w" else a.reshape(-1, a.shape[-1])
    back = lambda k, a: jnp.transpose(a, (1, 0, 2)) if k == "conv_w" else a.reshape(w_d[k].shape)
    group = [k for k in names if k not in ("w_in", "w_out", "w_up", "w_down")]
    for k in group:
        g_d[k] = gsm[k].reshape(w_d[k].shape)
    res = _adamw([lay(k, w_d[k]) for k in group], [lay(k, g_d[k]) for k in group], [lay(k, m_d[k]) for k in group],
                 [lay(k, v_d[k]) for k in group], "adamw_small")
    for j, k in enumerate(group):
        delta[k], new_m[k], new_v[k] = (back(k, res[part * len(group) + j]) for part in range(3))

    arr_in = _exchange_wait(in_sems, in_src, in_land, done + [res[0]], "exchange_in_wait")
    big_adamw("w_in", own_in, arr_in, True, 4)

    loss = gsm["loss"].reshape(())
    return (loss, grad_x[None], *[g_d[k] for k in names], *[delta[k] for k in names], *[new_m[k] for k in names],
            *[new_v[k] for k in names])
```

```python
import math

import numpy as np
import jax
import jax.numpy as jnp
from jax import lax
from jax.experimental import pallas as pl
from jax.experimental.pallas import tpu as pltpu

f32 = jnp.float32
bf16 = jnp.bfloat16

N_DEV = 8
T = 4096
D = 1024
CHUNK = 64
MIX_TILE = 512
RET_TILE = 256
HEADS = 4
DH = 128
RW = HEADS * DH
PW = 512
GROUPS = 4
WINDOWS = (2, 4, 8, 16)
IN_W = 4 * RW + PW
D_FF = 2816
LN_EPS = 1e-5
RMS_EPS = 1e-6
ALPHA = 2.0 ** 0.25
K_SCALE = DH ** -0.5

ADAM_LR = 0.001
ADAM_B1 = 0.9
ADAM_B2 = 0.999
ADAM_EPS = 1e-08
ADAM_WD = 0.01
ADAM_STEP = 10

ROWS_IN, ROWS_OUT, ROWS_UP, ROWS_DOWN = IN_W // N_DEV, D // N_DEV, 2 * D_FF // N_DEV, D_FF // N_DEV

V7X_VMEM_LIMIT = 56 * 2 ** 20
HALO = 32

NT = (((1,), (1,)), ((), ()))
TN = (((0,), (0,)), ((), ()))
NN = (((1,), (0,)), ((), ()))


def _dot(a, b, dims=NN):
    return lax.dot_general(a, b, dims, preferred_element_type=f32)


def _const_spec(shape):
    zeros = (0,) * len(shape)
    return pl.BlockSpec(shape, lambda i: zeros, pipeline_mode=pl.Buffered(1))


def _sigmoid(x):
    return 0.5 * jnp.tanh(0.5 * x) + 0.5


def _decay_tables(tt):
    h = np.arange(HEADS, dtype=np.float64)
    log_gamma = np.log(1.0 - 2.0 ** (-5.0 - h)).astype(np.float32).astype(np.float64)[:, None, None]
    idx = np.arange(tt, dtype=np.float64)
    visible = (idx[None, :] // CHUNK) <= (idx[:, None] // CHUNK)
    mask = np.where(visible[None], np.exp(log_gamma * np.abs(idx[:, None] - idx[None, :])[None]), 0.0)
    qd = np.broadcast_to(np.exp(log_gamma * (idx[None, :, None] + 1.0)), (HEADS, tt, DH))
    kd = np.broadcast_to(np.exp(log_gamma * (tt - 1.0 - idx[None, :, None])), (HEADS, tt, DH))
    cd = np.exp(log_gamma[:, 0, 0] * tt)
    return (jnp.asarray(mask, f32), jnp.asarray(qd, f32), jnp.asarray(kd, f32), [float(c) for c in cd])


def _rope_tables():
    inv_freq = (10000.0 ** (-np.arange(0, DH, 2, dtype=np.float64) / DH)).astype(np.float32)
    ang = (np.arange(T, dtype=np.float32)[:, None] * inv_freq[None, :]).astype(np.float64)
    cos, sin = np.cos(ang), np.sin(ang)
    return (jnp.asarray(np.concatenate([cos, cos], axis=1), f32), jnp.asarray(np.concatenate([-sin, sin], axis=1), f32))


def _swap_halves(t):
    return pltpu.roll(t, DH // 2, axis=1)


def _mix_forward(x, w_in_t, cos, sin, dmat, qd, kd, cdec, w_pool, pool_scale, w_out, ln1_g, ln1_b, gather,
                 tt=MIX_TILE):
    n_tiles = T // tt
    n_g = len(gather)

    def body(x_ref, wint_ref, cos_ref, sin_ref, dmat_ref, qd_ref, kd_ref, wpool_ref, pscale_ref, wout_ref,
             g1_ref, b1_ref, *rest):
        gin, rest = rest[:n_g], rest[n_g:]
        qkv_ref, g_ref, oret_ref, states_ref, cat_ref, pooled_ref, xhat_ref, rstd_ref, x1b_ref = rest[:9]
        gout, (state_s, pext_s, tmp_s, *sems) = rest[9:9 + n_g], rest[9 + n_g:]
        i = pl.program_id(0)

        @pl.when(i == 0)
        def _():
            state_s[...] = jnp.zeros_like(state_s)
            pext_s[:, pl.ds(0, HALO), :] = jnp.zeros((GROUPS, HALO, DH), f32)
            _gather_start(gin, gout, *sems)

        @pl.when(i == n_tiles - 2)
        def _():
            _gather_forward(gin, gout, *sems)

        xb = x_ref[...].astype(bf16)
        cos_t, sin_t = cos_ref[...], sin_ref[...]
        for part in range(2):
            pr = _dot(xb, wint_ref[pl.ds(part * RW, RW), :], NT)
            for h in range(HEADS):
                t = pr[:, h * DH:(h + 1) * DH]
                r = t * cos_t + _swap_halves(t) * sin_t
                if part == 1:
                    r = r * K_SCALE
                qkv_ref[:, part * RW + h * DH: part * RW + (h + 1) * DH] = r.astype(bf16)
        qkv_ref[:, 2 * RW:3 * RW] = _dot(xb, wint_ref[pl.ds(2 * RW, RW), :], NT).astype(bf16)
        g_ref[...] = _dot(xb, wint_ref[pl.ds(3 * RW, RW), :], NT)
        p = _dot(xb, wint_ref[pl.ds(4 * RW, PW), :], NT)
        for gi in range(GROUPS):
            pext_s[gi, pl.ds(HALO, tt), :] = p[:, gi * DH:(gi + 1) * DH]

        for sub in range(tt // RET_TILE):
            rows = pl.ds(sub * RET_TILE, RET_TILE)
            for h in range(HEADS):
                q = qkv_ref[rows, h * DH:(h + 1) * DH]
                k = qkv_ref[rows, RW + h * DH: RW + (h + 1) * DH]
                v = qkv_ref[rows, 2 * RW + h * DH: 2 * RW + (h + 1) * DH]
                s = _dot(q, k, NT) * dmat_ref[h]
                st = state_s[h]
                stb = st.astype(bf16)
                states_ref[sub, h] = stb
                oret_ref[rows, h * DH:(h + 1) * DH] = (_dot(s.astype(bf16), v)
                                                      + _dot((q.astype(f32) * qd_ref[h]).astype(bf16), stb))
                state_s[h] = st * cdec[h] + _dot((k.astype(f32) * kd_ref[h]).astype(bf16), v, TN)

        for h in range(HEADS):
            sl = slice(h * DH, (h + 1) * DH)
            o = oret_ref[:, sl]
            r = lax.rsqrt(jnp.mean(o * o, axis=-1, keepdims=True) + RMS_EPS)
            gg = g_ref[:, sl]
            cat_ref[:, sl] = (o * r * (gg * _sigmoid(gg))).astype(bf16)

        pos1 = (i * tt + lax.broadcasted_iota(jnp.int32, (tt, 1), 0) + 1).astype(f32)
        for gi, w in enumerate(WINDOWS):
            sl = slice(gi * DH, (gi + 1) * DH)
            stages = int(math.log2(w))
            src = pext_s
            for s in range(stages):
                lo = HALO - 8 * (stages - 1 - s)
                n = tt + HALO - lo
                shift = 2 ** s
                val = src[gi, pl.ds(lo, n), :] + src[gi, pl.ds(lo - shift, n), :]
                if s == stages - 1:
                    wsum = val
                else:
                    tmp_s[gi, pl.ds(lo, n), :] = val
                    src = tmp_s
            p_g = pext_s[gi, pl.ds(HALO, tt), :]
            pooled = (wsum / jnp.minimum(pos1, float(w)) - p_g).astype(bf16)
            pooled_ref[:, sl] = pooled
            y = _dot(pooled, wpool_ref[gi]) * pscale_ref[:, sl]
            cat_ref[:, RW + gi * DH: RW + (gi + 1) * DH] = y.astype(bf16)
        pext_s[:, pl.ds(0, HALO), :] = pext_s[:, pl.ds(tt, HALO), :]

        z = ALPHA * x_ref[...] + _dot(cat_ref[...], wout_ref[...])
        mu = jnp.mean(z, axis=-1, keepdims=True)
        zc = z - mu
        rstd = lax.rsqrt(jnp.mean(zc * zc, axis=-1, keepdims=True) + LN_EPS)
        xhat = zc * rstd
        xhat_ref[...] = xhat
        rstd_ref[...] = rstd
        x1b_ref[...] = (xhat * g1_ref[...] + b1_ref[...]).astype(bf16)

        @pl.when(i == n_tiles - 1)
        def _():
            _gather_finish(gin, gout, *sems)

    tile = lambda w: pl.BlockSpec((tt, w), lambda i: (i, 0))
    hbm = pl.BlockSpec(memory_space=pltpu.HBM)
    out_shape = (
        jax.ShapeDtypeStruct((T, 3 * RW), bf16),
        jax.ShapeDtypeStruct((T, RW), f32),
        jax.ShapeDtypeStruct((T, RW), f32),
        jax.ShapeDtypeStruct((T // RET_TILE, HEADS, DH, DH), bf16),
        jax.ShapeDtypeStruct((T, D), bf16),
        jax.ShapeDtypeStruct((T, PW), bf16),
        jax.ShapeDtypeStruct((T, D), f32),
        jax.ShapeDtypeStruct((T, 1), f32),
        jax.ShapeDtypeStruct((T, D), bf16),
    ) + tuple(jax.ShapeDtypeStruct((N_DEV,) + b.shape, b.dtype) for b in gather)
    return pl.pallas_call(
        body, name="mix_forward", grid=(n_tiles,), out_shape=out_shape,
        in_specs=[tile(D), _const_spec((IN_W, D)), tile(DH), tile(DH),
                  _const_spec((HEADS, RET_TILE, RET_TILE)), _const_spec((HEADS, RET_TILE, DH)),
                  _const_spec((HEADS, RET_TILE, DH)),
                  _const_spec((GROUPS, DH, DH)), _const_spec((1, PW)), _const_spec((D, D)),
                  _const_spec((1, D)), _const_spec((1, D))] + [hbm] * n_g,
        out_specs=(tile(3 * RW), tile(RW), tile(RW),
                   pl.BlockSpec((tt // RET_TILE, HEADS, DH, DH), lambda i: (i, 0, 0, 0)),
                   tile(D), tile(PW), tile(D), tile(1), tile(D)) + (hbm,) * n_g,
        scratch_shapes=[pltpu.VMEM((HEADS, DH, DH), f32), pltpu.VMEM((GROUPS, tt + HALO, DH), f32),
                        pltpu.VMEM((GROUPS, tt + HALO, DH), f32)] + _gather_sems(n_g),
        compiler_params=pltpu.CompilerParams(dimension_semantics=("arbitrary",), vmem_limit_bytes=V7X_VMEM_LIMIT,
                                             collective_id=GATHER_BARRIER),
    )(x, w_in_t, cos, sin, dmat, qd, kd, w_pool, pool_scale, w_out, ln1_g, ln1_b, *gather)


def _ffn_forward_backward(xhat1, rstd1, ln1_g, ln1_b, w_up_t, conv_w, conv_b, w_down, ln2_g, ln2_b, target,
                          tt=256):
    n_tiles = T // tt
    FH = 16
    hb = tt // FH

    def body(xhat_ref, halo_ref, rstd_ref, g1_ref, b1_ref, wupt_ref, cw_ref, cb_ref, wdown_ref, g2_ref, b2_ref, tgt_ref,
             dz1_ref, dz2b_ref, du_ref, f_ref, loss_ref, dg2_ref, db2_ref, dg1_ref, db1_ref, dcb_ref, dcw_ref,
             gext_s, val_s, dhext_s):
        i = pl.program_id(0)
        tile_idx = n_tiles - 1 - i

        def rd(ref, off):
            return jnp.concatenate([ref[k, pl.ds(off, tt), :] for k in range(D_FF // 128)], axis=1)

        def wr(ref, val):
            for k in range(D_FF // 128):
                ref[k, pl.ds(0, val.shape[0]), :] = val[:, k * 128:(k + 1) * 128]

        @pl.when(i == 0)
        def _():
            for r in (loss_ref, dg2_ref, db2_ref, dg1_ref, db1_ref, dcb_ref, dcw_ref):
                r[...] = jnp.zeros_like(r)
            dhext_s[:, pl.ds(tt, 8), :] = jnp.zeros((D_FF // 128, 8, 128), f32)

        g1, b1 = g1_ref[...], b1_ref[...]
        xhat = xhat_ref[...]
        x1 = xhat * g1 + b1
        x1b = x1.astype(bf16)
        x1h = ((halo_ref[...] * g1 + b1) * jnp.where(tile_idx == 0, 0.0, 1.0)).astype(bf16)
        x1ext = jnp.concatenate([x1h, x1b], axis=0)

        val = _dot(x1b, wupt_ref[pl.ds(0, D_FF), :], NT)
        gate_ext = _dot(x1ext, wupt_ref[pl.ds(D_FF, D_FF), :], NT)
        wr(gext_s, gate_ext)
        hh = (cb_ref[...] + cw_ref[0:1, :] * rd(gext_s, FH - 2) + cw_ref[1:2, :] * rd(gext_s, FH - 1)
              + cw_ref[2:3, :] * gate_ext[FH:])
        sg = _sigmoid(hh)
        act = hh * sg
        wr(dhext_s, act)
        val_s[...] = val * (sg + act * (1.0 - sg))
        fb = (act * val).astype(bf16)
        f_ref[...] = fb

        z = ALPHA * x1 + _dot(fb, wdown_ref[...])
        mu = jnp.mean(z, axis=-1, keepdims=True)
        zc = z - mu
        rstd2 = lax.rsqrt(jnp.mean(zc * zc, axis=-1, keepdims=True) + LN_EPS)
        xh2 = zc * rstd2
        diff = xh2 * g2_ref[...] + b2_ref[...] - tgt_ref[...]
        loss_ref[...] += 0.5 * jnp.sum(diff * diff) / D
        dy = diff * (1.0 / D)
        dg2_ref[...] += jnp.sum(dy * xh2, axis=0, keepdims=True)
        db2_ref[...] += jnp.sum(dy, axis=0, keepdims=True)
        dyg = dy * g2_ref[...]
        dz2 = rstd2 * (dyg - jnp.mean(dyg, axis=-1, keepdims=True) - xh2 * jnp.mean(dyg * xh2, axis=-1, keepdims=True))
        dz2b = dz2.astype(bf16)
        dz2b_ref[...] = dz2b

        df = _dot(dz2b, wdown_ref[...], NT)
        dval = df * rd(dhext_s, 0)
        dh = df * val_s[...]
        wr(dhext_s, dh)
        dh1, dh2, g0 = rd(dhext_s, 1), rd(dhext_s, 2), rd(gext_s, FH)
        dcb_ref[...] += jnp.sum(dh, axis=0, keepdims=True)
        dcw_ref[0:1, :] += jnp.sum(dh2 * g0, axis=0, keepdims=True)
        dcw_ref[1:2, :] += jnp.sum(dh1 * g0, axis=0, keepdims=True)
        dcw_ref[2:3, :] += jnp.sum(dh * g0, axis=0, keepdims=True)
        dgate = cw_ref[2:3, :] * dh + cw_ref[1:2, :] * dh1 + cw_ref[0:1, :] * dh2
        dvalb, dgateb = dval.astype(bf16), dgate.astype(bf16)
        du_ref[:, :D_FF] = dvalb
        du_ref[:, D_FF:] = dgateb
        dx1 = ALPHA * dz2 + _dot(dvalb, wupt_ref[pl.ds(0, D_FF), :]) + _dot(dgateb, wupt_ref[pl.ds(D_FF, D_FF), :])
        dhext_s[:, pl.ds(tt, 8), :] = dhext_s[:, pl.ds(0, 8), :]

        dg1_ref[...] += jnp.sum(dx1 * xhat, axis=0, keepdims=True)
        db1_ref[...] += jnp.sum(dx1, axis=0, keepdims=True)
        dxg = dx1 * g1
        dz1_ref[...] = rstd_ref[...] * (dxg - jnp.mean(dxg, axis=-1, keepdims=True)
                                        - xhat * jnp.mean(dxg * xhat, axis=-1, keepdims=True))

    rtile = lambda w: pl.BlockSpec((tt, w), lambda i: (n_tiles - 1 - i, 0))
    acc = lambda shape: pl.BlockSpec(shape, lambda i: (0, 0))
    out_shape = (
        jax.ShapeDtypeStruct((T, D), f32),
        jax.ShapeDtypeStruct((T, D), bf16),
        jax.ShapeDtypeStruct((T, 2 * D_FF), bf16),
        jax.ShapeDtypeStruct((T, D_FF), bf16),
        jax.ShapeDtypeStruct((8, 128), f32),
        jax.ShapeDtypeStruct((1, D), f32), jax.ShapeDtypeStruct((1, D), f32),
        jax.ShapeDtypeStruct((1, D), f32), jax.ShapeDtypeStruct((1, D), f32),
        jax.ShapeDtypeStruct((1, D_FF), f32), jax.ShapeDtypeStruct((3, D_FF), f32),
    )
    return pl.pallas_call(
        body, name="ffn_forward_backward", grid=(n_tiles,), out_shape=out_shape,
        in_specs=[rtile(D),
                  pl.BlockSpec((FH, D), lambda i: (jnp.maximum((n_tiles - 1 - i) * hb - 1, 0), 0)),
                  rtile(1), _const_spec((1, D)), _const_spec((1, D)), _const_spec((2 * D_FF, D)),
                  _const_spec((3, D_FF)), _const_spec((1, D_FF)), _const_spec((D_FF, D)),
                  _const_spec((1, D)), _const_spec((1, D)), rtile(D)],
        out_specs=(rtile(D), rtile(D), rtile(2 * D_FF), rtile(D_FF), acc((8, 128)),
                   acc((1, D)), acc((1, D)), acc((1, D)), acc((1, D)), acc((1, D_FF)), acc((3, D_FF))),
        scratch_shapes=[pltpu.VMEM((D_FF // 128, tt + FH, 128), f32), pltpu.VMEM((tt, D_FF), f32),
                        pltpu.VMEM((D_FF // 128, tt + 8, 128), f32)],
        compiler_params=pltpu.CompilerParams(dimension_semantics=("arbitrary",), vmem_limit_bytes=V7X_VMEM_LIMIT),
    )(xhat1, xhat1, rstd1, ln1_g, ln1_b, w_up_t, conv_w, conv_b, w_down, ln2_g, ln2_b, target)


def _mix_backward(dz1, w_out, qkv, g, oret, states, pooled, cat, cos, sin, dmat, qd, kd, cdec, w_pool, pool_scale, w_in_t,
                  after, tt=MIX_TILE):
    n_tiles = T // tt

    def body(dz1_ref, wout_ref, qkv_ref, g_ref, oret_ref, states_ref, pooled_ref, cat_ref, cos_ref, sin_ref, dmat_ref,
             qd_ref, kd_ref, wpool_ref, pscale_ref, wint_ref, after_ref,
             dproj_ref, gx_ref, dwpool_ref, dpscale_ref, dwout_ref, dstate_s, dout_s, eext_s, tmp_s, dwout_s):
        i = pl.program_id(0)
        tile_idx = n_tiles - 1 - i

        @pl.when(i == 0)
        def _():
            dstate_s[...] = jnp.zeros_like(dstate_s)
            dwpool_ref[...] = jnp.zeros_like(dwpool_ref)
            dpscale_ref[...] = jnp.zeros_like(dpscale_ref)
            dwout_s[...] = jnp.zeros_like(dwout_s)
            eext_s[:, pl.ds(tt, HALO), :] = jnp.zeros((GROUPS, HALO, DH), f32)

        dz1 = dz1_ref[...]
        dz1b = dz1.astype(bf16)
        dcat = _dot(dz1b, wout_ref[...], NT)
        dwout_s[...] += _dot(cat_ref[...], dz1b, TN)

        pos1 = (tile_idx * tt + lax.broadcasted_iota(jnp.int32, (tt, 1), 0) + 1).astype(f32)
        for gi, w in enumerate(WINDOWS):
            sl = slice(gi * DH, (gi + 1) * DH)
            dpo = dcat[:, RW + gi * DH: RW + (gi + 1) * DH]
            pooled_g = pooled_ref[:, sl]
            ylin = _dot(pooled_g, wpool_ref[gi])
            dpscale_ref[:, sl] += jnp.sum(dpo * ylin, axis=0, keepdims=True)
            dpw = (dpo * pscale_ref[:, sl]).astype(bf16)
            dwpool_ref[gi] += _dot(pooled_g, dpw, TN)
            dpooled = _dot(dpw, wpool_ref[gi], NT)
            eext_s[gi, pl.ds(0, tt), :] = dpooled / jnp.minimum(pos1, float(w))
            stages = int(math.log2(w))
            src = eext_s
            for s in range(stages):
                n = tt + 8 * (stages - 1 - s)
                shift = 2 ** s
                val = src[gi, pl.ds(0, n), :] + src[gi, pl.ds(shift, n), :]
                if s == stages - 1:
                    wsum = val
                else:
                    tmp_s[gi, pl.ds(0, n), :] = val
                    src = tmp_s
            dproj_ref[:, 4 * RW + gi * DH: 4 * RW + (gi + 1) * DH] = (wsum - dpooled).astype(bf16)
        eext_s[:, pl.ds(tt, HALO), :] = eext_s[:, pl.ds(0, HALO), :]

        for h in range(HEADS):
            sl = slice(h * DH, (h + 1) * DH)
            dr = dcat[:, sl]
            o = oret_ref[:, sl]
            r = lax.rsqrt(jnp.mean(o * o, axis=-1, keepdims=True) + RMS_EPS)
            rn = o * r
            gg = g_ref[:, sl]
            sg = _sigmoid(gg)
            dproj_ref[:, 3 * RW + h * DH: 3 * RW + (h + 1) * DH] = (dr * rn * (sg * (1.0 + gg * (1.0 - sg)))).astype(bf16)
            drn = dr * (gg * sg)
            dout_s[:, sl] = (r * (drn - rn * jnp.mean(drn * rn, axis=-1, keepdims=True))).astype(bf16)

        for sub in reversed(range(tt // RET_TILE)):
            rows = pl.ds(sub * RET_TILE, RET_TILE)
            cos_t, sin_t = cos_ref[rows, :], sin_ref[rows, :]
            for h in range(HEADS):
                q = qkv_ref[rows, h * DH:(h + 1) * DH]
                k = qkv_ref[rows, RW + h * DH: RW + (h + 1) * DH]
                v = qkv_ref[rows, 2 * RW + h * DH: 2 * RW + (h + 1) * DH]
                do = dout_s[rows, h * DH:(h + 1) * DH]
                stb = states_ref[sub, h]
                dst = dstate_s[h]
                dstb = dst.astype(bf16)
                sb = (_dot(q, k, NT) * dmat_ref[h]).astype(bf16)
                dsb = (_dot(do, v, NT) * dmat_ref[h]).astype(bf16)
                dq = _dot(dsb, k) + _dot(do, stb, NT) * qd_ref[h]
                dk = _dot(dsb, q, TN) + _dot(v, dstb, NT) * kd_ref[h]
                dv = _dot(sb, do, TN) + _dot((k.astype(f32) * kd_ref[h]).astype(bf16), dstb)
                dstate_s[h] = dst * cdec[h] + _dot((q.astype(f32) * qd_ref[h]).astype(bf16), do, TN)
                dproj_ref[rows, h * DH:(h + 1) * DH] = (dq * cos_t - _swap_halves(dq) * sin_t).astype(bf16)
                dproj_ref[rows, RW + h * DH: RW + (h + 1) * DH] = (
                    (dk * cos_t - _swap_halves(dk) * sin_t) * K_SCALE).astype(bf16)
                dproj_ref[rows, 2 * RW + h * DH: 2 * RW + (h + 1) * DH] = dv.astype(bf16)

        gx_ref[...] = ALPHA * dz1 + _dot(dproj_ref[...], wint_ref[...])

        @pl.when(i == n_tiles - 1)
        def _():
            dwout_ref[...] = dwout_s[...].astype(bf16)

    rtile = lambda w: pl.BlockSpec((tt, w), lambda i: (n_tiles - 1 - i, 0))
    out_shape = (
        jax.ShapeDtypeStruct((T, IN_W), bf16),
        jax.ShapeDtypeStruct((T, D), f32),
        jax.ShapeDtypeStruct((GROUPS, DH, DH), f32),
        jax.ShapeDtypeStruct((1, PW), f32),
        jax.ShapeDtypeStruct((D, D), bf16),
    )
    return pl.pallas_call(
        body, name="mix_backward", grid=(n_tiles,), out_shape=out_shape,
        in_specs=[rtile(D), _const_spec((D, D)), rtile(3 * RW), rtile(RW), rtile(RW),
                  pl.BlockSpec((tt // RET_TILE, HEADS, DH, DH), lambda i: (n_tiles - 1 - i, 0, 0, 0)),
                  rtile(PW), rtile(D), rtile(DH), rtile(DH),
                  _const_spec((HEADS, RET_TILE, RET_TILE)), _const_spec((HEADS, RET_TILE, DH)),
                  _const_spec((HEADS, RET_TILE, DH)),
                  _const_spec((GROUPS, DH, DH)), _const_spec((1, PW)), _const_spec((IN_W, D)),
                  pl.BlockSpec(memory_space=pl.ANY)],
        out_specs=(rtile(IN_W), rtile(D), pl.BlockSpec((GROUPS, DH, DH), lambda i: (0, 0, 0)),
                   pl.BlockSpec((1, PW), lambda i: (0, 0)),
                   pl.BlockSpec((D, D), lambda i: (0, 0), pipeline_mode=pl.Buffered(1))),
        scratch_shapes=[pltpu.VMEM((HEADS, DH, DH), f32), pltpu.VMEM((tt, RW), bf16),
                        pltpu.VMEM((GROUPS, tt + HALO, DH), f32), pltpu.VMEM((GROUPS, tt + HALO, DH), f32),
                        pltpu.VMEM((D, D), f32)],
        compiler_params=pltpu.CompilerParams(dimension_semantics=("arbitrary",), vmem_limit_bytes=V7X_VMEM_LIMIT),
    )(dz1, w_out, qkv, g, oret, states, pooled, cat, cos, sin, dmat, qd, kd, w_pool, pool_scale, w_in_t, after)


def _weight_grad(a, b, name, tm, exchange=()):
    m = a.shape[1]
    n_m, n_e = m // tm, len(exchange)

    def body(a_ref, b_ref, *rest):
        ein, o_ref, eout, sems = rest[:n_e], rest[n_e], rest[n_e + 1:2 * n_e + 1], rest[2 * n_e + 1:]
        i = pl.program_id(0)

        if n_e:
            @pl.when(i == 0)
            def _():
                _chip_exchange_start(ein, eout, *sems)

        o_ref[...] = _dot(a_ref[...], b_ref[...].astype(bf16), TN).astype(bf16)

        if n_e:
            @pl.when(i == n_m - 1)
            def _():
                _chip_exchange_finish(ein, eout, *sems)

    hbm = pl.BlockSpec(memory_space=pltpu.HBM)
    return pl.pallas_call(
        body, name=name, grid=(n_m,),
        out_shape=(jax.ShapeDtypeStruct((m, D), bf16),) + tuple(jax.ShapeDtypeStruct(e.shape, e.dtype) for e in exchange),
        in_specs=[pl.BlockSpec((T, tm), lambda i: (0, i)),
                  pl.BlockSpec((T, D), lambda i: (0, 0), pipeline_mode=pl.Buffered(1))] + [hbm] * n_e,
        out_specs=(pl.BlockSpec((tm, D), lambda i: (i, 0)),) + (hbm,) * n_e,
        scratch_shapes=_chip_exchange_sems(n_e),
        compiler_params=pltpu.CompilerParams(dimension_semantics=("arbitrary",), vmem_limit_bytes=V7X_VMEM_LIMIT,
                                             collective_id=CHIP_BARRIER if n_e else None),
    )(a, b, *exchange)


CHIP_FLIPS = ((1, 0), (0, 1), (1, 1))
PAIR_BARRIER, CHIP_BARRIER, GATHER_BARRIER, CHIP_BARRIER_SPLIT = 0, 1, 2, 3


def _barrier(peers):
    sem = pltpu.get_barrier_semaphore()
    for peer in peers:
        pl.semaphore_signal(sem, inc=1, device_id=peer, device_id_type=pl.DeviceIdType.MESH)
    pl.semaphore_wait(sem, len(peers))


def _me():
    return lax.axis_index("x"), lax.axis_index("y"), lax.axis_index("c")


def _chip(me, k):
    x, y, _ = me
    if k == 0:
        return x, y
    fx, fy = CHIP_FLIPS[k - 1]
    return (1 - x if fx else x), (1 - y if fy else y)


def _slot(x, y, c):
    return 4 * x + 2 * y + c


def _remote(src, dst, send_sem, recv_sem, to):
    return pltpu.make_async_remote_copy(src_ref=src, dst_ref=dst, send_sem=send_sem, recv_sem=recv_sem,
                                        device_id=to, device_id_type=pl.DeviceIdType.MESH)


def _gather_sems(n):
    return [pltpu.SemaphoreType.DMA((7, n)), pltpu.SemaphoreType.DMA((7, n)), pltpu.SemaphoreType.DMA((n,))] if n else []


def _gather_copy(k, j, gin, gout, send_sems, recv_sems, sending):
    x, y, c = _me()
    sibling, x_chip, y_chip, d_chip = (x, y, 1 - c), (1 - x, y), (x, 1 - y), (1 - x, 1 - y)
    south = c == 0
    passed_on = (jnp.where(south, 1 - x, x), jnp.where(south, y, 1 - y), c)
    src, to = gin[j], sibling
    if sending:
        block = {0: (x, y, c), 1: (x, y, c), 2: (x, y, c), 3: passed_on, 4: (*x_chip, c), 5: (*y_chip, c), 6: (*d_chip, c)}[k]
        to = {1: (*x_chip, c), 2: (*y_chip, c), 3: (jnp.where(south, x, 1 - x), jnp.where(south, 1 - y, y), c)}.get(k, sibling)
        if k >= 3:
            src = gout[j].at[_slot(*block)]
    else:
        block = {0: sibling, 1: (*x_chip, c), 2: (*y_chip, c), 3: (*d_chip, c), 4: (*x_chip, 1 - c), 5: (*y_chip, 1 - c),
                 6: (*d_chip, 1 - c)}[k]
    return _remote(src, gout[j].at[_slot(*block)], send_sems.at[k, j], recv_sems.at[k, j], to)


def _gather_do(ks, action, gin, gout, send_sems, recv_sems):
    for k in ks:
        for j in range(len(gin)):
            cp = _gather_copy(k, j, gin, gout, send_sems, recv_sems, action != "wait_recv")
            getattr(cp, action)()


def _gather_peers():
    x, y, c = _me()
    return [(x, y, 1 - c), (1 - x, y, c), (x, 1 - y, c)]


def _gather_start(gin, gout, send_sems, recv_sems, local_sems, barrier=True):
    if barrier:
        _barrier(_gather_peers())
    for j in range(len(gin)):
        pltpu.make_async_copy(gin[j], gout[j].at[_slot(*_me())], local_sems.at[j]).start()
    _gather_do((0, 1, 2), "start", gin, gout, send_sems, recv_sems)


def _gather_forward(gin, gout, send_sems, recv_sems, local_sems):
    _gather_do((1, 2), "wait_recv", gin, gout, send_sems, recv_sems)
    _gather_do((3, 4, 5), "start", gin, gout, send_sems, recv_sems)


def _gather_finish(gin, gout, send_sems, recv_sems, local_sems):
    _gather_do((3,), "wait_recv", gin, gout, send_sems, recv_sems)
    _gather_do((6,), "start", gin, gout, send_sems, recv_sems)
    _gather_do((0, 4, 5, 6), "wait_recv", gin, gout, send_sems, recv_sems)
    _gather_do(range(7), "wait_send", gin, gout, send_sems, recv_sems)
    for j in range(len(gin)):
        pltpu.make_async_copy(gin[j], gout[j].at[_slot(*_me())], local_sems.at[j]).wait()


def _all_gather(blocks, name):
    n = len(blocks)

    def body(*refs):
        gin, gout, sems = refs[:n], refs[n:2 * n], refs[2 * n:]
        _gather_start(gin, gout, *sems)
        _gather_forward(gin, gout, *sems)
        _gather_finish(gin, gout, *sems)

    hbm = pl.BlockSpec(memory_space=pltpu.HBM)
    return pl.pallas_call(
        body, name=name,
        out_shape=tuple(jax.ShapeDtypeStruct((N_DEV,) + b.shape, b.dtype) for b in blocks),
        in_specs=[hbm] * n, out_specs=(hbm,) * n, scratch_shapes=_gather_sems(n),
        compiler_params=pltpu.CompilerParams(collective_id=GATHER_BARRIER),
    )(*blocks)


def _pair_reduce(parts, name):
    n = len(parts)

    def body(*refs):
        ins, own, others, landing, mine = (refs[k * n:(k + 1) * n] for k in range(5))
        send_sems, recv_sems, local_sems = refs[5 * n:]
        me = _me()
        x, y, c = me
        sibling = (x, y, 1 - c)
        _barrier([sibling])
        sends, loads = [], []
        for k in range(4):
            for j in range(n):
                cp = _remote(ins[j].at[_slot(*_chip(me, k), 1 - c)], landing[j].at[k], send_sems.at[k, j],
                             recv_sems.at[k, j], sibling)
                cp.start()
                sends.append(cp)
                ld = pltpu.make_async_copy(ins[j].at[_slot(*_chip(me, k), c)], mine[j].at[k], local_sems.at[k, j])
                ld.start()
                loads.append(ld)
        for k in range(4):
            for j in range(n):
                loads[k * n + j].wait()
                _remote(ins[j].at[0], landing[j].at[k], send_sems.at[k, j], recv_sems.at[k, j], sibling).wait_recv()
                total = mine[j][k].astype(f32) + landing[j][k].astype(f32)
                if k == 0:
                    own[j][...] = total.astype(own[j].dtype)
                else:
                    others[j][k - 1] = total.astype(others[j].dtype)
        for cp in sends:
            cp.wait_send()

    vm = pl.BlockSpec(memory_space=pltpu.VMEM)
    return pl.pallas_call(
        body, name=name,
        out_shape=tuple(jax.ShapeDtypeStruct(p.shape[1:], p.dtype) for p in parts)
        + tuple(jax.ShapeDtypeStruct((3,) + p.shape[1:], p.dtype) for p in parts),
        in_specs=[pl.BlockSpec(memory_space=pltpu.HBM)] * n, out_specs=(vm,) * (2 * n),
        scratch_shapes=[pltpu.VMEM((4,) + p.shape[1:], p.dtype) for p in parts] * 2
        + [pltpu.SemaphoreType.DMA((4, n)), pltpu.SemaphoreType.DMA((4, n)), pltpu.SemaphoreType.DMA((4, n))],
        compiler_params=pltpu.CompilerParams(vmem_limit_bytes=V7X_VMEM_LIMIT, collective_id=PAIR_BARRIER),
    )(*parts)


def _chip_exchange_sems(n):
    return [pltpu.SemaphoreType.DMA((3, n)), pltpu.SemaphoreType.DMA((3, n))] if n else []


def _chip_exchange_copy(k, j, ein, eout, send_sems, recv_sems):
    me = _me()
    return _remote(ein[j].at[k - 1], eout[j].at[k - 1], send_sems.at[k - 1, j], recv_sems.at[k - 1, j],
                   (*_chip(me, k), me[2]))


def _chip_peers():
    me = _me()
    return [(*_chip(me, k), me[2]) for k in range(1, 4)]


def _chip_exchange_start(ein, eout, send_sems, recv_sems, barrier=True):
    if barrier:
        _barrier(_chip_peers())
    for k in range(1, 4):
        for j in range(len(ein)):
            _chip_exchange_copy(k, j, ein, eout, send_sems, recv_sems).start()


def _chip_exchange_finish(ein, eout, send_sems, recv_sems):
    for k in range(1, 4):
        for j in range(len(ein)):
            _chip_exchange_copy(k, j, ein, eout, send_sems, recv_sems).wait_recv()
    for k in range(1, 4):
        for j in range(len(ein)):
            _chip_exchange_copy(k, j, ein, eout, send_sems, recv_sems).wait_send()


def _split_copies(src_ref, dst_ref, sems):
    me = _me()
    return [_remote(src_ref.at[k - 1], dst_ref.at[k - 1], sems[k - 1], sems[2 + k], (*_chip(me, k), me[2]))
            for k in range(1, 4)]


def _exchange_start(others, name, barrier_id):
    def body(src_ref, land_ref, *rest):
        sems, token_ref = rest[:6], rest[8]
        _barrier(_chip_peers())
        for copy in _split_copies(src_ref, land_ref, sems):
            copy.start()
        token_ref[...] = jnp.zeros_like(token_ref)

    hbm, sem = pl.BlockSpec(memory_space=pltpu.HBM), pl.BlockSpec(memory_space=pltpu.SEMAPHORE)
    thru = pltpu.HBM(others.shape, others.dtype)
    res = pl.pallas_call(
        body, name=name,
        out_shape=(pltpu.SemaphoreType.DMA(()),) * 6 + (thru, thru, jax.ShapeDtypeStruct((8, 128), f32)),
        in_specs=(hbm, hbm), out_specs=(sem,) * 6 + (hbm, hbm, pl.BlockSpec(memory_space=pltpu.VMEM)),
        input_output_aliases={0: 6, 1: 7},
        compiler_params=pltpu.CompilerParams(has_side_effects=pltpu.SideEffectType.DATAFLOW_SIDE_EFFECTING,
                                             collective_id=barrier_id),
    )(pltpu.with_memory_space_constraint(others, pltpu.HBM),
      pltpu.with_memory_space_constraint(lax.empty(others.shape, others.dtype), pltpu.HBM))
    return res[:6], res[6], res[7], res[8]


def _exchange_wait(sems, src_thru, land_thru, after, name):
    n_after = len(after)

    def body(src_ref, land_ref, *rest):
        for copy in _split_copies(src_ref, land_ref, rest[:6]):
            copy.wait_send()
            copy.wait_recv()

    hbm, sem = pl.BlockSpec(memory_space=pltpu.HBM), pl.BlockSpec(memory_space=pltpu.SEMAPHORE)
    thru = pltpu.HBM(src_thru.shape, src_thru.dtype)
    return pl.pallas_call(
        body, name=name, out_shape=(thru, thru),
        in_specs=(hbm, hbm) + (sem,) * 6 + (pl.BlockSpec(memory_space=pl.ANY),) * n_after, out_specs=(hbm, hbm),
        input_output_aliases={0: 0, 1: 1},
        compiler_params=pltpu.CompilerParams(has_side_effects=pltpu.SideEffectType.DATAFLOW_SIDE_EFFECTING),
    )(src_thru, land_thru, *sems, *after)[1]


def _sum_parts(owns, arrived, name):
    n = len(owns)

    def body(*refs):
        for own, arr, out in zip(refs[:n], refs[n:2 * n], refs[2 * n:]):
            acc = own[...].astype(f32)
            for k in range(3):
                acc = acc + arr[k].astype(f32)
            out[...] = acc

    vm = pl.BlockSpec(memory_space=pltpu.VMEM)
    return pl.pallas_call(
        body, name=name, out_shape=tuple(jax.ShapeDtypeStruct(o.shape, f32) for o in owns),
        in_specs=[vm] * (2 * n), out_specs=(vm,) * n,
        compiler_params=pltpu.CompilerParams(vmem_limit_bytes=V7X_VMEM_LIMIT),
    )(*owns, *arrived)


def _adam_update(w, g, m, v):
    m = ADAM_B1 * m + (1.0 - ADAM_B1) * g
    v = ADAM_B2 * v + (1.0 - ADAM_B2) * (g * g)
    m_hat = m / (1.0 - ADAM_B1 ** ADAM_STEP)
    v_hat = v / (1.0 - ADAM_B2 ** ADAM_STEP)
    return -ADAM_LR * (m_hat / (jnp.sqrt(v_hat) + ADAM_EPS) + ADAM_WD * w), m, v


def _sum_adamw(own, arrived, w, m, v, name, steps, after=()):
    rows = own.shape[0]
    br = rows // steps

    def body(own_ref, arr_ref, w_ref, m_ref, v_ref, *rest):
        g_out, d_out, m_out, v_out = rest[len(after):]
        g = own_ref[...].astype(f32)
        for k in range(3):
            g = g + arr_ref[k].astype(f32)
        g_out[...] = g
        d_out[...], m_out[...], v_out[...] = _adam_update(w_ref[...], g, m_ref[...], v_ref[...])

    blk = pl.BlockSpec((br, D), lambda i: (i, 0))
    return pl.pallas_call(
        body, name=name, grid=(steps,), out_shape=(jax.ShapeDtypeStruct((rows, D), f32),) * 4,
        in_specs=[blk, pl.BlockSpec((3, br, D), lambda i: (0, i, 0)), blk, blk, blk]
        + [pl.BlockSpec(memory_space=pl.ANY)] * len(after), out_specs=(blk,) * 4,
        compiler_params=pltpu.CompilerParams(dimension_semantics=("parallel",), vmem_limit_bytes=V7X_VMEM_LIMIT),
    )(own, arrived, w, m, v, *after)


def _adamw(ws, gs, ms, vs, name):
    n = len(ws)

    def body(*refs):
        w_r, g_r, m_r, v_r = (refs[k * n:(k + 1) * n] for k in range(4))
        d_o, m_o, v_o = (refs[(4 + k) * n:(5 + k) * n] for k in range(3))
        for j in range(n):
            d_o[j][...], m_o[j][...], v_o[j][...] = _adam_update(w_r[j][...], g_r[j][...], m_r[j][...], v_r[j][...])

    vm = pl.BlockSpec(memory_space=pltpu.VMEM)
    shapes = tuple(jax.ShapeDtypeStruct(w.shape, f32) for w in ws)
    return pl.pallas_call(
        body, name=name, out_shape=shapes * 3, in_specs=[vm] * (4 * n), out_specs=tuple([vm] * (3 * n)),
        compiler_params=pltpu.CompilerParams(vmem_limit_bytes=V7X_VMEM_LIMIT),
    )(*ws, *gs, *ms, *vs)


SMALL = (("w_pool", GROUPS * DH * DH), ("pool_scale", PW), ("ln1_g", D), ("ln1_b", D), ("conv_b", D_FF),
         ("ln2_g", D), ("ln2_b", D), ("conv_w", 3 * D_FF), ("loss", 1))
SMALL_ROWS = 640


def _pack(named):
    flat = jnp.concatenate([named[k].reshape(-1) for k, _ in SMALL])
    return jnp.pad(flat, (0, SMALL_ROWS * 128 - flat.shape[0])).reshape(SMALL_ROWS, 128)


def _unpack(packed):
    flat, out, at = packed.reshape(-1), {}, 0
    for k, size in SMALL:
        out[k] = flat[at:at + size]
        at += size
    return out


def kernel(x, w_in, w_pool, pool_scale, w_out, ln1_g, ln1_b, w_up, conv_w, conv_b, w_down, ln2_g, ln2_b, loss_target, m_w_in, m_w_pool, m_pool_scale, m_w_out, m_ln1_g, m_ln1_b, m_w_up, m_conv_w, m_conv_b, m_w_down, m_ln2_g, m_ln2_b, v_w_in, v_w_pool, v_pool_scale, v_w_out, v_ln1_g, v_ln1_b, v_w_up, v_conv_w, v_conv_b, v_w_down, v_ln2_g, v_ln2_b):
    me = 4 * lax.axis_index("x") + 2 * lax.axis_index("y") + lax.axis_index("c")
    x2, tgt = x[0], loss_target[0]

    g_in, g_out, g_cw = _all_gather([w_in[0].T.astype(bf16), w_out[0].astype(bf16), jnp.transpose(conv_w, (1, 0, 2))],
                                    "gather_weights")
    w_in_t = g_in.reshape(IN_W, D)
    w_out_f = g_out.reshape(D, D)
    conv_w_f = jnp.transpose(g_cw[:, :, 0, :], (1, 0, 2)).reshape(3, D_FF)
    w_pool_b = w_pool[0].astype(bf16)

    cos, sin = _rope_tables()
    dmat, qd, kd, cdec = _decay_tables(RET_TILE)

    qkv, g, oret, states, cat, pooled, xhat1, rstd1, x1b, g_up, g_down = _mix_forward(
        x2, w_in_t, cos, sin, dmat, qd, kd, cdec, w_pool_b, pool_scale, w_out_f, ln1_g, ln1_b,
        gather=[w_up[0].T.astype(bf16), w_down[0].astype(bf16)])
    w_up_t = g_up.reshape(2 * D_FF, D)
    w_down_f = g_down.reshape(D_FF, D)
    dz1, dz2b, du, f, loss8, d_ln2_g, d_ln2_b, d_ln1_g, d_ln1_b, d_conv_b, d_conv_w = _ffn_forward_backward(
        xhat1, rstd1, ln1_g, ln1_b, w_up_t, conv_w_f, conv_b, w_down_f, ln2_g, ln2_b, tgt)

    (dw_down,) = _weight_grad(f, dz2b, "grad_w_down", tm=D_FF // 2)
    own_down, oth_down = _pair_reduce([dw_down.reshape(N_DEV, ROWS_DOWN, D)], "pair_reduce_down")
    dw_up_t, arr_down = _weight_grad(du, x1b, "grad_w_up", tm=D_FF // 2, exchange=[oth_down])
    own_up, oth_up = _pair_reduce([dw_up_t.reshape(N_DEV, ROWS_UP, D)], "pair_reduce_up")
    up_sems, up_src, up_land, up_started = _exchange_start(oth_up, "exchange_up_start", CHIP_BARRIER_SPLIT)
    dproj, grad_x, d_w_pool, d_pool_scale, dw_out = _mix_backward(
        dz1, w_out_f, qkv, g, oret, states, pooled, cat, cos, sin, dmat, qd, kd, cdec, w_pool_b, pool_scale, w_in_t,
        after=up_started)
    small = _pack({"w_pool": d_w_pool, "pool_scale": d_pool_scale, "ln1_g": d_ln1_g, "ln1_b": d_ln1_b,
                   "conv_b": d_conv_b, "ln2_g": d_ln2_g, "ln2_b": d_ln2_b, "conv_w": d_conv_w, "loss": loss8[0, :1]})
    own_out, own_small, oth_out, oth_small = _pair_reduce(
        [dw_out.reshape(N_DEV, ROWS_OUT, D), small.reshape(N_DEV, SMALL_ROWS // N_DEV, 128)], "pair_reduce_out")
    dw_in_t, arr_out, arr_small = _weight_grad(dproj, x2, "grad_w_in", tm=IN_W // 2, exchange=[oth_out, oth_small])
    arr_up = _exchange_wait(up_sems, up_src, up_land, [dw_in_t], "exchange_up_wait")
    own_in, oth_in = _pair_reduce([dw_in_t.reshape(N_DEV, ROWS_IN, D)], "pair_reduce_in")
    in_sems, in_src, in_land, started = _exchange_start(oth_in, "exchange_in_start", CHIP_BARRIER)
    (small_piece,) = _sum_parts([own_small], [arr_small], "sum_small_grads")
    (gs_small,) = _all_gather([small_piece], "gather_small_grads")

    names = ["w_in", "w_pool", "pool_scale", "w_out", "ln1_g", "ln1_b", "w_up", "conv_w", "conv_b", "w_down",
             "ln2_g", "ln2_b"]
    w_d = dict(w_in=w_in, w_pool=w_pool, pool_scale=pool_scale, w_out=w_out, ln1_g=ln1_g, ln1_b=ln1_b, w_up=w_up,
               conv_w=conv_w, conv_b=conv_b, w_down=w_down, ln2_g=ln2_g, ln2_b=ln2_b)
    m_d = dict(w_in=m_w_in, w_pool=m_w_pool, pool_scale=m_pool_scale, w_out=m_w_out, ln1_g=m_ln1_g, ln1_b=m_ln1_b,
               w_up=m_w_up, conv_w=m_conv_w, conv_b=m_conv_b, w_down=m_w_down, ln2_g=m_ln2_g, ln2_b=m_ln2_b)
    v_d = dict(w_in=v_w_in, w_pool=v_w_pool, pool_scale=v_pool_scale, w_out=v_w_out, ln1_g=v_ln1_g, ln1_b=v_ln1_b,
               w_up=v_w_up, conv_w=v_conv_w, conv_b=v_conv_b, w_down=v_w_down, ln2_g=v_ln2_g, ln2_b=v_ln2_b)
    g_d, delta, new_m, new_v = {}, {}, {}, {}

    def big_adamw(k, own, arr, transposed, steps, after=()):
        lay = (lambda a: a[0].T) if transposed else (lambda a: a[0])
        back = (lambda a: a.T[None]) if transposed else (lambda a: a[None])
        res = _sum_adamw(own, arr, lay(w_d[k]), lay(m_d[k]), lay(v_d[k]), "adamw_" + k, steps, after)
        g_d[k], delta[k], new_m[k], new_v[k] = (back(r) for r in res)
        return res[3]

    done = [big_adamw("w_up", own_up, arr_up, True, 4, after=(started,)),
            big_adamw("w_down", own_down, arr_down, False, 2, after=(started,)),
            big_adamw("w_out", own_out, arr_out, False, 2, after=(started,))]

    gsm = _unpack(gs_small)
    gsm["conv_w"] = lax.dynamic_slice(gsm["conv_w"].reshape(3, D_FF), (0, me * (D_FF // N_DEV)), (3, D_FF // N_DEV))
    lay = lambda k, a: jnp.transpose(a, (1, 0, 2)) if k == "conv_w" else a.reshape(-1, a.shape[-1])
    back = lambda k, a: jnp.transpose(a, (1, 0, 2)) if k == "conv_w" else a.reshape(w_d[k].shape)
    group = [k for k in names if k not in ("w_in", "w_out", "w_up", "w_down")]
    for k in group:
        g_d[k] = gsm[k].reshape(w_d[k].shape)
    res = _adamw([lay(k, w_d[k]) for k in group], [lay(k, g_d[k]) for k in group], [lay(k, m_d[k]) for k in group],
                 [lay(k, v_d[k]) for k in group], "adamw_small")
    for j, k in enumerate(group):
        delta[k], new_m[k], new_v[k] = (back(k, res[part * len(group) + j]) for part in range(3))

    arr_in = _exchange_wait(in_sems, in_src, in_land, done + [res[0]], "exchange_in_wait")
    big_adamw("w_in", own_in, arr_in, True, 4)

    loss = gsm["loss"].reshape(())
    return (loss, grad_x[None], *[g_d[k] for k in names], *[delta[k] for k in names], *[new_m[k] for k in names],
            *[new_v[k] for k in names])
```

```python
import math

import numpy as np
import jax
import jax.numpy as jnp
from jax import lax
from jax.experimental import pallas as pl
from jax.experimental.pallas import tpu as pltpu

f32 = jnp.float32
bf16 = jnp.bfloat16

N_DEV = 8
T = 4096
D = 1024
CHUNK = 64
MIX_TILE = 512
RET_TILE = 256
HEADS = 4
DH = 128
RW = HEADS * DH
PW = 512
GROUPS = 4
WINDOWS = (2, 4, 8, 16)
IN_W = 4 * RW + PW
D_FF = 2816
LN_EPS = 1e-5
RMS_EPS = 1e-6
ALPHA = 2.0 ** 0.25
K_SCALE = DH ** -0.5

ADAM_LR = 0.001
ADAM_B1 = 0.9
ADAM_B2 = 0.999
ADAM_EPS = 1e-08
ADAM_WD = 0.01
ADAM_STEP = 10

ROWS_IN, ROWS_OUT, ROWS_UP, ROWS_DOWN = IN_W // N_DEV, D // N_DEV, 2 * D_FF // N_DEV, D_FF // N_DEV

V7X_VMEM_LIMIT = 56 * 2 ** 20
HALO = 32

NT = (((1,), (1,)), ((), ()))
TN = (((0,), (0,)), ((), ()))
NN = (((1,), (0,)), ((), ()))


def _dot(a, b, dims=NN):
    return lax.dot_general(a, b, dims, preferred_element_type=f32)


def _const_spec(shape):
    zeros = (0,) * len(shape)
    return pl.BlockSpec(shape, lambda i: zeros, pipeline_mode=pl.Buffered(1))


def _sigmoid(x):
    return 0.5 * jnp.tanh(0.5 * x) + 0.5


def _decay_tables(tt):
    h = np.arange(HEADS, dtype=np.float64)
    log_gamma = np.log(1.0 - 2.0 ** (-5.0 - h)).astype(np.float32).astype(np.float64)[:, None, None]
    idx = np.arange(tt, dtype=np.float64)
    visible = (idx[None, :] // CHUNK) <= (idx[:, None] // CHUNK)
    mask = np.where(visible[None], np.exp(log_gamma * np.abs(idx[:, None] - idx[None, :])[None]), 0.0)
    qd = np.broadcast_to(np.exp(log_gamma * (idx[None, :, None] + 1.0)), (HEADS, tt, DH))
    kd = np.broadcast_to(np.exp(log_gamma * (tt - 1.0 - idx[None, :, None])), (HEADS, tt, DH))
    cd = np.exp(log_gamma[:, 0, 0] * tt)
    return (jnp.asarray(mask, f32), jnp.asarray(qd, f32), jnp.asarray(kd, f32), [float(c) for c in cd])


def _rope_tables():
    inv_freq = (10000.0 ** (-np.arange(0, DH, 2, dtype=np.float64) / DH)).astype(np.float32)
    ang = (np.arange(T, dtype=np.float32)[:, None] * inv_freq[None, :]).astype(np.float64)
    cos, sin = np.cos(ang), np.sin(ang)
    return (jnp.asarray(np.concatenate([cos, cos], axis=1), f32), jnp.asarray(np.concatenate([-sin, sin], axis=1), f32))


def _swap_halves(t):
    return pltpu.roll(t, DH // 2, axis=1)


def _mix_forward(x, w_in_t, cos, sin, dmat, qd, kd, cdec, w_pool, pool_scale, w_out, ln1_g, ln1_b, gather,
                 tt=MIX_TILE):
    n_tiles = T // tt
    n_g = len(gather)

    def body(x_ref, wint_ref, cos_ref, sin_ref, dmat_ref, qd_ref, kd_ref, wpool_ref, pscale_ref, wout_ref,
             g1_ref, b1_ref, *rest):
        gin, rest = rest[:n_g], rest[n_g:]
        qkv_ref, g_ref, oret_ref, states_ref, cat_ref, pooled_ref, xhat_ref, rstd_ref, x1b_ref = rest[:9]
        gout, (state_s, pext_s, tmp_s, *sems) = rest[9:9 + n_g], rest[9 + n_g:]
        i = pl.program_id(0)

        @pl.when(i == 0)
        def _():
            state_s[...] = jnp.zeros_like(state_s)
            pext_s[:, pl.ds(0, HALO), :] = jnp.zeros((GROUPS, HALO, DH), f32)
            _gather_start(gin, gout, *sems)

        @pl.when(i == n_tiles - 2)
        def _():
            _gather_forward(gin, gout, *sems)

        xb = x_ref[...].astype(bf16)
        cos_t, sin_t = cos_ref[...], sin_ref[...]
        for part in range(2):
            pr = _dot(xb, wint_ref[pl.ds(part * RW, RW), :], NT)
            for h in range(HEADS):
                t = pr[:, h * DH:(h + 1) * DH]
                r = t * cos_t + _swap_halves(t) * sin_t
                if part == 1:
                    r = r * K_SCALE
                qkv_ref[:, part * RW + h * DH: part * RW + (h + 1) * DH] = r.astype(bf16)
        qkv_ref[:, 2 * RW:3 * RW] = _dot(xb, wint_ref[pl.ds(2 * RW, RW), :], NT).astype(bf16)
        g_ref[...] = _dot(xb, wint_ref[pl.ds(3 * RW, RW), :], NT)
        p = _dot(xb, wint_ref[pl.ds(4 * RW, PW), :], NT)
        for gi in range(GROUPS):
            pext_s[gi, pl.ds(HALO, tt), :] = p[:, gi * DH:(gi + 1) * DH]

        for sub in range(tt // RET_TILE):
            rows = pl.ds(sub * RET_TILE, RET_TILE)
            for h in range(HEADS):
                q = qkv_ref[rows, h * DH:(h + 1) * DH]
                k = qkv_ref[rows, RW + h * DH: RW + (h + 1) * DH]
                v = qkv_ref[rows, 2 * RW + h * DH: 2 * RW + (h + 1) * DH]
                s = _dot(q, k, NT) * dmat_ref[h]
                st = state_s[h]
                stb = st.astype(bf16)
                states_ref[sub, h] = stb
                oret_ref[rows, h * DH:(h + 1) * DH] = (_dot(s.astype(bf16), v)
                                                      + _dot((q.astype(f32) * qd_ref[h]).astype(bf16), stb))
                state_s[h] = st * cdec[h] + _dot((k.astype(f32) * kd_ref[h]).astype(bf16), v, TN)

        for h in range(HEADS):
            sl = slice(h * DH, (h + 1) * DH)
            o = oret_ref[:, sl]
            r = lax.rsqrt(jnp.mean(o * o, axis=-1, keepdims=True) + RMS_EPS)
            gg = g_ref[:, sl]
            cat_ref[:, sl] = (o * r * (gg * _sigmoid(gg))).astype(bf16)

        pos1 = (i * tt + lax.broadcasted_iota(jnp.int32, (tt, 1), 0) + 1).astype(f32)
        for gi, w in enumerate(WINDOWS):
            sl = slice(gi * DH, (gi + 1) * DH)
            stages = int(math.log2(w))
            src = pext_s
            for s in range(stages):
                lo = HALO - 8 * (stages - 1 - s)
                n = tt + HALO - lo
                shift = 2 ** s
                val = src[gi, pl.ds(lo, n), :] + src[gi, pl.ds(lo - shift, n), :]
                if s == stages - 1:
                    wsum = val
                else:
                    tmp_s[gi, pl.ds(lo, n), :] = val
                    src = tmp_s
            p_g = pext_s[gi, pl.ds(HALO, tt), :]
            pooled = (wsum / jnp.minimum(pos1, float(w)) - p_g).astype(bf16)
            pooled_ref[:, sl] = pooled
            y = _dot(pooled, wpool_ref[gi]) * pscale_ref[:, sl]
            cat_ref[:, RW + gi * DH: RW + (gi + 1) * DH] = y.astype(bf16)
        pext_s[:, pl.ds(0, HALO), :] = pext_s[:, pl.ds(tt, HALO), :]

        z = ALPHA * x_ref[...] + _dot(cat_ref[...], wout_ref[...])
        mu = jnp.mean(z, axis=-1, keepdims=True)
        zc = z - mu
        rstd = lax.rsqrt(jnp.mean(zc * zc, axis=-1, keepdims=True) + LN_EPS)
        xhat = zc * rstd
        xhat_ref[...] = xhat
        rstd_ref[...] = rstd
        x1b_ref[...] = (xhat * g1_ref[...] + b1_ref[...]).astype(bf16)

        @pl.when(i == n_tiles - 1)
        def _():
            _gather_finish(gin, gout, *sems)

    tile = lambda w: pl.BlockSpec((tt, w), lambda i: (i, 0))
    hbm = pl.BlockSpec(memory_space=pltpu.HBM)
    out_shape = (
        jax.ShapeDtypeStruct((T, 3 * RW), bf16),
        jax.ShapeDtypeStruct((T, RW), f32),
        jax.ShapeDtypeStruct((T, RW), f32),
        jax.ShapeDtypeStruct((T // RET_TILE, HEADS, DH, DH), bf16),
        jax.ShapeDtypeStruct((T, D), bf16),
        jax.ShapeDtypeStruct((T, PW), bf16),
        jax.ShapeDtypeStruct((T, D), f32),
        jax.ShapeDtypeStruct((T, 1), f32),
        jax.ShapeDtypeStruct((T, D), bf16),
    ) + tuple(jax.ShapeDtypeStruct((N_DEV,) + b.shape, b.dtype) for b in gather)
    return pl.pallas_call(
        body, name="mix_forward", grid=(n_tiles,), out_shape=out_shape,
        in_specs=[tile(D), _const_spec((IN_W, D)), tile(DH), tile(DH),
                  _const_spec((HEADS, RET_TILE, RET_TILE)), _const_spec((HEADS, RET_TILE, DH)),
                  _const_spec((HEADS, RET_TILE, DH)),
                  _const_spec((GROUPS, DH, DH)), _const_spec((1, PW)), _const_spec((D, D)),
                  _const_spec((1, D)), _const_spec((1, D))] + [hbm] * n_g,
        out_specs=(tile(3 * RW), tile(RW), tile(RW),
                   pl.BlockSpec((tt // RET_TILE, HEADS, DH, DH), lambda i: (i, 0, 0, 0)),
                   tile(D), tile(PW), tile(D), tile(1), tile(D)) + (hbm,) * n_g,
        scratch_shapes=[pltpu.VMEM((HEADS, DH, DH), f32), pltpu.VMEM((GROUPS, tt + HALO, DH), f32),
                        pltpu.VMEM((GROUPS, tt + HALO, DH), f32)] + _gather_sems(n_g),
        compiler_params=pltpu.CompilerParams(dimension_semantics=("arbitrary",), vmem_limit_bytes=V7X_VMEM_LIMIT,
                                             collective_id=GATHER_BARRIER),
    )(x, w_in_t, cos, sin, dmat, qd, kd, w_pool, pool_scale, w_out, ln1_g, ln1_b, *gather)


def _ffn_forward_backward(xhat1, rstd1, ln1_g, ln1_b, w_up_t, conv_w, conv_b, w_down, ln2_g, ln2_b, target,
                          tt=256):
    n_tiles = T // tt
    FH = 16
    hb = tt // FH

    def body(xhat_ref, halo_ref, rstd_ref, g1_ref, b1_ref, wupt_ref, cw_ref, cb_ref, wdown_ref, g2_ref, b2_ref, tgt_ref,
             dz1_ref, dz2b_ref, du_ref, f_ref, loss_ref, dg2_ref, db2_ref, dg1_ref, db1_ref, dcb_ref, dcw_ref,
             gext_s, val_s, dhext_s):
        i = pl.program_id(0)
        tile_idx = n_tiles - 1 - i

        def rd(ref, off):
            return jnp.concatenate([ref[k, pl.ds(off, tt), :] for k in range(D_FF // 128)], axis=1)

        def wr(ref, val):
            for k in range(D_FF // 128):
                ref[k, pl.ds(0, val.shape[0]), :] = val[:, k * 128:(k + 1) * 128]

        @pl.when(i == 0)
        def _():
            for r in (loss_ref, dg2_ref, db2_ref, dg1_ref, db1_ref, dcb_ref, dcw_ref):
                r[...] = jnp.zeros_like(r)
            dhext_s[:, pl.ds(tt, 8), :] = jnp.zeros((D_FF // 128, 8, 128), f32)

        g1, b1 = g1_ref[...], b1_ref[...]
        xhat = xhat_ref[...]
        x1 = xhat * g1 + b1
        x1b = x1.astype(bf16)
        x1h = ((halo_ref[...] * g1 + b1) * jnp.where(tile_idx == 0, 0.0, 1.0)).astype(bf16)
        x1ext = jnp.concatenate([x1h, x1b], axis=0)

        val = _dot(x1b, wupt_ref[pl.ds(0, D_FF), :], NT)
        gate_ext = _dot(x1ext, wupt_ref[pl.ds(D_FF, D_FF), :], NT)
        wr(gext_s, gate_ext)
        hh = (cb_ref[...] + cw_ref[0:1, :] * rd(gext_s, FH - 2) + cw_ref[1:2, :] * rd(gext_s, FH - 1)
              + cw_ref[2:3, :] * gate_ext[FH:])
        sg = _sigmoid(hh)
        act = hh * sg
        wr(dhext_s, act)
        val_s[...] = val * (sg + act * (1.0 - sg))
        fb = (act * val).astype(bf16)
        f_ref[...] = fb

        z = ALPHA * x1 + _dot(fb, wdown_ref[...])
        mu = jnp.mean(z, axis=-1, keepdims=True)
        zc = z - mu
        rstd2 = lax.rsqrt(jnp.mean(zc * zc, axis=-1, keepdims=True) + LN_EPS)
        xh2 = zc * rstd2
        diff = xh2 * g2_ref[...] + b2_ref[...] - tgt_ref[...]
        loss_ref[...] += 0.5 * jnp.sum(diff * diff) / D
        dy = diff * (1.0 / D)
        dg2_ref[...] += jnp.sum(dy * xh2, axis=0, keepdims=True)
        db2_ref[...] += jnp.sum(dy, axis=0, keepdims=True)
        dyg = dy * g2_ref[...]
        dz2 = rstd2 * (dyg - jnp.mean(dyg, axis=-1, keepdims=True) - xh2 * jnp.mean(dyg * xh2, axis=-1, keepdims=True))
        dz2b = dz2.astype(bf16)
        dz2b_ref[...] = dz2b

        df = _dot(dz2b, wdown_ref[...], NT)
        dval = df * rd(dhext_s, 0)
        dh = df * val_s[...]
        wr(dhext_s, dh)
        dh1, dh2, g0 = rd(dhext_s, 1), rd(dhext_s, 2), rd(gext_s, FH)
        dcb_ref[...] += jnp.sum(dh, axis=0, keepdims=True)
        dcw_ref[0:1, :] += jnp.sum(dh2 * g0, axis=0, keepdims=True)
        dcw_ref[1:2, :] += jnp.sum(dh1 * g0, axis=0, keepdims=True)
        dcw_ref[2:3, :] += jnp.sum(dh * g0, axis=0, keepdims=True)
        dgate = cw_ref[2:3, :] * dh + cw_ref[1:2, :] * dh1 + cw_ref[0:1, :] * dh2
        dvalb, dgateb = dval.astype(bf16), dgate.astype(bf16)
        du_ref[:, :D_FF] = dvalb
        du_ref[:, D_FF:] = dgateb
        dx1 = ALPHA * dz2 + _dot(dvalb, wupt_ref[pl.ds(0, D_FF), :]) + _dot(dgateb, wupt_ref[pl.ds(D_FF, D_FF), :])
        dhext_s[:, pl.ds(tt, 8), :] = dhext_s[:, pl.ds(0, 8), :]

        dg1_ref[...] += jnp.sum(dx1 * xhat, axis=0, keepdims=True)
        db1_ref[...] += jnp.sum(dx1, axis=0, keepdims=True)
        dxg = dx1 * g1
        dz1_ref[...] = rstd_ref[...] * (dxg - jnp.mean(dxg, axis=-1, keepdims=True)
                                        - xhat * jnp.mean(dxg * xhat, axis=-1, keepdims=True))

    rtile = lambda w: pl.BlockSpec((tt, w), lambda i: (n_tiles - 1 - i, 0))
    acc = lambda shape: pl.BlockSpec(shape, lambda i: (0, 0))
    out_shape = (
        jax.ShapeDtypeStruct((T, D), f32),
        jax.ShapeDtypeStruct((T, D), bf16),
        jax.ShapeDtypeStruct((T, 2 * D_FF), bf16),
        jax.ShapeDtypeStruct((T, D_FF), bf16),
        jax.ShapeDtypeStruct((8, 128), f32),
        jax.ShapeDtypeStruct((1, D), f32), jax.ShapeDtypeStruct((1, D), f32),
        jax.ShapeDtypeStruct((1, D), f32), jax.ShapeDtypeStruct((1, D), f32),
        jax.ShapeDtypeStruct((1, D_FF), f32), jax.ShapeDtypeStruct((3, D_FF), f32),
    )
    return pl.pallas_call(
        body, name="ffn_forward_backward", grid=(n_tiles,), out_shape=out_shape,
        in_specs=[rtile(D),
                  pl.BlockSpec((FH, D), lambda i: (jnp.maximum((n_tiles - 1 - i) * hb - 1, 0), 0)),
                  rtile(1), _const_spec((1, D)), _const_spec((1, D)), _const_spec((2 * D_FF, D)),
                  _const_spec((3, D_FF)), _const_spec((1, D_FF)), _const_spec((D_FF, D)),
                  _const_spec((1, D)), _const_spec((1, D)), rtile(D)],
        out_specs=(rtile(D), rtile(D), rtile(2 * D_FF), rtile(D_FF), acc((8, 128)),
                   acc((1, D)), acc((1, D)), acc((1, D)), acc((1, D)), acc((1, D_FF)), acc((3, D_FF))),
        scratch_shapes=[pltpu.VMEM((D_FF // 128, tt + FH, 128), f32), pltpu.VMEM((tt, D_FF), f32),
                        pltpu.VMEM((D_FF // 128, tt + 8, 128), f32)],
        compiler_params=pltpu.CompilerParams(dimension_semantics=("arbitrary",), vmem_limit_bytes=V7X_VMEM_LIMIT),
    )(xhat1, xhat1, rstd1, ln1_g, ln1_b, w_up_t, conv_w, conv_b, w_down, ln2_g, ln2_b, target)


def _mix_backward(dz1, w_out, qkv, g, oret, states, pooled, cat, cos, sin, dmat, qd, kd, cdec, w_pool, pool_scale, w_in_t,
                  after, tt=MIX_TILE):
    n_tiles = T // tt

    def body(dz1_ref, wout_ref, qkv_ref, g_ref, oret_ref, states_ref, pooled_ref, cat_ref, cos_ref, sin_ref, dmat_ref,
             qd_ref, kd_ref, wpool_ref, pscale_ref, wint_ref, after_ref,
             dproj_ref, gx_ref, dwpool_ref, dpscale_ref, dwout_ref, dstate_s, dout_s, eext_s, tmp_s, dwout_s):
        i = pl.program_id(0)
        tile_idx = n_tiles - 1 - i

        @pl.when(i == 0)
        def _():
            dstate_s[...] = jnp.zeros_like(dstate_s)
            dwpool_ref[...] = jnp.zeros_like(dwpool_ref)
            dpscale_ref[...] = jnp.zeros_like(dpscale_ref)
            dwout_s[...] = jnp.zeros_like(dwout_s)
            eext_s[:, pl.ds(tt, HALO), :] = jnp.zeros((GROUPS, HALO, DH), f32)

        dz1 = dz1_ref[...]
        dz1b = dz1.astype(bf16)
        dcat = _dot(dz1b, wout_ref[...], NT)
        dwout_s[...] += _dot(cat_ref[...], dz1b, TN)

        pos1 = (tile_idx * tt + lax.broadcasted_iota(jnp.int32, (tt, 1), 0) + 1).astype(f32)
        for gi, w in enumerate(WINDOWS):
            sl = slice(gi * DH, (gi + 1) * DH)
            dpo = dcat[:, RW + gi * DH: RW + (gi + 1) * DH]
            pooled_g = pooled_ref[:, sl]
            ylin = _dot(pooled_g, wpool_ref[gi])
            dpscale_ref[:, sl] += jnp.sum(dpo * ylin, axis=0, keepdims=True)
            dpw = (dpo * pscale_ref[:, sl]).astype(bf16)
            dwpool_ref[gi] += _dot(pooled_g, dpw, TN)
            dpooled = _dot(dpw, wpool_ref[gi], NT)
            eext_s[gi, pl.ds(0, tt), :] = dpooled / jnp.minimum(pos1, float(w))
            stages = int(math.log2(w))
            src = eext_s
            for s in range(stages):
                n = tt + 8 * (stages - 1 - s)
                shift = 2 ** s
                val = src[gi, pl.ds(0, n), :] + src[gi, pl.ds(shift, n), :]
                if s == stages - 1:
                    wsum = val
                else:
                    tmp_s[gi, pl.ds(0, n), :] = val
                    src = tmp_s
            dproj_ref[:, 4 * RW + gi * DH: 4 * RW + (gi + 1) * DH] = (wsum - dpooled).astype(bf16)
        eext_s[:, pl.ds(tt, HALO), :] = eext_s[:, pl.ds(0, HALO), :]

        for h in range(HEADS):
            sl = slice(h * DH, (h + 1) * DH)
            dr = dcat[:, sl]
            o = oret_ref[:, sl]
            r = lax.rsqrt(jnp.mean(o * o, axis=-1, keepdims=True) + RMS_EPS)
            rn = o * r
            gg = g_ref[:, sl]
            sg = _sigmoid(gg)
            dproj_ref[:, 3 * RW + h * DH: 3 * RW + (h + 1) * DH] = (dr * rn * (sg * (1.0 + gg * (1.0 - sg)))).astype(bf16)
            drn = dr * (gg * sg)
            dout_s[:, sl] = (r * (drn - rn * jnp.mean(drn * rn, axis=-1, keepdims=True))).astype(bf16)

        for sub in reversed(range(tt // RET_TILE)):
            rows = pl.ds(sub * RET_TILE, RET_TILE)
            cos_t, sin_t = cos_ref[rows, :], sin_ref[rows, :]
            for h in range(HEADS):
                q = qkv_ref[rows, h * DH:(h + 1) * DH]
                k = qkv_ref[rows, RW + h * DH: RW + (h + 1) * DH]
                v = qkv_ref[rows, 2 * RW + h * DH: 2 * RW + (h + 1) * DH]
                do = dout_s[rows, h * DH:(h + 1) * DH]
                stb = states_ref[sub, h]
                dst = dstate_s[h]
                dstb = dst.astype(bf16)
                sb = (_dot(q, k, NT) * dmat_ref[h]).astype(bf16)
                dsb = (_dot(do, v, NT) * dmat_ref[h]).astype(bf16)
                dq = _dot(dsb, k) + _dot(do, stb, NT) * qd_ref[h]
                dk = _dot(dsb, q, TN) + _dot(v, dstb, NT) * kd_ref[h]
                dv = _dot(sb, do, TN) + _dot((k.astype(f32) * kd_ref[h]).astype(bf16), dstb)
                dstate_s[h] = dst * cdec[h] + _dot((q.astype(f32) * qd_ref[h]).astype(bf16), do, TN)
                dproj_ref[rows, h * DH:(h + 1) * DH] = (dq * cos_t - _swap_halves(dq) * sin_t).astype(bf16)
                dproj_ref[rows, RW + h * DH: RW + (h + 1) * DH] = (
                    (dk * cos_t - _swap_halves(dk) * sin_t) * K_SCALE).astype(bf16)
                dproj_ref[rows, 2 * RW + h * DH: 2 * RW + (h + 1) * DH] = dv.astype(bf16)

        gx_ref[...] = ALPHA * dz1 + _dot(dproj_ref[...], wint_ref[...])

        @pl.when(i == n_tiles - 1)
        def _():
            dwout_ref[...] = dwout_s[...].astype(bf16)

    rtile = lambda w: pl.BlockSpec((tt, w), lambda i: (n_tiles - 1 - i, 0))
    out_shape = (
        jax.ShapeDtypeStruct((T, IN_W), bf16),
        jax.ShapeDtypeStruct((T, D), f32),
        jax.ShapeDtypeStruct((GROUPS, DH, DH), f32),
        jax.ShapeDtypeStruct((1, PW), f32),
        jax.ShapeDtypeStruct((D, D), bf16),
    )
    return pl.pallas_call(
        body, name="mix_backward", grid=(n_tiles,), out_shape=out_shape,
        in_specs=[rtile(D), _const_spec((D, D)), rtile(3 * RW), rtile(RW), rtile(RW),
                  pl.BlockSpec((tt // RET_TILE, HEADS, DH, DH), lambda i: (n_tiles - 1 - i, 0, 0, 0)),
                  rtile(PW), rtile(D), rtile(DH), rtile(DH),
                  _const_spec((HEADS, RET_TILE, RET_TILE)), _const_spec((HEADS, RET_TILE, DH)),
                  _const_spec((HEADS, RET_TILE, DH)),
                  _const_spec((GROUPS, DH, DH)), _const_spec((1, PW)), _const_spec((IN_W, D)),
                  pl.BlockSpec(memory_space=pl.ANY)],
        out_specs=(rtile(IN_W), rtile(D), pl.BlockSpec((GROUPS, DH, DH), lambda i: (0, 0, 0)),
                   pl.BlockSpec((1, PW), lambda i: (0, 0)),
                   pl.BlockSpec((D, D), lambda i: (0, 0), pipeline_mode=pl.Buffered(1))),
        scratch_shapes=[pltpu.VMEM((HEADS, DH, DH), f32), pltpu.VMEM((tt, RW), bf16),
                        pltpu.VMEM((GROUPS, tt + HALO, DH), f32), pltpu.VMEM((GROUPS, tt + HALO, DH), f32),
                        pltpu.VMEM((D, D), f32)],
        compiler_params=pltpu.CompilerParams(dimension_semantics=("arbitrary",), vmem_limit_bytes=V7X_VMEM_LIMIT),
    )(dz1, w_out, qkv, g, oret, states, pooled, cat, cos, sin, dmat, qd, kd, w_pool, pool_scale, w_in_t, after)


def _weight_grad(a, b, name, tm, exchange=()):
    m = a.shape[1]
    n_m, n_e = m // tm, len(exchange)

    def body(a_ref, b_ref, *rest):
        ein, o_ref, eout, sems = rest[:n_e], rest[n_e], rest[n_e + 1:2 * n_e + 1], rest[2 * n_e + 1:]
        i = pl.program_id(0)

        if n_e:
            @pl.when(i == 0)
            def _():
                _chip_exchange_start(ein, eout, *sems)

        o_ref[...] = _dot(a_ref[...], b_ref[...].astype(bf16), TN).astype(bf16)

        if n_e:
            @pl.when(i == n_m - 1)
            def _():
                _chip_exchange_finish(ein, eout, *sems)

    hbm = pl.BlockSpec(memory_space=pltpu.HBM)
    return pl.pallas_call(
        body, name=name, grid=(n_m,),
        out_shape=(jax.ShapeDtypeStruct((m, D), bf16),) + tuple(jax.ShapeDtypeStruct(e.shape, e.dtype) for e in exchange),
        in_specs=[pl.BlockSpec((T, tm), lambda i: (0, i)),
                  pl.BlockSpec((T, D), lambda i: (0, 0), pipeline_mode=pl.Buffered(1))] + [hbm] * n_e,
        out_specs=(pl.BlockSpec((tm, D), lambda i: (i, 0)),) + (hbm,) * n_e,
        scratch_shapes=_chip_exchange_sems(n_e),
        compiler_params=pltpu.CompilerParams(dimension_semantics=("arbitrary",), vmem_limit_bytes=V7X_VMEM_LIMIT,
                                             collective_id=CHIP_BARRIER if n_e else None),
    )(a, b, *exchange)


CHIP_FLIPS = ((1, 0), (0, 1), (1, 1))
PAIR_BARRIER, CHIP_BARRIER, GATHER_BARRIER, CHIP_BARRIER_SPLIT = 0, 1, 2, 3


def _barrier(peers):
    sem = pltpu.get_barrier_semaphore()
    for peer in peers:
        pl.semaphore_signal(sem, inc=1, device_id=peer, device_id_type=pl.DeviceIdType.MESH)
    pl.semaphore_wait(sem, len(peers))


def _me():
    return lax.axis_index("x"), lax.axis_index("y"), lax.axis_index("c")


def _chip(me, k):
    x, y, _ = me
    if k == 0:
        return x, y
    fx, fy = CHIP_FLIPS[k - 1]
    return (1 - x if fx else x), (1 - y if fy else y)


def _slot(x, y, c):
    return 4 * x + 2 * y + c


def _remote(src, dst, send_sem, recv_sem, to):
    return pltpu.make_async_remote_copy(src_ref=src, dst_ref=dst, send_sem=send_sem, recv_sem=recv_sem,
                                        device_id=to, device_id_type=pl.DeviceIdType.MESH)


def _gather_sems(n):
    return [pltpu.SemaphoreType.DMA((7, n)), pltpu.SemaphoreType.DMA((7, n)), pltpu.SemaphoreType.DMA((n,))] if n else []


def _gather_copy(k, j, gin, gout, send_sems, recv_sems, sending):
    x, y, c = _me()
    sibling, x_chip, y_chip, d_chip = (x, y, 1 - c), (1 - x, y), (x, 1 - y), (1 - x, 1 - y)
    south = c == 0
    passed_on = (jnp.where(south, 1 - x, x), jnp.where(south, y, 1 - y), c)
    src, to = gin[j], sibling
    if sending:
        block = {0: (x, y, c), 1: (x, y, c), 2: (x, y, c), 3: passed_on, 4: (*x_chip, c), 5: (*y_chip, c), 6: (*d_chip, c)}[k]
        to = {1: (*x_chip, c), 2: (*y_chip, c), 3: (jnp.where(south, x, 1 - x), jnp.where(south, 1 - y, y), c)}.get(k, sibling)
        if k >= 3:
            src = gout[j].at[_slot(*block)]
    else:
        block = {0: sibling, 1: (*x_chip, c), 2: (*y_chip, c), 3: (*d_chip, c), 4: (*x_chip, 1 - c), 5: (*y_chip, 1 - c),
                 6: (*d_chip, 1 - c)}[k]
    return _remote(src, gout[j].at[_slot(*block)], send_sems.at[k, j], recv_sems.at[k, j], to)


def _gather_do(ks, action, gin, gout, send_sems, recv_sems):
    for k in ks:
        for j in range(len(gin)):
            cp = _gather_copy(k, j, gin, gout, send_sems, recv_sems, action != "wait_recv")
            getattr(cp, action)()


def _gather_peers():
    x, y, c = _me()
    return [(x, y, 1 - c), (1 - x, y, c), (x, 1 - y, c)]


def _gather_start(gin, gout, send_sems, recv_sems, local_sems, barrier=True):
    if barrier:
        _barrier(_gather_peers())
    for j in range(len(gin)):
        pltpu.make_async_copy(gin[j], gout[j].at[_slot(*_me())], local_sems.at[j]).start()
    _gather_do((0, 1, 2), "start", gin, gout, send_sems, recv_sems)


def _gather_forward(gin, gout, send_sems, recv_sems, local_sems):
    _gather_do((1, 2), "wait_recv", gin, gout, send_sems, recv_sems)
    _gather_do((3, 4, 5), "start", gin, gout, send_sems, recv_sems)


def _gather_finish(gin, gout, send_sems, recv_sems, local_sems):
    _gather_do((3,), "wait_recv", gin, gout, send_sems, recv_sems)
    _gather_do((6,), "start", gin, gout, send_sems, recv_sems)
    _gather_do((0, 4, 5, 6), "wait_recv", gin, gout, send_sems, recv_sems)
    _gather_do(range(7), "wait_send", gin, gout, send_sems, recv_sems)
    for j in range(len(gin)):
        pltpu.make_async_copy(gin[j], gout[j].at[_slot(*_me())], local_sems.at[j]).wait()


def _all_gather(blocks, name, to_bf16=()):
    n, n_c = len(blocks), len(to_bf16)

    def body(*refs):
        gin, cin, gout, cout = refs[:n], refs[n:n + n_c], refs[n + n_c:2 * n + n_c], refs[2 * n + n_c:2 * (n + n_c)]
        sems, stage = refs[2 * (n + n_c):2 * (n + n_c) + 3], refs[2 * (n + n_c) + 3:]
        _gather_start(gin, gout, *sems)
        for src, dst, buf in zip(cin, cout, stage):
            pltpu.sync_copy(src, buf)
            dst[...] = buf[...].astype(bf16)
        _gather_forward(gin, gout, *sems)
        _gather_finish(gin, gout, *sems)

    hbm, vm = pl.BlockSpec(memory_space=pltpu.HBM), pl.BlockSpec(memory_space=pltpu.VMEM)
    return pl.pallas_call(
        body, name=name,
        out_shape=tuple(jax.ShapeDtypeStruct((N_DEV,) + b.shape, b.dtype) for b in blocks)
        + tuple(jax.ShapeDtypeStruct(a.shape, bf16) for a in to_bf16),
        in_specs=[hbm] * (n + n_c), out_specs=(hbm,) * n + (vm,) * n_c,
        scratch_shapes=_gather_sems(n) + [pltpu.VMEM(a.shape, f32) for a in to_bf16],
        compiler_params=pltpu.CompilerParams(collective_id=GATHER_BARRIER),
    )(*blocks, *to_bf16)


def _pair_reduce(parts, name):
    n = len(parts)

    def body(*refs):
        ins, own, others, landing, mine = (refs[k * n:(k + 1) * n] for k in range(5))
        send_sems, recv_sems, local_sems = refs[5 * n:]
        me = _me()
        x, y, c = me
        sibling = (x, y, 1 - c)
        _barrier([sibling])
        sends, loads = [], []
        for k in range(4):
            for j in range(n):
                cp = _remote(ins[j].at[_slot(*_chip(me, k), 1 - c)], landing[j].at[k], send_sems.at[k, j],
                             recv_sems.at[k, j], sibling)
                cp.start()
                sends.append(cp)
                ld = pltpu.make_async_copy(ins[j].at[_slot(*_chip(me, k), c)], mine[j].at[k], local_sems.at[k, j])
                ld.start()
                loads.append(ld)
        for k in range(4):
            for j in range(n):
                loads[k * n + j].wait()
                _remote(ins[j].at[0], landing[j].at[k], send_sems.at[k, j], recv_sems.at[k, j], sibling).wait_recv()
                total = mine[j][k].astype(f32) + landing[j][k].astype(f32)
                if k == 0:
                    own[j][...] = total.astype(own[j].dtype)
                else:
                    others[j][k - 1] = total.astype(others[j].dtype)
        for cp in sends:
            cp.wait_send()

    vm = pl.BlockSpec(memory_space=pltpu.VMEM)
    return pl.pallas_call(
        body, name=name,
        out_shape=tuple(jax.ShapeDtypeStruct(p.shape[1:], p.dtype) for p in parts)
        + tuple(jax.ShapeDtypeStruct((3,) + p.shape[1:], p.dtype) for p in parts),
        in_specs=[pl.BlockSpec(memory_space=pltpu.HBM)] * n, out_specs=(vm,) * (2 * n),
        scratch_shapes=[pltpu.VMEM((4,) + p.shape[1:], p.dtype) for p in parts] * 2
        + [pltpu.SemaphoreType.DMA((4, n)), pltpu.SemaphoreType.DMA((4, n)), pltpu.SemaphoreType.DMA((4, n))],
        compiler_params=pltpu.CompilerParams(vmem_limit_bytes=V7X_VMEM_LIMIT, collective_id=PAIR_BARRIER),
    )(*parts)


def _chip_exchange_sems(n):
    return [pltpu.SemaphoreType.DMA((3, n)), pltpu.SemaphoreType.DMA((3, n))] if n else []


def _chip_exchange_copy(k, j, ein, eout, send_sems, recv_sems):
    me = _me()
    return _remote(ein[j].at[k - 1], eout[j].at[k - 1], send_sems.at[k - 1, j], recv_sems.at[k - 1, j],
                   (*_chip(me, k), me[2]))


def _chip_peers():
    me = _me()
    return [(*_chip(me, k), me[2]) for k in range(1, 4)]


def _chip_exchange_start(ein, eout, send_sems, recv_sems, barrier=True):
    if barrier:
        _barrier(_chip_peers())
    for k in range(1, 4):
        for j in range(len(ein)):
            _chip_exchange_copy(k, j, ein, eout, send_sems, recv_sems).start()


def _chip_exchange_finish(ein, eout, send_sems, recv_sems):
    for k in range(1, 4):
        for j in range(len(ein)):
            _chip_exchange_copy(k, j, ein, eout, send_sems, recv_sems).wait_recv()
    for k in range(1, 4):
        for j in range(len(ein)):
            _chip_exchange_copy(k, j, ein, eout, send_sems, recv_sems).wait_send()


def _split_copies(src_ref, dst_ref, sems):
    me = _me()
    return [_remote(src_ref.at[k - 1], dst_ref.at[k - 1], sems[k - 1], sems[2 + k], (*_chip(me, k), me[2]))
            for k in range(1, 4)]


def _exchange_start(others, name, barrier_id):
    def body(src_ref, land_ref, *rest):
        sems, token_ref = rest[:6], rest[8]
        _barrier(_chip_peers())
        for copy in _split_copies(src_ref, land_ref, sems):
            copy.start()
        token_ref[...] = jnp.zeros_like(token_ref)

    hbm, sem = pl.BlockSpec(memory_space=pltpu.HBM), pl.BlockSpec(memory_space=pltpu.SEMAPHORE)
    thru = pltpu.HBM(others.shape, others.dtype)
    res = pl.pallas_call(
        body, name=name,
        out_shape=(pltpu.SemaphoreType.DMA(()),) * 6 + (thru, thru, jax.ShapeDtypeStruct((8, 128), f32)),
        in_specs=(hbm, hbm), out_specs=(sem,) * 6 + (hbm, hbm, pl.BlockSpec(memory_space=pltpu.VMEM)),
        input_output_aliases={0: 6, 1: 7},
        compiler_params=pltpu.CompilerParams(has_side_effects=pltpu.SideEffectType.DATAFLOW_SIDE_EFFECTING,
                                             collective_id=barrier_id),
    )(pltpu.with_memory_space_constraint(others, pltpu.HBM),
      pltpu.with_memory_space_constraint(lax.empty(others.shape, others.dtype), pltpu.HBM))
    return res[:6], res[6], res[7], res[8]


def _exchange_wait(sems, src_thru, land_thru, after, name):
    n_after = len(after)

    def body(src_ref, land_ref, *rest):
        for copy in _split_copies(src_ref, land_ref, rest[:6]):
            copy.wait_send()
            copy.wait_recv()

    hbm, sem = pl.BlockSpec(memory_space=pltpu.HBM), pl.BlockSpec(memory_space=pltpu.SEMAPHORE)
    thru = pltpu.HBM(src_thru.shape, src_thru.dtype)
    return pl.pallas_call(
        body, name=name, out_shape=(thru, thru),
        in_specs=(hbm, hbm) + (sem,) * 6 + (pl.BlockSpec(memory_space=pl.ANY),) * n_after, out_specs=(hbm, hbm),
        input_output_aliases={0: 0, 1: 1},
        compiler_params=pltpu.CompilerParams(has_side_effects=pltpu.SideEffectType.DATAFLOW_SIDE_EFFECTING),
    )(src_thru, land_thru, *sems, *after)[1]


def _sum_parts(owns, arrived, name):
    n = len(owns)

    def body(*refs):
        for own, arr, out in zip(refs[:n], refs[n:2 * n], refs[2 * n:]):
            acc = own[...].astype(f32)
            for k in range(3):
                acc = acc + arr[k].astype(f32)
            out[...] = acc

    vm = pl.BlockSpec(memory_space=pltpu.VMEM)
    return pl.pallas_call(
        body, name=name, out_shape=tuple(jax.ShapeDtypeStruct(o.shape, f32) for o in owns),
        in_specs=[vm] * (2 * n), out_specs=(vm,) * n,
        compiler_params=pltpu.CompilerParams(vmem_limit_bytes=V7X_VMEM_LIMIT),
    )(*owns, *arrived)


def _adam_update(w, g, m, v):
    m = ADAM_B1 * m + (1.0 - ADAM_B1) * g
    v = ADAM_B2 * v + (1.0 - ADAM_B2) * (g * g)
    m_hat = m / (1.0 - ADAM_B1 ** ADAM_STEP)
    v_hat = v / (1.0 - ADAM_B2 ** ADAM_STEP)
    return -ADAM_LR * (m_hat / (jnp.sqrt(v_hat) + ADAM_EPS) + ADAM_WD * w), m, v


def _sum_adamw(own, arrived, w, m, v, name, steps, after=()):
    rows = own.shape[0]
    br = rows // steps

    def body(own_ref, arr_ref, w_ref, m_ref, v_ref, *rest):
        g_out, d_out, m_out, v_out = rest[len(after):]
        g = own_ref[...].astype(f32)
        for k in range(3):
            g = g + arr_ref[k].astype(f32)
        g_out[...] = g
        d_out[...], m_out[...], v_out[...] = _adam_update(w_ref[...], g, m_ref[...], v_ref[...])

    blk = pl.BlockSpec((br, D), lambda i: (i, 0))
    return pl.pallas_call(
        body, name=name, grid=(steps,), out_shape=(jax.ShapeDtypeStruct((rows, D), f32),) * 4,
        in_specs=[blk, pl.BlockSpec((3, br, D), lambda i: (0, i, 0)), blk, blk, blk]
        + [pl.BlockSpec(memory_space=pl.ANY)] * len(after), out_specs=(blk,) * 4,
        compiler_params=pltpu.CompilerParams(dimension_semantics=("parallel",), vmem_limit_bytes=V7X_VMEM_LIMIT),
    )(own, arrived, w, m, v, *after)


def _adamw(ws, gs, ms, vs, name):
    n = len(ws)

    def body(*refs):
        w_r, g_r, m_r, v_r = (refs[k * n:(k + 1) * n] for k in range(4))
        d_o, m_o, v_o = (refs[(4 + k) * n:(5 + k) * n] for k in range(3))
        for j in range(n):
            d_o[j][...], m_o[j][...], v_o[j][...] = _adam_update(w_r[j][...], g_r[j][...], m_r[j][...], v_r[j][...])

    vm = pl.BlockSpec(memory_space=pltpu.VMEM)
    shapes = tuple(jax.ShapeDtypeStruct(w.shape, f32) for w in ws)
    return pl.pallas_call(
        body, name=name, out_shape=shapes * 3, in_specs=[vm] * (4 * n), out_specs=tuple([vm] * (3 * n)),
        compiler_params=pltpu.CompilerParams(vmem_limit_bytes=V7X_VMEM_LIMIT),
    )(*ws, *gs, *ms, *vs)


SMALL = (("w_pool", GROUPS * DH * DH), ("pool_scale", PW), ("ln1_g", D), ("ln1_b", D), ("conv_b", D_FF),
         ("ln2_g", D), ("ln2_b", D), ("conv_w", 3 * D_FF), ("loss", 1))
SMALL_ROWS = 640


def _pack(named):
    flat = jnp.concatenate([named[k].reshape(-1) for k, _ in SMALL])
    return jnp.pad(flat, (0, SMALL_ROWS * 128 - flat.shape[0])).reshape(SMALL_ROWS, 128)


def _unpack(packed):
    flat, out, at = packed.reshape(-1), {}, 0
    for k, size in SMALL:
        out[k] = flat[at:at + size]
        at += size
    return out


def kernel(x, w_in, w_pool, pool_scale, w_out, ln1_g, ln1_b, w_up, conv_w, conv_b, w_down, ln2_g, ln2_b, loss_target, m_w_in, m_w_pool, m_pool_scale, m_w_out, m_ln1_g, m_ln1_b, m_w_up, m_conv_w, m_conv_b, m_w_down, m_ln2_g, m_ln2_b, v_w_in, v_w_pool, v_pool_scale, v_w_out, v_ln1_g, v_ln1_b, v_w_up, v_conv_w, v_conv_b, v_w_down, v_ln2_g, v_ln2_b):
    me = 4 * lax.axis_index("x") + 2 * lax.axis_index("y") + lax.axis_index("c")
    x2, tgt = x[0], loss_target[0]

    g_in, g_out, g_cw, up_shard, down_shard = _all_gather(
        [w_in[0].T.astype(bf16), w_out[0].astype(bf16), jnp.transpose(conv_w, (1, 0, 2))], "gather_weights",
        to_bf16=[w_up[0].T, w_down[0]])
    w_in_t = g_in.reshape(IN_W, D)
    w_out_f = g_out.reshape(D, D)
    conv_w_f = jnp.transpose(g_cw[:, :, 0, :], (1, 0, 2)).reshape(3, D_FF)
    w_pool_b = w_pool[0].astype(bf16)

    cos, sin = _rope_tables()
    dmat, qd, kd, cdec = _decay_tables(RET_TILE)

    qkv, g, oret, states, cat, pooled, xhat1, rstd1, x1b, g_up, g_down = _mix_forward(
        x2, w_in_t, cos, sin, dmat, qd, kd, cdec, w_pool_b, pool_scale, w_out_f, ln1_g, ln1_b,
        gather=[up_shard, down_shard])
    w_up_t = g_up.reshape(2 * D_FF, D)
    w_down_f = g_down.reshape(D_FF, D)
    dz1, dz2b, du, f, loss8, d_ln2_g, d_ln2_b, d_ln1_g, d_ln1_b, d_conv_b, d_conv_w = _ffn_forward_backward(
        xhat1, rstd1, ln1_g, ln1_b, w_up_t, conv_w_f, conv_b, w_down_f, ln2_g, ln2_b, tgt)

    (dw_down,) = _weight_grad(f, dz2b, "grad_w_down", tm=D_FF // 2)
    own_down, oth_down = _pair_reduce([dw_down.reshape(N_DEV, ROWS_DOWN, D)], "pair_reduce_down")
    dw_up_t, arr_down = _weight_grad(du, x1b, "grad_w_up", tm=D_FF // 2, exchange=[oth_down])
    own_up, oth_up = _pair_reduce([dw_up_t.reshape(N_DEV, ROWS_UP, D)], "pair_reduce_up")
    up_sems, up_src, up_land, up_started = _exchange_start(oth_up, "exchange_up_start", CHIP_BARRIER_SPLIT)
    dproj, grad_x, d_w_pool, d_pool_scale, dw_out = _mix_backward(
        dz1, w_out_f, qkv, g, oret, states, pooled, cat, cos, sin, dmat, qd, kd, cdec, w_pool_b, pool_scale, w_in_t,
        after=up_started)
    small = _pack({"w_pool": d_w_pool, "pool_scale": d_pool_scale, "ln1_g": d_ln1_g, "ln1_b": d_ln1_b,
                   "conv_b": d_conv_b, "ln2_g": d_ln2_g, "ln2_b": d_ln2_b, "conv_w": d_conv_w, "loss": loss8[0, :1]})
    own_out, own_small, oth_out, oth_small = _pair_reduce(
        [dw_out.reshape(N_DEV, ROWS_OUT, D), small.reshape(N_DEV, SMALL_ROWS // N_DEV, 128)], "pair_reduce_out")
    dw_in_t, arr_out, arr_small = _weight_grad(dproj, x2, "grad_w_in", tm=IN_W // 2, exchange=[oth_out, oth_small])
    arr_up = _exchange_wait(up_sems, up_src, up_land, [dw_in_t], "exchange_up_wait")
    own_in, oth_in = _pair_reduce([dw_in_t.reshape(N_DEV, ROWS_IN, D)], "pair_reduce_in")
    in_sems, in_src, in_land, started = _exchange_start(oth_in, "exchange_in_start", CHIP_BARRIER)
    (small_piece,) = _sum_parts([own_small], [arr_small], "sum_small_grads")
    (gs_small,) = _all_gather([small_piece], "gather_small_grads")

    names = ["w_in", "w_pool", "pool_scale", "w_out", "ln1_g", "ln1_b", "w_up", "conv_w", "conv_b", "w_down",
             "ln2_g", "ln2_b"]
    w_d = dict(w_in=w_in, w_pool=w_pool, pool_scale=pool_scale, w_out=w_out, ln1_g=ln1_g, ln1_b=ln1_b, w_up=w_up,
               conv_w=conv_w, conv_b=conv_b, w_down=w_down, ln2_g=ln2_g, ln2_b=ln2_b)
    m_d = dict(w_in=m_w_in, w_pool=m_w_pool, pool_scale=m_pool_scale, w_out=m_w_out, ln1_g=m_ln1_g, ln1_b=m_ln1_b,
               w_up=m_w_up, conv_w=m_conv_w, conv_b=m_conv_b, w_down=m_w_down, ln2_g=m_ln2_g, ln2_b=m_ln2_b)
    v_d = dict(w_in=v_w_in, w_pool=v_w_pool, pool_scale=v_pool_scale, w_out=v_w_out, ln1_g=v_ln1_g, ln1_b=v_ln1_b,
               w_up=v_w_up, conv_w=v_conv_w, conv_b=v_conv_b, w_down=v_w_down, ln2_g=v_ln2_g, ln2_b=v_ln2_b)
    g_d, delta, new_m, new_v = {}, {}, {}, {}

    def big_adamw(k, own, arr, transposed, steps, after=()):
        lay = (lambda a: a[0].T) if transposed else (lambda a: a[0])
        back = (lambda a: a.T[None]) if transposed else (lambda a: a[None])
        res = _sum_adamw(own, arr, lay(w_d[k]), lay(m_d[k]), lay(v_d[k]), "adamw_" + k, steps, after)
        g_d[k], delta[k], new_m[k], new_v[k] = (back(r) for r in res)
        return res[3]

    done = [big_adamw("w_up", own_up, arr_up, True, 4, after=(started,)),
            big_adamw("w_down", own_down, arr_down, False, 2, after=(started,)),
            big_adamw("w_out", own_out, arr_out, False, 2, after=(started,))]

    gsm = _unpack(gs_small)
    gsm["conv_w"] = lax.dynamic_slice(gsm["conv_w"].reshape(3, D_FF), (0, me * (D_FF // N_DEV)), (3, D_FF // N_DEV))
    lay = lambda k, a: jnp.transpose(a, (1, 0, 2)) if k == "conv_w" else a.reshape(-1, a.shape[-1])
    back = lambda k, a: jnp.transpose(a, (1, 0, 2)) if k == "conv_w" else a.reshape(w_d[k].shape)
    group = [k for k in names if k not in ("w_in", "w_out", "w_up", "w_down")]
    for k in group:
        g_d[k] = gsm[k].reshape(w_d[k].shape)
    res = _adamw([lay(k, w_d[k]) for k in group], [lay(k, g_d[k]) for k in group], [lay(k, m_d[k]) for k in group],
                 [lay(k, v_d[k]) for k in group], "adamw_small")
    for j, k in enumerate(group):
        delta[k], new_m[k], new_v[k] = (back(k, res[part * len(group) + j]) for part in range(3))

    arr_in = _exchange_wait(in_sems, in_src, in_land, done + [res[0]], "exchange_in_wait")
    big_adamw("w_in", own_in, arr_in, True, 4)

    loss = gsm["loss"].reshape(())
    return (loss, grad_x[None], *[g_d[k] for k in names], *[delta[k] for k in names], *[new_m[k] for k in names],
            *[new_v[k] for k in names])
```

```python
import math

import numpy as np
import jax
import jax.numpy as jnp
from jax import lax
from jax.experimental import pallas as pl
from jax.experimental.pallas import tpu as pltpu

f32 = jnp.float32
bf16 = jnp.bfloat16

N_DEV = 8
T = 4096
D = 1024
CHUNK = 64
MIX_TILE = 512
RET_TILE = 256
HEADS = 4
DH = 128
RW = HEADS * DH
PW = 512
GROUPS = 4
WINDOWS = (2, 4, 8, 16)
IN_W = 4 * RW + PW
D_FF = 2816
LN_EPS = 1e-5
RMS_EPS = 1e-6
ALPHA = 2.0 ** 0.25
K_SCALE = DH ** -0.5

ADAM_LR = 0.001
ADAM_B1 = 0.9
ADAM_B2 = 0.999
ADAM_EPS = 1e-08
ADAM_WD = 0.01
ADAM_STEP = 10

ROWS_IN, ROWS_OUT, ROWS_UP, ROWS_DOWN = IN_W // N_DEV, D // N_DEV, 2 * D_FF // N_DEV, D_FF // N_DEV

V7X_VMEM_LIMIT = 56 * 2 ** 20
HALO = 32

NT = (((1,), (1,)), ((), ()))
TN = (((0,), (0,)), ((), ()))
NN = (((1,), (0,)), ((), ()))


def _dot(a, b, dims=NN):
    return lax.dot_general(a, b, dims, preferred_element_type=f32)


def _const_spec(shape):
    zeros = (0,) * len(shape)
    return pl.BlockSpec(shape, lambda i: zeros, pipeline_mode=pl.Buffered(1))


def _sigmoid(x):
    return 0.5 * jnp.tanh(0.5 * x) + 0.5


def _decay_tables(tt):
    h = np.arange(HEADS, dtype=np.float64)
    log_gamma = np.log(1.0 - 2.0 ** (-5.0 - h)).astype(np.float32).astype(np.float64)[:, None, None]
    idx = np.arange(tt, dtype=np.float64)
    visible = (idx[None, :] // CHUNK) <= (idx[:, None] // CHUNK)
    mask = np.where(visible[None], np.exp(log_gamma * np.abs(idx[:, None] - idx[None, :])[None]), 0.0)
    qd = np.broadcast_to(np.exp(log_gamma * (idx[None, :, None] + 1.0)), (HEADS, tt, DH))
    kd = np.broadcast_to(np.exp(log_gamma * (tt - 1.0 - idx[None, :, None])), (HEADS, tt, DH))
    cd = np.exp(log_gamma[:, 0, 0] * tt)
    return (jnp.asarray(mask, f32), jnp.asarray(qd, f32), jnp.asarray(kd, f32), [float(c) for c in cd])


def _rope_tables():
    inv_freq = (10000.0 ** (-np.arange(0, DH, 2, dtype=np.float64) / DH)).astype(np.float32)
    ang = (np.arange(T, dtype=np.float32)[:, None] * inv_freq[None, :]).astype(np.float64)
    cos, sin = np.cos(ang), np.sin(ang)
    return (jnp.asarray(np.concatenate([cos, cos], axis=1), f32), jnp.asarray(np.concatenate([-sin, sin], axis=1), f32))


def _swap_halves(t):
    return pltpu.roll(t, DH // 2, axis=1)


def _mix_forward(x, w_in_shard, w_out_shard, cos, sin, dmat, qd, kd, cdec, w_pool, pool_scale, ln1_g, ln1_b, gather,
                 tt=MIX_TILE):
    n_tiles = T // tt
    n_g = len(gather)

    def body(x_ref, win_ref, wout_ref, cos_ref, sin_ref, dmat_ref, qd_ref, kd_ref, wpool_ref, pscale_ref,
             g1_ref, b1_ref, *rest):
        gin, rest = rest[:n_g], rest[n_g:]
        qkv_ref, g_ref, oret_ref, states_ref, cat_ref, pooled_ref, xhat_ref, rstd_ref, x1b_ref = rest[:9]
        fout, gout = rest[9:11], rest[11:11 + n_g]
        state_s, pext_s, tmp_s, wint_s, wout_s, load_sems, *sems = rest[11 + n_g:]
        fin, fsems, gsems = (win_ref, wout_ref), sems[:3], sems[3:]
        i = pl.program_id(0)

        @pl.when(i == 0)
        def _():
            state_s[...] = jnp.zeros_like(state_s)
            pext_s[:, pl.ds(0, HALO), :] = jnp.zeros((GROUPS, HALO, DH), f32)
            _barrier(_gather_peers())
            _gather_start(fin, fout, *fsems, barrier=False)
            _gather_forward(fin, fout, *fsems)
            _gather_start(gin, gout, *gsems, barrier=False)
            _gather_finish(fin, fout, *fsems)
            loads = [pltpu.make_async_copy(src.at[s], dst.at[pl.ds(s * src.shape[1], src.shape[1]), :],
                                           load_sems.at[j, s])
                     for j, (src, dst) in enumerate(((fout[0], wint_s), (fout[1], wout_s))) for s in range(N_DEV)]
            for ld in loads:
                ld.start()
            for ld in loads:
                ld.wait()

        @pl.when(i == n_tiles - 3)
        def _():
            _gather_forward(gin, gout, *gsems)

        xb = x_ref[...].astype(bf16)
        cos_t, sin_t = cos_ref[...], sin_ref[...]
        for part in range(2):
            pr = _dot(xb, wint_s[pl.ds(part * RW, RW), :], NT)
            for h in range(HEADS):
                t = pr[:, h * DH:(h + 1) * DH]
                r = t * cos_t + _swap_halves(t) * sin_t
                if part == 1:
                    r = r * K_SCALE
                qkv_ref[:, part * RW + h * DH: part * RW + (h + 1) * DH] = r.astype(bf16)
        qkv_ref[:, 2 * RW:3 * RW] = _dot(xb, wint_s[pl.ds(2 * RW, RW), :], NT).astype(bf16)
        g_ref[...] = _dot(xb, wint_s[pl.ds(3 * RW, RW), :], NT)
        p = _dot(xb, wint_s[pl.ds(4 * RW, PW), :], NT)
        for gi in range(GROUPS):
            pext_s[gi, pl.ds(HALO, tt), :] = p[:, gi * DH:(gi + 1) * DH]

        for sub in range(tt // RET_TILE):
            rows = pl.ds(sub * RET_TILE, RET_TILE)
            for h in range(HEADS):
                q = qkv_ref[rows, h * DH:(h + 1) * DH]
                k = qkv_ref[rows, RW + h * DH: RW + (h + 1) * DH]
                v = qkv_ref[rows, 2 * RW + h * DH: 2 * RW + (h + 1) * DH]
                s = _dot(q, k, NT) * dmat_ref[h]
                st = state_s[h]
                stb = st.astype(bf16)
                states_ref[sub, h] = stb
                oret_ref[rows, h * DH:(h + 1) * DH] = (_dot(s.astype(bf16), v)
                                                      + _dot((q.astype(f32) * qd_ref[h]).astype(bf16), stb))
                state_s[h] = st * cdec[h] + _dot((k.astype(f32) * kd_ref[h]).astype(bf16), v, TN)

        for h in range(HEADS):
            sl = slice(h * DH, (h + 1) * DH)
            o = oret_ref[:, sl]
            r = lax.rsqrt(jnp.mean(o * o, axis=-1, keepdims=True) + RMS_EPS)
            gg = g_ref[:, sl]
            cat_ref[:, sl] = (o * r * (gg * _sigmoid(gg))).astype(bf16)

        pos1 = (i * tt + lax.broadcasted_iota(jnp.int32, (tt, 1), 0) + 1).astype(f32)
        for gi, w in enumerate(WINDOWS):
            sl = slice(gi * DH, (gi + 1) * DH)
            stages = int(math.log2(w))
            src = pext_s
            for s in range(stages):
                lo = HALO - 8 * (stages - 1 - s)
                n = tt + HALO - lo
                shift = 2 ** s
                val = src[gi, pl.ds(lo, n), :] + src[gi, pl.ds(lo - shift, n), :]
                if s == stages - 1:
                    wsum = val
                else:
                    tmp_s[gi, pl.ds(lo, n), :] = val
                    src = tmp_s
            p_g = pext_s[gi, pl.ds(HALO, tt), :]
            pooled = (wsum / jnp.minimum(pos1, float(w)) - p_g).astype(bf16)
            pooled_ref[:, sl] = pooled
            y = _dot(pooled, wpool_ref[gi]) * pscale_ref[:, sl]
            cat_ref[:, RW + gi * DH: RW + (gi + 1) * DH] = y.astype(bf16)
        pext_s[:, pl.ds(0, HALO), :] = pext_s[:, pl.ds(tt, HALO), :]

        z = ALPHA * x_ref[...] + _dot(cat_ref[...], wout_s[...])
        mu = jnp.mean(z, axis=-1, keepdims=True)
        zc = z - mu
        rstd = lax.rsqrt(jnp.mean(zc * zc, axis=-1, keepdims=True) + LN_EPS)
        xhat = zc * rstd
        xhat_ref[...] = xhat
        rstd_ref[...] = rstd
        x1b_ref[...] = (xhat * g1_ref[...] + b1_ref[...]).astype(bf16)

        @pl.when(i == n_tiles - 1)
        def _():
            _gather_finish(gin, gout, *gsems)

    tile = lambda w: pl.BlockSpec((tt, w), lambda i: (i, 0))
    hbm = pl.BlockSpec(memory_space=pltpu.HBM)
    out_shape = (
        jax.ShapeDtypeStruct((T, 3 * RW), bf16),
        jax.ShapeDtypeStruct((T, RW), f32),
        jax.ShapeDtypeStruct((T, RW), f32),
        jax.ShapeDtypeStruct((T // RET_TILE, HEADS, DH, DH), bf16),
        jax.ShapeDtypeStruct((T, D), bf16),
        jax.ShapeDtypeStruct((T, PW), bf16),
        jax.ShapeDtypeStruct((T, D), f32),
        jax.ShapeDtypeStruct((T, 1), f32),
        jax.ShapeDtypeStruct((T, D), bf16),
    ) + tuple(jax.ShapeDtypeStruct((N_DEV,) + b.shape, b.dtype) for b in [w_in_shard, w_out_shard] + list(gather))
    return pl.pallas_call(
        body, name="mix_forward", grid=(n_tiles,), out_shape=out_shape,
        in_specs=[tile(D), hbm, hbm, tile(DH), tile(DH),
                  _const_spec((HEADS, RET_TILE, RET_TILE)), _const_spec((HEADS, RET_TILE, DH)),
                  _const_spec((HEADS, RET_TILE, DH)),
                  _const_spec((GROUPS, DH, DH)), _const_spec((1, PW)),
                  _const_spec((1, D)), _const_spec((1, D))] + [hbm] * n_g,
        out_specs=(tile(3 * RW), tile(RW), tile(RW),
                   pl.BlockSpec((tt // RET_TILE, HEADS, DH, DH), lambda i: (i, 0, 0, 0)),
                   tile(D), tile(PW), tile(D), tile(1), tile(D)) + (hbm,) * (2 + n_g),
        scratch_shapes=[pltpu.VMEM((HEADS, DH, DH), f32), pltpu.VMEM((GROUPS, tt + HALO, DH), f32),
                        pltpu.VMEM((GROUPS, tt + HALO, DH), f32), pltpu.VMEM((IN_W, D), bf16), pltpu.VMEM((D, D), bf16),
                        pltpu.SemaphoreType.DMA((2, N_DEV))] + _gather_sems(2) + _gather_sems(n_g),
        compiler_params=pltpu.CompilerParams(dimension_semantics=("arbitrary",), vmem_limit_bytes=V7X_VMEM_LIMIT,
                                             collective_id=GATHER_BARRIER),
    )(x, w_in_shard, w_out_shard, cos, sin, dmat, qd, kd, w_pool, pool_scale, ln1_g, ln1_b, *gather)


def _ffn_forward_backward(xhat1, rstd1, ln1_g, ln1_b, w_up_t, conv_w, conv_b, w_down, ln2_g, ln2_b, target,
                          tt=256):
    n_tiles = T // tt
    FH = 16
    hb = tt // FH

    def body(xhat_ref, halo_ref, rstd_ref, g1_ref, b1_ref, wupt_ref, cw_ref, cb_ref, wdown_ref, g2_ref, b2_ref, tgt_ref,
             dz1_ref, dz2b_ref, du_ref, f_ref, loss_ref, dg2_ref, db2_ref, dg1_ref, db1_ref, dcb_ref, dcw_ref,
             gext_s, val_s, dhext_s):
        i = pl.program_id(0)
        tile_idx = n_tiles - 1 - i

        def rd(ref, off):
            return jnp.concatenate([ref[k, pl.ds(off, tt), :] for k in range(D_FF // 128)], axis=1)

        def wr(ref, val):
            for k in range(D_FF // 128):
                ref[k, pl.ds(0, val.shape[0]), :] = val[:, k * 128:(k + 1) * 128]

        @pl.when(i == 0)
        def _():
            for r in (loss_ref, dg2_ref, db2_ref, dg1_ref, db1_ref, dcb_ref, dcw_ref):
                r[...] = jnp.zeros_like(r)
            dhext_s[:, pl.ds(tt, 8), :] = jnp.zeros((D_FF // 128, 8, 128), f32)

        g1, b1 = g1_ref[...], b1_ref[...]
        xhat = xhat_ref[...]
        x1 = xhat * g1 + b1
        x1b = x1.astype(bf16)
        x1h = ((halo_ref[...] * g1 + b1) * jnp.where(tile_idx == 0, 0.0, 1.0)).astype(bf16)
        x1ext = jnp.concatenate([x1h, x1b], axis=0)

        val = _dot(x1b, wupt_ref[pl.ds(0, D_FF), :], NT)
        gate_ext = _dot(x1ext, wupt_ref[pl.ds(D_FF, D_FF), :], NT)
        wr(gext_s, gate_ext)
        hh = (cb_ref[...] + cw_ref[0:1, :] * rd(gext_s, FH - 2) + cw_ref[1:2, :] * rd(gext_s, FH - 1)
              + cw_ref[2:3, :] * gate_ext[FH:])
        sg = _sigmoid(hh)
        act = hh * sg
        wr(dhext_s, act)
        val_s[...] = val * (sg + act * (1.0 - sg))
        fb = (act * val).astype(bf16)
        f_ref[...] = fb

        z = ALPHA * x1 + _dot(fb, wdown_ref[...])
        mu = jnp.mean(z, axis=-1, keepdims=True)
        zc = z - mu
        rstd2 = lax.rsqrt(jnp.mean(zc * zc, axis=-1, keepdims=True) + LN_EPS)
        xh2 = zc * rstd2
        diff = xh2 * g2_ref[...] + b2_ref[...] - tgt_ref[...]
        loss_ref[...] += 0.5 * jnp.sum(diff * diff) / D
        dy = diff * (1.0 / D)
        dg2_ref[...] += jnp.sum(dy * xh2, axis=0, keepdims=True)
        db2_ref[...] += jnp.sum(dy, axis=0, keepdims=True)
        dyg = dy * g2_ref[...]
        dz2 = rstd2 * (dyg - jnp.mean(dyg, axis=-1, keepdims=True) - xh2 * jnp.mean(dyg * xh2, axis=-1, keepdims=True))
        dz2b = dz2.astype(bf16)
        dz2b_ref[...] = dz2b

        df = _dot(dz2b, wdown_ref[...], NT)
        dval = df * rd(dhext_s, 0)
        dh = df * val_s[...]
        wr(dhext_s, dh)
        dh1, dh2, g0 = rd(dhext_s, 1), rd(dhext_s, 2), rd(gext_s, FH)
        dcb_ref[...] += jnp.sum(dh, axis=0, keepdims=True)
        dcw_ref[0:1, :] += jnp.sum(dh2 * g0, axis=0, keepdims=True)
        dcw_ref[1:2, :] += jnp.sum(dh1 * g0, axis=0, keepdims=True)
        dcw_ref[2:3, :] += jnp.sum(dh * g0, axis=0, keepdims=True)
        dgate = cw_ref[2:3, :] * dh + cw_ref[1:2, :] * dh1 + cw_ref[0:1, :] * dh2
        dvalb, dgateb = dval.astype(bf16), dgate.astype(bf16)
        du_ref[:, :D_FF] = dvalb
        du_ref[:, D_FF:] = dgateb
        dx1 = ALPHA * dz2 + _dot(dvalb, wupt_ref[pl.ds(0, D_FF), :]) + _dot(dgateb, wupt_ref[pl.ds(D_FF, D_FF), :])
        dhext_s[:, pl.ds(tt, 8), :] = dhext_s[:, pl.ds(0, 8), :]

        dg1_ref[...] += jnp.sum(dx1 * xhat, axis=0, keepdims=True)
        db1_ref[...] += jnp.sum(dx1, axis=0, keepdims=True)
        dxg = dx1 * g1
        dz1_ref[...] = rstd_ref[...] * (dxg - jnp.mean(dxg, axis=-1, keepdims=True)
                                        - xhat * jnp.mean(dxg * xhat, axis=-1, keepdims=True))

    rtile = lambda w: pl.BlockSpec((tt, w), lambda i: (n_tiles - 1 - i, 0))
    acc = lambda shape: pl.BlockSpec(shape, lambda i: (0, 0))
    out_shape = (
        jax.ShapeDtypeStruct((T, D), f32),
        jax.ShapeDtypeStruct((T, D), bf16),
        jax.ShapeDtypeStruct((T, 2 * D_FF), bf16),
        jax.ShapeDtypeStruct((T, D_FF), bf16),
        jax.ShapeDtypeStruct((8, 128), f32),
        jax.ShapeDtypeStruct((1, D), f32), jax.ShapeDtypeStruct((1, D), f32),
        jax.ShapeDtypeStruct((1, D), f32), jax.ShapeDtypeStruct((1, D), f32),
        jax.ShapeDtypeStruct((1, D_FF), f32), jax.ShapeDtypeStruct((3, D_FF), f32),
    )
    return pl.pallas_call(
        body, name="ffn_forward_backward", grid=(n_tiles,), out_shape=out_shape,
        in_specs=[rtile(D),
                  pl.BlockSpec((FH, D), lambda i: (jnp.maximum((n_tiles - 1 - i) * hb - 1, 0), 0)),
                  rtile(1), _const_spec((1, D)), _const_spec((1, D)), _const_spec((2 * D_FF, D)),
                  _const_spec((3, D_FF)), _const_spec((1, D_FF)), _const_spec((D_FF, D)),
                  _const_spec((1, D)), _const_spec((1, D)), rtile(D)],
        out_specs=(rtile(D), rtile(D), rtile(2 * D_FF), rtile(D_FF), acc((8, 128)),
                   acc((1, D)), acc((1, D)), acc((1, D)), acc((1, D)), acc((1, D_FF)), acc((3, D_FF))),
        scratch_shapes=[pltpu.VMEM((D_FF // 128, tt + FH, 128), f32), pltpu.VMEM((tt, D_FF), f32),
                        pltpu.VMEM((D_FF // 128, tt + 8, 128), f32)],
        compiler_params=pltpu.CompilerParams(dimension_semantics=("arbitrary",), vmem_limit_bytes=V7X_VMEM_LIMIT),
    )(xhat1, xhat1, rstd1, ln1_g, ln1_b, w_up_t, conv_w, conv_b, w_down, ln2_g, ln2_b, target)


def _mix_backward(dz1, w_out, qkv, g, oret, states, pooled, cat, cos, sin, dmat, qd, kd, cdec, w_pool, pool_scale, w_in_t,
                  after, tt=MIX_TILE):
    n_tiles = T // tt

    def body(dz1_ref, wout_ref, qkv_ref, g_ref, oret_ref, states_ref, pooled_ref, cat_ref, cos_ref, sin_ref, dmat_ref,
             qd_ref, kd_ref, wpool_ref, pscale_ref, wint_ref, after_ref,
             dproj_ref, gx_ref, dwpool_ref, dpscale_ref, dwout_ref, dstate_s, dout_s, eext_s, tmp_s, dwout_s):
        i = pl.program_id(0)
        tile_idx = n_tiles - 1 - i

        @pl.when(i == 0)
        def _():
            dstate_s[...] = jnp.zeros_like(dstate_s)
            dwpool_ref[...] = jnp.zeros_like(dwpool_ref)
            dpscale_ref[...] = jnp.zeros_like(dpscale_ref)
            dwout_s[...] = jnp.zeros_like(dwout_s)
            eext_s[:, pl.ds(tt, HALO), :] = jnp.zeros((GROUPS, HALO, DH), f32)

        dz1 = dz1_ref[...]
        dz1b = dz1.astype(bf16)
        dcat = _dot(dz1b, wout_ref[...], NT)
        dwout_s[...] += _dot(cat_ref[...], dz1b, TN)

        pos1 = (tile_idx * tt + lax.broadcasted_iota(jnp.int32, (tt, 1), 0) + 1).astype(f32)
        for gi, w in enumerate(WINDOWS):
            sl = slice(gi * DH, (gi + 1) * DH)
            dpo = dcat[:, RW + gi * DH: RW + (gi + 1) * DH]
            pooled_g = pooled_ref[:, sl]
            ylin = _dot(pooled_g, wpool_ref[gi])
            dpscale_ref[:, sl] += jnp.sum(dpo * ylin, axis=0, keepdims=True)
            dpw = (dpo * pscale_ref[:, sl]).astype(bf16)
            dwpool_ref[gi] += _dot(pooled_g, dpw, TN)
            dpooled = _dot(dpw, wpool_ref[gi], NT)
            eext_s[gi, pl.ds(0, tt), :] = dpooled / jnp.minimum(pos1, float(w))
            stages = int(math.log2(w))
            src = eext_s
            for s in range(stages):
                n = tt + 8 * (stages - 1 - s)
                shift = 2 ** s
                val = src[gi, pl.ds(0, n), :] + src[gi, pl.ds(shift, n), :]
                if s == stages - 1:
                    wsum = val
                else:
                    tmp_s[gi, pl.ds(0, n), :] = val
                    src = tmp_s
            dproj_ref[:, 4 * RW + gi * DH: 4 * RW + (gi + 1) * DH] = (wsum - dpooled).astype(bf16)
        eext_s[:, pl.ds(tt, HALO), :] = eext_s[:, pl.ds(0, HALO), :]

        for h in range(HEADS):
            sl = slice(h * DH, (h + 1) * DH)
            dr = dcat[:, sl]
            o = oret_ref[:, sl]
            r = lax.rsqrt(jnp.mean(o * o, axis=-1, keepdims=True) + RMS_EPS)
            rn = o * r
            gg = g_ref[:, sl]
            sg = _sigmoid(gg)
            dproj_ref[:, 3 * RW + h * DH: 3 * RW + (h + 1) * DH] = (dr * rn * (sg * (1.0 + gg * (1.0 - sg)))).astype(bf16)
            drn = dr * (gg * sg)
            dout_s[:, sl] = (r * (drn - rn * jnp.mean(drn * rn, axis=-1, keepdims=True))).astype(bf16)

        for sub in reversed(range(tt // RET_TILE)):
            rows = pl.ds(sub * RET_TILE, RET_TILE)
            cos_t, sin_t = cos_ref[rows, :], sin_ref[rows, :]
            for h in range(HEADS):
                q = qkv_ref[rows, h * DH:(h + 1) * DH]
                k = qkv_ref[rows, RW + h * DH: RW + (h + 1) * DH]
                v = qkv_ref[rows, 2 * RW + h * DH: 2 * RW + (h + 1) * DH]
                do = dout_s[rows, h * DH:(h + 1) * DH]
                stb = states_ref[sub, h]
                dst = dstate_s[h]
                dstb = dst.astype(bf16)
                sb = (_dot(q, k, NT) * dmat_ref[h]).astype(bf16)
                dsb = (_dot(do, v, NT) * dmat_ref[h]).astype(bf16)
                dq = _dot(dsb, k) + _dot(do, stb, NT) * qd_ref[h]
                dk = _dot(dsb, q, TN) + _dot(v, dstb, NT) * kd_ref[h]
                dv = _dot(sb, do, TN) + _dot((k.astype(f32) * kd_ref[h]).astype(bf16), dstb)
                dstate_s[h] = dst * cdec[h] + _dot((q.astype(f32) * qd_ref[h]).astype(bf16), do, TN)
                dproj_ref[rows, h * DH:(h + 1) * DH] = (dq * cos_t - _swap_halves(dq) * sin_t).astype(bf16)
                dproj_ref[rows, RW + h * DH: RW + (h + 1) * DH] = (
                    (dk * cos_t - _swap_halves(dk) * sin_t) * K_SCALE).astype(bf16)
                dproj_ref[rows, 2 * RW + h * DH: 2 * RW + (h + 1) * DH] = dv.astype(bf16)

        gx_ref[...] = ALPHA * dz1 + _dot(dproj_ref[...], wint_ref[...])

        @pl.when(i == n_tiles - 1)
        def _():
            dwout_ref[...] = dwout_s[...].astype(bf16)

    rtile = lambda w: pl.BlockSpec((tt, w), lambda i: (n_tiles - 1 - i, 0))
    out_shape = (
        jax.ShapeDtypeStruct((T, IN_W), bf16),
        jax.ShapeDtypeStruct((T, D), f32),
        jax.ShapeDtypeStruct((GROUPS, DH, DH), f32),
        jax.ShapeDtypeStruct((1, PW), f32),
        jax.ShapeDtypeStruct((D, D), bf16),
    )
    return pl.pallas_call(
        body, name="mix_backward", grid=(n_tiles,), out_shape=out_shape,
        in_specs=[rtile(D), _const_spec((D, D)), rtile(3 * RW), rtile(RW), rtile(RW),
                  pl.BlockSpec((tt // RET_TILE, HEADS, DH, DH), lambda i: (n_tiles - 1 - i, 0, 0, 0)),
                  rtile(PW), rtile(D), rtile(DH), rtile(DH),
                  _const_spec((HEADS, RET_TILE, RET_TILE)), _const_spec((HEADS, RET_TILE, DH)),
                  _const_spec((HEADS, RET_TILE, DH)),
                  _const_spec((GROUPS, DH, DH)), _const_spec((1, PW)), _const_spec((IN_W, D)),
                  pl.BlockSpec(memory_space=pl.ANY)],
        out_specs=(rtile(IN_W), rtile(D), pl.BlockSpec((GROUPS, DH, DH), lambda i: (0, 0, 0)),
                   pl.BlockSpec((1, PW), lambda i: (0, 0)),
                   pl.BlockSpec((D, D), lambda i: (0, 0), pipeline_mode=pl.Buffered(1))),
        scratch_shapes=[pltpu.VMEM((HEADS, DH, DH), f32), pltpu.VMEM((tt, RW), bf16),
                        pltpu.VMEM((GROUPS, tt + HALO, DH), f32), pltpu.VMEM((GROUPS, tt + HALO, DH), f32),
                        pltpu.VMEM((D, D), f32)],
        compiler_params=pltpu.CompilerParams(dimension_semantics=("arbitrary",), vmem_limit_bytes=V7X_VMEM_LIMIT),
    )(dz1, w_out, qkv, g, oret, states, pooled, cat, cos, sin, dmat, qd, kd, w_pool, pool_scale, w_in_t, after)


def _weight_grad(a, b, name, tm, exchange=()):
    m = a.shape[1]
    n_m, n_e = m // tm, len(exchange)

    def body(a_ref, b_ref, *rest):
        ein, o_ref, eout, sems = rest[:n_e], rest[n_e], rest[n_e + 1:2 * n_e + 1], rest[2 * n_e + 1:]
        i = pl.program_id(0)

        if n_e:
            @pl.when(i == 0)
            def _():
                _chip_exchange_start(ein, eout, *sems)

        o_ref[...] = _dot(a_ref[...], b_ref[...].astype(bf16), TN).astype(bf16)

        if n_e:
            @pl.when(i == n_m - 1)
            def _():
                _chip_exchange_finish(ein, eout, *sems)

    hbm = pl.BlockSpec(memory_space=pltpu.HBM)
    return pl.pallas_call(
        body, name=name, grid=(n_m,),
        out_shape=(jax.ShapeDtypeStruct((m, D), bf16),) + tuple(jax.ShapeDtypeStruct(e.shape, e.dtype) for e in exchange),
        in_specs=[pl.BlockSpec((T, tm), lambda i: (0, i)),
                  pl.BlockSpec((T, D), lambda i: (0, 0), pipeline_mode=pl.Buffered(1))] + [hbm] * n_e,
        out_specs=(pl.BlockSpec((tm, D), lambda i: (i, 0)),) + (hbm,) * n_e,
        scratch_shapes=_chip_exchange_sems(n_e),
        compiler_params=pltpu.CompilerParams(dimension_semantics=("arbitrary",), vmem_limit_bytes=V7X_VMEM_LIMIT,
                                             collective_id=CHIP_BARRIER if n_e else None),
    )(a, b, *exchange)


CHIP_FLIPS = ((1, 0), (0, 1), (1, 1))
PAIR_BARRIER, CHIP_BARRIER, GATHER_BARRIER, CHIP_BARRIER_SPLIT = 0, 1, 2, 3


def _barrier(peers):
    sem = pltpu.get_barrier_semaphore()
    for peer in peers:
        pl.semaphore_signal(sem, inc=1, device_id=peer, device_id_type=pl.DeviceIdType.MESH)
    pl.semaphore_wait(sem, len(peers))


def _me():
    return lax.axis_index("x"), lax.axis_index("y"), lax.axis_index("c")


def _chip(me, k):
    x, y, _ = me
    if k == 0:
        return x, y
    fx, fy = CHIP_FLIPS[k - 1]
    return (1 - x if fx else x), (1 - y if fy else y)


def _slot(x, y, c):
    return 4 * x + 2 * y + c


def _remote(src, dst, send_sem, recv_sem, to):
    return pltpu.make_async_remote_copy(src_ref=src, dst_ref=dst, send_sem=send_sem, recv_sem=recv_sem,
                                        device_id=to, device_id_type=pl.DeviceIdType.MESH)


def _gather_sems(n):
    return [pltpu.SemaphoreType.DMA((7, n)), pltpu.SemaphoreType.DMA((7, n)), pltpu.SemaphoreType.DMA((n,))] if n else []


def _gather_copy(k, j, gin, gout, send_sems, recv_sems, sending):
    x, y, c = _me()
    sibling, x_chip, y_chip, d_chip = (x, y, 1 - c), (1 - x, y), (x, 1 - y), (1 - x, 1 - y)
    south = c == 0
    passed_on = (jnp.where(south, 1 - x, x), jnp.where(south, y, 1 - y), c)
    src, to = gin[j], sibling
    if sending:
        block = {0: (x, y, c), 1: (x, y, c), 2: (x, y, c), 3: passed_on, 4: (*x_chip, c), 5: (*y_chip, c), 6: (*d_chip, c)}[k]
        to = {1: (*x_chip, c), 2: (*y_chip, c), 3: (jnp.where(south, x, 1 - x), jnp.where(south, 1 - y, y), c)}.get(k, sibling)
        if k >= 3:
            src = gout[j].at[_slot(*block)]
    else:
        block = {0: sibling, 1: (*x_chip, c), 2: (*y_chip, c), 3: (*d_chip, c), 4: (*x_chip, 1 - c), 5: (*y_chip, 1 - c),
                 6: (*d_chip, 1 - c)}[k]
    return _remote(src, gout[j].at[_slot(*block)], send_sems.at[k, j], recv_sems.at[k, j], to)


def _gather_do(ks, action, gin, gout, send_sems, recv_sems):
    for k in ks:
        for j in range(len(gin)):
            cp = _gather_copy(k, j, gin, gout, send_sems, recv_sems, action != "wait_recv")
            getattr(cp, action)()


def _gather_peers():
    x, y, c = _me()
    return [(x, y, 1 - c), (1 - x, y, c), (x, 1 - y, c)]


def _gather_start(gin, gout, send_sems, recv_sems, local_sems, barrier=True):
    if barrier:
        _barrier(_gather_peers())
    for j in range(len(gin)):
        pltpu.make_async_copy(gin[j], gout[j].at[_slot(*_me())], local_sems.at[j]).start()
    _gather_do((0, 1, 2), "start", gin, gout, send_sems, recv_sems)


def _gather_forward(gin, gout, send_sems, recv_sems, local_sems):
    _gather_do((1, 2), "wait_recv", gin, gout, send_sems, recv_sems)
    _gather_do((3, 4, 5), "start", gin, gout, send_sems, recv_sems)


def _gather_finish(gin, gout, send_sems, recv_sems, local_sems):
    _gather_do((3,), "wait_recv", gin, gout, send_sems, recv_sems)
    _gather_do((6,), "start", gin, gout, send_sems, recv_sems)
    _gather_do((0, 4, 5, 6), "wait_recv", gin, gout, send_sems, recv_sems)
    _gather_do(range(7), "wait_send", gin, gout, send_sems, recv_sems)
    for j in range(len(gin)):
        pltpu.make_async_copy(gin[j], gout[j].at[_slot(*_me())], local_sems.at[j]).wait()


def _all_gather(blocks, name):
    n = len(blocks)

    def body(*refs):
        gin, gout, sems = refs[:n], refs[n:2 * n], refs[2 * n:]
        _gather_start(gin, gout, *sems)
        _gather_forward(gin, gout, *sems)
        _gather_finish(gin, gout, *sems)

    hbm = pl.BlockSpec(memory_space=pltpu.HBM)
    return pl.pallas_call(
        body, name=name,
        out_shape=tuple(jax.ShapeDtypeStruct((N_DEV,) + b.shape, b.dtype) for b in blocks),
        in_specs=[hbm] * n, out_specs=(hbm,) * n, scratch_shapes=_gather_sems(n),
        compiler_params=pltpu.CompilerParams(collective_id=GATHER_BARRIER),
    )(*blocks)


def _pair_reduce(parts, name):
    n = len(parts)

    def body(*refs):
        ins, own, others, landing, mine = (refs[k * n:(k + 1) * n] for k in range(5))
        send_sems, recv_sems, local_sems = refs[5 * n:]
        me = _me()
        x, y, c = me
        sibling = (x, y, 1 - c)
        _barrier([sibling])
        sends, loads = [], []
        for k in range(4):
            for j in range(n):
                cp = _remote(ins[j].at[_slot(*_chip(me, k), 1 - c)], landing[j].at[k], send_sems.at[k, j],
                             recv_sems.at[k, j], sibling)
                cp.start()
                sends.append(cp)
                ld = pltpu.make_async_copy(ins[j].at[_slot(*_chip(me, k), c)], mine[j].at[k], local_sems.at[k, j])
                ld.start()
                loads.append(ld)
        for k in range(4):
            for j in range(n):
                loads[k * n + j].wait()
                _remote(ins[j].at[0], landing[j].at[k], send_sems.at[k, j], recv_sems.at[k, j], sibling).wait_recv()
                total = mine[j][k].astype(f32) + landing[j][k].astype(f32)
                if k == 0:
                    own[j][...] = total.astype(own[j].dtype)
                else:
                    others[j][k - 1] = total.astype(others[j].dtype)
        for cp in sends:
            cp.wait_send()

    vm = pl.BlockSpec(memory_space=pltpu.VMEM)
    return pl.pallas_call(
        body, name=name,
        out_shape=tuple(jax.ShapeDtypeStruct(p.shape[1:], p.dtype) for p in parts)
        + tuple(jax.ShapeDtypeStruct((3,) + p.shape[1:], p.dtype) for p in parts),
        in_specs=[pl.BlockSpec(memory_space=pltpu.HBM)] * n, out_specs=(vm,) * (2 * n),
        scratch_shapes=[pltpu.VMEM((4,) + p.shape[1:], p.dtype) for p in parts] * 2
        + [pltpu.SemaphoreType.DMA((4, n)), pltpu.SemaphoreType.DMA((4, n)), pltpu.SemaphoreType.DMA((4, n))],
        compiler_params=pltpu.CompilerParams(vmem_limit_bytes=V7X_VMEM_LIMIT, collective_id=PAIR_BARRIER),
    )(*parts)


def _chip_exchange_sems(n):
    return [pltpu.SemaphoreType.DMA((3, n)), pltpu.SemaphoreType.DMA((3, n))] if n else []


def _chip_exchange_copy(k, j, ein, eout, send_sems, recv_sems):
    me = _me()
    return _remote(ein[j].at[k - 1], eout[j].at[k - 1], send_sems.at[k - 1, j], recv_sems.at[k - 1, j],
                   (*_chip(me, k), me[2]))


def _chip_peers():
    me = _me()
    return [(*_chip(me, k), me[2]) for k in range(1, 4)]


def _chip_exchange_start(ein, eout, send_sems, recv_sems, barrier=True):
    if barrier:
        _barrier(_chip_peers())
    for k in range(1, 4):
        for j in range(len(ein)):
            _chip_exchange_copy(k, j, ein, eout, send_sems, recv_sems).start()


def _chip_exchange_finish(ein, eout, send_sems, recv_sems):
    for k in range(1, 4):
        for j in range(len(ein)):
            _chip_exchange_copy(k, j, ein, eout, send_sems, recv_sems).wait_recv()
    for k in range(1, 4):
        for j in range(len(ein)):
            _chip_exchange_copy(k, j, ein, eout, send_sems, recv_sems).wait_send()


def _split_copies(src_ref, dst_ref, sems):
    me = _me()
    return [_remote(src_ref.at[k - 1], dst_ref.at[k - 1], sems[k - 1], sems[2 + k], (*_chip(me, k), me[2]))
            for k in range(1, 4)]


def _exchange_start(others, name, barrier_id):
    def body(src_ref, land_ref, *rest):
        sems, token_ref = rest[:6], rest[8]
        _barrier(_chip_peers())
        for copy in _split_copies(src_ref, land_ref, sems):
            copy.start()
        token_ref[...] = jnp.zeros_like(token_ref)

    hbm, sem = pl.BlockSpec(memory_space=pltpu.HBM), pl.BlockSpec(memory_space=pltpu.SEMAPHORE)
    thru = pltpu.HBM(others.shape, others.dtype)
    res = pl.pallas_call(
        body, name=name,
        out_shape=(pltpu.SemaphoreType.DMA(()),) * 6 + (thru, thru, jax.ShapeDtypeStruct((8, 128), f32)),
        in_specs=(hbm, hbm), out_specs=(sem,) * 6 + (hbm, hbm, pl.BlockSpec(memory_space=pltpu.VMEM)),
        input_output_aliases={0: 6, 1: 7},
        compiler_params=pltpu.CompilerParams(has_side_effects=pltpu.SideEffectType.DATAFLOW_SIDE_EFFECTING,
                                             collective_id=barrier_id),
    )(pltpu.with_memory_space_constraint(others, pltpu.HBM),
      pltpu.with_memory_space_constraint(lax.empty(others.shape, others.dtype), pltpu.HBM))
    return res[:6], res[6], res[7], res[8]


def _exchange_wait(sems, src_thru, land_thru, after, name):
    n_after = len(after)

    def body(src_ref, land_ref, *rest):
        for copy in _split_copies(src_ref, land_ref, rest[:6]):
            copy.wait_send()
            copy.wait_recv()

    hbm, sem = pl.BlockSpec(memory_space=pltpu.HBM), pl.BlockSpec(memory_space=pltpu.SEMAPHORE)
    thru = pltpu.HBM(src_thru.shape, src_thru.dtype)
    return pl.pallas_call(
        body, name=name, out_shape=(thru, thru),
        in_specs=(hbm, hbm) + (sem,) * 6 + (pl.BlockSpec(memory_space=pl.ANY),) * n_after, out_specs=(hbm, hbm),
        input_output_aliases={0: 0, 1: 1},
        compiler_params=pltpu.CompilerParams(has_side_effects=pltpu.SideEffectType.DATAFLOW_SIDE_EFFECTING),
    )(src_thru, land_thru, *sems, *after)[1]


def _sum_parts(owns, arrived, name):
    n = len(owns)

    def body(*refs):
        for own, arr, out in zip(refs[:n], refs[n:2 * n], refs[2 * n:]):
            acc = own[...].astype(f32)
            for k in range(3):
                acc = acc + arr[k].astype(f32)
            out[...] = acc

    vm = pl.BlockSpec(memory_space=pltpu.VMEM)
    return pl.pallas_call(
        body, name=name, out_shape=tuple(jax.ShapeDtypeStruct(o.shape, f32) for o in owns),
        in_specs=[vm] * (2 * n), out_specs=(vm,) * n,
        compiler_params=pltpu.CompilerParams(vmem_limit_bytes=V7X_VMEM_LIMIT),
    )(*owns, *arrived)


def _adam_update(w, g, m, v):
    m = ADAM_B1 * m + (1.0 - ADAM_B1) * g
    v = ADAM_B2 * v + (1.0 - ADAM_B2) * (g * g)
    m_hat = m / (1.0 - ADAM_B1 ** ADAM_STEP)
    v_hat = v / (1.0 - ADAM_B2 ** ADAM_STEP)
    return -ADAM_LR * (m_hat / (jnp.sqrt(v_hat) + ADAM_EPS) + ADAM_WD * w), m, v


def _sum_adamw(own, arrived, w, m, v, name, steps, after=()):
    rows = own.shape[0]
    br = rows // steps

    def body(own_ref, arr_ref, w_ref, m_ref, v_ref, *rest):
        g_out, d_out, m_out, v_out = rest[len(after):]
        g = own_ref[...].astype(f32)
        for k in range(3):
            g = g + arr_ref[k].astype(f32)
        g_out[...] = g
        d_out[...], m_out[...], v_out[...] = _adam_update(w_ref[...], g, m_ref[...], v_ref[...])

    blk = pl.BlockSpec((br, D), lambda i: (i, 0))
    return pl.pallas_call(
        body, name=name, grid=(steps,), out_shape=(jax.ShapeDtypeStruct((rows, D), f32),) * 4,
        in_specs=[blk, pl.BlockSpec((3, br, D), lambda i: (0, i, 0)), blk, blk, blk]
        + [pl.BlockSpec(memory_space=pl.ANY)] * len(after), out_specs=(blk,) * 4,
        compiler_params=pltpu.CompilerParams(dimension_semantics=("parallel",), vmem_limit_bytes=V7X_VMEM_LIMIT),
    )(own, arrived, w, m, v, *after)


def _adamw(ws, gs, ms, vs, name):
    n = len(ws)

    def body(*refs):
        w_r, g_r, m_r, v_r = (refs[k * n:(k + 1) * n] for k in range(4))
        d_o, m_o, v_o = (refs[(4 + k) * n:(5 + k) * n] for k in range(3))
        for j in range(n):
            d_o[j][...], m_o[j][...], v_o[j][...] = _adam_update(w_r[j][...], g_r[j][...], m_r[j][...], v_r[j][...])

    vm = pl.BlockSpec(memory_space=pltpu.VMEM)
    shapes = tuple(jax.ShapeDtypeStruct(w.shape, f32) for w in ws)
    return pl.pallas_call(
        body, name=name, out_shape=shapes * 3, in_specs=[vm] * (4 * n), out_specs=tuple([vm] * (3 * n)),
        compiler_params=pltpu.CompilerParams(vmem_limit_bytes=V7X_VMEM_LIMIT),
    )(*ws, *gs, *ms, *vs)


SMALL = (("w_pool", GROUPS * DH * DH), ("pool_scale", PW), ("ln1_g", D), ("ln1_b", D), ("conv_b", D_FF),
         ("ln2_g", D), ("ln2_b", D), ("conv_w", 3 * D_FF), ("loss", 1))
SMALL_ROWS = 640


def _pack(named):
    flat = jnp.concatenate([named[k].reshape(-1) for k, _ in SMALL])
    return jnp.pad(flat, (0, SMALL_ROWS * 128 - flat.shape[0])).reshape(SMALL_ROWS, 128)


def _unpack(packed):
    flat, out, at = packed.reshape(-1), {}, 0
    for k, size in SMALL:
        out[k] = flat[at:at + size]
        at += size
    return out


def kernel(x, w_in, w_pool, pool_scale, w_out, ln1_g, ln1_b, w_up, conv_w, conv_b, w_down, ln2_g, ln2_b, loss_target, m_w_in, m_w_pool, m_pool_scale, m_w_out, m_ln1_g, m_ln1_b, m_w_up, m_conv_w, m_conv_b, m_w_down, m_ln2_g, m_ln2_b, v_w_in, v_w_pool, v_pool_scale, v_w_out, v_ln1_g, v_ln1_b, v_w_up, v_conv_w, v_conv_b, v_w_down, v_ln2_g, v_ln2_b):
    me = 4 * lax.axis_index("x") + 2 * lax.axis_index("y") + lax.axis_index("c")
    x2, tgt = x[0], loss_target[0]

    w_pool_b = w_pool[0].astype(bf16)
    cos, sin = _rope_tables()
    dmat, qd, kd, cdec = _decay_tables(RET_TILE)

    qkv, g, oret, states, cat, pooled, xhat1, rstd1, x1b, g_in, g_out, g_up, g_down, g_cw = _mix_forward(
        x2, w_in[0].T.astype(bf16), w_out[0].astype(bf16), cos, sin, dmat, qd, kd, cdec, w_pool_b, pool_scale,
        ln1_g, ln1_b, gather=[w_up[0].T.astype(bf16), w_down[0].astype(bf16), jnp.transpose(conv_w, (1, 0, 2))])
    w_in_t = g_in.reshape(IN_W, D)
    w_out_f = g_out.reshape(D, D)
    w_up_t = g_up.reshape(2 * D_FF, D)
    w_down_f = g_down.reshape(D_FF, D)
    conv_w_f = jnp.transpose(g_cw[:, :, 0, :], (1, 0, 2)).reshape(3, D_FF)
    dz1, dz2b, du, f, loss8, d_ln2_g, d_ln2_b, d_ln1_g, d_ln1_b, d_conv_b, d_conv_w = _ffn_forward_backward(
        xhat1, rstd1, ln1_g, ln1_b, w_up_t, conv_w_f, conv_b, w_down_f, ln2_g, ln2_b, tgt)

    (dw_down,) = _weight_grad(f, dz2b, "grad_w_down", tm=D_FF // 2)
    own_down, oth_down = _pair_reduce([dw_down.reshape(N_DEV, ROWS_DOWN, D)], "pair_reduce_down")
    dw_up_t, arr_down = _weight_grad(du, x1b, "grad_w_up", tm=D_FF // 2, exchange=[oth_down])
    own_up, oth_up = _pair_reduce([dw_up_t.reshape(N_DEV, ROWS_UP, D)], "pair_reduce_up")
    up_sems, up_src, up_land, up_started = _exchange_start(oth_up, "exchange_up_start", CHIP_BARRIER_SPLIT)
    dproj, grad_x, d_w_pool, d_pool_scale, dw_out = _mix_backward(
        dz1, w_out_f, qkv, g, oret, states, pooled, cat, cos, sin, dmat, qd, kd, cdec, w_pool_b, pool_scale, w_in_t,
        after=up_started)
    small = _pack({"w_pool": d_w_pool, "pool_scale": d_pool_scale, "ln1_g": d_ln1_g, "ln1_b": d_ln1_b,
                   "conv_b": d_conv_b, "ln2_g": d_ln2_g, "ln2_b": d_ln2_b, "conv_w": d_conv_w, "loss": loss8[0, :1]})
    own_out, own_small, oth_out, oth_small = _pair_reduce(
        [dw_out.reshape(N_DEV, ROWS_OUT, D), small.reshape(N_DEV, SMALL_ROWS // N_DEV, 128)], "pair_reduce_out")
    dw_in_t, arr_out, arr_small = _weight_grad(dproj, x2, "grad_w_in", tm=IN_W // 2, exchange=[oth_out, oth_small])
    arr_up = _exchange_wait(up_sems, up_src, up_land, [dw_in_t], "exchange_up_wait")
    own_in, oth_in = _pair_reduce([dw_in_t.reshape(N_DEV, ROWS_IN, D)], "pair_reduce_in")
    in_sems, in_src, in_land, started = _exchange_start(oth_in, "exchange_in_start", CHIP_BARRIER)
    (small_piece,) = _sum_parts([own_small], [arr_small], "sum_small_grads")
    (gs_small,) = _all_gather([small_piece], "gather_small_grads")

    names = ["w_in", "w_pool", "pool_scale", "w_out", "ln1_g", "ln1_b", "w_up", "conv_w", "conv_b", "w_down",
             "ln2_g", "ln2_b"]
    w_d = dict(w_in=w_in, w_pool=w_pool, pool_scale=pool_scale, w_out=w_out, ln1_g=ln1_g, ln1_b=ln1_b, w_up=w_up,
               conv_w=conv_w, conv_b=conv_b, w_down=w_down, ln2_g=ln2_g, ln2_b=ln2_b)
    m_d = dict(w_in=m_w_in, w_pool=m_w_pool, pool_scale=m_pool_scale, w_out=m_w_out, ln1_g=m_ln1_g, ln1_b=m_ln1_b,
               w_up=m_w_up, conv_w=m_conv_w, conv_b=m_conv_b, w_down=m_w_down, ln2_g=m_ln2_g, ln2_b=m_ln2_b)
    v_d = dict(w_in=v_w_in, w_pool=v_w_pool, pool_scale=v_pool_scale, w_out=v_w_out, ln1_g=v_ln1_g, ln1_b=v_ln1_b,
               w_up=v_w_up, conv_w=v_conv_w, conv_b=v_conv_b, w_down=v_w_down, ln2_g=v_ln2_g, ln2_b=v_ln2_b)
    g_d, delta, new_m, new_v = {}, {}, {}, {}

    def big_adamw(k, own, arr, transposed, steps, after=()):
        lay = (lambda a: a[0].T) if transposed else (lambda a: a[0])
        back = (lambda a: a.T[None]) if transposed else (lambda a: a[None])
        res = _sum_adamw(own, arr, lay(w_d[k]), lay(m_d[k]), lay(v_d[k]), "adamw_" + k, steps, after)
        g_d[k], delta[k], new_m[k], new_v[k] = (back(r) for r in res)
        return res[3]

    done = [big_adamw("w_up", own_up, arr_up, True, 4, after=(started,)),
            big_adamw("w_down", own_down, arr_down, False, 2, after=(started,)),
            big_adamw("w_out", own_out, arr_out, False, 2, after=(started,))]

    gsm = _unpack(gs_small)
    gsm["conv_w"] = lax.dynamic_slice(gsm["conv_w"].reshape(3, D_FF), (0, me * (D_FF // N_DEV)), (3, D_FF // N_DEV))
    lay = lambda k, a: jnp.transpose(a, (1, 0, 2)) if k == "conv_w" else a.reshape(-1, a.shape[-1])
    back = lambda k, a: jnp.transpose(a, (1, 0, 2)) if k == "conv_w" else a.reshape(w_d[k].shape)
    group = [k for k in names if k not in ("w_in", "w_out", "w_up", "w_down")]
    for k in group:
        g_d[k] = gsm[k].reshape(w_d[k].shape)
    res = _adamw([lay(k, w_d[k]) for k in group], [lay(k, g_d[k]) for k in group], [lay(k, m_d[k]) for k in group],
                 [lay(k, v_d[k]) for k in group], "adamw_small")
    for j, k in enumerate(group):
        delta[k], new_m[k], new_v[k] = (back(k, res[part * len(group) + j]) for part in range(3))

    arr_in = _exchange_wait(in_sems, in_src, in_land, done + [res[0]], "exchange_in_wait")
    big_adamw("w_in", own_in, arr_in, True, 4)

    loss = gsm["loss"].reshape(())
    return (loss, grad_x[None], *[g_d[k] for k in names], *[delta[k] for k in names], *[new_m[k] for k in names],
            *[new_v[k] for k in names])
```

```python
import math

import numpy as np
import jax
import jax.numpy as jnp
from jax import lax
from jax.experimental import pallas as pl
from jax.experimental.pallas import tpu as pltpu

f32 = jnp.float32
bf16 = jnp.bfloat16

N_DEV = 8
T = 4096
D = 1024
CHUNK = 64
MIX_TILE = 512
RET_TILE = 256
HEADS = 4
DH = 128
RW = HEADS * DH
PW = 512
GROUPS = 4
WINDOWS = (2, 4, 8, 16)
IN_W = 4 * RW + PW
D_FF = 2816
LN_EPS = 1e-5
RMS_EPS = 1e-6
ALPHA = 2.0 ** 0.25
K_SCALE = DH ** -0.5

ADAM_LR = 0.001
ADAM_B1 = 0.9
ADAM_B2 = 0.999
ADAM_EPS = 1e-08
ADAM_WD = 0.01
ADAM_STEP = 10

ROWS_IN, ROWS_OUT, ROWS_UP, ROWS_DOWN = IN_W // N_DEV, D // N_DEV, 2 * D_FF // N_DEV, D_FF // N_DEV

V7X_VMEM_LIMIT = 56 * 2 ** 20
HALO = 32

NT = (((1,), (1,)), ((), ()))
TN = (((0,), (0,)), ((), ()))
NN = (((1,), (0,)), ((), ()))


def _dot(a, b, dims=NN):
    return lax.dot_general(a, b, dims, preferred_element_type=f32)


def _const_spec(shape):
    zeros = (0,) * len(shape)
    return pl.BlockSpec(shape, lambda i: zeros, pipeline_mode=pl.Buffered(1))


def _sigmoid(x):
    return 0.5 * jnp.tanh(0.5 * x) + 0.5


def _decay_tables(tt):
    h = np.arange(HEADS, dtype=np.float64)
    log_gamma = np.log(1.0 - 2.0 ** (-5.0 - h)).astype(np.float32).astype(np.float64)[:, None, None]
    idx = np.arange(tt, dtype=np.float64)
    visible = (idx[None, :] // CHUNK) <= (idx[:, None] // CHUNK)
    mask = np.where(visible[None], np.exp(log_gamma * np.abs(idx[:, None] - idx[None, :])[None]), 0.0)
    qd = np.broadcast_to(np.exp(log_gamma * (idx[None, :, None] + 1.0)), (HEADS, tt, DH))
    kd = np.broadcast_to(np.exp(log_gamma * (tt - 1.0 - idx[None, :, None])), (HEADS, tt, DH))
    cd = np.exp(log_gamma[:, 0, 0] * tt)
    return (jnp.asarray(mask, f32), jnp.asarray(qd, f32), jnp.asarray(kd, f32), [float(c) for c in cd])


def _rope_tables():
    inv_freq = (10000.0 ** (-np.arange(0, DH, 2, dtype=np.float64) / DH)).astype(np.float32)
    ang = (np.arange(T, dtype=np.float32)[:, None] * inv_freq[None, :]).astype(np.float64)
    cos, sin = np.cos(ang), np.sin(ang)
    return (jnp.asarray(np.concatenate([cos, cos], axis=1), f32), jnp.asarray(np.concatenate([-sin, sin], axis=1), f32))


def _swap_halves(t):
    return pltpu.roll(t, DH // 2, axis=1)


def _mix_forward(x, w_in_shard, w_out_shard, cos, sin, dmat, qd, kd, cdec, w_pool, pool_scale, ln1_g, ln1_b, gather,
                 tt=MIX_TILE):
    n_tiles = T // tt
    n_g = len(gather)

    def body(x_ref, win_ref, wout_ref, cos_ref, sin_ref, dmat_ref, qd_ref, kd_ref, wpool_ref, pscale_ref,
             g1_ref, b1_ref, *rest):
        gin, rest = rest[:n_g], rest[n_g:]
        qkv_ref, g_ref, oret_ref, states_ref, cat_ref, pooled_ref, xhat_ref, rstd_ref, x1b_ref, xb_ref = rest[:10]
        fout, gout = rest[10:12], rest[12:12 + n_g]
        state_s, pext_s, tmp_s, wint_s, wout_s, load_sems, *sems = rest[12 + n_g:]
        fin, fsems, gsems = (win_ref, wout_ref), sems[:3], sems[3:]
        i = pl.program_id(0)

        @pl.when(i == 0)
        def _():
            state_s[...] = jnp.zeros_like(state_s)
            pext_s[:, pl.ds(0, HALO), :] = jnp.zeros((GROUPS, HALO, DH), f32)
            _barrier(_gather_peers())
            _gather_start(fin, fout, *fsems, barrier=False)
            _gather_forward(fin, fout, *fsems)
            _gather_start(gin, gout, *gsems, barrier=False)
            _gather_finish(fin, fout, *fsems)
            loads = [pltpu.make_async_copy(src.at[s], dst.at[pl.ds(s * src.shape[1], src.shape[1]), :],
                                           load_sems.at[j, s])
                     for j, (src, dst) in enumerate(((fout[0], wint_s), (fout[1], wout_s))) for s in range(N_DEV)]
            for ld in loads:
                ld.start()
            for ld in loads:
                ld.wait()

        @pl.when(i == n_tiles - 3)
        def _():
            _gather_forward(gin, gout, *gsems)

        xb = x_ref[...].astype(bf16)
        xb_ref[...] = xb
        cos_t, sin_t = cos_ref[...], sin_ref[...]
        for part in range(2):
            pr = _dot(xb, wint_s[pl.ds(part * RW, RW), :], NT)
            for h in range(HEADS):
                t = pr[:, h * DH:(h + 1) * DH]
                r = t * cos_t + _swap_halves(t) * sin_t
                if part == 1:
                    r = r * K_SCALE
                qkv_ref[:, part * RW + h * DH: part * RW + (h + 1) * DH] = r.astype(bf16)
        qkv_ref[:, 2 * RW:3 * RW] = _dot(xb, wint_s[pl.ds(2 * RW, RW), :], NT).astype(bf16)
        g_ref[...] = _dot(xb, wint_s[pl.ds(3 * RW, RW), :], NT)
        p = _dot(xb, wint_s[pl.ds(4 * RW, PW), :], NT)
        for gi in range(GROUPS):
            pext_s[gi, pl.ds(HALO, tt), :] = p[:, gi * DH:(gi + 1) * DH]

        for sub in range(tt // RET_TILE):
            rows = pl.ds(sub * RET_TILE, RET_TILE)
            for h in range(HEADS):
                q = qkv_ref[rows, h * DH:(h + 1) * DH]
                k = qkv_ref[rows, RW + h * DH: RW + (h + 1) * DH]
                v = qkv_ref[rows, 2 * RW + h * DH: 2 * RW + (h + 1) * DH]
                s = _dot(q, k, NT) * dmat_ref[h]
                st = state_s[h]
                stb = st.astype(bf16)
                states_ref[sub, h] = stb
                oret_ref[rows, h * DH:(h + 1) * DH] = (_dot(s.astype(bf16), v)
                                                      + _dot((q.astype(f32) * qd_ref[h]).astype(bf16), stb))
                state_s[h] = st * cdec[h] + _dot((k.astype(f32) * kd_ref[h]).astype(bf16), v, TN)

        for h in range(HEADS):
            sl = slice(h * DH, (h + 1) * DH)
            o = oret_ref[:, sl]
            r = lax.rsqrt(jnp.mean(o * o, axis=-1, keepdims=True) + RMS_EPS)
            gg = g_ref[:, sl]
            cat_ref[:, sl] = (o * r * (gg * _sigmoid(gg))).astype(bf16)

        pos1 = (i * tt + lax.broadcasted_iota(jnp.int32, (tt, 1), 0) + 1).astype(f32)
        for gi, w in enumerate(WINDOWS):
            sl = slice(gi * DH, (gi + 1) * DH)
            stages = int(math.log2(w))
            src = pext_s
            for s in range(stages):
                lo = HALO - 8 * (stages - 1 - s)
                n = tt + HALO - lo
                shift = 2 ** s
                val = src[gi, pl.ds(lo, n), :] + src[gi, pl.ds(lo - shift, n), :]
                if s == stages - 1:
                    wsum = val
                else:
                    tmp_s[gi, pl.ds(lo, n), :] = val
                    src = tmp_s
            p_g = pext_s[gi, pl.ds(HALO, tt), :]
            pooled = (wsum / jnp.minimum(pos1, float(w)) - p_g).astype(bf16)
            pooled_ref[:, sl] = pooled
            y = _dot(pooled, wpool_ref[gi]) * pscale_ref[:, sl]
            cat_ref[:, RW + gi * DH: RW + (gi + 1) * DH] = y.astype(bf16)
        pext_s[:, pl.ds(0, HALO), :] = pext_s[:, pl.ds(tt, HALO), :]

        z = ALPHA * x_ref[...] + _dot(cat_ref[...], wout_s[...])
        mu = jnp.mean(z, axis=-1, keepdims=True)
        zc = z - mu
        rstd = lax.rsqrt(jnp.mean(zc * zc, axis=-1, keepdims=True) + LN_EPS)
        xhat = zc * rstd
        xhat_ref[...] = xhat
        rstd_ref[...] = rstd
        x1b_ref[...] = (xhat * g1_ref[...] + b1_ref[...]).astype(bf16)

        @pl.when(i == n_tiles - 1)
        def _():
            _gather_finish(gin, gout, *gsems)

    tile = lambda w: pl.BlockSpec((tt, w), lambda i: (i, 0))
    hbm = pl.BlockSpec(memory_space=pltpu.HBM)
    out_shape = (
        jax.ShapeDtypeStruct((T, 3 * RW), bf16),
        jax.ShapeDtypeStruct((T, RW), f32),
        jax.ShapeDtypeStruct((T, RW), f32),
        jax.ShapeDtypeStruct((T // RET_TILE, HEADS, DH, DH), bf16),
        jax.ShapeDtypeStruct((T, D), bf16),
        jax.ShapeDtypeStruct((T, PW), bf16),
        jax.ShapeDtypeStruct((T, D), f32),
        jax.ShapeDtypeStruct((T, 1), f32),
        jax.ShapeDtypeStruct((T, D), bf16),
        jax.ShapeDtypeStruct((T, D), bf16),
    ) + tuple(jax.ShapeDtypeStruct((N_DEV,) + b.shape, b.dtype) for b in [w_in_shard, w_out_shard] + list(gather))
    return pl.pallas_call(
        body, name="mix_forward", grid=(n_tiles,), out_shape=out_shape,
        in_specs=[tile(D), hbm, hbm, tile(DH), tile(DH),
                  _const_spec((HEADS, RET_TILE, RET_TILE)), _const_spec((HEADS, RET_TILE, DH)),
                  _const_spec((HEADS, RET_TILE, DH)),
                  _const_spec((GROUPS, DH, DH)), _const_spec((1, PW)),
                  _const_spec((1, D)), _const_spec((1, D))] + [hbm] * n_g,
        out_specs=(tile(3 * RW), tile(RW), tile(RW),
                   pl.BlockSpec((tt // RET_TILE, HEADS, DH, DH), lambda i: (i, 0, 0, 0)),
                   tile(D), tile(PW), tile(D), tile(1), tile(D), tile(D)) + (hbm,) * (2 + n_g),
        scratch_shapes=[pltpu.VMEM((HEADS, DH, DH), f32), pltpu.VMEM((GROUPS, tt + HALO, DH), f32),
                        pltpu.VMEM((GROUPS, tt + HALO, DH), f32), pltpu.VMEM((IN_W, D), bf16), pltpu.VMEM((D, D), bf16),
                        pltpu.SemaphoreType.DMA((2, N_DEV))] + _gather_sems(2) + _gather_sems(n_g),
        compiler_params=pltpu.CompilerParams(dimension_semantics=("arbitrary",), vmem_limit_bytes=V7X_VMEM_LIMIT,
                                             collective_id=GATHER_BARRIER),
    )(x, w_in_shard, w_out_shard, cos, sin, dmat, qd, kd, w_pool, pool_scale, ln1_g, ln1_b, *gather)


def _ffn_forward_backward(xhat1, rstd1, ln1_g, ln1_b, w_up_t, conv_w, conv_b, w_down, ln2_g, ln2_b, target,
                          tt=256):
    n_tiles = T // tt
    FH = 16
    hb = tt // FH

    def body(xhat_ref, halo_ref, rstd_ref, g1_ref, b1_ref, wupt_ref, cw_ref, cb_ref, wdown_ref, g2_ref, b2_ref, tgt_ref,
             dz1_ref, dz2b_ref, du_ref, f_ref, loss_ref, dg2_ref, db2_ref, dg1_ref, db1_ref, dcb_ref, dcw_ref,
             gext_s, val_s, dhext_s):
        i = pl.program_id(0)
        tile_idx = n_tiles - 1 - i

        def rd(ref, off):
            return jnp.concatenate([ref[k, pl.ds(off, tt), :] for k in range(D_FF // 128)], axis=1)

        def wr(ref, val):
            for k in range(D_FF // 128):
                ref[k, pl.ds(0, val.shape[0]), :] = val[:, k * 128:(k + 1) * 128]

        @pl.when(i == 0)
        def _():
            for r in (loss_ref, dg2_ref, db2_ref, dg1_ref, db1_ref, dcb_ref, dcw_ref):
                r[...] = jnp.zeros_like(r)
            dhext_s[:, pl.ds(tt, 8), :] = jnp.zeros((D_FF // 128, 8, 128), f32)

        g1, b1 = g1_ref[...], b1_ref[...]
        xhat = xhat_ref[...]
        x1 = xhat * g1 + b1
        x1b = x1.astype(bf16)
        x1h = ((halo_ref[...] * g1 + b1) * jnp.where(tile_idx == 0, 0.0, 1.0)).astype(bf16)
        x1ext = jnp.concatenate([x1h, x1b], axis=0)

        val = _dot(x1b, wupt_ref[pl.ds(0, D_FF), :], NT)
        gate_ext = _dot(x1ext, wupt_ref[pl.ds(D_FF, D_FF), :], NT)
        wr(gext_s, gate_ext)
        hh = (cb_ref[...] + cw_ref[0:1, :] * rd(gext_s, FH - 2) + cw_ref[1:2, :] * rd(gext_s, FH - 1)
              + cw_ref[2:3, :] * gate_ext[FH:])
        sg = _sigmoid(hh)
        act = hh * sg
        wr(dhext_s, act)
        val_s[...] = val * (sg + act * (1.0 - sg))
        fb = (act * val).astype(bf16)
        f_ref[...] = fb

        z = ALPHA * x1 + _dot(fb, wdown_ref[...])
        mu = jnp.mean(z, axis=-1, keepdims=True)
        zc = z - mu
        rstd2 = lax.rsqrt(jnp.mean(zc * zc, axis=-1, keepdims=True) + LN_EPS)
        xh2 = zc * rstd2
        diff = xh2 * g2_ref[...] + b2_ref[...] - tgt_ref[...]
        loss_ref[...] += 0.5 * jnp.sum(diff * diff) / D
        dy = diff * (1.0 / D)
        dg2_ref[...] += jnp.sum(dy * xh2, axis=0, keepdims=True)
        db2_ref[...] += jnp.sum(dy, axis=0, keepdims=True)
        dyg = dy * g2_ref[...]
        dz2 = rstd2 * (dyg - jnp.mean(dyg, axis=-1, keepdims=True) - xh2 * jnp.mean(dyg * xh2, axis=-1, keepdims=True))
        dz2b = dz2.astype(bf16)
        dz2b_ref[...] = dz2b

        df = _dot(dz2b, wdown_ref[...], NT)
        dval = df * rd(dhext_s, 0)
        dh = df * val_s[...]
        wr(dhext_s, dh)
        dh1, dh2, g0 = rd(dhext_s, 1), rd(dhext_s, 2), rd(gext_s, FH)
        dcb_ref[...] += jnp.sum(dh, axis=0, keepdims=True)
        dcw_ref[0:1, :] += jnp.sum(dh2 * g0, axis=0, keepdims=True)
        dcw_ref[1:2, :] += jnp.sum(dh1 * g0, axis=0, keepdims=True)
        dcw_ref[2:3, :] += jnp.sum(dh * g0, axis=0, keepdims=True)
        dgate = cw_ref[2:3, :] * dh + cw_ref[1:2, :] * dh1 + cw_ref[0:1, :] * dh2
        dvalb, dgateb = dval.astype(bf16), dgate.astype(bf16)
        du_ref[:, :D_FF] = dvalb
        du_ref[:, D_FF:] = dgateb
        dx1 = ALPHA * dz2 + _dot(dvalb, wupt_ref[pl.ds(0, D_FF), :]) + _dot(dgateb, wupt_ref[pl.ds(D_FF, D_FF), :])
        dhext_s[:, pl.ds(tt, 8), :] = dhext_s[:, pl.ds(0, 8), :]

        dg1_ref[...] += jnp.sum(dx1 * xhat, axis=0, keepdims=True)
        db1_ref[...] += jnp.sum(dx1, axis=0, keepdims=True)
        dxg = dx1 * g1
        dz1_ref[...] = rstd_ref[...] * (dxg - jnp.mean(dxg, axis=-1, keepdims=True)
                                        - xhat * jnp.mean(dxg * xhat, axis=-1, keepdims=True))

    rtile = lambda w: pl.BlockSpec((tt, w), lambda i: (n_tiles - 1 - i, 0))
    acc = lambda shape: pl.BlockSpec(shape, lambda i: (0, 0))
    out_shape = (
        jax.ShapeDtypeStruct((T, D), f32),
        jax.ShapeDtypeStruct((T, D), bf16),
        jax.ShapeDtypeStruct((T, 2 * D_FF), bf16),
        jax.ShapeDtypeStruct((T, D_FF), bf16),
        jax.ShapeDtypeStruct((8, 128), f32),
        jax.ShapeDtypeStruct((1, D), f32), jax.ShapeDtypeStruct((1, D), f32),
        jax.ShapeDtypeStruct((1, D), f32), jax.ShapeDtypeStruct((1, D), f32),
        jax.ShapeDtypeStruct((1, D_FF), f32), jax.ShapeDtypeStruct((3, D_FF), f32),
    )
    return pl.pallas_call(
        body, name="ffn_forward_backward", grid=(n_tiles,), out_shape=out_shape,
        in_specs=[rtile(D),
                  pl.BlockSpec((FH, D), lambda i: (jnp.maximum((n_tiles - 1 - i) * hb - 1, 0), 0)),
                  rtile(1), _const_spec((1, D)), _const_spec((1, D)), _const_spec((2 * D_FF, D)),
                  _const_spec((3, D_FF)), _const_spec((1, D_FF)), _const_spec((D_FF, D)),
                  _const_spec((1, D)), _const_spec((1, D)), rtile(D)],
        out_specs=(rtile(D), rtile(D), rtile(2 * D_FF), rtile(D_FF), acc((8, 128)),
                   acc((1, D)), acc((1, D)), acc((1, D)), acc((1, D)), acc((1, D_FF)), acc((3, D_FF))),
        scratch_shapes=[pltpu.VMEM((D_FF // 128, tt + FH, 128), f32), pltpu.VMEM((tt, D_FF), f32),
                        pltpu.VMEM((D_FF // 128, tt + 8, 128), f32)],
        compiler_params=pltpu.CompilerParams(dimension_semantics=("arbitrary",), vmem_limit_bytes=V7X_VMEM_LIMIT),
    )(xhat1, xhat1, rstd1, ln1_g, ln1_b, w_up_t, conv_w, conv_b, w_down, ln2_g, ln2_b, target)


def _mix_backward(dz1, w_out, qkv, g, oret, states, pooled, cat, cos, sin, dmat, qd, kd, cdec, w_pool, pool_scale, w_in_t,
                  after, tt=MIX_TILE):
    n_tiles = T // tt

    def body(dz1_ref, wout_ref, qkv_ref, g_ref, oret_ref, states_ref, pooled_ref, cat_ref, cos_ref, sin_ref, dmat_ref,
             qd_ref, kd_ref, wpool_ref, pscale_ref, wint_ref, after_ref,
             dproj_ref, gx_ref, dwpool_ref, dpscale_ref, dwout_ref, dstate_s, dout_s, eext_s, tmp_s, dwout_s):
        i = pl.program_id(0)
        tile_idx = n_tiles - 1 - i

        @pl.when(i == 0)
        def _():
            dstate_s[...] = jnp.zeros_like(dstate_s)
            dwpool_ref[...] = jnp.zeros_like(dwpool_ref)
            dpscale_ref[...] = jnp.zeros_like(dpscale_ref)
            dwout_s[...] = jnp.zeros_like(dwout_s)
            eext_s[:, pl.ds(tt, HALO), :] = jnp.zeros((GROUPS, HALO, DH), f32)

        dz1 = dz1_ref[...]
        dz1b = dz1.astype(bf16)
        dcat = _dot(dz1b, wout_ref[...], NT)
        dwout_s[...] += _dot(cat_ref[...], dz1b, TN)

        pos1 = (tile_idx * tt + lax.broadcasted_iota(jnp.int32, (tt, 1), 0) + 1).astype(f32)
        for gi, w in enumerate(WINDOWS):
            sl = slice(gi * DH, (gi + 1) * DH)
            dpo = dcat[:, RW + gi * DH: RW + (gi + 1) * DH]
            pooled_g = pooled_ref[:, sl]
            ylin = _dot(pooled_g, wpool_ref[gi])
            dpscale_ref[:, sl] += jnp.sum(dpo * ylin, axis=0, keepdims=True)
            dpw = (dpo * pscale_ref[:, sl]).astype(bf16)
            dwpool_ref[gi] += _dot(pooled_g, dpw, TN)
            dpooled = _dot(dpw, wpool_ref[gi], NT)
            eext_s[gi, pl.ds(0, tt), :] = dpooled / jnp.minimum(pos1, float(w))
            stages = int(math.log2(w))
            src = eext_s
            for s in range(stages):
                n = tt + 8 * (stages - 1 - s)
                shift = 2 ** s
                val = src[gi, pl.ds(0, n), :] + src[gi, pl.ds(shift, n), :]
                if s == stages - 1:
                    wsum = val
                else:
                    tmp_s[gi, pl.ds(0, n), :] = val
                    src = tmp_s
            dproj_ref[:, 4 * RW + gi * DH: 4 * RW + (gi + 1) * DH] = (wsum - dpooled).astype(bf16)
        eext_s[:, pl.ds(tt, HALO), :] = eext_s[:, pl.ds(0, HALO), :]

        for h in range(HEADS):
            sl = slice(h * DH, (h + 1) * DH)
            dr = dcat[:, sl]
            o = oret_ref[:, sl]
            r = lax.rsqrt(jnp.mean(o * o, axis=-1, keepdims=True) + RMS_EPS)
            rn = o * r
            gg = g_ref[:, sl]
            sg = _sigmoid(gg)
            dproj_ref[:, 3 * RW + h * DH: 3 * RW + (h + 1) * DH] = (dr * rn * (sg * (1.0 + gg * (1.0 - sg)))).astype(bf16)
            drn = dr * (gg * sg)
            dout_s[:, sl] = (r * (drn - rn * jnp.mean(drn * rn, axis=-1, keepdims=True))).astype(bf16)

        for sub in reversed(range(tt // RET_TILE)):
            rows = pl.ds(sub * RET_TILE, RET_TILE)
            cos_t, sin_t = cos_ref[rows, :], sin_ref[rows, :]
            for h in range(HEADS):
                q = qkv_ref[rows, h * DH:(h + 1) * DH]
                k = qkv_ref[rows, RW + h * DH: RW + (h + 1) * DH]
                v = qkv_ref[rows, 2 * RW + h * DH: 2 * RW + (h + 1) * DH]
                do = dout_s[rows, h * DH:(h + 1) * DH]
                stb = states_ref[sub, h]
                dst = dstate_s[h]
                dstb = dst.astype(bf16)
                sb = (_dot(q, k, NT) * dmat_ref[h]).astype(bf16)
                dsb = (_dot(do, v, NT) * dmat_ref[h]).astype(bf16)
                dq = _dot(dsb, k) + _dot(do, stb, NT) * qd_ref[h]
                dk = _dot(dsb, q, TN) + _dot(v, dstb, NT) * kd_ref[h]
                dv = _dot(sb, do, TN) + _dot((k.astype(f32) * kd_ref[h]).astype(bf16), dstb)
                dstate_s[h] = dst * cdec[h] + _dot((q.astype(f32) * qd_ref[h]).astype(bf16), do, TN)
                dproj_ref[rows, h * DH:(h + 1) * DH] = (dq * cos_t - _swap_halves(dq) * sin_t).astype(bf16)
                dproj_ref[rows, RW + h * DH: RW + (h + 1) * DH] = (
                    (dk * cos_t - _swap_halves(dk) * sin_t) * K_SCALE).astype(bf16)
                dproj_ref[rows, 2 * RW + h * DH: 2 * RW + (h + 1) * DH] = dv.astype(bf16)

        gx_ref[...] = ALPHA * dz1 + _dot(dproj_ref[...], wint_ref[...])

        @pl.when(i == n_tiles - 1)
        def _():
            dwout_ref[...] = dwout_s[...].astype(bf16)

    rtile = lambda w: pl.BlockSpec((tt, w), lambda i: (n_tiles - 1 - i, 0))
    out_shape = (
        jax.ShapeDtypeStruct((T, IN_W), bf16),
        jax.ShapeDtypeStruct((T, D), f32),
        jax.ShapeDtypeStruct((GROUPS, DH, DH), f32),
        jax.ShapeDtypeStruct((1, PW), f32),
        jax.ShapeDtypeStruct((D, D), bf16),
    )
    return pl.pallas_call(
        body, name="mix_backward", grid=(n_tiles,), out_shape=out_shape,
        in_specs=[rtile(D), _const_spec((D, D)), rtile(3 * RW), rtile(RW), rtile(RW),
                  pl.BlockSpec((tt // RET_TILE, HEADS, DH, DH), lambda i: (n_tiles - 1 - i, 0, 0, 0)),
                  rtile(PW), rtile(D), rtile(DH), rtile(DH),
                  _const_spec((HEADS, RET_TILE, RET_TILE)), _const_spec((HEADS, RET_TILE, DH)),
                  _const_spec((HEADS, RET_TILE, DH)),
                  _const_spec((GROUPS, DH, DH)), _const_spec((1, PW)), _const_spec((IN_W, D)),
                  pl.BlockSpec(memory_space=pl.ANY)],
        out_specs=(rtile(IN_W), rtile(D), pl.BlockSpec((GROUPS, DH, DH), lambda i: (0, 0, 0)),
                   pl.BlockSpec((1, PW), lambda i: (0, 0)),
                   pl.BlockSpec((D, D), lambda i: (0, 0), pipeline_mode=pl.Buffered(1))),
        scratch_shapes=[pltpu.VMEM((HEADS, DH, DH), f32), pltpu.VMEM((tt, RW), bf16),
                        pltpu.VMEM((GROUPS, tt + HALO, DH), f32), pltpu.VMEM((GROUPS, tt + HALO, DH), f32),
                        pltpu.VMEM((D, D), f32)],
        compiler_params=pltpu.CompilerParams(dimension_semantics=("arbitrary",), vmem_limit_bytes=V7X_VMEM_LIMIT),
    )(dz1, w_out, qkv, g, oret, states, pooled, cat, cos, sin, dmat, qd, kd, w_pool, pool_scale, w_in_t, after)


def _weight_grad(a, b, name, tm, exchange=()):
    m = a.shape[1]
    n_m, n_e = m // tm, len(exchange)

    def body(a_ref, b_ref, *rest):
        ein, o_ref, eout, sems = rest[:n_e], rest[n_e], rest[n_e + 1:2 * n_e + 1], rest[2 * n_e + 1:]
        i = pl.program_id(0)

        if n_e:
            @pl.when(i == 0)
            def _():
                _chip_exchange_start(ein, eout, *sems)

        o_ref[...] = _dot(a_ref[...], b_ref[...].astype(bf16), TN).astype(bf16)

        if n_e:
            @pl.when(i == n_m - 1)
            def _():
                _chip_exchange_finish(ein, eout, *sems)

    hbm = pl.BlockSpec(memory_space=pltpu.HBM)
    return pl.pallas_call(
        body, name=name, grid=(n_m,),
        out_shape=(jax.ShapeDtypeStruct((m, D), bf16),) + tuple(jax.ShapeDtypeStruct(e.shape, e.dtype) for e in exchange),
        in_specs=[pl.BlockSpec((T, tm), lambda i: (0, i)),
                  pl.BlockSpec((T, D), lambda i: (0, 0), pipeline_mode=pl.Buffered(1))] + [hbm] * n_e,
        out_specs=(pl.BlockSpec((tm, D), lambda i: (i, 0)),) + (hbm,) * n_e,
        scratch_shapes=_chip_exchange_sems(n_e),
        compiler_params=pltpu.CompilerParams(dimension_semantics=("arbitrary",), vmem_limit_bytes=V7X_VMEM_LIMIT,
                                             collective_id=CHIP_BARRIER if n_e else None),
    )(a, b, *exchange)


CHIP_FLIPS = ((1, 0), (0, 1), (1, 1))
PAIR_BARRIER, CHIP_BARRIER, GATHER_BARRIER, CHIP_BARRIER_SPLIT = 0, 1, 2, 3


def _barrier(peers):
    sem = pltpu.get_barrier_semaphore()
    for peer in peers:
        pl.semaphore_signal(sem, inc=1, device_id=peer, device_id_type=pl.DeviceIdType.MESH)
    pl.semaphore_wait(sem, len(peers))


def _me():
    return lax.axis_index("x"), lax.axis_index("y"), lax.axis_index("c")


def _chip(me, k):
    x, y, _ = me
    if k == 0:
        return x, y
    fx, fy = CHIP_FLIPS[k - 1]
    return (1 - x if fx else x), (1 - y if fy else y)


def _slot(x, y, c):
    return 4 * x + 2 * y + c


def _remote(src, dst, send_sem, recv_sem, to):
    return pltpu.make_async_remote_copy(src_ref=src, dst_ref=dst, send_sem=send_sem, recv_sem=recv_sem,
                                        device_id=to, device_id_type=pl.DeviceIdType.MESH)


def _gather_sems(n):
    return [pltpu.SemaphoreType.DMA((7, n)), pltpu.SemaphoreType.DMA((7, n)), pltpu.SemaphoreType.DMA((n,))] if n else []


def _gather_copy(k, j, gin, gout, send_sems, recv_sems, sending):
    x, y, c = _me()
    sibling, x_chip, y_chip, d_chip = (x, y, 1 - c), (1 - x, y), (x, 1 - y), (1 - x, 1 - y)
    south = c == 0
    passed_on = (jnp.where(south, 1 - x, x), jnp.where(south, y, 1 - y), c)
    src, to = gin[j], sibling
    if sending:
        block = {0: (x, y, c), 1: (x, y, c), 2: (x, y, c), 3: passed_on, 4: (*x_chip, c), 5: (*y_chip, c), 6: (*d_chip, c)}[k]
        to = {1: (*x_chip, c), 2: (*y_chip, c), 3: (jnp.where(south, x, 1 - x), jnp.where(south, 1 - y, y), c)}.get(k, sibling)
        if k >= 3:
            src = gout[j].at[_slot(*block)]
    else:
        block = {0: sibling, 1: (*x_chip, c), 2: (*y_chip, c), 3: (*d_chip, c), 4: (*x_chip, 1 - c), 5: (*y_chip, 1 - c),
                 6: (*d_chip, 1 - c)}[k]
    return _remote(src, gout[j].at[_slot(*block)], send_sems.at[k, j], recv_sems.at[k, j], to)


def _gather_do(ks, action, gin, gout, send_sems, recv_sems):
    for k in ks:
        for j in range(len(gin)):
            cp = _gather_copy(k, j, gin, gout, send_sems, recv_sems, action != "wait_recv")
            getattr(cp, action)()


def _gather_peers():
    x, y, c = _me()
    return [(x, y, 1 - c), (1 - x, y, c), (x, 1 - y, c)]


def _gather_start(gin, gout, send_sems, recv_sems, local_sems, barrier=True):
    if barrier:
        _barrier(_gather_peers())
    for j in range(len(gin)):
        pltpu.make_async_copy(gin[j], gout[j].at[_slot(*_me())], local_sems.at[j]).start()
    _gather_do((0, 1, 2), "start", gin, gout, send_sems, recv_sems)


def _gather_forward(gin, gout, send_sems, recv_sems, local_sems):
    _gather_do((1, 2), "wait_recv", gin, gout, send_sems, recv_sems)
    _gather_do((3, 4, 5), "start", gin, gout, send_sems, recv_sems)


def _gather_finish(gin, gout, send_sems, recv_sems, local_sems):
    _gather_do((3,), "wait_recv", gin, gout, send_sems, recv_sems)
    _gather_do((6,), "start", gin, gout, send_sems, recv_sems)
    _gather_do((0, 4, 5, 6), "wait_recv", gin, gout, send_sems, recv_sems)
    _gather_do(range(7), "wait_send", gin, gout, send_sems, recv_sems)
    for j in range(len(gin)):
        pltpu.make_async_copy(gin[j], gout[j].at[_slot(*_me())], local_sems.at[j]).wait()


def _all_gather(blocks, name):
    n = len(blocks)

    def body(*refs):
        gin, gout, sems = refs[:n], refs[n:2 * n], refs[2 * n:]
        _gather_start(gin, gout, *sems)
        _gather_forward(gin, gout, *sems)
        _gather_finish(gin, gout, *sems)

    hbm = pl.BlockSpec(memory_space=pltpu.HBM)
    return pl.pallas_call(
        body, name=name,
        out_shape=tuple(jax.ShapeDtypeStruct((N_DEV,) + b.shape, b.dtype) for b in blocks),
        in_specs=[hbm] * n, out_specs=(hbm,) * n, scratch_shapes=_gather_sems(n),
        compiler_params=pltpu.CompilerParams(collective_id=GATHER_BARRIER),
    )(*blocks)


def _pair_reduce(parts, name):
    n = len(parts)

    def body(*refs):
        ins, own, others, landing, mine = (refs[k * n:(k + 1) * n] for k in range(5))
        send_sems, recv_sems, local_sems = refs[5 * n:]
        me = _me()
        x, y, c = me
        sibling = (x, y, 1 - c)
        _barrier([sibling])
        sends, loads = [], []
        for k in range(4):
            for j in range(n):
                cp = _remote(ins[j].at[_slot(*_chip(me, k), 1 - c)], landing[j].at[k], send_sems.at[k, j],
                             recv_sems.at[k, j], sibling)
                cp.start()
                sends.append(cp)
                ld = pltpu.make_async_copy(ins[j].at[_slot(*_chip(me, k), c)], mine[j].at[k], local_sems.at[k, j])
                ld.start()
                loads.append(ld)
        for k in range(4):
            for j in range(n):
                loads[k * n + j].wait()
                _remote(ins[j].at[0], landing[j].at[k], send_sems.at[k, j], recv_sems.at[k, j], sibling).wait_recv()
                total = mine[j][k].astype(f32) + landing[j][k].astype(f32)
                if k == 0:
                    own[j][...] = total.astype(own[j].dtype)
                else:
                    others[j][k - 1] = total.astype(others[j].dtype)
        for cp in sends:
            cp.wait_send()

    vm = pl.BlockSpec(memory_space=pltpu.VMEM)
    return pl.pallas_call(
        body, name=name,
        out_shape=tuple(jax.ShapeDtypeStruct(p.shape[1:], p.dtype) for p in parts)
        + tuple(jax.ShapeDtypeStruct((3,) + p.shape[1:], p.dtype) for p in parts),
        in_specs=[pl.BlockSpec(memory_space=pltpu.HBM)] * n, out_specs=(vm,) * (2 * n),
        scratch_shapes=[pltpu.VMEM((4,) + p.shape[1:], p.dtype) for p in parts] * 2
        + [pltpu.SemaphoreType.DMA((4, n)), pltpu.SemaphoreType.DMA((4, n)), pltpu.SemaphoreType.DMA((4, n))],
        compiler_params=pltpu.CompilerParams(vmem_limit_bytes=V7X_VMEM_LIMIT, collective_id=PAIR_BARRIER),
    )(*parts)


def _chip_exchange_sems(n):
    return [pltpu.SemaphoreType.DMA((3, n)), pltpu.SemaphoreType.DMA((3, n))] if n else []


def _chip_exchange_copy(k, j, ein, eout, send_sems, recv_sems):
    me = _me()
    return _remote(ein[j].at[k - 1], eout[j].at[k - 1], send_sems.at[k - 1, j], recv_sems.at[k - 1, j],
                   (*_chip(me, k), me[2]))


def _chip_peers():
    me = _me()
    return [(*_chip(me, k), me[2]) for k in range(1, 4)]


def _chip_exchange_start(ein, eout, send_sems, recv_sems, barrier=True):
    if barrier:
        _barrier(_chip_peers())
    for k in range(1, 4):
        for j in range(len(ein)):
            _chip_exchange_copy(k, j, ein, eout, send_sems, recv_sems).start()


def _chip_exchange_finish(ein, eout, send_sems, recv_sems):
    for k in range(1, 4):
        for j in range(len(ein)):
            _chip_exchange_copy(k, j, ein, eout, send_sems, recv_sems).wait_recv()
    for k in range(1, 4):
        for j in range(len(ein)):
            _chip_exchange_copy(k, j, ein, eout, send_sems, recv_sems).wait_send()


def _split_copies(src_ref, dst_ref, sems):
    me = _me()
    return [_remote(src_ref.at[k - 1], dst_ref.at[k - 1], sems[k - 1], sems[2 + k], (*_chip(me, k), me[2]))
            for k in range(1, 4)]


def _exchange_start(others, name, barrier_id):
    def body(src_ref, land_ref, *rest):
        sems, token_ref = rest[:6], rest[8]
        _barrier(_chip_peers())
        for copy in _split_copies(src_ref, land_ref, sems):
            copy.start()
        token_ref[...] = jnp.zeros_like(token_ref)

    hbm, sem = pl.BlockSpec(memory_space=pltpu.HBM), pl.BlockSpec(memory_space=pltpu.SEMAPHORE)
    thru = pltpu.HBM(others.shape, others.dtype)
    res = pl.pallas_call(
        body, name=name,
        out_shape=(pltpu.SemaphoreType.DMA(()),) * 6 + (thru, thru, jax.ShapeDtypeStruct((8, 128), f32)),
        in_specs=(hbm, hbm), out_specs=(sem,) * 6 + (hbm, hbm, pl.BlockSpec(memory_space=pltpu.VMEM)),
        input_output_aliases={0: 6, 1: 7},
        compiler_params=pltpu.CompilerParams(has_side_effects=pltpu.SideEffectType.DATAFLOW_SIDE_EFFECTING,
                                             collective_id=barrier_id),
    )(pltpu.with_memory_space_constraint(others, pltpu.HBM),
      pltpu.with_memory_space_constraint(lax.empty(others.shape, others.dtype), pltpu.HBM))
    return res[:6], res[6], res[7], res[8]


def _exchange_wait(sems, src_thru, land_thru, after, name):
    n_after = len(after)

    def body(src_ref, land_ref, *rest):
        for copy in _split_copies(src_ref, land_ref, rest[:6]):
            copy.wait_send()
            copy.wait_recv()

    hbm, sem = pl.BlockSpec(memory_space=pltpu.HBM), pl.BlockSpec(memory_space=pltpu.SEMAPHORE)
    thru = pltpu.HBM(src_thru.shape, src_thru.dtype)
    return pl.pallas_call(
        body, name=name, out_shape=(thru, thru),
        in_specs=(hbm, hbm) + (sem,) * 6 + (pl.BlockSpec(memory_space=pl.ANY),) * n_after, out_specs=(hbm, hbm),
        input_output_aliases={0: 0, 1: 1},
        compiler_params=pltpu.CompilerParams(has_side_effects=pltpu.SideEffectType.DATAFLOW_SIDE_EFFECTING),
    )(src_thru, land_thru, *sems, *after)[1]


def _sum_parts(owns, arrived, name):
    n = len(owns)

    def body(*refs):
        for own, arr, out in zip(refs[:n], refs[n:2 * n], refs[2 * n:]):
            acc = own[...].astype(f32)
            for k in range(3):
                acc = acc + arr[k].astype(f32)
            out[...] = acc

    vm = pl.BlockSpec(memory_space=pltpu.VMEM)
    return pl.pallas_call(
        body, name=name, out_shape=tuple(jax.ShapeDtypeStruct(o.shape, f32) for o in owns),
        in_specs=[vm] * (2 * n), out_specs=(vm,) * n,
        compiler_params=pltpu.CompilerParams(vmem_limit_bytes=V7X_VMEM_LIMIT),
    )(*owns, *arrived)


def _adam_update(w, g, m, v):
    m = ADAM_B1 * m + (1.0 - ADAM_B1) * g
    v = ADAM_B2 * v + (1.0 - ADAM_B2) * (g * g)
    m_hat = m / (1.0 - ADAM_B1 ** ADAM_STEP)
    v_hat = v / (1.0 - ADAM_B2 ** ADAM_STEP)
    return -ADAM_LR * (m_hat / (jnp.sqrt(v_hat) + ADAM_EPS) + ADAM_WD * w), m, v


def _sum_adamw(own, arrived, w, m, v, name, steps, after=()):
    rows = own.shape[0]
    br = rows // steps

    def body(own_ref, arr_ref, w_ref, m_ref, v_ref, *rest):
        g_out, d_out, m_out, v_out = rest[len(after):]
        g = own_ref[...].astype(f32)
        for k in range(3):
            g = g + arr_ref[k].astype(f32)
        g_out[...] = g
        d_out[...], m_out[...], v_out[...] = _adam_update(w_ref[...], g, m_ref[...], v_ref[...])

    blk = pl.BlockSpec((br, D), lambda i: (i, 0))
    return pl.pallas_call(
        body, name=name, grid=(steps,), out_shape=(jax.ShapeDtypeStruct((rows, D), f32),) * 4,
        in_specs=[blk, pl.BlockSpec((3, br, D), lambda i: (0, i, 0)), blk, blk, blk]
        + [pl.BlockSpec(memory_space=pl.ANY)] * len(after), out_specs=(blk,) * 4,
        compiler_params=pltpu.CompilerParams(dimension_semantics=("parallel",), vmem_limit_bytes=V7X_VMEM_LIMIT),
    )(own, arrived, w, m, v, *after)


def _adamw(ws, gs, ms, vs, name):
    n = len(ws)

    def body(*refs):
        w_r, g_r, m_r, v_r = (refs[k * n:(k + 1) * n] for k in range(4))
        d_o, m_o, v_o = (refs[(4 + k) * n:(5 + k) * n] for k in range(3))
        for j in range(n):
            d_o[j][...], m_o[j][...], v_o[j][...] = _adam_update(w_r[j][...], g_r[j][...], m_r[j][...], v_r[j][...])

    vm = pl.BlockSpec(memory_space=pltpu.VMEM)
    shapes = tuple(jax.ShapeDtypeStruct(w.shape, f32) for w in ws)
    return pl.pallas_call(
        body, name=name, out_shape=shapes * 3, in_specs=[vm] * (4 * n), out_specs=tuple([vm] * (3 * n)),
        compiler_params=pltpu.CompilerParams(vmem_limit_bytes=V7X_VMEM_LIMIT),
    )(*ws, *gs, *ms, *vs)


SMALL = (("w_pool", GROUPS * DH * DH), ("pool_scale", PW), ("ln1_g", D), ("ln1_b", D), ("conv_b", D_FF),
         ("ln2_g", D), ("ln2_b", D), ("conv_w", 3 * D_FF), ("loss", 1))
SMALL_ROWS = 640


def _pack(named):
    flat = jnp.concatenate([named[k].reshape(-1) for k, _ in SMALL])
    return jnp.pad(flat, (0, SMALL_ROWS * 128 - flat.shape[0])).reshape(SMALL_ROWS, 128)


def _unpack(packed):
    flat, out, at = packed.reshape(-1), {}, 0
    for k, size in SMALL:
        out[k] = flat[at:at + size]
        at += size
    return out


def kernel(x, w_in, w_pool, pool_scale, w_out, ln1_g, ln1_b, w_up, conv_w, conv_b, w_down, ln2_g, ln2_b, loss_target, m_w_in, m_w_pool, m_pool_scale, m_w_out, m_ln1_g, m_ln1_b, m_w_up, m_conv_w, m_conv_b, m_w_down, m_ln2_g, m_ln2_b, v_w_in, v_w_pool, v_pool_scale, v_w_out, v_ln1_g, v_ln1_b, v_w_up, v_conv_w, v_conv_b, v_w_down, v_ln2_g, v_ln2_b):
    me = 4 * lax.axis_index("x") + 2 * lax.axis_index("y") + lax.axis_index("c")
    x2, tgt = x[0], loss_target[0]

    w_pool_b = w_pool[0].astype(bf16)
    cos, sin = _rope_tables()
    dmat, qd, kd, cdec = _decay_tables(RET_TILE)

    qkv, g, oret, states, cat, pooled, xhat1, rstd1, x1b, xb, g_in, g_out, g_up, g_down, g_cw = _mix_forward(
        x2, w_in[0].T.astype(bf16), w_out[0].astype(bf16), cos, sin, dmat, qd, kd, cdec, w_pool_b, pool_scale,
        ln1_g, ln1_b, gather=[w_up[0].T.astype(bf16), w_down[0].astype(bf16), jnp.transpose(conv_w, (1, 0, 2))])
    w_in_t = g_in.reshape(IN_W, D)
    w_out_f = g_out.reshape(D, D)
    w_up_t = g_up.reshape(2 * D_FF, D)
    w_down_f = g_down.reshape(D_FF, D)
    conv_w_f = jnp.transpose(g_cw[:, :, 0, :], (1, 0, 2)).reshape(3, D_FF)
    dz1, dz2b, du, f, loss8, d_ln2_g, d_ln2_b, d_ln1_g, d_ln1_b, d_conv_b, d_conv_w = _ffn_forward_backward(
        xhat1, rstd1, ln1_g, ln1_b, w_up_t, conv_w_f, conv_b, w_down_f, ln2_g, ln2_b, tgt)

    (dw_down,) = _weight_grad(f, dz2b, "grad_w_down", tm=D_FF // 2)
    own_down, oth_down = _pair_reduce([dw_down.reshape(N_DEV, ROWS_DOWN, D)], "pair_reduce_down")
    dw_up_t, arr_down = _weight_grad(du, x1b, "grad_w_up", tm=D_FF // 2, exchange=[oth_down])
    own_up, oth_up = _pair_reduce([dw_up_t.reshape(N_DEV, ROWS_UP, D)], "pair_reduce_up")
    up_sems, up_src, up_land, up_started = _exchange_start(oth_up, "exchange_up_start", CHIP_BARRIER_SPLIT)
    dproj, grad_x, d_w_pool, d_pool_scale, dw_out = _mix_backward(
        dz1, w_out_f, qkv, g, oret, states, pooled, cat, cos, sin, dmat, qd, kd, cdec, w_pool_b, pool_scale, w_in_t,
        after=up_started)
    small = _pack({"w_pool": d_w_pool, "pool_scale": d_pool_scale, "ln1_g": d_ln1_g, "ln1_b": d_ln1_b,
                   "conv_b": d_conv_b, "ln2_g": d_ln2_g, "ln2_b": d_ln2_b, "conv_w": d_conv_w, "loss": loss8[0, :1]})
    own_out, own_small, oth_out, oth_small = _pair_reduce(
        [dw_out.reshape(N_DEV, ROWS_OUT, D), small.reshape(N_DEV, SMALL_ROWS // N_DEV, 128)], "pair_reduce_out")
    dw_in_t, arr_out, arr_small = _weight_grad(dproj, xb, "grad_w_in", tm=IN_W // 2, exchange=[oth_out, oth_small])
    arr_up = _exchange_wait(up_sems, up_src, up_land, [dw_in_t], "exchange_up_wait")
    own_in, oth_in = _pair_reduce([dw_in_t.reshape(N_DEV, ROWS_IN, D)], "pair_reduce_in")
    in_sems, in_src, in_land, started = _exchange_start(oth_in, "exchange_in_start", CHIP_BARRIER)
    (small_piece,) = _sum_parts([own_small], [arr_small], "sum_small_grads")
    (gs_small,) = _all_gather([small_piece], "gather_small_grads")

    names = ["w_in", "w_pool", "pool_scale", "w_out", "ln1_g", "ln1_b", "w_up", "conv_w", "conv_b", "w_down",
             "ln2_g", "ln2_b"]
    w_d = dict(w_in=w_in, w_pool=w_pool, pool_scale=pool_scale, w_out=w_out, ln1_g=ln1_g, ln1_b=ln1_b, w_up=w_up,
               conv_w=conv_w, conv_b=conv_b, w_down=w_down, ln2_g=ln2_g, ln2_b=ln2_b)
    m_d = dict(w_in=m_w_in, w_pool=m_w_pool, pool_scale=m_pool_scale, w_out=m_w_out, ln1_g=m_ln1_g, ln1_b=m_ln1_b,
               w_up=m_w_up, conv_w=m_conv_w, conv_b=m_conv_b, w_down=m_w_down, ln2_g=m_ln2_g, ln2_b=m_ln2_b)
    v_d = dict(w_in=v_w_in, w_pool=v_w_pool, pool_scale=v_pool_scale, w_out=v_w_out, ln1_g=v_ln1_g, ln1_b=v_ln1_b,
               w_up=v_w_up, conv_w=v_conv_w, conv_b=v_conv_b, w_down=v_w_down, ln2_g=v_ln2_g, ln2_b=v_ln2_b)
    g_d, delta, new_m, new_v = {}, {}, {}, {}

    def big_adamw(k, own, arr, transposed, steps, after=()):
        lay = (lambda a: a[0].T) if transposed else (lambda a: a[0])
        back = (lambda a: a.T[None]) if transposed else (lambda a: a[None])
        res = _sum_adamw(own, arr, lay(w_d[k]), lay(m_d[k]), lay(v_d[k]), "adamw_" + k, steps, after)
        g_d[k], delta[k], new_m[k], new_v[k] = (back(r) for r in res)
        return res[3]

    done = [big_adamw("w_up", own_up, arr_up, True, 4, after=(started,)),
            big_adamw("w_down", own_down, arr_down, False, 2, after=(started,)),
            big_adamw("w_out", own_out, arr_out, False, 2, after=(started,))]

    gsm = _unpack(gs_small)
    gsm["conv_w"] = lax.dynamic_slice(gsm["conv_w"].reshape(3, D_FF), (0, me * (D_FF // N_DEV)), (3, D_FF // N_DEV))
    lay = lambda k, a: jnp.transpose(a, (1, 0, 2)) if k == "conv_w" else a.reshape(-1, a.shape[-1])
    back = lambda k, a: jnp.transpose(a, (1, 0, 2)) if k == "conv_w" else a.reshape(w_d[k].shape)
    group = [k for k in names if k not in ("w_in", "w_out", "w_up", "w_down")]
    for k in group:
        g_d[k] = gsm[k].reshape(w_d[k].shape)
    res = _adamw([lay(k, w_d[k]) for k in group], [lay(k, g_d[k]) for k in group], [lay(k, m_d[k]) for k in group],
                 [lay(k, v_d[k]) for k in group], "adamw_small")
    for j, k in enumerate(group):
        delta[k], new_m[k], new_v[k] = (back(k, res[part * len(group) + j]) for part in range(3))

    arr_in = _exchange_wait(in_sems, in_src, in_land, done + [res[0]], "exchange_in_wait")
    big_adamw("w_in", own_in, arr_in, True, 4)

    loss = gsm["loss"].reshape(())
    return (loss, grad_x[None], *[g_d[k] for k in names], *[delta[k] for k in names], *[new_m[k] for k in names],
            *[new_v[k] for k in names])
```

```python
import math

import numpy as np
import jax
import jax.numpy as jnp
from jax import lax
from jax.experimental import pallas as pl
from jax.experimental.pallas import tpu as pltpu

f32 = jnp.float32
bf16 = jnp.bfloat16

N_DEV = 8
T = 4096
D = 1024
CHUNK = 64
MIX_TILE = 512
RET_TILE = 256
HEADS = 4
DH = 128
RW = HEADS * DH
PW = 512
GROUPS = 4
WINDOWS = (2, 4, 8, 16)
IN_W = 4 * RW + PW
D_FF = 2816
LN_EPS = 1e-5
RMS_EPS = 1e-6
ALPHA = 2.0 ** 0.25
K_SCALE = DH ** -0.5

ADAM_LR = 0.001
ADAM_B1 = 0.9
ADAM_B2 = 0.999
ADAM_EPS = 1e-08
ADAM_WD = 0.01
ADAM_STEP = 10

ROWS_IN, ROWS_OUT, ROWS_UP, ROWS_DOWN = IN_W // N_DEV, D // N_DEV, 2 * D_FF // N_DEV, D_FF // N_DEV

V7X_VMEM_LIMIT = 56 * 2 ** 20
HALO = 32

NT = (((1,), (1,)), ((), ()))
TN = (((0,), (0,)), ((), ()))
NN = (((1,), (0,)), ((), ()))


def _dot(a, b, dims=NN):
    return lax.dot_general(a, b, dims, preferred_element_type=f32)


def _const_spec(shape):
    zeros = (0,) * len(shape)
    return pl.BlockSpec(shape, lambda i: zeros, pipeline_mode=pl.Buffered(1))


def _sigmoid(x):
    return 0.5 * jnp.tanh(0.5 * x) + 0.5


def _decay_tables(tt):
    h = np.arange(HEADS, dtype=np.float64)
    log_gamma = np.log(1.0 - 2.0 ** (-5.0 - h)).astype(np.float32).astype(np.float64)[:, None, None]
    idx = np.arange(tt, dtype=np.float64)
    visible = (idx[None, :] // CHUNK) <= (idx[:, None] // CHUNK)
    mask = np.where(visible[None], np.exp(log_gamma * np.abs(idx[:, None] - idx[None, :])[None]), 0.0)
    qd = np.broadcast_to(np.exp(log_gamma * (idx[None, :, None] + 1.0)), (HEADS, tt, DH))
    kd = np.broadcast_to(np.exp(log_gamma * (tt - 1.0 - idx[None, :, None])), (HEADS, tt, DH))
    cd = np.exp(log_gamma[:, 0, 0] * tt)
    return (jnp.asarray(mask, f32), jnp.asarray(qd, f32), jnp.asarray(kd, f32), [float(c) for c in cd])


def _rope_tables():
    inv_freq = (10000.0 ** (-np.arange(0, DH, 2, dtype=np.float64) / DH)).astype(np.float32)
    ang = (np.arange(T, dtype=np.float32)[:, None] * inv_freq[None, :]).astype(np.float64)
    cos, sin = np.cos(ang), np.sin(ang)
    return (jnp.asarray(np.concatenate([cos, cos], axis=1), f32), jnp.asarray(np.concatenate([-sin, sin], axis=1), f32))


def _swap_halves(t):
    return pltpu.roll(t, DH // 2, axis=1)


def _mix_forward(x, w_in_shard, w_out_shard, cos, sin, dmat, qd, kd, cdec, w_pool, pool_scale, ln1_g, ln1_b,
                 gather_bf16, gather, tt=MIX_TILE):
    n_tiles = T // tt
    to_bf16 = [w_in_shard, w_out_shard] + list(gather_bf16)
    n_c, n_g = len(to_bf16), len(gather_bf16) + len(gather)

    def body(x_ref, cos_ref, sin_ref, dmat_ref, qd_ref, kd_ref, wpool_ref, pscale_ref, g1_ref, b1_ref, *rest):
        f32_in, plain_in, rest = rest[:n_c], rest[n_c:2 + n_g], rest[2 + n_g:]
        qkv_ref, g_ref, oret_ref, states_ref, cat_ref, pooled_ref, xhat_ref, rstd_ref, x1b_ref, xb_ref = rest[:10]
        fout, gout = rest[10:12], rest[12:12 + n_g]
        state_s, pext_s, tmp_s, wint_s, wout_s, load_sems, stage_sems, *rest = rest[12 + n_g:]
        stage_s, cast_s, sems = rest[:n_c], rest[n_c:2 * n_c], rest[2 * n_c:]
        fin, gin, fsems, gsems = cast_s[:2], tuple(cast_s[2:]) + tuple(plain_in), sems[:3], sems[3:]
        i = pl.program_id(0)

        @pl.when(i == 0)
        def _():
            stage = [pltpu.make_async_copy(src, dst, stage_sems.at[j]) for j, (src, dst) in enumerate(zip(f32_in, stage_s))]
            for cp in stage:
                cp.start()
            state_s[...] = jnp.zeros_like(state_s)
            pext_s[:, pl.ds(0, HALO), :] = jnp.zeros((GROUPS, HALO, DH), f32)

            def cast(js):
                for j in js:
                    stage[j].wait()
                    cast_s[j][...] = stage_s[j][...].astype(bf16)

            _barrier(_gather_peers())
            cast(range(2))
            _gather_start(fin, fout, *fsems, barrier=False)
            cast(range(2, n_c))
            _gather_forward(fin, fout, *fsems)
            _gather_start(gin, gout, *gsems, barrier=False)
            _gather_finish(fin, fout, *fsems)
            loads = [pltpu.make_async_copy(src.at[s], dst.at[pl.ds(s * src.shape[1], src.shape[1]), :],
                                           load_sems.at[j, s])
                     for j, (src, dst) in enumerate(((fout[0], wint_s), (fout[1], wout_s))) for s in range(N_DEV)]
            for ld in loads:
                ld.start()
            for ld in loads:
                ld.wait()

        @pl.when(i == n_tiles - 3)
        def _():
            _gather_forward(gin, gout, *gsems)

        xb = x_ref[...].astype(bf16)
        xb_ref[...] = xb
        cos_t, sin_t = cos_ref[...], sin_ref[...]
        for part in range(2):
            pr = _dot(xb, wint_s[pl.ds(part * RW, RW), :], NT)
            for h in range(HEADS):
                t = pr[:, h * DH:(h + 1) * DH]
                r = t * cos_t + _swap_halves(t) * sin_t
                if part == 1:
                    r = r * K_SCALE
                qkv_ref[:, part * RW + h * DH: part * RW + (h + 1) * DH] = r.astype(bf16)
        qkv_ref[:, 2 * RW:3 * RW] = _dot(xb, wint_s[pl.ds(2 * RW, RW), :], NT).astype(bf16)
        g_ref[...] = _dot(xb, wint_s[pl.ds(3 * RW, RW), :], NT)
        p = _dot(xb, wint_s[pl.ds(4 * RW, PW), :], NT)
        for gi in range(GROUPS):
            pext_s[gi, pl.ds(HALO, tt), :] = p[:, gi * DH:(gi + 1) * DH]

        for sub in range(tt // RET_TILE):
            rows = pl.ds(sub * RET_TILE, RET_TILE)
            for h in range(HEADS):
                q = qkv_ref[rows, h * DH:(h + 1) * DH]
                k = qkv_ref[rows, RW + h * DH: RW + (h + 1) * DH]
                v = qkv_ref[rows, 2 * RW + h * DH: 2 * RW + (h + 1) * DH]
                s = _dot(q, k, NT) * dmat_ref[h]
                st = state_s[h]
                stb = st.astype(bf16)
                states_ref[sub, h] = stb
                oret_ref[rows, h * DH:(h + 1) * DH] = (_dot(s.astype(bf16), v)
                                                      + _dot((q.astype(f32) * qd_ref[h]).astype(bf16), stb))
                state_s[h] = st * cdec[h] + _dot((k.astype(f32) * kd_ref[h]).astype(bf16), v, TN)

        for h in range(HEADS):
            sl = slice(h * DH, (h + 1) * DH)
            o = oret_ref[:, sl]
            r = lax.rsqrt(jnp.mean(o * o, axis=-1, keepdims=True) + RMS_EPS)
            gg = g_ref[:, sl]
            cat_ref[:, sl] = (o * r * (gg * _sigmoid(gg))).astype(bf16)

        pos1 = (i * tt + lax.broadcasted_iota(jnp.int32, (tt, 1), 0) + 1).astype(f32)
        for gi, w in enumerate(WINDOWS):
            sl = slice(gi * DH, (gi + 1) * DH)
            stages = int(math.log2(w))
            src = pext_s
            for s in range(stages):
                lo = HALO - 8 * (stages - 1 - s)
                n = tt + HALO - lo
                shift = 2 ** s
                val = src[gi, pl.ds(lo, n), :] + src[gi, pl.ds(lo - shift, n), :]
                if s == stages - 1:
                    wsum = val
                else:
                    tmp_s[gi, pl.ds(lo, n), :] = val
                    src = tmp_s
            p_g = pext_s[gi, pl.ds(HALO, tt), :]
            pooled = (wsum / jnp.minimum(pos1, float(w)) - p_g).astype(bf16)
            pooled_ref[:, sl] = pooled
            y = _dot(pooled, wpool_ref[gi]) * pscale_ref[:, sl]
            cat_ref[:, RW + gi * DH: RW + (gi + 1) * DH] = y.astype(bf16)
        pext_s[:, pl.ds(0, HALO), :] = pext_s[:, pl.ds(tt, HALO), :]

        z = ALPHA * x_ref[...] + _dot(cat_ref[...], wout_s[...])
        mu = jnp.mean(z, axis=-1, keepdims=True)
        zc = z - mu
        rstd = lax.rsqrt(jnp.mean(zc * zc, axis=-1, keepdims=True) + LN_EPS)
        xhat = zc * rstd
        xhat_ref[...] = xhat
        rstd_ref[...] = rstd
        x1b_ref[...] = (xhat * g1_ref[...] + b1_ref[...]).astype(bf16)

        @pl.when(i == n_tiles - 1)
        def _():
            _gather_finish(gin, gout, *gsems)

    tile = lambda w: pl.BlockSpec((tt, w), lambda i: (i, 0))
    hbm = pl.BlockSpec(memory_space=pltpu.HBM)
    out_shape = (
        jax.ShapeDtypeStruct((T, 3 * RW), bf16),
        jax.ShapeDtypeStruct((T, RW), f32),
        jax.ShapeDtypeStruct((T, RW), f32),
        jax.ShapeDtypeStruct((T // RET_TILE, HEADS, DH, DH), bf16),
        jax.ShapeDtypeStruct((T, D), bf16),
        jax.ShapeDtypeStruct((T, PW), bf16),
        jax.ShapeDtypeStruct((T, D), f32),
        jax.ShapeDtypeStruct((T, 1), f32),
        jax.ShapeDtypeStruct((T, D), bf16),
        jax.ShapeDtypeStruct((T, D), bf16),
    ) + tuple(jax.ShapeDtypeStruct((N_DEV,) + b.shape, bf16) for b in to_bf16
              ) + tuple(jax.ShapeDtypeStruct((N_DEV,) + b.shape, b.dtype) for b in gather)
    return pl.pallas_call(
        body, name="mix_forward", grid=(n_tiles,), out_shape=out_shape,
        in_specs=[tile(D), tile(DH), tile(DH),
                  _const_spec((HEADS, RET_TILE, RET_TILE)), _const_spec((HEADS, RET_TILE, DH)),
                  _const_spec((HEADS, RET_TILE, DH)),
                  _const_spec((GROUPS, DH, DH)), _const_spec((1, PW)),
                  _const_spec((1, D)), _const_spec((1, D))] + [hbm] * (2 + n_g),
        out_specs=(tile(3 * RW), tile(RW), tile(RW),
                   pl.BlockSpec((tt // RET_TILE, HEADS, DH, DH), lambda i: (i, 0, 0, 0)),
                   tile(D), tile(PW), tile(D), tile(1), tile(D), tile(D)) + (hbm,) * (2 + n_g),
        scratch_shapes=[pltpu.VMEM((HEADS, DH, DH), f32), pltpu.VMEM((GROUPS, tt + HALO, DH), f32),
                        pltpu.VMEM((GROUPS, tt + HALO, DH), f32), pltpu.VMEM((IN_W, D), bf16), pltpu.VMEM((D, D), bf16),
                        pltpu.SemaphoreType.DMA((2, N_DEV)), pltpu.SemaphoreType.DMA((n_c,))]
        + [pltpu.VMEM(b.shape, f32) for b in to_bf16] + [pltpu.VMEM(b.shape, bf16) for b in to_bf16]
        + _gather_sems(2) + _gather_sems(n_g),
        compiler_params=pltpu.CompilerParams(dimension_semantics=("arbitrary",), vmem_limit_bytes=V7X_VMEM_LIMIT,
                                             collective_id=GATHER_BARRIER),
    )(x, cos, sin, dmat, qd, kd, w_pool, pool_scale, ln1_g, ln1_b, *to_bf16, *gather)


def _ffn_forward_backward(xhat1, rstd1, ln1_g, ln1_b, w_up_t, conv_w, conv_b, w_down, ln2_g, ln2_b, target,
                          tt=256):
    n_tiles = T // tt
    FH = 16
    hb = tt // FH

    def body(xhat_ref, halo_ref, rstd_ref, g1_ref, b1_ref, wupt_ref, cw_ref, cb_ref, wdown_ref, g2_ref, b2_ref, tgt_ref,
             dz1_ref, dz2b_ref, du_ref, f_ref, loss_ref, dg2_ref, db2_ref, dg1_ref, db1_ref, dcb_ref, dcw_ref,
             gext_s, val_s, dhext_s):
        i = pl.program_id(0)
        tile_idx = n_tiles - 1 - i

        def rd(ref, off):
            return jnp.concatenate([ref[k, pl.ds(off, tt), :] for k in range(D_FF // 128)], axis=1)

        def wr(ref, val):
            for k in range(D_FF // 128):
                ref[k, pl.ds(0, val.shape[0]), :] = val[:, k * 128:(k + 1) * 128]

        @pl.when(i == 0)
        def _():
            for r in (loss_ref, dg2_ref, db2_ref, dg1_ref, db1_ref, dcb_ref, dcw_ref):
                r[...] = jnp.zeros_like(r)
            dhext_s[:, pl.ds(tt, 8), :] = jnp.zeros((D_FF // 128, 8, 128), f32)

        g1, b1 = g1_ref[...], b1_ref[...]
        xhat = xhat_ref[...]
        x1 = xhat * g1 + b1
        x1b = x1.astype(bf16)
        x1h = ((halo_ref[...] * g1 + b1) * jnp.where(tile_idx == 0, 0.0, 1.0)).astype(bf16)
        x1ext = jnp.concatenate([x1h, x1b], axis=0)

        val = _dot(x1b, wupt_ref[pl.ds(0, D_FF), :], NT)
        gate_ext = _dot(x1ext, wupt_ref[pl.ds(D_FF, D_FF), :], NT)
        wr(gext_s, gate_ext)
        hh = (cb_ref[...] + cw_ref[0:1, :] * rd(gext_s, FH - 2) + cw_ref[1:2, :] * rd(gext_s, FH - 1)
              + cw_ref[2:3, :] * gate_ext[FH:])
        sg = _sigmoid(hh)
        act = hh * sg
        wr(dhext_s, act)
        val_s[...] = val * (sg + act * (1.0 - sg))
        fb = (act * val).astype(bf16)
        f_ref[...] = fb

        z = ALPHA * x1 + _dot(fb, wdown_ref[...])
        mu = jnp.mean(z, axis=-1, keepdims=True)
        zc = z - mu
        rstd2 = lax.rsqrt(jnp.mean(zc * zc, axis=-1, keepdims=True) + LN_EPS)
        xh2 = zc * rstd2
        diff = xh2 * g2_ref[...] + b2_ref[...] - tgt_ref[...]
        loss_ref[...] += 0.5 * jnp.sum(diff * diff) / D
        dy = diff * (1.0 / D)
        dg2_ref[...] += jnp.sum(dy * xh2, axis=0, keepdims=True)
        db2_ref[...] += jnp.sum(dy, axis=0, keepdims=True)
        dyg = dy * g2_ref[...]
        dz2 = rstd2 * (dyg - jnp.mean(dyg, axis=-1, keepdims=True) - xh2 * jnp.mean(dyg * xh2, axis=-1, keepdims=True))
        dz2b = dz2.astype(bf16)
        dz2b_ref[...] = dz2b

        df = _dot(dz2b, wdown_ref[...], NT)
        dval = df * rd(dhext_s, 0)
        dh = df * val_s[...]
        wr(dhext_s, dh)
        dh1, dh2, g0 = rd(dhext_s, 1), rd(dhext_s, 2), rd(gext_s, FH)
        dcb_ref[...] += jnp.sum(dh, axis=0, keepdims=True)
        dcw_ref[0:1, :] += jnp.sum(dh2 * g0, axis=0, keepdims=True)
        dcw_ref[1:2, :] += jnp.sum(dh1 * g0, axis=0, keepdims=True)
        dcw_ref[2:3, :] += jnp.sum(dh * g0, axis=0, keepdims=True)
        dgate = cw_ref[2:3, :] * dh + cw_ref[1:2, :] * dh1 + cw_ref[0:1, :] * dh2
        dvalb, dgateb = dval.astype(bf16), dgate.astype(bf16)
        du_ref[:, :D_FF] = dvalb
        du_ref[:, D_FF:] = dgateb
        dx1 = ALPHA * dz2 + _dot(dvalb, wupt_ref[pl.ds(0, D_FF), :]) + _dot(dgateb, wupt_ref[pl.ds(D_FF, D_FF), :])
        dhext_s[:, pl.ds(tt, 8), :] = dhext_s[:, pl.ds(0, 8), :]

        dg1_ref[...] += jnp.sum(dx1 * xhat, axis=0, keepdims=True)
        db1_ref[...] += jnp.sum(dx1, axis=0, keepdims=True)
        dxg = dx1 * g1
        dz1_ref[...] = rstd_ref[...] * (dxg - jnp.mean(dxg, axis=-1, keepdims=True)
                                        - xhat * jnp.mean(dxg * xhat, axis=-1, keepdims=True))

    rtile = lambda w: pl.BlockSpec((tt, w), lambda i: (n_tiles - 1 - i, 0))
    acc = lambda shape: pl.BlockSpec(shape, lambda i: (0, 0))
    out_shape = (
        jax.ShapeDtypeStruct((T, D), f32),
        jax.ShapeDtypeStruct((T, D), bf16),
        jax.ShapeDtypeStruct((T, 2 * D_FF), bf16),
        jax.ShapeDtypeStruct((T, D_FF), bf16),
        jax.ShapeDtypeStruct((8, 128), f32),
        jax.ShapeDtypeStruct((1, D), f32), jax.ShapeDtypeStruct((1, D), f32),
        jax.ShapeDtypeStruct((1, D), f32), jax.ShapeDtypeStruct((1, D), f32),
        jax.ShapeDtypeStruct((1, D_FF), f32), jax.ShapeDtypeStruct((3, D_FF), f32),
    )
    return pl.pallas_call(
        body, name="ffn_forward_backward", grid=(n_tiles,), out_shape=out_shape,
        in_specs=[rtile(D),
                  pl.BlockSpec((FH, D), lambda i: (jnp.maximum((n_tiles - 1 - i) * hb - 1, 0), 0)),
                  rtile(1), _const_spec((1, D)), _const_spec((1, D)), _const_spec((2 * D_FF, D)),
                  _const_spec((3, D_FF)), _const_spec((1, D_FF)), _const_spec((D_FF, D)),
                  _const_spec((1, D)), _const_spec((1, D)), rtile(D)],
        out_specs=(rtile(D), rtile(D), rtile(2 * D_FF), rtile(D_FF), acc((8, 128)),
                   acc((1, D)), acc((1, D)), acc((1, D)), acc((1, D)), acc((1, D_FF)), acc((3, D_FF))),
        scratch_shapes=[pltpu.VMEM((D_FF // 128, tt + FH, 128), f32), pltpu.VMEM((tt, D_FF), f32),
                        pltpu.VMEM((D_FF // 128, tt + 8, 128), f32)],
        compiler_params=pltpu.CompilerParams(dimension_semantics=("arbitrary",), vmem_limit_bytes=V7X_VMEM_LIMIT),
    )(xhat1, xhat1, rstd1, ln1_g, ln1_b, w_up_t, conv_w, conv_b, w_down, ln2_g, ln2_b, target)


def _mix_backward(dz1, w_out, qkv, g, oret, states, pooled, cat, cos, sin, dmat, qd, kd, cdec, w_pool, pool_scale, w_in_t,
                  after, tt=MIX_TILE):
    n_tiles = T // tt

    def body(dz1_ref, wout_ref, qkv_ref, g_ref, oret_ref, states_ref, pooled_ref, cat_ref, cos_ref, sin_ref, dmat_ref,
             qd_ref, kd_ref, wpool_ref, pscale_ref, wint_ref, after_ref,
             dproj_ref, gx_ref, dwpool_ref, dpscale_ref, dwout_ref, dstate_s, dout_s, eext_s, tmp_s, dwout_s):
        i = pl.program_id(0)
        tile_idx = n_tiles - 1 - i

        @pl.when(i == 0)
        def _():
            dstate_s[...] = jnp.zeros_like(dstate_s)
            dwpool_ref[...] = jnp.zeros_like(dwpool_ref)
            dpscale_ref[...] = jnp.zeros_like(dpscale_ref)
            dwout_s[...] = jnp.zeros_like(dwout_s)
            eext_s[:, pl.ds(tt, HALO), :] = jnp.zeros((GROUPS, HALO, DH), f32)

        dz1 = dz1_ref[...]
        dz1b = dz1.astype(bf16)
        dcat = _dot(dz1b, wout_ref[...], NT)
        dwout_s[...] += _dot(cat_ref[...], dz1b, TN)

        pos1 = (tile_idx * tt + lax.broadcasted_iota(jnp.int32, (tt, 1), 0) + 1).astype(f32)
        for gi, w in enumerate(WINDOWS):
            sl = slice(gi * DH, (gi + 1) * DH)
            dpo = dcat[:, RW + gi * DH: RW + (gi + 1) * DH]
            pooled_g = pooled_ref[:, sl]
            ylin = _dot(pooled_g, wpool_ref[gi])
            dpscale_ref[:, sl] += jnp.sum(dpo * ylin, axis=0, keepdims=True)
            dpw = (dpo * pscale_ref[:, sl]).astype(bf16)
            dwpool_ref[gi] += _dot(pooled_g, dpw, TN)
            dpooled = _dot(dpw, wpool_ref[gi], NT)
            eext_s[gi, pl.ds(0, tt), :] = dpooled / jnp.minimum(pos1, float(w))
            stages = int(math.log2(w))
            src = eext_s
            for s in range(stages):
                n = tt + 8 * (stages - 1 - s)
                shift = 2 ** s
                val = src[gi, pl.ds(0, n), :] + src[gi, pl.ds(shift, n), :]
                if s == stages - 1:
                    wsum = val
                else:
                    tmp_s[gi, pl.ds(0, n), :] = val
                    src = tmp_s
            dproj_ref[:, 4 * RW + gi * DH: 4 * RW + (gi + 1) * DH] = (wsum - dpooled).astype(bf16)
        eext_s[:, pl.ds(tt, HALO), :] = eext_s[:, pl.ds(0, HALO), :]

        for h in range(HEADS):
            sl = slice(h * DH, (h + 1) * DH)
            dr = dcat[:, sl]
            o = oret_ref[:, sl]
            r = lax.rsqrt(jnp.mean(o * o, axis=-1, keepdims=True) + RMS_EPS)
            rn = o * r
            gg = g_ref[:, sl]
            sg = _sigmoid(gg)
            dproj_ref[:, 3 * RW + h * DH: 3 * RW + (h + 1) * DH] = (dr * rn * (sg * (1.0 + gg * (1.0 - sg)))).astype(bf16)
            drn = dr * (gg * sg)
            dout_s[:, sl] = (r * (drn - rn * jnp.mean(drn * rn, axis=-1, keepdims=True))).astype(bf16)

        for sub in reversed(range(tt // RET_TILE)):
            rows = pl.ds(sub * RET_TILE, RET_TILE)
            cos_t, sin_t = cos_ref[rows, :], sin_ref[rows, :]
            for h in range(HEADS):
                q = qkv_ref[rows, h * DH:(h + 1) * DH]
                k = qkv_ref[rows, RW + h * DH: RW + (h + 1) * DH]
                v = qkv_ref[rows, 2 * RW + h * DH: 2 * RW + (h + 1) * DH]
                do = dout_s[rows, h * DH:(h + 1) * DH]
                stb = states_ref[sub, h]
                dst = dstate_s[h]
                dstb = dst.astype(bf16)
                sb = (_dot(q, k, NT) * dmat_ref[h]).astype(bf16)
                dsb = (_dot(do, v, NT) * dmat_ref[h]).astype(bf16)
                dq = _dot(dsb, k) + _dot(do, stb, NT) * qd_ref[h]
                dk = _dot(dsb, q, TN) + _dot(v, dstb, NT) * kd_ref[h]
                dv = _dot(sb, do, TN) + _dot((k.astype(f32) * kd_ref[h]).astype(bf16), dstb)
                dstate_s[h] = dst * cdec[h] + _dot((q.astype(f32) * qd_ref[h]).astype(bf16), do, TN)
                dproj_ref[rows, h * DH:(h + 1) * DH] = (dq * cos_t - _swap_halves(dq) * sin_t).astype(bf16)
                dproj_ref[rows, RW + h * DH: RW + (h + 1) * DH] = (
                    (dk * cos_t - _swap_halves(dk) * sin_t) * K_SCALE).astype(bf16)
                dproj_ref[rows, 2 * RW + h * DH: 2 * RW + (h + 1) * DH] = dv.astype(bf16)

        gx_ref[...] = ALPHA * dz1 + _dot(dproj_ref[...], wint_ref[...])

        @pl.when(i == n_tiles - 1)
        def _():
            dwout_ref[...] = dwout_s[...].astype(bf16)

    rtile = lambda w: pl.BlockSpec((tt, w), lambda i: (n_tiles - 1 - i, 0))
    out_shape = (
        jax.ShapeDtypeStruct((T, IN_W), bf16),
        jax.ShapeDtypeStruct((T, D), f32),
        jax.ShapeDtypeStruct((GROUPS, DH, DH), f32),
        jax.ShapeDtypeStruct((1, PW), f32),
        jax.ShapeDtypeStruct((D, D), bf16),
    )
    return pl.pallas_call(
        body, name="mix_backward", grid=(n_tiles,), out_shape=out_shape,
        in_specs=[rtile(D), _const_spec((D, D)), rtile(3 * RW), rtile(RW), rtile(RW),
                  pl.BlockSpec((tt // RET_TILE, HEADS, DH, DH), lambda i: (n_tiles - 1 - i, 0, 0, 0)),
                  rtile(PW), rtile(D), rtile(DH), rtile(DH),
                  _const_spec((HEADS, RET_TILE, RET_TILE)), _const_spec((HEADS, RET_TILE, DH)),
                  _const_spec((HEADS, RET_TILE, DH)),
                  _const_spec((GROUPS, DH, DH)), _const_spec((1, PW)), _const_spec((IN_W, D)),
                  pl.BlockSpec(memory_space=pl.ANY)],
        out_specs=(rtile(IN_W), rtile(D), pl.BlockSpec((GROUPS, DH, DH), lambda i: (0, 0, 0)),
                   pl.BlockSpec((1, PW), lambda i: (0, 0)),
                   pl.BlockSpec((D, D), lambda i: (0, 0), pipeline_mode=pl.Buffered(1))),
        scratch_shapes=[pltpu.VMEM((HEADS, DH, DH), f32), pltpu.VMEM((tt, RW), bf16),
                        pltpu.VMEM((GROUPS, tt + HALO, DH), f32), pltpu.VMEM((GROUPS, tt + HALO, DH), f32),
                        pltpu.VMEM((D, D), f32)],
        compiler_params=pltpu.CompilerParams(dimension_semantics=("arbitrary",), vmem_limit_bytes=V7X_VMEM_LIMIT),
    )(dz1, w_out, qkv, g, oret, states, pooled, cat, cos, sin, dmat, qd, kd, w_pool, pool_scale, w_in_t, after)


def _weight_grad(a, b, name, tm, exchange=()):
    m = a.shape[1]
    n_m, n_e = m // tm, len(exchange)

    def body(a_ref, b_ref, *rest):
        ein, o_ref, eout, sems = rest[:n_e], rest[n_e], rest[n_e + 1:2 * n_e + 1], rest[2 * n_e + 1:]
        i = pl.program_id(0)

        if n_e:
            @pl.when(i == 0)
            def _():
                _chip_exchange_start(ein, eout, *sems)

        o_ref[...] = _dot(a_ref[...], b_ref[...].astype(bf16), TN).astype(bf16)

        if n_e:
            @pl.when(i == n_m - 1)
            def _():
                _chip_exchange_finish(ein, eout, *sems)

    hbm = pl.BlockSpec(memory_space=pltpu.HBM)
    return pl.pallas_call(
        body, name=name, grid=(n_m,),
        out_shape=(jax.ShapeDtypeStruct((m, D), bf16),) + tuple(jax.ShapeDtypeStruct(e.shape, e.dtype) for e in exchange),
        in_specs=[pl.BlockSpec((T, tm), lambda i: (0, i)),
                  pl.BlockSpec((T, D), lambda i: (0, 0), pipeline_mode=pl.Buffered(1))] + [hbm] * n_e,
        out_specs=(pl.BlockSpec((tm, D), lambda i: (i, 0)),) + (hbm,) * n_e,
        scratch_shapes=_chip_exchange_sems(n_e),
        compiler_params=pltpu.CompilerParams(dimension_semantics=("arbitrary",), vmem_limit_bytes=V7X_VMEM_LIMIT,
                                             collective_id=CHIP_BARRIER if n_e else None),
    )(a, b, *exchange)


CHIP_FLIPS = ((1, 0), (0, 1), (1, 1))
PAIR_BARRIER, CHIP_BARRIER, GATHER_BARRIER, CHIP_BARRIER_SPLIT = 0, 1, 2, 3


def _barrier(peers):
    sem = pltpu.get_barrier_semaphore()
    for peer in peers:
        pl.semaphore_signal(sem, inc=1, device_id=peer, device_id_type=pl.DeviceIdType.MESH)
    pl.semaphore_wait(sem, len(peers))


def _me():
    return lax.axis_index("x"), lax.axis_index("y"), lax.axis_index("c")


def _chip(me, k):
    x, y, _ = me
    if k == 0:
        return x, y
    fx, fy = CHIP_FLIPS[k - 1]
    return (1 - x if fx else x), (1 - y if fy else y)


def _slot(x, y, c):
    return 4 * x + 2 * y + c


def _remote(src, dst, send_sem, recv_sem, to):
    return pltpu.make_async_remote_copy(src_ref=src, dst_ref=dst, send_sem=send_sem, recv_sem=recv_sem,
                                        device_id=to, device_id_type=pl.DeviceIdType.MESH)


def _gather_sems(n):
    return [pltpu.SemaphoreType.DMA((7, n)), pltpu.SemaphoreType.DMA((7, n)), pltpu.SemaphoreType.DMA((n,))] if n else []


def _gather_copy(k, j, gin, gout, send_sems, recv_sems, sending):
    x, y, c = _me()
    sibling, x_chip, y_chip, d_chip = (x, y, 1 - c), (1 - x, y), (x, 1 - y), (1 - x, 1 - y)
    south = c == 0
    passed_on = (jnp.where(south, 1 - x, x), jnp.where(south, y, 1 - y), c)
    src, to = gin[j], sibling
    if sending:
        block = {0: (x, y, c), 1: (x, y, c), 2: (x, y, c), 3: passed_on, 4: (*x_chip, c), 5: (*y_chip, c), 6: (*d_chip, c)}[k]
        to = {1: (*x_chip, c), 2: (*y_chip, c), 3: (jnp.where(south, x, 1 - x), jnp.where(south, 1 - y, y), c)}.get(k, sibling)
        if k >= 3:
            src = gout[j].at[_slot(*block)]
    else:
        block = {0: sibling, 1: (*x_chip, c), 2: (*y_chip, c), 3: (*d_chip, c), 4: (*x_chip, 1 - c), 5: (*y_chip, 1 - c),
                 6: (*d_chip, 1 - c)}[k]
    return _remote(src, gout[j].at[_slot(*block)], send_sems.at[k, j], recv_sems.at[k, j], to)


def _gather_do(ks, action, gin, gout, send_sems, recv_sems):
    for k in ks:
        for j in range(len(gin)):
            cp = _gather_copy(k, j, gin, gout, send_sems, recv_sems, action != "wait_recv")
            getattr(cp, action)()


def _gather_peers():
    x, y, c = _me()
    return [(x, y, 1 - c), (1 - x, y, c), (x, 1 - y, c)]


def _gather_start(gin, gout, send_sems, recv_sems, local_sems, barrier=True):
    if barrier:
        _barrier(_gather_peers())
    for j in range(len(gin)):
        pltpu.make_async_copy(gin[j], gout[j].at[_slot(*_me())], local_sems.at[j]).start()
    _gather_do((0, 1, 2), "start", gin, gout, send_sems, recv_sems)


def _gather_forward(gin, gout, send_sems, recv_sems, local_sems):
    _gather_do((1, 2), "wait_recv", gin, gout, send_sems, recv_sems)
    _gather_do((3, 4, 5), "start", gin, gout, send_sems, recv_sems)


def _gather_finish(gin, gout, send_sems, recv_sems, local_sems):
    _gather_do((3,), "wait_recv", gin, gout, send_sems, recv_sems)
    _gather_do((6,), "start", gin, gout, send_sems, recv_sems)
    _gather_do((0, 4, 5, 6), "wait_recv", gin, gout, send_sems, recv_sems)
    _gather_do(range(7), "wait_send", gin, gout, send_sems, recv_sems)
    for j in range(len(gin)):
        pltpu.make_async_copy(gin[j], gout[j].at[_slot(*_me())], local_sems.at[j]).wait()


def _all_gather(blocks, name):
    n = len(blocks)

    def body(*refs):
        gin, gout, sems = refs[:n], refs[n:2 * n], refs[2 * n:]
        _gather_start(gin, gout, *sems)
        _gather_forward(gin, gout, *sems)
        _gather_finish(gin, gout, *sems)

    hbm = pl.BlockSpec(memory_space=pltpu.HBM)
    return pl.pallas_call(
        body, name=name,
        out_shape=tuple(jax.ShapeDtypeStruct((N_DEV,) + b.shape, b.dtype) for b in blocks),
        in_specs=[hbm] * n, out_specs=(hbm,) * n, scratch_shapes=_gather_sems(n),
        compiler_params=pltpu.CompilerParams(collective_id=GATHER_BARRIER),
    )(*blocks)


def _pair_reduce(parts, name):
    n = len(parts)

    def body(*refs):
        ins, own, others, landing, mine = (refs[k * n:(k + 1) * n] for k in range(5))
        send_sems, recv_sems, local_sems = refs[5 * n:]
        me = _me()
        x, y, c = me
        sibling = (x, y, 1 - c)
        _barrier([sibling])
        sends, loads = [], []
        for k in range(4):
            for j in range(n):
                cp = _remote(ins[j].at[_slot(*_chip(me, k), 1 - c)], landing[j].at[k], send_sems.at[k, j],
                             recv_sems.at[k, j], sibling)
                cp.start()
                sends.append(cp)
                ld = pltpu.make_async_copy(ins[j].at[_slot(*_chip(me, k), c)], mine[j].at[k], local_sems.at[k, j])
                ld.start()
                loads.append(ld)
        for k in range(4):
            for j in range(n):
                loads[k * n + j].wait()
                _remote(ins[j].at[0], landing[j].at[k], send_sems.at[k, j], recv_sems.at[k, j], sibling).wait_recv()
                total = mine[j][k].astype(f32) + landing[j][k].astype(f32)
                if k == 0:
                    own[j][...] = total.astype(own[j].dtype)
                else:
                    others[j][k - 1] = total.astype(others[j].dtype)
        for cp in sends:
            cp.wait_send()

    vm = pl.BlockSpec(memory_space=pltpu.VMEM)
    return pl.pallas_call(
        body, name=name,
        out_shape=tuple(jax.ShapeDtypeStruct(p.shape[1:], p.dtype) for p in parts)
        + tuple(jax.ShapeDtypeStruct((3,) + p.shape[1:], p.dtype) for p in parts),
        in_specs=[pl.BlockSpec(memory_space=pltpu.HBM)] * n, out_specs=(vm,) * (2 * n),
        scratch_shapes=[pltpu.VMEM((4,) + p.shape[1:], p.dtype) for p in parts] * 2
        + [pltpu.SemaphoreType.DMA((4, n)), pltpu.SemaphoreType.DMA((4, n)), pltpu.SemaphoreType.DMA((4, n))],
        compiler_params=pltpu.CompilerParams(vmem_limit_bytes=V7X_VMEM_LIMIT, collective_id=PAIR_BARRIER),
    )(*parts)


def _chip_exchange_sems(n):
    return [pltpu.SemaphoreType.DMA((3, n)), pltpu.SemaphoreType.DMA((3, n))] if n else []


def _chip_exchange_copy(k, j, ein, eout, send_sems, recv_sems):
    me = _me()
    return _remote(ein[j].at[k - 1], eout[j].at[k - 1], send_sems.at[k - 1, j], recv_sems.at[k - 1, j],
                   (*_chip(me, k), me[2]))


def _chip_peers():
    me = _me()
    return [(*_chip(me, k), me[2]) for k in range(1, 4)]


def _chip_exchange_start(ein, eout, send_sems, recv_sems, barrier=True):
    if barrier:
        _barrier(_chip_peers())
    for k in range(1, 4):
        for j in range(len(ein)):
            _chip_exchange_copy(k, j, ein, eout, send_sems, recv_sems).start()


def _chip_exchange_finish(ein, eout, send_sems, recv_sems):
    for k in range(1, 4):
        for j in range(len(ein)):
            _chip_exchange_copy(k, j, ein, eout, send_sems, recv_sems).wait_recv()
    for k in range(1, 4):
        for j in range(len(ein)):
            _chip_exchange_copy(k, j, ein, eout, send_sems, recv_sems).wait_send()


def _split_copies(src_ref, dst_ref, sems):
    me = _me()
    return [_remote(src_ref.at[k - 1], dst_ref.at[k - 1], sems[k - 1], sems[2 + k], (*_chip(me, k), me[2]))
            for k in range(1, 4)]


def _exchange_start(others, name, barrier_id):
    def body(src_ref, land_ref, *rest):
        sems, token_ref = rest[:6], rest[8]
        _barrier(_chip_peers())
        for copy in _split_copies(src_ref, land_ref, sems):
            copy.start()
        token_ref[...] = jnp.zeros_like(token_ref)

    hbm, sem = pl.BlockSpec(memory_space=pltpu.HBM), pl.BlockSpec(memory_space=pltpu.SEMAPHORE)
    thru = pltpu.HBM(others.shape, others.dtype)
    res = pl.pallas_call(
        body, name=name,
        out_shape=(pltpu.SemaphoreType.DMA(()),) * 6 + (thru, thru, jax.ShapeDtypeStruct((8, 128), f32)),
        in_specs=(hbm, hbm), out_specs=(sem,) * 6 + (hbm, hbm, pl.BlockSpec(memory_space=pltpu.VMEM)),
        input_output_aliases={0: 6, 1: 7},
        compiler_params=pltpu.CompilerParams(has_side_effects=pltpu.SideEffectType.DATAFLOW_SIDE_EFFECTING,
                                             collective_id=barrier_id),
    )(pltpu.with_memory_space_constraint(others, pltpu.HBM),
      pltpu.with_memory_space_constraint(lax.empty(others.shape, others.dtype), pltpu.HBM))
    return res[:6], res[6], res[7], res[8]


def _exchange_wait(sems, src_thru, land_thru, after, name):
    n_after = len(after)

    def body(src_ref, land_ref, *rest):
        for copy in _split_copies(src_ref, land_ref, rest[:6]):
            copy.wait_send()
            copy.wait_recv()

    hbm, sem = pl.BlockSpec(memory_space=pltpu.HBM), pl.BlockSpec(memory_space=pltpu.SEMAPHORE)
    thru = pltpu.HBM(src_thru.shape, src_thru.dtype)
    return pl.pallas_call(
        body, name=name, out_shape=(thru, thru),
        in_specs=(hbm, hbm) + (sem,) * 6 + (pl.BlockSpec(memory_space=pl.ANY),) * n_after, out_specs=(hbm, hbm),
        input_output_aliases={0: 0, 1: 1},
        compiler_params=pltpu.CompilerParams(has_side_effects=pltpu.SideEffectType.DATAFLOW_SIDE_EFFECTING),
    )(src_thru, land_thru, *sems, *after)[1]


def _sum_parts(owns, arrived, name):
    n = len(owns)

    def body(*refs):
        for own, arr, out in zip(refs[:n], refs[n:2 * n], refs[2 * n:]):
            acc = own[...].astype(f32)
            for k in range(3):
                acc = acc + arr[k].astype(f32)
            out[...] = acc

    vm = pl.BlockSpec(memory_space=pltpu.VMEM)
    return pl.pallas_call(
        body, name=name, out_shape=tuple(jax.ShapeDtypeStruct(o.shape, f32) for o in owns),
        in_specs=[vm] * (2 * n), out_specs=(vm,) * n,
        compiler_params=pltpu.CompilerParams(vmem_limit_bytes=V7X_VMEM_LIMIT),
    )(*owns, *arrived)


def _adam_update(w, g, m, v):
    m = ADAM_B1 * m + (1.0 - ADAM_B1) * g
    v = ADAM_B2 * v + (1.0 - ADAM_B2) * (g * g)
    m_hat = m / (1.0 - ADAM_B1 ** ADAM_STEP)
    v_hat = v / (1.0 - ADAM_B2 ** ADAM_STEP)
    return -ADAM_LR * (m_hat / (jnp.sqrt(v_hat) + ADAM_EPS) + ADAM_WD * w), m, v


def _sum_adamw(own, arrived, w, m, v, name, steps, after=()):
    rows = own.shape[0]
    br = rows // steps

    def body(own_ref, arr_ref, w_ref, m_ref, v_ref, *rest):
        g_out, d_out, m_out, v_out = rest[len(after):]
        g = own_ref[...].astype(f32)
        for k in range(3):
            g = g + arr_ref[k].astype(f32)
        g_out[...] = g
        d_out[...], m_out[...], v_out[...] = _adam_update(w_ref[...], g, m_ref[...], v_ref[...])

    blk = pl.BlockSpec((br, D), lambda i: (i, 0))
    return pl.pallas_call(
        body, name=name, grid=(steps,), out_shape=(jax.ShapeDtypeStruct((rows, D), f32),) * 4,
        in_specs=[blk, pl.BlockSpec((3, br, D), lambda i: (0, i, 0)), blk, blk, blk]
        + [pl.BlockSpec(memory_space=pl.ANY)] * len(after), out_specs=(blk,) * 4,
        compiler_params=pltpu.CompilerParams(dimension_semantics=("parallel",), vmem_limit_bytes=V7X_VMEM_LIMIT),
    )(own, arrived, w, m, v, *after)


def _adamw(ws, gs, ms, vs, name):
    n = len(ws)

    def body(*refs):
        w_r, g_r, m_r, v_r = (refs[k * n:(k + 1) * n] for k in range(4))
        d_o, m_o, v_o = (refs[(4 + k) * n:(5 + k) * n] for k in range(3))
        for j in range(n):
            d_o[j][...], m_o[j][...], v_o[j][...] = _adam_update(w_r[j][...], g_r[j][...], m_r[j][...], v_r[j][...])

    vm = pl.BlockSpec(memory_space=pltpu.VMEM)
    shapes = tuple(jax.ShapeDtypeStruct(w.shape, f32) for w in ws)
    return pl.pallas_call(
        body, name=name, out_shape=shapes * 3, in_specs=[vm] * (4 * n), out_specs=tuple([vm] * (3 * n)),
        compiler_params=pltpu.CompilerParams(vmem_limit_bytes=V7X_VMEM_LIMIT),
    )(*ws, *gs, *ms, *vs)


SMALL = (("w_pool", GROUPS * DH * DH), ("pool_scale", PW), ("ln1_g", D), ("ln1_b", D), ("conv_b", D_FF),
         ("ln2_g", D), ("ln2_b", D), ("conv_w", 3 * D_FF), ("loss", 1))
SMALL_ROWS = 640


def _pack(named):
    flat = jnp.concatenate([named[k].reshape(-1) for k, _ in SMALL])
    return jnp.pad(flat, (0, SMALL_ROWS * 128 - flat.shape[0])).reshape(SMALL_ROWS, 128)


def _unpack(packed):
    flat, out, at = packed.reshape(-1), {}, 0
    for k, size in SMALL:
        out[k] = flat[at:at + size]
        at += size
    return out


def kernel(x, w_in, w_pool, pool_scale, w_out, ln1_g, ln1_b, w_up, conv_w, conv_b, w_down, ln2_g, ln2_b, loss_target, m_w_in, m_w_pool, m_pool_scale, m_w_out, m_ln1_g, m_ln1_b, m_w_up, m_conv_w, m_conv_b, m_w_down, m_ln2_g, m_ln2_b, v_w_in, v_w_pool, v_pool_scale, v_w_out, v_ln1_g, v_ln1_b, v_w_up, v_conv_w, v_conv_b, v_w_down, v_ln2_g, v_ln2_b):
    me = 4 * lax.axis_index("x") + 2 * lax.axis_index("y") + lax.axis_index("c")
    x2, tgt = x[0], loss_target[0]

    w_pool_b = w_pool[0].astype(bf16)
    cos, sin = _rope_tables()
    dmat, qd, kd, cdec = _decay_tables(RET_TILE)

    qkv, g, oret, states, cat, pooled, xhat1, rstd1, x1b, xb, g_in, g_out, g_up, g_down, g_cw = _mix_forward(
        x2, w_in[0].T, w_out[0], cos, sin, dmat, qd, kd, cdec, w_pool_b, pool_scale, ln1_g, ln1_b,
        gather_bf16=[w_up[0].T, w_down[0]], gather=[jnp.transpose(conv_w, (1, 0, 2))])
    w_in_t = g_in.reshape(IN_W, D)
    w_out_f = g_out.reshape(D, D)
    w_up_t = g_up.reshape(2 * D_FF, D)
    w_down_f = g_down.reshape(D_FF, D)
    conv_w_f = jnp.transpose(g_cw[:, :, 0, :], (1, 0, 2)).reshape(3, D_FF)
    dz1, dz2b, du, f, loss8, d_ln2_g, d_ln2_b, d_ln1_g, d_ln1_b, d_conv_b, d_conv_w = _ffn_forward_backward(
        xhat1, rstd1, ln1_g, ln1_b, w_up_t, conv_w_f, conv_b, w_down_f, ln2_g, ln2_b, tgt)

    (dw_down,) = _weight_grad(f, dz2b, "grad_w_down", tm=D_FF // 2)
    own_down, oth_down = _pair_reduce([dw_down.reshape(N_DEV, ROWS_DOWN, D)], "pair_reduce_down")
    dw_up_t, arr_down = _weight_grad(du, x1b, "grad_w_up", tm=D_FF // 2, exchange=[oth_down])
    own_up, oth_up = _pair_reduce([dw_up_t.reshape(N_DEV, ROWS_UP, D)], "pair_reduce_up")
    up_sems, up_src, up_land, up_started = _exchange_start(oth_up, "exchange_up_start", CHIP_BARRIER_SPLIT)
    dproj, grad_x, d_w_pool, d_pool_scale, dw_out = _mix_backward(
        dz1, w_out_f, qkv, g, oret, states, pooled, cat, cos, sin, dmat, qd, kd, cdec, w_pool_b, pool_scale, w_in_t,
        after=up_started)
    small = _pack({"w_pool": d_w_pool, "pool_scale": d_pool_scale, "ln1_g": d_ln1_g, "ln1_b": d_ln1_b,
                   "conv_b": d_conv_b, "ln2_g": d_ln2_g, "ln2_b": d_ln2_b, "conv_w": d_conv_w, "loss": loss8[0, :1]})
    own_out, own_small, oth_out, oth_small = _pair_reduce(
        [dw_out.reshape(N_DEV, ROWS_OUT, D), small.reshape(N_DEV, SMALL_ROWS // N_DEV, 128)], "pair_reduce_out")
    dw_in_t, arr_out, arr_small = _weight_grad(dproj, xb, "grad_w_in", tm=IN_W // 2, exchange=[oth_out, oth_small])
    arr_up = _exchange_wait(up_sems, up_src, up_land, [dw_in_t], "exchange_up_wait")
    own_in, oth_in = _pair_reduce([dw_in_t.reshape(N_DEV, ROWS_IN, D)], "pair_reduce_in")
    in_sems, in_src, in_land, started = _exchange_start(oth_in, "exchange_in_start", CHIP_BARRIER)
    (small_piece,) = _sum_parts([own_small], [arr_small], "sum_small_grads")
    (gs_small,) = _all_gather([small_piece], "gather_small_grads")

    names = ["w_in", "w_pool", "pool_scale", "w_out", "ln1_g", "ln1_b", "w_up", "conv_w", "conv_b", "w_down",
             "ln2_g", "ln2_b"]
    w_d = dict(w_in=w_in, w_pool=w_pool, pool_scale=pool_scale, w_out=w_out, ln1_g=ln1_g, ln1_b=ln1_b, w_up=w_up,
               conv_w=conv_w, conv_b=conv_b, w_down=w_down, ln2_g=ln2_g, ln2_b=ln2_b)
    m_d = dict(w_in=m_w_in, w_pool=m_w_pool, pool_scale=m_pool_scale, w_out=m_w_out, ln1_g=m_ln1_g, ln1_b=m_ln1_b,
               w_up=m_w_up, conv_w=m_conv_w, conv_b=m_conv_b, w_down=m_w_down, ln2_g=m_ln2_g, ln2_b=m_ln2_b)
    v_d = dict(w_in=v_w_in, w_pool=v_w_pool, pool_scale=v_pool_scale, w_out=v_w_out, ln1_g=v_ln1_g, ln1_b=v_ln1_b,
               w_up=v_w_up, conv_w=v_conv_w, conv_b=v_conv_b, w_down=v_w_down, ln2_g=v_ln2_g, ln2_b=v_ln2_b)
    g_d, delta, new_m, new_v = {}, {}, {}, {}

    def big_adamw(k, own, arr, transposed, steps, after=()):
        lay = (lambda a: a[0].T) if transposed else (lambda a: a[0])
        back = (lambda a: a.T[None]) if transposed else (lambda a: a[None])
        res = _sum_adamw(own, arr, lay(w_d[k]), lay(m_d[k]), lay(v_d[k]), "adamw_" + k, steps, after)
        g_d[k], delta[k], new_m[k], new_v[k] = (back(r) for r in res)
        return res[3]

    done = [big_adamw("w_up", own_up, arr_up, True, 4, after=(started,)),
            big_adamw("w_down", own_down, arr_down, False, 2, after=(started,)),
            big_adamw("w_out", own_out, arr_out, False, 2, after=(started,))]

    gsm = _unpack(gs_small)
    gsm["conv_w"] = lax.dynamic_slice(gsm["conv_w"].reshape(3, D_FF), (0, me * (D_FF // N_DEV)), (3, D_FF // N_DEV))
    lay = lambda k, a: jnp.transpose(a, (1, 0, 2)) if k == "conv_w" else a.reshape(-1, a.shape[-1])
    back = lambda k, a: jnp.transpose(a, (1, 0, 2)) if k == "conv_w" else a.reshape(w_d[k].shape)
    group = [k for k in names if k not in ("w_in", "w_out", "w_up", "w_down")]
    for k in group:
        g_d[k] = gsm[k].reshape(w_d[k].shape)
    res = _adamw([lay(k, w_d[k]) for k in group], [lay(k, g_d[k]) for k in group], [lay(k, m_d[k]) for k in group],
                 [lay(k, v_d[k]) for k in group], "adamw_small")
    for j, k in enumerate(group):
        delta[k], new_m[k], new_v[k] = (back(k, res[part * len(group) + j]) for part in range(3))

    arr_in = _exchange_wait(in_sems, in_src, in_land, done + [res[0]], "exchange_in_wait")
    big_adamw("w_in", own_in, arr_in, True, 4)

    loss = gsm["loss"].reshape(())
    return (loss, grad_x[None], *[g_d[k] for k in names], *[delta[k] for k in names], *[new_m[k] for k in names],
            *[new_v[k] for k in names])
```

```python
import math

import numpy as np
import jax
import jax.numpy as jnp
from jax import lax
from jax.experimental import pallas as pl
from jax.experimental.pallas import tpu as pltpu

f32 = jnp.float32
bf16 = jnp.bfloat16

N_DEV = 8
T = 4096
D = 1024
CHUNK = 64
MIX_TILE = 512
RET_TILE = 256
HEADS = 4
DH = 128
RW = HEADS * DH
PW = 512
GROUPS = 4
WINDOWS = (2, 4, 8, 16)
IN_W = 4 * RW + PW
D_FF = 2816
LN_EPS = 1e-5
RMS_EPS = 1e-6
ALPHA = 2.0 ** 0.25
K_SCALE = DH ** -0.5

ADAM_LR = 0.001
ADAM_B1 = 0.9
ADAM_B2 = 0.999
ADAM_EPS = 1e-08
ADAM_WD = 0.01
ADAM_STEP = 10

ROWS_IN, ROWS_OUT, ROWS_UP, ROWS_DOWN = IN_W // N_DEV, D // N_DEV, 2 * D_FF // N_DEV, D_FF // N_DEV

V7X_VMEM_LIMIT = 56 * 2 ** 20
HALO = 32

NT = (((1,), (1,)), ((), ()))
TN = (((0,), (0,)), ((), ()))
NN = (((1,), (0,)), ((), ()))


def _dot(a, b, dims=NN):
    return lax.dot_general(a, b, dims, preferred_element_type=f32)


def _const_spec(shape):
    zeros = (0,) * len(shape)
    return pl.BlockSpec(shape, lambda i: zeros, pipeline_mode=pl.Buffered(1))


def _sigmoid(x):
    return 0.5 * jnp.tanh(0.5 * x) + 0.5


def _decay_tables(tt):
    h = np.arange(HEADS, dtype=np.float64)
    log_gamma = np.log(1.0 - 2.0 ** (-5.0 - h)).astype(np.float32).astype(np.float64)[:, None, None]
    idx = np.arange(tt, dtype=np.float64)
    visible = (idx[None, :] // CHUNK) <= (idx[:, None] // CHUNK)
    mask = np.where(visible[None], np.exp(log_gamma * np.abs(idx[:, None] - idx[None, :])[None]), 0.0)
    qd = np.broadcast_to(np.exp(log_gamma * (idx[None, :, None] + 1.0)), (HEADS, tt, DH))
    kd = np.broadcast_to(np.exp(log_gamma * (tt - 1.0 - idx[None, :, None])), (HEADS, tt, DH))
    cd = np.exp(log_gamma[:, 0, 0] * tt)
    return (jnp.asarray(mask, f32), jnp.asarray(qd, f32), jnp.asarray(kd, f32), [float(c) for c in cd])


def _rope_tables():
    inv_freq = (10000.0 ** (-np.arange(0, DH, 2, dtype=np.float64) / DH)).astype(np.float32)
    ang = (np.arange(T, dtype=np.float32)[:, None] * inv_freq[None, :]).astype(np.float64)
    cos, sin = np.cos(ang), np.sin(ang)
    return (jnp.asarray(np.concatenate([cos, cos], axis=1), f32), jnp.asarray(np.concatenate([-sin, sin], axis=1), f32))


def _swap_halves(t):
    return pltpu.roll(t, DH // 2, axis=1)


def _mix_forward(x, w_in_shard, w_out_shard, cos, sin, dmat, qd, kd, cdec, w_pool, pool_scale, ln1_g, ln1_b,
                 gather_bf16, gather, tt=MIX_TILE):
    n_tiles = T // tt
    to_bf16 = [w_in_shard, w_out_shard] + list(gather_bf16)
    n_c, n_g = len(to_bf16), len(gather_bf16) + len(gather)

    def body(x_ref, cos_ref, sin_ref, dmat_ref, qd_ref, kd_ref, wpool_ref, pscale_ref, g1_ref, b1_ref, *rest):
        f32_in, plain_in, rest = rest[:n_c], rest[n_c:2 + n_g], rest[2 + n_g:]
        qkv_ref, g_ref, oret_ref, states_ref, cat_ref, pooled_ref, xhat_ref, rstd_ref, x1b_ref, xb_ref = rest[:10]
        fout, gout = rest[10:12], rest[12:12 + n_g]
        state_s, pext_s, tmp_s, wint_s, wout_s, load_sems, stage_sems, *rest = rest[12 + n_g:]
        stage_s, cast_s, sems = rest[:n_c], rest[n_c:2 * n_c], rest[2 * n_c:]
        fin, gin, fsems, gsems = cast_s[:2], tuple(cast_s[2:]) + tuple(plain_in), sems[:3], sems[3:]
        i = pl.program_id(0)

        @pl.when(i == 0)
        def _():
            stage = [pltpu.make_async_copy(src, dst, stage_sems.at[j]) for j, (src, dst) in enumerate(zip(f32_in, stage_s))]
            for cp in stage:
                cp.start()
            state_s[...] = jnp.zeros_like(state_s)
            pext_s[:, pl.ds(0, HALO), :] = jnp.zeros((GROUPS, HALO, DH), f32)

            def cast(js):
                for j in js:
                    stage[j].wait()
                    cast_s[j][...] = stage_s[j][...].astype(bf16)

            _barrier(_gather_peers())
            cast(range(2))
            _gather_start(fin, fout, *fsems, barrier=False)
            cast(range(2, n_c))
            _gather_forward(fin, fout, *fsems)
            _gather_start(gin, gout, *gsems, barrier=False)
            _gather_finish(fin, fout, *fsems)
            loads = [pltpu.make_async_copy(src.at[s], dst.at[pl.ds(s * src.shape[1], src.shape[1]), :],
                                           load_sems.at[j, s])
                     for j, (src, dst) in enumerate(((fout[0], wint_s), (fout[1], wout_s))) for s in range(N_DEV)]
            for ld in loads:
                ld.start()
            for ld in loads:
                ld.wait()

        @pl.when(i == n_tiles - 3)
        def _():
            _gather_forward(gin, gout, *gsems)

        xb = x_ref[...].astype(bf16)
        xb_ref[...] = xb
        cos_t, sin_t = cos_ref[...], sin_ref[...]
        for part in range(2):
            pr = _dot(xb, wint_s[pl.ds(part * RW, RW), :], NT)
            for h in range(HEADS):
                t = pr[:, h * DH:(h + 1) * DH]
                r = t * cos_t + _swap_halves(t) * sin_t
                if part == 1:
                    r = r * K_SCALE
                qkv_ref[:, part * RW + h * DH: part * RW + (h + 1) * DH] = r.astype(bf16)
        qkv_ref[:, 2 * RW:3 * RW] = _dot(xb, wint_s[pl.ds(2 * RW, RW), :], NT).astype(bf16)
        g_ref[...] = _dot(xb, wint_s[pl.ds(3 * RW, RW), :], NT)
        p = _dot(xb, wint_s[pl.ds(4 * RW, PW), :], NT)
        for gi in range(GROUPS):
            pext_s[gi, pl.ds(HALO, tt), :] = p[:, gi * DH:(gi + 1) * DH]

        for sub in range(tt // RET_TILE):
            rows = pl.ds(sub * RET_TILE, RET_TILE)
            for h in range(HEADS):
                q = qkv_ref[rows, h * DH:(h + 1) * DH]
                k = qkv_ref[rows, RW + h * DH: RW + (h + 1) * DH]
                v = qkv_ref[rows, 2 * RW + h * DH: 2 * RW + (h + 1) * DH]
                s = _dot(q, k, NT) * dmat_ref[h]
                st = state_s[h]
                stb = st.astype(bf16)
                states_ref[sub, h] = stb
                oret_ref[rows, h * DH:(h + 1) * DH] = (_dot(s.astype(bf16), v)
                                                      + _dot((q.astype(f32) * qd_ref[h]).astype(bf16), stb))
                state_s[h] = st * cdec[h] + _dot((k.astype(f32) * kd_ref[h]).astype(bf16), v, TN)

        for h in range(HEADS):
            sl = slice(h * DH, (h + 1) * DH)
            o = oret_ref[:, sl]
            r = lax.rsqrt(jnp.mean(o * o, axis=-1, keepdims=True) + RMS_EPS)
            gg = g_ref[:, sl]
            cat_ref[:, sl] = (o * r * (gg * _sigmoid(gg))).astype(bf16)

        pos1 = (i * tt + lax.broadcasted_iota(jnp.int32, (tt, 1), 0) + 1).astype(f32)
        for gi, w in enumerate(WINDOWS):
            sl = slice(gi * DH, (gi + 1) * DH)
            stages = int(math.log2(w))
            src = pext_s
            for s in range(stages):
                lo = HALO - 8 * (stages - 1 - s)
                n = tt + HALO - lo
                shift = 2 ** s
                val = src[gi, pl.ds(lo, n), :] + src[gi, pl.ds(lo - shift, n), :]
                if s == stages - 1:
                    wsum = val
                else:
                    tmp_s[gi, pl.ds(lo, n), :] = val
                    src = tmp_s
            p_g = pext_s[gi, pl.ds(HALO, tt), :]
            pooled = (wsum / jnp.minimum(pos1, float(w)) - p_g).astype(bf16)
            pooled_ref[:, sl] = pooled
            y = _dot(pooled, wpool_ref[gi].astype(bf16)) * pscale_ref[:, sl]
            cat_ref[:, RW + gi * DH: RW + (gi + 1) * DH] = y.astype(bf16)
        pext_s[:, pl.ds(0, HALO), :] = pext_s[:, pl.ds(tt, HALO), :]

        z = ALPHA * x_ref[...] + _dot(cat_ref[...], wout_s[...])
        mu = jnp.mean(z, axis=-1, keepdims=True)
        zc = z - mu
        rstd = lax.rsqrt(jnp.mean(zc * zc, axis=-1, keepdims=True) + LN_EPS)
        xhat = zc * rstd
        xhat_ref[...] = xhat
        rstd_ref[...] = rstd
        x1b_ref[...] = (xhat * g1_ref[...] + b1_ref[...]).astype(bf16)

        @pl.when(i == n_tiles - 1)
        def _():
            _gather_finish(gin, gout, *gsems)

    tile = lambda w: pl.BlockSpec((tt, w), lambda i: (i, 0))
    hbm = pl.BlockSpec(memory_space=pltpu.HBM)
    out_shape = (
        jax.ShapeDtypeStruct((T, 3 * RW), bf16),
        jax.ShapeDtypeStruct((T, RW), f32),
        jax.ShapeDtypeStruct((T, RW), f32),
        jax.ShapeDtypeStruct((T // RET_TILE, HEADS, DH, DH), bf16),
        jax.ShapeDtypeStruct((T, D), bf16),
        jax.ShapeDtypeStruct((T, PW), bf16),
        jax.ShapeDtypeStruct((T, D), f32),
        jax.ShapeDtypeStruct((T, 1), f32),
        jax.ShapeDtypeStruct((T, D), bf16),
        jax.ShapeDtypeStruct((T, D), bf16),
    ) + tuple(jax.ShapeDtypeStruct((N_DEV,) + b.shape, bf16) for b in to_bf16
              ) + tuple(jax.ShapeDtypeStruct((N_DEV,) + b.shape, b.dtype) for b in gather)
    return pl.pallas_call(
        body, name="mix_forward", grid=(n_tiles,), out_shape=out_shape,
        in_specs=[tile(D), tile(DH), tile(DH),
                  _const_spec((HEADS, RET_TILE, RET_TILE)), _const_spec((HEADS, RET_TILE, DH)),
                  _const_spec((HEADS, RET_TILE, DH)),
                  _const_spec((GROUPS, DH, DH)), _const_spec((1, PW)),
                  _const_spec((1, D)), _const_spec((1, D))] + [hbm] * (2 + n_g),
        out_specs=(tile(3 * RW), tile(RW), tile(RW),
                   pl.BlockSpec((tt // RET_TILE, HEADS, DH, DH), lambda i: (i, 0, 0, 0)),
                   tile(D), tile(PW), tile(D), tile(1), tile(D), tile(D)) + (hbm,) * (2 + n_g),
        scratch_shapes=[pltpu.VMEM((HEADS, DH, DH), f32), pltpu.VMEM((GROUPS, tt + HALO, DH), f32),
                        pltpu.VMEM((GROUPS, tt + HALO, DH), f32), pltpu.VMEM((IN_W, D), bf16), pltpu.VMEM((D, D), bf16),
                        pltpu.SemaphoreType.DMA((2, N_DEV)), pltpu.SemaphoreType.DMA((n_c,))]
        + [pltpu.VMEM(b.shape, f32) for b in to_bf16] + [pltpu.VMEM(b.shape, bf16) for b in to_bf16]
        + _gather_sems(2) + _gather_sems(n_g),
        compiler_params=pltpu.CompilerParams(dimension_semantics=("arbitrary",), vmem_limit_bytes=V7X_VMEM_LIMIT,
                                             collective_id=GATHER_BARRIER),
    )(x, cos, sin, dmat, qd, kd, w_pool, pool_scale, ln1_g, ln1_b, *to_bf16, *gather)


def _ffn_forward_backward(xhat1, rstd1, ln1_g, ln1_b, w_up_t, conv_w, conv_b, w_down, ln2_g, ln2_b, target,
                          tt=256):
    n_tiles = T // tt
    FH = 16
    hb = tt // FH

    def body(xhat_ref, halo_ref, rstd_ref, g1_ref, b1_ref, wupt_ref, cw_ref, cb_ref, wdown_ref, g2_ref, b2_ref, tgt_ref,
             dz1_ref, dz2b_ref, du_ref, f_ref, small_ref,
             gext_s, val_s, dhext_s, loss_ref, dg2_ref, db2_ref, dg1_ref, db1_ref, dcb_ref, dcw_ref):
        i = pl.program_id(0)
        tile_idx = n_tiles - 1 - i

        def rd(ref, off):
            return jnp.concatenate([ref[k, pl.ds(off, tt), :] for k in range(D_FF // 128)], axis=1)

        def wr(ref, val):
            for k in range(D_FF // 128):
                ref[k, pl.ds(0, val.shape[0]), :] = val[:, k * 128:(k + 1) * 128]

        @pl.when(i == 0)
        def _():
            for r in (loss_ref, dg2_ref, db2_ref, dg1_ref, db1_ref, dcb_ref, dcw_ref):
                r[...] = jnp.zeros_like(r)
            dhext_s[:, pl.ds(tt, 8), :] = jnp.zeros((D_FF // 128, 8, 128), f32)

        g1, b1 = g1_ref[...], b1_ref[...]
        xhat = xhat_ref[...]
        x1 = xhat * g1 + b1
        x1b = x1.astype(bf16)
        x1h = ((halo_ref[...] * g1 + b1) * jnp.where(tile_idx == 0, 0.0, 1.0)).astype(bf16)
        x1ext = jnp.concatenate([x1h, x1b], axis=0)

        val = _dot(x1b, wupt_ref[pl.ds(0, D_FF), :], NT)
        gate_ext = _dot(x1ext, wupt_ref[pl.ds(D_FF, D_FF), :], NT)
        wr(gext_s, gate_ext)
        hh = (cb_ref[...] + cw_ref[0:1, :] * rd(gext_s, FH - 2) + cw_ref[1:2, :] * rd(gext_s, FH - 1)
              + cw_ref[2:3, :] * gate_ext[FH:])
        sg = _sigmoid(hh)
        act = hh * sg
        wr(dhext_s, act)
        val_s[...] = val * (sg + act * (1.0 - sg))
        fb = (act * val).astype(bf16)
        f_ref[...] = fb

        z = ALPHA * x1 + _dot(fb, wdown_ref[...])
        mu = jnp.mean(z, axis=-1, keepdims=True)
        zc = z - mu
        rstd2 = lax.rsqrt(jnp.mean(zc * zc, axis=-1, keepdims=True) + LN_EPS)
        xh2 = zc * rstd2
        diff = xh2 * g2_ref[...] + b2_ref[...] - tgt_ref[...]
        loss_ref[...] += 0.5 * jnp.sum(diff * diff) / D
        dy = diff * (1.0 / D)
        dg2_ref[...] += jnp.sum(dy * xh2, axis=0, keepdims=True)
        db2_ref[...] += jnp.sum(dy, axis=0, keepdims=True)
        dyg = dy * g2_ref[...]
        dz2 = rstd2 * (dyg - jnp.mean(dyg, axis=-1, keepdims=True) - xh2 * jnp.mean(dyg * xh2, axis=-1, keepdims=True))
        dz2b = dz2.astype(bf16)
        dz2b_ref[...] = dz2b

        df = _dot(dz2b, wdown_ref[...], NT)
        dval = df * rd(dhext_s, 0)
        dh = df * val_s[...]
        wr(dhext_s, dh)
        dh1, dh2, g0 = rd(dhext_s, 1), rd(dhext_s, 2), rd(gext_s, FH)
        dcb_ref[...] += jnp.sum(dh, axis=0, keepdims=True)
        dcw_ref[0:1, :] += jnp.sum(dh2 * g0, axis=0, keepdims=True)
        dcw_ref[1:2, :] += jnp.sum(dh1 * g0, axis=0, keepdims=True)
        dcw_ref[2:3, :] += jnp.sum(dh * g0, axis=0, keepdims=True)
        dgate = cw_ref[2:3, :] * dh + cw_ref[1:2, :] * dh1 + cw_ref[0:1, :] * dh2
        dvalb, dgateb = dval.astype(bf16), dgate.astype(bf16)
        du_ref[:, :D_FF] = dvalb
        du_ref[:, D_FF:] = dgateb
        dx1 = ALPHA * dz2 + _dot(dvalb, wupt_ref[pl.ds(0, D_FF), :]) + _dot(dgateb, wupt_ref[pl.ds(D_FF, D_FF), :])
        dhext_s[:, pl.ds(tt, 8), :] = dhext_s[:, pl.ds(0, 8), :]

        dg1_ref[...] += jnp.sum(dx1 * xhat, axis=0, keepdims=True)
        db1_ref[...] += jnp.sum(dx1, axis=0, keepdims=True)
        dxg = dx1 * g1
        dz1_ref[...] = rstd_ref[...] * (dxg - jnp.mean(dxg, axis=-1, keepdims=True)
                                        - xhat * jnp.mean(dxg * xhat, axis=-1, keepdims=True))

        @pl.when(i == n_tiles - 1)
        def _():
            small_ref[...] = jnp.zeros_like(small_ref)
            at = 0
            for ref in (dg1_ref, db1_ref, dg2_ref, db2_ref, dcb_ref, dcw_ref, loss_ref):
                for r in range(ref.shape[0]):
                    for k in range(ref.shape[1] // 128):
                        small_ref[at:at + 1, :] = ref[r:r + 1, k * 128:(k + 1) * 128]
                        at += 1

    rtile = lambda w: pl.BlockSpec((tt, w), lambda i: (n_tiles - 1 - i, 0))
    out_shape = (
        jax.ShapeDtypeStruct((T, D), f32),
        jax.ShapeDtypeStruct((T, D), bf16),
        jax.ShapeDtypeStruct((T, 2 * D_FF), bf16),
        jax.ShapeDtypeStruct((T, D_FF), bf16),
        jax.ShapeDtypeStruct((SMALL_FFN_ROWS, 128), f32),
    )
    return pl.pallas_call(
        body, name="ffn_forward_backward", grid=(n_tiles,), out_shape=out_shape,
        in_specs=[rtile(D),
                  pl.BlockSpec((FH, D), lambda i: (jnp.maximum((n_tiles - 1 - i) * hb - 1, 0), 0)),
                  rtile(1), _const_spec((1, D)), _const_spec((1, D)), _const_spec((2 * D_FF, D)),
                  _const_spec((3, D_FF)), _const_spec((1, D_FF)), _const_spec((D_FF, D)),
                  _const_spec((1, D)), _const_spec((1, D)), rtile(D)],
        out_specs=(rtile(D), rtile(D), rtile(2 * D_FF), rtile(D_FF),
                   pl.BlockSpec((SMALL_FFN_ROWS, 128), lambda i: (0, 0))),
        scratch_shapes=[pltpu.VMEM((D_FF // 128, tt + FH, 128), f32), pltpu.VMEM((tt, D_FF), f32),
                        pltpu.VMEM((D_FF // 128, tt + 8, 128), f32),
                        pltpu.VMEM((1, 128), f32), pltpu.VMEM((1, D), f32), pltpu.VMEM((1, D), f32),
                        pltpu.VMEM((1, D), f32), pltpu.VMEM((1, D), f32), pltpu.VMEM((1, D_FF), f32),
                        pltpu.VMEM((3, D_FF), f32)],
        compiler_params=pltpu.CompilerParams(dimension_semantics=("arbitrary",), vmem_limit_bytes=V7X_VMEM_LIMIT),
    )(xhat1, xhat1, rstd1, ln1_g, ln1_b, w_up_t, conv_w, conv_b, w_down, ln2_g, ln2_b, target)


def _mix_backward(dz1, w_out, qkv, g, oret, states, pooled, cat, cos, sin, dmat, qd, kd, cdec, w_pool, pool_scale, w_in_t,
                  small_ffn, after, tt=MIX_TILE):
    n_tiles = T // tt

    def body(dz1_ref, wout_ref, qkv_ref, g_ref, oret_ref, states_ref, pooled_ref, cat_ref, cos_ref, sin_ref, dmat_ref,
             qd_ref, kd_ref, wpool_ref, pscale_ref, wint_ref, small_ffn_ref, after_ref,
             dproj_ref, gx_ref, small_ref, dwout_ref, dstate_s, dout_s, eext_s, tmp_s, dwout_s, dpscale_s):
        i = pl.program_id(0)
        tile_idx = n_tiles - 1 - i

        @pl.when(i == 0)
        def _():
            dstate_s[...] = jnp.zeros_like(dstate_s)
            small_ref[...] = jnp.zeros_like(small_ref)
            dpscale_s[...] = jnp.zeros_like(dpscale_s)
            dwout_s[...] = jnp.zeros_like(dwout_s)
            eext_s[:, pl.ds(tt, HALO), :] = jnp.zeros((GROUPS, HALO, DH), f32)

        dz1 = dz1_ref[...]
        dz1b = dz1.astype(bf16)
        dcat = _dot(dz1b, wout_ref[...], NT)
        dwout_s[...] += _dot(cat_ref[...], dz1b, TN)

        pos1 = (tile_idx * tt + lax.broadcasted_iota(jnp.int32, (tt, 1), 0) + 1).astype(f32)
        for gi, w in enumerate(WINDOWS):
            sl = slice(gi * DH, (gi + 1) * DH)
            dpo = dcat[:, RW + gi * DH: RW + (gi + 1) * DH]
            pooled_g = pooled_ref[:, sl]
            wpool_g = wpool_ref[gi].astype(bf16)
            ylin = _dot(pooled_g, wpool_g)
            dpscale_s[:, sl] += jnp.sum(dpo * ylin, axis=0, keepdims=True)
            dpw = (dpo * pscale_ref[:, sl]).astype(bf16)
            small_ref[pl.ds(gi * DH, DH), :] += _dot(pooled_g, dpw, TN)
            dpooled = _dot(dpw, wpool_g, NT)
            eext_s[gi, pl.ds(0, tt), :] = dpooled / jnp.minimum(pos1, float(w))
            stages = int(math.log2(w))
            src = eext_s
            for s in range(stages):
                n = tt + 8 * (stages - 1 - s)
                shift = 2 ** s
                val = src[gi, pl.ds(0, n), :] + src[gi, pl.ds(shift, n), :]
                if s == stages - 1:
                    wsum = val
                else:
                    tmp_s[gi, pl.ds(0, n), :] = val
                    src = tmp_s
            dproj_ref[:, 4 * RW + gi * DH: 4 * RW + (gi + 1) * DH] = (wsum - dpooled).astype(bf16)
        eext_s[:, pl.ds(tt, HALO), :] = eext_s[:, pl.ds(0, HALO), :]

        for h in range(HEADS):
            sl = slice(h * DH, (h + 1) * DH)
            dr = dcat[:, sl]
            o = oret_ref[:, sl]
            r = lax.rsqrt(jnp.mean(o * o, axis=-1, keepdims=True) + RMS_EPS)
            rn = o * r
            gg = g_ref[:, sl]
            sg = _sigmoid(gg)
            dproj_ref[:, 3 * RW + h * DH: 3 * RW + (h + 1) * DH] = (dr * rn * (sg * (1.0 + gg * (1.0 - sg)))).astype(bf16)
            drn = dr * (gg * sg)
            dout_s[:, sl] = (r * (drn - rn * jnp.mean(drn * rn, axis=-1, keepdims=True))).astype(bf16)

        for sub in reversed(range(tt // RET_TILE)):
            rows = pl.ds(sub * RET_TILE, RET_TILE)
            cos_t, sin_t = cos_ref[rows, :], sin_ref[rows, :]
            for h in range(HEADS):
                q = qkv_ref[rows, h * DH:(h + 1) * DH]
                k = qkv_ref[rows, RW + h * DH: RW + (h + 1) * DH]
                v = qkv_ref[rows, 2 * RW + h * DH: 2 * RW + (h + 1) * DH]
                do = dout_s[rows, h * DH:(h + 1) * DH]
                stb = states_ref[sub, h]
                dst = dstate_s[h]
                dstb = dst.astype(bf16)
                sb = (_dot(q, k, NT) * dmat_ref[h]).astype(bf16)
                dsb = (_dot(do, v, NT) * dmat_ref[h]).astype(bf16)
                dq = _dot(dsb, k) + _dot(do, stb, NT) * qd_ref[h]
                dk = _dot(dsb, q, TN) + _dot(v, dstb, NT) * kd_ref[h]
                dv = _dot(sb, do, TN) + _dot((k.astype(f32) * kd_ref[h]).astype(bf16), dstb)
                dstate_s[h] = dst * cdec[h] + _dot((q.astype(f32) * qd_ref[h]).astype(bf16), do, TN)
                dproj_ref[rows, h * DH:(h + 1) * DH] = (dq * cos_t - _swap_halves(dq) * sin_t).astype(bf16)
                dproj_ref[rows, RW + h * DH: RW + (h + 1) * DH] = (
                    (dk * cos_t - _swap_halves(dk) * sin_t) * K_SCALE).astype(bf16)
                dproj_ref[rows, 2 * RW + h * DH: 2 * RW + (h + 1) * DH] = dv.astype(bf16)

        gx_ref[...] = ALPHA * dz1 + _dot(dproj_ref[...], wint_ref[...])

        @pl.when(i == n_tiles - 1)
        def _():
            dwout_ref[...] = dwout_s[...].astype(bf16)
            for k in range(PW // 128):
                small_ref[GROUPS * DH + k: GROUPS * DH + k + 1, :] = dpscale_s[:, k * 128:(k + 1) * 128]
            small_ref[pl.ds(SMALL_FFN_AT, SMALL_FFN_ROWS), :] = small_ffn_ref[...]

    rtile = lambda w: pl.BlockSpec((tt, w), lambda i: (n_tiles - 1 - i, 0))
    out_shape = (
        jax.ShapeDtypeStruct((T, IN_W), bf16),
        jax.ShapeDtypeStruct((T, D), f32),
        jax.ShapeDtypeStruct((SMALL_ROWS, 128), f32),
        jax.ShapeDtypeStruct((D, D), bf16),
    )
    return pl.pallas_call(
        body, name="mix_backward", grid=(n_tiles,), out_shape=out_shape,
        in_specs=[rtile(D), _const_spec((D, D)), rtile(3 * RW), rtile(RW), rtile(RW),
                  pl.BlockSpec((tt // RET_TILE, HEADS, DH, DH), lambda i: (n_tiles - 1 - i, 0, 0, 0)),
                  rtile(PW), rtile(D), rtile(DH), rtile(DH),
                  _const_spec((HEADS, RET_TILE, RET_TILE)), _const_spec((HEADS, RET_TILE, DH)),
                  _const_spec((HEADS, RET_TILE, DH)),
                  _const_spec((GROUPS, DH, DH)), _const_spec((1, PW)), _const_spec((IN_W, D)),
                  _const_spec((SMALL_FFN_ROWS, 128)), pl.BlockSpec(memory_space=pl.ANY)],
        out_specs=(rtile(IN_W), rtile(D), pl.BlockSpec((SMALL_ROWS, 128), lambda i: (0, 0)),
                   pl.BlockSpec((D, D), lambda i: (0, 0), pipeline_mode=pl.Buffered(1))),
        scratch_shapes=[pltpu.VMEM((HEADS, DH, DH), f32), pltpu.VMEM((tt, RW), bf16),
                        pltpu.VMEM((GROUPS, tt + HALO, DH), f32), pltpu.VMEM((GROUPS, tt + HALO, DH), f32),
                        pltpu.VMEM((D, D), f32), pltpu.VMEM((1, PW), f32)],
        compiler_params=pltpu.CompilerParams(dimension_semantics=("arbitrary",), vmem_limit_bytes=V7X_VMEM_LIMIT),
    )(dz1, w_out, qkv, g, oret, states, pooled, cat, cos, sin, dmat, qd, kd, w_pool, pool_scale, w_in_t, small_ffn,
      after)


def _weight_grad(a, b, name, tm, exchange=()):
    m = a.shape[1]
    n_m, n_e = m // tm, len(exchange)

    def body(a_ref, b_ref, *rest):
        ein, o_ref, eout, sems = rest[:n_e], rest[n_e], rest[n_e + 1:2 * n_e + 1], rest[2 * n_e + 1:]
        i = pl.program_id(0)

        if n_e:
            @pl.when(i == 0)
            def _():
                _chip_exchange_start(ein, eout, *sems)

        o_ref[...] = _dot(a_ref[...], b_ref[...].astype(bf16), TN).astype(bf16)

        if n_e:
            @pl.when(i == n_m - 1)
            def _():
                _chip_exchange_finish(ein, eout, *sems)

    hbm = pl.BlockSpec(memory_space=pltpu.HBM)
    return pl.pallas_call(
        body, name=name, grid=(n_m,),
        out_shape=(jax.ShapeDtypeStruct((m, D), bf16),) + tuple(jax.ShapeDtypeStruct(e.shape, e.dtype) for e in exchange),
        in_specs=[pl.BlockSpec((T, tm), lambda i: (0, i)),
                  pl.BlockSpec((T, D), lambda i: (0, 0), pipeline_mode=pl.Buffered(1))] + [hbm] * n_e,
        out_specs=(pl.BlockSpec((tm, D), lambda i: (i, 0)),) + (hbm,) * n_e,
        scratch_shapes=_chip_exchange_sems(n_e),
        compiler_params=pltpu.CompilerParams(dimension_semantics=("arbitrary",), vmem_limit_bytes=V7X_VMEM_LIMIT,
                                             collective_id=CHIP_BARRIER if n_e else None),
    )(a, b, *exchange)


CHIP_FLIPS = ((1, 0), (0, 1), (1, 1))
PAIR_BARRIER, CHIP_BARRIER, GATHER_BARRIER, CHIP_BARRIER_SPLIT = 0, 1, 2, 3


def _barrier(peers):
    sem = pltpu.get_barrier_semaphore()
    for peer in peers:
        pl.semaphore_signal(sem, inc=1, device_id=peer, device_id_type=pl.DeviceIdType.MESH)
    pl.semaphore_wait(sem, len(peers))


def _me():
    return lax.axis_index("x"), lax.axis_index("y"), lax.axis_index("c")


def _chip(me, k):
    x, y, _ = me
    if k == 0:
        return x, y
    fx, fy = CHIP_FLIPS[k - 1]
    return (1 - x if fx else x), (1 - y if fy else y)


def _slot(x, y, c):
    return 4 * x + 2 * y + c


def _remote(src, dst, send_sem, recv_sem, to):
    return pltpu.make_async_remote_copy(src_ref=src, dst_ref=dst, send_sem=send_sem, recv_sem=recv_sem,
                                        device_id=to, device_id_type=pl.DeviceIdType.MESH)


def _gather_sems(n):
    return [pltpu.SemaphoreType.DMA((7, n)), pltpu.SemaphoreType.DMA((7, n)), pltpu.SemaphoreType.DMA((n,))] if n else []


def _gather_copy(k, j, gin, gout, send_sems, recv_sems, sending):
    x, y, c = _me()
    sibling, x_chip, y_chip, d_chip = (x, y, 1 - c), (1 - x, y), (x, 1 - y), (1 - x, 1 - y)
    south = c == 0
    passed_on = (jnp.where(south, 1 - x, x), jnp.where(south, y, 1 - y), c)
    src, to = gin[j], sibling
    if sending:
        block = {0: (x, y, c), 1: (x, y, c), 2: (x, y, c), 3: passed_on, 4: (*x_chip, c), 5: (*y_chip, c), 6: (*d_chip, c)}[k]
        to = {1: (*x_chip, c), 2: (*y_chip, c), 3: (jnp.where(south, x, 1 - x), jnp.where(south, 1 - y, y), c)}.get(k, sibling)
        if k >= 3:
            src = gout[j].at[_slot(*block)]
    else:
        block = {0: sibling, 1: (*x_chip, c), 2: (*y_chip, c), 3: (*d_chip, c), 4: (*x_chip, 1 - c), 5: (*y_chip, 1 - c),
                 6: (*d_chip, 1 - c)}[k]
    return _remote(src, gout[j].at[_slot(*block)], send_sems.at[k, j], recv_sems.at[k, j], to)


def _gather_do(ks, action, gin, gout, send_sems, recv_sems):
    for k in ks:
        for j in range(len(gin)):
            cp = _gather_copy(k, j, gin, gout, send_sems, recv_sems, action != "wait_recv")
            getattr(cp, action)()


def _gather_peers():
    x, y, c = _me()
    return [(x, y, 1 - c), (1 - x, y, c), (x, 1 - y, c)]


def _gather_start(gin, gout, send_sems, recv_sems, local_sems, barrier=True):
    if barrier:
        _barrier(_gather_peers())
    for j in range(len(gin)):
        pltpu.make_async_copy(gin[j], gout[j].at[_slot(*_me())], local_sems.at[j]).start()
    _gather_do((0, 1, 2), "start", gin, gout, send_sems, recv_sems)


def _gather_forward(gin, gout, send_sems, recv_sems, local_sems):
    _gather_do((1, 2), "wait_recv", gin, gout, send_sems, recv_sems)
    _gather_do((3, 4, 5), "start", gin, gout, send_sems, recv_sems)


def _gather_finish(gin, gout, send_sems, recv_sems, local_sems):
    _gather_do((3,), "wait_recv", gin, gout, send_sems, recv_sems)
    _gather_do((6,), "start", gin, gout, send_sems, recv_sems)
    _gather_do((0, 4, 5, 6), "wait_recv", gin, gout, send_sems, recv_sems)
    _gather_do(range(7), "wait_send", gin, gout, send_sems, recv_sems)
    for j in range(len(gin)):
        pltpu.make_async_copy(gin[j], gout[j].at[_slot(*_me())], local_sems.at[j]).wait()


def _all_gather(blocks, name):
    n = len(blocks)

    def body(*refs):
        gin, gout, sems = refs[:n], refs[n:2 * n], refs[2 * n:]
        _gather_start(gin, gout, *sems)
        _gather_forward(gin, gout, *sems)
        _gather_finish(gin, gout, *sems)

    hbm = pl.BlockSpec(memory_space=pltpu.HBM)
    return pl.pallas_call(
        body, name=name,
        out_shape=tuple(jax.ShapeDtypeStruct((N_DEV,) + b.shape, b.dtype) for b in blocks),
        in_specs=[hbm] * n, out_specs=(hbm,) * n, scratch_shapes=_gather_sems(n),
        compiler_params=pltpu.CompilerParams(collective_id=GATHER_BARRIER),
    )(*blocks)


def _pair_reduce(parts, name):
    n = len(parts)

    def body(*refs):
        ins, own, others, landing, mine = (refs[k * n:(k + 1) * n] for k in range(5))
        send_sems, recv_sems, local_sems = refs[5 * n:]
        me = _me()
        x, y, c = me
        sibling = (x, y, 1 - c)
        _barrier([sibling])
        sends, loads = [], []
        for k in range(4):
            for j in range(n):
                cp = _remote(ins[j].at[_slot(*_chip(me, k), 1 - c)], landing[j].at[k], send_sems.at[k, j],
                             recv_sems.at[k, j], sibling)
                cp.start()
                sends.append(cp)
                ld = pltpu.make_async_copy(ins[j].at[_slot(*_chip(me, k), c)], mine[j].at[k], local_sems.at[k, j])
                ld.start()
                loads.append(ld)
        for k in range(4):
            for j in range(n):
                loads[k * n + j].wait()
                _remote(ins[j].at[0], landing[j].at[k], send_sems.at[k, j], recv_sems.at[k, j], sibling).wait_recv()
                total = mine[j][k].astype(f32) + landing[j][k].astype(f32)
                if k == 0:
                    own[j][...] = total.astype(own[j].dtype)
                else:
                    others[j][k - 1] = total.astype(others[j].dtype)
        for cp in sends:
            cp.wait_send()

    vm = pl.BlockSpec(memory_space=pltpu.VMEM)
    return pl.pallas_call(
        body, name=name,
        out_shape=tuple(jax.ShapeDtypeStruct(p.shape[1:], p.dtype) for p in parts)
        + tuple(jax.ShapeDtypeStruct((3,) + p.shape[1:], p.dtype) for p in parts),
        in_specs=[pl.BlockSpec(memory_space=pltpu.HBM)] * n, out_specs=(vm,) * (2 * n),
        scratch_shapes=[pltpu.VMEM((4,) + p.shape[1:], p.dtype) for p in parts] * 2
        + [pltpu.SemaphoreType.DMA((4, n)), pltpu.SemaphoreType.DMA((4, n)), pltpu.SemaphoreType.DMA((4, n))],
        compiler_params=pltpu.CompilerParams(vmem_limit_bytes=V7X_VMEM_LIMIT, collective_id=PAIR_BARRIER),
    )(*parts)


def _chip_exchange_sems(n):
    return [pltpu.SemaphoreType.DMA((3, n)), pltpu.SemaphoreType.DMA((3, n))] if n else []


def _chip_exchange_copy(k, j, ein, eout, send_sems, recv_sems):
    me = _me()
    return _remote(ein[j].at[k - 1], eout[j].at[k - 1], send_sems.at[k - 1, j], recv_sems.at[k - 1, j],
                   (*_chip(me, k), me[2]))


def _chip_peers():
    me = _me()
    return [(*_chip(me, k), me[2]) for k in range(1, 4)]


def _chip_exchange_start(ein, eout, send_sems, recv_sems, barrier=True):
    if barrier:
        _barrier(_chip_peers())
    for k in range(1, 4):
        for j in range(len(ein)):
            _chip_exchange_copy(k, j, ein, eout, send_sems, recv_sems).start()


def _chip_exchange_finish(ein, eout, send_sems, recv_sems):
    for k in range(1, 4):
        for j in range(len(ein)):
            _chip_exchange_copy(k, j, ein, eout, send_sems, recv_sems).wait_recv()
    for k in range(1, 4):
        for j in range(len(ein)):
            _chip_exchange_copy(k, j, ein, eout, send_sems, recv_sems).wait_send()


def _split_copies(src_ref, dst_ref, sems):
    me = _me()
    return [_remote(src_ref.at[k - 1], dst_ref.at[k - 1], sems[k - 1], sems[2 + k], (*_chip(me, k), me[2]))
            for k in range(1, 4)]


def _exchange_start(others, name, barrier_id):
    def body(src_ref, land_ref, *rest):
        sems, token_ref = rest[:6], rest[8]
        _barrier(_chip_peers())
        for copy in _split_copies(src_ref, land_ref, sems):
            copy.start()
        token_ref[...] = jnp.zeros_like(token_ref)

    hbm, sem = pl.BlockSpec(memory_space=pltpu.HBM), pl.BlockSpec(memory_space=pltpu.SEMAPHORE)
    thru = pltpu.HBM(others.shape, others.dtype)
    res = pl.pallas_call(
        body, name=name,
        out_shape=(pltpu.SemaphoreType.DMA(()),) * 6 + (thru, thru, jax.ShapeDtypeStruct((8, 128), f32)),
        in_specs=(hbm, hbm), out_specs=(sem,) * 6 + (hbm, hbm, pl.BlockSpec(memory_space=pltpu.VMEM)),
        input_output_aliases={0: 6, 1: 7},
        compiler_params=pltpu.CompilerParams(has_side_effects=pltpu.SideEffectType.DATAFLOW_SIDE_EFFECTING,
                                             collective_id=barrier_id),
    )(pltpu.with_memory_space_constraint(others, pltpu.HBM),
      pltpu.with_memory_space_constraint(lax.empty(others.shape, others.dtype), pltpu.HBM))
    return res[:6], res[6], res[7], res[8]


def _exchange_wait(sems, src_thru, land_thru, after, name):
    n_after = len(after)

    def body(src_ref, land_ref, *rest):
        for copy in _split_copies(src_ref, land_ref, rest[:6]):
            copy.wait_send()
            copy.wait_recv()

    hbm, sem = pl.BlockSpec(memory_space=pltpu.HBM), pl.BlockSpec(memory_space=pltpu.SEMAPHORE)
    thru = pltpu.HBM(src_thru.shape, src_thru.dtype)
    return pl.pallas_call(
        body, name=name, out_shape=(thru, thru),
        in_specs=(hbm, hbm) + (sem,) * 6 + (pl.BlockSpec(memory_space=pl.ANY),) * n_after, out_specs=(hbm, hbm),
        input_output_aliases={0: 0, 1: 1},
        compiler_params=pltpu.CompilerParams(has_side_effects=pltpu.SideEffectType.DATAFLOW_SIDE_EFFECTING),
    )(src_thru, land_thru, *sems, *after)[1]


def _sum_parts(owns, arrived, name):
    n = len(owns)

    def body(*refs):
        for own, arr, out in zip(refs[:n], refs[n:2 * n], refs[2 * n:]):
            acc = own[...].astype(f32)
            for k in range(3):
                acc = acc + arr[k].astype(f32)
            out[...] = acc

    vm = pl.BlockSpec(memory_space=pltpu.VMEM)
    return pl.pallas_call(
        body, name=name, out_shape=tuple(jax.ShapeDtypeStruct(o.shape, f32) for o in owns),
        in_specs=[vm] * (2 * n), out_specs=(vm,) * n,
        compiler_params=pltpu.CompilerParams(vmem_limit_bytes=V7X_VMEM_LIMIT),
    )(*owns, *arrived)


def _adam_update(w, g, m, v):
    m = ADAM_B1 * m + (1.0 - ADAM_B1) * g
    v = ADAM_B2 * v + (1.0 - ADAM_B2) * (g * g)
    m_hat = m / (1.0 - ADAM_B1 ** ADAM_STEP)
    v_hat = v / (1.0 - ADAM_B2 ** ADAM_STEP)
    return -ADAM_LR * (m_hat / (jnp.sqrt(v_hat) + ADAM_EPS) + ADAM_WD * w), m, v


def _sum_adamw(own, arrived, w, m, v, name, steps, after=()):
    rows = own.shape[0]
    br = rows // steps

    def body(own_ref, arr_ref, w_ref, m_ref, v_ref, *rest):
        g_out, d_out, m_out, v_out = rest[len(after):]
        g = own_ref[...].astype(f32)
        for k in range(3):
            g = g + arr_ref[k].astype(f32)
        g_out[...] = g
        d_out[...], m_out[...], v_out[...] = _adam_update(w_ref[...], g, m_ref[...], v_ref[...])

    blk = pl.BlockSpec((br, D), lambda i: (i, 0))
    return pl.pallas_call(
        body, name=name, grid=(steps,), out_shape=(jax.ShapeDtypeStruct((rows, D), f32),) * 4,
        in_specs=[blk, pl.BlockSpec((3, br, D), lambda i: (0, i, 0)), blk, blk, blk]
        + [pl.BlockSpec(memory_space=pl.ANY)] * len(after), out_specs=(blk,) * 4,
        compiler_params=pltpu.CompilerParams(dimension_semantics=("parallel",), vmem_limit_bytes=V7X_VMEM_LIMIT),
    )(own, arrived, w, m, v, *after)


def _adamw(ws, gs, ms, vs, name):
    n = len(ws)

    def body(*refs):
        w_r, g_r, m_r, v_r = (refs[k * n:(k + 1) * n] for k in range(4))
        d_o, m_o, v_o = (refs[(4 + k) * n:(5 + k) * n] for k in range(3))
        for j in range(n):
            d_o[j][...], m_o[j][...], v_o[j][...] = _adam_update(w_r[j][...], g_r[j][...], m_r[j][...], v_r[j][...])

    vm = pl.BlockSpec(memory_space=pltpu.VMEM)
    shapes = tuple(jax.ShapeDtypeStruct(w.shape, f32) for w in ws)
    return pl.pallas_call(
        body, name=name, out_shape=shapes * 3, in_specs=[vm] * (4 * n), out_specs=tuple([vm] * (3 * n)),
        compiler_params=pltpu.CompilerParams(vmem_limit_bytes=V7X_VMEM_LIMIT),
    )(*ws, *gs, *ms, *vs)


SMALL_FFN = (("ln1_g", D), ("ln1_b", D), ("ln2_g", D), ("ln2_b", D), ("conv_b", D_FF), ("conv_w", 3 * D_FF), ("loss", 128))
SMALL_FFN_ROWS = 128
SMALL_FFN_AT = 520
SMALL_ROWS = 704


def _unpack(packed):
    pieces, at = [("w_pool", 0, GROUPS * DH * DH), ("pool_scale", GROUPS * DH, PW)], SMALL_FFN_AT
    for k, size in SMALL_FFN:
        pieces.append((k, at, size))
        at += size // 128
    return {k: packed[row:row + size // 128] for k, row, size in pieces}


def kernel(x, w_in, w_pool, pool_scale, w_out, ln1_g, ln1_b, w_up, conv_w, conv_b, w_down, ln2_g, ln2_b, loss_target, m_w_in, m_w_pool, m_pool_scale, m_w_out, m_ln1_g, m_ln1_b, m_w_up, m_conv_w, m_conv_b, m_w_down, m_ln2_g, m_ln2_b, v_w_in, v_w_pool, v_pool_scale, v_w_out, v_ln1_g, v_ln1_b, v_w_up, v_conv_w, v_conv_b, v_w_down, v_ln2_g, v_ln2_b):
    me = 4 * lax.axis_index("x") + 2 * lax.axis_index("y") + lax.axis_index("c")
    x2, tgt = x[0], loss_target[0]

    cos, sin = _rope_tables()
    dmat, qd, kd, cdec = _decay_tables(RET_TILE)

    qkv, g, oret, states, cat, pooled, xhat1, rstd1, x1b, xb, g_in, g_out, g_up, g_down, g_cw = _mix_forward(
        x2, w_in[0].T, w_out[0], cos, sin, dmat, qd, kd, cdec, w_pool[0], pool_scale, ln1_g, ln1_b,
        gather_bf16=[w_up[0].T, w_down[0]], gather=[jnp.transpose(conv_w, (1, 0, 2))])
    w_in_t = g_in.reshape(IN_W, D)
    w_out_f = g_out.reshape(D, D)
    w_up_t = g_up.reshape(2 * D_FF, D)
    w_down_f = g_down.reshape(D_FF, D)
    conv_w_f = jnp.transpose(g_cw[:, :, 0, :], (1, 0, 2)).reshape(3, D_FF)
    dz1, dz2b, du, f, small_ffn = _ffn_forward_backward(
        xhat1, rstd1, ln1_g, ln1_b, w_up_t, conv_w_f, conv_b, w_down_f, ln2_g, ln2_b, tgt)

    (dw_down,) = _weight_grad(f, dz2b, "grad_w_down", tm=D_FF // 2)
    own_down, oth_down = _pair_reduce([dw_down.reshape(N_DEV, ROWS_DOWN, D)], "pair_reduce_down")
    dw_up_t, arr_down = _weight_grad(du, x1b, "grad_w_up", tm=D_FF // 2, exchange=[oth_down])
    own_up, oth_up = _pair_reduce([dw_up_t.reshape(N_DEV, ROWS_UP, D)], "pair_reduce_up")
    up_sems, up_src, up_land, up_started = _exchange_start(oth_up, "exchange_up_start", CHIP_BARRIER_SPLIT)
    dproj, grad_x, small, dw_out = _mix_backward(
        dz1, w_out_f, qkv, g, oret, states, pooled, cat, cos, sin, dmat, qd, kd, cdec, w_pool[0], pool_scale, w_in_t,
        small_ffn, after=up_started)
    own_out, own_small, oth_out, oth_small = _pair_reduce(
        [dw_out.reshape(N_DEV, ROWS_OUT, D), small.reshape(N_DEV, SMALL_ROWS // N_DEV, 128)], "pair_reduce_out")
    dw_in_t, arr_out, arr_small = _weight_grad(dproj, xb, "grad_w_in", tm=IN_W // 2, exchange=[oth_out, oth_small])
    arr_up = _exchange_wait(up_sems, up_src, up_land, [dw_in_t], "exchange_up_wait")
    own_in, oth_in = _pair_reduce([dw_in_t.reshape(N_DEV, ROWS_IN, D)], "pair_reduce_in")
    in_sems, in_src, in_land, started = _exchange_start(oth_in, "exchange_in_start", CHIP_BARRIER)
    (small_piece,) = _sum_parts([own_small], [arr_small], "sum_small_grads")
    (gs_small,) = _all_gather([small_piece], "gather_small_grads")

    names = ["w_in", "w_pool", "pool_scale", "w_out", "ln1_g", "ln1_b", "w_up", "conv_w", "conv_b", "w_down",
             "ln2_g", "ln2_b"]
    w_d = dict(w_in=w_in, w_pool=w_pool, pool_scale=pool_scale, w_out=w_out, ln1_g=ln1_g, ln1_b=ln1_b, w_up=w_up,
               conv_w=conv_w, conv_b=conv_b, w_down=w_down, ln2_g=ln2_g, ln2_b=ln2_b)
    m_d = dict(w_in=m_w_in, w_pool=m_w_pool, pool_scale=m_pool_scale, w_out=m_w_out, ln1_g=m_ln1_g, ln1_b=m_ln1_b,
               w_up=m_w_up, conv_w=m_conv_w, conv_b=m_conv_b, w_down=m_w_down, ln2_g=m_ln2_g, ln2_b=m_ln2_b)
    v_d = dict(w_in=v_w_in, w_pool=v_w_pool, pool_scale=v_pool_scale, w_out=v_w_out, ln1_g=v_ln1_g, ln1_b=v_ln1_b,
               w_up=v_w_up, conv_w=v_conv_w, conv_b=v_conv_b, w_down=v_w_down, ln2_g=v_ln2_g, ln2_b=v_ln2_b)
    g_d, delta, new_m, new_v = {}, {}, {}, {}

    def big_adamw(k, own, arr, transposed, steps, after=()):
        lay = (lambda a: a[0].T) if transposed else (lambda a: a[0])
        back = (lambda a: a.T[None]) if transposed else (lambda a: a[None])
        res = _sum_adamw(own, arr, lay(w_d[k]), lay(m_d[k]), lay(v_d[k]), "adamw_" + k, steps, after)
        g_d[k], delta[k], new_m[k], new_v[k] = (back(r) for r in res)
        return res[3]

    done = [big_adamw("w_up", own_up, arr_up, True, 4, after=(started,)),
            big_adamw("w_down", own_down, arr_down, False, 2, after=(started,)),
            big_adamw("w_out", own_out, arr_out, False, 2, after=(started,))]

    gsm = _unpack(gs_small.reshape(SMALL_ROWS, 128))
    gsm["conv_w"] = lax.dynamic_slice(gsm["conv_w"].reshape(3, D_FF), (0, me * (D_FF // N_DEV)), (3, D_FF // N_DEV))
    lay = lambda k, a: jnp.transpose(a, (1, 0, 2)) if k == "conv_w" else a.reshape(-1, a.shape[-1])
    back = lambda k, a: jnp.transpose(a, (1, 0, 2)) if k == "conv_w" else a.reshape(w_d[k].shape)
    group = [k for k in names if k not in ("w_in", "w_out", "w_up", "w_down")]
    for k in group:
        g_d[k] = gsm[k].reshape(w_d[k].shape)
    res = _adamw([lay(k, w_d[k]) for k in group], [lay(k, g_d[k]) for k in group], [lay(k, m_d[k]) for k in group],
                 [lay(k, v_d[k]) for k in group], "adamw_small")
    for j, k in enumerate(group):
        delta[k], new_m[k], new_v[k] = (back(k, res[part * len(group) + j]) for part in range(3))

    arr_in = _exchange_wait(in_sems, in_src, in_land, done + [res[0]], "exchange_in_wait")
    big_adamw("w_in", own_in, arr_in, True, 4)

    loss = gsm["loss"][0, 0]
    return (loss, grad_x[None], *[g_d[k] for k in names], *[delta[k] for k in names], *[new_m[k] for k in names],
            *[new_v[k] for k in names])
```

```python
import math

import numpy as np
import jax
import jax.numpy as jnp
from jax import lax
from jax.experimental import pallas as pl
from jax.experimental.pallas import tpu as pltpu

f32 = jnp.float32
bf16 = jnp.bfloat16

N_DEV = 8
T = 4096
D = 1024
CHUNK = 64
MIX_TILE = 512
RET_TILE = 256
HEADS = 4
DH = 128
RW = HEADS * DH
PW = 512
GROUPS = 4
WINDOWS = (2, 4, 8, 16)
IN_W = 4 * RW + PW
D_FF = 2816
LN_EPS = 1e-5
RMS_EPS = 1e-6
ALPHA = 2.0 ** 0.25
K_SCALE = DH ** -0.5

ADAM_LR = 0.001
ADAM_B1 = 0.9
ADAM_B2 = 0.999
ADAM_EPS = 1e-08
ADAM_WD = 0.01
ADAM_STEP = 10

ROWS_IN, ROWS_OUT, ROWS_UP, ROWS_DOWN = IN_W // N_DEV, D // N_DEV, 2 * D_FF // N_DEV, D_FF // N_DEV

V7X_VMEM_LIMIT = 56 * 2 ** 20
HALO = 32

NT = (((1,), (1,)), ((), ()))
TN = (((0,), (0,)), ((), ()))
NN = (((1,), (0,)), ((), ()))


def _dot(a, b, dims=NN):
    return lax.dot_general(a, b, dims, preferred_element_type=f32)


def _const_spec(shape):
    zeros = (0,) * len(shape)
    return pl.BlockSpec(shape, lambda i: zeros, pipeline_mode=pl.Buffered(1))


def _sigmoid(x):
    return 0.5 * jnp.tanh(0.5 * x) + 0.5


def _decay_tables(tt):
    h = np.arange(HEADS, dtype=np.float64)
    log_gamma = np.log(1.0 - 2.0 ** (-5.0 - h)).astype(np.float32).astype(np.float64)[:, None, None]
    idx = np.arange(tt, dtype=np.float64)
    visible = (idx[None, :] // CHUNK) <= (idx[:, None] // CHUNK)
    mask = np.where(visible[None], np.exp(log_gamma * np.abs(idx[:, None] - idx[None, :])[None]), 0.0)
    qd = np.broadcast_to(np.exp(log_gamma * (idx[None, :, None] + 1.0)), (HEADS, tt, DH))
    kd = np.broadcast_to(np.exp(log_gamma * (tt - 1.0 - idx[None, :, None])), (HEADS, tt, DH))
    cd = np.exp(log_gamma[:, 0, 0] * tt)
    return (jnp.asarray(mask, f32), jnp.asarray(qd, f32), jnp.asarray(kd, f32), [float(c) for c in cd])


def _rope_tables():
    inv_freq = (10000.0 ** (-np.arange(0, DH, 2, dtype=np.float64) / DH)).astype(np.float32)
    ang = (np.arange(T, dtype=np.float32)[:, None] * inv_freq[None, :]).astype(np.float64)
    cos, sin = np.cos(ang), np.sin(ang)
    return (jnp.asarray(np.concatenate([cos, cos], axis=1), f32), jnp.asarray(np.concatenate([-sin, sin], axis=1), f32))


def _swap_halves(t):
    return pltpu.roll(t, DH // 2, axis=1)


def _mix_forward(x, w_in_shard, w_out_shard, cos, sin, dmat, qd, kd, cdec, w_pool, pool_scale, ln1_g, ln1_b,
                 gather_bf16, gather, tt=MIX_TILE):
    n_tiles = T // tt
    to_bf16 = [w_in_shard, w_out_shard] + list(gather_bf16)
    n_c, n_g = len(to_bf16), len(gather_bf16) + len(gather)

    def body(x_ref, cos_ref, sin_ref, dmat_ref, qd_ref, kd_ref, wpool_ref, pscale_ref, g1_ref, b1_ref, *rest):
        f32_in, plain_in, rest = rest[:n_c], rest[n_c:2 + n_g], rest[2 + n_g:]
        qkv_ref, g_ref, oret_ref, states_ref, cat_ref, pooled_ref, xhat_ref, rstd_ref, x1b_ref, xb_ref = rest[:10]
        fout, gout = rest[10:12], rest[12:12 + n_g]
        state_s, pext_s, tmp_s, wint_s, wout_s, load_sems, stage_sems, *rest = rest[12 + n_g:]
        stage_s, cast_s, sems = rest[:n_c], rest[n_c:2 * n_c], rest[2 * n_c:]
        fin, gin, fsems, gsems = cast_s[:2], tuple(cast_s[2:]) + tuple(plain_in), sems[:3], sems[3:]
        i = pl.program_id(0)

        @pl.when(i == 0)
        def _():
            stage = [pltpu.make_async_copy(src, dst, stage_sems.at[j]) for j, (src, dst) in enumerate(zip(f32_in, stage_s))]
            for cp in stage:
                cp.start()
            state_s[...] = jnp.zeros_like(state_s)
            pext_s[:, pl.ds(0, HALO), :] = jnp.zeros((GROUPS, HALO, DH), f32)

            def cast(js):
                for j in js:
                    stage[j].wait()
                    cast_s[j][...] = stage_s[j][...].astype(bf16)

            _barrier(_gather_peers())
            cast(range(2))
            _gather_start(fin, fout, *fsems, barrier=False)
            cast(range(2, n_c))
            _gather_forward(fin, fout, *fsems)
            _gather_start(gin, gout, *gsems, barrier=False)
            _gather_finish(fin, fout, *fsems)
            loads = [pltpu.make_async_copy(src.at[s], dst.at[pl.ds(s * src.shape[1], src.shape[1]), :],
                                           load_sems.at[j, s])
                     for j, (src, dst) in enumerate(((fout[0], wint_s), (fout[1], wout_s))) for s in range(N_DEV)]
            for ld in loads:
                ld.start()
            for ld in loads:
                ld.wait()

        @pl.when(i == n_tiles - 3)
        def _():
            _gather_forward(gin, gout, *gsems)

        xb = x_ref[...].astype(bf16)
        xb_ref[...] = xb
        cos_t, sin_t = cos_ref[...], sin_ref[...]
        for part in range(2):
            pr = _dot(xb, wint_s[pl.ds(part * RW, RW), :], NT)
            for h in range(HEADS):
                t = pr[:, h * DH:(h + 1) * DH]
                r = t * cos_t + _swap_halves(t) * sin_t
                if part == 1:
                    r = r * K_SCALE
                qkv_ref[:, part * RW + h * DH: part * RW + (h + 1) * DH] = r.astype(bf16)
        qkv_ref[:, 2 * RW:3 * RW] = _dot(xb, wint_s[pl.ds(2 * RW, RW), :], NT).astype(bf16)
        g_ref[...] = _dot(xb, wint_s[pl.ds(3 * RW, RW), :], NT)
        p = _dot(xb, wint_s[pl.ds(4 * RW, PW), :], NT)
        for gi in range(GROUPS):
            pext_s[gi, pl.ds(HALO, tt), :] = p[:, gi * DH:(gi + 1) * DH]

        for sub in range(tt // RET_TILE):
            rows = pl.ds(sub * RET_TILE, RET_TILE)
            for h in range(HEADS):
                q = qkv_ref[rows, h * DH:(h + 1) * DH]
                k = qkv_ref[rows, RW + h * DH: RW + (h + 1) * DH]
                v = qkv_ref[rows, 2 * RW + h * DH: 2 * RW + (h + 1) * DH]
                s = _dot(q, k, NT) * dmat_ref[h]
                st = state_s[h]
                stb = st.astype(bf16)
                states_ref[sub, h] = stb
                oret_ref[rows, h * DH:(h + 1) * DH] = (_dot(s.astype(bf16), v)
                                                      + _dot((q.astype(f32) * qd_ref[h]).astype(bf16), stb))
                state_s[h] = st * cdec[h] + _dot((k.astype(f32) * kd_ref[h]).astype(bf16), v, TN)

        for h in range(HEADS):
            sl = slice(h * DH, (h + 1) * DH)
            o = oret_ref[:, sl]
            r = lax.rsqrt(jnp.mean(o * o, axis=-1, keepdims=True) + RMS_EPS)
            gg = g_ref[:, sl]
            cat_ref[:, sl] = (o * r * (gg * _sigmoid(gg))).astype(bf16)

        pos1 = (i * tt + lax.broadcasted_iota(jnp.int32, (tt, 1), 0) + 1).astype(f32)
        for gi, w in enumerate(WINDOWS):
            sl = slice(gi * DH, (gi + 1) * DH)
            stages = int(math.log2(w))
            src = pext_s
            for s in range(stages):
                lo = HALO - 8 * (stages - 1 - s)
                n = tt + HALO - lo
                shift = 2 ** s
                val = src[gi, pl.ds(lo, n), :] + src[gi, pl.ds(lo - shift, n), :]
                if s == stages - 1:
                    wsum = val
                else:
                    tmp_s[gi, pl.ds(lo, n), :] = val
                    src = tmp_s
            p_g = pext_s[gi, pl.ds(HALO, tt), :]
            pooled = (wsum / jnp.minimum(pos1, float(w)) - p_g).astype(bf16)
            pooled_ref[:, sl] = pooled
            y = _dot(pooled, wpool_ref[gi].astype(bf16)) * pscale_ref[:, sl]
            cat_ref[:, RW + gi * DH: RW + (gi + 1) * DH] = y.astype(bf16)
        pext_s[:, pl.ds(0, HALO), :] = pext_s[:, pl.ds(tt, HALO), :]

        z = ALPHA * x_ref[...] + _dot(cat_ref[...], wout_s[...])
        mu = jnp.mean(z, axis=-1, keepdims=True)
        zc = z - mu
        rstd = lax.rsqrt(jnp.mean(zc * zc, axis=-1, keepdims=True) + LN_EPS)
        xhat = zc * rstd
        xhat_ref[...] = xhat
        rstd_ref[...] = rstd
        x1b_ref[...] = (xhat * g1_ref[...] + b1_ref[...]).astype(bf16)

        @pl.when(i == n_tiles - 1)
        def _():
            _gather_finish(gin, gout, *gsems)

    tile = lambda w: pl.BlockSpec((tt, w), lambda i: (i, 0))
    hbm = pl.BlockSpec(memory_space=pltpu.HBM)
    out_shape = (
        jax.ShapeDtypeStruct((T, 3 * RW), bf16),
        jax.ShapeDtypeStruct((T, RW), f32),
        jax.ShapeDtypeStruct((T, RW), f32),
        jax.ShapeDtypeStruct((T // RET_TILE, HEADS, DH, DH), bf16),
        jax.ShapeDtypeStruct((T, D), bf16),
        jax.ShapeDtypeStruct((T, PW), bf16),
        jax.ShapeDtypeStruct((T, D), f32),
        jax.ShapeDtypeStruct((T, 1), f32),
        jax.ShapeDtypeStruct((T, D), bf16),
        jax.ShapeDtypeStruct((T, D), bf16),
    ) + tuple(jax.ShapeDtypeStruct((N_DEV,) + b.shape, bf16) for b in to_bf16
              ) + tuple(jax.ShapeDtypeStruct((N_DEV,) + b.shape, b.dtype) for b in gather)
    return pl.pallas_call(
        body, name="mix_forward", grid=(n_tiles,), out_shape=out_shape,
        in_specs=[tile(D), tile(DH), tile(DH),
                  _const_spec((HEADS, RET_TILE, RET_TILE)), _const_spec((HEADS, RET_TILE, DH)),
                  _const_spec((HEADS, RET_TILE, DH)),
                  _const_spec((GROUPS, DH, DH)), _const_spec((1, PW)),
                  _const_spec((1, D)), _const_spec((1, D))] + [hbm] * (2 + n_g),
        out_specs=(tile(3 * RW), tile(RW), tile(RW),
                   pl.BlockSpec((tt // RET_TILE, HEADS, DH, DH), lambda i: (i, 0, 0, 0)),
                   tile(D), tile(PW), tile(D), tile(1), tile(D), tile(D)) + (hbm,) * (2 + n_g),
        scratch_shapes=[pltpu.VMEM((HEADS, DH, DH), f32), pltpu.VMEM((GROUPS, tt + HALO, DH), f32),
                        pltpu.VMEM((GROUPS, tt + HALO, DH), f32), pltpu.VMEM((IN_W, D), bf16), pltpu.VMEM((D, D), bf16),
                        pltpu.SemaphoreType.DMA((2, N_DEV)), pltpu.SemaphoreType.DMA((n_c,))]
        + [pltpu.VMEM(b.shape, f32) for b in to_bf16] + [pltpu.VMEM(b.shape, bf16) for b in to_bf16]
        + _gather_sems(2) + _gather_sems(n_g),
        compiler_params=pltpu.CompilerParams(dimension_semantics=("arbitrary",), vmem_limit_bytes=V7X_VMEM_LIMIT,
                                             collective_id=GATHER_BARRIER),
    )(x, cos, sin, dmat, qd, kd, w_pool, pool_scale, ln1_g, ln1_b, *to_bf16, *gather)


def _ffn_forward_backward(xhat1, rstd1, ln1_g, ln1_b, w_up_t, conv_w, conv_b, w_down, ln2_g, ln2_b, target,
                          tt=256):
    n_tiles = T // tt
    FH = 16
    hb = tt // FH

    def body(xhat_ref, halo_ref, rstd_ref, g1_ref, b1_ref, wupt_ref, cw_ref, cb_ref, wdown_ref, g2_ref, b2_ref, tgt_ref,
             dz1_ref, dz2b_ref, du_ref, f_ref, loss_ref, dg2_ref, db2_ref, dg1_ref, db1_ref, dcb_ref, dcw_ref,
             gext_s, val_s, dhext_s):
        i = pl.program_id(0)
        tile_idx = n_tiles - 1 - i

        def rd(ref, off):
            return jnp.concatenate([ref[k, pl.ds(off, tt), :] for k in range(D_FF // 128)], axis=1)

        def wr(ref, val):
            for k in range(D_FF // 128):
                ref[k, pl.ds(0, val.shape[0]), :] = val[:, k * 128:(k + 1) * 128]

        @pl.when(i == 0)
        def _():
            for r in (loss_ref, dg2_ref, db2_ref, dg1_ref, db1_ref, dcb_ref, dcw_ref):
                r[...] = jnp.zeros_like(r)
            dhext_s[:, pl.ds(tt, 8), :] = jnp.zeros((D_FF // 128, 8, 128), f32)

        g1, b1 = g1_ref[...], b1_ref[...]
        xhat = xhat_ref[...]
        x1 = xhat * g1 + b1
        x1b = x1.astype(bf16)
        x1h = ((halo_ref[...] * g1 + b1) * jnp.where(tile_idx == 0, 0.0, 1.0)).astype(bf16)
        x1ext = jnp.concatenate([x1h, x1b], axis=0)

        val = _dot(x1b, wupt_ref[pl.ds(0, D_FF), :], NT)
        gate_ext = _dot(x1ext, wupt_ref[pl.ds(D_FF, D_FF), :], NT)
        wr(gext_s, gate_ext)
        hh = (cb_ref[...] + cw_ref[0:1, :] * rd(gext_s, FH - 2) + cw_ref[1:2, :] * rd(gext_s, FH - 1)
              + cw_ref[2:3, :] * gate_ext[FH:])
        sg = _sigmoid(hh)
        act = hh * sg
        wr(dhext_s, act)
        val_s[...] = val * (sg + act * (1.0 - sg))
        fb = (act * val).astype(bf16)
        f_ref[...] = fb

        z = ALPHA * x1 + _dot(fb, wdown_ref[...])
        mu = jnp.mean(z, axis=-1, keepdims=True)
        zc = z - mu
        rstd2 = lax.rsqrt(jnp.mean(zc * zc, axis=-1, keepdims=True) + LN_EPS)
        xh2 = zc * rstd2
        diff = xh2 * g2_ref[...] + b2_ref[...] - tgt_ref[...]
        loss_ref[...] += 0.5 * jnp.sum(diff * diff) / D
        dy = diff * (1.0 / D)
        dg2_ref[...] += jnp.sum(dy * xh2, axis=0, keepdims=True)
        db2_ref[...] += jnp.sum(dy, axis=0, keepdims=True)
        dyg = dy * g2_ref[...]
        dz2 = rstd2 * (dyg - jnp.mean(dyg, axis=-1, keepdims=True) - xh2 * jnp.mean(dyg * xh2, axis=-1, keepdims=True))
        dz2b = dz2.astype(bf16)
        dz2b_ref[...] = dz2b

        df = _dot(dz2b, wdown_ref[...], NT)
        dval = df * rd(dhext_s, 0)
        dh = df * val_s[...]
        wr(dhext_s, dh)
        dh1, dh2, g0 = rd(dhext_s, 1), rd(dhext_s, 2), rd(gext_s, FH)
        dcb_ref[...] += jnp.sum(dh, axis=0, keepdims=True)
        dcw_ref[0:1, :] += jnp.sum(dh2 * g0, axis=0, keepdims=True)
        dcw_ref[1:2, :] += jnp.sum(dh1 * g0, axis=0, keepdims=True)
        dcw_ref[2:3, :] += jnp.sum(dh * g0, axis=0, keepdims=True)
        dgate = cw_ref[2:3, :] * dh + cw_ref[1:2, :] * dh1 + cw_ref[0:1, :] * dh2
        dvalb, dgateb = dval.astype(bf16), dgate.astype(bf16)
        du_ref[:, :D_FF] = dvalb
        du_ref[:, D_FF:] = dgateb
        dx1 = ALPHA * dz2 + _dot(dvalb, wupt_ref[pl.ds(0, D_FF), :]) + _dot(dgateb, wupt_ref[pl.ds(D_FF, D_FF), :])
        dhext_s[:, pl.ds(tt, 8), :] = dhext_s[:, pl.ds(0, 8), :]

        dg1_ref[...] += jnp.sum(dx1 * xhat, axis=0, keepdims=True)
        db1_ref[...] += jnp.sum(dx1, axis=0, keepdims=True)
        dxg = dx1 * g1
        dz1_ref[...] = rstd_ref[...] * (dxg - jnp.mean(dxg, axis=-1, keepdims=True)
                                        - xhat * jnp.mean(dxg * xhat, axis=-1, keepdims=True))

    rtile = lambda w: pl.BlockSpec((tt, w), lambda i: (n_tiles - 1 - i, 0))
    acc = lambda shape: pl.BlockSpec(shape, lambda i: (0, 0))
    out_shape = (
        jax.ShapeDtypeStruct((T, D), f32),
        jax.ShapeDtypeStruct((T, D), bf16),
        jax.ShapeDtypeStruct((T, 2 * D_FF), bf16),
        jax.ShapeDtypeStruct((T, D_FF), bf16),
        jax.ShapeDtypeStruct((8, 128), f32),
        jax.ShapeDtypeStruct((1, D), f32), jax.ShapeDtypeStruct((1, D), f32),
        jax.ShapeDtypeStruct((1, D), f32), jax.ShapeDtypeStruct((1, D), f32),
        jax.ShapeDtypeStruct((1, D_FF), f32), jax.ShapeDtypeStruct((3, D_FF), f32),
    )
    return pl.pallas_call(
        body, name="ffn_forward_backward", grid=(n_tiles,), out_shape=out_shape,
        in_specs=[rtile(D),
                  pl.BlockSpec((FH, D), lambda i: (jnp.maximum((n_tiles - 1 - i) * hb - 1, 0), 0)),
                  rtile(1), _const_spec((1, D)), _const_spec((1, D)), _const_spec((2 * D_FF, D)),
                  _const_spec((3, D_FF)), _const_spec((1, D_FF)), _const_spec((D_FF, D)),
                  _const_spec((1, D)), _const_spec((1, D)), rtile(D)],
        out_specs=(rtile(D), rtile(D), rtile(2 * D_FF), rtile(D_FF), acc((8, 128)),
                   acc((1, D)), acc((1, D)), acc((1, D)), acc((1, D)), acc((1, D_FF)), acc((3, D_FF))),
        scratch_shapes=[pltpu.VMEM((D_FF // 128, tt + FH, 128), f32), pltpu.VMEM((tt, D_FF), f32),
                        pltpu.VMEM((D_FF // 128, tt + 8, 128), f32)],
        compiler_params=pltpu.CompilerParams(dimension_semantics=("arbitrary",), vmem_limit_bytes=V7X_VMEM_LIMIT),
    )(xhat1, xhat1, rstd1, ln1_g, ln1_b, w_up_t, conv_w, conv_b, w_down, ln2_g, ln2_b, target)


def _mix_backward(dz1, w_out, qkv, g, oret, states, pooled, cat, cos, sin, dmat, qd, kd, cdec, w_pool, pool_scale, w_in_t,
                  small_ffn, after, tt=MIX_TILE):
    n_tiles = T // tt

    def body(dz1_ref, wout_ref, qkv_ref, g_ref, oret_ref, states_ref, pooled_ref, cat_ref, cos_ref, sin_ref, dmat_ref,
             qd_ref, kd_ref, wpool_ref, pscale_ref, wint_ref, *rest):
        ffn_refs, rest = rest[:len(SMALL_FFN)], rest[len(SMALL_FFN):]
        after_ref, dproj_ref, gx_ref, small_ref, dwout_ref, dstate_s, dout_s, eext_s, tmp_s, dwout_s, dpscale_s = rest
        i = pl.program_id(0)
        tile_idx = n_tiles - 1 - i

        @pl.when(i == 0)
        def _():
            dstate_s[...] = jnp.zeros_like(dstate_s)
            small_ref[...] = jnp.zeros_like(small_ref)
            dpscale_s[...] = jnp.zeros_like(dpscale_s)
            dwout_s[...] = jnp.zeros_like(dwout_s)
            eext_s[:, pl.ds(tt, HALO), :] = jnp.zeros((GROUPS, HALO, DH), f32)

        dz1 = dz1_ref[...]
        dz1b = dz1.astype(bf16)
        dcat = _dot(dz1b, wout_ref[...], NT)
        dwout_s[...] += _dot(cat_ref[...], dz1b, TN)

        pos1 = (tile_idx * tt + lax.broadcasted_iota(jnp.int32, (tt, 1), 0) + 1).astype(f32)
        for gi, w in enumerate(WINDOWS):
            sl = slice(gi * DH, (gi + 1) * DH)
            dpo = dcat[:, RW + gi * DH: RW + (gi + 1) * DH]
            pooled_g = pooled_ref[:, sl]
            wpool_g = wpool_ref[gi].astype(bf16)
            ylin = _dot(pooled_g, wpool_g)
            dpscale_s[:, sl] += jnp.sum(dpo * ylin, axis=0, keepdims=True)
            dpw = (dpo * pscale_ref[:, sl]).astype(bf16)
            small_ref[pl.ds(gi * DH, DH), :] += _dot(pooled_g, dpw, TN)
            dpooled = _dot(dpw, wpool_g, NT)
            eext_s[gi, pl.ds(0, tt), :] = dpooled / jnp.minimum(pos1, float(w))
            stages = int(math.log2(w))
            src = eext_s
            for s in range(stages):
                n = tt + 8 * (stages - 1 - s)
                shift = 2 ** s
                val = src[gi, pl.ds(0, n), :] + src[gi, pl.ds(shift, n), :]
                if s == stages - 1:
                    wsum = val
                else:
                    tmp_s[gi, pl.ds(0, n), :] = val
                    src = tmp_s
            dproj_ref[:, 4 * RW + gi * DH: 4 * RW + (gi + 1) * DH] = (wsum - dpooled).astype(bf16)
        eext_s[:, pl.ds(tt, HALO), :] = eext_s[:, pl.ds(0, HALO), :]

        for h in range(HEADS):
            sl = slice(h * DH, (h + 1) * DH)
            dr = dcat[:, sl]
            o = oret_ref[:, sl]
            r = lax.rsqrt(jnp.mean(o * o, axis=-1, keepdims=True) + RMS_EPS)
            rn = o * r
            gg = g_ref[:, sl]
            sg = _sigmoid(gg)
            dproj_ref[:, 3 * RW + h * DH: 3 * RW + (h + 1) * DH] = (dr * rn * (sg * (1.0 + gg * (1.0 - sg)))).astype(bf16)
            drn = dr * (gg * sg)
            dout_s[:, sl] = (r * (drn - rn * jnp.mean(drn * rn, axis=-1, keepdims=True))).astype(bf16)

        for sub in reversed(range(tt // RET_TILE)):
            rows = pl.ds(sub * RET_TILE, RET_TILE)
            cos_t, sin_t = cos_ref[rows, :], sin_ref[rows, :]
            for h in range(HEADS):
                q = qkv_ref[rows, h * DH:(h + 1) * DH]
                k = qkv_ref[rows, RW + h * DH: RW + (h + 1) * DH]
                v = qkv_ref[rows, 2 * RW + h * DH: 2 * RW + (h + 1) * DH]
                do = dout_s[rows, h * DH:(h + 1) * DH]
                stb = states_ref[sub, h]
                dst = dstate_s[h]
                dstb = dst.astype(bf16)
                sb = (_dot(q, k, NT) * dmat_ref[h]).astype(bf16)
                dsb = (_dot(do, v, NT) * dmat_ref[h]).astype(bf16)
                dq = _dot(dsb, k) + _dot(do, stb, NT) * qd_ref[h]
                dk = _dot(dsb, q, TN) + _dot(v, dstb, NT) * kd_ref[h]
                dv = _dot(sb, do, TN) + _dot((k.astype(f32) * kd_ref[h]).astype(bf16), dstb)
                dstate_s[h] = dst * cdec[h] + _dot((q.astype(f32) * qd_ref[h]).astype(bf16), do, TN)
                dproj_ref[rows, h * DH:(h + 1) * DH] = (dq * cos_t - _swap_halves(dq) * sin_t).astype(bf16)
                dproj_ref[rows, RW + h * DH: RW + (h + 1) * DH] = (
                    (dk * cos_t - _swap_halves(dk) * sin_t) * K_SCALE).astype(bf16)
                dproj_ref[rows, 2 * RW + h * DH: 2 * RW + (h + 1) * DH] = dv.astype(bf16)

        gx_ref[...] = ALPHA * dz1 + _dot(dproj_ref[...], wint_ref[...])

        @pl.when(i == n_tiles - 1)
        def _():
            dwout_ref[...] = dwout_s[...].astype(bf16)
            at = GROUPS * DH
            for ref, size in [(dpscale_s, PW)] + [(ref, size) for ref, (_, size) in zip(ffn_refs, SMALL_FFN)]:
                for j in range(size // 128):
                    r, k = divmod(j, ref.shape[1] // 128)
                    small_ref[at + j: at + j + 1, :] = ref[r:r + 1, k * 128:(k + 1) * 128]
                at = SMALL_FFN_AT if ref is dpscale_s else at + size // 128

    rtile = lambda w: pl.BlockSpec((tt, w), lambda i: (n_tiles - 1 - i, 0))
    out_shape = (
        jax.ShapeDtypeStruct((T, IN_W), bf16),
        jax.ShapeDtypeStruct((T, D), f32),
        jax.ShapeDtypeStruct((SMALL_ROWS, 128), f32),
        jax.ShapeDtypeStruct((D, D), bf16),
    )
    return pl.pallas_call(
        body, name="mix_backward", grid=(n_tiles,), out_shape=out_shape,
        in_specs=[rtile(D), _const_spec((D, D)), rtile(3 * RW), rtile(RW), rtile(RW),
                  pl.BlockSpec((tt // RET_TILE, HEADS, DH, DH), lambda i: (n_tiles - 1 - i, 0, 0, 0)),
                  rtile(PW), rtile(D), rtile(DH), rtile(DH),
                  _const_spec((HEADS, RET_TILE, RET_TILE)), _const_spec((HEADS, RET_TILE, DH)),
                  _const_spec((HEADS, RET_TILE, DH)),
                  _const_spec((GROUPS, DH, DH)), _const_spec((1, PW)), _const_spec((IN_W, D)),
                  *[_const_spec(a.shape) for a in small_ffn], pl.BlockSpec(memory_space=pl.ANY)],
        out_specs=(rtile(IN_W), rtile(D), pl.BlockSpec((SMALL_ROWS, 128), lambda i: (0, 0)),
                   pl.BlockSpec((D, D), lambda i: (0, 0), pipeline_mode=pl.Buffered(1))),
        scratch_shapes=[pltpu.VMEM((HEADS, DH, DH), f32), pltpu.VMEM((tt, RW), bf16),
                        pltpu.VMEM((GROUPS, tt + HALO, DH), f32), pltpu.VMEM((GROUPS, tt + HALO, DH), f32),
                        pltpu.VMEM((D, D), f32), pltpu.VMEM((1, PW), f32)],
        compiler_params=pltpu.CompilerParams(dimension_semantics=("arbitrary",), vmem_limit_bytes=V7X_VMEM_LIMIT),
    )(dz1, w_out, qkv, g, oret, states, pooled, cat, cos, sin, dmat, qd, kd, w_pool, pool_scale, w_in_t, *small_ffn,
      after)


def _weight_grad(a, b, name, tm, exchange=()):
    m = a.shape[1]
    n_m, n_e = m // tm, len(exchange)

    def body(a_ref, b_ref, *rest):
        ein, o_ref, eout, sems = rest[:n_e], rest[n_e], rest[n_e + 1:2 * n_e + 1], rest[2 * n_e + 1:]
        i = pl.program_id(0)

        if n_e:
            @pl.when(i == 0)
            def _():
                _chip_exchange_start(ein, eout, *sems)

        o_ref[...] = _dot(a_ref[...], b_ref[...].astype(bf16), TN).astype(bf16)

        if n_e:
            @pl.when(i == n_m - 1)
            def _():
                _chip_exchange_finish(ein, eout, *sems)

    hbm = pl.BlockSpec(memory_space=pltpu.HBM)
    return pl.pallas_call(
        body, name=name, grid=(n_m,),
        out_shape=(jax.ShapeDtypeStruct((m, D), bf16),) + tuple(jax.ShapeDtypeStruct(e.shape, e.dtype) for e in exchange),
        in_specs=[pl.BlockSpec((T, tm), lambda i: (0, i)),
                  pl.BlockSpec((T, D), lambda i: (0, 0), pipeline_mode=pl.Buffered(1))] + [hbm] * n_e,
        out_specs=(pl.BlockSpec((tm, D), lambda i: (i, 0)),) + (hbm,) * n_e,
        scratch_shapes=_chip_exchange_sems(n_e),
        compiler_params=pltpu.CompilerParams(dimension_semantics=("arbitrary",), vmem_limit_bytes=V7X_VMEM_LIMIT,
                                             collective_id=CHIP_BARRIER if n_e else None),
    )(a, b, *exchange)


CHIP_FLIPS = ((1, 0), (0, 1), (1, 1))
PAIR_BARRIER, CHIP_BARRIER, GATHER_BARRIER, CHIP_BARRIER_SPLIT = 0, 1, 2, 3


def _barrier(peers):
    sem = pltpu.get_barrier_semaphore()
    for peer in peers:
        pl.semaphore_signal(sem, inc=1, device_id=peer, device_id_type=pl.DeviceIdType.MESH)
    pl.semaphore_wait(sem, len(peers))


def _me():
    return lax.axis_index("x"), lax.axis_index("y"), lax.axis_index("c")


def _chip(me, k):
    x, y, _ = me
    if k == 0:
        return x, y
    fx, fy = CHIP_FLIPS[k - 1]
    return (1 - x if fx else x), (1 - y if fy else y)


def _slot(x, y, c):
    return 4 * x + 2 * y + c


def _remote(src, dst, send_sem, recv_sem, to):
    return pltpu.make_async_remote_copy(src_ref=src, dst_ref=dst, send_sem=send_sem, recv_sem=recv_sem,
                                        device_id=to, device_id_type=pl.DeviceIdType.MESH)


def _gather_sems(n):
    return [pltpu.SemaphoreType.DMA((7, n)), pltpu.SemaphoreType.DMA((7, n)), pltpu.SemaphoreType.DMA((n,))] if n else []


def _gather_copy(k, j, gin, gout, send_sems, recv_sems, sending):
    x, y, c = _me()
    sibling, x_chip, y_chip, d_chip = (x, y, 1 - c), (1 - x, y), (x, 1 - y), (1 - x, 1 - y)
    south = c == 0
    passed_on = (jnp.where(south, 1 - x, x), jnp.where(south, y, 1 - y), c)
    src, to = gin[j], sibling
    if sending:
        block = {0: (x, y, c), 1: (x, y, c), 2: (x, y, c), 3: passed_on, 4: (*x_chip, c), 5: (*y_chip, c), 6: (*d_chip, c)}[k]
        to = {1: (*x_chip, c), 2: (*y_chip, c), 3: (jnp.where(south, x, 1 - x), jnp.where(south, 1 - y, y), c)}.get(k, sibling)
        if k >= 3:
            src = gout[j].at[_slot(*block)]
    else:
        block = {0: sibling, 1: (*x_chip, c), 2: (*y_chip, c), 3: (*d_chip, c), 4: (*x_chip, 1 - c), 5: (*y_chip, 1 - c),
                 6: (*d_chip, 1 - c)}[k]
    return _remote(src, gout[j].at[_slot(*block)], send_sems.at[k, j], recv_sems.at[k, j], to)


def _gather_do(ks, action, gin, gout, send_sems, recv_sems):
    for k in ks:
        for j in range(len(gin)):
            cp = _gather_copy(k, j, gin, gout, send_sems, recv_sems, action != "wait_recv")
            getattr(cp, action)()


def _gather_peers():
    x, y, c = _me()
    return [(x, y, 1 - c), (1 - x, y, c), (x, 1 - y, c)]


def _gather_start(gin, gout, send_sems, recv_sems, local_sems, barrier=True):
    if barrier:
        _barrier(_gather_peers())
    for j in range(len(gin)):
        pltpu.make_async_copy(gin[j], gout[j].at[_slot(*_me())], local_sems.at[j]).start()
    _gather_do((0, 1, 2), "start", gin, gout, send_sems, recv_sems)


def _gather_forward(gin, gout, send_sems, recv_sems, local_sems):
    _gather_do((1, 2), "wait_recv", gin, gout, send_sems, recv_sems)
    _gather_do((3, 4, 5), "start", gin, gout, send_sems, recv_sems)


def _gather_finish(gin, gout, send_sems, recv_sems, local_sems):
    _gather_do((3,), "wait_recv", gin, gout, send_sems, recv_sems)
    _gather_do((6,), "start", gin, gout, send_sems, recv_sems)
    _gather_do((0, 4, 5, 6), "wait_recv", gin, gout, send_sems, recv_sems)
    _gather_do(range(7), "wait_send", gin, gout, send_sems, recv_sems)
    for j in range(len(gin)):
        pltpu.make_async_copy(gin[j], gout[j].at[_slot(*_me())], local_sems.at[j]).wait()


def _all_gather(blocks, name):
    n = len(blocks)

    def body(*refs):
        gin, gout, sems = refs[:n], refs[n:2 * n], refs[2 * n:]
        _gather_start(gin, gout, *sems)
        _gather_forward(gin, gout, *sems)
        _gather_finish(gin, gout, *sems)

    hbm = pl.BlockSpec(memory_space=pltpu.HBM)
    return pl.pallas_call(
        body, name=name,
        out_shape=tuple(jax.ShapeDtypeStruct((N_DEV,) + b.shape, b.dtype) for b in blocks),
        in_specs=[hbm] * n, out_specs=(hbm,) * n, scratch_shapes=_gather_sems(n),
        compiler_params=pltpu.CompilerParams(collective_id=GATHER_BARRIER),
    )(*blocks)


def _pair_reduce(parts, name):
    n = len(parts)

    def body(*refs):
        ins, own, others, landing, mine = (refs[k * n:(k + 1) * n] for k in range(5))
        send_sems, recv_sems, local_sems = refs[5 * n:]
        me = _me()
        x, y, c = me
        sibling = (x, y, 1 - c)
        _barrier([sibling])
        sends, loads = [], []
        for k in range(4):
            for j in range(n):
                cp = _remote(ins[j].at[_slot(*_chip(me, k), 1 - c)], landing[j].at[k], send_sems.at[k, j],
                             recv_sems.at[k, j], sibling)
                cp.start()
                sends.append(cp)
                ld = pltpu.make_async_copy(ins[j].at[_slot(*_chip(me, k), c)], mine[j].at[k], local_sems.at[k, j])
                ld.start()
                loads.append(ld)
        for k in range(4):
            for j in range(n):
                loads[k * n + j].wait()
                _remote(ins[j].at[0], landing[j].at[k], send_sems.at[k, j], recv_sems.at[k, j], sibling).wait_recv()
                total = mine[j][k].astype(f32) + landing[j][k].astype(f32)
                if k == 0:
                    own[j][...] = total.astype(own[j].dtype)
                else:
                    others[j][k - 1] = total.astype(others[j].dtype)
        for cp in sends:
            cp.wait_send()

    vm = pl.BlockSpec(memory_space=pltpu.VMEM)
    return pl.pallas_call(
        body, name=name,
        out_shape=tuple(jax.ShapeDtypeStruct(p.shape[1:], p.dtype) for p in parts)
        + tuple(jax.ShapeDtypeStruct((3,) + p.shape[1:], p.dtype) for p in parts),
        in_specs=[pl.BlockSpec(memory_space=pltpu.HBM)] * n, out_specs=(vm,) * (2 * n),
        scratch_shapes=[pltpu.VMEM((4,) + p.shape[1:], p.dtype) for p in parts] * 2
        + [pltpu.SemaphoreType.DMA((4, n)), pltpu.SemaphoreType.DMA((4, n)), pltpu.SemaphoreType.DMA((4, n))],
        compiler_params=pltpu.CompilerParams(vmem_limit_bytes=V7X_VMEM_LIMIT, collective_id=PAIR_BARRIER),
    )(*parts)


def _chip_exchange_sems(n):
    return [pltpu.SemaphoreType.DMA((3, n)), pltpu.SemaphoreType.DMA((3, n))] if n else []


def _chip_exchange_copy(k, j, ein, eout, send_sems, recv_sems):
    me = _me()
    return _remote(ein[j].at[k - 1], eout[j].at[k - 1], send_sems.at[k - 1, j], recv_sems.at[k - 1, j],
                   (*_chip(me, k), me[2]))


def _chip_peers():
    me = _me()
    return [(*_chip(me, k), me[2]) for k in range(1, 4)]


def _chip_exchange_start(ein, eout, send_sems, recv_sems, barrier=True):
    if barrier:
        _barrier(_chip_peers())
    for k in range(1, 4):
        for j in range(len(ein)):
            _chip_exchange_copy(k, j, ein, eout, send_sems, recv_sems).start()


def _chip_exchange_finish(ein, eout, send_sems, recv_sems):
    for k in range(1, 4):
        for j in range(len(ein)):
            _chip_exchange_copy(k, j, ein, eout, send_sems, recv_sems).wait_recv()
    for k in range(1, 4):
        for j in range(len(ein)):
            _chip_exchange_copy(k, j, ein, eout, send_sems, recv_sems).wait_send()


def _split_copies(src_ref, dst_ref, sems):
    me = _me()
    return [_remote(src_ref.at[k - 1], dst_ref.at[k - 1], sems[k - 1], sems[2 + k], (*_chip(me, k), me[2]))
            for k in range(1, 4)]


def _exchange_start(others, name, barrier_id):
    def body(src_ref, land_ref, *rest):
        sems, token_ref = rest[:6], rest[8]
        _barrier(_chip_peers())
        for copy in _split_copies(src_ref, land_ref, sems):
            copy.start()
        token_ref[...] = jnp.zeros_like(token_ref)

    hbm, sem = pl.BlockSpec(memory_space=pltpu.HBM), pl.BlockSpec(memory_space=pltpu.SEMAPHORE)
    thru = pltpu.HBM(others.shape, others.dtype)
    res = pl.pallas_call(
        body, name=name,
        out_shape=(pltpu.SemaphoreType.DMA(()),) * 6 + (thru, thru, jax.ShapeDtypeStruct((8, 128), f32)),
        in_specs=(hbm, hbm), out_specs=(sem,) * 6 + (hbm, hbm, pl.BlockSpec(memory_space=pltpu.VMEM)),
        input_output_aliases={0: 6, 1: 7},
        compiler_params=pltpu.CompilerParams(has_side_effects=pltpu.SideEffectType.DATAFLOW_SIDE_EFFECTING,
                                             collective_id=barrier_id),
    )(pltpu.with_memory_space_constraint(others, pltpu.HBM),
      pltpu.with_memory_space_constraint(lax.empty(others.shape, others.dtype), pltpu.HBM))
    return res[:6], res[6], res[7], res[8]


def _exchange_wait(sems, src_thru, land_thru, after, name):
    n_after = len(after)

    def body(src_ref, land_ref, *rest):
        for copy in _split_copies(src_ref, land_ref, rest[:6]):
            copy.wait_send()
            copy.wait_recv()

    hbm, sem = pl.BlockSpec(memory_space=pltpu.HBM), pl.BlockSpec(memory_space=pltpu.SEMAPHORE)
    thru = pltpu.HBM(src_thru.shape, src_thru.dtype)
    return pl.pallas_call(
        body, name=name, out_shape=(thru, thru),
        in_specs=(hbm, hbm) + (sem,) * 6 + (pl.BlockSpec(memory_space=pl.ANY),) * n_after, out_specs=(hbm, hbm),
        input_output_aliases={0: 0, 1: 1},
        compiler_params=pltpu.CompilerParams(has_side_effects=pltpu.SideEffectType.DATAFLOW_SIDE_EFFECTING),
    )(src_thru, land_thru, *sems, *after)[1]


def _sum_parts(owns, arrived, name):
    n = len(owns)

    def body(*refs):
        for own, arr, out in zip(refs[:n], refs[n:2 * n], refs[2 * n:]):
            acc = own[...].astype(f32)
            for k in range(3):
                acc = acc + arr[k].astype(f32)
            out[...] = acc

    vm = pl.BlockSpec(memory_space=pltpu.VMEM)
    return pl.pallas_call(
        body, name=name, out_shape=tuple(jax.ShapeDtypeStruct(o.shape, f32) for o in owns),
        in_specs=[vm] * (2 * n), out_specs=(vm,) * n,
        compiler_params=pltpu.CompilerParams(vmem_limit_bytes=V7X_VMEM_LIMIT),
    )(*owns, *arrived)


def _adam_update(w, g, m, v):
    m = ADAM_B1 * m + (1.0 - ADAM_B1) * g
    v = ADAM_B2 * v + (1.0 - ADAM_B2) * (g * g)
    m_hat = m / (1.0 - ADAM_B1 ** ADAM_STEP)
    v_hat = v / (1.0 - ADAM_B2 ** ADAM_STEP)
    return -ADAM_LR * (m_hat / (jnp.sqrt(v_hat) + ADAM_EPS) + ADAM_WD * w), m, v


def _sum_adamw(own, arrived, w, m, v, name, steps, after=()):
    rows = own.shape[0]
    br = rows // steps

    def body(own_ref, arr_ref, w_ref, m_ref, v_ref, *rest):
        g_out, d_out, m_out, v_out = rest[len(after):]
        g = own_ref[...].astype(f32)
        for k in range(3):
            g = g + arr_ref[k].astype(f32)
        g_out[...] = g
        d_out[...], m_out[...], v_out[...] = _adam_update(w_ref[...], g, m_ref[...], v_ref[...])

    blk = pl.BlockSpec((br, D), lambda i: (i, 0))
    return pl.pallas_call(
        body, name=name, grid=(steps,), out_shape=(jax.ShapeDtypeStruct((rows, D), f32),) * 4,
        in_specs=[blk, pl.BlockSpec((3, br, D), lambda i: (0, i, 0)), blk, blk, blk]
        + [pl.BlockSpec(memory_space=pl.ANY)] * len(after), out_specs=(blk,) * 4,
        compiler_params=pltpu.CompilerParams(dimension_semantics=("parallel",), vmem_limit_bytes=V7X_VMEM_LIMIT),
    )(own, arrived, w, m, v, *after)


def _adamw(ws, gs, ms, vs, name):
    n = len(ws)

    def body(*refs):
        w_r, g_r, m_r, v_r = (refs[k * n:(k + 1) * n] for k in range(4))
        d_o, m_o, v_o = (refs[(4 + k) * n:(5 + k) * n] for k in range(3))
        for j in range(n):
            d_o[j][...], m_o[j][...], v_o[j][...] = _adam_update(w_r[j][...], g_r[j][...], m_r[j][...], v_r[j][...])

    vm = pl.BlockSpec(memory_space=pltpu.VMEM)
    shapes = tuple(jax.ShapeDtypeStruct(w.shape, f32) for w in ws)
    return pl.pallas_call(
        body, name=name, out_shape=shapes * 3, in_specs=[vm] * (4 * n), out_specs=tuple([vm] * (3 * n)),
        compiler_params=pltpu.CompilerParams(vmem_limit_bytes=V7X_VMEM_LIMIT),
    )(*ws, *gs, *ms, *vs)


SMALL_FFN = (("ln1_g", D), ("ln1_b", D), ("ln2_g", D), ("ln2_b", D), ("conv_b", D_FF), ("conv_w", 3 * D_FF), ("loss", 128))
SMALL_FFN_AT = 520
SMALL_ROWS = 704


def _unpack(packed):
    pieces, at = [("w_pool", 0, GROUPS * DH * DH), ("pool_scale", GROUPS * DH, PW)], SMALL_FFN_AT
    for k, size in SMALL_FFN:
        pieces.append((k, at, size))
        at += size // 128
    return {k: packed[row:row + size // 128] for k, row, size in pieces}


def kernel(x, w_in, w_pool, pool_scale, w_out, ln1_g, ln1_b, w_up, conv_w, conv_b, w_down, ln2_g, ln2_b, loss_target, m_w_in, m_w_pool, m_pool_scale, m_w_out, m_ln1_g, m_ln1_b, m_w_up, m_conv_w, m_conv_b, m_w_down, m_ln2_g, m_ln2_b, v_w_in, v_w_pool, v_pool_scale, v_w_out, v_ln1_g, v_ln1_b, v_w_up, v_conv_w, v_conv_b, v_w_down, v_ln2_g, v_ln2_b):
    me = 4 * lax.axis_index("x") + 2 * lax.axis_index("y") + lax.axis_index("c")
    x2, tgt = x[0], loss_target[0]

    cos, sin = _rope_tables()
    dmat, qd, kd, cdec = _decay_tables(RET_TILE)

    qkv, g, oret, states, cat, pooled, xhat1, rstd1, x1b, xb, g_in, g_out, g_up, g_down, g_cw = _mix_forward(
        x2, w_in[0].T, w_out[0], cos, sin, dmat, qd, kd, cdec, w_pool[0], pool_scale, ln1_g, ln1_b,
        gather_bf16=[w_up[0].T, w_down[0]], gather=[jnp.transpose(conv_w, (1, 0, 2))])
    w_in_t = g_in.reshape(IN_W, D)
    w_out_f = g_out.reshape(D, D)
    w_up_t = g_up.reshape(2 * D_FF, D)
    w_down_f = g_down.reshape(D_FF, D)
    conv_w_f = jnp.transpose(g_cw[:, :, 0, :], (1, 0, 2)).reshape(3, D_FF)
    dz1, dz2b, du, f, loss8, d_ln2_g, d_ln2_b, d_ln1_g, d_ln1_b, d_conv_b, d_conv_w = _ffn_forward_backward(
        xhat1, rstd1, ln1_g, ln1_b, w_up_t, conv_w_f, conv_b, w_down_f, ln2_g, ln2_b, tgt)
    small_ffn = [d_ln1_g, d_ln1_b, d_ln2_g, d_ln2_b, d_conv_b, d_conv_w, loss8]

    (dw_down,) = _weight_grad(f, dz2b, "grad_w_down", tm=D_FF // 2)
    own_down, oth_down = _pair_reduce([dw_down.reshape(N_DEV, ROWS_DOWN, D)], "pair_reduce_down")
    dw_up_t, arr_down = _weight_grad(du, x1b, "grad_w_up", tm=D_FF // 2, exchange=[oth_down])
    own_up, oth_up = _pair_reduce([dw_up_t.reshape(N_DEV, ROWS_UP, D)], "pair_reduce_up")
    up_sems, up_src, up_land, up_started = _exchange_start(oth_up, "exchange_up_start", CHIP_BARRIER_SPLIT)
    dproj, grad_x, small, dw_out = _mix_backward(
        dz1, w_out_f, qkv, g, oret, states, pooled, cat, cos, sin, dmat, qd, kd, cdec, w_pool[0], pool_scale, w_in_t,
        small_ffn, after=up_started)
    own_out, own_small, oth_out, oth_small = _pair_reduce(
        [dw_out.reshape(N_DEV, ROWS_OUT, D), small.reshape(N_DEV, SMALL_ROWS // N_DEV, 128)], "pair_reduce_out")
    dw_in_t, arr_out, arr_small = _weight_grad(dproj, xb, "grad_w_in", tm=IN_W // 2, exchange=[oth_out, oth_small])
    arr_up = _exchange_wait(up_sems, up_src, up_land, [dw_in_t], "exchange_up_wait")
    own_in, oth_in = _pair_reduce([dw_in_t.reshape(N_DEV, ROWS_IN, D)], "pair_reduce_in")
    in_sems, in_src, in_land, started = _exchange_start(oth_in, "exchange_in_start", CHIP_BARRIER)
    (small_piece,) = _sum_parts([own_small], [arr_small], "sum_small_grads")
    (gs_small,) = _all_gather([small_piece], "gather_small_grads")

    names = ["w_in", "w_pool", "pool_scale", "w_out", "ln1_g", "ln1_b", "w_up", "conv_w", "conv_b", "w_down",
             "ln2_g", "ln2_b"]
    w_d = dict(w_in=w_in, w_pool=w_pool, pool_scale=pool_scale, w_out=w_out, ln1_g=ln1_g, ln1_b=ln1_b, w_up=w_up,
               conv_w=conv_w, conv_b=conv_b, w_down=w_down, ln2_g=ln2_g, ln2_b=ln2_b)
    m_d = dict(w_in=m_w_in, w_pool=m_w_pool, pool_scale=m_pool_scale, w_out=m_w_out, ln1_g=m_ln1_g, ln1_b=m_ln1_b,
               w_up=m_w_up, conv_w=m_conv_w, conv_b=m_conv_b, w_down=m_w_down, ln2_g=m_ln2_g, ln2_b=m_ln2_b)
    v_d = dict(w_in=v_w_in, w_pool=v_w_pool, pool_scale=v_pool_scale, w_out=v_w_out, ln1_g=v_ln1_g, ln1_b=v_ln1_b,
               w_up=v_w_up, conv_w=v_conv_w, conv_b=v_conv_b, w_down=v_w_down, ln2_g=v_ln2_g, ln2_b=v_ln2_b)
    g_d, delta, new_m, new_v = {}, {}, {}, {}

    def big_adamw(k, own, arr, transposed, steps, after=()):
        lay = (lambda a: a[0].T) if transposed else (lambda a: a[0])
        back = (lambda a: a.T[None]) if transposed else (lambda a: a[None])
        res = _sum_adamw(own, arr, lay(w_d[k]), lay(m_d[k]), lay(v_d[k]), "adamw_" + k, steps, after)
        g_d[k], delta[k], new_m[k], new_v[k] = (back(r) for r in res)
        return res[3]

    done = [big_adamw("w_up", own_up, arr_up, True, 4, after=(started,)),
            big_adamw("w_down", own_down, arr_down, False, 2, after=(started,)),
            big_adamw("w_out", own_out, arr_out, False, 2, after=(started,))]

    gsm = _unpack(gs_small.reshape(SMALL_ROWS, 128))
    gsm["conv_w"] = lax.dynamic_slice(gsm["conv_w"].reshape(3, D_FF), (0, me * (D_FF // N_DEV)), (3, D_FF // N_DEV))
    lay = lambda k, a: jnp.transpose(a, (1, 0, 2)) if k == "conv_w" else a.reshape(-1, a.shape[-1])
    back = lambda k, a: jnp.transpose(a, (1, 0, 2)) if k == "conv_w" else a.reshape(w_d[k].shape)
    group = [k for k in names if k not in ("w_in", "w_out", "w_up", "w_down")]
    for k in group:
        g_d[k] = gsm[k].reshape(w_d[k].shape)
    res = _adamw([lay(k, w_d[k]) for k in group], [lay(k, g_d[k]) for k in group], [lay(k, m_d[k]) for k in group],
                 [lay(k, v_d[k]) for k in group], "adamw_small")
    for j, k in enumerate(group):
        delta[k], new_m[k], new_v[k] = (back(k, res[part * len(group) + j]) for part in range(3))

    arr_in = _exchange_wait(in_sems, in_src, in_land, done + [res[0]], "exchange_in_wait")
    big_adamw("w_in", own_in, arr_in, True, 4)

    loss = gsm["loss"][0, 0]
    return (loss, grad_x[None], *[g_d[k] for k in names], *[delta[k] for k in names], *[new_m[k] for k in names],
            *[new_v[k] for k in names])
```

```python
import math

import numpy as np
import jax
import jax.numpy as jnp
from jax import lax
from jax.experimental import pallas as pl
from jax.experimental.pallas import tpu as pltpu

f32 = jnp.float32
bf16 = jnp.bfloat16

N_DEV = 8
T = 4096
D = 1024
CHUNK = 64
MIX_TILE = 512
RET_TILE = 256
HEADS = 4
DH = 128
RW = HEADS * DH
PW = 512
GROUPS = 4
WINDOWS = (2, 4, 8, 16)
IN_W = 4 * RW + PW
D_FF = 2816
LN_EPS = 1e-5
RMS_EPS = 1e-6
ALPHA = 2.0 ** 0.25
K_SCALE = DH ** -0.5

ADAM_LR = 0.001
ADAM_B1 = 0.9
ADAM_B2 = 0.999
ADAM_EPS = 1e-08
ADAM_WD = 0.01
ADAM_STEP = 10

ROWS_IN, ROWS_OUT, ROWS_UP, ROWS_DOWN = IN_W // N_DEV, D // N_DEV, 2 * D_FF // N_DEV, D_FF // N_DEV

V7X_VMEM_LIMIT = 56 * 2 ** 20
HALO = 32

NT = (((1,), (1,)), ((), ()))
TN = (((0,), (0,)), ((), ()))
NN = (((1,), (0,)), ((), ()))


def _dot(a, b, dims=NN):
    return lax.dot_general(a, b, dims, preferred_element_type=f32)


def _const_spec(shape):
    zeros = (0,) * len(shape)
    return pl.BlockSpec(shape, lambda i: zeros, pipeline_mode=pl.Buffered(1))


def _sigmoid(x):
    return 0.5 * jnp.tanh(0.5 * x) + 0.5


def _decay_tables(tt):
    h = np.arange(HEADS, dtype=np.float64)
    log_gamma = np.log(1.0 - 2.0 ** (-5.0 - h)).astype(np.float32).astype(np.float64)[:, None, None]
    idx = np.arange(tt, dtype=np.float64)
    visible = (idx[None, :] // CHUNK) <= (idx[:, None] // CHUNK)
    mask = np.where(visible[None], np.exp(log_gamma * np.abs(idx[:, None] - idx[None, :])[None]), 0.0)
    qd = np.broadcast_to(np.exp(log_gamma * (idx[None, :, None] + 1.0)), (HEADS, tt, DH))
    kd = np.broadcast_to(np.exp(log_gamma * (tt - 1.0 - idx[None, :, None])), (HEADS, tt, DH))
    cd = np.exp(log_gamma[:, 0, 0] * tt)
    return (jnp.asarray(mask, f32), jnp.asarray(qd, f32), jnp.asarray(kd, f32), [float(c) for c in cd])


def _rope_tables():
    inv_freq = (10000.0 ** (-np.arange(0, DH, 2, dtype=np.float64) / DH)).astype(np.float32)
    ang = (np.arange(T, dtype=np.float32)[:, None] * inv_freq[None, :]).astype(np.float64)
    cos, sin = np.cos(ang), np.sin(ang)
    return (jnp.asarray(np.concatenate([cos, cos], axis=1), f32), jnp.asarray(np.concatenate([-sin, sin], axis=1), f32))


def _swap_halves(t):
    return pltpu.roll(t, DH // 2, axis=1)


def _mix_forward(x, w_in_shard, w_out_shard, cos, sin, dmat, qd, kd, cdec, w_pool, pool_scale, ln1_g, ln1_b,
                 gather_bf16, gather, tt=MIX_TILE):
    n_tiles = T // tt
    to_bf16 = [w_in_shard, w_out_shard] + list(gather_bf16)
    n_c, n_g = len(to_bf16), len(gather_bf16) + len(gather)

    def body(x_ref, cos_ref, sin_ref, dmat_ref, qd_ref, kd_ref, wpool_ref, pscale_ref, g1_ref, b1_ref, *rest):
        f32_in, plain_in, rest = rest[:n_c], rest[n_c:2 + n_g], rest[2 + n_g:]
        qkv_ref, g_ref, oret_ref, states_ref, cat_ref, pooled_ref, xhat_ref, rstd_ref, x1b_ref, xb_ref = rest[:10]
        fout, gout = rest[10:12], rest[12:12 + n_g]
        state_s, pext_s, tmp_s, wint_s, wout_s, load_sems, stage_sems, *rest = rest[12 + n_g:]
        stage_s, cast_s, sems = rest[:n_c], rest[n_c:2 * n_c], rest[2 * n_c:]
        fin, gin, fsems, gsems = cast_s[:2], tuple(cast_s[2:]) + tuple(plain_in), sems[:3], sems[3:]
        i = pl.program_id(0)

        @pl.when(i == 0)
        def _():
            stage = [pltpu.make_async_copy(src, dst, stage_sems.at[j]) for j, (src, dst) in enumerate(zip(f32_in, stage_s))]
            for cp in stage:
                cp.start()
            state_s[...] = jnp.zeros_like(state_s)
            pext_s[:, pl.ds(0, HALO), :] = jnp.zeros((GROUPS, HALO, DH), f32)

            def cast(js):
                for j in js:
                    stage[j].wait()
                    cast_s[j][...] = stage_s[j][...].astype(bf16)

            _barrier(_gather_peers())
            cast(range(2))
            _gather_start(fin, fout, *fsems, barrier=False)
            cast(range(2, n_c))
            _gather_forward(fin, fout, *fsems)
            _gather_start(gin, gout, *gsems, barrier=False)
            _gather_finish(fin, fout, *fsems)
            loads = [pltpu.make_async_copy(src.at[s], dst.at[pl.ds(s * src.shape[1], src.shape[1]), :],
                                           load_sems.at[j, s])
                     for j, (src, dst) in enumerate(((fout[0], wint_s), (fout[1], wout_s))) for s in range(N_DEV)]
            for ld in loads:
                ld.start()
            for ld in loads:
                ld.wait()

        @pl.when(i == n_tiles - 3)
        def _():
            _gather_forward(gin, gout, *gsems)

        xb = x_ref[...].astype(bf16)
        xb_ref[...] = xb
        cos_t, sin_t = cos_ref[...], sin_ref[...]
        for part in range(2):
            pr = _dot(xb, wint_s[pl.ds(part * RW, RW), :], NT)
            for h in range(HEADS):
                t = pr[:, h * DH:(h + 1) * DH]
                r = t * cos_t + _swap_halves(t) * sin_t
                if part == 1:
                    r = r * K_SCALE
                qkv_ref[:, part * RW + h * DH: part * RW + (h + 1) * DH] = r.astype(bf16)
        qkv_ref[:, 2 * RW:3 * RW] = _dot(xb, wint_s[pl.ds(2 * RW, RW), :], NT).astype(bf16)
        g_ref[...] = _dot(xb, wint_s[pl.ds(3 * RW, RW), :], NT)
        p = _dot(xb, wint_s[pl.ds(4 * RW, PW), :], NT)
        for gi in range(GROUPS):
            pext_s[gi, pl.ds(HALO, tt), :] = p[:, gi * DH:(gi + 1) * DH]

        for sub in range(tt // RET_TILE):
            rows = pl.ds(sub * RET_TILE, RET_TILE)
            for h in range(HEADS):
                q = qkv_ref[rows, h * DH:(h + 1) * DH]
                k = qkv_ref[rows, RW + h * DH: RW + (h + 1) * DH]
                v = qkv_ref[rows, 2 * RW + h * DH: 2 * RW + (h + 1) * DH]
                s = _dot(q, k, NT) * dmat_ref[h]
                st = state_s[h]
                stb = st.astype(bf16)
                states_ref[sub, h] = stb
                oret_ref[rows, h * DH:(h + 1) * DH] = (_dot(s.astype(bf16), v)
                                                      + _dot((q.astype(f32) * qd_ref[h]).astype(bf16), stb))
                state_s[h] = st * cdec[h] + _dot((k.astype(f32) * kd_ref[h]).astype(bf16), v, TN)

        for h in range(HEADS):
            sl = slice(h * DH, (h + 1) * DH)
            o = oret_ref[:, sl]
            r = lax.rsqrt(jnp.mean(o * o, axis=-1, keepdims=True) + RMS_EPS)
            gg = g_ref[:, sl]
            cat_ref[:, sl] = (o * r * (gg * _sigmoid(gg))).astype(bf16)

        pos1 = (i * tt + lax.broadcasted_iota(jnp.int32, (tt, 1), 0) + 1).astype(f32)
        for gi, w in enumerate(WINDOWS):
            sl = slice(gi * DH, (gi + 1) * DH)
            stages = int(math.log2(w))
            src = pext_s
            for s in range(stages):
                lo = HALO - 8 * (stages - 1 - s)
                n = tt + HALO - lo
                shift = 2 ** s
                val = src[gi, pl.ds(lo, n), :] + src[gi, pl.ds(lo - shift, n), :]
                if s == stages - 1:
                    wsum = val
                else:
                    tmp_s[gi, pl.ds(lo, n), :] = val
                    src = tmp_s
            p_g = pext_s[gi, pl.ds(HALO, tt), :]
            pooled = (wsum / jnp.minimum(pos1, float(w)) - p_g).astype(bf16)
            pooled_ref[:, sl] = pooled
            y = _dot(pooled, wpool_ref[gi].astype(bf16)) * pscale_ref[:, sl]
            cat_ref[:, RW + gi * DH: RW + (gi + 1) * DH] = y.astype(bf16)
        pext_s[:, pl.ds(0, HALO), :] = pext_s[:, pl.ds(tt, HALO), :]

        z = ALPHA * x_ref[...] + _dot(cat_ref[...], wout_s[...])
        mu = jnp.mean(z, axis=-1, keepdims=True)
        zc = z - mu
        rstd = lax.rsqrt(jnp.mean(zc * zc, axis=-1, keepdims=True) + LN_EPS)
        xhat = zc * rstd
        xhat_ref[...] = xhat
        rstd_ref[...] = rstd
        x1b_ref[...] = (xhat * g1_ref[...] + b1_ref[...]).astype(bf16)

        @pl.when(i == n_tiles - 1)
        def _():
            _gather_finish(gin, gout, *gsems)

    tile = lambda w: pl.BlockSpec((tt, w), lambda i: (i, 0))
    hbm = pl.BlockSpec(memory_space=pltpu.HBM)
    out_shape = (
        jax.ShapeDtypeStruct((T, 3 * RW), bf16),
        jax.ShapeDtypeStruct((T, RW), f32),
        jax.ShapeDtypeStruct((T, RW), f32),
        jax.ShapeDtypeStruct((T // RET_TILE, HEADS, DH, DH), bf16),
        jax.ShapeDtypeStruct((T, D), bf16),
        jax.ShapeDtypeStruct((T, PW), bf16),
        jax.ShapeDtypeStruct((T, D), f32),
        jax.ShapeDtypeStruct((T, 1), f32),
        jax.ShapeDtypeStruct((T, D), bf16),
        jax.ShapeDtypeStruct((T, D), bf16),
    ) + tuple(jax.ShapeDtypeStruct((N_DEV,) + b.shape, bf16) for b in to_bf16
              ) + tuple(jax.ShapeDtypeStruct((N_DEV,) + b.shape, b.dtype) for b in gather)
    return pl.pallas_call(
        body, name="mix_forward", grid=(n_tiles,), out_shape=out_shape,
        in_specs=[tile(D), tile(DH), tile(DH),
                  _const_spec((HEADS, RET_TILE, RET_TILE)), _const_spec((HEADS, RET_TILE, DH)),
                  _const_spec((HEADS, RET_TILE, DH)),
                  _const_spec((GROUPS, DH, DH)), _const_spec((1, PW)),
                  _const_spec((1, D)), _const_spec((1, D))] + [hbm] * (2 + n_g),
        out_specs=(tile(3 * RW), tile(RW), tile(RW),
                   pl.BlockSpec((tt // RET_TILE, HEADS, DH, DH), lambda i: (i, 0, 0, 0)),
                   tile(D), tile(PW), tile(D), tile(1), tile(D), tile(D)) + (hbm,) * (2 + n_g),
        scratch_shapes=[pltpu.VMEM((HEADS, DH, DH), f32), pltpu.VMEM((GROUPS, tt + HALO, DH), f32),
                        pltpu.VMEM((GROUPS, tt + HALO, DH), f32), pltpu.VMEM((IN_W, D), bf16), pltpu.VMEM((D, D), bf16),
                        pltpu.SemaphoreType.DMA((2, N_DEV)), pltpu.SemaphoreType.DMA((n_c,))]
        + [pltpu.VMEM(b.shape, f32) for b in to_bf16] + [pltpu.VMEM(b.shape, bf16) for b in to_bf16]
        + _gather_sems(2) + _gather_sems(n_g),
        compiler_params=pltpu.CompilerParams(dimension_semantics=("arbitrary",), vmem_limit_bytes=V7X_VMEM_LIMIT,
                                             collective_id=GATHER_BARRIER),
    )(x, cos, sin, dmat, qd, kd, w_pool, pool_scale, ln1_g, ln1_b, *to_bf16, *gather)


def _ffn_forward_backward(xhat1, rstd1, ln1_g, ln1_b, w_up_t, conv_w, conv_b, w_down, ln2_g, ln2_b, target,
                          tt=256):
    n_tiles = T // tt
    FH = 16
    hb = tt // FH

    def body(xhat_ref, halo_ref, rstd_ref, g1_ref, b1_ref, wupt_ref, cw_ref, cb_ref, wdown_ref, g2_ref, b2_ref, tgt_ref,
             dz1_ref, dz2b_ref, du_ref, f_ref, loss_ref, dg2_ref, db2_ref, dg1_ref, db1_ref, dcb_ref, dcw_ref,
             gext_s, val_s, dhext_s):
        i = pl.program_id(0)
        tile_idx = n_tiles - 1 - i

        def rd(ref, off):
            return jnp.concatenate([ref[k, pl.ds(off, tt), :] for k in range(D_FF // 128)], axis=1)

        def wr(ref, val):
            for k in range(D_FF // 128):
                ref[k, pl.ds(0, val.shape[0]), :] = val[:, k * 128:(k + 1) * 128]

        @pl.when(i == 0)
        def _():
            for r in (loss_ref, dg2_ref, db2_ref, dg1_ref, db1_ref, dcb_ref, dcw_ref):
                r[...] = jnp.zeros_like(r)
            dhext_s[:, pl.ds(tt, 8), :] = jnp.zeros((D_FF // 128, 8, 128), f32)

        g1, b1 = g1_ref[...], b1_ref[...]
        xhat = xhat_ref[...]
        x1 = xhat * g1 + b1
        x1b = x1.astype(bf16)
        x1h = ((halo_ref[...] * g1 + b1) * jnp.where(tile_idx == 0, 0.0, 1.0)).astype(bf16)
        x1ext = jnp.concatenate([x1h, x1b], axis=0)

        val = _dot(x1b, wupt_ref[pl.ds(0, D_FF), :], NT)
        gate_ext = _dot(x1ext, wupt_ref[pl.ds(D_FF, D_FF), :], NT)
        wr(gext_s, gate_ext)
        hh = (cb_ref[...] + cw_ref[0:1, :] * rd(gext_s, FH - 2) + cw_ref[1:2, :] * rd(gext_s, FH - 1)
              + cw_ref[2:3, :] * gate_ext[FH:])
        sg = _sigmoid(hh)
        act = hh * sg
        wr(dhext_s, act)
        val_s[...] = val * (sg + act * (1.0 - sg))
        fb = (act * val).astype(bf16)
        f_ref[...] = fb

        z = ALPHA * x1 + _dot(fb, wdown_ref[...])
        mu = jnp.mean(z, axis=-1, keepdims=True)
        zc = z - mu
        rstd2 = lax.rsqrt(jnp.mean(zc * zc, axis=-1, keepdims=True) + LN_EPS)
        xh2 = zc * rstd2
        diff = xh2 * g2_ref[...] + b2_ref[...] - tgt_ref[...]
        loss_ref[...] += 0.5 * jnp.sum(diff * diff) / D
        dy = diff * (1.0 / D)
        dg2_ref[...] += jnp.sum(dy * xh2, axis=0, keepdims=True)
        db2_ref[...] += jnp.sum(dy, axis=0, keepdims=True)
        dyg = dy * g2_ref[...]
        dz2 = rstd2 * (dyg - jnp.mean(dyg, axis=-1, keepdims=True) - xh2 * jnp.mean(dyg * xh2, axis=-1, keepdims=True))
        dz2b = dz2.astype(bf16)
        dz2b_ref[...] = dz2b

        df = _dot(dz2b, wdown_ref[...], NT)
        dval = df * rd(dhext_s, 0)
        dh = df * val_s[...]
        wr(dhext_s, dh)
        dh1, dh2, g0 = rd(dhext_s, 1), rd(dhext_s, 2), rd(gext_s, FH)
        dcb_ref[...] += jnp.sum(dh, axis=0, keepdims=True)
        dcw_ref[0:1, :] += jnp.sum(dh2 * g0, axis=0, keepdims=True)
        dcw_ref[1:2, :] += jnp.sum(dh1 * g0, axis=0, keepdims=True)
        dcw_ref[2:3, :] += jnp.sum(dh * g0, axis=0, keepdims=True)
        dgate = cw_ref[2:3, :] * dh + cw_ref[1:2, :] * dh1 + cw_ref[0:1, :] * dh2
        dvalb, dgateb = dval.astype(bf16), dgate.astype(bf16)
        du_ref[:, :D_FF] = dvalb
        du_ref[:, D_FF:] = dgateb
        dx1 = ALPHA * dz2 + _dot(dvalb, wupt_ref[pl.ds(0, D_FF), :]) + _dot(dgateb, wupt_ref[pl.ds(D_FF, D_FF), :])
        dhext_s[:, pl.ds(tt, 8), :] = dhext_s[:, pl.ds(0, 8), :]

        dg1_ref[...] += jnp.sum(dx1 * xhat, axis=0, keepdims=True)
        db1_ref[...] += jnp.sum(dx1, axis=0, keepdims=True)
        dxg = dx1 * g1
        dz1_ref[...] = rstd_ref[...] * (dxg - jnp.mean(dxg, axis=-1, keepdims=True)
                                        - xhat * jnp.mean(dxg * xhat, axis=-1, keepdims=True))

    rtile = lambda w: pl.BlockSpec((tt, w), lambda i: (n_tiles - 1 - i, 0))
    acc = lambda shape: pl.BlockSpec(shape, lambda i: (0, 0))
    out_shape = (
        jax.ShapeDtypeStruct((T, D), f32),
        jax.ShapeDtypeStruct((T, D), bf16),
        jax.ShapeDtypeStruct((T, 2 * D_FF), bf16),
        jax.ShapeDtypeStruct((T, D_FF), bf16),
        jax.ShapeDtypeStruct((8, 128), f32),
        jax.ShapeDtypeStruct((1, D), f32), jax.ShapeDtypeStruct((1, D), f32),
        jax.ShapeDtypeStruct((1, D), f32), jax.ShapeDtypeStruct((1, D), f32),
        jax.ShapeDtypeStruct((1, D_FF), f32), jax.ShapeDtypeStruct((3, D_FF), f32),
    )
    return pl.pallas_call(
        body, name="ffn_forward_backward", grid=(n_tiles,), out_shape=out_shape,
        in_specs=[rtile(D),
                  pl.BlockSpec((FH, D), lambda i: (jnp.maximum((n_tiles - 1 - i) * hb - 1, 0), 0)),
                  rtile(1), _const_spec((1, D)), _const_spec((1, D)), _const_spec((2 * D_FF, D)),
                  _const_spec((3, D_FF)), _const_spec((1, D_FF)), _const_spec((D_FF, D)),
                  _const_spec((1, D)), _const_spec((1, D)), rtile(D)],
        out_specs=(rtile(D), rtile(D), rtile(2 * D_FF), rtile(D_FF), acc((8, 128)),
                   acc((1, D)), acc((1, D)), acc((1, D)), acc((1, D)), acc((1, D_FF)), acc((3, D_FF))),
        scratch_shapes=[pltpu.VMEM((D_FF // 128, tt + FH, 128), f32), pltpu.VMEM((tt, D_FF), f32),
                        pltpu.VMEM((D_FF // 128, tt + 8, 128), f32)],
        compiler_params=pltpu.CompilerParams(dimension_semantics=("arbitrary",), vmem_limit_bytes=V7X_VMEM_LIMIT),
    )(xhat1, xhat1, rstd1, ln1_g, ln1_b, w_up_t, conv_w, conv_b, w_down, ln2_g, ln2_b, target)


def _mix_backward(dz1, w_out, qkv, g, oret, states, pooled, cat, cos, sin, dmat, qd, kd, cdec, w_pool, pool_scale, w_in_t,
                  small_ffn, after, tt=MIX_TILE):
    n_tiles = T // tt

    def body(dz1_ref, wout_ref, qkv_ref, g_ref, oret_ref, states_ref, pooled_ref, cat_ref, cos_ref, sin_ref, dmat_ref,
             qd_ref, kd_ref, wpool_ref, pscale_ref, wint_ref, *rest):
        ffn_refs, rest = rest[:len(SMALL_FFN)], rest[len(SMALL_FFN):]
        after_ref, dproj_ref, gx_ref, small_ref, dwout_ref, dstate_s, dout_s, eext_s, tmp_s, dwout_s, dpscale_s = rest
        i = pl.program_id(0)
        tile_idx = n_tiles - 1 - i

        @pl.when(i == 0)
        def _():
            dstate_s[...] = jnp.zeros_like(dstate_s)
            small_ref[...] = jnp.zeros_like(small_ref)
            dpscale_s[...] = jnp.zeros_like(dpscale_s)
            dwout_s[...] = jnp.zeros_like(dwout_s)
            eext_s[:, pl.ds(tt, HALO), :] = jnp.zeros((GROUPS, HALO, DH), f32)

        dz1 = dz1_ref[...]
        dz1b = dz1.astype(bf16)
        dcat = _dot(dz1b, wout_ref[...], NT)
        dwout_s[...] += _dot(cat_ref[...], dz1b, TN)

        pos1 = (tile_idx * tt + lax.broadcasted_iota(jnp.int32, (tt, 1), 0) + 1).astype(f32)
        for gi, w in enumerate(WINDOWS):
            sl = slice(gi * DH, (gi + 1) * DH)
            dpo = dcat[:, RW + gi * DH: RW + (gi + 1) * DH]
            pooled_g = pooled_ref[:, sl]
            wpool_g = wpool_ref[gi].astype(bf16)
            ylin = _dot(pooled_g, wpool_g)
            dpscale_s[:, sl] += jnp.sum(dpo * ylin, axis=0, keepdims=True)
            dpw = (dpo * pscale_ref[:, sl]).astype(bf16)
            small_ref[pl.ds(gi * DH, DH), :] += _dot(pooled_g, dpw, TN)
            dpooled = _dot(dpw, wpool_g, NT)
            eext_s[gi, pl.ds(0, tt), :] = dpooled / jnp.minimum(pos1, float(w))
            stages = int(math.log2(w))
            src = eext_s
            for s in range(stages):
                n = tt + 8 * (stages - 1 - s)
                shift = 2 ** s
                val = src[gi, pl.ds(0, n), :] + src[gi, pl.ds(shift, n), :]
                if s == stages - 1:
                    wsum = val
                else:
                    tmp_s[gi, pl.ds(0, n), :] = val
                    src = tmp_s
            dproj_ref[:, 4 * RW + gi * DH: 4 * RW + (gi + 1) * DH] = (wsum - dpooled).astype(bf16)
        eext_s[:, pl.ds(tt, HALO), :] = eext_s[:, pl.ds(0, HALO), :]

        for h in range(HEADS):
            sl = slice(h * DH, (h + 1) * DH)
            dr = dcat[:, sl]
            o = oret_ref[:, sl]
            r = lax.rsqrt(jnp.mean(o * o, axis=-1, keepdims=True) + RMS_EPS)
            rn = o * r
            gg = g_ref[:, sl]
            sg = _sigmoid(gg)
            dproj_ref[:, 3 * RW + h * DH: 3 * RW + (h + 1) * DH] = (dr * rn * (sg * (1.0 + gg * (1.0 - sg)))).astype(bf16)
            drn = dr * (gg * sg)
            dout_s[:, sl] = (r * (drn - rn * jnp.mean(drn * rn, axis=-1, keepdims=True))).astype(bf16)

        for sub in reversed(range(tt // RET_TILE)):
            rows = pl.ds(sub * RET_TILE, RET_TILE)
            cos_t, sin_t = cos_ref[rows, :], sin_ref[rows, :]
            for h in range(HEADS):
                q = qkv_ref[rows, h * DH:(h + 1) * DH]
                k = qkv_ref[rows, RW + h * DH: RW + (h + 1) * DH]
                v = qkv_ref[rows, 2 * RW + h * DH: 2 * RW + (h + 1) * DH]
                do = dout_s[rows, h * DH:(h + 1) * DH]
                stb = states_ref[sub, h]
                dst = dstate_s[h]
                dstb = dst.astype(bf16)
                sb = (_dot(q, k, NT) * dmat_ref[h]).astype(bf16)
                dsb = (_dot(do, v, NT) * dmat_ref[h]).astype(bf16)
                dq = _dot(dsb, k) + _dot(do, stb, NT) * qd_ref[h]
                dk = _dot(dsb, q, TN) + _dot(v, dstb, NT) * kd_ref[h]
                dv = _dot(sb, do, TN) + _dot((k.astype(f32) * kd_ref[h]).astype(bf16), dstb)
                dstate_s[h] = dst * cdec[h] + _dot((q.astype(f32) * qd_ref[h]).astype(bf16), do, TN)
                dproj_ref[rows, h * DH:(h + 1) * DH] = (dq * cos_t - _swap_halves(dq) * sin_t).astype(bf16)
                dproj_ref[rows, RW + h * DH: RW + (h + 1) * DH] = (
                    (dk * cos_t - _swap_halves(dk) * sin_t) * K_SCALE).astype(bf16)
                dproj_ref[rows, 2 * RW + h * DH: 2 * RW + (h + 1) * DH] = dv.astype(bf16)

        gx_ref[...] = ALPHA * dz1 + _dot(dproj_ref[...], wint_ref[...])

        @pl.when(i == n_tiles - 1)
        def _():
            dwout_ref[...] = dwout_s[...].astype(bf16)
            at = GROUPS * DH
            for ref, size in [(dpscale_s, PW)] + [(ref, size) for ref, (_, size) in zip(ffn_refs, SMALL_FFN)]:
                for j in range(size // 128):
                    r, k = divmod(j, ref.shape[1] // 128)
                    small_ref[at + j: at + j + 1, :] = ref[r:r + 1, k * 128:(k + 1) * 128]
                at = SMALL_FFN_AT if ref is dpscale_s else at + size // 128

    rtile = lambda w: pl.BlockSpec((tt, w), lambda i: (n_tiles - 1 - i, 0))
    out_shape = (
        jax.ShapeDtypeStruct((T, IN_W), bf16),
        jax.ShapeDtypeStruct((T, D), f32),
        jax.ShapeDtypeStruct((SMALL_ROWS, 128), f32),
        jax.ShapeDtypeStruct((D, D), bf16),
    )
    return pl.pallas_call(
        body, name="mix_backward", grid=(n_tiles,), out_shape=out_shape,
        in_specs=[rtile(D), _const_spec((D, D)), rtile(3 * RW), rtile(RW), rtile(RW),
                  pl.BlockSpec((tt // RET_TILE, HEADS, DH, DH), lambda i: (n_tiles - 1 - i, 0, 0, 0)),
                  rtile(PW), rtile(D), rtile(DH), rtile(DH),
                  _const_spec((HEADS, RET_TILE, RET_TILE)), _const_spec((HEADS, RET_TILE, DH)),
                  _const_spec((HEADS, RET_TILE, DH)),
                  _const_spec((GROUPS, DH, DH)), _const_spec((1, PW)), _const_spec((IN_W, D)),
                  *[_const_spec(a.shape) for a in small_ffn], pl.BlockSpec(memory_space=pl.ANY)],
        out_specs=(rtile(IN_W), rtile(D), pl.BlockSpec((SMALL_ROWS, 128), lambda i: (0, 0)),
                   pl.BlockSpec((D, D), lambda i: (0, 0), pipeline_mode=pl.Buffered(1))),
        scratch_shapes=[pltpu.VMEM((HEADS, DH, DH), f32), pltpu.VMEM((tt, RW), bf16),
                        pltpu.VMEM((GROUPS, tt + HALO, DH), f32), pltpu.VMEM((GROUPS, tt + HALO, DH), f32),
                        pltpu.VMEM((D, D), f32), pltpu.VMEM((1, PW), f32)],
        compiler_params=pltpu.CompilerParams(dimension_semantics=("arbitrary",), vmem_limit_bytes=V7X_VMEM_LIMIT),
    )(dz1, w_out, qkv, g, oret, states, pooled, cat, cos, sin, dmat, qd, kd, w_pool, pool_scale, w_in_t, *small_ffn,
      after)


def _weight_grad(a, b, name, tm, exchange=()):
    m = a.shape[1]
    n_m, n_e = m // tm, len(exchange)

    def body(a_ref, b_ref, *rest):
        ein, o_ref, eout, sems = rest[:n_e], rest[n_e], rest[n_e + 1:2 * n_e + 1], rest[2 * n_e + 1:]
        i = pl.program_id(0)

        if n_e:
            @pl.when(i == 0)
            def _():
                _chip_exchange_start(ein, eout, *sems)

        o_ref[...] = _dot(a_ref[...], b_ref[...].astype(bf16), TN).astype(bf16)

        if n_e:
            @pl.when(i == n_m - 1)
            def _():
                _chip_exchange_finish(ein, eout, *sems)

    hbm = pl.BlockSpec(memory_space=pltpu.HBM)
    return pl.pallas_call(
        body, name=name, grid=(n_m,),
        out_shape=(jax.ShapeDtypeStruct((m, D), bf16),) + tuple(jax.ShapeDtypeStruct(e.shape, e.dtype) for e in exchange),
        in_specs=[pl.BlockSpec((T, tm), lambda i: (0, i)),
                  pl.BlockSpec((T, D), lambda i: (0, 0), pipeline_mode=pl.Buffered(1))] + [hbm] * n_e,
        out_specs=(pl.BlockSpec((tm, D), lambda i: (i, 0)),) + (hbm,) * n_e,
        scratch_shapes=_chip_exchange_sems(n_e),
        compiler_params=pltpu.CompilerParams(dimension_semantics=("arbitrary",), vmem_limit_bytes=V7X_VMEM_LIMIT,
                                             collective_id=CHIP_BARRIER if n_e else None),
    )(a, b, *exchange)


CHIP_FLIPS = ((1, 0), (0, 1), (1, 1))
PAIR_BARRIER, CHIP_BARRIER, GATHER_BARRIER, CHIP_BARRIER_SPLIT = 0, 1, 2, 3


def _barrier(peers):
    sem = pltpu.get_barrier_semaphore()
    for peer in peers:
        pl.semaphore_signal(sem, inc=1, device_id=peer, device_id_type=pl.DeviceIdType.MESH)
    pl.semaphore_wait(sem, len(peers))


def _me():
    return lax.axis_index("x"), lax.axis_index("y"), lax.axis_index("c")


def _chip(me, k):
    x, y, _ = me
    if k == 0:
        return x, y
    fx, fy = CHIP_FLIPS[k - 1]
    return (1 - x if fx else x), (1 - y if fy else y)


def _slot(x, y, c):
    return 4 * x + 2 * y + c


def _remote(src, dst, send_sem, recv_sem, to):
    return pltpu.make_async_remote_copy(src_ref=src, dst_ref=dst, send_sem=send_sem, recv_sem=recv_sem,
                                        device_id=to, device_id_type=pl.DeviceIdType.MESH)


def _gather_sems(n):
    return [pltpu.SemaphoreType.DMA((7, n)), pltpu.SemaphoreType.DMA((7, n)), pltpu.SemaphoreType.DMA((n,))] if n else []


def _gather_copy(k, j, gin, gout, send_sems, recv_sems, sending):
    x, y, c = _me()
    sibling, x_chip, y_chip, d_chip = (x, y, 1 - c), (1 - x, y), (x, 1 - y), (1 - x, 1 - y)
    south = c == 0
    passed_on = (jnp.where(south, 1 - x, x), jnp.where(south, y, 1 - y), c)
    src, to = gin[j], sibling
    if sending:
        block = {0: (x, y, c), 1: (x, y, c), 2: (x, y, c), 3: passed_on, 4: (*x_chip, c), 5: (*y_chip, c), 6: (*d_chip, c)}[k]
        to = {1: (*x_chip, c), 2: (*y_chip, c), 3: (jnp.where(south, x, 1 - x), jnp.where(south, 1 - y, y), c)}.get(k, sibling)
        if k >= 3:
            src = gout[j].at[_slot(*block)]
    else:
        block = {0: sibling, 1: (*x_chip, c), 2: (*y_chip, c), 3: (*d_chip, c), 4: (*x_chip, 1 - c), 5: (*y_chip, 1 - c),
                 6: (*d_chip, 1 - c)}[k]
    return _remote(src, gout[j].at[_slot(*block)], send_sems.at[k, j], recv_sems.at[k, j], to)


def _gather_do(ks, action, gin, gout, send_sems, recv_sems):
    for k in ks:
        for j in range(len(gin)):
            cp = _gather_copy(k, j, gin, gout, send_sems, recv_sems, action != "wait_recv")
            getattr(cp, action)()


def _gather_peers():
    x, y, c = _me()
    return [(x, y, 1 - c), (1 - x, y, c), (x, 1 - y, c)]


def _gather_start(gin, gout, send_sems, recv_sems, local_sems, barrier=True):
    if barrier:
        _barrier(_gather_peers())
    for j in range(len(gin)):
        pltpu.make_async_copy(gin[j], gout[j].at[_slot(*_me())], local_sems.at[j]).start()
    _gather_do((0, 1, 2), "start", gin, gout, send_sems, recv_sems)


def _gather_forward(gin, gout, send_sems, recv_sems, local_sems):
    _gather_do((1, 2), "wait_recv", gin, gout, send_sems, recv_sems)
    _gather_do((3, 4, 5), "start", gin, gout, send_sems, recv_sems)


def _gather_finish(gin, gout, send_sems, recv_sems, local_sems):
    _gather_do((3,), "wait_recv", gin, gout, send_sems, recv_sems)
    _gather_do((6,), "start", gin, gout, send_sems, recv_sems)
    _gather_do((0, 4, 5, 6), "wait_recv", gin, gout, send_sems, recv_sems)
    _gather_do(range(7), "wait_send", gin, gout, send_sems, recv_sems)
    for j in range(len(gin)):
        pltpu.make_async_copy(gin[j], gout[j].at[_slot(*_me())], local_sems.at[j]).wait()


def _all_gather(blocks, name):
    n = len(blocks)

    def body(*refs):
        gin, gout, sems = refs[:n], refs[n:2 * n], refs[2 * n:]
        _gather_start(gin, gout, *sems)
        _gather_forward(gin, gout, *sems)
        _gather_finish(gin, gout, *sems)

    hbm = pl.BlockSpec(memory_space=pltpu.HBM)
    return pl.pallas_call(
        body, name=name,
        out_shape=tuple(jax.ShapeDtypeStruct((N_DEV,) + b.shape, b.dtype) for b in blocks),
        in_specs=[hbm] * n, out_specs=(hbm,) * n, scratch_shapes=_gather_sems(n),
        compiler_params=pltpu.CompilerParams(collective_id=GATHER_BARRIER),
    )(*blocks)


def _pair_reduce(parts, name):
    n = len(parts)

    def body(*refs):
        ins, own, others, landing, mine = (refs[k * n:(k + 1) * n] for k in range(5))
        send_sems, recv_sems, local_sems = refs[5 * n:]
        me = _me()
        x, y, c = me
        sibling = (x, y, 1 - c)
        _barrier([sibling])
        sends, loads = [], []
        for k in range(4):
            for j in range(n):
                cp = _remote(ins[j].at[_slot(*_chip(me, k), 1 - c)], landing[j].at[k], send_sems.at[k, j],
                             recv_sems.at[k, j], sibling)
                cp.start()
                sends.append(cp)
                ld = pltpu.make_async_copy(ins[j].at[_slot(*_chip(me, k), c)], mine[j].at[k], local_sems.at[k, j])
                ld.start()
                loads.append(ld)
        for k in range(4):
            for j in range(n):
                loads[k * n + j].wait()
                _remote(ins[j].at[0], landing[j].at[k], send_sems.at[k, j], recv_sems.at[k, j], sibling).wait_recv()
                total = mine[j][k].astype(f32) + landing[j][k].astype(f32)
                if k == 0:
                    own[j][...] = total.astype(own[j].dtype)
                else:
                    others[j][k - 1] = total.astype(others[j].dtype)
        for cp in sends:
            cp.wait_send()

    vm = pl.BlockSpec(memory_space=pltpu.VMEM)
    return pl.pallas_call(
        body, name=name,
        out_shape=tuple(jax.ShapeDtypeStruct(p.shape[1:], p.dtype) for p in parts)
        + tuple(jax.ShapeDtypeStruct((3,) + p.shape[1:], p.dtype) for p in parts),
        in_specs=[pl.BlockSpec(memory_space=pltpu.HBM)] * n, out_specs=(vm,) * (2 * n),
        scratch_shapes=[pltpu.VMEM((4,) + p.shape[1:], p.dtype) for p in parts] * 2
        + [pltpu.SemaphoreType.DMA((4, n)), pltpu.SemaphoreType.DMA((4, n)), pltpu.SemaphoreType.DMA((4, n))],
        compiler_params=pltpu.CompilerParams(vmem_limit_bytes=V7X_VMEM_LIMIT, collective_id=PAIR_BARRIER),
    )(*parts)


def _chip_exchange_sems(n):
    return [pltpu.SemaphoreType.DMA((3, n)), pltpu.SemaphoreType.DMA((3, n))] if n else []


def _chip_exchange_copy(k, j, ein, eout, send_sems, recv_sems):
    me = _me()
    return _remote(ein[j].at[k - 1], eout[j].at[k - 1], send_sems.at[k - 1, j], recv_sems.at[k - 1, j],
                   (*_chip(me, k), me[2]))


def _chip_peers():
    me = _me()
    return [(*_chip(me, k), me[2]) for k in range(1, 4)]


def _chip_exchange_start(ein, eout, send_sems, recv_sems, barrier=True):
    if barrier:
        _barrier(_chip_peers())
    for k in range(1, 4):
        for j in range(len(ein)):
            _chip_exchange_copy(k, j, ein, eout, send_sems, recv_sems).start()


def _chip_exchange_finish(ein, eout, send_sems, recv_sems):
    for k in range(1, 4):
        for j in range(len(ein)):
            _chip_exchange_copy(k, j, ein, eout, send_sems, recv_sems).wait_recv()
    for k in range(1, 4):
        for j in range(len(ein)):
            _chip_exchange_copy(k, j, ein, eout, send_sems, recv_sems).wait_send()


def _split_copies(src_ref, dst_ref, sems):
    me = _me()
    return [_remote(src_ref.at[k - 1], dst_ref.at[k - 1], sems[k - 1], sems[2 + k], (*_chip(me, k), me[2]))
            for k in range(1, 4)]


def _exchange_start(others, name, barrier_id):
    def body(src_ref, land_ref, *rest):
        sems, token_ref = rest[:6], rest[8]
        _barrier(_chip_peers())
        for copy in _split_copies(src_ref, land_ref, sems):
            copy.start()
        token_ref[...] = jnp.zeros_like(token_ref)

    hbm, sem = pl.BlockSpec(memory_space=pltpu.HBM), pl.BlockSpec(memory_space=pltpu.SEMAPHORE)
    thru = pltpu.HBM(others.shape, others.dtype)
    res = pl.pallas_call(
        body, name=name,
        out_shape=(pltpu.SemaphoreType.DMA(()),) * 6 + (thru, thru, jax.ShapeDtypeStruct((8, 128), f32)),
        in_specs=(hbm, hbm), out_specs=(sem,) * 6 + (hbm, hbm, pl.BlockSpec(memory_space=pltpu.VMEM)),
        input_output_aliases={0: 6, 1: 7},
        compiler_params=pltpu.CompilerParams(has_side_effects=pltpu.SideEffectType.DATAFLOW_SIDE_EFFECTING,
                                             collective_id=barrier_id),
    )(pltpu.with_memory_space_constraint(others, pltpu.HBM),
      pltpu.with_memory_space_constraint(lax.empty(others.shape, others.dtype), pltpu.HBM))
    return res[:6], res[6], res[7], res[8]


def _exchange_wait(sems, src_thru, land_thru, after, name):
    n_after = len(after)

    def body(src_ref, land_ref, *rest):
        for copy in _split_copies(src_ref, land_ref, rest[:6]):
            copy.wait_send()
            copy.wait_recv()

    hbm, sem = pl.BlockSpec(memory_space=pltpu.HBM), pl.BlockSpec(memory_space=pltpu.SEMAPHORE)
    thru = pltpu.HBM(src_thru.shape, src_thru.dtype)
    return pl.pallas_call(
        body, name=name, out_shape=(thru, thru),
        in_specs=(hbm, hbm) + (sem,) * 6 + (pl.BlockSpec(memory_space=pl.ANY),) * n_after, out_specs=(hbm, hbm),
        input_output_aliases={0: 0, 1: 1},
        compiler_params=pltpu.CompilerParams(has_side_effects=pltpu.SideEffectType.DATAFLOW_SIDE_EFFECTING),
    )(src_thru, land_thru, *sems, *after)[1]


def _sum_parts(owns, arrived, name):
    n = len(owns)

    def body(*refs):
        for own, arr, out in zip(refs[:n], refs[n:2 * n], refs[2 * n:]):
            acc = own[...].astype(f32)
            for k in range(3):
                acc = acc + arr[k].astype(f32)
            out[...] = acc

    vm = pl.BlockSpec(memory_space=pltpu.VMEM)
    return pl.pallas_call(
        body, name=name, out_shape=tuple(jax.ShapeDtypeStruct(o.shape, f32) for o in owns),
        in_specs=[vm] * (2 * n), out_specs=(vm,) * n,
        compiler_params=pltpu.CompilerParams(vmem_limit_bytes=V7X_VMEM_LIMIT),
    )(*owns, *arrived)


def _adam_update(w, g, m, v):
    m = ADAM_B1 * m + (1.0 - ADAM_B1) * g
    v = ADAM_B2 * v + (1.0 - ADAM_B2) * (g * g)
    m_hat = m / (1.0 - ADAM_B1 ** ADAM_STEP)
    v_hat = v / (1.0 - ADAM_B2 ** ADAM_STEP)
    return -ADAM_LR * (m_hat / (jnp.sqrt(v_hat) + ADAM_EPS) + ADAM_WD * w), m, v


def _sum_adamw(own, arrived, w, m, v, name, steps, after=()):
    rows = own.shape[0]
    br = rows // steps

    def body(own_ref, arr_ref, w_ref, m_ref, v_ref, *rest):
        g_out, d_out, m_out, v_out = rest[len(after):]
        g = own_ref[...].astype(f32)
        for k in range(3):
            g = g + arr_ref[k].astype(f32)
        g_out[...] = g
        d_out[...], m_out[...], v_out[...] = _adam_update(w_ref[...], g, m_ref[...], v_ref[...])

    blk = pl.BlockSpec((br, D), lambda i: (i, 0))
    return pl.pallas_call(
        body, name=name, grid=(steps,), out_shape=(jax.ShapeDtypeStruct((rows, D), f32),) * 4,
        in_specs=[blk, pl.BlockSpec((3, br, D), lambda i: (0, i, 0)), blk, blk, blk]
        + [pl.BlockSpec(memory_space=pl.ANY)] * len(after), out_specs=(blk,) * 4,
        compiler_params=pltpu.CompilerParams(dimension_semantics=("parallel",), vmem_limit_bytes=V7X_VMEM_LIMIT),
    )(own, arrived, w, m, v, *after)


def _adamw(ws, gs, ms, vs, packed, name):
    n = len(ws)
    given = [g for g in gs if not isinstance(g, int)]
    taken = [j for j in range(n) if isinstance(gs[j], int)]

    def body(packed_ref, *refs):
        w_r, m_r, v_r = (refs[k * n:(k + 1) * n] for k in range(3))
        given_r, outs = list(refs[3 * n:3 * n + len(given)]), refs[3 * n + len(given):]
        g_o, outs = dict(zip(taken, outs[:len(taken)])), outs[len(taken):]
        d_o, m_o, v_o = (outs[k * n:(k + 1) * n] for k in range(3))
        for j in range(n):
            if j in g_o:
                (r, c), at = ws[j].shape, gs[j]
                if c == 128:
                    g = packed_ref[at:at + r, :]
                else:
                    assert r == 1
                    g = jnp.concatenate([packed_ref[at + k:at + k + 1, :] for k in range(c // 128)], axis=1)
                g_o[j][...] = g
            else:
                g = given_r.pop(0)[...]
            d_o[j][...], m_o[j][...], v_o[j][...] = _adam_update(w_r[j][...], g, m_r[j][...], v_r[j][...])

    vm = pl.BlockSpec(memory_space=pltpu.VMEM)
    shapes = tuple(jax.ShapeDtypeStruct(w.shape, f32) for w in ws)
    n_out = len(taken) + 3 * n
    return pl.pallas_call(
        body, name=name, out_shape=tuple(shapes[j] for j in taken) + shapes * 3,
        in_specs=[vm] * (1 + 3 * n + len(given)), out_specs=tuple([vm] * n_out),
        compiler_params=pltpu.CompilerParams(vmem_limit_bytes=V7X_VMEM_LIMIT),
    )(packed, *ws, *ms, *vs, *given)


SMALL_FFN = (("ln1_g", D), ("ln1_b", D), ("ln2_g", D), ("ln2_b", D), ("conv_b", D_FF), ("conv_w", 3 * D_FF), ("loss", 128))
SMALL_FFN_AT = 520
SMALL_ROWS = 704


def _small_rows():
    rows, at = {"w_pool": 0, "pool_scale": GROUPS * DH}, SMALL_FFN_AT
    for k, size in SMALL_FFN:
        rows[k] = at
        at += size // 128
    return rows


def kernel(x, w_in, w_pool, pool_scale, w_out, ln1_g, ln1_b, w_up, conv_w, conv_b, w_down, ln2_g, ln2_b, loss_target, m_w_in, m_w_pool, m_pool_scale, m_w_out, m_ln1_g, m_ln1_b, m_w_up, m_conv_w, m_conv_b, m_w_down, m_ln2_g, m_ln2_b, v_w_in, v_w_pool, v_pool_scale, v_w_out, v_ln1_g, v_ln1_b, v_w_up, v_conv_w, v_conv_b, v_w_down, v_ln2_g, v_ln2_b):
    me = 4 * lax.axis_index("x") + 2 * lax.axis_index("y") + lax.axis_index("c")
    x2, tgt = x[0], loss_target[0]

    cos, sin = _rope_tables()
    dmat, qd, kd, cdec = _decay_tables(RET_TILE)

    qkv, g, oret, states, cat, pooled, xhat1, rstd1, x1b, xb, g_in, g_out, g_up, g_down, g_cw = _mix_forward(
        x2, w_in[0].T, w_out[0], cos, sin, dmat, qd, kd, cdec, w_pool[0], pool_scale, ln1_g, ln1_b,
        gather_bf16=[w_up[0].T, w_down[0]], gather=[jnp.transpose(conv_w, (1, 0, 2))])
    w_in_t = g_in.reshape(IN_W, D)
    w_out_f = g_out.reshape(D, D)
    w_up_t = g_up.reshape(2 * D_FF, D)
    w_down_f = g_down.reshape(D_FF, D)
    conv_w_f = jnp.transpose(g_cw[:, :, 0, :], (1, 0, 2)).reshape(3, D_FF)
    dz1, dz2b, du, f, loss8, d_ln2_g, d_ln2_b, d_ln1_g, d_ln1_b, d_conv_b, d_conv_w = _ffn_forward_backward(
        xhat1, rstd1, ln1_g, ln1_b, w_up_t, conv_w_f, conv_b, w_down_f, ln2_g, ln2_b, tgt)
    small_ffn = [d_ln1_g, d_ln1_b, d_ln2_g, d_ln2_b, d_conv_b, d_conv_w, loss8]

    (dw_down,) = _weight_grad(f, dz2b, "grad_w_down", tm=D_FF // 2)
    own_down, oth_down = _pair_reduce([dw_down.reshape(N_DEV, ROWS_DOWN, D)], "pair_reduce_down")
    dw_up_t, arr_down = _weight_grad(du, x1b, "grad_w_up", tm=D_FF // 2, exchange=[oth_down])
    own_up, oth_up = _pair_reduce([dw_up_t.reshape(N_DEV, ROWS_UP, D)], "pair_reduce_up")
    up_sems, up_src, up_land, up_started = _exchange_start(oth_up, "exchange_up_start", CHIP_BARRIER_SPLIT)
    dproj, grad_x, small, dw_out = _mix_backward(
        dz1, w_out_f, qkv, g, oret, states, pooled, cat, cos, sin, dmat, qd, kd, cdec, w_pool[0], pool_scale, w_in_t,
        small_ffn, after=up_started)
    own_out, own_small, oth_out, oth_small = _pair_reduce(
        [dw_out.reshape(N_DEV, ROWS_OUT, D), small.reshape(N_DEV, SMALL_ROWS // N_DEV, 128)], "pair_reduce_out")
    dw_in_t, arr_out, arr_small = _weight_grad(dproj, xb, "grad_w_in", tm=IN_W // 2, exchange=[oth_out, oth_small])
    arr_up = _exchange_wait(up_sems, up_src, up_land, [dw_in_t], "exchange_up_wait")
    own_in, oth_in = _pair_reduce([dw_in_t.reshape(N_DEV, ROWS_IN, D)], "pair_reduce_in")
    in_sems, in_src, in_land, started = _exchange_start(oth_in, "exchange_in_start", CHIP_BARRIER)
    (small_piece,) = _sum_parts([own_small], [arr_small], "sum_small_grads")
    (gs_small,) = _all_gather([small_piece], "gather_small_grads")

    names = ["w_in", "w_pool", "pool_scale", "w_out", "ln1_g", "ln1_b", "w_up", "conv_w", "conv_b", "w_down",
             "ln2_g", "ln2_b"]
    w_d = dict(w_in=w_in, w_pool=w_pool, pool_scale=pool_scale, w_out=w_out, ln1_g=ln1_g, ln1_b=ln1_b, w_up=w_up,
               conv_w=conv_w, conv_b=conv_b, w_down=w_down, ln2_g=ln2_g, ln2_b=ln2_b)
    m_d = dict(w_in=m_w_in, w_pool=m_w_pool, pool_scale=m_pool_scale, w_out=m_w_out, ln1_g=m_ln1_g, ln1_b=m_ln1_b,
               w_up=m_w_up, conv_w=m_conv_w, conv_b=m_conv_b, w_down=m_w_down, ln2_g=m_ln2_g, ln2_b=m_ln2_b)
    v_d = dict(w_in=v_w_in, w_pool=v_w_pool, pool_scale=v_pool_scale, w_out=v_w_out, ln1_g=v_ln1_g, ln1_b=v_ln1_b,
               w_up=v_w_up, conv_w=v_conv_w, conv_b=v_conv_b, w_down=v_w_down, ln2_g=v_ln2_g, ln2_b=v_ln2_b)
    g_d, delta, new_m, new_v = {}, {}, {}, {}

    def big_adamw(k, own, arr, transposed, steps, after=()):
        lay = (lambda a: a[0].T) if transposed else (lambda a: a[0])
        back = (lambda a: a.T[None]) if transposed else (lambda a: a[None])
        res = _sum_adamw(own, arr, lay(w_d[k]), lay(m_d[k]), lay(v_d[k]), "adamw_" + k, steps, after)
        g_d[k], delta[k], new_m[k], new_v[k] = (back(r) for r in res)
        return res[3]

    done = [big_adamw("w_up", own_up, arr_up, True, 4, after=(started,)),
            big_adamw("w_down", own_down, arr_down, False, 2, after=(started,)),
            big_adamw("w_out", own_out, arr_out, False, 2, after=(started,))]

    gs_small, rows = gs_small.reshape(SMALL_ROWS, 128), _small_rows()
    g_conv_w = gs_small[rows["conv_w"]:rows["conv_w"] + 3 * D_FF // 128].reshape(3, D_FF)
    g_d["conv_w"] = lax.dynamic_slice(g_conv_w, (0, me * (D_FF // N_DEV)), (3, D_FF // N_DEV))[None]
    lay = lambda k, a: jnp.transpose(a, (1, 0, 2)) if k == "conv_w" else a.reshape(-1, a.shape[-1])
    back = lambda k, a: jnp.transpose(a, (1, 0, 2)) if k == "conv_w" else a.reshape(w_d[k].shape)
    group = [k for k in names if k not in ("w_in", "w_out", "w_up", "w_down")]
    packed = [k for k in group if k != "conv_w"]
    res = _adamw([lay(k, w_d[k]) for k in group], [lay(k, g_d[k]) if k == "conv_w" else rows[k] for k in group],
                 [lay(k, m_d[k]) for k in group], [lay(k, v_d[k]) for k in group], gs_small, "adamw_small")
    for j, k in enumerate(packed):
        g_d[k] = back(k, res[j])
    for j, k in enumerate(group):
        delta[k], new_m[k], new_v[k] = (back(k, res[len(packed) + part * len(group) + j]) for part in range(3))

    arr_in = _exchange_wait(in_sems, in_src, in_land, done + [res[0]], "exchange_in_wait")
    big_adamw("w_in", own_in, arr_in, True, 4)

    loss = gs_small[rows["loss"], 0]
    return (loss, grad_x[None], *[g_d[k] for k in names], *[delta[k] for k in names], *[new_m[k] for k in names],
            *[new_v[k] for k in names])
```

```python
import math

import numpy as np
import jax
import jax.numpy as jnp
from jax import lax
from jax.experimental import pallas as pl
from jax.experimental.pallas import tpu as pltpu

f32 = jnp.float32
bf16 = jnp.bfloat16

N_DEV = 8
T = 4096
D = 1024
CHUNK = 64
MIX_TILE = 512
RET_TILE = 256
HEADS = 4
DH = 128
RW = HEADS * DH
PW = 512
GROUPS = 4
WINDOWS = (2, 4, 8, 16)
IN_W = 4 * RW + PW
D_FF = 2816
LN_EPS = 1e-5
RMS_EPS = 1e-6
ALPHA = 2.0 ** 0.25
K_SCALE = DH ** -0.5

ADAM_LR = 0.001
ADAM_B1 = 0.9
ADAM_B2 = 0.999
ADAM_EPS = 1e-08
ADAM_WD = 0.01
ADAM_STEP = 10

ROWS_IN, ROWS_OUT, ROWS_UP, ROWS_DOWN = IN_W // N_DEV, D // N_DEV, 2 * D_FF // N_DEV, D_FF // N_DEV

V7X_VMEM_LIMIT = 56 * 2 ** 20
HALO = 32

NT = (((1,), (1,)), ((), ()))
TN = (((0,), (0,)), ((), ()))
NN = (((1,), (0,)), ((), ()))


def _dot(a, b, dims=NN):
    return lax.dot_general(a, b, dims, preferred_element_type=f32)


def _const_spec(shape):
    zeros = (0,) * len(shape)
    return pl.BlockSpec(shape, lambda i: zeros, pipeline_mode=pl.Buffered(1))


def _sigmoid(x):
    return 0.5 * jnp.tanh(0.5 * x) + 0.5


def _decay_tables(tt):
    h = np.arange(HEADS, dtype=np.float64)
    log_gamma = np.log(1.0 - 2.0 ** (-5.0 - h)).astype(np.float32).astype(np.float64)[:, None, None]
    idx = np.arange(tt, dtype=np.float64)
    visible = (idx[None, :] // CHUNK) <= (idx[:, None] // CHUNK)
    mask = np.where(visible[None], np.exp(log_gamma * np.abs(idx[:, None] - idx[None, :])[None]), 0.0)
    qd = np.broadcast_to(np.exp(log_gamma * (idx[None, :, None] + 1.0)), (HEADS, tt, DH))
    kd = np.broadcast_to(np.exp(log_gamma * (tt - 1.0 - idx[None, :, None])), (HEADS, tt, DH))
    cd = np.exp(log_gamma[:, 0, 0] * tt)
    return (jnp.asarray(mask, f32), jnp.asarray(qd, f32), jnp.asarray(kd, f32), [float(c) for c in cd])


def _rope_tables():
    inv_freq = (10000.0 ** (-np.arange(0, DH, 2, dtype=np.float64) / DH)).astype(np.float32)
    ang = (np.arange(T, dtype=np.float32)[:, None] * inv_freq[None, :]).astype(np.float64)
    cos, sin = np.cos(ang), np.sin(ang)
    return (jnp.asarray(np.concatenate([cos, cos], axis=1), f32), jnp.asarray(np.concatenate([-sin, sin], axis=1), f32))


def _swap_halves(t):
    return pltpu.roll(t, DH // 2, axis=1)


def _mix_forward(x, w_in_shard, w_out_shard, cos, sin, dmat, qd, kd, cdec, w_pool, pool_scale, ln1_g, ln1_b,
                 gather_bf16, gather, tt=MIX_TILE):
    n_tiles = T // tt
    to_bf16 = [w_in_shard, w_out_shard] + list(gather_bf16)
    n_c, n_g = len(to_bf16), len(gather_bf16) + len(gather)

    def body(x_ref, cos_ref, sin_ref, dmat_ref, qd_ref, kd_ref, wpool_ref, pscale_ref, g1_ref, b1_ref, *rest):
        f32_in, plain_in, rest = rest[:n_c], rest[n_c:2 + n_g], rest[2 + n_g:]
        qkv_ref, g_ref, oret_ref, states_ref, cat_ref, pooled_ref, xhat_ref, rstd_ref, x1b_ref, xb_ref = rest[:10]
        fout, gout = rest[10:12], rest[12:12 + n_g]
        state_s, pext_s, tmp_s, wint_s, wout_s, load_sems, stage_sems, *rest = rest[12 + n_g:]
        stage_s, cast_s, sems = rest[:n_c], rest[n_c:2 * n_c], rest[2 * n_c:]
        fin, gin, fsems, gsems = cast_s[:2], tuple(cast_s[2:]) + tuple(plain_in), sems[:3], sems[3:]
        i = pl.program_id(0)

        @pl.when(i == 0)
        def _():
            stage = [pltpu.make_async_copy(src, dst, stage_sems.at[j]) for j, (src, dst) in enumerate(zip(f32_in, stage_s))]
            for cp in stage:
                cp.start()
            state_s[...] = jnp.zeros_like(state_s)
            pext_s[:, pl.ds(0, HALO), :] = jnp.zeros((GROUPS, HALO, DH), f32)

            def cast(js):
                for j in js:
                    stage[j].wait()
                    cast_s[j][...] = stage_s[j][...].astype(bf16)

            _barrier(_gather_peers())
            cast(range(2))
            _gather_start(fin, fout, *fsems, barrier=False)
            cast(range(2, n_c))
            _gather_forward(fin, fout, *fsems)
            _gather_start(gin, gout, *gsems, barrier=False)
            _gather_finish(fin, fout, *fsems)
            loads = [pltpu.make_async_copy(src.at[s], dst.at[pl.ds(s * src.shape[1], src.shape[1]), :],
                                           load_sems.at[j, s])
                     for j, (src, dst) in enumerate(((fout[0], wint_s), (fout[1], wout_s))) for s in range(N_DEV)]
            for ld in loads:
                ld.start()
            for ld in loads:
                ld.wait()

        @pl.when(i == n_tiles - 3)
        def _():
            _gather_forward(gin, gout, *gsems)

        xb = x_ref[...].astype(bf16)
        xb_ref[...] = xb
        cos_t, sin_t = cos_ref[...], sin_ref[...]
        for part in range(2):
            pr = _dot(xb, wint_s[pl.ds(part * RW, RW), :], NT)
            for h in range(HEADS):
                t = pr[:, h * DH:(h + 1) * DH]
                r = t * cos_t + _swap_halves(t) * sin_t
                if part == 1:
                    r = r * K_SCALE
                qkv_ref[:, part * RW + h * DH: part * RW + (h + 1) * DH] = r.astype(bf16)
        qkv_ref[:, 2 * RW:3 * RW] = _dot(xb, wint_s[pl.ds(2 * RW, RW), :], NT).astype(bf16)
        g_ref[...] = _dot(xb, wint_s[pl.ds(3 * RW, RW), :], NT)
        p = _dot(xb, wint_s[pl.ds(4 * RW, PW), :], NT)
        for gi in range(GROUPS):
            pext_s[gi, pl.ds(HALO, tt), :] = p[:, gi * DH:(gi + 1) * DH]

        for sub in range(tt // RET_TILE):
            rows = pl.ds(sub * RET_TILE, RET_TILE)
            for h in range(HEADS):
                q = qkv_ref[rows, h * DH:(h + 1) * DH]
                k = qkv_ref[rows, RW + h * DH: RW + (h + 1) * DH]
                v = qkv_ref[rows, 2 * RW + h * DH: 2 * RW + (h + 1) * DH]
                s = _dot(q, k, NT) * dmat_ref[h]
                st = state_s[h]
                stb = st.astype(bf16)
                states_ref[sub, h] = stb
                oret_ref[rows, h * DH:(h + 1) * DH] = (_dot(s.astype(bf16), v)
                                                      + _dot((q.astype(f32) * qd_ref[h]).astype(bf16), stb))
                state_s[h] = st * cdec[h] + _dot((k.astype(f32) * kd_ref[h]).astype(bf16), v, TN)

        for h in range(HEADS):
            sl = slice(h * DH, (h + 1) * DH)
            o = oret_ref[:, sl]
            r = lax.rsqrt(jnp.mean(o * o, axis=-1, keepdims=True) + RMS_EPS)
            gg = g_ref[:, sl]
            cat_ref[:, sl] = (o * r * (gg * _sigmoid(gg))).astype(bf16)

        pos1 = (i * tt + lax.broadcasted_iota(jnp.int32, (tt, 1), 0) + 1).astype(f32)
        for gi, w in enumerate(WINDOWS):
            sl = slice(gi * DH, (gi + 1) * DH)
            stages = int(math.log2(w))
            src = pext_s
            for s in range(stages):
                lo = HALO - 8 * (stages - 1 - s)
                n = tt + HALO - lo
                shift = 2 ** s
                val = src[gi, pl.ds(lo, n), :] + src[gi, pl.ds(lo - shift, n), :]
                if s == stages - 1:
                    wsum = val
                else:
                    tmp_s[gi, pl.ds(lo, n), :] = val
                    src = tmp_s
            p_g = pext_s[gi, pl.ds(HALO, tt), :]
            pooled = (wsum / jnp.minimum(pos1, float(w)) - p_g).astype(bf16)
            pooled_ref[:, sl] = pooled
            y = _dot(pooled, wpool_ref[gi].astype(bf16)) * pscale_ref[:, sl]
            cat_ref[:, RW + gi * DH: RW + (gi + 1) * DH] = y.astype(bf16)
        pext_s[:, pl.ds(0, HALO), :] = pext_s[:, pl.ds(tt, HALO), :]

        z = ALPHA * x_ref[...] + _dot(cat_ref[...], wout_s[...])
        mu = jnp.mean(z, axis=-1, keepdims=True)
        zc = z - mu
        rstd = lax.rsqrt(jnp.mean(zc * zc, axis=-1, keepdims=True) + LN_EPS)
        xhat = zc * rstd
        xhat_ref[...] = xhat
        rstd_ref[...] = rstd
        x1b_ref[...] = (xhat * g1_ref[...] + b1_ref[...]).astype(bf16)

        @pl.when(i == n_tiles - 1)
        def _():
            _gather_finish(gin, gout, *gsems)

    tile = lambda w: pl.BlockSpec((tt, w), lambda i: (i, 0))
    hbm = pl.BlockSpec(memory_space=pltpu.HBM)
    out_shape = (
        jax.ShapeDtypeStruct((T, 3 * RW), bf16),
        jax.ShapeDtypeStruct((T, RW), f32),
        jax.ShapeDtypeStruct((T, RW), f32),
        jax.ShapeDtypeStruct((T // RET_TILE, HEADS, DH, DH), bf16),
        jax.ShapeDtypeStruct((T, D), bf16),
        jax.ShapeDtypeStruct((T, PW), bf16),
        jax.ShapeDtypeStruct((T, D), f32),
        jax.ShapeDtypeStruct((T, 1), f32),
        jax.ShapeDtypeStruct((T, D), bf16),
        jax.ShapeDtypeStruct((T, D), bf16),
    ) + tuple(jax.ShapeDtypeStruct((N_DEV,) + b.shape, bf16) for b in to_bf16
              ) + tuple(jax.ShapeDtypeStruct((N_DEV,) + b.shape, b.dtype) for b in gather)
    return pl.pallas_call(
        body, name="mix_forward", grid=(n_tiles,), out_shape=out_shape,
        in_specs=[tile(D), tile(DH), tile(DH),
                  _const_spec((HEADS, RET_TILE, RET_TILE)), _const_spec((HEADS, RET_TILE, DH)),
                  _const_spec((HEADS, RET_TILE, DH)),
                  _const_spec((GROUPS, DH, DH)), _const_spec((1, PW)),
                  _const_spec((1, D)), _const_spec((1, D))] + [hbm] * (2 + n_g),
        out_specs=(tile(3 * RW), tile(RW), tile(RW),
                   pl.BlockSpec((tt // RET_TILE, HEADS, DH, DH), lambda i: (i, 0, 0, 0)),
                   tile(D), tile(PW), tile(D), tile(1), tile(D), tile(D)) + (hbm,) * (2 + n_g),
        scratch_shapes=[pltpu.VMEM((HEADS, DH, DH), f32), pltpu.VMEM((GROUPS, tt + HALO, DH), f32),
                        pltpu.VMEM((GROUPS, tt + HALO, DH), f32), pltpu.VMEM((IN_W, D), bf16), pltpu.VMEM((D, D), bf16),
                        pltpu.SemaphoreType.DMA((2, N_DEV)), pltpu.SemaphoreType.DMA((n_c,))]
        + [pltpu.VMEM(b.shape, f32) for b in to_bf16] + [pltpu.VMEM(b.shape, bf16) for b in to_bf16]
        + _gather_sems(2) + _gather_sems(n_g),
        compiler_params=pltpu.CompilerParams(dimension_semantics=("arbitrary",), vmem_limit_bytes=V7X_VMEM_LIMIT,
                                             collective_id=GATHER_BARRIER),
    )(x, cos, sin, dmat, qd, kd, w_pool, pool_scale, ln1_g, ln1_b, *to_bf16, *gather)


def _ffn_forward_backward(xhat1, rstd1, ln1_g, ln1_b, w_up_t, conv_w, conv_b, w_down, ln2_g, ln2_b, target,
                          tt=256):
    n_tiles = T // tt
    FH = 16
    hb = tt // FH

    def body(xhat_ref, halo_ref, rstd_ref, g1_ref, b1_ref, wupt_ref, cw_ref, cb_ref, wdown_ref, g2_ref, b2_ref, tgt_ref,
             dz1_ref, dz2b_ref, du_ref, f_ref, loss_ref, dg2_ref, db2_ref, dg1_ref, db1_ref, dcb_ref, dcw_ref,
             gext_s, val_s, dhext_s):
        i = pl.program_id(0)
        tile_idx = n_tiles - 1 - i

        def rd(ref, off):
            return jnp.concatenate([ref[k, pl.ds(off, tt), :] for k in range(D_FF // 128)], axis=1)

        def wr(ref, val):
            for k in range(D_FF // 128):
                ref[k, pl.ds(0, val.shape[0]), :] = val[:, k * 128:(k + 1) * 128]

        @pl.when(i == 0)
        def _():
            for r in (loss_ref, dg2_ref, db2_ref, dg1_ref, db1_ref, dcb_ref, dcw_ref):
                r[...] = jnp.zeros_like(r)
            dhext_s[:, pl.ds(tt, 8), :] = jnp.zeros((D_FF // 128, 8, 128), f32)

        g1, b1 = g1_ref[...], b1_ref[...]
        xhat = xhat_ref[...]
        x1 = xhat * g1 + b1
        x1b = x1.astype(bf16)
        x1h = ((halo_ref[...] * g1 + b1) * jnp.where(tile_idx == 0, 0.0, 1.0)).astype(bf16)
        x1ext = jnp.concatenate([x1h, x1b], axis=0)

        val = _dot(x1b, wupt_ref[pl.ds(0, D_FF), :], NT)
        gate_ext = _dot(x1ext, wupt_ref[pl.ds(D_FF, D_FF), :], NT)
        wr(gext_s, gate_ext)
        hh = (cb_ref[...] + cw_ref[0:1, :] * rd(gext_s, FH - 2) + cw_ref[1:2, :] * rd(gext_s, FH - 1)
              + cw_ref[2:3, :] * gate_ext[FH:])
        sg = _sigmoid(hh)
        act = hh * sg
        wr(dhext_s, act)
        val_s[...] = val * (sg + act * (1.0 - sg))
        fb = (act * val).astype(bf16)
        f_ref[...] = fb

        z = ALPHA * x1 + _dot(fb, wdown_ref[...])
        mu = jnp.mean(z, axis=-1, keepdims=True)
        zc = z - mu
        rstd2 = lax.rsqrt(jnp.mean(zc * zc, axis=-1, keepdims=True) + LN_EPS)
        xh2 = zc * rstd2
        diff = xh2 * g2_ref[...] + b2_ref[...] - tgt_ref[...]
        loss_ref[...] += 0.5 * jnp.sum(diff * diff) / D
        dy = diff * (1.0 / D)
        dg2_ref[...] += jnp.sum(dy * xh2, axis=0, keepdims=True)
        db2_ref[...] += jnp.sum(dy, axis=0, keepdims=True)
        dyg = dy * g2_ref[...]
        dz2 = rstd2 * (dyg - jnp.mean(dyg, axis=-1, keepdims=True) - xh2 * jnp.mean(dyg * xh2, axis=-1, keepdims=True))
        dz2b = dz2.astype(bf16)
        dz2b_ref[...] = dz2b

        df = _dot(dz2b, wdown_ref[...], NT)
        dval = df * rd(dhext_s, 0)
        dh = df * val_s[...]
        wr(dhext_s, dh)
        dh1, dh2, g0 = rd(dhext_s, 1), rd(dhext_s, 2), rd(gext_s, FH)
        dcb_ref[...] += jnp.sum(dh, axis=0, keepdims=True)
        dcw_ref[0:1, :] += jnp.sum(dh2 * g0, axis=0, keepdims=True)
        dcw_ref[1:2, :] += jnp.sum(dh1 * g0, axis=0, keepdims=True)
        dcw_ref[2:3, :] += jnp.sum(dh * g0, axis=0, keepdims=True)
        dgate = cw_ref[2:3, :] * dh + cw_ref[1:2, :] * dh1 + cw_ref[0:1, :] * dh2
        dvalb, dgateb = dval.astype(bf16), dgate.astype(bf16)
        du_ref[:, :D_FF] = dvalb
        du_ref[:, D_FF:] = dgateb
        dx1 = ALPHA * dz2 + _dot(dvalb, wupt_ref[pl.ds(0, D_FF), :]) + _dot(dgateb, wupt_ref[pl.ds(D_FF, D_FF), :])
        dhext_s[:, pl.ds(tt, 8), :] = dhext_s[:, pl.ds(0, 8), :]

        dg1_ref[...] += jnp.sum(dx1 * xhat, axis=0, keepdims=True)
        db1_ref[...] += jnp.sum(dx1, axis=0, keepdims=True)
        dxg = dx1 * g1
        dz1_ref[...] = rstd_ref[...] * (dxg - jnp.mean(dxg, axis=-1, keepdims=True)
                                        - xhat * jnp.mean(dxg * xhat, axis=-1, keepdims=True))

    rtile = lambda w: pl.BlockSpec((tt, w), lambda i: (n_tiles - 1 - i, 0))
    acc = lambda shape: pl.BlockSpec(shape, lambda i: (0, 0))
    out_shape = (
        jax.ShapeDtypeStruct((T, D), f32),
        jax.ShapeDtypeStruct((T, D), bf16),
        jax.ShapeDtypeStruct((T, 2 * D_FF), bf16),
        jax.ShapeDtypeStruct((T, D_FF), bf16),
        jax.ShapeDtypeStruct((8, 128), f32),
        jax.ShapeDtypeStruct((1, D), f32), jax.ShapeDtypeStruct((1, D), f32),
        jax.ShapeDtypeStruct((1, D), f32), jax.ShapeDtypeStruct((1, D), f32),
        jax.ShapeDtypeStruct((1, D_FF), f32), jax.ShapeDtypeStruct((3, D_FF), f32),
    )
    return pl.pallas_call(
        body, name="ffn_forward_backward", grid=(n_tiles,), out_shape=out_shape,
        in_specs=[rtile(D),
                  pl.BlockSpec((FH, D), lambda i: (jnp.maximum((n_tiles - 1 - i) * hb - 1, 0), 0)),
                  rtile(1), _const_spec((1, D)), _const_spec((1, D)), _const_spec((2 * D_FF, D)),
                  _const_spec((3, D_FF)), _const_spec((1, D_FF)), _const_spec((D_FF, D)),
                  _const_spec((1, D)), _const_spec((1, D)), rtile(D)],
        out_specs=(rtile(D), rtile(D), rtile(2 * D_FF), rtile(D_FF), acc((8, 128)),
                   acc((1, D)), acc((1, D)), acc((1, D)), acc((1, D)), acc((1, D_FF)), acc((3, D_FF))),
        scratch_shapes=[pltpu.VMEM((D_FF // 128, tt + FH, 128), f32), pltpu.VMEM((tt, D_FF), f32),
                        pltpu.VMEM((D_FF // 128, tt + 8, 128), f32)],
        compiler_params=pltpu.CompilerParams(dimension_semantics=("arbitrary",), vmem_limit_bytes=V7X_VMEM_LIMIT),
    )(xhat1, xhat1, rstd1, ln1_g, ln1_b, w_up_t, conv_w, conv_b, w_down, ln2_g, ln2_b, target)


def _mix_backward(dz1, w_out, qkv, g, oret, states, pooled, cat, cos, sin, dmat, qd, kd, cdec, w_pool, pool_scale, w_in_t,
                  small_ffn, after, tt=MIX_TILE):
    n_tiles = T // tt

    def body(dz1_ref, wout_ref, qkv_ref, g_ref, oret_ref, states_ref, pooled_ref, cat_ref, cos_ref, sin_ref, dmat_ref,
             qd_ref, kd_ref, wpool_ref, pscale_ref, wint_ref, *rest):
        ffn_refs, rest = rest[:len(SMALL_FFN)], rest[len(SMALL_FFN):]
        after_ref, dproj_ref, gx_ref, small_ref, dwout_ref, dstate_s, dout_s, eext_s, tmp_s, dwout_s, dpscale_s = rest
        i = pl.program_id(0)
        tile_idx = n_tiles - 1 - i

        @pl.when(i == 0)
        def _():
            dstate_s[...] = jnp.zeros_like(dstate_s)
            small_ref[...] = jnp.zeros_like(small_ref)
            dpscale_s[...] = jnp.zeros_like(dpscale_s)
            dwout_s[...] = jnp.zeros_like(dwout_s)
            eext_s[:, pl.ds(tt, HALO), :] = jnp.zeros((GROUPS, HALO, DH), f32)

        dz1 = dz1_ref[...]
        dz1b = dz1.astype(bf16)
        dcat = _dot(dz1b, wout_ref[...], NT)
        dwout_s[...] += _dot(cat_ref[...], dz1b, TN)

        pos1 = (tile_idx * tt + lax.broadcasted_iota(jnp.int32, (tt, 1), 0) + 1).astype(f32)
        for gi, w in enumerate(WINDOWS):
            sl = slice(gi * DH, (gi + 1) * DH)
            dpo = dcat[:, RW + gi * DH: RW + (gi + 1) * DH]
            pooled_g = pooled_ref[:, sl]
            wpool_g = wpool_ref[gi].astype(bf16)
            ylin = _dot(pooled_g, wpool_g)
            dpscale_s[:, sl] += jnp.sum(dpo * ylin, axis=0, keepdims=True)
            dpw = (dpo * pscale_ref[:, sl]).astype(bf16)
            small_ref[pl.ds(gi * DH, DH), :] += _dot(pooled_g, dpw, TN)
            dpooled = _dot(dpw, wpool_g, NT)
            eext_s[gi, pl.ds(0, tt), :] = dpooled / jnp.minimum(pos1, float(w))
            stages = int(math.log2(w))
            src = eext_s
            for s in range(stages):
                n = tt + 8 * (stages - 1 - s)
                shift = 2 ** s
                val = src[gi, pl.ds(0, n), :] + src[gi, pl.ds(shift, n), :]
                if s == stages - 1:
                    wsum = val
                else:
                    tmp_s[gi, pl.ds(0, n), :] = val
                    src = tmp_s
            dproj_ref[:, 4 * RW + gi * DH: 4 * RW + (gi + 1) * DH] = (wsum - dpooled).astype(bf16)
        eext_s[:, pl.ds(tt, HALO), :] = eext_s[:, pl.ds(0, HALO), :]

        for h in range(HEADS):
            sl = slice(h * DH, (h + 1) * DH)
            dr = dcat[:, sl]
            o = oret_ref[:, sl]
            r = lax.rsqrt(jnp.mean(o * o, axis=-1, keepdims=True) + RMS_EPS)
            rn = o * r
            gg = g_ref[:, sl]
            sg = _sigmoid(gg)
            dproj_ref[:, 3 * RW + h * DH: 3 * RW + (h + 1) * DH] = (dr * rn * (sg * (1.0 + gg * (1.0 - sg)))).astype(bf16)
            drn = dr * (gg * sg)
            dout_s[:, sl] = (r * (drn - rn * jnp.mean(drn * rn, axis=-1, keepdims=True))).astype(bf16)

        for sub in reversed(range(tt // RET_TILE)):
            rows = pl.ds(sub * RET_TILE, RET_TILE)
            cos_t, sin_t = cos_ref[rows, :], sin_ref[rows, :]
            for h in range(HEADS):
                q = qkv_ref[rows, h * DH:(h + 1) * DH]
                k = qkv_ref[rows, RW + h * DH: RW + (h + 1) * DH]
                v = qkv_ref[rows, 2 * RW + h * DH: 2 * RW + (h + 1) * DH]
                do = dout_s[rows, h * DH:(h + 1) * DH]
                stb = states_ref[sub, h]
                dst = dstate_s[h]
                dstb = dst.astype(bf16)
                sb = (_dot(q, k, NT) * dmat_ref[h]).astype(bf16)
                dsb = (_dot(do, v, NT) * dmat_ref[h]).astype(bf16)
                dq = _dot(dsb, k) + _dot(do, stb, NT) * qd_ref[h]
                dk = _dot(dsb, q, TN) + _dot(v, dstb, NT) * kd_ref[h]
                dv = _dot(sb, do, TN) + _dot((k.astype(f32) * kd_ref[h]).astype(bf16), dstb)
                dstate_s[h] = dst * cdec[h] + _dot((q.astype(f32) * qd_ref[h]).astype(bf16), do, TN)
                dproj_ref[rows, h * DH:(h + 1) * DH] = (dq * cos_t - _swap_halves(dq) * sin_t).astype(bf16)
                dproj_ref[rows, RW + h * DH: RW + (h + 1) * DH] = (
                    (dk * cos_t - _swap_halves(dk) * sin_t) * K_SCALE).astype(bf16)
                dproj_ref[rows, 2 * RW + h * DH: 2 * RW + (h + 1) * DH] = dv.astype(bf16)

        gx_ref[...] = ALPHA * dz1 + _dot(dproj_ref[...], wint_ref[...])

        @pl.when(i == n_tiles - 1)
        def _():
            dwout_ref[...] = dwout_s[...].astype(bf16)
            at = GROUPS * DH
            for ref, size in [(dpscale_s, PW)] + [(ref, size) for ref, (_, size) in zip(ffn_refs, SMALL_FFN)]:
                for j in range(size // 128):
                    r, k = divmod(j, ref.shape[1] // 128)
                    small_ref[at + j: at + j + 1, :] = ref[r:r + 1, k * 128:(k + 1) * 128]
                at = SMALL_FFN_AT if ref is dpscale_s else at + size // 128

    rtile = lambda w: pl.BlockSpec((tt, w), lambda i: (n_tiles - 1 - i, 0))
    out_shape = (
        jax.ShapeDtypeStruct((T, IN_W), bf16),
        jax.ShapeDtypeStruct((T, D), f32),
        jax.ShapeDtypeStruct((SMALL_ROWS, 128), f32),
        jax.ShapeDtypeStruct((D, D), bf16),
    )
    return pl.pallas_call(
        body, name="mix_backward", grid=(n_tiles,), out_shape=out_shape,
        in_specs=[rtile(D), _const_spec((D, D)), rtile(3 * RW), rtile(RW), rtile(RW),
                  pl.BlockSpec((tt // RET_TILE, HEADS, DH, DH), lambda i: (n_tiles - 1 - i, 0, 0, 0)),
                  rtile(PW), rtile(D), rtile(DH), rtile(DH),
                  _const_spec((HEADS, RET_TILE, RET_TILE)), _const_spec((HEADS, RET_TILE, DH)),
                  _const_spec((HEADS, RET_TILE, DH)),
                  _const_spec((GROUPS, DH, DH)), _const_spec((1, PW)), _const_spec((IN_W, D)),
                  *[_const_spec(a.shape) for a in small_ffn], pl.BlockSpec(memory_space=pl.ANY)],
        out_specs=(rtile(IN_W), rtile(D), pl.BlockSpec((SMALL_ROWS, 128), lambda i: (0, 0)),
                   pl.BlockSpec((D, D), lambda i: (0, 0), pipeline_mode=pl.Buffered(1))),
        scratch_shapes=[pltpu.VMEM((HEADS, DH, DH), f32), pltpu.VMEM((tt, RW), bf16),
                        pltpu.VMEM((GROUPS, tt + HALO, DH), f32), pltpu.VMEM((GROUPS, tt + HALO, DH), f32),
                        pltpu.VMEM((D, D), f32), pltpu.VMEM((1, PW), f32)],
        compiler_params=pltpu.CompilerParams(dimension_semantics=("arbitrary",), vmem_limit_bytes=V7X_VMEM_LIMIT),
    )(dz1, w_out, qkv, g, oret, states, pooled, cat, cos, sin, dmat, qd, kd, w_pool, pool_scale, w_in_t, *small_ffn,
      after)


def _weight_grad(a, b, name, tm, exchange=()):
    m = a.shape[1]
    n_m, n_e = m // tm, len(exchange)

    def body(a_ref, b_ref, *rest):
        ein, o_ref, eout, sems = rest[:n_e], rest[n_e], rest[n_e + 1:2 * n_e + 1], rest[2 * n_e + 1:]
        i = pl.program_id(0)

        if n_e:
            @pl.when(i == 0)
            def _():
                _chip_exchange_start(ein, eout, *sems)

        o_ref[...] = _dot(a_ref[...], b_ref[...].astype(bf16), TN).astype(bf16)

        if n_e:
            @pl.when(i == n_m - 1)
            def _():
                _chip_exchange_finish(ein, eout, *sems)

    hbm = pl.BlockSpec(memory_space=pltpu.HBM)
    return pl.pallas_call(
        body, name=name, grid=(n_m,),
        out_shape=(jax.ShapeDtypeStruct((m, D), bf16),) + tuple(jax.ShapeDtypeStruct(e.shape, e.dtype) for e in exchange),
        in_specs=[pl.BlockSpec((T, tm), lambda i: (0, i)),
                  pl.BlockSpec((T, D), lambda i: (0, 0), pipeline_mode=pl.Buffered(1))] + [hbm] * n_e,
        out_specs=(pl.BlockSpec((tm, D), lambda i: (i, 0)),) + (hbm,) * n_e,
        scratch_shapes=_chip_exchange_sems(n_e),
        compiler_params=pltpu.CompilerParams(dimension_semantics=("arbitrary",), vmem_limit_bytes=V7X_VMEM_LIMIT,
                                             collective_id=CHIP_BARRIER if n_e else None),
    )(a, b, *exchange)


CHIP_FLIPS = ((1, 0), (0, 1), (1, 1))
PAIR_BARRIER, CHIP_BARRIER, GATHER_BARRIER, CHIP_BARRIER_SPLIT = 0, 1, 2, 3


def _barrier(peers):
    sem = pltpu.get_barrier_semaphore()
    for peer in peers:
        pl.semaphore_signal(sem, inc=1, device_id=peer, device_id_type=pl.DeviceIdType.MESH)
    pl.semaphore_wait(sem, len(peers))


def _me():
    return lax.axis_index("x"), lax.axis_index("y"), lax.axis_index("c")


def _chip(me, k):
    x, y, _ = me
    if k == 0:
        return x, y
    fx, fy = CHIP_FLIPS[k - 1]
    return (1 - x if fx else x), (1 - y if fy else y)


def _slot(x, y, c):
    return 4 * x + 2 * y + c


def _remote(src, dst, send_sem, recv_sem, to):
    return pltpu.make_async_remote_copy(src_ref=src, dst_ref=dst, send_sem=send_sem, recv_sem=recv_sem,
                                        device_id=to, device_id_type=pl.DeviceIdType.MESH)


def _gather_sems(n):
    return [pltpu.SemaphoreType.DMA((7, n)), pltpu.SemaphoreType.DMA((7, n)), pltpu.SemaphoreType.DMA((n,))] if n else []


def _gather_copy(k, j, gin, gout, send_sems, recv_sems, sending):
    x, y, c = _me()
    sibling, x_chip, y_chip, d_chip = (x, y, 1 - c), (1 - x, y), (x, 1 - y), (1 - x, 1 - y)
    south = c == 0
    passed_on = (jnp.where(south, 1 - x, x), jnp.where(south, y, 1 - y), c)
    src, to = gin[j], sibling
    if sending:
        block = {0: (x, y, c), 1: (x, y, c), 2: (x, y, c), 3: passed_on, 4: (*x_chip, c), 5: (*y_chip, c), 6: (*d_chip, c)}[k]
        to = {1: (*x_chip, c), 2: (*y_chip, c), 3: (jnp.where(south, x, 1 - x), jnp.where(south, 1 - y, y), c)}.get(k, sibling)
        if k >= 3:
            src = gout[j].at[_slot(*block)]
    else:
        block = {0: sibling, 1: (*x_chip, c), 2: (*y_chip, c), 3: (*d_chip, c), 4: (*x_chip, 1 - c), 5: (*y_chip, 1 - c),
                 6: (*d_chip, 1 - c)}[k]
    return _remote(src, gout[j].at[_slot(*block)], send_sems.at[k, j], recv_sems.at[k, j], to)


def _gather_do(ks, action, gin, gout, send_sems, recv_sems):
    for k in ks:
        for j in range(len(gin)):
            cp = _gather_copy(k, j, gin, gout, send_sems, recv_sems, action != "wait_recv")
            getattr(cp, action)()


def _gather_peers():
    x, y, c = _me()
    return [(x, y, 1 - c), (1 - x, y, c), (x, 1 - y, c)]


def _gather_start(gin, gout, send_sems, recv_sems, local_sems, barrier=True):
    if barrier:
        _barrier(_gather_peers())
    for j in range(len(gin)):
        pltpu.make_async_copy(gin[j], gout[j].at[_slot(*_me())], local_sems.at[j]).start()
    _gather_do((0, 1, 2), "start", gin, gout, send_sems, recv_sems)


def _gather_forward(gin, gout, send_sems, recv_sems, local_sems):
    _gather_do((1, 2), "wait_recv", gin, gout, send_sems, recv_sems)
    _gather_do((3, 4, 5), "start", gin, gout, send_sems, recv_sems)


def _gather_finish(gin, gout, send_sems, recv_sems, local_sems):
    _gather_do((3,), "wait_recv", gin, gout, send_sems, recv_sems)
    _gather_do((6,), "start", gin, gout, send_sems, recv_sems)
    _gather_do((0, 4, 5, 6), "wait_recv", gin, gout, send_sems, recv_sems)
    _gather_do(range(7), "wait_send", gin, gout, send_sems, recv_sems)
    for j in range(len(gin)):
        pltpu.make_async_copy(gin[j], gout[j].at[_slot(*_me())], local_sems.at[j]).wait()


def _all_gather(blocks, name):
    n = len(blocks)

    def body(*refs):
        gin, gout, sems = refs[:n], refs[n:2 * n], refs[2 * n:]
        _gather_start(gin, gout, *sems)
        _gather_forward(gin, gout, *sems)
        _gather_finish(gin, gout, *sems)

    hbm = pl.BlockSpec(memory_space=pltpu.HBM)
    return pl.pallas_call(
        body, name=name,
        out_shape=tuple(jax.ShapeDtypeStruct((N_DEV,) + b.shape, b.dtype) for b in blocks),
        in_specs=[hbm] * n, out_specs=(hbm,) * n, scratch_shapes=_gather_sems(n),
        compiler_params=pltpu.CompilerParams(collective_id=GATHER_BARRIER),
    )(*blocks)


def _pair_reduce(parts, name):
    n = len(parts)

    def body(*refs):
        ins, own, others, landing, mine = (refs[k * n:(k + 1) * n] for k in range(5))
        send_sems, recv_sems, local_sems = refs[5 * n:]
        me = _me()
        x, y, c = me
        sibling = (x, y, 1 - c)
        _barrier([sibling])
        sends, loads = [], []
        for k in range(4):
            for j in range(n):
                cp = _remote(ins[j].at[_slot(*_chip(me, k), 1 - c)], landing[j].at[k], send_sems.at[k, j],
                             recv_sems.at[k, j], sibling)
                cp.start()
                sends.append(cp)
                ld = pltpu.make_async_copy(ins[j].at[_slot(*_chip(me, k), c)], mine[j].at[k], local_sems.at[k, j])
                ld.start()
                loads.append(ld)
        for k in range(4):
            for j in range(n):
                loads[k * n + j].wait()
                _remote(ins[j].at[0], landing[j].at[k], send_sems.at[k, j], recv_sems.at[k, j], sibling).wait_recv()
                total = mine[j][k].astype(f32) + landing[j][k].astype(f32)
                if k == 0:
                    own[j][...] = total.astype(own[j].dtype)
                else:
                    others[j][k - 1] = total.astype(others[j].dtype)
        for cp in sends:
            cp.wait_send()

    vm = pl.BlockSpec(memory_space=pltpu.VMEM)
    return pl.pallas_call(
        body, name=name,
        out_shape=tuple(jax.ShapeDtypeStruct(p.shape[1:], p.dtype) for p in parts)
        + tuple(jax.ShapeDtypeStruct((3,) + p.shape[1:], p.dtype) for p in parts),
        in_specs=[pl.BlockSpec(memory_space=pltpu.HBM)] * n, out_specs=(vm,) * (2 * n),
        scratch_shapes=[pltpu.VMEM((4,) + p.shape[1:], p.dtype) for p in parts] * 2
        + [pltpu.SemaphoreType.DMA((4, n)), pltpu.SemaphoreType.DMA((4, n)), pltpu.SemaphoreType.DMA((4, n))],
        compiler_params=pltpu.CompilerParams(vmem_limit_bytes=V7X_VMEM_LIMIT, collective_id=PAIR_BARRIER),
    )(*parts)


def _chip_exchange_sems(n):
    return [pltpu.SemaphoreType.DMA((3, n)), pltpu.SemaphoreType.DMA((3, n))] if n else []


def _chip_exchange_copy(k, j, ein, eout, send_sems, recv_sems):
    me = _me()
    return _remote(ein[j].at[k - 1], eout[j].at[k - 1], send_sems.at[k - 1, j], recv_sems.at[k - 1, j],
                   (*_chip(me, k), me[2]))


def _chip_peers():
    me = _me()
    return [(*_chip(me, k), me[2]) for k in range(1, 4)]


def _chip_exchange_start(ein, eout, send_sems, recv_sems, barrier=True):
    if barrier:
        _barrier(_chip_peers())
    for k in range(1, 4):
        for j in range(len(ein)):
            _chip_exchange_copy(k, j, ein, eout, send_sems, recv_sems).start()


def _chip_exchange_finish(ein, eout, send_sems, recv_sems):
    for k in range(1, 4):
        for j in range(len(ein)):
            _chip_exchange_copy(k, j, ein, eout, send_sems, recv_sems).wait_recv()
    for k in range(1, 4):
        for j in range(len(ein)):
            _chip_exchange_copy(k, j, ein, eout, send_sems, recv_sems).wait_send()


def _split_copies(src_ref, dst_ref, sems):
    me = _me()
    return [_remote(src_ref.at[k - 1], dst_ref.at[k - 1], sems[k - 1], sems[2 + k], (*_chip(me, k), me[2]))
            for k in range(1, 4)]


def _exchange_start(others, name, barrier_id):
    def body(src_ref, land_ref, *rest):
        sems, token_ref = rest[:6], rest[8]
        _barrier(_chip_peers())
        for copy in _split_copies(src_ref, land_ref, sems):
            copy.start()
        token_ref[...] = jnp.zeros_like(token_ref)

    hbm, sem = pl.BlockSpec(memory_space=pltpu.HBM), pl.BlockSpec(memory_space=pltpu.SEMAPHORE)
    thru = pltpu.HBM(others.shape, others.dtype)
    res = pl.pallas_call(
        body, name=name,
        out_shape=(pltpu.SemaphoreType.DMA(()),) * 6 + (thru, thru, jax.ShapeDtypeStruct((8, 128), f32)),
        in_specs=(hbm, hbm), out_specs=(sem,) * 6 + (hbm, hbm, pl.BlockSpec(memory_space=pltpu.VMEM)),
        input_output_aliases={0: 6, 1: 7},
        compiler_params=pltpu.CompilerParams(has_side_effects=pltpu.SideEffectType.DATAFLOW_SIDE_EFFECTING,
                                             collective_id=barrier_id),
    )(pltpu.with_memory_space_constraint(others, pltpu.HBM),
      pltpu.with_memory_space_constraint(lax.empty(others.shape, others.dtype), pltpu.HBM))
    return res[:6], res[6], res[7], res[8]


def _exchange_wait(sems, src_thru, land_thru, after, name):
    n_after = len(after)

    def body(src_ref, land_ref, *rest):
        for copy in _split_copies(src_ref, land_ref, rest[:6]):
            copy.wait_send()
            copy.wait_recv()

    hbm, sem = pl.BlockSpec(memory_space=pltpu.HBM), pl.BlockSpec(memory_space=pltpu.SEMAPHORE)
    thru = pltpu.HBM(src_thru.shape, src_thru.dtype)
    return pl.pallas_call(
        body, name=name, out_shape=(thru, thru),
        in_specs=(hbm, hbm) + (sem,) * 6 + (pl.BlockSpec(memory_space=pl.ANY),) * n_after, out_specs=(hbm, hbm),
        input_output_aliases={0: 0, 1: 1},
        compiler_params=pltpu.CompilerParams(has_side_effects=pltpu.SideEffectType.DATAFLOW_SIDE_EFFECTING),
    )(src_thru, land_thru, *sems, *after)[1]


def _sum_parts(owns, arrived, name, after=()):
    n = len(owns)

    def body(*refs):
        for own, arr, out in zip(refs[:n], refs[n:2 * n], refs[2 * n + len(after):]):
            acc = own[...].astype(f32)
            for k in range(3):
                acc = acc + arr[k].astype(f32)
            out[...] = acc

    vm = pl.BlockSpec(memory_space=pltpu.VMEM)
    return pl.pallas_call(
        body, name=name, out_shape=tuple(jax.ShapeDtypeStruct(o.shape, f32) for o in owns),
        in_specs=[vm] * (2 * n) + [pl.BlockSpec(memory_space=pl.ANY)] * len(after), out_specs=(vm,) * n,
        compiler_params=pltpu.CompilerParams(vmem_limit_bytes=V7X_VMEM_LIMIT),
    )(*owns, *arrived, *after)


def _adam_update(w, g, m, v):
    m = ADAM_B1 * m + (1.0 - ADAM_B1) * g
    v = ADAM_B2 * v + (1.0 - ADAM_B2) * (g * g)
    m_hat = m / (1.0 - ADAM_B1 ** ADAM_STEP)
    v_hat = v / (1.0 - ADAM_B2 ** ADAM_STEP)
    return -ADAM_LR * (m_hat / (jnp.sqrt(v_hat) + ADAM_EPS) + ADAM_WD * w), m, v


def _sum_adamw(own, arrived, w, m, v, name, steps, after=()):
    rows = own.shape[0]
    br = rows // steps

    def body(own_ref, arr_ref, w_ref, m_ref, v_ref, *rest):
        g_out, d_out, m_out, v_out = rest[len(after):]
        g = own_ref[...].astype(f32)
        for k in range(3):
            g = g + arr_ref[k].astype(f32)
        g_out[...] = g
        d_out[...], m_out[...], v_out[...] = _adam_update(w_ref[...], g, m_ref[...], v_ref[...])

    blk = pl.BlockSpec((br, D), lambda i: (i, 0))
    return pl.pallas_call(
        body, name=name, grid=(steps,), out_shape=(jax.ShapeDtypeStruct((rows, D), f32),) * 4,
        in_specs=[blk, pl.BlockSpec((3, br, D), lambda i: (0, i, 0)), blk, blk, blk]
        + [pl.BlockSpec(memory_space=pl.ANY)] * len(after), out_specs=(blk,) * 4,
        compiler_params=pltpu.CompilerParams(dimension_semantics=("parallel",), vmem_limit_bytes=V7X_VMEM_LIMIT),
    )(own, arrived, w, m, v, *after)


def _adamw(ws, gs, ms, vs, packed, name):
    n = len(ws)
    given = [g for g in gs if not isinstance(g, int)]
    taken = [j for j in range(n) if isinstance(gs[j], int)]

    def body(packed_ref, *refs):
        w_r, m_r, v_r = (refs[k * n:(k + 1) * n] for k in range(3))
        given_r, outs = list(refs[3 * n:3 * n + len(given)]), refs[3 * n + len(given):]
        g_o, outs = dict(zip(taken, outs[:len(taken)])), outs[len(taken):]
        d_o, m_o, v_o = (outs[k * n:(k + 1) * n] for k in range(3))
        for j in range(n):
            if j in g_o:
                (r, c), at = ws[j].shape, gs[j]
                if c == 128:
                    g = packed_ref[at:at + r, :]
                else:
                    assert r == 1
                    g = jnp.concatenate([packed_ref[at + k:at + k + 1, :] for k in range(c // 128)], axis=1)
                g_o[j][...] = g
            else:
                g = given_r.pop(0)[...]
            d_o[j][...], m_o[j][...], v_o[j][...] = _adam_update(w_r[j][...], g, m_r[j][...], v_r[j][...])

    vm = pl.BlockSpec(memory_space=pltpu.VMEM)
    shapes = tuple(jax.ShapeDtypeStruct(w.shape, f32) for w in ws)
    n_out = len(taken) + 3 * n
    return pl.pallas_call(
        body, name=name, out_shape=tuple(shapes[j] for j in taken) + shapes * 3,
        in_specs=[vm] * (1 + 3 * n + len(given)), out_specs=tuple([vm] * n_out),
        compiler_params=pltpu.CompilerParams(vmem_limit_bytes=V7X_VMEM_LIMIT),
    )(packed, *ws, *ms, *vs, *given)


SMALL_FFN = (("ln1_g", D), ("ln1_b", D), ("ln2_g", D), ("ln2_b", D), ("conv_b", D_FF), ("conv_w", 3 * D_FF), ("loss", 128))
SMALL_FFN_AT = 520
SMALL_ROWS = 704


def _small_rows():
    rows, at = {"w_pool": 0, "pool_scale": GROUPS * DH}, SMALL_FFN_AT
    for k, size in SMALL_FFN:
        rows[k] = at
        at += size // 128
    return rows


def kernel(x, w_in, w_pool, pool_scale, w_out, ln1_g, ln1_b, w_up, conv_w, conv_b, w_down, ln2_g, ln2_b, loss_target, m_w_in, m_w_pool, m_pool_scale, m_w_out, m_ln1_g, m_ln1_b, m_w_up, m_conv_w, m_conv_b, m_w_down, m_ln2_g, m_ln2_b, v_w_in, v_w_pool, v_pool_scale, v_w_out, v_ln1_g, v_ln1_b, v_w_up, v_conv_w, v_conv_b, v_w_down, v_ln2_g, v_ln2_b):
    me = 4 * lax.axis_index("x") + 2 * lax.axis_index("y") + lax.axis_index("c")
    x2, tgt = x[0], loss_target[0]

    cos, sin = _rope_tables()
    dmat, qd, kd, cdec = _decay_tables(RET_TILE)

    qkv, g, oret, states, cat, pooled, xhat1, rstd1, x1b, xb, g_in, g_out, g_up, g_down, g_cw = _mix_forward(
        x2, w_in[0].T, w_out[0], cos, sin, dmat, qd, kd, cdec, w_pool[0], pool_scale, ln1_g, ln1_b,
        gather_bf16=[w_up[0].T, w_down[0]], gather=[jnp.transpose(conv_w, (1, 0, 2))])
    w_in_t = g_in.reshape(IN_W, D)
    w_out_f = g_out.reshape(D, D)
    w_up_t = g_up.reshape(2 * D_FF, D)
    w_down_f = g_down.reshape(D_FF, D)
    conv_w_f = jnp.transpose(g_cw[:, :, 0, :], (1, 0, 2)).reshape(3, D_FF)
    dz1, dz2b, du, f, loss8, d_ln2_g, d_ln2_b, d_ln1_g, d_ln1_b, d_conv_b, d_conv_w = _ffn_forward_backward(
        xhat1, rstd1, ln1_g, ln1_b, w_up_t, conv_w_f, conv_b, w_down_f, ln2_g, ln2_b, tgt)
    small_ffn = [d_ln1_g, d_ln1_b, d_ln2_g, d_ln2_b, d_conv_b, d_conv_w, loss8]

    (dw_down,) = _weight_grad(f, dz2b, "grad_w_down", tm=D_FF // 2)
    own_down, oth_down = _pair_reduce([dw_down.reshape(N_DEV, ROWS_DOWN, D)], "pair_reduce_down")
    dw_up_t, arr_down = _weight_grad(du, x1b, "grad_w_up", tm=D_FF // 2, exchange=[oth_down])
    own_up, oth_up = _pair_reduce([dw_up_t.reshape(N_DEV, ROWS_UP, D)], "pair_reduce_up")
    up_sems, up_src, up_land, up_started = _exchange_start(oth_up, "exchange_up_start", CHIP_BARRIER_SPLIT)
    dproj, grad_x, small, dw_out = _mix_backward(
        dz1, w_out_f, qkv, g, oret, states, pooled, cat, cos, sin, dmat, qd, kd, cdec, w_pool[0], pool_scale, w_in_t,
        small_ffn, after=up_started)
    own_out, own_small, oth_out, oth_small = _pair_reduce(
        [dw_out.reshape(N_DEV, ROWS_OUT, D), small.reshape(N_DEV, SMALL_ROWS // N_DEV, 128)], "pair_reduce_out")
    dw_in_t, arr_out, arr_small = _weight_grad(dproj, xb, "grad_w_in", tm=IN_W // 2, exchange=[oth_out, oth_small])
    arr_up = _exchange_wait(up_sems, up_src, up_land, [dw_in_t], "exchange_up_wait")
    own_in, oth_in = _pair_reduce([dw_in_t.reshape(N_DEV, ROWS_IN, D)], "pair_reduce_in")
    in_sems, in_src, in_land, started = _exchange_start(oth_in, "exchange_in_start", CHIP_BARRIER)
    (small_piece,) = _sum_parts([own_small], [arr_small], "sum_small_grads", after=(started,))
    (gs_small,) = _all_gather([small_piece], "gather_small_grads")

    names = ["w_in", "w_pool", "pool_scale", "w_out", "ln1_g", "ln1_b", "w_up", "conv_w", "conv_b", "w_down",
             "ln2_g", "ln2_b"]
    w_d = dict(w_in=w_in, w_pool=w_pool, pool_scale=pool_scale, w_out=w_out, ln1_g=ln1_g, ln1_b=ln1_b, w_up=w_up,
               conv_w=conv_w, conv_b=conv_b, w_down=w_down, ln2_g=ln2_g, ln2_b=ln2_b)
    m_d = dict(w_in=m_w_in, w_pool=m_w_pool, pool_scale=m_pool_scale, w_out=m_w_out, ln1_g=m_ln1_g, ln1_b=m_ln1_b,
               w_up=m_w_up, conv_w=m_conv_w, conv_b=m_conv_b, w_down=m_w_down, ln2_g=m_ln2_g, ln2_b=m_ln2_b)
    v_d = dict(w_in=v_w_in, w_pool=v_w_pool, pool_scale=v_pool_scale, w_out=v_w_out, ln1_g=v_ln1_g, ln1_b=v_ln1_b,
               w_up=v_w_up, conv_w=v_conv_w, conv_b=v_conv_b, w_down=v_w_down, ln2_g=v_ln2_g, ln2_b=v_ln2_b)
    g_d, delta, new_m, new_v = {}, {}, {}, {}

    def big_adamw(k, own, arr, transposed, steps, after=()):
        lay = (lambda a: a[0].T) if transposed else (lambda a: a[0])
        back = (lambda a: a.T[None]) if transposed else (lambda a: a[None])
        res = _sum_adamw(own, arr, lay(w_d[k]), lay(m_d[k]), lay(v_d[k]), "adamw_" + k, steps, after)
        g_d[k], delta[k], new_m[k], new_v[k] = (back(r) for r in res)
        return res[3]

    done = [big_adamw("w_up", own_up, arr_up, True, 4, after=(started,)),
            big_adamw("w_down", own_down, arr_down, False, 2, after=(started,)),
            big_adamw("w_out", own_out, arr_out, False, 2, after=(started,))]

    gs_small, rows = gs_small.reshape(SMALL_ROWS, 128), _small_rows()
    g_conv_w = gs_small[rows["conv_w"]:rows["conv_w"] + 3 * D_FF // 128].reshape(3, D_FF)
    g_d["conv_w"] = lax.dynamic_slice(g_conv_w, (0, me * (D_FF // N_DEV)), (3, D_FF // N_DEV))[None]
    lay = lambda k, a: jnp.transpose(a, (1, 0, 2)) if k == "conv_w" else a.reshape(-1, a.shape[-1])
    back = lambda k, a: jnp.transpose(a, (1, 0, 2)) if k == "conv_w" else a.reshape(w_d[k].shape)
    group = [k for k in names if k not in ("w_in", "w_out", "w_up", "w_down")]
    packed = [k for k in group if k != "conv_w"]
    res = _adamw([lay(k, w_d[k]) for k in group], [lay(k, g_d[k]) if k == "conv_w" else rows[k] for k in group],
                 [lay(k, m_d[k]) for k in group], [lay(k, v_d[k]) for k in group], gs_small, "adamw_small")
    for j, k in enumerate(packed):
        g_d[k] = back(k, res[j])
    for j, k in enumerate(group):
        delta[k], new_m[k], new_v[k] = (back(k, res[len(packed) + part * len(group) + j]) for part in range(3))

    arr_in = _exchange_wait(in_sems, in_src, in_land, done + [res[0]], "exchange_in_wait")
    big_adamw("w_in", own_in, arr_in, True, 4)

    loss = gs_small[rows["loss"], 0]
    return (loss, grad_x[None], *[g_d[k] for k in names], *[delta[k] for k in names], *[new_m[k] for k in names],
            *[new_v[k] for k in names])
```

```python
import math

import numpy as np
import jax
import jax.numpy as jnp
from jax import lax
from jax.experimental import pallas as pl
from jax.experimental.pallas import tpu as pltpu

f32 = jnp.float32
bf16 = jnp.bfloat16

N_DEV = 8
T = 4096
D = 1024
CHUNK = 64
MIX_TILE = 512
RET_TILE = 256
HEADS = 4
DH = 128
RW = HEADS * DH
PW = 512
GROUPS = 4
WINDOWS = (2, 4, 8, 16)
IN_W = 4 * RW + PW
D_FF = 2816
LN_EPS = 1e-5
RMS_EPS = 1e-6
ALPHA = 2.0 ** 0.25
K_SCALE = DH ** -0.5

ADAM_LR = 0.001
ADAM_B1 = 0.9
ADAM_B2 = 0.999
ADAM_EPS = 1e-08
ADAM_WD = 0.01
ADAM_STEP = 10

ROWS_IN, ROWS_OUT, ROWS_UP, ROWS_DOWN = IN_W // N_DEV, D // N_DEV, 2 * D_FF // N_DEV, D_FF // N_DEV

V7X_VMEM_LIMIT = 56 * 2 ** 20
HALO = 32

NT = (((1,), (1,)), ((), ()))
TN = (((0,), (0,)), ((), ()))
NN = (((1,), (0,)), ((), ()))


def _dot(a, b, dims=NN):
    return lax.dot_general(a, b, dims, preferred_element_type=f32)


def _const_spec(shape):
    zeros = (0,) * len(shape)
    return pl.BlockSpec(shape, lambda i: zeros, pipeline_mode=pl.Buffered(1))


def _sigmoid(x):
    return 0.5 * jnp.tanh(0.5 * x) + 0.5


def _decay_tables(tt):
    h = np.arange(HEADS, dtype=np.float64)
    log_gamma = np.log(1.0 - 2.0 ** (-5.0 - h)).astype(np.float32).astype(np.float64)[:, None, None]
    idx = np.arange(tt, dtype=np.float64)
    visible = (idx[None, :] // CHUNK) <= (idx[:, None] // CHUNK)
    mask = np.where(visible[None], np.exp(log_gamma * np.abs(idx[:, None] - idx[None, :])[None]), 0.0)
    qd = np.broadcast_to(np.exp(log_gamma * (idx[None, :, None] + 1.0)), (HEADS, tt, DH))
    kd = np.broadcast_to(np.exp(log_gamma * (tt - 1.0 - idx[None, :, None])), (HEADS, tt, DH))
    cd = np.exp(log_gamma[:, 0, 0] * tt)
    return (jnp.asarray(mask, f32), jnp.asarray(qd, f32), jnp.asarray(kd, f32), [float(c) for c in cd])


def _rope_tables():
    inv_freq = (10000.0 ** (-np.arange(0, DH, 2, dtype=np.float64) / DH)).astype(np.float32)
    ang = (np.arange(T, dtype=np.float32)[:, None] * inv_freq[None, :]).astype(np.float64)
    cos, sin = np.cos(ang), np.sin(ang)
    return (jnp.asarray(np.concatenate([cos, cos], axis=1), f32), jnp.asarray(np.concatenate([-sin, sin], axis=1), f32))


def _swap_halves(t):
    return pltpu.roll(t, DH // 2, axis=1)


def _mix_forward(x, w_in_shard, w_out_shard, cos, sin, dmat, qd, kd, cdec, w_pool, pool_scale, ln1_g, ln1_b,
                 gather_bf16, gather, tt=MIX_TILE):
    n_tiles = T // tt
    to_bf16 = [w_in_shard, w_out_shard] + list(gather_bf16)
    n_c, n_g = len(to_bf16), len(gather_bf16) + len(gather)

    def body(x_ref, cos_ref, sin_ref, dmat_ref, qd_ref, kd_ref, wpool_ref, pscale_ref, g1_ref, b1_ref, *rest):
        f32_in, plain_in, rest = rest[:n_c], rest[n_c:2 + n_g], rest[2 + n_g:]
        qkv_ref, g_ref, oret_ref, states_ref, cat_ref, pooled_ref, xhat_ref, rstd_ref, x1b_ref, xb_ref = rest[:10]
        fout, gout = rest[10:12], rest[12:12 + n_g]
        state_s, pext_s, tmp_s, wint_s, wout_s, load_sems, stage_sems, *rest = rest[12 + n_g:]
        stage_s, cast_s, sems = rest[:n_c], rest[n_c:2 * n_c], rest[2 * n_c:]
        fin, gin, fsems, gsems = cast_s[:2], tuple(cast_s[2:]) + tuple(plain_in), sems[:3], sems[3:]
        i = pl.program_id(0)

        @pl.when(i == 0)
        def _():
            stage = [pltpu.make_async_copy(src, dst, stage_sems.at[j]) for j, (src, dst) in enumerate(zip(f32_in, stage_s))]
            for cp in stage:
                cp.start()
            state_s[...] = jnp.zeros_like(state_s)
            pext_s[:, pl.ds(0, HALO), :] = jnp.zeros((GROUPS, HALO, DH), f32)

            def cast(js):
                for j in js:
                    stage[j].wait()
                    cast_s[j][...] = stage_s[j][...].astype(bf16)

            _barrier(_gather_peers())
            cast(range(2))
            _gather_start(fin, fout, *fsems, barrier=False)
            cast(range(2, n_c))
            _gather_forward(fin, fout, *fsems)
            _gather_start(gin, gout, *gsems, barrier=False)
            _gather_finish(fin, fout, *fsems)
            loads = [pltpu.make_async_copy(src.at[s], dst.at[pl.ds(s * src.shape[1], src.shape[1]), :],
                                           load_sems.at[j, s])
                     for j, (src, dst) in enumerate(((fout[0], wint_s), (fout[1], wout_s))) for s in range(N_DEV)]
            for ld in loads:
                ld.start()
            for ld in loads:
                ld.wait()

        @pl.when(i == n_tiles - 3)
        def _():
            _gather_forward(gin, gout, *gsems)

        xb = x_ref[...].astype(bf16)
        xb_ref[...] = xb
        cos_t, sin_t = cos_ref[...], sin_ref[...]
        for part in range(2):
            pr = _dot(xb, wint_s[pl.ds(part * RW, RW), :], NT)
            for h in range(HEADS):
                t = pr[:, h * DH:(h + 1) * DH]
                r = t * cos_t + _swap_halves(t) * sin_t
                if part == 1:
                    r = r * K_SCALE
                qkv_ref[:, part * RW + h * DH: part * RW + (h + 1) * DH] = r.astype(bf16)
        qkv_ref[:, 2 * RW:3 * RW] = _dot(xb, wint_s[pl.ds(2 * RW, RW), :], NT).astype(bf16)
        g_ref[...] = _dot(xb, wint_s[pl.ds(3 * RW, RW), :], NT)
        p = _dot(xb, wint_s[pl.ds(4 * RW, PW), :], NT)
        for gi in range(GROUPS):
            pext_s[gi, pl.ds(HALO, tt), :] = p[:, gi * DH:(gi + 1) * DH]

        for sub in range(tt // RET_TILE):
            rows = pl.ds(sub * RET_TILE, RET_TILE)
            for h in range(HEADS):
                q = qkv_ref[rows, h * DH:(h + 1) * DH]
                k = qkv_ref[rows, RW + h * DH: RW + (h + 1) * DH]
                v = qkv_ref[rows, 2 * RW + h * DH: 2 * RW + (h + 1) * DH]
                s = _dot(q, k, NT) * dmat_ref[h]
                st = state_s[h]
                stb = st.astype(bf16)
                states_ref[sub, h] = stb
                oret_ref[rows, h * DH:(h + 1) * DH] = (_dot(s.astype(bf16), v)
                                                      + _dot((q.astype(f32) * qd_ref[h]).astype(bf16), stb))
                state_s[h] = st * cdec[h] + _dot((k.astype(f32) * kd_ref[h]).astype(bf16), v, TN)

        for h in range(HEADS):
            sl = slice(h * DH, (h + 1) * DH)
            o = oret_ref[:, sl]
            r = lax.rsqrt(jnp.mean(o * o, axis=-1, keepdims=True) + RMS_EPS)
            gg = g_ref[:, sl]
            cat_ref[:, sl] = (o * r * (gg * _sigmoid(gg))).astype(bf16)

        pos1 = (i * tt + lax.broadcasted_iota(jnp.int32, (tt, 1), 0) + 1).astype(f32)
        for gi, w in enumerate(WINDOWS):
            sl = slice(gi * DH, (gi + 1) * DH)
            stages = int(math.log2(w))
            src = pext_s
            for s in range(stages):
                lo = HALO - 8 * (stages - 1 - s)
                n = tt + HALO - lo
                shift = 2 ** s
                val = src[gi, pl.ds(lo, n), :] + src[gi, pl.ds(lo - shift, n), :]
                if s == stages - 1:
                    wsum = val
                else:
                    tmp_s[gi, pl.ds(lo, n), :] = val
                    src = tmp_s
            p_g = pext_s[gi, pl.ds(HALO, tt), :]
            pooled = (wsum / jnp.minimum(pos1, float(w)) - p_g).astype(bf16)
            pooled_ref[:, sl] = pooled
            y = _dot(pooled, wpool_ref[gi].astype(bf16)) * pscale_ref[:, sl]
            cat_ref[:, RW + gi * DH: RW + (gi + 1) * DH] = y.astype(bf16)
        pext_s[:, pl.ds(0, HALO), :] = pext_s[:, pl.ds(tt, HALO), :]

        z = ALPHA * x_ref[...] + _dot(cat_ref[...], wout_s[...])
        mu = jnp.mean(z, axis=-1, keepdims=True)
        zc = z - mu
        rstd = lax.rsqrt(jnp.mean(zc * zc, axis=-1, keepdims=True) + LN_EPS)
        xhat = zc * rstd
        xhat_ref[...] = xhat
        rstd_ref[...] = rstd
        x1b_ref[...] = (xhat * g1_ref[...] + b1_ref[...]).astype(bf16)

        @pl.when(i == n_tiles - 1)
        def _():
            _gather_finish(gin, gout, *gsems)

    tile = lambda w: pl.BlockSpec((tt, w), lambda i: (i, 0))
    hbm = pl.BlockSpec(memory_space=pltpu.HBM)
    out_shape = (
        jax.ShapeDtypeStruct((T, 3 * RW), bf16),
        jax.ShapeDtypeStruct((T, RW), f32),
        jax.ShapeDtypeStruct((T, RW), f32),
        jax.ShapeDtypeStruct((T // RET_TILE, HEADS, DH, DH), bf16),
        jax.ShapeDtypeStruct((T, D), bf16),
        jax.ShapeDtypeStruct((T, PW), bf16),
        jax.ShapeDtypeStruct((T, D), f32),
        jax.ShapeDtypeStruct((T, 1), f32),
        jax.ShapeDtypeStruct((T, D), bf16),
        jax.ShapeDtypeStruct((T, D), bf16),
    ) + tuple(jax.ShapeDtypeStruct((N_DEV,) + b.shape, bf16) for b in to_bf16
              ) + tuple(jax.ShapeDtypeStruct((N_DEV,) + b.shape, b.dtype) for b in gather)
    return pl.pallas_call(
        body, name="mix_forward", grid=(n_tiles,), out_shape=out_shape,
        in_specs=[tile(D), tile(DH), tile(DH),
                  _const_spec((HEADS, RET_TILE, RET_TILE)), _const_spec((HEADS, RET_TILE, DH)),
                  _const_spec((HEADS, RET_TILE, DH)),
                  _const_spec((GROUPS, DH, DH)), _const_spec((1, PW)),
                  _const_spec((1, D)), _const_spec((1, D))] + [hbm] * (2 + n_g),
        out_specs=(tile(3 * RW), tile(RW), tile(RW),
                   pl.BlockSpec((tt // RET_TILE, HEADS, DH, DH), lambda i: (i, 0, 0, 0)),
                   tile(D), tile(PW), tile(D), tile(1), tile(D), tile(D)) + (hbm,) * (2 + n_g),
        scratch_shapes=[pltpu.VMEM((HEADS, DH, DH), f32), pltpu.VMEM((GROUPS, tt + HALO, DH), f32),
                        pltpu.VMEM((GROUPS, tt + HALO, DH), f32), pltpu.VMEM((IN_W, D), bf16), pltpu.VMEM((D, D), bf16),
                        pltpu.SemaphoreType.DMA((2, N_DEV)), pltpu.SemaphoreType.DMA((n_c,))]
        + [pltpu.VMEM(b.shape, f32) for b in to_bf16] + [pltpu.VMEM(b.shape, bf16) for b in to_bf16]
        + _gather_sems(2) + _gather_sems(n_g),
        compiler_params=pltpu.CompilerParams(dimension_semantics=("arbitrary",), vmem_limit_bytes=V7X_VMEM_LIMIT,
                                             collective_id=GATHER_BARRIER),
    )(x, cos, sin, dmat, qd, kd, w_pool, pool_scale, ln1_g, ln1_b, *to_bf16, *gather)


def _ffn_forward_backward(xhat1, rstd1, ln1_g, ln1_b, w_up_t, conv_w, conv_b, w_down, ln2_g, ln2_b, target,
                          tt=256):
    n_tiles = T // tt
    FH = 16
    hb = tt // FH

    def body(xhat_ref, halo_ref, rstd_ref, g1_ref, b1_ref, wupt_ref, cw_ref, cb_ref, wdown_ref, g2_ref, b2_ref, tgt_ref,
             dz1_ref, dz2b_ref, du_ref, f_ref, loss_ref, dg2_ref, db2_ref, dg1_ref, db1_ref, dcb_ref, dcw_ref,
             gext_s, val_s, dhext_s):
        i = pl.program_id(0)
        tile_idx = n_tiles - 1 - i

        def rd(ref, off):
            return jnp.concatenate([ref[k, pl.ds(off, tt), :] for k in range(D_FF // 128)], axis=1)

        def wr(ref, val):
            for k in range(D_FF // 128):
                ref[k, pl.ds(0, val.shape[0]), :] = val[:, k * 128:(k + 1) * 128]

        @pl.when(i == 0)
        def _():
            for r in (loss_ref, dg2_ref, db2_ref, dg1_ref, db1_ref, dcb_ref, dcw_ref):
                r[...] = jnp.zeros_like(r)
            dhext_s[:, pl.ds(tt, 8), :] = jnp.zeros((D_FF // 128, 8, 128), f32)

        g1, b1 = g1_ref[...], b1_ref[...]
        xhat = xhat_ref[...]
        x1 = xhat * g1 + b1
        x1b = x1.astype(bf16)
        x1h = ((halo_ref[...] * g1 + b1) * jnp.where(tile_idx == 0, 0.0, 1.0)).astype(bf16)
        x1ext = jnp.concatenate([x1h, x1b], axis=0)

        val = _dot(x1b, wupt_ref[pl.ds(0, D_FF), :], NT)
        gate_ext = _dot(x1ext, wupt_ref[pl.ds(D_FF, D_FF), :], NT)
        wr(gext_s, gate_ext)
        hh = (cb_ref[...] + cw_ref[0:1, :] * rd(gext_s, FH - 2) + cw_ref[1:2, :] * rd(gext_s, FH - 1)
              + cw_ref[2:3, :] * gate_ext[FH:])
        sg = _sigmoid(hh)
        act = hh * sg
        wr(dhext_s, act)
        val_s[...] = val * (sg + act * (1.0 - sg))
        fb = (act * val).astype(bf16)
        f_ref[...] = fb

        z = ALPHA * x1 + _dot(fb, wdown_ref[...])
        mu = jnp.mean(z, axis=-1, keepdims=True)
        zc = z - mu
        rstd2 = lax.rsqrt(jnp.mean(zc * zc, axis=-1, keepdims=True) + LN_EPS)
        xh2 = zc * rstd2
        diff = xh2 * g2_ref[...] + b2_ref[...] - tgt_ref[...]
        loss_ref[...] += 0.5 * jnp.sum(diff * diff) / D
        dy = diff * (1.0 / D)
        dg2_ref[...] += jnp.sum(dy * xh2, axis=0, keepdims=True)
        db2_ref[...] += jnp.sum(dy, axis=0, keepdims=True)
        dyg = dy * g2_ref[...]
        dz2 = rstd2 * (dyg - jnp.mean(dyg, axis=-1, keepdims=True) - xh2 * jnp.mean(dyg * xh2, axis=-1, keepdims=True))
        dz2b = dz2.astype(bf16)
        dz2b_ref[...] = dz2b

        df = _dot(dz2b, wdown_ref[...], NT)
        dval = df * rd(dhext_s, 0)
        dh = df * val_s[...]
        wr(dhext_s, dh)
        dh1, dh2, g0 = rd(dhext_s, 1), rd(dhext_s, 2), rd(gext_s, FH)
        dcb_ref[...] += jnp.sum(dh, axis=0, keepdims=True)
        dcw_ref[0:1, :] += jnp.sum(dh2 * g0, axis=0, keepdims=True)
        dcw_ref[1:2, :] += jnp.sum(dh1 * g0, axis=0, keepdims=True)
        dcw_ref[2:3, :] += jnp.sum(dh * g0, axis=0, keepdims=True)
        dgate = cw_ref[2:3, :] * dh + cw_ref[1:2, :] * dh1 + cw_ref[0:1, :] * dh2
        dvalb, dgateb = dval.astype(bf16), dgate.astype(bf16)
        du_ref[:, :D_FF] = dvalb
        du_ref[:, D_FF:] = dgateb
        dx1 = ALPHA * dz2 + _dot(dvalb, wupt_ref[pl.ds(0, D_FF), :]) + _dot(dgateb, wupt_ref[pl.ds(D_FF, D_FF), :])
        dhext_s[:, pl.ds(tt, 8), :] = dhext_s[:, pl.ds(0, 8), :]

        dg1_ref[...] += jnp.sum(dx1 * xhat, axis=0, keepdims=True)
        db1_ref[...] += jnp.sum(dx1, axis=0, keepdims=True)
        dxg = dx1 * g1
        dz1_ref[...] = rstd_ref[...] * (dxg - jnp.mean(dxg, axis=-1, keepdims=True)
                                        - xhat * jnp.mean(dxg * xhat, axis=-1, keepdims=True))

    rtile = lambda w: pl.BlockSpec((tt, w), lambda i: (n_tiles - 1 - i, 0))
    acc = lambda shape: pl.BlockSpec(shape, lambda i: (0, 0))
    out_shape = (
        jax.ShapeDtypeStruct((T, D), f32),
        jax.ShapeDtypeStruct((T, D), bf16),
        jax.ShapeDtypeStruct((T, 2 * D_FF), bf16),
        jax.ShapeDtypeStruct((T, D_FF), bf16),
        jax.ShapeDtypeStruct((8, 128), f32),
        jax.ShapeDtypeStruct((1, D), f32), jax.ShapeDtypeStruct((1, D), f32),
        jax.ShapeDtypeStruct((1, D), f32), jax.ShapeDtypeStruct((1, D), f32),
        jax.ShapeDtypeStruct((1, D_FF), f32), jax.ShapeDtypeStruct((3, D_FF), f32),
    )
    return pl.pallas_call(
        body, name="ffn_forward_backward", grid=(n_tiles,), out_shape=out_shape,
        in_specs=[rtile(D),
                  pl.BlockSpec((FH, D), lambda i: (jnp.maximum((n_tiles - 1 - i) * hb - 1, 0), 0)),
                  rtile(1), _const_spec((1, D)), _const_spec((1, D)), _const_spec((2 * D_FF, D)),
                  _const_spec((3, D_FF)), _const_spec((1, D_FF)), _const_spec((D_FF, D)),
                  _const_spec((1, D)), _const_spec((1, D)), rtile(D)],
        out_specs=(rtile(D), rtile(D), rtile(2 * D_FF), rtile(D_FF), acc((8, 128)),
                   acc((1, D)), acc((1, D)), acc((1, D)), acc((1, D)), acc((1, D_FF)), acc((3, D_FF))),
        scratch_shapes=[pltpu.VMEM((D_FF // 128, tt + FH, 128), f32), pltpu.VMEM((tt, D_FF), f32),
                        pltpu.VMEM((D_FF // 128, tt + 8, 128), f32)],
        compiler_params=pltpu.CompilerParams(dimension_semantics=("arbitrary",), vmem_limit_bytes=V7X_VMEM_LIMIT),
    )(xhat1, xhat1, rstd1, ln1_g, ln1_b, w_up_t, conv_w, conv_b, w_down, ln2_g, ln2_b, target)


def _mix_backward(dz1, w_out, qkv, g, oret, states, pooled, cat, cos, sin, dmat, qd, kd, cdec, w_pool, pool_scale, w_in_t,
                  small_ffn, after, tt=MIX_TILE):
    n_tiles = T // tt

    def body(dz1_ref, wout_ref, qkv_ref, g_ref, oret_ref, states_ref, pooled_ref, cat_ref, cos_ref, sin_ref, dmat_ref,
             qd_ref, kd_ref, wpool_ref, pscale_ref, wint_ref, *rest):
        ffn_refs, rest = rest[:len(SMALL_FFN)], rest[len(SMALL_FFN):]
        after_ref, dproj_ref, gx_ref, small_ref, dwout_ref, dstate_s, dout_s, eext_s, tmp_s, dwout_s, dpscale_s = rest
        i = pl.program_id(0)
        tile_idx = n_tiles - 1 - i

        @pl.when(i == 0)
        def _():
            dstate_s[...] = jnp.zeros_like(dstate_s)
            small_ref[...] = jnp.zeros_like(small_ref)
            dpscale_s[...] = jnp.zeros_like(dpscale_s)
            dwout_s[...] = jnp.zeros_like(dwout_s)
            eext_s[:, pl.ds(tt, HALO), :] = jnp.zeros((GROUPS, HALO, DH), f32)

        dz1 = dz1_ref[...]
        dz1b = dz1.astype(bf16)
        dcat = _dot(dz1b, wout_ref[...], NT)
        dwout_s[...] += _dot(cat_ref[...], dz1b, TN)

        pos1 = (tile_idx * tt + lax.broadcasted_iota(jnp.int32, (tt, 1), 0) + 1).astype(f32)
        for gi, w in enumerate(WINDOWS):
            sl = slice(gi * DH, (gi + 1) * DH)
            dpo = dcat[:, RW + gi * DH: RW + (gi + 1) * DH]
            pooled_g = pooled_ref[:, sl]
            wpool_g = wpool_ref[gi].astype(bf16)
            ylin = _dot(pooled_g, wpool_g)
            dpscale_s[:, sl] += jnp.sum(dpo * ylin, axis=0, keepdims=True)
            dpw = (dpo * pscale_ref[:, sl]).astype(bf16)
            small_ref[pl.ds(gi * DH, DH), :] += _dot(pooled_g, dpw, TN)
            dpooled = _dot(dpw, wpool_g, NT)
            eext_s[gi, pl.ds(0, tt), :] = dpooled / jnp.minimum(pos1, float(w))
            stages = int(math.log2(w))
            src = eext_s
            for s in range(stages):
                n = tt + 8 * (stages - 1 - s)
                shift = 2 ** s
                val = src[gi, pl.ds(0, n), :] + src[gi, pl.ds(shift, n), :]
                if s == stages - 1:
                    wsum = val
                else:
                    tmp_s[gi, pl.ds(0, n), :] = val
                    src = tmp_s
            dproj_ref[:, 4 * RW + gi * DH: 4 * RW + (gi + 1) * DH] = (wsum - dpooled).astype(bf16)
        eext_s[:, pl.ds(tt, HALO), :] = eext_s[:, pl.ds(0, HALO), :]

        for h in range(HEADS):
            sl = slice(h * DH, (h + 1) * DH)
            dr = dcat[:, sl]
            o = oret_ref[:, sl]
            r = lax.rsqrt(jnp.mean(o * o, axis=-1, keepdims=True) + RMS_EPS)
            rn = o * r
            gg = g_ref[:, sl]
            sg = _sigmoid(gg)
            dproj_ref[:, 3 * RW + h * DH: 3 * RW + (h + 1) * DH] = (dr * rn * (sg * (1.0 + gg * (1.0 - sg)))).astype(bf16)
            drn = dr * (gg * sg)
            dout_s[:, sl] = (r * (drn - rn * jnp.mean(drn * rn, axis=-1, keepdims=True))).astype(bf16)

        for sub in reversed(range(tt // RET_TILE)):
            rows = pl.ds(sub * RET_TILE, RET_TILE)
            cos_t, sin_t = cos_ref[rows, :], sin_ref[rows, :]
            for h in range(HEADS):
                q = qkv_ref[rows, h * DH:(h + 1) * DH]
                k = qkv_ref[rows, RW + h * DH: RW + (h + 1) * DH]
                v = qkv_ref[rows, 2 * RW + h * DH: 2 * RW + (h + 1) * DH]
                do = dout_s[rows, h * DH:(h + 1) * DH]
                stb = states_ref[sub, h]
                dst = dstate_s[h]
                dstb = dst.astype(bf16)
                sb = (_dot(q, k, NT) * dmat_ref[h]).astype(bf16)
                dsb = (_dot(do, v, NT) * dmat_ref[h]).astype(bf16)
                dq = _dot(dsb, k) + _dot(do, stb, NT) * qd_ref[h]
                dk = _dot(dsb, q, TN) + _dot(v, dstb, NT) * kd_ref[h]
                dv = _dot(sb, do, TN) + _dot((k.astype(f32) * kd_ref[h]).astype(bf16), dstb)
                dstate_s[h] = dst * cdec[h] + _dot((q.astype(f32) * qd_ref[h]).astype(bf16), do, TN)
                dproj_ref[rows, h * DH:(h + 1) * DH] = (dq * cos_t - _swap_halves(dq) * sin_t).astype(bf16)
                dproj_ref[rows, RW + h * DH: RW + (h + 1) * DH] = (
                    (dk * cos_t - _swap_halves(dk) * sin_t) * K_SCALE).astype(bf16)
                dproj_ref[rows, 2 * RW + h * DH: 2 * RW + (h + 1) * DH] = dv.astype(bf16)

        gx_ref[...] = ALPHA * dz1 + _dot(dproj_ref[...], wint_ref[...])

        @pl.when(i == n_tiles - 1)
        def _():
            dwout_ref[...] = dwout_s[...].astype(bf16)
            at = GROUPS * DH
            for ref, size in [(dpscale_s, PW)] + [(ref, size) for ref, (_, size) in zip(ffn_refs, SMALL_FFN)]:
                for j in range(size // 128):
                    r, k = divmod(j, ref.shape[1] // 128)
                    small_ref[at + j: at + j + 1, :] = ref[r:r + 1, k * 128:(k + 1) * 128]
                at = SMALL_FFN_AT if ref is dpscale_s else at + size // 128

    rtile = lambda w: pl.BlockSpec((tt, w), lambda i: (n_tiles - 1 - i, 0))
    out_shape = (
        jax.ShapeDtypeStruct((T, IN_W), bf16),
        jax.ShapeDtypeStruct((T, D), f32),
        jax.ShapeDtypeStruct((SMALL_ROWS, 128), f32),
        jax.ShapeDtypeStruct((D, D), bf16),
    )
    return pl.pallas_call(
        body, name="mix_backward", grid=(n_tiles,), out_shape=out_shape,
        in_specs=[rtile(D), _const_spec((D, D)), rtile(3 * RW), rtile(RW), rtile(RW),
                  pl.BlockSpec((tt // RET_TILE, HEADS, DH, DH), lambda i: (n_tiles - 1 - i, 0, 0, 0)),
                  rtile(PW), rtile(D), rtile(DH), rtile(DH),
                  _const_spec((HEADS, RET_TILE, RET_TILE)), _const_spec((HEADS, RET_TILE, DH)),
                  _const_spec((HEADS, RET_TILE, DH)),
                  _const_spec((GROUPS, DH, DH)), _const_spec((1, PW)), _const_spec((IN_W, D)),
                  *[_const_spec(a.shape) for a in small_ffn], pl.BlockSpec(memory_space=pl.ANY)],
        out_specs=(rtile(IN_W), rtile(D), pl.BlockSpec((SMALL_ROWS, 128), lambda i: (0, 0)),
                   pl.BlockSpec((D, D), lambda i: (0, 0), pipeline_mode=pl.Buffered(1))),
        scratch_shapes=[pltpu.VMEM((HEADS, DH, DH), f32), pltpu.VMEM((tt, RW), bf16),
                        pltpu.VMEM((GROUPS, tt + HALO, DH), f32), pltpu.VMEM((GROUPS, tt + HALO, DH), f32),
                        pltpu.VMEM((D, D), f32), pltpu.VMEM((1, PW), f32)],
        compiler_params=pltpu.CompilerParams(dimension_semantics=("arbitrary",), vmem_limit_bytes=V7X_VMEM_LIMIT),
    )(dz1, w_out, qkv, g, oret, states, pooled, cat, cos, sin, dmat, qd, kd, w_pool, pool_scale, w_in_t, *small_ffn,
      after)


def _weight_grad(a, b, name, tm, exchange=()):
    m = a.shape[1]
    n_m, n_e = m // tm, len(exchange)

    def body(a_ref, b_ref, *rest):
        ein, o_ref, eout, sems = rest[:n_e], rest[n_e], rest[n_e + 1:2 * n_e + 1], rest[2 * n_e + 1:]
        i = pl.program_id(0)

        if n_e:
            @pl.when(i == 0)
            def _():
                _chip_exchange_start(ein, eout, *sems)

        o_ref[...] = _dot(a_ref[...], b_ref[...].astype(bf16), TN).astype(bf16)

        if n_e:
            @pl.when(i == n_m - 1)
            def _():
                _chip_exchange_finish(ein, eout, *sems)

    hbm = pl.BlockSpec(memory_space=pltpu.HBM)
    return pl.pallas_call(
        body, name=name, grid=(n_m,),
        out_shape=(jax.ShapeDtypeStruct((m, D), bf16),) + tuple(jax.ShapeDtypeStruct(e.shape, e.dtype) for e in exchange),
        in_specs=[pl.BlockSpec((T, tm), lambda i: (0, i)),
                  pl.BlockSpec((T, D), lambda i: (0, 0), pipeline_mode=pl.Buffered(1))] + [hbm] * n_e,
        out_specs=(pl.BlockSpec((tm, D), lambda i: (i, 0)),) + (hbm,) * n_e,
        scratch_shapes=_chip_exchange_sems(n_e),
        compiler_params=pltpu.CompilerParams(dimension_semantics=("arbitrary",), vmem_limit_bytes=V7X_VMEM_LIMIT,
                                             collective_id=CHIP_BARRIER if n_e else None),
    )(a, b, *exchange)


CHIP_FLIPS = ((1, 0), (0, 1), (1, 1))
PAIR_BARRIER, CHIP_BARRIER, GATHER_BARRIER, CHIP_BARRIER_SPLIT = 0, 1, 2, 3


def _barrier(peers):
    sem = pltpu.get_barrier_semaphore()
    for peer in peers:
        pl.semaphore_signal(sem, inc=1, device_id=peer, device_id_type=pl.DeviceIdType.MESH)
    pl.semaphore_wait(sem, len(peers))


def _me():
    return lax.axis_index("x"), lax.axis_index("y"), lax.axis_index("c")


def _chip(me, k):
    x, y, _ = me
    if k == 0:
        return x, y
    fx, fy = CHIP_FLIPS[k - 1]
    return (1 - x if fx else x), (1 - y if fy else y)


def _slot(x, y, c):
    return 4 * x + 2 * y + c


def _remote(src, dst, send_sem, recv_sem, to):
    return pltpu.make_async_remote_copy(src_ref=src, dst_ref=dst, send_sem=send_sem, recv_sem=recv_sem,
                                        device_id=to, device_id_type=pl.DeviceIdType.MESH)


def _gather_sems(n):
    return [pltpu.SemaphoreType.DMA((7, n)), pltpu.SemaphoreType.DMA((7, n)), pltpu.SemaphoreType.DMA((n,))] if n else []


def _gather_copy(k, j, gin, gout, send_sems, recv_sems, sending):
    x, y, c = _me()
    sibling, x_chip, y_chip, d_chip = (x, y, 1 - c), (1 - x, y), (x, 1 - y), (1 - x, 1 - y)
    south = c == 0
    passed_on = (jnp.where(south, 1 - x, x), jnp.where(south, y, 1 - y), c)
    src, to = gin[j], sibling
    if sending:
        block = {0: (x, y, c), 1: (x, y, c), 2: (x, y, c), 3: passed_on, 4: (*x_chip, c), 5: (*y_chip, c), 6: (*d_chip, c)}[k]
        to = {1: (*x_chip, c), 2: (*y_chip, c), 3: (jnp.where(south, x, 1 - x), jnp.where(south, 1 - y, y), c)}.get(k, sibling)
        if k >= 3:
            src = gout[j].at[_slot(*block)]
    else:
        block = {0: sibling, 1: (*x_chip, c), 2: (*y_chip, c), 3: (*d_chip, c), 4: (*x_chip, 1 - c), 5: (*y_chip, 1 - c),
                 6: (*d_chip, 1 - c)}[k]
    return _remote(src, gout[j].at[_slot(*block)], send_sems.at[k, j], recv_sems.at[k, j], to)


def _gather_do(ks, action, gin, gout, send_sems, recv_sems):
    for k in ks:
        for j in range(len(gin)):
            cp = _gather_copy(k, j, gin, gout, send_sems, recv_sems, action != "wait_recv")
            getattr(cp, action)()


def _gather_peers():
    x, y, c = _me()
    return [(x, y, 1 - c), (1 - x, y, c), (x, 1 - y, c)]


def _gather_start(gin, gout, send_sems, recv_sems, local_sems, barrier=True):
    if barrier:
        _barrier(_gather_peers())
    for j in range(len(gin)):
        pltpu.make_async_copy(gin[j], gout[j].at[_slot(*_me())], local_sems.at[j]).start()
    _gather_do((0, 1, 2), "start", gin, gout, send_sems, recv_sems)


def _gather_forward(gin, gout, send_sems, recv_sems, local_sems):
    _gather_do((1, 2), "wait_recv", gin, gout, send_sems, recv_sems)
    _gather_do((3, 4, 5), "start", gin, gout, send_sems, recv_sems)


def _gather_finish(gin, gout, send_sems, recv_sems, local_sems):
    _gather_do((3,), "wait_recv", gin, gout, send_sems, recv_sems)
    _gather_do((6,), "start", gin, gout, send_sems, recv_sems)
    _gather_do((0, 4, 5, 6), "wait_recv", gin, gout, send_sems, recv_sems)
    _gather_do(range(7), "wait_send", gin, gout, send_sems, recv_sems)
    for j in range(len(gin)):
        pltpu.make_async_copy(gin[j], gout[j].at[_slot(*_me())], local_sems.at[j]).wait()


def _all_gather(blocks, name):
    n = len(blocks)

    def body(*refs):
        gin, gout, sems = refs[:n], refs[n:2 * n], refs[2 * n:]
        _gather_start(gin, gout, *sems)
        _gather_forward(gin, gout, *sems)
        _gather_finish(gin, gout, *sems)

    hbm = pl.BlockSpec(memory_space=pltpu.HBM)
    return pl.pallas_call(
        body, name=name,
        out_shape=tuple(jax.ShapeDtypeStruct((N_DEV,) + b.shape, b.dtype) for b in blocks),
        in_specs=[hbm] * n, out_specs=(hbm,) * n, scratch_shapes=_gather_sems(n),
        compiler_params=pltpu.CompilerParams(collective_id=GATHER_BARRIER),
    )(*blocks)


def _pair_reduce(parts, name):
    n = len(parts)

    def body(*refs):
        ins, own, others, landing, mine = (refs[k * n:(k + 1) * n] for k in range(5))
        send_sems, recv_sems, local_sems = refs[5 * n:]
        me = _me()
        x, y, c = me
        sibling = (x, y, 1 - c)
        _barrier([sibling])
        sends, loads = [], []
        for k in range(4):
            for j in range(n):
                cp = _remote(ins[j].at[_slot(*_chip(me, k), 1 - c)], landing[j].at[k], send_sems.at[k, j],
                             recv_sems.at[k, j], sibling)
                cp.start()
                sends.append(cp)
                ld = pltpu.make_async_copy(ins[j].at[_slot(*_chip(me, k), c)], mine[j].at[k], local_sems.at[k, j])
                ld.start()
                loads.append(ld)
        for k in range(4):
            for j in range(n):
                loads[k * n + j].wait()
                _remote(ins[j].at[0], landing[j].at[k], send_sems.at[k, j], recv_sems.at[k, j], sibling).wait_recv()
                total = mine[j][k].astype(f32) + landing[j][k].astype(f32)
                if k == 0:
                    own[j][...] = total.astype(own[j].dtype)
                else:
                    others[j][k - 1] = total.astype(others[j].dtype)
        for cp in sends:
            cp.wait_send()

    vm = pl.BlockSpec(memory_space=pltpu.VMEM)
    return pl.pallas_call(
        body, name=name,
        out_shape=tuple(jax.ShapeDtypeStruct(p.shape[1:], p.dtype) for p in parts)
        + tuple(jax.ShapeDtypeStruct((3,) + p.shape[1:], p.dtype) for p in parts),
        in_specs=[pl.BlockSpec(memory_space=pltpu.HBM)] * n, out_specs=(vm,) * (2 * n),
        scratch_shapes=[pltpu.VMEM((4,) + p.shape[1:], p.dtype) for p in parts] * 2
        + [pltpu.SemaphoreType.DMA((4, n)), pltpu.SemaphoreType.DMA((4, n)), pltpu.SemaphoreType.DMA((4, n))],
        compiler_params=pltpu.CompilerParams(vmem_limit_bytes=V7X_VMEM_LIMIT, collective_id=PAIR_BARRIER),
    )(*parts)


def _chip_exchange_sems(n):
    return [pltpu.SemaphoreType.DMA((3, n)), pltpu.SemaphoreType.DMA((3, n))] if n else []


def _chip_exchange_copy(k, j, ein, eout, send_sems, recv_sems):
    me = _me()
    return _remote(ein[j].at[k - 1], eout[j].at[k - 1], send_sems.at[k - 1, j], recv_sems.at[k - 1, j],
                   (*_chip(me, k), me[2]))


def _chip_peers():
    me = _me()
    return [(*_chip(me, k), me[2]) for k in range(1, 4)]


def _chip_exchange_start(ein, eout, send_sems, recv_sems, barrier=True):
    if barrier:
        _barrier(_chip_peers())
    for k in range(1, 4):
        for j in range(len(ein)):
            _chip_exchange_copy(k, j, ein, eout, send_sems, recv_sems).start()


def _chip_exchange_finish(ein, eout, send_sems, recv_sems):
    for k in range(1, 4):
        for j in range(len(ein)):
            _chip_exchange_copy(k, j, ein, eout, send_sems, recv_sems).wait_recv()
    for k in range(1, 4):
        for j in range(len(ein)):
            _chip_exchange_copy(k, j, ein, eout, send_sems, recv_sems).wait_send()


def _split_copies(src_ref, dst_ref, sems):
    me = _me()
    return [_remote(src_ref.at[k - 1], dst_ref.at[k - 1], sems[k - 1], sems[2 + k], (*_chip(me, k), me[2]))
            for k in range(1, 4)]


def _exchange_start(others, name, barrier_id, after=()):
    def body(src_ref, land_ref, *rest):
        sems, token_ref = rest[len(after):len(after) + 6], rest[len(after) + 8]
        _barrier(_chip_peers())
        for copy in _split_copies(src_ref, land_ref, sems):
            copy.start()
        token_ref[...] = jnp.zeros_like(token_ref)

    hbm, sem = pl.BlockSpec(memory_space=pltpu.HBM), pl.BlockSpec(memory_space=pltpu.SEMAPHORE)
    thru = pltpu.HBM(others.shape, others.dtype)
    res = pl.pallas_call(
        body, name=name,
        out_shape=(pltpu.SemaphoreType.DMA(()),) * 6 + (thru, thru, jax.ShapeDtypeStruct((8, 128), f32)),
        in_specs=(hbm, hbm) + (pl.BlockSpec(memory_space=pl.ANY),) * len(after),
        out_specs=(sem,) * 6 + (hbm, hbm, pl.BlockSpec(memory_space=pltpu.VMEM)),
        input_output_aliases={0: 6, 1: 7},
        compiler_params=pltpu.CompilerParams(has_side_effects=pltpu.SideEffectType.DATAFLOW_SIDE_EFFECTING,
                                             collective_id=barrier_id),
    )(pltpu.with_memory_space_constraint(others, pltpu.HBM),
      pltpu.with_memory_space_constraint(lax.empty(others.shape, others.dtype), pltpu.HBM), *after)
    return res[:6], res[6], res[7], res[8]


def _exchange_wait(sems, src_thru, land_thru, after, name):
    n_after = len(after)

    def body(src_ref, land_ref, *rest):
        for copy in _split_copies(src_ref, land_ref, rest[:6]):
            copy.wait_send()
            copy.wait_recv()

    hbm, sem = pl.BlockSpec(memory_space=pltpu.HBM), pl.BlockSpec(memory_space=pltpu.SEMAPHORE)
    thru = pltpu.HBM(src_thru.shape, src_thru.dtype)
    return pl.pallas_call(
        body, name=name, out_shape=(thru, thru),
        in_specs=(hbm, hbm) + (sem,) * 6 + (pl.BlockSpec(memory_space=pl.ANY),) * n_after, out_specs=(hbm, hbm),
        input_output_aliases={0: 0, 1: 1},
        compiler_params=pltpu.CompilerParams(has_side_effects=pltpu.SideEffectType.DATAFLOW_SIDE_EFFECTING),
    )(src_thru, land_thru, *sems, *after)[1]


def _sum_parts(owns, arrived, name, after=()):
    n = len(owns)

    def body(*refs):
        for own, arr, out in zip(refs[:n], refs[n:2 * n], refs[2 * n + len(after):]):
            acc = own[...].astype(f32)
            for k in range(3):
                acc = acc + arr[k].astype(f32)
            out[...] = acc

    vm = pl.BlockSpec(memory_space=pltpu.VMEM)
    return pl.pallas_call(
        body, name=name, out_shape=tuple(jax.ShapeDtypeStruct(o.shape, f32) for o in owns),
        in_specs=[vm] * (2 * n) + [pl.BlockSpec(memory_space=pl.ANY)] * len(after), out_specs=(vm,) * n,
        compiler_params=pltpu.CompilerParams(vmem_limit_bytes=V7X_VMEM_LIMIT),
    )(*owns, *arrived, *after)


def _adam_update(w, g, m, v):
    m = ADAM_B1 * m + (1.0 - ADAM_B1) * g
    v = ADAM_B2 * v + (1.0 - ADAM_B2) * (g * g)
    m_hat = m / (1.0 - ADAM_B1 ** ADAM_STEP)
    v_hat = v / (1.0 - ADAM_B2 ** ADAM_STEP)
    return -ADAM_LR * (m_hat / (jnp.sqrt(v_hat) + ADAM_EPS) + ADAM_WD * w), m, v


def _sum_adamw(own, arrived, w, m, v, name, steps, after=()):
    rows = own.shape[0]
    br = rows // steps

    def body(own_ref, arr_ref, w_ref, m_ref, v_ref, *rest):
        g_out, d_out, m_out, v_out = rest[len(after):]
        g = own_ref[...].astype(f32)
        for k in range(3):
            g = g + arr_ref[k].astype(f32)
        g_out[...] = g
        d_out[...], m_out[...], v_out[...] = _adam_update(w_ref[...], g, m_ref[...], v_ref[...])

    blk = pl.BlockSpec((br, D), lambda i: (i, 0))
    return pl.pallas_call(
        body, name=name, grid=(steps,), out_shape=(jax.ShapeDtypeStruct((rows, D), f32),) * 4,
        in_specs=[blk, pl.BlockSpec((3, br, D), lambda i: (0, i, 0)), blk, blk, blk]
        + [pl.BlockSpec(memory_space=pl.ANY)] * len(after), out_specs=(blk,) * 4,
        compiler_params=pltpu.CompilerParams(dimension_semantics=("parallel",), vmem_limit_bytes=V7X_VMEM_LIMIT),
    )(own, arrived, w, m, v, *after)


def _adamw(ws, gs, ms, vs, packed, name, after=()):
    n = len(ws)
    given = [g for g in gs if not isinstance(g, int)]
    taken = [j for j in range(n) if isinstance(gs[j], int)]

    def body(packed_ref, *refs):
        w_r, m_r, v_r = (refs[k * n:(k + 1) * n] for k in range(3))
        given_r, outs = list(refs[3 * n:3 * n + len(given)]), refs[3 * n + len(given) + len(after):]
        g_o, outs = dict(zip(taken, outs[:len(taken)])), outs[len(taken):]
        d_o, m_o, v_o = (outs[k * n:(k + 1) * n] for k in range(3))
        for j in range(n):
            if j in g_o:
                (r, c), at = ws[j].shape, gs[j]
                if c == 128:
                    g = packed_ref[at:at + r, :]
                else:
                    assert r == 1
                    g = jnp.concatenate([packed_ref[at + k:at + k + 1, :] for k in range(c // 128)], axis=1)
                g_o[j][...] = g
            else:
                g = given_r.pop(0)[...]
            d_o[j][...], m_o[j][...], v_o[j][...] = _adam_update(w_r[j][...], g, m_r[j][...], v_r[j][...])

    vm = pl.BlockSpec(memory_space=pltpu.VMEM)
    shapes = tuple(jax.ShapeDtypeStruct(w.shape, f32) for w in ws)
    n_out = len(taken) + 3 * n
    return pl.pallas_call(
        body, name=name, out_shape=tuple(shapes[j] for j in taken) + shapes * 3,
        in_specs=[vm] * (1 + 3 * n + len(given)) + [pl.BlockSpec(memory_space=pl.ANY)] * len(after),
        out_specs=tuple([vm] * n_out),
        compiler_params=pltpu.CompilerParams(vmem_limit_bytes=V7X_VMEM_LIMIT),
    )(packed, *ws, *ms, *vs, *given, *after)


SMALL_FFN = (("ln1_g", D), ("ln1_b", D), ("ln2_g", D), ("ln2_b", D), ("conv_b", D_FF), ("conv_w", 3 * D_FF), ("loss", 128))
SMALL_FFN_AT = 520
SMALL_ROWS = 704


def _small_rows():
    rows, at = {"w_pool": 0, "pool_scale": GROUPS * DH}, SMALL_FFN_AT
    for k, size in SMALL_FFN:
        rows[k] = at
        at += size // 128
    return rows


def kernel(x, w_in, w_pool, pool_scale, w_out, ln1_g, ln1_b, w_up, conv_w, conv_b, w_down, ln2_g, ln2_b, loss_target, m_w_in, m_w_pool, m_pool_scale, m_w_out, m_ln1_g, m_ln1_b, m_w_up, m_conv_w, m_conv_b, m_w_down, m_ln2_g, m_ln2_b, v_w_in, v_w_pool, v_pool_scale, v_w_out, v_ln1_g, v_ln1_b, v_w_up, v_conv_w, v_conv_b, v_w_down, v_ln2_g, v_ln2_b):
    me = 4 * lax.axis_index("x") + 2 * lax.axis_index("y") + lax.axis_index("c")
    x2, tgt = x[0], loss_target[0]

    cos, sin = _rope_tables()
    dmat, qd, kd, cdec = _decay_tables(RET_TILE)

    qkv, g, oret, states, cat, pooled, xhat1, rstd1, x1b, xb, g_in, g_out, g_up, g_down, g_cw = _mix_forward(
        x2, w_in[0].T, w_out[0], cos, sin, dmat, qd, kd, cdec, w_pool[0], pool_scale, ln1_g, ln1_b,
        gather_bf16=[w_up[0].T, w_down[0]], gather=[jnp.transpose(conv_w, (1, 0, 2))])
    w_in_t = g_in.reshape(IN_W, D)
    w_out_f = g_out.reshape(D, D)
    w_up_t = g_up.reshape(2 * D_FF, D)
    w_down_f = g_down.reshape(D_FF, D)
    conv_w_f = jnp.transpose(g_cw[:, :, 0, :], (1, 0, 2)).reshape(3, D_FF)
    dz1, dz2b, du, f, loss8, d_ln2_g, d_ln2_b, d_ln1_g, d_ln1_b, d_conv_b, d_conv_w = _ffn_forward_backward(
        xhat1, rstd1, ln1_g, ln1_b, w_up_t, conv_w_f, conv_b, w_down_f, ln2_g, ln2_b, tgt)
    small_ffn = [d_ln1_g, d_ln1_b, d_ln2_g, d_ln2_b, d_conv_b, d_conv_w, loss8]

    (dw_down,) = _weight_grad(f, dz2b, "grad_w_down", tm=D_FF // 2)
    own_down, oth_down = _pair_reduce([dw_down.reshape(N_DEV, ROWS_DOWN, D)], "pair_reduce_down")
    dw_up_t, arr_down = _weight_grad(du, x1b, "grad_w_up", tm=D_FF // 2, exchange=[oth_down])
    own_up, oth_up = _pair_reduce([dw_up_t.reshape(N_DEV, ROWS_UP, D)], "pair_reduce_up")
    up_sems, up_src, up_land, up_started = _exchange_start(oth_up, "exchange_up_start", CHIP_BARRIER_SPLIT)
    dproj, grad_x, small, dw_out = _mix_backward(
        dz1, w_out_f, qkv, g, oret, states, pooled, cat, cos, sin, dmat, qd, kd, cdec, w_pool[0], pool_scale, w_in_t,
        small_ffn, after=up_started)
    own_out, own_small, oth_out, oth_small = _pair_reduce(
        [dw_out.reshape(N_DEV, ROWS_OUT, D), small.reshape(N_DEV, SMALL_ROWS // N_DEV, 128)], "pair_reduce_out")
    dw_in_t, arr_out, arr_small = _weight_grad(dproj, xb, "grad_w_in", tm=IN_W // 2, exchange=[oth_out, oth_small])
    arr_up = _exchange_wait(up_sems, up_src, up_land, [dw_in_t], "exchange_up_wait")
    own_in, oth_in = _pair_reduce([dw_in_t.reshape(N_DEV, ROWS_IN, D)], "pair_reduce_in")
    (small_piece,) = _sum_parts([own_small], [arr_small], "sum_small_grads")
    (gs_small,) = _all_gather([small_piece], "gather_small_grads")
    in_sems, in_src, in_land, started = _exchange_start(oth_in, "exchange_in_start", CHIP_BARRIER, after=(gs_small,))

    names = ["w_in", "w_pool", "pool_scale", "w_out", "ln1_g", "ln1_b", "w_up", "conv_w", "conv_b", "w_down",
             "ln2_g", "ln2_b"]
    w_d = dict(w_in=w_in, w_pool=w_pool, pool_scale=pool_scale, w_out=w_out, ln1_g=ln1_g, ln1_b=ln1_b, w_up=w_up,
               conv_w=conv_w, conv_b=conv_b, w_down=w_down, ln2_g=ln2_g, ln2_b=ln2_b)
    m_d = dict(w_in=m_w_in, w_pool=m_w_pool, pool_scale=m_pool_scale, w_out=m_w_out, ln1_g=m_ln1_g, ln1_b=m_ln1_b,
               w_up=m_w_up, conv_w=m_conv_w, conv_b=m_conv_b, w_down=m_w_down, ln2_g=m_ln2_g, ln2_b=m_ln2_b)
    v_d = dict(w_in=v_w_in, w_pool=v_w_pool, pool_scale=v_pool_scale, w_out=v_w_out, ln1_g=v_ln1_g, ln1_b=v_ln1_b,
               w_up=v_w_up, conv_w=v_conv_w, conv_b=v_conv_b, w_down=v_w_down, ln2_g=v_ln2_g, ln2_b=v_ln2_b)
    g_d, delta, new_m, new_v = {}, {}, {}, {}

    def big_adamw(k, own, arr, transposed, steps, after=()):
        lay = (lambda a: a[0].T) if transposed else (lambda a: a[0])
        back = (lambda a: a.T[None]) if transposed else (lambda a: a[None])
        res = _sum_adamw(own, arr, lay(w_d[k]), lay(m_d[k]), lay(v_d[k]), "adamw_" + k, steps, after)
        g_d[k], delta[k], new_m[k], new_v[k] = (back(r) for r in res)
        return res[3]

    done = [big_adamw("w_up", own_up, arr_up, True, 4, after=(started,)),
            big_adamw("w_down", own_down, arr_down, False, 2, after=(started,)),
            big_adamw("w_out", own_out, arr_out, False, 2, after=(started,))]

    gs_small, rows = gs_small.reshape(SMALL_ROWS, 128), _small_rows()
    g_conv_w = gs_small[rows["conv_w"]:rows["conv_w"] + 3 * D_FF // 128].reshape(3, D_FF)
    g_d["conv_w"] = lax.dynamic_slice(g_conv_w, (0, me * (D_FF // N_DEV)), (3, D_FF // N_DEV))[None]
    lay = lambda k, a: jnp.transpose(a, (1, 0, 2)) if k == "conv_w" else a.reshape(-1, a.shape[-1])
    back = lambda k, a: jnp.transpose(a, (1, 0, 2)) if k == "conv_w" else a.reshape(w_d[k].shape)
    group = [k for k in names if k not in ("w_in", "w_out", "w_up", "w_down")]
    packed = [k for k in group if k != "conv_w"]
    res = _adamw([lay(k, w_d[k]) for k in group], [lay(k, g_d[k]) if k == "conv_w" else rows[k] for k in group],
                 [lay(k, m_d[k]) for k in group], [lay(k, v_d[k]) for k in group], gs_small, "adamw_small",
                 after=(started,))
    for j, k in enumerate(packed):
        g_d[k] = back(k, res[j])
    for j, k in enumerate(group):
        delta[k], new_m[k], new_v[k] = (back(k, res[len(packed) + part * len(group) + j]) for part in range(3))

    arr_in = _exchange_wait(in_sems, in_src, in_land, done + [res[0]], "exchange_in_wait")
    big_adamw("w_in", own_in, arr_in, True, 4)

    loss = gs_small[rows["loss"], 0]
    return (loss, grad_x[None], *[g_d[k] for k in names], *[delta[k] for k in names], *[new_m[k] for k in names],
            *[new_v[k] for k in names])
```

```python
import math

import numpy as np
import jax
import jax.numpy as jnp
from jax import lax
from jax.experimental import pallas as pl
from jax.experimental.pallas import tpu as pltpu

f32 = jnp.float32
bf16 = jnp.bfloat16

N_DEV = 8
T = 4096
D = 1024
CHUNK = 64
MIX_TILE = 512
RET_TILE = 256
HEADS = 4
DH = 128
RW = HEADS * DH
PW = 512
GROUPS = 4
WINDOWS = (2, 4, 8, 16)
IN_W = 4 * RW + PW
D_FF = 2816
LN_EPS = 1e-5
RMS_EPS = 1e-6
ALPHA = 2.0 ** 0.25
K_SCALE = DH ** -0.5

ADAM_LR = 0.001
ADAM_B1 = 0.9
ADAM_B2 = 0.999
ADAM_EPS = 1e-08
ADAM_WD = 0.01
ADAM_STEP = 10

ROWS_IN, ROWS_OUT, ROWS_UP, ROWS_DOWN = IN_W // N_DEV, D // N_DEV, 2 * D_FF // N_DEV, D_FF // N_DEV

V7X_VMEM_LIMIT = 56 * 2 ** 20
HALO = 32

NT = (((1,), (1,)), ((), ()))
TN = (((0,), (0,)), ((), ()))
NN = (((1,), (0,)), ((), ()))


def _dot(a, b, dims=NN):
    return lax.dot_general(a, b, dims, preferred_element_type=f32)


def _const_spec(shape):
    zeros = (0,) * len(shape)
    return pl.BlockSpec(shape, lambda i: zeros, pipeline_mode=pl.Buffered(1))


def _sigmoid(x):
    return 0.5 * jnp.tanh(0.5 * x) + 0.5


def _decay_tables(tt):
    h = np.arange(HEADS, dtype=np.float64)
    log_gamma = np.log(1.0 - 2.0 ** (-5.0 - h)).astype(np.float32).astype(np.float64)[:, None, None]
    idx = np.arange(tt, dtype=np.float64)
    visible = (idx[None, :] // CHUNK) <= (idx[:, None] // CHUNK)
    mask = np.where(visible[None], np.exp(log_gamma * np.abs(idx[:, None] - idx[None, :])[None]), 0.0)
    qd = np.broadcast_to(np.exp(log_gamma * (idx[None, :, None] + 1.0)), (HEADS, tt, DH))
    kd = np.broadcast_to(np.exp(log_gamma * (tt - 1.0 - idx[None, :, None])), (HEADS, tt, DH))
    cd = np.exp(log_gamma[:, 0, 0] * tt)
    return (jnp.asarray(mask, f32), jnp.asarray(qd, f32), jnp.asarray(kd, f32), [float(c) for c in cd])


def _rope_tables():
    inv_freq = (10000.0 ** (-np.arange(0, DH, 2, dtype=np.float64) / DH)).astype(np.float32)
    ang = (np.arange(T, dtype=np.float32)[:, None] * inv_freq[None, :]).astype(np.float64)
    cos, sin = np.cos(ang), np.sin(ang)
    return (jnp.asarray(np.concatenate([cos, cos], axis=1), f32), jnp.asarray(np.concatenate([-sin, sin], axis=1), f32))


def _swap_halves(t):
    return pltpu.roll(t, DH // 2, axis=1)


def _mix_forward(x, w_in_shard, w_out_shard, cos, sin, dmat, qd, kd, cdec, w_pool, pool_scale, ln1_g, ln1_b,
                 gather_bf16, gather, tt=MIX_TILE):
    n_tiles = T // tt
    to_bf16 = [w_in_shard, w_out_shard] + list(gather_bf16)
    n_c, n_g = len(to_bf16), len(gather_bf16) + len(gather)

    def body(x_ref, cos_ref, sin_ref, dmat_ref, qd_ref, kd_ref, wpool_ref, pscale_ref, g1_ref, b1_ref, *rest):
        f32_in, plain_in, rest = rest[:n_c], rest[n_c:2 + n_g], rest[2 + n_g:]
        qkv_ref, g_ref, oret_ref, states_ref, cat_ref, pooled_ref, xhat_ref, rstd_ref, x1b_ref, xb_ref = rest[:10]
        fout, gout = rest[10:12], rest[12:12 + n_g]
        state_s, pext_s, tmp_s, wint_s, wout_s, load_sems, stage_sems, *rest = rest[12 + n_g:]
        stage_s, cast_s, sems = rest[:n_c], rest[n_c:2 * n_c], rest[2 * n_c:]
        fin, gin, fsems, gsems = cast_s[:2], tuple(cast_s[2:]) + tuple(plain_in), sems[:3], sems[3:]
        i = pl.program_id(0)

        @pl.when(i == 0)
        def _():
            stage = [pltpu.make_async_copy(src, dst, stage_sems.at[j]) for j, (src, dst) in enumerate(zip(f32_in, stage_s))]
            for cp in stage:
                cp.start()
            state_s[...] = jnp.zeros_like(state_s)
            pext_s[:, pl.ds(0, HALO), :] = jnp.zeros((GROUPS, HALO, DH), f32)

            def cast(js):
                for j in js:
                    stage[j].wait()
                    cast_s[j][...] = stage_s[j][...].astype(bf16)

            _barrier(_gather_peers())
            cast(range(2))
            _gather_start(fin, fout, *fsems)
            cast(range(2, n_c))
            _gather_forward(fin, fout, *fsems)
            _gather_start(gin, gout, *gsems)
            _gather_finish(fin, fout, *fsems)
            loads = [pltpu.make_async_copy(src.at[s], dst.at[pl.ds(s * src.shape[1], src.shape[1]), :],
                                           load_sems.at[j, s])
                     for j, (src, dst) in enumerate(((fout[0], wint_s), (fout[1], wout_s))) for s in range(N_DEV)]
            for ld in loads:
                ld.start()
            for ld in loads:
                ld.wait()

        @pl.when(i == n_tiles - 3)
        def _():
            _gather_forward(gin, gout, *gsems)

        xb = x_ref[...].astype(bf16)
        xb_ref[...] = xb
        cos_t, sin_t = cos_ref[...], sin_ref[...]
        for part in range(2):
            pr = _dot(xb, wint_s[pl.ds(part * RW, RW), :], NT)
            for h in range(HEADS):
                t = pr[:, h * DH:(h + 1) * DH]
                r = t * cos_t + _swap_halves(t) * sin_t
                if part == 1:
                    r = r * K_SCALE
                qkv_ref[:, part * RW + h * DH: part * RW + (h + 1) * DH] = r.astype(bf16)
        qkv_ref[:, 2 * RW:3 * RW] = _dot(xb, wint_s[pl.ds(2 * RW, RW), :], NT).astype(bf16)
        g_ref[...] = _dot(xb, wint_s[pl.ds(3 * RW, RW), :], NT)
        p = _dot(xb, wint_s[pl.ds(4 * RW, PW), :], NT)
        for gi in range(GROUPS):
            pext_s[gi, pl.ds(HALO, tt), :] = p[:, gi * DH:(gi + 1) * DH]

        for sub in range(tt // RET_TILE):
            rows = pl.ds(sub * RET_TILE, RET_TILE)
            for h in range(HEADS):
                q = qkv_ref[rows, h * DH:(h + 1) * DH]
                k = qkv_ref[rows, RW + h * DH: RW + (h + 1) * DH]
                v = qkv_ref[rows, 2 * RW + h * DH: 2 * RW + (h + 1) * DH]
                s = _dot(q, k, NT) * dmat_ref[h]
                st = state_s[h]
                stb = st.astype(bf16)
                states_ref[sub, h] = stb
                oret_ref[rows, h * DH:(h + 1) * DH] = (_dot(s.astype(bf16), v)
                                                      + _dot((q.astype(f32) * qd_ref[h]).astype(bf16), stb))
                state_s[h] = st * cdec[h] + _dot((k.astype(f32) * kd_ref[h]).astype(bf16), v, TN)

        for h in range(HEADS):
            sl = slice(h * DH, (h + 1) * DH)
            o = oret_ref[:, sl]
            r = lax.rsqrt(jnp.mean(o * o, axis=-1, keepdims=True) + RMS_EPS)
            gg = g_ref[:, sl]
            cat_ref[:, sl] = (o * r * (gg * _sigmoid(gg))).astype(bf16)

        pos1 = (i * tt + lax.broadcasted_iota(jnp.int32, (tt, 1), 0) + 1).astype(f32)
        for gi, w in enumerate(WINDOWS):
            sl = slice(gi * DH, (gi + 1) * DH)
            stages = int(math.log2(w))
            src = pext_s
            for s in range(stages):
                lo = HALO - 8 * (stages - 1 - s)
                n = tt + HALO - lo
                shift = 2 ** s
                val = src[gi, pl.ds(lo, n), :] + src[gi, pl.ds(lo - shift, n), :]
                if s == stages - 1:
                    wsum = val
                else:
                    tmp_s[gi, pl.ds(lo, n), :] = val
                    src = tmp_s
            p_g = pext_s[gi, pl.ds(HALO, tt), :]
            pooled = (wsum / jnp.minimum(pos1, float(w)) - p_g).astype(bf16)
            pooled_ref[:, sl] = pooled
            y = _dot(pooled, wpool_ref[gi].astype(bf16)) * pscale_ref[:, sl]
            cat_ref[:, RW + gi * DH: RW + (gi + 1) * DH] = y.astype(bf16)
        pext_s[:, pl.ds(0, HALO), :] = pext_s[:, pl.ds(tt, HALO), :]

        z = ALPHA * x_ref[...] + _dot(cat_ref[...], wout_s[...])
        mu = jnp.mean(z, axis=-1, keepdims=True)
        zc = z - mu
        rstd = lax.rsqrt(jnp.mean(zc * zc, axis=-1, keepdims=True) + LN_EPS)
        xhat = zc * rstd
        xhat_ref[...] = xhat
        rstd_ref[...] = rstd
        x1b_ref[...] = (xhat * g1_ref[...] + b1_ref[...]).astype(bf16)

        @pl.when(i == n_tiles - 1)
        def _():
            _gather_finish(gin, gout, *gsems)

    tile = lambda w: pl.BlockSpec((tt, w), lambda i: (i, 0))
    hbm = pl.BlockSpec(memory_space=pltpu.HBM)
    out_shape = (
        jax.ShapeDtypeStruct((T, 3 * RW), bf16),
        jax.ShapeDtypeStruct((T, RW), f32),
        jax.ShapeDtypeStruct((T, RW), f32),
        jax.ShapeDtypeStruct((T // RET_TILE, HEADS, DH, DH), bf16),
        jax.ShapeDtypeStruct((T, D), bf16),
        jax.ShapeDtypeStruct((T, PW), bf16),
        jax.ShapeDtypeStruct((T, D), f32),
        jax.ShapeDtypeStruct((T, 1), f32),
        jax.ShapeDtypeStruct((T, D), bf16),
        jax.ShapeDtypeStruct((T, D), bf16),
    ) + tuple(jax.ShapeDtypeStruct((N_DEV,) + b.shape, bf16) for b in to_bf16
              ) + tuple(jax.ShapeDtypeStruct((N_DEV,) + b.shape, b.dtype) for b in gather)
    return pl.pallas_call(
        body, name="mix_forward", grid=(n_tiles,), out_shape=out_shape,
        in_specs=[tile(D), tile(DH), tile(DH),
                  _const_spec((HEADS, RET_TILE, RET_TILE)), _const_spec((HEADS, RET_TILE, DH)),
                  _const_spec((HEADS, RET_TILE, DH)),
                  _const_spec((GROUPS, DH, DH)), _const_spec((1, PW)),
                  _const_spec((1, D)), _const_spec((1, D))] + [hbm] * (2 + n_g),
        out_specs=(tile(3 * RW), tile(RW), tile(RW),
                   pl.BlockSpec((tt // RET_TILE, HEADS, DH, DH), lambda i: (i, 0, 0, 0)),
                   tile(D), tile(PW), tile(D), tile(1), tile(D), tile(D)) + (hbm,) * (2 + n_g),
        scratch_shapes=[pltpu.VMEM((HEADS, DH, DH), f32), pltpu.VMEM((GROUPS, tt + HALO, DH), f32),
                        pltpu.VMEM((GROUPS, tt + HALO, DH), f32), pltpu.VMEM((IN_W, D), bf16), pltpu.VMEM((D, D), bf16),
                        pltpu.SemaphoreType.DMA((2, N_DEV)), pltpu.SemaphoreType.DMA((n_c,))]
        + [pltpu.VMEM(b.shape, f32) for b in to_bf16] + [pltpu.VMEM(b.shape, bf16) for b in to_bf16]
        + _gather_sems(2) + _gather_sems(n_g),
        compiler_params=pltpu.CompilerParams(dimension_semantics=("arbitrary",), vmem_limit_bytes=V7X_VMEM_LIMIT,
                                             collective_id=GATHER_BARRIER),
    )(x, cos, sin, dmat, qd, kd, w_pool, pool_scale, ln1_g, ln1_b, *to_bf16, *gather)


def _ffn_forward_backward(xhat1, rstd1, ln1_g, ln1_b, w_up_t, conv_w, conv_b, w_down, ln2_g, ln2_b, target,
                          tt=256):
    n_tiles = T // tt
    FH = 16
    hb = tt // FH

    def body(xhat_ref, halo_ref, rstd_ref, g1_ref, b1_ref, wupt_ref, cw_ref, cb_ref, wdown_ref, g2_ref, b2_ref, tgt_ref,
             dz1_ref, dz2b_ref, du_ref, f_ref, loss_ref, dg2_ref, db2_ref, dg1_ref, db1_ref, dcb_ref, dcw_ref,
             gext_s, val_s, dhext_s):
        i = pl.program_id(0)
        tile_idx = n_tiles - 1 - i

        def rd(ref, off):
            return jnp.concatenate([ref[k, pl.ds(off, tt), :] for k in range(D_FF // 128)], axis=1)

        def wr(ref, val):
            for k in range(D_FF // 128):
                ref[k, pl.ds(0, val.shape[0]), :] = val[:, k * 128:(k + 1) * 128]

        @pl.when(i == 0)
        def _():
            for r in (loss_ref, dg2_ref, db2_ref, dg1_ref, db1_ref, dcb_ref, dcw_ref):
                r[...] = jnp.zeros_like(r)
            dhext_s[:, pl.ds(tt, 8), :] = jnp.zeros((D_FF // 128, 8, 128), f32)

        g1, b1 = g1_ref[...], b1_ref[...]
        xhat = xhat_ref[...]
        x1 = xhat * g1 + b1
        x1b = x1.astype(bf16)
        x1h = ((halo_ref[...] * g1 + b1) * jnp.where(tile_idx == 0, 0.0, 1.0)).astype(bf16)
        x1ext = jnp.concatenate([x1h, x1b], axis=0)

        val = _dot(x1b, wupt_ref[pl.ds(0, D_FF), :], NT)
        gate_ext = _dot(x1ext, wupt_ref[pl.ds(D_FF, D_FF), :], NT)
        wr(gext_s, gate_ext)
        hh = (cb_ref[...] + cw_ref[0:1, :] * rd(gext_s, FH - 2) + cw_ref[1:2, :] * rd(gext_s, FH - 1)
              + cw_ref[2:3, :] * gate_ext[FH:])
        sg = _sigmoid(hh)
        act = hh * sg
        wr(dhext_s, act)
        val_s[...] = val * (sg + act * (1.0 - sg))
        fb = (act * val).astype(bf16)
        f_ref[...] = fb

        z = ALPHA * x1 + _dot(fb, wdown_ref[...])
        mu = jnp.mean(z, axis=-1, keepdims=True)
        zc = z - mu
        rstd2 = lax.rsqrt(jnp.mean(zc * zc, axis=-1, keepdims=True) + LN_EPS)
        xh2 = zc * rstd2
        diff = xh2 * g2_ref[...] + b2_ref[...] - tgt_ref[...]
        loss_ref[...] += 0.5 * jnp.sum(diff * diff) / D
        dy = diff * (1.0 / D)
        dg2_ref[...] += jnp.sum(dy * xh2, axis=0, keepdims=True)
        db2_ref[...] += jnp.sum(dy, axis=0, keepdims=True)
        dyg = dy * g2_ref[...]
        dz2 = rstd2 * (dyg - jnp.mean(dyg, axis=-1, keepdims=True) - xh2 * jnp.mean(dyg * xh2, axis=-1, keepdims=True))
        dz2b = dz2.astype(bf16)
        dz2b_ref[...] = dz2b

        df = _dot(dz2b, wdown_ref[...], NT)
        dval = df * rd(dhext_s, 0)
        dh = df * val_s[...]
        wr(dhext_s, dh)
        dh1, dh2, g0 = rd(dhext_s, 1), rd(dhext_s, 2), rd(gext_s, FH)
        dcb_ref[...] += jnp.sum(dh, axis=0, keepdims=True)
        dcw_ref[0:1, :] += jnp.sum(dh2 * g0, axis=0, keepdims=True)
        dcw_ref[1:2, :] += jnp.sum(dh1 * g0, axis=0, keepdims=True)
        dcw_ref[2:3, :] += jnp.sum(dh * g0, axis=0, keepdims=True)
        dgate = cw_ref[2:3, :] * dh + cw_ref[1:2, :] * dh1 + cw_ref[0:1, :] * dh2
        dvalb, dgateb = dval.astype(bf16), dgate.astype(bf16)
        du_ref[:, :D_FF] = dvalb
        du_ref[:, D_FF:] = dgateb
        dx1 = ALPHA * dz2 + _dot(dvalb, wupt_ref[pl.ds(0, D_FF), :]) + _dot(dgateb, wupt_ref[pl.ds(D_FF, D_FF), :])
        dhext_s[:, pl.ds(tt, 8), :] = dhext_s[:, pl.ds(0, 8), :]

        dg1_ref[...] += jnp.sum(dx1 * xhat, axis=0, keepdims=True)
        db1_ref[...] += jnp.sum(dx1, axis=0, keepdims=True)
        dxg = dx1 * g1
        dz1_ref[...] = rstd_ref[...] * (dxg - jnp.mean(dxg, axis=-1, keepdims=True)
                                        - xhat * jnp.mean(dxg * xhat, axis=-1, keepdims=True))

    rtile = lambda w: pl.BlockSpec((tt, w), lambda i: (n_tiles - 1 - i, 0))
    acc = lambda shape: pl.BlockSpec(shape, lambda i: (0, 0))
    out_shape = (
        jax.ShapeDtypeStruct((T, D), f32),
        jax.ShapeDtypeStruct((T, D), bf16),
        jax.ShapeDtypeStruct((T, 2 * D_FF), bf16),
        jax.ShapeDtypeStruct((T, D_FF), bf16),
        jax.ShapeDtypeStruct((8, 128), f32),
        jax.ShapeDtypeStruct((1, D), f32), jax.ShapeDtypeStruct((1, D), f32),
        jax.ShapeDtypeStruct((1, D), f32), jax.ShapeDtypeStruct((1, D), f32),
        jax.ShapeDtypeStruct((1, D_FF), f32), jax.ShapeDtypeStruct((3, D_FF), f32),
    )
    return pl.pallas_call(
        body, name="ffn_forward_backward", grid=(n_tiles,), out_shape=out_shape,
        in_specs=[rtile(D),
                  pl.BlockSpec((FH, D), lambda i: (jnp.maximum((n_tiles - 1 - i) * hb - 1, 0), 0)),
                  rtile(1), _const_spec((1, D)), _const_spec((1, D)), _const_spec((2 * D_FF, D)),
                  _const_spec((3, D_FF)), _const_spec((1, D_FF)), _const_spec((D_FF, D)),
                  _const_spec((1, D)), _const_spec((1, D)), rtile(D)],
        out_specs=(rtile(D), rtile(D), rtile(2 * D_FF), rtile(D_FF), acc((8, 128)),
                   acc((1, D)), acc((1, D)), acc((1, D)), acc((1, D)), acc((1, D_FF)), acc((3, D_FF))),
        scratch_shapes=[pltpu.VMEM((D_FF // 128, tt + FH, 128), f32), pltpu.VMEM((tt, D_FF), f32),
                        pltpu.VMEM((D_FF // 128, tt + 8, 128), f32)],
        compiler_params=pltpu.CompilerParams(dimension_semantics=("arbitrary",), vmem_limit_bytes=V7X_VMEM_LIMIT),
    )(xhat1, xhat1, rstd1, ln1_g, ln1_b, w_up_t, conv_w, conv_b, w_down, ln2_g, ln2_b, target)


def _mix_backward(dz1, w_out, qkv, g, oret, states, pooled, cat, cos, sin, dmat, qd, kd, cdec, w_pool, pool_scale, w_in_t,
                  small_ffn, after, tt=MIX_TILE):
    n_tiles = T // tt

    def body(dz1_ref, wout_ref, qkv_ref, g_ref, oret_ref, states_ref, pooled_ref, cat_ref, cos_ref, sin_ref, dmat_ref,
             qd_ref, kd_ref, wpool_ref, pscale_ref, wint_ref, *rest):
        ffn_refs, rest = rest[:len(SMALL_FFN)], rest[len(SMALL_FFN):]
        after_ref, dproj_ref, gx_ref, small_ref, dwout_ref, dstate_s, dout_s, eext_s, tmp_s, dwout_s, dpscale_s = rest
        i = pl.program_id(0)
        tile_idx = n_tiles - 1 - i

        @pl.when(i == 0)
        def _():
            dstate_s[...] = jnp.zeros_like(dstate_s)
            small_ref[...] = jnp.zeros_like(small_ref)
            dpscale_s[...] = jnp.zeros_like(dpscale_s)
            dwout_s[...] = jnp.zeros_like(dwout_s)
            eext_s[:, pl.ds(tt, HALO), :] = jnp.zeros((GROUPS, HALO, DH), f32)

        dz1 = dz1_ref[...]
        dz1b = dz1.astype(bf16)
        dcat = _dot(dz1b, wout_ref[...], NT)
        dwout_s[...] += _dot(cat_ref[...], dz1b, TN)

        pos1 = (tile_idx * tt + lax.broadcasted_iota(jnp.int32, (tt, 1), 0) + 1).astype(f32)
        for gi, w in enumerate(WINDOWS):
            sl = slice(gi * DH, (gi + 1) * DH)
            dpo = dcat[:, RW + gi * DH: RW + (gi + 1) * DH]
            pooled_g = pooled_ref[:, sl]
            wpool_g = wpool_ref[gi].astype(bf16)
            ylin = _dot(pooled_g, wpool_g)
            dpscale_s[:, sl] += jnp.sum(dpo * ylin, axis=0, keepdims=True)
            dpw = (dpo * pscale_ref[:, sl]).astype(bf16)
            small_ref[pl.ds(gi * DH, DH), :] += _dot(pooled_g, dpw, TN)
            dpooled = _dot(dpw, wpool_g, NT)
            eext_s[gi, pl.ds(0, tt), :] = dpooled / jnp.minimum(pos1, float(w))
            stages = int(math.log2(w))
            src = eext_s
            for s in range(stages):
                n = tt + 8 * (stages - 1 - s)
                shift = 2 ** s
                val = src[gi, pl.ds(0, n), :] + src[gi, pl.ds(shift, n), :]
                if s == stages - 1:
                    wsum = val
                else:
                    tmp_s[gi, pl.ds(0, n), :] = val
                    src = tmp_s
            dproj_ref[:, 4 * RW + gi * DH: 4 * RW + (gi + 1) * DH] = (wsum - dpooled).astype(bf16)
        eext_s[:, pl.ds(tt, HALO), :] = eext_s[:, pl.ds(0, HALO), :]

        for h in range(HEADS):
            sl = slice(h * DH, (h + 1) * DH)
            dr = dcat[:, sl]
            o = oret_ref[:, sl]
            r = lax.rsqrt(jnp.mean(o * o, axis=-1, keepdims=True) + RMS_EPS)
            rn = o * r
            gg = g_ref[:, sl]
            sg = _sigmoid(gg)
            dproj_ref[:, 3 * RW + h * DH: 3 * RW + (h + 1) * DH] = (dr * rn * (sg * (1.0 + gg * (1.0 - sg)))).astype(bf16)
            drn = dr * (gg * sg)
            dout_s[:, sl] = (r * (drn - rn * jnp.mean(drn * rn, axis=-1, keepdims=True))).astype(bf16)

        for sub in reversed(range(tt // RET_TILE)):
            rows = pl.ds(sub * RET_TILE, RET_TILE)
            cos_t, sin_t = cos_ref[rows, :], sin_ref[rows, :]
            for h in range(HEADS):
                q = qkv_ref[rows, h * DH:(h + 1) * DH]
                k = qkv_ref[rows, RW + h * DH: RW + (h + 1) * DH]
                v = qkv_ref[rows, 2 * RW + h * DH: 2 * RW + (h + 1) * DH]
                do = dout_s[rows, h * DH:(h + 1) * DH]
                stb = states_ref[sub, h]
                dst = dstate_s[h]
                dstb = dst.astype(bf16)
                sb = (_dot(q, k, NT) * dmat_ref[h]).astype(bf16)
                dsb = (_dot(do, v, NT) * dmat_ref[h]).astype(bf16)
                dq = _dot(dsb, k) + _dot(do, stb, NT) * qd_ref[h]
                dk = _dot(dsb, q, TN) + _dot(v, dstb, NT) * kd_ref[h]
                dv = _dot(sb, do, TN) + _dot((k.astype(f32) * kd_ref[h]).astype(bf16), dstb)
                dstate_s[h] = dst * cdec[h] + _dot((q.astype(f32) * qd_ref[h]).astype(bf16), do, TN)
                dproj_ref[rows, h * DH:(h + 1) * DH] = (dq * cos_t - _swap_halves(dq) * sin_t).astype(bf16)
                dproj_ref[rows, RW + h * DH: RW + (h + 1) * DH] = (
                    (dk * cos_t - _swap_halves(dk) * sin_t) * K_SCALE).astype(bf16)
                dproj_ref[rows, 2 * RW + h * DH: 2 * RW + (h + 1) * DH] = dv.astype(bf16)

        gx_ref[...] = ALPHA * dz1 + _dot(dproj_ref[...], wint_ref[...])

        @pl.when(i == n_tiles - 1)
        def _():
            dwout_ref[...] = dwout_s[...].astype(bf16)
            at = GROUPS * DH
            for ref, size in [(dpscale_s, PW)] + [(ref, size) for ref, (_, size) in zip(ffn_refs, SMALL_FFN)]:
                for j in range(size // 128):
                    r, k = divmod(j, ref.shape[1] // 128)
                    small_ref[at + j: at + j + 1, :] = ref[r:r + 1, k * 128:(k + 1) * 128]
                at = SMALL_FFN_AT if ref is dpscale_s else at + size // 128

    rtile = lambda w: pl.BlockSpec((tt, w), lambda i: (n_tiles - 1 - i, 0))
    out_shape = (
        jax.ShapeDtypeStruct((T, IN_W), bf16),
        jax.ShapeDtypeStruct((T, D), f32),
        jax.ShapeDtypeStruct((SMALL_ROWS, 128), f32),
        jax.ShapeDtypeStruct((D, D), bf16),
    )
    return pl.pallas_call(
        body, name="mix_backward", grid=(n_tiles,), out_shape=out_shape,
        in_specs=[rtile(D), _const_spec((D, D)), rtile(3 * RW), rtile(RW), rtile(RW),
                  pl.BlockSpec((tt // RET_TILE, HEADS, DH, DH), lambda i: (n_tiles - 1 - i, 0, 0, 0)),
                  rtile(PW), rtile(D), rtile(DH), rtile(DH),
                  _const_spec((HEADS, RET_TILE, RET_TILE)), _const_spec((HEADS, RET_TILE, DH)),
                  _const_spec((HEADS, RET_TILE, DH)),
                  _const_spec((GROUPS, DH, DH)), _const_spec((1, PW)), _const_spec((IN_W, D)),
                  *[_const_spec(a.shape) for a in small_ffn], pl.BlockSpec(memory_space=pl.ANY)],
        out_specs=(rtile(IN_W), rtile(D), pl.BlockSpec((SMALL_ROWS, 128), lambda i: (0, 0)),
                   pl.BlockSpec((D, D), lambda i: (0, 0), pipeline_mode=pl.Buffered(1))),
        scratch_shapes=[pltpu.VMEM((HEADS, DH, DH), f32), pltpu.VMEM((tt, RW), bf16),
                        pltpu.VMEM((GROUPS, tt + HALO, DH), f32), pltpu.VMEM((GROUPS, tt + HALO, DH), f32),
                        pltpu.VMEM((D, D), f32), pltpu.VMEM((1, PW), f32)],
        compiler_params=pltpu.CompilerParams(dimension_semantics=("arbitrary",), vmem_limit_bytes=V7X_VMEM_LIMIT),
    )(dz1, w_out, qkv, g, oret, states, pooled, cat, cos, sin, dmat, qd, kd, w_pool, pool_scale, w_in_t, *small_ffn,
      after)


def _weight_grad(a, b, name, tm, exchange=()):
    m = a.shape[1]
    n_m, n_e = m // tm, len(exchange)

    def body(a_ref, b_ref, *rest):
        ein, o_ref, eout, sems = rest[:n_e], rest[n_e], rest[n_e + 1:2 * n_e + 1], rest[2 * n_e + 1:]
        i = pl.program_id(0)

        if n_e:
            @pl.when(i == 0)
            def _():
                _chip_exchange_start(ein, eout, *sems)

        o_ref[...] = _dot(a_ref[...], b_ref[...].astype(bf16), TN).astype(bf16)

        if n_e:
            @pl.when(i == n_m - 1)
            def _():
                _chip_exchange_finish(ein, eout, *sems)

    hbm = pl.BlockSpec(memory_space=pltpu.HBM)
    return pl.pallas_call(
        body, name=name, grid=(n_m,),
        out_shape=(jax.ShapeDtypeStruct((m, D), bf16),) + tuple(jax.ShapeDtypeStruct(e.shape, e.dtype) for e in exchange),
        in_specs=[pl.BlockSpec((T, tm), lambda i: (0, i)),
                  pl.BlockSpec((T, D), lambda i: (0, 0), pipeline_mode=pl.Buffered(1))] + [hbm] * n_e,
        out_specs=(pl.BlockSpec((tm, D), lambda i: (i, 0)),) + (hbm,) * n_e,
        scratch_shapes=_chip_exchange_sems(n_e),
        compiler_params=pltpu.CompilerParams(dimension_semantics=("arbitrary",), vmem_limit_bytes=V7X_VMEM_LIMIT,
                                             collective_id=CHIP_BARRIER if n_e else None),
    )(a, b, *exchange)


CHIP_FLIPS = ((1, 0), (0, 1), (1, 1))
PAIR_BARRIER, CHIP_BARRIER, GATHER_BARRIER, CHIP_BARRIER_SPLIT = 0, 1, 2, 3


def _barrier(peers):
    sem = pltpu.get_barrier_semaphore()
    for peer in peers:
        pl.semaphore_signal(sem, inc=1, device_id=peer, device_id_type=pl.DeviceIdType.MESH)
    pl.semaphore_wait(sem, len(peers))


def _me():
    return lax.axis_index("x"), lax.axis_index("y"), lax.axis_index("c")


def _chip(me, k):
    x, y, _ = me
    if k == 0:
        return x, y
    fx, fy = CHIP_FLIPS[k - 1]
    return (1 - x if fx else x), (1 - y if fy else y)


def _slot(x, y, c):
    return 4 * x + 2 * y + c


def _remote(src, dst, send_sem, recv_sem, to):
    return pltpu.make_async_remote_copy(src_ref=src, dst_ref=dst, send_sem=send_sem, recv_sem=recv_sem,
                                        device_id=to, device_id_type=pl.DeviceIdType.MESH)


def _gather_sems(n):
    return [pltpu.SemaphoreType.DMA((7, n)), pltpu.SemaphoreType.DMA((7, n)), pltpu.SemaphoreType.DMA((n,))] if n else []


def _gather_copy(k, j, gin, gout, send_sems, recv_sems, sending):
    x, y, c = _me()
    sibling, x_chip, y_chip, d_chip = (x, y, 1 - c), (1 - x, y), (x, 1 - y), (1 - x, 1 - y)
    south = c == 0
    passed_on = (jnp.where(south, 1 - x, x), jnp.where(south, y, 1 - y), c)
    src, to = gin[j], sibling
    if sending:
        block = {0: (x, y, c), 1: (x, y, c), 2: (x, y, c), 3: passed_on, 4: (*x_chip, c), 5: (*y_chip, c), 6: (*d_chip, c)}[k]
        to = {1: (*x_chip, c), 2: (*y_chip, c), 3: (jnp.where(south, x, 1 - x), jnp.where(south, 1 - y, y), c)}.get(k, sibling)
        if k >= 3:
            src = gout[j].at[_slot(*block)]
    else:
        block = {0: sibling, 1: (*x_chip, c), 2: (*y_chip, c), 3: (*d_chip, c), 4: (*x_chip, 1 - c), 5: (*y_chip, 1 - c),
                 6: (*d_chip, 1 - c)}[k]
    return _remote(src, gout[j].at[_slot(*block)], send_sems.at[k, j], recv_sems.at[k, j], to)


def _gather_do(ks, action, gin, gout, send_sems, recv_sems):
    for k in ks:
        for j in range(len(gin)):
            cp = _gather_copy(k, j, gin, gout, send_sems, recv_sems, action != "wait_recv")
            getattr(cp, action)()


def _gather_peers():
    x, y, c = _me()
    return [(x, y, 1 - c), (1 - x, y, c), (x, 1 - y, c)]


def _gather_start(gin, gout, send_sems, recv_sems, local_sems):
    for j in range(len(gin)):
        pltpu.make_async_copy(gin[j], gout[j].at[_slot(*_me())], local_sems.at[j]).start()
    _gather_do((0, 1, 2), "start", gin, gout, send_sems, recv_sems)


def _gather_forward(gin, gout, send_sems, recv_sems, local_sems):
    _gather_do((1, 2), "wait_recv", gin, gout, send_sems, recv_sems)
    _gather_do((3, 4, 5), "start", gin, gout, send_sems, recv_sems)


def _gather_finish(gin, gout, send_sems, recv_sems, local_sems):
    _gather_do((3,), "wait_recv", gin, gout, send_sems, recv_sems)
    _gather_do((6,), "start", gin, gout, send_sems, recv_sems)
    _gather_do((0, 4, 5, 6), "wait_recv", gin, gout, send_sems, recv_sems)
    _gather_do(range(7), "wait_send", gin, gout, send_sems, recv_sems)
    for j in range(len(gin)):
        pltpu.make_async_copy(gin[j], gout[j].at[_slot(*_me())], local_sems.at[j]).wait()


def _pair_reduce(parts, name, gather_sum=None):
    n = len(parts)
    n_h = 0 if gather_sum is None else 1

    def body(*refs):
        ins, g_terms, refs = refs[:n], refs[n:n + 2 * n_h], refs[n + 2 * n_h:]
        own, others, g_out, refs = refs[:n], refs[n:2 * n], refs[2 * n:2 * n + n_h], refs[2 * n + n_h:]
        landing, mine, (send_sems, recv_sems, local_sems), g_scratch = refs[:n], refs[n:2 * n], refs[2 * n:2 * n + 3], refs[2 * n + 3:]
        me = _me()
        x, y, c = me
        sibling = (x, y, 1 - c)
        _barrier(_gather_peers() if n_h else [sibling])
        if n_h:
            piece_s, g_sems = g_scratch[0], g_scratch[1:]
            acc = g_terms[0][...].astype(f32)
            for k in range(3):
                acc = acc + g_terms[1][k].astype(f32)
            piece_s[...] = acc
            _gather_start([piece_s], g_out, *g_sems)
        sends, loads = [], []
        for k in range(4):
            for j in range(n):
                cp = _remote(ins[j].at[_slot(*_chip(me, k), 1 - c)], landing[j].at[k], send_sems.at[k, j],
                             recv_sems.at[k, j], sibling)
                cp.start()
                sends.append(cp)
                ld = pltpu.make_async_copy(ins[j].at[_slot(*_chip(me, k), c)], mine[j].at[k], local_sems.at[k, j])
                ld.start()
                loads.append(ld)
        if n_h:
            _gather_forward([piece_s], g_out, *g_sems)
        for k in range(4):
            for j in range(n):
                loads[k * n + j].wait()
                _remote(ins[j].at[0], landing[j].at[k], send_sems.at[k, j], recv_sems.at[k, j], sibling).wait_recv()
                total = mine[j][k].astype(f32) + landing[j][k].astype(f32)
                if k == 0:
                    own[j][...] = total.astype(own[j].dtype)
                else:
                    others[j][k - 1] = total.astype(others[j].dtype)
        for cp in sends:
            cp.wait_send()
        if n_h:
            _gather_finish([piece_s], g_out, *g_sems)

    vm, hbm = pl.BlockSpec(memory_space=pltpu.VMEM), pl.BlockSpec(memory_space=pltpu.HBM)
    g_shape = gather_sum[0].shape if n_h else ()
    return pl.pallas_call(
        body, name=name,
        out_shape=tuple(jax.ShapeDtypeStruct(p.shape[1:], p.dtype) for p in parts)
        + tuple(jax.ShapeDtypeStruct((3,) + p.shape[1:], p.dtype) for p in parts)
        + tuple([jax.ShapeDtypeStruct((N_DEV,) + g_shape, f32)] * n_h),
        in_specs=[hbm] * n + [vm] * (2 * n_h), out_specs=(vm,) * (2 * n) + (hbm,) * n_h,
        scratch_shapes=[pltpu.VMEM((4,) + p.shape[1:], p.dtype) for p in parts] * 2
        + [pltpu.SemaphoreType.DMA((4, n)), pltpu.SemaphoreType.DMA((4, n)), pltpu.SemaphoreType.DMA((4, n))]
        + ([pltpu.VMEM(g_shape, f32)] + _gather_sems(1)) * n_h,
        compiler_params=pltpu.CompilerParams(vmem_limit_bytes=V7X_VMEM_LIMIT,
                                             collective_id=GATHER_BARRIER if n_h else PAIR_BARRIER),
    )(*parts, *(gather_sum or ()))


def _chip_exchange_sems(n):
    return [pltpu.SemaphoreType.DMA((3, n)), pltpu.SemaphoreType.DMA((3, n))] if n else []


def _chip_exchange_copy(k, j, ein, eout, send_sems, recv_sems):
    me = _me()
    return _remote(ein[j].at[k - 1], eout[j].at[k - 1], send_sems.at[k - 1, j], recv_sems.at[k - 1, j],
                   (*_chip(me, k), me[2]))


def _chip_peers():
    me = _me()
    return [(*_chip(me, k), me[2]) for k in range(1, 4)]


def _chip_exchange_start(ein, eout, send_sems, recv_sems, barrier=True):
    if barrier:
        _barrier(_chip_peers())
    for k in range(1, 4):
        for j in range(len(ein)):
            _chip_exchange_copy(k, j, ein, eout, send_sems, recv_sems).start()


def _chip_exchange_finish(ein, eout, send_sems, recv_sems):
    for k in range(1, 4):
        for j in range(len(ein)):
            _chip_exchange_copy(k, j, ein, eout, send_sems, recv_sems).wait_recv()
    for k in range(1, 4):
        for j in range(len(ein)):
            _chip_exchange_copy(k, j, ein, eout, send_sems, recv_sems).wait_send()


def _split_copies(src_ref, dst_ref, sems):
    me = _me()
    return [_remote(src_ref.at[k - 1], dst_ref.at[k - 1], sems[k - 1], sems[2 + k], (*_chip(me, k), me[2]))
            for k in range(1, 4)]


def _exchange_start(others, name, barrier_id):
    def body(src_ref, land_ref, *rest):
        sems, token_ref = rest[:6], rest[8]
        _barrier(_chip_peers())
        for copy in _split_copies(src_ref, land_ref, sems):
            copy.start()
        token_ref[...] = jnp.zeros_like(token_ref)

    hbm, sem = pl.BlockSpec(memory_space=pltpu.HBM), pl.BlockSpec(memory_space=pltpu.SEMAPHORE)
    thru = pltpu.HBM(others.shape, others.dtype)
    res = pl.pallas_call(
        body, name=name,
        out_shape=(pltpu.SemaphoreType.DMA(()),) * 6 + (thru, thru, jax.ShapeDtypeStruct((8, 128), f32)),
        in_specs=(hbm, hbm), out_specs=(sem,) * 6 + (hbm, hbm, pl.BlockSpec(memory_space=pltpu.VMEM)),
        input_output_aliases={0: 6, 1: 7},
        compiler_params=pltpu.CompilerParams(has_side_effects=pltpu.SideEffectType.DATAFLOW_SIDE_EFFECTING,
                                             collective_id=barrier_id),
    )(pltpu.with_memory_space_constraint(others, pltpu.HBM),
      pltpu.with_memory_space_constraint(lax.empty(others.shape, others.dtype), pltpu.HBM))
    return res[:6], res[6], res[7], res[8]


def _exchange_wait(sems, src_thru, land_thru, after, name):
    n_after = len(after)

    def body(src_ref, land_ref, *rest):
        for copy in _split_copies(src_ref, land_ref, rest[:6]):
            copy.wait_send()
            copy.wait_recv()

    hbm, sem = pl.BlockSpec(memory_space=pltpu.HBM), pl.BlockSpec(memory_space=pltpu.SEMAPHORE)
    thru = pltpu.HBM(src_thru.shape, src_thru.dtype)
    return pl.pallas_call(
        body, name=name, out_shape=(thru, thru),
        in_specs=(hbm, hbm) + (sem,) * 6 + (pl.BlockSpec(memory_space=pl.ANY),) * n_after, out_specs=(hbm, hbm),
        input_output_aliases={0: 0, 1: 1},
        compiler_params=pltpu.CompilerParams(has_side_effects=pltpu.SideEffectType.DATAFLOW_SIDE_EFFECTING),
    )(src_thru, land_thru, *sems, *after)[1]


def _adam_update(w, g, m, v):
    m = ADAM_B1 * m + (1.0 - ADAM_B1) * g
    v = ADAM_B2 * v + (1.0 - ADAM_B2) * (g * g)
    m_hat = m / (1.0 - ADAM_B1 ** ADAM_STEP)
    v_hat = v / (1.0 - ADAM_B2 ** ADAM_STEP)
    return -ADAM_LR * (m_hat / (jnp.sqrt(v_hat) + ADAM_EPS) + ADAM_WD * w), m, v


def _sum_adamw(own, arrived, w, m, v, name, steps, after=()):
    rows = own.shape[0]
    br = rows // steps

    def body(own_ref, arr_ref, w_ref, m_ref, v_ref, *rest):
        g_out, d_out, m_out, v_out = rest[len(after):]
        g = own_ref[...].astype(f32)
        for k in range(3):
            g = g + arr_ref[k].astype(f32)
        g_out[...] = g
        d_out[...], m_out[...], v_out[...] = _adam_update(w_ref[...], g, m_ref[...], v_ref[...])

    blk = pl.BlockSpec((br, D), lambda i: (i, 0))
    return pl.pallas_call(
        body, name=name, grid=(steps,), out_shape=(jax.ShapeDtypeStruct((rows, D), f32),) * 4,
        in_specs=[blk, pl.BlockSpec((3, br, D), lambda i: (0, i, 0)), blk, blk, blk]
        + [pl.BlockSpec(memory_space=pl.ANY)] * len(after), out_specs=(blk,) * 4,
        compiler_params=pltpu.CompilerParams(dimension_semantics=("parallel",), vmem_limit_bytes=V7X_VMEM_LIMIT),
    )(own, arrived, w, m, v, *after)


def _adamw(ws, gs, ms, vs, packed, name, after=()):
    n = len(ws)
    given = [g for g in gs if not isinstance(g, int)]
    taken = [j for j in range(n) if isinstance(gs[j], int)]

    def body(packed_ref, *refs):
        w_r, m_r, v_r = (refs[k * n:(k + 1) * n] for k in range(3))
        given_r, outs = list(refs[3 * n:3 * n + len(given)]), refs[3 * n + len(given) + len(after):]
        g_o, outs = dict(zip(taken, outs[:len(taken)])), outs[len(taken):]
        d_o, m_o, v_o = (outs[k * n:(k + 1) * n] for k in range(3))
        for j in range(n):
            if j in g_o:
                (r, c), at = ws[j].shape, gs[j]
                if c == 128:
                    g = packed_ref[at:at + r, :]
                else:
                    assert r == 1
                    g = jnp.concatenate([packed_ref[at + k:at + k + 1, :] for k in range(c // 128)], axis=1)
                g_o[j][...] = g
            else:
                g = given_r.pop(0)[...]
            d_o[j][...], m_o[j][...], v_o[j][...] = _adam_update(w_r[j][...], g, m_r[j][...], v_r[j][...])

    vm = pl.BlockSpec(memory_space=pltpu.VMEM)
    shapes = tuple(jax.ShapeDtypeStruct(w.shape, f32) for w in ws)
    n_out = len(taken) + 3 * n
    return pl.pallas_call(
        body, name=name, out_shape=tuple(shapes[j] for j in taken) + shapes * 3,
        in_specs=[vm] * (1 + 3 * n + len(given)) + [pl.BlockSpec(memory_space=pl.ANY)] * len(after),
        out_specs=tuple([vm] * n_out),
        compiler_params=pltpu.CompilerParams(vmem_limit_bytes=V7X_VMEM_LIMIT),
    )(packed, *ws, *ms, *vs, *given, *after)


SMALL_FFN = (("ln1_g", D), ("ln1_b", D), ("ln2_g", D), ("ln2_b", D), ("conv_b", D_FF), ("conv_w", 3 * D_FF), ("loss", 128))
SMALL_FFN_AT = 520
SMALL_ROWS = 704


def _small_rows():
    rows, at = {"w_pool": 0, "pool_scale": GROUPS * DH}, SMALL_FFN_AT
    for k, size in SMALL_FFN:
        rows[k] = at
        at += size // 128
    return rows


def kernel(x, w_in, w_pool, pool_scale, w_out, ln1_g, ln1_b, w_up, conv_w, conv_b, w_down, ln2_g, ln2_b, loss_target, m_w_in, m_w_pool, m_pool_scale, m_w_out, m_ln1_g, m_ln1_b, m_w_up, m_conv_w, m_conv_b, m_w_down, m_ln2_g, m_ln2_b, v_w_in, v_w_pool, v_pool_scale, v_w_out, v_ln1_g, v_ln1_b, v_w_up, v_conv_w, v_conv_b, v_w_down, v_ln2_g, v_ln2_b):
    me = 4 * lax.axis_index("x") + 2 * lax.axis_index("y") + lax.axis_index("c")
    x2, tgt = x[0], loss_target[0]

    cos, sin = _rope_tables()
    dmat, qd, kd, cdec = _decay_tables(RET_TILE)

    qkv, g, oret, states, cat, pooled, xhat1, rstd1, x1b, xb, g_in, g_out, g_up, g_down, g_cw = _mix_forward(
        x2, w_in[0].T, w_out[0], cos, sin, dmat, qd, kd, cdec, w_pool[0], pool_scale, ln1_g, ln1_b,
        gather_bf16=[w_up[0].T, w_down[0]], gather=[jnp.transpose(conv_w, (1, 0, 2))])
    w_in_t = g_in.reshape(IN_W, D)
    w_out_f = g_out.reshape(D, D)
    w_up_t = g_up.reshape(2 * D_FF, D)
    w_down_f = g_down.reshape(D_FF, D)
    conv_w_f = jnp.transpose(g_cw[:, :, 0, :], (1, 0, 2)).reshape(3, D_FF)
    dz1, dz2b, du, f, loss8, d_ln2_g, d_ln2_b, d_ln1_g, d_ln1_b, d_conv_b, d_conv_w = _ffn_forward_backward(
        xhat1, rstd1, ln1_g, ln1_b, w_up_t, conv_w_f, conv_b, w_down_f, ln2_g, ln2_b, tgt)
    small_ffn = [d_ln1_g, d_ln1_b, d_ln2_g, d_ln2_b, d_conv_b, d_conv_w, loss8]

    (dw_down,) = _weight_grad(f, dz2b, "grad_w_down", tm=D_FF // 2)
    own_down, oth_down = _pair_reduce([dw_down.reshape(N_DEV, ROWS_DOWN, D)], "pair_reduce_down")
    dw_up_t, arr_down = _weight_grad(du, x1b, "grad_w_up", tm=D_FF // 2, exchange=[oth_down])
    own_up, oth_up = _pair_reduce([dw_up_t.reshape(N_DEV, ROWS_UP, D)], "pair_reduce_up")
    up_sems, up_src, up_land, up_started = _exchange_start(oth_up, "exchange_up_start", CHIP_BARRIER_SPLIT)
    dproj, grad_x, small, dw_out = _mix_backward(
        dz1, w_out_f, qkv, g, oret, states, pooled, cat, cos, sin, dmat, qd, kd, cdec, w_pool[0], pool_scale, w_in_t,
        small_ffn, after=up_started)
    own_out, own_small, oth_out, oth_small = _pair_reduce(
        [dw_out.reshape(N_DEV, ROWS_OUT, D), small.reshape(N_DEV, SMALL_ROWS // N_DEV, 128)], "pair_reduce_out")
    dw_in_t, arr_out, arr_small = _weight_grad(dproj, xb, "grad_w_in", tm=IN_W // 2, exchange=[oth_out, oth_small])
    arr_up = _exchange_wait(up_sems, up_src, up_land, [dw_in_t], "exchange_up_wait")
    own_in, oth_in, gs_small = _pair_reduce([dw_in_t.reshape(N_DEV, ROWS_IN, D)], "pair_reduce_in",
                                            gather_sum=(own_small, arr_small))
    in_sems, in_src, in_land, started = _exchange_start(oth_in, "exchange_in_start", CHIP_BARRIER)

    names = ["w_in", "w_pool", "pool_scale", "w_out", "ln1_g", "ln1_b", "w_up", "conv_w", "conv_b", "w_down",
             "ln2_g", "ln2_b"]
    w_d = dict(w_in=w_in, w_pool=w_pool, pool_scale=pool_scale, w_out=w_out, ln1_g=ln1_g, ln1_b=ln1_b, w_up=w_up,
               conv_w=conv_w, conv_b=conv_b, w_down=w_down, ln2_g=ln2_g, ln2_b=ln2_b)
    m_d = dict(w_in=m_w_in, w_pool=m_w_pool, pool_scale=m_pool_scale, w_out=m_w_out, ln1_g=m_ln1_g, ln1_b=m_ln1_b,
               w_up=m_w_up, conv_w=m_conv_w, conv_b=m_conv_b, w_down=m_w_down, ln2_g=m_ln2_g, ln2_b=m_ln2_b)
    v_d = dict(w_in=v_w_in, w_pool=v_w_pool, pool_scale=v_pool_scale, w_out=v_w_out, ln1_g=v_ln1_g, ln1_b=v_ln1_b,
               w_up=v_w_up, conv_w=v_conv_w, conv_b=v_conv_b, w_down=v_w_down, ln2_g=v_ln2_g, ln2_b=v_ln2_b)
    g_d, delta, new_m, new_v = {}, {}, {}, {}

    def big_adamw(k, own, arr, transposed, steps, after=()):
        lay = (lambda a: a[0].T) if transposed else (lambda a: a[0])
        back = (lambda a: a.T[None]) if transposed else (lambda a: a[None])
        res = _sum_adamw(own, arr, lay(w_d[k]), lay(m_d[k]), lay(v_d[k]), "adamw_" + k, steps, after)
        g_d[k], delta[k], new_m[k], new_v[k] = (back(r) for r in res)
        return res[3]

    done = [big_adamw("w_up", own_up, arr_up, True, 4, after=(started,)),
            big_adamw("w_down", own_down, arr_down, False, 2, after=(started,)),
            big_adamw("w_out", own_out, arr_out, False, 2, after=(started,))]

    gs_small, rows = gs_small.reshape(SMALL_ROWS, 128), _small_rows()
    g_conv_w = gs_small[rows["conv_w"]:rows["conv_w"] + 3 * D_FF // 128].reshape(3, D_FF)
    g_d["conv_w"] = lax.dynamic_slice(g_conv_w, (0, me * (D_FF // N_DEV)), (3, D_FF // N_DEV))[None]
    lay = lambda k, a: jnp.transpose(a, (1, 0, 2)) if k == "conv_w" else a.reshape(-1, a.shape[-1])
    back = lambda k, a: jnp.transpose(a, (1, 0, 2)) if k == "conv_w" else a.reshape(w_d[k].shape)
    group = [k for k in names if k not in ("w_in", "w_out", "w_up", "w_down")]
    packed = [k for k in group if k != "conv_w"]
    res = _adamw([lay(k, w_d[k]) for k in group], [lay(k, g_d[k]) if k == "conv_w" else rows[k] for k in group],
                 [lay(k, m_d[k]) for k in group], [lay(k, v_d[k]) for k in group], gs_small, "adamw_small",
                 after=(started,))
    for j, k in enumerate(packed):
        g_d[k] = back(k, res[j])
    for j, k in enumerate(group):
        delta[k], new_m[k], new_v[k] = (back(k, res[len(packed) + part * len(group) + j]) for part in range(3))

    arr_in = _exchange_wait(in_sems, in_src, in_land, done + [res[0]], "exchange_in_wait")
    big_adamw("w_in", own_in, arr_in, True, 4)

    loss = gs_small[rows["loss"], 0]
    return (loss, grad_x[None], *[g_d[k] for k in names], *[delta[k] for k in names], *[new_m[k] for k in names],
            *[new_v[k] for k in names])
```

```python
import math

import numpy as np
import jax
import jax.numpy as jnp
from jax import lax
from jax.experimental import pallas as pl
from jax.experimental.pallas import tpu as pltpu

f32 = jnp.float32
bf16 = jnp.bfloat16

N_DEV = 8
T = 4096
D = 1024
CHUNK = 64
MIX_TILE = 512
RET_TILE = 256
HEADS = 4
DH = 128
RW = HEADS * DH
PW = 512
GROUPS = 4
WINDOWS = (2, 4, 8, 16)
IN_W = 4 * RW + PW
D_FF = 2816
LN_EPS = 1e-5
RMS_EPS = 1e-6
ALPHA = 2.0 ** 0.25
K_SCALE = DH ** -0.5

ADAM_LR = 0.001
ADAM_B1 = 0.9
ADAM_B2 = 0.999
ADAM_EPS = 1e-08
ADAM_WD = 0.01
ADAM_STEP = 10

ROWS_IN, ROWS_OUT, ROWS_UP, ROWS_DOWN = IN_W // N_DEV, D // N_DEV, 2 * D_FF // N_DEV, D_FF // N_DEV

V7X_VMEM_LIMIT = 56 * 2 ** 20
HALO = 32

NT = (((1,), (1,)), ((), ()))
TN = (((0,), (0,)), ((), ()))
NN = (((1,), (0,)), ((), ()))


def _dot(a, b, dims=NN):
    return lax.dot_general(a, b, dims, preferred_element_type=f32)


def _const_spec(shape):
    zeros = (0,) * len(shape)
    return pl.BlockSpec(shape, lambda i: zeros, pipeline_mode=pl.Buffered(1))


def _sigmoid(x):
    return 0.5 * jnp.tanh(0.5 * x) + 0.5


def _decay_tables(tt):
    h = np.arange(HEADS, dtype=np.float64)
    log_gamma = np.log(1.0 - 2.0 ** (-5.0 - h)).astype(np.float32).astype(np.float64)[:, None, None]
    idx = np.arange(tt, dtype=np.float64)
    visible = (idx[None, :] // CHUNK) <= (idx[:, None] // CHUNK)
    mask = np.where(visible[None], np.exp(log_gamma * np.abs(idx[:, None] - idx[None, :])[None]), 0.0)
    qd = np.broadcast_to(np.exp(log_gamma * (idx[None, :, None] + 1.0)), (HEADS, tt, DH))
    kd = np.broadcast_to(np.exp(log_gamma * (tt - 1.0 - idx[None, :, None])), (HEADS, tt, DH))
    cd = np.exp(log_gamma[:, 0, 0] * tt)
    return (jnp.asarray(mask, f32), jnp.asarray(qd, f32), jnp.asarray(kd, f32), [float(c) for c in cd])


def _rope_tables():
    inv_freq = (10000.0 ** (-np.arange(0, DH, 2, dtype=np.float64) / DH)).astype(np.float32)
    ang = (np.arange(T, dtype=np.float32)[:, None] * inv_freq[None, :]).astype(np.float64)
    cos, sin = np.cos(ang), np.sin(ang)
    return (jnp.asarray(np.concatenate([cos, cos], axis=1), f32), jnp.asarray(np.concatenate([-sin, sin], axis=1), f32))


def _swap_halves(t):
    return pltpu.roll(t, DH // 2, axis=1)


def _mix_forward(x, w_in_shard, w_out_shard, cos, sin, dmat, qd, kd, cdec, w_pool, pool_scale, ln1_g, ln1_b,
                 gather_bf16, gather, tt=MIX_TILE):
    n_tiles = T // tt
    to_bf16 = [w_in_shard, w_out_shard] + list(gather_bf16)
    n_c, n_g = len(to_bf16), len(gather_bf16) + len(gather)

    def body(x_ref, cos_ref, sin_ref, dmat_ref, qd_ref, kd_ref, wpool_ref, pscale_ref, g1_ref, b1_ref, *rest):
        f32_in, plain_in, rest = rest[:n_c], rest[n_c:2 + n_g], rest[2 + n_g:]
        qkv_ref, g_ref, oret_ref, states_ref, cat_ref, pooled_ref, xhat_ref, rstd_ref, x1b_ref, xb_ref = rest[:10]
        fout, gout = rest[10:12], rest[12:12 + n_g]
        state_s, pext_s, tmp_s, wint_s, wout_s, load_sems, stage_sems, *rest = rest[12 + n_g:]
        stage_s, cast_s, sems = rest[:n_c], rest[n_c:2 * n_c], rest[2 * n_c:]
        fin, gin, fsems, gsems = cast_s[:2], tuple(cast_s[2:]) + tuple(plain_in), sems[:3], sems[3:]
        i = pl.program_id(0)

        @pl.when(i == 0)
        def _():
            stage = [pltpu.make_async_copy(src, dst, stage_sems.at[j]) for j, (src, dst) in enumerate(zip(f32_in, stage_s))]
            for cp in stage:
                cp.start()
            state_s[...] = jnp.zeros_like(state_s)
            pext_s[:, pl.ds(0, HALO), :] = jnp.zeros((GROUPS, HALO, DH), f32)

            def cast(js):
                for j in js:
                    stage[j].wait()
                    cast_s[j][...] = stage_s[j][...].astype(bf16)

            _barrier(_gather_peers())
            cast(range(2))
            _gather_start(fin, fout, *fsems)
            cast(range(2, n_c))
            _gather_forward(fin, fout, *fsems)
            _gather_start(gin, gout, *gsems)
            _gather_finish(fin, fout, *fsems)
            loads = [pltpu.make_async_copy(src.at[s], dst.at[pl.ds(s * src.shape[1], src.shape[1]), :],
                                           load_sems.at[j, s])
                     for j, (src, dst) in enumerate(((fout[0], wint_s), (fout[1], wout_s))) for s in range(N_DEV)]
            for ld in loads:
                ld.start()
            for ld in loads:
                ld.wait()

        @pl.when(i == n_tiles - 3)
        def _():
            _gather_forward(gin, gout, *gsems)

        xb = x_ref[...].astype(bf16)
        xb_ref[...] = xb
        cos_t, sin_t = cos_ref[...], sin_ref[...]
        for part in range(2):
            pr = _dot(xb, wint_s[pl.ds(part * RW, RW), :], NT)
            for h in range(HEADS):
                t = pr[:, h * DH:(h + 1) * DH]
                r = t * cos_t + _swap_halves(t) * sin_t
                if part == 1:
                    r = r * K_SCALE
                qkv_ref[:, part * RW + h * DH: part * RW + (h + 1) * DH] = r.astype(bf16)
        qkv_ref[:, 2 * RW:3 * RW] = _dot(xb, wint_s[pl.ds(2 * RW, RW), :], NT).astype(bf16)
        g_ref[...] = _dot(xb, wint_s[pl.ds(3 * RW, RW), :], NT)
        p = _dot(xb, wint_s[pl.ds(4 * RW, PW), :], NT)
        for gi in range(GROUPS):
            pext_s[gi, pl.ds(HALO, tt), :] = p[:, gi * DH:(gi + 1) * DH]

        for sub in range(tt // RET_TILE):
            rows = pl.ds(sub * RET_TILE, RET_TILE)
            for h in range(HEADS):
                q = qkv_ref[rows, h * DH:(h + 1) * DH]
                k = qkv_ref[rows, RW + h * DH: RW + (h + 1) * DH]
                v = qkv_ref[rows, 2 * RW + h * DH: 2 * RW + (h + 1) * DH]
                s = _dot(q, k, NT) * dmat_ref[h]
                st = state_s[h]
                stb = st.astype(bf16)
                states_ref[sub, h] = stb
                oret_ref[rows, h * DH:(h + 1) * DH] = (_dot(s.astype(bf16), v)
                                                      + _dot((q.astype(f32) * qd_ref[h]).astype(bf16), stb))
                state_s[h] = st * cdec[h] + _dot((k.astype(f32) * kd_ref[h]).astype(bf16), v, TN)

        for h in range(HEADS):
            sl = slice(h * DH, (h + 1) * DH)
            o = oret_ref[:, sl]
            r = lax.rsqrt(jnp.mean(o * o, axis=-1, keepdims=True) + RMS_EPS)
            gg = g_ref[:, sl]
            cat_ref[:, sl] = (o * r * (gg * _sigmoid(gg))).astype(bf16)

        pos1 = (i * tt + lax.broadcasted_iota(jnp.int32, (tt, 1), 0) + 1).astype(f32)
        for gi, w in enumerate(WINDOWS):
            sl = slice(gi * DH, (gi + 1) * DH)
            stages = int(math.log2(w))
            src = pext_s
            for s in range(stages):
                lo = HALO - 8 * (stages - 1 - s)
                n = tt + HALO - lo
                shift = 2 ** s
                val = src[gi, pl.ds(lo, n), :] + src[gi, pl.ds(lo - shift, n), :]
                if s == stages - 1:
                    wsum = val
                else:
                    tmp_s[gi, pl.ds(lo, n), :] = val
                    src = tmp_s
            p_g = pext_s[gi, pl.ds(HALO, tt), :]
            pooled = (wsum / jnp.minimum(pos1, float(w)) - p_g).astype(bf16)
            pooled_ref[:, sl] = pooled
            y = _dot(pooled, wpool_ref[gi].astype(bf16)) * pscale_ref[:, sl]
            cat_ref[:, RW + gi * DH: RW + (gi + 1) * DH] = y.astype(bf16)
        pext_s[:, pl.ds(0, HALO), :] = pext_s[:, pl.ds(tt, HALO), :]

        z = ALPHA * x_ref[...] + _dot(cat_ref[...], wout_s[...])
        mu = jnp.mean(z, axis=-1, keepdims=True)
        zc = z - mu
        rstd = lax.rsqrt(jnp.mean(zc * zc, axis=-1, keepdims=True) + LN_EPS)
        xhat = zc * rstd
        xhat_ref[...] = xhat
        rstd_ref[...] = rstd
        x1b_ref[...] = (xhat * g1_ref[...] + b1_ref[...]).astype(bf16)

        @pl.when(i == n_tiles - 1)
        def _():
            _gather_finish(gin, gout, *gsems)

    tile = lambda w: pl.BlockSpec((tt, w), lambda i: (i, 0))
    hbm = pl.BlockSpec(memory_space=pltpu.HBM)
    out_shape = (
        jax.ShapeDtypeStruct((T, 3 * RW), bf16),
        jax.ShapeDtypeStruct((T, RW), f32),
        jax.ShapeDtypeStruct((T, RW), f32),
        jax.ShapeDtypeStruct((T // RET_TILE, HEADS, DH, DH), bf16),
        jax.ShapeDtypeStruct((T, D), bf16),
        jax.ShapeDtypeStruct((T, PW), bf16),
        jax.ShapeDtypeStruct((T, D), f32),
        jax.ShapeDtypeStruct((T, 1), f32),
        jax.ShapeDtypeStruct((T, D), bf16),
        jax.ShapeDtypeStruct((T, D), bf16),
    ) + tuple(jax.ShapeDtypeStruct((N_DEV,) + b.shape, bf16) for b in to_bf16
              ) + tuple(jax.ShapeDtypeStruct((N_DEV,) + b.shape, b.dtype) for b in gather)
    return pl.pallas_call(
        body, name="mix_forward", grid=(n_tiles,), out_shape=out_shape,
        in_specs=[tile(D), tile(DH), tile(DH),
                  _const_spec((HEADS, RET_TILE, RET_TILE)), _const_spec((HEADS, RET_TILE, DH)),
                  _const_spec((HEADS, RET_TILE, DH)),
                  _const_spec((GROUPS, DH, DH)), _const_spec((1, PW)),
                  _const_spec((1, D)), _const_spec((1, D))] + [hbm] * (2 + n_g),
        out_specs=(tile(3 * RW), tile(RW), tile(RW),
                   pl.BlockSpec((tt // RET_TILE, HEADS, DH, DH), lambda i: (i, 0, 0, 0)),
                   tile(D), tile(PW), tile(D), tile(1), tile(D), tile(D)) + (hbm,) * (2 + n_g),
        scratch_shapes=[pltpu.VMEM((HEADS, DH, DH), f32), pltpu.VMEM((GROUPS, tt + HALO, DH), f32),
                        pltpu.VMEM((GROUPS, tt + HALO, DH), f32), pltpu.VMEM((IN_W, D), bf16), pltpu.VMEM((D, D), bf16),
                        pltpu.SemaphoreType.DMA((2, N_DEV)), pltpu.SemaphoreType.DMA((n_c,))]
        + [pltpu.VMEM(b.shape, f32) for b in to_bf16] + [pltpu.VMEM(b.shape, bf16) for b in to_bf16]
        + _gather_sems(2) + _gather_sems(n_g),
        compiler_params=pltpu.CompilerParams(dimension_semantics=("arbitrary",), vmem_limit_bytes=V7X_VMEM_LIMIT,
                                             collective_id=GATHER_BARRIER),
    )(x, cos, sin, dmat, qd, kd, w_pool, pool_scale, ln1_g, ln1_b, *to_bf16, *gather)


def _ffn_forward_backward(xhat1, rstd1, ln1_g, ln1_b, w_up_t, conv_w, conv_b, w_down, ln2_g, ln2_b, target,
                          tt=256):
    n_tiles = T // tt
    FH = 16
    hb = tt // FH

    def body(xhat_ref, halo_ref, rstd_ref, g1_ref, b1_ref, wupt_ref, cw_ref, cb_ref, wdown_ref, g2_ref, b2_ref, tgt_ref,
             dz1_ref, dz2b_ref, du_ref, f_ref, loss_ref, dg2_ref, db2_ref, dg1_ref, db1_ref, dcb_ref, dcw_ref,
             gext_s, val_s, dhext_s):
        i = pl.program_id(0)
        tile_idx = n_tiles - 1 - i

        def rd(ref, off):
            return jnp.concatenate([ref[k, pl.ds(off, tt), :] for k in range(D_FF // 128)], axis=1)

        def wr(ref, val):
            for k in range(D_FF // 128):
                ref[k, pl.ds(0, val.shape[0]), :] = val[:, k * 128:(k + 1) * 128]

        @pl.when(i == 0)
        def _():
            for r in (loss_ref, dg2_ref, db2_ref, dg1_ref, db1_ref, dcb_ref, dcw_ref):
                r[...] = jnp.zeros_like(r)
            dhext_s[:, pl.ds(tt, 8), :] = jnp.zeros((D_FF // 128, 8, 128), f32)

        g1, b1 = g1_ref[...], b1_ref[...]
        xhat = xhat_ref[...]
        x1 = xhat * g1 + b1
        x1b = x1.astype(bf16)
        x1h = ((halo_ref[...] * g1 + b1) * jnp.where(tile_idx == 0, 0.0, 1.0)).astype(bf16)
        x1ext = jnp.concatenate([x1h, x1b], axis=0)

        val = _dot(x1b, wupt_ref[pl.ds(0, D_FF), :], NT)
        gate_ext = _dot(x1ext, wupt_ref[pl.ds(D_FF, D_FF), :], NT)
        wr(gext_s, gate_ext)
        hh = (cb_ref[...] + cw_ref[0:1, :] * rd(gext_s, FH - 2) + cw_ref[1:2, :] * rd(gext_s, FH - 1)
              + cw_ref[2:3, :] * gate_ext[FH:])
        sg = _sigmoid(hh)
        act = hh * sg
        wr(dhext_s, act)
        val_s[...] = val * (sg + act * (1.0 - sg))
        fb = (act * val).astype(bf16)
        f_ref[...] = fb

        z = ALPHA * x1 + _dot(fb, wdown_ref[...])
        mu = jnp.mean(z, axis=-1, keepdims=True)
        zc = z - mu
        rstd2 = lax.rsqrt(jnp.mean(zc * zc, axis=-1, keepdims=True) + LN_EPS)
        xh2 = zc * rstd2
        diff = xh2 * g2_ref[...] + b2_ref[...] - tgt_ref[...]
        loss_ref[...] += 0.5 * jnp.sum(diff * diff) / D
        dy = diff * (1.0 / D)
        dg2_ref[...] += jnp.sum(dy * xh2, axis=0, keepdims=True)
        db2_ref[...] += jnp.sum(dy, axis=0, keepdims=True)
        dyg = dy * g2_ref[...]
        dz2 = rstd2 * (dyg - jnp.mean(dyg, axis=-1, keepdims=True) - xh2 * jnp.mean(dyg * xh2, axis=-1, keepdims=True))
        dz2b = dz2.astype(bf16)
        dz2b_ref[...] = dz2b

        df = _dot(dz2b, wdown_ref[...], NT)
        dval = df * rd(dhext_s, 0)
        dh = df * val_s[...]
        wr(dhext_s, dh)
        dh1, dh2, g0 = rd(dhext_s, 1), rd(dhext_s, 2), rd(gext_s, FH)
        dcb_ref[...] += jnp.sum(dh, axis=0, keepdims=True)
        dcw_ref[0:1, :] += jnp.sum(dh2 * g0, axis=0, keepdims=True)
        dcw_ref[1:2, :] += jnp.sum(dh1 * g0, axis=0, keepdims=True)
        dcw_ref[2:3, :] += jnp.sum(dh * g0, axis=0, keepdims=True)
        dgate = cw_ref[2:3, :] * dh + cw_ref[1:2, :] * dh1 + cw_ref[0:1, :] * dh2
        dvalb, dgateb = dval.astype(bf16), dgate.astype(bf16)
        du_ref[:, :D_FF] = dvalb
        du_ref[:, D_FF:] = dgateb
        dx1 = ALPHA * dz2 + _dot(dvalb, wupt_ref[pl.ds(0, D_FF), :]) + _dot(dgateb, wupt_ref[pl.ds(D_FF, D_FF), :])
        dhext_s[:, pl.ds(tt, 8), :] = dhext_s[:, pl.ds(0, 8), :]

        dg1_ref[...] += jnp.sum(dx1 * xhat, axis=0, keepdims=True)
        db1_ref[...] += jnp.sum(dx1, axis=0, keepdims=True)
        dxg = dx1 * g1
        dz1_ref[...] = rstd_ref[...] * (dxg - jnp.mean(dxg, axis=-1, keepdims=True)
                                        - xhat * jnp.mean(dxg * xhat, axis=-1, keepdims=True))

    rtile = lambda w: pl.BlockSpec((tt, w), lambda i: (n_tiles - 1 - i, 0))
    acc = lambda shape: pl.BlockSpec(shape, lambda i: (0, 0))
    out_shape = (
        jax.ShapeDtypeStruct((T, D), f32),
        jax.ShapeDtypeStruct((T, D), bf16),
        jax.ShapeDtypeStruct((T, 2 * D_FF), bf16),
        jax.ShapeDtypeStruct((T, D_FF), bf16),
        jax.ShapeDtypeStruct((8, 128), f32),
        jax.ShapeDtypeStruct((1, D), f32), jax.ShapeDtypeStruct((1, D), f32),
        jax.ShapeDtypeStruct((1, D), f32), jax.ShapeDtypeStruct((1, D), f32),
        jax.ShapeDtypeStruct((1, D_FF), f32), jax.ShapeDtypeStruct((3, D_FF), f32),
    )
    return pl.pallas_call(
        body, name="ffn_forward_backward", grid=(n_tiles,), out_shape=out_shape,
        in_specs=[rtile(D),
                  pl.BlockSpec((FH, D), lambda i: (jnp.maximum((n_tiles - 1 - i) * hb - 1, 0), 0)),
                  rtile(1), _const_spec((1, D)), _const_spec((1, D)), _const_spec((2 * D_FF, D)),
                  _const_spec((3, D_FF)), _const_spec((1, D_FF)), _const_spec((D_FF, D)),
                  _const_spec((1, D)), _const_spec((1, D)), rtile(D)],
        out_specs=(rtile(D), rtile(D), rtile(2 * D_FF), rtile(D_FF), acc((8, 128)),
                   acc((1, D)), acc((1, D)), acc((1, D)), acc((1, D)), acc((1, D_FF)), acc((3, D_FF))),
        scratch_shapes=[pltpu.VMEM((D_FF // 128, tt + FH, 128), f32), pltpu.VMEM((tt, D_FF), f32),
                        pltpu.VMEM((D_FF // 128, tt + 8, 128), f32)],
        compiler_params=pltpu.CompilerParams(dimension_semantics=("arbitrary",), vmem_limit_bytes=V7X_VMEM_LIMIT),
    )(xhat1, xhat1, rstd1, ln1_g, ln1_b, w_up_t, conv_w, conv_b, w_down, ln2_g, ln2_b, target)


def _mix_backward(dz1, w_out, qkv, g, oret, states, pooled, cat, cos, sin, dmat, qd, kd, cdec, w_pool, pool_scale, w_in_t,
                  small_ffn, after, tt=MIX_TILE):
    n_tiles = T // tt

    def body(dz1_ref, wout_ref, qkv_ref, g_ref, oret_ref, states_ref, pooled_ref, cat_ref, cos_ref, sin_ref, dmat_ref,
             qd_ref, kd_ref, wpool_ref, pscale_ref, wint_ref, *rest):
        ffn_refs, rest = rest[:len(SMALL_FFN)], rest[len(SMALL_FFN):]
        after_ref, dproj_ref, gx_ref, small_ref, dwout_ref, dstate_s, dout_s, eext_s, tmp_s, dwout_s, dpscale_s = rest
        i = pl.program_id(0)
        tile_idx = n_tiles - 1 - i

        @pl.when(i == 0)
        def _():
            dstate_s[...] = jnp.zeros_like(dstate_s)
            small_ref[...] = jnp.zeros_like(small_ref)
            dpscale_s[...] = jnp.zeros_like(dpscale_s)
            dwout_s[...] = jnp.zeros_like(dwout_s)
            eext_s[:, pl.ds(tt, HALO), :] = jnp.zeros((GROUPS, HALO, DH), f32)

        dz1 = dz1_ref[...]
        dz1b = dz1.astype(bf16)
        dcat = _dot(dz1b, wout_ref[...], NT)
        dwout_s[...] += _dot(cat_ref[...], dz1b, TN)

        pos1 = (tile_idx * tt + lax.broadcasted_iota(jnp.int32, (tt, 1), 0) + 1).astype(f32)
        for gi, w in enumerate(WINDOWS):
            sl = slice(gi * DH, (gi + 1) * DH)
            dpo = dcat[:, RW + gi * DH: RW + (gi + 1) * DH]
            pooled_g = pooled_ref[:, sl]
            wpool_g = wpool_ref[gi].astype(bf16)
            ylin = _dot(pooled_g, wpool_g)
            dpscale_s[:, sl] += jnp.sum(dpo * ylin, axis=0, keepdims=True)
            dpw = (dpo * pscale_ref[:, sl]).astype(bf16)
            small_ref[pl.ds(gi * DH, DH), :] += _dot(pooled_g, dpw, TN)
            dpooled = _dot(dpw, wpool_g, NT)
            eext_s[gi, pl.ds(0, tt), :] = dpooled / jnp.minimum(pos1, float(w))
            stages = int(math.log2(w))
            src = eext_s
            for s in range(stages):
                n = tt + 8 * (stages - 1 - s)
                shift = 2 ** s
                val = src[gi, pl.ds(0, n), :] + src[gi, pl.ds(shift, n), :]
                if s == stages - 1:
                    wsum = val
                else:
                    tmp_s[gi, pl.ds(0, n), :] = val
                    src = tmp_s
            dproj_ref[:, 4 * RW + gi * DH: 4 * RW + (gi + 1) * DH] = (wsum - dpooled).astype(bf16)
        eext_s[:, pl.ds(tt, HALO), :] = eext_s[:, pl.ds(0, HALO), :]

        for h in range(HEADS):
            sl = slice(h * DH, (h + 1) * DH)
            dr = dcat[:, sl]
            o = oret_ref[:, sl]
            r = lax.rsqrt(jnp.mean(o * o, axis=-1, keepdims=True) + RMS_EPS)
            rn = o * r
            gg = g_ref[:, sl]
            sg = _sigmoid(gg)
            dproj_ref[:, 3 * RW + h * DH: 3 * RW + (h + 1) * DH] = (dr * rn * (sg * (1.0 + gg * (1.0 - sg)))).astype(bf16)
            drn = dr * (gg * sg)
            dout_s[:, sl] = (r * (drn - rn * jnp.mean(drn * rn, axis=-1, keepdims=True))).astype(bf16)

        for sub in reversed(range(tt // RET_TILE)):
            rows = pl.ds(sub * RET_TILE, RET_TILE)
            cos_t, sin_t = cos_ref[rows, :], sin_ref[rows, :]
            for h in range(HEADS):
                q = qkv_ref[rows, h * DH:(h + 1) * DH]
                k = qkv_ref[rows, RW + h * DH: RW + (h + 1) * DH]
                v = qkv_ref[rows, 2 * RW + h * DH: 2 * RW + (h + 1) * DH]
                do = dout_s[rows, h * DH:(h + 1) * DH]
                stb = states_ref[sub, h]
                dst = dstate_s[h]
                dstb = dst.astype(bf16)
                sb = (_dot(q, k, NT) * dmat_ref[h]).astype(bf16)
                dsb = (_dot(do, v, NT) * dmat_ref[h]).astype(bf16)
                dq = _dot(dsb, k) + _dot(do, stb, NT) * qd_ref[h]
                dk = _dot(dsb, q, TN) + _dot(v, dstb, NT) * kd_ref[h]
                dv = _dot(sb, do, TN) + _dot((k.astype(f32) * kd_ref[h]).astype(bf16), dstb)
                dstate_s[h] = dst * cdec[h] + _dot((q.astype(f32) * qd_ref[h]).astype(bf16), do, TN)
                dproj_ref[rows, h * DH:(h + 1) * DH] = (dq * cos_t - _swap_halves(dq) * sin_t).astype(bf16)
                dproj_ref[rows, RW + h * DH: RW + (h + 1) * DH] = (
                    (dk * cos_t - _swap_halves(dk) * sin_t) * K_SCALE).astype(bf16)
                dproj_ref[rows, 2 * RW + h * DH: 2 * RW + (h + 1) * DH] = dv.astype(bf16)

        gx_ref[...] = ALPHA * dz1 + _dot(dproj_ref[...], wint_ref[...])

        @pl.when(i == n_tiles - 1)
        def _():
            dwout_ref[...] = dwout_s[...].astype(bf16)
            at = GROUPS * DH
            for ref, size in [(dpscale_s, PW)] + [(ref, size) for ref, (_, size) in zip(ffn_refs, SMALL_FFN)]:
                for j in range(size // 128):
                    r, k = divmod(j, ref.shape[1] // 128)
                    small_ref[at + j: at + j + 1, :] = ref[r:r + 1, k * 128:(k + 1) * 128]
                at = SMALL_FFN_AT if ref is dpscale_s else at + size // 128

    rtile = lambda w: pl.BlockSpec((tt, w), lambda i: (n_tiles - 1 - i, 0))
    out_shape = (
        jax.ShapeDtypeStruct((T, IN_W), bf16),
        jax.ShapeDtypeStruct((T, D), f32),
        jax.ShapeDtypeStruct((SMALL_ROWS, 128), f32),
        jax.ShapeDtypeStruct((D, D), bf16),
    )
    return pl.pallas_call(
        body, name="mix_backward", grid=(n_tiles,), out_shape=out_shape,
        in_specs=[rtile(D), _const_spec((D, D)), rtile(3 * RW), rtile(RW), rtile(RW),
                  pl.BlockSpec((tt // RET_TILE, HEADS, DH, DH), lambda i: (n_tiles - 1 - i, 0, 0, 0)),
                  rtile(PW), rtile(D), rtile(DH), rtile(DH),
                  _const_spec((HEADS, RET_TILE, RET_TILE)), _const_spec((HEADS, RET_TILE, DH)),
                  _const_spec((HEADS, RET_TILE, DH)),
                  _const_spec((GROUPS, DH, DH)), _const_spec((1, PW)), _const_spec((IN_W, D)),
                  *[_const_spec(a.shape) for a in small_ffn], pl.BlockSpec(memory_space=pl.ANY)],
        out_specs=(rtile(IN_W), rtile(D), pl.BlockSpec((SMALL_ROWS, 128), lambda i: (0, 0)),
                   pl.BlockSpec((D, D), lambda i: (0, 0), pipeline_mode=pl.Buffered(1))),
        scratch_shapes=[pltpu.VMEM((HEADS, DH, DH), f32), pltpu.VMEM((tt, RW), bf16),
                        pltpu.VMEM((GROUPS, tt + HALO, DH), f32), pltpu.VMEM((GROUPS, tt + HALO, DH), f32),
                        pltpu.VMEM((D, D), f32), pltpu.VMEM((1, PW), f32)],
        compiler_params=pltpu.CompilerParams(dimension_semantics=("arbitrary",), vmem_limit_bytes=V7X_VMEM_LIMIT),
    )(dz1, w_out, qkv, g, oret, states, pooled, cat, cos, sin, dmat, qd, kd, w_pool, pool_scale, w_in_t, *small_ffn,
      after)


def _weight_grad(a, b, name, tm, exchange=()):
    m = a.shape[1]
    n_m, n_e = m // tm, len(exchange)

    def body(a_ref, b_ref, *rest):
        ein, o_ref, eout, sems = rest[:n_e], rest[n_e], rest[n_e + 1:2 * n_e + 1], rest[2 * n_e + 1:]
        i = pl.program_id(0)

        if n_e:
            @pl.when(i == 0)
            def _():
                _chip_exchange_start(ein, eout, *sems)

        o_ref[...] = _dot(a_ref[...], b_ref[...].astype(bf16), TN).astype(bf16)

        if n_e:
            @pl.when(i == n_m - 1)
            def _():
                _chip_exchange_finish(ein, eout, *sems)

    hbm = pl.BlockSpec(memory_space=pltpu.HBM)
    return pl.pallas_call(
        body, name=name, grid=(n_m,),
        out_shape=(jax.ShapeDtypeStruct((m, D), bf16),) + tuple(jax.ShapeDtypeStruct(e.shape, e.dtype) for e in exchange),
        in_specs=[pl.BlockSpec((T, tm), lambda i: (0, i)),
                  pl.BlockSpec((T, D), lambda i: (0, 0), pipeline_mode=pl.Buffered(1))] + [hbm] * n_e,
        out_specs=(pl.BlockSpec((tm, D), lambda i: (i, 0)),) + (hbm,) * n_e,
        scratch_shapes=_chip_exchange_sems(n_e),
        compiler_params=pltpu.CompilerParams(dimension_semantics=("arbitrary",), vmem_limit_bytes=V7X_VMEM_LIMIT,
                                             collective_id=CHIP_BARRIER if n_e else None),
    )(a, b, *exchange)


CHIP_FLIPS = ((1, 0), (0, 1), (1, 1))
PAIR_BARRIER, CHIP_BARRIER, GATHER_BARRIER, CHIP_BARRIER_SPLIT = 0, 1, 2, 3


def _barrier(peers):
    sem = pltpu.get_barrier_semaphore()
    for peer in peers:
        pl.semaphore_signal(sem, inc=1, device_id=peer, device_id_type=pl.DeviceIdType.MESH)
    pl.semaphore_wait(sem, len(peers))


def _me():
    return lax.axis_index("x"), lax.axis_index("y"), lax.axis_index("c")


def _chip(me, k):
    x, y, _ = me
    if k == 0:
        return x, y
    fx, fy = CHIP_FLIPS[k - 1]
    return (1 - x if fx else x), (1 - y if fy else y)


def _slot(x, y, c):
    return 4 * x + 2 * y + c


def _remote(src, dst, send_sem, recv_sem, to):
    return pltpu.make_async_remote_copy(src_ref=src, dst_ref=dst, send_sem=send_sem, recv_sem=recv_sem,
                                        device_id=to, device_id_type=pl.DeviceIdType.MESH)


def _gather_sems(n):
    return [pltpu.SemaphoreType.DMA((7, n)), pltpu.SemaphoreType.DMA((7, n)), pltpu.SemaphoreType.DMA((n,))] if n else []


def _gather_copy(k, j, gin, gout, send_sems, recv_sems, sending):
    x, y, c = _me()
    sibling, x_chip, y_chip, d_chip = (x, y, 1 - c), (1 - x, y), (x, 1 - y), (1 - x, 1 - y)
    south = c == 0
    passed_on = (jnp.where(south, 1 - x, x), jnp.where(south, y, 1 - y), c)
    src, to = gin[j], sibling
    if sending:
        block = {0: (x, y, c), 1: (x, y, c), 2: (x, y, c), 3: passed_on, 4: (*x_chip, c), 5: (*y_chip, c), 6: (*d_chip, c)}[k]
        to = {1: (*x_chip, c), 2: (*y_chip, c), 3: (jnp.where(south, x, 1 - x), jnp.where(south, 1 - y, y), c)}.get(k, sibling)
        if k >= 3:
            src = gout[j].at[_slot(*block)]
    else:
        block = {0: sibling, 1: (*x_chip, c), 2: (*y_chip, c), 3: (*d_chip, c), 4: (*x_chip, 1 - c), 5: (*y_chip, 1 - c),
                 6: (*d_chip, 1 - c)}[k]
    return _remote(src, gout[j].at[_slot(*block)], send_sems.at[k, j], recv_sems.at[k, j], to)


def _gather_do(ks, action, gin, gout, send_sems, recv_sems):
    for k in ks:
        for j in range(len(gin)):
            cp = _gather_copy(k, j, gin, gout, send_sems, recv_sems, action != "wait_recv")
            getattr(cp, action)()


def _gather_peers():
    x, y, c = _me()
    return [(x, y, 1 - c), (1 - x, y, c), (x, 1 - y, c)]


def _gather_start(gin, gout, send_sems, recv_sems, local_sems):
    for j in range(len(gin)):
        pltpu.make_async_copy(gin[j], gout[j].at[_slot(*_me())], local_sems.at[j]).start()
    _gather_do((0, 1, 2), "start", gin, gout, send_sems, recv_sems)


def _gather_forward(gin, gout, send_sems, recv_sems, local_sems):
    _gather_do((1, 2), "wait_recv", gin, gout, send_sems, recv_sems)
    _gather_do((3, 4, 5), "start", gin, gout, send_sems, recv_sems)


def _gather_finish(gin, gout, send_sems, recv_sems, local_sems):
    _gather_do((3,), "wait_recv", gin, gout, send_sems, recv_sems)
    _gather_do((6,), "start", gin, gout, send_sems, recv_sems)
    _gather_do((0, 4, 5, 6), "wait_recv", gin, gout, send_sems, recv_sems)
    _gather_do(range(7), "wait_send", gin, gout, send_sems, recv_sems)
    for j in range(len(gin)):
        pltpu.make_async_copy(gin[j], gout[j].at[_slot(*_me())], local_sems.at[j]).wait()


def _pair_reduce(parts, name, gather_sum=None):
    n = len(parts)
    n_h = 0 if gather_sum is None else 1

    def body(*refs):
        ins, g_terms, refs = refs[:n], refs[n:n + 2 * n_h], refs[n + 2 * n_h:]
        own, others, g_out, refs = refs[:n], refs[n:2 * n], refs[2 * n:2 * n + n_h], refs[2 * n + n_h:]
        landing, mine, (send_sems, recv_sems, local_sems), g_scratch = refs[:n], refs[n:2 * n], refs[2 * n:2 * n + 3], refs[2 * n + 3:]
        me = _me()
        x, y, c = me
        sibling = (x, y, 1 - c)
        _barrier(_gather_peers() if n_h else [sibling])
        if n_h:
            piece_s, g_sems = g_scratch[0], g_scratch[1:]
            acc = g_terms[0][...].astype(f32)
            for k in range(3):
                acc = acc + g_terms[1][k].astype(f32)
            piece_s[...] = acc
            _gather_start([piece_s], g_out, *g_sems)
        sends, loads = [], []
        for k in range(4):
            for j in range(n):
                cp = _remote(ins[j].at[_slot(*_chip(me, k), 1 - c)], landing[j].at[k], send_sems.at[k, j],
                             recv_sems.at[k, j], sibling)
                cp.start()
                sends.append(cp)
                ld = pltpu.make_async_copy(ins[j].at[_slot(*_chip(me, k), c)], mine[j].at[k], local_sems.at[k, j])
                ld.start()
                loads.append(ld)
        if n_h:
            _gather_forward([piece_s], g_out, *g_sems)
        stores = []
        for k in range(4):
            for j in range(n):
                loads[k * n + j].wait()
                _remote(ins[j].at[0], landing[j].at[k], send_sems.at[k, j], recv_sems.at[k, j], sibling).wait_recv()
                mine[j][k] = (mine[j][k].astype(f32) + landing[j][k].astype(f32)).astype(mine[j].dtype)
                st = pltpu.make_async_copy(mine[j].at[k], own[j] if k == 0 else others[j].at[k - 1], local_sems.at[k, j])
                st.start()
                stores.append(st)
        for cp in sends:
            cp.wait_send()
        for st in stores:
            st.wait()
        if n_h:
            _gather_finish([piece_s], g_out, *g_sems)

    vm, hbm = pl.BlockSpec(memory_space=pltpu.VMEM), pl.BlockSpec(memory_space=pltpu.HBM)
    g_shape = gather_sum[0].shape if n_h else ()
    return pl.pallas_call(
        body, name=name,
        out_shape=tuple(jax.ShapeDtypeStruct(p.shape[1:], p.dtype) for p in parts)
        + tuple(jax.ShapeDtypeStruct((3,) + p.shape[1:], p.dtype) for p in parts)
        + tuple([jax.ShapeDtypeStruct((N_DEV,) + g_shape, f32)] * n_h),
        in_specs=[hbm] * n + [vm] * (2 * n_h), out_specs=(hbm,) * (2 * n + n_h),
        scratch_shapes=[pltpu.VMEM((4,) + p.shape[1:], p.dtype) for p in parts] * 2
        + [pltpu.SemaphoreType.DMA((4, n)), pltpu.SemaphoreType.DMA((4, n)), pltpu.SemaphoreType.DMA((4, n))]
        + ([pltpu.VMEM(g_shape, f32)] + _gather_sems(1)) * n_h,
        compiler_params=pltpu.CompilerParams(vmem_limit_bytes=V7X_VMEM_LIMIT,
                                             collective_id=GATHER_BARRIER if n_h else PAIR_BARRIER),
    )(*parts, *(gather_sum or ()))


def _chip_exchange_sems(n):
    return [pltpu.SemaphoreType.DMA((3, n)), pltpu.SemaphoreType.DMA((3, n))] if n else []


def _chip_exchange_copy(k, j, ein, eout, send_sems, recv_sems):
    me = _me()
    return _remote(ein[j].at[k - 1], eout[j].at[k - 1], send_sems.at[k - 1, j], recv_sems.at[k - 1, j],
                   (*_chip(me, k), me[2]))


def _chip_peers():
    me = _me()
    return [(*_chip(me, k), me[2]) for k in range(1, 4)]


def _chip_exchange_start(ein, eout, send_sems, recv_sems, barrier=True):
    if barrier:
        _barrier(_chip_peers())
    for k in range(1, 4):
        for j in range(len(ein)):
            _chip_exchange_copy(k, j, ein, eout, send_sems, recv_sems).start()


def _chip_exchange_finish(ein, eout, send_sems, recv_sems):
    for k in range(1, 4):
        for j in range(len(ein)):
            _chip_exchange_copy(k, j, ein, eout, send_sems, recv_sems).wait_recv()
    for k in range(1, 4):
        for j in range(len(ein)):
            _chip_exchange_copy(k, j, ein, eout, send_sems, recv_sems).wait_send()


def _split_copies(src_ref, dst_ref, sems):
    me = _me()
    return [_remote(src_ref.at[k - 1], dst_ref.at[k - 1], sems[k - 1], sems[2 + k], (*_chip(me, k), me[2]))
            for k in range(1, 4)]


def _exchange_start(others, name, barrier_id):
    def body(src_ref, land_ref, *rest):
        sems, token_ref = rest[:6], rest[8]
        _barrier(_chip_peers())
        for copy in _split_copies(src_ref, land_ref, sems):
            copy.start()
        token_ref[...] = jnp.zeros_like(token_ref)

    hbm, sem = pl.BlockSpec(memory_space=pltpu.HBM), pl.BlockSpec(memory_space=pltpu.SEMAPHORE)
    thru = pltpu.HBM(others.shape, others.dtype)
    res = pl.pallas_call(
        body, name=name,
        out_shape=(pltpu.SemaphoreType.DMA(()),) * 6 + (thru, thru, jax.ShapeDtypeStruct((8, 128), f32)),
        in_specs=(hbm, hbm), out_specs=(sem,) * 6 + (hbm, hbm, pl.BlockSpec(memory_space=pltpu.VMEM)),
        input_output_aliases={0: 6, 1: 7},
        compiler_params=pltpu.CompilerParams(has_side_effects=pltpu.SideEffectType.DATAFLOW_SIDE_EFFECTING,
                                             collective_id=barrier_id),
    )(pltpu.with_memory_space_constraint(others, pltpu.HBM),
      pltpu.with_memory_space_constraint(lax.empty(others.shape, others.dtype), pltpu.HBM))
    return res[:6], res[6], res[7], res[8]


def _exchange_wait(sems, src_thru, land_thru, after, name):
    n_after = len(after)

    def body(src_ref, land_ref, *rest):
        for copy in _split_copies(src_ref, land_ref, rest[:6]):
            copy.wait_send()
            copy.wait_recv()

    hbm, sem = pl.BlockSpec(memory_space=pltpu.HBM), pl.BlockSpec(memory_space=pltpu.SEMAPHORE)
    thru = pltpu.HBM(src_thru.shape, src_thru.dtype)
    return pl.pallas_call(
        body, name=name, out_shape=(thru, thru),
        in_specs=(hbm, hbm) + (sem,) * 6 + (pl.BlockSpec(memory_space=pl.ANY),) * n_after, out_specs=(hbm, hbm),
        input_output_aliases={0: 0, 1: 1},
        compiler_params=pltpu.CompilerParams(has_side_effects=pltpu.SideEffectType.DATAFLOW_SIDE_EFFECTING),
    )(src_thru, land_thru, *sems, *after)[1]


def _adam_update(w, g, m, v):
    m = ADAM_B1 * m + (1.0 - ADAM_B1) * g
    v = ADAM_B2 * v + (1.0 - ADAM_B2) * (g * g)
    m_hat = m / (1.0 - ADAM_B1 ** ADAM_STEP)
    v_hat = v / (1.0 - ADAM_B2 ** ADAM_STEP)
    return -ADAM_LR * (m_hat / (jnp.sqrt(v_hat) + ADAM_EPS) + ADAM_WD * w), m, v


def _sum_adamw(own, arrived, w, m, v, name, steps, after=()):
    rows = own.shape[0]
    br = rows // steps

    def body(own_ref, arr_ref, w_ref, m_ref, v_ref, *rest):
        g_out, d_out, m_out, v_out = rest[len(after):]
        g = own_ref[...].astype(f32)
        for k in range(3):
            g = g + arr_ref[k].astype(f32)
        g_out[...] = g
        d_out[...], m_out[...], v_out[...] = _adam_update(w_ref[...], g, m_ref[...], v_ref[...])

    blk = pl.BlockSpec((br, D), lambda i: (i, 0))
    return pl.pallas_call(
        body, name=name, grid=(steps,), out_shape=(jax.ShapeDtypeStruct((rows, D), f32),) * 4,
        in_specs=[blk, pl.BlockSpec((3, br, D), lambda i: (0, i, 0)), blk, blk, blk]
        + [pl.BlockSpec(memory_space=pl.ANY)] * len(after), out_specs=(blk,) * 4,
        compiler_params=pltpu.CompilerParams(dimension_semantics=("parallel",), vmem_limit_bytes=V7X_VMEM_LIMIT),
    )(own, arrived, w, m, v, *after)


def _adamw(ws, gs, ms, vs, packed, name, after=()):
    n = len(ws)
    given = [g for g in gs if not isinstance(g, int)]
    taken = [j for j in range(n) if isinstance(gs[j], int)]

    def body(packed_ref, *refs):
        w_r, m_r, v_r = (refs[k * n:(k + 1) * n] for k in range(3))
        given_r, outs = list(refs[3 * n:3 * n + len(given)]), refs[3 * n + len(given) + len(after):]
        g_o, outs = dict(zip(taken, outs[:len(taken)])), outs[len(taken):]
        d_o, m_o, v_o = (outs[k * n:(k + 1) * n] for k in range(3))
        for j in range(n):
            if j in g_o:
                (r, c), at = ws[j].shape, gs[j]
                if c == 128:
                    g = packed_ref[at:at + r, :]
                else:
                    assert r == 1
                    g = jnp.concatenate([packed_ref[at + k:at + k + 1, :] for k in range(c // 128)], axis=1)
                g_o[j][...] = g
            else:
                g = given_r.pop(0)[...]
            d_o[j][...], m_o[j][...], v_o[j][...] = _adam_update(w_r[j][...], g, m_r[j][...], v_r[j][...])

    vm = pl.BlockSpec(memory_space=pltpu.VMEM)
    shapes = tuple(jax.ShapeDtypeStruct(w.shape, f32) for w in ws)
    n_out = len(taken) + 3 * n
    return pl.pallas_call(
        body, name=name, out_shape=tuple(shapes[j] for j in taken) + shapes * 3,
        in_specs=[vm] * (1 + 3 * n + len(given)) + [pl.BlockSpec(memory_space=pl.ANY)] * len(after),
        out_specs=tuple([vm] * n_out),
        compiler_params=pltpu.CompilerParams(vmem_limit_bytes=V7X_VMEM_LIMIT),
    )(packed, *ws, *ms, *vs, *given, *after)


SMALL_FFN = (("ln1_g", D), ("ln1_b", D), ("ln2_g", D), ("ln2_b", D), ("conv_b", D_FF), ("conv_w", 3 * D_FF), ("loss", 128))
SMALL_FFN_AT = 520
SMALL_ROWS = 704


def _small_rows():
    rows, at = {"w_pool": 0, "pool_scale": GROUPS * DH}, SMALL_FFN_AT
    for k, size in SMALL_FFN:
        rows[k] = at
        at += size // 128
    return rows


def kernel(x, w_in, w_pool, pool_scale, w_out, ln1_g, ln1_b, w_up, conv_w, conv_b, w_down, ln2_g, ln2_b, loss_target, m_w_in, m_w_pool, m_pool_scale, m_w_out, m_ln1_g, m_ln1_b, m_w_up, m_conv_w, m_conv_b, m_w_down, m_ln2_g, m_ln2_b, v_w_in, v_w_pool, v_pool_scale, v_w_out, v_ln1_g, v_ln1_b, v_w_up, v_conv_w, v_conv_b, v_w_down, v_ln2_g, v_ln2_b):
    me = 4 * lax.axis_index("x") + 2 * lax.axis_index("y") + lax.axis_index("c")
    x2, tgt = x[0], loss_target[0]

    cos, sin = _rope_tables()
    dmat, qd, kd, cdec = _decay_tables(RET_TILE)

    qkv, g, oret, states, cat, pooled, xhat1, rstd1, x1b, xb, g_in, g_out, g_up, g_down, g_cw = _mix_forward(
        x2, w_in[0].T, w_out[0], cos, sin, dmat, qd, kd, cdec, w_pool[0], pool_scale, ln1_g, ln1_b,
        gather_bf16=[w_up[0].T, w_down[0]], gather=[jnp.transpose(conv_w, (1, 0, 2))])
    w_in_t = g_in.reshape(IN_W, D)
    w_out_f = g_out.reshape(D, D)
    w_up_t = g_up.reshape(2 * D_FF, D)
    w_down_f = g_down.reshape(D_FF, D)
    conv_w_f = jnp.transpose(g_cw[:, :, 0, :], (1, 0, 2)).reshape(3, D_FF)
    dz1, dz2b, du, f, loss8, d_ln2_g, d_ln2_b, d_ln1_g, d_ln1_b, d_conv_b, d_conv_w = _ffn_forward_backward(
        xhat1, rstd1, ln1_g, ln1_b, w_up_t, conv_w_f, conv_b, w_down_f, ln2_g, ln2_b, tgt)
    small_ffn = [d_ln1_g, d_ln1_b, d_ln2_g, d_ln2_b, d_conv_b, d_conv_w, loss8]

    (dw_down,) = _weight_grad(f, dz2b, "grad_w_down", tm=D_FF // 2)
    own_down, oth_down = _pair_reduce([dw_down.reshape(N_DEV, ROWS_DOWN, D)], "pair_reduce_down")
    dw_up_t, arr_down = _weight_grad(du, x1b, "grad_w_up", tm=D_FF // 2, exchange=[oth_down])
    own_up, oth_up = _pair_reduce([dw_up_t.reshape(N_DEV, ROWS_UP, D)], "pair_reduce_up")
    up_sems, up_src, up_land, up_started = _exchange_start(oth_up, "exchange_up_start", CHIP_BARRIER_SPLIT)
    dproj, grad_x, small, dw_out = _mix_backward(
        dz1, w_out_f, qkv, g, oret, states, pooled, cat, cos, sin, dmat, qd, kd, cdec, w_pool[0], pool_scale, w_in_t,
        small_ffn, after=up_started)
    own_out, own_small, oth_out, oth_small = _pair_reduce(
        [dw_out.reshape(N_DEV, ROWS_OUT, D), small.reshape(N_DEV, SMALL_ROWS // N_DEV, 128)], "pair_reduce_out")
    dw_in_t, arr_out, arr_small = _weight_grad(dproj, xb, "grad_w_in", tm=IN_W // 2, exchange=[oth_out, oth_small])
    arr_up = _exchange_wait(up_sems, up_src, up_land, [dw_in_t], "exchange_up_wait")
    own_in, oth_in, gs_small = _pair_reduce([dw_in_t.reshape(N_DEV, ROWS_IN, D)], "pair_reduce_in",
                                            gather_sum=(own_small, arr_small))
    in_sems, in_src, in_land, started = _exchange_start(oth_in, "exchange_in_start", CHIP_BARRIER)

    names = ["w_in", "w_pool", "pool_scale", "w_out", "ln1_g", "ln1_b", "w_up", "conv_w", "conv_b", "w_down",
             "ln2_g", "ln2_b"]
    w_d = dict(w_in=w_in, w_pool=w_pool, pool_scale=pool_scale, w_out=w_out, ln1_g=ln1_g, ln1_b=ln1_b, w_up=w_up,
               conv_w=conv_w, conv_b=conv_b, w_down=w_down, ln2_g=ln2_g, ln2_b=ln2_b)
    m_d = dict(w_in=m_w_in, w_pool=m_w_pool, pool_scale=m_pool_scale, w_out=m_w_out, ln1_g=m_ln1_g, ln1_b=m_ln1_b,
               w_up=m_w_up, conv_w=m_conv_w, conv_b=m_conv_b, w_down=m_w_down, ln2_g=m_ln2_g, ln2_b=m_ln2_b)
    v_d = dict(w_in=v_w_in, w_pool=v_w_pool, pool_scale=v_pool_scale, w_out=v_w_out, ln1_g=v_ln1_g, ln1_b=v_ln1_b,
               w_up=v_w_up, conv_w=v_conv_w, conv_b=v_conv_b, w_down=v_w_down, ln2_g=v_ln2_g, ln2_b=v_ln2_b)
    g_d, delta, new_m, new_v = {}, {}, {}, {}

    def big_adamw(k, own, arr, transposed, steps, after=()):
        lay = (lambda a: a[0].T) if transposed else (lambda a: a[0])
        back = (lambda a: a.T[None]) if transposed else (lambda a: a[None])
        res = _sum_adamw(own, arr, lay(w_d[k]), lay(m_d[k]), lay(v_d[k]), "adamw_" + k, steps, after)
        g_d[k], delta[k], new_m[k], new_v[k] = (back(r) for r in res)
        return res[3]

    done = [big_adamw("w_up", own_up, arr_up, True, 4, after=(started,)),
            big_adamw("w_down", own_down, arr_down, False, 2, after=(started,)),
            big_adamw("w_out", own_out, arr_out, False, 2, after=(started,))]

    gs_small, rows = gs_small.reshape(SMALL_ROWS, 128), _small_rows()
    g_conv_w = gs_small[rows["conv_w"]:rows["conv_w"] + 3 * D_FF // 128].reshape(3, D_FF)
    g_d["conv_w"] = lax.dynamic_slice(g_conv_w, (0, me * (D_FF // N_DEV)), (3, D_FF // N_DEV))[None]
    lay = lambda k, a: jnp.transpose(a, (1, 0, 2)) if k == "conv_w" else a.reshape(-1, a.shape[-1])
    back = lambda k, a: jnp.transpose(a, (1, 0, 2)) if k == "conv_w" else a.reshape(w_d[k].shape)
    group = [k for k in names if k not in ("w_in", "w_out", "w_up", "w_down")]
    packed = [k for k in group if k != "conv_w"]
    res = _adamw([lay(k, w_d[k]) for k in group], [lay(k, g_d[k]) if k == "conv_w" else rows[k] for k in group],
                 [lay(k, m_d[k]) for k in group], [lay(k, v_d[k]) for k in group], gs_small, "adamw_small",
                 after=(started,))
    for j, k in enumerate(packed):
        g_d[k] = back(k, res[j])
    for j, k in enumerate(group):
        delta[k], new_m[k], new_v[k] = (back(k, res[len(packed) + part * len(group) + j]) for part in range(3))

    arr_in = _exchange_wait(in_sems, in_src, in_land, done + [res[0]], "exchange_in_wait")
    big_adamw("w_in", own_in, arr_in, True, 4)

    loss = gs_small[rows["loss"], 0]
    return (loss, grad_x[None], *[g_d[k] for k in names], *[delta[k] for k in names], *[new_m[k] for k in names],
            *[new_v[k] for k in names])
```

```python
import math

import numpy as np
import jax
import jax.numpy as jnp
from jax import lax
from jax.experimental import pallas as pl
from jax.experimental.pallas import tpu as pltpu

f32 = jnp.float32
bf16 = jnp.bfloat16

N_DEV = 8
T = 4096
D = 1024
CHUNK = 64
MIX_TILE = 512
RET_TILE = 256
HEADS = 4
DH = 128
RW = HEADS * DH
PW = 512
GROUPS = 4
WINDOWS = (2, 4, 8, 16)
IN_W = 4 * RW + PW
D_FF = 2816
LN_EPS = 1e-5
RMS_EPS = 1e-6
ALPHA = 2.0 ** 0.25
K_SCALE = DH ** -0.5

ADAM_LR = 0.001
ADAM_B1 = 0.9
ADAM_B2 = 0.999
ADAM_EPS = 1e-08
ADAM_WD = 0.01
ADAM_STEP = 10

ROWS_IN, ROWS_OUT, ROWS_UP, ROWS_DOWN = IN_W // N_DEV, D // N_DEV, 2 * D_FF // N_DEV, D_FF // N_DEV

V7X_VMEM_LIMIT = 56 * 2 ** 20
HALO = 32

NT = (((1,), (1,)), ((), ()))
TN = (((0,), (0,)), ((), ()))
NN = (((1,), (0,)), ((), ()))


def _dot(a, b, dims=NN):
    return lax.dot_general(a, b, dims, preferred_element_type=f32)


def _const_spec(shape):
    zeros = (0,) * len(shape)
    return pl.BlockSpec(shape, lambda i: zeros, pipeline_mode=pl.Buffered(1))


def _sigmoid(x):
    return 0.5 * jnp.tanh(0.5 * x) + 0.5


def _decay_tables(tt):
    h = np.arange(HEADS, dtype=np.float64)
    log_gamma = np.log(1.0 - 2.0 ** (-5.0 - h)).astype(np.float32).astype(np.float64)[:, None, None]
    idx = np.arange(tt, dtype=np.float64)
    visible = (idx[None, :] // CHUNK) <= (idx[:, None] // CHUNK)
    mask = np.where(visible[None], np.exp(log_gamma * np.abs(idx[:, None] - idx[None, :])[None]), 0.0)
    qd = np.broadcast_to(np.exp(log_gamma * (idx[None, :, None] + 1.0)), (HEADS, tt, DH))
    kd = np.broadcast_to(np.exp(log_gamma * (tt - 1.0 - idx[None, :, None])), (HEADS, tt, DH))
    cd = np.exp(log_gamma[:, 0, 0] * tt)
    return (jnp.asarray(mask, f32), jnp.asarray(qd, f32), jnp.asarray(kd, f32), [float(c) for c in cd])


def _rope_tables():
    inv_freq = (10000.0 ** (-np.arange(0, DH, 2, dtype=np.float64) / DH)).astype(np.float32)
    ang = (np.arange(T, dtype=np.float32)[:, None] * inv_freq[None, :]).astype(np.float64)
    cos, sin = np.cos(ang), np.sin(ang)
    return (jnp.asarray(np.concatenate([cos, cos], axis=1), f32), jnp.asarray(np.concatenate([-sin, sin], axis=1), f32))


def _swap_halves(t):
    return pltpu.roll(t, DH // 2, axis=1)


def _mix_forward(x, w_in_shard, w_out_shard, cos, sin, dmat, qd, kd, cdec, w_pool, pool_scale, ln1_g, ln1_b,
                 gather_bf16, gather, tt=MIX_TILE):
    n_tiles = T // tt
    to_bf16 = [w_in_shard, w_out_shard] + list(gather_bf16)
    n_c, n_g = len(to_bf16), len(gather_bf16) + len(gather)

    def body(x_ref, cos_ref, sin_ref, dmat_ref, qd_ref, kd_ref, wpool_ref, pscale_ref, g1_ref, b1_ref, *rest):
        f32_in, plain_in, rest = rest[:n_c], rest[n_c:2 + n_g], rest[2 + n_g:]
        qkv_ref, g_ref, oret_ref, states_ref, cat_ref, pooled_ref, xhat_ref, rstd_ref, x1b_ref, xb_ref = rest[:10]
        fout, gout = rest[10:12], rest[12:12 + n_g]
        state_s, pext_s, tmp_s, wint_s, wout_s, load_sems, stage_sems, *rest = rest[12 + n_g:]
        stage_s, cast_s, sems = rest[:n_c], rest[n_c:2 * n_c], rest[2 * n_c:]
        fin, gin, fsems, gsems = cast_s[:2], tuple(cast_s[2:]) + tuple(plain_in), sems[:3], sems[3:]
        i = pl.program_id(0)

        @pl.when(i == 0)
        def _():
            stage = [pltpu.make_async_copy(src, dst, stage_sems.at[j]) for j, (src, dst) in enumerate(zip(f32_in, stage_s))]
            for cp in stage:
                cp.start()
            state_s[...] = jnp.zeros_like(state_s)
            pext_s[:, pl.ds(0, HALO), :] = jnp.zeros((GROUPS, HALO, DH), f32)

            def cast(js):
                for j in js:
                    stage[j].wait()
                    cast_s[j][...] = stage_s[j][...].astype(bf16)

            _barrier(_gather_peers())
            cast(range(2))
            _gather_start(fin, fout, *fsems)
            cast(range(2, n_c))
            _gather_forward(fin, fout, *fsems)
            _gather_start(gin, gout, *gsems)
            _gather_finish(fin, fout, *fsems)
            loads = [pltpu.make_async_copy(src.at[s], dst.at[pl.ds(s * src.shape[1], src.shape[1]), :],
                                           load_sems.at[j, s])
                     for j, (src, dst) in enumerate(((fout[0], wint_s), (fout[1], wout_s))) for s in range(N_DEV)]
            for ld in loads:
                ld.start()
            for ld in loads:
                ld.wait()

        @pl.when(i == n_tiles - 3)
        def _():
            _gather_forward(gin, gout, *gsems)

        xb = x_ref[...].astype(bf16)
        xb_ref[...] = xb
        cos_t, sin_t = cos_ref[...], sin_ref[...]
        for part in range(2):
            pr = _dot(xb, wint_s[pl.ds(part * RW, RW), :], NT)
            for h in range(HEADS):
                t = pr[:, h * DH:(h + 1) * DH]
                r = t * cos_t + _swap_halves(t) * sin_t
                if part == 1:
                    r = r * K_SCALE
                qkv_ref[:, part * RW + h * DH: part * RW + (h + 1) * DH] = r.astype(bf16)
        qkv_ref[:, 2 * RW:3 * RW] = _dot(xb, wint_s[pl.ds(2 * RW, RW), :], NT).astype(bf16)
        g_ref[...] = _dot(xb, wint_s[pl.ds(3 * RW, RW), :], NT)
        p = _dot(xb, wint_s[pl.ds(4 * RW, PW), :], NT)
        for gi in range(GROUPS):
            pext_s[gi, pl.ds(HALO, tt), :] = p[:, gi * DH:(gi + 1) * DH]

        for sub in range(tt // RET_TILE):
            rows = pl.ds(sub * RET_TILE, RET_TILE)
            for h in range(HEADS):
                q = qkv_ref[rows, h * DH:(h + 1) * DH]
                k = qkv_ref[rows, RW + h * DH: RW + (h + 1) * DH]
                v = qkv_ref[rows, 2 * RW + h * DH: 2 * RW + (h + 1) * DH]
                s = _dot(q, k, NT) * dmat_ref[h]
                st = state_s[h]
                stb = st.astype(bf16)
                states_ref[sub, h] = stb
                oret_ref[rows, h * DH:(h + 1) * DH] = (_dot(s.astype(bf16), v)
                                                      + _dot((q.astype(f32) * qd_ref[h]).astype(bf16), stb))
                state_s[h] = st * cdec[h] + _dot((k.astype(f32) * kd_ref[h]).astype(bf16), v, TN)

        for h in range(HEADS):
            sl = slice(h * DH, (h + 1) * DH)
            o = oret_ref[:, sl]
            r = lax.rsqrt(jnp.mean(o * o, axis=-1, keepdims=True) + RMS_EPS)
            gg = g_ref[:, sl]
            cat_ref[:, sl] = (o * r * (gg * _sigmoid(gg))).astype(bf16)

        pos1 = (i * tt + lax.broadcasted_iota(jnp.int32, (tt, 1), 0) + 1).astype(f32)
        for gi, w in enumerate(WINDOWS):
            sl = slice(gi * DH, (gi + 1) * DH)
            stages = int(math.log2(w))
            src = pext_s
            for s in range(stages):
                lo = HALO - 8 * (stages - 1 - s)
                n = tt + HALO - lo
                shift = 2 ** s
                val = src[gi, pl.ds(lo, n), :] + src[gi, pl.ds(lo - shift, n), :]
                if s == stages - 1:
                    wsum = val
                else:
                    tmp_s[gi, pl.ds(lo, n), :] = val
                    src = tmp_s
            p_g = pext_s[gi, pl.ds(HALO, tt), :]
            pooled = (wsum / jnp.minimum(pos1, float(w)) - p_g).astype(bf16)
            pooled_ref[:, sl] = pooled
            y = _dot(pooled, wpool_ref[gi].astype(bf16)) * pscale_ref[:, sl]
            cat_ref[:, RW + gi * DH: RW + (gi + 1) * DH] = y.astype(bf16)
        pext_s[:, pl.ds(0, HALO), :] = pext_s[:, pl.ds(tt, HALO), :]

        z = ALPHA * x_ref[...] + _dot(cat_ref[...], wout_s[...])
        mu = jnp.mean(z, axis=-1, keepdims=True)
        zc = z - mu
        rstd = lax.rsqrt(jnp.mean(zc * zc, axis=-1, keepdims=True) + LN_EPS)
        xhat = zc * rstd
        xhat_ref[...] = xhat
        rstd_ref[...] = rstd
        x1b_ref[...] = (xhat * g1_ref[...] + b1_ref[...]).astype(bf16)

        @pl.when(i == n_tiles - 1)
        def _():
            _gather_finish(gin, gout, *gsems)

    tile = lambda w: pl.BlockSpec((tt, w), lambda i: (i, 0))
    hbm = pl.BlockSpec(memory_space=pltpu.HBM)
    out_shape = (
        jax.ShapeDtypeStruct((T, 3 * RW), bf16),
        jax.ShapeDtypeStruct((T, RW), f32),
        jax.ShapeDtypeStruct((T, RW), f32),
        jax.ShapeDtypeStruct((T // RET_TILE, HEADS, DH, DH), bf16),
        jax.ShapeDtypeStruct((T, D), bf16),
        jax.ShapeDtypeStruct((T, PW), bf16),
        jax.ShapeDtypeStruct((T, D), f32),
        jax.ShapeDtypeStruct((T, 1), f32),
        jax.ShapeDtypeStruct((T, D), bf16),
        jax.ShapeDtypeStruct((T, D), bf16),
    ) + tuple(jax.ShapeDtypeStruct((N_DEV,) + b.shape, bf16) for b in to_bf16
              ) + tuple(jax.ShapeDtypeStruct((N_DEV,) + b.shape, b.dtype) for b in gather)
    return pl.pallas_call(
        body, name="mix_forward", grid=(n_tiles,), out_shape=out_shape,
        in_specs=[tile(D), tile(DH), tile(DH),
                  _const_spec((HEADS, RET_TILE, RET_TILE)), _const_spec((HEADS, RET_TILE, DH)),
                  _const_spec((HEADS, RET_TILE, DH)),
                  _const_spec((GROUPS, DH, DH)), _const_spec((1, PW)),
                  _const_spec((1, D)), _const_spec((1, D))] + [hbm] * (2 + n_g),
        out_specs=(tile(3 * RW), tile(RW), tile(RW),
                   pl.BlockSpec((tt // RET_TILE, HEADS, DH, DH), lambda i: (i, 0, 0, 0)),
                   tile(D), tile(PW), tile(D), tile(1), tile(D), tile(D)) + (hbm,) * (2 + n_g),
        scratch_shapes=[pltpu.VMEM((HEADS, DH, DH), f32), pltpu.VMEM((GROUPS, tt + HALO, DH), f32),
                        pltpu.VMEM((GROUPS, tt + HALO, DH), f32), pltpu.VMEM((IN_W, D), bf16), pltpu.VMEM((D, D), bf16),
                        pltpu.SemaphoreType.DMA((2, N_DEV)), pltpu.SemaphoreType.DMA((n_c,))]
        + [pltpu.VMEM(b.shape, f32) for b in to_bf16] + [pltpu.VMEM(b.shape, bf16) for b in to_bf16]
        + _gather_sems(2) + _gather_sems(n_g),
        compiler_params=pltpu.CompilerParams(dimension_semantics=("arbitrary",), vmem_limit_bytes=V7X_VMEM_LIMIT,
                                             collective_id=GATHER_BARRIER),
    )(x, cos, sin, dmat, qd, kd, w_pool, pool_scale, ln1_g, ln1_b, *to_bf16, *gather)


def _ffn_forward_backward(xhat1, rstd1, ln1_g, ln1_b, w_up_t, conv_w, conv_b, w_down, ln2_g, ln2_b, target,
                          tt=256):
    n_tiles = T // tt
    FH = 16
    hb = tt // FH

    def body(xhat_ref, halo_ref, rstd_ref, g1_ref, b1_ref, wupt_ref, cw_ref, cb_ref, wdown_ref, g2_ref, b2_ref, tgt_ref,
             dz1_ref, dz2b_ref, du_ref, f_ref, loss_ref, dg2_ref, db2_ref, dg1_ref, db1_ref, dcb_ref, dcw_ref,
             gext_s, val_s, dhext_s):
        i = pl.program_id(0)
        tile_idx = n_tiles - 1 - i

        def rd(ref, off):
            return jnp.concatenate([ref[k, pl.ds(off, tt), :] for k in range(D_FF // 128)], axis=1)

        def wr(ref, val):
            for k in range(D_FF // 128):
                ref[k, pl.ds(0, val.shape[0]), :] = val[:, k * 128:(k + 1) * 128]

        @pl.when(i == 0)
        def _():
            for r in (loss_ref, dg2_ref, db2_ref, dg1_ref, db1_ref, dcb_ref, dcw_ref):
                r[...] = jnp.zeros_like(r)
            dhext_s[:, pl.ds(tt, 8), :] = jnp.zeros((D_FF // 128, 8, 128), f32)

        g1, b1 = g1_ref[...], b1_ref[...]
        xhat = xhat_ref[...]
        x1 = xhat * g1 + b1
        x1b = x1.astype(bf16)
        x1h = ((halo_ref[...] * g1 + b1) * jnp.where(tile_idx == 0, 0.0, 1.0)).astype(bf16)
        x1ext = jnp.concatenate([x1h, x1b], axis=0)

        val = _dot(x1b, wupt_ref[pl.ds(0, D_FF), :], NT)
        gate_ext = _dot(x1ext, wupt_ref[pl.ds(D_FF, D_FF), :], NT)
        wr(gext_s, gate_ext)
        hh = (cb_ref[...] + cw_ref[0:1, :] * rd(gext_s, FH - 2) + cw_ref[1:2, :] * rd(gext_s, FH - 1)
              + cw_ref[2:3, :] * gate_ext[FH:])
        sg = _sigmoid(hh)
        act = hh * sg
        wr(dhext_s, act)
        val_s[...] = val * (sg + act * (1.0 - sg))
        fb = (act * val).astype(bf16)
        f_ref[...] = fb

        z = ALPHA * x1 + _dot(fb, wdown_ref[...])
        mu = jnp.mean(z, axis=-1, keepdims=True)
        zc = z - mu
        rstd2 = lax.rsqrt(jnp.mean(zc * zc, axis=-1, keepdims=True) + LN_EPS)
        xh2 = zc * rstd2
        diff = xh2 * g2_ref[...] + b2_ref[...] - tgt_ref[...]
        loss_ref[...] += 0.5 * jnp.sum(diff * diff) / D
        dy = diff * (1.0 / D)
        dg2_ref[...] += jnp.sum(dy * xh2, axis=0, keepdims=True)
        db2_ref[...] += jnp.sum(dy, axis=0, keepdims=True)
        dyg = dy * g2_ref[...]
        dz2 = rstd2 * (dyg - jnp.mean(dyg, axis=-1, keepdims=True) - xh2 * jnp.mean(dyg * xh2, axis=-1, keepdims=True))
        dz2b = dz2.astype(bf16)
        dz2b_ref[...] = dz2b

        df = _dot(dz2b, wdown_ref[...], NT)
        dval = df * rd(dhext_s, 0)
        dh = df * val_s[...]
        wr(dhext_s, dh)
        dh1, dh2, g0 = rd(dhext_s, 1), rd(dhext_s, 2), rd(gext_s, FH)
        dcb_ref[...] += jnp.sum(dh, axis=0, keepdims=True)
        dcw_ref[0:1, :] += jnp.sum(dh2 * g0, axis=0, keepdims=True)
        dcw_ref[1:2, :] += jnp.sum(dh1 * g0, axis=0, keepdims=True)
        dcw_ref[2:3, :] += jnp.sum(dh * g0, axis=0, keepdims=True)
        dgate = cw_ref[2:3, :] * dh + cw_ref[1:2, :] * dh1 + cw_ref[0:1, :] * dh2
        dvalb, dgateb = dval.astype(bf16), dgate.astype(bf16)
        du_ref[:, :D_FF] = dvalb
        du_ref[:, D_FF:] = dgateb
        dx1 = ALPHA * dz2 + _dot(dvalb, wupt_ref[pl.ds(0, D_FF), :]) + _dot(dgateb, wupt_ref[pl.ds(D_FF, D_FF), :])
        dhext_s[:, pl.ds(tt, 8), :] = dhext_s[:, pl.ds(0, 8), :]

        dg1_ref[...] += jnp.sum(dx1 * xhat, axis=0, keepdims=True)
        db1_ref[...] += jnp.sum(dx1, axis=0, keepdims=True)
        dxg = dx1 * g1
        dz1_ref[...] = rstd_ref[...] * (dxg - jnp.mean(dxg, axis=-1, keepdims=True)
                                        - xhat * jnp.mean(dxg * xhat, axis=-1, keepdims=True))

    rtile = lambda w: pl.BlockSpec((tt, w), lambda i: (n_tiles - 1 - i, 0))
    acc = lambda shape: pl.BlockSpec(shape, lambda i: (0, 0))
    out_shape = (
        jax.ShapeDtypeStruct((T, D), f32),
        jax.ShapeDtypeStruct((T, D), bf16),
        jax.ShapeDtypeStruct((T, 2 * D_FF), bf16),
        jax.ShapeDtypeStruct((T, D_FF), bf16),
        jax.ShapeDtypeStruct((8, 128), f32),
        jax.ShapeDtypeStruct((1, D), f32), jax.ShapeDtypeStruct((1, D), f32),
        jax.ShapeDtypeStruct((1, D), f32), jax.ShapeDtypeStruct((1, D), f32),
        jax.ShapeDtypeStruct((1, D_FF), f32), jax.ShapeDtypeStruct((3, D_FF), f32),
    )
    return pl.pallas_call(
        body, name="ffn_forward_backward", grid=(n_tiles,), out_shape=out_shape,
        in_specs=[rtile(D),
                  pl.BlockSpec((FH, D), lambda i: (jnp.maximum((n_tiles - 1 - i) * hb - 1, 0), 0)),
                  rtile(1), _const_spec((1, D)), _const_spec((1, D)), _const_spec((2 * D_FF, D)),
                  _const_spec((3, D_FF)), _const_spec((1, D_FF)), _const_spec((D_FF, D)),
                  _const_spec((1, D)), _const_spec((1, D)), rtile(D)],
        out_specs=(rtile(D), rtile(D), rtile(2 * D_FF), rtile(D_FF), acc((8, 128)),
                   acc((1, D)), acc((1, D)), acc((1, D)), acc((1, D)), acc((1, D_FF)), acc((3, D_FF))),
        scratch_shapes=[pltpu.VMEM((D_FF // 128, tt + FH, 128), f32), pltpu.VMEM((tt, D_FF), f32),
                        pltpu.VMEM((D_FF // 128, tt + 8, 128), f32)],
        compiler_params=pltpu.CompilerParams(dimension_semantics=("arbitrary",), vmem_limit_bytes=V7X_VMEM_LIMIT),
    )(xhat1, xhat1, rstd1, ln1_g, ln1_b, w_up_t, conv_w, conv_b, w_down, ln2_g, ln2_b, target)


def _mix_backward(dz1, w_out, qkv, g, oret, states, pooled, cat, cos, sin, dmat, qd, kd, cdec, w_pool, pool_scale, w_in_t,
                  small_ffn, after, tt=MIX_TILE):
    n_tiles = T // tt

    def body(dz1_ref, wout_ref, qkv_ref, g_ref, oret_ref, states_ref, pooled_ref, cat_ref, cos_ref, sin_ref, dmat_ref,
             qd_ref, kd_ref, wpool_ref, pscale_ref, wint_ref, *rest):
        ffn_refs, rest = rest[:len(SMALL_FFN)], rest[len(SMALL_FFN):]
        after_ref, dproj_ref, gx_ref, small_ref, dwout_ref, dstate_s, dout_s, eext_s, tmp_s, dwout_s, dpscale_s = rest
        i = pl.program_id(0)
        tile_idx = n_tiles - 1 - i

        @pl.when(i == 0)
        def _():
            dstate_s[...] = jnp.zeros_like(dstate_s)
            small_ref[...] = jnp.zeros_like(small_ref)
            dpscale_s[...] = jnp.zeros_like(dpscale_s)
            dwout_s[...] = jnp.zeros_like(dwout_s)
            eext_s[:, pl.ds(tt, HALO), :] = jnp.zeros((GROUPS, HALO, DH), f32)

        dz1 = dz1_ref[...]
        dz1b = dz1.astype(bf16)
        dcat = _dot(dz1b, wout_ref[...], NT)
        dwout_s[...] += _dot(cat_ref[...], dz1b, TN)

        pos1 = (tile_idx * tt + lax.broadcasted_iota(jnp.int32, (tt, 1), 0) + 1).astype(f32)
        for gi, w in enumerate(WINDOWS):
            sl = slice(gi * DH, (gi + 1) * DH)
            dpo = dcat[:, RW + gi * DH: RW + (gi + 1) * DH]
            pooled_g = pooled_ref[:, sl]
            wpool_g = wpool_ref[gi].astype(bf16)
            ylin = _dot(pooled_g, wpool_g)
            dpscale_s[:, sl] += jnp.sum(dpo * ylin, axis=0, keepdims=True)
            dpw = (dpo * pscale_ref[:, sl]).astype(bf16)
            small_ref[pl.ds(gi * DH, DH), :] += _dot(pooled_g, dpw, TN)
            dpooled = _dot(dpw, wpool_g, NT)
            eext_s[gi, pl.ds(0, tt), :] = dpooled / jnp.minimum(pos1, float(w))
            stages = int(math.log2(w))
            src = eext_s
            for s in range(stages):
                n = tt + 8 * (stages - 1 - s)
                shift = 2 ** s
                val = src[gi, pl.ds(0, n), :] + src[gi, pl.ds(shift, n), :]
                if s == stages - 1:
                    wsum = val
                else:
                    tmp_s[gi, pl.ds(0, n), :] = val
                    src = tmp_s
            dproj_ref[:, 4 * RW + gi * DH: 4 * RW + (gi + 1) * DH] = (wsum - dpooled).astype(bf16)
        eext_s[:, pl.ds(tt, HALO), :] = eext_s[:, pl.ds(0, HALO), :]

        for h in range(HEADS):
            sl = slice(h * DH, (h + 1) * DH)
            dr = dcat[:, sl]
            o = oret_ref[:, sl]
            r = lax.rsqrt(jnp.mean(o * o, axis=-1, keepdims=True) + RMS_EPS)
            rn = o * r
            gg = g_ref[:, sl]
            sg = _sigmoid(gg)
            dproj_ref[:, 3 * RW + h * DH: 3 * RW + (h + 1) * DH] = (dr * rn * (sg * (1.0 + gg * (1.0 - sg)))).astype(bf16)
            drn = dr * (gg * sg)
            dout_s[:, sl] = (r * (drn - rn * jnp.mean(drn * rn, axis=-1, keepdims=True))).astype(bf16)

        for sub in reversed(range(tt // RET_TILE)):
            rows = pl.ds(sub * RET_TILE, RET_TILE)
            cos_t, sin_t = cos_ref[rows, :], sin_ref[rows, :]
            for h in range(HEADS):
                q = qkv_ref[rows, h * DH:(h + 1) * DH]
                k = qkv_ref[rows, RW + h * DH: RW + (h + 1) * DH]
                v = qkv_ref[rows, 2 * RW + h * DH: 2 * RW + (h + 1) * DH]
                do = dout_s[rows, h * DH:(h + 1) * DH]
                stb = states_ref[sub, h]
                dst = dstate_s[h]
                dstb = dst.astype(bf16)
                sb = (_dot(q, k, NT) * dmat_ref[h]).astype(bf16)
                dsb = (_dot(do, v, NT) * dmat_ref[h]).astype(bf16)
                dq = _dot(dsb, k) + _dot(do, stb, NT) * qd_ref[h]
                dk = _dot(dsb, q, TN) + _dot(v, dstb, NT) * kd_ref[h]
                dv = _dot(sb, do, TN) + _dot((k.astype(f32) * kd_ref[h]).astype(bf16), dstb)
                dstate_s[h] = dst * cdec[h] + _dot((q.astype(f32) * qd_ref[h]).astype(bf16), do, TN)
                dproj_ref[rows, h * DH:(h + 1) * DH] = (dq * cos_t - _swap_halves(dq) * sin_t).astype(bf16)
                dproj_ref[rows, RW + h * DH: RW + (h + 1) * DH] = (
                    (dk * cos_t - _swap_halves(dk) * sin_t) * K_SCALE).astype(bf16)
                dproj_ref[rows, 2 * RW + h * DH: 2 * RW + (h + 1) * DH] = dv.astype(bf16)

        gx_ref[...] = ALPHA * dz1 + _dot(dproj_ref[...], wint_ref[...])

        @pl.when(i == n_tiles - 1)
        def _():
            dwout_ref[...] = dwout_s[...].astype(bf16)
            at = GROUPS * DH
            for ref, size in [(dpscale_s, PW)] + [(ref, size) for ref, (_, size) in zip(ffn_refs, SMALL_FFN)]:
                for j in range(size // 128):
                    r, k = divmod(j, ref.shape[1] // 128)
                    small_ref[at + j: at + j + 1, :] = ref[r:r + 1, k * 128:(k + 1) * 128]
                at = SMALL_FFN_AT if ref is dpscale_s else at + size // 128

    rtile = lambda w: pl.BlockSpec((tt, w), lambda i: (n_tiles - 1 - i, 0))
    out_shape = (
        jax.ShapeDtypeStruct((T, IN_W), bf16),
        jax.ShapeDtypeStruct((T, D), f32),
        jax.ShapeDtypeStruct((SMALL_ROWS, 128), f32),
        jax.ShapeDtypeStruct((D, D), bf16),
    )
    return pl.pallas_call(
        body, name="mix_backward", grid=(n_tiles,), out_shape=out_shape,
        in_specs=[rtile(D), _const_spec((D, D)), rtile(3 * RW), rtile(RW), rtile(RW),
                  pl.BlockSpec((tt // RET_TILE, HEADS, DH, DH), lambda i: (n_tiles - 1 - i, 0, 0, 0)),
                  rtile(PW), rtile(D), rtile(DH), rtile(DH),
                  _const_spec((HEADS, RET_TILE, RET_TILE)), _const_spec((HEADS, RET_TILE, DH)),
                  _const_spec((HEADS, RET_TILE, DH)),
                  _const_spec((GROUPS, DH, DH)), _const_spec((1, PW)), _const_spec((IN_W, D)),
                  *[_const_spec(a.shape) for a in small_ffn], pl.BlockSpec(memory_space=pl.ANY)],
        out_specs=(rtile(IN_W), rtile(D), pl.BlockSpec((SMALL_ROWS, 128), lambda i: (0, 0)),
                   pl.BlockSpec((D, D), lambda i: (0, 0), pipeline_mode=pl.Buffered(1))),
        scratch_shapes=[pltpu.VMEM((HEADS, DH, DH), f32), pltpu.VMEM((tt, RW), bf16),
                        pltpu.VMEM((GROUPS, tt + HALO, DH), f32), pltpu.VMEM((GROUPS, tt + HALO, DH), f32),
                        pltpu.VMEM((D, D), f32), pltpu.VMEM((1, PW), f32)],
        compiler_params=pltpu.CompilerParams(dimension_semantics=("arbitrary",), vmem_limit_bytes=V7X_VMEM_LIMIT),
    )(dz1, w_out, qkv, g, oret, states, pooled, cat, cos, sin, dmat, qd, kd, w_pool, pool_scale, w_in_t, *small_ffn,
      after)


def _weight_grad(a, b, name, tm, exchange=()):
    m = a.shape[1]
    n_m, n_e = m // tm, len(exchange)

    def body(a_ref, b_ref, *rest):
        ein, o_ref, eout, sems = rest[:n_e], rest[n_e], rest[n_e + 1:2 * n_e + 1], rest[2 * n_e + 1:]
        i = pl.program_id(0)

        if n_e:
            @pl.when(i == 0)
            def _():
                _chip_exchange_start(ein, eout, *sems)

        o_ref[...] = _dot(a_ref[...], b_ref[...].astype(bf16), TN).astype(bf16)

        if n_e:
            @pl.when(i == n_m - 1)
            def _():
                _chip_exchange_finish(ein, eout, *sems)

    hbm = pl.BlockSpec(memory_space=pltpu.HBM)
    return pl.pallas_call(
        body, name=name, grid=(n_m,),
        out_shape=(jax.ShapeDtypeStruct((m, D), bf16),) + tuple(jax.ShapeDtypeStruct(e.shape, e.dtype) for e in exchange),
        in_specs=[pl.BlockSpec((T, tm), lambda i: (0, i)),
                  pl.BlockSpec((T, D), lambda i: (0, 0), pipeline_mode=pl.Buffered(1))] + [hbm] * n_e,
        out_specs=(pl.BlockSpec((tm, D), lambda i: (i, 0)),) + (hbm,) * n_e,
        scratch_shapes=_chip_exchange_sems(n_e),
        compiler_params=pltpu.CompilerParams(dimension_semantics=("arbitrary",), vmem_limit_bytes=V7X_VMEM_LIMIT,
                                             collective_id=CHIP_BARRIER if n_e else None),
    )(a, b, *exchange)


CHIP_FLIPS = ((1, 0), (0, 1), (1, 1))
PAIR_BARRIER, CHIP_BARRIER, GATHER_BARRIER, CHIP_BARRIER_SPLIT = 0, 1, 2, 3


def _barrier(peers):
    sem = pltpu.get_barrier_semaphore()
    for peer in peers:
        pl.semaphore_signal(sem, inc=1, device_id=peer, device_id_type=pl.DeviceIdType.MESH)
    pl.semaphore_wait(sem, len(peers))


def _me():
    return lax.axis_index("x"), lax.axis_index("y"), lax.axis_index("c")


def _chip(me, k):
    x, y, _ = me
    if k == 0:
        return x, y
    fx, fy = CHIP_FLIPS[k - 1]
    return (1 - x if fx else x), (1 - y if fy else y)


def _slot(x, y, c):
    return 4 * x + 2 * y + c


def _remote(src, dst, send_sem, recv_sem, to):
    return pltpu.make_async_remote_copy(src_ref=src, dst_ref=dst, send_sem=send_sem, recv_sem=recv_sem,
                                        device_id=to, device_id_type=pl.DeviceIdType.MESH)


def _gather_sems(n):
    return [pltpu.SemaphoreType.DMA((7, n)), pltpu.SemaphoreType.DMA((7, n)), pltpu.SemaphoreType.DMA((n,))] if n else []


def _gather_copy(k, j, gin, gout, send_sems, recv_sems, sending):
    x, y, c = _me()
    sibling, x_chip, y_chip, d_chip = (x, y, 1 - c), (1 - x, y), (x, 1 - y), (1 - x, 1 - y)
    south = c == 0
    passed_on = (jnp.where(south, 1 - x, x), jnp.where(south, y, 1 - y), c)
    src, to = gin[j], sibling
    if sending:
        block = {0: (x, y, c), 1: (x, y, c), 2: (x, y, c), 3: passed_on, 4: (*x_chip, c), 5: (*y_chip, c), 6: (*d_chip, c)}[k]
        to = {1: (*x_chip, c), 2: (*y_chip, c), 3: (jnp.where(south, x, 1 - x), jnp.where(south, 1 - y, y), c)}.get(k, sibling)
        if k >= 3:
            src = gout[j].at[_slot(*block)]
    else:
        block = {0: sibling, 1: (*x_chip, c), 2: (*y_chip, c), 3: (*d_chip, c), 4: (*x_chip, 1 - c), 5: (*y_chip, 1 - c),
                 6: (*d_chip, 1 - c)}[k]
    return _remote(src, gout[j].at[_slot(*block)], send_sems.at[k, j], recv_sems.at[k, j], to)


def _gather_do(ks, action, gin, gout, send_sems, recv_sems):
    for k in ks:
        for j in range(len(gin)):
            cp = _gather_copy(k, j, gin, gout, send_sems, recv_sems, action != "wait_recv")
            getattr(cp, action)()


def _gather_peers():
    x, y, c = _me()
    return [(x, y, 1 - c), (1 - x, y, c), (x, 1 - y, c)]


def _gather_start(gin, gout, send_sems, recv_sems, local_sems):
    for j in range(len(gin)):
        pltpu.make_async_copy(gin[j], gout[j].at[_slot(*_me())], local_sems.at[j]).start()
    _gather_do((0, 1, 2), "start", gin, gout, send_sems, recv_sems)


def _gather_forward(gin, gout, send_sems, recv_sems, local_sems):
    _gather_do((1, 2), "wait_recv", gin, gout, send_sems, recv_sems)
    _gather_do((3, 4, 5), "start", gin, gout, send_sems, recv_sems)


def _gather_finish(gin, gout, send_sems, recv_sems, local_sems):
    _gather_do((3,), "wait_recv", gin, gout, send_sems, recv_sems)
    _gather_do((6,), "start", gin, gout, send_sems, recv_sems)
    _gather_do((0, 4, 5, 6), "wait_recv", gin, gout, send_sems, recv_sems)
    _gather_do(range(7), "wait_send", gin, gout, send_sems, recv_sems)
    for j in range(len(gin)):
        pltpu.make_async_copy(gin[j], gout[j].at[_slot(*_me())], local_sems.at[j]).wait()


def _pair_reduce(parts, name, gather_sum=None):
    n = len(parts)
    n_h = 0 if gather_sum is None else 1

    def body(*refs):
        ins, g_terms, refs = refs[:n], refs[n:n + 2 * n_h], refs[n + 2 * n_h:]
        own, others, g_out, refs = refs[:n], refs[n:2 * n], refs[2 * n:2 * n + n_h], refs[2 * n + n_h:]
        landing, mine, (send_sems, recv_sems, local_sems), g_scratch = refs[:n], refs[n:2 * n], refs[2 * n:2 * n + 3], refs[2 * n + 3:]
        me = _me()
        x, y, c = me
        sibling = (x, y, 1 - c)
        _barrier(_gather_peers() if n_h else [sibling])
        if n_h:
            piece_s, g_sems = g_scratch[0], g_scratch[1:]
            acc = g_terms[0][...].astype(f32)
            for k in range(3):
                acc = acc + g_terms[1][k].astype(f32)
            piece_s[...] = acc
            _gather_start([piece_s], g_out, *g_sems)
        sends, loads = [], []
        for k in range(4):
            for j in range(n):
                cp = _remote(ins[j].at[_slot(*_chip(me, k), 1 - c)], landing[j].at[k], send_sems.at[k, j],
                             recv_sems.at[k, j], sibling)
                cp.start()
                sends.append(cp)
                ld = pltpu.make_async_copy(ins[j].at[_slot(*_chip(me, k), c)], mine[j].at[k], local_sems.at[k, j])
                ld.start()
                loads.append(ld)
        if n_h:
            _gather_forward([piece_s], g_out, *g_sems)
        stores = []
        for k in range(4):
            for j in range(n):
                loads[k * n + j].wait()
                _remote(ins[j].at[0], landing[j].at[k], send_sems.at[k, j], recv_sems.at[k, j], sibling).wait_recv()
                mine[j][k] = (mine[j][k].astype(f32) + landing[j][k].astype(f32)).astype(mine[j].dtype)
                st = pltpu.make_async_copy(mine[j].at[k], own[j] if k == 0 else others[j].at[k - 1], local_sems.at[k, j])
                st.start()
                stores.append(st)
        for cp in sends:
            cp.wait_send()
        for st in stores:
            st.wait()
        if n_h:
            _gather_finish([piece_s], g_out, *g_sems)

    vm, hbm = pl.BlockSpec(memory_space=pltpu.VMEM), pl.BlockSpec(memory_space=pltpu.HBM)
    g_shape = gather_sum[0].shape if n_h else ()
    return pl.pallas_call(
        body, name=name,
        out_shape=tuple(jax.ShapeDtypeStruct(p.shape[1:], p.dtype) for p in parts)
        + tuple(jax.ShapeDtypeStruct((3,) + p.shape[1:], p.dtype) for p in parts)
        + tuple([jax.ShapeDtypeStruct((N_DEV,) + g_shape, f32)] * n_h),
        in_specs=[hbm] * n + [vm] * (2 * n_h), out_specs=(hbm,) * (2 * n + n_h),
        scratch_shapes=[pltpu.VMEM((4,) + p.shape[1:], p.dtype) for p in parts] * 2
        + [pltpu.SemaphoreType.DMA((4, n)), pltpu.SemaphoreType.DMA((4, n)), pltpu.SemaphoreType.DMA((4, n))]
        + ([pltpu.VMEM(g_shape, f32)] + _gather_sems(1)) * n_h,
        compiler_params=pltpu.CompilerParams(vmem_limit_bytes=V7X_VMEM_LIMIT,
                                             collective_id=GATHER_BARRIER if n_h else PAIR_BARRIER),
    )(*parts, *(gather_sum or ()))


def _chip_exchange_sems(n):
    return [pltpu.SemaphoreType.DMA((3, n)), pltpu.SemaphoreType.DMA((3, n))] if n else []


def _chip_exchange_copy(k, j, ein, eout, send_sems, recv_sems):
    me = _me()
    return _remote(ein[j].at[k - 1], eout[j].at[k - 1], send_sems.at[k - 1, j], recv_sems.at[k - 1, j],
                   (*_chip(me, k), me[2]))


def _chip_peers():
    me = _me()
    return [(*_chip(me, k), me[2]) for k in range(1, 4)]


def _chip_exchange_start(ein, eout, send_sems, recv_sems, barrier=True):
    if barrier:
        _barrier(_chip_peers())
    for k in range(1, 4):
        for j in range(len(ein)):
            _chip_exchange_copy(k, j, ein, eout, send_sems, recv_sems).start()


def _chip_exchange_finish(ein, eout, send_sems, recv_sems):
    for k in range(1, 4):
        for j in range(len(ein)):
            _chip_exchange_copy(k, j, ein, eout, send_sems, recv_sems).wait_recv()
    for k in range(1, 4):
        for j in range(len(ein)):
            _chip_exchange_copy(k, j, ein, eout, send_sems, recv_sems).wait_send()


def _split_copies(src_ref, dst_ref, sems):
    me = _me()
    return [_remote(src_ref.at[k - 1], dst_ref.at[k - 1], sems[k - 1], sems[2 + k], (*_chip(me, k), me[2]))
            for k in range(1, 4)]


def _exchange_start(others, name, barrier_id):
    def body(src_ref, land_ref, *rest):
        sems, token_ref = rest[:6], rest[8]
        _barrier(_chip_peers())
        for copy in _split_copies(src_ref, land_ref, sems):
            copy.start()
        token_ref[...] = jnp.zeros_like(token_ref)

    hbm, sem = pl.BlockSpec(memory_space=pltpu.HBM), pl.BlockSpec(memory_space=pltpu.SEMAPHORE)
    thru = pltpu.HBM(others.shape, others.dtype)
    res = pl.pallas_call(
        body, name=name,
        out_shape=(pltpu.SemaphoreType.DMA(()),) * 6 + (thru, thru, jax.ShapeDtypeStruct((8, 128), f32)),
        in_specs=(hbm, hbm), out_specs=(sem,) * 6 + (hbm, hbm, pl.BlockSpec(memory_space=pltpu.VMEM)),
        input_output_aliases={0: 6, 1: 7},
        compiler_params=pltpu.CompilerParams(has_side_effects=pltpu.SideEffectType.DATAFLOW_SIDE_EFFECTING,
                                             collective_id=barrier_id),
    )(pltpu.with_memory_space_constraint(others, pltpu.HBM),
      pltpu.with_memory_space_constraint(lax.empty(others.shape, others.dtype), pltpu.HBM))
    return res[:6], res[6], res[7], res[8]


def _exchange_wait(sems, src_thru, land_thru, after, name):
    n_after = len(after)

    def body(src_ref, land_ref, *rest):
        for copy in _split_copies(src_ref, land_ref, rest[:6]):
            copy.wait_send()
            copy.wait_recv()

    hbm, sem = pl.BlockSpec(memory_space=pltpu.HBM), pl.BlockSpec(memory_space=pltpu.SEMAPHORE)
    thru = pltpu.HBM(src_thru.shape, src_thru.dtype)
    return pl.pallas_call(
        body, name=name, out_shape=(thru, thru),
        in_specs=(hbm, hbm) + (sem,) * 6 + (pl.BlockSpec(memory_space=pl.ANY),) * n_after, out_specs=(hbm, hbm),
        input_output_aliases={0: 0, 1: 1},
        compiler_params=pltpu.CompilerParams(has_side_effects=pltpu.SideEffectType.DATAFLOW_SIDE_EFFECTING),
    )(src_thru, land_thru, *sems, *after)[1]


def _adam_update(w, g, m, v):
    m = ADAM_B1 * m + (1.0 - ADAM_B1) * g
    v = ADAM_B2 * v + (1.0 - ADAM_B2) * (g * g)
    m_hat = m / (1.0 - ADAM_B1 ** ADAM_STEP)
    v_hat = v / (1.0 - ADAM_B2 ** ADAM_STEP)
    return -ADAM_LR * (m_hat / (jnp.sqrt(v_hat) + ADAM_EPS) + ADAM_WD * w), m, v


def _sum_adamw(own, arrived, w, m, v, name, steps, after=()):
    rows = own.shape[0]
    br = rows // steps

    def body(own_ref, arr_ref, w_ref, m_ref, v_ref, *rest):
        g_out, d_out, m_out, v_out = rest[len(after):]
        g = own_ref[...].astype(f32)
        for k in range(3):
            g = g + arr_ref[k].astype(f32)
        g_out[...] = g
        d_out[...], m_out[...], v_out[...] = _adam_update(w_ref[...], g, m_ref[...], v_ref[...])

    blk = pl.BlockSpec((br, D), lambda i: (i, 0))
    return pl.pallas_call(
        body, name=name, grid=(steps,), out_shape=(jax.ShapeDtypeStruct((rows, D), f32),) * 4,
        in_specs=[blk, pl.BlockSpec((3, br, D), lambda i: (0, i, 0)), blk, blk, blk]
        + [pl.BlockSpec(memory_space=pl.ANY)] * len(after), out_specs=(blk,) * 4,
        compiler_params=pltpu.CompilerParams(dimension_semantics=("parallel",), vmem_limit_bytes=V7X_VMEM_LIMIT),
    )(own, arrived, w, m, v, *after)


def _adamw(ws, gs, ms, vs, packed, scalar_row, name, after=()):
    n = len(ws)
    given = [g for g in gs if not isinstance(g, int)]
    taken = [j for j in range(n) if isinstance(gs[j], int)]

    def body(packed_ref, *refs):
        w_r, m_r, v_r = (refs[k * n:(k + 1) * n] for k in range(3))
        given_r, outs = list(refs[3 * n:3 * n + len(given)]), refs[3 * n + len(given) + len(after):]
        g_o, outs = dict(zip(taken, outs[:len(taken)])), outs[len(taken):]
        d_o, m_o, v_o = (outs[k * n:(k + 1) * n] for k in range(3))
        outs[3 * n][...] = packed_ref[scalar_row:scalar_row + 1, 0:1]
        for j in range(n):
            if j in g_o:
                (r, c), at = ws[j].shape, gs[j]
                if c == 128:
                    g = packed_ref[at:at + r, :]
                else:
                    assert r == 1
                    g = jnp.concatenate([packed_ref[at + k:at + k + 1, :] for k in range(c // 128)], axis=1)
                g_o[j][...] = g
            else:
                g = given_r.pop(0)[...]
            d_o[j][...], m_o[j][...], v_o[j][...] = _adam_update(w_r[j][...], g, m_r[j][...], v_r[j][...])

    vm = pl.BlockSpec(memory_space=pltpu.VMEM)
    shapes = tuple(jax.ShapeDtypeStruct(w.shape, f32) for w in ws)
    n_out = len(taken) + 3 * n + 1
    return pl.pallas_call(
        body, name=name,
        out_shape=tuple(shapes[j] for j in taken) + shapes * 3 + (jax.ShapeDtypeStruct((1, 1), f32),),
        in_specs=[vm] * (1 + 3 * n + len(given)) + [pl.BlockSpec(memory_space=pl.ANY)] * len(after),
        out_specs=tuple([vm] * n_out),
        compiler_params=pltpu.CompilerParams(vmem_limit_bytes=V7X_VMEM_LIMIT),
    )(packed, *ws, *ms, *vs, *given, *after)


SMALL_FFN = (("ln1_g", D), ("ln1_b", D), ("ln2_g", D), ("ln2_b", D), ("conv_b", D_FF), ("conv_w", 3 * D_FF), ("loss", 128))
SMALL_FFN_AT = 520
SMALL_ROWS = 704


def _small_rows():
    rows, at = {"w_pool": 0, "pool_scale": GROUPS * DH}, SMALL_FFN_AT
    for k, size in SMALL_FFN:
        rows[k] = at
        at += size // 128
    return rows


def kernel(x, w_in, w_pool, pool_scale, w_out, ln1_g, ln1_b, w_up, conv_w, conv_b, w_down, ln2_g, ln2_b, loss_target, m_w_in, m_w_pool, m_pool_scale, m_w_out, m_ln1_g, m_ln1_b, m_w_up, m_conv_w, m_conv_b, m_w_down, m_ln2_g, m_ln2_b, v_w_in, v_w_pool, v_pool_scale, v_w_out, v_ln1_g, v_ln1_b, v_w_up, v_conv_w, v_conv_b, v_w_down, v_ln2_g, v_ln2_b):
    me = 4 * lax.axis_index("x") + 2 * lax.axis_index("y") + lax.axis_index("c")
    x2, tgt = x[0], loss_target[0]

    cos, sin = _rope_tables()
    dmat, qd, kd, cdec = _decay_tables(RET_TILE)

    qkv, g, oret, states, cat, pooled, xhat1, rstd1, x1b, xb, g_in, g_out, g_up, g_down, g_cw = _mix_forward(
        x2, w_in[0].T, w_out[0], cos, sin, dmat, qd, kd, cdec, w_pool[0], pool_scale, ln1_g, ln1_b,
        gather_bf16=[w_up[0].T, w_down[0]], gather=[jnp.transpose(conv_w, (1, 0, 2))])
    w_in_t = g_in.reshape(IN_W, D)
    w_out_f = g_out.reshape(D, D)
    w_up_t = g_up.reshape(2 * D_FF, D)
    w_down_f = g_down.reshape(D_FF, D)
    conv_w_f = jnp.transpose(g_cw[:, :, 0, :], (1, 0, 2)).reshape(3, D_FF)
    dz1, dz2b, du, f, loss8, d_ln2_g, d_ln2_b, d_ln1_g, d_ln1_b, d_conv_b, d_conv_w = _ffn_forward_backward(
        xhat1, rstd1, ln1_g, ln1_b, w_up_t, conv_w_f, conv_b, w_down_f, ln2_g, ln2_b, tgt)
    small_ffn = [d_ln1_g, d_ln1_b, d_ln2_g, d_ln2_b, d_conv_b, d_conv_w, loss8]

    (dw_down,) = _weight_grad(f, dz2b, "grad_w_down", tm=D_FF // 2)
    own_down, oth_down = _pair_reduce([dw_down.reshape(N_DEV, ROWS_DOWN, D)], "pair_reduce_down")
    dw_up_t, arr_down = _weight_grad(du, x1b, "grad_w_up", tm=D_FF // 2, exchange=[oth_down])
    own_up, oth_up = _pair_reduce([dw_up_t.reshape(N_DEV, ROWS_UP, D)], "pair_reduce_up")
    up_sems, up_src, up_land, up_started = _exchange_start(oth_up, "exchange_up_start", CHIP_BARRIER_SPLIT)
    dproj, grad_x, small, dw_out = _mix_backward(
        dz1, w_out_f, qkv, g, oret, states, pooled, cat, cos, sin, dmat, qd, kd, cdec, w_pool[0], pool_scale, w_in_t,
        small_ffn, after=up_started)
    own_out, own_small, oth_out, oth_small = _pair_reduce(
        [dw_out.reshape(N_DEV, ROWS_OUT, D), small.reshape(N_DEV, SMALL_ROWS // N_DEV, 128)], "pair_reduce_out")
    dw_in_t, arr_out, arr_small = _weight_grad(dproj, xb, "grad_w_in", tm=IN_W // 2, exchange=[oth_out, oth_small])
    arr_up = _exchange_wait(up_sems, up_src, up_land, [dw_in_t], "exchange_up_wait")
    own_in, oth_in, gs_small = _pair_reduce([dw_in_t.reshape(N_DEV, ROWS_IN, D)], "pair_reduce_in",
                                            gather_sum=(own_small, arr_small))
    in_sems, in_src, in_land, started = _exchange_start(oth_in, "exchange_in_start", CHIP_BARRIER)

    names = ["w_in", "w_pool", "pool_scale", "w_out", "ln1_g", "ln1_b", "w_up", "conv_w", "conv_b", "w_down",
             "ln2_g", "ln2_b"]
    w_d = dict(w_in=w_in, w_pool=w_pool, pool_scale=pool_scale, w_out=w_out, ln1_g=ln1_g, ln1_b=ln1_b, w_up=w_up,
               conv_w=conv_w, conv_b=conv_b, w_down=w_down, ln2_g=ln2_g, ln2_b=ln2_b)
    m_d = dict(w_in=m_w_in, w_pool=m_w_pool, pool_scale=m_pool_scale, w_out=m_w_out, ln1_g=m_ln1_g, ln1_b=m_ln1_b,
               w_up=m_w_up, conv_w=m_conv_w, conv_b=m_conv_b, w_down=m_w_down, ln2_g=m_ln2_g, ln2_b=m_ln2_b)
    v_d = dict(w_in=v_w_in, w_pool=v_w_pool, pool_scale=v_pool_scale, w_out=v_w_out, ln1_g=v_ln1_g, ln1_b=v_ln1_b,
               w_up=v_w_up, conv_w=v_conv_w, conv_b=v_conv_b, w_down=v_w_down, ln2_g=v_ln2_g, ln2_b=v_ln2_b)
    g_d, delta, new_m, new_v = {}, {}, {}, {}

    def big_adamw(k, own, arr, transposed, steps, after=()):
        lay = (lambda a: a[0].T) if transposed else (lambda a: a[0])
        back = (lambda a: a.T[None]) if transposed else (lambda a: a[None])
        res = _sum_adamw(own, arr, lay(w_d[k]), lay(m_d[k]), lay(v_d[k]), "adamw_" + k, steps, after)
        g_d[k], delta[k], new_m[k], new_v[k] = (back(r) for r in res)
        return res[3]

    done = [big_adamw("w_up", own_up, arr_up, True, 4, after=(started,)),
            big_adamw("w_down", own_down, arr_down, False, 2, after=(started,)),
            big_adamw("w_out", own_out, arr_out, False, 2, after=(started,))]

    gs_small, rows = gs_small.reshape(SMALL_ROWS, 128), _small_rows()
    g_conv_w = gs_small[rows["conv_w"]:rows["conv_w"] + 3 * D_FF // 128].reshape(3, D_FF)
    g_d["conv_w"] = lax.dynamic_slice(g_conv_w, (0, me * (D_FF // N_DEV)), (3, D_FF // N_DEV))[None]
    lay = lambda k, a: jnp.transpose(a, (1, 0, 2)) if k == "conv_w" else a.reshape(-1, a.shape[-1])
    back = lambda k, a: jnp.transpose(a, (1, 0, 2)) if k == "conv_w" else a.reshape(w_d[k].shape)
    group = [k for k in names if k not in ("w_in", "w_out", "w_up", "w_down")]
    packed = [k for k in group if k != "conv_w"]
    res = _adamw([lay(k, w_d[k]) for k in group], [lay(k, g_d[k]) if k == "conv_w" else rows[k] for k in group],
                 [lay(k, m_d[k]) for k in group], [lay(k, v_d[k]) for k in group], gs_small, rows["loss"],
                 "adamw_small", after=(started,))
    for j, k in enumerate(packed):
        g_d[k] = back(k, res[j])
    for j, k in enumerate(group):
        delta[k], new_m[k], new_v[k] = (back(k, res[len(packed) + part * len(group) + j]) for part in range(3))

    arr_in = _exchange_wait(in_sems, in_src, in_land, done + [res[0]], "exchange_in_wait")
    big_adamw("w_in", own_in, arr_in, True, 4)

    loss = res[-1].reshape(())
    return (loss, grad_x[None], *[g_d[k] for k in names], *[delta[k] for k in names], *[new_m[k] for k in names],
            *[new_v[k] for k in names])
```

```python
import math

import numpy as np
import jax
import jax.numpy as jnp
from jax import lax
from jax.experimental import pallas as pl
from jax.experimental.pallas import tpu as pltpu

f32 = jnp.float32
bf16 = jnp.bfloat16

N_DEV = 8
T = 4096
D = 1024
CHUNK = 64
MIX_TILE = 512
RET_TILE = 256
HEADS = 4
DH = 128
RW = HEADS * DH
PW = 512
GROUPS = 4
WINDOWS = (2, 4, 8, 16)
IN_W = 4 * RW + PW
D_FF = 2816
LN_EPS = 1e-5
RMS_EPS = 1e-6
ALPHA = 2.0 ** 0.25
K_SCALE = DH ** -0.5

ADAM_LR = 0.001
ADAM_B1 = 0.9
ADAM_B2 = 0.999
ADAM_EPS = 1e-08
ADAM_WD = 0.01
ADAM_STEP = 10

ROWS_IN, ROWS_OUT, ROWS_UP, ROWS_DOWN = IN_W // N_DEV, D // N_DEV, 2 * D_FF // N_DEV, D_FF // N_DEV

V7X_VMEM_LIMIT = 56 * 2 ** 20
HALO = 32

NT = (((1,), (1,)), ((), ()))
TN = (((0,), (0,)), ((), ()))
NN = (((1,), (0,)), ((), ()))


def _dot(a, b, dims=NN):
    return lax.dot_general(a, b, dims, preferred_element_type=f32)


def _const_spec(shape):
    zeros = (0,) * len(shape)
    return pl.BlockSpec(shape, lambda i: zeros, pipeline_mode=pl.Buffered(1))


def _sigmoid(x):
    return 0.5 * jnp.tanh(0.5 * x) + 0.5


def _decay_tables(tt):
    h = np.arange(HEADS, dtype=np.float64)
    log_gamma = np.log(1.0 - 2.0 ** (-5.0 - h)).astype(np.float32).astype(np.float64)[:, None, None]
    idx = np.arange(tt, dtype=np.float64)
    visible = (idx[None, :] // CHUNK) <= (idx[:, None] // CHUNK)
    mask = np.where(visible[None], np.exp(log_gamma * np.abs(idx[:, None] - idx[None, :])[None]), 0.0)
    qd = np.broadcast_to(np.exp(log_gamma * (idx[None, :, None] + 1.0)), (HEADS, tt, DH))
    kd = np.broadcast_to(np.exp(log_gamma * (tt - 1.0 - idx[None, :, None])), (HEADS, tt, DH))
    cd = np.exp(log_gamma[:, 0, 0] * tt)
    return (jnp.asarray(mask, f32), jnp.asarray(qd, f32), jnp.asarray(kd, f32), [float(c) for c in cd])


def _rope_tables():
    inv_freq = (10000.0 ** (-np.arange(0, DH, 2, dtype=np.float64) / DH)).astype(np.float32)
    ang = (np.arange(T, dtype=np.float32)[:, None] * inv_freq[None, :]).astype(np.float64)
    cos, sin = np.cos(ang), np.sin(ang)
    return (jnp.asarray(np.concatenate([cos, cos], axis=1), f32), jnp.asarray(np.concatenate([-sin, sin], axis=1), f32))


def _swap_halves(t):
    return pltpu.roll(t, DH // 2, axis=1)


def _mix_forward(x, w_in_shard, w_out_shard, cos, sin, dmat, qd, kd, cdec, w_pool, pool_scale, ln1_g, ln1_b,
                 gather_bf16, gather, tt=MIX_TILE):
    n_tiles = T // tt
    to_bf16 = [w_in_shard, w_out_shard] + list(gather_bf16)
    n_c, n_g = len(to_bf16), len(gather_bf16) + len(gather)

    def body(x_ref, cos_ref, sin_ref, dmat_ref, qd_ref, kd_ref, wpool_ref, pscale_ref, g1_ref, b1_ref, *rest):
        f32_in, plain_in, rest = rest[:n_c], rest[n_c:2 + n_g], rest[2 + n_g:]
        qkv_ref, g_ref, oret_ref, states_ref, cat_ref, pooled_ref, xhat_ref, rstd_ref, x1b_ref, xb_ref = rest[:10]
        fout, gout = rest[10:12], rest[12:12 + n_g]
        state_s, pext_s, tmp_s, wint_s, wout_s, load_sems, stage_sems, *rest = rest[12 + n_g:]
        stage_s, cast_s, sems = rest[:n_c], rest[n_c:2 * n_c], rest[2 * n_c:]
        fin, gin, fsems, gsems = cast_s[:2], tuple(cast_s[2:]) + tuple(plain_in), sems[:3], sems[3:]
        i = pl.program_id(0)

        @pl.when(i == 0)
        def _():
            stage = [pltpu.make_async_copy(src, dst, stage_sems.at[j]) for j, (src, dst) in enumerate(zip(f32_in, stage_s))]
            for cp in stage:
                cp.start()
            state_s[...] = jnp.zeros_like(state_s)
            pext_s[:, pl.ds(0, HALO), :] = jnp.zeros((GROUPS, HALO, DH), f32)

            def cast(js):
                for j in js:
                    stage[j].wait()
                    cast_s[j][...] = stage_s[j][...].astype(bf16)

            _barrier(_gather_peers())
            cast(range(2))
            _gather_start(fin, fout, *fsems)
            cast(range(2, n_c))
            _gather_forward(fin, fout, *fsems)
            _gather_start(gin, gout, *gsems)
            _gather_finish(fin, fout, *fsems)
            loads = [pltpu.make_async_copy(src.at[s], dst.at[pl.ds(s * src.shape[1], src.shape[1]), :],
                                           load_sems.at[j, s])
                     for j, (src, dst) in enumerate(((fout[0], wint_s), (fout[1], wout_s))) for s in range(N_DEV)]
            for ld in loads:
                ld.start()
            for ld in loads:
                ld.wait()

        @pl.when(i == n_tiles - 3)
        def _():
            _gather_forward(gin, gout, *gsems)

        xb = x_ref[...].astype(bf16)
        xb_ref[...] = xb
        cos_t, sin_t = cos_ref[...], sin_ref[...]
        for part in range(2):
            pr = _dot(xb, wint_s[pl.ds(part * RW, RW), :], NT)
            for h in range(HEADS):
                t = pr[:, h * DH:(h + 1) * DH]
                r = t * cos_t + _swap_halves(t) * sin_t
                if part == 1:
                    r = r * K_SCALE
                qkv_ref[:, part * RW + h * DH: part * RW + (h + 1) * DH] = r.astype(bf16)
        qkv_ref[:, 2 * RW:3 * RW] = _dot(xb, wint_s[pl.ds(2 * RW, RW), :], NT).astype(bf16)
        g_ref[...] = _dot(xb, wint_s[pl.ds(3 * RW, RW), :], NT)
        p = _dot(xb, wint_s[pl.ds(4 * RW, PW), :], NT)
        for gi in range(GROUPS):
            pext_s[gi, pl.ds(HALO, tt), :] = p[:, gi * DH:(gi + 1) * DH]

        for sub in range(tt // RET_TILE):
            rows = pl.ds(sub * RET_TILE, RET_TILE)
            for h in range(HEADS):
                q = qkv_ref[rows, h * DH:(h + 1) * DH]
                k = qkv_ref[rows, RW + h * DH: RW + (h + 1) * DH]
                v = qkv_ref[rows, 2 * RW + h * DH: 2 * RW + (h + 1) * DH]
                s = _dot(q, k, NT) * dmat_ref[h]
                st = state_s[h]
                stb = st.astype(bf16)
                states_ref[sub, h] = stb
                oret_ref[rows, h * DH:(h + 1) * DH] = (_dot(s.astype(bf16), v)
                                                      + _dot((q.astype(f32) * qd_ref[h]).astype(bf16), stb))
                state_s[h] = st * cdec[h] + _dot((k.astype(f32) * kd_ref[h]).astype(bf16), v, TN)

        for h in range(HEADS):
            sl = slice(h * DH, (h + 1) * DH)
            o = oret_ref[:, sl]
            r = lax.rsqrt(jnp.mean(o * o, axis=-1, keepdims=True) + RMS_EPS)
            gg = g_ref[:, sl]
            cat_ref[:, sl] = (o * r * (gg * _sigmoid(gg))).astype(bf16)

        pos1 = (i * tt + lax.broadcasted_iota(jnp.int32, (tt, 1), 0) + 1).astype(f32)
        for gi, w in enumerate(WINDOWS):
            sl = slice(gi * DH, (gi + 1) * DH)
            stages = int(math.log2(w))
            src = pext_s
            for s in range(stages):
                lo = HALO - 8 * (stages - 1 - s)
                n = tt + HALO - lo
                shift = 2 ** s
                val = src[gi, pl.ds(lo, n), :] + src[gi, pl.ds(lo - shift, n), :]
                if s == stages - 1:
                    wsum = val
                else:
                    tmp_s[gi, pl.ds(lo, n), :] = val
                    src = tmp_s
            p_g = pext_s[gi, pl.ds(HALO, tt), :]
            pooled = (wsum / jnp.minimum(pos1, float(w)) - p_g).astype(bf16)
            pooled_ref[:, sl] = pooled
            y = _dot(pooled, wpool_ref[gi].astype(bf16)) * pscale_ref[:, sl]
            cat_ref[:, RW + gi * DH: RW + (gi + 1) * DH] = y.astype(bf16)
        pext_s[:, pl.ds(0, HALO), :] = pext_s[:, pl.ds(tt, HALO), :]

        z = ALPHA * x_ref[...] + _dot(cat_ref[...], wout_s[...])
        mu = jnp.mean(z, axis=-1, keepdims=True)
        zc = z - mu
        rstd = lax.rsqrt(jnp.mean(zc * zc, axis=-1, keepdims=True) + LN_EPS)
        xhat = zc * rstd
        xhat_ref[...] = xhat
        rstd_ref[...] = rstd
        x1b_ref[...] = (xhat * g1_ref[...] + b1_ref[...]).astype(bf16)

        @pl.when(i == n_tiles - 1)
        def _():
            _gather_finish(gin, gout, *gsems)

    tile = lambda w: pl.BlockSpec((tt, w), lambda i: (i, 0))
    hbm = pl.BlockSpec(memory_space=pltpu.HBM)
    out_shape = (
        jax.ShapeDtypeStruct((T, 3 * RW), bf16),
        jax.ShapeDtypeStruct((T, RW), f32),
        jax.ShapeDtypeStruct((T, RW), f32),
        jax.ShapeDtypeStruct((T // RET_TILE, HEADS, DH, DH), bf16),
        jax.ShapeDtypeStruct((T, D), bf16),
        jax.ShapeDtypeStruct((T, PW), bf16),
        jax.ShapeDtypeStruct((T, D), f32),
        jax.ShapeDtypeStruct((T, 1), f32),
        jax.ShapeDtypeStruct((T, D), bf16),
        jax.ShapeDtypeStruct((T, D), bf16),
    ) + tuple(jax.ShapeDtypeStruct((N_DEV,) + b.shape, bf16) for b in to_bf16
              ) + tuple(jax.ShapeDtypeStruct((N_DEV,) + b.shape, b.dtype) for b in gather)
    return pl.pallas_call(
        body, name="mix_forward", grid=(n_tiles,), out_shape=out_shape,
        in_specs=[tile(D), tile(DH), tile(DH),
                  _const_spec((HEADS, RET_TILE, RET_TILE)), _const_spec((HEADS, RET_TILE, DH)),
                  _const_spec((HEADS, RET_TILE, DH)),
                  _const_spec((GROUPS, DH, DH)), _const_spec((1, PW)),
                  _const_spec((1, D)), _const_spec((1, D))] + [hbm] * (2 + n_g),
        out_specs=(tile(3 * RW), tile(RW), tile(RW),
                   pl.BlockSpec((tt // RET_TILE, HEADS, DH, DH), lambda i: (i, 0, 0, 0)),
                   tile(D), tile(PW), tile(D), tile(1), tile(D), tile(D)) + (hbm,) * (2 + n_g),
        scratch_shapes=[pltpu.VMEM((HEADS, DH, DH), f32), pltpu.VMEM((GROUPS, tt + HALO, DH), f32),
                        pltpu.VMEM((GROUPS, tt + HALO, DH), f32), pltpu.VMEM((IN_W, D), bf16), pltpu.VMEM((D, D), bf16),
                        pltpu.SemaphoreType.DMA((2, N_DEV)), pltpu.SemaphoreType.DMA((n_c,))]
        + [pltpu.VMEM(b.shape, f32) for b in to_bf16] + [pltpu.VMEM(b.shape, bf16) for b in to_bf16]
        + _gather_sems(2) + _gather_sems(n_g),
        compiler_params=pltpu.CompilerParams(dimension_semantics=("arbitrary",), vmem_limit_bytes=V7X_VMEM_LIMIT,
                                             collective_id=GATHER_BARRIER),
    )(x, cos, sin, dmat, qd, kd, w_pool, pool_scale, ln1_g, ln1_b, *to_bf16, *gather)


def _ffn_forward_backward(xhat1, rstd1, ln1_g, ln1_b, w_up_t, conv_w, conv_b, w_down, ln2_g, ln2_b, target,
                          tt=256):
    n_tiles = T // tt
    FH = 16
    hb = tt // FH

    def body(xhat_ref, halo_ref, rstd_ref, g1_ref, b1_ref, wupt_ref, cw_ref, cb_ref, wdown_ref, g2_ref, b2_ref, tgt_ref,
             dz1_ref, dz2b_ref, du_ref, f_ref, loss_ref, dg2_ref, db2_ref, dg1_ref, db1_ref, dcb_ref, dcw_ref,
             gext_s, val_s, dhext_s):
        i = pl.program_id(0)
        tile_idx = n_tiles - 1 - i

        def rd(ref, off):
            return jnp.concatenate([ref[k, pl.ds(off, tt), :] for k in range(D_FF // 128)], axis=1)

        def wr(ref, val):
            for k in range(D_FF // 128):
                ref[k, pl.ds(0, val.shape[0]), :] = val[:, k * 128:(k + 1) * 128]

        @pl.when(i == 0)
        def _():
            for r in (loss_ref, dg2_ref, db2_ref, dg1_ref, db1_ref, dcb_ref, dcw_ref):
                r[...] = jnp.zeros_like(r)
            dhext_s[:, pl.ds(tt, 8), :] = jnp.zeros((D_FF // 128, 8, 128), f32)

        g1, b1 = g1_ref[...], b1_ref[...]
        xhat = xhat_ref[...]
        x1 = xhat * g1 + b1
        x1b = x1.astype(bf16)
        x1h = ((halo_ref[...] * g1 + b1) * jnp.where(tile_idx == 0, 0.0, 1.0)).astype(bf16)
        x1ext = jnp.concatenate([x1h, x1b], axis=0)

        val = _dot(x1b, wupt_ref[pl.ds(0, D_FF), :], NT)
        gate_ext = _dot(x1ext, wupt_ref[pl.ds(D_FF, D_FF), :], NT)
        wr(gext_s, gate_ext)
        hh = (cb_ref[...] + cw_ref[0:1, :] * rd(gext_s, FH - 2) + cw_ref[1:2, :] * rd(gext_s, FH - 1)
              + cw_ref[2:3, :] * gate_ext[FH:])
        sg = _sigmoid(hh)
        act = hh * sg
        wr(dhext_s, act)
        val_s[...] = val * (sg + act * (1.0 - sg))
        fb = (act * val).astype(bf16)
        f_ref[...] = fb

        z = ALPHA * x1 + _dot(fb, wdown_ref[...])
        mu = jnp.mean(z, axis=-1, keepdims=True)
        zc = z - mu
        rstd2 = lax.rsqrt(jnp.mean(zc * zc, axis=-1, keepdims=True) + LN_EPS)
        xh2 = zc * rstd2
        diff = xh2 * g2_ref[...] + b2_ref[...] - tgt_ref[...]
        loss_ref[...] += 0.5 * jnp.sum(diff * diff) / D
        dy = diff * (1.0 / D)
        dg2_ref[...] += jnp.sum(dy * xh2, axis=0, keepdims=True)
        db2_ref[...] += jnp.sum(dy, axis=0, keepdims=True)
        dyg = dy * g2_ref[...]
        dz2 = rstd2 * (dyg - jnp.mean(dyg, axis=-1, keepdims=True) - xh2 * jnp.mean(dyg * xh2, axis=-1, keepdims=True))
        dz2b = dz2.astype(bf16)
        dz2b_ref[...] = dz2b

        df = _dot(dz2b, wdown_ref[...], NT)
        dval = df * rd(dhext_s, 0)
        dh = df * val_s[...]
        wr(dhext_s, dh)
        dh1, dh2, g0 = rd(dhext_s, 1), rd(dhext_s, 2), rd(gext_s, FH)
        dcb_ref[...] += jnp.sum(dh, axis=0, keepdims=True)
        dcw_ref[0:1, :] += jnp.sum(dh2 * g0, axis=0, keepdims=True)
        dcw_ref[1:2, :] += jnp.sum(dh1 * g0, axis=0, keepdims=True)
        dcw_ref[2:3, :] += jnp.sum(dh * g0, axis=0, keepdims=True)
        dgate = cw_ref[2:3, :] * dh + cw_ref[1:2, :] * dh1 + cw_ref[0:1, :] * dh2
        dvalb, dgateb = dval.astype(bf16), dgate.astype(bf16)
        du_ref[:, :D_FF] = dvalb
        du_ref[:, D_FF:] = dgateb
        dx1 = ALPHA * dz2 + _dot(dvalb, wupt_ref[pl.ds(0, D_FF), :]) + _dot(dgateb, wupt_ref[pl.ds(D_FF, D_FF), :])
        dhext_s[:, pl.ds(tt, 8), :] = dhext_s[:, pl.ds(0, 8), :]

        dg1_ref[...] += jnp.sum(dx1 * xhat, axis=0, keepdims=True)
        db1_ref[...] += jnp.sum(dx1, axis=0, keepdims=True)
        dxg = dx1 * g1
        dz1_ref[...] = rstd_ref[...] * (dxg - jnp.mean(dxg, axis=-1, keepdims=True)
                                        - xhat * jnp.mean(dxg * xhat, axis=-1, keepdims=True))

    rtile = lambda w: pl.BlockSpec((tt, w), lambda i: (n_tiles - 1 - i, 0))
    acc = lambda shape: pl.BlockSpec(shape, lambda i: (0, 0))
    out_shape = (
        jax.ShapeDtypeStruct((T, D), f32),
        jax.ShapeDtypeStruct((T, D), bf16),
        jax.ShapeDtypeStruct((T, 2 * D_FF), bf16),
        jax.ShapeDtypeStruct((T, D_FF), bf16),
        jax.ShapeDtypeStruct((8, 128), f32),
        jax.ShapeDtypeStruct((1, D), f32), jax.ShapeDtypeStruct((1, D), f32),
        jax.ShapeDtypeStruct((1, D), f32), jax.ShapeDtypeStruct((1, D), f32),
        jax.ShapeDtypeStruct((1, D_FF), f32), jax.ShapeDtypeStruct((3, D_FF), f32),
    )
    return pl.pallas_call(
        body, name="ffn_forward_backward", grid=(n_tiles,), out_shape=out_shape,
        in_specs=[rtile(D),
                  pl.BlockSpec((FH, D), lambda i: (jnp.maximum((n_tiles - 1 - i) * hb - 1, 0), 0)),
                  rtile(1), _const_spec((1, D)), _const_spec((1, D)), _const_spec((2 * D_FF, D)),
                  _const_spec((3, D_FF)), _const_spec((1, D_FF)), _const_spec((D_FF, D)),
                  _const_spec((1, D)), _const_spec((1, D)), rtile(D)],
        out_specs=(rtile(D), rtile(D), rtile(2 * D_FF), rtile(D_FF), acc((8, 128)),
                   acc((1, D)), acc((1, D)), acc((1, D)), acc((1, D)), acc((1, D_FF)), acc((3, D_FF))),
        scratch_shapes=[pltpu.VMEM((D_FF // 128, tt + FH, 128), f32), pltpu.VMEM((tt, D_FF), f32),
                        pltpu.VMEM((D_FF // 128, tt + 8, 128), f32)],
        compiler_params=pltpu.CompilerParams(dimension_semantics=("arbitrary",), vmem_limit_bytes=V7X_VMEM_LIMIT),
    )(xhat1, xhat1, rstd1, ln1_g, ln1_b, w_up_t, conv_w, conv_b, w_down, ln2_g, ln2_b, target)


def _mix_backward(dz1, w_out, qkv, g, oret, states, pooled, cat, cos, sin, dmat, qd, kd, cdec, w_pool, pool_scale, w_in_t,
                  small_ffn, after, tt=MIX_TILE):
    n_tiles = T // tt

    def body(dz1_ref, wout_ref, qkv_ref, g_ref, oret_ref, states_ref, pooled_ref, cat_ref, cos_ref, sin_ref, dmat_ref,
             qd_ref, kd_ref, wpool_ref, pscale_ref, wint_ref, *rest):
        ffn_refs, rest = rest[:len(SMALL_FFN)], rest[len(SMALL_FFN):]
        after_ref, dproj_ref, gx_ref, small_ref, dwout_ref, dstate_s, dout_s, eext_s, tmp_s, dwout_s, dpscale_s = rest
        i = pl.program_id(0)
        tile_idx = n_tiles - 1 - i

        @pl.when(i == 0)
        def _():
            dstate_s[...] = jnp.zeros_like(dstate_s)
            small_ref[...] = jnp.zeros_like(small_ref)
            dpscale_s[...] = jnp.zeros_like(dpscale_s)
            dwout_s[...] = jnp.zeros_like(dwout_s)
            eext_s[:, pl.ds(tt, HALO), :] = jnp.zeros((GROUPS, HALO, DH), f32)

        dz1 = dz1_ref[...]
        dz1b = dz1.astype(bf16)
        dcat = _dot(dz1b, wout_ref[...], NT)
        dwout_s[...] += _dot(cat_ref[...], dz1b, TN)

        pos1 = (tile_idx * tt + lax.broadcasted_iota(jnp.int32, (tt, 1), 0) + 1).astype(f32)
        for gi, w in enumerate(WINDOWS):
            sl = slice(gi * DH, (gi + 1) * DH)
            dpo = dcat[:, RW + gi * DH: RW + (gi + 1) * DH]
            pooled_g = pooled_ref[:, sl]
            wpool_g = wpool_ref[gi].astype(bf16)
            ylin = _dot(pooled_g, wpool_g)
            dpscale_s[:, sl] += jnp.sum(dpo * ylin, axis=0, keepdims=True)
            dpw = (dpo * pscale_ref[:, sl]).astype(bf16)
            small_ref[pl.ds(gi * DH, DH), :] += _dot(pooled_g, dpw, TN)
            dpooled = _dot(dpw, wpool_g, NT)
            eext_s[gi, pl.ds(0, tt), :] = dpooled / jnp.minimum(pos1, float(w))
            stages = int(math.log2(w))
            src = eext_s
            for s in range(stages):
                n = tt + 8 * (stages - 1 - s)
                shift = 2 ** s
                val = src[gi, pl.ds(0, n), :] + src[gi, pl.ds(shift, n), :]
                if s == stages - 1:
                    wsum = val
                else:
                    tmp_s[gi, pl.ds(0, n), :] = val
                    src = tmp_s
            dproj_ref[:, 4 * RW + gi * DH: 4 * RW + (gi + 1) * DH] = (wsum - dpooled).astype(bf16)
        eext_s[:, pl.ds(tt, HALO), :] = eext_s[:, pl.ds(0, HALO), :]

        for h in range(HEADS):
            sl = slice(h * DH, (h + 1) * DH)
            dr = dcat[:, sl]
            o = oret_ref[:, sl]
            r = lax.rsqrt(jnp.mean(o * o, axis=-1, keepdims=True) + RMS_EPS)
            rn = o * r
            gg = g_ref[:, sl]
            sg = _sigmoid(gg)
            dproj_ref[:, 3 * RW + h * DH: 3 * RW + (h + 1) * DH] = (dr * rn * (sg * (1.0 + gg * (1.0 - sg)))).astype(bf16)
            drn = dr * (gg * sg)
            dout_s[:, sl] = (r * (drn - rn * jnp.mean(drn * rn, axis=-1, keepdims=True))).astype(bf16)

        for sub in reversed(range(tt // RET_TILE)):
            rows = pl.ds(sub * RET_TILE, RET_TILE)
            cos_t, sin_t = cos_ref[rows, :], sin_ref[rows, :]
            for h in range(HEADS):
                q = qkv_ref[rows, h * DH:(h + 1) * DH]
                k = qkv_ref[rows, RW + h * DH: RW + (h + 1) * DH]
                v = qkv_ref[rows, 2 * RW + h * DH: 2 * RW + (h + 1) * DH]
                do = dout_s[rows, h * DH:(h + 1) * DH]
                stb = states_ref[sub, h]
                dst = dstate_s[h]
                dstb = dst.astype(bf16)
                sb = (_dot(q, k, NT) * dmat_ref[h]).astype(bf16)
                dsb = (_dot(do, v, NT) * dmat_ref[h]).astype(bf16)
                dq = _dot(dsb, k) + _dot(do, stb, NT) * qd_ref[h]
                dk = _dot(dsb, q, TN) + _dot(v, dstb, NT) * kd_ref[h]
                dv = _dot(sb, do, TN) + _dot((k.astype(f32) * kd_ref[h]).astype(bf16), dstb)
                dstate_s[h] = dst * cdec[h] + _dot((q.astype(f32) * qd_ref[h]).astype(bf16), do, TN)
                dproj_ref[rows, h * DH:(h + 1) * DH] = (dq * cos_t - _swap_halves(dq) * sin_t).astype(bf16)
                dproj_ref[rows, RW + h * DH: RW + (h + 1) * DH] = (
                    (dk * cos_t - _swap_halves(dk) * sin_t) * K_SCALE).astype(bf16)
                dproj_ref[rows, 2 * RW + h * DH: 2 * RW + (h + 1) * DH] = dv.astype(bf16)

        gx_ref[...] = ALPHA * dz1 + _dot(dproj_ref[...], wint_ref[...])

        @pl.when(i == n_tiles - 1)
        def _():
            dwout_ref[...] = dwout_s[...].astype(bf16)
            at = GROUPS * DH
            for ref, size in [(dpscale_s, PW)] + [(ref, size) for ref, (_, size) in zip(ffn_refs, SMALL_FFN)]:
                for j in range(size // 128):
                    r, k = divmod(j, ref.shape[1] // 128)
                    small_ref[at + j: at + j + 1, :] = ref[r:r + 1, k * 128:(k + 1) * 128]
                at = SMALL_FFN_AT if ref is dpscale_s else at + size // 128

    rtile = lambda w: pl.BlockSpec((tt, w), lambda i: (n_tiles - 1 - i, 0))
    out_shape = (
        jax.ShapeDtypeStruct((T, IN_W), bf16),
        jax.ShapeDtypeStruct((T, D), f32),
        jax.ShapeDtypeStruct((SMALL_ROWS, 128), f32),
        jax.ShapeDtypeStruct((D, D), bf16),
    )
    return pl.pallas_call(
        body, name="mix_backward", grid=(n_tiles,), out_shape=out_shape,
        in_specs=[rtile(D), _const_spec((D, D)), rtile(3 * RW), rtile(RW), rtile(RW),
                  pl.BlockSpec((tt // RET_TILE, HEADS, DH, DH), lambda i: (n_tiles - 1 - i, 0, 0, 0)),
                  rtile(PW), rtile(D), rtile(DH), rtile(DH),
                  _const_spec((HEADS, RET_TILE, RET_TILE)), _const_spec((HEADS, RET_TILE, DH)),
                  _const_spec((HEADS, RET_TILE, DH)),
                  _const_spec((GROUPS, DH, DH)), _const_spec((1, PW)), _const_spec((IN_W, D)),
                  *[_const_spec(a.shape) for a in small_ffn], pl.BlockSpec(memory_space=pl.ANY)],
        out_specs=(rtile(IN_W), rtile(D), pl.BlockSpec((SMALL_ROWS, 128), lambda i: (0, 0)),
                   pl.BlockSpec((D, D), lambda i: (0, 0), pipeline_mode=pl.Buffered(1))),
        scratch_shapes=[pltpu.VMEM((HEADS, DH, DH), f32), pltpu.VMEM((tt, RW), bf16),
                        pltpu.VMEM((GROUPS, tt + HALO, DH), f32), pltpu.VMEM((GROUPS, tt + HALO, DH), f32),
                        pltpu.VMEM((D, D), f32), pltpu.VMEM((1, PW), f32)],
        compiler_params=pltpu.CompilerParams(dimension_semantics=("arbitrary",), vmem_limit_bytes=V7X_VMEM_LIMIT),
    )(dz1, w_out, qkv, g, oret, states, pooled, cat, cos, sin, dmat, qd, kd, w_pool, pool_scale, w_in_t, *small_ffn,
      after)


def _weight_grad(a, b, name, tm, reduce=()):
    m = a.shape[1]
    n_m, n_r = m // tm, len(reduce)
    assert not n_r or n_m >= 2

    def body(a_ref, b_ref, *rest):
        ins, o_ref, own, arrived = rest[:n_r], rest[n_r], rest[n_r + 1:2 * n_r + 1], rest[2 * n_r + 1:3 * n_r + 1]
        landing, mine, sems = rest[3 * n_r + 1:4 * n_r + 1], rest[4 * n_r + 1:5 * n_r + 1], rest[5 * n_r + 1:]
        i = pl.program_id(0)

        if n_r:
            pair_send, pair_recv, local_sems, chip_send, chip_recv = sems
            me = _me()
            x, y, c = me
            sibling = (x, y, 1 - c)

            def pair_copy(k, j):
                return _remote(ins[j].at[_slot(*_chip(me, k), 1 - c)], landing[j].at[k], pair_send.at[k, j],
                               pair_recv.at[k, j], sibling)

            def load(k, j):
                return pltpu.make_async_copy(ins[j].at[_slot(*_chip(me, k), c)], mine[j].at[k], local_sems.at[k, j])

            def store(j):
                return pltpu.make_async_copy(mine[j].at[0], own[j], local_sems.at[0, j])

            def chip_copy(k, j):
                return _remote(mine[j].at[k], arrived[j].at[k - 1], chip_send.at[k - 1, j], chip_recv.at[k - 1, j],
                               (*_chip(me, k), c))

            @pl.when(i == 0)
            def _():
                _barrier([sibling] + _chip_peers())
                for k in range(4):
                    for j in range(n_r):
                        pair_copy(k, j).start()
                        load(k, j).start()

            @pl.when(i == 1)
            def _():
                for k in range(4):
                    for j in range(n_r):
                        load(k, j).wait()
                        pair_copy(k, j).wait_recv()
                        mine[j][k] = (mine[j][k].astype(f32) + landing[j][k].astype(f32)).astype(mine[j].dtype)
                        (store(j) if k == 0 else chip_copy(k, j)).start()

        o_ref[...] = _dot(a_ref[...], b_ref[...].astype(bf16), TN).astype(bf16)

        if n_r:
            @pl.when(i == n_m - 1)
            def _():
                for j in range(n_r):
                    store(j).wait()
                    for k in range(1, 4):
                        chip_copy(k, j).wait_recv()
                for j in range(n_r):
                    for k in range(1, 4):
                        chip_copy(k, j).wait_send()
                    for k in range(4):
                        pair_copy(k, j).wait_send()

    hbm = pl.BlockSpec(memory_space=pltpu.HBM)
    return pl.pallas_call(
        body, name=name, grid=(n_m,),
        out_shape=(jax.ShapeDtypeStruct((m, D), bf16),)
        + tuple(jax.ShapeDtypeStruct(p.shape[1:], p.dtype) for p in reduce)
        + tuple(jax.ShapeDtypeStruct((3,) + p.shape[1:], p.dtype) for p in reduce),
        in_specs=[pl.BlockSpec((T, tm), lambda i: (0, i)),
                  pl.BlockSpec((T, D), lambda i: (0, 0), pipeline_mode=pl.Buffered(1))] + [hbm] * n_r,
        out_specs=(pl.BlockSpec((tm, D), lambda i: (i, 0)),) + (hbm,) * (2 * n_r),
        scratch_shapes=[pltpu.VMEM((4,) + p.shape[1:], p.dtype) for p in reduce] * 2
        + ([pltpu.SemaphoreType.DMA((4, n_r))] * 3 + [pltpu.SemaphoreType.DMA((3, n_r))] * 2 if n_r else []),
        compiler_params=pltpu.CompilerParams(dimension_semantics=("arbitrary",), vmem_limit_bytes=V7X_VMEM_LIMIT,
                                             collective_id=REDUCE_BARRIER if n_r else None),
    )(a, b, *reduce)


CHIP_FLIPS = ((1, 0), (0, 1), (1, 1))
PAIR_BARRIER, CHIP_BARRIER, GATHER_BARRIER, CHIP_BARRIER_SPLIT, REDUCE_BARRIER = 0, 1, 2, 3, 4


def _barrier(peers):
    sem = pltpu.get_barrier_semaphore()
    for peer in peers:
        pl.semaphore_signal(sem, inc=1, device_id=peer, device_id_type=pl.DeviceIdType.MESH)
    pl.semaphore_wait(sem, len(peers))


def _me():
    return lax.axis_index("x"), lax.axis_index("y"), lax.axis_index("c")


def _chip(me, k):
    x, y, _ = me
    if k == 0:
        return x, y
    fx, fy = CHIP_FLIPS[k - 1]
    return (1 - x if fx else x), (1 - y if fy else y)


def _slot(x, y, c):
    return 4 * x + 2 * y + c


def _remote(src, dst, send_sem, recv_sem, to):
    return pltpu.make_async_remote_copy(src_ref=src, dst_ref=dst, send_sem=send_sem, recv_sem=recv_sem,
                                        device_id=to, device_id_type=pl.DeviceIdType.MESH)


def _gather_sems(n):
    return [pltpu.SemaphoreType.DMA((7, n)), pltpu.SemaphoreType.DMA((7, n)), pltpu.SemaphoreType.DMA((n,))] if n else []


def _gather_copy(k, j, gin, gout, send_sems, recv_sems, sending):
    x, y, c = _me()
    sibling, x_chip, y_chip, d_chip = (x, y, 1 - c), (1 - x, y), (x, 1 - y), (1 - x, 1 - y)
    south = c == 0
    passed_on = (jnp.where(south, 1 - x, x), jnp.where(south, y, 1 - y), c)
    src, to = gin[j], sibling
    if sending:
        block = {0: (x, y, c), 1: (x, y, c), 2: (x, y, c), 3: passed_on, 4: (*x_chip, c), 5: (*y_chip, c), 6: (*d_chip, c)}[k]
        to = {1: (*x_chip, c), 2: (*y_chip, c), 3: (jnp.where(south, x, 1 - x), jnp.where(south, 1 - y, y), c)}.get(k, sibling)
        if k >= 3:
            src = gout[j].at[_slot(*block)]
    else:
        block = {0: sibling, 1: (*x_chip, c), 2: (*y_chip, c), 3: (*d_chip, c), 4: (*x_chip, 1 - c), 5: (*y_chip, 1 - c),
                 6: (*d_chip, 1 - c)}[k]
    return _remote(src, gout[j].at[_slot(*block)], send_sems.at[k, j], recv_sems.at[k, j], to)


def _gather_do(ks, action, gin, gout, send_sems, recv_sems):
    for k in ks:
        for j in range(len(gin)):
            cp = _gather_copy(k, j, gin, gout, send_sems, recv_sems, action != "wait_recv")
            getattr(cp, action)()


def _gather_peers():
    x, y, c = _me()
    return [(x, y, 1 - c), (1 - x, y, c), (x, 1 - y, c)]


def _gather_start(gin, gout, send_sems, recv_sems, local_sems):
    for j in range(len(gin)):
        pltpu.make_async_copy(gin[j], gout[j].at[_slot(*_me())], local_sems.at[j]).start()
    _gather_do((0, 1, 2), "start", gin, gout, send_sems, recv_sems)


def _gather_forward(gin, gout, send_sems, recv_sems, local_sems):
    _gather_do((1, 2), "wait_recv", gin, gout, send_sems, recv_sems)
    _gather_do((3, 4, 5), "start", gin, gout, send_sems, recv_sems)


def _gather_finish(gin, gout, send_sems, recv_sems, local_sems):
    _gather_do((3,), "wait_recv", gin, gout, send_sems, recv_sems)
    _gather_do((6,), "start", gin, gout, send_sems, recv_sems)
    _gather_do((0, 4, 5, 6), "wait_recv", gin, gout, send_sems, recv_sems)
    _gather_do(range(7), "wait_send", gin, gout, send_sems, recv_sems)
    for j in range(len(gin)):
        pltpu.make_async_copy(gin[j], gout[j].at[_slot(*_me())], local_sems.at[j]).wait()


def _pair_reduce(parts, name, gather_sum=None):
    n = len(parts)
    n_h = 0 if gather_sum is None else 1

    def body(*refs):
        ins, g_terms, refs = refs[:n], refs[n:n + 2 * n_h], refs[n + 2 * n_h:]
        own, others, g_out, refs = refs[:n], refs[n:2 * n], refs[2 * n:2 * n + n_h], refs[2 * n + n_h:]
        landing, mine, (send_sems, recv_sems, local_sems), g_scratch = refs[:n], refs[n:2 * n], refs[2 * n:2 * n + 3], refs[2 * n + 3:]
        me = _me()
        x, y, c = me
        sibling = (x, y, 1 - c)
        _barrier(_gather_peers() if n_h else [sibling])
        if n_h:
            piece_s, g_sems = g_scratch[0], g_scratch[1:]
            acc = g_terms[0][...].astype(f32)
            for k in range(3):
                acc = acc + g_terms[1][k].astype(f32)
            piece_s[...] = acc
            _gather_start([piece_s], g_out, *g_sems)
        sends, loads = [], []
        for k in range(4):
            for j in range(n):
                cp = _remote(ins[j].at[_slot(*_chip(me, k), 1 - c)], landing[j].at[k], send_sems.at[k, j],
                             recv_sems.at[k, j], sibling)
                cp.start()
                sends.append(cp)
                ld = pltpu.make_async_copy(ins[j].at[_slot(*_chip(me, k), c)], mine[j].at[k], local_sems.at[k, j])
                ld.start()
                loads.append(ld)
        if n_h:
            _gather_forward([piece_s], g_out, *g_sems)
        stores = []
        for k in range(4):
            for j in range(n):
                loads[k * n + j].wait()
                _remote(ins[j].at[0], landing[j].at[k], send_sems.at[k, j], recv_sems.at[k, j], sibling).wait_recv()
                mine[j][k] = (mine[j][k].astype(f32) + landing[j][k].astype(f32)).astype(mine[j].dtype)
                st = pltpu.make_async_copy(mine[j].at[k], own[j] if k == 0 else others[j].at[k - 1], local_sems.at[k, j])
                st.start()
                stores.append(st)
        for cp in sends:
            cp.wait_send()
        for st in stores:
            st.wait()
        if n_h:
            _gather_finish([piece_s], g_out, *g_sems)

    vm, hbm = pl.BlockSpec(memory_space=pltpu.VMEM), pl.BlockSpec(memory_space=pltpu.HBM)
    g_shape = gather_sum[0].shape if n_h else ()
    return pl.pallas_call(
        body, name=name,
        out_shape=tuple(jax.ShapeDtypeStruct(p.shape[1:], p.dtype) for p in parts)
        + tuple(jax.ShapeDtypeStruct((3,) + p.shape[1:], p.dtype) for p in parts)
        + tuple([jax.ShapeDtypeStruct((N_DEV,) + g_shape, f32)] * n_h),
        in_specs=[hbm] * n + [vm] * (2 * n_h), out_specs=(hbm,) * (2 * n + n_h),
        scratch_shapes=[pltpu.VMEM((4,) + p.shape[1:], p.dtype) for p in parts] * 2
        + [pltpu.SemaphoreType.DMA((4, n)), pltpu.SemaphoreType.DMA((4, n)), pltpu.SemaphoreType.DMA((4, n))]
        + ([pltpu.VMEM(g_shape, f32)] + _gather_sems(1)) * n_h,
        compiler_params=pltpu.CompilerParams(vmem_limit_bytes=V7X_VMEM_LIMIT,
                                             collective_id=GATHER_BARRIER if n_h else PAIR_BARRIER),
    )(*parts, *(gather_sum or ()))


def _chip_peers():
    me = _me()
    return [(*_chip(me, k), me[2]) for k in range(1, 4)]


def _split_copies(src_ref, dst_ref, sems):
    me = _me()
    return [_remote(src_ref.at[k - 1], dst_ref.at[k - 1], sems[k - 1], sems[2 + k], (*_chip(me, k), me[2]))
            for k in range(1, 4)]


def _exchange_start(others, name, barrier_id):
    def body(src_ref, land_ref, *rest):
        sems, token_ref = rest[:6], rest[8]
        _barrier(_chip_peers())
        for copy in _split_copies(src_ref, land_ref, sems):
            copy.start()
        token_ref[...] = jnp.zeros_like(token_ref)

    hbm, sem = pl.BlockSpec(memory_space=pltpu.HBM), pl.BlockSpec(memory_space=pltpu.SEMAPHORE)
    thru = pltpu.HBM(others.shape, others.dtype)
    res = pl.pallas_call(
        body, name=name,
        out_shape=(pltpu.SemaphoreType.DMA(()),) * 6 + (thru, thru, jax.ShapeDtypeStruct((8, 128), f32)),
        in_specs=(hbm, hbm), out_specs=(sem,) * 6 + (hbm, hbm, pl.BlockSpec(memory_space=pltpu.VMEM)),
        input_output_aliases={0: 6, 1: 7},
        compiler_params=pltpu.CompilerParams(has_side_effects=pltpu.SideEffectType.DATAFLOW_SIDE_EFFECTING,
                                             collective_id=barrier_id),
    )(pltpu.with_memory_space_constraint(others, pltpu.HBM),
      pltpu.with_memory_space_constraint(lax.empty(others.shape, others.dtype), pltpu.HBM))
    return res[:6], res[6], res[7], res[8]


def _exchange_wait(sems, src_thru, land_thru, after, name):
    n_after = len(after)

    def body(src_ref, land_ref, *rest):
        for copy in _split_copies(src_ref, land_ref, rest[:6]):
            copy.wait_send()
            copy.wait_recv()

    hbm, sem = pl.BlockSpec(memory_space=pltpu.HBM), pl.BlockSpec(memory_space=pltpu.SEMAPHORE)
    thru = pltpu.HBM(src_thru.shape, src_thru.dtype)
    return pl.pallas_call(
        body, name=name, out_shape=(thru, thru),
        in_specs=(hbm, hbm) + (sem,) * 6 + (pl.BlockSpec(memory_space=pl.ANY),) * n_after, out_specs=(hbm, hbm),
        input_output_aliases={0: 0, 1: 1},
        compiler_params=pltpu.CompilerParams(has_side_effects=pltpu.SideEffectType.DATAFLOW_SIDE_EFFECTING),
    )(src_thru, land_thru, *sems, *after)[1]


def _adam_update(w, g, m, v):
    m = ADAM_B1 * m + (1.0 - ADAM_B1) * g
    v = ADAM_B2 * v + (1.0 - ADAM_B2) * (g * g)
    m_hat = m / (1.0 - ADAM_B1 ** ADAM_STEP)
    v_hat = v / (1.0 - ADAM_B2 ** ADAM_STEP)
    return -ADAM_LR * (m_hat / (jnp.sqrt(v_hat) + ADAM_EPS) + ADAM_WD * w), m, v


def _sum_adamw(own, arrived, w, m, v, name, steps, after=()):
    rows = own.shape[0]
    br = rows // steps

    def body(own_ref, arr_ref, w_ref, m_ref, v_ref, *rest):
        g_out, d_out, m_out, v_out = rest[len(after):]
        g = own_ref[...].astype(f32)
        for k in range(3):
            g = g + arr_ref[k].astype(f32)
        g_out[...] = g
        d_out[...], m_out[...], v_out[...] = _adam_update(w_ref[...], g, m_ref[...], v_ref[...])

    blk = pl.BlockSpec((br, D), lambda i: (i, 0))
    return pl.pallas_call(
        body, name=name, grid=(steps,), out_shape=(jax.ShapeDtypeStruct((rows, D), f32),) * 4,
        in_specs=[blk, pl.BlockSpec((3, br, D), lambda i: (0, i, 0)), blk, blk, blk]
        + [pl.BlockSpec(memory_space=pl.ANY)] * len(after), out_specs=(blk,) * 4,
        compiler_params=pltpu.CompilerParams(dimension_semantics=("parallel",), vmem_limit_bytes=V7X_VMEM_LIMIT),
    )(own, arrived, w, m, v, *after)


def _adamw(ws, gs, ms, vs, packed, scalar_row, name, after=()):
    n = len(ws)
    given = [g for g in gs if not isinstance(g, int)]
    taken = [j for j in range(n) if isinstance(gs[j], int)]

    def body(packed_ref, *refs):
        w_r, m_r, v_r = (refs[k * n:(k + 1) * n] for k in range(3))
        given_r, outs = list(refs[3 * n:3 * n + len(given)]), refs[3 * n + len(given) + len(after):]
        g_o, outs = dict(zip(taken, outs[:len(taken)])), outs[len(taken):]
        d_o, m_o, v_o = (outs[k * n:(k + 1) * n] for k in range(3))
        outs[3 * n][...] = packed_ref[scalar_row:scalar_row + 1, 0:1]
        for j in range(n):
            if j in g_o:
                (r, c), at = ws[j].shape, gs[j]
                if c == 128:
                    g = packed_ref[at:at + r, :]
                else:
                    assert r == 1
                    g = jnp.concatenate([packed_ref[at + k:at + k + 1, :] for k in range(c // 128)], axis=1)
                g_o[j][...] = g
            else:
                g = given_r.pop(0)[...]
            d_o[j][...], m_o[j][...], v_o[j][...] = _adam_update(w_r[j][...], g, m_r[j][...], v_r[j][...])

    vm = pl.BlockSpec(memory_space=pltpu.VMEM)
    shapes = tuple(jax.ShapeDtypeStruct(w.shape, f32) for w in ws)
    n_out = len(taken) + 3 * n + 1
    return pl.pallas_call(
        body, name=name,
        out_shape=tuple(shapes[j] for j in taken) + shapes * 3 + (jax.ShapeDtypeStruct((1, 1), f32),),
        in_specs=[vm] * (1 + 3 * n + len(given)) + [pl.BlockSpec(memory_space=pl.ANY)] * len(after),
        out_specs=tuple([vm] * n_out),
        compiler_params=pltpu.CompilerParams(vmem_limit_bytes=V7X_VMEM_LIMIT),
    )(packed, *ws, *ms, *vs, *given, *after)


SMALL_FFN = (("ln1_g", D), ("ln1_b", D), ("ln2_g", D), ("ln2_b", D), ("conv_b", D_FF), ("conv_w", 3 * D_FF), ("loss", 128))
SMALL_FFN_AT = 520
SMALL_ROWS = 704


def _small_rows():
    rows, at = {"w_pool": 0, "pool_scale": GROUPS * DH}, SMALL_FFN_AT
    for k, size in SMALL_FFN:
        rows[k] = at
        at += size // 128
    return rows


def kernel(x, w_in, w_pool, pool_scale, w_out, ln1_g, ln1_b, w_up, conv_w, conv_b, w_down, ln2_g, ln2_b, loss_target, m_w_in, m_w_pool, m_pool_scale, m_w_out, m_ln1_g, m_ln1_b, m_w_up, m_conv_w, m_conv_b, m_w_down, m_ln2_g, m_ln2_b, v_w_in, v_w_pool, v_pool_scale, v_w_out, v_ln1_g, v_ln1_b, v_w_up, v_conv_w, v_conv_b, v_w_down, v_ln2_g, v_ln2_b):
    me = 4 * lax.axis_index("x") + 2 * lax.axis_index("y") + lax.axis_index("c")
    x2, tgt = x[0], loss_target[0]

    cos, sin = _rope_tables()
    dmat, qd, kd, cdec = _decay_tables(RET_TILE)

    qkv, g, oret, states, cat, pooled, xhat1, rstd1, x1b, xb, g_in, g_out, g_up, g_down, g_cw = _mix_forward(
        x2, w_in[0].T, w_out[0], cos, sin, dmat, qd, kd, cdec, w_pool[0], pool_scale, ln1_g, ln1_b,
        gather_bf16=[w_up[0].T, w_down[0]], gather=[jnp.transpose(conv_w, (1, 0, 2))])
    w_in_t = g_in.reshape(IN_W, D)
    w_out_f = g_out.reshape(D, D)
    w_up_t = g_up.reshape(2 * D_FF, D)
    w_down_f = g_down.reshape(D_FF, D)
    conv_w_f = jnp.transpose(g_cw[:, :, 0, :], (1, 0, 2)).reshape(3, D_FF)
    dz1, dz2b, du, f, loss8, d_ln2_g, d_ln2_b, d_ln1_g, d_ln1_b, d_conv_b, d_conv_w = _ffn_forward_backward(
        xhat1, rstd1, ln1_g, ln1_b, w_up_t, conv_w_f, conv_b, w_down_f, ln2_g, ln2_b, tgt)
    small_ffn = [d_ln1_g, d_ln1_b, d_ln2_g, d_ln2_b, d_conv_b, d_conv_w, loss8]

    (dw_down,) = _weight_grad(f, dz2b, "grad_w_down", tm=D_FF // 2)
    dw_up_t, own_down, arr_down = _weight_grad(du, x1b, "grad_w_up", tm=D_FF // 2,
                                               reduce=[dw_down.reshape(N_DEV, ROWS_DOWN, D)])
    own_up, oth_up = _pair_reduce([dw_up_t.reshape(N_DEV, ROWS_UP, D)], "pair_reduce_up")
    up_sems, up_src, up_land, up_started = _exchange_start(oth_up, "exchange_up_start", CHIP_BARRIER_SPLIT)
    dproj, grad_x, small, dw_out = _mix_backward(
        dz1, w_out_f, qkv, g, oret, states, pooled, cat, cos, sin, dmat, qd, kd, cdec, w_pool[0], pool_scale, w_in_t,
        small_ffn, after=up_started)
    dw_in_t, own_out, own_small, arr_out, arr_small = _weight_grad(
        dproj, xb, "grad_w_in", tm=IN_W // 2,
        reduce=[dw_out.reshape(N_DEV, ROWS_OUT, D), small.reshape(N_DEV, SMALL_ROWS // N_DEV, 128)])
    arr_up = _exchange_wait(up_sems, up_src, up_land, [dw_in_t], "exchange_up_wait")
    own_in, oth_in, gs_small = _pair_reduce([dw_in_t.reshape(N_DEV, ROWS_IN, D)], "pair_reduce_in",
                                            gather_sum=(own_small, arr_small))
    in_sems, in_src, in_land, started = _exchange_start(oth_in, "exchange_in_start", CHIP_BARRIER)

    names = ["w_in", "w_pool", "pool_scale", "w_out", "ln1_g", "ln1_b", "w_up", "conv_w", "conv_b", "w_down",
             "ln2_g", "ln2_b"]
    w_d = dict(w_in=w_in, w_pool=w_pool, pool_scale=pool_scale, w_out=w_out, ln1_g=ln1_g, ln1_b=ln1_b, w_up=w_up,
               conv_w=conv_w, conv_b=conv_b, w_down=w_down, ln2_g=ln2_g, ln2_b=ln2_b)
    m_d = dict(w_in=m_w_in, w_pool=m_w_pool, pool_scale=m_pool_scale, w_out=m_w_out, ln1_g=m_ln1_g, ln1_b=m_ln1_b,
               w_up=m_w_up, conv_w=m_conv_w, conv_b=m_conv_b, w_down=m_w_down, ln2_g=m_ln2_g, ln2_b=m_ln2_b)
    v_d = dict(w_in=v_w_in, w_pool=v_w_pool, pool_scale=v_pool_scale, w_out=v_w_out, ln1_g=v_ln1_g, ln1_b=v_ln1_b,
               w_up=v_w_up, conv_w=v_conv_w, conv_b=v_conv_b, w_down=v_w_down, ln2_g=v_ln2_g, ln2_b=v_ln2_b)
    g_d, delta, new_m, new_v = {}, {}, {}, {}

    def big_adamw(k, own, arr, transposed, steps, after=()):
        lay = (lambda a: a[0].T) if transposed else (lambda a: a[0])
        back = (lambda a: a.T[None]) if transposed else (lambda a: a[None])
        res = _sum_adamw(own, arr, lay(w_d[k]), lay(m_d[k]), lay(v_d[k]), "adamw_" + k, steps, after)
        g_d[k], delta[k], new_m[k], new_v[k] = (back(r) for r in res)
        return res[3]

    done = [big_adamw("w_up", own_up, arr_up, True, 4, after=(started,)),
            big_adamw("w_down", own_down, arr_down, False, 2, after=(started,)),
            big_adamw("w_out", own_out, arr_out, False, 2, after=(started,))]

    gs_small, rows = gs_small.reshape(SMALL_ROWS, 128), _small_rows()
    g_conv_w = gs_small[rows["conv_w"]:rows["conv_w"] + 3 * D_FF // 128].reshape(3, D_FF)
    g_d["conv_w"] = lax.dynamic_slice(g_conv_w, (0, me * (D_FF // N_DEV)), (3, D_FF // N_DEV))[None]
    lay = lambda k, a: jnp.transpose(a, (1, 0, 2)) if k == "conv_w" else a.reshape(-1, a.shape[-1])
    back = lambda k, a: jnp.transpose(a, (1, 0, 2)) if k == "conv_w" else a.reshape(w_d[k].shape)
    group = [k for k in names if k not in ("w_in", "w_out", "w_up", "w_down")]
    packed = [k for k in group if k != "conv_w"]
    res = _adamw([lay(k, w_d[k]) for k in group], [lay(k, g_d[k]) if k == "conv_w" else rows[k] for k in group],
                 [lay(k, m_d[k]) for k in group], [lay(k, v_d[k]) for k in group], gs_small, rows["loss"],
                 "adamw_small", after=(started,))
    for j, k in enumerate(packed):
        g_d[k] = back(k, res[j])
    for j, k in enumerate(group):
        delta[k], new_m[k], new_v[k] = (back(k, res[len(packed) + part * len(group) + j]) for part in range(3))

    arr_in = _exchange_wait(in_sems, in_src, in_land, done + [res[0]], "exchange_in_wait")
    big_adamw("w_in", own_in, arr_in, True, 4)

    loss = res[-1].reshape(())
    return (loss, grad_x[None], *[g_d[k] for k in names], *[delta[k] for k in names], *[new_m[k] for k in names],
            *[new_v[k] for k in names])
```

```python
import math

import numpy as np
import jax
import jax.numpy as jnp
from jax import lax
from jax.experimental import pallas as pl
from jax.experimental.pallas import tpu as pltpu

f32 = jnp.float32
bf16 = jnp.bfloat16

N_DEV = 8
T = 4096
D = 1024
CHUNK = 64
MIX_TILE = 512
RET_TILE = 256
HEADS = 4
DH = 128
RW = HEADS * DH
PW = 512
GROUPS = 4
WINDOWS = (2, 4, 8, 16)
IN_W = 4 * RW + PW
D_FF = 2816
LN_EPS = 1e-5
RMS_EPS = 1e-6
ALPHA = 2.0 ** 0.25
K_SCALE = DH ** -0.5

ADAM_LR = 0.001
ADAM_B1 = 0.9
ADAM_B2 = 0.999
ADAM_EPS = 1e-08
ADAM_WD = 0.01
ADAM_STEP = 10

ROWS_IN, ROWS_OUT, ROWS_UP, ROWS_DOWN = IN_W // N_DEV, D // N_DEV, 2 * D_FF // N_DEV, D_FF // N_DEV

V7X_VMEM_LIMIT = 56 * 2 ** 20
HALO = 32

NT = (((1,), (1,)), ((), ()))
TN = (((0,), (0,)), ((), ()))
NN = (((1,), (0,)), ((), ()))


def _dot(a, b, dims=NN):
    return lax.dot_general(a, b, dims, preferred_element_type=f32)


def _const_spec(shape):
    zeros = (0,) * len(shape)
    return pl.BlockSpec(shape, lambda i: zeros, pipeline_mode=pl.Buffered(1))


def _sigmoid(x):
    return 0.5 * jnp.tanh(0.5 * x) + 0.5


def _decay_tables(tt):
    h = np.arange(HEADS, dtype=np.float64)
    log_gamma = np.log(1.0 - 2.0 ** (-5.0 - h)).astype(np.float32).astype(np.float64)[:, None, None]
    idx = np.arange(tt, dtype=np.float64)
    visible = (idx[None, :] // CHUNK) <= (idx[:, None] // CHUNK)
    mask = np.where(visible[None], np.exp(log_gamma * np.abs(idx[:, None] - idx[None, :])[None]), 0.0)
    qd = np.broadcast_to(np.exp(log_gamma * (idx[None, :, None] + 1.0)), (HEADS, tt, DH))
    kd = np.broadcast_to(np.exp(log_gamma * (tt - 1.0 - idx[None, :, None])), (HEADS, tt, DH))
    cd = np.exp(log_gamma[:, 0, 0] * tt)
    return (jnp.asarray(mask, f32), jnp.asarray(qd, f32), jnp.asarray(kd, f32), [float(c) for c in cd])


def _rope_tables():
    inv_freq = (10000.0 ** (-np.arange(0, DH, 2, dtype=np.float64) / DH)).astype(np.float32)
    ang = (np.arange(T, dtype=np.float32)[:, None] * inv_freq[None, :]).astype(np.float64)
    cos, sin = np.cos(ang), np.sin(ang)
    return (jnp.asarray(np.concatenate([cos, cos], axis=1), f32), jnp.asarray(np.concatenate([-sin, sin], axis=1), f32))


def _swap_halves(t):
    return pltpu.roll(t, DH // 2, axis=1)


def _mix_forward(x, w_in_shard, w_out_shard, cos, sin, dmat, qd, kd, cdec, w_pool, pool_scale, ln1_g, ln1_b,
                 gather_bf16, gather, tt=MIX_TILE):
    n_tiles = T // tt
    to_bf16 = [w_in_shard, w_out_shard] + list(gather_bf16)
    n_c, n_g = len(to_bf16), len(gather_bf16) + len(gather)

    def body(x_ref, cos_ref, sin_ref, dmat_ref, qd_ref, kd_ref, wpool_ref, pscale_ref, g1_ref, b1_ref, *rest):
        f32_in, plain_in, rest = rest[:n_c], rest[n_c:2 + n_g], rest[2 + n_g:]
        qkv_ref, g_ref, oret_ref, states_ref, cat_ref, pooled_ref, xhat_ref, rstd_ref, x1b_ref, xb_ref = rest[:10]
        fout, gout = rest[10:12], rest[12:12 + n_g]
        state_s, pext_s, tmp_s, wint_s, wout_s, load_sems, stage_sems, *rest = rest[12 + n_g:]
        stage_s, cast_s, sems = rest[:n_c], rest[n_c:2 * n_c], rest[2 * n_c:]
        fin, gin, fsems, gsems = cast_s[:2], tuple(cast_s[2:]) + tuple(plain_in), sems[:3], sems[3:]
        i = pl.program_id(0)

        @pl.when(i == 0)
        def _():
            stage = [pltpu.make_async_copy(src, dst, stage_sems.at[j]) for j, (src, dst) in enumerate(zip(f32_in, stage_s))]
            for cp in stage:
                cp.start()
            state_s[...] = jnp.zeros_like(state_s)
            pext_s[:, pl.ds(0, HALO), :] = jnp.zeros((GROUPS, HALO, DH), f32)

            def cast(js):
                for j in js:
                    stage[j].wait()
                    cast_s[j][...] = stage_s[j][...].astype(bf16)

            _barrier(_gather_peers())
            cast(range(2))
            _gather_start(fin, fout, *fsems)
            cast(range(2, n_c))
            _gather_forward(fin, fout, *fsems)
            _gather_start(gin, gout, *gsems)
            _gather_finish(fin, fout, *fsems)
            loads = [pltpu.make_async_copy(src.at[s], dst.at[pl.ds(s * src.shape[1], src.shape[1]), :],
                                           load_sems.at[j, s])
                     for j, (src, dst) in enumerate(((fout[0], wint_s), (fout[1], wout_s))) for s in range(N_DEV)]
            for ld in loads:
                ld.start()
            for ld in loads:
                ld.wait()

        @pl.when(i == n_tiles - 3)
        def _():
            _gather_forward(gin, gout, *gsems)

        xb = x_ref[...].astype(bf16)
        xb_ref[...] = xb
        cos_t, sin_t = cos_ref[...], sin_ref[...]
        for part in range(2):
            pr = _dot(xb, wint_s[pl.ds(part * RW, RW), :], NT)
            for h in range(HEADS):
                t = pr[:, h * DH:(h + 1) * DH]
                r = t * cos_t + _swap_halves(t) * sin_t
                if part == 1:
                    r = r * K_SCALE
                qkv_ref[:, part * RW + h * DH: part * RW + (h + 1) * DH] = r.astype(bf16)
        qkv_ref[:, 2 * RW:3 * RW] = _dot(xb, wint_s[pl.ds(2 * RW, RW), :], NT).astype(bf16)
        g_ref[...] = _dot(xb, wint_s[pl.ds(3 * RW, RW), :], NT)
        p = _dot(xb, wint_s[pl.ds(4 * RW, PW), :], NT)
        for gi in range(GROUPS):
            pext_s[gi, pl.ds(HALO, tt), :] = p[:, gi * DH:(gi + 1) * DH]

        for sub in range(tt // RET_TILE):
            rows = pl.ds(sub * RET_TILE, RET_TILE)
            for h in range(HEADS):
                q = qkv_ref[rows, h * DH:(h + 1) * DH]
                k = qkv_ref[rows, RW + h * DH: RW + (h + 1) * DH]
                v = qkv_ref[rows, 2 * RW + h * DH: 2 * RW + (h + 1) * DH]
                s = _dot(q, k, NT) * dmat_ref[h]
                st = state_s[h]
                stb = st.astype(bf16)
                states_ref[sub, h] = stb
                oret_ref[rows, h * DH:(h + 1) * DH] = (_dot(s.astype(bf16), v)
                                                      + _dot((q.astype(f32) * qd_ref[h]).astype(bf16), stb))
                state_s[h] = st * cdec[h] + _dot((k.astype(f32) * kd_ref[h]).astype(bf16), v, TN)

        for h in range(HEADS):
            sl = slice(h * DH, (h + 1) * DH)
            o = oret_ref[:, sl]
            r = lax.rsqrt(jnp.mean(o * o, axis=-1, keepdims=True) + RMS_EPS)
            gg = g_ref[:, sl]
            cat_ref[:, sl] = (o * r * (gg * _sigmoid(gg))).astype(bf16)

        pos1 = (i * tt + lax.broadcasted_iota(jnp.int32, (tt, 1), 0) + 1).astype(f32)
        for gi, w in enumerate(WINDOWS):
            sl = slice(gi * DH, (gi + 1) * DH)
            stages = int(math.log2(w))
            src = pext_s
            for s in range(stages):
                lo = HALO - 8 * (stages - 1 - s)
                n = tt + HALO - lo
                shift = 2 ** s
                val = src[gi, pl.ds(lo, n), :] + src[gi, pl.ds(lo - shift, n), :]
                if s == stages - 1:
                    wsum = val
                else:
                    tmp_s[gi, pl.ds(lo, n), :] = val
                    src = tmp_s
            p_g = pext_s[gi, pl.ds(HALO, tt), :]
            pooled = (wsum / jnp.minimum(pos1, float(w)) - p_g).astype(bf16)
            pooled_ref[:, sl] = pooled
            y = _dot(pooled, wpool_ref[gi].astype(bf16)) * pscale_ref[:, sl]
            cat_ref[:, RW + gi * DH: RW + (gi + 1) * DH] = y.astype(bf16)
        pext_s[:, pl.ds(0, HALO), :] = pext_s[:, pl.ds(tt, HALO), :]

        z = ALPHA * x_ref[...] + _dot(cat_ref[...], wout_s[...])
        mu = jnp.mean(z, axis=-1, keepdims=True)
        zc = z - mu
        rstd = lax.rsqrt(jnp.mean(zc * zc, axis=-1, keepdims=True) + LN_EPS)
        xhat = zc * rstd
        xhat_ref[...] = xhat
        rstd_ref[...] = rstd
        x1b_ref[...] = (xhat * g1_ref[...] + b1_ref[...]).astype(bf16)

        @pl.when(i == n_tiles - 1)
        def _():
            _gather_finish(gin, gout, *gsems)

    tile = lambda w: pl.BlockSpec((tt, w), lambda i: (i, 0))
    hbm = pl.BlockSpec(memory_space=pltpu.HBM)
    out_shape = (
        jax.ShapeDtypeStruct((T, 3 * RW), bf16),
        jax.ShapeDtypeStruct((T, RW), f32),
        jax.ShapeDtypeStruct((T, RW), f32),
        jax.ShapeDtypeStruct((T // RET_TILE, HEADS, DH, DH), bf16),
        jax.ShapeDtypeStruct((T, D), bf16),
        jax.ShapeDtypeStruct((T, PW), bf16),
        jax.ShapeDtypeStruct((T, D), f32),
        jax.ShapeDtypeStruct((T, 1), f32),
        jax.ShapeDtypeStruct((T, D), bf16),
        jax.ShapeDtypeStruct((T, D), bf16),
    ) + tuple(jax.ShapeDtypeStruct((N_DEV,) + b.shape, bf16) for b in to_bf16
              ) + tuple(jax.ShapeDtypeStruct((N_DEV,) + b.shape, b.dtype) for b in gather)
    return pl.pallas_call(
        body, name="mix_forward", grid=(n_tiles,), out_shape=out_shape,
        in_specs=[tile(D), tile(DH), tile(DH),
                  _const_spec((HEADS, RET_TILE, RET_TILE)), _const_spec((HEADS, RET_TILE, DH)),
                  _const_spec((HEADS, RET_TILE, DH)),
                  _const_spec((GROUPS, DH, DH)), _const_spec((1, PW)),
                  _const_spec((1, D)), _const_spec((1, D))] + [hbm] * (2 + n_g),
        out_specs=(tile(3 * RW), tile(RW), tile(RW),
                   pl.BlockSpec((tt // RET_TILE, HEADS, DH, DH), lambda i: (i, 0, 0, 0)),
                   tile(D), tile(PW), tile(D), tile(1), tile(D), tile(D)) + (hbm,) * (2 + n_g),
        scratch_shapes=[pltpu.VMEM((HEADS, DH, DH), f32), pltpu.VMEM((GROUPS, tt + HALO, DH), f32),
                        pltpu.VMEM((GROUPS, tt + HALO, DH), f32), pltpu.VMEM((IN_W, D), bf16), pltpu.VMEM((D, D), bf16),
                        pltpu.SemaphoreType.DMA((2, N_DEV)), pltpu.SemaphoreType.DMA((n_c,))]
        + [pltpu.VMEM(b.shape, f32) for b in to_bf16] + [pltpu.VMEM(b.shape, bf16) for b in to_bf16]
        + _gather_sems(2) + _gather_sems(n_g),
        compiler_params=pltpu.CompilerParams(dimension_semantics=("arbitrary",), vmem_limit_bytes=V7X_VMEM_LIMIT,
                                             collective_id=GATHER_BARRIER),
    )(x, cos, sin, dmat, qd, kd, w_pool, pool_scale, ln1_g, ln1_b, *to_bf16, *gather)


def _ffn_forward_backward(xhat1, rstd1, ln1_g, ln1_b, w_up_t, conv_w, conv_b, w_down, ln2_g, ln2_b, target,
                          tt=256):
    n_tiles = T // tt
    FH = 16
    hb = tt // FH

    def body(xhat_ref, halo_ref, rstd_ref, g1_ref, b1_ref, wupt_ref, cw_ref, cb_ref, wdown_ref, g2_ref, b2_ref, tgt_ref,
             dz1_ref, dz2b_ref, du_ref, f_ref, loss_ref, dg2_ref, db2_ref, dg1_ref, db1_ref, dcb_ref, dcw_ref,
             gext_s, val_s, dhext_s):
        i = pl.program_id(0)
        tile_idx = n_tiles - 1 - i

        def rd(ref, off):
            return jnp.concatenate([ref[k, pl.ds(off, tt), :] for k in range(D_FF // 128)], axis=1)

        def wr(ref, val):
            for k in range(D_FF // 128):
                ref[k, pl.ds(0, val.shape[0]), :] = val[:, k * 128:(k + 1) * 128]

        @pl.when(i == 0)
        def _():
            for r in (loss_ref, dg2_ref, db2_ref, dg1_ref, db1_ref, dcb_ref, dcw_ref):
                r[...] = jnp.zeros_like(r)
            dhext_s[:, pl.ds(tt, 8), :] = jnp.zeros((D_FF // 128, 8, 128), f32)

        g1, b1 = g1_ref[...], b1_ref[...]
        xhat = xhat_ref[...]
        x1 = xhat * g1 + b1
        x1b = x1.astype(bf16)
        x1h = ((halo_ref[...] * g1 + b1) * jnp.where(tile_idx == 0, 0.0, 1.0)).astype(bf16)
        x1ext = jnp.concatenate([x1h, x1b], axis=0)

        val = _dot(x1b, wupt_ref[pl.ds(0, D_FF), :], NT)
        gate_ext = _dot(x1ext, wupt_ref[pl.ds(D_FF, D_FF), :], NT)
        wr(gext_s, gate_ext)
        hh = (cb_ref[...] + cw_ref[0:1, :] * rd(gext_s, FH - 2) + cw_ref[1:2, :] * rd(gext_s, FH - 1)
              + cw_ref[2:3, :] * gate_ext[FH:])
        sg = _sigmoid(hh)
        act = hh * sg
        wr(dhext_s, act)
        val_s[...] = val * (sg + act * (1.0 - sg))
        fb = (act * val).astype(bf16)
        f_ref[...] = fb

        z = ALPHA * x1 + _dot(fb, wdown_ref[...])
        mu = jnp.mean(z, axis=-1, keepdims=True)
        zc = z - mu
        rstd2 = lax.rsqrt(jnp.mean(zc * zc, axis=-1, keepdims=True) + LN_EPS)
        xh2 = zc * rstd2
        diff = xh2 * g2_ref[...] + b2_ref[...] - tgt_ref[...]
        loss_ref[...] += 0.5 * jnp.sum(diff * diff) / D
        dy = diff * (1.0 / D)
        dg2_ref[...] += jnp.sum(dy * xh2, axis=0, keepdims=True)
        db2_ref[...] += jnp.sum(dy, axis=0, keepdims=True)
        dyg = dy * g2_ref[...]
        dz2 = rstd2 * (dyg - jnp.mean(dyg, axis=-1, keepdims=True) - xh2 * jnp.mean(dyg * xh2, axis=-1, keepdims=True))
        dz2b = dz2.astype(bf16)
        dz2b_ref[...] = dz2b

        df = _dot(dz2b, wdown_ref[...], NT)
        dval = df * rd(dhext_s, 0)
        dh = df * val_s[...]
        wr(dhext_s, dh)
        dh1, dh2, g0 = rd(dhext_s, 1), rd(dhext_s, 2), rd(gext_s, FH)
        dcb_ref[...] += jnp.sum(dh, axis=0, keepdims=True)
        dcw_ref[0:1, :] += jnp.sum(dh2 * g0, axis=0, keepdims=True)
        dcw_ref[1:2, :] += jnp.sum(dh1 * g0, axis=0, keepdims=True)
        dcw_ref[2:3, :] += jnp.sum(dh * g0, axis=0, keepdims=True)
        dgate = cw_ref[2:3, :] * dh + cw_ref[1:2, :] * dh1 + cw_ref[0:1, :] * dh2
        dvalb, dgateb = dval.astype(bf16), dgate.astype(bf16)
        du_ref[:, :D_FF] = dvalb
        du_ref[:, D_FF:] = dgateb
        dx1 = ALPHA * dz2 + _dot(dvalb, wupt_ref[pl.ds(0, D_FF), :]) + _dot(dgateb, wupt_ref[pl.ds(D_FF, D_FF), :])
        dhext_s[:, pl.ds(tt, 8), :] = dhext_s[:, pl.ds(0, 8), :]

        dg1_ref[...] += jnp.sum(dx1 * xhat, axis=0, keepdims=True)
        db1_ref[...] += jnp.sum(dx1, axis=0, keepdims=True)
        dxg = dx1 * g1
        dz1_ref[...] = rstd_ref[...] * (dxg - jnp.mean(dxg, axis=-1, keepdims=True)
                                        - xhat * jnp.mean(dxg * xhat, axis=-1, keepdims=True))

    rtile = lambda w: pl.BlockSpec((tt, w), lambda i: (n_tiles - 1 - i, 0))
    acc = lambda shape: pl.BlockSpec(shape, lambda i: (0, 0))
    out_shape = (
        jax.ShapeDtypeStruct((T, D), f32),
        jax.ShapeDtypeStruct((T, D), bf16),
        jax.ShapeDtypeStruct((T, 2 * D_FF), bf16),
        jax.ShapeDtypeStruct((T, D_FF), bf16),
        jax.ShapeDtypeStruct((8, 128), f32),
        jax.ShapeDtypeStruct((1, D), f32), jax.ShapeDtypeStruct((1, D), f32),
        jax.ShapeDtypeStruct((1, D), f32), jax.ShapeDtypeStruct((1, D), f32),
        jax.ShapeDtypeStruct((1, D_FF), f32), jax.ShapeDtypeStruct((3, D_FF), f32),
    )
    return pl.pallas_call(
        body, name="ffn_forward_backward", grid=(n_tiles,), out_shape=out_shape,
        in_specs=[rtile(D),
                  pl.BlockSpec((FH, D), lambda i: (jnp.maximum((n_tiles - 1 - i) * hb - 1, 0), 0)),
                  rtile(1), _const_spec((1, D)), _const_spec((1, D)), _const_spec((2 * D_FF, D)),
                  _const_spec((3, D_FF)), _const_spec((1, D_FF)), _const_spec((D_FF, D)),
                  _const_spec((1, D)), _const_spec((1, D)), rtile(D)],
        out_specs=(rtile(D), rtile(D), rtile(2 * D_FF), rtile(D_FF), acc((8, 128)),
                   acc((1, D)), acc((1, D)), acc((1, D)), acc((1, D)), acc((1, D_FF)), acc((3, D_FF))),
        scratch_shapes=[pltpu.VMEM((D_FF // 128, tt + FH, 128), f32), pltpu.VMEM((tt, D_FF), f32),
                        pltpu.VMEM((D_FF // 128, tt + 8, 128), f32)],
        compiler_params=pltpu.CompilerParams(dimension_semantics=("arbitrary",), vmem_limit_bytes=V7X_VMEM_LIMIT),
    )(xhat1, xhat1, rstd1, ln1_g, ln1_b, w_up_t, conv_w, conv_b, w_down, ln2_g, ln2_b, target)


def _mix_backward(dz1, w_out, qkv, g, oret, states, pooled, cat, cos, sin, dmat, qd, kd, cdec, w_pool, pool_scale, w_in_t,
                  small_ffn, after, tt=MIX_TILE):
    n_tiles = T // tt

    def body(dz1_ref, wout_ref, qkv_ref, g_ref, oret_ref, states_ref, pooled_ref, cat_ref, cos_ref, sin_ref, dmat_ref,
             qd_ref, kd_ref, wpool_ref, pscale_ref, wint_ref, *rest):
        ffn_refs, rest = rest[:len(SMALL_FFN)], rest[len(SMALL_FFN):]
        after_ref, dproj_ref, gx_ref, small_ref, dwout_ref, dstate_s, dout_s, eext_s, tmp_s, dwout_s, dpscale_s = rest
        i = pl.program_id(0)
        tile_idx = n_tiles - 1 - i

        @pl.when(i == 0)
        def _():
            dstate_s[...] = jnp.zeros_like(dstate_s)
            small_ref[...] = jnp.zeros_like(small_ref)
            dpscale_s[...] = jnp.zeros_like(dpscale_s)
            dwout_s[...] = jnp.zeros_like(dwout_s)
            eext_s[:, pl.ds(tt, HALO), :] = jnp.zeros((GROUPS, HALO, DH), f32)

        dz1 = dz1_ref[...]
        dz1b = dz1.astype(bf16)
        dcat = _dot(dz1b, wout_ref[...], NT)
        dwout_s[...] += _dot(cat_ref[...], dz1b, TN)

        pos1 = (tile_idx * tt + lax.broadcasted_iota(jnp.int32, (tt, 1), 0) + 1).astype(f32)
        for gi, w in enumerate(WINDOWS):
            sl = slice(gi * DH, (gi + 1) * DH)
            dpo = dcat[:, RW + gi * DH: RW + (gi + 1) * DH]
            pooled_g = pooled_ref[:, sl]
            wpool_g = wpool_ref[gi].astype(bf16)
            ylin = _dot(pooled_g, wpool_g)
            dpscale_s[:, sl] += jnp.sum(dpo * ylin, axis=0, keepdims=True)
            dpw = (dpo * pscale_ref[:, sl]).astype(bf16)
            small_ref[pl.ds(gi * DH, DH), :] += _dot(pooled_g, dpw, TN)
            dpooled = _dot(dpw, wpool_g, NT)
            eext_s[gi, pl.ds(0, tt), :] = dpooled / jnp.minimum(pos1, float(w))
            stages = int(math.log2(w))
            src = eext_s
            for s in range(stages):
                n = tt + 8 * (stages - 1 - s)
                shift = 2 ** s
                val = src[gi, pl.ds(0, n), :] + src[gi, pl.ds(shift, n), :]
                if s == stages - 1:
                    wsum = val
                else:
                    tmp_s[gi, pl.ds(0, n), :] = val
                    src = tmp_s
            dproj_ref[:, 4 * RW + gi * DH: 4 * RW + (gi + 1) * DH] = (wsum - dpooled).astype(bf16)
        eext_s[:, pl.ds(tt, HALO), :] = eext_s[:, pl.ds(0, HALO), :]

        for h in range(HEADS):
            sl = slice(h * DH, (h + 1) * DH)
            dr = dcat[:, sl]
            o = oret_ref[:, sl]
            r = lax.rsqrt(jnp.mean(o * o, axis=-1, keepdims=True) + RMS_EPS)
            rn = o * r
            gg = g_ref[:, sl]
            sg = _sigmoid(gg)
            dproj_ref[:, 3 * RW + h * DH: 3 * RW + (h + 1) * DH] = (dr * rn * (sg * (1.0 + gg * (1.0 - sg)))).astype(bf16)
            drn = dr * (gg * sg)
            dout_s[:, sl] = (r * (drn - rn * jnp.mean(drn * rn, axis=-1, keepdims=True))).astype(bf16)

        for sub in reversed(range(tt // RET_TILE)):
            rows = pl.ds(sub * RET_TILE, RET_TILE)
            cos_t, sin_t = cos_ref[rows, :], sin_ref[rows, :]
            for h in range(HEADS):
                q = qkv_ref[rows, h * DH:(h + 1) * DH]
                k = qkv_ref[rows, RW + h * DH: RW + (h + 1) * DH]
                v = qkv_ref[rows, 2 * RW + h * DH: 2 * RW + (h + 1) * DH]
                do = dout_s[rows, h * DH:(h + 1) * DH]
                stb = states_ref[sub, h]
                dst = dstate_s[h]
                dstb = dst.astype(bf16)
                sb = (_dot(q, k, NT) * dmat_ref[h]).astype(bf16)
                dsb = (_dot(do, v, NT) * dmat_ref[h]).astype(bf16)
                dq = _dot(dsb, k) + _dot(do, stb, NT) * qd_ref[h]
                dk = _dot(dsb, q, TN) + _dot(v, dstb, NT) * kd_ref[h]
                dv = _dot(sb, do, TN) + _dot((k.astype(f32) * kd_ref[h]).astype(bf16), dstb)
                dstate_s[h] = dst * cdec[h] + _dot((q.astype(f32) * qd_ref[h]).astype(bf16), do, TN)
                dproj_ref[rows, h * DH:(h + 1) * DH] = (dq * cos_t - _swap_halves(dq) * sin_t).astype(bf16)
                dproj_ref[rows, RW + h * DH: RW + (h + 1) * DH] = (
                    (dk * cos_t - _swap_halves(dk) * sin_t) * K_SCALE).astype(bf16)
                dproj_ref[rows, 2 * RW + h * DH: 2 * RW + (h + 1) * DH] = dv.astype(bf16)

        gx_ref[...] = ALPHA * dz1 + _dot(dproj_ref[...], wint_ref[...])

        @pl.when(i == n_tiles - 1)
        def _():
            dwout_ref[...] = dwout_s[...].astype(bf16)
            at = GROUPS * DH
            for ref, size in [(dpscale_s, PW)] + [(ref, size) for ref, (_, size) in zip(ffn_refs, SMALL_FFN)]:
                for j in range(size // 128):
                    r, k = divmod(j, ref.shape[1] // 128)
                    small_ref[at + j: at + j + 1, :] = ref[r:r + 1, k * 128:(k + 1) * 128]
                at = SMALL_FFN_AT if ref is dpscale_s else at + size // 128

    rtile = lambda w: pl.BlockSpec((tt, w), lambda i: (n_tiles - 1 - i, 0))
    out_shape = (
        jax.ShapeDtypeStruct((T, IN_W), bf16),
        jax.ShapeDtypeStruct((T, D), f32),
        jax.ShapeDtypeStruct((SMALL_ROWS, 128), f32),
        jax.ShapeDtypeStruct((D, D), bf16),
    )
    return pl.pallas_call(
        body, name="mix_backward", grid=(n_tiles,), out_shape=out_shape,
        in_specs=[rtile(D), _const_spec((D, D)), rtile(3 * RW), rtile(RW), rtile(RW),
                  pl.BlockSpec((tt // RET_TILE, HEADS, DH, DH), lambda i: (n_tiles - 1 - i, 0, 0, 0)),
                  rtile(PW), rtile(D), rtile(DH), rtile(DH),
                  _const_spec((HEADS, RET_TILE, RET_TILE)), _const_spec((HEADS, RET_TILE, DH)),
                  _const_spec((HEADS, RET_TILE, DH)),
                  _const_spec((GROUPS, DH, DH)), _const_spec((1, PW)), _const_spec((IN_W, D)),
                  *[_const_spec(a.shape) for a in small_ffn], pl.BlockSpec(memory_space=pl.ANY)],
        out_specs=(rtile(IN_W), rtile(D), pl.BlockSpec((SMALL_ROWS, 128), lambda i: (0, 0)),
                   pl.BlockSpec((D, D), lambda i: (0, 0), pipeline_mode=pl.Buffered(1))),
        scratch_shapes=[pltpu.VMEM((HEADS, DH, DH), f32), pltpu.VMEM((tt, RW), bf16),
                        pltpu.VMEM((GROUPS, tt + HALO, DH), f32), pltpu.VMEM((GROUPS, tt + HALO, DH), f32),
                        pltpu.VMEM((D, D), f32), pltpu.VMEM((1, PW), f32)],
        compiler_params=pltpu.CompilerParams(dimension_semantics=("arbitrary",), vmem_limit_bytes=V7X_VMEM_LIMIT),
    )(dz1, w_out, qkv, g, oret, states, pooled, cat, cos, sin, dmat, qd, kd, w_pool, pool_scale, w_in_t, *small_ffn,
      after)


def _weight_grad(a, b, name, tm, reduce=()):
    m = a.shape[1]
    n_m, n_r = m // tm, len(reduce)
    assert not n_r or n_m >= 2

    def body(a_ref, b_ref, *rest):
        ins, o_ref, own, arrived = rest[:n_r], rest[n_r], rest[n_r + 1:2 * n_r + 1], rest[2 * n_r + 1:3 * n_r + 1]
        landing, mine, sems = rest[3 * n_r + 1:4 * n_r + 1], rest[4 * n_r + 1:5 * n_r + 1], rest[5 * n_r + 1:]
        i = pl.program_id(0)

        if n_r:
            pair_send, pair_recv, local_sems, chip_send, chip_recv = sems
            me = _me()
            x, y, c = me
            sibling = (x, y, 1 - c)

            def pair_copy(k, j):
                return _remote(ins[j].at[_slot(*_chip(me, k), 1 - c)], landing[j].at[k], pair_send.at[k, j],
                               pair_recv.at[k, j], sibling)

            def load(k, j):
                return pltpu.make_async_copy(ins[j].at[_slot(*_chip(me, k), c)], mine[j].at[k], local_sems.at[k, j])

            def store(j):
                return pltpu.make_async_copy(mine[j].at[0], own[j], local_sems.at[0, j])

            def chip_copy(k, j):
                return _remote(mine[j].at[k], arrived[j].at[k - 1], chip_send.at[k - 1, j], chip_recv.at[k - 1, j],
                               (*_chip(me, k), c))

            @pl.when(i == 0)
            def _():
                _barrier([sibling] + _chip_peers())
                for k in range(4):
                    for j in range(n_r):
                        pair_copy(k, j).start()
                        load(k, j).start()

            @pl.when(i == 1)
            def _():
                for k in range(4):
                    for j in range(n_r):
                        load(k, j).wait()
                        pair_copy(k, j).wait_recv()
                        mine[j][k] = (mine[j][k].astype(f32) + landing[j][k].astype(f32)).astype(mine[j].dtype)
                        (store(j) if k == 0 else chip_copy(k, j)).start()

        o_ref[...] = _dot(a_ref[...], b_ref[...].astype(bf16), TN).astype(bf16)

        if n_r:
            @pl.when(i == n_m - 1)
            def _():
                for j in range(n_r):
                    store(j).wait()
                    for k in range(1, 4):
                        chip_copy(k, j).wait_recv()
                for j in range(n_r):
                    for k in range(1, 4):
                        chip_copy(k, j).wait_send()
                    for k in range(4):
                        pair_copy(k, j).wait_send()

    hbm = pl.BlockSpec(memory_space=pltpu.HBM)
    return pl.pallas_call(
        body, name=name, grid=(n_m,),
        out_shape=(jax.ShapeDtypeStruct((m, D), bf16),)
        + tuple(jax.ShapeDtypeStruct(p.shape[1:], p.dtype) for p in reduce)
        + tuple(jax.ShapeDtypeStruct((3,) + p.shape[1:], p.dtype) for p in reduce),
        in_specs=[pl.BlockSpec((T, tm), lambda i: (0, i)),
                  pl.BlockSpec((T, D), lambda i: (0, 0), pipeline_mode=pl.Buffered(1))] + [hbm] * n_r,
        out_specs=(pl.BlockSpec((tm, D), lambda i: (i, 0)),) + (hbm,) * (2 * n_r),
        scratch_shapes=[pltpu.VMEM((4,) + p.shape[1:], p.dtype) for p in reduce] * 2
        + ([pltpu.SemaphoreType.DMA((4, n_r))] * 3 + [pltpu.SemaphoreType.DMA((3, n_r))] * 2 if n_r else []),
        compiler_params=pltpu.CompilerParams(dimension_semantics=("arbitrary",), vmem_limit_bytes=V7X_VMEM_LIMIT,
                                             collective_id=REDUCE_BARRIER if n_r else None),
    )(a, b, *reduce)


CHIP_FLIPS = ((1, 0), (0, 1), (1, 1))
PAIR_BARRIER, CHIP_BARRIER, GATHER_BARRIER, CHIP_BARRIER_SPLIT, REDUCE_BARRIER = 0, 1, 2, 3, 4


def _barrier(peers):
    sem = pltpu.get_barrier_semaphore()
    for peer in peers:
        pl.semaphore_signal(sem, inc=1, device_id=peer, device_id_type=pl.DeviceIdType.MESH)
    pl.semaphore_wait(sem, len(peers))


def _me():
    return lax.axis_index("x"), lax.axis_index("y"), lax.axis_index("c")


def _chip(me, k):
    x, y, _ = me
    if k == 0:
        return x, y
    fx, fy = CHIP_FLIPS[k - 1]
    return (1 - x if fx else x), (1 - y if fy else y)


def _slot(x, y, c):
    return 4 * x + 2 * y + c


def _remote(src, dst, send_sem, recv_sem, to):
    return pltpu.make_async_remote_copy(src_ref=src, dst_ref=dst, send_sem=send_sem, recv_sem=recv_sem,
                                        device_id=to, device_id_type=pl.DeviceIdType.MESH)


def _gather_sems(n):
    return [pltpu.SemaphoreType.DMA((7, n)), pltpu.SemaphoreType.DMA((7, n)), pltpu.SemaphoreType.DMA((n,))] if n else []


def _gather_copy(k, j, gin, gout, send_sems, recv_sems, sending):
    x, y, c = _me()
    sibling, x_chip, y_chip, d_chip = (x, y, 1 - c), (1 - x, y), (x, 1 - y), (1 - x, 1 - y)
    south = c == 0
    passed_on = (jnp.where(south, 1 - x, x), jnp.where(south, y, 1 - y), c)
    src, to = gin[j], sibling
    if sending:
        block = {0: (x, y, c), 1: (x, y, c), 2: (x, y, c), 3: passed_on, 4: (*x_chip, c), 5: (*y_chip, c), 6: (*d_chip, c)}[k]
        to = {1: (*x_chip, c), 2: (*y_chip, c), 3: (jnp.where(south, x, 1 - x), jnp.where(south, 1 - y, y), c)}.get(k, sibling)
        if k >= 3:
            src = gout[j].at[_slot(*block)]
    else:
        block = {0: sibling, 1: (*x_chip, c), 2: (*y_chip, c), 3: (*d_chip, c), 4: (*x_chip, 1 - c), 5: (*y_chip, 1 - c),
                 6: (*d_chip, 1 - c)}[k]
    return _remote(src, gout[j].at[_slot(*block)], send_sems.at[k, j], recv_sems.at[k, j], to)


def _gather_do(ks, action, gin, gout, send_sems, recv_sems):
    for k in ks:
        for j in range(len(gin)):
            cp = _gather_copy(k, j, gin, gout, send_sems, recv_sems, action != "wait_recv")
            getattr(cp, action)()


def _gather_peers():
    x, y, c = _me()
    return [(x, y, 1 - c), (1 - x, y, c), (x, 1 - y, c)]


def _gather_start(gin, gout, send_sems, recv_sems, local_sems):
    for j in range(len(gin)):
        pltpu.make_async_copy(gin[j], gout[j].at[_slot(*_me())], local_sems.at[j]).start()
    _gather_do((0, 1, 2), "start", gin, gout, send_sems, recv_sems)


def _gather_forward(gin, gout, send_sems, recv_sems, local_sems):
    _gather_do((1, 2), "wait_recv", gin, gout, send_sems, recv_sems)
    _gather_do((3, 4, 5), "start", gin, gout, send_sems, recv_sems)


def _gather_finish(gin, gout, send_sems, recv_sems, local_sems):
    _gather_do((3,), "wait_recv", gin, gout, send_sems, recv_sems)
    _gather_do((6,), "start", gin, gout, send_sems, recv_sems)
    _gather_do((0, 4, 5, 6), "wait_recv", gin, gout, send_sems, recv_sems)
    _gather_do(range(7), "wait_send", gin, gout, send_sems, recv_sems)
    for j in range(len(gin)):
        pltpu.make_async_copy(gin[j], gout[j].at[_slot(*_me())], local_sems.at[j]).wait()


def _pair_reduce(parts, name, gather_sum=None):
    n = len(parts)
    n_h = 0 if gather_sum is None else 1

    def body(*refs):
        ins, g_terms, refs = refs[:n], refs[n:n + 2 * n_h], refs[n + 2 * n_h:]
        own, others, g_out, refs = refs[:n], refs[n:2 * n], refs[2 * n:2 * n + n_h], refs[2 * n + n_h:]
        landing, mine, (send_sems, recv_sems, local_sems), g_scratch = refs[:n], refs[n:2 * n], refs[2 * n:2 * n + 3], refs[2 * n + 3:]
        me = _me()
        x, y, c = me
        sibling = (x, y, 1 - c)
        _barrier(_gather_peers() if n_h else [sibling])
        if n_h:
            piece_s, g_sems = g_scratch[0], g_scratch[1:]
            acc = g_terms[0][...].astype(f32)
            for k in range(3):
                acc = acc + g_terms[1][k].astype(f32)
            piece_s[...] = acc
            _gather_start([piece_s], g_out, *g_sems)
        sends, loads = [], []
        for k in range(4):
            for j in range(n):
                cp = _remote(ins[j].at[_slot(*_chip(me, k), 1 - c)], landing[j].at[k], send_sems.at[k, j],
                             recv_sems.at[k, j], sibling)
                cp.start()
                sends.append(cp)
                ld = pltpu.make_async_copy(ins[j].at[_slot(*_chip(me, k), c)], mine[j].at[k], local_sems.at[k, j])
                ld.start()
                loads.append(ld)
        if n_h:
            _gather_forward([piece_s], g_out, *g_sems)
        stores = []
        for k in range(4):
            for j in range(n):
                loads[k * n + j].wait()
                _remote(ins[j].at[0], landing[j].at[k], send_sems.at[k, j], recv_sems.at[k, j], sibling).wait_recv()
                mine[j][k] = (mine[j][k].astype(f32) + landing[j][k].astype(f32)).astype(mine[j].dtype)
                st = pltpu.make_async_copy(mine[j].at[k], own[j] if k == 0 else others[j].at[k - 1], local_sems.at[k, j])
                st.start()
                stores.append(st)
        for cp in sends:
            cp.wait_send()
        for st in stores:
            st.wait()
        if n_h:
            _gather_finish([piece_s], g_out, *g_sems)

    vm, hbm = pl.BlockSpec(memory_space=pltpu.VMEM), pl.BlockSpec(memory_space=pltpu.HBM)
    g_shape = gather_sum[0].shape if n_h else ()
    return pl.pallas_call(
        body, name=name,
        out_shape=tuple(jax.ShapeDtypeStruct(p.shape[1:], p.dtype) for p in parts)
        + tuple(jax.ShapeDtypeStruct((3,) + p.shape[1:], p.dtype) for p in parts)
        + tuple([jax.ShapeDtypeStruct((N_DEV,) + g_shape, f32)] * n_h),
        in_specs=[hbm] * n + [vm] * (2 * n_h), out_specs=(hbm,) * (2 * n + n_h),
        scratch_shapes=[pltpu.VMEM((4,) + p.shape[1:], p.dtype) for p in parts] * 2
        + [pltpu.SemaphoreType.DMA((4, n)), pltpu.SemaphoreType.DMA((4, n)), pltpu.SemaphoreType.DMA((4, n))]
        + ([pltpu.VMEM(g_shape, f32)] + _gather_sems(1)) * n_h,
        compiler_params=pltpu.CompilerParams(vmem_limit_bytes=V7X_VMEM_LIMIT,
                                             collective_id=GATHER_BARRIER if n_h else PAIR_BARRIER),
    )(*parts, *(gather_sum or ()))


def _chip_peers():
    me = _me()
    return [(*_chip(me, k), me[2]) for k in range(1, 4)]


def _split_copies(src_ref, dst_ref, sems):
    me = _me()
    return [_remote(src_ref.at[k - 1], dst_ref.at[k - 1], sems[k - 1], sems[2 + k], (*_chip(me, k), me[2]))
            for k in range(1, 4)]


def _exchange_start(others, name, barrier_id):
    def body(src_ref, land_ref, *rest):
        sems, token_ref = rest[:6], rest[8]
        _barrier(_chip_peers())
        for copy in _split_copies(src_ref, land_ref, sems):
            copy.start()
        token_ref[...] = jnp.zeros_like(token_ref)

    hbm, sem = pl.BlockSpec(memory_space=pltpu.HBM), pl.BlockSpec(memory_space=pltpu.SEMAPHORE)
    thru = pltpu.HBM(others.shape, others.dtype)
    res = pl.pallas_call(
        body, name=name,
        out_shape=(pltpu.SemaphoreType.DMA(()),) * 6 + (thru, thru, jax.ShapeDtypeStruct((8, 128), f32)),
        in_specs=(hbm, hbm), out_specs=(sem,) * 6 + (hbm, hbm, pl.BlockSpec(memory_space=pltpu.VMEM)),
        input_output_aliases={0: 6, 1: 7},
        compiler_params=pltpu.CompilerParams(has_side_effects=pltpu.SideEffectType.DATAFLOW_SIDE_EFFECTING,
                                             collective_id=barrier_id),
    )(pltpu.with_memory_space_constraint(others, pltpu.HBM),
      pltpu.with_memory_space_constraint(lax.empty(others.shape, others.dtype), pltpu.HBM))
    return res[:6], res[6], res[7], res[8]


def _exchange_wait(sems, src_thru, land_thru, after, name):
    n_after = len(after)

    def body(src_ref, land_ref, *rest):
        for copy in _split_copies(src_ref, land_ref, rest[:6]):
            copy.wait_send()
            copy.wait_recv()

    hbm, sem = pl.BlockSpec(memory_space=pltpu.HBM), pl.BlockSpec(memory_space=pltpu.SEMAPHORE)
    thru = pltpu.HBM(src_thru.shape, src_thru.dtype)
    return pl.pallas_call(
        body, name=name, out_shape=(thru, thru),
        in_specs=(hbm, hbm) + (sem,) * 6 + (pl.BlockSpec(memory_space=pl.ANY),) * n_after, out_specs=(hbm, hbm),
        input_output_aliases={0: 0, 1: 1},
        compiler_params=pltpu.CompilerParams(has_side_effects=pltpu.SideEffectType.DATAFLOW_SIDE_EFFECTING),
    )(src_thru, land_thru, *sems, *after)[1]


def _adam_update(w, g, m, v):
    m = ADAM_B1 * m + (1.0 - ADAM_B1) * g
    v = ADAM_B2 * v + (1.0 - ADAM_B2) * (g * g)
    m_hat = m / (1.0 - ADAM_B1 ** ADAM_STEP)
    v_hat = v / (1.0 - ADAM_B2 ** ADAM_STEP)
    return -ADAM_LR * (m_hat / (jnp.sqrt(v_hat) + ADAM_EPS) + ADAM_WD * w), m, v


def _sum_adamw(own, arrived, w, m, v, name, steps, after=()):
    rows = own.shape[0]
    br = rows // steps

    def body(own_ref, arr_ref, w_ref, m_ref, v_ref, *rest):
        g_out, d_out, m_out, v_out = rest[len(after):]
        g = own_ref[...].astype(f32)
        for k in range(3):
            g = g + arr_ref[k].astype(f32)
        g_out[...] = g
        d_out[...], m_out[...], v_out[...] = _adam_update(w_ref[...], g, m_ref[...], v_ref[...])

    blk = pl.BlockSpec((br, D), lambda i: (i, 0))
    return pl.pallas_call(
        body, name=name, grid=(steps,), out_shape=(jax.ShapeDtypeStruct((rows, D), f32),) * 4,
        in_specs=[blk, pl.BlockSpec((3, br, D), lambda i: (0, i, 0)), blk, blk, blk]
        + [pl.BlockSpec(memory_space=pl.ANY)] * len(after), out_specs=(blk,) * 4,
        compiler_params=pltpu.CompilerParams(dimension_semantics=("parallel",), vmem_limit_bytes=V7X_VMEM_LIMIT),
    )(own, arrived, w, m, v, *after)


def _adamw(ws, gs, ms, vs, packed, scalar_row, name, after=()):
    n = len(ws)
    given = [g for g in gs if not isinstance(g, int)]
    taken = [j for j in range(n) if isinstance(gs[j], int)]

    def body(packed_ref, *refs):
        w_r, m_r, v_r = (refs[k * n:(k + 1) * n] for k in range(3))
        given_r, outs = list(refs[3 * n:3 * n + len(given)]), refs[3 * n + len(given) + len(after):]
        g_o, outs = dict(zip(taken, outs[:len(taken)])), outs[len(taken):]
        d_o, m_o, v_o = (outs[k * n:(k + 1) * n] for k in range(3))
        outs[3 * n][...] = packed_ref[scalar_row:scalar_row + 1, 0:1]
        for j in range(n):
            if j in g_o:
                (r, c), at = ws[j].shape, gs[j]
                if c == 128:
                    g = packed_ref[at:at + r, :]
                else:
                    assert r == 1
                    g = jnp.concatenate([packed_ref[at + k:at + k + 1, :] for k in range(c // 128)], axis=1)
                g_o[j][...] = g
            else:
                g = given_r.pop(0)[...]
            d_o[j][...], m_o[j][...], v_o[j][...] = _adam_update(w_r[j][...], g, m_r[j][...], v_r[j][...])

    vm = pl.BlockSpec(memory_space=pltpu.VMEM)
    shapes = tuple(jax.ShapeDtypeStruct(w.shape, f32) for w in ws)
    n_out = len(taken) + 3 * n + 1
    return pl.pallas_call(
        body, name=name,
        out_shape=tuple(shapes[j] for j in taken) + shapes * 3 + (jax.ShapeDtypeStruct((1, 1), f32),),
        in_specs=[vm] * (1 + 3 * n + len(given)) + [pl.BlockSpec(memory_space=pl.ANY)] * len(after),
        out_specs=tuple([vm] * n_out),
        compiler_params=pltpu.CompilerParams(vmem_limit_bytes=V7X_VMEM_LIMIT),
    )(packed, *ws, *ms, *vs, *given, *after)


SMALL_FFN = (("ln1_g", D), ("ln1_b", D), ("ln2_g", D), ("ln2_b", D), ("conv_b", D_FF), ("conv_w", 3 * D_FF), ("loss", 128))
SMALL_FFN_AT = 520
SMALL_ROWS = 704


def _small_rows():
    rows, at = {"w_pool": 0, "pool_scale": GROUPS * DH}, SMALL_FFN_AT
    for k, size in SMALL_FFN:
        rows[k] = at
        at += size // 128
    return rows


def kernel(x, w_in, w_pool, pool_scale, w_out, ln1_g, ln1_b, w_up, conv_w, conv_b, w_down, ln2_g, ln2_b, loss_target, m_w_in, m_w_pool, m_pool_scale, m_w_out, m_ln1_g, m_ln1_b, m_w_up, m_conv_w, m_conv_b, m_w_down, m_ln2_g, m_ln2_b, v_w_in, v_w_pool, v_pool_scale, v_w_out, v_ln1_g, v_ln1_b, v_w_up, v_conv_w, v_conv_b, v_w_down, v_ln2_g, v_ln2_b):
    me = 4 * lax.axis_index("x") + 2 * lax.axis_index("y") + lax.axis_index("c")
    x2, tgt = x[0], loss_target[0]

    cos, sin = _rope_tables()
    dmat, qd, kd, cdec = _decay_tables(RET_TILE)

    qkv, g, oret, states, cat, pooled, xhat1, rstd1, x1b, xb, g_in, g_out, g_up, g_down, g_cw = _mix_forward(
        x2, w_in[0].T, w_out[0], cos, sin, dmat, qd, kd, cdec, w_pool[0], pool_scale, ln1_g, ln1_b,
        gather_bf16=[w_up[0].T, w_down[0]], gather=[jnp.transpose(conv_w, (1, 0, 2))])
    w_in_t = g_in.reshape(IN_W, D)
    w_out_f = g_out.reshape(D, D)
    w_up_t = g_up.reshape(2 * D_FF, D)
    w_down_f = g_down.reshape(D_FF, D)
    conv_w_f = jnp.transpose(g_cw[:, :, 0, :], (1, 0, 2)).reshape(3, D_FF)
    dz1, dz2b, du, f, loss8, d_ln2_g, d_ln2_b, d_ln1_g, d_ln1_b, d_conv_b, d_conv_w = _ffn_forward_backward(
        xhat1, rstd1, ln1_g, ln1_b, w_up_t, conv_w_f, conv_b, w_down_f, ln2_g, ln2_b, tgt)
    small_ffn = [d_ln1_g, d_ln1_b, d_ln2_g, d_ln2_b, d_conv_b, d_conv_w, loss8]

    (dw_down,) = _weight_grad(f, dz2b, "grad_w_down", tm=D_FF // 2)
    dw_up_t, own_down, arr_down = _weight_grad(du, x1b, "grad_w_up", tm=D_FF // 2,
                                               reduce=[dw_down.reshape(N_DEV, ROWS_DOWN, D)])
    own_up, oth_up = _pair_reduce([dw_up_t.reshape(N_DEV, ROWS_UP, D)], "pair_reduce_up")
    up_sems, up_src, up_land, up_started = _exchange_start(oth_up, "exchange_up_start", CHIP_BARRIER_SPLIT)
    dproj, grad_x, small, dw_out = _mix_backward(
        dz1, w_out_f, qkv, g, oret, states, pooled, cat, cos, sin, dmat, qd, kd, cdec, w_pool[0], pool_scale, w_in_t,
        small_ffn, after=up_started)
    dw_in_t, own_out, own_small, arr_out, arr_small = _weight_grad(
        dproj, xb, "grad_w_in", tm=IN_W // 4,
        reduce=[dw_out.reshape(N_DEV, ROWS_OUT, D), small.reshape(N_DEV, SMALL_ROWS // N_DEV, 128)])
    arr_up = _exchange_wait(up_sems, up_src, up_land, [dw_in_t], "exchange_up_wait")
    own_in, oth_in, gs_small = _pair_reduce([dw_in_t.reshape(N_DEV, ROWS_IN, D)], "pair_reduce_in",
                                            gather_sum=(own_small, arr_small))
    in_sems, in_src, in_land, started = _exchange_start(oth_in, "exchange_in_start", CHIP_BARRIER)

    names = ["w_in", "w_pool", "pool_scale", "w_out", "ln1_g", "ln1_b", "w_up", "conv_w", "conv_b", "w_down",
             "ln2_g", "ln2_b"]
    w_d = dict(w_in=w_in, w_pool=w_pool, pool_scale=pool_scale, w_out=w_out, ln1_g=ln1_g, ln1_b=ln1_b, w_up=w_up,
               conv_w=conv_w, conv_b=conv_b, w_down=w_down, ln2_g=ln2_g, ln2_b=ln2_b)
    m_d = dict(w_in=m_w_in, w_pool=m_w_pool, pool_scale=m_pool_scale, w_out=m_w_out, ln1_g=m_ln1_g, ln1_b=m_ln1_b,
               w_up=m_w_up, conv_w=m_conv_w, conv_b=m_conv_b, w_down=m_w_down, ln2_g=m_ln2_g, ln2_b=m_ln2_b)
    v_d = dict(w_in=v_w_in, w_pool=v_w_pool, pool_scale=v_pool_scale, w_out=v_w_out, ln1_g=v_ln1_g, ln1_b=v_ln1_b,
               w_up=v_w_up, conv_w=v_conv_w, conv_b=v_conv_b, w_down=v_w_down, ln2_g=v_ln2_g, ln2_b=v_ln2_b)
    g_d, delta, new_m, new_v = {}, {}, {}, {}

    def big_adamw(k, own, arr, transposed, steps, after=()):
        lay = (lambda a: a[0].T) if transposed else (lambda a: a[0])
        back = (lambda a: a.T[None]) if transposed else (lambda a: a[None])
        res = _sum_adamw(own, arr, lay(w_d[k]), lay(m_d[k]), lay(v_d[k]), "adamw_" + k, steps, after)
        g_d[k], delta[k], new_m[k], new_v[k] = (back(r) for r in res)
        return res[3]

    done = [big_adamw("w_up", own_up, arr_up, True, 4, after=(started,)),
            big_adamw("w_down", own_down, arr_down, False, 2, after=(started,)),
            big_adamw("w_out", own_out, arr_out, False, 2, after=(started,))]

    gs_small, rows = gs_small.reshape(SMALL_ROWS, 128), _small_rows()
    g_conv_w = gs_small[rows["conv_w"]:rows["conv_w"] + 3 * D_FF // 128].reshape(3, D_FF)
    g_d["conv_w"] = lax.dynamic_slice(g_conv_w, (0, me * (D_FF // N_DEV)), (3, D_FF // N_DEV))[None]
    lay = lambda k, a: jnp.transpose(a, (1, 0, 2)) if k == "conv_w" else a.reshape(-1, a.shape[-1])
    back = lambda k, a: jnp.transpose(a, (1, 0, 2)) if k == "conv_w" else a.reshape(w_d[k].shape)
    group = [k for k in names if k not in ("w_in", "w_out", "w_up", "w_down")]
    packed = [k for k in group if k != "conv_w"]
    res = _adamw([lay(k, w_d[k]) for k in group], [lay(k, g_d[k]) if k == "conv_w" else rows[k] for k in group],
                 [lay(k, m_d[k]) for k in group], [lay(k, v_d[k]) for k in group], gs_small, rows["loss"],
                 "adamw_small", after=(started,))
    for j, k in enumerate(packed):
        g_d[k] = back(k, res[j])
    for j, k in enumerate(group):
        delta[k], new_m[k], new_v[k] = (back(k, res[len(packed) + part * len(group) + j]) for part in range(3))

    arr_in = _exchange_wait(in_sems, in_src, in_land, done + [res[0]], "exchange_in_wait")
    big_adamw("w_in", own_in, arr_in, True, 4)

    loss = res[-1].reshape(())
    return (loss, grad_x[None], *[g_d[k] for k in names], *[delta[k] for k in names], *[new_m[k] for k in names],
            *[new_v[k] for k in names])
```

```python
import math

import numpy as np
import jax
import jax.numpy as jnp
from jax import lax
from jax.experimental import pallas as pl
from jax.experimental.pallas import tpu as pltpu

f32 = jnp.float32
bf16 = jnp.bfloat16

N_DEV = 8
T = 4096
D = 1024
CHUNK = 64
MIX_TILE = 512
RET_TILE = 256
HEADS = 4
DH = 128
RW = HEADS * DH
PW = 512
GROUPS = 4
WINDOWS = (2, 4, 8, 16)
IN_W = 4 * RW + PW
D_FF = 2816
LN_EPS = 1e-5
RMS_EPS = 1e-6
ALPHA = 2.0 ** 0.25
K_SCALE = DH ** -0.5

ADAM_LR = 0.001
ADAM_B1 = 0.9
ADAM_B2 = 0.999
ADAM_EPS = 1e-08
ADAM_WD = 0.01
ADAM_STEP = 10

ROWS_IN, ROWS_OUT, ROWS_UP, ROWS_DOWN = IN_W // N_DEV, D // N_DEV, 2 * D_FF // N_DEV, D_FF // N_DEV

V7X_VMEM_LIMIT = 56 * 2 ** 20
HALO = 32

NT = (((1,), (1,)), ((), ()))
TN = (((0,), (0,)), ((), ()))
NN = (((1,), (0,)), ((), ()))


def _dot(a, b, dims=NN):
    return lax.dot_general(a, b, dims, preferred_element_type=f32)


def _const_spec(shape):
    zeros = (0,) * len(shape)
    return pl.BlockSpec(shape, lambda i: zeros, pipeline_mode=pl.Buffered(1))


def _sigmoid(x):
    return 0.5 * jnp.tanh(0.5 * x) + 0.5


def _decay_tables(tt):
    h = np.arange(HEADS, dtype=np.float64)
    log_gamma = np.log(1.0 - 2.0 ** (-5.0 - h)).astype(np.float32).astype(np.float64)[:, None, None]
    idx = np.arange(tt, dtype=np.float64)
    visible = (idx[None, :] // CHUNK) <= (idx[:, None] // CHUNK)
    mask = np.where(visible[None], np.exp(log_gamma * np.abs(idx[:, None] - idx[None, :])[None]), 0.0)
    qd = np.broadcast_to(np.exp(log_gamma * (idx[None, :, None] + 1.0)), (HEADS, tt, DH))
    kd = np.broadcast_to(np.exp(log_gamma * (tt - 1.0 - idx[None, :, None])), (HEADS, tt, DH))
    cd = np.exp(log_gamma[:, 0, 0] * tt)
    return (jnp.asarray(mask, f32), jnp.asarray(qd, f32), jnp.asarray(kd, f32), [float(c) for c in cd])


def _rope_tables():
    inv_freq = (10000.0 ** (-np.arange(0, DH, 2, dtype=np.float64) / DH)).astype(np.float32)
    ang = (np.arange(T, dtype=np.float32)[:, None] * inv_freq[None, :]).astype(np.float64)
    cos, sin = np.cos(ang), np.sin(ang)
    return (jnp.asarray(np.concatenate([cos, cos], axis=1), f32), jnp.asarray(np.concatenate([-sin, sin], axis=1), f32))


def _swap_halves(t):
    return pltpu.roll(t, DH // 2, axis=1)


def _mix_forward(x, w_in_shard, w_out_shard, cos, sin, dmat, qd, kd, cdec, w_pool, pool_scale, ln1_g, ln1_b,
                 gather_bf16, gather, tt=MIX_TILE):
    n_tiles = T // tt
    to_bf16 = [w_in_shard, w_out_shard] + list(gather_bf16)
    n_c, n_g = len(to_bf16), len(gather_bf16) + len(gather)

    def body(x_ref, cos_ref, sin_ref, dmat_ref, qd_ref, kd_ref, wpool_ref, pscale_ref, g1_ref, b1_ref, *rest):
        f32_in, plain_in, rest = rest[:n_c], rest[n_c:2 + n_g], rest[2 + n_g:]
        qkv_ref, g_ref, oret_ref, states_ref, cat_ref, pooled_ref, xhat_ref, rstd_ref, x1b_ref, xb_ref = rest[:10]
        fout, gout = rest[10:12], rest[12:12 + n_g]
        state_s, pext_s, tmp_s, wint_s, wout_s, load_sems, stage_sems, *rest = rest[12 + n_g:]
        stage_s, cast_s, sems = rest[:n_c], rest[n_c:2 * n_c], rest[2 * n_c:]
        fin, gin, fsems, gsems = cast_s[:2], tuple(cast_s[2:]) + tuple(plain_in), sems[:3], sems[3:]
        i = pl.program_id(0)

        @pl.when(i == 0)
        def _():
            stage = [pltpu.make_async_copy(src, dst, stage_sems.at[j]) for j, (src, dst) in enumerate(zip(f32_in, stage_s))]
            for cp in stage:
                cp.start()
            state_s[...] = jnp.zeros_like(state_s)
            pext_s[:, pl.ds(0, HALO), :] = jnp.zeros((GROUPS, HALO, DH), f32)

            def cast(js):
                for j in js:
                    stage[j].wait()
                    cast_s[j][...] = stage_s[j][...].astype(bf16)

            _barrier(_gather_peers())
            cast(range(2))
            _gather_start(fin, fout, *fsems)
            cast(range(2, n_c))
            _gather_forward(fin, fout, *fsems)
            _gather_start(gin, gout, *gsems)
            _gather_finish(fin, fout, *fsems)
            loads = [pltpu.make_async_copy(src.at[s], dst.at[pl.ds(s * src.shape[1], src.shape[1]), :],
                                           load_sems.at[j, s])
                     for j, (src, dst) in enumerate(((fout[0], wint_s), (fout[1], wout_s))) for s in range(N_DEV)]
            for ld in loads:
                ld.start()
            for ld in loads:
                ld.wait()

        @pl.when(i == n_tiles - 3)
        def _():
            _gather_forward(gin, gout, *gsems)

        xb = x_ref[...].astype(bf16)
        xb_ref[...] = xb
        cos_t, sin_t = cos_ref[...], sin_ref[...]
        for part in range(2):
            pr = _dot(xb, wint_s[pl.ds(part * RW, RW), :], NT)
            for h in range(HEADS):
                t = pr[:, h * DH:(h + 1) * DH]
                r = t * cos_t + _swap_halves(t) * sin_t
                if part == 1:
                    r = r * K_SCALE
                qkv_ref[:, part * RW + h * DH: part * RW + (h + 1) * DH] = r.astype(bf16)
        qkv_ref[:, 2 * RW:3 * RW] = _dot(xb, wint_s[pl.ds(2 * RW, RW), :], NT).astype(bf16)
        g_ref[...] = _dot(xb, wint_s[pl.ds(3 * RW, RW), :], NT)
        p = _dot(xb, wint_s[pl.ds(4 * RW, PW), :], NT)
        for gi in range(GROUPS):
            pext_s[gi, pl.ds(HALO, tt), :] = p[:, gi * DH:(gi + 1) * DH]

        for sub in range(tt // RET_TILE):
            rows = pl.ds(sub * RET_TILE, RET_TILE)
            for h in range(HEADS):
                q = qkv_ref[rows, h * DH:(h + 1) * DH]
                k = qkv_ref[rows, RW + h * DH: RW + (h + 1) * DH]
                v = qkv_ref[rows, 2 * RW + h * DH: 2 * RW + (h + 1) * DH]
                s = _dot(q, k, NT) * dmat_ref[h]
                st = state_s[h]
                stb = st.astype(bf16)
                states_ref[sub, h] = stb
                oret_ref[rows, h * DH:(h + 1) * DH] = (_dot(s.astype(bf16), v)
                                                      + _dot((q.astype(f32) * qd_ref[h]).astype(bf16), stb))
                state_s[h] = st * cdec[h] + _dot((k.astype(f32) * kd_ref[h]).astype(bf16), v, TN)

        for h in range(HEADS):
            sl = slice(h * DH, (h + 1) * DH)
            o = oret_ref[:, sl]
            r = lax.rsqrt(jnp.mean(o * o, axis=-1, keepdims=True) + RMS_EPS)
            gg = g_ref[:, sl]
            cat_ref[:, sl] = (o * r * (gg * _sigmoid(gg))).astype(bf16)

        pos1 = (i * tt + lax.broadcasted_iota(jnp.int32, (tt, 1), 0) + 1).astype(f32)
        for gi, w in enumerate(WINDOWS):
            sl = slice(gi * DH, (gi + 1) * DH)
            stages = int(math.log2(w))
            src = pext_s
            for s in range(stages):
                lo = HALO - 8 * (stages - 1 - s)
                n = tt + HALO - lo
                shift = 2 ** s
                val = src[gi, pl.ds(lo, n), :] + src[gi, pl.ds(lo - shift, n), :]
                if s == stages - 1:
                    wsum = val
                else:
                    tmp_s[gi, pl.ds(lo, n), :] = val
                    src = tmp_s
            p_g = pext_s[gi, pl.ds(HALO, tt), :]
            pooled = (wsum / jnp.minimum(pos1, float(w)) - p_g).astype(bf16)
            pooled_ref[:, sl] = pooled
            y = _dot(pooled, wpool_ref[gi].astype(bf16)) * pscale_ref[:, sl]
            cat_ref[:, RW + gi * DH: RW + (gi + 1) * DH] = y.astype(bf16)
        pext_s[:, pl.ds(0, HALO), :] = pext_s[:, pl.ds(tt, HALO), :]

        z = ALPHA * x_ref[...] + _dot(cat_ref[...], wout_s[...])
        mu = jnp.mean(z, axis=-1, keepdims=True)
        zc = z - mu
        rstd = lax.rsqrt(jnp.mean(zc * zc, axis=-1, keepdims=True) + LN_EPS)
        xhat = zc * rstd
        xhat_ref[...] = xhat
        rstd_ref[...] = rstd
        x1b_ref[...] = (xhat * g1_ref[...] + b1_ref[...]).astype(bf16)

        @pl.when(i == n_tiles - 1)
        def _():
            _gather_finish(gin, gout, *gsems)

    tile = lambda w: pl.BlockSpec((tt, w), lambda i: (i, 0))
    hbm = pl.BlockSpec(memory_space=pltpu.HBM)
    out_shape = (
        jax.ShapeDtypeStruct((T, 3 * RW), bf16),
        jax.ShapeDtypeStruct((T, RW), f32),
        jax.ShapeDtypeStruct((T, RW), f32),
        jax.ShapeDtypeStruct((T // RET_TILE, HEADS, DH, DH), bf16),
        jax.ShapeDtypeStruct((T, D), bf16),
        jax.ShapeDtypeStruct((T, PW), bf16),
        jax.ShapeDtypeStruct((T, D), f32),
        jax.ShapeDtypeStruct((T, 1), f32),
        jax.ShapeDtypeStruct((T, D), bf16),
        jax.ShapeDtypeStruct((T, D), bf16),
    ) + tuple(jax.ShapeDtypeStruct((N_DEV,) + b.shape, bf16) for b in to_bf16
              ) + tuple(jax.ShapeDtypeStruct((N_DEV,) + b.shape, b.dtype) for b in gather)
    return pl.pallas_call(
        body, name="mix_forward", grid=(n_tiles,), out_shape=out_shape,
        in_specs=[tile(D), tile(DH), tile(DH),
                  _const_spec((HEADS, RET_TILE, RET_TILE)), _const_spec((HEADS, RET_TILE, DH)),
                  _const_spec((HEADS, RET_TILE, DH)),
                  _const_spec((GROUPS, DH, DH)), _const_spec((1, PW)),
                  _const_spec((1, D)), _const_spec((1, D))] + [hbm] * (2 + n_g),
        out_specs=(tile(3 * RW), tile(RW), tile(RW),
                   pl.BlockSpec((tt // RET_TILE, HEADS, DH, DH), lambda i: (i, 0, 0, 0)),
                   tile(D), tile(PW), tile(D), tile(1), tile(D), tile(D)) + (hbm,) * (2 + n_g),
        scratch_shapes=[pltpu.VMEM((HEADS, DH, DH), f32), pltpu.VMEM((GROUPS, tt + HALO, DH), f32),
                        pltpu.VMEM((GROUPS, tt + HALO, DH), f32), pltpu.VMEM((IN_W, D), bf16), pltpu.VMEM((D, D), bf16),
                        pltpu.SemaphoreType.DMA((2, N_DEV)), pltpu.SemaphoreType.DMA((n_c,))]
        + [pltpu.VMEM(b.shape, f32) for b in to_bf16] + [pltpu.VMEM(b.shape, bf16) for b in to_bf16]
        + _gather_sems(2) + _gather_sems(n_g),
        compiler_params=pltpu.CompilerParams(dimension_semantics=("arbitrary",), vmem_limit_bytes=V7X_VMEM_LIMIT,
                                             collective_id=GATHER_BARRIER),
    )(x, cos, sin, dmat, qd, kd, w_pool, pool_scale, ln1_g, ln1_b, *to_bf16, *gather)


def _ffn_forward_backward(xhat1, rstd1, ln1_g, ln1_b, w_up_t, conv_w, conv_b, w_down, ln2_g, ln2_b, target,
                          tt=256):
    n_tiles = T // tt
    FH = 16
    hb = tt // FH

    def body(xhat_ref, halo_ref, rstd_ref, g1_ref, b1_ref, wupt_ref, cw_ref, cb_ref, wdown_ref, g2_ref, b2_ref, tgt_ref,
             dz1_ref, dz2b_ref, du_ref, f_ref, loss_ref, dg2_ref, db2_ref, dg1_ref, db1_ref, dcb_ref, dcw_ref,
             gext_s, val_s, dhext_s):
        i = pl.program_id(0)
        tile_idx = n_tiles - 1 - i

        def rd(ref, off):
            return jnp.concatenate([ref[k, pl.ds(off, tt), :] for k in range(D_FF // 128)], axis=1)

        def wr(ref, val):
            for k in range(D_FF // 128):
                ref[k, pl.ds(0, val.shape[0]), :] = val[:, k * 128:(k + 1) * 128]

        @pl.when(i == 0)
        def _():
            for r in (loss_ref, dg2_ref, db2_ref, dg1_ref, db1_ref, dcb_ref, dcw_ref):
                r[...] = jnp.zeros_like(r)
            dhext_s[:, pl.ds(tt, 8), :] = jnp.zeros((D_FF // 128, 8, 128), f32)

        g1, b1 = g1_ref[...], b1_ref[...]
        xhat = xhat_ref[...]
        x1 = xhat * g1 + b1
        x1b = x1.astype(bf16)
        x1h = ((halo_ref[...] * g1 + b1) * jnp.where(tile_idx == 0, 0.0, 1.0)).astype(bf16)
        x1ext = jnp.concatenate([x1h, x1b], axis=0)

        val = _dot(x1b, wupt_ref[pl.ds(0, D_FF), :], NT)
        gate_ext = _dot(x1ext, wupt_ref[pl.ds(D_FF, D_FF), :], NT)
        wr(gext_s, gate_ext)
        hh = (cb_ref[...] + cw_ref[0:1, :] * rd(gext_s, FH - 2) + cw_ref[1:2, :] * rd(gext_s, FH - 1)
              + cw_ref[2:3, :] * gate_ext[FH:])
        sg = _sigmoid(hh)
        act = hh * sg
        wr(dhext_s, act)
        val_s[...] = val * (sg + act * (1.0 - sg))
        fb = (act * val).astype(bf16)
        f_ref[...] = fb

        z = ALPHA * x1 + _dot(fb, wdown_ref[...])
        mu = jnp.mean(z, axis=-1, keepdims=True)
        zc = z - mu
        rstd2 = lax.rsqrt(jnp.mean(zc * zc, axis=-1, keepdims=True) + LN_EPS)
        xh2 = zc * rstd2
        diff = xh2 * g2_ref[...] + b2_ref[...] - tgt_ref[...]
        loss_ref[...] += 0.5 * jnp.sum(diff * diff) / D
        dy = diff * (1.0 / D)
        dg2_ref[...] += jnp.sum(dy * xh2, axis=0, keepdims=True)
        db2_ref[...] += jnp.sum(dy, axis=0, keepdims=True)
        dyg = dy * g2_ref[...]
        dz2 = rstd2 * (dyg - jnp.mean(dyg, axis=-1, keepdims=True) - xh2 * jnp.mean(dyg * xh2, axis=-1, keepdims=True))
        dz2b = dz2.astype(bf16)
        dz2b_ref[...] = dz2b

        df = _dot(dz2b, wdown_ref[...], NT)
        dval = df * rd(dhext_s, 0)
        dh = df * val_s[...]
        wr(dhext_s, dh)
        dh1, dh2, g0 = rd(dhext_s, 1), rd(dhext_s, 2), rd(gext_s, FH)
        dcb_ref[...] += jnp.sum(dh, axis=0, keepdims=True)
        dcw_ref[0:1, :] += jnp.sum(dh2 * g0, axis=0, keepdims=True)
        dcw_ref[1:2, :] += jnp.sum(dh1 * g0, axis=0, keepdims=True)
        dcw_ref[2:3, :] += jnp.sum(dh * g0, axis=0, keepdims=True)
        dgate = cw_ref[2:3, :] * dh + cw_ref[1:2, :] * dh1 + cw_ref[0:1, :] * dh2
        dvalb, dgateb = dval.astype(bf16), dgate.astype(bf16)
        du_ref[:, :D_FF] = dvalb
        du_ref[:, D_FF:] = dgateb
        dx1 = ALPHA * dz2 + _dot(dvalb, wupt_ref[pl.ds(0, D_FF), :]) + _dot(dgateb, wupt_ref[pl.ds(D_FF, D_FF), :])
        dhext_s[:, pl.ds(tt, 8), :] = dhext_s[:, pl.ds(0, 8), :]

        dg1_ref[...] += jnp.sum(dx1 * xhat, axis=0, keepdims=True)
        db1_ref[...] += jnp.sum(dx1, axis=0, keepdims=True)
        dxg = dx1 * g1
        dz1_ref[...] = rstd_ref[...] * (dxg - jnp.mean(dxg, axis=-1, keepdims=True)
                                        - xhat * jnp.mean(dxg * xhat, axis=-1, keepdims=True))

    rtile = lambda w: pl.BlockSpec((tt, w), lambda i: (n_tiles - 1 - i, 0))
    acc = lambda shape: pl.BlockSpec(shape, lambda i: (0, 0))
    out_shape = (
        jax.ShapeDtypeStruct((T, D), f32),
        jax.ShapeDtypeStruct((T, D), bf16),
        jax.ShapeDtypeStruct((T, 2 * D_FF), bf16),
        jax.ShapeDtypeStruct((T, D_FF), bf16),
        jax.ShapeDtypeStruct((8, 128), f32),
        jax.ShapeDtypeStruct((1, D), f32), jax.ShapeDtypeStruct((1, D), f32),
        jax.ShapeDtypeStruct((1, D), f32), jax.ShapeDtypeStruct((1, D), f32),
        jax.ShapeDtypeStruct((1, D_FF), f32), jax.ShapeDtypeStruct((3, D_FF), f32),
    )
    return pl.pallas_call(
        body, name="ffn_forward_backward", grid=(n_tiles,), out_shape=out_shape,
        in_specs=[rtile(D),
                  pl.BlockSpec((FH, D), lambda i: (jnp.maximum((n_tiles - 1 - i) * hb - 1, 0), 0)),
                  rtile(1), _const_spec((1, D)), _const_spec((1, D)), _const_spec((2 * D_FF, D)),
                  _const_spec((3, D_FF)), _const_spec((1, D_FF)), _const_spec((D_FF, D)),
                  _const_spec((1, D)), _const_spec((1, D)), rtile(D)],
        out_specs=(rtile(D), rtile(D), rtile(2 * D_FF), rtile(D_FF), acc((8, 128)),
                   acc((1, D)), acc((1, D)), acc((1, D)), acc((1, D)), acc((1, D_FF)), acc((3, D_FF))),
        scratch_shapes=[pltpu.VMEM((D_FF // 128, tt + FH, 128), f32), pltpu.VMEM((tt, D_FF), f32),
                        pltpu.VMEM((D_FF // 128, tt + 8, 128), f32)],
        compiler_params=pltpu.CompilerParams(dimension_semantics=("arbitrary",), vmem_limit_bytes=V7X_VMEM_LIMIT),
    )(xhat1, xhat1, rstd1, ln1_g, ln1_b, w_up_t, conv_w, conv_b, w_down, ln2_g, ln2_b, target)


def _mix_backward(dz1, w_out, qkv, g, oret, states, pooled, cat, cos, sin, dmat, qd, kd, cdec, w_pool, pool_scale,
                  small_ffn, reduce, tt=MIX_TILE):
    n_tiles, n_r = T // tt, len(reduce)

    def body(dz1_ref, wout_ref, qkv_ref, g_ref, oret_ref, states_ref, pooled_ref, cat_ref, cos_ref, sin_ref, dmat_ref,
             qd_ref, kd_ref, wpool_ref, pscale_ref, *rest):
        ffn_refs, ins, rest = rest[:len(SMALL_FFN)], rest[len(SMALL_FFN):len(SMALL_FFN) + n_r], rest[len(SMALL_FFN) + n_r:]
        (dproj_ref, small_ref, dwout_ref), hosted_out, rest = rest[:3], rest[3:3 + 2 * n_r], rest[3 + 2 * n_r:]
        dstate_s, dout_s, eext_s, tmp_s, dwout_s, dpscale_s = rest[:6]
        i = pl.program_id(0)
        tile_idx = n_tiles - 1 - i
        finish_reduce = _hosted_reduce(i, n_tiles, ins, tuple(hosted_out) + tuple(rest[6:]))

        @pl.when(i == 0)
        def _():
            dstate_s[...] = jnp.zeros_like(dstate_s)
            small_ref[...] = jnp.zeros_like(small_ref)
            dpscale_s[...] = jnp.zeros_like(dpscale_s)
            dwout_s[...] = jnp.zeros_like(dwout_s)
            eext_s[:, pl.ds(tt, HALO), :] = jnp.zeros((GROUPS, HALO, DH), f32)

        dz1 = dz1_ref[...]
        dz1b = dz1.astype(bf16)
        dcat = _dot(dz1b, wout_ref[...], NT)
        dwout_s[...] += _dot(cat_ref[...], dz1b, TN)

        pos1 = (tile_idx * tt + lax.broadcasted_iota(jnp.int32, (tt, 1), 0) + 1).astype(f32)
        for gi, w in enumerate(WINDOWS):
            sl = slice(gi * DH, (gi + 1) * DH)
            dpo = dcat[:, RW + gi * DH: RW + (gi + 1) * DH]
            pooled_g = pooled_ref[:, sl]
            wpool_g = wpool_ref[gi].astype(bf16)
            ylin = _dot(pooled_g, wpool_g)
            dpscale_s[:, sl] += jnp.sum(dpo * ylin, axis=0, keepdims=True)
            dpw = (dpo * pscale_ref[:, sl]).astype(bf16)
            small_ref[pl.ds(gi * DH, DH), :] += _dot(pooled_g, dpw, TN)
            dpooled = _dot(dpw, wpool_g, NT)
            eext_s[gi, pl.ds(0, tt), :] = dpooled / jnp.minimum(pos1, float(w))
            stages = int(math.log2(w))
            src = eext_s
            for s in range(stages):
                n = tt + 8 * (stages - 1 - s)
                shift = 2 ** s
                val = src[gi, pl.ds(0, n), :] + src[gi, pl.ds(shift, n), :]
                if s == stages - 1:
                    wsum = val
                else:
                    tmp_s[gi, pl.ds(0, n), :] = val
                    src = tmp_s
            dproj_ref[:, 4 * RW + gi * DH: 4 * RW + (gi + 1) * DH] = (wsum - dpooled).astype(bf16)
        eext_s[:, pl.ds(tt, HALO), :] = eext_s[:, pl.ds(0, HALO), :]

        for h in range(HEADS):
            sl = slice(h * DH, (h + 1) * DH)
            dr = dcat[:, sl]
            o = oret_ref[:, sl]
            r = lax.rsqrt(jnp.mean(o * o, axis=-1, keepdims=True) + RMS_EPS)
            rn = o * r
            gg = g_ref[:, sl]
            sg = _sigmoid(gg)
            dproj_ref[:, 3 * RW + h * DH: 3 * RW + (h + 1) * DH] = (dr * rn * (sg * (1.0 + gg * (1.0 - sg)))).astype(bf16)
            drn = dr * (gg * sg)
            dout_s[:, sl] = (r * (drn - rn * jnp.mean(drn * rn, axis=-1, keepdims=True))).astype(bf16)

        for sub in reversed(range(tt // RET_TILE)):
            rows = pl.ds(sub * RET_TILE, RET_TILE)
            cos_t, sin_t = cos_ref[rows, :], sin_ref[rows, :]
            for h in range(HEADS):
                q = qkv_ref[rows, h * DH:(h + 1) * DH]
                k = qkv_ref[rows, RW + h * DH: RW + (h + 1) * DH]
                v = qkv_ref[rows, 2 * RW + h * DH: 2 * RW + (h + 1) * DH]
                do = dout_s[rows, h * DH:(h + 1) * DH]
                stb = states_ref[sub, h]
                dst = dstate_s[h]
                dstb = dst.astype(bf16)
                sb = (_dot(q, k, NT) * dmat_ref[h]).astype(bf16)
                dsb = (_dot(do, v, NT) * dmat_ref[h]).astype(bf16)
                dq = _dot(dsb, k) + _dot(do, stb, NT) * qd_ref[h]
                dk = _dot(dsb, q, TN) + _dot(v, dstb, NT) * kd_ref[h]
                dv = _dot(sb, do, TN) + _dot((k.astype(f32) * kd_ref[h]).astype(bf16), dstb)
                dstate_s[h] = dst * cdec[h] + _dot((q.astype(f32) * qd_ref[h]).astype(bf16), do, TN)
                dproj_ref[rows, h * DH:(h + 1) * DH] = (dq * cos_t - _swap_halves(dq) * sin_t).astype(bf16)
                dproj_ref[rows, RW + h * DH: RW + (h + 1) * DH] = (
                    (dk * cos_t - _swap_halves(dk) * sin_t) * K_SCALE).astype(bf16)
                dproj_ref[rows, 2 * RW + h * DH: 2 * RW + (h + 1) * DH] = dv.astype(bf16)

        finish_reduce()

        @pl.when(i == n_tiles - 1)
        def _():
            dwout_ref[...] = dwout_s[...].astype(bf16)
            at = GROUPS * DH
            for ref, size in [(dpscale_s, PW)] + [(ref, size) for ref, (_, size) in zip(ffn_refs, SMALL_FFN)]:
                for j in range(size // 128):
                    r, k = divmod(j, ref.shape[1] // 128)
                    small_ref[at + j: at + j + 1, :] = ref[r:r + 1, k * 128:(k + 1) * 128]
                at = SMALL_FFN_AT if ref is dpscale_s else at + size // 128

    rtile = lambda w: pl.BlockSpec((tt, w), lambda i: (n_tiles - 1 - i, 0))
    out_shape = (
        jax.ShapeDtypeStruct((T, IN_W), bf16),
        jax.ShapeDtypeStruct((SMALL_ROWS, 128), f32),
        jax.ShapeDtypeStruct((D, D), bf16),
    )
    hbm = pl.BlockSpec(memory_space=pltpu.HBM)
    r_out, r_scratch = _hosted_reduce_shapes(reduce)
    return pl.pallas_call(
        body, name="mix_backward", grid=(n_tiles,), out_shape=out_shape + r_out,
        in_specs=[rtile(D), _const_spec((D, D)), rtile(3 * RW), rtile(RW), rtile(RW),
                  pl.BlockSpec((tt // RET_TILE, HEADS, DH, DH), lambda i: (n_tiles - 1 - i, 0, 0, 0)),
                  rtile(PW), rtile(D), rtile(DH), rtile(DH),
                  _const_spec((HEADS, RET_TILE, RET_TILE)), _const_spec((HEADS, RET_TILE, DH)),
                  _const_spec((HEADS, RET_TILE, DH)),
                  _const_spec((GROUPS, DH, DH)), _const_spec((1, PW)),
                  *[_const_spec(a.shape) for a in small_ffn]] + [hbm] * n_r,
        out_specs=(rtile(IN_W), pl.BlockSpec((SMALL_ROWS, 128), lambda i: (0, 0)),
                   pl.BlockSpec((D, D), lambda i: (0, 0), pipeline_mode=pl.Buffered(1))) + (hbm,) * len(r_out),
        scratch_shapes=[pltpu.VMEM((HEADS, DH, DH), f32), pltpu.VMEM((tt, RW), bf16),
                        pltpu.VMEM((GROUPS, tt + HALO, DH), f32), pltpu.VMEM((GROUPS, tt + HALO, DH), f32),
                        pltpu.VMEM((D, D), f32), pltpu.VMEM((1, PW), f32)] + r_scratch,
        compiler_params=pltpu.CompilerParams(dimension_semantics=("arbitrary",), vmem_limit_bytes=V7X_VMEM_LIMIT,
                                             collective_id=REDUCE_BARRIER),
    )(dz1, w_out, qkv, g, oret, states, pooled, cat, cos, sin, dmat, qd, kd, w_pool, pool_scale, *small_ffn, *reduce)


def _grad_x(dz1, dproj, w_in_t, tt=MIX_TILE):
    def body(dz1_ref, dproj_ref, wint_ref, gx_ref):
        gx_ref[...] = ALPHA * dz1_ref[...] + _dot(dproj_ref[...], wint_ref[...])

    tile = lambda w: pl.BlockSpec((tt, w), lambda i: (i, 0))
    return pl.pallas_call(
        body, name="grad_x", grid=(T // tt,), out_shape=jax.ShapeDtypeStruct((T, D), f32),
        in_specs=[tile(D), tile(IN_W), _const_spec((IN_W, D))], out_specs=tile(D),
        compiler_params=pltpu.CompilerParams(dimension_semantics=("parallel",), vmem_limit_bytes=V7X_VMEM_LIMIT),
    )(dz1, dproj, w_in_t)


def _weight_grad(a, b, name, tm, reduce=()):
    m = a.shape[1]
    n_m, n_r = m // tm, len(reduce)

    def body(a_ref, b_ref, *rest):
        ins, o_ref, hosted = rest[:n_r], rest[n_r], rest[n_r + 1:]
        finish = _hosted_reduce(pl.program_id(0), n_m, ins, hosted) if n_r else None
        o_ref[...] = _dot(a_ref[...], b_ref[...].astype(bf16), TN).astype(bf16)
        if n_r:
            finish()

    hbm = pl.BlockSpec(memory_space=pltpu.HBM)
    r_out, r_scratch = _hosted_reduce_shapes(reduce)
    return pl.pallas_call(
        body, name=name, grid=(n_m,), out_shape=(jax.ShapeDtypeStruct((m, D), bf16),) + r_out,
        in_specs=[pl.BlockSpec((T, tm), lambda i: (0, i)),
                  pl.BlockSpec((T, D), lambda i: (0, 0), pipeline_mode=pl.Buffered(1))] + [hbm] * n_r,
        out_specs=(pl.BlockSpec((tm, D), lambda i: (i, 0)),) + (hbm,) * len(r_out), scratch_shapes=r_scratch,
        compiler_params=pltpu.CompilerParams(dimension_semantics=("arbitrary",), vmem_limit_bytes=V7X_VMEM_LIMIT,
                                             collective_id=REDUCE_BARRIER if n_r else None),
    )(a, b, *reduce)


def _hosted_reduce_shapes(reduce):
    n_r = len(reduce)
    out = (tuple(jax.ShapeDtypeStruct(p.shape[1:], p.dtype) for p in reduce)
           + tuple(jax.ShapeDtypeStruct((3,) + p.shape[1:], p.dtype) for p in reduce))
    scratch = [pltpu.VMEM((4,) + p.shape[1:], p.dtype) for p in reduce] * 2
    if n_r:
        scratch += [pltpu.SemaphoreType.DMA((4, n_r))] * 3 + [pltpu.SemaphoreType.DMA((3, n_r))] * 2
    return out, scratch


def _hosted_reduce(i, n_steps, ins, refs):
    assert n_steps >= 2
    n_r = len(ins)
    own, arrived, landing, mine = (refs[k * n_r:(k + 1) * n_r] for k in range(4))
    pair_send, pair_recv, local_sems, chip_send, chip_recv = refs[4 * n_r:]
    me = _me()
    x, y, c = me
    sibling = (x, y, 1 - c)

    def pair_copy(k, j):
        return _remote(ins[j].at[_slot(*_chip(me, k), 1 - c)], landing[j].at[k], pair_send.at[k, j],
                       pair_recv.at[k, j], sibling)

    def load(k, j):
        return pltpu.make_async_copy(ins[j].at[_slot(*_chip(me, k), c)], mine[j].at[k], local_sems.at[k, j])

    def store(j):
        return pltpu.make_async_copy(mine[j].at[0], own[j], local_sems.at[0, j])

    def chip_copy(k, j):
        return _remote(mine[j].at[k], arrived[j].at[k - 1], chip_send.at[k - 1, j], chip_recv.at[k - 1, j],
                       (*_chip(me, k), c))

    @pl.when(i == 0)
    def _():
        _barrier([sibling] + _chip_peers())
        for k in range(4):
            for j in range(n_r):
                pair_copy(k, j).start()
                load(k, j).start()

    @pl.when(i == 1)
    def _():
        for k in range(4):
            for j in range(n_r):
                load(k, j).wait()
                pair_copy(k, j).wait_recv()
                mine[j][k] = (mine[j][k].astype(f32) + landing[j][k].astype(f32)).astype(mine[j].dtype)
                (store(j) if k == 0 else chip_copy(k, j)).start()

    def finish():
        @pl.when(i == n_steps - 1)
        def _():
            for j in range(n_r):
                store(j).wait()
                for k in range(1, 4):
                    chip_copy(k, j).wait_recv()
            for j in range(n_r):
                for k in range(1, 4):
                    chip_copy(k, j).wait_send()
                for k in range(4):
                    pair_copy(k, j).wait_send()

    return finish


CHIP_FLIPS = ((1, 0), (0, 1), (1, 1))
PAIR_BARRIER, CHIP_BARRIER, GATHER_BARRIER, REDUCE_BARRIER = 0, 1, 2, 3


def _barrier(peers):
    sem = pltpu.get_barrier_semaphore()
    for peer in peers:
        pl.semaphore_signal(sem, inc=1, device_id=peer, device_id_type=pl.DeviceIdType.MESH)
    pl.semaphore_wait(sem, len(peers))


def _me():
    return lax.axis_index("x"), lax.axis_index("y"), lax.axis_index("c")


def _chip(me, k):
    x, y, _ = me
    if k == 0:
        return x, y
    fx, fy = CHIP_FLIPS[k - 1]
    return (1 - x if fx else x), (1 - y if fy else y)


def _slot(x, y, c):
    return 4 * x + 2 * y + c


def _remote(src, dst, send_sem, recv_sem, to):
    return pltpu.make_async_remote_copy(src_ref=src, dst_ref=dst, send_sem=send_sem, recv_sem=recv_sem,
                                        device_id=to, device_id_type=pl.DeviceIdType.MESH)


def _gather_sems(n):
    return [pltpu.SemaphoreType.DMA((7, n)), pltpu.SemaphoreType.DMA((7, n)), pltpu.SemaphoreType.DMA((n,))] if n else []


def _gather_copy(k, j, gin, gout, send_sems, recv_sems, sending):
    x, y, c = _me()
    sibling, x_chip, y_chip, d_chip = (x, y, 1 - c), (1 - x, y), (x, 1 - y), (1 - x, 1 - y)
    south = c == 0
    passed_on = (jnp.where(south, 1 - x, x), jnp.where(south, y, 1 - y), c)
    src, to = gin[j], sibling
    if sending:
        block = {0: (x, y, c), 1: (x, y, c), 2: (x, y, c), 3: passed_on, 4: (*x_chip, c), 5: (*y_chip, c), 6: (*d_chip, c)}[k]
        to = {1: (*x_chip, c), 2: (*y_chip, c), 3: (jnp.where(south, x, 1 - x), jnp.where(south, 1 - y, y), c)}.get(k, sibling)
        if k >= 3:
            src = gout[j].at[_slot(*block)]
    else:
        block = {0: sibling, 1: (*x_chip, c), 2: (*y_chip, c), 3: (*d_chip, c), 4: (*x_chip, 1 - c), 5: (*y_chip, 1 - c),
                 6: (*d_chip, 1 - c)}[k]
    return _remote(src, gout[j].at[_slot(*block)], send_sems.at[k, j], recv_sems.at[k, j], to)


def _gather_do(ks, action, gin, gout, send_sems, recv_sems):
    for k in ks:
        for j in range(len(gin)):
            cp = _gather_copy(k, j, gin, gout, send_sems, recv_sems, action != "wait_recv")
            getattr(cp, action)()


def _gather_peers():
    x, y, c = _me()
    return [(x, y, 1 - c), (1 - x, y, c), (x, 1 - y, c)]


def _gather_start(gin, gout, send_sems, recv_sems, local_sems):
    for j in range(len(gin)):
        pltpu.make_async_copy(gin[j], gout[j].at[_slot(*_me())], local_sems.at[j]).start()
    _gather_do((0, 1, 2), "start", gin, gout, send_sems, recv_sems)


def _gather_forward(gin, gout, send_sems, recv_sems, local_sems):
    _gather_do((1, 2), "wait_recv", gin, gout, send_sems, recv_sems)
    _gather_do((3, 4, 5), "start", gin, gout, send_sems, recv_sems)


def _gather_finish(gin, gout, send_sems, recv_sems, local_sems):
    _gather_do((3,), "wait_recv", gin, gout, send_sems, recv_sems)
    _gather_do((6,), "start", gin, gout, send_sems, recv_sems)
    _gather_do((0, 4, 5, 6), "wait_recv", gin, gout, send_sems, recv_sems)
    _gather_do(range(7), "wait_send", gin, gout, send_sems, recv_sems)
    for j in range(len(gin)):
        pltpu.make_async_copy(gin[j], gout[j].at[_slot(*_me())], local_sems.at[j]).wait()


def _pair_reduce(parts, name, gather_sum=None):
    n = len(parts)
    n_h = 0 if gather_sum is None else 1

    def body(*refs):
        ins, g_terms, refs = refs[:n], refs[n:n + 2 * n_h], refs[n + 2 * n_h:]
        own, others, g_out, refs = refs[:n], refs[n:2 * n], refs[2 * n:2 * n + n_h], refs[2 * n + n_h:]
        landing, mine, (send_sems, recv_sems, local_sems), g_scratch = refs[:n], refs[n:2 * n], refs[2 * n:2 * n + 3], refs[2 * n + 3:]
        me = _me()
        x, y, c = me
        sibling = (x, y, 1 - c)
        _barrier(_gather_peers() if n_h else [sibling])
        if n_h:
            piece_s, g_sems = g_scratch[0], g_scratch[1:]
            acc = g_terms[0][...].astype(f32)
            for k in range(3):
                acc = acc + g_terms[1][k].astype(f32)
            piece_s[...] = acc
            _gather_start([piece_s], g_out, *g_sems)
        sends, loads = [], []
        for k in range(4):
            for j in range(n):
                cp = _remote(ins[j].at[_slot(*_chip(me, k), 1 - c)], landing[j].at[k], send_sems.at[k, j],
                             recv_sems.at[k, j], sibling)
                cp.start()
                sends.append(cp)
                ld = pltpu.make_async_copy(ins[j].at[_slot(*_chip(me, k), c)], mine[j].at[k], local_sems.at[k, j])
                ld.start()
                loads.append(ld)
        if n_h:
            _gather_forward([piece_s], g_out, *g_sems)
        stores = []
        for k in range(4):
            for j in range(n):
                loads[k * n + j].wait()
                _remote(ins[j].at[0], landing[j].at[k], send_sems.at[k, j], recv_sems.at[k, j], sibling).wait_recv()
                mine[j][k] = (mine[j][k].astype(f32) + landing[j][k].astype(f32)).astype(mine[j].dtype)
                st = pltpu.make_async_copy(mine[j].at[k], own[j] if k == 0 else others[j].at[k - 1], local_sems.at[k, j])
                st.start()
                stores.append(st)
        for cp in sends:
            cp.wait_send()
        for st in stores:
            st.wait()
        if n_h:
            _gather_finish([piece_s], g_out, *g_sems)

    vm, hbm = pl.BlockSpec(memory_space=pltpu.VMEM), pl.BlockSpec(memory_space=pltpu.HBM)
    g_shape = gather_sum[0].shape if n_h else ()
    return pl.pallas_call(
        body, name=name,
        out_shape=tuple(jax.ShapeDtypeStruct(p.shape[1:], p.dtype) for p in parts)
        + tuple(jax.ShapeDtypeStruct((3,) + p.shape[1:], p.dtype) for p in parts)
        + tuple([jax.ShapeDtypeStruct((N_DEV,) + g_shape, f32)] * n_h),
        in_specs=[hbm] * n + [vm] * (2 * n_h), out_specs=(hbm,) * (2 * n + n_h),
        scratch_shapes=[pltpu.VMEM((4,) + p.shape[1:], p.dtype) for p in parts] * 2
        + [pltpu.SemaphoreType.DMA((4, n)), pltpu.SemaphoreType.DMA((4, n)), pltpu.SemaphoreType.DMA((4, n))]
        + ([pltpu.VMEM(g_shape, f32)] + _gather_sems(1)) * n_h,
        compiler_params=pltpu.CompilerParams(vmem_limit_bytes=V7X_VMEM_LIMIT,
                                             collective_id=GATHER_BARRIER if n_h else PAIR_BARRIER),
    )(*parts, *(gather_sum or ()))


def _chip_peers():
    me = _me()
    return [(*_chip(me, k), me[2]) for k in range(1, 4)]


def _split_copies(src_ref, dst_ref, sems):
    me = _me()
    return [_remote(src_ref.at[k - 1], dst_ref.at[k - 1], sems[k - 1], sems[2 + k], (*_chip(me, k), me[2]))
            for k in range(1, 4)]


def _exchange_start(others, name, barrier_id):
    def body(src_ref, land_ref, *rest):
        sems, token_ref = rest[:6], rest[8]
        _barrier(_chip_peers())
        for copy in _split_copies(src_ref, land_ref, sems):
            copy.start()
        token_ref[...] = jnp.zeros_like(token_ref)

    hbm, sem = pl.BlockSpec(memory_space=pltpu.HBM), pl.BlockSpec(memory_space=pltpu.SEMAPHORE)
    thru = pltpu.HBM(others.shape, others.dtype)
    res = pl.pallas_call(
        body, name=name,
        out_shape=(pltpu.SemaphoreType.DMA(()),) * 6 + (thru, thru, jax.ShapeDtypeStruct((8, 128), f32)),
        in_specs=(hbm, hbm), out_specs=(sem,) * 6 + (hbm, hbm, pl.BlockSpec(memory_space=pltpu.VMEM)),
        input_output_aliases={0: 6, 1: 7},
        compiler_params=pltpu.CompilerParams(has_side_effects=pltpu.SideEffectType.DATAFLOW_SIDE_EFFECTING,
                                             collective_id=barrier_id),
    )(pltpu.with_memory_space_constraint(others, pltpu.HBM),
      pltpu.with_memory_space_constraint(lax.empty(others.shape, others.dtype), pltpu.HBM))
    return res[:6], res[6], res[7], res[8]


def _exchange_wait(sems, src_thru, land_thru, after, name):
    n_after = len(after)

    def body(src_ref, land_ref, *rest):
        for copy in _split_copies(src_ref, land_ref, rest[:6]):
            copy.wait_send()
            copy.wait_recv()

    hbm, sem = pl.BlockSpec(memory_space=pltpu.HBM), pl.BlockSpec(memory_space=pltpu.SEMAPHORE)
    thru = pltpu.HBM(src_thru.shape, src_thru.dtype)
    return pl.pallas_call(
        body, name=name, out_shape=(thru, thru),
        in_specs=(hbm, hbm) + (sem,) * 6 + (pl.BlockSpec(memory_space=pl.ANY),) * n_after, out_specs=(hbm, hbm),
        input_output_aliases={0: 0, 1: 1},
        compiler_params=pltpu.CompilerParams(has_side_effects=pltpu.SideEffectType.DATAFLOW_SIDE_EFFECTING),
    )(src_thru, land_thru, *sems, *after)[1]


def _adam_update(w, g, m, v):
    m = ADAM_B1 * m + (1.0 - ADAM_B1) * g
    v = ADAM_B2 * v + (1.0 - ADAM_B2) * (g * g)
    m_hat = m / (1.0 - ADAM_B1 ** ADAM_STEP)
    v_hat = v / (1.0 - ADAM_B2 ** ADAM_STEP)
    return -ADAM_LR * (m_hat / (jnp.sqrt(v_hat) + ADAM_EPS) + ADAM_WD * w), m, v


def _sum_adamw(own, arrived, w, m, v, name, steps, after=()):
    rows = own.shape[0]
    br = rows // steps

    def body(own_ref, arr_ref, w_ref, m_ref, v_ref, *rest):
        g_out, d_out, m_out, v_out = rest[len(after):]
        g = own_ref[...].astype(f32)
        for k in range(3):
            g = g + arr_ref[k].astype(f32)
        g_out[...] = g
        d_out[...], m_out[...], v_out[...] = _adam_update(w_ref[...], g, m_ref[...], v_ref[...])

    blk = pl.BlockSpec((br, D), lambda i: (i, 0))
    return pl.pallas_call(
        body, name=name, grid=(steps,), out_shape=(jax.ShapeDtypeStruct((rows, D), f32),) * 4,
        in_specs=[blk, pl.BlockSpec((3, br, D), lambda i: (0, i, 0)), blk, blk, blk]
        + [pl.BlockSpec(memory_space=pl.ANY)] * len(after), out_specs=(blk,) * 4,
        compiler_params=pltpu.CompilerParams(dimension_semantics=("parallel",), vmem_limit_bytes=V7X_VMEM_LIMIT),
    )(own, arrived, w, m, v, *after)


def _adamw(ws, gs, ms, vs, packed, scalar_row, name, after=()):
    n = len(ws)
    given = [g for g in gs if not isinstance(g, int)]
    taken = [j for j in range(n) if isinstance(gs[j], int)]

    def body(packed_ref, *refs):
        w_r, m_r, v_r = (refs[k * n:(k + 1) * n] for k in range(3))
        given_r, outs = list(refs[3 * n:3 * n + len(given)]), refs[3 * n + len(given) + len(after):]
        g_o, outs = dict(zip(taken, outs[:len(taken)])), outs[len(taken):]
        d_o, m_o, v_o = (outs[k * n:(k + 1) * n] for k in range(3))
        outs[3 * n][...] = packed_ref[scalar_row:scalar_row + 1, 0:1]
        for j in range(n):
            if j in g_o:
                (r, c), at = ws[j].shape, gs[j]
                if c == 128:
                    g = packed_ref[at:at + r, :]
                else:
                    assert r == 1
                    g = jnp.concatenate([packed_ref[at + k:at + k + 1, :] for k in range(c // 128)], axis=1)
                g_o[j][...] = g
            else:
                g = given_r.pop(0)[...]
            d_o[j][...], m_o[j][...], v_o[j][...] = _adam_update(w_r[j][...], g, m_r[j][...], v_r[j][...])

    vm = pl.BlockSpec(memory_space=pltpu.VMEM)
    shapes = tuple(jax.ShapeDtypeStruct(w.shape, f32) for w in ws)
    n_out = len(taken) + 3 * n + 1
    return pl.pallas_call(
        body, name=name,
        out_shape=tuple(shapes[j] for j in taken) + shapes * 3 + (jax.ShapeDtypeStruct((1, 1), f32),),
        in_specs=[vm] * (1 + 3 * n + len(given)) + [pl.BlockSpec(memory_space=pl.ANY)] * len(after),
        out_specs=tuple([vm] * n_out),
        compiler_params=pltpu.CompilerParams(vmem_limit_bytes=V7X_VMEM_LIMIT),
    )(packed, *ws, *ms, *vs, *given, *after)


SMALL_FFN = (("ln1_g", D), ("ln1_b", D), ("ln2_g", D), ("ln2_b", D), ("conv_b", D_FF), ("conv_w", 3 * D_FF), ("loss", 128))
SMALL_FFN_AT = 520
SMALL_ROWS = 704


def _small_rows():
    rows, at = {"w_pool": 0, "pool_scale": GROUPS * DH}, SMALL_FFN_AT
    for k, size in SMALL_FFN:
        rows[k] = at
        at += size // 128
    return rows


def kernel(x, w_in, w_pool, pool_scale, w_out, ln1_g, ln1_b, w_up, conv_w, conv_b, w_down, ln2_g, ln2_b, loss_target, m_w_in, m_w_pool, m_pool_scale, m_w_out, m_ln1_g, m_ln1_b, m_w_up, m_conv_w, m_conv_b, m_w_down, m_ln2_g, m_ln2_b, v_w_in, v_w_pool, v_pool_scale, v_w_out, v_ln1_g, v_ln1_b, v_w_up, v_conv_w, v_conv_b, v_w_down, v_ln2_g, v_ln2_b):
    me = 4 * lax.axis_index("x") + 2 * lax.axis_index("y") + lax.axis_index("c")
    x2, tgt = x[0], loss_target[0]

    cos, sin = _rope_tables()
    dmat, qd, kd, cdec = _decay_tables(RET_TILE)

    qkv, g, oret, states, cat, pooled, xhat1, rstd1, x1b, xb, g_in, g_out, g_up, g_down, g_cw = _mix_forward(
        x2, w_in[0].T, w_out[0], cos, sin, dmat, qd, kd, cdec, w_pool[0], pool_scale, ln1_g, ln1_b,
        gather_bf16=[w_up[0].T, w_down[0]], gather=[jnp.transpose(conv_w, (1, 0, 2))])
    w_in_t = g_in.reshape(IN_W, D)
    w_out_f = g_out.reshape(D, D)
    w_up_t = g_up.reshape(2 * D_FF, D)
    w_down_f = g_down.reshape(D_FF, D)
    conv_w_f = jnp.transpose(g_cw[:, :, 0, :], (1, 0, 2)).reshape(3, D_FF)
    dz1, dz2b, du, f, loss8, d_ln2_g, d_ln2_b, d_ln1_g, d_ln1_b, d_conv_b, d_conv_w = _ffn_forward_backward(
        xhat1, rstd1, ln1_g, ln1_b, w_up_t, conv_w_f, conv_b, w_down_f, ln2_g, ln2_b, tgt)
    small_ffn = [d_ln1_g, d_ln1_b, d_ln2_g, d_ln2_b, d_conv_b, d_conv_w, loss8]

    (dw_down,) = _weight_grad(f, dz2b, "grad_w_down", tm=D_FF // 2)
    dw_up_t, own_down, arr_down = _weight_grad(du, x1b, "grad_w_up", tm=D_FF // 2,
                                               reduce=[dw_down.reshape(N_DEV, ROWS_DOWN, D)])
    dproj, small, dw_out, own_up, arr_up = _mix_backward(
        dz1, w_out_f, qkv, g, oret, states, pooled, cat, cos, sin, dmat, qd, kd, cdec, w_pool[0], pool_scale,
        small_ffn, reduce=[dw_up_t.reshape(N_DEV, ROWS_UP, D)])
    grad_x = _grad_x(dz1, dproj, w_in_t)
    dw_in_t, own_out, own_small, arr_out, arr_small = _weight_grad(
        dproj, xb, "grad_w_in", tm=IN_W // 4,
        reduce=[dw_out.reshape(N_DEV, ROWS_OUT, D), small.reshape(N_DEV, SMALL_ROWS // N_DEV, 128)])
    own_in, oth_in, gs_small = _pair_reduce([dw_in_t.reshape(N_DEV, ROWS_IN, D)], "pair_reduce_in",
                                            gather_sum=(own_small, arr_small))
    in_sems, in_src, in_land, started = _exchange_start(oth_in, "exchange_in_start", CHIP_BARRIER)

    names = ["w_in", "w_pool", "pool_scale", "w_out", "ln1_g", "ln1_b", "w_up", "conv_w", "conv_b", "w_down",
             "ln2_g", "ln2_b"]
    w_d = dict(w_in=w_in, w_pool=w_pool, pool_scale=pool_scale, w_out=w_out, ln1_g=ln1_g, ln1_b=ln1_b, w_up=w_up,
               conv_w=conv_w, conv_b=conv_b, w_down=w_down, ln2_g=ln2_g, ln2_b=ln2_b)
    m_d = dict(w_in=m_w_in, w_pool=m_w_pool, pool_scale=m_pool_scale, w_out=m_w_out, ln1_g=m_ln1_g, ln1_b=m_ln1_b,
               w_up=m_w_up, conv_w=m_conv_w, conv_b=m_conv_b, w_down=m_w_down, ln2_g=m_ln2_g, ln2_b=m_ln2_b)
    v_d = dict(w_in=v_w_in, w_pool=v_w_pool, pool_scale=v_pool_scale, w_out=v_w_out, ln1_g=v_ln1_g, ln1_b=v_ln1_b,
               w_up=v_w_up, conv_w=v_conv_w, conv_b=v_conv_b, w_down=v_w_down, ln2_g=v_ln2_g, ln2_b=v_ln2_b)
    g_d, delta, new_m, new_v = {}, {}, {}, {}

    def big_adamw(k, own, arr, transposed, steps, after=()):
        lay = (lambda a: a[0].T) if transposed else (lambda a: a[0])
        back = (lambda a: a.T[None]) if transposed else (lambda a: a[None])
        res = _sum_adamw(own, arr, lay(w_d[k]), lay(m_d[k]), lay(v_d[k]), "adamw_" + k, steps, after)
        g_d[k], delta[k], new_m[k], new_v[k] = (back(r) for r in res)
        return res[3]

    done = [big_adamw("w_up", own_up, arr_up, True, 4, after=(started,)),
            big_adamw("w_down", own_down, arr_down, False, 2, after=(started,)),
            big_adamw("w_out", own_out, arr_out, False, 2, after=(started,))]

    gs_small, rows = gs_small.reshape(SMALL_ROWS, 128), _small_rows()
    g_conv_w = gs_small[rows["conv_w"]:rows["conv_w"] + 3 * D_FF // 128].reshape(3, D_FF)
    g_d["conv_w"] = lax.dynamic_slice(g_conv_w, (0, me * (D_FF // N_DEV)), (3, D_FF // N_DEV))[None]
    lay = lambda k, a: jnp.transpose(a, (1, 0, 2)) if k == "conv_w" else a.reshape(-1, a.shape[-1])
    back = lambda k, a: jnp.transpose(a, (1, 0, 2)) if k == "conv_w" else a.reshape(w_d[k].shape)
    group = [k for k in names if k not in ("w_in", "w_out", "w_up", "w_down")]
    packed = [k for k in group if k != "conv_w"]
    res = _adamw([lay(k, w_d[k]) for k in group], [lay(k, g_d[k]) if k == "conv_w" else rows[k] for k in group],
                 [lay(k, m_d[k]) for k in group], [lay(k, v_d[k]) for k in group], gs_small, rows["loss"],
                 "adamw_small", after=(started,))
    for j, k in enumerate(packed):
        g_d[k] = back(k, res[j])
    for j, k in enumerate(group):
        delta[k], new_m[k], new_v[k] = (back(k, res[len(packed) + part * len(group) + j]) for part in range(3))

    arr_in = _exchange_wait(in_sems, in_src, in_land, done + [res[0]], "exchange_in_wait")
    big_adamw("w_in", own_in, arr_in, True, 4)

    loss = res[-1].reshape(())
    return (loss, grad_x[None], *[g_d[k] for k in names], *[delta[k] for k in names], *[new_m[k] for k in names],
            *[new_v[k] for k in names])
```

```python
import math

import numpy as np
import jax
import jax.numpy as jnp
from jax import lax
from jax.experimental import pallas as pl
from jax.experimental.pallas import tpu as pltpu

f32 = jnp.float32
bf16 = jnp.bfloat16

N_DEV = 8
T = 4096
D = 1024
CHUNK = 64
MIX_TILE = 512
RET_TILE = 256
HEADS = 4
DH = 128
RW = HEADS * DH
PW = 512
GROUPS = 4
WINDOWS = (2, 4, 8, 16)
IN_W = 4 * RW + PW
D_FF = 2816
LN_EPS = 1e-5
RMS_EPS = 1e-6
ALPHA = 2.0 ** 0.25
K_SCALE = DH ** -0.5

ADAM_LR = 0.001
ADAM_B1 = 0.9
ADAM_B2 = 0.999
ADAM_EPS = 1e-08
ADAM_WD = 0.01
ADAM_STEP = 10

ROWS_IN, ROWS_OUT, ROWS_UP, ROWS_DOWN = IN_W // N_DEV, D // N_DEV, 2 * D_FF // N_DEV, D_FF // N_DEV

V7X_VMEM_LIMIT = 56 * 2 ** 20
HALO = 32

NT = (((1,), (1,)), ((), ()))
TN = (((0,), (0,)), ((), ()))
NN = (((1,), (0,)), ((), ()))


def _dot(a, b, dims=NN):
    return lax.dot_general(a, b, dims, preferred_element_type=f32)


def _const_spec(shape):
    zeros = (0,) * len(shape)
    return pl.BlockSpec(shape, lambda i: zeros, pipeline_mode=pl.Buffered(1))


def _sigmoid(x):
    return 0.5 * jnp.tanh(0.5 * x) + 0.5


def _decay_tables(tt):
    h = np.arange(HEADS, dtype=np.float64)
    log_gamma = np.log(1.0 - 2.0 ** (-5.0 - h)).astype(np.float32).astype(np.float64)[:, None, None]
    idx = np.arange(tt, dtype=np.float64)
    visible = (idx[None, :] // CHUNK) <= (idx[:, None] // CHUNK)
    mask = np.where(visible[None], np.exp(log_gamma * np.abs(idx[:, None] - idx[None, :])[None]), 0.0)
    qd = np.broadcast_to(np.exp(log_gamma * (idx[None, :, None] + 1.0)), (HEADS, tt, DH))
    kd = np.broadcast_to(np.exp(log_gamma * (tt - 1.0 - idx[None, :, None])), (HEADS, tt, DH))
    cd = np.exp(log_gamma[:, 0, 0] * tt)
    return (jnp.asarray(mask, f32), jnp.asarray(qd, f32), jnp.asarray(kd, f32), [float(c) for c in cd])


def _rope_tables():
    inv_freq = (10000.0 ** (-np.arange(0, DH, 2, dtype=np.float64) / DH)).astype(np.float32)
    ang = (np.arange(T, dtype=np.float32)[:, None] * inv_freq[None, :]).astype(np.float64)
    cos, sin = np.cos(ang), np.sin(ang)
    return (jnp.asarray(np.concatenate([cos, cos], axis=1), f32), jnp.asarray(np.concatenate([-sin, sin], axis=1), f32))


def _swap_halves(t):
    return pltpu.roll(t, DH // 2, axis=1)


def _mix_forward(x, w_in_shard, w_out_shard, cos, sin, dmat, qd, kd, cdec, w_pool, pool_scale, ln1_g, ln1_b,
                 gather_bf16, gather, tt=MIX_TILE):
    n_tiles = T // tt
    to_bf16 = [w_in_shard, w_out_shard] + list(gather_bf16)
    n_c, n_g = len(to_bf16), len(gather_bf16) + len(gather)

    def body(x_ref, cos_ref, sin_ref, dmat_ref, qd_ref, kd_ref, wpool_ref, pscale_ref, g1_ref, b1_ref, *rest):
        f32_in, plain_in, rest = rest[:n_c], rest[n_c:2 + n_g], rest[2 + n_g:]
        qkv_ref, g_ref, oret_ref, states_ref, cat_ref, pooled_ref, xhat_ref, rstd_ref, x1b_ref, xb_ref = rest[:10]
        fout, gout = rest[10:12], rest[12:12 + n_g]
        state_s, pext_s, tmp_s, wint_s, wout_s, load_sems, stage_sems, *rest = rest[12 + n_g:]
        stage_s, cast_s, sems = rest[:n_c], rest[n_c:2 * n_c], rest[2 * n_c:]
        fin, gin, fsems, gsems = cast_s[:2], tuple(cast_s[2:]) + tuple(plain_in), sems[:3], sems[3:]
        i = pl.program_id(0)

        @pl.when(i == 0)
        def _():
            stage = [pltpu.make_async_copy(src, dst, stage_sems.at[j]) for j, (src, dst) in enumerate(zip(f32_in, stage_s))]
            for cp in stage:
                cp.start()
            state_s[...] = jnp.zeros_like(state_s)
            pext_s[:, pl.ds(0, HALO), :] = jnp.zeros((GROUPS, HALO, DH), f32)

            def cast(js):
                for j in js:
                    stage[j].wait()
                    cast_s[j][...] = stage_s[j][...].astype(bf16)

            _barrier(_gather_peers())
            cast(range(2))
            _gather_start(fin, fout, *fsems)
            cast(range(2, n_c))
            _gather_forward(fin, fout, *fsems)
            _gather_start(gin, gout, *gsems)
            _gather_finish(fin, fout, *fsems)
            loads = [pltpu.make_async_copy(src.at[s], dst.at[pl.ds(s * src.shape[1], src.shape[1]), :],
                                           load_sems.at[j, s])
                     for j, (src, dst) in enumerate(((fout[0], wint_s), (fout[1], wout_s))) for s in range(N_DEV)]
            for ld in loads:
                ld.start()
            for ld in loads:
                ld.wait()

        @pl.when(i == n_tiles - 3)
        def _():
            _gather_forward(gin, gout, *gsems)

        xb = x_ref[...].astype(bf16)
        xb_ref[...] = xb
        cos_t, sin_t = cos_ref[...], sin_ref[...]
        for part in range(2):
            pr = _dot(xb, wint_s[pl.ds(part * RW, RW), :], NT)
            for h in range(HEADS):
                t = pr[:, h * DH:(h + 1) * DH]
                r = t * cos_t + _swap_halves(t) * sin_t
                if part == 1:
                    r = r * K_SCALE
                qkv_ref[:, part * RW + h * DH: part * RW + (h + 1) * DH] = r.astype(bf16)
        qkv_ref[:, 2 * RW:3 * RW] = _dot(xb, wint_s[pl.ds(2 * RW, RW), :], NT).astype(bf16)
        g_ref[...] = _dot(xb, wint_s[pl.ds(3 * RW, RW), :], NT)
        p = _dot(xb, wint_s[pl.ds(4 * RW, PW), :], NT)
        for gi in range(GROUPS):
            pext_s[gi, pl.ds(HALO, tt), :] = p[:, gi * DH:(gi + 1) * DH]

        for sub in range(tt // RET_TILE):
            rows = pl.ds(sub * RET_TILE, RET_TILE)
            for h in range(HEADS):
                q = qkv_ref[rows, h * DH:(h + 1) * DH]
                k = qkv_ref[rows, RW + h * DH: RW + (h + 1) * DH]
                v = qkv_ref[rows, 2 * RW + h * DH: 2 * RW + (h + 1) * DH]
                s = _dot(q, k, NT) * dmat_ref[h]
                st = state_s[h]
                stb = st.astype(bf16)
                states_ref[sub, h] = stb
                oret_ref[rows, h * DH:(h + 1) * DH] = (_dot(s.astype(bf16), v)
                                                      + _dot((q.astype(f32) * qd_ref[h]).astype(bf16), stb))
                state_s[h] = st * cdec[h] + _dot((k.astype(f32) * kd_ref[h]).astype(bf16), v, TN)

        for h in range(HEADS):
            sl = slice(h * DH, (h + 1) * DH)
            o = oret_ref[:, sl]
            r = lax.rsqrt(jnp.mean(o * o, axis=-1, keepdims=True) + RMS_EPS)
            gg = g_ref[:, sl]
            cat_ref[:, sl] = (o * r * (gg * _sigmoid(gg))).astype(bf16)

        pos1 = (i * tt + lax.broadcasted_iota(jnp.int32, (tt, 1), 0) + 1).astype(f32)
        for gi, w in enumerate(WINDOWS):
            sl = slice(gi * DH, (gi + 1) * DH)
            stages = int(math.log2(w))
            src = pext_s
            for s in range(stages):
                lo = HALO - 8 * (stages - 1 - s)
                n = tt + HALO - lo
                shift = 2 ** s
                val = src[gi, pl.ds(lo, n), :] + src[gi, pl.ds(lo - shift, n), :]
                if s == stages - 1:
                    wsum = val
                else:
                    tmp_s[gi, pl.ds(lo, n), :] = val
                    src = tmp_s
            p_g = pext_s[gi, pl.ds(HALO, tt), :]
            pooled = (wsum / jnp.minimum(pos1, float(w)) - p_g).astype(bf16)
            pooled_ref[:, sl] = pooled
            y = _dot(pooled, wpool_ref[gi].astype(bf16)) * pscale_ref[:, sl]
            cat_ref[:, RW + gi * DH: RW + (gi + 1) * DH] = y.astype(bf16)
        pext_s[:, pl.ds(0, HALO), :] = pext_s[:, pl.ds(tt, HALO), :]

        z = ALPHA * x_ref[...] + _dot(cat_ref[...], wout_s[...])
        mu = jnp.mean(z, axis=-1, keepdims=True)
        zc = z - mu
        rstd = lax.rsqrt(jnp.mean(zc * zc, axis=-1, keepdims=True) + LN_EPS)
        xhat = zc * rstd
        xhat_ref[...] = xhat
        rstd_ref[...] = rstd
        x1b_ref[...] = (xhat * g1_ref[...] + b1_ref[...]).astype(bf16)

        @pl.when(i == n_tiles - 1)
        def _():
            _gather_finish(gin, gout, *gsems)

    tile = lambda w: pl.BlockSpec((tt, w), lambda i: (i, 0))
    hbm = pl.BlockSpec(memory_space=pltpu.HBM)
    out_shape = (
        jax.ShapeDtypeStruct((T, 3 * RW), bf16),
        jax.ShapeDtypeStruct((T, RW), f32),
        jax.ShapeDtypeStruct((T, RW), f32),
        jax.ShapeDtypeStruct((T // RET_TILE, HEADS, DH, DH), bf16),
        jax.ShapeDtypeStruct((T, D), bf16),
        jax.ShapeDtypeStruct((T, PW), bf16),
        jax.ShapeDtypeStruct((T, D), f32),
        jax.ShapeDtypeStruct((T, 1), f32),
        jax.ShapeDtypeStruct((T, D), bf16),
        jax.ShapeDtypeStruct((T, D), bf16),
    ) + tuple(jax.ShapeDtypeStruct((N_DEV,) + b.shape, bf16) for b in to_bf16
              ) + tuple(jax.ShapeDtypeStruct((N_DEV,) + b.shape, b.dtype) for b in gather)
    return pl.pallas_call(
        body, name="mix_forward", grid=(n_tiles,), out_shape=out_shape,
        in_specs=[tile(D), tile(DH), tile(DH),
                  _const_spec((HEADS, RET_TILE, RET_TILE)), _const_spec((HEADS, RET_TILE, DH)),
                  _const_spec((HEADS, RET_TILE, DH)),
                  _const_spec((GROUPS, DH, DH)), _const_spec((1, PW)),
                  _const_spec((1, D)), _const_spec((1, D))] + [hbm] * (2 + n_g),
        out_specs=(tile(3 * RW), tile(RW), tile(RW),
                   pl.BlockSpec((tt // RET_TILE, HEADS, DH, DH), lambda i: (i, 0, 0, 0)),
                   tile(D), tile(PW), tile(D), tile(1), tile(D), tile(D)) + (hbm,) * (2 + n_g),
        scratch_shapes=[pltpu.VMEM((HEADS, DH, DH), f32), pltpu.VMEM((GROUPS, tt + HALO, DH), f32),
                        pltpu.VMEM((GROUPS, tt + HALO, DH), f32), pltpu.VMEM((IN_W, D), bf16), pltpu.VMEM((D, D), bf16),
                        pltpu.SemaphoreType.DMA((2, N_DEV)), pltpu.SemaphoreType.DMA((n_c,))]
        + [pltpu.VMEM(b.shape, f32) for b in to_bf16] + [pltpu.VMEM(b.shape, bf16) for b in to_bf16]
        + _gather_sems(2) + _gather_sems(n_g),
        compiler_params=pltpu.CompilerParams(dimension_semantics=("arbitrary",), vmem_limit_bytes=V7X_VMEM_LIMIT,
                                             collective_id=GATHER_BARRIER),
    )(x, cos, sin, dmat, qd, kd, w_pool, pool_scale, ln1_g, ln1_b, *to_bf16, *gather)


def _ffn_forward_backward(xhat1, rstd1, ln1_g, ln1_b, w_up_t, conv_w, conv_b, w_down, ln2_g, ln2_b, target,
                          tt=256):
    n_tiles = T // tt
    FH = 16
    hb = tt // FH

    def body(xhat_ref, halo_ref, rstd_ref, g1_ref, b1_ref, wupt_ref, cw_ref, cb_ref, wdown_ref, g2_ref, b2_ref, tgt_ref,
             dz1_ref, dz2b_ref, du_ref, f_ref, loss_ref, dg2_ref, db2_ref, dg1_ref, db1_ref, dcb_ref, dcw_ref,
             gext_s, val_s, dhext_s):
        i = pl.program_id(0)
        tile_idx = n_tiles - 1 - i

        def rd(ref, off):
            return jnp.concatenate([ref[k, pl.ds(off, tt), :] for k in range(D_FF // 128)], axis=1)

        def wr(ref, val):
            for k in range(D_FF // 128):
                ref[k, pl.ds(0, val.shape[0]), :] = val[:, k * 128:(k + 1) * 128]

        @pl.when(i == 0)
        def _():
            for r in (loss_ref, dg2_ref, db2_ref, dg1_ref, db1_ref, dcb_ref, dcw_ref):
                r[...] = jnp.zeros_like(r)
            dhext_s[:, pl.ds(tt, 8), :] = jnp.zeros((D_FF // 128, 8, 128), f32)

        g1, b1 = g1_ref[...], b1_ref[...]
        xhat = xhat_ref[...]
        x1 = xhat * g1 + b1
        x1b = x1.astype(bf16)
        x1h = ((halo_ref[...] * g1 + b1) * jnp.where(tile_idx == 0, 0.0, 1.0)).astype(bf16)
        x1ext = jnp.concatenate([x1h, x1b], axis=0)

        val = _dot(x1b, wupt_ref[pl.ds(0, D_FF), :], NT)
        gate_ext = _dot(x1ext, wupt_ref[pl.ds(D_FF, D_FF), :], NT)
        wr(gext_s, gate_ext)
        hh = (cb_ref[...] + cw_ref[0:1, :] * rd(gext_s, FH - 2) + cw_ref[1:2, :] * rd(gext_s, FH - 1)
              + cw_ref[2:3, :] * gate_ext[FH:])
        sg = _sigmoid(hh)
        act = hh * sg
        wr(dhext_s, act)
        val_s[...] = val * (sg + act * (1.0 - sg))
        fb = (act * val).astype(bf16)
        f_ref[...] = fb

        z = ALPHA * x1 + _dot(fb, wdown_ref[...])
        mu = jnp.mean(z, axis=-1, keepdims=True)
        zc = z - mu
        rstd2 = lax.rsqrt(jnp.mean(zc * zc, axis=-1, keepdims=True) + LN_EPS)
        xh2 = zc * rstd2
        diff = xh2 * g2_ref[...] + b2_ref[...] - tgt_ref[...]
        loss_ref[...] += 0.5 * jnp.sum(diff * diff) / D
        dy = diff * (1.0 / D)
        dg2_ref[...] += jnp.sum(dy * xh2, axis=0, keepdims=True)
        db2_ref[...] += jnp.sum(dy, axis=0, keepdims=True)
        dyg = dy * g2_ref[...]
        dz2 = rstd2 * (dyg - jnp.mean(dyg, axis=-1, keepdims=True) - xh2 * jnp.mean(dyg * xh2, axis=-1, keepdims=True))
        dz2b = dz2.astype(bf16)
        dz2b_ref[...] = dz2b

        df = _dot(dz2b, wdown_ref[...], NT)
        dval = df * rd(dhext_s, 0)
        dh = df * val_s[...]
        wr(dhext_s, dh)
        dh1, dh2, g0 = rd(dhext_s, 1), rd(dhext_s, 2), rd(gext_s, FH)
        dcb_ref[...] += jnp.sum(dh, axis=0, keepdims=True)
        dcw_ref[0:1, :] += jnp.sum(dh2 * g0, axis=0, keepdims=True)
        dcw_ref[1:2, :] += jnp.sum(dh1 * g0, axis=0, keepdims=True)
        dcw_ref[2:3, :] += jnp.sum(dh * g0, axis=0, keepdims=True)
        dgate = cw_ref[2:3, :] * dh + cw_ref[1:2, :] * dh1 + cw_ref[0:1, :] * dh2
        dvalb, dgateb = dval.astype(bf16), dgate.astype(bf16)
        du_ref[:, :D_FF] = dvalb
        du_ref[:, D_FF:] = dgateb
        dx1 = ALPHA * dz2 + _dot(dvalb, wupt_ref[pl.ds(0, D_FF), :]) + _dot(dgateb, wupt_ref[pl.ds(D_FF, D_FF), :])
        dhext_s[:, pl.ds(tt, 8), :] = dhext_s[:, pl.ds(0, 8), :]

        dg1_ref[...] += jnp.sum(dx1 * xhat, axis=0, keepdims=True)
        db1_ref[...] += jnp.sum(dx1, axis=0, keepdims=True)
        dxg = dx1 * g1
        dz1_ref[...] = rstd_ref[...] * (dxg - jnp.mean(dxg, axis=-1, keepdims=True)
                                        - xhat * jnp.mean(dxg * xhat, axis=-1, keepdims=True))

    rtile = lambda w: pl.BlockSpec((tt, w), lambda i: (n_tiles - 1 - i, 0))
    acc = lambda shape: pl.BlockSpec(shape, lambda i: (0, 0))
    out_shape = (
        jax.ShapeDtypeStruct((T, D), f32),
        jax.ShapeDtypeStruct((T, D), bf16),
        jax.ShapeDtypeStruct((T, 2 * D_FF), bf16),
        jax.ShapeDtypeStruct((T, D_FF), bf16),
        jax.ShapeDtypeStruct((8, 128), f32),
        jax.ShapeDtypeStruct((1, D), f32), jax.ShapeDtypeStruct((1, D), f32),
        jax.ShapeDtypeStruct((1, D), f32), jax.ShapeDtypeStruct((1, D), f32),
        jax.ShapeDtypeStruct((1, D_FF), f32), jax.ShapeDtypeStruct((3, D_FF), f32),
    )
    return pl.pallas_call(
        body, name="ffn_forward_backward", grid=(n_tiles,), out_shape=out_shape,
        in_specs=[rtile(D),
                  pl.BlockSpec((FH, D), lambda i: (jnp.maximum((n_tiles - 1 - i) * hb - 1, 0), 0)),
                  rtile(1), _const_spec((1, D)), _const_spec((1, D)), _const_spec((2 * D_FF, D)),
                  _const_spec((3, D_FF)), _const_spec((1, D_FF)), _const_spec((D_FF, D)),
                  _const_spec((1, D)), _const_spec((1, D)), rtile(D)],
        out_specs=(rtile(D), rtile(D), rtile(2 * D_FF), rtile(D_FF), acc((8, 128)),
                   acc((1, D)), acc((1, D)), acc((1, D)), acc((1, D)), acc((1, D_FF)), acc((3, D_FF))),
        scratch_shapes=[pltpu.VMEM((D_FF // 128, tt + FH, 128), f32), pltpu.VMEM((tt, D_FF), f32),
                        pltpu.VMEM((D_FF // 128, tt + 8, 128), f32)],
        compiler_params=pltpu.CompilerParams(dimension_semantics=("arbitrary",), vmem_limit_bytes=V7X_VMEM_LIMIT),
    )(xhat1, xhat1, rstd1, ln1_g, ln1_b, w_up_t, conv_w, conv_b, w_down, ln2_g, ln2_b, target)


def _mix_backward(dz1, w_out, qkv, g, oret, states, pooled, cat, cos, sin, dmat, qd, kd, cdec, w_pool, pool_scale, w_in_t,
                  small_ffn, after, tt=MIX_TILE):
    n_tiles = T // tt

    def body(dz1_ref, wout_ref, qkv_ref, g_ref, oret_ref, states_ref, pooled_ref, cat_ref, cos_ref, sin_ref, dmat_ref,
             qd_ref, kd_ref, wpool_ref, pscale_ref, wint_ref, *rest):
        ffn_refs, rest = rest[:len(SMALL_FFN)], rest[len(SMALL_FFN):]
        after_ref, dproj_ref, gx_ref, small_ref, dwout_ref, dstate_s, dout_s, eext_s, tmp_s, dwout_s, dpscale_s = rest
        i = pl.program_id(0)
        tile_idx = n_tiles - 1 - i

        @pl.when(i == 0)
        def _():
            dstate_s[...] = jnp.zeros_like(dstate_s)
            small_ref[...] = jnp.zeros_like(small_ref)
            dpscale_s[...] = jnp.zeros_like(dpscale_s)
            dwout_s[...] = jnp.zeros_like(dwout_s)
            eext_s[:, pl.ds(tt, HALO), :] = jnp.zeros((GROUPS, HALO, DH), f32)

        dz1 = dz1_ref[...]
        dz1b = dz1.astype(bf16)
        dcat = _dot(dz1b, wout_ref[...], NT)
        dwout_s[...] += _dot(cat_ref[...], dz1b, TN)

        pos1 = (tile_idx * tt + lax.broadcasted_iota(jnp.int32, (tt, 1), 0) + 1).astype(f32)
        for gi, w in enumerate(WINDOWS):
            sl = slice(gi * DH, (gi + 1) * DH)
            dpo = dcat[:, RW + gi * DH: RW + (gi + 1) * DH]
            pooled_g = pooled_ref[:, sl]
            wpool_g = wpool_ref[gi].astype(bf16)
            ylin = _dot(pooled_g, wpool_g)
            dpscale_s[:, sl] += jnp.sum(dpo * ylin, axis=0, keepdims=True)
            dpw = (dpo * pscale_ref[:, sl]).astype(bf16)
            small_ref[pl.ds(gi * DH, DH), :] += _dot(pooled_g, dpw, TN)
            dpooled = _dot(dpw, wpool_g, NT)
            eext_s[gi, pl.ds(0, tt), :] = dpooled / jnp.minimum(pos1, float(w))
            stages = int(math.log2(w))
            src = eext_s
            for s in range(stages):
                n = tt + 8 * (stages - 1 - s)
                shift = 2 ** s
                val = src[gi, pl.ds(0, n), :] + src[gi, pl.ds(shift, n), :]
                if s == stages - 1:
                    wsum = val
                else:
                    tmp_s[gi, pl.ds(0, n), :] = val
                    src = tmp_s
            dproj_ref[:, 4 * RW + gi * DH: 4 * RW + (gi + 1) * DH] = (wsum - dpooled).astype(bf16)
        eext_s[:, pl.ds(tt, HALO), :] = eext_s[:, pl.ds(0, HALO), :]

        for h in range(HEADS):
            sl = slice(h * DH, (h + 1) * DH)
            dr = dcat[:, sl]
            o = oret_ref[:, sl]
            r = lax.rsqrt(jnp.mean(o * o, axis=-1, keepdims=True) + RMS_EPS)
            rn = o * r
            gg = g_ref[:, sl]
            sg = _sigmoid(gg)
            dproj_ref[:, 3 * RW + h * DH: 3 * RW + (h + 1) * DH] = (dr * rn * (sg * (1.0 + gg * (1.0 - sg)))).astype(bf16)
            drn = dr * (gg * sg)
            dout_s[:, sl] = (r * (drn - rn * jnp.mean(drn * rn, axis=-1, keepdims=True))).astype(bf16)

        for sub in reversed(range(tt // RET_TILE)):
            rows = pl.ds(sub * RET_TILE, RET_TILE)
            cos_t, sin_t = cos_ref[rows, :], sin_ref[rows, :]
            for h in range(HEADS):
                q = qkv_ref[rows, h * DH:(h + 1) * DH]
                k = qkv_ref[rows, RW + h * DH: RW + (h + 1) * DH]
                v = qkv_ref[rows, 2 * RW + h * DH: 2 * RW + (h + 1) * DH]
                do = dout_s[rows, h * DH:(h + 1) * DH]
                stb = states_ref[sub, h]
                dst = dstate_s[h]
                dstb = dst.astype(bf16)
                sb = (_dot(q, k, NT) * dmat_ref[h]).astype(bf16)
                dsb = (_dot(do, v, NT) * dmat_ref[h]).astype(bf16)
                dq = _dot(dsb, k) + _dot(do, stb, NT) * qd_ref[h]
                dk = _dot(dsb, q, TN) + _dot(v, dstb, NT) * kd_ref[h]
                dv = _dot(sb, do, TN) + _dot((k.astype(f32) * kd_ref[h]).astype(bf16), dstb)
                dstate_s[h] = dst * cdec[h] + _dot((q.astype(f32) * qd_ref[h]).astype(bf16), do, TN)
                dproj_ref[rows, h * DH:(h + 1) * DH] = (dq * cos_t - _swap_halves(dq) * sin_t).astype(bf16)
                dproj_ref[rows, RW + h * DH: RW + (h + 1) * DH] = (
                    (dk * cos_t - _swap_halves(dk) * sin_t) * K_SCALE).astype(bf16)
                dproj_ref[rows, 2 * RW + h * DH: 2 * RW + (h + 1) * DH] = dv.astype(bf16)

        gx_ref[...] = ALPHA * dz1 + _dot(dproj_ref[...], wint_ref[...])

        @pl.when(i == n_tiles - 1)
        def _():
            dwout_ref[...] = dwout_s[...].astype(bf16)
            at = GROUPS * DH
            for ref, size in [(dpscale_s, PW)] + [(ref, size) for ref, (_, size) in zip(ffn_refs, SMALL_FFN)]:
                for j in range(size // 128):
                    r, k = divmod(j, ref.shape[1] // 128)
                    small_ref[at + j: at + j + 1, :] = ref[r:r + 1, k * 128:(k + 1) * 128]
                at = SMALL_FFN_AT if ref is dpscale_s else at + size // 128

    rtile = lambda w: pl.BlockSpec((tt, w), lambda i: (n_tiles - 1 - i, 0))
    out_shape = (
        jax.ShapeDtypeStruct((T, IN_W), bf16),
        jax.ShapeDtypeStruct((T, D), f32),
        jax.ShapeDtypeStruct((SMALL_ROWS, 128), f32),
        jax.ShapeDtypeStruct((D, D), bf16),
    )
    return pl.pallas_call(
        body, name="mix_backward", grid=(n_tiles,), out_shape=out_shape,
        in_specs=[rtile(D), _const_spec((D, D)), rtile(3 * RW), rtile(RW), rtile(RW),
                  pl.BlockSpec((tt // RET_TILE, HEADS, DH, DH), lambda i: (n_tiles - 1 - i, 0, 0, 0)),
                  rtile(PW), rtile(D), rtile(DH), rtile(DH),
                  _const_spec((HEADS, RET_TILE, RET_TILE)), _const_spec((HEADS, RET_TILE, DH)),
                  _const_spec((HEADS, RET_TILE, DH)),
                  _const_spec((GROUPS, DH, DH)), _const_spec((1, PW)), _const_spec((IN_W, D)),
                  *[_const_spec(a.shape) for a in small_ffn], pl.BlockSpec(memory_space=pl.ANY)],
        out_specs=(rtile(IN_W), rtile(D), pl.BlockSpec((SMALL_ROWS, 128), lambda i: (0, 0)),
                   pl.BlockSpec((D, D), lambda i: (0, 0), pipeline_mode=pl.Buffered(1))),
        scratch_shapes=[pltpu.VMEM((HEADS, DH, DH), f32), pltpu.VMEM((tt, RW), bf16),
                        pltpu.VMEM((GROUPS, tt + HALO, DH), f32), pltpu.VMEM((GROUPS, tt + HALO, DH), f32),
                        pltpu.VMEM((D, D), f32), pltpu.VMEM((1, PW), f32)],
        compiler_params=pltpu.CompilerParams(dimension_semantics=("arbitrary",), vmem_limit_bytes=V7X_VMEM_LIMIT),
    )(dz1, w_out, qkv, g, oret, states, pooled, cat, cos, sin, dmat, qd, kd, w_pool, pool_scale, w_in_t, *small_ffn,
      after)


def _weight_grad(a, b, name, tm, reduce=()):
    m = a.shape[1]
    n_m, n_r = m // tm, len(reduce)
    assert not n_r or n_m >= 2

    def body(a_ref, b_ref, *rest):
        ins, o_ref, own, arrived = rest[:n_r], rest[n_r], rest[n_r + 1:2 * n_r + 1], rest[2 * n_r + 1:3 * n_r + 1]
        landing, mine, sems = rest[3 * n_r + 1:4 * n_r + 1], rest[4 * n_r + 1:5 * n_r + 1], rest[5 * n_r + 1:]
        i = pl.program_id(0)

        if n_r:
            pair_send, pair_recv, local_sems, chip_send, chip_recv = sems
            me = _me()
            x, y, c = me
            sibling = (x, y, 1 - c)

            def pair_copy(k, j):
                return _remote(ins[j].at[_slot(*_chip(me, k), 1 - c)], landing[j].at[k], pair_send.at[k, j],
                               pair_recv.at[k, j], sibling)

            def load(k, j):
                return pltpu.make_async_copy(ins[j].at[_slot(*_chip(me, k), c)], mine[j].at[k], local_sems.at[k, j])

            def store(j):
                return pltpu.make_async_copy(mine[j].at[0], own[j], local_sems.at[0, j])

            def chip_copy(k, j):
                return _remote(mine[j].at[k], arrived[j].at[k - 1], chip_send.at[k - 1, j], chip_recv.at[k - 1, j],
                               (*_chip(me, k), c))

            @pl.when(i == 0)
            def _():
                _barrier([sibling] + _chip_peers())
                for k in range(4):
                    for j in range(n_r):
                        pair_copy(k, j).start()
                        load(k, j).start()

            @pl.when(i == 1)
            def _():
                for k in range(4):
                    for j in range(n_r):
                        load(k, j).wait()
                        pair_copy(k, j).wait_recv()
                        mine[j][k] = (mine[j][k].astype(f32) + landing[j][k].astype(f32)).astype(mine[j].dtype)
                        (store(j) if k == 0 else chip_copy(k, j)).start()

        o_ref[...] = _dot(a_ref[...], b_ref[...].astype(bf16), TN).astype(bf16)

        if n_r:
            @pl.when(i == n_m - 1)
            def _():
                for j in range(n_r):
                    store(j).wait()
                    for k in range(1, 4):
                        chip_copy(k, j).wait_recv()
                for j in range(n_r):
                    for k in range(1, 4):
                        chip_copy(k, j).wait_send()
                    for k in range(4):
                        pair_copy(k, j).wait_send()

    hbm = pl.BlockSpec(memory_space=pltpu.HBM)
    return pl.pallas_call(
        body, name=name, grid=(n_m,),
        out_shape=(jax.ShapeDtypeStruct((m, D), bf16),)
        + tuple(jax.ShapeDtypeStruct(p.shape[1:], p.dtype) for p in reduce)
        + tuple(jax.ShapeDtypeStruct((3,) + p.shape[1:], p.dtype) for p in reduce),
        in_specs=[pl.BlockSpec((T, tm), lambda i: (0, i)),
                  pl.BlockSpec((T, D), lambda i: (0, 0), pipeline_mode=pl.Buffered(1))] + [hbm] * n_r,
        out_specs=(pl.BlockSpec((tm, D), lambda i: (i, 0)),) + (hbm,) * (2 * n_r),
        scratch_shapes=[pltpu.VMEM((4,) + p.shape[1:], p.dtype) for p in reduce] * 2
        + ([pltpu.SemaphoreType.DMA((4, n_r))] * 3 + [pltpu.SemaphoreType.DMA((3, n_r))] * 2 if n_r else []),
        compiler_params=pltpu.CompilerParams(dimension_semantics=("arbitrary",), vmem_limit_bytes=V7X_VMEM_LIMIT,
                                             collective_id=REDUCE_BARRIER if n_r else None),
    )(a, b, *reduce)


CHIP_FLIPS = ((1, 0), (0, 1), (1, 1))
PAIR_BARRIER, CHIP_BARRIER, GATHER_BARRIER, CHIP_BARRIER_SPLIT, REDUCE_BARRIER, ALL_BARRIER = 0, 1, 2, 3, 4, 5
ALL_FLIPS = tuple((fx, fy, fc) for fx in (0, 1) for fy in (0, 1) for fc in (0, 1))[1:]


def _flip(me, f):
    return tuple(1 - v if b else v for v, b in zip(me, f))


def _barrier(peers):
    sem = pltpu.get_barrier_semaphore()
    for peer in peers:
        pl.semaphore_signal(sem, inc=1, device_id=peer, device_id_type=pl.DeviceIdType.MESH)
    pl.semaphore_wait(sem, len(peers))


def _me():
    return lax.axis_index("x"), lax.axis_index("y"), lax.axis_index("c")


def _chip(me, k):
    x, y, _ = me
    if k == 0:
        return x, y
    fx, fy = CHIP_FLIPS[k - 1]
    return (1 - x if fx else x), (1 - y if fy else y)


def _slot(x, y, c):
    return 4 * x + 2 * y + c


def _remote(src, dst, send_sem, recv_sem, to):
    return pltpu.make_async_remote_copy(src_ref=src, dst_ref=dst, send_sem=send_sem, recv_sem=recv_sem,
                                        device_id=to, device_id_type=pl.DeviceIdType.MESH)


def _gather_sems(n):
    return [pltpu.SemaphoreType.DMA((7, n)), pltpu.SemaphoreType.DMA((7, n)), pltpu.SemaphoreType.DMA((n,))] if n else []


def _gather_copy(k, j, gin, gout, send_sems, recv_sems, sending):
    x, y, c = _me()
    sibling, x_chip, y_chip, d_chip = (x, y, 1 - c), (1 - x, y), (x, 1 - y), (1 - x, 1 - y)
    south = c == 0
    passed_on = (jnp.where(south, 1 - x, x), jnp.where(south, y, 1 - y), c)
    src, to = gin[j], sibling
    if sending:
        block = {0: (x, y, c), 1: (x, y, c), 2: (x, y, c), 3: passed_on, 4: (*x_chip, c), 5: (*y_chip, c), 6: (*d_chip, c)}[k]
        to = {1: (*x_chip, c), 2: (*y_chip, c), 3: (jnp.where(south, x, 1 - x), jnp.where(south, 1 - y, y), c)}.get(k, sibling)
        if k >= 3:
            src = gout[j].at[_slot(*block)]
    else:
        block = {0: sibling, 1: (*x_chip, c), 2: (*y_chip, c), 3: (*d_chip, c), 4: (*x_chip, 1 - c), 5: (*y_chip, 1 - c),
                 6: (*d_chip, 1 - c)}[k]
    return _remote(src, gout[j].at[_slot(*block)], send_sems.at[k, j], recv_sems.at[k, j], to)


def _gather_do(ks, action, gin, gout, send_sems, recv_sems):
    for k in ks:
        for j in range(len(gin)):
            cp = _gather_copy(k, j, gin, gout, send_sems, recv_sems, action != "wait_recv")
            getattr(cp, action)()


def _gather_peers():
    x, y, c = _me()
    return [(x, y, 1 - c), (1 - x, y, c), (x, 1 - y, c)]


def _gather_start(gin, gout, send_sems, recv_sems, local_sems):
    for j in range(len(gin)):
        pltpu.make_async_copy(gin[j], gout[j].at[_slot(*_me())], local_sems.at[j]).start()
    _gather_do((0, 1, 2), "start", gin, gout, send_sems, recv_sems)


def _gather_forward(gin, gout, send_sems, recv_sems, local_sems):
    _gather_do((1, 2), "wait_recv", gin, gout, send_sems, recv_sems)
    _gather_do((3, 4, 5), "start", gin, gout, send_sems, recv_sems)


def _gather_finish(gin, gout, send_sems, recv_sems, local_sems):
    _gather_do((3,), "wait_recv", gin, gout, send_sems, recv_sems)
    _gather_do((6,), "start", gin, gout, send_sems, recv_sems)
    _gather_do((0, 4, 5, 6), "wait_recv", gin, gout, send_sems, recv_sems)
    _gather_do(range(7), "wait_send", gin, gout, send_sems, recv_sems)
    for j in range(len(gin)):
        pltpu.make_async_copy(gin[j], gout[j].at[_slot(*_me())], local_sems.at[j]).wait()


def _pair_reduce(parts, name, gather_sum=None):
    n = len(parts)
    n_h = 0 if gather_sum is None else 1

    def body(*refs):
        ins, g_terms, refs = refs[:n], refs[n:n + 2 * n_h], refs[n + 2 * n_h:]
        own, others, g_out, refs = refs[:n], refs[n:2 * n], refs[2 * n:2 * n + n_h], refs[2 * n + n_h:]
        landing, mine, (send_sems, recv_sems, local_sems), g_scratch = refs[:n], refs[n:2 * n], refs[2 * n:2 * n + 3], refs[2 * n + 3:]
        me = _me()
        x, y, c = me
        sibling = (x, y, 1 - c)
        _barrier([_flip(me, f) for f in ALL_FLIPS] if n_h else [sibling])
        if n_h:
            piece_s, g_send, g_recv, g_local = g_scratch
            acc = g_terms[0][...].astype(f32)
            for k in range(3):
                acc = acc + g_terms[1][k].astype(f32)
            piece_s[...] = acc

            def g_copy(q):
                return _remote(piece_s, g_out[0].at[_slot(*me)], g_send.at[q, 0], g_recv.at[q, 0], _flip(me, ALL_FLIPS[q]))

            g_mine = pltpu.make_async_copy(piece_s, g_out[0].at[_slot(*me)], g_local.at[0])
            g_mine.start()
            for q in range(len(ALL_FLIPS)):
                g_copy(q).start()
        sends, loads = [], []
        for k in range(4):
            for j in range(n):
                cp = _remote(ins[j].at[_slot(*_chip(me, k), 1 - c)], landing[j].at[k], send_sems.at[k, j],
                             recv_sems.at[k, j], sibling)
                cp.start()
                sends.append(cp)
                ld = pltpu.make_async_copy(ins[j].at[_slot(*_chip(me, k), c)], mine[j].at[k], local_sems.at[k, j])
                ld.start()
                loads.append(ld)
        stores = []
        for k in range(4):
            for j in range(n):
                loads[k * n + j].wait()
                _remote(ins[j].at[0], landing[j].at[k], send_sems.at[k, j], recv_sems.at[k, j], sibling).wait_recv()
                mine[j][k] = (mine[j][k].astype(f32) + landing[j][k].astype(f32)).astype(mine[j].dtype)
                st = pltpu.make_async_copy(mine[j].at[k], own[j] if k == 0 else others[j].at[k - 1], local_sems.at[k, j])
                st.start()
                stores.append(st)
        for cp in sends:
            cp.wait_send()
        for st in stores:
            st.wait()
        if n_h:
            for q in range(len(ALL_FLIPS)):
                g_copy(q).wait_recv()
            for q in range(len(ALL_FLIPS)):
                g_copy(q).wait_send()
            g_mine.wait()

    vm, hbm = pl.BlockSpec(memory_space=pltpu.VMEM), pl.BlockSpec(memory_space=pltpu.HBM)
    g_shape = gather_sum[0].shape if n_h else ()
    return pl.pallas_call(
        body, name=name,
        out_shape=tuple(jax.ShapeDtypeStruct(p.shape[1:], p.dtype) for p in parts)
        + tuple(jax.ShapeDtypeStruct((3,) + p.shape[1:], p.dtype) for p in parts)
        + tuple([jax.ShapeDtypeStruct((N_DEV,) + g_shape, f32)] * n_h),
        in_specs=[hbm] * n + [vm] * (2 * n_h), out_specs=(hbm,) * (2 * n + n_h),
        scratch_shapes=[pltpu.VMEM((4,) + p.shape[1:], p.dtype) for p in parts] * 2
        + [pltpu.SemaphoreType.DMA((4, n)), pltpu.SemaphoreType.DMA((4, n)), pltpu.SemaphoreType.DMA((4, n))]
        + ([pltpu.VMEM(g_shape, f32)] + _gather_sems(1)) * n_h,
        compiler_params=pltpu.CompilerParams(vmem_limit_bytes=V7X_VMEM_LIMIT,
                                             collective_id=ALL_BARRIER if n_h else PAIR_BARRIER),
    )(*parts, *(gather_sum or ()))


def _chip_peers():
    me = _me()
    return [(*_chip(me, k), me[2]) for k in range(1, 4)]


def _split_copies(src_ref, dst_ref, sems):
    me = _me()
    return [_remote(src_ref.at[k - 1], dst_ref.at[k - 1], sems[k - 1], sems[2 + k], (*_chip(me, k), me[2]))
            for k in range(1, 4)]


def _exchange_start(others, name, barrier_id):
    def body(src_ref, land_ref, *rest):
        sems, token_ref = rest[:6], rest[8]
        _barrier(_chip_peers())
        for copy in _split_copies(src_ref, land_ref, sems):
            copy.start()
        token_ref[...] = jnp.zeros_like(token_ref)

    hbm, sem = pl.BlockSpec(memory_space=pltpu.HBM), pl.BlockSpec(memory_space=pltpu.SEMAPHORE)
    thru = pltpu.HBM(others.shape, others.dtype)
    res = pl.pallas_call(
        body, name=name,
        out_shape=(pltpu.SemaphoreType.DMA(()),) * 6 + (thru, thru, jax.ShapeDtypeStruct((8, 128), f32)),
        in_specs=(hbm, hbm), out_specs=(sem,) * 6 + (hbm, hbm, pl.BlockSpec(memory_space=pltpu.VMEM)),
        input_output_aliases={0: 6, 1: 7},
        compiler_params=pltpu.CompilerParams(has_side_effects=pltpu.SideEffectType.DATAFLOW_SIDE_EFFECTING,
                                             collective_id=barrier_id),
    )(pltpu.with_memory_space_constraint(others, pltpu.HBM),
      pltpu.with_memory_space_constraint(lax.empty(others.shape, others.dtype), pltpu.HBM))
    return res[:6], res[6], res[7], res[8]


def _exchange_wait(sems, src_thru, land_thru, after, name):
    n_after = len(after)

    def body(src_ref, land_ref, *rest):
        for copy in _split_copies(src_ref, land_ref, rest[:6]):
            copy.wait_send()
            copy.wait_recv()

    hbm, sem = pl.BlockSpec(memory_space=pltpu.HBM), pl.BlockSpec(memory_space=pltpu.SEMAPHORE)
    thru = pltpu.HBM(src_thru.shape, src_thru.dtype)
    return pl.pallas_call(
        body, name=name, out_shape=(thru, thru),
        in_specs=(hbm, hbm) + (sem,) * 6 + (pl.BlockSpec(memory_space=pl.ANY),) * n_after, out_specs=(hbm, hbm),
        input_output_aliases={0: 0, 1: 1},
        compiler_params=pltpu.CompilerParams(has_side_effects=pltpu.SideEffectType.DATAFLOW_SIDE_EFFECTING),
    )(src_thru, land_thru, *sems, *after)[1]


def _adam_update(w, g, m, v):
    m = ADAM_B1 * m + (1.0 - ADAM_B1) * g
    v = ADAM_B2 * v + (1.0 - ADAM_B2) * (g * g)
    m_hat = m / (1.0 - ADAM_B1 ** ADAM_STEP)
    v_hat = v / (1.0 - ADAM_B2 ** ADAM_STEP)
    return -ADAM_LR * (m_hat / (jnp.sqrt(v_hat) + ADAM_EPS) + ADAM_WD * w), m, v


def _sum_adamw(own, arrived, w, m, v, name, steps, after=()):
    rows = own.shape[0]
    br = rows // steps

    def body(own_ref, arr_ref, w_ref, m_ref, v_ref, *rest):
        g_out, d_out, m_out, v_out = rest[len(after):]
        g = own_ref[...].astype(f32)
        for k in range(3):
            g = g + arr_ref[k].astype(f32)
        g_out[...] = g
        d_out[...], m_out[...], v_out[...] = _adam_update(w_ref[...], g, m_ref[...], v_ref[...])

    blk = pl.BlockSpec((br, D), lambda i: (i, 0))
    return pl.pallas_call(
        body, name=name, grid=(steps,), out_shape=(jax.ShapeDtypeStruct((rows, D), f32),) * 4,
        in_specs=[blk, pl.BlockSpec((3, br, D), lambda i: (0, i, 0)), blk, blk, blk]
        + [pl.BlockSpec(memory_space=pl.ANY)] * len(after), out_specs=(blk,) * 4,
        compiler_params=pltpu.CompilerParams(dimension_semantics=("parallel",), vmem_limit_bytes=V7X_VMEM_LIMIT),
    )(own, arrived, w, m, v, *after)


def _adamw(ws, gs, ms, vs, packed, scalar_row, name, after=()):
    n = len(ws)
    given = [g for g in gs if not isinstance(g, int)]
    taken = [j for j in range(n) if isinstance(gs[j], int)]

    def body(packed_ref, *refs):
        w_r, m_r, v_r = (refs[k * n:(k + 1) * n] for k in range(3))
        given_r, outs = list(refs[3 * n:3 * n + len(given)]), refs[3 * n + len(given) + len(after):]
        g_o, outs = dict(zip(taken, outs[:len(taken)])), outs[len(taken):]
        d_o, m_o, v_o = (outs[k * n:(k + 1) * n] for k in range(3))
        outs[3 * n][...] = packed_ref[scalar_row:scalar_row + 1, 0:1]
        for j in range(n):
            if j in g_o:
                (r, c), at = ws[j].shape, gs[j]
                if c == 128:
                    g = packed_ref[at:at + r, :]
                else:
                    assert r == 1
                    g = jnp.concatenate([packed_ref[at + k:at + k + 1, :] for k in range(c // 128)], axis=1)
                g_o[j][...] = g
            else:
                g = given_r.pop(0)[...]
            d_o[j][...], m_o[j][...], v_o[j][...] = _adam_update(w_r[j][...], g, m_r[j][...], v_r[j][...])

    vm = pl.BlockSpec(memory_space=pltpu.VMEM)
    shapes = tuple(jax.ShapeDtypeStruct(w.shape, f32) for w in ws)
    n_out = len(taken) + 3 * n + 1
    return pl.pallas_call(
        body, name=name,
        out_shape=tuple(shapes[j] for j in taken) + shapes * 3 + (jax.ShapeDtypeStruct((1, 1), f32),),
        in_specs=[vm] * (1 + 3 * n + len(given)) + [pl.BlockSpec(memory_space=pl.ANY)] * len(after),
        out_specs=tuple([vm] * n_out),
        compiler_params=pltpu.CompilerParams(vmem_limit_bytes=V7X_VMEM_LIMIT),
    )(packed, *ws, *ms, *vs, *given, *after)


SMALL_FFN = (("ln1_g", D), ("ln1_b", D), ("ln2_g", D), ("ln2_b", D), ("conv_b", D_FF), ("conv_w", 3 * D_FF), ("loss", 128))
SMALL_FFN_AT = 520
SMALL_ROWS = 704


def _small_rows():
    rows, at = {"w_pool": 0, "pool_scale": GROUPS * DH}, SMALL_FFN_AT
    for k, size in SMALL_FFN:
        rows[k] = at
        at += size // 128
    return rows


def kernel(x, w_in, w_pool, pool_scale, w_out, ln1_g, ln1_b, w_up, conv_w, conv_b, w_down, ln2_g, ln2_b, loss_target, m_w_in, m_w_pool, m_pool_scale, m_w_out, m_ln1_g, m_ln1_b, m_w_up, m_conv_w, m_conv_b, m_w_down, m_ln2_g, m_ln2_b, v_w_in, v_w_pool, v_pool_scale, v_w_out, v_ln1_g, v_ln1_b, v_w_up, v_conv_w, v_conv_b, v_w_down, v_ln2_g, v_ln2_b):
    me = 4 * lax.axis_index("x") + 2 * lax.axis_index("y") + lax.axis_index("c")
    x2, tgt = x[0], loss_target[0]

    cos, sin = _rope_tables()
    dmat, qd, kd, cdec = _decay_tables(RET_TILE)

    qkv, g, oret, states, cat, pooled, xhat1, rstd1, x1b, xb, g_in, g_out, g_up, g_down, g_cw = _mix_forward(
        x2, w_in[0].T, w_out[0], cos, sin, dmat, qd, kd, cdec, w_pool[0], pool_scale, ln1_g, ln1_b,
        gather_bf16=[w_up[0].T, w_down[0]], gather=[jnp.transpose(conv_w, (1, 0, 2))])
    w_in_t = g_in.reshape(IN_W, D)
    w_out_f = g_out.reshape(D, D)
    w_up_t = g_up.reshape(2 * D_FF, D)
    w_down_f = g_down.reshape(D_FF, D)
    conv_w_f = jnp.transpose(g_cw[:, :, 0, :], (1, 0, 2)).reshape(3, D_FF)
    dz1, dz2b, du, f, loss8, d_ln2_g, d_ln2_b, d_ln1_g, d_ln1_b, d_conv_b, d_conv_w = _ffn_forward_backward(
        xhat1, rstd1, ln1_g, ln1_b, w_up_t, conv_w_f, conv_b, w_down_f, ln2_g, ln2_b, tgt)
    small_ffn = [d_ln1_g, d_ln1_b, d_ln2_g, d_ln2_b, d_conv_b, d_conv_w, loss8]

    (dw_down,) = _weight_grad(f, dz2b, "grad_w_down", tm=D_FF // 2)
    dw_up_t, own_down, arr_down = _weight_grad(du, x1b, "grad_w_up", tm=D_FF // 2,
                                               reduce=[dw_down.reshape(N_DEV, ROWS_DOWN, D)])
    own_up, oth_up = _pair_reduce([dw_up_t.reshape(N_DEV, ROWS_UP, D)], "pair_reduce_up")
    up_sems, up_src, up_land, up_started = _exchange_start(oth_up, "exchange_up_start", CHIP_BARRIER_SPLIT)
    dproj, grad_x, small, dw_out = _mix_backward(
        dz1, w_out_f, qkv, g, oret, states, pooled, cat, cos, sin, dmat, qd, kd, cdec, w_pool[0], pool_scale, w_in_t,
        small_ffn, after=up_started)
    dw_in_t, own_out, own_small, arr_out, arr_small = _weight_grad(
        dproj, xb, "grad_w_in", tm=IN_W // 4,
        reduce=[dw_out.reshape(N_DEV, ROWS_OUT, D), small.reshape(N_DEV, SMALL_ROWS // N_DEV, 128)])
    arr_up = _exchange_wait(up_sems, up_src, up_land, [dw_in_t], "exchange_up_wait")
    own_in, oth_in, gs_small = _pair_reduce([dw_in_t.reshape(N_DEV, ROWS_IN, D)], "pair_reduce_in",
                                            gather_sum=(own_small, arr_small))
    in_sems, in_src, in_land, started = _exchange_start(oth_in, "exchange_in_start", CHIP_BARRIER)

    names = ["w_in", "w_pool", "pool_scale", "w_out", "ln1_g", "ln1_b", "w_up", "conv_w", "conv_b", "w_down",
             "ln2_g", "ln2_b"]
    w_d = dict(w_in=w_in, w_pool=w_pool, pool_scale=pool_scale, w_out=w_out, ln1_g=ln1_g, ln1_b=ln1_b, w_up=w_up,
               conv_w=conv_w, conv_b=conv_b, w_down=w_down, ln2_g=ln2_g, ln2_b=ln2_b)
    m_d = dict(w_in=m_w_in, w_pool=m_w_pool, pool_scale=m_pool_scale, w_out=m_w_out, ln1_g=m_ln1_g, ln1_b=m_ln1_b,
               w_up=m_w_up, conv_w=m_conv_w, conv_b=m_conv_b, w_down=m_w_down, ln2_g=m_ln2_g, ln2_b=m_ln2_b)
    v_d = dict(w_in=v_w_in, w_pool=v_w_pool, pool_scale=v_pool_scale, w_out=v_w_out, ln1_g=v_ln1_g, ln1_b=v_ln1_b,
               w_up=v_w_up, conv_w=v_conv_w, conv_b=v_conv_b, w_down=v_w_down, ln2_g=v_ln2_g, ln2_b=v_ln2_b)
    g_d, delta, new_m, new_v = {}, {}, {}, {}

    def big_adamw(k, own, arr, transposed, steps, after=()):
        lay = (lambda a: a[0].T) if transposed else (lambda a: a[0])
        back = (lambda a: a.T[None]) if transposed else (lambda a: a[None])
        res = _sum_adamw(own, arr, lay(w_d[k]), lay(m_d[k]), lay(v_d[k]), "adamw_" + k, steps, after)
        g_d[k], delta[k], new_m[k], new_v[k] = (back(r) for r in res)
        return res[3]

    done = [big_adamw("w_up", own_up, arr_up, True, 4, after=(started,)),
            big_adamw("w_down", own_down, arr_down, False, 2, after=(started,)),
            big_adamw("w_out", own_out, arr_out, False, 2, after=(started,))]

    gs_small, rows = gs_small.reshape(SMALL_ROWS, 128), _small_rows()
    g_conv_w = gs_small[rows["conv_w"]:rows["conv_w"] + 3 * D_FF // 128].reshape(3, D_FF)
    g_d["conv_w"] = lax.dynamic_slice(g_conv_w, (0, me * (D_FF // N_DEV)), (3, D_FF // N_DEV))[None]
    lay = lambda k, a: jnp.transpose(a, (1, 0, 2)) if k == "conv_w" else a.reshape(-1, a.shape[-1])
    back = lambda k, a: jnp.transpose(a, (1, 0, 2)) if k == "conv_w" else a.reshape(w_d[k].shape)
    group = [k for k in names if k not in ("w_in", "w_out", "w_up", "w_down")]
    packed = [k for k in group if k != "conv_w"]
    res = _adamw([lay(k, w_d[k]) for k in group], [lay(k, g_d[k]) if k == "conv_w" else rows[k] for k in group],
                 [lay(k, m_d[k]) for k in group], [lay(k, v_d[k]) for k in group], gs_small, rows["loss"],
                 "adamw_small", after=(started,))
    for j, k in enumerate(packed):
        g_d[k] = back(k, res[j])
    for j, k in enumerate(group):
        delta[k], new_m[k], new_v[k] = (back(k, res[len(packed) + part * len(group) + j]) for part in range(3))

    arr_in = _exchange_wait(in_sems, in_src, in_land, done + [res[0]], "exchange_in_wait")
    big_adamw("w_in", own_in, arr_in, True, 4)

    loss = res[-1].reshape(())
    return (loss, grad_x[None], *[g_d[k] for k in names], *[delta[k] for k in names], *[new_m[k] for k in names],
            *[new_v[k] for k in names])
```

```python
import math

import numpy as np
import jax
import jax.numpy as jnp
from jax import lax
from jax.experimental import pallas as pl
from jax.experimental.pallas import tpu as pltpu

f32 = jnp.float32
bf16 = jnp.bfloat16

N_DEV = 8
T = 4096
D = 1024
CHUNK = 64
MIX_TILE = 512
RET_TILE = 256
HEADS = 4
DH = 128
RW = HEADS * DH
PW = 512
GROUPS = 4
WINDOWS = (2, 4, 8, 16)
IN_W = 4 * RW + PW
D_FF = 2816
LN_EPS = 1e-5
RMS_EPS = 1e-6
ALPHA = 2.0 ** 0.25
K_SCALE = DH ** -0.5

ADAM_LR = 0.001
ADAM_B1 = 0.9
ADAM_B2 = 0.999
ADAM_EPS = 1e-08
ADAM_WD = 0.01
ADAM_STEP = 10

ROWS_IN, ROWS_OUT, ROWS_UP, ROWS_DOWN = IN_W // N_DEV, D // N_DEV, 2 * D_FF // N_DEV, D_FF // N_DEV

V7X_VMEM_LIMIT = 56 * 2 ** 20
HALO = 32

NT = (((1,), (1,)), ((), ()))
TN = (((0,), (0,)), ((), ()))
NN = (((1,), (0,)), ((), ()))


def _dot(a, b, dims=NN):
    return lax.dot_general(a, b, dims, preferred_element_type=f32)


def _const_spec(shape):
    zeros = (0,) * len(shape)
    return pl.BlockSpec(shape, lambda i: zeros, pipeline_mode=pl.Buffered(1))


def _sigmoid(x):
    return 0.5 * jnp.tanh(0.5 * x) + 0.5


def _decay_tables(tt):
    h = np.arange(HEADS, dtype=np.float64)
    log_gamma = np.log(1.0 - 2.0 ** (-5.0 - h)).astype(np.float32).astype(np.float64)[:, None, None]
    idx = np.arange(tt, dtype=np.float64)
    visible = (idx[None, :] // CHUNK) <= (idx[:, None] // CHUNK)
    mask = np.where(visible[None], np.exp(log_gamma * np.abs(idx[:, None] - idx[None, :])[None]), 0.0)
    qd = np.broadcast_to(np.exp(log_gamma * (idx[None, :, None] + 1.0)), (HEADS, tt, DH))
    kd = np.broadcast_to(np.exp(log_gamma * (tt - 1.0 - idx[None, :, None])), (HEADS, tt, DH))
    cd = np.exp(log_gamma[:, 0, 0] * tt)
    return (jnp.asarray(mask, f32), jnp.asarray(qd, f32), jnp.asarray(kd, f32), [float(c) for c in cd])


def _rope_tables():
    inv_freq = (10000.0 ** (-np.arange(0, DH, 2, dtype=np.float64) / DH)).astype(np.float32)
    ang = (np.arange(T, dtype=np.float32)[:, None] * inv_freq[None, :]).astype(np.float64)
    cos, sin = np.cos(ang), np.sin(ang)
    return (jnp.asarray(np.concatenate([cos, cos], axis=1), f32), jnp.asarray(np.concatenate([-sin, sin], axis=1), f32))


def _swap_halves(t):
    return pltpu.roll(t, DH // 2, axis=1)


def _mix_forward(x, w_in_shard, w_out_shard, cos, sin, dmat, qd, kd, cdec, w_pool, pool_scale, ln1_g, ln1_b,
                 gather_bf16, gather, tt=MIX_TILE):
    n_tiles = T // tt
    to_bf16 = [w_in_shard, w_out_shard] + list(gather_bf16)
    n_c, n_g = len(to_bf16), len(gather_bf16) + len(gather)

    def body(x_ref, cos_ref, sin_ref, dmat_ref, qd_ref, kd_ref, wpool_ref, pscale_ref, g1_ref, b1_ref, *rest):
        f32_in, plain_in, rest = rest[:n_c], rest[n_c:2 + n_g], rest[2 + n_g:]
        qkv_ref, g_ref, oret_ref, states_ref, cat_ref, pooled_ref, xhat_ref, rstd_ref, x1b_ref, xb_ref = rest[:10]
        fout, gout = rest[10:12], rest[12:12 + n_g]
        state_s, pext_s, tmp_s, wint_s, wout_s, load_sems, stage_sems, *rest = rest[12 + n_g:]
        stage_s, cast_s, sems = rest[:n_c], rest[n_c:2 * n_c], rest[2 * n_c:]
        fin, gin, fsems, gsems = cast_s[:2], tuple(cast_s[2:]) + tuple(plain_in), sems[:3], sems[3:]
        i = pl.program_id(0)

        @pl.when(i == 0)
        def _():
            stage = [pltpu.make_async_copy(src, dst, stage_sems.at[j]) for j, (src, dst) in enumerate(zip(f32_in, stage_s))]
            for cp in stage:
                cp.start()
            state_s[...] = jnp.zeros_like(state_s)
            pext_s[:, pl.ds(0, HALO), :] = jnp.zeros((GROUPS, HALO, DH), f32)

            def cast(js):
                for j in js:
                    stage[j].wait()
                    cast_s[j][...] = stage_s[j][...].astype(bf16)

            _barrier(_gather_peers())
            cast(range(2))
            _gather_start(fin, fout, *fsems)
            cast(range(2, n_c))
            _gather_forward(fin, fout, *fsems)
            _gather_start(gin, gout, *gsems)
            _gather_finish(fin, fout, *fsems)
            loads = [pltpu.make_async_copy(src.at[s], dst.at[pl.ds(s * src.shape[1], src.shape[1]), :],
                                           load_sems.at[j, s])
                     for j, (src, dst) in enumerate(((fout[0], wint_s), (fout[1], wout_s))) for s in range(N_DEV)]
            for ld in loads:
                ld.start()
            for ld in loads:
                ld.wait()

        @pl.when(i == n_tiles - 3)
        def _():
            _gather_forward(gin, gout, *gsems)

        xb = x_ref[...].astype(bf16)
        xb_ref[...] = xb
        cos_t, sin_t = cos_ref[...], sin_ref[...]
        for part in range(2):
            pr = _dot(xb, wint_s[pl.ds(part * RW, RW), :], NT)
            for h in range(HEADS):
                t = pr[:, h * DH:(h + 1) * DH]
                r = t * cos_t + _swap_halves(t) * sin_t
                if part == 1:
                    r = r * K_SCALE
                qkv_ref[:, part * RW + h * DH: part * RW + (h + 1) * DH] = r.astype(bf16)
        qkv_ref[:, 2 * RW:3 * RW] = _dot(xb, wint_s[pl.ds(2 * RW, RW), :], NT).astype(bf16)
        g_ref[...] = _dot(xb, wint_s[pl.ds(3 * RW, RW), :], NT)
        p = _dot(xb, wint_s[pl.ds(4 * RW, PW), :], NT)
        for gi in range(GROUPS):
            pext_s[gi, pl.ds(HALO, tt), :] = p[:, gi * DH:(gi + 1) * DH]

        for sub in range(tt // RET_TILE):
            rows = pl.ds(sub * RET_TILE, RET_TILE)
            for h in range(HEADS):
                q = qkv_ref[rows, h * DH:(h + 1) * DH]
                k = qkv_ref[rows, RW + h * DH: RW + (h + 1) * DH]
                v = qkv_ref[rows, 2 * RW + h * DH: 2 * RW + (h + 1) * DH]
                s = _dot(q, k, NT) * dmat_ref[h]
                st = state_s[h]
                stb = st.astype(bf16)
                states_ref[sub, h] = stb
                oret_ref[rows, h * DH:(h + 1) * DH] = (_dot(s.astype(bf16), v)
                                                      + _dot((q.astype(f32) * qd_ref[h]).astype(bf16), stb))
                state_s[h] = st * cdec[h] + _dot((k.astype(f32) * kd_ref[h]).astype(bf16), v, TN)

        for h in range(HEADS):
            sl = slice(h * DH, (h + 1) * DH)
            o = oret_ref[:, sl]
            r = lax.rsqrt(jnp.mean(o * o, axis=-1, keepdims=True) + RMS_EPS)
            gg = g_ref[:, sl]
            cat_ref[:, sl] = (o * r * (gg * _sigmoid(gg))).astype(bf16)

        pos1 = (i * tt + lax.broadcasted_iota(jnp.int32, (tt, 1), 0) + 1).astype(f32)
        for gi, w in enumerate(WINDOWS):
            sl = slice(gi * DH, (gi + 1) * DH)
            stages = int(math.log2(w))
            src = pext_s
            for s in range(stages):
                lo = HALO - 8 * (stages - 1 - s)
                n = tt + HALO - lo
                shift = 2 ** s
                val = src[gi, pl.ds(lo, n), :] + src[gi, pl.ds(lo - shift, n), :]
                if s == stages - 1:
                    wsum = val
                else:
                    tmp_s[gi, pl.ds(lo, n), :] = val
                    src = tmp_s
            p_g = pext_s[gi, pl.ds(HALO, tt), :]
            pooled = (wsum / jnp.minimum(pos1, float(w)) - p_g).astype(bf16)
            pooled_ref[:, sl] = pooled
            y = _dot(pooled, wpool_ref[gi].astype(bf16)) * pscale_ref[:, sl]
            cat_ref[:, RW + gi * DH: RW + (gi + 1) * DH] = y.astype(bf16)
        pext_s[:, pl.ds(0, HALO), :] = pext_s[:, pl.ds(tt, HALO), :]

        z = ALPHA * x_ref[...] + _dot(cat_ref[...], wout_s[...])
        mu = jnp.mean(z, axis=-1, keepdims=True)
        zc = z - mu
        rstd = lax.rsqrt(jnp.mean(zc * zc, axis=-1, keepdims=True) + LN_EPS)
        xhat = zc * rstd
        xhat_ref[...] = xhat
        rstd_ref[...] = rstd
        x1b_ref[...] = (xhat * g1_ref[...] + b1_ref[...]).astype(bf16)

        @pl.when(i == n_tiles - 1)
        def _():
            _gather_finish(gin, gout, *gsems)

    tile = lambda w: pl.BlockSpec((tt, w), lambda i: (i, 0))
    hbm = pl.BlockSpec(memory_space=pltpu.HBM)
    out_shape = (
        jax.ShapeDtypeStruct((T, 3 * RW), bf16),
        jax.ShapeDtypeStruct((T, RW), f32),
        jax.ShapeDtypeStruct((T, RW), f32),
        jax.ShapeDtypeStruct((T // RET_TILE, HEADS, DH, DH), bf16),
        jax.ShapeDtypeStruct((T, D), bf16),
        jax.ShapeDtypeStruct((T, PW), bf16),
        jax.ShapeDtypeStruct((T, D), f32),
        jax.ShapeDtypeStruct((T, 1), f32),
        jax.ShapeDtypeStruct((T, D), bf16),
        jax.ShapeDtypeStruct((T, D), bf16),
    ) + tuple(jax.ShapeDtypeStruct((N_DEV,) + b.shape, bf16) for b in to_bf16
              ) + tuple(jax.ShapeDtypeStruct((N_DEV,) + b.shape, b.dtype) for b in gather)
    return pl.pallas_call(
        body, name="mix_forward", grid=(n_tiles,), out_shape=out_shape,
        in_specs=[tile(D), tile(DH), tile(DH),
                  _const_spec((HEADS, RET_TILE, RET_TILE)), _const_spec((HEADS, RET_TILE, DH)),
                  _const_spec((HEADS, RET_TILE, DH)),
                  _const_spec((GROUPS, DH, DH)), _const_spec((1, PW)),
                  _const_spec((1, D)), _const_spec((1, D))] + [hbm] * (2 + n_g),
        out_specs=(tile(3 * RW), tile(RW), tile(RW),
                   pl.BlockSpec((tt // RET_TILE, HEADS, DH, DH), lambda i: (i, 0, 0, 0)),
                   tile(D), tile(PW), tile(D), tile(1), tile(D), tile(D)) + (hbm,) * (2 + n_g),
        scratch_shapes=[pltpu.VMEM((HEADS, DH, DH), f32), pltpu.VMEM((GROUPS, tt + HALO, DH), f32),
                        pltpu.VMEM((GROUPS, tt + HALO, DH), f32), pltpu.VMEM((IN_W, D), bf16), pltpu.VMEM((D, D), bf16),
                        pltpu.SemaphoreType.DMA((2, N_DEV)), pltpu.SemaphoreType.DMA((n_c,))]
        + [pltpu.VMEM(b.shape, f32) for b in to_bf16] + [pltpu.VMEM(b.shape, bf16) for b in to_bf16]
        + _gather_sems(2) + _gather_sems(n_g),
        compiler_params=pltpu.CompilerParams(dimension_semantics=("arbitrary",), vmem_limit_bytes=V7X_VMEM_LIMIT,
                                             collective_id=GATHER_BARRIER),
    )(x, cos, sin, dmat, qd, kd, w_pool, pool_scale, ln1_g, ln1_b, *to_bf16, *gather)


def _ffn_forward_backward(xhat1, rstd1, ln1_g, ln1_b, w_up_t, conv_w, conv_b, w_down, ln2_g, ln2_b, target,
                          tt=256):
    n_tiles = T // tt
    FH = 16
    hb = tt // FH

    def body(xhat_ref, halo_ref, rstd_ref, g1_ref, b1_ref, wupt_ref, cw_ref, cb_ref, wdown_ref, g2_ref, b2_ref, tgt_ref,
             dz1_ref, dz2b_ref, du_ref, f_ref, loss_ref, dg2_ref, db2_ref, dg1_ref, db1_ref, dcb_ref, dcw_ref,
             gext_s, val_s, dhext_s):
        i = pl.program_id(0)
        tile_idx = n_tiles - 1 - i

        def rd(ref, off):
            return jnp.concatenate([ref[k, pl.ds(off, tt), :] for k in range(D_FF // 128)], axis=1)

        def wr(ref, val):
            for k in range(D_FF // 128):
                ref[k, pl.ds(0, val.shape[0]), :] = val[:, k * 128:(k + 1) * 128]

        @pl.when(i == 0)
        def _():
            for r in (loss_ref, dg2_ref, db2_ref, dg1_ref, db1_ref, dcb_ref, dcw_ref):
                r[...] = jnp.zeros_like(r)
            dhext_s[:, pl.ds(tt, 8), :] = jnp.zeros((D_FF // 128, 8, 128), f32)

        g1, b1 = g1_ref[...], b1_ref[...]
        xhat = xhat_ref[...]
        x1 = xhat * g1 + b1
        x1b = x1.astype(bf16)
        x1h = ((halo_ref[...] * g1 + b1) * jnp.where(tile_idx == 0, 0.0, 1.0)).astype(bf16)
        x1ext = jnp.concatenate([x1h, x1b], axis=0)

        val = _dot(x1b, wupt_ref[pl.ds(0, D_FF), :], NT)
        gate_ext = _dot(x1ext, wupt_ref[pl.ds(D_FF, D_FF), :], NT)
        wr(gext_s, gate_ext)
        hh = (cb_ref[...] + cw_ref[0:1, :] * rd(gext_s, FH - 2) + cw_ref[1:2, :] * rd(gext_s, FH - 1)
              + cw_ref[2:3, :] * gate_ext[FH:])
        sg = _sigmoid(hh)
        act = hh * sg
        wr(dhext_s, act)
        val_s[...] = val * (sg + act * (1.0 - sg))
        fb = (act * val).astype(bf16)
        f_ref[...] = fb

        z = ALPHA * x1 + _dot(fb, wdown_ref[...])
        mu = jnp.mean(z, axis=-1, keepdims=True)
        zc = z - mu
        rstd2 = lax.rsqrt(jnp.mean(zc * zc, axis=-1, keepdims=True) + LN_EPS)
        xh2 = zc * rstd2
        diff = xh2 * g2_ref[...] + b2_ref[...] - tgt_ref[...]
        loss_ref[...] += 0.5 * jnp.sum(diff * diff) / D
        dy = diff * (1.0 / D)
        dg2_ref[...] += jnp.sum(dy * xh2, axis=0, keepdims=True)
        db2_ref[...] += jnp.sum(dy, axis=0, keepdims=True)
        dyg = dy * g2_ref[...]
        dz2 = rstd2 * (dyg - jnp.mean(dyg, axis=-1, keepdims=True) - xh2 * jnp.mean(dyg * xh2, axis=-1, keepdims=True))
        dz2b = dz2.astype(bf16)
        dz2b_ref[...] = dz2b

        df = _dot(dz2b, wdown_ref[...], NT)
        dval = df * rd(dhext_s, 0)
        dh = df * val_s[...]
        wr(dhext_s, dh)
        dh1, dh2, g0 = rd(dhext_s, 1), rd(dhext_s, 2), rd(gext_s, FH)
        dcb_ref[...] += jnp.sum(dh, axis=0, keepdims=True)
        dcw_ref[0:1, :] += jnp.sum(dh2 * g0, axis=0, keepdims=True)
        dcw_ref[1:2, :] += jnp.sum(dh1 * g0, axis=0, keepdims=True)
        dcw_ref[2:3, :] += jnp.sum(dh * g0, axis=0, keepdims=True)
        dgate = cw_ref[2:3, :] * dh + cw_ref[1:2, :] * dh1 + cw_ref[0:1, :] * dh2
        dvalb, dgateb = dval.astype(bf16), dgate.astype(bf16)
        du_ref[:, :D_FF] = dvalb
        du_ref[:, D_FF:] = dgateb
        dx1 = ALPHA * dz2 + _dot(dvalb, wupt_ref[pl.ds(0, D_FF), :]) + _dot(dgateb, wupt_ref[pl.ds(D_FF, D_FF), :])
        dhext_s[:, pl.ds(tt, 8), :] = dhext_s[:, pl.ds(0, 8), :]

        dg1_ref[...] += jnp.sum(dx1 * xhat, axis=0, keepdims=True)
        db1_ref[...] += jnp.sum(dx1, axis=0, keepdims=True)
        dxg = dx1 * g1
        dz1_ref[...] = rstd_ref[...] * (dxg - jnp.mean(dxg, axis=-1, keepdims=True)
                                        - xhat * jnp.mean(dxg * xhat, axis=-1, keepdims=True))

    rtile = lambda w: pl.BlockSpec((tt, w), lambda i: (n_tiles - 1 - i, 0))
    acc = lambda shape: pl.BlockSpec(shape, lambda i: (0, 0))
    out_shape = (
        jax.ShapeDtypeStruct((T, D), f32),
        jax.ShapeDtypeStruct((T, D), bf16),
        jax.ShapeDtypeStruct((T, 2 * D_FF), bf16),
        jax.ShapeDtypeStruct((T, D_FF), bf16),
        jax.ShapeDtypeStruct((8, 128), f32),
        jax.ShapeDtypeStruct((1, D), f32), jax.ShapeDtypeStruct((1, D), f32),
        jax.ShapeDtypeStruct((1, D), f32), jax.ShapeDtypeStruct((1, D), f32),
        jax.ShapeDtypeStruct((1, D_FF), f32), jax.ShapeDtypeStruct((3, D_FF), f32),
    )
    return pl.pallas_call(
        body, name="ffn_forward_backward", grid=(n_tiles,), out_shape=out_shape,
        in_specs=[rtile(D),
                  pl.BlockSpec((FH, D), lambda i: (jnp.maximum((n_tiles - 1 - i) * hb - 1, 0), 0)),
                  rtile(1), _const_spec((1, D)), _const_spec((1, D)), _const_spec((2 * D_FF, D)),
                  _const_spec((3, D_FF)), _const_spec((1, D_FF)), _const_spec((D_FF, D)),
                  _const_spec((1, D)), _const_spec((1, D)), rtile(D)],
        out_specs=(rtile(D), rtile(D), rtile(2 * D_FF), rtile(D_FF), acc((8, 128)),
                   acc((1, D)), acc((1, D)), acc((1, D)), acc((1, D)), acc((1, D_FF)), acc((3, D_FF))),
        scratch_shapes=[pltpu.VMEM((D_FF // 128, tt + FH, 128), f32), pltpu.VMEM((tt, D_FF), f32),
                        pltpu.VMEM((D_FF // 128, tt + 8, 128), f32)],
        compiler_params=pltpu.CompilerParams(dimension_semantics=("arbitrary",), vmem_limit_bytes=V7X_VMEM_LIMIT),
    )(xhat1, xhat1, rstd1, ln1_g, ln1_b, w_up_t, conv_w, conv_b, w_down, ln2_g, ln2_b, target)


def _mix_backward(dz1, w_out, qkv, g, oret, states, pooled, cat, cos, sin, dmat, qd, kd, cdec, w_pool, pool_scale, w_in_t,
                  small_ffn, after, tt=MIX_TILE):
    n_tiles = T // tt

    def body(dz1_ref, wout_ref, qkv_ref, g_ref, oret_ref, states_ref, pooled_ref, cat_ref, cos_ref, sin_ref, dmat_ref,
             qd_ref, kd_ref, wpool_ref, pscale_ref, wint_ref, *rest):
        ffn_refs, rest = rest[:len(SMALL_FFN)], rest[len(SMALL_FFN):]
        after_ref, dproj_ref, gx_ref, small_ref, dwout_ref, dstate_s, dout_s, eext_s, tmp_s, dwout_s, dpscale_s = rest
        i = pl.program_id(0)
        tile_idx = n_tiles - 1 - i

        @pl.when(i == 0)
        def _():
            dstate_s[...] = jnp.zeros_like(dstate_s)
            small_ref[...] = jnp.zeros_like(small_ref)
            dpscale_s[...] = jnp.zeros_like(dpscale_s)
            dwout_s[...] = jnp.zeros_like(dwout_s)
            eext_s[:, pl.ds(tt, HALO), :] = jnp.zeros((GROUPS, HALO, DH), f32)

        dz1 = dz1_ref[...]
        dz1b = dz1.astype(bf16)
        dcat = _dot(dz1b, wout_ref[...], NT)
        dwout_s[...] += _dot(cat_ref[...], dz1b, TN)

        pos1 = (tile_idx * tt + lax.broadcasted_iota(jnp.int32, (tt, 1), 0) + 1).astype(f32)
        for gi, w in enumerate(WINDOWS):
            sl = slice(gi * DH, (gi + 1) * DH)
            dpo = dcat[:, RW + gi * DH: RW + (gi + 1) * DH]
            pooled_g = pooled_ref[:, sl]
            wpool_g = wpool_ref[gi].astype(bf16)
            ylin = _dot(pooled_g, wpool_g)
            dpscale_s[:, sl] += jnp.sum(dpo * ylin, axis=0, keepdims=True)
            dpw = (dpo * pscale_ref[:, sl]).astype(bf16)
            small_ref[pl.ds(gi * DH, DH), :] += _dot(pooled_g, dpw, TN)
            dpooled = _dot(dpw, wpool_g, NT)
            eext_s[gi, pl.ds(0, tt), :] = dpooled / jnp.minimum(pos1, float(w))
            stages = int(math.log2(w))
            src = eext_s
            for s in range(stages):
                n = tt + 8 * (stages - 1 - s)
                shift = 2 ** s
                val = src[gi, pl.ds(0, n), :] + src[gi, pl.ds(shift, n), :]
                if s == stages - 1:
                    wsum = val
                else:
                    tmp_s[gi, pl.ds(0, n), :] = val
                    src = tmp_s
            dproj_ref[:, 4 * RW + gi * DH: 4 * RW + (gi + 1) * DH] = (wsum - dpooled).astype(bf16)
        eext_s[:, pl.ds(tt, HALO), :] = eext_s[:, pl.ds(0, HALO), :]

        for h in range(HEADS):
            sl = slice(h * DH, (h + 1) * DH)
            dr = dcat[:, sl]
            o = oret_ref[:, sl]
            r = lax.rsqrt(jnp.mean(o * o, axis=-1, keepdims=True) + RMS_EPS)
            rn = o * r
            gg = g_ref[:, sl]
            sg = _sigmoid(gg)
            dproj_ref[:, 3 * RW + h * DH: 3 * RW + (h + 1) * DH] = (dr * rn * (sg * (1.0 + gg * (1.0 - sg)))).astype(bf16)
            drn = dr * (gg * sg)
            dout_s[:, sl] = (r * (drn - rn * jnp.mean(drn * rn, axis=-1, keepdims=True))).astype(bf16)

        for sub in reversed(range(tt // RET_TILE)):
            rows = pl.ds(sub * RET_TILE, RET_TILE)
            cos_t, sin_t = cos_ref[rows, :], sin_ref[rows, :]
            for h in range(HEADS):
                q = qkv_ref[rows, h * DH:(h + 1) * DH]
                k = qkv_ref[rows, RW + h * DH: RW + (h + 1) * DH]
                v = qkv_ref[rows, 2 * RW + h * DH: 2 * RW + (h + 1) * DH]
                do = dout_s[rows, h * DH:(h + 1) * DH]
                stb = states_ref[sub, h]
                dst = dstate_s[h]
                dstb = dst.astype(bf16)
                sb = (_dot(q, k, NT) * dmat_ref[h]).astype(bf16)
                dsb = (_dot(do, v, NT) * dmat_ref[h]).astype(bf16)
                dq = _dot(dsb, k) + _dot(do, stb, NT) * qd_ref[h]
                dk = _dot(dsb, q, TN) + _dot(v, dstb, NT) * kd_ref[h]
                dv = _dot(sb, do, TN) + _dot((k.astype(f32) * kd_ref[h]).astype(bf16), dstb)
                dstate_s[h] = dst * cdec[h] + _dot((q.astype(f32) * qd_ref[h]).astype(bf16), do, TN)
                dproj_ref[rows, h * DH:(h + 1) * DH] = (dq * cos_t - _swap_halves(dq) * sin_t).astype(bf16)
                dproj_ref[rows, RW + h * DH: RW + (h + 1) * DH] = (
                    (dk * cos_t - _swap_halves(dk) * sin_t) * K_SCALE).astype(bf16)
                dproj_ref[rows, 2 * RW + h * DH: 2 * RW + (h + 1) * DH] = dv.astype(bf16)

        gx_ref[...] = ALPHA * dz1 + _dot(dproj_ref[...], wint_ref[...])

        @pl.when(i == n_tiles - 1)
        def _():
            dwout_ref[...] = dwout_s[...].astype(bf16)
            at = GROUPS * DH
            for ref, size in [(dpscale_s, PW)] + [(ref, size) for ref, (_, size) in zip(ffn_refs, SMALL_FFN)]:
                for j in range(size // 128):
                    r, k = divmod(j, ref.shape[1] // 128)
                    small_ref[at + j: at + j + 1, :] = ref[r:r + 1, k * 128:(k + 1) * 128]
                at = SMALL_FFN_AT if ref is dpscale_s else at + size // 128

    rtile = lambda w: pl.BlockSpec((tt, w), lambda i: (n_tiles - 1 - i, 0))
    out_shape = (
        jax.ShapeDtypeStruct((T, IN_W), bf16),
        jax.ShapeDtypeStruct((T, D), f32),
        jax.ShapeDtypeStruct((SMALL_ROWS, 128), f32),
        jax.ShapeDtypeStruct((D, D), bf16),
    )
    return pl.pallas_call(
        body, name="mix_backward", grid=(n_tiles,), out_shape=out_shape,
        in_specs=[rtile(D), _const_spec((D, D)), rtile(3 * RW), rtile(RW), rtile(RW),
                  pl.BlockSpec((tt // RET_TILE, HEADS, DH, DH), lambda i: (n_tiles - 1 - i, 0, 0, 0)),
                  rtile(PW), rtile(D), rtile(DH), rtile(DH),
                  _const_spec((HEADS, RET_TILE, RET_TILE)), _const_spec((HEADS, RET_TILE, DH)),
                  _const_spec((HEADS, RET_TILE, DH)),
                  _const_spec((GROUPS, DH, DH)), _const_spec((1, PW)), _const_spec((IN_W, D)),
                  *[_const_spec(a.shape) for a in small_ffn], pl.BlockSpec(memory_space=pl.ANY)],
        out_specs=(rtile(IN_W), rtile(D), pl.BlockSpec((SMALL_ROWS, 128), lambda i: (0, 0)),
                   pl.BlockSpec((D, D), lambda i: (0, 0), pipeline_mode=pl.Buffered(1))),
        scratch_shapes=[pltpu.VMEM((HEADS, DH, DH), f32), pltpu.VMEM((tt, RW), bf16),
                        pltpu.VMEM((GROUPS, tt + HALO, DH), f32), pltpu.VMEM((GROUPS, tt + HALO, DH), f32),
                        pltpu.VMEM((D, D), f32), pltpu.VMEM((1, PW), f32)],
        compiler_params=pltpu.CompilerParams(dimension_semantics=("arbitrary",), vmem_limit_bytes=V7X_VMEM_LIMIT),
    )(dz1, w_out, qkv, g, oret, states, pooled, cat, cos, sin, dmat, qd, kd, w_pool, pool_scale, w_in_t, *small_ffn,
      after)


def _weight_grad(a, b, name, tm, reduce=()):
    m = a.shape[1]
    n_m, n_r = m // tm, len(reduce)
    assert not n_r or n_m >= 2
    assert b.dtype == bf16
    n_b = 4
    tb = T // n_b
    tr = tm // 2 if tm > 1024 else tm

    def body(a_ref, b_ref, *rest):
        ins, o_ref, own, arrived = rest[:n_r], rest[n_r], rest[n_r + 1:2 * n_r + 1], rest[2 * n_r + 1:3 * n_r + 1]
        landing, mine, sems = rest[3 * n_r + 1:4 * n_r + 1], rest[4 * n_r + 1:5 * n_r + 1], rest[5 * n_r + 1:-3]
        b_s, b_sems, acc_s = rest[-3:]
        i = pl.program_id(0)

        def b_load(kb):
            return pltpu.make_async_copy(b_ref.at[pl.ds(kb * tb, tb)], b_s.at[pl.ds(kb * tb, tb)], b_sems.at[kb])

        @pl.when(i == 0)
        def _():
            for kb in range(n_b):
                b_load(kb).start()

        if n_r:
            pair_send, pair_recv, local_sems, chip_send, chip_recv = sems
            me = _me()
            x, y, c = me
            sibling = (x, y, 1 - c)

            def pair_copy(k, j):
                return _remote(ins[j].at[_slot(*_chip(me, k), 1 - c)], landing[j].at[k], pair_send.at[k, j],
                               pair_recv.at[k, j], sibling)

            def load(k, j):
                return pltpu.make_async_copy(ins[j].at[_slot(*_chip(me, k), c)], mine[j].at[k], local_sems.at[k, j])

            def store(j):
                return pltpu.make_async_copy(mine[j].at[0], own[j], local_sems.at[0, j])

            def chip_copy(k, j):
                return _remote(mine[j].at[k], arrived[j].at[k - 1], chip_send.at[k - 1, j], chip_recv.at[k - 1, j],
                               (*_chip(me, k), c))

            @pl.when(i == 0)
            def _():
                _barrier([sibling] + _chip_peers())
                for k in range(4):
                    for j in range(n_r):
                        pair_copy(k, j).start()
                        load(k, j).start()

            @pl.when(i == 1)
            def _():
                for k in range(4):
                    for j in range(n_r):
                        load(k, j).wait()
                        pair_copy(k, j).wait_recv()
                        mine[j][k] = (mine[j][k].astype(f32) + landing[j][k].astype(f32)).astype(mine[j].dtype)
                        (store(j) if k == 0 else chip_copy(k, j)).start()

        @pl.when(i == 0)
        def _():
            for r0 in range(0, tm, tr):
                for kb in range(n_b):
                    if r0 == 0:
                        b_load(kb).wait()
                    part = _dot(a_ref[pl.ds(kb * tb, tb), r0:r0 + tr], b_s[pl.ds(kb * tb, tb), :], TN)
                    if kb == 0:
                        acc_s[...] = part
                    elif kb < n_b - 1:
                        acc_s[...] += part
                    else:
                        o_ref[r0:r0 + tr, :] = (acc_s[...] + part).astype(bf16)

        @pl.when(i > 0)
        def _():
            o_ref[...] = _dot(a_ref[...], b_s[...], TN).astype(bf16)

        if n_r:
            @pl.when(i == n_m - 1)
            def _():
                for j in range(n_r):
                    store(j).wait()
                    for k in range(1, 4):
                        chip_copy(k, j).wait_recv()
                for j in range(n_r):
                    for k in range(1, 4):
                        chip_copy(k, j).wait_send()
                    for k in range(4):
                        pair_copy(k, j).wait_send()

    hbm = pl.BlockSpec(memory_space=pltpu.HBM)
    return pl.pallas_call(
        body, name=name, grid=(n_m,),
        out_shape=(jax.ShapeDtypeStruct((m, D), bf16),)
        + tuple(jax.ShapeDtypeStruct(p.shape[1:], p.dtype) for p in reduce)
        + tuple(jax.ShapeDtypeStruct((3,) + p.shape[1:], p.dtype) for p in reduce),
        in_specs=[pl.BlockSpec((T, tm), lambda i: (0, i)), pl.BlockSpec(memory_space=pl.ANY)] + [hbm] * n_r,
        out_specs=(pl.BlockSpec((tm, D), lambda i: (i, 0)),) + (hbm,) * (2 * n_r),
        scratch_shapes=[pltpu.VMEM((4,) + p.shape[1:], p.dtype) for p in reduce] * 2
        + ([pltpu.SemaphoreType.DMA((4, n_r))] * 3 + [pltpu.SemaphoreType.DMA((3, n_r))] * 2 if n_r else [])
        + [pltpu.VMEM((T, D), bf16), pltpu.SemaphoreType.DMA((n_b,)), pltpu.VMEM((tr, D), f32)],
        compiler_params=pltpu.CompilerParams(dimension_semantics=("arbitrary",), vmem_limit_bytes=V7X_VMEM_LIMIT,
                                             collective_id=REDUCE_BARRIER if n_r else None),
    )(a, b, *reduce)


CHIP_FLIPS = ((1, 0), (0, 1), (1, 1))
PAIR_BARRIER, CHIP_BARRIER, GATHER_BARRIER, CHIP_BARRIER_SPLIT, REDUCE_BARRIER, ALL_BARRIER = 0, 1, 2, 3, 4, 5
ALL_FLIPS = tuple((fx, fy, fc) for fx in (0, 1) for fy in (0, 1) for fc in (0, 1))[1:]


def _flip(me, f):
    return tuple(1 - v if b else v for v, b in zip(me, f))


def _barrier(peers):
    sem = pltpu.get_barrier_semaphore()
    for peer in peers:
        pl.semaphore_signal(sem, inc=1, device_id=peer, device_id_type=pl.DeviceIdType.MESH)
    pl.semaphore_wait(sem, len(peers))


def _me():
    return lax.axis_index("x"), lax.axis_index("y"), lax.axis_index("c")


def _chip(me, k):
    x, y, _ = me
    if k == 0:
        return x, y
    fx, fy = CHIP_FLIPS[k - 1]
    return (1 - x if fx else x), (1 - y if fy else y)


def _slot(x, y, c):
    return 4 * x + 2 * y + c


def _remote(src, dst, send_sem, recv_sem, to):
    return pltpu.make_async_remote_copy(src_ref=src, dst_ref=dst, send_sem=send_sem, recv_sem=recv_sem,
                                        device_id=to, device_id_type=pl.DeviceIdType.MESH)


def _gather_sems(n):
    return [pltpu.SemaphoreType.DMA((7, n)), pltpu.SemaphoreType.DMA((7, n)), pltpu.SemaphoreType.DMA((n,))] if n else []


def _gather_copy(k, j, gin, gout, send_sems, recv_sems, sending):
    x, y, c = _me()
    sibling, x_chip, y_chip, d_chip = (x, y, 1 - c), (1 - x, y), (x, 1 - y), (1 - x, 1 - y)
    south = c == 0
    passed_on = (jnp.where(south, 1 - x, x), jnp.where(south, y, 1 - y), c)
    src, to = gin[j], sibling
    if sending:
        block = {0: (x, y, c), 1: (x, y, c), 2: (x, y, c), 3: passed_on, 4: (*x_chip, c), 5: (*y_chip, c), 6: (*d_chip, c)}[k]
        to = {1: (*x_chip, c), 2: (*y_chip, c), 3: (jnp.where(south, x, 1 - x), jnp.where(south, 1 - y, y), c)}.get(k, sibling)
        if k >= 3:
            src = gout[j].at[_slot(*block)]
    else:
        block = {0: sibling, 1: (*x_chip, c), 2: (*y_chip, c), 3: (*d_chip, c), 4: (*x_chip, 1 - c), 5: (*y_chip, 1 - c),
                 6: (*d_chip, 1 - c)}[k]
    return _remote(src, gout[j].at[_slot(*block)], send_sems.at[k, j], recv_sems.at[k, j], to)


def _gather_do(ks, action, gin, gout, send_sems, recv_sems):
    for k in ks:
        for j in range(len(gin)):
            cp = _gather_copy(k, j, gin, gout, send_sems, recv_sems, action != "wait_recv")
            getattr(cp, action)()


def _gather_peers():
    x, y, c = _me()
    return [(x, y, 1 - c), (1 - x, y, c), (x, 1 - y, c)]


def _gather_start(gin, gout, send_sems, recv_sems, local_sems):
    for j in range(len(gin)):
        pltpu.make_async_copy(gin[j], gout[j].at[_slot(*_me())], local_sems.at[j]).start()
    _gather_do((0, 1, 2), "start", gin, gout, send_sems, recv_sems)


def _gather_forward(gin, gout, send_sems, recv_sems, local_sems):
    _gather_do((1, 2), "wait_recv", gin, gout, send_sems, recv_sems)
    _gather_do((3, 4, 5), "start", gin, gout, send_sems, recv_sems)


def _gather_finish(gin, gout, send_sems, recv_sems, local_sems):
    _gather_do((3,), "wait_recv", gin, gout, send_sems, recv_sems)
    _gather_do((6,), "start", gin, gout, send_sems, recv_sems)
    _gather_do((0, 4, 5, 6), "wait_recv", gin, gout, send_sems, recv_sems)
    _gather_do(range(7), "wait_send", gin, gout, send_sems, recv_sems)
    for j in range(len(gin)):
        pltpu.make_async_copy(gin[j], gout[j].at[_slot(*_me())], local_sems.at[j]).wait()


def _pair_reduce(parts, name, gather_sum=None):
    n = len(parts)
    n_h = 0 if gather_sum is None else 1

    def body(*refs):
        ins, g_terms, refs = refs[:n], refs[n:n + 2 * n_h], refs[n + 2 * n_h:]
        own, others, g_out, refs = refs[:n], refs[n:2 * n], refs[2 * n:2 * n + n_h], refs[2 * n + n_h:]
        landing, mine, (send_sems, recv_sems, local_sems), g_scratch = refs[:n], refs[n:2 * n], refs[2 * n:2 * n + 3], refs[2 * n + 3:]
        me = _me()
        x, y, c = me
        sibling = (x, y, 1 - c)
        _barrier([_flip(me, f) for f in ALL_FLIPS] if n_h else [sibling])
        if n_h:
            piece_s, g_send, g_recv, g_local = g_scratch
            acc = g_terms[0][...].astype(f32)
            for k in range(3):
                acc = acc + g_terms[1][k].astype(f32)
            piece_s[...] = acc

            def g_copy(q):
                return _remote(piece_s, g_out[0].at[_slot(*me)], g_send.at[q, 0], g_recv.at[q, 0], _flip(me, ALL_FLIPS[q]))

            g_mine = pltpu.make_async_copy(piece_s, g_out[0].at[_slot(*me)], g_local.at[0])
            g_mine.start()
            for q in range(len(ALL_FLIPS)):
                g_copy(q).start()
        sends, loads = [], []
        for k in range(4):
            for j in range(n):
                cp = _remote(ins[j].at[_slot(*_chip(me, k), 1 - c)], landing[j].at[k], send_sems.at[k, j],
                             recv_sems.at[k, j], sibling)
                cp.start()
                sends.append(cp)
                ld = pltpu.make_async_copy(ins[j].at[_slot(*_chip(me, k), c)], mine[j].at[k], local_sems.at[k, j])
                ld.start()
                loads.append(ld)
        stores = []
        for k in range(4):
            for j in range(n):
                loads[k * n + j].wait()
                _remote(ins[j].at[0], landing[j].at[k], send_sems.at[k, j], recv_sems.at[k, j], sibling).wait_recv()
                mine[j][k] = (mine[j][k].astype(f32) + landing[j][k].astype(f32)).astype(mine[j].dtype)
                st = pltpu.make_async_copy(mine[j].at[k], own[j] if k == 0 else others[j].at[k - 1], local_sems.at[k, j])
                st.start()
                stores.append(st)
        for cp in sends:
            cp.wait_send()
        for st in stores:
            st.wait()
        if n_h:
            for q in range(len(ALL_FLIPS)):
                g_copy(q).wait_recv()
            for q in range(len(ALL_FLIPS)):
                g_copy(q).wait_send()
            g_mine.wait()

    vm, hbm = pl.BlockSpec(memory_space=pltpu.VMEM), pl.BlockSpec(memory_space=pltpu.HBM)
    g_shape = gather_sum[0].shape if n_h else ()
    return pl.pallas_call(
        body, name=name,
        out_shape=tuple(jax.ShapeDtypeStruct(p.shape[1:], p.dtype) for p in parts)
        + tuple(jax.ShapeDtypeStruct((3,) + p.shape[1:], p.dtype) for p in parts)
        + tuple([jax.ShapeDtypeStruct((N_DEV,) + g_shape, f32)] * n_h),
        in_specs=[hbm] * n + [vm] * (2 * n_h), out_specs=(hbm,) * (2 * n + n_h),
        scratch_shapes=[pltpu.VMEM((4,) + p.shape[1:], p.dtype) for p in parts] * 2
        + [pltpu.SemaphoreType.DMA((4, n)), pltpu.SemaphoreType.DMA((4, n)), pltpu.SemaphoreType.DMA((4, n))]
        + ([pltpu.VMEM(g_shape, f32)] + _gather_sems(1)) * n_h,
        compiler_params=pltpu.CompilerParams(vmem_limit_bytes=V7X_VMEM_LIMIT,
                                             collective_id=ALL_BARRIER if n_h else PAIR_BARRIER),
    )(*parts, *(gather_sum or ()))


def _chip_peers():
    me = _me()
    return [(*_chip(me, k), me[2]) for k in range(1, 4)]


def _split_copies(src_ref, dst_ref, sems):
    me = _me()
    return [_remote(src_ref.at[k - 1], dst_ref.at[k - 1], sems[k - 1], sems[2 + k], (*_chip(me, k), me[2]))
            for k in range(1, 4)]


def _exchange_start(others, name, barrier_id):
    def body(src_ref, land_ref, *rest):
        sems, token_ref = rest[:6], rest[8]
        _barrier(_chip_peers())
        for copy in _split_copies(src_ref, land_ref, sems):
            copy.start()
        token_ref[...] = jnp.zeros_like(token_ref)

    hbm, sem = pl.BlockSpec(memory_space=pltpu.HBM), pl.BlockSpec(memory_space=pltpu.SEMAPHORE)
    thru = pltpu.HBM(others.shape, others.dtype)
    res = pl.pallas_call(
        body, name=name,
        out_shape=(pltpu.SemaphoreType.DMA(()),) * 6 + (thru, thru, jax.ShapeDtypeStruct((8, 128), f32)),
        in_specs=(hbm, hbm), out_specs=(sem,) * 6 + (hbm, hbm, pl.BlockSpec(memory_space=pltpu.VMEM)),
        input_output_aliases={0: 6, 1: 7},
        compiler_params=pltpu.CompilerParams(has_side_effects=pltpu.SideEffectType.DATAFLOW_SIDE_EFFECTING,
                                             collective_id=barrier_id),
    )(pltpu.with_memory_space_constraint(others, pltpu.HBM),
      pltpu.with_memory_space_constraint(lax.empty(others.shape, others.dtype), pltpu.HBM))
    return res[:6], res[6], res[7], res[8]


def _exchange_wait(sems, src_thru, land_thru, after, name):
    n_after = len(after)

    def body(src_ref, land_ref, *rest):
        for copy in _split_copies(src_ref, land_ref, rest[:6]):
            copy.wait_send()
            copy.wait_recv()

    hbm, sem = pl.BlockSpec(memory_space=pltpu.HBM), pl.BlockSpec(memory_space=pltpu.SEMAPHORE)
    thru = pltpu.HBM(src_thru.shape, src_thru.dtype)
    return pl.pallas_call(
        body, name=name, out_shape=(thru, thru),
        in_specs=(hbm, hbm) + (sem,) * 6 + (pl.BlockSpec(memory_space=pl.ANY),) * n_after, out_specs=(hbm, hbm),
        input_output_aliases={0: 0, 1: 1},
        compiler_params=pltpu.CompilerParams(has_side_effects=pltpu.SideEffectType.DATAFLOW_SIDE_EFFECTING),
    )(src_thru, land_thru, *sems, *after)[1]


def _adam_update(w, g, m, v):
    m = ADAM_B1 * m + (1.0 - ADAM_B1) * g
    v = ADAM_B2 * v + (1.0 - ADAM_B2) * (g * g)
    m_hat = m / (1.0 - ADAM_B1 ** ADAM_STEP)
    v_hat = v / (1.0 - ADAM_B2 ** ADAM_STEP)
    return -ADAM_LR * (m_hat / (jnp.sqrt(v_hat) + ADAM_EPS) + ADAM_WD * w), m, v


def _sum_adamw(own, arrived, w, m, v, name, steps, after=()):
    rows = own.shape[0]
    br = rows // steps

    def body(own_ref, arr_ref, w_ref, m_ref, v_ref, *rest):
        g_out, d_out, m_out, v_out = rest[len(after):]
        g = own_ref[...].astype(f32)
        for k in range(3):
            g = g + arr_ref[k].astype(f32)
        g_out[...] = g
        d_out[...], m_out[...], v_out[...] = _adam_update(w_ref[...], g, m_ref[...], v_ref[...])

    blk = pl.BlockSpec((br, D), lambda i: (i, 0))
    return pl.pallas_call(
        body, name=name, grid=(steps,), out_shape=(jax.ShapeDtypeStruct((rows, D), f32),) * 4,
        in_specs=[blk, pl.BlockSpec((3, br, D), lambda i: (0, i, 0)), blk, blk, blk]
        + [pl.BlockSpec(memory_space=pl.ANY)] * len(after), out_specs=(blk,) * 4,
        compiler_params=pltpu.CompilerParams(dimension_semantics=("parallel",), vmem_limit_bytes=V7X_VMEM_LIMIT),
    )(own, arrived, w, m, v, *after)


def _adamw(ws, gs, ms, vs, packed, scalar_row, name, after=()):
    n = len(ws)
    given = [g for g in gs if not isinstance(g, int)]
    taken = [j for j in range(n) if isinstance(gs[j], int)]

    def body(packed_ref, *refs):
        w_r, m_r, v_r = (refs[k * n:(k + 1) * n] for k in range(3))
        given_r, outs = list(refs[3 * n:3 * n + len(given)]), refs[3 * n + len(given) + len(after):]
        g_o, outs = dict(zip(taken, outs[:len(taken)])), outs[len(taken):]
        d_o, m_o, v_o = (outs[k * n:(k + 1) * n] for k in range(3))
        outs[3 * n][...] = packed_ref[scalar_row:scalar_row + 1, 0:1]
        for j in range(n):
            if j in g_o:
                (r, c), at = ws[j].shape, gs[j]
                if c == 128:
                    g = packed_ref[at:at + r, :]
                else:
                    assert r == 1
                    g = jnp.concatenate([packed_ref[at + k:at + k + 1, :] for k in range(c // 128)], axis=1)
                g_o[j][...] = g
            else:
                g = given_r.pop(0)[...]
            d_o[j][...], m_o[j][...], v_o[j][...] = _adam_update(w_r[j][...], g, m_r[j][...], v_r[j][...])

    vm = pl.BlockSpec(memory_space=pltpu.VMEM)
    shapes = tuple(jax.ShapeDtypeStruct(w.shape, f32) for w in ws)
    n_out = len(taken) + 3 * n + 1
    return pl.pallas_call(
        body, name=name,
        out_shape=tuple(shapes[j] for j in taken) + shapes * 3 + (jax.ShapeDtypeStruct((1, 1), f32),),
        in_specs=[vm] * (1 + 3 * n + len(given)) + [pl.BlockSpec(memory_space=pl.ANY)] * len(after),
        out_specs=tuple([vm] * n_out),
        compiler_params=pltpu.CompilerParams(vmem_limit_bytes=V7X_VMEM_LIMIT),
    )(packed, *ws, *ms, *vs, *given, *after)


SMALL_FFN = (("ln1_g", D), ("ln1_b", D), ("ln2_g", D), ("ln2_b", D), ("conv_b", D_FF), ("conv_w", 3 * D_FF), ("loss", 128))
SMALL_FFN_AT = 520
SMALL_ROWS = 704


def _small_rows():
    rows, at = {"w_pool": 0, "pool_scale": GROUPS * DH}, SMALL_FFN_AT
    for k, size in SMALL_FFN:
        rows[k] = at
        at += size // 128
    return rows


def kernel(x, w_in, w_pool, pool_scale, w_out, ln1_g, ln1_b, w_up, conv_w, conv_b, w_down, ln2_g, ln2_b, loss_target, m_w_in, m_w_pool, m_pool_scale, m_w_out, m_ln1_g, m_ln1_b, m_w_up, m_conv_w, m_conv_b, m_w_down, m_ln2_g, m_ln2_b, v_w_in, v_w_pool, v_pool_scale, v_w_out, v_ln1_g, v_ln1_b, v_w_up, v_conv_w, v_conv_b, v_w_down, v_ln2_g, v_ln2_b):
    me = 4 * lax.axis_index("x") + 2 * lax.axis_index("y") + lax.axis_index("c")
    x2, tgt = x[0], loss_target[0]

    cos, sin = _rope_tables()
    dmat, qd, kd, cdec = _decay_tables(RET_TILE)

    qkv, g, oret, states, cat, pooled, xhat1, rstd1, x1b, xb, g_in, g_out, g_up, g_down, g_cw = _mix_forward(
        x2, w_in[0].T, w_out[0], cos, sin, dmat, qd, kd, cdec, w_pool[0], pool_scale, ln1_g, ln1_b,
        gather_bf16=[w_up[0].T, w_down[0]], gather=[jnp.transpose(conv_w, (1, 0, 2))])
    w_in_t = g_in.reshape(IN_W, D)
    w_out_f = g_out.reshape(D, D)
    w_up_t = g_up.reshape(2 * D_FF, D)
    w_down_f = g_down.reshape(D_FF, D)
    conv_w_f = jnp.transpose(g_cw[:, :, 0, :], (1, 0, 2)).reshape(3, D_FF)
    dz1, dz2b, du, f, loss8, d_ln2_g, d_ln2_b, d_ln1_g, d_ln1_b, d_conv_b, d_conv_w = _ffn_forward_backward(
        xhat1, rstd1, ln1_g, ln1_b, w_up_t, conv_w_f, conv_b, w_down_f, ln2_g, ln2_b, tgt)
    small_ffn = [d_ln1_g, d_ln1_b, d_ln2_g, d_ln2_b, d_conv_b, d_conv_w, loss8]

    (dw_down,) = _weight_grad(f, dz2b, "grad_w_down", tm=D_FF // 2)
    dw_up_t, own_down, arr_down = _weight_grad(du, x1b, "grad_w_up", tm=D_FF // 2,
                                               reduce=[dw_down.reshape(N_DEV, ROWS_DOWN, D)])
    own_up, oth_up = _pair_reduce([dw_up_t.reshape(N_DEV, ROWS_UP, D)], "pair_reduce_up")
    up_sems, up_src, up_land, up_started = _exchange_start(oth_up, "exchange_up_start", CHIP_BARRIER_SPLIT)
    dproj, grad_x, small, dw_out = _mix_backward(
        dz1, w_out_f, qkv, g, oret, states, pooled, cat, cos, sin, dmat, qd, kd, cdec, w_pool[0], pool_scale, w_in_t,
        small_ffn, after=up_started)
    dw_in_t, own_out, own_small, arr_out, arr_small = _weight_grad(
        dproj, xb, "grad_w_in", tm=IN_W // 4,
        reduce=[dw_out.reshape(N_DEV, ROWS_OUT, D), small.reshape(N_DEV, SMALL_ROWS // N_DEV, 128)])
    arr_up = _exchange_wait(up_sems, up_src, up_land, [dw_in_t], "exchange_up_wait")
    own_in, oth_in, gs_small = _pair_reduce([dw_in_t.reshape(N_DEV, ROWS_IN, D)], "pair_reduce_in",
                                            gather_sum=(own_small, arr_small))
    in_sems, in_src, in_land, started = _exchange_start(oth_in, "exchange_in_start", CHIP_BARRIER)

    names = ["w_in", "w_pool", "pool_scale", "w_out", "ln1_g", "ln1_b", "w_up", "conv_w", "conv_b", "w_down",
             "ln2_g", "ln2_b"]
    w_d = dict(w_in=w_in, w_pool=w_pool, pool_scale=pool_scale, w_out=w_out, ln1_g=ln1_g, ln1_b=ln1_b, w_up=w_up,
               conv_w=conv_w, conv_b=conv_b, w_down=w_down, ln2_g=ln2_g, ln2_b=ln2_b)
    m_d = dict(w_in=m_w_in, w_pool=m_w_pool, pool_scale=m_pool_scale, w_out=m_w_out, ln1_g=m_ln1_g, ln1_b=m_ln1_b,
               w_up=m_w_up, conv_w=m_conv_w, conv_b=m_conv_b, w_down=m_w_down, ln2_g=m_ln2_g, ln2_b=m_ln2_b)
    v_d = dict(w_in=v_w_in, w_pool=v_w_pool, pool_scale=v_pool_scale, w_out=v_w_out, ln1_g=v_ln1_g, ln1_b=v_ln1_b,
               w_up=v_w_up, conv_w=v_conv_w, conv_b=v_conv_b, w_down=v_w_down, ln2_g=v_ln2_g, ln2_b=v_ln2_b)
    g_d, delta, new_m, new_v = {}, {}, {}, {}

    def big_adamw(k, own, arr, transposed, steps, after=()):
        lay = (lambda a: a[0].T) if transposed else (lambda a: a[0])
        back = (lambda a: a.T[None]) if transposed else (lambda a: a[None])
        res = _sum_adamw(own, arr, lay(w_d[k]), lay(m_d[k]), lay(v_d[k]), "adamw_" + k, steps, after)
        g_d[k], delta[k], new_m[k], new_v[k] = (back(r) for r in res)
        return res[3]

    done = [big_adamw("w_up", own_up, arr_up, True, 4, after=(started,)),
            big_adamw("w_down", own_down, arr_down, False, 2, after=(started,)),
            big_adamw("w_out", own_out, arr_out, False, 2, after=(started,))]

    gs_small, rows = gs_small.reshape(SMALL_ROWS, 128), _small_rows()
    g_conv_w = gs_small[rows["conv_w"]:rows["conv_w"] + 3 * D_FF // 128].reshape(3, D_FF)
    g_d["conv_w"] = lax.dynamic_slice(g_conv_w, (0, me * (D_FF // N_DEV)), (3, D_FF // N_DEV))[None]
    lay = lambda k, a: jnp.transpose(a, (1, 0, 2)) if k == "conv_w" else a.reshape(-1, a.shape[-1])
    back = lambda k, a: jnp.transpose(a, (1, 0, 2)) if k == "conv_w" else a.reshape(w_d[k].shape)
    group = [k for k in names if k not in ("w_in", "w_out", "w_up", "w_down")]
    packed = [k for k in group if k != "conv_w"]
    res = _adamw([lay(k, w_d[k]) for k in group], [lay(k, g_d[k]) if k == "conv_w" else rows[k] for k in group],
                 [lay(k, m_d[k]) for k in group], [lay(k, v_d[k]) for k in group], gs_small, rows["loss"],
                 "adamw_small", after=(started,))
    for j, k in enumerate(packed):
        g_d[k] = back(k, res[j])
    for j, k in enumerate(group):
        delta[k], new_m[k], new_v[k] = (back(k, res[len(packed) + part * len(group) + j]) for part in range(3))

    arr_in = _exchange_wait(in_sems, in_src, in_land, done + [res[0]], "exchange_in_wait")
    big_adamw("w_in", own_in, arr_in, True, 4)

    loss = res[-1].reshape(())
    return (loss, grad_x[None], *[g_d[k] for k in names], *[delta[k] for k in names], *[new_m[k] for k in names],
            *[new_v[k] for k in names])
```

```python
import math

import numpy as np
import jax
import jax.numpy as jnp
from jax import lax
from jax.experimental import pallas as pl
from jax.experimental.pallas import tpu as pltpu

f32 = jnp.float32
bf16 = jnp.bfloat16

N_DEV = 8
T = 4096
D = 1024
CHUNK = 64
MIX_TILE = 512
RET_TILE = 256
HEADS = 4
DH = 128
RW = HEADS * DH
PW = 512
GROUPS = 4
WINDOWS = (2, 4, 8, 16)
IN_W = 4 * RW + PW
D_FF = 2816
LN_EPS = 1e-5
RMS_EPS = 1e-6
ALPHA = 2.0 ** 0.25
K_SCALE = DH ** -0.5

ADAM_LR = 0.001
ADAM_B1 = 0.9
ADAM_B2 = 0.999
ADAM_EPS = 1e-08
ADAM_WD = 0.01
ADAM_STEP = 10

ROWS_IN, ROWS_OUT, ROWS_UP, ROWS_DOWN = IN_W // N_DEV, D // N_DEV, 2 * D_FF // N_DEV, D_FF // N_DEV

V7X_VMEM_LIMIT = 56 * 2 ** 20
HALO = 32

NT = (((1,), (1,)), ((), ()))
TN = (((0,), (0,)), ((), ()))
NN = (((1,), (0,)), ((), ()))


def _dot(a, b, dims=NN):
    return lax.dot_general(a, b, dims, preferred_element_type=f32)


def _const_spec(shape):
    zeros = (0,) * len(shape)
    return pl.BlockSpec(shape, lambda i: zeros, pipeline_mode=pl.Buffered(1))


def _sigmoid(x):
    return 0.5 * jnp.tanh(0.5 * x) + 0.5


def _decay_tables(tt):
    h = np.arange(HEADS, dtype=np.float64)
    log_gamma = np.log(1.0 - 2.0 ** (-5.0 - h)).astype(np.float32).astype(np.float64)[:, None, None]
    idx = np.arange(tt, dtype=np.float64)
    visible = (idx[None, :] // CHUNK) <= (idx[:, None] // CHUNK)
    mask = np.where(visible[None], np.exp(log_gamma * np.abs(idx[:, None] - idx[None, :])[None]), 0.0)
    qd = np.broadcast_to(np.exp(log_gamma * (idx[None, :, None] + 1.0)), (HEADS, tt, DH))
    kd = np.broadcast_to(np.exp(log_gamma * (tt - 1.0 - idx[None, :, None])), (HEADS, tt, DH))
    cd = np.exp(log_gamma[:, 0, 0] * tt)
    return (jnp.asarray(mask, f32), jnp.asarray(qd, f32), jnp.asarray(kd, f32), [float(c) for c in cd])


def _rope_tables():
    inv_freq = (10000.0 ** (-np.arange(0, DH, 2, dtype=np.float64) / DH)).astype(np.float32)
    ang = (np.arange(T, dtype=np.float32)[:, None] * inv_freq[None, :]).astype(np.float64)
    cos, sin = np.cos(ang), np.sin(ang)
    return (jnp.asarray(np.concatenate([cos, cos], axis=1), f32), jnp.asarray(np.concatenate([-sin, sin], axis=1), f32))


def _swap_halves(t):
    return pltpu.roll(t, DH // 2, axis=1)


def _mix_forward(x, w_in_shard, w_out_shard, cos, sin, dmat, qd, kd, cdec, w_pool, pool_scale, ln1_g, ln1_b,
                 gather_bf16, gather, tt=MIX_TILE):
    n_tiles = T // tt
    to_bf16 = [w_in_shard, w_out_shard] + list(gather_bf16)
    n_c, n_g = len(to_bf16), len(gather_bf16) + len(gather)

    def body(x_ref, cos_ref, sin_ref, dmat_ref, qd_ref, kd_ref, wpool_ref, pscale_ref, g1_ref, b1_ref, *rest):
        f32_in, plain_in, rest = rest[:n_c], rest[n_c:2 + n_g], rest[2 + n_g:]
        qkv_ref, g_ref, oret_ref, states_ref, cat_ref, pooled_ref, xhat_ref, rstd_ref, x1b_ref, xb_ref = rest[:10]
        fout, gout = rest[10:12], rest[12:12 + n_g]
        state_s, pext_s, tmp_s, wint_s, wout_s, load_sems, stage_sems, *rest = rest[12 + n_g:]
        stage_s, cast_s, sems = rest[:n_c], rest[n_c:2 * n_c], rest[2 * n_c:]
        fin, gin, fsems, gsems = cast_s[:2], tuple(cast_s[2:]) + tuple(plain_in), sems[:3], sems[3:]
        i = pl.program_id(0)

        @pl.when(i == 0)
        def _():
            stage = [pltpu.make_async_copy(src, dst, stage_sems.at[j]) for j, (src, dst) in enumerate(zip(f32_in, stage_s))]
            for cp in stage:
                cp.start()
            state_s[...] = jnp.zeros_like(state_s)
            pext_s[:, pl.ds(0, HALO), :] = jnp.zeros((GROUPS, HALO, DH), f32)

            def cast(js):
                for j in js:
                    stage[j].wait()
                    cast_s[j][...] = stage_s[j][...].astype(bf16)

            _barrier(_gather_peers())
            cast(range(2))
            _gather_start(fin, fout, *fsems)
            cast(range(2, n_c))
            _gather_forward(fin, fout, *fsems)
            _gather_start(gin, gout, *gsems)
            _gather_finish(fin, fout, *fsems)
            loads = [pltpu.make_async_copy(src.at[s], dst.at[pl.ds(s * src.shape[1], src.shape[1]), :],
                                           load_sems.at[j, s])
                     for j, (src, dst) in enumerate(((fout[0], wint_s), (fout[1], wout_s))) for s in range(N_DEV)]
            for ld in loads:
                ld.start()
            for ld in loads:
                ld.wait()

        @pl.when(i == n_tiles - 3)
        def _():
            _gather_forward(gin, gout, *gsems)

        xb = x_ref[...].astype(bf16)
        xb_ref[...] = xb
        cos_t, sin_t = cos_ref[...], sin_ref[...]
        for part in range(2):
            pr = _dot(xb, wint_s[pl.ds(part * RW, RW), :], NT)
            for h in range(HEADS):
                t = pr[:, h * DH:(h + 1) * DH]
                r = t * cos_t + _swap_halves(t) * sin_t
                if part == 1:
                    r = r * K_SCALE
                qkv_ref[:, part * RW + h * DH: part * RW + (h + 1) * DH] = r.astype(bf16)
        qkv_ref[:, 2 * RW:3 * RW] = _dot(xb, wint_s[pl.ds(2 * RW, RW), :], NT).astype(bf16)
        g_ref[...] = _dot(xb, wint_s[pl.ds(3 * RW, RW), :], NT)
        p = _dot(xb, wint_s[pl.ds(4 * RW, PW), :], NT)
        for gi in range(GROUPS):
            pext_s[gi, pl.ds(HALO, tt), :] = p[:, gi * DH:(gi + 1) * DH]

        for sub in range(tt // RET_TILE):
            rows = pl.ds(sub * RET_TILE, RET_TILE)
            for h in range(HEADS):
                q = qkv_ref[rows, h * DH:(h + 1) * DH]
                k = qkv_ref[rows, RW + h * DH: RW + (h + 1) * DH]
                v = qkv_ref[rows, 2 * RW + h * DH: 2 * RW + (h + 1) * DH]
                s = _dot(q, k, NT) * dmat_ref[h]
                st = state_s[h]
                stb = st.astype(bf16)
                states_ref[sub, h] = stb
                oret_ref[rows, h * DH:(h + 1) * DH] = (_dot(s.astype(bf16), v)
                                                      + _dot((q.astype(f32) * qd_ref[h]).astype(bf16), stb))
                state_s[h] = st * cdec[h] + _dot((k.astype(f32) * kd_ref[h]).astype(bf16), v, TN)

        for h in range(HEADS):
            sl = slice(h * DH, (h + 1) * DH)
            o = oret_ref[:, sl]
            r = lax.rsqrt(jnp.mean(o * o, axis=-1, keepdims=True) + RMS_EPS)
            gg = g_ref[:, sl]
            cat_ref[:, sl] = (o * r * (gg * _sigmoid(gg))).astype(bf16)

        pos1 = (i * tt + lax.broadcasted_iota(jnp.int32, (tt, 1), 0) + 1).astype(f32)
        for gi, w in enumerate(WINDOWS):
            sl = slice(gi * DH, (gi + 1) * DH)
            stages = int(math.log2(w))
            src = pext_s
            for s in range(stages):
                lo = HALO - 8 * (stages - 1 - s)
                n = tt + HALO - lo
                shift = 2 ** s
                val = src[gi, pl.ds(lo, n), :] + src[gi, pl.ds(lo - shift, n), :]
                if s == stages - 1:
                    wsum = val
                else:
                    tmp_s[gi, pl.ds(lo, n), :] = val
                    src = tmp_s
            p_g = pext_s[gi, pl.ds(HALO, tt), :]
            pooled = (wsum / jnp.minimum(pos1, float(w)) - p_g).astype(bf16)
            pooled_ref[:, sl] = pooled
            y = _dot(pooled, wpool_ref[gi].astype(bf16)) * pscale_ref[:, sl]
            cat_ref[:, RW + gi * DH: RW + (gi + 1) * DH] = y.astype(bf16)
        pext_s[:, pl.ds(0, HALO), :] = pext_s[:, pl.ds(tt, HALO), :]

        z = ALPHA * x_ref[...] + _dot(cat_ref[...], wout_s[...])
        mu = jnp.mean(z, axis=-1, keepdims=True)
        zc = z - mu
        rstd = lax.rsqrt(jnp.mean(zc * zc, axis=-1, keepdims=True) + LN_EPS)
        xhat = zc * rstd
        xhat_ref[...] = xhat
        rstd_ref[...] = rstd
        x1b_ref[...] = (xhat * g1_ref[...] + b1_ref[...]).astype(bf16)

        @pl.when(i == n_tiles - 1)
        def _():
            _gather_finish(gin, gout, *gsems)

    tile = lambda w: pl.BlockSpec((tt, w), lambda i: (i, 0))
    hbm = pl.BlockSpec(memory_space=pltpu.HBM)
    out_shape = (
        jax.ShapeDtypeStruct((T, 3 * RW), bf16),
        jax.ShapeDtypeStruct((T, RW), f32),
        jax.ShapeDtypeStruct((T, RW), f32),
        jax.ShapeDtypeStruct((T // RET_TILE, HEADS, DH, DH), bf16),
        jax.ShapeDtypeStruct((T, D), bf16),
        jax.ShapeDtypeStruct((T, PW), bf16),
        jax.ShapeDtypeStruct((T, D), f32),
        jax.ShapeDtypeStruct((T, 1), f32),
        jax.ShapeDtypeStruct((T, D), bf16),
        jax.ShapeDtypeStruct((T, D), bf16),
    ) + tuple(jax.ShapeDtypeStruct((N_DEV,) + b.shape, bf16) for b in to_bf16
              ) + tuple(jax.ShapeDtypeStruct((N_DEV,) + b.shape, b.dtype) for b in gather)
    return pl.pallas_call(
        body, name="mix_forward", grid=(n_tiles,), out_shape=out_shape,
        in_specs=[tile(D), tile(DH), tile(DH),
                  _const_spec((HEADS, RET_TILE, RET_TILE)), _const_spec((HEADS, RET_TILE, DH)),
                  _const_spec((HEADS, RET_TILE, DH)),
                  _const_spec((GROUPS, DH, DH)), _const_spec((1, PW)),
                  _const_spec((1, D)), _const_spec((1, D))] + [hbm] * (2 + n_g),
        out_specs=(tile(3 * RW), tile(RW), tile(RW),
                   pl.BlockSpec((tt // RET_TILE, HEADS, DH, DH), lambda i: (i, 0, 0, 0)),
                   tile(D), tile(PW), tile(D), tile(1), tile(D), tile(D)) + (hbm,) * (2 + n_g),
        scratch_shapes=[pltpu.VMEM((HEADS, DH, DH), f32), pltpu.VMEM((GROUPS, tt + HALO, DH), f32),
                        pltpu.VMEM((GROUPS, tt + HALO, DH), f32), pltpu.VMEM((IN_W, D), bf16), pltpu.VMEM((D, D), bf16),
                        pltpu.SemaphoreType.DMA((2, N_DEV)), pltpu.SemaphoreType.DMA((n_c,))]
        + [pltpu.VMEM(b.shape, f32) for b in to_bf16] + [pltpu.VMEM(b.shape, bf16) for b in to_bf16]
        + _gather_sems(2) + _gather_sems(n_g),
        compiler_params=pltpu.CompilerParams(dimension_semantics=("arbitrary",), vmem_limit_bytes=V7X_VMEM_LIMIT,
                                             collective_id=GATHER_BARRIER),
    )(x, cos, sin, dmat, qd, kd, w_pool, pool_scale, ln1_g, ln1_b, *to_bf16, *gather)


def _ffn_forward_backward(xhat1, rstd1, ln1_g, ln1_b, w_up_t, conv_w, conv_b, w_down, ln2_g, ln2_b, target,
                          tt=256):
    n_tiles = T // tt
    FH = 16
    hb = tt // FH

    def body(xhat_ref, halo_ref, rstd_ref, g1_ref, b1_ref, wupt_ref, cw_ref, cb_ref, wdown_ref, g2_ref, b2_ref, tgt_ref,
             dz1_ref, dz2b_ref, du_ref, f_ref, loss_ref, dg2_ref, db2_ref, dg1_ref, db1_ref, dcb_ref, dcw_ref,
             gext_s, val_s, dhext_s):
        i = pl.program_id(0)
        tile_idx = n_tiles - 1 - i

        def rd(ref, off):
            return jnp.concatenate([ref[k, pl.ds(off, tt), :] for k in range(D_FF // 128)], axis=1)

        def wr(ref, val):
            for k in range(D_FF // 128):
                ref[k, pl.ds(0, val.shape[0]), :] = val[:, k * 128:(k + 1) * 128]

        @pl.when(i == 0)
        def _():
            for r in (loss_ref, dg2_ref, db2_ref, dg1_ref, db1_ref, dcb_ref, dcw_ref):
                r[...] = jnp.zeros_like(r)
            dhext_s[:, pl.ds(tt, 8), :] = jnp.zeros((D_FF // 128, 8, 128), f32)

        g1, b1 = g1_ref[...], b1_ref[...]
        xhat = xhat_ref[...]
        x1 = xhat * g1 + b1
        x1b = x1.astype(bf16)
        x1h = ((halo_ref[...] * g1 + b1) * jnp.where(tile_idx == 0, 0.0, 1.0)).astype(bf16)
        x1ext = jnp.concatenate([x1h, x1b], axis=0)

        val = _dot(x1b, wupt_ref[pl.ds(0, D_FF), :], NT)
        gate_ext = _dot(x1ext, wupt_ref[pl.ds(D_FF, D_FF), :], NT)
        wr(gext_s, gate_ext)
        hh = (cb_ref[...] + cw_ref[0:1, :] * rd(gext_s, FH - 2) + cw_ref[1:2, :] * rd(gext_s, FH - 1)
              + cw_ref[2:3, :] * gate_ext[FH:])
        sg = _sigmoid(hh)
        act = hh * sg
        wr(dhext_s, act)
        val_s[...] = val * (sg + act * (1.0 - sg))
        fb = (act * val).astype(bf16)
        f_ref[...] = fb

        z = ALPHA * x1 + _dot(fb, wdown_ref[...])
        mu = jnp.mean(z, axis=-1, keepdims=True)
        zc = z - mu
        rstd2 = lax.rsqrt(jnp.mean(zc * zc, axis=-1, keepdims=True) + LN_EPS)
        xh2 = zc * rstd2
        diff = xh2 * g2_ref[...] + b2_ref[...] - tgt_ref[...]
        loss_ref[...] += 0.5 * jnp.sum(diff * diff) / D
        dy = diff * (1.0 / D)
        dg2_ref[...] += jnp.sum(dy * xh2, axis=0, keepdims=True)
        db2_ref[...] += jnp.sum(dy, axis=0, keepdims=True)
        dyg = dy * g2_ref[...]
        dz2 = rstd2 * (dyg - jnp.mean(dyg, axis=-1, keepdims=True) - xh2 * jnp.mean(dyg * xh2, axis=-1, keepdims=True))
        dz2b = dz2.astype(bf16)
        dz2b_ref[...] = dz2b

        df = _dot(dz2b, wdown_ref[...], NT)
        dval = df * rd(dhext_s, 0)
        dh = df * val_s[...]
        wr(dhext_s, dh)
        dh1, dh2, g0 = rd(dhext_s, 1), rd(dhext_s, 2), rd(gext_s, FH)
        dcb_ref[...] += jnp.sum(dh, axis=0, keepdims=True)
        dcw_ref[0:1, :] += jnp.sum(dh2 * g0, axis=0, keepdims=True)
        dcw_ref[1:2, :] += jnp.sum(dh1 * g0, axis=0, keepdims=True)
        dcw_ref[2:3, :] += jnp.sum(dh * g0, axis=0, keepdims=True)
        dgate = cw_ref[2:3, :] * dh + cw_ref[1:2, :] * dh1 + cw_ref[0:1, :] * dh2
        dvalb, dgateb = dval.astype(bf16), dgate.astype(bf16)
        du_ref[:, :D_FF] = dvalb
        du_ref[:, D_FF:] = dgateb
        dx1 = ALPHA * dz2 + _dot(dvalb, wupt_ref[pl.ds(0, D_FF), :]) + _dot(dgateb, wupt_ref[pl.ds(D_FF, D_FF), :])
        dhext_s[:, pl.ds(tt, 8), :] = dhext_s[:, pl.ds(0, 8), :]

        dg1_ref[...] += jnp.sum(dx1 * xhat, axis=0, keepdims=True)
        db1_ref[...] += jnp.sum(dx1, axis=0, keepdims=True)
        dxg = dx1 * g1
        dz1_ref[...] = rstd_ref[...] * (dxg - jnp.mean(dxg, axis=-1, keepdims=True)
                                        - xhat * jnp.mean(dxg * xhat, axis=-1, keepdims=True))

    rtile = lambda w: pl.BlockSpec((tt, w), lambda i: (n_tiles - 1 - i, 0))
    acc = lambda shape: pl.BlockSpec(shape, lambda i: (0, 0))
    out_shape = (
        jax.ShapeDtypeStruct((T, D), f32),
        jax.ShapeDtypeStruct((T, D), bf16),
        jax.ShapeDtypeStruct((T, 2 * D_FF), bf16),
        jax.ShapeDtypeStruct((T, D_FF), bf16),
        jax.ShapeDtypeStruct((8, 128), f32),
        jax.ShapeDtypeStruct((1, D), f32), jax.ShapeDtypeStruct((1, D), f32),
        jax.ShapeDtypeStruct((1, D), f32), jax.ShapeDtypeStruct((1, D), f32),
        jax.ShapeDtypeStruct((1, D_FF), f32), jax.ShapeDtypeStruct((3, D_FF), f32),
    )
    return pl.pallas_call(
        body, name="ffn_forward_backward", grid=(n_tiles,), out_shape=out_shape,
        in_specs=[rtile(D),
                  pl.BlockSpec((FH, D), lambda i: (jnp.maximum((n_tiles - 1 - i) * hb - 1, 0), 0)),
                  rtile(1), _const_spec((1, D)), _const_spec((1, D)), _const_spec((2 * D_FF, D)),
                  _const_spec((3, D_FF)), _const_spec((1, D_FF)), _const_spec((D_FF, D)),
                  _const_spec((1, D)), _const_spec((1, D)), rtile(D)],
        out_specs=(rtile(D), rtile(D), rtile(2 * D_FF), rtile(D_FF), acc((8, 128)),
                   acc((1, D)), acc((1, D)), acc((1, D)), acc((1, D)), acc((1, D_FF)), acc((3, D_FF))),
        scratch_shapes=[pltpu.VMEM((D_FF // 128, tt + FH, 128), f32), pltpu.VMEM((tt, D_FF), f32),
                        pltpu.VMEM((D_FF // 128, tt + 8, 128), f32)],
        compiler_params=pltpu.CompilerParams(dimension_semantics=("arbitrary",), vmem_limit_bytes=V7X_VMEM_LIMIT),
    )(xhat1, xhat1, rstd1, ln1_g, ln1_b, w_up_t, conv_w, conv_b, w_down, ln2_g, ln2_b, target)


def _mix_backward(dz1, w_out, qkv, g, oret, states, pooled, cat, cos, sin, dmat, qd, kd, cdec, w_pool, pool_scale, w_in_t,
                  small_ffn, after, tt=MIX_TILE):
    n_tiles = T // tt

    def body(dz1_ref, wout_ref, qkv_ref, g_ref, oret_ref, states_ref, pooled_ref, cat_ref, cos_ref, sin_ref, dmat_ref,
             qd_ref, kd_ref, wpool_ref, pscale_ref, wint_ref, *rest):
        ffn_refs, rest = rest[:len(SMALL_FFN)], rest[len(SMALL_FFN):]
        after_ref, dproj_ref, gx_ref, small_ref, dwout_ref, dstate_s, dout_s, eext_s, tmp_s, dwout_s, dpscale_s = rest
        i = pl.program_id(0)
        tile_idx = n_tiles - 1 - i

        @pl.when(i == 0)
        def _():
            dstate_s[...] = jnp.zeros_like(dstate_s)
            small_ref[...] = jnp.zeros_like(small_ref)
            dpscale_s[...] = jnp.zeros_like(dpscale_s)
            dwout_s[...] = jnp.zeros_like(dwout_s)
            eext_s[:, pl.ds(tt, HALO), :] = jnp.zeros((GROUPS, HALO, DH), f32)

        dz1 = dz1_ref[...]
        dz1b = dz1.astype(bf16)
        dcat = _dot(dz1b, wout_ref[...], NT)
        dwout_s[...] += _dot(cat_ref[...], dz1b, TN)

        pos1 = (tile_idx * tt + lax.broadcasted_iota(jnp.int32, (tt, 1), 0) + 1).astype(f32)
        for gi, w in enumerate(WINDOWS):
            sl = slice(gi * DH, (gi + 1) * DH)
            dpo = dcat[:, RW + gi * DH: RW + (gi + 1) * DH]
            pooled_g = pooled_ref[:, sl]
            wpool_g = wpool_ref[gi].astype(bf16)
            ylin = _dot(pooled_g, wpool_g)
            dpscale_s[:, sl] += jnp.sum(dpo * ylin, axis=0, keepdims=True)
            dpw = (dpo * pscale_ref[:, sl]).astype(bf16)
            small_ref[pl.ds(gi * DH, DH), :] += _dot(pooled_g, dpw, TN)
            dpooled = _dot(dpw, wpool_g, NT)
            eext_s[gi, pl.ds(0, tt), :] = dpooled / jnp.minimum(pos1, float(w))
            stages = int(math.log2(w))
            src = eext_s
            for s in range(stages):
                n = tt + 8 * (stages - 1 - s)
                shift = 2 ** s
                val = src[gi, pl.ds(0, n), :] + src[gi, pl.ds(shift, n), :]
                if s == stages - 1:
                    wsum = val
                else:
                    tmp_s[gi, pl.ds(0, n), :] = val
                    src = tmp_s
            dproj_ref[:, 4 * RW + gi * DH: 4 * RW + (gi + 1) * DH] = (wsum - dpooled).astype(bf16)
        eext_s[:, pl.ds(tt, HALO), :] = eext_s[:, pl.ds(0, HALO), :]

        for h in range(HEADS):
            sl = slice(h * DH, (h + 1) * DH)
            dr = dcat[:, sl]
            o = oret_ref[:, sl]
            r = lax.rsqrt(jnp.mean(o * o, axis=-1, keepdims=True) + RMS_EPS)
            rn = o * r
            gg = g_ref[:, sl]
            sg = _sigmoid(gg)
            dproj_ref[:, 3 * RW + h * DH: 3 * RW + (h + 1) * DH] = (dr * rn * (sg * (1.0 + gg * (1.0 - sg)))).astype(bf16)
            drn = dr * (gg * sg)
            dout_s[:, sl] = (r * (drn - rn * jnp.mean(drn * rn, axis=-1, keepdims=True))).astype(bf16)

        for sub in reversed(range(tt // RET_TILE)):
            rows = pl.ds(sub * RET_TILE, RET_TILE)
            cos_t, sin_t = cos_ref[rows, :], sin_ref[rows, :]
            for h in range(HEADS):
                q = qkv_ref[rows, h * DH:(h + 1) * DH]
                k = qkv_ref[rows, RW + h * DH: RW + (h + 1) * DH]
                v = qkv_ref[rows, 2 * RW + h * DH: 2 * RW + (h + 1) * DH]
                do = dout_s[rows, h * DH:(h + 1) * DH]
                stb = states_ref[sub, h]
                dst = dstate_s[h]
                dstb = dst.astype(bf16)
                sb = (_dot(q, k, NT) * dmat_ref[h]).astype(bf16)
                dsb = (_dot(do, v, NT) * dmat_ref[h]).astype(bf16)
                dq = _dot(dsb, k) + _dot(do, stb, NT) * qd_ref[h]
                dk = _dot(dsb, q, TN) + _dot(v, dstb, NT) * kd_ref[h]
                dv = _dot(sb, do, TN) + _dot((k.astype(f32) * kd_ref[h]).astype(bf16), dstb)
                dstate_s[h] = dst * cdec[h] + _dot((q.astype(f32) * qd_ref[h]).astype(bf16), do, TN)
                dproj_ref[rows, h * DH:(h + 1) * DH] = (dq * cos_t - _swap_halves(dq) * sin_t).astype(bf16)
                dproj_ref[rows, RW + h * DH: RW + (h + 1) * DH] = (
                    (dk * cos_t - _swap_halves(dk) * sin_t) * K_SCALE).astype(bf16)
                dproj_ref[rows, 2 * RW + h * DH: 2 * RW + (h + 1) * DH] = dv.astype(bf16)

        gx_ref[...] = ALPHA * dz1 + _dot(dproj_ref[...], wint_ref[...])

        @pl.when(i == n_tiles - 1)
        def _():
            dwout_ref[...] = dwout_s[...].astype(bf16)
            at = GROUPS * DH
            for ref, size in [(dpscale_s, PW)] + [(ref, size) for ref, (_, size) in zip(ffn_refs, SMALL_FFN)]:
                for j in range(size // 128):
                    r, k = divmod(j, ref.shape[1] // 128)
                    small_ref[at + j: at + j + 1, :] = ref[r:r + 1, k * 128:(k + 1) * 128]
                at = SMALL_FFN_AT if ref is dpscale_s else at + size // 128

    rtile = lambda w: pl.BlockSpec((tt, w), lambda i: (n_tiles - 1 - i, 0))
    out_shape = (
        jax.ShapeDtypeStruct((T, IN_W), bf16),
        jax.ShapeDtypeStruct((T, D), f32),
        jax.ShapeDtypeStruct((SMALL_ROWS, 128), f32),
        jax.ShapeDtypeStruct((D, D), bf16),
    )
    return pl.pallas_call(
        body, name="mix_backward", grid=(n_tiles,), out_shape=out_shape,
        in_specs=[rtile(D), _const_spec((D, D)), rtile(3 * RW), rtile(RW), rtile(RW),
                  pl.BlockSpec((tt // RET_TILE, HEADS, DH, DH), lambda i: (n_tiles - 1 - i, 0, 0, 0)),
                  rtile(PW), rtile(D), rtile(DH), rtile(DH),
                  _const_spec((HEADS, RET_TILE, RET_TILE)), _const_spec((HEADS, RET_TILE, DH)),
                  _const_spec((HEADS, RET_TILE, DH)),
                  _const_spec((GROUPS, DH, DH)), _const_spec((1, PW)), _const_spec((IN_W, D)),
                  *[_const_spec(a.shape) for a in small_ffn], pl.BlockSpec(memory_space=pl.ANY)],
        out_specs=(rtile(IN_W), rtile(D), pl.BlockSpec((SMALL_ROWS, 128), lambda i: (0, 0)),
                   pl.BlockSpec((D, D), lambda i: (0, 0), pipeline_mode=pl.Buffered(1))),
        scratch_shapes=[pltpu.VMEM((HEADS, DH, DH), f32), pltpu.VMEM((tt, RW), bf16),
                        pltpu.VMEM((GROUPS, tt + HALO, DH), f32), pltpu.VMEM((GROUPS, tt + HALO, DH), f32),
                        pltpu.VMEM((D, D), f32), pltpu.VMEM((1, PW), f32)],
        compiler_params=pltpu.CompilerParams(dimension_semantics=("arbitrary",), vmem_limit_bytes=V7X_VMEM_LIMIT),
    )(dz1, w_out, qkv, g, oret, states, pooled, cat, cos, sin, dmat, qd, kd, w_pool, pool_scale, w_in_t, *small_ffn,
      after)


def _weight_grad(a, b, name, tm, reduce=()):
    m = a.shape[1]
    n_m, n_r = m // tm, len(reduce)
    assert not n_r or n_m >= 2
    assert b.dtype == bf16
    blocked = tm <= 1024
    n_b = 4
    tb = T // n_b

    def body(a_ref, b_ref, *rest):
        ins, o_ref, own, arrived = rest[:n_r], rest[n_r], rest[n_r + 1:2 * n_r + 1], rest[2 * n_r + 1:3 * n_r + 1]
        landing, mine = rest[3 * n_r + 1:4 * n_r + 1], rest[4 * n_r + 1:5 * n_r + 1]
        sems = rest[5 * n_r + 1:5 * n_r + 6] if n_r else ()
        i = pl.program_id(0)

        if blocked:
            b_s, b_sems, acc_s = rest[-3:]

            def b_load(kb):
                return pltpu.make_async_copy(b_ref.at[pl.ds(kb * tb, tb)], b_s.at[pl.ds(kb * tb, tb)], b_sems.at[kb])

            @pl.when(i == 0)
            def _():
                for kb in range(n_b):
                    b_load(kb).start()

        if n_r:
            pair_send, pair_recv, local_sems, chip_send, chip_recv = sems
            me = _me()
            x, y, c = me
            sibling = (x, y, 1 - c)

            def pair_copy(k, j):
                return _remote(ins[j].at[_slot(*_chip(me, k), 1 - c)], landing[j].at[k], pair_send.at[k, j],
                               pair_recv.at[k, j], sibling)

            def load(k, j):
                return pltpu.make_async_copy(ins[j].at[_slot(*_chip(me, k), c)], mine[j].at[k], local_sems.at[k, j])

            def store(j):
                return pltpu.make_async_copy(mine[j].at[0], own[j], local_sems.at[0, j])

            def chip_copy(k, j):
                return _remote(mine[j].at[k], arrived[j].at[k - 1], chip_send.at[k - 1, j], chip_recv.at[k - 1, j],
                               (*_chip(me, k), c))

            @pl.when(i == 0)
            def _():
                _barrier([sibling] + _chip_peers())
                for k in range(4):
                    for j in range(n_r):
                        pair_copy(k, j).start()
                        load(k, j).start()

            @pl.when(i == 1)
            def _():
                for k in range(4):
                    for j in range(n_r):
                        load(k, j).wait()
                        pair_copy(k, j).wait_recv()
                        mine[j][k] = (mine[j][k].astype(f32) + landing[j][k].astype(f32)).astype(mine[j].dtype)
                        (store(j) if k == 0 else chip_copy(k, j)).start()

        if blocked:
            @pl.when(i == 0)
            def _():
                for kb in range(n_b):
                    b_load(kb).wait()
                    part = _dot(a_ref[pl.ds(kb * tb, tb), :], b_s[pl.ds(kb * tb, tb), :], TN)
                    if kb == 0:
                        acc_s[...] = part
                    elif kb < n_b - 1:
                        acc_s[...] += part
                    else:
                        o_ref[...] = (acc_s[...] + part).astype(bf16)

            @pl.when(i > 0)
            def _():
                o_ref[...] = _dot(a_ref[...], b_s[...], TN).astype(bf16)
        else:
            o_ref[...] = _dot(a_ref[...], b_ref[...], TN).astype(bf16)

        if n_r:
            @pl.when(i == n_m - 1)
            def _():
                for j in range(n_r):
                    store(j).wait()
                    for k in range(1, 4):
                        chip_copy(k, j).wait_recv()
                for j in range(n_r):
                    for k in range(1, 4):
                        chip_copy(k, j).wait_send()
                    for k in range(4):
                        pair_copy(k, j).wait_send()

    hbm = pl.BlockSpec(memory_space=pltpu.HBM)
    return pl.pallas_call(
        body, name=name, grid=(n_m,),
        out_shape=(jax.ShapeDtypeStruct((m, D), bf16),)
        + tuple(jax.ShapeDtypeStruct(p.shape[1:], p.dtype) for p in reduce)
        + tuple(jax.ShapeDtypeStruct((3,) + p.shape[1:], p.dtype) for p in reduce),
        in_specs=[pl.BlockSpec((T, tm), lambda i: (0, i)),
                  pl.BlockSpec(memory_space=pl.ANY) if blocked else _const_spec((T, D))] + [hbm] * n_r,
        out_specs=(pl.BlockSpec((tm, D), lambda i: (i, 0)),) + (hbm,) * (2 * n_r),
        scratch_shapes=[pltpu.VMEM((4,) + p.shape[1:], p.dtype) for p in reduce] * 2
        + ([pltpu.SemaphoreType.DMA((4, n_r))] * 3 + [pltpu.SemaphoreType.DMA((3, n_r))] * 2 if n_r else [])
        + ([pltpu.VMEM((T, D), bf16), pltpu.SemaphoreType.DMA((n_b,)), pltpu.VMEM((tm, D), f32)] if blocked else []),
        compiler_params=pltpu.CompilerParams(dimension_semantics=("arbitrary",), vmem_limit_bytes=V7X_VMEM_LIMIT,
                                             collective_id=REDUCE_BARRIER if n_r else None),
    )(a, b, *reduce)


CHIP_FLIPS = ((1, 0), (0, 1), (1, 1))
PAIR_BARRIER, CHIP_BARRIER, GATHER_BARRIER, CHIP_BARRIER_SPLIT, REDUCE_BARRIER, ALL_BARRIER = 0, 1, 2, 3, 4, 5
ALL_FLIPS = tuple((fx, fy, fc) for fx in (0, 1) for fy in (0, 1) for fc in (0, 1))[1:]


def _flip(me, f):
    return tuple(1 - v if b else v for v, b in zip(me, f))


def _barrier(peers):
    sem = pltpu.get_barrier_semaphore()
    for peer in peers:
        pl.semaphore_signal(sem, inc=1, device_id=peer, device_id_type=pl.DeviceIdType.MESH)
    pl.semaphore_wait(sem, len(peers))


def _me():
    return lax.axis_index("x"), lax.axis_index("y"), lax.axis_index("c")


def _chip(me, k):
    x, y, _ = me
    if k == 0:
        return x, y
    fx, fy = CHIP_FLIPS[k - 1]
    return (1 - x if fx else x), (1 - y if fy else y)


def _slot(x, y, c):
    return 4 * x + 2 * y + c


def _remote(src, dst, send_sem, recv_sem, to):
    return pltpu.make_async_remote_copy(src_ref=src, dst_ref=dst, send_sem=send_sem, recv_sem=recv_sem,
                                        device_id=to, device_id_type=pl.DeviceIdType.MESH)


def _gather_sems(n):
    return [pltpu.SemaphoreType.DMA((7, n)), pltpu.SemaphoreType.DMA((7, n)), pltpu.SemaphoreType.DMA((n,))] if n else []


def _gather_copy(k, j, gin, gout, send_sems, recv_sems, sending):
    x, y, c = _me()
    sibling, x_chip, y_chip, d_chip = (x, y, 1 - c), (1 - x, y), (x, 1 - y), (1 - x, 1 - y)
    south = c == 0
    passed_on = (jnp.where(south, 1 - x, x), jnp.where(south, y, 1 - y), c)
    src, to = gin[j], sibling
    if sending:
        block = {0: (x, y, c), 1: (x, y, c), 2: (x, y, c), 3: passed_on, 4: (*x_chip, c), 5: (*y_chip, c), 6: (*d_chip, c)}[k]
        to = {1: (*x_chip, c), 2: (*y_chip, c), 3: (jnp.where(south, x, 1 - x), jnp.where(south, 1 - y, y), c)}.get(k, sibling)
        if k >= 3:
            src = gout[j].at[_slot(*block)]
    else:
        block = {0: sibling, 1: (*x_chip, c), 2: (*y_chip, c), 3: (*d_chip, c), 4: (*x_chip, 1 - c), 5: (*y_chip, 1 - c),
                 6: (*d_chip, 1 - c)}[k]
    return _remote(src, gout[j].at[_slot(*block)], send_sems.at[k, j], recv_sems.at[k, j], to)


def _gather_do(ks, action, gin, gout, send_sems, recv_sems):
    for k in ks:
        for j in range(len(gin)):
            cp = _gather_copy(k, j, gin, gout, send_sems, recv_sems, action != "wait_recv")
            getattr(cp, action)()


def _gather_peers():
    x, y, c = _me()
    return [(x, y, 1 - c), (1 - x, y, c), (x, 1 - y, c)]


def _gather_start(gin, gout, send_sems, recv_sems, local_sems):
    for j in range(len(gin)):
        pltpu.make_async_copy(gin[j], gout[j].at[_slot(*_me())], local_sems.at[j]).start()
    _gather_do((0, 1, 2), "start", gin, gout, send_sems, recv_sems)


def _gather_forward(gin, gout, send_sems, recv_sems, local_sems):
    _gather_do((1, 2), "wait_recv", gin, gout, send_sems, recv_sems)
    _gather_do((3, 4, 5), "start", gin, gout, send_sems, recv_sems)


def _gather_finish(gin, gout, send_sems, recv_sems, local_sems):
    _gather_do((3,), "wait_recv", gin, gout, send_sems, recv_sems)
    _gather_do((6,), "start", gin, gout, send_sems, recv_sems)
    _gather_do((0, 4, 5, 6), "wait_recv", gin, gout, send_sems, recv_sems)
    _gather_do(range(7), "wait_send", gin, gout, send_sems, recv_sems)
    for j in range(len(gin)):
        pltpu.make_async_copy(gin[j], gout[j].at[_slot(*_me())], local_sems.at[j]).wait()


def _pair_reduce(parts, name, gather_sum=None):
    n = len(parts)
    n_h = 0 if gather_sum is None else 1

    def body(*refs):
        ins, g_terms, refs = refs[:n], refs[n:n + 2 * n_h], refs[n + 2 * n_h:]
        own, others, g_out, refs = refs[:n], refs[n:2 * n], refs[2 * n:2 * n + n_h], refs[2 * n + n_h:]
        landing, mine, (send_sems, recv_sems, local_sems), g_scratch = refs[:n], refs[n:2 * n], refs[2 * n:2 * n + 3], refs[2 * n + 3:]
        me = _me()
        x, y, c = me
        sibling = (x, y, 1 - c)
        _barrier([_flip(me, f) for f in ALL_FLIPS] if n_h else [sibling])
        if n_h:
            piece_s, g_send, g_recv, g_local = g_scratch
            acc = g_terms[0][...].astype(f32)
            for k in range(3):
                acc = acc + g_terms[1][k].astype(f32)
            piece_s[...] = acc

            def g_copy(q):
                return _remote(piece_s, g_out[0].at[_slot(*me)], g_send.at[q, 0], g_recv.at[q, 0], _flip(me, ALL_FLIPS[q]))

            g_mine = pltpu.make_async_copy(piece_s, g_out[0].at[_slot(*me)], g_local.at[0])
            g_mine.start()
            for q in range(len(ALL_FLIPS)):
                g_copy(q).start()
        sends, loads = [], []
        for k in range(4):
            for j in range(n):
                cp = _remote(ins[j].at[_slot(*_chip(me, k), 1 - c)], landing[j].at[k], send_sems.at[k, j],
                             recv_sems.at[k, j], sibling)
                cp.start()
                sends.append(cp)
                ld = pltpu.make_async_copy(ins[j].at[_slot(*_chip(me, k), c)], mine[j].at[k], local_sems.at[k, j])
                ld.start()
                loads.append(ld)
        stores = []
        for k in range(4):
            for j in range(n):
                loads[k * n + j].wait()
                _remote(ins[j].at[0], landing[j].at[k], send_sems.at[k, j], recv_sems.at[k, j], sibling).wait_recv()
                mine[j][k] = (mine[j][k].astype(f32) + landing[j][k].astype(f32)).astype(mine[j].dtype)
                st = pltpu.make_async_copy(mine[j].at[k], own[j] if k == 0 else others[j].at[k - 1], local_sems.at[k, j])
                st.start()
                stores.append(st)
        for cp in sends:
            cp.wait_send()
        for st in stores:
            st.wait()
        if n_h:
            for q in range(len(ALL_FLIPS)):
                g_copy(q).wait_recv()
            for q in range(len(ALL_FLIPS)):
                g_copy(q).wait_send()
            g_mine.wait()

    vm, hbm = pl.BlockSpec(memory_space=pltpu.VMEM), pl.BlockSpec(memory_space=pltpu.HBM)
    g_shape = gather_sum[0].shape if n_h else ()
    return pl.pallas_call(
        body, name=name,
        out_shape=tuple(jax.ShapeDtypeStruct(p.shape[1:], p.dtype) for p in parts)
        + tuple(jax.ShapeDtypeStruct((3,) + p.shape[1:], p.dtype) for p in parts)
        + tuple([jax.ShapeDtypeStruct((N_DEV,) + g_shape, f32)] * n_h),
        in_specs=[hbm] * n + [vm] * (2 * n_h), out_specs=(hbm,) * (2 * n + n_h),
        scratch_shapes=[pltpu.VMEM((4,) + p.shape[1:], p.dtype) for p in parts] * 2
        + [pltpu.SemaphoreType.DMA((4, n)), pltpu.SemaphoreType.DMA((4, n)), pltpu.SemaphoreType.DMA((4, n))]
        + ([pltpu.VMEM(g_shape, f32)] + _gather_sems(1)) * n_h,
        compiler_params=pltpu.CompilerParams(vmem_limit_bytes=V7X_VMEM_LIMIT,
                                             collective_id=ALL_BARRIER if n_h else PAIR_BARRIER),
    )(*parts, *(gather_sum or ()))


def _chip_peers():
    me = _me()
    return [(*_chip(me, k), me[2]) for k in range(1, 4)]


def _split_copies(src_ref, dst_ref, sems):
    me = _me()
    return [_remote(src_ref.at[k - 1], dst_ref.at[k - 1], sems[k - 1], sems[2 + k], (*_chip(me, k), me[2]))
            for k in range(1, 4)]


def _exchange_start(others, name, barrier_id):
    def body(src_ref, land_ref, *rest):
        sems, token_ref = rest[:6], rest[8]
        _barrier(_chip_peers())
        for copy in _split_copies(src_ref, land_ref, sems):
            copy.start()
        token_ref[...] = jnp.zeros_like(token_ref)

    hbm, sem = pl.BlockSpec(memory_space=pltpu.HBM), pl.BlockSpec(memory_space=pltpu.SEMAPHORE)
    thru = pltpu.HBM(others.shape, others.dtype)
    res = pl.pallas_call(
        body, name=name,
        out_shape=(pltpu.SemaphoreType.DMA(()),) * 6 + (thru, thru, jax.ShapeDtypeStruct((8, 128), f32)),
        in_specs=(hbm, hbm), out_specs=(sem,) * 6 + (hbm, hbm, pl.BlockSpec(memory_space=pltpu.VMEM)),
        input_output_aliases={0: 6, 1: 7},
        compiler_params=pltpu.CompilerParams(has_side_effects=pltpu.SideEffectType.DATAFLOW_SIDE_EFFECTING,
                                             collective_id=barrier_id),
    )(pltpu.with_memory_space_constraint(others, pltpu.HBM),
      pltpu.with_memory_space_constraint(lax.empty(others.shape, others.dtype), pltpu.HBM))
    return res[:6], res[6], res[7], res[8]


def _exchange_wait(sems, src_thru, land_thru, after, name):
    n_after = len(after)

    def body(src_ref, land_ref, *rest):
        for copy in _split_copies(src_ref, land_ref, rest[:6]):
            copy.wait_send()
            copy.wait_recv()

    hbm, sem = pl.BlockSpec(memory_space=pltpu.HBM), pl.BlockSpec(memory_space=pltpu.SEMAPHORE)
    thru = pltpu.HBM(src_thru.shape, src_thru.dtype)
    return pl.pallas_call(
        body, name=name, out_shape=(thru, thru),
        in_specs=(hbm, hbm) + (sem,) * 6 + (pl.BlockSpec(memory_space=pl.ANY),) * n_after, out_specs=(hbm, hbm),
        input_output_aliases={0: 0, 1: 1},
        compiler_params=pltpu.CompilerParams(has_side_effects=pltpu.SideEffectType.DATAFLOW_SIDE_EFFECTING),
    )(src_thru, land_thru, *sems, *after)[1]


def _adam_update(w, g, m, v):
    m = ADAM_B1 * m + (1.0 - ADAM_B1) * g
    v = ADAM_B2 * v + (1.0 - ADAM_B2) * (g * g)
    m_hat = m / (1.0 - ADAM_B1 ** ADAM_STEP)
    v_hat = v / (1.0 - ADAM_B2 ** ADAM_STEP)
    return -ADAM_LR * (m_hat / (jnp.sqrt(v_hat) + ADAM_EPS) + ADAM_WD * w), m, v


def _sum_adamw(own, arrived, w, m, v, name, steps, after=()):
    rows = own.shape[0]
    br = rows // steps

    def body(own_ref, arr_ref, w_ref, m_ref, v_ref, *rest):
        g_out, d_out, m_out, v_out = rest[len(after):]
        g = own_ref[...].astype(f32)
        for k in range(3):
            g = g + arr_ref[k].astype(f32)
        g_out[...] = g
        d_out[...], m_out[...], v_out[...] = _adam_update(w_ref[...], g, m_ref[...], v_ref[...])

    blk = pl.BlockSpec((br, D), lambda i: (i, 0))
    return pl.pallas_call(
        body, name=name, grid=(steps,), out_shape=(jax.ShapeDtypeStruct((rows, D), f32),) * 4,
        in_specs=[blk, pl.BlockSpec((3, br, D), lambda i: (0, i, 0)), blk, blk, blk]
        + [pl.BlockSpec(memory_space=pl.ANY)] * len(after), out_specs=(blk,) * 4,
        compiler_params=pltpu.CompilerParams(dimension_semantics=("parallel",), vmem_limit_bytes=V7X_VMEM_LIMIT),
    )(own, arrived, w, m, v, *after)


def _adamw(ws, gs, ms, vs, packed, scalar_row, name, after=()):
    n = len(ws)
    given = [g for g in gs if not isinstance(g, int)]
    taken = [j for j in range(n) if isinstance(gs[j], int)]

    def body(packed_ref, *refs):
        w_r, m_r, v_r = (refs[k * n:(k + 1) * n] for k in range(3))
        given_r, outs = list(refs[3 * n:3 * n + len(given)]), refs[3 * n + len(given) + len(after):]
        g_o, outs = dict(zip(taken, outs[:len(taken)])), outs[len(taken):]
        d_o, m_o, v_o = (outs[k * n:(k + 1) * n] for k in range(3))
        outs[3 * n][...] = packed_ref[scalar_row:scalar_row + 1, 0:1]
        for j in range(n):
            if j in g_o:
                (r, c), at = ws[j].shape, gs[j]
                if c == 128:
                    g = packed_ref[at:at + r, :]
                else:
                    assert r == 1
                    g = jnp.concatenate([packed_ref[at + k:at + k + 1, :] for k in range(c // 128)], axis=1)
                g_o[j][...] = g
            else:
                g = given_r.pop(0)[...]
            d_o[j][...], m_o[j][...], v_o[j][...] = _adam_update(w_r[j][...], g, m_r[j][...], v_r[j][...])

    vm = pl.BlockSpec(memory_space=pltpu.VMEM)
    shapes = tuple(jax.ShapeDtypeStruct(w.shape, f32) for w in ws)
    n_out = len(taken) + 3 * n + 1
    return pl.pallas_call(
        body, name=name,
        out_shape=tuple(shapes[j] for j in taken) + shapes * 3 + (jax.ShapeDtypeStruct((1, 1), f32),),
        in_specs=[vm] * (1 + 3 * n + len(given)) + [pl.BlockSpec(memory_space=pl.ANY)] * len(after),
        out_specs=tuple([vm] * n_out),
        compiler_params=pltpu.CompilerParams(vmem_limit_bytes=V7X_VMEM_LIMIT),
    )(packed, *ws, *ms, *vs, *given, *after)


SMALL_FFN = (("ln1_g", D), ("ln1_b", D), ("ln2_g", D), ("ln2_b", D), ("conv_b", D_FF), ("conv_w", 3 * D_FF), ("loss", 128))
SMALL_FFN_AT = 520
SMALL_ROWS = 704


def _small_rows():
    rows, at = {"w_pool": 0, "pool_scale": GROUPS * DH}, SMALL_FFN_AT
    for k, size in SMALL_FFN:
        rows[k] = at
        at += size // 128
    return rows


def kernel(x, w_in, w_pool, pool_scale, w_out, ln1_g, ln1_b, w_up, conv_w, conv_b, w_down, ln2_g, ln2_b, loss_target, m_w_in, m_w_pool, m_pool_scale, m_w_out, m_ln1_g, m_ln1_b, m_w_up, m_conv_w, m_conv_b, m_w_down, m_ln2_g, m_ln2_b, v_w_in, v_w_pool, v_pool_scale, v_w_out, v_ln1_g, v_ln1_b, v_w_up, v_conv_w, v_conv_b, v_w_down, v_ln2_g, v_ln2_b):
    me = 4 * lax.axis_index("x") + 2 * lax.axis_index("y") + lax.axis_index("c")
    x2, tgt = x[0], loss_target[0]

    cos, sin = _rope_tables()
    dmat, qd, kd, cdec = _decay_tables(RET_TILE)

    qkv, g, oret, states, cat, pooled, xhat1, rstd1, x1b, xb, g_in, g_out, g_up, g_down, g_cw = _mix_forward(
        x2, w_in[0].T, w_out[0], cos, sin, dmat, qd, kd, cdec, w_pool[0], pool_scale, ln1_g, ln1_b,
        gather_bf16=[w_up[0].T, w_down[0]], gather=[jnp.transpose(conv_w, (1, 0, 2))])
    w_in_t = g_in.reshape(IN_W, D)
    w_out_f = g_out.reshape(D, D)
    w_up_t = g_up.reshape(2 * D_FF, D)
    w_down_f = g_down.reshape(D_FF, D)
    conv_w_f = jnp.transpose(g_cw[:, :, 0, :], (1, 0, 2)).reshape(3, D_FF)
    dz1, dz2b, du, f, loss8, d_ln2_g, d_ln2_b, d_ln1_g, d_ln1_b, d_conv_b, d_conv_w = _ffn_forward_backward(
        xhat1, rstd1, ln1_g, ln1_b, w_up_t, conv_w_f, conv_b, w_down_f, ln2_g, ln2_b, tgt)
    small_ffn = [d_ln1_g, d_ln1_b, d_ln2_g, d_ln2_b, d_conv_b, d_conv_w, loss8]

    (dw_down,) = _weight_grad(f, dz2b, "grad_w_down", tm=D_FF // 2)
    dw_up_t, own_down, arr_down = _weight_grad(du, x1b, "grad_w_up", tm=D_FF // 2,
                                               reduce=[dw_down.reshape(N_DEV, ROWS_DOWN, D)])
    own_up, oth_up = _pair_reduce([dw_up_t.reshape(N_DEV, ROWS_UP, D)], "pair_reduce_up")
    up_sems, up_src, up_land, up_started = _exchange_start(oth_up, "exchange_up_start", CHIP_BARRIER_SPLIT)
    dproj, grad_x, small, dw_out = _mix_backward(
        dz1, w_out_f, qkv, g, oret, states, pooled, cat, cos, sin, dmat, qd, kd, cdec, w_pool[0], pool_scale, w_in_t,
        small_ffn, after=up_started)
    dw_in_t, own_out, own_small, arr_out, arr_small = _weight_grad(
        dproj, xb, "grad_w_in", tm=IN_W // 4,
        reduce=[dw_out.reshape(N_DEV, ROWS_OUT, D), small.reshape(N_DEV, SMALL_ROWS // N_DEV, 128)])
    arr_up = _exchange_wait(up_sems, up_src, up_land, [dw_in_t], "exchange_up_wait")
    own_in, oth_in, gs_small = _pair_reduce([dw_in_t.reshape(N_DEV, ROWS_IN, D)], "pair_reduce_in",
                                            gather_sum=(own_small, arr_small))
    in_sems, in_src, in_land, started = _exchange_start(oth_in, "exchange_in_start", CHIP_BARRIER)

    names = ["w_in", "w_pool", "pool_scale", "w_out", "ln1_g", "ln1_b", "w_up", "conv_w", "conv_b", "w_down",
             "ln2_g", "ln2_b"]
    w_d = dict(w_in=w_in, w_pool=w_pool, pool_scale=pool_scale, w_out=w_out, ln1_g=ln1_g, ln1_b=ln1_b, w_up=w_up,
               conv_w=conv_w, conv_b=conv_b, w_down=w_down, ln2_g=ln2_g, ln2_b=ln2_b)
    m_d = dict(w_in=m_w_in, w_pool=m_w_pool, pool_scale=m_pool_scale, w_out=m_w_out, ln1_g=m_ln1_g, ln1_b=m_ln1_b,
               w_up=m_w_up, conv_w=m_conv_w, conv_b=m_conv_b, w_down=m_w_down, ln2_g=m_ln2_g, ln2_b=m_ln2_b)
    v_d = dict(w_in=v_w_in, w_pool=v_w_pool, pool_scale=v_pool_scale, w_out=v_w_out, ln1_g=v_ln1_g, ln1_b=v_ln1_b,
               w_up=v_w_up, conv_w=v_conv_w, conv_b=v_conv_b, w_down=v_w_down, ln2_g=v_ln2_g, ln2_b=v_ln2_b)
    g_d, delta, new_m, new_v = {}, {}, {}, {}

    def big_adamw(k, own, arr, transposed, steps, after=()):
        lay = (lambda a: a[0].T) if transposed else (lambda a: a[0])
        back = (lambda a: a.T[None]) if transposed else (lambda a: a[None])
        res = _sum_adamw(own, arr, lay(w_d[k]), lay(m_d[k]), lay(v_d[k]), "adamw_" + k, steps, after)
        g_d[k], delta[k], new_m[k], new_v[k] = (back(r) for r in res)
        return res[3]

    done = [big_adamw("w_up", own_up, arr_up, True, 4, after=(started,)),
            big_adamw("w_down", own_down, arr_down, False, 2, after=(started,)),
            big_adamw("w_out", own_out, arr_out, False, 2, after=(started,))]

    gs_small, rows = gs_small.reshape(SMALL_ROWS, 128), _small_rows()
    g_conv_w = gs_small[rows["conv_w"]:rows["conv_w"] + 3 * D_FF // 128].reshape(3, D_FF)
    g_d["conv_w"] = lax.dynamic_slice(g_conv_w, (0, me * (D_FF // N_DEV)), (3, D_FF // N_DEV))[None]
    lay = lambda k, a: jnp.transpose(a, (1, 0, 2)) if k == "conv_w" else a.reshape(-1, a.shape[-1])
    back = lambda k, a: jnp.transpose(a, (1, 0, 2)) if k == "conv_w" else a.reshape(w_d[k].shape)
    group = [k for k in names if k not in ("w_in", "w_out", "w_up", "w_down")]
    packed = [k for k in group if k != "conv_w"]
    res = _adamw([lay(k, w_d[k]) for k in group], [lay(k, g_d[k]) if k == "conv_w" else rows[k] for k in group],
                 [lay(k, m_d[k]) for k in group], [lay(k, v_d[k]) for k in group], gs_small, rows["loss"],
                 "adamw_small", after=(started,))
    for j, k in enumerate(packed):
        g_d[k] = back(k, res[j])
    for j, k in enumerate(group):
        delta[k], new_m[k], new_v[k] = (back(k, res[len(packed) + part * len(group) + j]) for part in range(3))

    arr_in = _exchange_wait(in_sems, in_src, in_land, done + [res[0]], "exchange_in_wait")
    big_adamw("w_in", own_in, arr_in, True, 4)

    loss = res[-1].reshape(())
    return (loss, grad_x[None], *[g_d[k] for k in names], *[delta[k] for k in names], *[new_m[k] for k in names],
            *[new_v[k] for k in names])
```
